```python
import jax, jax.numpy as jnp
from jax import lax
import numpy as np

D_MODEL = 1024
BATCH = 8
SEQ = 2048
DEPTH = 2

MIX_WIDTH = D_MODEL
RWKV_WIDTH = MIX_WIDTH // 2
RWKV_HEAD = 64
RWKV_HEADS = RWKV_WIDTH // RWKV_HEAD
W_LORA = 64
A_LORA = 64
G_LORA = 128
LRU_WIDTH = MIX_WIDTH - RWKV_WIDTH
LRU_BLOCKS = 8
LRU_BLOCK = LRU_WIDTH // LRU_BLOCKS
LRU_CONV = 4
LRU_C = 8.0
S5_WIDTH = MIX_WIDTH
S5_GROUP = 16
S5_GROUPS = S5_WIDTH // S5_GROUP
S5_STATE = 64
D_FF = 2816
FFN_CONV = 3
NORM_EPS = 1e-6
GN_EPS = 64e-5
N_EVEN = (DEPTH + 1) // 2
N_ODD = DEPTH // 2
RWKV_SHIFT_COLS = 3 * RWKV_WIDTH + W_LORA + A_LORA + G_LORA
EVEN_IN_COLS = RWKV_SHIFT_COLS + 2 * LRU_WIDTH

kernel_name = "hybrid_rwkv7_rglru_s5_convffn"


def rms_norm(x, g):
    x32 = x.astype(jnp.float32)
    y = x32 * lax.rsqrt(jnp.mean(x32 * x32, axis=-1, keepdims=True) + NORM_EPS)
    return (y * g.astype(jnp.float32)).astype(x.dtype)


def causal_dwconv(x, w, b):
    k, c = w.shape
    y = lax.conv_general_dilated(x, w[:, None, :].astype(x.dtype), window_strides=(1,),
                                 padding=[(k - 1, 0)], dimension_numbers=('NWC', 'WIO', 'NWC'),
                                 feature_group_count=c)
    return y + b.astype(y.dtype)


def token_shift(p):
    return jnp.pad(p, ((0, 0), (1, 0), (0, 0)))[:, :-1]


def _lin_op(e1, e2):
    a1, b1 = e1
    a2, b2 = e2
    return a1 * a2, a2 * b1 + b2


def _cplx_op(e1, e2):
    a1r, a1i, b1r, b1i = e1
    a2r, a2i, b2r, b2i = e2
    return (a1r * a2r - a1i * a2i, a1r * a2i + a1i * a2r,
            a2r * b1r - a2i * b1i + b2r, a2r * b1i + a2i * b1r + b2i)


def wkv7(r, w, k, v, z, b):
    bn, s, h, n = r.shape

    def step(state, inp):
        r_t, w_t, k_t, v_t, z_t, b_t = inp
        sa = jnp.einsum('bhvk,bhk->bhv', state, z_t)
        state = (state * w_t[:, :, None, :] + sa[..., None] * b_t[:, :, None, :]
                 + v_t[..., None] * k_t[:, :, None, :])
        return state, jnp.einsum('bhvk,bhk->bhv', state, r_t)

    xs = tuple(jnp.swapaxes(t.astype(jnp.float32), 0, 1) for t in (r, w, k, v, z, b))
    state0 = jnp.zeros((bn, h, n, n), jnp.float32)
    _, ys = lax.scan(step, state0, xs)
    return jnp.swapaxes(ys, 0, 1)


def even_mixer(xn, w_in, mu, w0, w2, a0, a2, g2, k_k, k_a, r_k, ln_w, ln_b,
               conv_w, conv_b, gate_a_w, gate_a_b, gate_x_w, gate_x_b, lru_lambda, w_out):
    bn, s, _ = xn.shape
    f32 = jnp.float32
    rw = RWKV_WIDTH
    p = (xn @ w_in).astype(f32)
    pa = p[..., :RWKV_SHIFT_COLS]
    pa = pa + mu.astype(f32) * (token_shift(pa) - pa)
    r, k, v, wd, ad, gd = jnp.split(pa, [rw, 2 * rw, 3 * rw, 3 * rw + W_LORA, 3 * rw + W_LORA + A_LORA], axis=-1)
    w_log = -jax.nn.softplus(-(w0.astype(f32) + jnp.tanh(wd) @ w2.astype(f32))) - 0.5
    decay = jnp.exp(-jnp.exp(w_log))
    a = jax.nn.sigmoid(a0.astype(f32) + ad @ a2.astype(f32))
    g = jax.nn.sigmoid(gd) @ g2.astype(f32)
    hs = lambda t: t.reshape(bn, s, RWKV_HEADS, RWKV_HEAD)
    kk = hs(k * k_k.astype(f32))
    kk = kk / jnp.maximum(jnp.sqrt(jnp.sum(kk * kk, axis=-1, keepdims=True)), 1e-12)
    k = k * (1.0 + (a - 1.0) * k_a.astype(f32))
    y = wkv7(hs(r), hs(decay), hs(k), hs(v), -kk, kk * hs(a))
    mean = jnp.mean(y, axis=-1, keepdims=True)
    var = jnp.mean(jnp.square(y - mean), axis=-1, keepdims=True)
    y = ((y - mean) * lax.rsqrt(var + GN_EPS)).reshape(bn, s, rw) * ln_w.astype(f32) + ln_b.astype(f32)
    bonus = jnp.sum(hs(r) * hs(k) * r_k.astype(f32), axis=-1, keepdims=True) * hs(v)
    y_a = (y + bonus.reshape(bn, s, rw)) * g
    bx = p[..., RWKV_SHIFT_COLS:RWKV_SHIFT_COLS + LRU_WIDTH]
    bg = p[..., RWKV_SHIFT_COLS + LRU_WIDTH:]
    xc = causal_dwconv(bx, conv_w.astype(f32), conv_b.astype(f32))
    xb = xc.reshape(bn, s, LRU_BLOCKS, LRU_BLOCK)
    gr = jax.nn.sigmoid(jnp.einsum('bshi,hij->bshj', xb, gate_a_w.astype(f32)).reshape(bn, s, LRU_WIDTH) + gate_a_b.astype(f32))
    gi = jax.nn.sigmoid(jnp.einsum('bshi,hij->bshj', xb, gate_x_w.astype(f32)).reshape(bn, s, LRU_WIDTH) + gate_x_b.astype(f32))
    log_a = -LRU_C * gr * jax.nn.softplus(-lru_lambda.astype(f32))
    a_t = jnp.exp(log_a)
    u = xc * gi * jnp.sqrt(-jnp.expm1(2.0 * log_a))
    _, h = lax.associative_scan(_lin_op, (a_t, u), axis=1)
    y_b = h * jax.nn.gelu(bg)
    y_cat = jnp.concatenate([y_a, y_b], axis=-1).astype(xn.dtype)
    return y_cat @ w_out


def odd_mixer(xn, w_in, A_re, A_im, log_dt, B_re, B_im, C_re, C_im, D, w_glu):
    bn, s, _ = xn.shape
    f32 = jnp.float32
    u = (xn @ w_in).astype(f32)
    ug = u.reshape(bn, s, S5_GROUPS, S5_GROUP)
    lam_re = jnp.minimum(A_re.astype(f32), -1e-4)
    lam_im = A_im.astype(f32)
    dt = jnp.exp(log_dt.astype(f32))[:, None]
    mag = jnp.exp(lam_re * dt)
    ab_re = mag * jnp.cos(lam_im * dt)
    ab_im = mag * jnp.sin(lam_im * dt)
    den = lam_re * lam_re + lam_im * lam_im
    zr = ab_re - 1.0
    q_re = (zr * lam_re + ab_im * lam_im) / den
    q_im = (ab_im * lam_re - zr * lam_im) / den
    b_re = B_re.astype(f32)
    b_im = B_im.astype(f32)
    bb_re = q_re[..., None] * b_re - q_im[..., None] * b_im
    bb_im = q_re[..., None] * b_im + q_im[..., None] * b_re
    bu_re = jnp.einsum('bsgc,gnc->bsgn', ug, bb_re)
    bu_im = jnp.einsum('bsgc,gnc->bsgn', ug, bb_im)
    a_re = jnp.broadcast_to(ab_re[None, None], (1, s, S5_GROUPS, S5_STATE))
    a_im = jnp.broadcast_to(ab_im[None, None], (1, s, S5_GROUPS, S5_STATE))
    _, _, st_re, st_im = lax.associative_scan(_cplx_op, (a_re, a_im, bu_re, bu_im), axis=1)
    y = (jnp.einsum('gcn,bsgn->bsgc', C_re.astype(f32), st_re)
         - jnp.einsum('gcn,bsgn->bsgc', C_im.astype(f32), st_im)).reshape(bn, s, S5_WIDTH)
    y = jax.nn.gelu(y + D.astype(f32) * u).astype(xn.dtype)
    val, gate = jnp.split(y @ w_glu, 2, axis=-1)
    return val * jax.nn.sigmoid(gate)


def conv_ffn(xn, w_up, conv_w, conv_b, w_down):
    h = causal_dwconv(xn @ w_up, conv_w, conv_b)
    gate, val = jnp.split(h, 2, axis=-1)
    return (jax.nn.silu(gate) * val) @ w_down


def _fwd_setup_inputs(seed: int = 0) -> dict:
    key = jax.random.key(seed)
    ks = iter(jax.random.split(key, 64))
    f32 = jnp.float32
    nrm = lambda shape, scale: jax.random.normal(next(ks), shape, f32) * scale
    uni = lambda shape, lo, hi: jax.random.uniform(next(ks), shape, f32, lo, hi)
    rw = RWKV_WIDTH
    inp = {}
    inp['x'] = nrm((BATCH, SEQ, D_MODEL), 1.0)
    inp['e_norm_g'] = 1.0 + nrm((N_EVEN, D_MODEL), 0.02)
    inp['e_w_in'] = nrm((N_EVEN, D_MODEL, EVEN_IN_COLS), D_MODEL ** -0.5)
    inp['e_mu'] = uni((N_EVEN, RWKV_SHIFT_COLS), 0.0, 1.0)
    inp['e_w0'] = uni((N_EVEN, rw), -6.0, -1.0)
    inp['e_w2'] = nrm((N_EVEN, W_LORA, rw), 0.1 * W_LORA ** -0.5)
    inp['e_a0'] = nrm((N_EVEN, rw), 0.1)
    inp['e_a2'] = nrm((N_EVEN, A_LORA, rw), 0.1 * A_LORA ** -0.5)
    inp['e_g2'] = nrm((N_EVEN, G_LORA, rw), G_LORA ** -0.5)
    inp['e_k_k'] = 0.85 + nrm((N_EVEN, rw), 0.02)
    inp['e_k_a'] = 1.0 + nrm((N_EVEN, rw), 0.02)
    inp['e_r_k'] = nrm((N_EVEN, RWKV_HEADS, RWKV_HEAD), 0.1)
    inp['e_ln_w'] = 1.0 + nrm((N_EVEN, rw), 0.02)
    inp['e_ln_b'] = nrm((N_EVEN, rw), 0.02)
    inp['e_conv_w'] = nrm((N_EVEN, LRU_CONV, LRU_WIDTH), LRU_CONV ** -0.5)
    inp['e_conv_b'] = nrm((N_EVEN, LRU_WIDTH), 0.02)
    inp['e_gate_a_w'] = nrm((N_EVEN, LRU_BLOCKS, LRU_BLOCK, LRU_BLOCK), LRU_BLOCK ** -0.5)
    inp['e_gate_a_b'] = nrm((N_EVEN, LRU_WIDTH), 0.02)
    inp['e_gate_x_w'] = nrm((N_EVEN, LRU_BLOCKS, LRU_BLOCK, LRU_BLOCK), LRU_BLOCK ** -0.5)
    inp['e_gate_x_b'] = nrm((N_EVEN, LRU_WIDTH), 0.02)
    a_c = uni((N_EVEN, LRU_WIDTH), 0.9, 0.999)
    a_base = a_c ** (1.0 / LRU_C)
    inp['e_lru_lambda'] = jnp.log(a_base) - jnp.log1p(-a_base)
    inp['e_w_out'] = nrm((N_EVEN, MIX_WIDTH, D_MODEL), MIX_WIDTH ** -0.5)
    n_idx = jnp.arange(S5_STATE, dtype=f32)
    inp['o_norm_g'] = 1.0 + nrm((N_ODD, D_MODEL), 0.02)
    inp['o_w_in'] = nrm((N_ODD, D_MODEL, S5_WIDTH), D_MODEL ** -0.5)
    inp['o_A_re'] = -0.5 + nrm((N_ODD, S5_GROUPS, S5_STATE), 0.01)
    inp['o_A_im'] = jnp.pi * n_idx + nrm((N_ODD, S5_GROUPS, S5_STATE), 0.01)
    inp['o_log_dt'] = uni((N_ODD, S5_GROUPS), float(np.log(1e-3)), float(np.log(1e-1)))
    inp['o_B_re'] = nrm((N_ODD, S5_GROUPS, S5_STATE, S5_GROUP), (2 * S5_GROUP) ** -0.5)
    inp['o_B_im'] = nrm((N_ODD, S5_GROUPS, S5_STATE, S5_GROUP), (2 * S5_GROUP) ** -0.5)
    inp['o_C_re'] = nrm((N_ODD, S5_GROUPS, S5_GROUP, S5_STATE), S5_STATE ** -0.5)
    inp['o_C_im'] = nrm((N_ODD, S5_GROUPS, S5_GROUP, S5_STATE), S5_STATE ** -0.5)
    inp['o_D'] = nrm((N_ODD, S5_WIDTH), 1.0)
    inp['o_w_glu'] = nrm((N_ODD, S5_WIDTH, 2 * D_MODEL), S5_WIDTH ** -0.5)
    inp['f_norm_g'] = 1.0 + nrm((DEPTH, D_MODEL), 0.02)
    inp['f_w_up'] = nrm((DEPTH, D_MODEL, 2 * D_FF), D_MODEL ** -0.5)
    inp['f_conv_w'] = nrm((DEPTH, FFN_CONV, 2 * D_FF), FFN_CONV ** -0.5)
    inp['f_conv_b'] = nrm((DEPTH, 2 * D_FF), 0.02)
    inp['f_w_down'] = nrm((DEPTH, D_FF, D_MODEL), D_FF ** -0.5)
    inp['final_norm_g'] = 1.0 + nrm((D_MODEL,), 0.02)
    return inp


def _fwd_reference(x, e_norm_g, e_w_in, e_mu, e_w0, e_w2, e_a0, e_a2, e_g2, e_k_k, e_k_a, e_r_k,
              e_ln_w, e_ln_b, e_conv_w, e_conv_b, e_gate_a_w, e_gate_a_b, e_gate_x_w, e_gate_x_b,
              e_lru_lambda, e_w_out, o_norm_g, o_w_in, o_A_re, o_A_im, o_log_dt, o_B_re, o_B_im,
              o_C_re, o_C_im, o_D, o_w_glu, f_norm_g, f_w_up, f_conv_w, f_conv_b, f_w_down,
              final_norm_g):
    for i in range(DEPTH):
        j = i // 2
        if i % 2 == 0:
            x = x + even_mixer(rms_norm(x, e_norm_g[j]), e_w_in[j], e_mu[j], e_w0[j], e_w2[j],
                               e_a0[j], e_a2[j], e_g2[j], e_k_k[j], e_k_a[j], e_r_k[j],
                               e_ln_w[j], e_ln_b[j], e_conv_w[j], e_conv_b[j],
                               e_gate_a_w[j], e_gate_a_b[j], e_gate_x_w[j], e_gate_x_b[j],
                               e_lru_lambda[j], e_w_out[j])
        else:
            x = x + odd_mixer(rms_norm(x, o_norm_g[j]), o_w_in[j], o_A_re[j], o_A_im[j],
                              o_log_dt[j], o_B_re[j], o_B_im[j], o_C_re[j], o_C_im[j],
                              o_D[j], o_w_glu[j])
        x = x + conv_ffn(rms_norm(x, f_norm_g[i]), f_w_up[i], f_conv_w[i], f_conv_b[i], f_w_down[i])
    return rms_norm(x, final_norm_g)


import jax as _jax
import jax.numpy as _jnp

TWIN_FORMAT = 'train_step'
FWD_PARAMS = ['x', 'e_norm_g', 'e_w_in', 'e_mu', 'e_w0', 'e_w2', 'e_a0', 'e_a2', 'e_g2', 'e_k_k', 'e_k_a', 'e_r_k', 'e_ln_w', 'e_ln_b', 'e_conv_w', 'e_conv_b', 'e_gate_a_w', 'e_gate_a_b', 'e_gate_x_w', 'e_gate_x_b', 'e_lru_lambda', 'e_w_out', 'o_norm_g', 'o_w_in', 'o_A_re', 'o_A_im', 'o_log_dt', 'o_B_re', 'o_B_im', 'o_C_re', 'o_C_im', 'o_D', 'o_w_glu', 'f_norm_g', 'f_w_up', 'f_conv_w', 'f_conv_b', 'f_w_down', 'final_norm_g']
TWIN_WEIGHTS = ['e_norm_g', 'e_w_in', 'e_mu', 'e_w0', 'e_w2', 'e_a0', 'e_a2', 'e_g2', 'e_k_k', 'e_k_a', 'e_r_k', 'e_ln_w', 'e_ln_b', 'e_conv_w', 'e_conv_b', 'e_gate_a_w', 'e_gate_a_b', 'e_gate_x_w', 'e_gate_x_b', 'e_lru_lambda', 'e_w_out', 'o_norm_g', 'o_w_in', 'o_A_re', 'o_A_im', 'o_log_dt', 'o_B_re', 'o_B_im', 'o_C_re', 'o_C_im', 'o_D', 'o_w_glu', 'f_norm_g', 'f_w_up', 'f_conv_w', 'f_conv_b', 'f_w_down', 'final_norm_g']
TWIN_DIFF_INPUT = 'x'
TWIN_INPUTS = ['x', 'e_norm_g', 'e_w_in', 'e_mu', 'e_w0', 'e_w2', 'e_a0', 'e_a2', 'e_g2', 'e_k_k', 'e_k_a', 'e_r_k', 'e_ln_w', 'e_ln_b', 'e_conv_w', 'e_conv_b', 'e_gate_a_w', 'e_gate_a_b', 'e_gate_x_w', 'e_gate_x_b', 'e_lru_lambda', 'e_w_out', 'o_norm_g', 'o_w_in', 'o_A_re', 'o_A_im', 'o_log_dt', 'o_B_re', 'o_B_im', 'o_C_re', 'o_C_im', 'o_D', 'o_w_glu', 'f_norm_g', 'f_w_up', 'f_conv_w', 'f_conv_b', 'f_w_down', 'final_norm_g', 'loss_target', 'm_e_norm_g', 'm_e_w_in', 'm_e_mu', 'm_e_w0', 'm_e_w2', 'm_e_a0', 'm_e_a2', 'm_e_g2', 'm_e_k_k', 'm_e_k_a', 'm_e_r_k', 'm_e_ln_w', 'm_e_ln_b', 'm_e_conv_w', 'm_e_conv_b', 'm_e_gate_a_w', 'm_e_gate_a_b', 'm_e_gate_x_w', 'm_e_gate_x_b', 'm_e_lru_lambda', 'm_e_w_out', 'm_o_norm_g', 'm_o_w_in', 'm_o_A_re', 'm_o_A_im', 'm_o_log_dt', 'm_o_B_re', 'm_o_B_im', 'm_o_C_re', 'm_o_C_im', 'm_o_D', 'm_o_w_glu', 'm_f_norm_g', 'm_f_w_up', 'm_f_conv_w', 'm_f_conv_b', 'm_f_w_down', 'm_final_norm_g', 'v_e_norm_g', 'v_e_w_in', 'v_e_mu', 'v_e_w0', 'v_e_w2', 'v_e_a0', 'v_e_a2', 'v_e_g2', 'v_e_k_k', 'v_e_k_a', 'v_e_r_k', 'v_e_ln_w', 'v_e_ln_b', 'v_e_conv_w', 'v_e_conv_b', 'v_e_gate_a_w', 'v_e_gate_a_b', 'v_e_gate_x_w', 'v_e_gate_x_b', 'v_e_lru_lambda', 'v_e_w_out', 'v_o_norm_g', 'v_o_w_in', 'v_o_A_re', 'v_o_A_im', 'v_o_log_dt', 'v_o_B_re', 'v_o_B_im', 'v_o_C_re', 'v_o_C_im', 'v_o_D', 'v_o_w_glu', 'v_f_norm_g', 'v_f_w_up', 'v_f_conv_w', 'v_f_conv_b', 'v_f_w_down', 'v_final_norm_g']
TWIN_OUTPUTS = ['loss', 'grad_x', 'grad_e_norm_g', 'grad_e_w_in', 'grad_e_mu', 'grad_e_w0', 'grad_e_w2', 'grad_e_a0', 'grad_e_a2', 'grad_e_g2', 'grad_e_k_k', 'grad_e_k_a', 'grad_e_r_k', 'grad_e_ln_w', 'grad_e_ln_b', 'grad_e_conv_w', 'grad_e_conv_b', 'grad_e_gate_a_w', 'grad_e_gate_a_b', 'grad_e_gate_x_w', 'grad_e_gate_x_b', 'grad_e_lru_lambda', 'grad_e_w_out', 'grad_o_norm_g', 'grad_o_w_in', 'grad_o_A_re', 'grad_o_A_im', 'grad_o_log_dt', 'grad_o_B_re', 'grad_o_B_im', 'grad_o_C_re', 'grad_o_C_im', 'grad_o_D', 'grad_o_w_glu', 'grad_f_norm_g', 'grad_f_w_up', 'grad_f_conv_w', 'grad_f_conv_b', 'grad_f_w_down', 'grad_final_norm_g', 'delta_e_norm_g', 'delta_e_w_in', 'delta_e_mu', 'delta_e_w0', 'delta_e_w2', 'delta_e_a0', 'delta_e_a2', 'delta_e_g2', 'delta_e_k_k', 'delta_e_k_a', 'delta_e_r_k', 'delta_e_ln_w', 'delta_e_ln_b', 'delta_e_conv_w', 'delta_e_conv_b', 'delta_e_gate_a_w', 'delta_e_gate_a_b', 'delta_e_gate_x_w', 'delta_e_gate_x_b', 'delta_e_lru_lambda', 'delta_e_w_out', 'delta_o_norm_g', 'delta_o_w_in', 'delta_o_A_re', 'delta_o_A_im', 'delta_o_log_dt', 'delta_o_B_re', 'delta_o_B_im', 'delta_o_C_re', 'delta_o_C_im', 'delta_o_D', 'delta_o_w_glu', 'delta_f_norm_g', 'delta_f_w_up', 'delta_f_conv_w', 'delta_f_conv_b', 'delta_f_w_down', 'delta_final_norm_g', 'new_m_e_norm_g', 'new_m_e_w_in', 'new_m_e_mu', 'new_m_e_w0', 'new_m_e_w2', 'new_m_e_a0', 'new_m_e_a2', 'new_m_e_g2', 'new_m_e_k_k', 'new_m_e_k_a', 'new_m_e_r_k', 'new_m_e_ln_w', 'new_m_e_ln_b', 'new_m_e_conv_w', 'new_m_e_conv_b', 'new_m_e_gate_a_w', 'new_m_e_gate_a_b', 'new_m_e_gate_x_w', 'new_m_e_gate_x_b', 'new_m_e_lru_lambda', 'new_m_e_w_out', 'new_m_o_norm_g', 'new_m_o_w_in', 'new_m_o_A_re', 'new_m_o_A_im', 'new_m_o_log_dt', 'new_m_o_B_re', 'new_m_o_B_im', 'new_m_o_C_re', 'new_m_o_C_im', 'new_m_o_D', 'new_m_o_w_glu', 'new_m_f_norm_g', 'new_m_f_w_up', 'new_m_f_conv_w', 'new_m_f_conv_b', 'new_m_f_w_down', 'new_m_final_norm_g', 'new_v_e_norm_g', 'new_v_e_w_in', 'new_v_e_mu', 'new_v_e_w0', 'new_v_e_w2', 'new_v_e_a0', 'new_v_e_a2', 'new_v_e_g2', 'new_v_e_k_k', 'new_v_e_k_a', 'new_v_e_r_k', 'new_v_e_ln_w', 'new_v_e_ln_b', 'new_v_e_conv_w', 'new_v_e_conv_b', 'new_v_e_gate_a_w', 'new_v_e_gate_a_b', 'new_v_e_gate_x_w', 'new_v_e_gate_x_b', 'new_v_e_lru_lambda', 'new_v_e_w_out', 'new_v_o_norm_g', 'new_v_o_w_in', 'new_v_o_A_re', 'new_v_o_A_im', 'new_v_o_log_dt', 'new_v_o_B_re', 'new_v_o_B_im', 'new_v_o_C_re', 'new_v_o_C_im', 'new_v_o_D', 'new_v_o_w_glu', 'new_v_f_norm_g', 'new_v_f_w_up', 'new_v_f_conv_w', 'new_v_f_conv_b', 'new_v_f_w_down', 'new_v_final_norm_g']
TWIN_LEAF_KINDS = {'loss': 'loss', 'grad_x': 'grad_x', 'grad_e_norm_g': 'grad_w', 'grad_e_w_in': 'grad_w', 'grad_e_mu': 'grad_w', 'grad_e_w0': 'grad_w', 'grad_e_w2': 'grad_w', 'grad_e_a0': 'grad_w', 'grad_e_a2': 'grad_w', 'grad_e_g2': 'grad_w', 'grad_e_k_k': 'grad_w', 'grad_e_k_a': 'grad_w', 'grad_e_r_k': 'grad_w', 'grad_e_ln_w': 'grad_w', 'grad_e_ln_b': 'grad_w', 'grad_e_conv_w': 'grad_w', 'grad_e_conv_b': 'grad_w', 'grad_e_gate_a_w': 'grad_w', 'grad_e_gate_a_b': 'grad_w', 'grad_e_gate_x_w': 'grad_w', 'grad_e_gate_x_b': 'grad_w', 'grad_e_lru_lambda': 'grad_w', 'grad_e_w_out': 'grad_w', 'grad_o_norm_g': 'grad_w', 'grad_o_w_in': 'grad_w', 'grad_o_A_re': 'grad_w', 'grad_o_A_im': 'grad_w', 'grad_o_log_dt': 'grad_w', 'grad_o_B_re': 'grad_w', 'grad_o_B_im': 'grad_w', 'grad_o_C_re': 'grad_w', 'grad_o_C_im': 'grad_w', 'grad_o_D': 'grad_w', 'grad_o_w_glu': 'grad_w', 'grad_f_norm_g': 'grad_w', 'grad_f_w_up': 'grad_w', 'grad_f_conv_w': 'grad_w', 'grad_f_conv_b': 'grad_w', 'grad_f_w_down': 'grad_w', 'grad_final_norm_g': 'grad_w', 'delta_e_norm_g': 'delta_w', 'delta_e_w_in': 'delta_w', 'delta_e_mu': 'delta_w', 'delta_e_w0': 'delta_w', 'delta_e_w2': 'delta_w', 'delta_e_a0': 'delta_w', 'delta_e_a2': 'delta_w', 'delta_e_g2': 'delta_w', 'delta_e_k_k': 'delta_w', 'delta_e_k_a': 'delta_w', 'delta_e_r_k': 'delta_w', 'delta_e_ln_w': 'delta_w', 'delta_e_ln_b': 'delta_w', 'delta_e_conv_w': 'delta_w', 'delta_e_conv_b': 'delta_w', 'delta_e_gate_a_w': 'delta_w', 'delta_e_gate_a_b': 'delta_w', 'delta_e_gate_x_w': 'delta_w', 'delta_e_gate_x_b': 'delta_w', 'delta_e_lru_lambda': 'delta_w', 'delta_e_w_out': 'delta_w', 'delta_o_norm_g': 'delta_w', 'delta_o_w_in': 'delta_w', 'delta_o_A_re': 'delta_w', 'delta_o_A_im': 'delta_w', 'delta_o_log_dt': 'delta_w', 'delta_o_B_re': 'delta_w', 'delta_o_B_im': 'delta_w', 'delta_o_C_re': 'delta_w', 'delta_o_C_im': 'delta_w', 'delta_o_D': 'delta_w', 'delta_o_w_glu': 'delta_w', 'delta_f_norm_g': 'delta_w', 'delta_f_w_up': 'delta_w', 'delta_f_conv_w': 'delta_w', 'delta_f_conv_b': 'delta_w', 'delta_f_w_down': 'delta_w', 'delta_final_norm_g': 'delta_w', 'new_m_e_norm_g': 'new_m', 'new_m_e_w_in': 'new_m', 'new_m_e_mu': 'new_m', 'new_m_e_w0': 'new_m', 'new_m_e_w2': 'new_m', 'new_m_e_a0': 'new_m', 'new_m_e_a2': 'new_m', 'new_m_e_g2': 'new_m', 'new_m_e_k_k': 'new_m', 'new_m_e_k_a': 'new_m', 'new_m_e_r_k': 'new_m', 'new_m_e_ln_w': 'new_m', 'new_m_e_ln_b': 'new_m', 'new_m_e_conv_w': 'new_m', 'new_m_e_conv_b': 'new_m', 'new_m_e_gate_a_w': 'new_m', 'new_m_e_gate_a_b': 'new_m', 'new_m_e_gate_x_w': 'new_m', 'new_m_e_gate_x_b': 'new_m', 'new_m_e_lru_lambda': 'new_m', 'new_m_e_w_out': 'new_m', 'new_m_o_norm_g': 'new_m', 'new_m_o_w_in': 'new_m', 'new_m_o_A_re': 'new_m', 'new_m_o_A_im': 'new_m', 'new_m_o_log_dt': 'new_m', 'new_m_o_B_re': 'new_m', 'new_m_o_B_im': 'new_m', 'new_m_o_C_re': 'new_m', 'new_m_o_C_im': 'new_m', 'new_m_o_D': 'new_m', 'new_m_o_w_glu': 'new_m', 'new_m_f_norm_g': 'new_m', 'new_m_f_w_up': 'new_m', 'new_m_f_conv_w': 'new_m', 'new_m_f_conv_b': 'new_m', 'new_m_f_w_down': 'new_m', 'new_m_final_norm_g': 'new_m', 'new_v_e_norm_g': 'new_v', 'new_v_e_w_in': 'new_v', 'new_v_e_mu': 'new_v', 'new_v_e_w0': 'new_v', 'new_v_e_w2': 'new_v', 'new_v_e_a0': 'new_v', 'new_v_e_a2': 'new_v', 'new_v_e_g2': 'new_v', 'new_v_e_k_k': 'new_v', 'new_v_e_k_a': 'new_v', 'new_v_e_r_k': 'new_v', 'new_v_e_ln_w': 'new_v', 'new_v_e_ln_b': 'new_v', 'new_v_e_conv_w': 'new_v', 'new_v_e_conv_b': 'new_v', 'new_v_e_gate_a_w': 'new_v', 'new_v_e_gate_a_b': 'new_v', 'new_v_e_gate_x_w': 'new_v', 'new_v_e_gate_x_b': 'new_v', 'new_v_e_lru_lambda': 'new_v', 'new_v_e_w_out': 'new_v', 'new_v_o_norm_g': 'new_v', 'new_v_o_w_in': 'new_v', 'new_v_o_A_re': 'new_v', 'new_v_o_A_im': 'new_v', 'new_v_o_log_dt': 'new_v', 'new_v_o_B_re': 'new_v', 'new_v_o_B_im': 'new_v', 'new_v_o_C_re': 'new_v', 'new_v_o_C_im': 'new_v', 'new_v_o_D': 'new_v', 'new_v_o_w_glu': 'new_v', 'new_v_f_norm_g': 'new_v', 'new_v_f_w_up': 'new_v', 'new_v_f_conv_w': 'new_v', 'new_v_f_conv_b': 'new_v', 'new_v_f_w_down': 'new_v', 'new_v_final_norm_g': 'new_v'}


def _forward(args):
    return _fwd_reference(*[args[k] for k in FWD_PARAMS])


def _output_shape():
    out = _jax.eval_shape(lambda: _forward(_fwd_setup_inputs(0)))
    return out.shape, out.dtype

N_MICROBATCH = 1
ADAM_LR = 0.001
ADAM_B1 = 0.9
ADAM_B2 = 0.999
ADAM_EPS = 1e-08
ADAM_WD = 0.01
ADAM_STEP = 10
PER_EXAMPLE_BATCH_AXIS = {'x': 0, 'loss_target': 0}
SHARED_INPUTS = []
_WEIGHT_DTYPES = {'e_norm_g': _jnp.float32, 'e_w_in': _jnp.float32, 'e_mu': _jnp.float32, 'e_w0': _jnp.float32, 'e_w2': _jnp.float32, 'e_a0': _jnp.float32, 'e_a2': _jnp.float32, 'e_g2': _jnp.float32, 'e_k_k': _jnp.float32, 'e_k_a': _jnp.float32, 'e_r_k': _jnp.float32, 'e_ln_w': _jnp.float32, 'e_ln_b': _jnp.float32, 'e_conv_w': _jnp.float32, 'e_conv_b': _jnp.float32, 'e_gate_a_w': _jnp.float32, 'e_gate_a_b': _jnp.float32, 'e_gate_x_w': _jnp.float32, 'e_gate_x_b': _jnp.float32, 'e_lru_lambda': _jnp.float32, 'e_w_out': _jnp.float32, 'o_norm_g': _jnp.float32, 'o_w_in': _jnp.float32, 'o_A_re': _jnp.float32, 'o_A_im': _jnp.float32, 'o_log_dt': _jnp.float32, 'o_B_re': _jnp.float32, 'o_B_im': _jnp.float32, 'o_C_re': _jnp.float32, 'o_C_im': _jnp.float32, 'o_D': _jnp.float32, 'o_w_glu': _jnp.float32, 'f_norm_g': _jnp.float32, 'f_w_up': _jnp.float32, 'f_conv_w': _jnp.float32, 'f_conv_b': _jnp.float32, 'f_w_down': _jnp.float32, 'final_norm_g': _jnp.float32}
MOMENT_SCALE = {'e_norm_g': 1.215212e-01, 'e_w_in': 6.939183e-02, 'e_mu': 1.219600e-01, 'e_w0': 3.144930e-02, 'e_w2': 3.253450e-03, 'e_a0': 3.419217e-02, 'e_a2': 3.057074e-02, 'e_g2': 7.300666e-02, 'e_k_k': 1.026379e-01, 'e_k_a': 8.171299e-02, 'e_r_k': 1.590045e-01, 'e_ln_w': 7.537048e-02, 'e_ln_b': 6.771782e-02, 'e_conv_w': 6.125692e-02, 'e_conv_b': 5.245113e-01, 'e_gate_a_w': 2.045099e-02, 'e_gate_a_b': 1.564577e-02, 'e_gate_x_w': 3.686318e-02, 'e_gate_x_b': 2.384950e-02, 'e_lru_lambda': 3.468980e-02, 'e_w_out': 7.061923e-02, 'o_norm_g': 4.014050e-02, 'o_w_in': 3.916378e-02, 'o_A_re': 2.554123e-03, 'o_A_im': 2.797898e-03, 'o_log_dt': 2.169995e+00, 'o_B_re': 1.759136e-03, 'o_B_im': 1.773559e-03, 'o_C_re': 2.547890e-03, 'o_C_im': 2.550640e-03, 'o_D': 3.921030e-02, 'o_w_glu': 2.717828e-02, 'f_norm_g': 8.539981e-02, 'f_w_up': 3.653571e-02, 'f_conv_w': 3.667235e-02, 'f_conv_b': 3.584493e-02, 'f_w_down': 5.951134e-02, 'final_norm_g': 1.599169e+01}


def _to_microbatches(a, axis):
    t = _jnp.moveaxis(a, axis, 0)
    t = t.reshape((N_MICROBATCH, t.shape[0] // N_MICROBATCH) + t.shape[1:])
    return _jnp.moveaxis(t, 1, axis + 1)


def setup_inputs(seed: int = 0) -> dict:
    inp = _fwd_setup_inputs(seed)
    key = _jax.random.fold_in(_jax.random.key(seed), 7919)
    shape, _ = _output_shape()
    out = dict(inp)
    out["loss_target"] = _jax.random.normal(_jax.random.fold_in(key, 0), shape, _jnp.float32)
    for i, name in enumerate(TWIN_WEIGHTS):
        w = inp[name].astype(_jnp.float32)
        if MOMENT_SCALE is None:
            s = _jnp.sqrt(_jnp.mean(_jnp.square(w)) + 1e-30)
        else:
            s = MOMENT_SCALE[name]
        km, kv = _jax.random.split(_jax.random.fold_in(key, i + 1))
        out[name] = w
        out["m_" + name] = s * _jax.random.normal(km, w.shape, _jnp.float32)
        out["v_" + name] = (s * s) * _jax.random.uniform(kv, w.shape, _jnp.float32, 0.5, 1.5)
    if N_MICROBATCH > 1:
        for name, axis in PER_EXAMPLE_BATCH_AXIS.items():
            out[name] = _to_microbatches(out[name], axis)
    return {'x': out['x'], 'e_norm_g': out['e_norm_g'], 'e_w_in': out['e_w_in'], 'e_mu': out['e_mu'], 'e_w0': out['e_w0'], 'e_w2': out['e_w2'], 'e_a0': out['e_a0'], 'e_a2': out['e_a2'], 'e_g2': out['e_g2'], 'e_k_k': out['e_k_k'], 'e_k_a': out['e_k_a'], 'e_r_k': out['e_r_k'], 'e_ln_w': out['e_ln_w'], 'e_ln_b': out['e_ln_b'], 'e_conv_w': out['e_conv_w'], 'e_conv_b': out['e_conv_b'], 'e_gate_a_w': out['e_gate_a_w'], 'e_gate_a_b': out['e_gate_a_b'], 'e_gate_x_w': out['e_gate_x_w'], 'e_gate_x_b': out['e_gate_x_b'], 'e_lru_lambda': out['e_lru_lambda'], 'e_w_out': out['e_w_out'], 'o_norm_g': out['o_norm_g'], 'o_w_in': out['o_w_in'], 'o_A_re': out['o_A_re'], 'o_A_im': out['o_A_im'], 'o_log_dt': out['o_log_dt'], 'o_B_re': out['o_B_re'], 'o_B_im': out['o_B_im'], 'o_C_re': out['o_C_re'], 'o_C_im': out['o_C_im'], 'o_D': out['o_D'], 'o_w_glu': out['o_w_glu'], 'f_norm_g': out['f_norm_g'], 'f_w_up': out['f_w_up'], 'f_conv_w': out['f_conv_w'], 'f_conv_b': out['f_conv_b'], 'f_w_down': out['f_w_down'], 'final_norm_g': out['final_norm_g'], 'loss_target': out['loss_target'], 'm_e_norm_g': out['m_e_norm_g'], 'm_e_w_in': out['m_e_w_in'], 'm_e_mu': out['m_e_mu'], 'm_e_w0': out['m_e_w0'], 'm_e_w2': out['m_e_w2'], 'm_e_a0': out['m_e_a0'], 'm_e_a2': out['m_e_a2'], 'm_e_g2': out['m_e_g2'], 'm_e_k_k': out['m_e_k_k'], 'm_e_k_a': out['m_e_k_a'], 'm_e_r_k': out['m_e_r_k'], 'm_e_ln_w': out['m_e_ln_w'], 'm_e_ln_b': out['m_e_ln_b'], 'm_e_conv_w': out['m_e_conv_w'], 'm_e_conv_b': out['m_e_conv_b'], 'm_e_gate_a_w': out['m_e_gate_a_w'], 'm_e_gate_a_b': out['m_e_gate_a_b'], 'm_e_gate_x_w': out['m_e_gate_x_w'], 'm_e_gate_x_b': out['m_e_gate_x_b'], 'm_e_lru_lambda': out['m_e_lru_lambda'], 'm_e_w_out': out['m_e_w_out'], 'm_o_norm_g': out['m_o_norm_g'], 'm_o_w_in': out['m_o_w_in'], 'm_o_A_re': out['m_o_A_re'], 'm_o_A_im': out['m_o_A_im'], 'm_o_log_dt': out['m_o_log_dt'], 'm_o_B_re': out['m_o_B_re'], 'm_o_B_im': out['m_o_B_im'], 'm_o_C_re': out['m_o_C_re'], 'm_o_C_im': out['m_o_C_im'], 'm_o_D': out['m_o_D'], 'm_o_w_glu': out['m_o_w_glu'], 'm_f_norm_g': out['m_f_norm_g'], 'm_f_w_up': out['m_f_w_up'], 'm_f_conv_w': out['m_f_conv_w'], 'm_f_conv_b': out['m_f_conv_b'], 'm_f_w_down': out['m_f_w_down'], 'm_final_norm_g': out['m_final_norm_g'], 'v_e_norm_g': out['v_e_norm_g'], 'v_e_w_in': out['v_e_w_in'], 'v_e_mu': out['v_e_mu'], 'v_e_w0': out['v_e_w0'], 'v_e_w2': out['v_e_w2'], 'v_e_a0': out['v_e_a0'], 'v_e_a2': out['v_e_a2'], 'v_e_g2': out['v_e_g2'], 'v_e_k_k': out['v_e_k_k'], 'v_e_k_a': out['v_e_k_a'], 'v_e_r_k': out['v_e_r_k'], 'v_e_ln_w': out['v_e_ln_w'], 'v_e_ln_b': out['v_e_ln_b'], 'v_e_conv_w': out['v_e_conv_w'], 'v_e_conv_b': out['v_e_conv_b'], 'v_e_gate_a_w': out['v_e_gate_a_w'], 'v_e_gate_a_b': out['v_e_gate_a_b'], 'v_e_gate_x_w': out['v_e_gate_x_w'], 'v_e_gate_x_b': out['v_e_gate_x_b'], 'v_e_lru_lambda': out['v_e_lru_lambda'], 'v_e_w_out': out['v_e_w_out'], 'v_o_norm_g': out['v_o_norm_g'], 'v_o_w_in': out['v_o_w_in'], 'v_o_A_re': out['v_o_A_re'], 'v_o_A_im': out['v_o_A_im'], 'v_o_log_dt': out['v_o_log_dt'], 'v_o_B_re': out['v_o_B_re'], 'v_o_B_im': out['v_o_B_im'], 'v_o_C_re': out['v_o_C_re'], 'v_o_C_im': out['v_o_C_im'], 'v_o_D': out['v_o_D'], 'v_o_w_glu': out['v_o_w_glu'], 'v_f_norm_g': out['v_f_norm_g'], 'v_f_w_up': out['v_f_w_up'], 'v_f_conv_w': out['v_f_conv_w'], 'v_f_conv_b': out['v_f_conv_b'], 'v_f_w_down': out['v_f_w_down'], 'v_final_norm_g': out['v_final_norm_g']}


def _loss(weights, diff, rest, loss_target):
    with _jax.named_scope("forward"):
        args = {**rest, TWIN_DIFF_INPUT: diff, **{k: w.astype(_WEIGHT_DTYPES[k]) for k, w in weights.items()}}
        y = _forward(args)
    with _jax.named_scope("loss_head"):
        err = _jnp.square(y.astype(_jnp.float32) - loss_target)
        return 0.5 * _jnp.sum(_jnp.mean(err, axis=-1)) if err.ndim else 0.5 * err


def _adamw(w, g, m, v):
    m = ADAM_B1 * m + (1.0 - ADAM_B1) * g
    v = ADAM_B2 * v + (1.0 - ADAM_B2) * _jnp.square(g)
    m_hat = m / (1.0 - ADAM_B1 ** ADAM_STEP)
    v_hat = v / (1.0 - ADAM_B2 ** ADAM_STEP)
    delta = -ADAM_LR * (m_hat / (_jnp.sqrt(v_hat) + ADAM_EPS) + ADAM_WD * w)
    return delta, m, v


def reference(x, e_norm_g, e_w_in, e_mu, e_w0, e_w2, e_a0, e_a2, e_g2, e_k_k, e_k_a, e_r_k, e_ln_w, e_ln_b, e_conv_w, e_conv_b, e_gate_a_w, e_gate_a_b, e_gate_x_w, e_gate_x_b, e_lru_lambda, e_w_out, o_norm_g, o_w_in, o_A_re, o_A_im, o_log_dt, o_B_re, o_B_im, o_C_re, o_C_im, o_D, o_w_glu, f_norm_g, f_w_up, f_conv_w, f_conv_b, f_w_down, final_norm_g, loss_target, m_e_norm_g, m_e_w_in, m_e_mu, m_e_w0, m_e_w2, m_e_a0, m_e_a2, m_e_g2, m_e_k_k, m_e_k_a, m_e_r_k, m_e_ln_w, m_e_ln_b, m_e_conv_w, m_e_conv_b, m_e_gate_a_w, m_e_gate_a_b, m_e_gate_x_w, m_e_gate_x_b, m_e_lru_lambda, m_e_w_out, m_o_norm_g, m_o_w_in, m_o_A_re, m_o_A_im, m_o_log_dt, m_o_B_re, m_o_B_im, m_o_C_re, m_o_C_im, m_o_D, m_o_w_glu, m_f_norm_g, m_f_w_up, m_f_conv_w, m_f_conv_b, m_f_w_down, m_final_norm_g, v_e_norm_g, v_e_w_in, v_e_mu, v_e_w0, v_e_w2, v_e_a0, v_e_a2, v_e_g2, v_e_k_k, v_e_k_a, v_e_r_k, v_e_ln_w, v_e_ln_b, v_e_conv_w, v_e_conv_b, v_e_gate_a_w, v_e_gate_a_b, v_e_gate_x_w, v_e_gate_x_b, v_e_lru_lambda, v_e_w_out, v_o_norm_g, v_o_w_in, v_o_A_re, v_o_A_im, v_o_log_dt, v_o_B_re, v_o_B_im, v_o_C_re, v_o_C_im, v_o_D, v_o_w_glu, v_f_norm_g, v_f_w_up, v_f_conv_w, v_f_conv_b, v_f_w_down, v_final_norm_g):
    given = dict(x=x, e_norm_g=e_norm_g, e_w_in=e_w_in, e_mu=e_mu, e_w0=e_w0, e_w2=e_w2, e_a0=e_a0, e_a2=e_a2, e_g2=e_g2, e_k_k=e_k_k, e_k_a=e_k_a, e_r_k=e_r_k, e_ln_w=e_ln_w, e_ln_b=e_ln_b, e_conv_w=e_conv_w, e_conv_b=e_conv_b, e_gate_a_w=e_gate_a_w, e_gate_a_b=e_gate_a_b, e_gate_x_w=e_gate_x_w, e_gate_x_b=e_gate_x_b, e_lru_lambda=e_lru_lambda, e_w_out=e_w_out, o_norm_g=o_norm_g, o_w_in=o_w_in, o_A_re=o_A_re, o_A_im=o_A_im, o_log_dt=o_log_dt, o_B_re=o_B_re, o_B_im=o_B_im, o_C_re=o_C_re, o_C_im=o_C_im, o_D=o_D, o_w_glu=o_w_glu, f_norm_g=f_norm_g, f_w_up=f_w_up, f_conv_w=f_conv_w, f_conv_b=f_conv_b, f_w_down=f_w_down, final_norm_g=final_norm_g, loss_target=loss_target, m_e_norm_g=m_e_norm_g, m_e_w_in=m_e_w_in, m_e_mu=m_e_mu, m_e_w0=m_e_w0, m_e_w2=m_e_w2, m_e_a0=m_e_a0, m_e_a2=m_e_a2, m_e_g2=m_e_g2, m_e_k_k=m_e_k_k, m_e_k_a=m_e_k_a, m_e_r_k=m_e_r_k, m_e_ln_w=m_e_ln_w, m_e_ln_b=m_e_ln_b, m_e_conv_w=m_e_conv_w, m_e_conv_b=m_e_conv_b, m_e_gate_a_w=m_e_gate_a_w, m_e_gate_a_b=m_e_gate_a_b, m_e_gate_x_w=m_e_gate_x_w, m_e_gate_x_b=m_e_gate_x_b, m_e_lru_lambda=m_e_lru_lambda, m_e_w_out=m_e_w_out, m_o_norm_g=m_o_norm_g, m_o_w_in=m_o_w_in, m_o_A_re=m_o_A_re, m_o_A_im=m_o_A_im, m_o_log_dt=m_o_log_dt, m_o_B_re=m_o_B_re, m_o_B_im=m_o_B_im, m_o_C_re=m_o_C_re, m_o_C_im=m_o_C_im, m_o_D=m_o_D, m_o_w_glu=m_o_w_glu, m_f_norm_g=m_f_norm_g, m_f_w_up=m_f_w_up, m_f_conv_w=m_f_conv_w, m_f_conv_b=m_f_conv_b, m_f_w_down=m_f_w_down, m_final_norm_g=m_final_norm_g, v_e_norm_g=v_e_norm_g, v_e_w_in=v_e_w_in, v_e_mu=v_e_mu, v_e_w0=v_e_w0, v_e_w2=v_e_w2, v_e_a0=v_e_a0, v_e_a2=v_e_a2, v_e_g2=v_e_g2, v_e_k_k=v_e_k_k, v_e_k_a=v_e_k_a, v_e_r_k=v_e_r_k, v_e_ln_w=v_e_ln_w, v_e_ln_b=v_e_ln_b, v_e_conv_w=v_e_conv_w, v_e_conv_b=v_e_conv_b, v_e_gate_a_w=v_e_gate_a_w, v_e_gate_a_b=v_e_gate_a_b, v_e_gate_x_w=v_e_gate_x_w, v_e_gate_x_b=v_e_gate_x_b, v_e_lru_lambda=v_e_lru_lambda, v_e_w_out=v_e_w_out, v_o_norm_g=v_o_norm_g, v_o_w_in=v_o_w_in, v_o_A_re=v_o_A_re, v_o_A_im=v_o_A_im, v_o_log_dt=v_o_log_dt, v_o_B_re=v_o_B_re, v_o_B_im=v_o_B_im, v_o_C_re=v_o_C_re, v_o_C_im=v_o_C_im, v_o_D=v_o_D, v_o_w_glu=v_o_w_glu, v_f_norm_g=v_f_norm_g, v_f_w_up=v_f_w_up, v_f_conv_w=v_f_conv_w, v_f_conv_b=v_f_conv_b, v_f_w_down=v_f_w_down, v_final_norm_g=v_final_norm_g)
    weights = {n: given[n] for n in TWIN_WEIGHTS}
    shared = {n: given[n] for n in SHARED_INPUTS}
    per_example = {n: given[n] for n in ['x']}
    grad_fn = _jax.value_and_grad(_loss, argnums=(0, 1))

    def one_microbatch(ex, loss_target):
        ex = dict(ex)
        diff = ex.pop(TWIN_DIFF_INPUT)
        return grad_fn(weights, diff, {**shared, **ex}, loss_target)

    if N_MICROBATCH == 1:
        loss, (grad_w, grad_x) = one_microbatch(per_example, given["loss_target"])
    else:
        def body(carry, xs):
            loss_sum, grad_sum = carry
            l_k, (gw_k, gx_k) = one_microbatch(xs[0], xs[1])
            with _jax.named_scope("update"):
                return (loss_sum + l_k, _jax.tree.map(_jnp.add, grad_sum, gw_k)), gx_k

        init = (_jnp.zeros((), _jnp.float32), _jax.tree.map(_jnp.zeros_like, weights))
        (loss, grad_w), grad_x = _jax.lax.scan(body, init, (per_example, given["loss_target"]))
    with _jax.named_scope("update"):
        delta_w, new_m, new_v = {}, {}, {}
        for n in TWIN_WEIGHTS:
            delta_w[n], new_m[n], new_v[n] = _adamw(weights[n], grad_w[n], given["m_" + n], given["v_" + n])
    return (loss, grad_x, *[grad_w[n] for n in TWIN_WEIGHTS], *[delta_w[n] for n in TWIN_WEIGHTS],
            *[new_m[n] for n in TWIN_WEIGHTS], *[new_v[n] for n in TWIN_WEIGHTS])
```

```python
import functools

import jax
import jax.numpy as jnp
from jax import lax
from jax.experimental import pallas as pl
from jax.experimental.pallas import tpu as pltpu

F32 = jnp.float32
BF16 = jnp.bfloat16
MESH = pl.DeviceIdType.MESH

HEAD = 64
RW = 512
N_HEADS = RW // HEAD
LRU_W = 512
SHIFT_COLS = 1792
W_LORA, A_LORA, G_LORA = 64, 64, 128
S5_GROUPS, S5_GROUP, S5_STATE = 64, 16, 64
D_FF = 2816
NORM_EPS = 1e-6
GN_EPS = 64e-5
LRU_C = 8.0
ADAM_LR, ADAM_B1, ADAM_B2, ADAM_EPS, ADAM_WD, ADAM_STEP = 0.001, 0.9, 0.999, 1e-08, 0.01, 10

VMEM_BIG = 56 * 1024 * 1024
VMEM_MID = 40 * 1024 * 1024
LANES = 128
PT = 16
WKV_CHUNK = 32
S5_SLAB = 128


def _cparams(sem=None, vmem=None):
    kw = {}
    if sem is not None:
        kw["dimension_semantics"] = sem
    if vmem is not None:
        kw["vmem_limit_bytes"] = vmem
    return pltpu.CompilerParams(**kw)


def _tile(dim, cands):
    for c in cands:
        if dim % c == 0:
            return c
    return dim


def _full(shape):
    n = len(shape)
    return pl.BlockSpec(shape, lambda *_: (0,) * n)


_TILES = (1408, 1024, 512, 256, 128)


def _matmul(a, b, mode, name, out_dtype=F32, add=None):
    if mode == "nn":
        (m, k), (k2, n) = a.shape, b.shape
    elif mode == "nt":
        (m, k), (n, k2) = a.shape, b.shape
    else:
        (k, m), (k2, n) = a.shape, b.shape
    assert k == k2, (a.shape, b.shape, mode)
    if mode == "tn":
        tm, tn, tk = _tile(m, _TILES), _tile(n, (1024, 512, 256, 128)), _tile(k, (512, 256, 128))
    else:
        tm, tn, tk = _tile(m, (512, 256, 128)), _tile(n, _TILES), _tile(k, _TILES)
    nk = k // tk
    dims = {"nn": (((1,), (0,)), ((), ())), "nt": (((1,), (1,)), ((), ())), "tn": (((0,), (0,)), ((), ()))}[mode]

    def body(*refs):
        if add is None:
            a_ref, b_ref, o_ref, acc = refs
            add_ref = None
        else:
            a_ref, b_ref, add_ref, o_ref, acc = refs
        kk = pl.program_id(2)

        @pl.when(kk == 0)
        def _():
            acc[...] = jnp.zeros_like(acc)

        acc[...] += lax.dot_general(a_ref[...].astype(BF16), b_ref[...].astype(BF16), dims,
                                    preferred_element_type=F32)

        @pl.when(kk == nk - 1)
        def _():
            r = acc[...]
            if add_ref is not None:
                r = r + add_ref[...]
            o_ref[...] = r.astype(o_ref.dtype)

    if mode == "nn":
        a_spec = pl.BlockSpec((tm, tk), lambda i, j, kk: (i, kk))
        b_spec = pl.BlockSpec((tk, tn), lambda i, j, kk: (kk, j))
    elif mode == "nt":
        a_spec = pl.BlockSpec((tm, tk), lambda i, j, kk: (i, kk))
        b_spec = pl.BlockSpec((tn, tk), lambda i, j, kk: (j, kk))
    else:
        a_spec = pl.BlockSpec((tk, tm), lambda i, j, kk: (kk, i))
        b_spec = pl.BlockSpec((tk, tn), lambda i, j, kk: (kk, j))
    o_spec = pl.BlockSpec((tm, tn), lambda i, j, kk: (i, j))
    in_specs = [a_spec, b_spec] + ([o_spec] if add is not None else [])
    args = (a, b) + ((add,) if add is not None else ())
    return pl.pallas_call(
        body, name=name, grid=(m // tm, n // tn, nk),
        in_specs=in_specs, out_specs=o_spec,
        out_shape=jax.ShapeDtypeStruct((m, n), out_dtype),
        scratch_shapes=[pltpu.VMEM((tm, tn), F32)],
        compiler_params=_cparams(("parallel", "parallel", "arbitrary"), VMEM_MID),
    )(*args)


TOK = 256


def _rms(x, g):
    return x * lax.rsqrt(jnp.mean(x * x, axis=-1, keepdims=True) + NORM_EPS) * g


def _rms_fwd(x, g, name):
    t, d = x.shape

    def body(x_ref, g_ref, o_ref):
        o_ref[...] = _rms(x_ref[...], g_ref[...]).astype(BF16)

    row = pl.BlockSpec((TOK, d), lambda i: (i, 0))
    return pl.pallas_call(body, name=name, grid=(t // TOK,), in_specs=[row, _full((1, d))], out_specs=row,
                          out_shape=jax.ShapeDtypeStruct((t, d), BF16),
                          compiler_params=_cparams(("parallel",)))(x, g)


def _rms_bwd(x, g, dxn, res, name):
    t, d = x.shape

    def body(x_ref, g_ref, d_ref, res_ref, dx_ref, dg_ref):
        _, vjp = jax.vjp(_rms, x_ref[...], g_ref[...])
        dx, dg = vjp(d_ref[...].astype(F32))
        dx_ref[...] = dx + res_ref[...]

        @pl.when(pl.program_id(0) == 0)
        def _():
            dg_ref[...] = jnp.zeros_like(dg_ref)

        dg_ref[...] += dg

    row = pl.BlockSpec((TOK, d), lambda i: (i, 0))
    return pl.pallas_call(body, name=name, grid=(t // TOK,), in_specs=[row, _full((1, d)), row, row],
                          out_specs=[row, _full((1, d))],
                          out_shape=[jax.ShapeDtypeStruct((t, d), F32), jax.ShapeDtypeStruct((1, d), F32)],
                          compiler_params=_cparams(("arbitrary",)))(x, g, dxn, res)


def _loss_head(x, g, tgt):
    t, d = x.shape

    def body(x_ref, g_ref, t_ref, l_ref, dx_ref, dg_ref):
        tg = t_ref[...]

        def fn(xv, gv):
            err = _rms(xv, gv) - tg
            per_tok = jnp.mean(err * err, axis=-1, keepdims=True)
            return 0.5 * jnp.sum(per_tok, axis=0, keepdims=True)

        l, vjp = jax.vjp(fn, x_ref[...], g_ref[...])
        dx, dg = vjp(jnp.ones((1, 1), F32))
        dx_ref[...] = dx

        @pl.when(pl.program_id(0) == 0)
        def _():
            dg_ref[...] = jnp.zeros_like(dg_ref)
            l_ref[...] = jnp.zeros_like(l_ref)

        dg_ref[...] += dg
        l_ref[...] += jnp.broadcast_to(l, l_ref.shape)

    row = pl.BlockSpec((TOK, d), lambda i: (i, 0))
    return pl.pallas_call(body, name="loss_head", grid=(t // TOK,), in_specs=[row, _full((1, d)), row],
                          out_specs=[_full((1, LANES)), row, _full((1, d))],
                          out_shape=[jax.ShapeDtypeStruct((1, LANES), F32), jax.ShapeDtypeStruct((t, d), F32),
                                     jax.ShapeDtypeStruct((1, d), F32)],
                          compiler_params=_cparams(("arbitrary",)))(x, g, tgt)


def _glu_fwd(x, z):
    t, d = x.shape

    def body(x_ref, v_ref, g_ref, o_ref):
        o_ref[...] = x_ref[...] + v_ref[...] * jax.nn.sigmoid(g_ref[...])

    row = pl.BlockSpec((TOK, d), lambda i: (i, 0))
    gate = pl.BlockSpec((TOK, d), lambda i: (i, 1))
    return pl.pallas_call(body, name="glu_fwd", grid=(t // TOK,), in_specs=[row, row, gate], out_specs=row,
                          out_shape=jax.ShapeDtypeStruct((t, d), F32),
                          compiler_params=_cparams(("parallel",)))(x, z, z)


def _glu_bwd(z, g):
    t, d = g.shape

    def body(v_ref, g_ref, d_ref, o_ref):
        s = jax.nn.sigmoid(g_ref[...])
        dy = d_ref[...]
        o_ref[:, :d] = (dy * s).astype(BF16)
        o_ref[:, d:] = (dy * v_ref[...] * s * (1.0 - s)).astype(BF16)

    row = pl.BlockSpec((TOK, d), lambda i: (i, 0))
    gate = pl.BlockSpec((TOK, d), lambda i: (i, 1))
    return pl.pallas_call(body, name="glu_bwd", grid=(t // TOK,), in_specs=[row, gate, row],
                          out_specs=pl.BlockSpec((TOK, 2 * d), lambda i: (i, 0)),
                          out_shape=jax.ShapeDtypeStruct((t, 2 * d), BF16),
                          compiler_params=_cparams(("parallel",)))(z, z, g)


def _shift_down(x, d):
    row = lax.broadcasted_iota(jnp.int32, x.shape, 0)
    return jnp.where(row < d, 0.0, pltpu.roll(x, d, 0))


def _shift_up(x, d):
    n = x.shape[0]
    row = lax.broadcasted_iota(jnp.int32, x.shape, 0)
    return jnp.where(row >= n - d, 0.0, pltpu.roll(x, n - d, 0))


def _make_sd():
    @functools.partial(jax.custom_vjp, nondiff_argnums=(1,))
    def sd(x, d):
        return _shift_down(x, d)

    def fwd(x, d):
        return _shift_down(x, d), None

    def bwd(d, _, g):
        return (_shift_up(g, d),)

    sd.defvjp(fwd, bwd)
    return sd


def _lin_scan(a, u, reverse=False):
    n = a.shape[0]
    row = lax.broadcasted_iota(jnp.int32, a.shape, 0)
    d = 1
    while d < n:
        if reverse:
            keep = row < n - d
            a_s, u_s = pltpu.roll(a, n - d, 0), pltpu.roll(u, n - d, 0)
        else:
            keep = row >= d
            a_s, u_s = pltpu.roll(a, d, 0), pltpu.roll(u, d, 0)
        u = u + a * jnp.where(keep, u_s, 0.0)
        a = a * jnp.where(keep, a_s, 1.0)
        d *= 2
    return u


def _make_scan():
    @jax.custom_vjp
    def scan(a, u):
        return _lin_scan(a, u)

    def fwd(a, u):
        h = _lin_scan(a, u)
        return h, (a, h)

    def bwd(res, dh):
        a, h = res
        g = _lin_scan(_shift_up(a, 1), dh, reverse=True)
        return g * _shift_down(h, 1), g

    scan.defvjp(fwd, bwd)
    return scan


def _acc_out(ref, val):
    @pl.when(pl.program_id(0) == 0)
    def _():
        ref[...] = jnp.zeros_like(ref)

    ref[...] += val


FFN_CW = 128


def _ffn_fn(hg, hv, wg, wv, bg, bv, sd):
    cg = wg[0:1] * sd(hg, 2) + wg[1:2] * sd(hg, 1) + wg[2:3] * hg + bg
    cv = wv[0:1] * sd(hv, 2) + wv[1:2] * sd(hv, 1) + wv[2:3] * hv + bv
    return jax.nn.silu(cg) * cv


def _ffn_specs(t):
    nb = D_FF // FFN_CW
    col = lambda r, off: pl.BlockSpec((r, FFN_CW), lambda j: (0, j + off))
    return nb, [col(t, 0), col(t, nb), col(3, 0), col(3, nb), col(1, 0), col(1, nb)], col


def _ffn_mid_fwd(h, cw, cb, name):
    t = h.shape[0]
    nb, in_specs, col = _ffn_specs(t)

    def body(hg, hv, wg, wv, bg, bv, o_ref):
        o_ref[...] = _ffn_fn(hg[...], hv[...], wg[...], wv[...], bg[...], bv[...], _shift_down).astype(BF16)

    return pl.pallas_call(body, name=name, grid=(nb,), in_specs=in_specs, out_specs=col(t, 0),
                          out_shape=jax.ShapeDtypeStruct((t, D_FF), BF16),
                          compiler_params=_cparams(("parallel",), VMEM_MID))(h, h, cw, cw, cb, cb)


def _ffn_mid_bwd(h, cw, cb, dact, name):
    t = h.shape[0]
    nb, in_specs, col = _ffn_specs(t)

    def body(hg, hv, wg, wv, bg, bv, d_ref, dhg, dhv, dwg, dwv, dbg, dbv):
        fn = functools.partial(_ffn_fn, sd=_make_sd())
        _, vjp = jax.vjp(fn, hg[...], hv[...], wg[...], wv[...], bg[...], bv[...])
        g = vjp(d_ref[...])
        dhg[...] = g[0].astype(BF16)
        dhv[...] = g[1].astype(BF16)
        dwg[...], dwv[...], dbg[...], dbv[...] = g[2], g[3], g[4], g[5]

    big = jax.ShapeDtypeStruct((t, D_FF), BF16)
    w3 = jax.ShapeDtypeStruct((3, D_FF), F32)
    b1 = jax.ShapeDtypeStruct((1, D_FF), F32)
    return pl.pallas_call(body, name=name, grid=(nb,), in_specs=in_specs + [col(t, 0)],
                          out_specs=[col(t, 0), col(t, 0), col(3, 0), col(3, 0), col(1, 0), col(1, 0)],
                          out_shape=[big, big, w3, w3, b1, b1],
                          compiler_params=_cparams(("parallel",), VMEM_BIG))(h, h, cw, cw, cb, cb, dact)


TS_CW = 256


def _tshift_fn(p, mu, sd):
    return p + mu * (sd(p, 1) - p)


def _tshift_fwd(p, mu):
    t = p.shape[0]
    col = lambda r: pl.BlockSpec((r, TS_CW), lambda j: (0, j))

    def body(p_ref, mu_ref, o_ref):
        o_ref[...] = _tshift_fn(p_ref[...], mu_ref[...], _shift_down)

    return pl.pallas_call(body, name="tshift_fwd", grid=(SHIFT_COLS // TS_CW,), in_specs=[col(t), col(1)],
                          out_specs=col(t), out_shape=jax.ShapeDtypeStruct((t, SHIFT_COLS), F32),
                          compiler_params=_cparams(("parallel",), VMEM_MID))(p, mu)


def _tshift_bwd(p, mu, dpam):
    t = p.shape[0]
    col = lambda r: pl.BlockSpec((r, TS_CW), lambda j: (0, j))

    def body(p_ref, mu_ref, d_ref, dp_ref, dmu_ref):
        _, vjp = jax.vjp(functools.partial(_tshift_fn, sd=_make_sd()), p_ref[...], mu_ref[...])
        dp, dmu = vjp(d_ref[...])
        dp_ref[...] = dp.astype(BF16)
        dmu_ref[...] = dmu

    return pl.pallas_call(body, name="tshift_bwd", grid=(SHIFT_COLS // TS_CW,), in_specs=[col(t), col(1), col(t)],
                          out_specs=[col(t), col(1)],
                          out_shape=[jax.ShapeDtypeStruct((t, SHIFT_COLS), BF16),
                                     jax.ShapeDtypeStruct((1, SHIFT_COLS), F32)],
                          compiler_params=_cparams(("parallel",), VMEM_MID))(p, mu, dpam)


_HI = lax.Precision.HIGHEST
_O = (0, RW, 2 * RW, 3 * RW, 3 * RW + W_LORA, 3 * RW + W_LORA + A_LORA, SHIFT_COLS)


def _seg(x, gm):
    return jnp.dot(x, gm, precision=_HI)


def _prep_fn(r, k, v, wd, ad, gd, w0, w2, a0, a2, g2, k_k, k_a, gm):
    w_log = -jax.nn.softplus(-(w0 + jnp.tanh(wd) @ w2)) - 0.5
    decay = jnp.exp(-jnp.exp(w_log))
    a = jax.nn.sigmoid(a0 + ad @ a2)
    g = jax.nn.sigmoid(gd) @ g2
    kk = k * k_k
    kk = kk / jnp.maximum(jnp.sqrt(_seg(kk * kk, gm)), 1e-12)
    k2 = k * (1.0 + (a - 1.0) * k_a)
    return r, decay, k2, v, -kk, kk * a, g


_PREP_W = ("w0", "w2", "a0", "a2", "g2", "k_k", "k_a")


def _prep_wspecs(w):
    return [_full(w[n].shape) for n in _PREP_W] + [_full((RW, RW))]


def _rwkv_prep_fwd(pam, w, gm):
    t = pam.shape[0]

    def body(p_ref, *refs):
        wr, outs = refs[:8], refs[8:]
        pieces = [p_ref[:, _O[i]:_O[i + 1]] for i in range(6)]
        res = _prep_fn(*pieces, *[x[...] for x in wr])
        for o, val in zip(outs, res):
            o[...] = val

    row = lambda c: pl.BlockSpec((TOK, c), lambda i: (i, 0))
    return pl.pallas_call(body, name="rwkv_prep_fwd", grid=(t // TOK,),
                          in_specs=[row(SHIFT_COLS)] + _prep_wspecs(w), out_specs=[row(RW)] * 7,
                          out_shape=[jax.ShapeDtypeStruct((t, RW), F32)] * 7,
                          compiler_params=_cparams(("parallel",), VMEM_MID))(pam, *[w[n] for n in _PREP_W], gm)


def _rwkv_prep_bwd(pam, w, gm, cts, more):
    t = pam.shape[0]

    def body(p_ref, *refs):
        wr, ct, ex, dp_ref, dws = refs[:8], refs[8:15], refs[15:18], refs[18], refs[19:]
        pieces = [p_ref[:, _O[i]:_O[i + 1]] for i in range(6)]
        fn = lambda *a: _prep_fn(*a, wr[7][...])
        _, vjp = jax.vjp(fn, *pieces, *[x[...] for x in wr[:7]])
        c = [x[...] for x in ct]
        c[0] = c[0] + ex[0][...]
        c[2] = c[2] + ex[1][...]
        c[3] = c[3] + ex[2][...]
        g = vjp(tuple(c))
        for i in range(6):
            dp_ref[:, _O[i]:_O[i + 1]] = g[i]
        for o, val in zip(dws, g[6:]):
            _acc_out(o, val)

    row = lambda c: pl.BlockSpec((TOK, c), lambda i: (i, 0))
    return pl.pallas_call(body, name="rwkv_prep_bwd", grid=(t // TOK,),
                          in_specs=[row(SHIFT_COLS)] + _prep_wspecs(w) + [row(RW)] * 10,
                          out_specs=[row(SHIFT_COLS)] + [_full(w[n].shape) for n in _PREP_W],
                          out_shape=[jax.ShapeDtypeStruct((t, SHIFT_COLS), F32)]
                          + [jax.ShapeDtypeStruct(w[n].shape, F32) for n in _PREP_W],
                          compiler_params=_cparams(("arbitrary",), VMEM_MID))(
                              pam, *[w[n] for n in _PREP_W], gm, *cts, *more)


def _post_fn(y, r, k2, v, g, ln_w, ln_b, r_k, gm):
    inv = 1.0 / HEAD
    d = y - _seg(y, gm) * inv
    yn = d * lax.rsqrt(_seg(d * d, gm) * inv + GN_EPS) * ln_w + ln_b
    bonus = _seg(r * k2 * r_k, gm) * v
    return (yn + bonus) * g


def _rwkv_post_fwd(y, r, k2, v, g, ln_w, ln_b, r_k, gm):
    t = y.shape[0]

    def body(*refs):
        o_ref = refs[-1]
        o_ref[...] = _post_fn(*[x[...] for x in refs[:-1]]).astype(BF16)

    row = pl.BlockSpec((TOK, RW), lambda i: (i, 0))
    return pl.pallas_call(body, name="rwkv_post_fwd", grid=(t // TOK,),
                          in_specs=[row] * 5 + [_full((1, RW))] * 3 + [_full((RW, RW))], out_specs=row,
                          out_shape=jax.ShapeDtypeStruct((t, RW), BF16),
                          compiler_params=_cparams(("parallel",), VMEM_MID))(y, r, k2, v, g, ln_w, ln_b, r_k, gm)


def _rwkv_post_bwd(y, r, k2, v, g, ln_w, ln_b, r_k, gm, dya):
    t = y.shape[0]

    def body(*refs):
        ins, gm_ref, d_ref, outs = refs[:8], refs[8], refs[9], refs[10:]
        fn = lambda *a: _post_fn(*a, gm_ref[...])
        _, vjp = jax.vjp(fn, *[x[...] for x in ins])
        gr = vjp(d_ref[...])
        for o, val in zip(outs[:5], gr[:5]):
            o[...] = val
        for o, val in zip(outs[5:], gr[5:]):
            _acc_out(o, val)

    row = pl.BlockSpec((TOK, RW), lambda i: (i, 0))
    vec = _full((1, RW))
    return pl.pallas_call(body, name="rwkv_post_bwd", grid=(t // TOK,),
                          in_specs=[row] * 5 + [vec] * 3 + [_full((RW, RW)), row],
                          out_specs=[row] * 5 + [vec] * 3,
                          out_shape=[jax.ShapeDtypeStruct((t, RW), F32)] * 5 + [jax.ShapeDtypeStruct((1, RW), F32)] * 3,
                          compiler_params=_cparams(("arbitrary",), VMEM_MID))(y, r, k2, v, g, ln_w, ln_b, r_k, gm, dya)


def _to_pt(x):
    t = x.shape[0]
    return x.reshape(t // PT, PT, N_HEADS, HEAD).transpose(0, 3, 2, 1).reshape(t // PT, HEAD, N_HEADS * PT)


def _from_pt(x):
    n = x.shape[0]
    return x.reshape(n, HEAD, N_HEADS, PT).transpose(0, 3, 2, 1).reshape(n * PT, N_HEADS * HEAD)


def _lane_sum(x):
    return jnp.sum(x, axis=-1, keepdims=True)


def _pair_consts():
    lane = lax.broadcasted_iota(jnp.int32, (HEAD, LANES), 1)
    return lane, lane < HEAD


def _seg_sum_pair(x, first):
    return jnp.where(first, _lane_sum(jnp.where(first, x, 0.0)), _lane_sum(jnp.where(first, 0.0, x)))


def _col_pair(tile, lane, first, pair, j):
    c0 = _lane_sum(jnp.where(lane == (2 * pair) * PT + j, tile, 0.0))
    c1 = _lane_sum(jnp.where(lane == (2 * pair + 1) * PT + j, tile, 0.0))
    return jnp.where(first, c0, c1), c0, c1


def _wkv_fwd(r, w, k, z, b, v_pt):
    t = r.shape[0]
    nc = t // WKV_CHUNK
    tiles = WKV_CHUNK // PT

    def body(r_ref, w_ref, k_ref, z_ref, b_ref, v_ref, y_ref, s_all, s_ref):
        @pl.when(pl.program_id(0) == 0)
        def _():
            s_ref[...] = jnp.zeros_like(s_ref)

        lane, first = _pair_consts()
        for tl in range(tiles):
            vt = v_ref[tl]

            def group(gi, ytile):
                base = pl.multiple_of(tl * PT + gi * 8, 8)
                rows = [ref[pl.ds(base, 8), :] for ref in (r_ref, w_ref, k_ref, z_ref, b_ref)]
                for jj in range(8):
                    j = gi * 8 + jj
                    s_all[base + jj] = s_ref[...]
                    for p in range(N_HEADS // 2):
                        cs = slice(p * LANES, (p + 1) * LANES)
                        rr, wr, kr, zr, br = [x[jj:jj + 1, cs] for x in rows]
                        s = s_ref[:, cs]
                        sa = _seg_sum_pair(s * zr, first)
                        vc, _, _ = _col_pair(vt, lane, first, p, j)
                        s = s * wr + sa * br + vc * kr
                        s_ref[:, cs] = s
                        pr = s * rr
                        y0 = _lane_sum(jnp.where(first, pr, 0.0))
                        y1 = _lane_sum(jnp.where(first, 0.0, pr))
                        ytile = jnp.where(lane == (2 * p) * PT + j, y0, ytile)
                        ytile = jnp.where(lane == (2 * p + 1) * PT + j, y1, ytile)
                return ytile

            y_ref[tl] = lax.fori_loop(0, PT // 8, group, jnp.zeros((HEAD, LANES), F32))

    row = pl.BlockSpec((WKV_CHUNK, RW), lambda i: (i, 0))
    pt = pl.BlockSpec((tiles, HEAD, LANES), lambda i: (i, 0, 0))
    return pl.pallas_call(
        body, name="wkv_fwd", grid=(nc,), in_specs=[row] * 5 + [pt],
        out_specs=[pt, pl.BlockSpec((WKV_CHUNK, HEAD, RW), lambda i: (i, 0, 0))],
        out_shape=[jax.ShapeDtypeStruct((t // PT, HEAD, LANES), F32), jax.ShapeDtypeStruct((t, HEAD, RW), F32)],
        scratch_shapes=[pltpu.VMEM((HEAD, RW), F32)],
        compiler_params=_cparams(("arbitrary",), VMEM_MID))(r, w, k, z, b, v_pt)


def _wkv_bwd(r, w, k, z, b, v_pt, s_all, dy_pt):
    t = r.shape[0]
    nc = t // WKV_CHUNK
    tiles = WKV_CHUNK // PT

    def body(r_ref, w_ref, k_ref, z_ref, b_ref, v_ref, s_all_ref, dy_ref,
             dr_ref, dw_ref, dk_ref, dz_ref, db_ref, dv_ref, ds_ref):
        @pl.when(pl.program_id(0) == 0)
        def _():
            ds_ref[...] = jnp.zeros_like(ds_ref)

        lane, first = _pair_consts()
        col_sum = lambda x: jnp.sum(x, axis=0, keepdims=True)
        row8 = lax.broadcasted_iota(jnp.int32, (8, LANES), 0)
        for tl in reversed(range(tiles)):
            vt = v_ref[tl]
            dyt = dy_ref[tl]

            def group(gg, dvtile):
                gi = PT // 8 - 1 - gg
                base = pl.multiple_of(tl * PT + gi * 8, 8)
                rows = [ref[pl.ds(base, 8), :] for ref in (r_ref, w_ref, k_ref, z_ref, b_ref)]
                outs = (dr_ref, dw_ref, dk_ref, dz_ref, db_ref)
                tiles8 = {(id(o), p): jnp.zeros((8, LANES), F32) for o in outs for p in range(N_HEADS // 2)}
                for jj in reversed(range(8)):
                    j = gi * 8 + jj
                    for p in range(N_HEADS // 2):
                        cs = slice(p * LANES, (p + 1) * LANES)

                        def put(ref, val, p=p, jj=jj):
                            tiles8[(id(ref), p)] = jnp.where(row8 == jj, val, tiles8[(id(ref), p)])

                        rr, wr, kr, zr, br = [x[jj:jj + 1, cs] for x in rows]
                        sp = s_all_ref[base + jj, :, cs]
                        sa = _seg_sum_pair(sp * zr, first)
                        vc, _, _ = _col_pair(vt, lane, first, p, j)
                        dyc, _, _ = _col_pair(dyt, lane, first, p, j)
                        st = sp * wr + sa * br + vc * kr
                        ds = ds_ref[:, cs] + dyc * rr
                        put(dr_ref, col_sum(st * dyc))
                        dvk = ds * kr
                        dv0 = _lane_sum(jnp.where(first, dvk, 0.0))
                        dv1 = _lane_sum(jnp.where(first, 0.0, dvk))
                        dvtile = jnp.where(lane == (2 * p) * PT + j, dv0, dvtile)
                        dvtile = jnp.where(lane == (2 * p + 1) * PT + j, dv1, dvtile)
                        put(dk_ref, col_sum(ds * vc))
                        put(dw_ref, col_sum(sp * ds))
                        u = _seg_sum_pair(ds * br, first)
                        put(dz_ref, col_sum(sp * u))
                        put(db_ref, col_sum(ds * sa))
                        ds_ref[:, cs] = ds * wr + u * zr
                for o in outs:
                    for p in range(N_HEADS // 2):
                        o[pl.ds(base, 8), p * LANES:(p + 1) * LANES] = tiles8[(id(o), p)]
                return dvtile

            dv_ref[tl] = lax.fori_loop(0, PT // 8, group, jnp.zeros((HEAD, LANES), F32))

    rev = lambda i: nc - 1 - i
    row = pl.BlockSpec((WKV_CHUNK, RW), lambda i: (rev(i), 0))
    pt = pl.BlockSpec((tiles, HEAD, LANES), lambda i: (rev(i), 0, 0))
    return pl.pallas_call(
        body, name="wkv_bwd", grid=(nc,),
        in_specs=[row] * 5 + [pt, pl.BlockSpec((WKV_CHUNK, HEAD, RW), lambda i: (rev(i), 0, 0)), pt],
        out_specs=[row] * 5 + [pt],
        out_shape=[jax.ShapeDtypeStruct((t, RW), F32)] * 5 + [jax.ShapeDtypeStruct((t // PT, HEAD, LANES), F32)],
        scratch_shapes=[pltpu.VMEM((HEAD, RW), F32)],
        compiler_params=_cparams(("arbitrary",), VMEM_MID))(r, w, k, z, b, v_pt, s_all, dy_pt)


LRU_CW = 128
_BX0 = SHIFT_COLS // LRU_CW
_BG0 = (SHIFT_COLS + LRU_W) // LRU_CW


def _lru_fn(bx, bg, cw, cb, ga, ba, gx, bxb, lam, sd, scan):
    xc = cw[0:1] * sd(bx, 3) + cw[1:2] * sd(bx, 2) + cw[2:3] * sd(bx, 1) + cw[3:4] * bx + cb
    gr = jax.nn.sigmoid(xc @ ga + ba)
    gi = jax.nn.sigmoid(xc @ gx + bxb)
    log_a = -LRU_C * gr * jax.nn.softplus(-lam)
    a = jnp.exp(log_a)
    mult = jnp.sqrt(-jnp.tanh(log_a) * (jnp.exp(2.0 * log_a) + 1.0))
    return scan(a, xc * gi * mult) * jax.nn.gelu(bg)


def _lru_specs(t):
    col = lambda r, off=0: pl.BlockSpec((r, LRU_CW), lambda j: (0, j + off))
    diag = pl.BlockSpec((LRU_CW, LRU_CW), lambda j: (j, j))
    return col, [col(t, _BX0), col(t, _BG0), col(4), col(1), diag, col(1), diag, col(1), col(1)]


def _lru_fwd(p, cw, cb, ga, ba, gx, bxb, lam):
    t = p.shape[0]
    col, in_specs = _lru_specs(t)

    def body(*refs):
        o_ref = refs[-1]
        o_ref[...] = _lru_fn(*[x[...] for x in refs[:-1]], _shift_down, _lin_scan).astype(BF16)

    return pl.pallas_call(body, name="lru_fwd", grid=(LRU_W // LRU_CW,), in_specs=in_specs, out_specs=col(t),
                          out_shape=jax.ShapeDtypeStruct((t, LRU_W), BF16),
                          compiler_params=_cparams(("parallel",), VMEM_MID))(p, p, cw, cb, ga, ba, gx, bxb, lam)


def _lru_bwd(p, cw, cb, ga, ba, gx, bxb, lam, dyb):
    t = p.shape[0]
    col, in_specs = _lru_specs(t)

    def body(*refs):
        ins, d_ref, outs = refs[:9], refs[9], refs[10:]
        fn = functools.partial(_lru_fn, sd=_make_sd(), scan=_make_scan())
        _, vjp = jax.vjp(fn, *[x[...] for x in ins])
        g = vjp(d_ref[...])
        outs[0][...] = g[0].astype(BF16)
        outs[1][...] = g[1].astype(BF16)
        for o, val in zip(outs[2:], g[2:]):
            o[...] = val

    sq = pl.BlockSpec((LRU_CW, LRU_CW), lambda j: (j, 0))
    act = jax.ShapeDtypeStruct((t, LRU_W), BF16)
    vec = jax.ShapeDtypeStruct((1, LRU_W), F32)
    sqs = jax.ShapeDtypeStruct((LRU_W, LRU_CW), F32)
    return pl.pallas_call(body, name="lru_bwd", grid=(LRU_W // LRU_CW,), in_specs=in_specs + [col(t, RW // LRU_CW)],
                          out_specs=[col(t), col(t), col(4), col(1), sq, col(1), sq, col(1), col(1)],
                          out_shape=[act, act, jax.ShapeDtypeStruct((4, LRU_W), F32), vec, sqs, vec, sqs, vec, vec],
                          compiler_params=_cparams(("parallel",), VMEM_BIG))(p, p, cw, cb, ga, ba, gx, bxb, lam, dyb)


def _s5_disc_fn(a_re, a_im, log_dt, b_re, b_im, e):
    lam_re = jnp.minimum(a_re, -1e-4)
    lam_im = a_im
    dt = jnp.exp(log_dt)
    mag = jnp.exp(lam_re * dt)
    ab_re = mag * jnp.cos(lam_im * dt)
    ab_im = mag * jnp.sin(lam_im * dt)
    den = lam_re * lam_re + lam_im * lam_im
    zr = ab_re - 1.0
    q_re = jnp.dot((zr * lam_re + ab_im * lam_im) / den, e, precision=_HI)
    q_im = jnp.dot((ab_im * lam_re - zr * lam_im) / den, e, precision=_HI)
    return ab_re, ab_im, q_re * b_re - q_im * b_im, q_re * b_im + q_im * b_re


def _s5_disc_fwd(a_re, a_im, log_dt, b_re, b_im, e):
    def body(*refs):
        res = _s5_disc_fn(*[x[...] for x in refs[:6]])
        for o, val in zip(refs[6:], res):
            o[...] = val

    small = jax.ShapeDtypeStruct(a_re.shape, F32)
    wide = jax.ShapeDtypeStruct(b_re.shape, F32)
    return pl.pallas_call(body, name="s5_disc_fwd", out_shape=[small, small, wide, wide])(
        a_re, a_im, log_dt, b_re, b_im, e)


def _s5_disc_bwd(a_re, a_im, log_dt, b_re, b_im, e, cts):
    def body(*refs):
        ins, e_ref, ct, outs = refs[:5], refs[5], refs[6:10], refs[10:]
        _, vjp = jax.vjp(lambda *a: _s5_disc_fn(*a, e_ref[...]), *[x[...] for x in ins])
        for o, val in zip(outs, vjp(tuple(c[...] for c in ct))):
            o[...] = val

    shapes = [jax.ShapeDtypeStruct(x.shape, F32) for x in (a_re, a_im, log_dt, b_re, b_im)]
    return pl.pallas_call(body, name="s5_disc_bwd", out_shape=shapes)(a_re, a_im, log_dt, b_re, b_im, e, *cts)


def _cmul(a, b):
    return a[0] * b[0] - a[1] * b[1], a[0] * b[1] + a[1] * b[0]


def _s5_scan(sr, si, ab, reverse):
    n_tiles = sr.shape[0] // 8
    width = sr.shape[1]
    row8 = lax.broadcasted_iota(jnp.int32, (8, width), 0)
    p1 = ab
    p2 = _cmul(p1, p1)
    p4 = _cmul(p2, p2)
    pw = [p1]
    for _ in range(7):
        pw.append(_cmul(pw[-1], p1))
    cr = jnp.zeros((8, width), F32)
    ci = jnp.zeros((8, width), F32)
    for j in range(8):
        e = pw[7 - j] if reverse else pw[j]
        cr = jnp.where(row8 == j, e[0], cr)
        ci = jnp.where(row8 == j, e[1], ci)

    def tile(i, carry):
        idx = n_tiles - 1 - i if reverse else i
        base = pl.multiple_of(idx * 8, 8)
        x = (sr[pl.ds(base, 8), :], si[pl.ds(base, 8), :])
        for d, q in ((1, p1), (2, p2), (4, p4)):
            keep = row8 < 8 - d if reverse else row8 >= d
            amt = 8 - d if reverse else d
            sh = (jnp.where(keep, pltpu.roll(x[0], amt, 0), 0.0), jnp.where(keep, pltpu.roll(x[1], amt, 0), 0.0))
            m = _cmul(q, sh)
            x = (x[0] + m[0], x[1] + m[1])
        m = _cmul((cr, ci), carry)
        x = (x[0] + m[0], x[1] + m[1])
        sr[pl.ds(base, 8), :] = x[0]
        si[pl.ds(base, 8), :] = x[1]
        edge = slice(0, 1) if reverse else slice(7, 8)
        return x[0][edge], x[1][edge]

    zero = jnp.zeros((1, width), F32)
    lax.fori_loop(0, n_tiles, tile, (zero, zero))


_S5_W = S5_SLAB // S5_GROUP * S5_STATE


def _s5_specs(t):
    col = lambda r: pl.BlockSpec((r, S5_SLAB), lambda j: (0, j))
    bb = pl.BlockSpec((None, S5_SLAB, _S5_W), lambda j: (j, 0, 0))
    cd = pl.BlockSpec((None, _S5_W, S5_SLAB), lambda j: (j, 0, 0))
    ab = pl.BlockSpec((None, 1, _S5_W), lambda j: (j, 0, 0))
    return col, bb, cd, ab


def _s5_fwd(u, dvec, bbr, bbi, cdr, cdi, abr, abi):
    t, width = u.shape
    col, bb, cd, ab = _s5_specs(t)

    def body(u_ref, d_ref, bbr_ref, bbi_ref, cdr_ref, cdi_ref, abr_ref, abi_ref, o_ref, sr, si):
        uv = u_ref[...]
        sr[...] = jnp.dot(uv, bbr_ref[...], preferred_element_type=F32)
        si[...] = jnp.dot(uv, bbi_ref[...], preferred_element_type=F32)
        _s5_scan(sr, si, (abr_ref[...], abi_ref[...]), False)
        y = jnp.dot(sr[...], cdr_ref[...], preferred_element_type=F32) - jnp.dot(si[...], cdi_ref[...],
                                                                                 preferred_element_type=F32)
        o_ref[...] = jax.nn.gelu(y + d_ref[...] * uv).astype(BF16)

    return pl.pallas_call(body, name="s5_fwd", grid=(width // S5_SLAB,),
                          in_specs=[col(t), col(1), bb, bb, cd, cd, ab, ab], out_specs=col(t),
                          out_shape=jax.ShapeDtypeStruct((t, width), BF16),
                          scratch_shapes=[pltpu.VMEM((t, _S5_W), F32)] * 2,
                          compiler_params=_cparams(("parallel",), VMEM_BIG))(u, dvec, bbr, bbi, cdr, cdi, abr, abi)


def _s5_bwd(u, dvec, bbr, bbi, cdr, cdi, abr, abi, dyact):
    t, width = u.shape
    col, bb, cd, ab = _s5_specs(t)
    ns = width // S5_SLAB
    tn = (((0,), (0,)), ((), ()))
    nt = (((1,), (1,)), ((), ()))

    def body(u_ref, d_ref, bbr_ref, bbi_ref, cdr_ref, cdi_ref, abr_ref, abi_ref, dy_ref,
             du_ref, dd_ref, dbbr_ref, dbbi_ref, dcdr_ref, dcdi_ref, dabr_ref, dabi_ref, sr, si, gr, gi):
        uv = u_ref[...]
        dv = d_ref[...]
        abv = (abr_ref[...], abi_ref[...])
        sr[...] = jnp.dot(uv, bbr_ref[...], preferred_element_type=F32)
        si[...] = jnp.dot(uv, bbi_ref[...], preferred_element_type=F32)
        _s5_scan(sr, si, abv, False)
        y = jnp.dot(sr[...], cdr_ref[...], preferred_element_type=F32) - jnp.dot(si[...], cdi_ref[...],
                                                                                 preferred_element_type=F32)
        _, vjp = jax.vjp(jax.nn.gelu, y + dv * uv)
        (dpre,) = vjp(dy_ref[...].astype(F32))
        dd_ref[...] = jnp.sum(dpre * uv, axis=0, keepdims=True)
        dcdr_ref[...] = lax.dot_general(sr[...], dpre, tn, preferred_element_type=F32)
        dcdi_ref[...] = -lax.dot_general(si[...], dpre, tn, preferred_element_type=F32)
        gr[...] = lax.dot_general(dpre, cdr_ref[...], nt, preferred_element_type=F32)
        gi[...] = -lax.dot_general(dpre, cdi_ref[...], nt, preferred_element_type=F32)
        _s5_scan(gr, gi, (abv[0], -abv[1]), True)

        row8 = lax.broadcasted_iota(jnp.int32, (8, _S5_W), 0)

        def tile(i, carry):
            acc_r, acc_i, last_r, last_i = carry
            base = pl.multiple_of(i * 8, 8)
            s_r, s_i = sr[pl.ds(base, 8), :], si[pl.ds(base, 8), :]
            g_r, g_i = gr[pl.ds(base, 8), :], gi[pl.ds(base, 8), :]
            p_r = jnp.where(row8 == 0, last_r, pltpu.roll(s_r, 1, 0))
            p_i = jnp.where(row8 == 0, last_i, pltpu.roll(s_i, 1, 0))
            acc_r = acc_r + jnp.sum(g_r * p_r + g_i * p_i, axis=0, keepdims=True)
            acc_i = acc_i + jnp.sum(g_i * p_r - g_r * p_i, axis=0, keepdims=True)
            return acc_r, acc_i, s_r[7:8], s_i[7:8]

        zero = jnp.zeros((1, _S5_W), F32)
        acc_r, acc_i, _, _ = lax.fori_loop(0, t // 8, tile, (zero, zero, zero, zero))
        dabr_ref[...] = acc_r
        dabi_ref[...] = acc_i
        du_ref[...] = (dpre * dv + lax.dot_general(gr[...], bbr_ref[...], nt, preferred_element_type=F32)
                       + lax.dot_general(gi[...], bbi_ref[...], nt, preferred_element_type=F32))
        dbbr_ref[...] = lax.dot_general(uv, gr[...], tn, preferred_element_type=F32)
        dbbi_ref[...] = lax.dot_general(uv, gi[...], tn, preferred_element_type=F32)

    sds = jax.ShapeDtypeStruct
    return pl.pallas_call(
        body, name="s5_bwd", grid=(ns,), in_specs=[col(t), col(1), bb, bb, cd, cd, ab, ab, col(t)],
        out_specs=[col(t), col(1), bb, bb, cd, cd, ab, ab],
        out_shape=[sds((t, width), F32), sds((1, width), F32), sds((ns, S5_SLAB, _S5_W), F32),
                   sds((ns, S5_SLAB, _S5_W), F32), sds((ns, _S5_W, S5_SLAB), F32), sds((ns, _S5_W, S5_SLAB), F32),
                   sds((ns, 1, _S5_W), F32), sds((ns, 1, _S5_W), F32)],
        scratch_shapes=[pltpu.VMEM((t, _S5_W), F32)] * 4,
        compiler_params=_cparams(("parallel",), VMEM_BIG))(u, dvec, bbr, bbi, cdr, cdi, abr, abi, dyact)


def _gate_dense(w):
    h = w.shape[0]
    return jnp.einsum("hij,hg->higj", w, jnp.eye(h, dtype=F32)).reshape(h * HEAD, h * HEAD)


def _gate_blocks(d):
    x = d.reshape(LRU_W // LRU_CW, 2, HEAD, 2, HEAD)
    return jnp.einsum("tgihj,gh->tgij", x, jnp.eye(2, dtype=F32)).reshape(LRU_W // HEAD, HEAD, HEAD)


_GPS = S5_SLAB // S5_GROUP
_NS = S5_GROUPS // _GPS


def _s5_in_dense(bb):
    x = bb.reshape(_NS, _GPS, S5_STATE, S5_GROUP)
    return jnp.einsum("sgnc,gh->sgchn", x, jnp.eye(_GPS, dtype=F32)).reshape(_NS, S5_SLAB, _S5_W)


def _s5_in_blocks(d):
    x = d.reshape(_NS, _GPS, S5_GROUP, _GPS, S5_STATE)
    return jnp.einsum("sgchn,gh->sgnc", x, jnp.eye(_GPS, dtype=F32)).reshape(S5_GROUPS, S5_STATE * S5_GROUP)


def _s5_out_dense(c):
    x = c.reshape(_NS, _GPS, S5_GROUP, S5_STATE)
    return jnp.einsum("sgcn,gh->shngc", x, jnp.eye(_GPS, dtype=F32)).reshape(_NS, _S5_W, S5_SLAB)


def _s5_out_blocks(d):
    x = d.reshape(_NS, _GPS, S5_STATE, _GPS, S5_GROUP)
    return jnp.einsum("shngc,gh->sgcn", x, jnp.eye(_GPS, dtype=F32)).reshape(S5_GROUPS, S5_GROUP, S5_STATE)


def _local_step(x, tgt, w):
    d_model = x.shape[1]
    gs, gb = {}, {}
    gm = jnp.kron(jnp.eye(N_HEADS, dtype=F32), jnp.ones((HEAD, HEAD), F32))
    n_layers = w["f_norm_g"].shape[0]

    def ffn_fwd(xin, l):
        xn = _rms_fwd(xin, w["f_norm_g"][l:l + 1], f"rms_f{l}")
        h = _matmul(xn, w["f_w_up_t"][l], "nt", f"mm_f{l}_up")
        act = _ffn_mid_fwd(h, w["f_conv_w"][l], w["f_conv_b"][l:l + 1], f"ffn_mid_fwd{l}")
        return _matmul(act, w["f_w_down"][l], "nn", f"mm_f{l}_down", add=xin), (xin, xn, h, act)

    def ffn_bwd(g, saved, l):
        xin, xn, h, act = saved
        dact = _matmul(g, w["f_w_down"][l], "nt", f"mm_f{l}_dact")
        d_down = _matmul(act, g, "tn", f"mm_f{l}_ddown", out_dtype=BF16)
        dhg, dhv, dwg, dwv, dbg, dbv = _ffn_mid_bwd(h, w["f_conv_w"][l], w["f_conv_b"][l:l + 1], dact,
                                                    f"ffn_mid_bwd{l}")
        dh = jnp.concatenate([dhg, dhv], axis=1)
        dxn = _matmul(dh, w["f_w_up_t"][l], "nn", f"mm_f{l}_dxn")
        d_up = _matmul(dh, xn, "tn", f"mm_f{l}_dup", out_dtype=BF16)
        dx, dgn = _rms_bwd(xin, w["f_norm_g"][l:l + 1], dxn, g, f"rms_f{l}_bwd")
        return dx, d_up, d_down, jnp.concatenate([dwg, dwv], axis=1), jnp.concatenate([dbg, dbv], axis=1), dgn

    xn0 = _rms_fwd(x, w["e_norm_g"], "rms_e")
    p = _matmul(xn0, w["e_w_in_t"], "nt", "mm_e_in")
    pam = _tshift_fwd(p, w["e_mu"])
    pw = dict(w0=w["e_w0"], w2=w["e_w2"][0], a0=w["e_a0"], a2=w["e_a2"][0], g2=w["e_g2"][0],
              k_k=w["e_k_k"], k_a=w["e_k_a"])
    r, dec, k2, v, z, b, gate = _rwkv_prep_fwd(pam, pw, gm)
    v_pt = _to_pt(v)
    y_pt, s_all = _wkv_fwd(r, dec, k2, z, b, v_pt)
    y = _from_pt(y_pt)
    rk = w["e_r_k"].reshape(1, RW)
    ya = _rwkv_post_fwd(y, r, k2, v, gate, w["e_ln_w"], w["e_ln_b"], rk, gm)
    ga, gx = _gate_dense(w["e_gate_a_w"][0]), _gate_dense(w["e_gate_x_w"][0])
    lru_w = (w["e_conv_w"][0], w["e_conv_b"], ga, w["e_gate_a_b"], gx, w["e_gate_x_b"], w["e_lru_lambda"])
    yb = _lru_fwd(p, *lru_w)
    ycat = jnp.concatenate([ya, yb], axis=1)
    x1 = _matmul(ycat, w["e_w_out"], "nn", "mm_e_out", add=x)
    x2, ffn0 = ffn_fwd(x1, 0)

    xn1 = _rms_fwd(x2, w["o_norm_g"], "rms_o")
    u = _matmul(xn1, w["o_w_in"], "nn", "mm_o_in")
    expand = jnp.kron(jnp.eye(S5_STATE, dtype=F32), jnp.ones((1, S5_GROUP), F32))
    disc_in = (w["o_A_re"][0], w["o_A_im"][0], w["o_log_dt"].reshape(S5_GROUPS, 1),
               w["o_B_re"][0].reshape(S5_GROUPS, -1), w["o_B_im"][0].reshape(S5_GROUPS, -1), expand)
    ab_re, ab_im, bb_re, bb_im = _s5_disc_fwd(*disc_in)
    s5_w = (w["o_D"], _s5_in_dense(bb_re), _s5_in_dense(bb_im), _s5_out_dense(w["o_C_re"][0]),
            _s5_out_dense(w["o_C_im"][0]), ab_re.reshape(_NS, 1, _S5_W), ab_im.reshape(_NS, 1, _S5_W))
    yact = _s5_fwd(u, *s5_w)
    zz = _matmul(yact, w["o_w_glu_t"], "nt", "mm_o_glu")
    x3 = _glu_fwd(x2, zz)
    x4, ffn1 = ffn_fwd(x3, 1)

    loss, g, gs["final_norm_g"] = _loss_head(x4, w["final_norm_g"].reshape(1, d_model), tgt)
    gs["final_norm_g"] = gs["final_norm_g"].reshape(d_model)

    g, up1, down1, dcw1, dcb1, dfn1 = ffn_bwd(g, ffn1, 1)
    dz = _glu_bwd(zz, g)
    dyact = _matmul(dz, w["o_w_glu_t"], "nn", "mm_o_dyact")
    gb["o_w_glu_t"] = _matmul(dz, yact, "tn", "mm_o_dglu", out_dtype=BF16)
    du, gs["o_D"], dbbr, dbbi, dcdr, dcdi, dabr, dabi = _s5_bwd(u, *s5_w, dyact)
    gs["o_C_re"] = _s5_out_blocks(dcdr)[None]
    gs["o_C_im"] = _s5_out_blocks(dcdi)[None]
    cts = (dabr.reshape(S5_GROUPS, S5_STATE), dabi.reshape(S5_GROUPS, S5_STATE), _s5_in_blocks(dbbr),
           _s5_in_blocks(dbbi))
    da_re, da_im, dlog_dt, db_re, db_im = _s5_disc_bwd(*disc_in, cts)
    gs["o_A_re"], gs["o_A_im"], gs["o_log_dt"] = da_re[None], da_im[None], dlog_dt.reshape(1, S5_GROUPS)
    gs["o_B_re"] = db_re.reshape(w["o_B_re"].shape)
    gs["o_B_im"] = db_im.reshape(w["o_B_im"].shape)
    dxn = _matmul(du, w["o_w_in"], "nt", "mm_o_dxn")
    gb["o_w_in"] = _matmul(xn1, du, "tn", "mm_o_din", out_dtype=BF16)
    g, gs["o_norm_g"] = _rms_bwd(x2, w["o_norm_g"], dxn, g, "rms_o_bwd")

    g, up0, down0, dcw0, dcb0, dfn0 = ffn_bwd(g, ffn0, 0)
    gb["f_w_up_t"] = [up0, up1]
    gb["f_w_down"] = [down0, down1]
    gs["f_conv_w"] = jnp.stack([dcw0, dcw1])
    gs["f_conv_b"] = jnp.concatenate([dcb0, dcb1], axis=0)
    gs["f_norm_g"] = jnp.concatenate([dfn0, dfn1], axis=0)

    dycat = _matmul(g, w["e_w_out"], "nt", "mm_e_dycat")
    gb["e_w_out"] = _matmul(ycat, g, "tn", "mm_e_dout", out_dtype=BF16)
    dy, dr1, dk1, dv1, dgate, gs["e_ln_w"], gs["e_ln_b"], drk = _rwkv_post_bwd(
        y, r, k2, v, gate, w["e_ln_w"], w["e_ln_b"], rk, gm, dycat)
    gs["e_r_k"] = drk.reshape(w["e_r_k"].shape)
    dr2, ddec, dk2, dzz, dbb, dv_pt = _wkv_bwd(r, dec, k2, z, b, v_pt, s_all, _to_pt(dy))
    dpam, gs["e_w0"], dw2, gs["e_a0"], da2, dg2, gs["e_k_k"], gs["e_k_a"] = _rwkv_prep_bwd(
        pam, pw, gm, (dr2, ddec, dk2, _from_pt(dv_pt), dzz, dbb, dgate), (dr1, dk1, dv1))
    gs["e_w2"], gs["e_a2"], gs["e_g2"] = dw2[None], da2[None], dg2[None]
    dpa, gs["e_mu"] = _tshift_bwd(p, w["e_mu"], dpam)
    dbx, dbg, dcw, gs["e_conv_b"], dga, gs["e_gate_a_b"], dgx, gs["e_gate_x_b"], gs["e_lru_lambda"] = _lru_bwd(
        p, *lru_w, dycat)
    gs["e_conv_w"] = dcw[None]
    gs["e_gate_a_w"] = _gate_blocks(dga)[None]
    gs["e_gate_x_w"] = _gate_blocks(dgx)[None]
    dp = jnp.concatenate([dpa, dbx, dbg], axis=1)
    dxn = _matmul(dp, w["e_w_in_t"], "nn", "mm_e_dxn")
    gb["e_w_in_t"] = _matmul(dp, xn0, "tn", "mm_e_din", out_dtype=BF16)
    grad_x, gs["e_norm_g"] = _rms_bwd(x, w["e_norm_g"], dxn, g, "rms_e_bwd")
    return loss, grad_x, gb, gs


CAST_ROWS = 256


def _cast_shard(w3, layer, transpose, name):
    _, rows, cols = w3.shape
    tr = _tile(rows, (CAST_ROWS, 176, 128))

    def body(w_ref, o_ref):
        v = w_ref[...]
        o_ref[...] = (v.T if transpose else v).astype(BF16)

    in_spec = pl.BlockSpec((None, tr, cols), lambda i: (layer, i, 0))
    if transpose:
        out_spec, shape = pl.BlockSpec((cols, tr), lambda i: (0, i)), (cols, rows)
    else:
        out_spec, shape = pl.BlockSpec((tr, cols), lambda i: (i, 0)), (rows, cols)
    return pl.pallas_call(body, name=name, grid=(rows // tr,), in_specs=[in_spec], out_specs=out_spec,
                          out_shape=jax.ShapeDtypeStruct(shape, BF16),
                          compiler_params=_cparams(("parallel",), VMEM_MID))(w3)


_ANY = pl.BlockSpec(memory_space=pl.ANY)


def _coords():
    return lax.axis_index("x"), lax.axis_index("y"), lax.axis_index("c")


def _flip(v, d):
    return 1 - v if d else v


_CHIP_RELS = ((1, 0), (0, 1), (1, 1))
_DEV_RELS = tuple((dx, dy, dc) for dx in (0, 1) for dy in (0, 1) for dc in (0, 1))[1:]


def _gather_chips(arrs):
    n = len(arrs)
    nr = len(_CHIP_RELS)

    def body(*refs):
        ins, outs, (send, recv, loc) = refs[:n], refs[n:2 * n], refs[2 * n:]
        x, y, c = _coords()
        me = 2 * x + y
        locals_, sends, recvs = [], [], []
        for i in range(n):
            cp = pltpu.make_async_copy(ins[i], outs[i].at[me], loc.at[i])
            cp.start()
            locals_.append(cp)
            for j, (dx, dy) in enumerate(_CHIP_RELS):
                px, py = _flip(x, dx), _flip(y, dy)
                k = i * nr + j
                cp = pltpu.make_async_remote_copy(src_ref=ins[i], dst_ref=outs[i].at[me], send_sem=send.at[k],
                                                  recv_sem=recv.at[k], device_id=(px, py, c), device_id_type=MESH)
                cp.start()
                sends.append(cp)
                recvs.append(pltpu.make_async_remote_copy(
                    src_ref=ins[i], dst_ref=outs[i].at[2 * px + py], send_sem=send.at[k], recv_sem=recv.at[k],
                    device_id=(px, py, c), device_id_type=MESH))
        for cp in recvs:
            cp.wait_recv()
        for cp in sends:
            cp.wait_send()
        for cp in locals_:
            cp.wait()

    return pl.pallas_call(
        body, name="gather_chips", in_specs=[_ANY] * n, out_specs=[_ANY] * n,
        out_shape=[jax.ShapeDtypeStruct((4,) + a.shape, a.dtype) for a in arrs],
        scratch_shapes=[pltpu.SemaphoreType.DMA((n * nr,)), pltpu.SemaphoreType.DMA((n * nr,)),
                        pltpu.SemaphoreType.DMA((n,))])(*arrs)


def _scatter_devices(arrs):
    n = len(arrs)
    nr = len(_DEV_RELS)

    def body(*refs):
        ins, outs, (send, recv, loc) = refs[:n], refs[n:2 * n], refs[2 * n:]
        x, y, c = _coords()
        me = 4 * x + 2 * y + c
        locals_, sends, recvs = [], [], []
        for i in range(n):
            cp = pltpu.make_async_copy(ins[i].at[me], outs[i].at[me], loc.at[i])
            cp.start()
            locals_.append(cp)
            for j, (dx, dy, dc) in enumerate(_DEV_RELS):
                peer = (_flip(x, dx), _flip(y, dy), _flip(c, dc))
                pid = 4 * peer[0] + 2 * peer[1] + peer[2]
                k = i * nr + j
                cp = pltpu.make_async_remote_copy(src_ref=ins[i].at[pid], dst_ref=outs[i].at[me], send_sem=send.at[k],
                                                  recv_sem=recv.at[k], device_id=peer, device_id_type=MESH)
                cp.start()
                sends.append(cp)
                recvs.append(pltpu.make_async_remote_copy(
                    src_ref=ins[i].at[pid], dst_ref=outs[i].at[pid], send_sem=send.at[k], recv_sem=recv.at[k],
                    device_id=peer, device_id_type=MESH))
        for cp in recvs:
            cp.wait_recv()
        for cp in sends:
            cp.wait_send()
        for cp in locals_:
            cp.wait()

    return pl.pallas_call(
        body, name="scatter_devices", in_specs=[_ANY] * n, out_specs=[_ANY] * n,
        out_shape=[jax.ShapeDtypeStruct(a.shape, a.dtype) for a in arrs],
        scratch_shapes=[pltpu.SemaphoreType.DMA((n * nr,)), pltpu.SemaphoreType.DMA((n * nr,)),
                        pltpu.SemaphoreType.DMA((n,))])(*arrs)


def _sum_segments(stage, name):
    nd, seg, cols = stage.shape
    ts = _tile(seg, (256, 176, 128))

    def body(s_ref, o_ref):
        acc = s_ref[0].astype(F32)
        for d in range(1, nd):
            acc = acc + s_ref[d].astype(F32)
        o_ref[...] = acc

    return pl.pallas_call(body, name=name, grid=(seg // ts,),
                          in_specs=[pl.BlockSpec((nd, ts, cols), lambda i: (0, i, 0))],
                          out_specs=pl.BlockSpec((ts, cols), lambda i: (i, 0)),
                          out_shape=jax.ShapeDtypeStruct((seg, cols), F32),
                          compiler_params=_cparams(("parallel",), VMEM_MID))(stage)


def _exchange_sibling(arrs):
    n = len(arrs)

    def body(*refs):
        ins, outs, (send, recv, loc) = refs[:n], refs[n:2 * n], refs[2 * n:]
        x, y, c = _coords()
        sib = (x, y, 1 - c)
        locals_, sends, recvs = [], [], []
        for i in range(n):
            cp = pltpu.make_async_copy(ins[i], outs[i].at[c], loc.at[i])
            cp.start()
            locals_.append(cp)
            cp = pltpu.make_async_remote_copy(src_ref=ins[i], dst_ref=outs[i].at[c], send_sem=send.at[i],
                                              recv_sem=recv.at[i], device_id=sib, device_id_type=MESH)
            cp.start()
            sends.append(cp)
            recvs.append(pltpu.make_async_remote_copy(src_ref=ins[i], dst_ref=outs[i].at[1 - c], send_sem=send.at[i],
                                                      recv_sem=recv.at[i], device_id=sib, device_id_type=MESH))
        for cp in recvs:
            cp.wait_recv()
        for cp in sends:
            cp.wait_send()
        for cp in locals_:
            cp.wait()

    return pl.pallas_call(
        body, name="exchange_sibling", in_specs=[_ANY] * n, out_specs=[_ANY] * n,
        out_shape=[jax.ShapeDtypeStruct((2,) + a.shape, a.dtype) for a in arrs],
        scratch_shapes=[pltpu.SemaphoreType.DMA((n,)), pltpu.SemaphoreType.DMA((n,)),
                        pltpu.SemaphoreType.DMA((n,))])(*arrs)


def _allreduce_small(vec):
    nd, rows, lanes = vec.shape
    nr = len(_DEV_RELS)

    def body(in_ref, out_ref, stage, red, send, recv):
        x, y, c = _coords()
        me = 4 * x + 2 * y + c
        peers = []
        for dx, dy, dc in _DEV_RELS:
            peer = (_flip(x, dx), _flip(y, dy), _flip(c, dc))
            peers.append((peer, 4 * peer[0] + 2 * peer[1] + peer[2]))

        def copy(src, dst, k, peer):
            return pltpu.make_async_remote_copy(src_ref=src, dst_ref=dst, send_sem=send.at[k], recv_sem=recv.at[k],
                                                device_id=peer, device_id_type=MESH)

        first = [copy(in_ref.at[pid], stage.at[me], j, peer) for j, (peer, pid) in enumerate(peers)]
        for cp in first:
            cp.start()
        stage[me] = in_ref[me]
        for j, (peer, pid) in enumerate(peers):
            copy(in_ref.at[pid], stage.at[pid], j, peer).wait_recv()
        acc = stage[0]
        for d in range(1, nd):
            acc = acc + stage[d]
        red[...] = acc
        out_ref[me] = acc
        second = [copy(red, out_ref.at[me], nr + j, peer) for j, (peer, pid) in enumerate(peers)]
        for cp in second:
            cp.start()
        for j, (peer, pid) in enumerate(peers):
            copy(red, out_ref.at[pid], nr + j, peer).wait_recv()
        for cp in first + second:
            cp.wait_send()

    vm = pl.BlockSpec(memory_space=pltpu.VMEM)
    return pl.pallas_call(
        body, name="allreduce_small", in_specs=[vm], out_specs=vm,
        out_shape=jax.ShapeDtypeStruct(vec.shape, F32),
        scratch_shapes=[pltpu.VMEM(vec.shape, F32), pltpu.VMEM((rows, lanes), F32),
                        pltpu.SemaphoreType.DMA((2 * nr,)), pltpu.SemaphoreType.DMA((2 * nr,))],
        compiler_params=_cparams(None, VMEM_MID))(vec)


def _adam_math(w, g, m, v):
    m2 = ADAM_B1 * m + (1.0 - ADAM_B1) * g
    v2 = ADAM_B2 * v + (1.0 - ADAM_B2) * (g * g)
    m_hat = m2 / (1.0 - ADAM_B1 ** ADAM_STEP)
    v_hat = v2 / (1.0 - ADAM_B2 ** ADAM_STEP)
    return -ADAM_LR * (m_hat / (jnp.sqrt(v_hat) + ADAM_EPS) + ADAM_WD * w), m2, v2


def _adamw_big(w3, m3, v3, layer, g, transposed, name):
    _, rows, cols = w3.shape
    tr = 128 if transposed else _tile(rows, (256, 176, 128))

    def body(w_ref, m_ref, v_ref, g_ref, go_ref, d_ref, mo_ref, vo_ref):
        g_val = g_ref[...].T if transposed else g_ref[...]
        go_ref[...] = g_val
        d_ref[...], mo_ref[...], vo_ref[...] = _adam_math(w_ref[...], g_val, m_ref[...], v_ref[...])

    wspec = pl.BlockSpec((None, tr, cols), lambda i: (layer, i, 0))
    gspec = pl.BlockSpec((cols, tr), lambda i: (0, i)) if transposed else pl.BlockSpec((tr, cols), lambda i: (i, 0))
    ospec = pl.BlockSpec((tr, cols), lambda i: (i, 0))
    return pl.pallas_call(body, name=name, grid=(rows // tr,), in_specs=[wspec, wspec, wspec, gspec],
                          out_specs=[ospec] * 4, out_shape=[jax.ShapeDtypeStruct((rows, cols), F32)] * 4,
                          compiler_params=_cparams(("parallel",), VMEM_MID))(w3, m3, v3, g)


def _adamw_small(w, g, m, v):
    rows = w.shape[0]
    tr = _tile(rows, (512, 256, 128, 64, 32, 16, 8))

    def body(w_ref, g_ref, m_ref, v_ref, d_ref, mo_ref, vo_ref):
        d_ref[...], mo_ref[...], vo_ref[...] = _adam_math(w_ref[...], g_ref[...], m_ref[...], v_ref[...])

    spec = pl.BlockSpec((tr, LANES), lambda i: (i, 0))
    return pl.pallas_call(body, name="adamw_small", grid=(rows // tr,), in_specs=[spec] * 4, out_specs=[spec] * 3,
                          out_shape=[jax.ShapeDtypeStruct(w.shape, F32)] * 3,
                          compiler_params=_cparams(("parallel",)))(w, g, m, v)


def _pack(arrs, row_mult):
    flat = jnp.concatenate([a.reshape(-1).astype(F32) for a in arrs])
    rows = -(-flat.shape[0] // LANES)
    rows = -(-rows // row_mult) * row_mult
    return jnp.pad(flat, (0, rows * LANES - flat.shape[0])).reshape(rows, LANES)


def _unpack(packed, shapes):
    flat = packed.reshape(-1)
    out, off = [], 0
    for s in shapes:
        size = 1
        for d in s:
            size *= d
        out.append(flat[off:off + size].reshape(s))
        off += size
    return out


_SMALL_REP = ("e_norm_g", "e_mu", "e_w0", "e_a0", "e_k_k", "e_k_a", "e_r_k", "e_ln_w", "e_ln_b", "e_conv_b",
              "e_gate_a_w", "e_gate_a_b", "e_gate_x_w", "e_gate_x_b", "e_lru_lambda", "o_A_re", "o_A_im", "o_log_dt",
              "o_B_re", "o_B_im", "o_C_re", "o_C_im", "f_norm_g", "f_conv_b", "final_norm_g")
_SMALL_SH = ("e_w2", "e_a2", "e_g2", "e_conv_w", "o_norm_g", "o_D", "f_conv_w")
_LARGE = (("e_w_in", True), ("e_w_out", False), ("o_w_in", False), ("o_w_glu", True), ("f_w_up", True),
        ("f_w_down", False))
_ORDER = ("e_norm_g", "e_w_in", "e_mu", "e_w0", "e_w2", "e_a0", "e_a2", "e_g2", "e_k_k", "e_k_a", "e_r_k", "e_ln_w",
          "e_ln_b", "e_conv_w", "e_conv_b", "e_gate_a_w", "e_gate_a_b", "e_gate_x_w", "e_gate_x_b", "e_lru_lambda",
          "e_w_out", "o_norm_g", "o_w_in", "o_A_re", "o_A_im", "o_log_dt", "o_B_re", "o_B_im", "o_C_re", "o_C_im",
          "o_D", "o_w_glu", "f_norm_g", "f_w_up", "f_conv_w", "f_conv_b", "f_w_down", "final_norm_g")
N_CHIPS = 4
N_DEV = 8


def _step(x, tgt, wts, ms, vs):
    xi, yi, _ = _coords()
    chip = 2 * xi + yi

    send, keys = [], []
    for name, by_cols in _LARGE:
        for l in range(wts[name].shape[0]):
            send.append(_cast_shard(wts[name], l, by_cols, f"cast_{name}{l}"))
            keys.append((name, l))
    sh_shapes = [wts[n].shape for n in _SMALL_SH]
    send.append(_pack([wts[n] for n in _SMALL_SH], 8))
    got = _gather_chips(send)
    full = {n: wts[n] for n in _SMALL_REP}
    for (name, l), g in zip(keys, got[:-1]):
        key = name + "_t" if dict(_LARGE)[name] else name
        full.setdefault(key, []).append(g.reshape(N_CHIPS * g.shape[1], g.shape[2]))
    for key in ("e_w_in_t", "e_w_out", "o_w_in", "o_w_glu_t"):
        full[key] = full[key][0]
    per_chip = [_unpack(got[-1][k], sh_shapes) for k in range(N_CHIPS)]
    for i, n in enumerate(_SMALL_SH):
        full[n] = jnp.concatenate([per_chip[k][i] for k in range(N_CHIPS)], axis=-1)

    loss, grad_x, gb, gs = _local_step(x, tgt, full)

    parts, keys = [], []
    for name, by_cols in _LARGE:
        g = gb[name + "_t" if by_cols else name]
        for l, gl in enumerate(g if isinstance(g, list) else [g]):
            parts.append(gl.reshape(N_DEV, gl.shape[0] // N_DEV, gl.shape[1]))
            keys.append((name, l, by_cols))
    staged = _scatter_devices(parts)
    halves = [_sum_segments(s, f"sum_{name}{l}") for s, (name, l, _) in zip(staged, keys)]
    shards = _exchange_sibling(halves)
    outs = {}
    for s, (name, l, by_cols) in zip(shards, keys):
        res = _adamw_big(wts[name], ms[name], vs[name], l, s.reshape(2 * s.shape[1], s.shape[2]), by_cols,
                         f"adamw_{name}{l}")
        outs.setdefault(name, []).append(res)
    final = {}
    for name, res in outs.items():
        final[name] = [r[0][None] for r in zip(*res)] if len(res) == 1 else [jnp.stack(r) for r in zip(*res)]

    small = _SMALL_REP + _SMALL_SH
    shapes = [gs[n].shape for n in small]
    red = _allreduce_small(_pack([gs[n] for n in small], 8 * N_DEV).reshape(N_DEV, -1, LANES))
    tot = dict(zip(small, _unpack(red, shapes)))
    for n in _SMALL_SH:
        width = wts[n].shape[-1]
        tot[n] = lax.dynamic_slice_in_dim(tot[n], chip * width, width, axis=tot[n].ndim - 1)
    loc_shapes = [wts[n].shape for n in small]
    pk = lambda d: _pack([d[n] for n in small], 8)
    delta, new_m, new_v = _adamw_small(pk(wts), pk(tot), pk(ms), pk(vs))
    for n, g, d, m2, v2 in zip(small, [tot[n] for n in small], _unpack(delta, loc_shapes), _unpack(new_m, loc_shapes),
                               _unpack(new_v, loc_shapes)):
        final[n] = [g.reshape(wts[n].shape), d, m2, v2]

    loss = lax.psum(loss[0, 0], ("x", "y", "c"))
    res = [loss, grad_x[None]]
    for k in range(4):
        res += [final[n][k] for n in _ORDER]
    return tuple(res)


def kernel(x, e_norm_g, e_w_in, e_mu, e_w0, e_w2, e_a0, e_a2, e_g2, e_k_k, e_k_a, e_r_k, e_ln_w, e_ln_b, e_conv_w, e_conv_b, e_gate_a_w, e_gate_a_b, e_gate_x_w, e_gate_x_b, e_lru_lambda, e_w_out, o_norm_g, o_w_in, o_A_re, o_A_im, o_log_dt, o_B_re, o_B_im, o_C_re, o_C_im, o_D, o_w_glu, f_norm_g, f_w_up, f_conv_w, f_conv_b, f_w_down, final_norm_g, loss_target, m_e_norm_g, m_e_w_in, m_e_mu, m_e_w0, m_e_w2, m_e_a0, m_e_a2, m_e_g2, m_e_k_k, m_e_k_a, m_e_r_k, m_e_ln_w, m_e_ln_b, m_e_conv_w, m_e_conv_b, m_e_gate_a_w, m_e_gate_a_b, m_e_gate_x_w, m_e_gate_x_b, m_e_lru_lambda, m_e_w_out, m_o_norm_g, m_o_w_in, m_o_A_re, m_o_A_im, m_o_log_dt, m_o_B_re, m_o_B_im, m_o_C_re, m_o_C_im, m_o_D, m_o_w_glu, m_f_norm_g, m_f_w_up, m_f_conv_w, m_f_conv_b, m_f_w_down, m_final_norm_g, v_e_norm_g, v_e_w_in, v_e_mu, v_e_w0, v_e_w2, v_e_a0, v_e_a2, v_e_g2, v_e_k_k, v_e_k_a, v_e_r_k, v_e_ln_w, v_e_ln_b, v_e_conv_w, v_e_conv_b, v_e_gate_a_w, v_e_gate_a_b, v_e_gate_x_w, v_e_gate_x_b, v_e_lru_lambda, v_e_w_out, v_o_norm_g, v_o_w_in, v_o_A_re, v_o_A_im, v_o_log_dt, v_o_B_re, v_o_B_im, v_o_C_re, v_o_C_im, v_o_D, v_o_w_glu, v_f_norm_g, v_f_w_up, v_f_conv_w, v_f_conv_b, v_f_w_down, v_final_norm_g):
    args = locals()
    wts = {n: args[n] for n in _ORDER}
    ms = {n: args["m_" + n] for n in _ORDER}
    vs = {n: args["v_" + n] for n in _ORDER}
    return _step(x[0], loss_target[0], wts, ms, vs)
```

```python
import functools

import jax
import jax.numpy as jnp
from jax import lax
from jax.experimental import pallas as pl
from jax.experimental.pallas import tpu as pltpu

F32 = jnp.float32
BF16 = jnp.bfloat16
MESH = pl.DeviceIdType.MESH

HEAD = 64
RW = 512
N_HEADS = RW // HEAD
LRU_W = 512
SHIFT_COLS = 1792
W_LORA, A_LORA, G_LORA = 64, 64, 128
S5_GROUPS, S5_GROUP, S5_STATE = 64, 16, 64
D_FF = 2816
NORM_EPS = 1e-6
GN_EPS = 64e-5
LRU_C = 8.0
ADAM_LR, ADAM_B1, ADAM_B2, ADAM_EPS, ADAM_WD, ADAM_STEP = 0.001, 0.9, 0.999, 1e-08, 0.01, 10

VMEM_BIG = 56 * 1024 * 1024
VMEM_MID = 40 * 1024 * 1024
LANES = 128
PT = 16
WKV_CHUNK = 32
S5_SLAB = 128


def _cparams(sem=None, vmem=None):
    kw = {}
    if sem is not None:
        kw["dimension_semantics"] = sem
    if vmem is not None:
        kw["vmem_limit_bytes"] = vmem
    return pltpu.CompilerParams(**kw)


def _tile(dim, cands):
    for c in cands:
        if dim % c == 0:
            return c
    return dim


def _full(shape):
    n = len(shape)
    return pl.BlockSpec(shape, lambda *_: (0,) * n)


_TILES = (1408, 1024, 512, 256, 128)


def _matmul(a, b, mode, name, out_dtype=F32, add=None):
    if mode == "nn":
        (m, k), (k2, n) = a.shape, b.shape
    elif mode == "nt":
        (m, k), (n, k2) = a.shape, b.shape
    else:
        (k, m), (k2, n) = a.shape, b.shape
    assert k == k2, (a.shape, b.shape, mode)
    if mode == "tn":
        tm, tn, tk = _tile(m, _TILES), _tile(n, (1024, 512, 256, 128)), _tile(k, (512, 256, 128))
    else:
        tm, tn, tk = _tile(m, (512, 256, 128)), _tile(n, _TILES), _tile(k, _TILES)
    nk = k // tk
    dims = {"nn": (((1,), (0,)), ((), ())), "nt": (((1,), (1,)), ((), ())), "tn": (((0,), (0,)), ((), ()))}[mode]

    def body(*refs):
        if add is None:
            a_ref, b_ref, o_ref, acc = refs
            add_ref = None
        else:
            a_ref, b_ref, add_ref, o_ref, acc = refs
        kk = pl.program_id(2)

        @pl.when(kk == 0)
        def _():
            acc[...] = jnp.zeros_like(acc)

        acc[...] += lax.dot_general(a_ref[...].astype(BF16), b_ref[...].astype(BF16), dims,
                                    preferred_element_type=F32)

        @pl.when(kk == nk - 1)
        def _():
            r = acc[...]
            if add_ref is not None:
                r = r + add_ref[...]
            o_ref[...] = r.astype(o_ref.dtype)

    if mode == "nn":
        a_spec = pl.BlockSpec((tm, tk), lambda i, j, kk: (i, kk))
        b_spec = pl.BlockSpec((tk, tn), lambda i, j, kk: (kk, j))
    elif mode == "nt":
        a_spec = pl.BlockSpec((tm, tk), lambda i, j, kk: (i, kk))
        b_spec = pl.BlockSpec((tn, tk), lambda i, j, kk: (j, kk))
    else:
        a_spec = pl.BlockSpec((tk, tm), lambda i, j, kk: (kk, i))
        b_spec = pl.BlockSpec((tk, tn), lambda i, j, kk: (kk, j))
    o_spec = pl.BlockSpec((tm, tn), lambda i, j, kk: (i, j))
    in_specs = [a_spec, b_spec] + ([o_spec] if add is not None else [])
    args = (a, b) + ((add,) if add is not None else ())
    return pl.pallas_call(
        body, name=name, grid=(m // tm, n // tn, nk),
        in_specs=in_specs, out_specs=o_spec,
        out_shape=jax.ShapeDtypeStruct((m, n), out_dtype),
        scratch_shapes=[pltpu.VMEM((tm, tn), F32)],
        compiler_params=_cparams(("parallel", "parallel", "arbitrary"), VMEM_MID),
    )(*args)


TOK = 256


def _rms(x, g):
    return x * lax.rsqrt(jnp.mean(x * x, axis=-1, keepdims=True) + NORM_EPS) * g


def _rms_fwd(x, g, name):
    t, d = x.shape

    def body(x_ref, g_ref, o_ref):
        o_ref[...] = _rms(x_ref[...], g_ref[...]).astype(BF16)

    row = pl.BlockSpec((TOK, d), lambda i: (i, 0))
    return pl.pallas_call(body, name=name, grid=(t // TOK,), in_specs=[row, _full((1, d))], out_specs=row,
                          out_shape=jax.ShapeDtypeStruct((t, d), BF16),
                          compiler_params=_cparams(("parallel",)))(x, g)


def _rms_bwd(x, g, dxn, res, name):
    t, d = x.shape

    def body(x_ref, g_ref, d_ref, res_ref, dx_ref, dg_ref):
        _, vjp = jax.vjp(_rms, x_ref[...], g_ref[...])
        dx, dg = vjp(d_ref[...].astype(F32))
        dx_ref[...] = dx + res_ref[...]

        @pl.when(pl.program_id(0) == 0)
        def _():
            dg_ref[...] = jnp.zeros_like(dg_ref)

        dg_ref[...] += dg

    row = pl.BlockSpec((TOK, d), lambda i: (i, 0))
    return pl.pallas_call(body, name=name, grid=(t // TOK,), in_specs=[row, _full((1, d)), row, row],
                          out_specs=[row, _full((1, d))],
                          out_shape=[jax.ShapeDtypeStruct((t, d), F32), jax.ShapeDtypeStruct((1, d), F32)],
                          compiler_params=_cparams(("arbitrary",)))(x, g, dxn, res)


def _loss_head(x, g, tgt):
    t, d = x.shape

    def body(x_ref, g_ref, t_ref, l_ref, dx_ref, dg_ref):
        tg = t_ref[...]

        def fn(xv, gv):
            err = _rms(xv, gv) - tg
            per_tok = jnp.mean(err * err, axis=-1, keepdims=True)
            return 0.5 * jnp.sum(per_tok, axis=0, keepdims=True)

        l, vjp = jax.vjp(fn, x_ref[...], g_ref[...])
        dx, dg = vjp(jnp.ones((1, 1), F32))
        dx_ref[...] = dx

        @pl.when(pl.program_id(0) == 0)
        def _():
            dg_ref[...] = jnp.zeros_like(dg_ref)
            l_ref[...] = jnp.zeros_like(l_ref)

        dg_ref[...] += dg
        l_ref[...] += jnp.broadcast_to(l, l_ref.shape)

    row = pl.BlockSpec((TOK, d), lambda i: (i, 0))
    return pl.pallas_call(body, name="loss_head", grid=(t // TOK,), in_specs=[row, _full((1, d)), row],
                          out_specs=[_full((1, LANES)), row, _full((1, d))],
                          out_shape=[jax.ShapeDtypeStruct((1, LANES), F32), jax.ShapeDtypeStruct((t, d), F32),
                                     jax.ShapeDtypeStruct((1, d), F32)],
                          compiler_params=_cparams(("arbitrary",)))(x, g, tgt)


def _glu_fwd(x, z):
    t, d = x.shape

    def body(x_ref, v_ref, g_ref, o_ref):
        o_ref[...] = x_ref[...] + v_ref[...] * jax.nn.sigmoid(g_ref[...])

    row = pl.BlockSpec((TOK, d), lambda i: (i, 0))
    gate = pl.BlockSpec((TOK, d), lambda i: (i, 1))
    return pl.pallas_call(body, name="glu_fwd", grid=(t // TOK,), in_specs=[row, row, gate], out_specs=row,
                          out_shape=jax.ShapeDtypeStruct((t, d), F32),
                          compiler_params=_cparams(("parallel",)))(x, z, z)


def _glu_bwd(z, g):
    t, d = g.shape

    def body(v_ref, g_ref, d_ref, o_ref):
        s = jax.nn.sigmoid(g_ref[...])
        dy = d_ref[...]
        o_ref[:, :d] = (dy * s).astype(BF16)
        o_ref[:, d:] = (dy * v_ref[...] * s * (1.0 - s)).astype(BF16)

    row = pl.BlockSpec((TOK, d), lambda i: (i, 0))
    gate = pl.BlockSpec((TOK, d), lambda i: (i, 1))
    return pl.pallas_call(body, name="glu_bwd", grid=(t // TOK,), in_specs=[row, gate, row],
                          out_specs=pl.BlockSpec((TOK, 2 * d), lambda i: (i, 0)),
                          out_shape=jax.ShapeDtypeStruct((t, 2 * d), BF16),
                          compiler_params=_cparams(("parallel",)))(z, z, g)


def _shift_down(x, d):
    row = lax.broadcasted_iota(jnp.int32, x.shape, 0)
    return jnp.where(row < d, 0.0, pltpu.roll(x, d, 0))


def _shift_up(x, d):
    n = x.shape[0]
    row = lax.broadcasted_iota(jnp.int32, x.shape, 0)
    return jnp.where(row >= n - d, 0.0, pltpu.roll(x, n - d, 0))


def _make_sd():
    @functools.partial(jax.custom_vjp, nondiff_argnums=(1,))
    def sd(x, d):
        return _shift_down(x, d)

    def fwd(x, d):
        return _shift_down(x, d), None

    def bwd(d, _, g):
        return (_shift_up(g, d),)

    sd.defvjp(fwd, bwd)
    return sd


def _lin_scan(a, u, reverse=False):
    n = a.shape[0]
    row = lax.broadcasted_iota(jnp.int32, a.shape, 0)
    d = 1
    while d < n:
        if reverse:
            keep = row < n - d
            a_s, u_s = pltpu.roll(a, n - d, 0), pltpu.roll(u, n - d, 0)
        else:
            keep = row >= d
            a_s, u_s = pltpu.roll(a, d, 0), pltpu.roll(u, d, 0)
        u = u + a * jnp.where(keep, u_s, 0.0)
        a = a * jnp.where(keep, a_s, 1.0)
        d *= 2
    return u


def _make_scan():
    @jax.custom_vjp
    def scan(a, u):
        return _lin_scan(a, u)

    def fwd(a, u):
        h = _lin_scan(a, u)
        return h, (a, h)

    def bwd(res, dh):
        a, h = res
        g = _lin_scan(_shift_up(a, 1), dh, reverse=True)
        return g * _shift_down(h, 1), g

    scan.defvjp(fwd, bwd)
    return scan


def _acc_out(ref, val):
    @pl.when(pl.program_id(0) == 0)
    def _():
        ref[...] = jnp.zeros_like(ref)

    ref[...] += val


FFN_CW = 128


def _ffn_fn(hg, hv, wg, wv, bg, bv, sd):
    cg = wg[0:1] * sd(hg, 2) + wg[1:2] * sd(hg, 1) + wg[2:3] * hg + bg
    cv = wv[0:1] * sd(hv, 2) + wv[1:2] * sd(hv, 1) + wv[2:3] * hv + bv
    return jax.nn.silu(cg) * cv


def _ffn_specs(t):
    nb = D_FF // FFN_CW
    col = lambda r, off: pl.BlockSpec((r, FFN_CW), lambda j: (0, j + off))
    return nb, [col(t, 0), col(t, nb), col(3, 0), col(3, nb), col(1, 0), col(1, nb)], col


def _ffn_mid_fwd(h, cw, cb, name):
    t = h.shape[0]
    nb, in_specs, col = _ffn_specs(t)

    def body(hg, hv, wg, wv, bg, bv, o_ref):
        o_ref[...] = _ffn_fn(hg[...], hv[...], wg[...], wv[...], bg[...], bv[...], _shift_down).astype(BF16)

    return pl.pallas_call(body, name=name, grid=(nb,), in_specs=in_specs, out_specs=col(t, 0),
                          out_shape=jax.ShapeDtypeStruct((t, D_FF), BF16),
                          compiler_params=_cparams(("parallel",), VMEM_MID))(h, h, cw, cw, cb, cb)


def _ffn_mid_bwd(h, cw, cb, dact, name):
    t = h.shape[0]
    nb, in_specs, col = _ffn_specs(t)

    def body(hg, hv, wg, wv, bg, bv, d_ref, dhg, dhv, dwg, dwv, dbg, dbv):
        fn = functools.partial(_ffn_fn, sd=_make_sd())
        _, vjp = jax.vjp(fn, hg[...], hv[...], wg[...], wv[...], bg[...], bv[...])
        g = vjp(d_ref[...])
        dhg[...] = g[0].astype(BF16)
        dhv[...] = g[1].astype(BF16)
        dwg[...], dwv[...], dbg[...], dbv[...] = g[2], g[3], g[4], g[5]

    big = jax.ShapeDtypeStruct((t, D_FF), BF16)
    w3 = jax.ShapeDtypeStruct((3, D_FF), F32)
    b1 = jax.ShapeDtypeStruct((1, D_FF), F32)
    return pl.pallas_call(body, name=name, grid=(nb,), in_specs=in_specs + [col(t, 0)],
                          out_specs=[col(t, 0), col(t, 0), col(3, 0), col(3, 0), col(1, 0), col(1, 0)],
                          out_shape=[big, big, w3, w3, b1, b1],
                          compiler_params=_cparams(("parallel",), VMEM_BIG))(h, h, cw, cw, cb, cb, dact)


TS_CW = 256


def _tshift_fn(p, mu, sd):
    return p + mu * (sd(p, 1) - p)


def _tshift_fwd(p, mu):
    t = p.shape[0]
    col = lambda r: pl.BlockSpec((r, TS_CW), lambda j: (0, j))

    def body(p_ref, mu_ref, o_ref):
        o_ref[...] = _tshift_fn(p_ref[...], mu_ref[...], _shift_down)

    return pl.pallas_call(body, name="tshift_fwd", grid=(SHIFT_COLS // TS_CW,), in_specs=[col(t), col(1)],
                          out_specs=col(t), out_shape=jax.ShapeDtypeStruct((t, SHIFT_COLS), F32),
                          compiler_params=_cparams(("parallel",), VMEM_MID))(p, mu)


def _tshift_bwd(p, mu, dpam):
    t = p.shape[0]
    col = lambda r: pl.BlockSpec((r, TS_CW), lambda j: (0, j))

    def body(p_ref, mu_ref, d_ref, dp_ref, dmu_ref):
        _, vjp = jax.vjp(functools.partial(_tshift_fn, sd=_make_sd()), p_ref[...], mu_ref[...])
        dp, dmu = vjp(d_ref[...])
        dp_ref[...] = dp.astype(BF16)
        dmu_ref[...] = dmu

    return pl.pallas_call(body, name="tshift_bwd", grid=(SHIFT_COLS // TS_CW,), in_specs=[col(t), col(1), col(t)],
                          out_specs=[col(t), col(1)],
                          out_shape=[jax.ShapeDtypeStruct((t, SHIFT_COLS), BF16),
                                     jax.ShapeDtypeStruct((1, SHIFT_COLS), F32)],
                          compiler_params=_cparams(("parallel",), VMEM_MID))(p, mu, dpam)


_HI = lax.Precision.HIGHEST
_O = (0, RW, 2 * RW, 3 * RW, 3 * RW + W_LORA, 3 * RW + W_LORA + A_LORA, SHIFT_COLS)


def _seg(x, gm):
    return jnp.dot(x, gm, precision=_HI)


def _prep_fn(r, k, v, wd, ad, gd, w0, w2, a0, a2, g2, k_k, k_a, gm):
    w_log = -jax.nn.softplus(-(w0 + jnp.tanh(wd) @ w2)) - 0.5
    decay = jnp.exp(-jnp.exp(w_log))
    a = jax.nn.sigmoid(a0 + ad @ a2)
    g = jax.nn.sigmoid(gd) @ g2
    kk = k * k_k
    kk = kk / jnp.maximum(jnp.sqrt(_seg(kk * kk, gm)), 1e-12)
    k2 = k * (1.0 + (a - 1.0) * k_a)
    return r, decay, k2, v, -kk, kk * a, g


_PREP_W = ("w0", "w2", "a0", "a2", "g2", "k_k", "k_a")


def _prep_wspecs(w):
    return [_full(w[n].shape) for n in _PREP_W] + [_full((RW, RW))]


def _rwkv_prep_fwd(pam, w, gm):
    t = pam.shape[0]

    def body(p_ref, *refs):
        wr, outs = refs[:8], refs[8:]
        pieces = [p_ref[:, _O[i]:_O[i + 1]] for i in range(6)]
        res = _prep_fn(*pieces, *[x[...] for x in wr])
        for o, val in zip(outs, res):
            o[...] = val

    row = lambda c: pl.BlockSpec((TOK, c), lambda i: (i, 0))
    return pl.pallas_call(body, name="rwkv_prep_fwd", grid=(t // TOK,),
                          in_specs=[row(SHIFT_COLS)] + _prep_wspecs(w), out_specs=[row(RW)] * 7,
                          out_shape=[jax.ShapeDtypeStruct((t, RW), F32)] * 7,
                          compiler_params=_cparams(("parallel",), VMEM_MID))(pam, *[w[n] for n in _PREP_W], gm)


def _rwkv_prep_bwd(pam, w, gm, cts, more):
    t = pam.shape[0]

    def body(p_ref, *refs):
        wr, ct, ex, dp_ref, dws = refs[:8], refs[8:15], refs[15:18], refs[18], refs[19:]
        pieces = [p_ref[:, _O[i]:_O[i + 1]] for i in range(6)]
        fn = lambda *a: _prep_fn(*a, wr[7][...])
        _, vjp = jax.vjp(fn, *pieces, *[x[...] for x in wr[:7]])
        c = [x[...] for x in ct]
        c[0] = c[0] + ex[0][...]
        c[2] = c[2] + ex[1][...]
        c[3] = c[3] + ex[2][...]
        g = vjp(tuple(c))
        for i in range(6):
            dp_ref[:, _O[i]:_O[i + 1]] = g[i]
        for o, val in zip(dws, g[6:]):
            _acc_out(o, val)

    row = lambda c: pl.BlockSpec((TOK, c), lambda i: (i, 0))
    return pl.pallas_call(body, name="rwkv_prep_bwd", grid=(t // TOK,),
                          in_specs=[row(SHIFT_COLS)] + _prep_wspecs(w) + [row(RW)] * 10,
                          out_specs=[row(SHIFT_COLS)] + [_full(w[n].shape) for n in _PREP_W],
                          out_shape=[jax.ShapeDtypeStruct((t, SHIFT_COLS), F32)]
                          + [jax.ShapeDtypeStruct(w[n].shape, F32) for n in _PREP_W],
                          compiler_params=_cparams(("arbitrary",), VMEM_MID))(
                              pam, *[w[n] for n in _PREP_W], gm, *cts, *more)


def _post_fn(y, r, k2, v, g, ln_w, ln_b, r_k, gm):
    inv = 1.0 / HEAD
    d = y - _seg(y, gm) * inv
    yn = d * lax.rsqrt(_seg(d * d, gm) * inv + GN_EPS) * ln_w + ln_b
    bonus = _seg(r * k2 * r_k, gm) * v
    return (yn + bonus) * g


def _rwkv_post_fwd(y, r, k2, v, g, ln_w, ln_b, r_k, gm):
    t = y.shape[0]

    def body(*refs):
        o_ref = refs[-1]
        o_ref[...] = _post_fn(*[x[...] for x in refs[:-1]]).astype(BF16)

    row = pl.BlockSpec((TOK, RW), lambda i: (i, 0))
    return pl.pallas_call(body, name="rwkv_post_fwd", grid=(t // TOK,),
                          in_specs=[row] * 5 + [_full((1, RW))] * 3 + [_full((RW, RW))], out_specs=row,
                          out_shape=jax.ShapeDtypeStruct((t, RW), BF16),
                          compiler_params=_cparams(("parallel",), VMEM_MID))(y, r, k2, v, g, ln_w, ln_b, r_k, gm)


def _rwkv_post_bwd(y, r, k2, v, g, ln_w, ln_b, r_k, gm, dya):
    t = y.shape[0]

    def body(*refs):
        ins, gm_ref, d_ref, outs = refs[:8], refs[8], refs[9], refs[10:]
        fn = lambda *a: _post_fn(*a, gm_ref[...])
        _, vjp = jax.vjp(fn, *[x[...] for x in ins])
        gr = vjp(d_ref[...])
        for o, val in zip(outs[:5], gr[:5]):
            o[...] = val
        for o, val in zip(outs[5:], gr[5:]):
            _acc_out(o, val)

    row = pl.BlockSpec((TOK, RW), lambda i: (i, 0))
    vec = _full((1, RW))
    return pl.pallas_call(body, name="rwkv_post_bwd", grid=(t // TOK,),
                          in_specs=[row] * 5 + [vec] * 3 + [_full((RW, RW)), row],
                          out_specs=[row] * 5 + [vec] * 3,
                          out_shape=[jax.ShapeDtypeStruct((t, RW), F32)] * 5 + [jax.ShapeDtypeStruct((1, RW), F32)] * 3,
                          compiler_params=_cparams(("arbitrary",), VMEM_MID))(y, r, k2, v, g, ln_w, ln_b, r_k, gm, dya)


def _from_pt(x):
    n = x.shape[0]
    return x.reshape(n, HEAD, N_HEADS, PT).transpose(0, 3, 2, 1).reshape(n * PT, N_HEADS * HEAD)


def _lane_sum(x):
    return jnp.sum(x, axis=-1, keepdims=True)


def _pair_consts():
    lane = lax.broadcasted_iota(jnp.int32, (HEAD, LANES), 1)
    return lane, lane < HEAD


def _seg_sum_pair(x, first):
    return jnp.where(first, _lane_sum(jnp.where(first, x, 0.0)), _lane_sum(jnp.where(first, 0.0, x)))


def _expand_cols(x):
    t = x.shape[0]
    y = x.reshape(t, N_HEADS, HEAD).transpose(0, 2, 1)
    return jnp.broadcast_to(y[:, :, :, None], (t, HEAD, N_HEADS, HEAD)).reshape(t, HEAD, RW)


def _wkv_fwd(w, k, z, b, v_exp):
    t = w.shape[0]
    nc = t // WKV_CHUNK
    pairs = N_HEADS // 2

    def body(w_ref, k_ref, z_ref, b_ref, v_ref, s_all, s_ref):
        @pl.when(pl.program_id(0) == 0)
        def _():
            s_ref[...] = jnp.zeros_like(s_ref)

        _, first = _pair_consts()

        def group(gi, carry):
            base = pl.multiple_of(gi * 8, 8)
            rows = [ref[pl.ds(base, 8), :] for ref in (w_ref, k_ref, z_ref, b_ref)]
            s = [s_ref[:, p * LANES:(p + 1) * LANES] for p in range(pairs)]
            for jj in range(8):
                for p in range(pairs):
                    cs = slice(p * LANES, (p + 1) * LANES)
                    wr, kr, zr, br = [x[jj:jj + 1, cs] for x in rows]
                    s_all[base + jj, :, cs] = s[p]
                    sa = _seg_sum_pair(s[p] * zr, first)
                    s[p] = s[p] * wr + sa * br + v_ref[base + jj, :, cs] * kr
            for p in range(pairs):
                s_ref[:, p * LANES:(p + 1) * LANES] = s[p]
            return carry

        lax.fori_loop(0, WKV_CHUNK // 8, group, 0)

    row = pl.BlockSpec((WKV_CHUNK, RW), lambda i: (i, 0))
    big = pl.BlockSpec((WKV_CHUNK, HEAD, RW), lambda i: (i, 0, 0))
    return pl.pallas_call(
        body, name="wkv_fwd", grid=(nc,), in_specs=[row] * 4 + [big], out_specs=[big, _full((HEAD, RW))],
        out_shape=[jax.ShapeDtypeStruct((t, HEAD, RW), F32), jax.ShapeDtypeStruct((HEAD, RW), F32)],
        compiler_params=_cparams(("arbitrary",), VMEM_MID))(w, k, z, b, v_exp)


def _wkv_out(r, s_all, s_last):
    t = r.shape[0]
    nc = t // WKV_CHUNK
    tiles = WKV_CHUNK // PT
    pairs = N_HEADS // 2

    def body(r_ref, s_ref, nxt_ref, last_ref, y_ref):
        lane, first = _pair_consts()
        after = jnp.where(pl.program_id(0) == nc - 1, last_ref[...], nxt_ref[0])
        for tl in range(tiles):
            ytile = jnp.zeros((HEAD, LANES), F32)
            for g in range(PT // 8):
                rows = r_ref[tl * PT + g * 8:tl * PT + g * 8 + 8, :]
                for jj in range(8):
                    tt = tl * PT + g * 8 + jj
                    j = g * 8 + jj
                    for p in range(pairs):
                        cs = slice(p * LANES, (p + 1) * LANES)
                        s = s_ref[tt + 1, :, cs] if tt + 1 < WKV_CHUNK else after[:, cs]
                        pr = s * rows[jj:jj + 1, cs]
                        y0 = _lane_sum(jnp.where(first, pr, 0.0))
                        y1 = _lane_sum(jnp.where(first, 0.0, pr))
                        ytile = jnp.where(lane == (2 * p) * PT + j, y0, ytile)
                        ytile = jnp.where(lane == (2 * p + 1) * PT + j, y1, ytile)
            y_ref[tl] = ytile

    row = pl.BlockSpec((WKV_CHUNK, RW), lambda i: (i, 0))
    pt = pl.BlockSpec((tiles, HEAD, LANES), lambda i: (i, 0, 0))
    big = pl.BlockSpec((WKV_CHUNK, HEAD, RW), lambda i: (i, 0, 0))
    nxt = pl.BlockSpec((1, HEAD, RW), lambda i: (jnp.minimum((i + 1) * WKV_CHUNK, t - 1), 0, 0))
    return pl.pallas_call(
        body, name="wkv_out", grid=(nc,), in_specs=[row, big, nxt, _full((HEAD, RW))], out_specs=pt,
        out_shape=jax.ShapeDtypeStruct((t // PT, HEAD, LANES), F32),
        compiler_params=_cparams(("parallel",), VMEM_MID))(r, s_all, s_all, s_last)


def _wkv_bwd(r, w, k, z, b, v_exp, s_all, dy_exp):
    t = r.shape[0]
    nc = t // WKV_CHUNK
    tiles = WKV_CHUNK // PT
    pairs = N_HEADS // 2

    def body(r_ref, w_ref, k_ref, z_ref, b_ref, v_ref, s_all_ref, dy_ref,
             dr_ref, dw_ref, dk_ref, dz_ref, db_ref, dv_ref, ds_ref):
        @pl.when(pl.program_id(0) == 0)
        def _():
            ds_ref[...] = jnp.zeros_like(ds_ref)

        lane, first = _pair_consts()
        col_sum = lambda x: jnp.sum(x, axis=0, keepdims=True)
        row8 = lax.broadcasted_iota(jnp.int32, (8, LANES), 0)
        for tl in reversed(range(tiles)):
            def group(gg, dvtile):
                gi = PT // 8 - 1 - gg
                base = pl.multiple_of(tl * PT + gi * 8, 8)
                rows = [ref[pl.ds(base, 8), :] for ref in (r_ref, w_ref, k_ref, z_ref, b_ref)]
                outs = (dr_ref, dw_ref, dk_ref, dz_ref, db_ref)
                tiles8 = {(id(o), p): jnp.zeros((8, LANES), F32) for o in outs for p in range(pairs)}
                ds = [ds_ref[:, p * LANES:(p + 1) * LANES] for p in range(pairs)]
                for jj in reversed(range(8)):
                    j = gi * 8 + jj
                    for p in range(pairs):
                        cs = slice(p * LANES, (p + 1) * LANES)

                        def put(ref, val, p=p, jj=jj):
                            tiles8[(id(ref), p)] = jnp.where(row8 == jj, val, tiles8[(id(ref), p)])

                        rr, wr, kr, zr, br = [x[jj:jj + 1, cs] for x in rows]
                        sp = s_all_ref[base + jj, :, cs]
                        vc = v_ref[base + jj, :, cs]
                        dyc = dy_ref[base + jj, :, cs]
                        sa = _seg_sum_pair(sp * zr, first)
                        st = sp * wr + sa * br + vc * kr
                        d = ds[p] + dyc * rr
                        put(dr_ref, col_sum(st * dyc))
                        dvk = d * kr
                        dv0 = _lane_sum(jnp.where(first, dvk, 0.0))
                        dv1 = _lane_sum(jnp.where(first, 0.0, dvk))
                        dvtile = jnp.where(lane == (2 * p) * PT + j, dv0, dvtile)
                        dvtile = jnp.where(lane == (2 * p + 1) * PT + j, dv1, dvtile)
                        put(dk_ref, col_sum(d * vc))
                        put(dw_ref, col_sum(sp * d))
                        u = _seg_sum_pair(d * br, first)
                        put(dz_ref, col_sum(sp * u))
                        put(db_ref, col_sum(d * sa))
                        ds[p] = d * wr + u * zr
                for p in range(pairs):
                    ds_ref[:, p * LANES:(p + 1) * LANES] = ds[p]
                for o in outs:
                    for p in range(pairs):
                        o[pl.ds(base, 8), p * LANES:(p + 1) * LANES] = tiles8[(id(o), p)]
                return dvtile

            dv_ref[tl] = lax.fori_loop(0, PT // 8, group, jnp.zeros((HEAD, LANES), F32))

    rev = lambda i: nc - 1 - i
    row = pl.BlockSpec((WKV_CHUNK, RW), lambda i: (rev(i), 0))
    pt = pl.BlockSpec((tiles, HEAD, LANES), lambda i: (rev(i), 0, 0))
    big = pl.BlockSpec((WKV_CHUNK, HEAD, RW), lambda i: (rev(i), 0, 0))
    return pl.pallas_call(
        body, name="wkv_bwd", grid=(nc,), in_specs=[row] * 5 + [big, big, big], out_specs=[row] * 5 + [pt],
        out_shape=[jax.ShapeDtypeStruct((t, RW), F32)] * 5 + [jax.ShapeDtypeStruct((t // PT, HEAD, LANES), F32)],
        scratch_shapes=[pltpu.VMEM((HEAD, RW), F32)],
        compiler_params=_cparams(("arbitrary",), VMEM_BIG))(r, w, k, z, b, v_exp, s_all, dy_exp)


LRU_CW = 128
_BX0 = SHIFT_COLS // LRU_CW
_BG0 = (SHIFT_COLS + LRU_W) // LRU_CW


def _lru_fn(bx, bg, cw, cb, ga, ba, gx, bxb, lam, sd, scan):
    xc = cw[0:1] * sd(bx, 3) + cw[1:2] * sd(bx, 2) + cw[2:3] * sd(bx, 1) + cw[3:4] * bx + cb
    gr = jax.nn.sigmoid(xc @ ga + ba)
    gi = jax.nn.sigmoid(xc @ gx + bxb)
    log_a = -LRU_C * gr * jax.nn.softplus(-lam)
    a = jnp.exp(log_a)
    mult = jnp.sqrt(-jnp.tanh(log_a) * (jnp.exp(2.0 * log_a) + 1.0))
    return scan(a, xc * gi * mult) * jax.nn.gelu(bg)


def _lru_specs(t):
    col = lambda r, off=0: pl.BlockSpec((r, LRU_CW), lambda j: (0, j + off))
    diag = pl.BlockSpec((LRU_CW, LRU_CW), lambda j: (j, j))
    return col, [col(t, _BX0), col(t, _BG0), col(4), col(1), diag, col(1), diag, col(1), col(1)]


def _lru_fwd(p, cw, cb, ga, ba, gx, bxb, lam):
    t = p.shape[0]
    col, in_specs = _lru_specs(t)

    def body(*refs):
        o_ref = refs[-1]
        o_ref[...] = _lru_fn(*[x[...] for x in refs[:-1]], _shift_down, _lin_scan).astype(BF16)

    return pl.pallas_call(body, name="lru_fwd", grid=(LRU_W // LRU_CW,), in_specs=in_specs, out_specs=col(t),
                          out_shape=jax.ShapeDtypeStruct((t, LRU_W), BF16),
                          compiler_params=_cparams(("parallel",), VMEM_MID))(p, p, cw, cb, ga, ba, gx, bxb, lam)


def _lru_bwd(p, cw, cb, ga, ba, gx, bxb, lam, dyb):
    t = p.shape[0]
    col, in_specs = _lru_specs(t)

    def body(*refs):
        ins, d_ref, outs = refs[:9], refs[9], refs[10:]
        fn = functools.partial(_lru_fn, sd=_make_sd(), scan=_make_scan())
        _, vjp = jax.vjp(fn, *[x[...] for x in ins])
        g = vjp(d_ref[...])
        outs[0][...] = g[0].astype(BF16)
        outs[1][...] = g[1].astype(BF16)
        for o, val in zip(outs[2:], g[2:]):
            o[...] = val

    sq = pl.BlockSpec((LRU_CW, LRU_CW), lambda j: (j, 0))
    act = jax.ShapeDtypeStruct((t, LRU_W), BF16)
    vec = jax.ShapeDtypeStruct((1, LRU_W), F32)
    sqs = jax.ShapeDtypeStruct((LRU_W, LRU_CW), F32)
    return pl.pallas_call(body, name="lru_bwd", grid=(LRU_W // LRU_CW,), in_specs=in_specs + [col(t, RW // LRU_CW)],
                          out_specs=[col(t), col(t), col(4), col(1), sq, col(1), sq, col(1), col(1)],
                          out_shape=[act, act, jax.ShapeDtypeStruct((4, LRU_W), F32), vec, sqs, vec, sqs, vec, vec],
                          compiler_params=_cparams(("parallel",), VMEM_BIG))(p, p, cw, cb, ga, ba, gx, bxb, lam, dyb)


def _s5_disc_fn(a_re, a_im, log_dt, b_re, b_im, e):
    lam_re = jnp.minimum(a_re, -1e-4)
    lam_im = a_im
    dt = jnp.exp(log_dt)
    mag = jnp.exp(lam_re * dt)
    ab_re = mag * jnp.cos(lam_im * dt)
    ab_im = mag * jnp.sin(lam_im * dt)
    den = lam_re * lam_re + lam_im * lam_im
    zr = ab_re - 1.0
    q_re = jnp.dot((zr * lam_re + ab_im * lam_im) / den, e, precision=_HI)
    q_im = jnp.dot((ab_im * lam_re - zr * lam_im) / den, e, precision=_HI)
    return ab_re, ab_im, q_re * b_re - q_im * b_im, q_re * b_im + q_im * b_re


def _s5_disc_fwd(a_re, a_im, log_dt, b_re, b_im, e):
    def body(*refs):
        res = _s5_disc_fn(*[x[...] for x in refs[:6]])
        for o, val in zip(refs[6:], res):
            o[...] = val

    small = jax.ShapeDtypeStruct(a_re.shape, F32)
    wide = jax.ShapeDtypeStruct(b_re.shape, F32)
    return pl.pallas_call(body, name="s5_disc_fwd", out_shape=[small, small, wide, wide])(
        a_re, a_im, log_dt, b_re, b_im, e)


def _s5_disc_bwd(a_re, a_im, log_dt, b_re, b_im, e, cts):
    def body(*refs):
        ins, e_ref, ct, outs = refs[:5], refs[5], refs[6:10], refs[10:]
        _, vjp = jax.vjp(lambda *a: _s5_disc_fn(*a, e_ref[...]), *[x[...] for x in ins])
        for o, val in zip(outs, vjp(tuple(c[...] for c in ct))):
            o[...] = val

    shapes = [jax.ShapeDtypeStruct(x.shape, F32) for x in (a_re, a_im, log_dt, b_re, b_im)]
    return pl.pallas_call(body, name="s5_disc_bwd", out_shape=shapes)(a_re, a_im, log_dt, b_re, b_im, e, *cts)


def _cmul(a, b):
    return a[0] * b[0] - a[1] * b[1], a[0] * b[1] + a[1] * b[0]


def _s5_scan(sr, si, ab, reverse):
    n_tiles = sr.shape[0] // 8
    width = sr.shape[1]
    row8 = lax.broadcasted_iota(jnp.int32, (8, width), 0)
    p1 = ab
    p2 = _cmul(p1, p1)
    p4 = _cmul(p2, p2)
    pw = [p1]
    for _ in range(7):
        pw.append(_cmul(pw[-1], p1))
    cr = jnp.zeros((8, width), F32)
    ci = jnp.zeros((8, width), F32)
    for j in range(8):
        e = pw[7 - j] if reverse else pw[j]
        cr = jnp.where(row8 == j, e[0], cr)
        ci = jnp.where(row8 == j, e[1], ci)

    def tile(i, carry):
        idx = n_tiles - 1 - i if reverse else i
        base = pl.multiple_of(idx * 8, 8)
        x = (sr[pl.ds(base, 8), :], si[pl.ds(base, 8), :])
        for d, q in ((1, p1), (2, p2), (4, p4)):
            keep = row8 < 8 - d if reverse else row8 >= d
            amt = 8 - d if reverse else d
            sh = (jnp.where(keep, pltpu.roll(x[0], amt, 0), 0.0), jnp.where(keep, pltpu.roll(x[1], amt, 0), 0.0))
            m = _cmul(q, sh)
            x = (x[0] + m[0], x[1] + m[1])
        m = _cmul((cr, ci), carry)
        x = (x[0] + m[0], x[1] + m[1])
        sr[pl.ds(base, 8), :] = x[0]
        si[pl.ds(base, 8), :] = x[1]
        edge = slice(0, 1) if reverse else slice(7, 8)
        return x[0][edge], x[1][edge]

    zero = jnp.zeros((1, width), F32)
    lax.fori_loop(0, n_tiles, tile, (zero, zero))


_S5_W = S5_SLAB // S5_GROUP * S5_STATE


def _s5_specs(t):
    col = lambda r: pl.BlockSpec((r, S5_SLAB), lambda j: (0, j))
    bb = pl.BlockSpec((None, S5_SLAB, _S5_W), lambda j: (j, 0, 0))
    cd = pl.BlockSpec((None, _S5_W, S5_SLAB), lambda j: (j, 0, 0))
    ab = pl.BlockSpec((None, 1, _S5_W), lambda j: (j, 0, 0))
    return col, bb, cd, ab


def _s5_fwd(u, dvec, bbr, bbi, cdr, cdi, abr, abi):
    t, width = u.shape
    col, bb, cd, ab = _s5_specs(t)

    def body(u_ref, d_ref, bbr_ref, bbi_ref, cdr_ref, cdi_ref, abr_ref, abi_ref, o_ref, sr, si):
        uv = u_ref[...]
        sr[...] = jnp.dot(uv, bbr_ref[...], preferred_element_type=F32)
        si[...] = jnp.dot(uv, bbi_ref[...], preferred_element_type=F32)
        _s5_scan(sr, si, (abr_ref[...], abi_ref[...]), False)
        y = jnp.dot(sr[...], cdr_ref[...], preferred_element_type=F32) - jnp.dot(si[...], cdi_ref[...],
                                                                                 preferred_element_type=F32)
        o_ref[...] = jax.nn.gelu(y + d_ref[...] * uv).astype(BF16)

    return pl.pallas_call(body, name="s5_fwd", grid=(width // S5_SLAB,),
                          in_specs=[col(t), col(1), bb, bb, cd, cd, ab, ab], out_specs=col(t),
                          out_shape=jax.ShapeDtypeStruct((t, width), BF16),
                          scratch_shapes=[pltpu.VMEM((t, _S5_W), F32)] * 2,
                          compiler_params=_cparams(("parallel",), VMEM_BIG))(u, dvec, bbr, bbi, cdr, cdi, abr, abi)


def _s5_bwd(u, dvec, bbr, bbi, cdr, cdi, abr, abi, dyact):
    t, width = u.shape
    col, bb, cd, ab = _s5_specs(t)
    ns = width // S5_SLAB
    tn = (((0,), (0,)), ((), ()))
    nt = (((1,), (1,)), ((), ()))

    def body(u_ref, d_ref, bbr_ref, bbi_ref, cdr_ref, cdi_ref, abr_ref, abi_ref, dy_ref,
             du_ref, dd_ref, dbbr_ref, dbbi_ref, dcdr_ref, dcdi_ref, dabr_ref, dabi_ref, sr, si, gr, gi):
        uv = u_ref[...]
        dv = d_ref[...]
        abv = (abr_ref[...], abi_ref[...])
        sr[...] = jnp.dot(uv, bbr_ref[...], preferred_element_type=F32)
        si[...] = jnp.dot(uv, bbi_ref[...], preferred_element_type=F32)
        _s5_scan(sr, si, abv, False)
        y = jnp.dot(sr[...], cdr_ref[...], preferred_element_type=F32) - jnp.dot(si[...], cdi_ref[...],
                                                                                 preferred_element_type=F32)
        _, vjp = jax.vjp(jax.nn.gelu, y + dv * uv)
        (dpre,) = vjp(dy_ref[...].astype(F32))
        dd_ref[...] = jnp.sum(dpre * uv, axis=0, keepdims=True)
        dcdr_ref[...] = lax.dot_general(sr[...], dpre, tn, preferred_element_type=F32)
        dcdi_ref[...] = -lax.dot_general(si[...], dpre, tn, preferred_element_type=F32)
        gr[...] = lax.dot_general(dpre, cdr_ref[...], nt, preferred_element_type=F32)
        gi[...] = -lax.dot_general(dpre, cdi_ref[...], nt, preferred_element_type=F32)
        _s5_scan(gr, gi, (abv[0], -abv[1]), True)

        row8 = lax.broadcasted_iota(jnp.int32, (8, _S5_W), 0)

        def tile(i, carry):
            acc_r, acc_i, last_r, last_i = carry
            base = pl.multiple_of(i * 8, 8)
            s_r, s_i = sr[pl.ds(base, 8), :], si[pl.ds(base, 8), :]
            g_r, g_i = gr[pl.ds(base, 8), :], gi[pl.ds(base, 8), :]
            p_r = jnp.where(row8 == 0, last_r, pltpu.roll(s_r, 1, 0))
            p_i = jnp.where(row8 == 0, last_i, pltpu.roll(s_i, 1, 0))
            acc_r = acc_r + jnp.sum(g_r * p_r + g_i * p_i, axis=0, keepdims=True)
            acc_i = acc_i + jnp.sum(g_i * p_r - g_r * p_i, axis=0, keepdims=True)
            return acc_r, acc_i, s_r[7:8], s_i[7:8]

        zero = jnp.zeros((1, _S5_W), F32)
        acc_r, acc_i, _, _ = lax.fori_loop(0, t // 8, tile, (zero, zero, zero, zero))
        dabr_ref[...] = acc_r
        dabi_ref[...] = acc_i
        du_ref[...] = (dpre * dv + lax.dot_general(gr[...], bbr_ref[...], nt, preferred_element_type=F32)
                       + lax.dot_general(gi[...], bbi_ref[...], nt, preferred_element_type=F32))
        dbbr_ref[...] = lax.dot_general(uv, gr[...], tn, preferred_element_type=F32)
        dbbi_ref[...] = lax.dot_general(uv, gi[...], tn, preferred_element_type=F32)

    sds = jax.ShapeDtypeStruct
    return pl.pallas_call(
        body, name="s5_bwd", grid=(ns,), in_specs=[col(t), col(1), bb, bb, cd, cd, ab, ab, col(t)],
        out_specs=[col(t), col(1), bb, bb, cd, cd, ab, ab],
        out_shape=[sds((t, width), F32), sds((1, width), F32), sds((ns, S5_SLAB, _S5_W), F32),
                   sds((ns, S5_SLAB, _S5_W), F32), sds((ns, _S5_W, S5_SLAB), F32), sds((ns, _S5_W, S5_SLAB), F32),
                   sds((ns, 1, _S5_W), F32), sds((ns, 1, _S5_W), F32)],
        scratch_shapes=[pltpu.VMEM((t, _S5_W), F32)] * 4,
        compiler_params=_cparams(("parallel",), VMEM_BIG))(u, dvec, bbr, bbi, cdr, cdi, abr, abi, dyact)


def _gate_dense(w):
    h = w.shape[0]
    return jnp.einsum("hij,hg->higj", w, jnp.eye(h, dtype=F32)).reshape(h * HEAD, h * HEAD)


def _gate_blocks(d):
    x = d.reshape(LRU_W // LRU_CW, 2, HEAD, 2, HEAD)
    return jnp.einsum("tgihj,gh->tgij", x, jnp.eye(2, dtype=F32)).reshape(LRU_W // HEAD, HEAD, HEAD)


_GPS = S5_SLAB // S5_GROUP
_NS = S5_GROUPS // _GPS


def _s5_in_dense(bb):
    x = bb.reshape(_NS, _GPS, S5_STATE, S5_GROUP)
    return jnp.einsum("sgnc,gh->sgchn", x, jnp.eye(_GPS, dtype=F32)).reshape(_NS, S5_SLAB, _S5_W)


def _s5_in_blocks(d):
    x = d.reshape(_NS, _GPS, S5_GROUP, _GPS, S5_STATE)
    return jnp.einsum("sgchn,gh->sgnc", x, jnp.eye(_GPS, dtype=F32)).reshape(S5_GROUPS, S5_STATE * S5_GROUP)


def _s5_out_dense(c):
    x = c.reshape(_NS, _GPS, S5_GROUP, S5_STATE)
    return jnp.einsum("sgcn,gh->shngc", x, jnp.eye(_GPS, dtype=F32)).reshape(_NS, _S5_W, S5_SLAB)


def _s5_out_blocks(d):
    x = d.reshape(_NS, _GPS, S5_STATE, _GPS, S5_GROUP)
    return jnp.einsum("shngc,gh->sgcn", x, jnp.eye(_GPS, dtype=F32)).reshape(S5_GROUPS, S5_GROUP, S5_STATE)


def _local_step(x, tgt, w):
    d_model = x.shape[1]
    gs, gb = {}, {}
    gm = jnp.kron(jnp.eye(N_HEADS, dtype=F32), jnp.ones((HEAD, HEAD), F32))
    n_layers = w["f_norm_g"].shape[0]

    def ffn_fwd(xin, l):
        xn = _rms_fwd(xin, w["f_norm_g"][l:l + 1], f"rms_f{l}")
        h = _matmul(xn, w["f_w_up_t"][l], "nt", f"mm_f{l}_up")
        act = _ffn_mid_fwd(h, w["f_conv_w"][l], w["f_conv_b"][l:l + 1], f"ffn_mid_fwd{l}")
        return _matmul(act, w["f_w_down"][l], "nn", f"mm_f{l}_down", add=xin), (xin, xn, h, act)

    def ffn_bwd(g, saved, l):
        xin, xn, h, act = saved
        dact = _matmul(g, w["f_w_down"][l], "nt", f"mm_f{l}_dact")
        d_down = _matmul(act, g, "tn", f"mm_f{l}_ddown", out_dtype=BF16)
        dhg, dhv, dwg, dwv, dbg, dbv = _ffn_mid_bwd(h, w["f_conv_w"][l], w["f_conv_b"][l:l + 1], dact,
                                                    f"ffn_mid_bwd{l}")
        dh = jnp.concatenate([dhg, dhv], axis=1)
        dxn = _matmul(dh, w["f_w_up_t"][l], "nn", f"mm_f{l}_dxn")
        d_up = _matmul(dh, xn, "tn", f"mm_f{l}_dup", out_dtype=BF16)
        dx, dgn = _rms_bwd(xin, w["f_norm_g"][l:l + 1], dxn, g, f"rms_f{l}_bwd")
        return dx, d_up, d_down, jnp.concatenate([dwg, dwv], axis=1), jnp.concatenate([dbg, dbv], axis=1), dgn

    xn0 = _rms_fwd(x, w["e_norm_g"], "rms_e")
    p = _matmul(xn0, w["e_w_in_t"], "nt", "mm_e_in")
    pam = _tshift_fwd(p, w["e_mu"])
    pw = dict(w0=w["e_w0"], w2=w["e_w2"][0], a0=w["e_a0"], a2=w["e_a2"][0], g2=w["e_g2"][0],
              k_k=w["e_k_k"], k_a=w["e_k_a"])
    r, dec, k2, v, z, b, gate = _rwkv_prep_fwd(pam, pw, gm)
    v_exp = _expand_cols(v)
    s_all, s_last = _wkv_fwd(dec, k2, z, b, v_exp)
    y_pt = _wkv_out(r, s_all, s_last)
    y = _from_pt(y_pt)
    rk = w["e_r_k"].reshape(1, RW)
    ya = _rwkv_post_fwd(y, r, k2, v, gate, w["e_ln_w"], w["e_ln_b"], rk, gm)
    ga, gx = _gate_dense(w["e_gate_a_w"][0]), _gate_dense(w["e_gate_x_w"][0])
    lru_w = (w["e_conv_w"][0], w["e_conv_b"], ga, w["e_gate_a_b"], gx, w["e_gate_x_b"], w["e_lru_lambda"])
    yb = _lru_fwd(p, *lru_w)
    ycat = jnp.concatenate([ya, yb], axis=1)
    x1 = _matmul(ycat, w["e_w_out"], "nn", "mm_e_out", add=x)
    x2, ffn0 = ffn_fwd(x1, 0)

    xn1 = _rms_fwd(x2, w["o_norm_g"], "rms_o")
    u = _matmul(xn1, w["o_w_in"], "nn", "mm_o_in")
    expand = jnp.kron(jnp.eye(S5_STATE, dtype=F32), jnp.ones((1, S5_GROUP), F32))
    disc_in = (w["o_A_re"][0], w["o_A_im"][0], w["o_log_dt"].reshape(S5_GROUPS, 1),
               w["o_B_re"][0].reshape(S5_GROUPS, -1), w["o_B_im"][0].reshape(S5_GROUPS, -1), expand)
    ab_re, ab_im, bb_re, bb_im = _s5_disc_fwd(*disc_in)
    s5_w = (w["o_D"], _s5_in_dense(bb_re), _s5_in_dense(bb_im), _s5_out_dense(w["o_C_re"][0]),
            _s5_out_dense(w["o_C_im"][0]), ab_re.reshape(_NS, 1, _S5_W), ab_im.reshape(_NS, 1, _S5_W))
    yact = _s5_fwd(u, *s5_w)
    zz = _matmul(yact, w["o_w_glu_t"], "nt", "mm_o_glu")
    x3 = _glu_fwd(x2, zz)
    x4, ffn1 = ffn_fwd(x3, 1)

    loss, g, gs["final_norm_g"] = _loss_head(x4, w["final_norm_g"].reshape(1, d_model), tgt)
    gs["final_norm_g"] = gs["final_norm_g"].reshape(d_model)

    g, up1, down1, dcw1, dcb1, dfn1 = ffn_bwd(g, ffn1, 1)
    dz = _glu_bwd(zz, g)
    dyact = _matmul(dz, w["o_w_glu_t"], "nn", "mm_o_dyact")
    gb["o_w_glu_t"] = _matmul(dz, yact, "tn", "mm_o_dglu", out_dtype=BF16)
    du, gs["o_D"], dbbr, dbbi, dcdr, dcdi, dabr, dabi = _s5_bwd(u, *s5_w, dyact)
    gs["o_C_re"] = _s5_out_blocks(dcdr)[None]
    gs["o_C_im"] = _s5_out_blocks(dcdi)[None]
    cts = (dabr.reshape(S5_GROUPS, S5_STATE), dabi.reshape(S5_GROUPS, S5_STATE), _s5_in_blocks(dbbr),
           _s5_in_blocks(dbbi))
    da_re, da_im, dlog_dt, db_re, db_im = _s5_disc_bwd(*disc_in, cts)
    gs["o_A_re"], gs["o_A_im"], gs["o_log_dt"] = da_re[None], da_im[None], dlog_dt.reshape(1, S5_GROUPS)
    gs["o_B_re"] = db_re.reshape(w["o_B_re"].shape)
    gs["o_B_im"] = db_im.reshape(w["o_B_im"].shape)
    dxn = _matmul(du, w["o_w_in"], "nt", "mm_o_dxn")
    gb["o_w_in"] = _matmul(xn1, du, "tn", "mm_o_din", out_dtype=BF16)
    g, gs["o_norm_g"] = _rms_bwd(x2, w["o_norm_g"], dxn, g, "rms_o_bwd")

    g, up0, down0, dcw0, dcb0, dfn0 = ffn_bwd(g, ffn0, 0)
    gb["f_w_up_t"] = [up0, up1]
    gb["f_w_down"] = [down0, down1]
    gs["f_conv_w"] = jnp.stack([dcw0, dcw1])
    gs["f_conv_b"] = jnp.concatenate([dcb0, dcb1], axis=0)
    gs["f_norm_g"] = jnp.concatenate([dfn0, dfn1], axis=0)

    dycat = _matmul(g, w["e_w_out"], "nt", "mm_e_dycat")
    gb["e_w_out"] = _matmul(ycat, g, "tn", "mm_e_dout", out_dtype=BF16)
    dy, dr1, dk1, dv1, dgate, gs["e_ln_w"], gs["e_ln_b"], drk = _rwkv_post_bwd(
        y, r, k2, v, gate, w["e_ln_w"], w["e_ln_b"], rk, gm, dycat)
    gs["e_r_k"] = drk.reshape(w["e_r_k"].shape)
    dr2, ddec, dk2, dzz, dbb, dv_pt = _wkv_bwd(r, dec, k2, z, b, v_exp, s_all, _expand_cols(dy))
    dpam, gs["e_w0"], dw2, gs["e_a0"], da2, dg2, gs["e_k_k"], gs["e_k_a"] = _rwkv_prep_bwd(
        pam, pw, gm, (dr2, ddec, dk2, _from_pt(dv_pt), dzz, dbb, dgate), (dr1, dk1, dv1))
    gs["e_w2"], gs["e_a2"], gs["e_g2"] = dw2[None], da2[None], dg2[None]
    dpa, gs["e_mu"] = _tshift_bwd(p, w["e_mu"], dpam)
    dbx, dbg, dcw, gs["e_conv_b"], dga, gs["e_gate_a_b"], dgx, gs["e_gate_x_b"], gs["e_lru_lambda"] = _lru_bwd(
        p, *lru_w, dycat)
    gs["e_conv_w"] = dcw[None]
    gs["e_gate_a_w"] = _gate_blocks(dga)[None]
    gs["e_gate_x_w"] = _gate_blocks(dgx)[None]
    dp = jnp.concatenate([dpa, dbx, dbg], axis=1)
    dxn = _matmul(dp, w["e_w_in_t"], "nn", "mm_e_dxn")
    gb["e_w_in_t"] = _matmul(dp, xn0, "tn", "mm_e_din", out_dtype=BF16)
    grad_x, gs["e_norm_g"] = _rms_bwd(x, w["e_norm_g"], dxn, g, "rms_e_bwd")
    return loss, grad_x, gb, gs


CAST_ROWS = 256


def _cast_shard(w3, layer, transpose, name):
    _, rows, cols = w3.shape
    tr = _tile(rows, (CAST_ROWS, 176, 128))

    def body(w_ref, o_ref):
        v = w_ref[...]
        o_ref[...] = (v.T if transpose else v).astype(BF16)

    in_spec = pl.BlockSpec((None, tr, cols), lambda i: (layer, i, 0))
    if transpose:
        out_spec, shape = pl.BlockSpec((cols, tr), lambda i: (0, i)), (cols, rows)
    else:
        out_spec, shape = pl.BlockSpec((tr, cols), lambda i: (i, 0)), (rows, cols)
    return pl.pallas_call(body, name=name, grid=(rows // tr,), in_specs=[in_spec], out_specs=out_spec,
                          out_shape=jax.ShapeDtypeStruct(shape, BF16),
                          compiler_params=_cparams(("parallel",), VMEM_MID))(w3)


_ANY = pl.BlockSpec(memory_space=pl.ANY)


def _coords():
    return lax.axis_index("x"), lax.axis_index("y"), lax.axis_index("c")


def _flip(v, d):
    return 1 - v if d else v


_CHIP_RELS = ((1, 0), (0, 1), (1, 1))
_DEV_RELS = tuple((dx, dy, dc) for dx in (0, 1) for dy in (0, 1) for dc in (0, 1))[1:]


def _gather_chips(arrs):
    n = len(arrs)
    nr = len(_CHIP_RELS)

    def body(*refs):
        ins, outs, (send, recv, loc) = refs[:n], refs[n:2 * n], refs[2 * n:]
        x, y, c = _coords()
        me = 2 * x + y
        locals_, sends, recvs = [], [], []
        for i in range(n):
            cp = pltpu.make_async_copy(ins[i], outs[i].at[me], loc.at[i])
            cp.start()
            locals_.append(cp)
            for j, (dx, dy) in enumerate(_CHIP_RELS):
                px, py = _flip(x, dx), _flip(y, dy)
                k = i * nr + j
                cp = pltpu.make_async_remote_copy(src_ref=ins[i], dst_ref=outs[i].at[me], send_sem=send.at[k],
                                                  recv_sem=recv.at[k], device_id=(px, py, c), device_id_type=MESH)
                cp.start()
                sends.append(cp)
                recvs.append(pltpu.make_async_remote_copy(
                    src_ref=ins[i], dst_ref=outs[i].at[2 * px + py], send_sem=send.at[k], recv_sem=recv.at[k],
                    device_id=(px, py, c), device_id_type=MESH))
        for cp in recvs:
            cp.wait_recv()
        for cp in sends:
            cp.wait_send()
        for cp in locals_:
            cp.wait()

    return pl.pallas_call(
        body, name="gather_chips", in_specs=[_ANY] * n, out_specs=[_ANY] * n,
        out_shape=[jax.ShapeDtypeStruct((4,) + a.shape, a.dtype) for a in arrs],
        scratch_shapes=[pltpu.SemaphoreType.DMA((n * nr,)), pltpu.SemaphoreType.DMA((n * nr,)),
                        pltpu.SemaphoreType.DMA((n,))])(*arrs)


def _scatter_devices(arrs):
    n = len(arrs)
    nr = len(_DEV_RELS)

    def body(*refs):
        ins, outs, (send, recv, loc) = refs[:n], refs[n:2 * n], refs[2 * n:]
        x, y, c = _coords()
        me = 4 * x + 2 * y + c
        locals_, sends, recvs = [], [], []
        for i in range(n):
            cp = pltpu.make_async_copy(ins[i].at[me], outs[i].at[me], loc.at[i])
            cp.start()
            locals_.append(cp)
            for j, (dx, dy, dc) in enumerate(_DEV_RELS):
                peer = (_flip(x, dx), _flip(y, dy), _flip(c, dc))
                pid = 4 * peer[0] + 2 * peer[1] + peer[2]
                k = i * nr + j
                cp = pltpu.make_async_remote_copy(src_ref=ins[i].at[pid], dst_ref=outs[i].at[me], send_sem=send.at[k],
                                                  recv_sem=recv.at[k], device_id=peer, device_id_type=MESH)
                cp.start()
                sends.append(cp)
                recvs.append(pltpu.make_async_remote_copy(
                    src_ref=ins[i].at[pid], dst_ref=outs[i].at[pid], send_sem=send.at[k], recv_sem=recv.at[k],
                    device_id=peer, device_id_type=MESH))
        for cp in recvs:
            cp.wait_recv()
        for cp in sends:
            cp.wait_send()
        for cp in locals_:
            cp.wait()

    return pl.pallas_call(
        body, name="scatter_devices", in_specs=[_ANY] * n, out_specs=[_ANY] * n,
        out_shape=[jax.ShapeDtypeStruct(a.shape, a.dtype) for a in arrs],
        scratch_shapes=[pltpu.SemaphoreType.DMA((n * nr,)), pltpu.SemaphoreType.DMA((n * nr,)),
                        pltpu.SemaphoreType.DMA((n,))])(*arrs)


def _sum_segments(stage, name):
    nd, seg, cols = stage.shape
    ts = _tile(seg, (256, 176, 128))

    def body(s_ref, o_ref):
        acc = s_ref[0].astype(F32)
        for d in range(1, nd):
            acc = acc + s_ref[d].astype(F32)
        o_ref[...] = acc

    return pl.pallas_call(body, name=name, grid=(seg // ts,),
                          in_specs=[pl.BlockSpec((nd, ts, cols), lambda i: (0, i, 0))],
                          out_specs=pl.BlockSpec((ts, cols), lambda i: (i, 0)),
                          out_shape=jax.ShapeDtypeStruct((seg, cols), F32),
                          compiler_params=_cparams(("parallel",), VMEM_MID))(stage)


SIB_CHUNKS = 4


def _exchange_sibling(arrs):
    n = len(arrs)
    nk = n * SIB_CHUNKS

    def body(*refs):
        ins, outs, (send, recv, loc) = refs[:n], refs[n:2 * n], refs[2 * n:]
        x, y, c = _coords()
        sib = (x, y, 1 - c)
        locals_, sends, recvs = [], [], []
        for i in range(n):
            cp = pltpu.make_async_copy(ins[i], outs[i].at[c], loc.at[i])
            cp.start()
            locals_.append(cp)
            rows = ins[i].shape[0] // SIB_CHUNKS
            for q in range(SIB_CHUNKS):
                part = pl.ds(q * rows, rows)
                k = i * SIB_CHUNKS + q
                cp = pltpu.make_async_remote_copy(src_ref=ins[i].at[part], dst_ref=outs[i].at[c, part],
                                                  send_sem=send.at[k], recv_sem=recv.at[k], device_id=sib,
                                                  device_id_type=MESH)
                cp.start()
                sends.append(cp)
                recvs.append(pltpu.make_async_remote_copy(
                    src_ref=ins[i].at[part], dst_ref=outs[i].at[1 - c, part], send_sem=send.at[k],
                    recv_sem=recv.at[k], device_id=sib, device_id_type=MESH))
        for cp in recvs:
            cp.wait_recv()
        for cp in sends:
            cp.wait_send()
        for cp in locals_:
            cp.wait()

    return pl.pallas_call(
        body, name="exchange_sibling", in_specs=[_ANY] * n, out_specs=[_ANY] * n,
        out_shape=[jax.ShapeDtypeStruct((2,) + a.shape, a.dtype) for a in arrs],
        scratch_shapes=[pltpu.SemaphoreType.DMA((nk,)), pltpu.SemaphoreType.DMA((nk,)),
                        pltpu.SemaphoreType.DMA((n,))])(*arrs)


def _allreduce_small(vec):
    nd, rows, lanes = vec.shape
    nr = len(_DEV_RELS)

    def body(in_ref, out_ref, stage, red, send, recv):
        x, y, c = _coords()
        me = 4 * x + 2 * y + c
        peers = []
        for dx, dy, dc in _DEV_RELS:
            peer = (_flip(x, dx), _flip(y, dy), _flip(c, dc))
            peers.append((peer, 4 * peer[0] + 2 * peer[1] + peer[2]))

        def copy(src, dst, k, peer):
            return pltpu.make_async_remote_copy(src_ref=src, dst_ref=dst, send_sem=send.at[k], recv_sem=recv.at[k],
                                                device_id=peer, device_id_type=MESH)

        first = [copy(in_ref.at[pid], stage.at[me], j, peer) for j, (peer, pid) in enumerate(peers)]
        for cp in first:
            cp.start()
        stage[me] = in_ref[me]
        for j, (peer, pid) in enumerate(peers):
            copy(in_ref.at[pid], stage.at[pid], j, peer).wait_recv()
        acc = stage[0]
        for d in range(1, nd):
            acc = acc + stage[d]
        red[...] = acc
        out_ref[me] = acc
        second = [copy(red, out_ref.at[me], nr + j, peer) for j, (peer, pid) in enumerate(peers)]
        for cp in second:
            cp.start()
        for j, (peer, pid) in enumerate(peers):
            copy(red, out_ref.at[pid], nr + j, peer).wait_recv()
        for cp in first + second:
            cp.wait_send()

    vm = pl.BlockSpec(memory_space=pltpu.VMEM)
    return pl.pallas_call(
        body, name="allreduce_small", in_specs=[vm], out_specs=vm,
        out_shape=jax.ShapeDtypeStruct(vec.shape, F32),
        scratch_shapes=[pltpu.VMEM(vec.shape, F32), pltpu.VMEM((rows, lanes), F32),
                        pltpu.SemaphoreType.DMA((2 * nr,)), pltpu.SemaphoreType.DMA((2 * nr,))],
        compiler_params=_cparams(None, VMEM_MID))(vec)


def _adam_math(w, g, m, v):
    m2 = ADAM_B1 * m + (1.0 - ADAM_B1) * g
    v2 = ADAM_B2 * v + (1.0 - ADAM_B2) * (g * g)
    m_hat = m2 / (1.0 - ADAM_B1 ** ADAM_STEP)
    v_hat = v2 / (1.0 - ADAM_B2 ** ADAM_STEP)
    return -ADAM_LR * (m_hat / (jnp.sqrt(v_hat) + ADAM_EPS) + ADAM_WD * w), m2, v2


def _adamw_big(w3, m3, v3, layer, g, transposed, name, prev=None):
    nl, rows, cols = w3.shape
    tr = 128 if transposed else _tile(rows, (256, 176, 128))

    def body(w_ref, m_ref, v_ref, g_ref, *rest):
        go_ref, d_ref, mo_ref, vo_ref = rest[-4:]
        g_val = g_ref[...].T if transposed else g_ref[...]
        go_ref[...] = g_val
        d_ref[...], mo_ref[...], vo_ref[...] = _adam_math(w_ref[...], g_val, m_ref[...], v_ref[...])

    wspec = pl.BlockSpec((None, tr, cols), lambda i: (layer, i, 0))
    gspec = pl.BlockSpec((cols, tr), lambda i: (0, i)) if transposed else pl.BlockSpec((tr, cols), lambda i: (i, 0))
    extra = [] if prev is None else list(prev)
    return pl.pallas_call(body, name=name, grid=(rows // tr,),
                          in_specs=[wspec, wspec, wspec, gspec] + [_ANY] * len(extra),
                          out_specs=[wspec] * 4, out_shape=[jax.ShapeDtypeStruct((nl, rows, cols), F32)] * 4,
                          input_output_aliases={4 + i: i for i in range(len(extra))},
                          compiler_params=_cparams(("parallel",), VMEM_MID))(w3, m3, v3, g, *extra)


def _adamw_small(w, g, m, v):
    rows = w.shape[0]
    tr = _tile(rows, (512, 256, 128, 64, 32, 16, 8))

    def body(w_ref, g_ref, m_ref, v_ref, d_ref, mo_ref, vo_ref):
        d_ref[...], mo_ref[...], vo_ref[...] = _adam_math(w_ref[...], g_ref[...], m_ref[...], v_ref[...])

    spec = pl.BlockSpec((tr, LANES), lambda i: (i, 0))
    return pl.pallas_call(body, name="adamw_small", grid=(rows // tr,), in_specs=[spec] * 4, out_specs=[spec] * 3,
                          out_shape=[jax.ShapeDtypeStruct(w.shape, F32)] * 3,
                          compiler_params=_cparams(("parallel",)))(w, g, m, v)


def _pack(arrs, row_mult):
    flat = jnp.concatenate([a.reshape(-1).astype(F32) for a in arrs])
    rows = -(-flat.shape[0] // LANES)
    rows = -(-rows // row_mult) * row_mult
    return jnp.pad(flat, (0, rows * LANES - flat.shape[0])).reshape(rows, LANES)


def _unpack(packed, shapes):
    flat = packed.reshape(-1)
    out, off = [], 0
    for s in shapes:
        size = 1
        for d in s:
            size *= d
        out.append(flat[off:off + size].reshape(s))
        off += size
    return out


_SMALL_REP = ("e_norm_g", "e_mu", "e_w0", "e_a0", "e_k_k", "e_k_a", "e_r_k", "e_ln_w", "e_ln_b", "e_conv_b",
              "e_gate_a_w", "e_gate_a_b", "e_gate_x_w", "e_gate_x_b", "e_lru_lambda", "o_A_re", "o_A_im", "o_log_dt",
              "o_B_re", "o_B_im", "o_C_re", "o_C_im", "f_norm_g", "f_conv_b", "final_norm_g")
_SMALL_SH = ("e_w2", "e_a2", "e_g2", "e_conv_w", "o_norm_g", "o_D", "f_conv_w")
_LARGE = (("e_w_in", True), ("e_w_out", False), ("o_w_in", False), ("o_w_glu", True), ("f_w_up", True),
        ("f_w_down", False))
_ORDER = ("e_norm_g", "e_w_in", "e_mu", "e_w0", "e_w2", "e_a0", "e_a2", "e_g2", "e_k_k", "e_k_a", "e_r_k", "e_ln_w",
          "e_ln_b", "e_conv_w", "e_conv_b", "e_gate_a_w", "e_gate_a_b", "e_gate_x_w", "e_gate_x_b", "e_lru_lambda",
          "e_w_out", "o_norm_g", "o_w_in", "o_A_re", "o_A_im", "o_log_dt", "o_B_re", "o_B_im", "o_C_re", "o_C_im",
          "o_D", "o_w_glu", "f_norm_g", "f_w_up", "f_conv_w", "f_conv_b", "f_w_down", "final_norm_g")
N_CHIPS = 4
N_DEV = 8


def _step(x, tgt, wts, ms, vs):
    xi, yi, _ = _coords()
    chip = 2 * xi + yi

    send, keys = [], []
    for name, by_cols in _LARGE:
        for l in range(wts[name].shape[0]):
            send.append(_cast_shard(wts[name], l, by_cols, f"cast_{name}{l}"))
            keys.append((name, l))
    sh_shapes = [wts[n].shape for n in _SMALL_SH]
    send.append(_pack([wts[n] for n in _SMALL_SH], 8))
    got = _gather_chips(send)
    full = {n: wts[n] for n in _SMALL_REP}
    for (name, l), g in zip(keys, got[:-1]):
        key = name + "_t" if dict(_LARGE)[name] else name
        full.setdefault(key, []).append(g.reshape(N_CHIPS * g.shape[1], g.shape[2]))
    for key in ("e_w_in_t", "e_w_out", "o_w_in", "o_w_glu_t"):
        full[key] = full[key][0]
    per_chip = [_unpack(got[-1][k], sh_shapes) for k in range(N_CHIPS)]
    for i, n in enumerate(_SMALL_SH):
        full[n] = jnp.concatenate([per_chip[k][i] for k in range(N_CHIPS)], axis=-1)

    loss, grad_x, gb, gs = _local_step(x, tgt, full)

    parts, keys = [], []
    for name, by_cols in _LARGE:
        g = gb[name + "_t" if by_cols else name]
        for l, gl in enumerate(g if isinstance(g, list) else [g]):
            parts.append(gl.reshape(N_DEV, gl.shape[0] // N_DEV, gl.shape[1]))
            keys.append((name, l, by_cols))
    staged = _scatter_devices(parts)
    halves = [_sum_segments(s, f"sum_{name}{l}") for s, (name, l, _) in zip(staged, keys)]
    shards = _exchange_sibling(halves)
    final = {}
    for s, (name, l, by_cols) in zip(shards, keys):
        final[name] = _adamw_big(wts[name], ms[name], vs[name], l, s.reshape(2 * s.shape[1], s.shape[2]), by_cols,
                                 f"adamw_{name}{l}", prev=final.get(name))

    small = _SMALL_REP + _SMALL_SH
    shapes = [gs[n].shape for n in small]
    red = _allreduce_small(_pack([gs[n] for n in small], 8 * N_DEV).reshape(N_DEV, -1, LANES))
    tot = dict(zip(small, _unpack(red, shapes)))
    for n in _SMALL_SH:
        width = wts[n].shape[-1]
        tot[n] = lax.dynamic_slice_in_dim(tot[n], chip * width, width, axis=tot[n].ndim - 1)
    loc_shapes = [wts[n].shape for n in small]
    pk = lambda d: _pack([d[n] for n in small], 8)
    delta, new_m, new_v = _adamw_small(pk(wts), pk(tot), pk(ms), pk(vs))
    for n, g, d, m2, v2 in zip(small, [tot[n] for n in small], _unpack(delta, loc_shapes), _unpack(new_m, loc_shapes),
                               _unpack(new_v, loc_shapes)):
        final[n] = [g.reshape(wts[n].shape), d, m2, v2]

    loss = lax.psum(loss[0, 0], ("x", "y", "c"))
    res = [loss, grad_x[None]]
    for k in range(4):
        res += [final[n][k] for n in _ORDER]
    return tuple(res)


def kernel(x, e_norm_g, e_w_in, e_mu, e_w0, e_w2, e_a0, e_a2, e_g2, e_k_k, e_k_a, e_r_k, e_ln_w, e_ln_b, e_conv_w, e_conv_b, e_gate_a_w, e_gate_a_b, e_gate_x_w, e_gate_x_b, e_lru_lambda, e_w_out, o_norm_g, o_w_in, o_A_re, o_A_im, o_log_dt, o_B_re, o_B_im, o_C_re, o_C_im, o_D, o_w_glu, f_norm_g, f_w_up, f_conv_w, f_conv_b, f_w_down, final_norm_g, loss_target, m_e_norm_g, m_e_w_in, m_e_mu, m_e_w0, m_e_w2, m_e_a0, m_e_a2, m_e_g2, m_e_k_k, m_e_k_a, m_e_r_k, m_e_ln_w, m_e_ln_b, m_e_conv_w, m_e_conv_b, m_e_gate_a_w, m_e_gate_a_b, m_e_gate_x_w, m_e_gate_x_b, m_e_lru_lambda, m_e_w_out, m_o_norm_g, m_o_w_in, m_o_A_re, m_o_A_im, m_o_log_dt, m_o_B_re, m_o_B_im, m_o_C_re, m_o_C_im, m_o_D, m_o_w_glu, m_f_norm_g, m_f_w_up, m_f_conv_w, m_f_conv_b, m_f_w_down, m_final_norm_g, v_e_norm_g, v_e_w_in, v_e_mu, v_e_w0, v_e_w2, v_e_a0, v_e_a2, v_e_g2, v_e_k_k, v_e_k_a, v_e_r_k, v_e_ln_w, v_e_ln_b, v_e_conv_w, v_e_conv_b, v_e_gate_a_w, v_e_gate_a_b, v_e_gate_x_w, v_e_gate_x_b, v_e_lru_lambda, v_e_w_out, v_o_norm_g, v_o_w_in, v_o_A_re, v_o_A_im, v_o_log_dt, v_o_B_re, v_o_B_im, v_o_C_re, v_o_C_im, v_o_D, v_o_w_glu, v_f_norm_g, v_f_w_up, v_f_conv_w, v_f_conv_b, v_f_w_down, v_final_norm_g):
    args = locals()
    wts = {n: args[n] for n in _ORDER}
    ms = {n: args["m_" + n] for n in _ORDER}
    vs = {n: args["v_" + n] for n in _ORDER}
    return _step(x[0], loss_target[0], wts, ms, vs)
```

```python
import functools

import jax
import jax.numpy as jnp
from jax import lax
from jax.experimental import pallas as pl
from jax.experimental.pallas import tpu as pltpu

F32 = jnp.float32
BF16 = jnp.bfloat16
MESH = pl.DeviceIdType.MESH

HEAD = 64
RW = 512
N_HEADS = RW // HEAD
LRU_W = 512
SHIFT_COLS = 1792
W_LORA, A_LORA, G_LORA = 64, 64, 128
S5_GROUPS, S5_GROUP, S5_STATE = 64, 16, 64
D_FF = 2816
NORM_EPS = 1e-6
GN_EPS = 64e-5
LRU_C = 8.0
ADAM_LR, ADAM_B1, ADAM_B2, ADAM_EPS, ADAM_WD, ADAM_STEP = 0.001, 0.9, 0.999, 1e-08, 0.01, 10

VMEM_BIG = 56 * 1024 * 1024
VMEM_MID = 40 * 1024 * 1024
LANES = 128
PT = 16
WKV_CHUNK = 32
S5_SLAB = 128


def _cparams(sem=None, vmem=None):
    kw = {}
    if sem is not None:
        kw["dimension_semantics"] = sem
    if vmem is not None:
        kw["vmem_limit_bytes"] = vmem
    return pltpu.CompilerParams(**kw)


def _tile(dim, cands):
    for c in cands:
        if dim % c == 0:
            return c
    return dim


def _full(shape):
    n = len(shape)
    return pl.BlockSpec(shape, lambda *_: (0,) * n)


_TILES = (1408, 1024, 512, 256, 128)


def _matmul(a, b, mode, name, out_dtype=F32, add=None):
    if mode == "nn":
        (m, k), (k2, n) = a.shape, b.shape
    elif mode == "nt":
        (m, k), (n, k2) = a.shape, b.shape
    else:
        (k, m), (k2, n) = a.shape, b.shape
    assert k == k2, (a.shape, b.shape, mode)
    if mode == "tn":
        tm, tn, tk = _tile(m, _TILES), _tile(n, (1024, 512, 256, 128)), _tile(k, (512, 256, 128))
    else:
        tm, tn, tk = _tile(m, (512, 256, 128)), _tile(n, _TILES), _tile(k, _TILES)
    nk = k // tk
    dims = {"nn": (((1,), (0,)), ((), ())), "nt": (((1,), (1,)), ((), ())), "tn": (((0,), (0,)), ((), ()))}[mode]

    def body(*refs):
        if add is None:
            a_ref, b_ref, o_ref, acc = refs
            add_ref = None
        else:
            a_ref, b_ref, add_ref, o_ref, acc = refs
        kk = pl.program_id(2)

        @pl.when(kk == 0)
        def _():
            acc[...] = jnp.zeros_like(acc)

        acc[...] += lax.dot_general(a_ref[...].astype(BF16), b_ref[...].astype(BF16), dims,
                                    preferred_element_type=F32)

        @pl.when(kk == nk - 1)
        def _():
            r = acc[...]
            if add_ref is not None:
                r = r + add_ref[...]
            o_ref[...] = r.astype(o_ref.dtype)

    if mode == "nn":
        a_spec = pl.BlockSpec((tm, tk), lambda i, j, kk: (i, kk))
        b_spec = pl.BlockSpec((tk, tn), lambda i, j, kk: (kk, j))
    elif mode == "nt":
        a_spec = pl.BlockSpec((tm, tk), lambda i, j, kk: (i, kk))
        b_spec = pl.BlockSpec((tn, tk), lambda i, j, kk: (j, kk))
    else:
        a_spec = pl.BlockSpec((tk, tm), lambda i, j, kk: (kk, i))
        b_spec = pl.BlockSpec((tk, tn), lambda i, j, kk: (kk, j))
    o_spec = pl.BlockSpec((tm, tn), lambda i, j, kk: (i, j))
    in_specs = [a_spec, b_spec] + ([o_spec] if add is not None else [])
    args = (a, b) + ((add,) if add is not None else ())
    return pl.pallas_call(
        body, name=name, grid=(m // tm, n // tn, nk),
        in_specs=in_specs, out_specs=o_spec,
        out_shape=jax.ShapeDtypeStruct((m, n), out_dtype),
        scratch_shapes=[pltpu.VMEM((tm, tn), F32)],
        compiler_params=_cparams(("parallel", "parallel", "arbitrary"), VMEM_MID),
    )(*args)


TOK = 256


def _rms(x, g):
    return x * lax.rsqrt(jnp.mean(x * x, axis=-1, keepdims=True) + NORM_EPS) * g


def _rms_fwd(x, g, name):
    t, d = x.shape

    def body(x_ref, g_ref, o_ref):
        o_ref[...] = _rms(x_ref[...], g_ref[...]).astype(BF16)

    row = pl.BlockSpec((TOK, d), lambda i: (i, 0))
    return pl.pallas_call(body, name=name, grid=(t // TOK,), in_specs=[row, _full((1, d))], out_specs=row,
                          out_shape=jax.ShapeDtypeStruct((t, d), BF16),
                          compiler_params=_cparams(("parallel",)))(x, g)


def _rms_bwd(x, g, dxn, res, name):
    t, d = x.shape

    def body(x_ref, g_ref, d_ref, res_ref, dx_ref, dg_ref):
        _, vjp = jax.vjp(_rms, x_ref[...], g_ref[...])
        dx, dg = vjp(d_ref[...].astype(F32))
        dx_ref[...] = dx + res_ref[...]

        @pl.when(pl.program_id(0) == 0)
        def _():
            dg_ref[...] = jnp.zeros_like(dg_ref)

        dg_ref[...] += dg

    row = pl.BlockSpec((TOK, d), lambda i: (i, 0))
    return pl.pallas_call(body, name=name, grid=(t // TOK,), in_specs=[row, _full((1, d)), row, row],
                          out_specs=[row, _full((1, d))],
                          out_shape=[jax.ShapeDtypeStruct((t, d), F32), jax.ShapeDtypeStruct((1, d), F32)],
                          compiler_params=_cparams(("arbitrary",)))(x, g, dxn, res)


def _loss_head(x, g, tgt):
    t, d = x.shape

    def body(x_ref, g_ref, t_ref, l_ref, dx_ref, dg_ref):
        tg = t_ref[...]

        def fn(xv, gv):
            err = _rms(xv, gv) - tg
            per_tok = jnp.mean(err * err, axis=-1, keepdims=True)
            return 0.5 * jnp.sum(per_tok, axis=0, keepdims=True)

        l, vjp = jax.vjp(fn, x_ref[...], g_ref[...])
        dx, dg = vjp(jnp.ones((1, 1), F32))
        dx_ref[...] = dx

        @pl.when(pl.program_id(0) == 0)
        def _():
            dg_ref[...] = jnp.zeros_like(dg_ref)
            l_ref[...] = jnp.zeros_like(l_ref)

        dg_ref[...] += dg
        l_ref[...] += jnp.broadcast_to(l, l_ref.shape)

    row = pl.BlockSpec((TOK, d), lambda i: (i, 0))
    return pl.pallas_call(body, name="loss_head", grid=(t // TOK,), in_specs=[row, _full((1, d)), row],
                          out_specs=[_full((1, LANES)), row, _full((1, d))],
                          out_shape=[jax.ShapeDtypeStruct((1, LANES), F32), jax.ShapeDtypeStruct((t, d), F32),
                                     jax.ShapeDtypeStruct((1, d), F32)],
                          compiler_params=_cparams(("arbitrary",)))(x, g, tgt)


def _glu_fwd(x, z):
    t, d = x.shape

    def body(x_ref, v_ref, g_ref, o_ref):
        o_ref[...] = x_ref[...] + v_ref[...] * jax.nn.sigmoid(g_ref[...])

    row = pl.BlockSpec((TOK, d), lambda i: (i, 0))
    gate = pl.BlockSpec((TOK, d), lambda i: (i, 1))
    return pl.pallas_call(body, name="glu_fwd", grid=(t // TOK,), in_specs=[row, row, gate], out_specs=row,
                          out_shape=jax.ShapeDtypeStruct((t, d), F32),
                          compiler_params=_cparams(("parallel",)))(x, z, z)


def _glu_bwd(z, g):
    t, d = g.shape

    def body(v_ref, g_ref, d_ref, o_ref):
        s = jax.nn.sigmoid(g_ref[...])
        dy = d_ref[...]
        o_ref[:, :d] = (dy * s).astype(BF16)
        o_ref[:, d:] = (dy * v_ref[...] * s * (1.0 - s)).astype(BF16)

    row = pl.BlockSpec((TOK, d), lambda i: (i, 0))
    gate = pl.BlockSpec((TOK, d), lambda i: (i, 1))
    return pl.pallas_call(body, name="glu_bwd", grid=(t // TOK,), in_specs=[row, gate, row],
                          out_specs=pl.BlockSpec((TOK, 2 * d), lambda i: (i, 0)),
                          out_shape=jax.ShapeDtypeStruct((t, 2 * d), BF16),
                          compiler_params=_cparams(("parallel",)))(z, z, g)


def _shift_down(x, d):
    row = lax.broadcasted_iota(jnp.int32, x.shape, 0)
    return jnp.where(row < d, 0.0, pltpu.roll(x, d, 0))


def _shift_up(x, d):
    n = x.shape[0]
    row = lax.broadcasted_iota(jnp.int32, x.shape, 0)
    return jnp.where(row >= n - d, 0.0, pltpu.roll(x, n - d, 0))


def _make_sd():
    @functools.partial(jax.custom_vjp, nondiff_argnums=(1,))
    def sd(x, d):
        return _shift_down(x, d)

    def fwd(x, d):
        return _shift_down(x, d), None

    def bwd(d, _, g):
        return (_shift_up(g, d),)

    sd.defvjp(fwd, bwd)
    return sd


def _lin_scan(a, u, reverse=False):
    n = a.shape[0]
    row = lax.broadcasted_iota(jnp.int32, a.shape, 0)
    d = 1
    while d < n:
        if reverse:
            keep = row < n - d
            a_s, u_s = pltpu.roll(a, n - d, 0), pltpu.roll(u, n - d, 0)
        else:
            keep = row >= d
            a_s, u_s = pltpu.roll(a, d, 0), pltpu.roll(u, d, 0)
        u = u + a * jnp.where(keep, u_s, 0.0)
        a = a * jnp.where(keep, a_s, 1.0)
        d *= 2
    return u


def _make_scan():
    @jax.custom_vjp
    def scan(a, u):
        return _lin_scan(a, u)

    def fwd(a, u):
        h = _lin_scan(a, u)
        return h, (a, h)

    def bwd(res, dh):
        a, h = res
        g = _lin_scan(_shift_up(a, 1), dh, reverse=True)
        return g * _shift_down(h, 1), g

    scan.defvjp(fwd, bwd)
    return scan


def _acc_out(ref, val):
    @pl.when(pl.program_id(0) == 0)
    def _():
        ref[...] = jnp.zeros_like(ref)

    ref[...] += val


FFN_CW = 128


def _ffn_fn(hg, hv, wg, wv, bg, bv, sd):
    cg = wg[0:1] * sd(hg, 2) + wg[1:2] * sd(hg, 1) + wg[2:3] * hg + bg
    cv = wv[0:1] * sd(hv, 2) + wv[1:2] * sd(hv, 1) + wv[2:3] * hv + bv
    return jax.nn.silu(cg) * cv


def _ffn_specs(t):
    nb = D_FF // FFN_CW
    col = lambda r, off: pl.BlockSpec((r, FFN_CW), lambda j: (0, j + off))
    return nb, [col(t, 0), col(t, nb), col(3, 0), col(3, nb), col(1, 0), col(1, nb)], col


def _ffn_mid_fwd(h, cw, cb, name):
    t = h.shape[0]
    nb, in_specs, col = _ffn_specs(t)

    def body(hg, hv, wg, wv, bg, bv, o_ref):
        o_ref[...] = _ffn_fn(hg[...], hv[...], wg[...], wv[...], bg[...], bv[...], _shift_down).astype(BF16)

    return pl.pallas_call(body, name=name, grid=(nb,), in_specs=in_specs, out_specs=col(t, 0),
                          out_shape=jax.ShapeDtypeStruct((t, D_FF), BF16),
                          compiler_params=_cparams(("parallel",), VMEM_MID))(h, h, cw, cw, cb, cb)


def _ffn_mid_bwd(h, cw, cb, dact, name):
    t = h.shape[0]
    nb, in_specs, col = _ffn_specs(t)

    def body(hg, hv, wg, wv, bg, bv, d_ref, dhg, dhv, dwg, dwv, dbg, dbv):
        fn = functools.partial(_ffn_fn, sd=_make_sd())
        _, vjp = jax.vjp(fn, hg[...], hv[...], wg[...], wv[...], bg[...], bv[...])
        g = vjp(d_ref[...])
        dhg[...] = g[0].astype(BF16)
        dhv[...] = g[1].astype(BF16)
        dwg[...], dwv[...], dbg[...], dbv[...] = g[2], g[3], g[4], g[5]

    big = jax.ShapeDtypeStruct((t, D_FF), BF16)
    w3 = jax.ShapeDtypeStruct((3, D_FF), F32)
    b1 = jax.ShapeDtypeStruct((1, D_FF), F32)
    return pl.pallas_call(body, name=name, grid=(nb,), in_specs=in_specs + [col(t, 0)],
                          out_specs=[col(t, 0), col(t, 0), col(3, 0), col(3, 0), col(1, 0), col(1, 0)],
                          out_shape=[big, big, w3, w3, b1, b1],
                          compiler_params=_cparams(("parallel",), VMEM_BIG))(h, h, cw, cw, cb, cb, dact)


TS_CW = 256


def _tshift_fn(p, mu, sd):
    return p + mu * (sd(p, 1) - p)


def _tshift_fwd(p, mu):
    t = p.shape[0]
    col = lambda r: pl.BlockSpec((r, TS_CW), lambda j: (0, j))

    def body(p_ref, mu_ref, o_ref):
        o_ref[...] = _tshift_fn(p_ref[...], mu_ref[...], _shift_down)

    return pl.pallas_call(body, name="tshift_fwd", grid=(SHIFT_COLS // TS_CW,), in_specs=[col(t), col(1)],
                          out_specs=col(t), out_shape=jax.ShapeDtypeStruct((t, SHIFT_COLS), F32),
                          compiler_params=_cparams(("parallel",), VMEM_MID))(p, mu)


def _tshift_bwd(p, mu, dpam):
    t = p.shape[0]
    col = lambda r: pl.BlockSpec((r, TS_CW), lambda j: (0, j))

    def body(p_ref, mu_ref, d_ref, dp_ref, dmu_ref):
        _, vjp = jax.vjp(functools.partial(_tshift_fn, sd=_make_sd()), p_ref[...], mu_ref[...])
        dp, dmu = vjp(d_ref[...])
        dp_ref[...] = dp.astype(BF16)
        dmu_ref[...] = dmu

    return pl.pallas_call(body, name="tshift_bwd", grid=(SHIFT_COLS // TS_CW,), in_specs=[col(t), col(1), col(t)],
                          out_specs=[col(t), col(1)],
                          out_shape=[jax.ShapeDtypeStruct((t, SHIFT_COLS), BF16),
                                     jax.ShapeDtypeStruct((1, SHIFT_COLS), F32)],
                          compiler_params=_cparams(("parallel",), VMEM_MID))(p, mu, dpam)


_HI = lax.Precision.HIGHEST
_O = (0, RW, 2 * RW, 3 * RW, 3 * RW + W_LORA, 3 * RW + W_LORA + A_LORA, SHIFT_COLS)


def _seg(x, gm):
    return jnp.dot(x, gm, precision=_HI)


def _prep_fn(r, k, v, wd, ad, gd, w0, w2, a0, a2, g2, k_k, k_a, gm):
    w_log = -jax.nn.softplus(-(w0 + jnp.tanh(wd) @ w2)) - 0.5
    decay = jnp.exp(-jnp.exp(w_log))
    a = jax.nn.sigmoid(a0 + ad @ a2)
    g = jax.nn.sigmoid(gd) @ g2
    kk = k * k_k
    kk = kk / jnp.maximum(jnp.sqrt(_seg(kk * kk, gm)), 1e-12)
    k2 = k * (1.0 + (a - 1.0) * k_a)
    return r, decay, k2, v, -kk, kk * a, g


_PREP_W = ("w0", "w2", "a0", "a2", "g2", "k_k", "k_a")


def _prep_wspecs(w):
    return [_full(w[n].shape) for n in _PREP_W] + [_full((RW, RW))]


def _rwkv_prep_fwd(pam, w, gm):
    t = pam.shape[0]

    def body(p_ref, *refs):
        wr, outs = refs[:8], refs[8:]
        pieces = [p_ref[:, _O[i]:_O[i + 1]] for i in range(6)]
        res = _prep_fn(*pieces, *[x[...] for x in wr])
        for o, val in zip(outs, res):
            o[...] = val

    row = lambda c: pl.BlockSpec((TOK, c), lambda i: (i, 0))
    return pl.pallas_call(body, name="rwkv_prep_fwd", grid=(t // TOK,),
                          in_specs=[row(SHIFT_COLS)] + _prep_wspecs(w), out_specs=[row(RW)] * 7,
                          out_shape=[jax.ShapeDtypeStruct((t, RW), F32)] * 7,
                          compiler_params=_cparams(("parallel",), VMEM_MID))(pam, *[w[n] for n in _PREP_W], gm)


def _rwkv_prep_bwd(pam, w, gm, cts, more):
    t = pam.shape[0]

    def body(p_ref, *refs):
        wr, ct, ex, dp_ref, dws = refs[:8], refs[8:15], refs[15:18], refs[18], refs[19:]
        pieces = [p_ref[:, _O[i]:_O[i + 1]] for i in range(6)]
        fn = lambda *a: _prep_fn(*a, wr[7][...])
        _, vjp = jax.vjp(fn, *pieces, *[x[...] for x in wr[:7]])
        c = [x[...] for x in ct]
        c[0] = c[0] + ex[0][...]
        c[2] = c[2] + ex[1][...]
        c[3] = c[3] + ex[2][...]
        g = vjp(tuple(c))
        for i in range(6):
            dp_ref[:, _O[i]:_O[i + 1]] = g[i]
        for o, val in zip(dws, g[6:]):
            _acc_out(o, val)

    row = lambda c: pl.BlockSpec((TOK, c), lambda i: (i, 0))
    return pl.pallas_call(body, name="rwkv_prep_bwd", grid=(t // TOK,),
                          in_specs=[row(SHIFT_COLS)] + _prep_wspecs(w) + [row(RW)] * 10,
                          out_specs=[row(SHIFT_COLS)] + [_full(w[n].shape) for n in _PREP_W],
                          out_shape=[jax.ShapeDtypeStruct((t, SHIFT_COLS), F32)]
                          + [jax.ShapeDtypeStruct(w[n].shape, F32) for n in _PREP_W],
                          compiler_params=_cparams(("arbitrary",), VMEM_MID))(
                              pam, *[w[n] for n in _PREP_W], gm, *cts, *more)


def _post_fn(y, r, k2, v, g, ln_w, ln_b, r_k, gm):
    inv = 1.0 / HEAD
    d = y - _seg(y, gm) * inv
    yn = d * lax.rsqrt(_seg(d * d, gm) * inv + GN_EPS) * ln_w + ln_b
    bonus = _seg(r * k2 * r_k, gm) * v
    return (yn + bonus) * g


def _rwkv_post_fwd(y, r, k2, v, g, ln_w, ln_b, r_k, gm):
    t = y.shape[0]

    def body(*refs):
        o_ref = refs[-1]
        o_ref[...] = _post_fn(*[x[...] for x in refs[:-1]]).astype(BF16)

    row = pl.BlockSpec((TOK, RW), lambda i: (i, 0))
    return pl.pallas_call(body, name="rwkv_post_fwd", grid=(t // TOK,),
                          in_specs=[row] * 5 + [_full((1, RW))] * 3 + [_full((RW, RW))], out_specs=row,
                          out_shape=jax.ShapeDtypeStruct((t, RW), BF16),
                          compiler_params=_cparams(("parallel",), VMEM_MID))(y, r, k2, v, g, ln_w, ln_b, r_k, gm)


def _rwkv_post_bwd(y, r, k2, v, g, ln_w, ln_b, r_k, gm, dya):
    t = y.shape[0]

    def body(*refs):
        ins, gm_ref, d_ref, outs = refs[:8], refs[8], refs[9], refs[10:]
        fn = lambda *a: _post_fn(*a, gm_ref[...])
        _, vjp = jax.vjp(fn, *[x[...] for x in ins])
        gr = vjp(d_ref[...])
        for o, val in zip(outs[:5], gr[:5]):
            o[...] = val
        for o, val in zip(outs[5:], gr[5:]):
            _acc_out(o, val)

    row = pl.BlockSpec((TOK, RW), lambda i: (i, 0))
    vec = _full((1, RW))
    return pl.pallas_call(body, name="rwkv_post_bwd", grid=(t // TOK,),
                          in_specs=[row] * 5 + [vec] * 3 + [_full((RW, RW)), row],
                          out_specs=[row] * 5 + [vec] * 3,
                          out_shape=[jax.ShapeDtypeStruct((t, RW), F32)] * 5 + [jax.ShapeDtypeStruct((1, RW), F32)] * 3,
                          compiler_params=_cparams(("arbitrary",), VMEM_MID))(y, r, k2, v, g, ln_w, ln_b, r_k, gm, dya)


def _from_pt(x):
    n = x.shape[0]
    return x.reshape(n, HEAD, N_HEADS, PT).transpose(0, 3, 2, 1).reshape(n * PT, N_HEADS * HEAD)


def _lane_sum(x):
    return jnp.sum(x, axis=-1, keepdims=True)


def _pair_consts():
    lane = lax.broadcasted_iota(jnp.int32, (HEAD, LANES), 1)
    return lane, lane < HEAD


def _seg_sum_pair(x, first):
    return jnp.where(first, _lane_sum(jnp.where(first, x, 0.0)), _lane_sum(jnp.where(first, 0.0, x)))


def _to_pt(x):
    t = x.shape[0]
    return x.reshape(t // PT, PT, N_HEADS, HEAD).transpose(0, 3, 2, 1).reshape(t // PT, HEAD, N_HEADS * PT)


def _expand_cols(x, name):
    t = x.shape[0]
    tiles = WKV_CHUNK // PT

    def body(x_ref, o_ref):
        lane, first = _pair_consts()
        for tl in range(tiles):
            tile = x_ref[tl]
            for j in range(PT):
                for p in range(N_HEADS // 2):
                    c0 = _lane_sum(jnp.where(lane == (2 * p) * PT + j, tile, 0.0))
                    c1 = _lane_sum(jnp.where(lane == (2 * p + 1) * PT + j, tile, 0.0))
                    o_ref[tl * PT + j, :, p * LANES:(p + 1) * LANES] = jnp.where(first, c0, c1)

    return pl.pallas_call(
        body, name=name, grid=(t // WKV_CHUNK,),
        in_specs=[pl.BlockSpec((tiles, HEAD, LANES), lambda i: (i, 0, 0))],
        out_specs=pl.BlockSpec((WKV_CHUNK, HEAD, RW), lambda i: (i, 0, 0)),
        out_shape=jax.ShapeDtypeStruct((t, HEAD, RW), F32),
        compiler_params=_cparams(("parallel",), VMEM_MID))(_to_pt(x))


def _wkv_fwd(w, k, z, b, v_exp):
    t = w.shape[0]
    nc = t // WKV_CHUNK
    pairs = N_HEADS // 2

    def body(w_ref, k_ref, z_ref, b_ref, v_ref, s_all, s_ref):
        @pl.when(pl.program_id(0) == 0)
        def _():
            s_ref[...] = jnp.zeros_like(s_ref)

        _, first = _pair_consts()

        def group(gi, carry):
            base = pl.multiple_of(gi * 8, 8)
            rows = [ref[pl.ds(base, 8), :] for ref in (w_ref, k_ref, z_ref, b_ref)]
            s = [s_ref[:, p * LANES:(p + 1) * LANES] for p in range(pairs)]
            for jj in range(8):
                for p in range(pairs):
                    cs = slice(p * LANES, (p + 1) * LANES)
                    wr, kr, zr, br = [x[jj:jj + 1, cs] for x in rows]
                    s_all[base + jj, :, cs] = s[p]
                    sa = _seg_sum_pair(s[p] * zr, first)
                    s[p] = s[p] * wr + sa * br + v_ref[base + jj, :, cs] * kr
            for p in range(pairs):
                s_ref[:, p * LANES:(p + 1) * LANES] = s[p]
            return carry

        lax.fori_loop(0, WKV_CHUNK // 8, group, 0)

    row = pl.BlockSpec((WKV_CHUNK, RW), lambda i: (i, 0))
    big = pl.BlockSpec((WKV_CHUNK, HEAD, RW), lambda i: (i, 0, 0))
    return pl.pallas_call(
        body, name="wkv_fwd", grid=(nc,), in_specs=[row] * 4 + [big], out_specs=[big, _full((HEAD, RW))],
        out_shape=[jax.ShapeDtypeStruct((t, HEAD, RW), F32), jax.ShapeDtypeStruct((HEAD, RW), F32)],
        compiler_params=_cparams(("arbitrary",), VMEM_MID))(w, k, z, b, v_exp)


def _wkv_out(r, s_all, s_last):
    t = r.shape[0]
    nc = t // WKV_CHUNK
    tiles = WKV_CHUNK // PT
    pairs = N_HEADS // 2

    def body(r_ref, s_ref, nxt_ref, last_ref, y_ref):
        lane, first = _pair_consts()
        after = jnp.where(pl.program_id(0) == nc - 1, last_ref[...], nxt_ref[0])
        for tl in range(tiles):
            ytile = jnp.zeros((HEAD, LANES), F32)
            for g in range(PT // 8):
                rows = r_ref[tl * PT + g * 8:tl * PT + g * 8 + 8, :]
                for jj in range(8):
                    tt = tl * PT + g * 8 + jj
                    j = g * 8 + jj
                    for p in range(pairs):
                        cs = slice(p * LANES, (p + 1) * LANES)
                        s = s_ref[tt + 1, :, cs] if tt + 1 < WKV_CHUNK else after[:, cs]
                        pr = s * rows[jj:jj + 1, cs]
                        y0 = _lane_sum(jnp.where(first, pr, 0.0))
                        y1 = _lane_sum(jnp.where(first, 0.0, pr))
                        ytile = jnp.where(lane == (2 * p) * PT + j, y0, ytile)
                        ytile = jnp.where(lane == (2 * p + 1) * PT + j, y1, ytile)
            y_ref[tl] = ytile

    row = pl.BlockSpec((WKV_CHUNK, RW), lambda i: (i, 0))
    pt = pl.BlockSpec((tiles, HEAD, LANES), lambda i: (i, 0, 0))
    big = pl.BlockSpec((WKV_CHUNK, HEAD, RW), lambda i: (i, 0, 0))
    nxt = pl.BlockSpec((1, HEAD, RW), lambda i: (jnp.minimum((i + 1) * WKV_CHUNK, t - 1), 0, 0))
    return pl.pallas_call(
        body, name="wkv_out", grid=(nc,), in_specs=[row, big, nxt, _full((HEAD, RW))], out_specs=pt,
        out_shape=jax.ShapeDtypeStruct((t // PT, HEAD, LANES), F32),
        compiler_params=_cparams(("parallel",), VMEM_MID))(r, s_all, s_all, s_last)


def _wkv_bwd(r, w, k, z, b, v_exp, s_all, dy_exp):
    t = r.shape[0]
    nc = t // WKV_CHUNK
    tiles = WKV_CHUNK // PT
    pairs = N_HEADS // 2

    def body(r_ref, w_ref, k_ref, z_ref, b_ref, v_ref, s_all_ref, dy_ref,
             dr_ref, dw_ref, dk_ref, dz_ref, db_ref, dv_ref, ds_ref):
        @pl.when(pl.program_id(0) == 0)
        def _():
            ds_ref[...] = jnp.zeros_like(ds_ref)

        lane, first = _pair_consts()
        col_sum = lambda x: jnp.sum(x, axis=0, keepdims=True)
        row8 = lax.broadcasted_iota(jnp.int32, (8, LANES), 0)
        for tl in reversed(range(tiles)):
            def group(gg, dvtile):
                gi = PT // 8 - 1 - gg
                base = pl.multiple_of(tl * PT + gi * 8, 8)
                rows = [ref[pl.ds(base, 8), :] for ref in (r_ref, w_ref, k_ref, z_ref, b_ref)]
                outs = (dr_ref, dw_ref, dk_ref, dz_ref, db_ref)
                tiles8 = {(id(o), p): jnp.zeros((8, LANES), F32) for o in outs for p in range(pairs)}
                ds = [ds_ref[:, p * LANES:(p + 1) * LANES] for p in range(pairs)]
                for jj in reversed(range(8)):
                    j = gi * 8 + jj
                    for p in range(pairs):
                        cs = slice(p * LANES, (p + 1) * LANES)

                        def put(ref, val, p=p, jj=jj):
                            tiles8[(id(ref), p)] = jnp.where(row8 == jj, val, tiles8[(id(ref), p)])

                        rr, wr, kr, zr, br = [x[jj:jj + 1, cs] for x in rows]
                        sp = s_all_ref[base + jj, :, cs]
                        vc = v_ref[base + jj, :, cs]
                        dyc = dy_ref[base + jj, :, cs]
                        sa = _seg_sum_pair(sp * zr, first)
                        st = sp * wr + sa * br + vc * kr
                        d = ds[p] + dyc * rr
                        put(dr_ref, col_sum(st * dyc))
                        dvk = d * kr
                        dv0 = _lane_sum(jnp.where(first, dvk, 0.0))
                        dv1 = _lane_sum(jnp.where(first, 0.0, dvk))
                        dvtile = jnp.where(lane == (2 * p) * PT + j, dv0, dvtile)
                        dvtile = jnp.where(lane == (2 * p + 1) * PT + j, dv1, dvtile)
                        put(dk_ref, col_sum(d * vc))
                        put(dw_ref, col_sum(sp * d))
                        u = _seg_sum_pair(d * br, first)
                        put(dz_ref, col_sum(sp * u))
                        put(db_ref, col_sum(d * sa))
                        ds[p] = d * wr + u * zr
                for p in range(pairs):
                    ds_ref[:, p * LANES:(p + 1) * LANES] = ds[p]
                for o in outs:
                    for p in range(pairs):
                        o[pl.ds(base, 8), p * LANES:(p + 1) * LANES] = tiles8[(id(o), p)]
                return dvtile

            dv_ref[tl] = lax.fori_loop(0, PT // 8, group, jnp.zeros((HEAD, LANES), F32))

    rev = lambda i: nc - 1 - i
    row = pl.BlockSpec((WKV_CHUNK, RW), lambda i: (rev(i), 0))
    pt = pl.BlockSpec((tiles, HEAD, LANES), lambda i: (rev(i), 0, 0))
    big = pl.BlockSpec((WKV_CHUNK, HEAD, RW), lambda i: (rev(i), 0, 0))
    return pl.pallas_call(
        body, name="wkv_bwd", grid=(nc,), in_specs=[row] * 5 + [big, big, big], out_specs=[row] * 5 + [pt],
        out_shape=[jax.ShapeDtypeStruct((t, RW), F32)] * 5 + [jax.ShapeDtypeStruct((t // PT, HEAD, LANES), F32)],
        scratch_shapes=[pltpu.VMEM((HEAD, RW), F32)],
        compiler_params=_cparams(("arbitrary",), VMEM_BIG))(r, w, k, z, b, v_exp, s_all, dy_exp)


LRU_CW = 128
_BX0 = SHIFT_COLS // LRU_CW
_BG0 = (SHIFT_COLS + LRU_W) // LRU_CW


def _lru_fn(bx, bg, cw, cb, ga, ba, gx, bxb, lam, sd, scan):
    xc = cw[0:1] * sd(bx, 3) + cw[1:2] * sd(bx, 2) + cw[2:3] * sd(bx, 1) + cw[3:4] * bx + cb
    gr = jax.nn.sigmoid(xc @ ga + ba)
    gi = jax.nn.sigmoid(xc @ gx + bxb)
    log_a = -LRU_C * gr * jax.nn.softplus(-lam)
    a = jnp.exp(log_a)
    mult = jnp.sqrt(-jnp.tanh(log_a) * (jnp.exp(2.0 * log_a) + 1.0))
    return scan(a, xc * gi * mult) * jax.nn.gelu(bg)


def _lru_specs(t):
    col = lambda r, off=0: pl.BlockSpec((r, LRU_CW), lambda j: (0, j + off))
    diag = pl.BlockSpec((LRU_CW, LRU_CW), lambda j: (j, j))
    return col, [col(t, _BX0), col(t, _BG0), col(4), col(1), diag, col(1), diag, col(1), col(1)]


def _lru_fwd(p, cw, cb, ga, ba, gx, bxb, lam):
    t = p.shape[0]
    col, in_specs = _lru_specs(t)

    def body(*refs):
        o_ref = refs[-1]
        o_ref[...] = _lru_fn(*[x[...] for x in refs[:-1]], _shift_down, _lin_scan).astype(BF16)

    return pl.pallas_call(body, name="lru_fwd", grid=(LRU_W // LRU_CW,), in_specs=in_specs, out_specs=col(t),
                          out_shape=jax.ShapeDtypeStruct((t, LRU_W), BF16),
                          compiler_params=_cparams(("parallel",), VMEM_MID))(p, p, cw, cb, ga, ba, gx, bxb, lam)


def _lru_bwd(p, cw, cb, ga, ba, gx, bxb, lam, dyb):
    t = p.shape[0]
    col, in_specs = _lru_specs(t)

    def body(*refs):
        ins, d_ref, outs = refs[:9], refs[9], refs[10:]
        fn = functools.partial(_lru_fn, sd=_make_sd(), scan=_make_scan())
        _, vjp = jax.vjp(fn, *[x[...] for x in ins])
        g = vjp(d_ref[...])
        outs[0][...] = g[0].astype(BF16)
        outs[1][...] = g[1].astype(BF16)
        for o, val in zip(outs[2:], g[2:]):
            o[...] = val

    sq = pl.BlockSpec((LRU_CW, LRU_CW), lambda j: (j, 0))
    act = jax.ShapeDtypeStruct((t, LRU_W), BF16)
    vec = jax.ShapeDtypeStruct((1, LRU_W), F32)
    sqs = jax.ShapeDtypeStruct((LRU_W, LRU_CW), F32)
    return pl.pallas_call(body, name="lru_bwd", grid=(LRU_W // LRU_CW,), in_specs=in_specs + [col(t, RW // LRU_CW)],
                          out_specs=[col(t), col(t), col(4), col(1), sq, col(1), sq, col(1), col(1)],
                          out_shape=[act, act, jax.ShapeDtypeStruct((4, LRU_W), F32), vec, sqs, vec, sqs, vec, vec],
                          compiler_params=_cparams(("parallel",), VMEM_BIG))(p, p, cw, cb, ga, ba, gx, bxb, lam, dyb)


def _s5_disc_fn(a_re, a_im, log_dt, b_re, b_im, e):
    lam_re = jnp.minimum(a_re, -1e-4)
    lam_im = a_im
    dt = jnp.exp(log_dt)
    mag = jnp.exp(lam_re * dt)
    ab_re = mag * jnp.cos(lam_im * dt)
    ab_im = mag * jnp.sin(lam_im * dt)
    den = lam_re * lam_re + lam_im * lam_im
    zr = ab_re - 1.0
    q_re = jnp.dot((zr * lam_re + ab_im * lam_im) / den, e, precision=_HI)
    q_im = jnp.dot((ab_im * lam_re - zr * lam_im) / den, e, precision=_HI)
    return ab_re, ab_im, q_re * b_re - q_im * b_im, q_re * b_im + q_im * b_re


def _s5_disc_fwd(a_re, a_im, log_dt, b_re, b_im, e):
    def body(*refs):
        res = _s5_disc_fn(*[x[...] for x in refs[:6]])
        for o, val in zip(refs[6:], res):
            o[...] = val

    small = jax.ShapeDtypeStruct(a_re.shape, F32)
    wide = jax.ShapeDtypeStruct(b_re.shape, F32)
    return pl.pallas_call(body, name="s5_disc_fwd", out_shape=[small, small, wide, wide])(
        a_re, a_im, log_dt, b_re, b_im, e)


def _s5_disc_bwd(a_re, a_im, log_dt, b_re, b_im, e, cts):
    def body(*refs):
        ins, e_ref, ct, outs = refs[:5], refs[5], refs[6:10], refs[10:]
        _, vjp = jax.vjp(lambda *a: _s5_disc_fn(*a, e_ref[...]), *[x[...] for x in ins])
        for o, val in zip(outs, vjp(tuple(c[...] for c in ct))):
            o[...] = val

    shapes = [jax.ShapeDtypeStruct(x.shape, F32) for x in (a_re, a_im, log_dt, b_re, b_im)]
    return pl.pallas_call(body, name="s5_disc_bwd", out_shape=shapes)(a_re, a_im, log_dt, b_re, b_im, e, *cts)


def _cmul(a, b):
    return a[0] * b[0] - a[1] * b[1], a[0] * b[1] + a[1] * b[0]


def _s5_scan(sr, si, ab, reverse):
    n_tiles = sr.shape[0] // 8
    width = sr.shape[1]
    row8 = lax.broadcasted_iota(jnp.int32, (8, width), 0)
    p1 = ab
    p2 = _cmul(p1, p1)
    p4 = _cmul(p2, p2)
    pw = [p1]
    for _ in range(7):
        pw.append(_cmul(pw[-1], p1))
    cr = jnp.zeros((8, width), F32)
    ci = jnp.zeros((8, width), F32)
    for j in range(8):
        e = pw[7 - j] if reverse else pw[j]
        cr = jnp.where(row8 == j, e[0], cr)
        ci = jnp.where(row8 == j, e[1], ci)

    def tile(i, carry):
        idx = n_tiles - 1 - i if reverse else i
        base = pl.multiple_of(idx * 8, 8)
        x = (sr[pl.ds(base, 8), :], si[pl.ds(base, 8), :])
        for d, q in ((1, p1), (2, p2), (4, p4)):
            keep = row8 < 8 - d if reverse else row8 >= d
            amt = 8 - d if reverse else d
            sh = (jnp.where(keep, pltpu.roll(x[0], amt, 0), 0.0), jnp.where(keep, pltpu.roll(x[1], amt, 0), 0.0))
            m = _cmul(q, sh)
            x = (x[0] + m[0], x[1] + m[1])
        m = _cmul((cr, ci), carry)
        x = (x[0] + m[0], x[1] + m[1])
        sr[pl.ds(base, 8), :] = x[0]
        si[pl.ds(base, 8), :] = x[1]
        edge = slice(0, 1) if reverse else slice(7, 8)
        return x[0][edge], x[1][edge]

    zero = jnp.zeros((1, width), F32)
    lax.fori_loop(0, n_tiles, tile, (zero, zero))


_S5_W = S5_SLAB // S5_GROUP * S5_STATE


def _s5_specs(t):
    col = lambda r: pl.BlockSpec((r, S5_SLAB), lambda j: (0, j))
    bb = pl.BlockSpec((None, S5_SLAB, _S5_W), lambda j: (j, 0, 0))
    cd = pl.BlockSpec((None, _S5_W, S5_SLAB), lambda j: (j, 0, 0))
    ab = pl.BlockSpec((None, 1, _S5_W), lambda j: (j, 0, 0))
    return col, bb, cd, ab


def _s5_fwd(u, dvec, bbr, bbi, cdr, cdi, abr, abi):
    t, width = u.shape
    col, bb, cd, ab = _s5_specs(t)

    def body(u_ref, d_ref, bbr_ref, bbi_ref, cdr_ref, cdi_ref, abr_ref, abi_ref, o_ref, sr, si):
        uv = u_ref[...]
        sr[...] = jnp.dot(uv, bbr_ref[...], preferred_element_type=F32)
        si[...] = jnp.dot(uv, bbi_ref[...], preferred_element_type=F32)
        _s5_scan(sr, si, (abr_ref[...], abi_ref[...]), False)
        y = jnp.dot(sr[...], cdr_ref[...], preferred_element_type=F32) - jnp.dot(si[...], cdi_ref[...],
                                                                                 preferred_element_type=F32)
        o_ref[...] = jax.nn.gelu(y + d_ref[...] * uv).astype(BF16)

    return pl.pallas_call(body, name="s5_fwd", grid=(width // S5_SLAB,),
                          in_specs=[col(t), col(1), bb, bb, cd, cd, ab, ab], out_specs=col(t),
                          out_shape=jax.ShapeDtypeStruct((t, width), BF16),
                          scratch_shapes=[pltpu.VMEM((t, _S5_W), F32)] * 2,
                          compiler_params=_cparams(("parallel",), VMEM_BIG))(u, dvec, bbr, bbi, cdr, cdi, abr, abi)


def _s5_bwd(u, dvec, bbr, bbi, cdr, cdi, abr, abi, dyact):
    t, width = u.shape
    col, bb, cd, ab = _s5_specs(t)
    ns = width // S5_SLAB
    tn = (((0,), (0,)), ((), ()))
    nt = (((1,), (1,)), ((), ()))

    def body(u_ref, d_ref, bbr_ref, bbi_ref, cdr_ref, cdi_ref, abr_ref, abi_ref, dy_ref,
             du_ref, dd_ref, dbbr_ref, dbbi_ref, dcdr_ref, dcdi_ref, dabr_ref, dabi_ref, sr, si, gr, gi):
        uv = u_ref[...]
        dv = d_ref[...]
        abv = (abr_ref[...], abi_ref[...])
        sr[...] = jnp.dot(uv, bbr_ref[...], preferred_element_type=F32)
        si[...] = jnp.dot(uv, bbi_ref[...], preferred_element_type=F32)
        _s5_scan(sr, si, abv, False)
        y = jnp.dot(sr[...], cdr_ref[...], preferred_element_type=F32) - jnp.dot(si[...], cdi_ref[...],
                                                                                 preferred_element_type=F32)
        _, vjp = jax.vjp(jax.nn.gelu, y + dv * uv)
        (dpre,) = vjp(dy_ref[...].astype(F32))
        dd_ref[...] = jnp.sum(dpre * uv, axis=0, keepdims=True)
        dcdr_ref[...] = lax.dot_general(sr[...], dpre, tn, preferred_element_type=F32)
        dcdi_ref[...] = -lax.dot_general(si[...], dpre, tn, preferred_element_type=F32)
        gr[...] = lax.dot_general(dpre, cdr_ref[...], nt, preferred_element_type=F32)
        gi[...] = -lax.dot_general(dpre, cdi_ref[...], nt, preferred_element_type=F32)
        _s5_scan(gr, gi, (abv[0], -abv[1]), True)

        row8 = lax.broadcasted_iota(jnp.int32, (8, _S5_W), 0)

        def tile(i, carry):
            acc_r, acc_i, last_r, last_i = carry
            base = pl.multiple_of(i * 8, 8)
            s_r, s_i = sr[pl.ds(base, 8), :], si[pl.ds(base, 8), :]
            g_r, g_i = gr[pl.ds(base, 8), :], gi[pl.ds(base, 8), :]
            p_r = jnp.where(row8 == 0, last_r, pltpu.roll(s_r, 1, 0))
            p_i = jnp.where(row8 == 0, last_i, pltpu.roll(s_i, 1, 0))
            acc_r = acc_r + jnp.sum(g_r * p_r + g_i * p_i, axis=0, keepdims=True)
            acc_i = acc_i + jnp.sum(g_i * p_r - g_r * p_i, axis=0, keepdims=True)
            return acc_r, acc_i, s_r[7:8], s_i[7:8]

        zero = jnp.zeros((1, _S5_W), F32)
        acc_r, acc_i, _, _ = lax.fori_loop(0, t // 8, tile, (zero, zero, zero, zero))
        dabr_ref[...] = acc_r
        dabi_ref[...] = acc_i
        du_ref[...] = (dpre * dv + lax.dot_general(gr[...], bbr_ref[...], nt, preferred_element_type=F32)
                       + lax.dot_general(gi[...], bbi_ref[...], nt, preferred_element_type=F32))
        dbbr_ref[...] = lax.dot_general(uv, gr[...], tn, preferred_element_type=F32)
        dbbi_ref[...] = lax.dot_general(uv, gi[...], tn, preferred_element_type=F32)

    sds = jax.ShapeDtypeStruct
    return pl.pallas_call(
        body, name="s5_bwd", grid=(ns,), in_specs=[col(t), col(1), bb, bb, cd, cd, ab, ab, col(t)],
        out_specs=[col(t), col(1), bb, bb, cd, cd, ab, ab],
        out_shape=[sds((t, width), F32), sds((1, width), F32), sds((ns, S5_SLAB, _S5_W), F32),
                   sds((ns, S5_SLAB, _S5_W), F32), sds((ns, _S5_W, S5_SLAB), F32), sds((ns, _S5_W, S5_SLAB), F32),
                   sds((ns, 1, _S5_W), F32), sds((ns, 1, _S5_W), F32)],
        scratch_shapes=[pltpu.VMEM((t, _S5_W), F32)] * 4,
        compiler_params=_cparams(("parallel",), VMEM_BIG))(u, dvec, bbr, bbi, cdr, cdi, abr, abi, dyact)


def _gate_dense(w):
    h = w.shape[0]
    return jnp.einsum("hij,hg->higj", w, jnp.eye(h, dtype=F32)).reshape(h * HEAD, h * HEAD)


def _gate_blocks(d):
    x = d.reshape(LRU_W // LRU_CW, 2, HEAD, 2, HEAD)
    return jnp.einsum("tgihj,gh->tgij", x, jnp.eye(2, dtype=F32)).reshape(LRU_W // HEAD, HEAD, HEAD)


_GPS = S5_SLAB // S5_GROUP
_NS = S5_GROUPS // _GPS


def _s5_in_dense(bb):
    x = bb.reshape(_NS, _GPS, S5_STATE, S5_GROUP)
    return jnp.einsum("sgnc,gh->sgchn", x, jnp.eye(_GPS, dtype=F32)).reshape(_NS, S5_SLAB, _S5_W)


def _s5_in_blocks(d):
    x = d.reshape(_NS, _GPS, S5_GROUP, _GPS, S5_STATE)
    return jnp.einsum("sgchn,gh->sgnc", x, jnp.eye(_GPS, dtype=F32)).reshape(S5_GROUPS, S5_STATE * S5_GROUP)


def _s5_out_dense(c):
    x = c.reshape(_NS, _GPS, S5_GROUP, S5_STATE)
    return jnp.einsum("sgcn,gh->shngc", x, jnp.eye(_GPS, dtype=F32)).reshape(_NS, _S5_W, S5_SLAB)


def _s5_out_blocks(d):
    x = d.reshape(_NS, _GPS, S5_STATE, _GPS, S5_GROUP)
    return jnp.einsum("shngc,gh->sgcn", x, jnp.eye(_GPS, dtype=F32)).reshape(S5_GROUPS, S5_GROUP, S5_STATE)


def _local_step(x, tgt, w):
    d_model = x.shape[1]
    gs, gb = {}, {}
    gm = jnp.kron(jnp.eye(N_HEADS, dtype=F32), jnp.ones((HEAD, HEAD), F32))
    n_layers = w["f_norm_g"].shape[0]

    def ffn_fwd(xin, l):
        xn = _rms_fwd(xin, w["f_norm_g"][l:l + 1], f"rms_f{l}")
        h = _matmul(xn, w["f_w_up_t"][l], "nt", f"mm_f{l}_up")
        act = _ffn_mid_fwd(h, w["f_conv_w"][l], w["f_conv_b"][l:l + 1], f"ffn_mid_fwd{l}")
        return _matmul(act, w["f_w_down"][l], "nn", f"mm_f{l}_down", add=xin), (xin, xn, h, act)

    def ffn_bwd(g, saved, l):
        xin, xn, h, act = saved
        dact = _matmul(g, w["f_w_down"][l], "nt", f"mm_f{l}_dact")
        d_down = _matmul(act, g, "tn", f"mm_f{l}_ddown", out_dtype=BF16)
        dhg, dhv, dwg, dwv, dbg, dbv = _ffn_mid_bwd(h, w["f_conv_w"][l], w["f_conv_b"][l:l + 1], dact,
                                                    f"ffn_mid_bwd{l}")
        dh = jnp.concatenate([dhg, dhv], axis=1)
        dxn = _matmul(dh, w["f_w_up_t"][l], "nn", f"mm_f{l}_dxn")
        d_up = _matmul(dh, xn, "tn", f"mm_f{l}_dup", out_dtype=BF16)
        dx, dgn = _rms_bwd(xin, w["f_norm_g"][l:l + 1], dxn, g, f"rms_f{l}_bwd")
        return dx, d_up, d_down, jnp.concatenate([dwg, dwv], axis=1), jnp.concatenate([dbg, dbv], axis=1), dgn

    xn0 = _rms_fwd(x, w["e_norm_g"], "rms_e")
    p = _matmul(xn0, w["e_w_in_t"], "nt", "mm_e_in")
    pam = _tshift_fwd(p, w["e_mu"])
    pw = dict(w0=w["e_w0"], w2=w["e_w2"][0], a0=w["e_a0"], a2=w["e_a2"][0], g2=w["e_g2"][0],
              k_k=w["e_k_k"], k_a=w["e_k_a"])
    r, dec, k2, v, z, b, gate = _rwkv_prep_fwd(pam, pw, gm)
    v_exp = _expand_cols(v, "wkv_expand_v")
    s_all, s_last = _wkv_fwd(dec, k2, z, b, v_exp)
    y_pt = _wkv_out(r, s_all, s_last)
    y = _from_pt(y_pt)
    rk = w["e_r_k"].reshape(1, RW)
    ya = _rwkv_post_fwd(y, r, k2, v, gate, w["e_ln_w"], w["e_ln_b"], rk, gm)
    ga, gx = _gate_dense(w["e_gate_a_w"][0]), _gate_dense(w["e_gate_x_w"][0])
    lru_w = (w["e_conv_w"][0], w["e_conv_b"], ga, w["e_gate_a_b"], gx, w["e_gate_x_b"], w["e_lru_lambda"])
    yb = _lru_fwd(p, *lru_w)
    ycat = jnp.concatenate([ya, yb], axis=1)
    x1 = _matmul(ycat, w["e_w_out"], "nn", "mm_e_out", add=x)
    x2, ffn0 = ffn_fwd(x1, 0)

    xn1 = _rms_fwd(x2, w["o_norm_g"], "rms_o")
    u = _matmul(xn1, w["o_w_in"], "nn", "mm_o_in")
    expand = jnp.kron(jnp.eye(S5_STATE, dtype=F32), jnp.ones((1, S5_GROUP), F32))
    disc_in = (w["o_A_re"][0], w["o_A_im"][0], w["o_log_dt"].reshape(S5_GROUPS, 1),
               w["o_B_re"][0].reshape(S5_GROUPS, -1), w["o_B_im"][0].reshape(S5_GROUPS, -1), expand)
    ab_re, ab_im, bb_re, bb_im = _s5_disc_fwd(*disc_in)
    s5_w = (w["o_D"], _s5_in_dense(bb_re), _s5_in_dense(bb_im), _s5_out_dense(w["o_C_re"][0]),
            _s5_out_dense(w["o_C_im"][0]), ab_re.reshape(_NS, 1, _S5_W), ab_im.reshape(_NS, 1, _S5_W))
    yact = _s5_fwd(u, *s5_w)
    zz = _matmul(yact, w["o_w_glu_t"], "nt", "mm_o_glu")
    x3 = _glu_fwd(x2, zz)
    x4, ffn1 = ffn_fwd(x3, 1)

    loss, g, gs["final_norm_g"] = _loss_head(x4, w["final_norm_g"].reshape(1, d_model), tgt)
    gs["final_norm_g"] = gs["final_norm_g"].reshape(d_model)

    g, up1, down1, dcw1, dcb1, dfn1 = ffn_bwd(g, ffn1, 1)
    dz = _glu_bwd(zz, g)
    dyact = _matmul(dz, w["o_w_glu_t"], "nn", "mm_o_dyact")
    gb["o_w_glu_t"] = _matmul(dz, yact, "tn", "mm_o_dglu", out_dtype=BF16)
    du, gs["o_D"], dbbr, dbbi, dcdr, dcdi, dabr, dabi = _s5_bwd(u, *s5_w, dyact)
    gs["o_C_re"] = _s5_out_blocks(dcdr)[None]
    gs["o_C_im"] = _s5_out_blocks(dcdi)[None]
    cts = (dabr.reshape(S5_GROUPS, S5_STATE), dabi.reshape(S5_GROUPS, S5_STATE), _s5_in_blocks(dbbr),
           _s5_in_blocks(dbbi))
    da_re, da_im, dlog_dt, db_re, db_im = _s5_disc_bwd(*disc_in, cts)
    gs["o_A_re"], gs["o_A_im"], gs["o_log_dt"] = da_re[None], da_im[None], dlog_dt.reshape(1, S5_GROUPS)
    gs["o_B_re"] = db_re.reshape(w["o_B_re"].shape)
    gs["o_B_im"] = db_im.reshape(w["o_B_im"].shape)
    dxn = _matmul(du, w["o_w_in"], "nt", "mm_o_dxn")
    gb["o_w_in"] = _matmul(xn1, du, "tn", "mm_o_din", out_dtype=BF16)
    g, gs["o_norm_g"] = _rms_bwd(x2, w["o_norm_g"], dxn, g, "rms_o_bwd")

    g, up0, down0, dcw0, dcb0, dfn0 = ffn_bwd(g, ffn0, 0)
    gb["f_w_up_t"] = [up0, up1]
    gb["f_w_down"] = [down0, down1]
    gs["f_conv_w"] = jnp.stack([dcw0, dcw1])
    gs["f_conv_b"] = jnp.concatenate([dcb0, dcb1], axis=0)
    gs["f_norm_g"] = jnp.concatenate([dfn0, dfn1], axis=0)

    dycat = _matmul(g, w["e_w_out"], "nt", "mm_e_dycat")
    gb["e_w_out"] = _matmul(ycat, g, "tn", "mm_e_dout", out_dtype=BF16)
    dy, dr1, dk1, dv1, dgate, gs["e_ln_w"], gs["e_ln_b"], drk = _rwkv_post_bwd(
        y, r, k2, v, gate, w["e_ln_w"], w["e_ln_b"], rk, gm, dycat)
    gs["e_r_k"] = drk.reshape(w["e_r_k"].shape)
    dr2, ddec, dk2, dzz, dbb, dv_pt = _wkv_bwd(r, dec, k2, z, b, v_exp, s_all, _expand_cols(dy, "wkv_expand_dy"))
    dpam, gs["e_w0"], dw2, gs["e_a0"], da2, dg2, gs["e_k_k"], gs["e_k_a"] = _rwkv_prep_bwd(
        pam, pw, gm, (dr2, ddec, dk2, _from_pt(dv_pt), dzz, dbb, dgate), (dr1, dk1, dv1))
    gs["e_w2"], gs["e_a2"], gs["e_g2"] = dw2[None], da2[None], dg2[None]
    dpa, gs["e_mu"] = _tshift_bwd(p, w["e_mu"], dpam)
    dbx, dbg, dcw, gs["e_conv_b"], dga, gs["e_gate_a_b"], dgx, gs["e_gate_x_b"], gs["e_lru_lambda"] = _lru_bwd(
        p, *lru_w, dycat)
    gs["e_conv_w"] = dcw[None]
    gs["e_gate_a_w"] = _gate_blocks(dga)[None]
    gs["e_gate_x_w"] = _gate_blocks(dgx)[None]
    dp = jnp.concatenate([dpa, dbx, dbg], axis=1)
    dxn = _matmul(dp, w["e_w_in_t"], "nn", "mm_e_dxn")
    gb["e_w_in_t"] = _matmul(dp, xn0, "tn", "mm_e_din", out_dtype=BF16)
    grad_x, gs["e_norm_g"] = _rms_bwd(x, w["e_norm_g"], dxn, g, "rms_e_bwd")
    return loss, grad_x, gb, gs


CAST_ROWS = 256


def _cast_shard(w3, layer, transpose, name):
    _, rows, cols = w3.shape
    tr = _tile(rows, (CAST_ROWS, 176, 128))

    def body(w_ref, o_ref):
        v = w_ref[...]
        o_ref[...] = (v.T if transpose else v).astype(BF16)

    in_spec = pl.BlockSpec((None, tr, cols), lambda i: (layer, i, 0))
    if transpose:
        out_spec, shape = pl.BlockSpec((cols, tr), lambda i: (0, i)), (cols, rows)
    else:
        out_spec, shape = pl.BlockSpec((tr, cols), lambda i: (i, 0)), (rows, cols)
    return pl.pallas_call(body, name=name, grid=(rows // tr,), in_specs=[in_spec], out_specs=out_spec,
                          out_shape=jax.ShapeDtypeStruct(shape, BF16),
                          compiler_params=_cparams(("parallel",), VMEM_MID))(w3)


_ANY = pl.BlockSpec(memory_space=pl.ANY)


def _coords():
    return lax.axis_index("x"), lax.axis_index("y"), lax.axis_index("c")


def _flip(v, d):
    return 1 - v if d else v


_CHIP_RELS = ((1, 0), (0, 1), (1, 1))
_DEV_RELS = tuple((dx, dy, dc) for dx in (0, 1) for dy in (0, 1) for dc in (0, 1))[1:]


def _gather_chips(arrs):
    n = len(arrs)
    nr = len(_CHIP_RELS)

    def body(*refs):
        ins, outs, (send, recv, loc) = refs[:n], refs[n:2 * n], refs[2 * n:]
        x, y, c = _coords()
        me = 2 * x + y
        locals_, sends, recvs = [], [], []
        for i in range(n):
            cp = pltpu.make_async_copy(ins[i], outs[i].at[me], loc.at[i])
            cp.start()
            locals_.append(cp)
            for j, (dx, dy) in enumerate(_CHIP_RELS):
                px, py = _flip(x, dx), _flip(y, dy)
                k = i * nr + j
                cp = pltpu.make_async_remote_copy(src_ref=ins[i], dst_ref=outs[i].at[me], send_sem=send.at[k],
                                                  recv_sem=recv.at[k], device_id=(px, py, c), device_id_type=MESH)
                cp.start()
                sends.append(cp)
                recvs.append(pltpu.make_async_remote_copy(
                    src_ref=ins[i], dst_ref=outs[i].at[2 * px + py], send_sem=send.at[k], recv_sem=recv.at[k],
                    device_id=(px, py, c), device_id_type=MESH))
        for cp in recvs:
            cp.wait_recv()
        for cp in sends:
            cp.wait_send()
        for cp in locals_:
            cp.wait()

    return pl.pallas_call(
        body, name="gather_chips", in_specs=[_ANY] * n, out_specs=[_ANY] * n,
        out_shape=[jax.ShapeDtypeStruct((4,) + a.shape, a.dtype) for a in arrs],
        scratch_shapes=[pltpu.SemaphoreType.DMA((n * nr,)), pltpu.SemaphoreType.DMA((n * nr,)),
                        pltpu.SemaphoreType.DMA((n,))])(*arrs)


def _scatter_devices(arrs):
    n = len(arrs)
    nr = len(_DEV_RELS)

    def body(*refs):
        ins, outs, (send, recv, loc) = refs[:n], refs[n:2 * n], refs[2 * n:]
        x, y, c = _coords()
        me = 4 * x + 2 * y + c
        locals_, sends, recvs = [], [], []
        for i in range(n):
            cp = pltpu.make_async_copy(ins[i].at[me], outs[i].at[me], loc.at[i])
            cp.start()
            locals_.append(cp)
            for j, (dx, dy, dc) in enumerate(_DEV_RELS):
                peer = (_flip(x, dx), _flip(y, dy), _flip(c, dc))
                pid = 4 * peer[0] + 2 * peer[1] + peer[2]
                k = i * nr + j
                cp = pltpu.make_async_remote_copy(src_ref=ins[i].at[pid], dst_ref=outs[i].at[me], send_sem=send.at[k],
                                                  recv_sem=recv.at[k], device_id=peer, device_id_type=MESH)
                cp.start()
                sends.append(cp)
                recvs.append(pltpu.make_async_remote_copy(
                    src_ref=ins[i].at[pid], dst_ref=outs[i].at[pid], send_sem=send.at[k], recv_sem=recv.at[k],
                    device_id=peer, device_id_type=MESH))
        for cp in recvs:
            cp.wait_recv()
        for cp in sends:
            cp.wait_send()
        for cp in locals_:
            cp.wait()

    return pl.pallas_call(
        body, name="scatter_devices", in_specs=[_ANY] * n, out_specs=[_ANY] * n,
        out_shape=[jax.ShapeDtypeStruct(a.shape, a.dtype) for a in arrs],
        scratch_shapes=[pltpu.SemaphoreType.DMA((n * nr,)), pltpu.SemaphoreType.DMA((n * nr,)),
                        pltpu.SemaphoreType.DMA((n,))])(*arrs)


def _sum_segments(stage, core, name):
    nd, seg, cols = stage.shape
    ts = _tile(seg, (256, 176, 128))

    def body(c_ref, s_ref, o_ref):
        acc = s_ref[0].astype(F32)
        for d in range(1, nd):
            acc = acc + s_ref[d].astype(F32)
        o_ref[...] = acc

    grid_spec = pltpu.PrefetchScalarGridSpec(
        num_scalar_prefetch=1, grid=(seg // ts,),
        in_specs=[pl.BlockSpec((nd, ts, cols), lambda i, c: (0, i, 0))],
        out_specs=pl.BlockSpec((None, ts, cols), lambda i, c: (c[0], i, 0)))
    return pl.pallas_call(body, name=name, grid_spec=grid_spec,
                          out_shape=jax.ShapeDtypeStruct((2, seg, cols), F32),
                          compiler_params=_cparams(("parallel",), VMEM_MID))(core, stage)


def _exchange_sibling(arrs):
    n = len(arrs)

    def body(*refs):
        outs, (send, recv) = refs[n:2 * n], refs[2 * n:]
        x, y, c = _coords()
        sib = (x, y, 1 - c)
        sends, recvs = [], []
        for i in range(n):
            cp = pltpu.make_async_remote_copy(src_ref=outs[i].at[c], dst_ref=outs[i].at[c], send_sem=send.at[i],
                                              recv_sem=recv.at[i], device_id=sib, device_id_type=MESH)
            cp.start()
            sends.append(cp)
            recvs.append(pltpu.make_async_remote_copy(src_ref=outs[i].at[c], dst_ref=outs[i].at[1 - c],
                                                      send_sem=send.at[i], recv_sem=recv.at[i], device_id=sib,
                                                      device_id_type=MESH))
        for cp in recvs:
            cp.wait_recv()
        for cp in sends:
            cp.wait_send()

    return pl.pallas_call(
        body, name="exchange_sibling", in_specs=[_ANY] * n, out_specs=[_ANY] * n,
        out_shape=[jax.ShapeDtypeStruct(a.shape, a.dtype) for a in arrs],
        input_output_aliases={i: i for i in range(n)},
        scratch_shapes=[pltpu.SemaphoreType.DMA((n,)), pltpu.SemaphoreType.DMA((n,))])(*arrs)


def _allreduce_small(vec):
    nd, rows, lanes = vec.shape
    nr = len(_DEV_RELS)

    def body(in_ref, out_ref, stage, red, send, recv):
        x, y, c = _coords()
        me = 4 * x + 2 * y + c
        peers = []
        for dx, dy, dc in _DEV_RELS:
            peer = (_flip(x, dx), _flip(y, dy), _flip(c, dc))
            peers.append((peer, 4 * peer[0] + 2 * peer[1] + peer[2]))

        def copy(src, dst, k, peer):
            return pltpu.make_async_remote_copy(src_ref=src, dst_ref=dst, send_sem=send.at[k], recv_sem=recv.at[k],
                                                device_id=peer, device_id_type=MESH)

        first = [copy(in_ref.at[pid], stage.at[me], j, peer) for j, (peer, pid) in enumerate(peers)]
        for cp in first:
            cp.start()
        stage[me] = in_ref[me]
        for j, (peer, pid) in enumerate(peers):
            copy(in_ref.at[pid], stage.at[pid], j, peer).wait_recv()
        acc = stage[0]
        for d in range(1, nd):
            acc = acc + stage[d]
        red[...] = acc
        out_ref[me] = acc
        second = [copy(red, out_ref.at[me], nr + j, peer) for j, (peer, pid) in enumerate(peers)]
        for cp in second:
            cp.start()
        for j, (peer, pid) in enumerate(peers):
            copy(red, out_ref.at[pid], nr + j, peer).wait_recv()
        for cp in first + second:
            cp.wait_send()

    vm = pl.BlockSpec(memory_space=pltpu.VMEM)
    return pl.pallas_call(
        body, name="allreduce_small", in_specs=[vm], out_specs=vm,
        out_shape=jax.ShapeDtypeStruct(vec.shape, F32),
        scratch_shapes=[pltpu.VMEM(vec.shape, F32), pltpu.VMEM((rows, lanes), F32),
                        pltpu.SemaphoreType.DMA((2 * nr,)), pltpu.SemaphoreType.DMA((2 * nr,))],
        compiler_params=_cparams(None, VMEM_MID))(vec)


def _adam_math(w, g, m, v):
    m2 = ADAM_B1 * m + (1.0 - ADAM_B1) * g
    v2 = ADAM_B2 * v + (1.0 - ADAM_B2) * (g * g)
    m_hat = m2 / (1.0 - ADAM_B1 ** ADAM_STEP)
    v_hat = v2 / (1.0 - ADAM_B2 ** ADAM_STEP)
    return -ADAM_LR * (m_hat / (jnp.sqrt(v_hat) + ADAM_EPS) + ADAM_WD * w), m2, v2


def _adamw_big(w3, m3, v3, layer, g, transposed, name, prev=None):
    nl, rows, cols = w3.shape
    tr = 128 if transposed else _tile(rows, (256, 176, 128))

    def body(w_ref, m_ref, v_ref, g_ref, *rest):
        go_ref, d_ref, mo_ref, vo_ref = rest[-4:]
        g_val = g_ref[...].T if transposed else g_ref[...]
        go_ref[...] = g_val
        d_ref[...], mo_ref[...], vo_ref[...] = _adam_math(w_ref[...], g_val, m_ref[...], v_ref[...])

    wspec = pl.BlockSpec((None, tr, cols), lambda i: (layer, i, 0))
    gspec = pl.BlockSpec((cols, tr), lambda i: (0, i)) if transposed else pl.BlockSpec((tr, cols), lambda i: (i, 0))
    extra = [] if prev is None else list(prev)
    return pl.pallas_call(body, name=name, grid=(rows // tr,),
                          in_specs=[wspec, wspec, wspec, gspec] + [_ANY] * len(extra),
                          out_specs=[wspec] * 4, out_shape=[jax.ShapeDtypeStruct((nl, rows, cols), F32)] * 4,
                          input_output_aliases={4 + i: i for i in range(len(extra))},
                          compiler_params=_cparams(("parallel",), VMEM_MID))(w3, m3, v3, g, *extra)


def _adamw_small(w, g, m, v):
    rows = w.shape[0]
    tr = _tile(rows, (512, 256, 128, 64, 32, 16, 8))

    def body(w_ref, g_ref, m_ref, v_ref, d_ref, mo_ref, vo_ref):
        d_ref[...], mo_ref[...], vo_ref[...] = _adam_math(w_ref[...], g_ref[...], m_ref[...], v_ref[...])

    spec = pl.BlockSpec((tr, LANES), lambda i: (i, 0))
    return pl.pallas_call(body, name="adamw_small", grid=(rows // tr,), in_specs=[spec] * 4, out_specs=[spec] * 3,
                          out_shape=[jax.ShapeDtypeStruct(w.shape, F32)] * 3,
                          compiler_params=_cparams(("parallel",)))(w, g, m, v)


def _pack(arrs, row_mult):
    flat = jnp.concatenate([a.reshape(-1).astype(F32) for a in arrs])
    rows = -(-flat.shape[0] // LANES)
    rows = -(-rows // row_mult) * row_mult
    return jnp.pad(flat, (0, rows * LANES - flat.shape[0])).reshape(rows, LANES)


def _unpack(packed, shapes):
    flat = packed.reshape(-1)
    out, off = [], 0
    for s in shapes:
        size = 1
        for d in s:
            size *= d
        out.append(flat[off:off + size].reshape(s))
        off += size
    return out


_SMALL_REP = ("e_norm_g", "e_mu", "e_w0", "e_a0", "e_k_k", "e_k_a", "e_r_k", "e_ln_w", "e_ln_b", "e_conv_b",
              "e_gate_a_w", "e_gate_a_b", "e_gate_x_w", "e_gate_x_b", "e_lru_lambda", "o_A_re", "o_A_im", "o_log_dt",
              "o_B_re", "o_B_im", "o_C_re", "o_C_im", "f_norm_g", "f_conv_b", "final_norm_g")
_SMALL_SH = ("e_w2", "e_a2", "e_g2", "e_conv_w", "o_norm_g", "o_D", "f_conv_w")
_LARGE = (("e_w_in", True), ("e_w_out", False), ("o_w_in", False), ("o_w_glu", True), ("f_w_up", True),
        ("f_w_down", False))
_ORDER = ("e_norm_g", "e_w_in", "e_mu", "e_w0", "e_w2", "e_a0", "e_a2", "e_g2", "e_k_k", "e_k_a", "e_r_k", "e_ln_w",
          "e_ln_b", "e_conv_w", "e_conv_b", "e_gate_a_w", "e_gate_a_b", "e_gate_x_w", "e_gate_x_b", "e_lru_lambda",
          "e_w_out", "o_norm_g", "o_w_in", "o_A_re", "o_A_im", "o_log_dt", "o_B_re", "o_B_im", "o_C_re", "o_C_im",
          "o_D", "o_w_glu", "f_norm_g", "f_w_up", "f_conv_w", "f_conv_b", "f_w_down", "final_norm_g")
N_CHIPS = 4
N_DEV = 8


def _step(x, tgt, wts, ms, vs):
    xi, yi, _ = _coords()
    chip = 2 * xi + yi

    send, keys = [], []
    for name, by_cols in _LARGE:
        for l in range(wts[name].shape[0]):
            send.append(_cast_shard(wts[name], l, by_cols, f"cast_{name}{l}"))
            keys.append((name, l))
    sh_shapes = [wts[n].shape for n in _SMALL_SH]
    send.append(_pack([wts[n] for n in _SMALL_SH], 8))
    got = _gather_chips(send)
    full = {n: wts[n] for n in _SMALL_REP}
    for (name, l), g in zip(keys, got[:-1]):
        key = name + "_t" if dict(_LARGE)[name] else name
        full.setdefault(key, []).append(g.reshape(N_CHIPS * g.shape[1], g.shape[2]))
    for key in ("e_w_in_t", "e_w_out", "o_w_in", "o_w_glu_t"):
        full[key] = full[key][0]
    per_chip = [_unpack(got[-1][k], sh_shapes) for k in range(N_CHIPS)]
    for i, n in enumerate(_SMALL_SH):
        full[n] = jnp.concatenate([per_chip[k][i] for k in range(N_CHIPS)], axis=-1)

    loss, grad_x, gb, gs = _local_step(x, tgt, full)

    parts, keys = [], []
    for name, by_cols in _LARGE:
        g = gb[name + "_t" if by_cols else name]
        for l, gl in enumerate(g if isinstance(g, list) else [g]):
            parts.append(gl.reshape(N_DEV, gl.shape[0] // N_DEV, gl.shape[1]))
            keys.append((name, l, by_cols))
    staged = _scatter_devices(parts)
    core = lax.axis_index("c").astype(jnp.int32).reshape(1)
    halves = [_sum_segments(s, core, f"sum_{name}{l}") for s, (name, l, _) in zip(staged, keys)]
    shards = _exchange_sibling(halves)
    final = {}
    for s, (name, l, by_cols) in zip(shards, keys):
        final[name] = _adamw_big(wts[name], ms[name], vs[name], l, s.reshape(2 * s.shape[1], s.shape[2]), by_cols,
                                 f"adamw_{name}{l}", prev=final.get(name))

    small = _SMALL_REP + _SMALL_SH
    shapes = [gs[n].shape for n in small]
    red = _allreduce_small(_pack([gs[n] for n in small], 8 * N_DEV).reshape(N_DEV, -1, LANES))
    tot = dict(zip(small, _unpack(red, shapes)))
    for n in _SMALL_SH:
        width = wts[n].shape[-1]
        tot[n] = lax.dynamic_slice_in_dim(tot[n], chip * width, width, axis=tot[n].ndim - 1)
    loc_shapes = [wts[n].shape for n in small]
    pk = lambda d: _pack([d[n] for n in small], 8)
    delta, new_m, new_v = _adamw_small(pk(wts), pk(tot), pk(ms), pk(vs))
    for n, g, d, m2, v2 in zip(small, [tot[n] for n in small], _unpack(delta, loc_shapes), _unpack(new_m, loc_shapes),
                               _unpack(new_v, loc_shapes)):
        final[n] = [g.reshape(wts[n].shape), d, m2, v2]

    loss = lax.psum(loss[0, 0], ("x", "y", "c"))
    res = [loss, grad_x[None]]
    for k in range(4):
        res += [final[n][k] for n in _ORDER]
    return tuple(res)


def kernel(x, e_norm_g, e_w_in, e_mu, e_w0, e_w2, e_a0, e_a2, e_g2, e_k_k, e_k_a, e_r_k, e_ln_w, e_ln_b, e_conv_w, e_conv_b, e_gate_a_w, e_gate_a_b, e_gate_x_w, e_gate_x_b, e_lru_lambda, e_w_out, o_norm_g, o_w_in, o_A_re, o_A_im, o_log_dt, o_B_re, o_B_im, o_C_re, o_C_im, o_D, o_w_glu, f_norm_g, f_w_up, f_conv_w, f_conv_b, f_w_down, final_norm_g, loss_target, m_e_norm_g, m_e_w_in, m_e_mu, m_e_w0, m_e_w2, m_e_a0, m_e_a2, m_e_g2, m_e_k_k, m_e_k_a, m_e_r_k, m_e_ln_w, m_e_ln_b, m_e_conv_w, m_e_conv_b, m_e_gate_a_w, m_e_gate_a_b, m_e_gate_x_w, m_e_gate_x_b, m_e_lru_lambda, m_e_w_out, m_o_norm_g, m_o_w_in, m_o_A_re, m_o_A_im, m_o_log_dt, m_o_B_re, m_o_B_im, m_o_C_re, m_o_C_im, m_o_D, m_o_w_glu, m_f_norm_g, m_f_w_up, m_f_conv_w, m_f_conv_b, m_f_w_down, m_final_norm_g, v_e_norm_g, v_e_w_in, v_e_mu, v_e_w0, v_e_w2, v_e_a0, v_e_a2, v_e_g2, v_e_k_k, v_e_k_a, v_e_r_k, v_e_ln_w, v_e_ln_b, v_e_conv_w, v_e_conv_b, v_e_gate_a_w, v_e_gate_a_b, v_e_gate_x_w, v_e_gate_x_b, v_e_lru_lambda, v_e_w_out, v_o_norm_g, v_o_w_in, v_o_A_re, v_o_A_im, v_o_log_dt, v_o_B_re, v_o_B_im, v_o_C_re, v_o_C_im, v_o_D, v_o_w_glu, v_f_norm_g, v_f_w_up, v_f_conv_w, v_f_conv_b, v_f_w_down, v_final_norm_g):
    args = locals()
    wts = {n: args[n] for n in _ORDER}
    ms = {n: args["m_" + n] for n in _ORDER}
    vs = {n: args["v_" + n] for n in _ORDER}
    return _step(x[0], loss_target[0], wts, ms, vs)
```

```python
import functools

import jax
import jax.numpy as jnp
from jax import lax
from jax.experimental import pallas as pl
from jax.experimental.pallas import tpu as pltpu

F32 = jnp.float32
BF16 = jnp.bfloat16
MESH = pl.DeviceIdType.MESH

HEAD = 64
RW = 512
N_HEADS = RW // HEAD
LRU_W = 512
SHIFT_COLS = 1792
W_LORA, A_LORA, G_LORA = 64, 64, 128
S5_GROUPS, S5_GROUP, S5_STATE = 64, 16, 64
D_FF = 2816
NORM_EPS = 1e-6
GN_EPS = 64e-5
LRU_C = 8.0
ADAM_LR, ADAM_B1, ADAM_B2, ADAM_EPS, ADAM_WD, ADAM_STEP = 0.001, 0.9, 0.999, 1e-08, 0.01, 10

VMEM_BIG = 56 * 1024 * 1024
VMEM_MID = 40 * 1024 * 1024
LANES = 128
PT = 16
WKV_CHUNK = 32
S5_SLAB = 128


def _cparams(sem=None, vmem=None):
    kw = {}
    if sem is not None:
        kw["dimension_semantics"] = sem
    if vmem is not None:
        kw["vmem_limit_bytes"] = vmem
    return pltpu.CompilerParams(**kw)


def _tile(dim, cands):
    for c in cands:
        if dim % c == 0:
            return c
    return dim


def _full(shape):
    n = len(shape)
    return pl.BlockSpec(shape, lambda *_: (0,) * n)


_TILES = (1408, 1024, 512, 256, 128)


def _matmul(a, b, mode, name, out_dtype=F32, add=None):
    if mode == "nn":
        (m, k), (k2, n) = a.shape, b.shape
    elif mode == "nt":
        (m, k), (n, k2) = a.shape, b.shape
    else:
        (k, m), (k2, n) = a.shape, b.shape
    assert k == k2, (a.shape, b.shape, mode)
    if mode == "tn":
        tm, tn, tk = _tile(m, _TILES), _tile(n, (1024, 512, 256, 128)), _tile(k, (512, 256, 128))
    else:
        tm, tn, tk = _tile(m, (512, 256, 128)), _tile(n, _TILES), _tile(k, _TILES)
    nk = k // tk
    dims = {"nn": (((1,), (0,)), ((), ())), "nt": (((1,), (1,)), ((), ())), "tn": (((0,), (0,)), ((), ()))}[mode]

    def body(*refs):
        if add is None:
            a_ref, b_ref, o_ref, acc = refs
            add_ref = None
        else:
            a_ref, b_ref, add_ref, o_ref, acc = refs
        kk = pl.program_id(2)

        @pl.when(kk == 0)
        def _():
            acc[...] = jnp.zeros_like(acc)

        acc[...] += lax.dot_general(a_ref[...].astype(BF16), b_ref[...].astype(BF16), dims,
                                    preferred_element_type=F32)

        @pl.when(kk == nk - 1)
        def _():
            r = acc[...]
            if add_ref is not None:
                r = r + add_ref[...]
            o_ref[...] = r.astype(o_ref.dtype)

    if mode == "nn":
        a_spec = pl.BlockSpec((tm, tk), lambda i, j, kk: (i, kk))
        b_spec = pl.BlockSpec((tk, tn), lambda i, j, kk: (kk, j))
    elif mode == "nt":
        a_spec = pl.BlockSpec((tm, tk), lambda i, j, kk: (i, kk))
        b_spec = pl.BlockSpec((tn, tk), lambda i, j, kk: (j, kk))
    else:
        a_spec = pl.BlockSpec((tk, tm), lambda i, j, kk: (kk, i))
        b_spec = pl.BlockSpec((tk, tn), lambda i, j, kk: (kk, j))
    o_spec = pl.BlockSpec((tm, tn), lambda i, j, kk: (i, j))
    in_specs = [a_spec, b_spec] + ([o_spec] if add is not None else [])
    args = (a, b) + ((add,) if add is not None else ())
    return pl.pallas_call(
        body, name=name, grid=(m // tm, n // tn, nk),
        in_specs=in_specs, out_specs=o_spec,
        out_shape=jax.ShapeDtypeStruct((m, n), out_dtype),
        scratch_shapes=[pltpu.VMEM((tm, tn), F32)],
        compiler_params=_cparams(("parallel", "parallel", "arbitrary"), VMEM_MID),
    )(*args)


TOK = 256


def _rms(x, g):
    return x * lax.rsqrt(jnp.mean(x * x, axis=-1, keepdims=True) + NORM_EPS) * g


def _rms_fwd(x, g, name):
    t, d = x.shape

    def body(x_ref, g_ref, o_ref):
        o_ref[...] = _rms(x_ref[...], g_ref[...]).astype(BF16)

    row = pl.BlockSpec((TOK, d), lambda i: (i, 0))
    return pl.pallas_call(body, name=name, grid=(t // TOK,), in_specs=[row, _full((1, d))], out_specs=row,
                          out_shape=jax.ShapeDtypeStruct((t, d), BF16),
                          compiler_params=_cparams(("parallel",)))(x, g)


def _rms_bwd(x, g, dxn, res, name):
    t, d = x.shape

    def body(x_ref, g_ref, d_ref, res_ref, dx_ref, dg_ref):
        _, vjp = jax.vjp(_rms, x_ref[...], g_ref[...])
        dx, dg = vjp(d_ref[...].astype(F32))
        dx_ref[...] = dx + res_ref[...]

        @pl.when(pl.program_id(0) == 0)
        def _():
            dg_ref[...] = jnp.zeros_like(dg_ref)

        dg_ref[...] += dg

    row = pl.BlockSpec((TOK, d), lambda i: (i, 0))
    return pl.pallas_call(body, name=name, grid=(t // TOK,), in_specs=[row, _full((1, d)), row, row],
                          out_specs=[row, _full((1, d))],
                          out_shape=[jax.ShapeDtypeStruct((t, d), F32), jax.ShapeDtypeStruct((1, d), F32)],
                          compiler_params=_cparams(("arbitrary",)))(x, g, dxn, res)


def _loss_head(x, g, tgt):
    t, d = x.shape

    def body(x_ref, g_ref, t_ref, l_ref, dx_ref, dg_ref):
        tg = t_ref[...]

        def fn(xv, gv):
            err = _rms(xv, gv) - tg
            per_tok = jnp.mean(err * err, axis=-1, keepdims=True)
            return 0.5 * jnp.sum(per_tok, axis=0, keepdims=True)

        l, vjp = jax.vjp(fn, x_ref[...], g_ref[...])
        dx, dg = vjp(jnp.ones((1, 1), F32))
        dx_ref[...] = dx

        @pl.when(pl.program_id(0) == 0)
        def _():
            dg_ref[...] = jnp.zeros_like(dg_ref)
            l_ref[...] = jnp.zeros_like(l_ref)

        dg_ref[...] += dg
        l_ref[...] += jnp.broadcast_to(l, l_ref.shape)

    row = pl.BlockSpec((TOK, d), lambda i: (i, 0))
    return pl.pallas_call(body, name="loss_head", grid=(t // TOK,), in_specs=[row, _full((1, d)), row],
                          out_specs=[_full((1, LANES)), row, _full((1, d))],
                          out_shape=[jax.ShapeDtypeStruct((1, LANES), F32), jax.ShapeDtypeStruct((t, d), F32),
                                     jax.ShapeDtypeStruct((1, d), F32)],
                          compiler_params=_cparams(("arbitrary",)))(x, g, tgt)


def _glu_fwd(x, z):
    t, d = x.shape

    def body(x_ref, v_ref, g_ref, o_ref):
        o_ref[...] = x_ref[...] + v_ref[...] * jax.nn.sigmoid(g_ref[...])

    row = pl.BlockSpec((TOK, d), lambda i: (i, 0))
    gate = pl.BlockSpec((TOK, d), lambda i: (i, 1))
    return pl.pallas_call(body, name="glu_fwd", grid=(t // TOK,), in_specs=[row, row, gate], out_specs=row,
                          out_shape=jax.ShapeDtypeStruct((t, d), F32),
                          compiler_params=_cparams(("parallel",)))(x, z, z)


def _glu_bwd(z, g):
    t, d = g.shape

    def body(v_ref, g_ref, d_ref, o_ref):
        s = jax.nn.sigmoid(g_ref[...])
        dy = d_ref[...]
        o_ref[:, :d] = (dy * s).astype(BF16)
        o_ref[:, d:] = (dy * v_ref[...] * s * (1.0 - s)).astype(BF16)

    row = pl.BlockSpec((TOK, d), lambda i: (i, 0))
    gate = pl.BlockSpec((TOK, d), lambda i: (i, 1))
    return pl.pallas_call(body, name="glu_bwd", grid=(t // TOK,), in_specs=[row, gate, row],
                          out_specs=pl.BlockSpec((TOK, 2 * d), lambda i: (i, 0)),
                          out_shape=jax.ShapeDtypeStruct((t, 2 * d), BF16),
                          compiler_params=_cparams(("parallel",)))(z, z, g)


def _shift_down(x, d):
    row = lax.broadcasted_iota(jnp.int32, x.shape, 0)
    return jnp.where(row < d, 0.0, pltpu.roll(x, d, 0))


def _shift_up(x, d):
    n = x.shape[0]
    row = lax.broadcasted_iota(jnp.int32, x.shape, 0)
    return jnp.where(row >= n - d, 0.0, pltpu.roll(x, n - d, 0))


def _make_sd():
    @functools.partial(jax.custom_vjp, nondiff_argnums=(1,))
    def sd(x, d):
        return _shift_down(x, d)

    def fwd(x, d):
        return _shift_down(x, d), None

    def bwd(d, _, g):
        return (_shift_up(g, d),)

    sd.defvjp(fwd, bwd)
    return sd


def _lin_scan(a, u, reverse=False):
    n = a.shape[0]
    row = lax.broadcasted_iota(jnp.int32, a.shape, 0)
    d = 1
    while d < n:
        if reverse:
            keep = row < n - d
            a_s, u_s = pltpu.roll(a, n - d, 0), pltpu.roll(u, n - d, 0)
        else:
            keep = row >= d
            a_s, u_s = pltpu.roll(a, d, 0), pltpu.roll(u, d, 0)
        u = u + a * jnp.where(keep, u_s, 0.0)
        a = a * jnp.where(keep, a_s, 1.0)
        d *= 2
    return u


def _make_scan():
    @jax.custom_vjp
    def scan(a, u):
        return _lin_scan(a, u)

    def fwd(a, u):
        h = _lin_scan(a, u)
        return h, (a, h)

    def bwd(res, dh):
        a, h = res
        g = _lin_scan(_shift_up(a, 1), dh, reverse=True)
        return g * _shift_down(h, 1), g

    scan.defvjp(fwd, bwd)
    return scan


def _acc_out(ref, val):
    @pl.when(pl.program_id(0) == 0)
    def _():
        ref[...] = jnp.zeros_like(ref)

    ref[...] += val


FFN_CW = 128


def _ffn_fn(hg, hv, wg, wv, bg, bv, sd):
    cg = wg[0:1] * sd(hg, 2) + wg[1:2] * sd(hg, 1) + wg[2:3] * hg + bg
    cv = wv[0:1] * sd(hv, 2) + wv[1:2] * sd(hv, 1) + wv[2:3] * hv + bv
    return jax.nn.silu(cg) * cv


def _ffn_specs(t):
    nb = D_FF // FFN_CW
    col = lambda r, off: pl.BlockSpec((r, FFN_CW), lambda j: (0, j + off))
    return nb, [col(t, 0), col(t, nb), col(3, 0), col(3, nb), col(1, 0), col(1, nb)], col


def _ffn_mid_fwd(h, cw, cb, name):
    t = h.shape[0]
    nb, in_specs, col = _ffn_specs(t)

    def body(hg, hv, wg, wv, bg, bv, o_ref):
        o_ref[...] = _ffn_fn(hg[...], hv[...], wg[...], wv[...], bg[...], bv[...], _shift_down).astype(BF16)

    return pl.pallas_call(body, name=name, grid=(nb,), in_specs=in_specs, out_specs=col(t, 0),
                          out_shape=jax.ShapeDtypeStruct((t, D_FF), BF16),
                          compiler_params=_cparams(("parallel",), VMEM_MID))(h, h, cw, cw, cb, cb)


def _ffn_mid_bwd(h, cw, cb, dact, name):
    t = h.shape[0]
    nb, in_specs, col = _ffn_specs(t)

    def body(hg, hv, wg, wv, bg, bv, d_ref, dhg, dhv, dwg, dwv, dbg, dbv):
        fn = functools.partial(_ffn_fn, sd=_make_sd())
        _, vjp = jax.vjp(fn, hg[...], hv[...], wg[...], wv[...], bg[...], bv[...])
        g = vjp(d_ref[...])
        dhg[...] = g[0].astype(BF16)
        dhv[...] = g[1].astype(BF16)
        dwg[...], dwv[...], dbg[...], dbv[...] = g[2], g[3], g[4], g[5]

    big = jax.ShapeDtypeStruct((t, D_FF), BF16)
    w3 = jax.ShapeDtypeStruct((3, D_FF), F32)
    b1 = jax.ShapeDtypeStruct((1, D_FF), F32)
    return pl.pallas_call(body, name=name, grid=(nb,), in_specs=in_specs + [col(t, 0)],
                          out_specs=[col(t, 0), col(t, 0), col(3, 0), col(3, 0), col(1, 0), col(1, 0)],
                          out_shape=[big, big, w3, w3, b1, b1],
                          compiler_params=_cparams(("parallel",), VMEM_BIG))(h, h, cw, cw, cb, cb, dact)


TS_CW = 256


def _tshift_fn(p, mu, sd):
    return p + mu * (sd(p, 1) - p)


def _tshift_fwd(p, mu):
    t = p.shape[0]
    col = lambda r: pl.BlockSpec((r, TS_CW), lambda j: (0, j))

    def body(p_ref, mu_ref, o_ref):
        o_ref[...] = _tshift_fn(p_ref[...], mu_ref[...], _shift_down)

    return pl.pallas_call(body, name="tshift_fwd", grid=(SHIFT_COLS // TS_CW,), in_specs=[col(t), col(1)],
                          out_specs=col(t), out_shape=jax.ShapeDtypeStruct((t, SHIFT_COLS), F32),
                          compiler_params=_cparams(("parallel",), VMEM_MID))(p, mu)


def _tshift_bwd(p, mu, dpam):
    t = p.shape[0]
    col = lambda r: pl.BlockSpec((r, TS_CW), lambda j: (0, j))

    def body(p_ref, mu_ref, d_ref, dp_ref, dmu_ref):
        _, vjp = jax.vjp(functools.partial(_tshift_fn, sd=_make_sd()), p_ref[...], mu_ref[...])
        dp, dmu = vjp(d_ref[...])
        dp_ref[...] = dp.astype(BF16)
        dmu_ref[...] = dmu

    return pl.pallas_call(body, name="tshift_bwd", grid=(SHIFT_COLS // TS_CW,), in_specs=[col(t), col(1), col(t)],
                          out_specs=[col(t), col(1)],
                          out_shape=[jax.ShapeDtypeStruct((t, SHIFT_COLS), BF16),
                                     jax.ShapeDtypeStruct((1, SHIFT_COLS), F32)],
                          compiler_params=_cparams(("parallel",), VMEM_MID))(p, mu, dpam)


_HI = lax.Precision.HIGHEST
_O = (0, RW, 2 * RW, 3 * RW, 3 * RW + W_LORA, 3 * RW + W_LORA + A_LORA, SHIFT_COLS)


def _seg(x, gm):
    return jnp.dot(x, gm, precision=_HI)


def _prep_fn(r, k, v, wd, ad, gd, w0, w2, a0, a2, g2, k_k, k_a, gm):
    w_log = -jax.nn.softplus(-(w0 + jnp.tanh(wd) @ w2)) - 0.5
    decay = jnp.exp(-jnp.exp(w_log))
    a = jax.nn.sigmoid(a0 + ad @ a2)
    g = jax.nn.sigmoid(gd) @ g2
    kk = k * k_k
    kk = kk / jnp.maximum(jnp.sqrt(_seg(kk * kk, gm)), 1e-12)
    k2 = k * (1.0 + (a - 1.0) * k_a)
    return r, decay, k2, v, -kk, kk * a, g


_PREP_W = ("w0", "w2", "a0", "a2", "g2", "k_k", "k_a")


def _prep_wspecs(w):
    return [_full(w[n].shape) for n in _PREP_W] + [_full((RW, RW))]


def _rwkv_prep_fwd(pam, w, gm):
    t = pam.shape[0]

    def body(p_ref, *refs):
        wr, outs = refs[:8], refs[8:]
        pieces = [p_ref[:, _O[i]:_O[i + 1]] for i in range(6)]
        res = _prep_fn(*pieces, *[x[...] for x in wr])
        for o, val in zip(outs, res):
            o[...] = val

    row = lambda c: pl.BlockSpec((TOK, c), lambda i: (i, 0))
    return pl.pallas_call(body, name="rwkv_prep_fwd", grid=(t // TOK,),
                          in_specs=[row(SHIFT_COLS)] + _prep_wspecs(w), out_specs=[row(RW)] * 7,
                          out_shape=[jax.ShapeDtypeStruct((t, RW), F32)] * 7,
                          compiler_params=_cparams(("parallel",), VMEM_MID))(pam, *[w[n] for n in _PREP_W], gm)


def _rwkv_prep_bwd(pam, w, gm, cts, more):
    t = pam.shape[0]

    def body(p_ref, *refs):
        wr, ct, ex, dp_ref, dws = refs[:8], refs[8:15], refs[15:18], refs[18], refs[19:]
        pieces = [p_ref[:, _O[i]:_O[i + 1]] for i in range(6)]
        fn = lambda *a: _prep_fn(*a, wr[7][...])
        _, vjp = jax.vjp(fn, *pieces, *[x[...] for x in wr[:7]])
        c = [x[...] for x in ct]
        c[0] = c[0] + ex[0][...]
        c[2] = c[2] + ex[1][...]
        c[3] = c[3] + ex[2][...]
        g = vjp(tuple(c))
        for i in range(6):
            dp_ref[:, _O[i]:_O[i + 1]] = g[i]
        for o, val in zip(dws, g[6:]):
            _acc_out(o, val)

    row = lambda c: pl.BlockSpec((TOK, c), lambda i: (i, 0))
    return pl.pallas_call(body, name="rwkv_prep_bwd", grid=(t // TOK,),
                          in_specs=[row(SHIFT_COLS)] + _prep_wspecs(w) + [row(RW)] * 10,
                          out_specs=[row(SHIFT_COLS)] + [_full(w[n].shape) for n in _PREP_W],
                          out_shape=[jax.ShapeDtypeStruct((t, SHIFT_COLS), F32)]
                          + [jax.ShapeDtypeStruct(w[n].shape, F32) for n in _PREP_W],
                          compiler_params=_cparams(("arbitrary",), VMEM_MID))(
                              pam, *[w[n] for n in _PREP_W], gm, *cts, *more)


def _post_fn(y, r, k2, v, g, ln_w, ln_b, r_k, gm):
    inv = 1.0 / HEAD
    d = y - _seg(y, gm) * inv
    yn = d * lax.rsqrt(_seg(d * d, gm) * inv + GN_EPS) * ln_w + ln_b
    bonus = _seg(r * k2 * r_k, gm) * v
    return (yn + bonus) * g


def _rwkv_post_fwd(y, r, k2, v, g, ln_w, ln_b, r_k, gm):
    t = y.shape[0]

    def body(*refs):
        o_ref = refs[-1]
        o_ref[...] = _post_fn(*[x[...] for x in refs[:-1]]).astype(BF16)

    row = pl.BlockSpec((TOK, RW), lambda i: (i, 0))
    return pl.pallas_call(body, name="rwkv_post_fwd", grid=(t // TOK,),
                          in_specs=[row] * 5 + [_full((1, RW))] * 3 + [_full((RW, RW))], out_specs=row,
                          out_shape=jax.ShapeDtypeStruct((t, RW), BF16),
                          compiler_params=_cparams(("parallel",), VMEM_MID))(y, r, k2, v, g, ln_w, ln_b, r_k, gm)


def _rwkv_post_bwd(y, r, k2, v, g, ln_w, ln_b, r_k, gm, dya):
    t = y.shape[0]

    def body(*refs):
        ins, gm_ref, d_ref, outs = refs[:8], refs[8], refs[9], refs[10:]
        fn = lambda *a: _post_fn(*a, gm_ref[...])
        _, vjp = jax.vjp(fn, *[x[...] for x in ins])
        gr = vjp(d_ref[...])
        for o, val in zip(outs[:5], gr[:5]):
            o[...] = val
        for o, val in zip(outs[5:], gr[5:]):
            _acc_out(o, val)

    row = pl.BlockSpec((TOK, RW), lambda i: (i, 0))
    vec = _full((1, RW))
    return pl.pallas_call(body, name="rwkv_post_bwd", grid=(t // TOK,),
                          in_specs=[row] * 5 + [vec] * 3 + [_full((RW, RW)), row],
                          out_specs=[row] * 5 + [vec] * 3,
                          out_shape=[jax.ShapeDtypeStruct((t, RW), F32)] * 5 + [jax.ShapeDtypeStruct((1, RW), F32)] * 3,
                          compiler_params=_cparams(("arbitrary",), VMEM_MID))(y, r, k2, v, g, ln_w, ln_b, r_k, gm, dya)


def _from_pt(x):
    n = x.shape[0]
    return x.reshape(n, HEAD, N_HEADS, PT).transpose(0, 3, 2, 1).reshape(n * PT, N_HEADS * HEAD)


def _lane_sum(x):
    return jnp.sum(x, axis=-1, keepdims=True)


def _pair_consts():
    lane = lax.broadcasted_iota(jnp.int32, (HEAD, LANES), 1)
    return lane, lane < HEAD


def _seg_sum_pair(x, first):
    return jnp.where(first, _lane_sum(jnp.where(first, x, 0.0)), _lane_sum(jnp.where(first, 0.0, x)))


def _to_pt(x):
    t = x.shape[0]
    return x.reshape(t // PT, PT, N_HEADS, HEAD).transpose(0, 3, 2, 1).reshape(t // PT, HEAD, N_HEADS * PT)


def _expand_cols(x, name):
    t = x.shape[0]
    tiles = WKV_CHUNK // PT

    def body(x_ref, o_ref):
        lane, first = _pair_consts()
        for tl in range(tiles):
            tile = x_ref[tl]
            for j in range(PT):
                for p in range(N_HEADS // 2):
                    c0 = _lane_sum(jnp.where(lane == (2 * p) * PT + j, tile, 0.0))
                    c1 = _lane_sum(jnp.where(lane == (2 * p + 1) * PT + j, tile, 0.0))
                    o_ref[tl * PT + j, :, p * LANES:(p + 1) * LANES] = jnp.where(first, c0, c1)

    return pl.pallas_call(
        body, name=name, grid=(t // WKV_CHUNK,),
        in_specs=[pl.BlockSpec((tiles, HEAD, LANES), lambda i: (i, 0, 0))],
        out_specs=pl.BlockSpec((WKV_CHUNK, HEAD, RW), lambda i: (i, 0, 0)),
        out_shape=jax.ShapeDtypeStruct((t, HEAD, RW), F32),
        compiler_params=_cparams(("parallel",), VMEM_MID))(_to_pt(x))


def _wkv_fwd(w, k, z, b, v_exp):
    t = w.shape[0]
    nc = t // WKV_CHUNK
    pairs = N_HEADS // 2

    def body(w_ref, k_ref, z_ref, b_ref, v_ref, s_all, s_ref):
        @pl.when(pl.program_id(0) == 0)
        def _():
            s_ref[...] = jnp.zeros_like(s_ref)

        _, first = _pair_consts()

        def group(gi, carry):
            base = pl.multiple_of(gi * 8, 8)
            rows = [ref[pl.ds(base, 8), :] for ref in (w_ref, k_ref, z_ref, b_ref)]
            s = [s_ref[:, p * LANES:(p + 1) * LANES] for p in range(pairs)]
            for jj in range(8):
                for p in range(pairs):
                    cs = slice(p * LANES, (p + 1) * LANES)
                    wr, kr, zr, br = [x[jj:jj + 1, cs] for x in rows]
                    s_all[base + jj, :, cs] = s[p]
                    sa = _seg_sum_pair(s[p] * zr, first)
                    s[p] = s[p] * wr + sa * br + v_ref[base + jj, :, cs] * kr
            for p in range(pairs):
                s_ref[:, p * LANES:(p + 1) * LANES] = s[p]
            return carry

        lax.fori_loop(0, WKV_CHUNK // 8, group, 0)

    row = pl.BlockSpec((WKV_CHUNK, RW), lambda i: (i, 0))
    big = pl.BlockSpec((WKV_CHUNK, HEAD, RW), lambda i: (i, 0, 0))
    return pl.pallas_call(
        body, name="wkv_fwd", grid=(nc,), in_specs=[row] * 4 + [big], out_specs=[big, _full((HEAD, RW))],
        out_shape=[jax.ShapeDtypeStruct((t, HEAD, RW), F32), jax.ShapeDtypeStruct((HEAD, RW), F32)],
        compiler_params=_cparams(("arbitrary",), VMEM_MID))(w, k, z, b, v_exp)


def _wkv_out(r, s_all, s_last):
    t = r.shape[0]
    nc = t // WKV_CHUNK
    tiles = WKV_CHUNK // PT
    pairs = N_HEADS // 2

    def body(r_ref, s_ref, nxt_ref, last_ref, y_ref):
        lane, first = _pair_consts()
        after = jnp.where(pl.program_id(0) == nc - 1, last_ref[...], nxt_ref[0])
        for tl in range(tiles):
            ytile = jnp.zeros((HEAD, LANES), F32)
            for g in range(PT // 8):
                rows = r_ref[tl * PT + g * 8:tl * PT + g * 8 + 8, :]
                for jj in range(8):
                    tt = tl * PT + g * 8 + jj
                    j = g * 8 + jj
                    for p in range(pairs):
                        cs = slice(p * LANES, (p + 1) * LANES)
                        s = s_ref[tt + 1, :, cs] if tt + 1 < WKV_CHUNK else after[:, cs]
                        pr = s * rows[jj:jj + 1, cs]
                        y0 = _lane_sum(jnp.where(first, pr, 0.0))
                        y1 = _lane_sum(jnp.where(first, 0.0, pr))
                        ytile = jnp.where(lane == (2 * p) * PT + j, y0, ytile)
                        ytile = jnp.where(lane == (2 * p + 1) * PT + j, y1, ytile)
            y_ref[tl] = ytile

    row = pl.BlockSpec((WKV_CHUNK, RW), lambda i: (i, 0))
    pt = pl.BlockSpec((tiles, HEAD, LANES), lambda i: (i, 0, 0))
    big = pl.BlockSpec((WKV_CHUNK, HEAD, RW), lambda i: (i, 0, 0))
    nxt = pl.BlockSpec((1, HEAD, RW), lambda i: (jnp.minimum((i + 1) * WKV_CHUNK, t - 1), 0, 0))
    return pl.pallas_call(
        body, name="wkv_out", grid=(nc,), in_specs=[row, big, nxt, _full((HEAD, RW))], out_specs=pt,
        out_shape=jax.ShapeDtypeStruct((t // PT, HEAD, LANES), F32),
        compiler_params=_cparams(("parallel",), VMEM_MID))(r, s_all, s_all, s_last)


def _wkv_bwd(r, w, k, z, b, v_exp, s_all, dy_exp):
    t = r.shape[0]
    nc = t // WKV_CHUNK
    tiles = WKV_CHUNK // PT
    pairs = N_HEADS // 2

    def body(r_ref, w_ref, k_ref, z_ref, b_ref, v_ref, s_all_ref, dy_ref,
             dr_ref, dw_ref, dk_ref, dz_ref, db_ref, dv_ref, ds_ref):
        @pl.when(pl.program_id(0) == 0)
        def _():
            ds_ref[...] = jnp.zeros_like(ds_ref)

        lane, first = _pair_consts()
        col_sum = lambda x: jnp.sum(x, axis=0, keepdims=True)
        row8 = lax.broadcasted_iota(jnp.int32, (8, LANES), 0)
        for tl in reversed(range(tiles)):
            def group(gg, dvtile):
                gi = PT // 8 - 1 - gg
                base = pl.multiple_of(tl * PT + gi * 8, 8)
                rows = [ref[pl.ds(base, 8), :] for ref in (r_ref, w_ref, k_ref, z_ref, b_ref)]
                outs = (dr_ref, dw_ref, dk_ref, dz_ref, db_ref)
                tiles8 = {(id(o), p): jnp.zeros((8, LANES), F32) for o in outs for p in range(pairs)}
                ds = [ds_ref[:, p * LANES:(p + 1) * LANES] for p in range(pairs)]
                for jj in reversed(range(8)):
                    j = gi * 8 + jj
                    for p in range(pairs):
                        cs = slice(p * LANES, (p + 1) * LANES)

                        def put(ref, val, p=p, jj=jj):
                            tiles8[(id(ref), p)] = jnp.where(row8 == jj, val, tiles8[(id(ref), p)])

                        rr, wr, kr, zr, br = [x[jj:jj + 1, cs] for x in rows]
                        sp = s_all_ref[base + jj, :, cs]
                        vc = v_ref[base + jj, :, cs]
                        dyc = dy_ref[base + jj, :, cs]
                        sa = _seg_sum_pair(sp * zr, first)
                        st = sp * wr + sa * br + vc * kr
                        d = ds[p] + dyc * rr
                        put(dr_ref, col_sum(st * dyc))
                        dvk = d * kr
                        dv0 = _lane_sum(jnp.where(first, dvk, 0.0))
                        dv1 = _lane_sum(jnp.where(first, 0.0, dvk))
                        dvtile = jnp.where(lane == (2 * p) * PT + j, dv0, dvtile)
                        dvtile = jnp.where(lane == (2 * p + 1) * PT + j, dv1, dvtile)
                        put(dk_ref, col_sum(d * vc))
                        put(dw_ref, col_sum(sp * d))
                        u = _seg_sum_pair(d * br, first)
                        put(dz_ref, col_sum(sp * u))
                        put(db_ref, col_sum(d * sa))
                        ds[p] = d * wr + u * zr
                for p in range(pairs):
                    ds_ref[:, p * LANES:(p + 1) * LANES] = ds[p]
                for o in outs:
                    for p in range(pairs):
                        o[pl.ds(base, 8), p * LANES:(p + 1) * LANES] = tiles8[(id(o), p)]
                return dvtile

            dv_ref[tl] = lax.fori_loop(0, PT // 8, group, jnp.zeros((HEAD, LANES), F32))

    rev = lambda i: nc - 1 - i
    row = pl.BlockSpec((WKV_CHUNK, RW), lambda i: (rev(i), 0))
    pt = pl.BlockSpec((tiles, HEAD, LANES), lambda i: (rev(i), 0, 0))
    big = pl.BlockSpec((WKV_CHUNK, HEAD, RW), lambda i: (rev(i), 0, 0))
    return pl.pallas_call(
        body, name="wkv_bwd", grid=(nc,), in_specs=[row] * 5 + [big, big, big], out_specs=[row] * 5 + [pt],
        out_shape=[jax.ShapeDtypeStruct((t, RW), F32)] * 5 + [jax.ShapeDtypeStruct((t // PT, HEAD, LANES), F32)],
        scratch_shapes=[pltpu.VMEM((HEAD, RW), F32)],
        compiler_params=_cparams(("arbitrary",), VMEM_BIG))(r, w, k, z, b, v_exp, s_all, dy_exp)


LRU_CW = 128
_BX0 = SHIFT_COLS // LRU_CW
_BG0 = (SHIFT_COLS + LRU_W) // LRU_CW


def _lru_fn(bx, bg, cw, cb, ga, ba, gx, bxb, lam, sd, scan):
    xc = cw[0:1] * sd(bx, 3) + cw[1:2] * sd(bx, 2) + cw[2:3] * sd(bx, 1) + cw[3:4] * bx + cb
    gr = jax.nn.sigmoid(xc @ ga + ba)
    gi = jax.nn.sigmoid(xc @ gx + bxb)
    log_a = -LRU_C * gr * jax.nn.softplus(-lam)
    a = jnp.exp(log_a)
    mult = jnp.sqrt(-jnp.tanh(log_a) * (jnp.exp(2.0 * log_a) + 1.0))
    return scan(a, xc * gi * mult) * jax.nn.gelu(bg)


def _lru_specs(t):
    col = lambda r, off=0: pl.BlockSpec((r, LRU_CW), lambda j: (0, j + off))
    diag = pl.BlockSpec((LRU_CW, LRU_CW), lambda j: (j, j))
    return col, [col(t, _BX0), col(t, _BG0), col(4), col(1), diag, col(1), diag, col(1), col(1)]


def _lru_fwd(p, cw, cb, ga, ba, gx, bxb, lam):
    t = p.shape[0]
    col, in_specs = _lru_specs(t)

    def body(*refs):
        o_ref = refs[-1]
        o_ref[...] = _lru_fn(*[x[...] for x in refs[:-1]], _shift_down, _lin_scan).astype(BF16)

    return pl.pallas_call(body, name="lru_fwd", grid=(LRU_W // LRU_CW,), in_specs=in_specs, out_specs=col(t),
                          out_shape=jax.ShapeDtypeStruct((t, LRU_W), BF16),
                          compiler_params=_cparams(("parallel",), VMEM_MID))(p, p, cw, cb, ga, ba, gx, bxb, lam)


def _lru_bwd(p, cw, cb, ga, ba, gx, bxb, lam, dyb):
    t = p.shape[0]
    col, in_specs = _lru_specs(t)

    def body(*refs):
        ins, d_ref, outs = refs[:9], refs[9], refs[10:]
        fn = functools.partial(_lru_fn, sd=_make_sd(), scan=_make_scan())
        _, vjp = jax.vjp(fn, *[x[...] for x in ins])
        g = vjp(d_ref[...])
        outs[0][...] = g[0].astype(BF16)
        outs[1][...] = g[1].astype(BF16)
        for o, val in zip(outs[2:], g[2:]):
            o[...] = val

    sq = pl.BlockSpec((LRU_CW, LRU_CW), lambda j: (j, 0))
    act = jax.ShapeDtypeStruct((t, LRU_W), BF16)
    vec = jax.ShapeDtypeStruct((1, LRU_W), F32)
    sqs = jax.ShapeDtypeStruct((LRU_W, LRU_CW), F32)
    return pl.pallas_call(body, name="lru_bwd", grid=(LRU_W // LRU_CW,), in_specs=in_specs + [col(t, RW // LRU_CW)],
                          out_specs=[col(t), col(t), col(4), col(1), sq, col(1), sq, col(1), col(1)],
                          out_shape=[act, act, jax.ShapeDtypeStruct((4, LRU_W), F32), vec, sqs, vec, sqs, vec, vec],
                          compiler_params=_cparams(("parallel",), VMEM_BIG))(p, p, cw, cb, ga, ba, gx, bxb, lam, dyb)


def _s5_disc_fn(a_re, a_im, log_dt, b_re, b_im, e):
    lam_re = jnp.minimum(a_re, -1e-4)
    lam_im = a_im
    dt = jnp.exp(log_dt)
    mag = jnp.exp(lam_re * dt)
    ab_re = mag * jnp.cos(lam_im * dt)
    ab_im = mag * jnp.sin(lam_im * dt)
    den = lam_re * lam_re + lam_im * lam_im
    zr = ab_re - 1.0
    q_re = jnp.dot((zr * lam_re + ab_im * lam_im) / den, e, precision=_HI)
    q_im = jnp.dot((ab_im * lam_re - zr * lam_im) / den, e, precision=_HI)
    return ab_re, ab_im, q_re * b_re - q_im * b_im, q_re * b_im + q_im * b_re


def _s5_disc_fwd(a_re, a_im, log_dt, b_re, b_im, e):
    def body(*refs):
        res = _s5_disc_fn(*[x[...] for x in refs[:6]])
        for o, val in zip(refs[6:], res):
            o[...] = val

    small = jax.ShapeDtypeStruct(a_re.shape, F32)
    wide = jax.ShapeDtypeStruct(b_re.shape, F32)
    return pl.pallas_call(body, name="s5_disc_fwd", out_shape=[small, small, wide, wide])(
        a_re, a_im, log_dt, b_re, b_im, e)


def _s5_disc_bwd(a_re, a_im, log_dt, b_re, b_im, e, cts):
    def body(*refs):
        ins, e_ref, ct, outs = refs[:5], refs[5], refs[6:10], refs[10:]
        _, vjp = jax.vjp(lambda *a: _s5_disc_fn(*a, e_ref[...]), *[x[...] for x in ins])
        for o, val in zip(outs, vjp(tuple(c[...] for c in ct))):
            o[...] = val

    shapes = [jax.ShapeDtypeStruct(x.shape, F32) for x in (a_re, a_im, log_dt, b_re, b_im)]
    return pl.pallas_call(body, name="s5_disc_bwd", out_shape=shapes)(a_re, a_im, log_dt, b_re, b_im, e, *cts)


def _cmul(a, b):
    return a[0] * b[0] - a[1] * b[1], a[0] * b[1] + a[1] * b[0]


def _s5_scan(sr, si, ab, reverse):
    n_tiles = sr.shape[0] // 8
    width = sr.shape[1]
    row8 = lax.broadcasted_iota(jnp.int32, (8, width), 0)
    p1 = ab
    p2 = _cmul(p1, p1)
    p4 = _cmul(p2, p2)
    pw = [p1]
    for _ in range(7):
        pw.append(_cmul(pw[-1], p1))
    cr = jnp.zeros((8, width), F32)
    ci = jnp.zeros((8, width), F32)
    for j in range(8):
        e = pw[7 - j] if reverse else pw[j]
        cr = jnp.where(row8 == j, e[0], cr)
        ci = jnp.where(row8 == j, e[1], ci)

    def tile(i, carry):
        idx = n_tiles - 1 - i if reverse else i
        base = pl.multiple_of(idx * 8, 8)
        x = (sr[pl.ds(base, 8), :], si[pl.ds(base, 8), :])
        for d, q in ((1, p1), (2, p2), (4, p4)):
            keep = row8 < 8 - d if reverse else row8 >= d
            amt = 8 - d if reverse else d
            sh = (jnp.where(keep, pltpu.roll(x[0], amt, 0), 0.0), jnp.where(keep, pltpu.roll(x[1], amt, 0), 0.0))
            m = _cmul(q, sh)
            x = (x[0] + m[0], x[1] + m[1])
        m = _cmul((cr, ci), carry)
        x = (x[0] + m[0], x[1] + m[1])
        sr[pl.ds(base, 8), :] = x[0]
        si[pl.ds(base, 8), :] = x[1]
        edge = slice(0, 1) if reverse else slice(7, 8)
        return x[0][edge], x[1][edge]

    zero = jnp.zeros((1, width), F32)
    lax.fori_loop(0, n_tiles, tile, (zero, zero))


_S5_W = S5_SLAB // S5_GROUP * S5_STATE


def _s5_specs(t):
    col = lambda r: pl.BlockSpec((r, S5_SLAB), lambda j: (0, j))
    bb = pl.BlockSpec((None, S5_SLAB, _S5_W), lambda j: (j, 0, 0))
    cd = pl.BlockSpec((None, _S5_W, S5_SLAB), lambda j: (j, 0, 0))
    ab = pl.BlockSpec((None, 1, _S5_W), lambda j: (j, 0, 0))
    return col, bb, cd, ab


def _s5_fwd(u, dvec, bbr, bbi, cdr, cdi, abr, abi):
    t, width = u.shape
    col, bb, cd, ab = _s5_specs(t)

    def body(u_ref, d_ref, bbr_ref, bbi_ref, cdr_ref, cdi_ref, abr_ref, abi_ref, o_ref, sr, si):
        uv = u_ref[...]
        sr[...] = jnp.dot(uv, bbr_ref[...], preferred_element_type=F32)
        si[...] = jnp.dot(uv, bbi_ref[...], preferred_element_type=F32)
        _s5_scan(sr, si, (abr_ref[...], abi_ref[...]), False)
        y = jnp.dot(sr[...], cdr_ref[...], preferred_element_type=F32) - jnp.dot(si[...], cdi_ref[...],
                                                                                 preferred_element_type=F32)
        o_ref[...] = jax.nn.gelu(y + d_ref[...] * uv).astype(BF16)

    return pl.pallas_call(body, name="s5_fwd", grid=(width // S5_SLAB,),
                          in_specs=[col(t), col(1), bb, bb, cd, cd, ab, ab], out_specs=col(t),
                          out_shape=jax.ShapeDtypeStruct((t, width), BF16),
                          scratch_shapes=[pltpu.VMEM((t, _S5_W), F32)] * 2,
                          compiler_params=_cparams(("parallel",), VMEM_BIG))(u, dvec, bbr, bbi, cdr, cdi, abr, abi)


def _s5_bwd(u, dvec, bbr, bbi, cdr, cdi, abr, abi, dyact):
    t, width = u.shape
    col, bb, cd, ab = _s5_specs(t)
    ns = width // S5_SLAB
    tn = (((0,), (0,)), ((), ()))
    nt = (((1,), (1,)), ((), ()))

    def body(u_ref, d_ref, bbr_ref, bbi_ref, cdr_ref, cdi_ref, abr_ref, abi_ref, dy_ref,
             du_ref, dd_ref, dbbr_ref, dbbi_ref, dcdr_ref, dcdi_ref, dabr_ref, dabi_ref, sr, si, gr, gi):
        uv = u_ref[...]
        dv = d_ref[...]
        abv = (abr_ref[...], abi_ref[...])
        sr[...] = jnp.dot(uv, bbr_ref[...], preferred_element_type=F32)
        si[...] = jnp.dot(uv, bbi_ref[...], preferred_element_type=F32)
        _s5_scan(sr, si, abv, False)
        y = jnp.dot(sr[...], cdr_ref[...], preferred_element_type=F32) - jnp.dot(si[...], cdi_ref[...],
                                                                                 preferred_element_type=F32)
        _, vjp = jax.vjp(jax.nn.gelu, y + dv * uv)
        (dpre,) = vjp(dy_ref[...].astype(F32))
        dd_ref[...] = jnp.sum(dpre * uv, axis=0, keepdims=True)
        dcdr_ref[...] = lax.dot_general(sr[...], dpre, tn, preferred_element_type=F32)
        dcdi_ref[...] = -lax.dot_general(si[...], dpre, tn, preferred_element_type=F32)
        gr[...] = lax.dot_general(dpre, cdr_ref[...], nt, preferred_element_type=F32)
        gi[...] = -lax.dot_general(dpre, cdi_ref[...], nt, preferred_element_type=F32)
        _s5_scan(gr, gi, (abv[0], -abv[1]), True)

        row8 = lax.broadcasted_iota(jnp.int32, (8, _S5_W), 0)

        def tile(i, carry):
            acc_r, acc_i, last_r, last_i = carry
            base = pl.multiple_of(i * 8, 8)
            s_r, s_i = sr[pl.ds(base, 8), :], si[pl.ds(base, 8), :]
            g_r, g_i = gr[pl.ds(base, 8), :], gi[pl.ds(base, 8), :]
            p_r = jnp.where(row8 == 0, last_r, pltpu.roll(s_r, 1, 0))
            p_i = jnp.where(row8 == 0, last_i, pltpu.roll(s_i, 1, 0))
            acc_r = acc_r + jnp.sum(g_r * p_r + g_i * p_i, axis=0, keepdims=True)
            acc_i = acc_i + jnp.sum(g_i * p_r - g_r * p_i, axis=0, keepdims=True)
            return acc_r, acc_i, s_r[7:8], s_i[7:8]

        zero = jnp.zeros((1, _S5_W), F32)
        acc_r, acc_i, _, _ = lax.fori_loop(0, t // 8, tile, (zero, zero, zero, zero))
        dabr_ref[...] = acc_r
        dabi_ref[...] = acc_i
        du_ref[...] = (dpre * dv + lax.dot_general(gr[...], bbr_ref[...], nt, preferred_element_type=F32)
                       + lax.dot_general(gi[...], bbi_ref[...], nt, preferred_element_type=F32))
        dbbr_ref[...] = lax.dot_general(uv, gr[...], tn, preferred_element_type=F32)
        dbbi_ref[...] = lax.dot_general(uv, gi[...], tn, preferred_element_type=F32)

    sds = jax.ShapeDtypeStruct
    return pl.pallas_call(
        body, name="s5_bwd", grid=(ns,), in_specs=[col(t), col(1), bb, bb, cd, cd, ab, ab, col(t)],
        out_specs=[col(t), col(1), bb, bb, cd, cd, ab, ab],
        out_shape=[sds((t, width), F32), sds((1, width), F32), sds((ns, S5_SLAB, _S5_W), F32),
                   sds((ns, S5_SLAB, _S5_W), F32), sds((ns, _S5_W, S5_SLAB), F32), sds((ns, _S5_W, S5_SLAB), F32),
                   sds((ns, 1, _S5_W), F32), sds((ns, 1, _S5_W), F32)],
        scratch_shapes=[pltpu.VMEM((t, _S5_W), F32)] * 4,
        compiler_params=_cparams(("parallel",), VMEM_BIG))(u, dvec, bbr, bbi, cdr, cdi, abr, abi, dyact)


def _gate_dense(w):
    h = w.shape[0]
    return jnp.einsum("hij,hg->higj", w, jnp.eye(h, dtype=F32)).reshape(h * HEAD, h * HEAD)


def _gate_blocks(d):
    x = d.reshape(LRU_W // LRU_CW, 2, HEAD, 2, HEAD)
    return jnp.einsum("tgihj,gh->tgij", x, jnp.eye(2, dtype=F32)).reshape(LRU_W // HEAD, HEAD, HEAD)


_GPS = S5_SLAB // S5_GROUP
_NS = S5_GROUPS // _GPS


def _s5_in_dense(bb):
    x = bb.reshape(_NS, _GPS, S5_STATE, S5_GROUP)
    return jnp.einsum("sgnc,gh->sgchn", x, jnp.eye(_GPS, dtype=F32)).reshape(_NS, S5_SLAB, _S5_W)


def _s5_in_blocks(d):
    x = d.reshape(_NS, _GPS, S5_GROUP, _GPS, S5_STATE)
    return jnp.einsum("sgchn,gh->sgnc", x, jnp.eye(_GPS, dtype=F32)).reshape(S5_GROUPS, S5_STATE * S5_GROUP)


def _s5_out_dense(c):
    x = c.reshape(_NS, _GPS, S5_GROUP, S5_STATE)
    return jnp.einsum("sgcn,gh->shngc", x, jnp.eye(_GPS, dtype=F32)).reshape(_NS, _S5_W, S5_SLAB)


def _s5_out_blocks(d):
    x = d.reshape(_NS, _GPS, S5_STATE, _GPS, S5_GROUP)
    return jnp.einsum("shngc,gh->sgcn", x, jnp.eye(_GPS, dtype=F32)).reshape(S5_GROUPS, S5_GROUP, S5_STATE)


def _local_step(x, tgt, w, late_weights, send_grads):
    d_model = x.shape[1]
    gs = {}
    gm = jnp.kron(jnp.eye(N_HEADS, dtype=F32), jnp.ones((HEAD, HEAD), F32))
    n_layers = w["f_norm_g"].shape[0]

    def ffn_fwd(xin, l):
        xn = _rms_fwd(xin, w["f_norm_g"][l:l + 1], f"rms_f{l}")
        h = _matmul(xn, w["f_w_up_t"][l], "nt", f"mm_f{l}_up")
        act = _ffn_mid_fwd(h, w["f_conv_w"][l], w["f_conv_b"][l:l + 1], f"ffn_mid_fwd{l}")
        return _matmul(act, w["f_w_down"][l], "nn", f"mm_f{l}_down", add=xin), (xin, xn, h, act)

    def ffn_bwd(g, saved, l):
        xin, xn, h, act = saved
        dact = _matmul(g, w["f_w_down"][l], "nt", f"mm_f{l}_dact")
        d_down = _matmul(act, g, "tn", f"mm_f{l}_ddown", out_dtype=BF16)
        dhg, dhv, dwg, dwv, dbg, dbv = _ffn_mid_bwd(h, w["f_conv_w"][l], w["f_conv_b"][l:l + 1], dact,
                                                    f"ffn_mid_bwd{l}")
        dh = jnp.concatenate([dhg, dhv], axis=1)
        dxn = _matmul(dh, w["f_w_up_t"][l], "nn", f"mm_f{l}_dxn")
        d_up = _matmul(dh, xn, "tn", f"mm_f{l}_dup", out_dtype=BF16)
        dx, dgn = _rms_bwd(xin, w["f_norm_g"][l:l + 1], dxn, g, f"rms_f{l}_bwd")
        return dx, d_up, d_down, jnp.concatenate([dwg, dwv], axis=1), jnp.concatenate([dbg, dbv], axis=1), dgn

    xn0 = _rms_fwd(x, w["e_norm_g"], "rms_e")
    p = _matmul(xn0, w["e_w_in_t"], "nt", "mm_e_in")
    pam = _tshift_fwd(p, w["e_mu"])
    pw = dict(w0=w["e_w0"], w2=w["e_w2"][0], a0=w["e_a0"], a2=w["e_a2"][0], g2=w["e_g2"][0],
              k_k=w["e_k_k"], k_a=w["e_k_a"])
    r, dec, k2, v, z, b, gate = _rwkv_prep_fwd(pam, pw, gm)
    v_exp = _expand_cols(v, "wkv_expand_v")
    s_all, s_last = _wkv_fwd(dec, k2, z, b, v_exp)
    y_pt = _wkv_out(r, s_all, s_last)
    y = _from_pt(y_pt)
    rk = w["e_r_k"].reshape(1, RW)
    ya = _rwkv_post_fwd(y, r, k2, v, gate, w["e_ln_w"], w["e_ln_b"], rk, gm)
    ga, gx = _gate_dense(w["e_gate_a_w"][0]), _gate_dense(w["e_gate_x_w"][0])
    lru_w = (w["e_conv_w"][0], w["e_conv_b"], ga, w["e_gate_a_b"], gx, w["e_gate_x_b"], w["e_lru_lambda"])
    yb = _lru_fwd(p, *lru_w)
    ycat = jnp.concatenate([ya, yb], axis=1)
    x1 = _matmul(ycat, w["e_w_out"], "nn", "mm_e_out", add=x)
    w = {**w, **late_weights(x1)}
    x2, ffn0 = ffn_fwd(x1, 0)

    xn1 = _rms_fwd(x2, w["o_norm_g"], "rms_o")
    u = _matmul(xn1, w["o_w_in"], "nn", "mm_o_in")
    expand = jnp.kron(jnp.eye(S5_STATE, dtype=F32), jnp.ones((1, S5_GROUP), F32))
    disc_in = (w["o_A_re"][0], w["o_A_im"][0], w["o_log_dt"].reshape(S5_GROUPS, 1),
               w["o_B_re"][0].reshape(S5_GROUPS, -1), w["o_B_im"][0].reshape(S5_GROUPS, -1), expand)
    ab_re, ab_im, bb_re, bb_im = _s5_disc_fwd(*disc_in)
    s5_w = (w["o_D"], _s5_in_dense(bb_re), _s5_in_dense(bb_im), _s5_out_dense(w["o_C_re"][0]),
            _s5_out_dense(w["o_C_im"][0]), ab_re.reshape(_NS, 1, _S5_W), ab_im.reshape(_NS, 1, _S5_W))
    yact = _s5_fwd(u, *s5_w)
    zz = _matmul(yact, w["o_w_glu_t"], "nt", "mm_o_glu")
    x3 = _glu_fwd(x2, zz)
    x4, ffn1 = ffn_fwd(x3, 1)

    loss, g, gs["final_norm_g"] = _loss_head(x4, w["final_norm_g"].reshape(1, d_model), tgt)
    gs["final_norm_g"] = gs["final_norm_g"].reshape(d_model)

    g, up1, down1, dcw1, dcb1, dfn1 = ffn_bwd(g, ffn1, 1)
    dz = _glu_bwd(zz, g)
    dyact = _matmul(dz, w["o_w_glu_t"], "nn", "mm_o_dyact")
    d_glu = _matmul(dz, yact, "tn", "mm_o_dglu", out_dtype=BF16)
    du, gs["o_D"], dbbr, dbbi, dcdr, dcdi, dabr, dabi = _s5_bwd(u, *s5_w, dyact)
    gs["o_C_re"] = _s5_out_blocks(dcdr)[None]
    gs["o_C_im"] = _s5_out_blocks(dcdi)[None]
    cts = (dabr.reshape(S5_GROUPS, S5_STATE), dabi.reshape(S5_GROUPS, S5_STATE), _s5_in_blocks(dbbr),
           _s5_in_blocks(dbbi))
    da_re, da_im, dlog_dt, db_re, db_im = _s5_disc_bwd(*disc_in, cts)
    gs["o_A_re"], gs["o_A_im"], gs["o_log_dt"] = da_re[None], da_im[None], dlog_dt.reshape(1, S5_GROUPS)
    gs["o_B_re"] = db_re.reshape(w["o_B_re"].shape)
    gs["o_B_im"] = db_im.reshape(w["o_B_im"].shape)
    dxn = _matmul(du, w["o_w_in"], "nt", "mm_o_dxn")
    d_oin = _matmul(xn1, du, "tn", "mm_o_din", out_dtype=BF16)
    g, gs["o_norm_g"] = _rms_bwd(x2, w["o_norm_g"], dxn, g, "rms_o_bwd")
    g = send_grads("a", [("f_w_up", 1, up1), ("f_w_down", 1, down1), ("o_w_glu", 0, d_glu), ("o_w_in", 0, d_oin)], g)

    g, up0, down0, dcw0, dcb0, dfn0 = ffn_bwd(g, ffn0, 0)
    gs["f_conv_w"] = jnp.stack([dcw0, dcw1])
    gs["f_conv_b"] = jnp.concatenate([dcb0, dcb1], axis=0)
    gs["f_norm_g"] = jnp.concatenate([dfn0, dfn1], axis=0)

    dycat = _matmul(g, w["e_w_out"], "nt", "mm_e_dycat")
    d_eout = _matmul(ycat, g, "tn", "mm_e_dout", out_dtype=BF16)
    dycat = send_grads("b", [("f_w_up", 0, up0), ("f_w_down", 0, down0), ("e_w_out", 0, d_eout)], dycat)
    dy, dr1, dk1, dv1, dgate, gs["e_ln_w"], gs["e_ln_b"], drk = _rwkv_post_bwd(
        y, r, k2, v, gate, w["e_ln_w"], w["e_ln_b"], rk, gm, dycat)
    gs["e_r_k"] = drk.reshape(w["e_r_k"].shape)
    dr2, ddec, dk2, dzz, dbb, dv_pt = _wkv_bwd(r, dec, k2, z, b, v_exp, s_all, _expand_cols(dy, "wkv_expand_dy"))
    dpam, gs["e_w0"], dw2, gs["e_a0"], da2, dg2, gs["e_k_k"], gs["e_k_a"] = _rwkv_prep_bwd(
        pam, pw, gm, (dr2, ddec, dk2, _from_pt(dv_pt), dzz, dbb, dgate), (dr1, dk1, dv1))
    gs["e_w2"], gs["e_a2"], gs["e_g2"] = dw2[None], da2[None], dg2[None]
    dpa, gs["e_mu"] = _tshift_bwd(p, w["e_mu"], dpam)
    dbx, dbg, dcw, gs["e_conv_b"], dga, gs["e_gate_a_b"], dgx, gs["e_gate_x_b"], gs["e_lru_lambda"] = _lru_bwd(
        p, *lru_w, dycat)
    gs["e_conv_w"] = dcw[None]
    gs["e_gate_a_w"] = _gate_blocks(dga)[None]
    gs["e_gate_x_w"] = _gate_blocks(dgx)[None]
    dp = jnp.concatenate([dpa, dbx, dbg], axis=1)
    dxn = _matmul(dp, w["e_w_in_t"], "nn", "mm_e_dxn")
    d_ein = _matmul(dp, xn0, "tn", "mm_e_din", out_dtype=BF16)
    grad_x, gs["e_norm_g"] = _rms_bwd(x, w["e_norm_g"], dxn, g, "rms_e_bwd")
    grad_x = send_grads("c", [("e_w_in", 0, d_ein)], grad_x)
    return loss, grad_x, gs


CAST_ROWS = 256


def _cast_shard(w3, layer, transpose, chip, name):
    _, rows, cols = w3.shape
    tr = _tile(rows, (CAST_ROWS, 176, 128))

    def body(c_ref, w_ref, o_ref):
        v = w_ref[...]
        o_ref[...] = (v.T if transpose else v).astype(BF16)

    in_spec = pl.BlockSpec((None, tr, cols), lambda i, c: (layer, i, 0))
    if transpose:
        out_spec, shape = pl.BlockSpec((None, cols, tr), lambda i, c: (c[0], 0, i)), (cols, rows)
    else:
        out_spec, shape = pl.BlockSpec((None, tr, cols), lambda i, c: (c[0], i, 0)), (rows, cols)
    grid_spec = pltpu.PrefetchScalarGridSpec(num_scalar_prefetch=1, grid=(rows // tr,), in_specs=[in_spec],
                                             out_specs=out_spec)
    return pl.pallas_call(body, name=name, grid_spec=grid_spec,
                          out_shape=jax.ShapeDtypeStruct((N_CHIPS,) + shape, BF16),
                          compiler_params=_cparams(("parallel",), VMEM_MID))(chip, w3)


_ANY = pl.BlockSpec(memory_space=pl.ANY)


def _coords():
    return lax.axis_index("x"), lax.axis_index("y"), lax.axis_index("c")


def _flip(v, d):
    return 1 - v if d else v


_CHIP_RELS = ((1, 0), (0, 1), (1, 1))
_DEV_RELS = tuple((dx, dy, dc) for dx in (0, 1) for dy in (0, 1) for dc in (0, 1))[1:]


_HBM = pl.BlockSpec(memory_space=pltpu.HBM)
_SEM = pl.BlockSpec(memory_space=pltpu.SEMAPHORE)
_EFFECT = pltpu.SideEffectType.DATAFLOW_SIDE_EFFECTING


def _in_hbm(a):
    return pltpu.with_memory_space_constraint(a, pltpu.HBM)


def _gather_copies(bufs, send, recv, landed):
    x, y, c = _coords()
    me = 2 * x + y
    res = []
    for i, buf in enumerate(bufs):
        for j, (dx, dy) in enumerate(_CHIP_RELS):
            px, py = _flip(x, dx), _flip(y, dy)
            k = i * len(_CHIP_RELS) + j
            res.append(pltpu.make_async_remote_copy(
                src_ref=buf.at[me], dst_ref=buf.at[2 * px + py if landed else me], send_sem=send.at[k],
                recv_sem=recv.at[k], device_id=(px, py, c), device_id_type=MESH))
    return res


def _scatter_copies(srcs, lands, send, recv, landed):
    x, y, c = _coords()
    me = 4 * x + 2 * y + c
    res = []
    for i, (src, land) in enumerate(zip(srcs, lands)):
        for j, (dx, dy, dc) in enumerate(_DEV_RELS):
            peer = (_flip(x, dx), _flip(y, dy), _flip(c, dc))
            pid = 4 * peer[0] + 2 * peer[1] + peer[2]
            k = i * len(_DEV_RELS) + j
            res.append(pltpu.make_async_remote_copy(
                src_ref=src.at[pid], dst_ref=land.at[pid if landed else me], send_sem=send.at[k],
                recv_sem=recv.at[k], device_id=peer, device_id_type=MESH))
    return res


def _split_start(bufs, n_src, copies, n_rel, name, after):
    n = len(bufs)
    nk = n_src * n_rel

    def body(*refs):
        ins, send, recv, token = refs[:n], refs[n + 1 + n], refs[n + 2 + n], refs[-1]
        for cp in copies(ins, send, recv, False):
            cp.start()
        token[...] = jnp.zeros_like(token)

    res = pl.pallas_call(
        body, name=name, in_specs=[_HBM] * n + [_ANY],
        out_specs=[_HBM] * n + [_SEM, _SEM, pl.BlockSpec(memory_space=pltpu.VMEM)],
        out_shape=[pltpu.HBM(b.shape, b.dtype) for b in bufs]
        + [pltpu.SemaphoreType.DMA((nk,)), pltpu.SemaphoreType.DMA((nk,)), jax.ShapeDtypeStruct((8, LANES), F32)],
        input_output_aliases={i: i for i in range(n)},
        compiler_params=pltpu.CompilerParams(has_side_effects=_EFFECT))(*[_in_hbm(b) for b in bufs], after)
    return res[n], res[n + 1], list(res[:n]), res[n + 2]


def _split_wait(bufs, send, recv, copies, name, after):
    n = len(bufs)

    def body(*refs):
        ins, send_ref, recv_ref = refs[:n], refs[n], refs[n + 1]
        for cp in copies(ins, send_ref, recv_ref, True):
            cp.wait_send()
            cp.wait_recv()

    return pl.pallas_call(
        body, name=name, in_specs=[_HBM] * n + [_SEM, _SEM, _ANY], out_specs=[_HBM] * n,
        out_shape=[pltpu.HBM(b.shape, b.dtype) for b in bufs], input_output_aliases={i: i for i in range(n)},
        compiler_params=pltpu.CompilerParams(has_side_effects=_EFFECT))(*bufs, send, recv, after)


def _gather_start(bufs, name, after):
    return _split_start(bufs, len(bufs), _gather_copies, len(_CHIP_RELS), name, after)


def _gather_wait(bufs, send, recv, name, after):
    return _split_wait(bufs, send, recv, _gather_copies, name, after)


def _scatter_start(srcs, name, after):
    n = len(srcs)
    lands = [lax.empty(a.shape, a.dtype) for a in srcs]
    fn = lambda refs, send, recv, landed: _scatter_copies(refs[:n], refs[n:], send, recv, landed)
    send, recv, bufs, token = _split_start(list(srcs) + lands, n, fn, len(_DEV_RELS), name, after)
    return send, recv, bufs, token


def _scatter_wait(bufs, send, recv, name, after):
    n = len(bufs) // 2
    fn = lambda refs, s, r, landed: _scatter_copies(refs[:n], refs[n:], s, r, landed)
    res = _split_wait(bufs, send, recv, fn, name, after)
    return res[:n], res[n:]


def _sum_segments(src, land, me, name):
    nd, seg, cols = src.shape
    ts = _tile(seg, (256, 176, 128))

    def body(m_ref, *refs):
        o_ref = refs[-1]
        acc = refs[0][...].astype(F32)
        for r in refs[1:-1]:
            acc = acc + r[...].astype(F32)
        o_ref[...] = acc

    def peer(rel):
        bits = 4 * rel[0] + 2 * rel[1] + rel[2]
        return pl.BlockSpec((None, ts, cols), lambda i, m: (jnp.bitwise_xor(m[0], bits), i, 0))

    grid_spec = pltpu.PrefetchScalarGridSpec(
        num_scalar_prefetch=1, grid=(seg // ts,),
        in_specs=[pl.BlockSpec((None, ts, cols), lambda i, m: (m[0], i, 0))] + [peer(r) for r in _DEV_RELS],
        out_specs=pl.BlockSpec((None, ts, cols), lambda i, m: (m[1], i, 0)))
    return pl.pallas_call(body, name=name, grid_spec=grid_spec,
                          out_shape=jax.ShapeDtypeStruct((2, seg, cols), F32),
                          compiler_params=_cparams(("parallel",), VMEM_MID))(me, src, *[land] * len(_DEV_RELS))


def _exchange_sibling(arrs):
    n = len(arrs)

    def body(*refs):
        outs, (send, recv) = refs[n:2 * n], refs[2 * n:]
        x, y, c = _coords()
        sib = (x, y, 1 - c)
        sends, recvs = [], []
        for i in range(n):
            cp = pltpu.make_async_remote_copy(src_ref=outs[i].at[c], dst_ref=outs[i].at[c], send_sem=send.at[i],
                                              recv_sem=recv.at[i], device_id=sib, device_id_type=MESH)
            cp.start()
            sends.append(cp)
            recvs.append(pltpu.make_async_remote_copy(src_ref=outs[i].at[c], dst_ref=outs[i].at[1 - c],
                                                      send_sem=send.at[i], recv_sem=recv.at[i], device_id=sib,
                                                      device_id_type=MESH))
        for cp in recvs:
            cp.wait_recv()
        for cp in sends:
            cp.wait_send()

    return pl.pallas_call(
        body, name="exchange_sibling", in_specs=[_ANY] * n, out_specs=[_ANY] * n,
        out_shape=[jax.ShapeDtypeStruct(a.shape, a.dtype) for a in arrs],
        input_output_aliases={i: i for i in range(n)},
        scratch_shapes=[pltpu.SemaphoreType.DMA((n,)), pltpu.SemaphoreType.DMA((n,))])(*arrs)


def _allreduce_small(vec):
    nd, rows, lanes = vec.shape
    nr = len(_DEV_RELS)

    def body(in_ref, out_ref, stage, red, send, recv):
        x, y, c = _coords()
        me = 4 * x + 2 * y + c
        peers = []
        for dx, dy, dc in _DEV_RELS:
            peer = (_flip(x, dx), _flip(y, dy), _flip(c, dc))
            peers.append((peer, 4 * peer[0] + 2 * peer[1] + peer[2]))

        def copy(src, dst, k, peer):
            return pltpu.make_async_remote_copy(src_ref=src, dst_ref=dst, send_sem=send.at[k], recv_sem=recv.at[k],
                                                device_id=peer, device_id_type=MESH)

        first = [copy(in_ref.at[pid], stage.at[me], j, peer) for j, (peer, pid) in enumerate(peers)]
        for cp in first:
            cp.start()
        stage[me] = in_ref[me]
        for j, (peer, pid) in enumerate(peers):
            copy(in_ref.at[pid], stage.at[pid], j, peer).wait_recv()
        acc = stage[0]
        for d in range(1, nd):
            acc = acc + stage[d]
        red[...] = acc
        out_ref[me] = acc
        second = [copy(red, out_ref.at[me], nr + j, peer) for j, (peer, pid) in enumerate(peers)]
        for cp in second:
            cp.start()
        for j, (peer, pid) in enumerate(peers):
            copy(red, out_ref.at[pid], nr + j, peer).wait_recv()
        for cp in first + second:
            cp.wait_send()

    vm = pl.BlockSpec(memory_space=pltpu.VMEM)
    return pl.pallas_call(
        body, name="allreduce_small", in_specs=[vm], out_specs=vm,
        out_shape=jax.ShapeDtypeStruct(vec.shape, F32),
        scratch_shapes=[pltpu.VMEM(vec.shape, F32), pltpu.VMEM((rows, lanes), F32),
                        pltpu.SemaphoreType.DMA((2 * nr,)), pltpu.SemaphoreType.DMA((2 * nr,))],
        compiler_params=_cparams(None, VMEM_MID))(vec)


def _adam_math(w, g, m, v):
    m2 = ADAM_B1 * m + (1.0 - ADAM_B1) * g
    v2 = ADAM_B2 * v + (1.0 - ADAM_B2) * (g * g)
    m_hat = m2 / (1.0 - ADAM_B1 ** ADAM_STEP)
    v_hat = v2 / (1.0 - ADAM_B2 ** ADAM_STEP)
    return -ADAM_LR * (m_hat / (jnp.sqrt(v_hat) + ADAM_EPS) + ADAM_WD * w), m2, v2


def _adamw_big(w3, m3, v3, layer, g, transposed, name, prev=None):
    nl, rows, cols = w3.shape
    tr = 128 if transposed else _tile(rows, (256, 176, 128))

    def body(w_ref, m_ref, v_ref, g_ref, *rest):
        go_ref, d_ref, mo_ref, vo_ref = rest[-4:]
        g_val = g_ref[...].T if transposed else g_ref[...]
        go_ref[...] = g_val
        d_ref[...], mo_ref[...], vo_ref[...] = _adam_math(w_ref[...], g_val, m_ref[...], v_ref[...])

    wspec = pl.BlockSpec((None, tr, cols), lambda i: (layer, i, 0))
    gspec = pl.BlockSpec((cols, tr), lambda i: (0, i)) if transposed else pl.BlockSpec((tr, cols), lambda i: (i, 0))
    extra = [] if prev is None else list(prev)
    return pl.pallas_call(body, name=name, grid=(rows // tr,),
                          in_specs=[wspec, wspec, wspec, gspec] + [_ANY] * len(extra),
                          out_specs=[wspec] * 4, out_shape=[jax.ShapeDtypeStruct((nl, rows, cols), F32)] * 4,
                          input_output_aliases={4 + i: i for i in range(len(extra))},
                          compiler_params=_cparams(("parallel",), VMEM_MID))(w3, m3, v3, g, *extra)


def _adamw_small(w, g, m, v):
    rows = w.shape[0]
    tr = _tile(rows, (512, 256, 128, 64, 32, 16, 8))

    def body(w_ref, g_ref, m_ref, v_ref, d_ref, mo_ref, vo_ref):
        d_ref[...], mo_ref[...], vo_ref[...] = _adam_math(w_ref[...], g_ref[...], m_ref[...], v_ref[...])

    spec = pl.BlockSpec((tr, LANES), lambda i: (i, 0))
    return pl.pallas_call(body, name="adamw_small", grid=(rows // tr,), in_specs=[spec] * 4, out_specs=[spec] * 3,
                          out_shape=[jax.ShapeDtypeStruct(w.shape, F32)] * 3,
                          compiler_params=_cparams(("parallel",)))(w, g, m, v)


def _pack(arrs, row_mult):
    flat = jnp.concatenate([a.reshape(-1).astype(F32) for a in arrs])
    rows = -(-flat.shape[0] // LANES)
    rows = -(-rows // row_mult) * row_mult
    return jnp.pad(flat, (0, rows * LANES - flat.shape[0])).reshape(rows, LANES)


def _unpack(packed, shapes):
    flat = packed.reshape(-1)
    out, off = [], 0
    for s in shapes:
        size = 1
        for d in s:
            size *= d
        out.append(flat[off:off + size].reshape(s))
        off += size
    return out


_SMALL_REP = ("e_norm_g", "e_mu", "e_w0", "e_a0", "e_k_k", "e_k_a", "e_r_k", "e_ln_w", "e_ln_b", "e_conv_b",
              "e_gate_a_w", "e_gate_a_b", "e_gate_x_w", "e_gate_x_b", "e_lru_lambda", "o_A_re", "o_A_im", "o_log_dt",
              "o_B_re", "o_B_im", "o_C_re", "o_C_im", "f_norm_g", "f_conv_b", "final_norm_g")
_SMALL_SH = ("e_w2", "e_a2", "e_g2", "e_conv_w", "o_norm_g", "o_D", "f_conv_w")
_LARGE = (("e_w_in", True), ("e_w_out", False), ("o_w_in", False), ("o_w_glu", True), ("f_w_up", True),
        ("f_w_down", False))
_ORDER = ("e_norm_g", "e_w_in", "e_mu", "e_w0", "e_w2", "e_a0", "e_a2", "e_g2", "e_k_k", "e_k_a", "e_r_k", "e_ln_w",
          "e_ln_b", "e_conv_w", "e_conv_b", "e_gate_a_w", "e_gate_a_b", "e_gate_x_w", "e_gate_x_b", "e_lru_lambda",
          "e_w_out", "o_norm_g", "o_w_in", "o_A_re", "o_A_im", "o_log_dt", "o_B_re", "o_B_im", "o_C_re", "o_C_im",
          "o_D", "o_w_glu", "f_norm_g", "f_w_up", "f_conv_w", "f_conv_b", "f_w_down", "final_norm_g")
N_CHIPS = 4
N_DEV = 8


def _step(x, tgt, wts, ms, vs):
    xi, yi, ci = _coords()
    chip = 2 * xi + yi
    chip1 = chip.astype(jnp.int32).reshape(1)
    me2 = jnp.stack([4 * xi + 2 * yi + ci, ci]).astype(jnp.int32)
    by_cols = dict(_LARGE)

    bufs = {(name, l): _cast_shard(wts[name], l, by_cols[name], chip1, f"cast_{name}{l}")
            for name, _ in _LARGE for l in range(wts[name].shape[0])}
    sh_shapes = [wts[n].shape for n in _SMALL_SH]
    packed = _pack([wts[n] for n in _SMALL_SH], 8)
    small_buf = lax.dynamic_update_slice(jnp.zeros((N_CHIPS,) + packed.shape, F32), packed[None], (chip, 0, 0))
    early = [("e_w_in", 0), ("e_w_out", 0)]
    late = [k for k in bufs if k not in early]
    send, recv, thru, token = _gather_start([bufs[k] for k in early] + [small_buf], "gather_start_a", x)
    got = _gather_wait(thru, send, recv, "gather_wait_a", token)
    send_b, recv_b, thru_b, token = _gather_start([bufs[k] for k in late], "gather_start_b", got[0])
    x, _ = lax.optimization_barrier((x, token))

    def rows(g):
        return g.reshape(N_CHIPS * g.shape[1], g.shape[2])

    full = {n: wts[n] for n in _SMALL_REP}
    full["e_w_in_t"], full["e_w_out"] = rows(got[0]), rows(got[1])
    per_chip = [_unpack(got[2][k], sh_shapes) for k in range(N_CHIPS)]
    for i, n in enumerate(_SMALL_SH):
        full[n] = jnp.concatenate([per_chip[k][i] for k in range(N_CHIPS)], axis=-1)

    def late_weights(after):
        res = dict(zip(late, _gather_wait(thru_b, send_b, recv_b, "gather_wait_b", after)))
        return {"o_w_in": rows(res[("o_w_in", 0)]), "o_w_glu_t": rows(res[("o_w_glu", 0)]),
                "f_w_up_t": [rows(res[("f_w_up", l)]) for l in range(2)],
                "f_w_down": [rows(res[("f_w_down", l)]) for l in range(2)]}

    pending = []

    def send_grads(tag, items, carry):
        srcs = [g.reshape(N_DEV, g.shape[0] // N_DEV, g.shape[1]) for _, _, g in items]
        s_sem, r_sem, both, tok = _scatter_start(srcs, f"scatter_start_{tag}", carry)
        pending.append((tag, [(name, l) for name, l, _ in items], s_sem, r_sem, both))
        carry, _ = lax.optimization_barrier((carry, tok))
        return carry

    loss, grad_x, gs = _local_step(x, tgt, full, late_weights, send_grads)

    halves, keys = [], []
    for tag, names, s_sem, r_sem, both in pending:
        srcs, lands = _scatter_wait(both, s_sem, r_sem, f"scatter_wait_{tag}", grad_x)
        for (name, l), src, land in zip(names, srcs, lands):
            halves.append(_sum_segments(src, land, me2, f"sum_{name}{l}"))
            keys.append((name, l))
    shards = _exchange_sibling(halves)
    final = {}
    for s, (name, l) in zip(shards, keys):
        final[name] = _adamw_big(wts[name], ms[name], vs[name], l, s.reshape(2 * s.shape[1], s.shape[2]),
                                 by_cols[name], f"adamw_{name}{l}", prev=final.get(name))

    small = _SMALL_REP + _SMALL_SH
    shapes = [gs[n].shape for n in small]
    red = _allreduce_small(_pack([gs[n] for n in small], 8 * N_DEV).reshape(N_DEV, -1, LANES))
    tot = dict(zip(small, _unpack(red, shapes)))
    for n in _SMALL_SH:
        width = wts[n].shape[-1]
        tot[n] = lax.dynamic_slice_in_dim(tot[n], chip * width, width, axis=tot[n].ndim - 1)
    loc_shapes = [wts[n].shape for n in small]
    pk = lambda d: _pack([d[n] for n in small], 8)
    delta, new_m, new_v = _adamw_small(pk(wts), pk(tot), pk(ms), pk(vs))
    for n, g, d, m2, v2 in zip(small, [tot[n] for n in small], _unpack(delta, loc_shapes), _unpack(new_m, loc_shapes),
                               _unpack(new_v, loc_shapes)):
        final[n] = [g.reshape(wts[n].shape), d, m2, v2]

    loss = lax.psum(loss[0, 0], ("x", "y", "c"))
    res = [loss, grad_x[None]]
    for k in range(4):
        res += [final[n][k] for n in _ORDER]
    return tuple(res)


def kernel(x, e_norm_g, e_w_in, e_mu, e_w0, e_w2, e_a0, e_a2, e_g2, e_k_k, e_k_a, e_r_k, e_ln_w, e_ln_b, e_conv_w, e_conv_b, e_gate_a_w, e_gate_a_b, e_gate_x_w, e_gate_x_b, e_lru_lambda, e_w_out, o_norm_g, o_w_in, o_A_re, o_A_im, o_log_dt, o_B_re, o_B_im, o_C_re, o_C_im, o_D, o_w_glu, f_norm_g, f_w_up, f_conv_w, f_conv_b, f_w_down, final_norm_g, loss_target, m_e_norm_g, m_e_w_in, m_e_mu, m_e_w0, m_e_w2, m_e_a0, m_e_a2, m_e_g2, m_e_k_k, m_e_k_a, m_e_r_k, m_e_ln_w, m_e_ln_b, m_e_conv_w, m_e_conv_b, m_e_gate_a_w, m_e_gate_a_b, m_e_gate_x_w, m_e_gate_x_b, m_e_lru_lambda, m_e_w_out, m_o_norm_g, m_o_w_in, m_o_A_re, m_o_A_im, m_o_log_dt, m_o_B_re, m_o_B_im, m_o_C_re, m_o_C_im, m_o_D, m_o_w_glu, m_f_norm_g, m_f_w_up, m_f_conv_w, m_f_conv_b, m_f_w_down, m_final_norm_g, v_e_norm_g, v_e_w_in, v_e_mu, v_e_w0, v_e_w2, v_e_a0, v_e_a2, v_e_g2, v_e_k_k, v_e_k_a, v_e_r_k, v_e_ln_w, v_e_ln_b, v_e_conv_w, v_e_conv_b, v_e_gate_a_w, v_e_gate_a_b, v_e_gate_x_w, v_e_gate_x_b, v_e_lru_lambda, v_e_w_out, v_o_norm_g, v_o_w_in, v_o_A_re, v_o_A_im, v_o_log_dt, v_o_B_re, v_o_B_im, v_o_C_re, v_o_C_im, v_o_D, v_o_w_glu, v_f_norm_g, v_f_w_up, v_f_conv_w, v_f_conv_b, v_f_w_down, v_final_norm_g):
    args = locals()
    wts = {n: args[n] for n in _ORDER}
    ms = {n: args["m_" + n] for n in _ORDER}
    vs = {n: args["v_" + n] for n in _ORDER}
    return _step(x[0], loss_target[0], wts, ms, vs)
```

```python
import functools

import jax
import jax.numpy as jnp
from jax import lax
from jax.experimental import pallas as pl
from jax.experimental.pallas import tpu as pltpu

F32 = jnp.float32
BF16 = jnp.bfloat16
MESH = pl.DeviceIdType.MESH

HEAD = 64
RW = 512
N_HEADS = RW // HEAD
LRU_W = 512
SHIFT_COLS = 1792
W_LORA, A_LORA, G_LORA = 64, 64, 128
S5_GROUPS, S5_GROUP, S5_STATE = 64, 16, 64
D_FF = 2816
NORM_EPS = 1e-6
GN_EPS = 64e-5
LRU_C = 8.0
ADAM_LR, ADAM_B1, ADAM_B2, ADAM_EPS, ADAM_WD, ADAM_STEP = 0.001, 0.9, 0.999, 1e-08, 0.01, 10

VMEM_BIG = 56 * 1024 * 1024
VMEM_MID = 40 * 1024 * 1024
LANES = 128
PT = 16
WKV_CHUNK = 32
S5_SLAB = 128


def _cparams(sem=None, vmem=None):
    kw = {}
    if sem is not None:
        kw["dimension_semantics"] = sem
    if vmem is not None:
        kw["vmem_limit_bytes"] = vmem
    return pltpu.CompilerParams(**kw)


def _tile(dim, cands):
    for c in cands:
        if dim % c == 0:
            return c
    return dim


def _full(shape):
    n = len(shape)
    return pl.BlockSpec(shape, lambda *_: (0,) * n)


_TILES = (1408, 1024, 512, 256, 128)


def _matmul(a, b, mode, name, out_dtype=F32, add=None):
    if mode == "nn":
        (m, k), (k2, n) = a.shape, b.shape
    elif mode == "nt":
        (m, k), (n, k2) = a.shape, b.shape
    else:
        (k, m), (k2, n) = a.shape, b.shape
    assert k == k2, (a.shape, b.shape, mode)
    if mode == "tn":
        tm, tn, tk = _tile(m, _TILES), _tile(n, (1024, 512, 256, 128)), _tile(k, (512, 256, 128))
    else:
        tm, tn, tk = _tile(m, (512, 256, 128)), _tile(n, _TILES), _tile(k, _TILES)
    nk = k // tk
    dims = {"nn": (((1,), (0,)), ((), ())), "nt": (((1,), (1,)), ((), ())), "tn": (((0,), (0,)), ((), ()))}[mode]

    def body(*refs):
        if add is None:
            a_ref, b_ref, o_ref, acc = refs
            add_ref = None
        else:
            a_ref, b_ref, add_ref, o_ref, acc = refs
        kk = pl.program_id(2)

        @pl.when(kk == 0)
        def _():
            acc[...] = jnp.zeros_like(acc)

        acc[...] += lax.dot_general(a_ref[...].astype(BF16), b_ref[...].astype(BF16), dims,
                                    preferred_element_type=F32)

        @pl.when(kk == nk - 1)
        def _():
            r = acc[...]
            if add_ref is not None:
                r = r + add_ref[...]
            o_ref[...] = r.astype(o_ref.dtype)

    if mode == "nn":
        a_spec = pl.BlockSpec((tm, tk), lambda i, j, kk: (i, kk))
        b_spec = pl.BlockSpec((tk, tn), lambda i, j, kk: (kk, j))
    elif mode == "nt":
        a_spec = pl.BlockSpec((tm, tk), lambda i, j, kk: (i, kk))
        b_spec = pl.BlockSpec((tn, tk), lambda i, j, kk: (j, kk))
    else:
        a_spec = pl.BlockSpec((tk, tm), lambda i, j, kk: (kk, i))
        b_spec = pl.BlockSpec((tk, tn), lambda i, j, kk: (kk, j))
    o_spec = pl.BlockSpec((tm, tn), lambda i, j, kk: (i, j))
    in_specs = [a_spec, b_spec] + ([o_spec] if add is not None else [])
    args = (a, b) + ((add,) if add is not None else ())
    return pl.pallas_call(
        body, name=name, grid=(m // tm, n // tn, nk),
        in_specs=in_specs, out_specs=o_spec,
        out_shape=jax.ShapeDtypeStruct((m, n), out_dtype),
        scratch_shapes=[pltpu.VMEM((tm, tn), F32)],
        compiler_params=_cparams(("parallel", "parallel", "arbitrary"), VMEM_MID),
    )(*args)


TOK = 256


def _rms(x, g):
    return x * lax.rsqrt(jnp.mean(x * x, axis=-1, keepdims=True) + NORM_EPS) * g


def _rms_fwd(x, g, name):
    t, d = x.shape

    def body(x_ref, g_ref, o_ref):
        o_ref[...] = _rms(x_ref[...], g_ref[...]).astype(BF16)

    row = pl.BlockSpec((TOK, d), lambda i: (i, 0))
    return pl.pallas_call(body, name=name, grid=(t // TOK,), in_specs=[row, _full((1, d))], out_specs=row,
                          out_shape=jax.ShapeDtypeStruct((t, d), BF16),
                          compiler_params=_cparams(("parallel",)))(x, g)


def _rms_bwd(x, g, dxn, res, name):
    t, d = x.shape

    def body(x_ref, g_ref, d_ref, res_ref, dx_ref, dg_ref):
        _, vjp = jax.vjp(_rms, x_ref[...], g_ref[...])
        dx, dg = vjp(d_ref[...].astype(F32))
        dx_ref[...] = dx + res_ref[...]

        @pl.when(pl.program_id(0) == 0)
        def _():
            dg_ref[...] = jnp.zeros_like(dg_ref)

        dg_ref[...] += dg

    row = pl.BlockSpec((TOK, d), lambda i: (i, 0))
    return pl.pallas_call(body, name=name, grid=(t // TOK,), in_specs=[row, _full((1, d)), row, row],
                          out_specs=[row, _full((1, d))],
                          out_shape=[jax.ShapeDtypeStruct((t, d), F32), jax.ShapeDtypeStruct((1, d), F32)],
                          compiler_params=_cparams(("arbitrary",)))(x, g, dxn, res)


def _loss_head(x, g, tgt):
    t, d = x.shape

    def body(x_ref, g_ref, t_ref, l_ref, dx_ref, dg_ref):
        tg = t_ref[...]

        def fn(xv, gv):
            err = _rms(xv, gv) - tg
            per_tok = jnp.mean(err * err, axis=-1, keepdims=True)
            return 0.5 * jnp.sum(per_tok, axis=0, keepdims=True)

        l, vjp = jax.vjp(fn, x_ref[...], g_ref[...])
        dx, dg = vjp(jnp.ones((1, 1), F32))
        dx_ref[...] = dx

        @pl.when(pl.program_id(0) == 0)
        def _():
            dg_ref[...] = jnp.zeros_like(dg_ref)
            l_ref[...] = jnp.zeros_like(l_ref)

        dg_ref[...] += dg
        l_ref[...] += jnp.broadcast_to(l, l_ref.shape)

    row = pl.BlockSpec((TOK, d), lambda i: (i, 0))
    return pl.pallas_call(body, name="loss_head", grid=(t // TOK,), in_specs=[row, _full((1, d)), row],
                          out_specs=[_full((1, LANES)), row, _full((1, d))],
                          out_shape=[jax.ShapeDtypeStruct((1, LANES), F32), jax.ShapeDtypeStruct((t, d), F32),
                                     jax.ShapeDtypeStruct((1, d), F32)],
                          compiler_params=_cparams(("arbitrary",)))(x, g, tgt)


def _glu_fwd(x, z):
    t, d = x.shape

    def body(x_ref, v_ref, g_ref, o_ref):
        o_ref[...] = x_ref[...] + v_ref[...] * jax.nn.sigmoid(g_ref[...])

    row = pl.BlockSpec((TOK, d), lambda i: (i, 0))
    gate = pl.BlockSpec((TOK, d), lambda i: (i, 1))
    return pl.pallas_call(body, name="glu_fwd", grid=(t // TOK,), in_specs=[row, row, gate], out_specs=row,
                          out_shape=jax.ShapeDtypeStruct((t, d), F32),
                          compiler_params=_cparams(("parallel",)))(x, z, z)


def _glu_bwd(z, g):
    t, d = g.shape

    def body(v_ref, g_ref, d_ref, o_ref):
        s = jax.nn.sigmoid(g_ref[...])
        dy = d_ref[...]
        o_ref[:, :d] = (dy * s).astype(BF16)
        o_ref[:, d:] = (dy * v_ref[...] * s * (1.0 - s)).astype(BF16)

    row = pl.BlockSpec((TOK, d), lambda i: (i, 0))
    gate = pl.BlockSpec((TOK, d), lambda i: (i, 1))
    return pl.pallas_call(body, name="glu_bwd", grid=(t // TOK,), in_specs=[row, gate, row],
                          out_specs=pl.BlockSpec((TOK, 2 * d), lambda i: (i, 0)),
                          out_shape=jax.ShapeDtypeStruct((t, 2 * d), BF16),
                          compiler_params=_cparams(("parallel",)))(z, z, g)


def _shift_down(x, d):
    row = lax.broadcasted_iota(jnp.int32, x.shape, 0)
    return jnp.where(row < d, 0.0, pltpu.roll(x, d, 0))


def _shift_up(x, d):
    n = x.shape[0]
    row = lax.broadcasted_iota(jnp.int32, x.shape, 0)
    return jnp.where(row >= n - d, 0.0, pltpu.roll(x, n - d, 0))


def _make_sd():
    @functools.partial(jax.custom_vjp, nondiff_argnums=(1,))
    def sd(x, d):
        return _shift_down(x, d)

    def fwd(x, d):
        return _shift_down(x, d), None

    def bwd(d, _, g):
        return (_shift_up(g, d),)

    sd.defvjp(fwd, bwd)
    return sd


def _lin_scan(a, u, reverse=False):
    n = a.shape[0]
    row = lax.broadcasted_iota(jnp.int32, a.shape, 0)
    d = 1
    while d < n:
        if reverse:
            keep = row < n - d
            a_s, u_s = pltpu.roll(a, n - d, 0), pltpu.roll(u, n - d, 0)
        else:
            keep = row >= d
            a_s, u_s = pltpu.roll(a, d, 0), pltpu.roll(u, d, 0)
        u = u + a * jnp.where(keep, u_s, 0.0)
        a = a * jnp.where(keep, a_s, 1.0)
        d *= 2
    return u


def _make_scan():
    @jax.custom_vjp
    def scan(a, u):
        return _lin_scan(a, u)

    def fwd(a, u):
        h = _lin_scan(a, u)
        return h, (a, h)

    def bwd(res, dh):
        a, h = res
        g = _lin_scan(_shift_up(a, 1), dh, reverse=True)
        return g * _shift_down(h, 1), g

    scan.defvjp(fwd, bwd)
    return scan


def _acc_out(ref, val):
    @pl.when(pl.program_id(0) == 0)
    def _():
        ref[...] = jnp.zeros_like(ref)

    ref[...] += val


FFN_CW = 128


def _ffn_fn(hg, hv, wg, wv, bg, bv, sd):
    cg = wg[0:1] * sd(hg, 2) + wg[1:2] * sd(hg, 1) + wg[2:3] * hg + bg
    cv = wv[0:1] * sd(hv, 2) + wv[1:2] * sd(hv, 1) + wv[2:3] * hv + bv
    return jax.nn.silu(cg) * cv


def _ffn_specs(t):
    nb = D_FF // FFN_CW
    col = lambda r, off: pl.BlockSpec((r, FFN_CW), lambda j: (0, j + off))
    return nb, [col(t, 0), col(t, nb), col(3, 0), col(3, nb), col(1, 0), col(1, nb)], col


def _ffn_mid_fwd(h, cw, cb, name):
    t = h.shape[0]
    nb, in_specs, col = _ffn_specs(t)

    def body(hg, hv, wg, wv, bg, bv, o_ref):
        o_ref[...] = _ffn_fn(hg[...], hv[...], wg[...], wv[...], bg[...], bv[...], _shift_down).astype(BF16)

    return pl.pallas_call(body, name=name, grid=(nb,), in_specs=in_specs, out_specs=col(t, 0),
                          out_shape=jax.ShapeDtypeStruct((t, D_FF), BF16),
                          compiler_params=_cparams(("parallel",), VMEM_MID))(h, h, cw, cw, cb, cb)


def _ffn_mid_bwd(h, cw, cb, dact, name):
    t = h.shape[0]
    nb, in_specs, col = _ffn_specs(t)

    def body(hg, hv, wg, wv, bg, bv, d_ref, dhg, dhv, dwg, dwv, dbg, dbv):
        fn = functools.partial(_ffn_fn, sd=_make_sd())
        _, vjp = jax.vjp(fn, hg[...], hv[...], wg[...], wv[...], bg[...], bv[...])
        g = vjp(d_ref[...])
        dhg[...] = g[0].astype(BF16)
        dhv[...] = g[1].astype(BF16)
        dwg[...], dwv[...], dbg[...], dbv[...] = g[2], g[3], g[4], g[5]

    big = jax.ShapeDtypeStruct((t, D_FF), BF16)
    w3 = jax.ShapeDtypeStruct((3, D_FF), F32)
    b1 = jax.ShapeDtypeStruct((1, D_FF), F32)
    return pl.pallas_call(body, name=name, grid=(nb,), in_specs=in_specs + [col(t, 0)],
                          out_specs=[col(t, 0), col(t, 0), col(3, 0), col(3, 0), col(1, 0), col(1, 0)],
                          out_shape=[big, big, w3, w3, b1, b1],
                          compiler_params=_cparams(("parallel",), VMEM_BIG))(h, h, cw, cw, cb, cb, dact)


TS_CW = 256


def _tshift_fn(p, mu, sd):
    return p + mu * (sd(p, 1) - p)


def _tshift_fwd(p, mu):
    t = p.shape[0]
    col = lambda r: pl.BlockSpec((r, TS_CW), lambda j: (0, j))

    def body(p_ref, mu_ref, o_ref):
        o_ref[...] = _tshift_fn(p_ref[...], mu_ref[...], _shift_down)

    return pl.pallas_call(body, name="tshift_fwd", grid=(SHIFT_COLS // TS_CW,), in_specs=[col(t), col(1)],
                          out_specs=col(t), out_shape=jax.ShapeDtypeStruct((t, SHIFT_COLS), F32),
                          compiler_params=_cparams(("parallel",), VMEM_MID))(p, mu)


def _tshift_bwd(p, mu, dpam):
    t = p.shape[0]
    col = lambda r: pl.BlockSpec((r, TS_CW), lambda j: (0, j))

    def body(p_ref, mu_ref, d_ref, dp_ref, dmu_ref):
        _, vjp = jax.vjp(functools.partial(_tshift_fn, sd=_make_sd()), p_ref[...], mu_ref[...])
        dp, dmu = vjp(d_ref[...])
        dp_ref[...] = dp.astype(BF16)
        dmu_ref[...] = dmu

    return pl.pallas_call(body, name="tshift_bwd", grid=(SHIFT_COLS // TS_CW,), in_specs=[col(t), col(1), col(t)],
                          out_specs=[col(t), col(1)],
                          out_shape=[jax.ShapeDtypeStruct((t, SHIFT_COLS), BF16),
                                     jax.ShapeDtypeStruct((1, SHIFT_COLS), F32)],
                          compiler_params=_cparams(("parallel",), VMEM_MID))(p, mu, dpam)


_HI = lax.Precision.HIGHEST
_O = (0, RW, 2 * RW, 3 * RW, 3 * RW + W_LORA, 3 * RW + W_LORA + A_LORA, SHIFT_COLS)


def _seg(x, gm):
    return jnp.dot(x, gm, precision=_HI)


def _prep_fn(r, k, v, wd, ad, gd, w0, w2, a0, a2, g2, k_k, k_a, gm):
    w_log = -jax.nn.softplus(-(w0 + jnp.tanh(wd) @ w2)) - 0.5
    decay = jnp.exp(-jnp.exp(w_log))
    a = jax.nn.sigmoid(a0 + ad @ a2)
    g = jax.nn.sigmoid(gd) @ g2
    kk = k * k_k
    kk = kk / jnp.maximum(jnp.sqrt(_seg(kk * kk, gm)), 1e-12)
    k2 = k * (1.0 + (a - 1.0) * k_a)
    return r, decay, k2, v, -kk, kk * a, g


_PREP_W = ("w0", "w2", "a0", "a2", "g2", "k_k", "k_a")


def _prep_wspecs(w):
    return [_full(w[n].shape) for n in _PREP_W] + [_full((RW, RW))]


def _rwkv_prep_fwd(pam, w, gm):
    t = pam.shape[0]

    def body(p_ref, *refs):
        wr, outs = refs[:8], refs[8:]
        pieces = [p_ref[:, _O[i]:_O[i + 1]] for i in range(6)]
        res = _prep_fn(*pieces, *[x[...] for x in wr])
        for o, val in zip(outs, res):
            o[...] = val

    row = lambda c: pl.BlockSpec((TOK, c), lambda i: (i, 0))
    return pl.pallas_call(body, name="rwkv_prep_fwd", grid=(t // TOK,),
                          in_specs=[row(SHIFT_COLS)] + _prep_wspecs(w), out_specs=[row(RW)] * 7,
                          out_shape=[jax.ShapeDtypeStruct((t, RW), F32)] * 7,
                          compiler_params=_cparams(("parallel",), VMEM_MID))(pam, *[w[n] for n in _PREP_W], gm)


def _rwkv_prep_bwd(pam, w, gm, cts, more):
    t = pam.shape[0]

    def body(p_ref, *refs):
        wr, ct, ex, dp_ref, dws = refs[:8], refs[8:15], refs[15:18], refs[18], refs[19:]
        pieces = [p_ref[:, _O[i]:_O[i + 1]] for i in range(6)]
        fn = lambda *a: _prep_fn(*a, wr[7][...])
        _, vjp = jax.vjp(fn, *pieces, *[x[...] for x in wr[:7]])
        c = [x[...] for x in ct]
        c[0] = c[0] + ex[0][...]
        c[2] = c[2] + ex[1][...]
        c[3] = c[3] + ex[2][...]
        g = vjp(tuple(c))
        for i in range(6):
            dp_ref[:, _O[i]:_O[i + 1]] = g[i]
        for o, val in zip(dws, g[6:]):
            _acc_out(o, val)

    row = lambda c: pl.BlockSpec((TOK, c), lambda i: (i, 0))
    return pl.pallas_call(body, name="rwkv_prep_bwd", grid=(t // TOK,),
                          in_specs=[row(SHIFT_COLS)] + _prep_wspecs(w) + [row(RW)] * 10,
                          out_specs=[row(SHIFT_COLS)] + [_full(w[n].shape) for n in _PREP_W],
                          out_shape=[jax.ShapeDtypeStruct((t, SHIFT_COLS), F32)]
                          + [jax.ShapeDtypeStruct(w[n].shape, F32) for n in _PREP_W],
                          compiler_params=_cparams(("arbitrary",), VMEM_MID))(
                              pam, *[w[n] for n in _PREP_W], gm, *cts, *more)


def _post_fn(y, r, k2, v, g, ln_w, ln_b, r_k, gm):
    inv = 1.0 / HEAD
    d = y - _seg(y, gm) * inv
    yn = d * lax.rsqrt(_seg(d * d, gm) * inv + GN_EPS) * ln_w + ln_b
    bonus = _seg(r * k2 * r_k, gm) * v
    return (yn + bonus) * g


def _rwkv_post_fwd(y, r, k2, v, g, ln_w, ln_b, r_k, gm):
    t = y.shape[0]

    def body(*refs):
        o_ref = refs[-1]
        o_ref[...] = _post_fn(*[x[...] for x in refs[:-1]]).astype(BF16)

    row = pl.BlockSpec((TOK, RW), lambda i: (i, 0))
    return pl.pallas_call(body, name="rwkv_post_fwd", grid=(t // TOK,),
                          in_specs=[row] * 5 + [_full((1, RW))] * 3 + [_full((RW, RW))], out_specs=row,
                          out_shape=jax.ShapeDtypeStruct((t, RW), BF16),
                          compiler_params=_cparams(("parallel",), VMEM_MID))(y, r, k2, v, g, ln_w, ln_b, r_k, gm)


def _rwkv_post_bwd(y, r, k2, v, g, ln_w, ln_b, r_k, gm, dya):
    t = y.shape[0]

    def body(*refs):
        ins, gm_ref, d_ref, outs = refs[:8], refs[8], refs[9], refs[10:]
        fn = lambda *a: _post_fn(*a, gm_ref[...])
        _, vjp = jax.vjp(fn, *[x[...] for x in ins])
        gr = vjp(d_ref[...])
        for o, val in zip(outs[:5], gr[:5]):
            o[...] = val
        for o, val in zip(outs[5:], gr[5:]):
            _acc_out(o, val)

    row = pl.BlockSpec((TOK, RW), lambda i: (i, 0))
    vec = _full((1, RW))
    return pl.pallas_call(body, name="rwkv_post_bwd", grid=(t // TOK,),
                          in_specs=[row] * 5 + [vec] * 3 + [_full((RW, RW)), row],
                          out_specs=[row] * 5 + [vec] * 3,
                          out_shape=[jax.ShapeDtypeStruct((t, RW), F32)] * 5 + [jax.ShapeDtypeStruct((1, RW), F32)] * 3,
                          compiler_params=_cparams(("arbitrary",), VMEM_MID))(y, r, k2, v, g, ln_w, ln_b, r_k, gm, dya)


def _from_pt(x):
    n = x.shape[0]
    return x.reshape(n, HEAD, N_HEADS, PT).transpose(0, 3, 2, 1).reshape(n * PT, N_HEADS * HEAD)


def _lane_sum(x):
    return jnp.sum(x, axis=-1, keepdims=True)


def _pair_consts():
    lane = lax.broadcasted_iota(jnp.int32, (HEAD, LANES), 1)
    return lane, lane < HEAD


def _seg_sum_pair(x, first):
    return jnp.where(first, _lane_sum(jnp.where(first, x, 0.0)), _lane_sum(jnp.where(first, 0.0, x)))


def _to_pt(x):
    t = x.shape[0]
    return x.reshape(t // PT, PT, N_HEADS, HEAD).transpose(0, 3, 2, 1).reshape(t // PT, HEAD, N_HEADS * PT)


def _expand_cols(x, name):
    t = x.shape[0]
    tiles = WKV_CHUNK // PT

    def body(x_ref, o_ref):
        _, first = _pair_consts()
        for tl in range(tiles):
            tile = x_ref[tl]
            for j in range(PT):
                for p in range(N_HEADS // 2):
                    src = jnp.where(first, (2 * p) * PT + j, (2 * p + 1) * PT + j)
                    o_ref[tl * PT + j, :, p * LANES:(p + 1) * LANES] = jnp.take_along_axis(tile, src, axis=1)

    return pl.pallas_call(
        body, name=name, grid=(t // WKV_CHUNK,),
        in_specs=[pl.BlockSpec((tiles, HEAD, LANES), lambda i: (i, 0, 0))],
        out_specs=pl.BlockSpec((WKV_CHUNK, HEAD, RW), lambda i: (i, 0, 0)),
        out_shape=jax.ShapeDtypeStruct((t, HEAD, RW), F32),
        compiler_params=_cparams(("parallel",), VMEM_MID))(_to_pt(x))


def _wkv_fwd(w, k, z, b, v_exp):
    t = w.shape[0]
    nc = t // WKV_CHUNK
    pairs = N_HEADS // 2

    def body(w_ref, k_ref, z_ref, b_ref, v_ref, s_all, s_ref):
        @pl.when(pl.program_id(0) == 0)
        def _():
            s_ref[...] = jnp.zeros_like(s_ref)

        _, first = _pair_consts()

        def group(gi, carry):
            base = pl.multiple_of(gi * 8, 8)
            rows = [ref[pl.ds(base, 8), :] for ref in (w_ref, k_ref, z_ref, b_ref)]
            s = [s_ref[:, p * LANES:(p + 1) * LANES] for p in range(pairs)]
            for jj in range(8):
                for p in range(pairs):
                    cs = slice(p * LANES, (p + 1) * LANES)
                    wr, kr, zr, br = [x[jj:jj + 1, cs] for x in rows]
                    s_all[base + jj, :, cs] = s[p]
                    sa = _seg_sum_pair(s[p] * zr, first)
                    s[p] = s[p] * wr + sa * br + v_ref[base + jj, :, cs] * kr
            for p in range(pairs):
                s_ref[:, p * LANES:(p + 1) * LANES] = s[p]
            return carry

        lax.fori_loop(0, WKV_CHUNK // 8, group, 0)

    row = pl.BlockSpec((WKV_CHUNK, RW), lambda i: (i, 0))
    big = pl.BlockSpec((WKV_CHUNK, HEAD, RW), lambda i: (i, 0, 0))
    return pl.pallas_call(
        body, name="wkv_fwd", grid=(nc,), in_specs=[row] * 4 + [big], out_specs=[big, _full((HEAD, RW))],
        out_shape=[jax.ShapeDtypeStruct((t, HEAD, RW), F32), jax.ShapeDtypeStruct((HEAD, RW), F32)],
        compiler_params=_cparams(("arbitrary",), VMEM_MID))(w, k, z, b, v_exp)


def _wkv_out(r, s_all, s_last):
    t = r.shape[0]
    nc = t // WKV_CHUNK
    tiles = WKV_CHUNK // PT
    pairs = N_HEADS // 2

    def body(r_ref, s_ref, nxt_ref, last_ref, y_ref):
        lane, first = _pair_consts()
        after = jnp.where(pl.program_id(0) == nc - 1, last_ref[...], nxt_ref[0])
        for tl in range(tiles):
            ytile = jnp.zeros((HEAD, LANES), F32)
            for g in range(PT // 8):
                rows = r_ref[tl * PT + g * 8:tl * PT + g * 8 + 8, :]
                for jj in range(8):
                    tt = tl * PT + g * 8 + jj
                    j = g * 8 + jj
                    for p in range(pairs):
                        cs = slice(p * LANES, (p + 1) * LANES)
                        s = s_ref[tt + 1, :, cs] if tt + 1 < WKV_CHUNK else after[:, cs]
                        pr = s * rows[jj:jj + 1, cs]
                        y0 = _lane_sum(jnp.where(first, pr, 0.0))
                        y1 = _lane_sum(jnp.where(first, 0.0, pr))
                        ytile = jnp.where(lane == (2 * p) * PT + j, y0, ytile)
                        ytile = jnp.where(lane == (2 * p + 1) * PT + j, y1, ytile)
            y_ref[tl] = ytile

    row = pl.BlockSpec((WKV_CHUNK, RW), lambda i: (i, 0))
    pt = pl.BlockSpec((tiles, HEAD, LANES), lambda i: (i, 0, 0))
    big = pl.BlockSpec((WKV_CHUNK, HEAD, RW), lambda i: (i, 0, 0))
    nxt = pl.BlockSpec((1, HEAD, RW), lambda i: (jnp.minimum((i + 1) * WKV_CHUNK, t - 1), 0, 0))
    return pl.pallas_call(
        body, name="wkv_out", grid=(nc,), in_specs=[row, big, nxt, _full((HEAD, RW))], out_specs=pt,
        out_shape=jax.ShapeDtypeStruct((t // PT, HEAD, LANES), F32),
        compiler_params=_cparams(("parallel",), VMEM_MID))(r, s_all, s_all, s_last)


def _wkv_bwd(r, w, k, z, b, v_exp, s_all, dy_exp):
    t = r.shape[0]
    nc = t // WKV_CHUNK
    tiles = WKV_CHUNK // PT
    pairs = N_HEADS // 2

    def body(r_ref, w_ref, k_ref, z_ref, b_ref, v_ref, s_all_ref, dy_ref,
             dr_ref, dw_ref, dk_ref, dz_ref, db_ref, dv_ref, ds_ref):
        @pl.when(pl.program_id(0) == 0)
        def _():
            ds_ref[...] = jnp.zeros_like(ds_ref)

        lane, first = _pair_consts()
        col_sum = lambda x: jnp.sum(x, axis=0, keepdims=True)
        row8 = lax.broadcasted_iota(jnp.int32, (8, LANES), 0)
        for tl in reversed(range(tiles)):
            def group(gg, dvtile):
                gi = PT // 8 - 1 - gg
                base = pl.multiple_of(tl * PT + gi * 8, 8)
                rows = [ref[pl.ds(base, 8), :] for ref in (r_ref, w_ref, k_ref, z_ref, b_ref)]
                outs = (dr_ref, dw_ref, dk_ref, dz_ref, db_ref)
                tiles8 = {(id(o), p): jnp.zeros((8, LANES), F32) for o in outs for p in range(pairs)}
                ds = [ds_ref[:, p * LANES:(p + 1) * LANES] for p in range(pairs)]
                for jj in reversed(range(8)):
                    j = gi * 8 + jj
                    for p in range(pairs):
                        cs = slice(p * LANES, (p + 1) * LANES)

                        def put(ref, val, p=p, jj=jj):
                            tiles8[(id(ref), p)] = jnp.where(row8 == jj, val, tiles8[(id(ref), p)])

                        rr, wr, kr, zr, br = [x[jj:jj + 1, cs] for x in rows]
                        sp = s_all_ref[base + jj, :, cs]
                        vc = v_ref[base + jj, :, cs]
                        dyc = dy_ref[base + jj, :, cs]
                        sa = _seg_sum_pair(sp * zr, first)
                        st = sp * wr + sa * br + vc * kr
                        d = ds[p] + dyc * rr
                        put(dr_ref, col_sum(st * dyc))
                        dvk = d * kr
                        dv0 = _lane_sum(jnp.where(first, dvk, 0.0))
                        dv1 = _lane_sum(jnp.where(first, 0.0, dvk))
                        dvtile = jnp.where(lane == (2 * p) * PT + j, dv0, dvtile)
                        dvtile = jnp.where(lane == (2 * p + 1) * PT + j, dv1, dvtile)
                        put(dk_ref, col_sum(d * vc))
                        put(dw_ref, col_sum(sp * d))
                        u = _seg_sum_pair(d * br, first)
                        put(dz_ref, col_sum(sp * u))
                        put(db_ref, col_sum(d * sa))
                        ds[p] = d * wr + u * zr
                for p in range(pairs):
                    ds_ref[:, p * LANES:(p + 1) * LANES] = ds[p]
                for o in outs:
                    for p in range(pairs):
                        o[pl.ds(base, 8), p * LANES:(p + 1) * LANES] = tiles8[(id(o), p)]
                return dvtile

            dv_ref[tl] = lax.fori_loop(0, PT // 8, group, jnp.zeros((HEAD, LANES), F32))

    rev = lambda i: nc - 1 - i
    row = pl.BlockSpec((WKV_CHUNK, RW), lambda i: (rev(i), 0))
    pt = pl.BlockSpec((tiles, HEAD, LANES), lambda i: (rev(i), 0, 0))
    big = pl.BlockSpec((WKV_CHUNK, HEAD, RW), lambda i: (rev(i), 0, 0))
    return pl.pallas_call(
        body, name="wkv_bwd", grid=(nc,), in_specs=[row] * 5 + [big, big, big], out_specs=[row] * 5 + [pt],
        out_shape=[jax.ShapeDtypeStruct((t, RW), F32)] * 5 + [jax.ShapeDtypeStruct((t // PT, HEAD, LANES), F32)],
        scratch_shapes=[pltpu.VMEM((HEAD, RW), F32)],
        compiler_params=_cparams(("arbitrary",), VMEM_BIG))(r, w, k, z, b, v_exp, s_all, dy_exp)


LRU_CW = 128
_BX0 = SHIFT_COLS // LRU_CW
_BG0 = (SHIFT_COLS + LRU_W) // LRU_CW


def _lru_fn(bx, bg, cw, cb, ga, ba, gx, bxb, lam, sd, scan):
    xc = cw[0:1] * sd(bx, 3) + cw[1:2] * sd(bx, 2) + cw[2:3] * sd(bx, 1) + cw[3:4] * bx + cb
    gr = jax.nn.sigmoid(xc @ ga + ba)
    gi = jax.nn.sigmoid(xc @ gx + bxb)
    log_a = -LRU_C * gr * jax.nn.softplus(-lam)
    a = jnp.exp(log_a)
    mult = jnp.sqrt(-jnp.tanh(log_a) * (jnp.exp(2.0 * log_a) + 1.0))
    return scan(a, xc * gi * mult) * jax.nn.gelu(bg)


def _lru_specs(t):
    col = lambda r, off=0: pl.BlockSpec((r, LRU_CW), lambda j: (0, j + off))
    diag = pl.BlockSpec((LRU_CW, LRU_CW), lambda j: (j, j))
    return col, [col(t, _BX0), col(t, _BG0), col(4), col(1), diag, col(1), diag, col(1), col(1)]


def _lru_fwd(p, cw, cb, ga, ba, gx, bxb, lam):
    t = p.shape[0]
    col, in_specs = _lru_specs(t)

    def body(*refs):
        o_ref = refs[-1]
        o_ref[...] = _lru_fn(*[x[...] for x in refs[:-1]], _shift_down, _lin_scan).astype(BF16)

    return pl.pallas_call(body, name="lru_fwd", grid=(LRU_W // LRU_CW,), in_specs=in_specs, out_specs=col(t),
                          out_shape=jax.ShapeDtypeStruct((t, LRU_W), BF16),
                          compiler_params=_cparams(("parallel",), VMEM_MID))(p, p, cw, cb, ga, ba, gx, bxb, lam)


def _lru_bwd(p, cw, cb, ga, ba, gx, bxb, lam, dyb):
    t = p.shape[0]
    col, in_specs = _lru_specs(t)

    def body(*refs):
        ins, d_ref, outs = refs[:9], refs[9], refs[10:]
        fn = functools.partial(_lru_fn, sd=_make_sd(), scan=_make_scan())
        _, vjp = jax.vjp(fn, *[x[...] for x in ins])
        g = vjp(d_ref[...])
        outs[0][...] = g[0].astype(BF16)
        outs[1][...] = g[1].astype(BF16)
        for o, val in zip(outs[2:], g[2:]):
            o[...] = val

    sq = pl.BlockSpec((LRU_CW, LRU_CW), lambda j: (j, 0))
    act = jax.ShapeDtypeStruct((t, LRU_W), BF16)
    vec = jax.ShapeDtypeStruct((1, LRU_W), F32)
    sqs = jax.ShapeDtypeStruct((LRU_W, LRU_CW), F32)
    return pl.pallas_call(body, name="lru_bwd", grid=(LRU_W // LRU_CW,), in_specs=in_specs + [col(t, RW // LRU_CW)],
                          out_specs=[col(t), col(t), col(4), col(1), sq, col(1), sq, col(1), col(1)],
                          out_shape=[act, act, jax.ShapeDtypeStruct((4, LRU_W), F32), vec, sqs, vec, sqs, vec, vec],
                          compiler_params=_cparams(("parallel",), VMEM_BIG))(p, p, cw, cb, ga, ba, gx, bxb, lam, dyb)


def _s5_disc_fn(a_re, a_im, log_dt, b_re, b_im, e):
    lam_re = jnp.minimum(a_re, -1e-4)
    lam_im = a_im
    dt = jnp.exp(log_dt)
    mag = jnp.exp(lam_re * dt)
    ab_re = mag * jnp.cos(lam_im * dt)
    ab_im = mag * jnp.sin(lam_im * dt)
    den = lam_re * lam_re + lam_im * lam_im
    zr = ab_re - 1.0
    q_re = jnp.dot((zr * lam_re + ab_im * lam_im) / den, e, precision=_HI)
    q_im = jnp.dot((ab_im * lam_re - zr * lam_im) / den, e, precision=_HI)
    return ab_re, ab_im, q_re * b_re - q_im * b_im, q_re * b_im + q_im * b_re


def _s5_disc_fwd(a_re, a_im, log_dt, b_re, b_im, e):
    def body(*refs):
        res = _s5_disc_fn(*[x[...] for x in refs[:6]])
        for o, val in zip(refs[6:], res):
            o[...] = val

    small = jax.ShapeDtypeStruct(a_re.shape, F32)
    wide = jax.ShapeDtypeStruct(b_re.shape, F32)
    return pl.pallas_call(body, name="s5_disc_fwd", out_shape=[small, small, wide, wide])(
        a_re, a_im, log_dt, b_re, b_im, e)


def _s5_disc_bwd(a_re, a_im, log_dt, b_re, b_im, e, cts):
    def body(*refs):
        ins, e_ref, ct, outs = refs[:5], refs[5], refs[6:10], refs[10:]
        _, vjp = jax.vjp(lambda *a: _s5_disc_fn(*a, e_ref[...]), *[x[...] for x in ins])
        for o, val in zip(outs, vjp(tuple(c[...] for c in ct))):
            o[...] = val

    shapes = [jax.ShapeDtypeStruct(x.shape, F32) for x in (a_re, a_im, log_dt, b_re, b_im)]
    return pl.pallas_call(body, name="s5_disc_bwd", out_shape=shapes)(a_re, a_im, log_dt, b_re, b_im, e, *cts)


def _cmul(a, b):
    return a[0] * b[0] - a[1] * b[1], a[0] * b[1] + a[1] * b[0]


def _s5_scan(sr, si, ab, reverse):
    n_tiles = sr.shape[0] // 8
    width = sr.shape[1]
    row8 = lax.broadcasted_iota(jnp.int32, (8, width), 0)
    p1 = ab
    p2 = _cmul(p1, p1)
    p4 = _cmul(p2, p2)
    pw = [p1]
    for _ in range(7):
        pw.append(_cmul(pw[-1], p1))
    cr = jnp.zeros((8, width), F32)
    ci = jnp.zeros((8, width), F32)
    for j in range(8):
        e = pw[7 - j] if reverse else pw[j]
        cr = jnp.where(row8 == j, e[0], cr)
        ci = jnp.where(row8 == j, e[1], ci)

    def tile(i, carry):
        idx = n_tiles - 1 - i if reverse else i
        base = pl.multiple_of(idx * 8, 8)
        x = (sr[pl.ds(base, 8), :], si[pl.ds(base, 8), :])
        for d, q in ((1, p1), (2, p2), (4, p4)):
            keep = row8 < 8 - d if reverse else row8 >= d
            amt = 8 - d if reverse else d
            sh = (jnp.where(keep, pltpu.roll(x[0], amt, 0), 0.0), jnp.where(keep, pltpu.roll(x[1], amt, 0), 0.0))
            m = _cmul(q, sh)
            x = (x[0] + m[0], x[1] + m[1])
        m = _cmul((cr, ci), carry)
        x = (x[0] + m[0], x[1] + m[1])
        sr[pl.ds(base, 8), :] = x[0]
        si[pl.ds(base, 8), :] = x[1]
        edge = slice(0, 1) if reverse else slice(7, 8)
        return x[0][edge], x[1][edge]

    zero = jnp.zeros((1, width), F32)
    lax.fori_loop(0, n_tiles, tile, (zero, zero))


_S5_W = S5_SLAB // S5_GROUP * S5_STATE


def _s5_specs(t):
    col = lambda r: pl.BlockSpec((r, S5_SLAB), lambda j: (0, j))
    bb = pl.BlockSpec((None, S5_SLAB, _S5_W), lambda j: (j, 0, 0))
    cd = pl.BlockSpec((None, _S5_W, S5_SLAB), lambda j: (j, 0, 0))
    ab = pl.BlockSpec((None, 1, _S5_W), lambda j: (j, 0, 0))
    return col, bb, cd, ab


def _s5_fwd(u, dvec, bbr, bbi, cdr, cdi, abr, abi):
    t, width = u.shape
    col, bb, cd, ab = _s5_specs(t)

    def body(u_ref, d_ref, bbr_ref, bbi_ref, cdr_ref, cdi_ref, abr_ref, abi_ref, o_ref, sr, si):
        uv = u_ref[...]
        sr[...] = jnp.dot(uv, bbr_ref[...], preferred_element_type=F32)
        si[...] = jnp.dot(uv, bbi_ref[...], preferred_element_type=F32)
        _s5_scan(sr, si, (abr_ref[...], abi_ref[...]), False)
        y = jnp.dot(sr[...], cdr_ref[...], preferred_element_type=F32) - jnp.dot(si[...], cdi_ref[...],
                                                                                 preferred_element_type=F32)
        o_ref[...] = jax.nn.gelu(y + d_ref[...] * uv).astype(BF16)

    return pl.pallas_call(body, name="s5_fwd", grid=(width // S5_SLAB,),
                          in_specs=[col(t), col(1), bb, bb, cd, cd, ab, ab], out_specs=col(t),
                          out_shape=jax.ShapeDtypeStruct((t, width), BF16),
                          scratch_shapes=[pltpu.VMEM((t, _S5_W), F32)] * 2,
                          compiler_params=_cparams(("parallel",), VMEM_BIG))(u, dvec, bbr, bbi, cdr, cdi, abr, abi)


def _s5_bwd(u, dvec, bbr, bbi, cdr, cdi, abr, abi, dyact):
    t, width = u.shape
    col, bb, cd, ab = _s5_specs(t)
    ns = width // S5_SLAB
    tn = (((0,), (0,)), ((), ()))
    nt = (((1,), (1,)), ((), ()))

    def body(u_ref, d_ref, bbr_ref, bbi_ref, cdr_ref, cdi_ref, abr_ref, abi_ref, dy_ref,
             du_ref, dd_ref, dbbr_ref, dbbi_ref, dcdr_ref, dcdi_ref, dabr_ref, dabi_ref, sr, si, gr, gi):
        uv = u_ref[...]
        dv = d_ref[...]
        abv = (abr_ref[...], abi_ref[...])
        sr[...] = jnp.dot(uv, bbr_ref[...], preferred_element_type=F32)
        si[...] = jnp.dot(uv, bbi_ref[...], preferred_element_type=F32)
        _s5_scan(sr, si, abv, False)
        y = jnp.dot(sr[...], cdr_ref[...], preferred_element_type=F32) - jnp.dot(si[...], cdi_ref[...],
                                                                                 preferred_element_type=F32)
        _, vjp = jax.vjp(jax.nn.gelu, y + dv * uv)
        (dpre,) = vjp(dy_ref[...].astype(F32))
        dd_ref[...] = jnp.sum(dpre * uv, axis=0, keepdims=True)
        dcdr_ref[...] = lax.dot_general(sr[...], dpre, tn, preferred_element_type=F32)
        dcdi_ref[...] = -lax.dot_general(si[...], dpre, tn, preferred_element_type=F32)
        gr[...] = lax.dot_general(dpre, cdr_ref[...], nt, preferred_element_type=F32)
        gi[...] = -lax.dot_general(dpre, cdi_ref[...], nt, preferred_element_type=F32)
        _s5_scan(gr, gi, (abv[0], -abv[1]), True)

        row8 = lax.broadcasted_iota(jnp.int32, (8, _S5_W), 0)

        def tile(i, carry):
            acc_r, acc_i, last_r, last_i = carry
            base = pl.multiple_of(i * 8, 8)
            s_r, s_i = sr[pl.ds(base, 8), :], si[pl.ds(base, 8), :]
            g_r, g_i = gr[pl.ds(base, 8), :], gi[pl.ds(base, 8), :]
            p_r = jnp.where(row8 == 0, last_r, pltpu.roll(s_r, 1, 0))
            p_i = jnp.where(row8 == 0, last_i, pltpu.roll(s_i, 1, 0))
            acc_r = acc_r + jnp.sum(g_r * p_r + g_i * p_i, axis=0, keepdims=True)
            acc_i = acc_i + jnp.sum(g_i * p_r - g_r * p_i, axis=0, keepdims=True)
            return acc_r, acc_i, s_r[7:8], s_i[7:8]

        zero = jnp.zeros((1, _S5_W), F32)
        acc_r, acc_i, _, _ = lax.fori_loop(0, t // 8, tile, (zero, zero, zero, zero))
        dabr_ref[...] = acc_r
        dabi_ref[...] = acc_i
        du_ref[...] = (dpre * dv + lax.dot_general(gr[...], bbr_ref[...], nt, preferred_element_type=F32)
                       + lax.dot_general(gi[...], bbi_ref[...], nt, preferred_element_type=F32))
        dbbr_ref[...] = lax.dot_general(uv, gr[...], tn, preferred_element_type=F32)
        dbbi_ref[...] = lax.dot_general(uv, gi[...], tn, preferred_element_type=F32)

    sds = jax.ShapeDtypeStruct
    return pl.pallas_call(
        body, name="s5_bwd", grid=(ns,), in_specs=[col(t), col(1), bb, bb, cd, cd, ab, ab, col(t)],
        out_specs=[col(t), col(1), bb, bb, cd, cd, ab, ab],
        out_shape=[sds((t, width), F32), sds((1, width), F32), sds((ns, S5_SLAB, _S5_W), F32),
                   sds((ns, S5_SLAB, _S5_W), F32), sds((ns, _S5_W, S5_SLAB), F32), sds((ns, _S5_W, S5_SLAB), F32),
                   sds((ns, 1, _S5_W), F32), sds((ns, 1, _S5_W), F32)],
        scratch_shapes=[pltpu.VMEM((t, _S5_W), F32)] * 4,
        compiler_params=_cparams(("parallel",), VMEM_BIG))(u, dvec, bbr, bbi, cdr, cdi, abr, abi, dyact)


def _gate_dense(w):
    h = w.shape[0]
    return jnp.einsum("hij,hg->higj", w, jnp.eye(h, dtype=F32)).reshape(h * HEAD, h * HEAD)


def _gate_blocks(d):
    x = d.reshape(LRU_W // LRU_CW, 2, HEAD, 2, HEAD)
    return jnp.einsum("tgihj,gh->tgij", x, jnp.eye(2, dtype=F32)).reshape(LRU_W // HEAD, HEAD, HEAD)


_GPS = S5_SLAB // S5_GROUP
_NS = S5_GROUPS // _GPS


def _s5_in_dense(bb):
    x = bb.reshape(_NS, _GPS, S5_STATE, S5_GROUP)
    return jnp.einsum("sgnc,gh->sgchn", x, jnp.eye(_GPS, dtype=F32)).reshape(_NS, S5_SLAB, _S5_W)


def _s5_in_blocks(d):
    x = d.reshape(_NS, _GPS, S5_GROUP, _GPS, S5_STATE)
    return jnp.einsum("sgchn,gh->sgnc", x, jnp.eye(_GPS, dtype=F32)).reshape(S5_GROUPS, S5_STATE * S5_GROUP)


def _s5_out_dense(c):
    x = c.reshape(_NS, _GPS, S5_GROUP, S5_STATE)
    return jnp.einsum("sgcn,gh->shngc", x, jnp.eye(_GPS, dtype=F32)).reshape(_NS, _S5_W, S5_SLAB)


def _s5_out_blocks(d):
    x = d.reshape(_NS, _GPS, S5_STATE, _GPS, S5_GROUP)
    return jnp.einsum("shngc,gh->sgcn", x, jnp.eye(_GPS, dtype=F32)).reshape(S5_GROUPS, S5_GROUP, S5_STATE)


def _local_step(x, tgt, w, late_weights, send_grads):
    d_model = x.shape[1]
    gs = {}
    gm = jnp.kron(jnp.eye(N_HEADS, dtype=F32), jnp.ones((HEAD, HEAD), F32))
    n_layers = w["f_norm_g"].shape[0]

    def ffn_fwd(xin, l):
        xn = _rms_fwd(xin, w["f_norm_g"][l:l + 1], f"rms_f{l}")
        h = _matmul(xn, w["f_w_up_t"][l], "nt", f"mm_f{l}_up")
        act = _ffn_mid_fwd(h, w["f_conv_w"][l], w["f_conv_b"][l:l + 1], f"ffn_mid_fwd{l}")
        return _matmul(act, w["f_w_down"][l], "nn", f"mm_f{l}_down", add=xin), (xin, xn, h, act)

    def ffn_bwd(g, saved, l):
        xin, xn, h, act = saved
        dact = _matmul(g, w["f_w_down"][l], "nt", f"mm_f{l}_dact")
        d_down = _matmul(act, g, "tn", f"mm_f{l}_ddown", out_dtype=BF16)
        dhg, dhv, dwg, dwv, dbg, dbv = _ffn_mid_bwd(h, w["f_conv_w"][l], w["f_conv_b"][l:l + 1], dact,
                                                    f"ffn_mid_bwd{l}")
        dh = jnp.concatenate([dhg, dhv], axis=1)
        dxn = _matmul(dh, w["f_w_up_t"][l], "nn", f"mm_f{l}_dxn")
        d_up = _matmul(dh, xn, "tn", f"mm_f{l}_dup", out_dtype=BF16)
        dx, dgn = _rms_bwd(xin, w["f_norm_g"][l:l + 1], dxn, g, f"rms_f{l}_bwd")
        return dx, d_up, d_down, jnp.concatenate([dwg, dwv], axis=1), jnp.concatenate([dbg, dbv], axis=1), dgn

    xn0 = _rms_fwd(x, w["e_norm_g"], "rms_e")
    p = _matmul(xn0, w["e_w_in_t"], "nt", "mm_e_in")
    pam = _tshift_fwd(p, w["e_mu"])
    pw = dict(w0=w["e_w0"], w2=w["e_w2"][0], a0=w["e_a0"], a2=w["e_a2"][0], g2=w["e_g2"][0],
              k_k=w["e_k_k"], k_a=w["e_k_a"])
    r, dec, k2, v, z, b, gate = _rwkv_prep_fwd(pam, pw, gm)
    v_exp = _expand_cols(v, "wkv_expand_v")
    s_all, s_last = _wkv_fwd(dec, k2, z, b, v_exp)
    y_pt = _wkv_out(r, s_all, s_last)
    y = _from_pt(y_pt)
    rk = w["e_r_k"].reshape(1, RW)
    ya = _rwkv_post_fwd(y, r, k2, v, gate, w["e_ln_w"], w["e_ln_b"], rk, gm)
    ga, gx = _gate_dense(w["e_gate_a_w"][0]), _gate_dense(w["e_gate_x_w"][0])
    lru_w = (w["e_conv_w"][0], w["e_conv_b"], ga, w["e_gate_a_b"], gx, w["e_gate_x_b"], w["e_lru_lambda"])
    yb = _lru_fwd(p, *lru_w)
    ycat = jnp.concatenate([ya, yb], axis=1)
    x1 = _matmul(ycat, w["e_w_out"], "nn", "mm_e_out", add=x)
    w = {**w, **late_weights(x1)}
    x2, ffn0 = ffn_fwd(x1, 0)

    xn1 = _rms_fwd(x2, w["o_norm_g"], "rms_o")
    u = _matmul(xn1, w["o_w_in"], "nn", "mm_o_in")
    expand = jnp.kron(jnp.eye(S5_STATE, dtype=F32), jnp.ones((1, S5_GROUP), F32))
    disc_in = (w["o_A_re"][0], w["o_A_im"][0], w["o_log_dt"].reshape(S5_GROUPS, 1),
               w["o_B_re"][0].reshape(S5_GROUPS, -1), w["o_B_im"][0].reshape(S5_GROUPS, -1), expand)
    ab_re, ab_im, bb_re, bb_im = _s5_disc_fwd(*disc_in)
    s5_w = (w["o_D"], _s5_in_dense(bb_re), _s5_in_dense(bb_im), _s5_out_dense(w["o_C_re"][0]),
            _s5_out_dense(w["o_C_im"][0]), ab_re.reshape(_NS, 1, _S5_W), ab_im.reshape(_NS, 1, _S5_W))
    yact = _s5_fwd(u, *s5_w)
    zz = _matmul(yact, w["o_w_glu_t"], "nt", "mm_o_glu")
    x3 = _glu_fwd(x2, zz)
    x4, ffn1 = ffn_fwd(x3, 1)

    loss, g, gs["final_norm_g"] = _loss_head(x4, w["final_norm_g"].reshape(1, d_model), tgt)
    gs["final_norm_g"] = gs["final_norm_g"].reshape(d_model)

    g, up1, down1, dcw1, dcb1, dfn1 = ffn_bwd(g, ffn1, 1)
    dz = _glu_bwd(zz, g)
    dyact = _matmul(dz, w["o_w_glu_t"], "nn", "mm_o_dyact")
    d_glu = _matmul(dz, yact, "tn", "mm_o_dglu", out_dtype=BF16)
    du, gs["o_D"], dbbr, dbbi, dcdr, dcdi, dabr, dabi = _s5_bwd(u, *s5_w, dyact)
    gs["o_C_re"] = _s5_out_blocks(dcdr)[None]
    gs["o_C_im"] = _s5_out_blocks(dcdi)[None]
    cts = (dabr.reshape(S5_GROUPS, S5_STATE), dabi.reshape(S5_GROUPS, S5_STATE), _s5_in_blocks(dbbr),
           _s5_in_blocks(dbbi))
    da_re, da_im, dlog_dt, db_re, db_im = _s5_disc_bwd(*disc_in, cts)
    gs["o_A_re"], gs["o_A_im"], gs["o_log_dt"] = da_re[None], da_im[None], dlog_dt.reshape(1, S5_GROUPS)
    gs["o_B_re"] = db_re.reshape(w["o_B_re"].shape)
    gs["o_B_im"] = db_im.reshape(w["o_B_im"].shape)
    dxn = _matmul(du, w["o_w_in"], "nt", "mm_o_dxn")
    d_oin = _matmul(xn1, du, "tn", "mm_o_din", out_dtype=BF16)
    g, gs["o_norm_g"] = _rms_bwd(x2, w["o_norm_g"], dxn, g, "rms_o_bwd")
    g = send_grads("a", [("f_w_up", 1, up1), ("f_w_down", 1, down1), ("o_w_glu", 0, d_glu), ("o_w_in", 0, d_oin)], g)

    g, up0, down0, dcw0, dcb0, dfn0 = ffn_bwd(g, ffn0, 0)
    gs["f_conv_w"] = jnp.stack([dcw0, dcw1])
    gs["f_conv_b"] = jnp.concatenate([dcb0, dcb1], axis=0)
    gs["f_norm_g"] = jnp.concatenate([dfn0, dfn1], axis=0)

    dycat = _matmul(g, w["e_w_out"], "nt", "mm_e_dycat")
    d_eout = _matmul(ycat, g, "tn", "mm_e_dout", out_dtype=BF16)
    dycat = send_grads("b", [("f_w_up", 0, up0), ("f_w_down", 0, down0), ("e_w_out", 0, d_eout)], dycat)
    dy, dr1, dk1, dv1, dgate, gs["e_ln_w"], gs["e_ln_b"], drk = _rwkv_post_bwd(
        y, r, k2, v, gate, w["e_ln_w"], w["e_ln_b"], rk, gm, dycat)
    gs["e_r_k"] = drk.reshape(w["e_r_k"].shape)
    dr2, ddec, dk2, dzz, dbb, dv_pt = _wkv_bwd(r, dec, k2, z, b, v_exp, s_all, _expand_cols(dy, "wkv_expand_dy"))
    dpam, gs["e_w0"], dw2, gs["e_a0"], da2, dg2, gs["e_k_k"], gs["e_k_a"] = _rwkv_prep_bwd(
        pam, pw, gm, (dr2, ddec, dk2, _from_pt(dv_pt), dzz, dbb, dgate), (dr1, dk1, dv1))
    gs["e_w2"], gs["e_a2"], gs["e_g2"] = dw2[None], da2[None], dg2[None]
    dpa, gs["e_mu"] = _tshift_bwd(p, w["e_mu"], dpam)
    dbx, dbg, dcw, gs["e_conv_b"], dga, gs["e_gate_a_b"], dgx, gs["e_gate_x_b"], gs["e_lru_lambda"] = _lru_bwd(
        p, *lru_w, dycat)
    gs["e_conv_w"] = dcw[None]
    gs["e_gate_a_w"] = _gate_blocks(dga)[None]
    gs["e_gate_x_w"] = _gate_blocks(dgx)[None]
    dp = jnp.concatenate([dpa, dbx, dbg], axis=1)
    dxn = _matmul(dp, w["e_w_in_t"], "nn", "mm_e_dxn")
    d_ein = _matmul(dp, xn0, "tn", "mm_e_din", out_dtype=BF16)
    grad_x, gs["e_norm_g"] = _rms_bwd(x, w["e_norm_g"], dxn, g, "rms_e_bwd")
    grad_x = send_grads("c", [("e_w_in", 0, d_ein)], grad_x)
    return loss, grad_x, gs


CAST_ROWS = 256


def _cast_shard(w3, layer, transpose, chip, name):
    _, rows, cols = w3.shape
    tr = _tile(rows, (CAST_ROWS, 176, 128))

    def body(c_ref, w_ref, o_ref):
        v = w_ref[...]
        o_ref[...] = (v.T if transpose else v).astype(BF16)

    in_spec = pl.BlockSpec((None, tr, cols), lambda i, c: (layer, i, 0))
    if transpose:
        out_spec, shape = pl.BlockSpec((None, cols, tr), lambda i, c: (c[0], 0, i)), (cols, rows)
    else:
        out_spec, shape = pl.BlockSpec((None, tr, cols), lambda i, c: (c[0], i, 0)), (rows, cols)
    grid_spec = pltpu.PrefetchScalarGridSpec(num_scalar_prefetch=1, grid=(rows // tr,), in_specs=[in_spec],
                                             out_specs=out_spec)
    return pl.pallas_call(body, name=name, grid_spec=grid_spec,
                          out_shape=jax.ShapeDtypeStruct((N_CHIPS,) + shape, BF16),
                          compiler_params=_cparams(("parallel",), VMEM_MID))(chip, w3)


_ANY = pl.BlockSpec(memory_space=pl.ANY)


def _coords():
    return lax.axis_index("x"), lax.axis_index("y"), lax.axis_index("c")


def _flip(v, d):
    return 1 - v if d else v


_CHIP_RELS = ((1, 0), (0, 1), (1, 1))
_DEV_RELS = tuple((dx, dy, dc) for dx in (0, 1) for dy in (0, 1) for dc in (0, 1))[1:]


_HBM = pl.BlockSpec(memory_space=pltpu.HBM)
_SEM = pl.BlockSpec(memory_space=pltpu.SEMAPHORE)
_EFFECT = pltpu.SideEffectType.DATAFLOW_SIDE_EFFECTING


def _in_hbm(a):
    return pltpu.with_memory_space_constraint(a, pltpu.HBM)


def _gather_copies(bufs, send, recv, landed):
    x, y, c = _coords()
    me = 2 * x + y
    res = []
    for i, buf in enumerate(bufs):
        for j, (dx, dy) in enumerate(_CHIP_RELS):
            px, py = _flip(x, dx), _flip(y, dy)
            k = i * len(_CHIP_RELS) + j
            res.append(pltpu.make_async_remote_copy(
                src_ref=buf.at[me], dst_ref=buf.at[2 * px + py if landed else me], send_sem=send.at[k],
                recv_sem=recv.at[k], device_id=(px, py, c), device_id_type=MESH))
    return res


def _scatter_copies(srcs, lands, send, recv, landed):
    x, y, c = _coords()
    me = 4 * x + 2 * y + c
    res = []
    for i, (src, land) in enumerate(zip(srcs, lands)):
        for j, (dx, dy, dc) in enumerate(_DEV_RELS):
            peer = (_flip(x, dx), _flip(y, dy), _flip(c, dc))
            pid = 4 * peer[0] + 2 * peer[1] + peer[2]
            k = i * len(_DEV_RELS) + j
            res.append(pltpu.make_async_remote_copy(
                src_ref=src.at[pid], dst_ref=land.at[pid if landed else me], send_sem=send.at[k],
                recv_sem=recv.at[k], device_id=peer, device_id_type=MESH))
    return res


def _split_start(bufs, n_src, copies, n_rel, name, after):
    n = len(bufs)
    nk = n_src * n_rel

    def body(*refs):
        ins, send, recv, token = refs[:n], refs[n + 1 + n], refs[n + 2 + n], refs[-1]
        for cp in copies(ins, send, recv, False):
            cp.start()
        token[...] = jnp.zeros_like(token)

    res = pl.pallas_call(
        body, name=name, in_specs=[_HBM] * n + [_ANY],
        out_specs=[_HBM] * n + [_SEM, _SEM, pl.BlockSpec(memory_space=pltpu.VMEM)],
        out_shape=[pltpu.HBM(b.shape, b.dtype) for b in bufs]
        + [pltpu.SemaphoreType.DMA((nk,)), pltpu.SemaphoreType.DMA((nk,)), jax.ShapeDtypeStruct((8, LANES), F32)],
        input_output_aliases={i: i for i in range(n)},
        compiler_params=pltpu.CompilerParams(has_side_effects=_EFFECT))(*[_in_hbm(b) for b in bufs], after)
    return res[n], res[n + 1], list(res[:n]), res[n + 2]


def _split_wait(bufs, send, recv, copies, name, after):
    n = len(bufs)

    def body(*refs):
        ins, send_ref, recv_ref = refs[:n], refs[n], refs[n + 1]
        for cp in copies(ins, send_ref, recv_ref, True):
            cp.wait_send()
            cp.wait_recv()

    return pl.pallas_call(
        body, name=name, in_specs=[_HBM] * n + [_SEM, _SEM, _ANY], out_specs=[_HBM] * n,
        out_shape=[pltpu.HBM(b.shape, b.dtype) for b in bufs], input_output_aliases={i: i for i in range(n)},
        compiler_params=pltpu.CompilerParams(has_side_effects=_EFFECT))(*bufs, send, recv, after)


def _gather_start(bufs, name, after):
    return _split_start(bufs, len(bufs), _gather_copies, len(_CHIP_RELS), name, after)


def _gather_wait(bufs, send, recv, name, after):
    return _split_wait(bufs, send, recv, _gather_copies, name, after)


def _scatter_start(srcs, name, after):
    n = len(srcs)
    lands = [lax.empty(a.shape, a.dtype) for a in srcs]
    fn = lambda refs, send, recv, landed: _scatter_copies(refs[:n], refs[n:], send, recv, landed)
    send, recv, bufs, token = _split_start(list(srcs) + lands, n, fn, len(_DEV_RELS), name, after)
    return send, recv, bufs, token


def _scatter_wait(bufs, send, recv, name, after):
    n = len(bufs) // 2
    fn = lambda refs, s, r, landed: _scatter_copies(refs[:n], refs[n:], s, r, landed)
    res = _split_wait(bufs, send, recv, fn, name, after)
    return res[:n], res[n:]


def _sum_segments(src, land, me, name):
    nd, seg, cols = src.shape
    ts = _tile(seg, (256, 176, 128))

    def body(m_ref, *refs):
        o_ref = refs[-1]
        acc = refs[0][...].astype(F32)
        for r in refs[1:-1]:
            acc = acc + r[...].astype(F32)
        o_ref[...] = acc

    def peer(rel):
        bits = 4 * rel[0] + 2 * rel[1] + rel[2]
        return pl.BlockSpec((None, ts, cols), lambda i, m: (jnp.bitwise_xor(m[0], bits), i, 0))

    grid_spec = pltpu.PrefetchScalarGridSpec(
        num_scalar_prefetch=1, grid=(seg // ts,),
        in_specs=[pl.BlockSpec((None, ts, cols), lambda i, m: (m[0], i, 0))] + [peer(r) for r in _DEV_RELS],
        out_specs=pl.BlockSpec((None, ts, cols), lambda i, m: (m[1], i, 0)))
    return pl.pallas_call(body, name=name, grid_spec=grid_spec,
                          out_shape=jax.ShapeDtypeStruct((2, seg, cols), F32),
                          compiler_params=_cparams(("parallel",), VMEM_MID))(me, src, *[land] * len(_DEV_RELS))


def _exchange_sibling(arrs):
    n = len(arrs)

    def body(*refs):
        outs, (send, recv) = refs[n:2 * n], refs[2 * n:]
        x, y, c = _coords()
        sib = (x, y, 1 - c)
        sends, recvs = [], []
        for i in range(n):
            cp = pltpu.make_async_remote_copy(src_ref=outs[i].at[c], dst_ref=outs[i].at[c], send_sem=send.at[i],
                                              recv_sem=recv.at[i], device_id=sib, device_id_type=MESH)
            cp.start()
            sends.append(cp)
            recvs.append(pltpu.make_async_remote_copy(src_ref=outs[i].at[c], dst_ref=outs[i].at[1 - c],
                                                      send_sem=send.at[i], recv_sem=recv.at[i], device_id=sib,
                                                      device_id_type=MESH))
        for cp in recvs:
            cp.wait_recv()
        for cp in sends:
            cp.wait_send()

    return pl.pallas_call(
        body, name="exchange_sibling", in_specs=[_ANY] * n, out_specs=[_ANY] * n,
        out_shape=[jax.ShapeDtypeStruct(a.shape, a.dtype) for a in arrs],
        input_output_aliases={i: i for i in range(n)},
        scratch_shapes=[pltpu.SemaphoreType.DMA((n,)), pltpu.SemaphoreType.DMA((n,))])(*arrs)


def _allreduce_small(vec):
    nd, rows, lanes = vec.shape
    nr = len(_DEV_RELS)

    def body(in_ref, out_ref, stage, red, send, recv):
        x, y, c = _coords()
        me = 4 * x + 2 * y + c
        peers = []
        for dx, dy, dc in _DEV_RELS:
            peer = (_flip(x, dx), _flip(y, dy), _flip(c, dc))
            peers.append((peer, 4 * peer[0] + 2 * peer[1] + peer[2]))

        def copy(src, dst, k, peer):
            return pltpu.make_async_remote_copy(src_ref=src, dst_ref=dst, send_sem=send.at[k], recv_sem=recv.at[k],
                                                device_id=peer, device_id_type=MESH)

        first = [copy(in_ref.at[pid], stage.at[me], j, peer) for j, (peer, pid) in enumerate(peers)]
        for cp in first:
            cp.start()
        stage[me] = in_ref[me]
        for j, (peer, pid) in enumerate(peers):
            copy(in_ref.at[pid], stage.at[pid], j, peer).wait_recv()
        acc = stage[0]
        for d in range(1, nd):
            acc = acc + stage[d]
        red[...] = acc
        out_ref[me] = acc
        second = [copy(red, out_ref.at[me], nr + j, peer) for j, (peer, pid) in enumerate(peers)]
        for cp in second:
            cp.start()
        for j, (peer, pid) in enumerate(peers):
            copy(red, out_ref.at[pid], nr + j, peer).wait_recv()
        for cp in first + second:
            cp.wait_send()

    vm = pl.BlockSpec(memory_space=pltpu.VMEM)
    return pl.pallas_call(
        body, name="allreduce_small", in_specs=[vm], out_specs=vm,
        out_shape=jax.ShapeDtypeStruct(vec.shape, F32),
        scratch_shapes=[pltpu.VMEM(vec.shape, F32), pltpu.VMEM((rows, lanes), F32),
                        pltpu.SemaphoreType.DMA((2 * nr,)), pltpu.SemaphoreType.DMA((2 * nr,))],
        compiler_params=_cparams(None, VMEM_MID))(vec)


def _adam_math(w, g, m, v):
    m2 = ADAM_B1 * m + (1.0 - ADAM_B1) * g
    v2 = ADAM_B2 * v + (1.0 - ADAM_B2) * (g * g)
    m_hat = m2 / (1.0 - ADAM_B1 ** ADAM_STEP)
    v_hat = v2 / (1.0 - ADAM_B2 ** ADAM_STEP)
    return -ADAM_LR * (m_hat / (jnp.sqrt(v_hat) + ADAM_EPS) + ADAM_WD * w), m2, v2


def _adamw_big(w3, m3, v3, layer, g, transposed, name, prev=None):
    nl, rows, cols = w3.shape
    tr = 128 if transposed else _tile(rows, (256, 176, 128))

    def body(w_ref, m_ref, v_ref, g_ref, *rest):
        go_ref, d_ref, mo_ref, vo_ref = rest[-4:]
        g_val = g_ref[...].T if transposed else g_ref[...]
        go_ref[...] = g_val
        d_ref[...], mo_ref[...], vo_ref[...] = _adam_math(w_ref[...], g_val, m_ref[...], v_ref[...])

    wspec = pl.BlockSpec((None, tr, cols), lambda i: (layer, i, 0))
    gspec = pl.BlockSpec((cols, tr), lambda i: (0, i)) if transposed else pl.BlockSpec((tr, cols), lambda i: (i, 0))
    extra = [] if prev is None else list(prev)
    return pl.pallas_call(body, name=name, grid=(rows // tr,),
                          in_specs=[wspec, wspec, wspec, gspec] + [_ANY] * len(extra),
                          out_specs=[wspec] * 4, out_shape=[jax.ShapeDtypeStruct((nl, rows, cols), F32)] * 4,
                          input_output_aliases={4 + i: i for i in range(len(extra))},
                          compiler_params=_cparams(("parallel",), VMEM_MID))(w3, m3, v3, g, *extra)


def _adamw_small(w, g, m, v):
    rows = w.shape[0]
    tr = _tile(rows, (512, 256, 128, 64, 32, 16, 8))

    def body(w_ref, g_ref, m_ref, v_ref, d_ref, mo_ref, vo_ref):
        d_ref[...], mo_ref[...], vo_ref[...] = _adam_math(w_ref[...], g_ref[...], m_ref[...], v_ref[...])

    spec = pl.BlockSpec((tr, LANES), lambda i: (i, 0))
    return pl.pallas_call(body, name="adamw_small", grid=(rows // tr,), in_specs=[spec] * 4, out_specs=[spec] * 3,
                          out_shape=[jax.ShapeDtypeStruct(w.shape, F32)] * 3,
                          compiler_params=_cparams(("parallel",)))(w, g, m, v)


PACK_ROWS = 8


def _packed_rows(shape):
    size = 1
    for d in shape:
        size *= d
    return -(-size // (PACK_ROWS * LANES)) * PACK_ROWS


def _pack(arrs, row_mult):
    parts = []
    for a in arrs:
        flat = a.reshape(-1).astype(F32)
        rows = _packed_rows(a.shape)
        parts.append(jnp.pad(flat, (0, rows * LANES - flat.shape[0])).reshape(rows, LANES))
    total = sum(p.shape[0] for p in parts)
    fill = -(-total // row_mult) * row_mult - total
    if fill:
        parts.append(jnp.zeros((fill, LANES), F32))
    return jnp.concatenate(parts, axis=0)


def _unpack(packed, shapes):
    out, off = [], 0
    for s in shapes:
        rows = _packed_rows(s)
        size = 1
        for d in s:
            size *= d
        out.append(packed[off:off + rows].reshape(-1)[:size].reshape(s))
        off += rows
    return out


_SMALL_REP = ("e_norm_g", "e_mu", "e_w0", "e_a0", "e_k_k", "e_k_a", "e_r_k", "e_ln_w", "e_ln_b", "e_conv_b",
              "e_gate_a_w", "e_gate_a_b", "e_gate_x_w", "e_gate_x_b", "e_lru_lambda", "o_A_re", "o_A_im", "o_log_dt",
              "o_B_re", "o_B_im", "o_C_re", "o_C_im", "f_norm_g", "f_conv_b", "final_norm_g")
_SMALL_SH = ("e_w2", "e_a2", "e_g2", "e_conv_w", "o_norm_g", "o_D", "f_conv_w")
_LARGE = (("e_w_in", True), ("e_w_out", False), ("o_w_in", False), ("o_w_glu", True), ("f_w_up", True),
        ("f_w_down", False))
_ORDER = ("e_norm_g", "e_w_in", "e_mu", "e_w0", "e_w2", "e_a0", "e_a2", "e_g2", "e_k_k", "e_k_a", "e_r_k", "e_ln_w",
          "e_ln_b", "e_conv_w", "e_conv_b", "e_gate_a_w", "e_gate_a_b", "e_gate_x_w", "e_gate_x_b", "e_lru_lambda",
          "e_w_out", "o_norm_g", "o_w_in", "o_A_re", "o_A_im", "o_log_dt", "o_B_re", "o_B_im", "o_C_re", "o_C_im",
          "o_D", "o_w_glu", "f_norm_g", "f_w_up", "f_conv_w", "f_conv_b", "f_w_down", "final_norm_g")
N_CHIPS = 4
N_DEV = 8


def _step(x, tgt, wts, ms, vs):
    xi, yi, ci = _coords()
    chip = 2 * xi + yi
    chip1 = chip.astype(jnp.int32).reshape(1)
    me2 = jnp.stack([4 * xi + 2 * yi + ci, ci]).astype(jnp.int32)
    by_cols = dict(_LARGE)

    bufs = {(name, l): _cast_shard(wts[name], l, by_cols[name], chip1, f"cast_{name}{l}")
            for name, _ in _LARGE for l in range(wts[name].shape[0])}
    sh_shapes = [wts[n].shape for n in _SMALL_SH]
    packed = _pack([wts[n] for n in _SMALL_SH], 8)
    small_buf = lax.dynamic_update_slice(jnp.zeros((N_CHIPS,) + packed.shape, F32), packed[None], (chip, 0, 0))
    early = [("e_w_in", 0), ("e_w_out", 0)]
    late = [k for k in bufs if k not in early]
    send, recv, thru, token = _gather_start([bufs[k] for k in early] + [small_buf], "gather_start_a", x)
    got = _gather_wait(thru, send, recv, "gather_wait_a", token)
    send_b, recv_b, thru_b, token = _gather_start([bufs[k] for k in late], "gather_start_b", got[0])
    x, _ = lax.optimization_barrier((x, token))

    def rows(g):
        return g.reshape(N_CHIPS * g.shape[1], g.shape[2])

    full = {n: wts[n] for n in _SMALL_REP}
    full["e_w_in_t"], full["e_w_out"] = rows(got[0]), rows(got[1])
    per_chip = [_unpack(got[2][k], sh_shapes) for k in range(N_CHIPS)]
    for i, n in enumerate(_SMALL_SH):
        full[n] = jnp.concatenate([per_chip[k][i] for k in range(N_CHIPS)], axis=-1)

    def late_weights(after):
        res = dict(zip(late, _gather_wait(thru_b, send_b, recv_b, "gather_wait_b", after)))
        return {"o_w_in": rows(res[("o_w_in", 0)]), "o_w_glu_t": rows(res[("o_w_glu", 0)]),
                "f_w_up_t": [rows(res[("f_w_up", l)]) for l in range(2)],
                "f_w_down": [rows(res[("f_w_down", l)]) for l in range(2)]}

    pending = []

    def send_grads(tag, items, carry):
        srcs = [g.reshape(N_DEV, g.shape[0] // N_DEV, g.shape[1]) for _, _, g in items]
        s_sem, r_sem, both, tok = _scatter_start(srcs, f"scatter_start_{tag}", carry)
        pending.append((tag, [(name, l) for name, l, _ in items], s_sem, r_sem, both))
        carry, _ = lax.optimization_barrier((carry, tok))
        return carry

    loss, grad_x, gs = _local_step(x, tgt, full, late_weights, send_grads)

    final = {}
    small = _SMALL_REP + _SMALL_SH
    shapes = [gs[n].shape for n in small]
    red = _allreduce_small(_pack([gs[n] for n in small], 8 * N_DEV).reshape(N_DEV, -1, LANES))
    tot = dict(zip(small, _unpack(red.reshape(-1, LANES), shapes)))
    for n in _SMALL_SH:
        width = wts[n].shape[-1]
        tot[n] = lax.dynamic_slice_in_dim(tot[n], chip * width, width, axis=tot[n].ndim - 1)
    loc_shapes = [wts[n].shape for n in small]
    pk = lambda d: _pack([d[n] for n in small], 8)
    delta, new_m, new_v = _adamw_small(pk(wts), pk(tot), pk(ms), pk(vs))
    for n, g, d, m2, v2 in zip(small, [tot[n] for n in small], _unpack(delta, loc_shapes), _unpack(new_m, loc_shapes),
                               _unpack(new_v, loc_shapes)):
        final[n] = [g.reshape(wts[n].shape), d, m2, v2]

    halves, keys = [], []
    for tag, names, s_sem, r_sem, both in pending:
        srcs, lands = _scatter_wait(both, s_sem, r_sem, f"scatter_wait_{tag}", new_v)
        for (name, l), src, land in zip(names, srcs, lands):
            halves.append(_sum_segments(src, land, me2, f"sum_{name}{l}"))
            keys.append((name, l))
    shards = _exchange_sibling(halves)
    for s, (name, l) in zip(shards, keys):
        final[name] = _adamw_big(wts[name], ms[name], vs[name], l, s.reshape(2 * s.shape[1], s.shape[2]),
                                 by_cols[name], f"adamw_{name}{l}", prev=final.get(name))

    loss = lax.psum(loss[0, 0], ("x", "y", "c"))
    res = [loss, grad_x[None]]
    for k in range(4):
        res += [final[n][k] for n in _ORDER]
    return tuple(res)


def kernel(x, e_norm_g, e_w_in, e_mu, e_w0, e_w2, e_a0, e_a2, e_g2, e_k_k, e_k_a, e_r_k, e_ln_w, e_ln_b, e_conv_w, e_conv_b, e_gate_a_w, e_gate_a_b, e_gate_x_w, e_gate_x_b, e_lru_lambda, e_w_out, o_norm_g, o_w_in, o_A_re, o_A_im, o_log_dt, o_B_re, o_B_im, o_C_re, o_C_im, o_D, o_w_glu, f_norm_g, f_w_up, f_conv_w, f_conv_b, f_w_down, final_norm_g, loss_target, m_e_norm_g, m_e_w_in, m_e_mu, m_e_w0, m_e_w2, m_e_a0, m_e_a2, m_e_g2, m_e_k_k, m_e_k_a, m_e_r_k, m_e_ln_w, m_e_ln_b, m_e_conv_w, m_e_conv_b, m_e_gate_a_w, m_e_gate_a_b, m_e_gate_x_w, m_e_gate_x_b, m_e_lru_lambda, m_e_w_out, m_o_norm_g, m_o_w_in, m_o_A_re, m_o_A_im, m_o_log_dt, m_o_B_re, m_o_B_im, m_o_C_re, m_o_C_im, m_o_D, m_o_w_glu, m_f_norm_g, m_f_w_up, m_f_conv_w, m_f_conv_b, m_f_w_down, m_final_norm_g, v_e_norm_g, v_e_w_in, v_e_mu, v_e_w0, v_e_w2, v_e_a0, v_e_a2, v_e_g2, v_e_k_k, v_e_k_a, v_e_r_k, v_e_ln_w, v_e_ln_b, v_e_conv_w, v_e_conv_b, v_e_gate_a_w, v_e_gate_a_b, v_e_gate_x_w, v_e_gate_x_b, v_e_lru_lambda, v_e_w_out, v_o_norm_g, v_o_w_in, v_o_A_re, v_o_A_im, v_o_log_dt, v_o_B_re, v_o_B_im, v_o_C_re, v_o_C_im, v_o_D, v_o_w_glu, v_f_norm_g, v_f_w_up, v_f_conv_w, v_f_conv_b, v_f_w_down, v_final_norm_g):
    args = locals()
    wts = {n: args[n] for n in _ORDER}
    ms = {n: args["m_" + n] for n in _ORDER}
    vs = {n: args["v_" + n] for n in _ORDER}
    return _step(x[0], loss_target[0], wts, ms, vs)
```

```python
import functools

import jax
import jax.numpy as jnp
from jax import lax
from jax.experimental import pallas as pl
from jax.experimental.pallas import tpu as pltpu

F32 = jnp.float32
BF16 = jnp.bfloat16
MESH = pl.DeviceIdType.MESH

HEAD = 64
RW = 512
N_HEADS = RW // HEAD
LRU_W = 512
SHIFT_COLS = 1792
W_LORA, A_LORA, G_LORA = 64, 64, 128
S5_GROUPS, S5_GROUP, S5_STATE = 64, 16, 64
D_FF = 2816
NORM_EPS = 1e-6
GN_EPS = 64e-5
LRU_C = 8.0
ADAM_LR, ADAM_B1, ADAM_B2, ADAM_EPS, ADAM_WD, ADAM_STEP = 0.001, 0.9, 0.999, 1e-08, 0.01, 10

VMEM_BIG = 56 * 1024 * 1024
VMEM_MID = 40 * 1024 * 1024
LANES = 128
PT = 16
WKV_CHUNK = 32
S5_SLAB = 128


def _cparams(sem=None, vmem=None):
    kw = {}
    if sem is not None:
        kw["dimension_semantics"] = sem
    if vmem is not None:
        kw["vmem_limit_bytes"] = vmem
    return pltpu.CompilerParams(**kw)


def _tile(dim, cands):
    for c in cands:
        if dim % c == 0:
            return c
    return dim


def _full(shape):
    n = len(shape)
    return pl.BlockSpec(shape, lambda *_: (0,) * n)


_TILES = (2816, 2048, 1408, 1024, 512, 256, 128)
MM_BUDGET = 36 * 1024 * 1024
VMEM_SLACK = 12 * 1024 * 1024


def _mm_tiles(m, n, k, size_a, size_b, size_o, has_add):
    best = None
    for tm in _TILES:
        for tk in _TILES:
            for tn in _TILES:
                if m % tm or n % tn or k % tk:
                    continue
                need = 2 * (tm * tk * size_a + tk * tn * size_b + tm * tn * size_o) + tm * tn * 4 * (1 + 2 * has_add)
                if k > tk:
                    need += tm * tn * 4
                if need <= MM_BUDGET:
                    cand = (tm, tk, tn)
                    if best is None or cand > best[0]:
                        best = (cand, need)
    (tm, tk, tn), need = best
    return tm, tn, tk, need


def _matmul(a, b, mode, name, out_dtype=F32, add=None):
    if mode == "nn":
        (m, k), (k2, n) = a.shape, b.shape
    elif mode == "nt":
        (m, k), (n, k2) = a.shape, b.shape
    else:
        (k, m), (k2, n) = a.shape, b.shape
    assert k == k2, (a.shape, b.shape, mode)
    tm, tn, tk, need = _mm_tiles(m, n, k, a.dtype.itemsize, b.dtype.itemsize, jnp.dtype(out_dtype).itemsize,
                                 add is not None)
    nk = k // tk
    dims = {"nn": (((1,), (0,)), ((), ())), "nt": (((1,), (1,)), ((), ())), "tn": (((0,), (0,)), ((), ()))}[mode]

    def body(*refs):
        a_ref, b_ref = refs[:2]
        add_ref = refs[2] if add is not None else None
        o_ref = refs[3] if add is not None else refs[2]
        part = lax.dot_general(a_ref[...].astype(BF16), b_ref[...].astype(BF16), dims, preferred_element_type=F32)

        def finish(r):
            if add_ref is not None:
                r = r + add_ref[...]
            o_ref[...] = r.astype(o_ref.dtype)

        if nk == 1:
            finish(part)
            return
        acc = refs[-1]
        kk = pl.program_id(2)

        @pl.when(kk == 0)
        def _():
            acc[...] = part

        @pl.when(kk > 0)
        def _():
            acc[...] += part

        @pl.when(kk == nk - 1)
        def _():
            finish(acc[...])

    if mode == "nn":
        a_spec = pl.BlockSpec((tm, tk), lambda i, j, kk: (i, kk))
        b_spec = pl.BlockSpec((tk, tn), lambda i, j, kk: (kk, j))
    elif mode == "nt":
        a_spec = pl.BlockSpec((tm, tk), lambda i, j, kk: (i, kk))
        b_spec = pl.BlockSpec((tn, tk), lambda i, j, kk: (j, kk))
    else:
        a_spec = pl.BlockSpec((tk, tm), lambda i, j, kk: (kk, i))
        b_spec = pl.BlockSpec((tk, tn), lambda i, j, kk: (kk, j))
    o_spec = pl.BlockSpec((tm, tn), lambda i, j, kk: (i, j))
    in_specs = [a_spec, b_spec] + ([o_spec] if add is not None else [])
    args = (a, b) + ((add,) if add is not None else ())
    return pl.pallas_call(
        body, name=name, grid=(m // tm, n // tn, nk),
        in_specs=in_specs, out_specs=o_spec,
        out_shape=jax.ShapeDtypeStruct((m, n), out_dtype),
        scratch_shapes=[pltpu.VMEM((tm, tn), F32)] if nk > 1 else [],
        compiler_params=_cparams(("parallel", "parallel", "arbitrary"), min(VMEM_BIG, need + VMEM_SLACK)),
    )(*args)


TOK = 256


def _rms(x, g):
    return x * lax.rsqrt(jnp.mean(x * x, axis=-1, keepdims=True) + NORM_EPS) * g


def _rms_fwd(x, g, name):
    t, d = x.shape

    def body(x_ref, g_ref, o_ref):
        o_ref[...] = _rms(x_ref[...], g_ref[...]).astype(BF16)

    row = pl.BlockSpec((TOK, d), lambda i: (i, 0))
    return pl.pallas_call(body, name=name, grid=(t // TOK,), in_specs=[row, _full((1, d))], out_specs=row,
                          out_shape=jax.ShapeDtypeStruct((t, d), BF16),
                          compiler_params=_cparams(("parallel",)))(x, g)


def _rms_bwd(x, g, dxn, res, name):
    t, d = x.shape

    def body(x_ref, g_ref, d_ref, res_ref, dx_ref, dg_ref):
        _, vjp = jax.vjp(_rms, x_ref[...], g_ref[...])
        dx, dg = vjp(d_ref[...].astype(F32))
        dx_ref[...] = dx + res_ref[...]

        @pl.when(pl.program_id(0) == 0)
        def _():
            dg_ref[...] = jnp.zeros_like(dg_ref)

        dg_ref[...] += dg

    row = pl.BlockSpec((TOK, d), lambda i: (i, 0))
    return pl.pallas_call(body, name=name, grid=(t // TOK,), in_specs=[row, _full((1, d)), row, row],
                          out_specs=[row, _full((1, d))],
                          out_shape=[jax.ShapeDtypeStruct((t, d), F32), jax.ShapeDtypeStruct((1, d), F32)],
                          compiler_params=_cparams(("arbitrary",)))(x, g, dxn, res)


def _loss_head(x, g, tgt):
    t, d = x.shape

    def body(x_ref, g_ref, t_ref, l_ref, dx_ref, dg_ref):
        tg = t_ref[...]

        def fn(xv, gv):
            err = _rms(xv, gv) - tg
            per_tok = jnp.mean(err * err, axis=-1, keepdims=True)
            return 0.5 * jnp.sum(per_tok, axis=0, keepdims=True)

        l, vjp = jax.vjp(fn, x_ref[...], g_ref[...])
        dx, dg = vjp(jnp.ones((1, 1), F32))
        dx_ref[...] = dx

        @pl.when(pl.program_id(0) == 0)
        def _():
            dg_ref[...] = jnp.zeros_like(dg_ref)
            l_ref[...] = jnp.zeros_like(l_ref)

        dg_ref[...] += dg
        l_ref[...] += jnp.broadcast_to(l, l_ref.shape)

    row = pl.BlockSpec((TOK, d), lambda i: (i, 0))
    return pl.pallas_call(body, name="loss_head", grid=(t // TOK,), in_specs=[row, _full((1, d)), row],
                          out_specs=[_full((1, LANES)), row, _full((1, d))],
                          out_shape=[jax.ShapeDtypeStruct((1, LANES), F32), jax.ShapeDtypeStruct((t, d), F32),
                                     jax.ShapeDtypeStruct((1, d), F32)],
                          compiler_params=_cparams(("arbitrary",)))(x, g, tgt)


def _glu_fwd(x, z):
    t, d = x.shape

    def body(x_ref, v_ref, g_ref, o_ref):
        o_ref[...] = x_ref[...] + v_ref[...] * jax.nn.sigmoid(g_ref[...])

    row = pl.BlockSpec((TOK, d), lambda i: (i, 0))
    gate = pl.BlockSpec((TOK, d), lambda i: (i, 1))
    return pl.pallas_call(body, name="glu_fwd", grid=(t // TOK,), in_specs=[row, row, gate], out_specs=row,
                          out_shape=jax.ShapeDtypeStruct((t, d), F32),
                          compiler_params=_cparams(("parallel",)))(x, z, z)


def _glu_bwd(z, g):
    t, d = g.shape

    def body(v_ref, g_ref, d_ref, o_ref):
        s = jax.nn.sigmoid(g_ref[...])
        dy = d_ref[...]
        o_ref[:, :d] = (dy * s).astype(BF16)
        o_ref[:, d:] = (dy * v_ref[...] * s * (1.0 - s)).astype(BF16)

    row = pl.BlockSpec((TOK, d), lambda i: (i, 0))
    gate = pl.BlockSpec((TOK, d), lambda i: (i, 1))
    return pl.pallas_call(body, name="glu_bwd", grid=(t // TOK,), in_specs=[row, gate, row],
                          out_specs=pl.BlockSpec((TOK, 2 * d), lambda i: (i, 0)),
                          out_shape=jax.ShapeDtypeStruct((t, 2 * d), BF16),
                          compiler_params=_cparams(("parallel",)))(z, z, g)


def _shift_down(x, d):
    row = lax.broadcasted_iota(jnp.int32, x.shape, 0)
    return jnp.where(row < d, 0.0, pltpu.roll(x, d, 0))


def _shift_up(x, d):
    n = x.shape[0]
    row = lax.broadcasted_iota(jnp.int32, x.shape, 0)
    return jnp.where(row >= n - d, 0.0, pltpu.roll(x, n - d, 0))


def _make_sd():
    @functools.partial(jax.custom_vjp, nondiff_argnums=(1,))
    def sd(x, d):
        return _shift_down(x, d)

    def fwd(x, d):
        return _shift_down(x, d), None

    def bwd(d, _, g):
        return (_shift_up(g, d),)

    sd.defvjp(fwd, bwd)
    return sd


def _lin_scan(a, u, reverse=False):
    n = a.shape[0]
    row = lax.broadcasted_iota(jnp.int32, a.shape, 0)
    d = 1
    while d < n:
        if reverse:
            keep = row < n - d
            a_s, u_s = pltpu.roll(a, n - d, 0), pltpu.roll(u, n - d, 0)
        else:
            keep = row >= d
            a_s, u_s = pltpu.roll(a, d, 0), pltpu.roll(u, d, 0)
        u = u + a * jnp.where(keep, u_s, 0.0)
        a = a * jnp.where(keep, a_s, 1.0)
        d *= 2
    return u


def _make_scan():
    @jax.custom_vjp
    def scan(a, u):
        return _lin_scan(a, u)

    def fwd(a, u):
        h = _lin_scan(a, u)
        return h, (a, h)

    def bwd(res, dh):
        a, h = res
        g = _lin_scan(_shift_up(a, 1), dh, reverse=True)
        return g * _shift_down(h, 1), g

    scan.defvjp(fwd, bwd)
    return scan


def _acc_out(ref, val):
    @pl.when(pl.program_id(0) == 0)
    def _():
        ref[...] = jnp.zeros_like(ref)

    ref[...] += val


FFN_CW = 128


def _ffn_fn(hg, hv, wg, wv, bg, bv, sd):
    cg = wg[0:1] * sd(hg, 2) + wg[1:2] * sd(hg, 1) + wg[2:3] * hg + bg
    cv = wv[0:1] * sd(hv, 2) + wv[1:2] * sd(hv, 1) + wv[2:3] * hv + bv
    return jax.nn.silu(cg) * cv


def _ffn_specs(t):
    nb = D_FF // FFN_CW
    col = lambda r, off: pl.BlockSpec((r, FFN_CW), lambda j: (0, j + off))
    return nb, [col(t, 0), col(t, nb), col(3, 0), col(3, nb), col(1, 0), col(1, nb)], col


def _ffn_mid_fwd(h, cw, cb, name):
    t = h.shape[0]
    nb, in_specs, col = _ffn_specs(t)

    def body(hg, hv, wg, wv, bg, bv, o_ref):
        o_ref[...] = _ffn_fn(hg[...], hv[...], wg[...], wv[...], bg[...], bv[...], _shift_down).astype(BF16)

    return pl.pallas_call(body, name=name, grid=(nb,), in_specs=in_specs, out_specs=col(t, 0),
                          out_shape=jax.ShapeDtypeStruct((t, D_FF), BF16),
                          compiler_params=_cparams(("parallel",), VMEM_MID))(h, h, cw, cw, cb, cb)


def _ffn_mid_bwd(h, cw, cb, dact, name):
    t = h.shape[0]
    nb, in_specs, col = _ffn_specs(t)

    def body(hg, hv, wg, wv, bg, bv, d_ref, dhg, dhv, dwg, dwv, dbg, dbv):
        fn = functools.partial(_ffn_fn, sd=_make_sd())
        _, vjp = jax.vjp(fn, hg[...], hv[...], wg[...], wv[...], bg[...], bv[...])
        g = vjp(d_ref[...])
        dhg[...] = g[0].astype(BF16)
        dhv[...] = g[1].astype(BF16)
        dwg[...], dwv[...], dbg[...], dbv[...] = g[2], g[3], g[4], g[5]

    big = jax.ShapeDtypeStruct((t, D_FF), BF16)
    w3 = jax.ShapeDtypeStruct((3, D_FF), F32)
    b1 = jax.ShapeDtypeStruct((1, D_FF), F32)
    return pl.pallas_call(body, name=name, grid=(nb,), in_specs=in_specs + [col(t, 0)],
                          out_specs=[col(t, 0), col(t, 0), col(3, 0), col(3, 0), col(1, 0), col(1, 0)],
                          out_shape=[big, big, w3, w3, b1, b1],
                          compiler_params=_cparams(("parallel",), VMEM_BIG))(h, h, cw, cw, cb, cb, dact)


TS_CW = 256


def _tshift_fn(p, mu, sd):
    return p + mu * (sd(p, 1) - p)


def _tshift_fwd(p, mu):
    t = p.shape[0]
    col = lambda r: pl.BlockSpec((r, TS_CW), lambda j: (0, j))

    def body(p_ref, mu_ref, o_ref):
        o_ref[...] = _tshift_fn(p_ref[...], mu_ref[...], _shift_down)

    return pl.pallas_call(body, name="tshift_fwd", grid=(SHIFT_COLS // TS_CW,), in_specs=[col(t), col(1)],
                          out_specs=col(t), out_shape=jax.ShapeDtypeStruct((t, SHIFT_COLS), F32),
                          compiler_params=_cparams(("parallel",), VMEM_MID))(p, mu)


def _tshift_bwd(p, mu, dpam):
    t = p.shape[0]
    col = lambda r: pl.BlockSpec((r, TS_CW), lambda j: (0, j))

    def body(p_ref, mu_ref, d_ref, dp_ref, dmu_ref):
        _, vjp = jax.vjp(functools.partial(_tshift_fn, sd=_make_sd()), p_ref[...], mu_ref[...])
        dp, dmu = vjp(d_ref[...])
        dp_ref[...] = dp.astype(BF16)
        dmu_ref[...] = dmu

    return pl.pallas_call(body, name="tshift_bwd", grid=(SHIFT_COLS // TS_CW,), in_specs=[col(t), col(1), col(t)],
                          out_specs=[col(t), col(1)],
                          out_shape=[jax.ShapeDtypeStruct((t, SHIFT_COLS), BF16),
                                     jax.ShapeDtypeStruct((1, SHIFT_COLS), F32)],
                          compiler_params=_cparams(("parallel",), VMEM_MID))(p, mu, dpam)


_HI = lax.Precision.HIGHEST
_O = (0, RW, 2 * RW, 3 * RW, 3 * RW + W_LORA, 3 * RW + W_LORA + A_LORA, SHIFT_COLS)


def _seg(x, gm):
    return jnp.dot(x, gm, precision=_HI)


def _prep_fn(r, k, v, wd, ad, gd, w0, w2, a0, a2, g2, k_k, k_a, gm):
    w_log = -jax.nn.softplus(-(w0 + jnp.tanh(wd) @ w2)) - 0.5
    decay = jnp.exp(-jnp.exp(w_log))
    a = jax.nn.sigmoid(a0 + ad @ a2)
    g = jax.nn.sigmoid(gd) @ g2
    kk = k * k_k
    kk = kk / jnp.maximum(jnp.sqrt(_seg(kk * kk, gm)), 1e-12)
    k2 = k * (1.0 + (a - 1.0) * k_a)
    return r, decay, k2, v, -kk, kk * a, g


_PREP_W = ("w0", "w2", "a0", "a2", "g2", "k_k", "k_a")


def _prep_wspecs(w):
    return [_full(w[n].shape) for n in _PREP_W] + [_full((RW, RW))]


def _rwkv_prep_fwd(pam, w, gm):
    t = pam.shape[0]

    def body(p_ref, *refs):
        wr, outs = refs[:8], refs[8:]
        pieces = [p_ref[:, _O[i]:_O[i + 1]] for i in range(6)]
        res = _prep_fn(*pieces, *[x[...] for x in wr])
        for o, val in zip(outs, res):
            o[...] = val

    row = lambda c: pl.BlockSpec((TOK, c), lambda i: (i, 0))
    return pl.pallas_call(body, name="rwkv_prep_fwd", grid=(t // TOK,),
                          in_specs=[row(SHIFT_COLS)] + _prep_wspecs(w), out_specs=[row(RW)] * 7,
                          out_shape=[jax.ShapeDtypeStruct((t, RW), F32)] * 7,
                          compiler_params=_cparams(("parallel",), VMEM_MID))(pam, *[w[n] for n in _PREP_W], gm)


def _rwkv_prep_bwd(pam, w, gm, cts, more):
    t = pam.shape[0]

    def body(p_ref, *refs):
        wr, ct, ex, dp_ref, dws = refs[:8], refs[8:15], refs[15:18], refs[18], refs[19:]
        pieces = [p_ref[:, _O[i]:_O[i + 1]] for i in range(6)]
        fn = lambda *a: _prep_fn(*a, wr[7][...])
        _, vjp = jax.vjp(fn, *pieces, *[x[...] for x in wr[:7]])
        c = [x[...] for x in ct]
        c[0] = c[0] + ex[0][...]
        c[2] = c[2] + ex[1][...]
        c[3] = c[3] + ex[2][...]
        g = vjp(tuple(c))
        for i in range(6):
            dp_ref[:, _O[i]:_O[i + 1]] = g[i]
        for o, val in zip(dws, g[6:]):
            _acc_out(o, val)

    row = lambda c: pl.BlockSpec((TOK, c), lambda i: (i, 0))
    return pl.pallas_call(body, name="rwkv_prep_bwd", grid=(t // TOK,),
                          in_specs=[row(SHIFT_COLS)] + _prep_wspecs(w) + [row(RW)] * 10,
                          out_specs=[row(SHIFT_COLS)] + [_full(w[n].shape) for n in _PREP_W],
                          out_shape=[jax.ShapeDtypeStruct((t, SHIFT_COLS), F32)]
                          + [jax.ShapeDtypeStruct(w[n].shape, F32) for n in _PREP_W],
                          compiler_params=_cparams(("arbitrary",), VMEM_MID))(
                              pam, *[w[n] for n in _PREP_W], gm, *cts, *more)


def _post_fn(y, r, k2, v, g, ln_w, ln_b, r_k, gm):
    inv = 1.0 / HEAD
    d = y - _seg(y, gm) * inv
    yn = d * lax.rsqrt(_seg(d * d, gm) * inv + GN_EPS) * ln_w + ln_b
    bonus = _seg(r * k2 * r_k, gm) * v
    return (yn + bonus) * g


def _rwkv_post_fwd(y, r, k2, v, g, ln_w, ln_b, r_k, gm):
    t = y.shape[0]

    def body(*refs):
        o_ref = refs[-1]
        o_ref[...] = _post_fn(*[x[...] for x in refs[:-1]]).astype(BF16)

    row = pl.BlockSpec((TOK, RW), lambda i: (i, 0))
    return pl.pallas_call(body, name="rwkv_post_fwd", grid=(t // TOK,),
                          in_specs=[row] * 5 + [_full((1, RW))] * 3 + [_full((RW, RW))], out_specs=row,
                          out_shape=jax.ShapeDtypeStruct((t, RW), BF16),
                          compiler_params=_cparams(("parallel",), VMEM_MID))(y, r, k2, v, g, ln_w, ln_b, r_k, gm)


def _rwkv_post_bwd(y, r, k2, v, g, ln_w, ln_b, r_k, gm, dya):
    t = y.shape[0]

    def body(*refs):
        ins, gm_ref, d_ref, outs = refs[:8], refs[8], refs[9], refs[10:]
        fn = lambda *a: _post_fn(*a, gm_ref[...])
        _, vjp = jax.vjp(fn, *[x[...] for x in ins])
        gr = vjp(d_ref[...])
        for o, val in zip(outs[:5], gr[:5]):
            o[...] = val
        for o, val in zip(outs[5:], gr[5:]):
            _acc_out(o, val)

    row = pl.BlockSpec((TOK, RW), lambda i: (i, 0))
    vec = _full((1, RW))
    return pl.pallas_call(body, name="rwkv_post_bwd", grid=(t // TOK,),
                          in_specs=[row] * 5 + [vec] * 3 + [_full((RW, RW)), row],
                          out_specs=[row] * 5 + [vec] * 3,
                          out_shape=[jax.ShapeDtypeStruct((t, RW), F32)] * 5 + [jax.ShapeDtypeStruct((1, RW), F32)] * 3,
                          compiler_params=_cparams(("arbitrary",), VMEM_MID))(y, r, k2, v, g, ln_w, ln_b, r_k, gm, dya)


def _from_pt(x):
    n = x.shape[0]
    return x.reshape(n, HEAD, N_HEADS, PT).transpose(0, 3, 2, 1).reshape(n * PT, N_HEADS * HEAD)


def _lane_sum(x):
    return jnp.sum(x, axis=-1, keepdims=True)


def _pair_consts():
    lane = lax.broadcasted_iota(jnp.int32, (HEAD, LANES), 1)
    return lane, lane < HEAD


def _seg_sum_pair(x, first):
    return jnp.where(first, _lane_sum(jnp.where(first, x, 0.0)), _lane_sum(jnp.where(first, 0.0, x)))


def _to_pt(x):
    t = x.shape[0]
    return x.reshape(t // PT, PT, N_HEADS, HEAD).transpose(0, 3, 2, 1).reshape(t // PT, HEAD, N_HEADS * PT)


def _expand_cols(x, name):
    t = x.shape[0]
    tiles = WKV_CHUNK // PT

    def body(x_ref, o_ref):
        _, first = _pair_consts()
        for tl in range(tiles):
            tile = x_ref[tl]
            for j in range(PT):
                for p in range(N_HEADS // 2):
                    src = jnp.where(first, (2 * p) * PT + j, (2 * p + 1) * PT + j)
                    o_ref[tl * PT + j, :, p * LANES:(p + 1) * LANES] = jnp.take_along_axis(tile, src, axis=1)

    return pl.pallas_call(
        body, name=name, grid=(t // WKV_CHUNK,),
        in_specs=[pl.BlockSpec((tiles, HEAD, LANES), lambda i: (i, 0, 0))],
        out_specs=pl.BlockSpec((WKV_CHUNK, HEAD, RW), lambda i: (i, 0, 0)),
        out_shape=jax.ShapeDtypeStruct((t, HEAD, RW), F32),
        compiler_params=_cparams(("parallel",), VMEM_MID))(_to_pt(x))


def _wkv_fwd(w, k, z, b, v_exp):
    t = w.shape[0]
    nc = t // WKV_CHUNK
    pairs = N_HEADS // 2

    def body(w_ref, k_ref, z_ref, b_ref, v_ref, s_all, s_ref):
        @pl.when(pl.program_id(0) == 0)
        def _():
            s_ref[...] = jnp.zeros_like(s_ref)

        _, first = _pair_consts()

        def group(gi, carry):
            base = pl.multiple_of(gi * 8, 8)
            rows = [ref[pl.ds(base, 8), :] for ref in (w_ref, k_ref, z_ref, b_ref)]
            s = [s_ref[:, p * LANES:(p + 1) * LANES] for p in range(pairs)]
            for jj in range(8):
                for p in range(pairs):
                    cs = slice(p * LANES, (p + 1) * LANES)
                    wr, kr, zr, br = [x[jj:jj + 1, cs] for x in rows]
                    s_all[base + jj, :, cs] = s[p]
                    sa = _seg_sum_pair(s[p] * zr, first)
                    s[p] = s[p] * wr + sa * br + v_ref[base + jj, :, cs] * kr
            for p in range(pairs):
                s_ref[:, p * LANES:(p + 1) * LANES] = s[p]
            return carry

        lax.fori_loop(0, WKV_CHUNK // 8, group, 0)

    row = pl.BlockSpec((WKV_CHUNK, RW), lambda i: (i, 0))
    big = pl.BlockSpec((WKV_CHUNK, HEAD, RW), lambda i: (i, 0, 0))
    return pl.pallas_call(
        body, name="wkv_fwd", grid=(nc,), in_specs=[row] * 4 + [big], out_specs=[big, _full((HEAD, RW))],
        out_shape=[jax.ShapeDtypeStruct((t, HEAD, RW), F32), jax.ShapeDtypeStruct((HEAD, RW), F32)],
        compiler_params=_cparams(("arbitrary",), VMEM_MID))(w, k, z, b, v_exp)


def _wkv_out(r, s_all, s_last):
    t = r.shape[0]
    nc = t // WKV_CHUNK
    tiles = WKV_CHUNK // PT
    pairs = N_HEADS // 2

    def body(r_ref, s_ref, nxt_ref, last_ref, y_ref):
        lane, first = _pair_consts()
        after = jnp.where(pl.program_id(0) == nc - 1, last_ref[...], nxt_ref[0])
        for tl in range(tiles):
            ytile = jnp.zeros((HEAD, LANES), F32)
            for g in range(PT // 8):
                rows = r_ref[tl * PT + g * 8:tl * PT + g * 8 + 8, :]
                for jj in range(8):
                    tt = tl * PT + g * 8 + jj
                    j = g * 8 + jj
                    for p in range(pairs):
                        cs = slice(p * LANES, (p + 1) * LANES)
                        s = s_ref[tt + 1, :, cs] if tt + 1 < WKV_CHUNK else after[:, cs]
                        pr = s * rows[jj:jj + 1, cs]
                        y0 = _lane_sum(jnp.where(first, pr, 0.0))
                        y1 = _lane_sum(jnp.where(first, 0.0, pr))
                        ytile = jnp.where(lane == (2 * p) * PT + j, y0, ytile)
                        ytile = jnp.where(lane == (2 * p + 1) * PT + j, y1, ytile)
            y_ref[tl] = ytile

    row = pl.BlockSpec((WKV_CHUNK, RW), lambda i: (i, 0))
    pt = pl.BlockSpec((tiles, HEAD, LANES), lambda i: (i, 0, 0))
    big = pl.BlockSpec((WKV_CHUNK, HEAD, RW), lambda i: (i, 0, 0))
    nxt = pl.BlockSpec((1, HEAD, RW), lambda i: (jnp.minimum((i + 1) * WKV_CHUNK, t - 1), 0, 0))
    return pl.pallas_call(
        body, name="wkv_out", grid=(nc,), in_specs=[row, big, nxt, _full((HEAD, RW))], out_specs=pt,
        out_shape=jax.ShapeDtypeStruct((t // PT, HEAD, LANES), F32),
        compiler_params=_cparams(("parallel",), VMEM_MID))(r, s_all, s_all, s_last)


def _wkv_bwd(r, w, k, z, b, v_exp, s_all, dy_exp):
    t = r.shape[0]
    nc = t // WKV_CHUNK
    tiles = WKV_CHUNK // PT
    pairs = N_HEADS // 2

    def body(r_ref, w_ref, k_ref, z_ref, b_ref, v_ref, s_all_ref, dy_ref,
             dr_ref, dw_ref, dk_ref, dz_ref, db_ref, dv_ref, ds_ref):
        @pl.when(pl.program_id(0) == 0)
        def _():
            ds_ref[...] = jnp.zeros_like(ds_ref)

        lane, first = _pair_consts()
        col_sum = lambda x: jnp.sum(x, axis=0, keepdims=True)
        row8 = lax.broadcasted_iota(jnp.int32, (8, LANES), 0)
        for tl in reversed(range(tiles)):
            def group(gg, dvtile):
                gi = PT // 8 - 1 - gg
                base = pl.multiple_of(tl * PT + gi * 8, 8)
                rows = [ref[pl.ds(base, 8), :] for ref in (r_ref, w_ref, k_ref, z_ref, b_ref)]
                outs = (dr_ref, dw_ref, dk_ref, dz_ref, db_ref)
                tiles8 = {(id(o), p): jnp.zeros((8, LANES), F32) for o in outs for p in range(pairs)}
                ds = [ds_ref[:, p * LANES:(p + 1) * LANES] for p in range(pairs)]
                for jj in reversed(range(8)):
                    j = gi * 8 + jj
                    for p in range(pairs):
                        cs = slice(p * LANES, (p + 1) * LANES)

                        def put(ref, val, p=p, jj=jj):
                            tiles8[(id(ref), p)] = jnp.where(row8 == jj, val, tiles8[(id(ref), p)])

                        rr, wr, kr, zr, br = [x[jj:jj + 1, cs] for x in rows]
                        sp = s_all_ref[base + jj, :, cs]
                        vc = v_ref[base + jj, :, cs]
                        dyc = dy_ref[base + jj, :, cs]
                        sa = _seg_sum_pair(sp * zr, first)
                        st = sp * wr + sa * br + vc * kr
                        d = ds[p] + dyc * rr
                        put(dr_ref, col_sum(st * dyc))
                        dvk = d * kr
                        dv0 = _lane_sum(jnp.where(first, dvk, 0.0))
                        dv1 = _lane_sum(jnp.where(first, 0.0, dvk))
                        dvtile = jnp.where(lane == (2 * p) * PT + j, dv0, dvtile)
                        dvtile = jnp.where(lane == (2 * p + 1) * PT + j, dv1, dvtile)
                        put(dk_ref, col_sum(d * vc))
                        put(dw_ref, col_sum(sp * d))
                        u = _seg_sum_pair(d * br, first)
                        put(dz_ref, col_sum(sp * u))
                        put(db_ref, col_sum(d * sa))
                        ds[p] = d * wr + u * zr
                for p in range(pairs):
                    ds_ref[:, p * LANES:(p + 1) * LANES] = ds[p]
                for o in outs:
                    for p in range(pairs):
                        o[pl.ds(base, 8), p * LANES:(p + 1) * LANES] = tiles8[(id(o), p)]
                return dvtile

            dv_ref[tl] = lax.fori_loop(0, PT // 8, group, jnp.zeros((HEAD, LANES), F32))

    rev = lambda i: nc - 1 - i
    row = pl.BlockSpec((WKV_CHUNK, RW), lambda i: (rev(i), 0))
    pt = pl.BlockSpec((tiles, HEAD, LANES), lambda i: (rev(i), 0, 0))
    big = pl.BlockSpec((WKV_CHUNK, HEAD, RW), lambda i: (rev(i), 0, 0))
    return pl.pallas_call(
        body, name="wkv_bwd", grid=(nc,), in_specs=[row] * 5 + [big, big, big], out_specs=[row] * 5 + [pt],
        out_shape=[jax.ShapeDtypeStruct((t, RW), F32)] * 5 + [jax.ShapeDtypeStruct((t // PT, HEAD, LANES), F32)],
        scratch_shapes=[pltpu.VMEM((HEAD, RW), F32)],
        compiler_params=_cparams(("arbitrary",), VMEM_BIG))(r, w, k, z, b, v_exp, s_all, dy_exp)


LRU_CW = 128
_BX0 = SHIFT_COLS // LRU_CW
_BG0 = (SHIFT_COLS + LRU_W) // LRU_CW


def _lru_fn(bx, bg, cw, cb, ga, ba, gx, bxb, lam, sd, scan):
    xc = cw[0:1] * sd(bx, 3) + cw[1:2] * sd(bx, 2) + cw[2:3] * sd(bx, 1) + cw[3:4] * bx + cb
    gr = jax.nn.sigmoid(xc @ ga + ba)
    gi = jax.nn.sigmoid(xc @ gx + bxb)
    log_a = -LRU_C * gr * jax.nn.softplus(-lam)
    a = jnp.exp(log_a)
    mult = jnp.sqrt(-jnp.tanh(log_a) * (jnp.exp(2.0 * log_a) + 1.0))
    return scan(a, xc * gi * mult) * jax.nn.gelu(bg)


def _lru_specs(t):
    col = lambda r, off=0: pl.BlockSpec((r, LRU_CW), lambda j: (0, j + off))
    diag = pl.BlockSpec((LRU_CW, LRU_CW), lambda j: (j, j))
    return col, [col(t, _BX0), col(t, _BG0), col(4), col(1), diag, col(1), diag, col(1), col(1)]


def _lru_fwd(p, cw, cb, ga, ba, gx, bxb, lam):
    t = p.shape[0]
    col, in_specs = _lru_specs(t)

    def body(*refs):
        o_ref = refs[-1]
        o_ref[...] = _lru_fn(*[x[...] for x in refs[:-1]], _shift_down, _lin_scan).astype(BF16)

    return pl.pallas_call(body, name="lru_fwd", grid=(LRU_W // LRU_CW,), in_specs=in_specs, out_specs=col(t),
                          out_shape=jax.ShapeDtypeStruct((t, LRU_W), BF16),
                          compiler_params=_cparams(("parallel",), VMEM_MID))(p, p, cw, cb, ga, ba, gx, bxb, lam)


def _lru_bwd(p, cw, cb, ga, ba, gx, bxb, lam, dyb):
    t = p.shape[0]
    col, in_specs = _lru_specs(t)

    def body(*refs):
        ins, d_ref, outs = refs[:9], refs[9], refs[10:]
        fn = functools.partial(_lru_fn, sd=_make_sd(), scan=_make_scan())
        _, vjp = jax.vjp(fn, *[x[...] for x in ins])
        g = vjp(d_ref[...])
        outs[0][...] = g[0].astype(BF16)
        outs[1][...] = g[1].astype(BF16)
        for o, val in zip(outs[2:], g[2:]):
            o[...] = val

    sq = pl.BlockSpec((LRU_CW, LRU_CW), lambda j: (j, 0))
    act = jax.ShapeDtypeStruct((t, LRU_W), BF16)
    vec = jax.ShapeDtypeStruct((1, LRU_W), F32)
    sqs = jax.ShapeDtypeStruct((LRU_W, LRU_CW), F32)
    return pl.pallas_call(body, name="lru_bwd", grid=(LRU_W // LRU_CW,), in_specs=in_specs + [col(t, RW // LRU_CW)],
                          out_specs=[col(t), col(t), col(4), col(1), sq, col(1), sq, col(1), col(1)],
                          out_shape=[act, act, jax.ShapeDtypeStruct((4, LRU_W), F32), vec, sqs, vec, sqs, vec, vec],
                          compiler_params=_cparams(("parallel",), VMEM_BIG))(p, p, cw, cb, ga, ba, gx, bxb, lam, dyb)


def _s5_disc_fn(a_re, a_im, log_dt, b_re, b_im, e):
    lam_re = jnp.minimum(a_re, -1e-4)
    lam_im = a_im
    dt = jnp.exp(log_dt)
    mag = jnp.exp(lam_re * dt)
    ab_re = mag * jnp.cos(lam_im * dt)
    ab_im = mag * jnp.sin(lam_im * dt)
    den = lam_re * lam_re + lam_im * lam_im
    zr = ab_re - 1.0
    q_re = jnp.dot((zr * lam_re + ab_im * lam_im) / den, e, precision=_HI)
    q_im = jnp.dot((ab_im * lam_re - zr * lam_im) / den, e, precision=_HI)
    return ab_re, ab_im, q_re * b_re - q_im * b_im, q_re * b_im + q_im * b_re


def _s5_disc_fwd(a_re, a_im, log_dt, b_re, b_im, e):
    def body(*refs):
        res = _s5_disc_fn(*[x[...] for x in refs[:6]])
        for o, val in zip(refs[6:], res):
            o[...] = val

    small = jax.ShapeDtypeStruct(a_re.shape, F32)
    wide = jax.ShapeDtypeStruct(b_re.shape, F32)
    return pl.pallas_call(body, name="s5_disc_fwd", out_shape=[small, small, wide, wide])(
        a_re, a_im, log_dt, b_re, b_im, e)


def _s5_disc_bwd(a_re, a_im, log_dt, b_re, b_im, e, cts):
    def body(*refs):
        ins, e_ref, ct, outs = refs[:5], refs[5], refs[6:10], refs[10:]
        _, vjp = jax.vjp(lambda *a: _s5_disc_fn(*a, e_ref[...]), *[x[...] for x in ins])
        for o, val in zip(outs, vjp(tuple(c[...] for c in ct))):
            o[...] = val

    shapes = [jax.ShapeDtypeStruct(x.shape, F32) for x in (a_re, a_im, log_dt, b_re, b_im)]
    return pl.pallas_call(body, name="s5_disc_bwd", out_shape=shapes)(a_re, a_im, log_dt, b_re, b_im, e, *cts)


def _cmul(a, b):
    return a[0] * b[0] - a[1] * b[1], a[0] * b[1] + a[1] * b[0]


def _s5_scan(sr, si, ab, reverse):
    n_tiles = sr.shape[0] // 8
    width = sr.shape[1]
    row8 = lax.broadcasted_iota(jnp.int32, (8, width), 0)
    p1 = ab
    p2 = _cmul(p1, p1)
    p4 = _cmul(p2, p2)
    pw = [p1]
    for _ in range(7):
        pw.append(_cmul(pw[-1], p1))
    cr = jnp.zeros((8, width), F32)
    ci = jnp.zeros((8, width), F32)
    for j in range(8):
        e = pw[7 - j] if reverse else pw[j]
        cr = jnp.where(row8 == j, e[0], cr)
        ci = jnp.where(row8 == j, e[1], ci)

    levels = []
    for d, q in ((1, p1), (2, p2), (4, p4)):
        keep = row8 < 8 - d if reverse else row8 >= d
        levels.append((d, (jnp.where(keep, q[0], 0.0), jnp.where(keep, q[1], 0.0))))

    def tile(i, carry):
        idx = n_tiles - 1 - i if reverse else i
        base = pl.multiple_of(idx * 8, 8)
        x = (sr[pl.ds(base, 8), :], si[pl.ds(base, 8), :])
        for d, q in levels:
            amt = 8 - d if reverse else d
            m = _cmul(q, (pltpu.roll(x[0], amt, 0), pltpu.roll(x[1], amt, 0)))
            x = (x[0] + m[0], x[1] + m[1])
        m = _cmul((cr, ci), carry)
        x = (x[0] + m[0], x[1] + m[1])
        sr[pl.ds(base, 8), :] = x[0]
        si[pl.ds(base, 8), :] = x[1]
        edge = slice(0, 1) if reverse else slice(7, 8)
        return x[0][edge], x[1][edge]

    zero = jnp.zeros((1, width), F32)
    lax.fori_loop(0, n_tiles, tile, (zero, zero))


_S5_W = S5_SLAB // S5_GROUP * S5_STATE


def _s5_specs(t):
    col = lambda r: pl.BlockSpec((r, S5_SLAB), lambda j: (0, j))
    bb = pl.BlockSpec((None, S5_SLAB, _S5_W), lambda j: (j, 0, 0))
    cd = pl.BlockSpec((None, _S5_W, S5_SLAB), lambda j: (j, 0, 0))
    ab = pl.BlockSpec((None, 1, _S5_W), lambda j: (j, 0, 0))
    return col, bb, cd, ab


def _s5_fwd(u, dvec, bbr, bbi, cdr, cdi, abr, abi):
    t, width = u.shape
    col, bb, cd, ab = _s5_specs(t)

    def body(u_ref, d_ref, bbr_ref, bbi_ref, cdr_ref, cdi_ref, abr_ref, abi_ref, o_ref, sr, si):
        uv = u_ref[...]
        sr[...] = jnp.dot(uv, bbr_ref[...], preferred_element_type=F32)
        si[...] = jnp.dot(uv, bbi_ref[...], preferred_element_type=F32)
        _s5_scan(sr, si, (abr_ref[...], abi_ref[...]), False)
        y = jnp.dot(sr[...], cdr_ref[...], preferred_element_type=F32) - jnp.dot(si[...], cdi_ref[...],
                                                                                 preferred_element_type=F32)
        o_ref[...] = jax.nn.gelu(y + d_ref[...] * uv).astype(BF16)

    return pl.pallas_call(body, name="s5_fwd", grid=(width // S5_SLAB,),
                          in_specs=[col(t), col(1), bb, bb, cd, cd, ab, ab], out_specs=col(t),
                          out_shape=jax.ShapeDtypeStruct((t, width), BF16),
                          scratch_shapes=[pltpu.VMEM((t, _S5_W), F32)] * 2,
                          compiler_params=_cparams(("parallel",), VMEM_BIG))(u, dvec, bbr, bbi, cdr, cdi, abr, abi)


def _s5_bwd(u, dvec, bbr, bbi, cdr, cdi, abr, abi, dyact):
    t, width = u.shape
    col, bb, cd, ab = _s5_specs(t)
    ns = width // S5_SLAB
    tn = (((0,), (0,)), ((), ()))
    nt = (((1,), (1,)), ((), ()))

    def body(u_ref, d_ref, bbr_ref, bbi_ref, cdr_ref, cdi_ref, abr_ref, abi_ref, dy_ref,
             du_ref, dd_ref, dbbr_ref, dbbi_ref, dcdr_ref, dcdi_ref, dabr_ref, dabi_ref, sr, si, gr, gi):
        uv = u_ref[...]
        dv = d_ref[...]
        abv = (abr_ref[...], abi_ref[...])
        sr[...] = jnp.dot(uv, bbr_ref[...], preferred_element_type=F32)
        si[...] = jnp.dot(uv, bbi_ref[...], preferred_element_type=F32)
        _s5_scan(sr, si, abv, False)
        y = jnp.dot(sr[...], cdr_ref[...], preferred_element_type=F32) - jnp.dot(si[...], cdi_ref[...],
                                                                                 preferred_element_type=F32)
        _, vjp = jax.vjp(jax.nn.gelu, y + dv * uv)
        (dpre,) = vjp(dy_ref[...].astype(F32))
        dd_ref[...] = jnp.sum(dpre * uv, axis=0, keepdims=True)
        dcdr_ref[...] = lax.dot_general(sr[...], dpre, tn, preferred_element_type=F32)
        dcdi_ref[...] = -lax.dot_general(si[...], dpre, tn, preferred_element_type=F32)
        gr[...] = lax.dot_general(dpre, cdr_ref[...], nt, preferred_element_type=F32)
        gi[...] = -lax.dot_general(dpre, cdi_ref[...], nt, preferred_element_type=F32)
        _s5_scan(gr, gi, (abv[0], -abv[1]), True)

        row8 = lax.broadcasted_iota(jnp.int32, (8, _S5_W), 0)

        def tile(i, carry):
            acc_r, acc_i, last_r, last_i = carry
            base = pl.multiple_of(i * 8, 8)
            s_r, s_i = sr[pl.ds(base, 8), :], si[pl.ds(base, 8), :]
            g_r, g_i = gr[pl.ds(base, 8), :], gi[pl.ds(base, 8), :]
            p_r = jnp.where(row8 == 0, last_r, pltpu.roll(s_r, 1, 0))
            p_i = jnp.where(row8 == 0, last_i, pltpu.roll(s_i, 1, 0))
            acc_r = acc_r + jnp.sum(g_r * p_r + g_i * p_i, axis=0, keepdims=True)
            acc_i = acc_i + jnp.sum(g_i * p_r - g_r * p_i, axis=0, keepdims=True)
            return acc_r, acc_i, s_r[7:8], s_i[7:8]

        zero = jnp.zeros((1, _S5_W), F32)
        acc_r, acc_i, _, _ = lax.fori_loop(0, t // 8, tile, (zero, zero, zero, zero))
        dabr_ref[...] = acc_r
        dabi_ref[...] = acc_i
        du_ref[...] = (dpre * dv + lax.dot_general(gr[...], bbr_ref[...], nt, preferred_element_type=F32)
                       + lax.dot_general(gi[...], bbi_ref[...], nt, preferred_element_type=F32))
        dbbr_ref[...] = lax.dot_general(uv, gr[...], tn, preferred_element_type=F32)
        dbbi_ref[...] = lax.dot_general(uv, gi[...], tn, preferred_element_type=F32)

    sds = jax.ShapeDtypeStruct
    return pl.pallas_call(
        body, name="s5_bwd", grid=(ns,), in_specs=[col(t), col(1), bb, bb, cd, cd, ab, ab, col(t)],
        out_specs=[col(t), col(1), bb, bb, cd, cd, ab, ab],
        out_shape=[sds((t, width), F32), sds((1, width), F32), sds((ns, S5_SLAB, _S5_W), F32),
                   sds((ns, S5_SLAB, _S5_W), F32), sds((ns, _S5_W, S5_SLAB), F32), sds((ns, _S5_W, S5_SLAB), F32),
                   sds((ns, 1, _S5_W), F32), sds((ns, 1, _S5_W), F32)],
        scratch_shapes=[pltpu.VMEM((t, _S5_W), F32)] * 4,
        compiler_params=_cparams(("parallel",), VMEM_BIG))(u, dvec, bbr, bbi, cdr, cdi, abr, abi, dyact)


def _gate_dense(w):
    h = w.shape[0]
    return jnp.einsum("hij,hg->higj", w, jnp.eye(h, dtype=F32)).reshape(h * HEAD, h * HEAD)


def _gate_blocks(d):
    x = d.reshape(LRU_W // LRU_CW, 2, HEAD, 2, HEAD)
    return jnp.einsum("tgihj,gh->tgij", x, jnp.eye(2, dtype=F32)).reshape(LRU_W // HEAD, HEAD, HEAD)


_GPS = S5_SLAB // S5_GROUP
_NS = S5_GROUPS // _GPS


def _s5_in_dense(bb):
    x = bb.reshape(_NS, _GPS, S5_STATE, S5_GROUP)
    return jnp.einsum("sgnc,gh->sgchn", x, jnp.eye(_GPS, dtype=F32)).reshape(_NS, S5_SLAB, _S5_W)


def _s5_in_blocks(d):
    x = d.reshape(_NS, _GPS, S5_GROUP, _GPS, S5_STATE)
    return jnp.einsum("sgchn,gh->sgnc", x, jnp.eye(_GPS, dtype=F32)).reshape(S5_GROUPS, S5_STATE * S5_GROUP)


def _s5_out_dense(c):
    x = c.reshape(_NS, _GPS, S5_GROUP, S5_STATE)
    return jnp.einsum("sgcn,gh->shngc", x, jnp.eye(_GPS, dtype=F32)).reshape(_NS, _S5_W, S5_SLAB)


def _s5_out_blocks(d):
    x = d.reshape(_NS, _GPS, S5_STATE, _GPS, S5_GROUP)
    return jnp.einsum("shngc,gh->sgcn", x, jnp.eye(_GPS, dtype=F32)).reshape(S5_GROUPS, S5_GROUP, S5_STATE)


def _local_step(x, tgt, w, late_weights, send_grads):
    d_model = x.shape[1]
    gs = {}
    gm = jnp.kron(jnp.eye(N_HEADS, dtype=F32), jnp.ones((HEAD, HEAD), F32))
    n_layers = w["f_norm_g"].shape[0]

    def ffn_fwd(xin, l):
        xn = _rms_fwd(xin, w["f_norm_g"][l:l + 1], f"rms_f{l}")
        h = _matmul(xn, w["f_w_up_t"][l], "nt", f"mm_f{l}_up")
        act = _ffn_mid_fwd(h, w["f_conv_w"][l], w["f_conv_b"][l:l + 1], f"ffn_mid_fwd{l}")
        return _matmul(act, w["f_w_down"][l], "nn", f"mm_f{l}_down", add=xin), (xin, xn, h, act)

    def ffn_bwd(g, saved, l):
        xin, xn, h, act = saved
        dact = _matmul(g, w["f_w_down"][l], "nt", f"mm_f{l}_dact")
        d_down = _matmul(act, g, "tn", f"mm_f{l}_ddown", out_dtype=BF16)
        dhg, dhv, dwg, dwv, dbg, dbv = _ffn_mid_bwd(h, w["f_conv_w"][l], w["f_conv_b"][l:l + 1], dact,
                                                    f"ffn_mid_bwd{l}")
        dh = jnp.concatenate([dhg, dhv], axis=1)
        dxn = _matmul(dh, w["f_w_up_t"][l], "nn", f"mm_f{l}_dxn")
        d_up = _matmul(dh, xn, "tn", f"mm_f{l}_dup", out_dtype=BF16)
        dx, dgn = _rms_bwd(xin, w["f_norm_g"][l:l + 1], dxn, g, f"rms_f{l}_bwd")
        return dx, d_up, d_down, jnp.concatenate([dwg, dwv], axis=1), jnp.concatenate([dbg, dbv], axis=1), dgn

    xn0 = _rms_fwd(x, w["e_norm_g"], "rms_e")
    p = _matmul(xn0, w["e_w_in_t"], "nt", "mm_e_in")
    pam = _tshift_fwd(p, w["e_mu"])
    pw = dict(w0=w["e_w0"], w2=w["e_w2"][0], a0=w["e_a0"], a2=w["e_a2"][0], g2=w["e_g2"][0],
              k_k=w["e_k_k"], k_a=w["e_k_a"])
    r, dec, k2, v, z, b, gate = _rwkv_prep_fwd(pam, pw, gm)
    v_exp = _expand_cols(v, "wkv_expand_v")
    s_all, s_last = _wkv_fwd(dec, k2, z, b, v_exp)
    y_pt = _wkv_out(r, s_all, s_last)
    y = _from_pt(y_pt)
    rk = w["e_r_k"].reshape(1, RW)
    ya = _rwkv_post_fwd(y, r, k2, v, gate, w["e_ln_w"], w["e_ln_b"], rk, gm)
    ga, gx = _gate_dense(w["e_gate_a_w"][0]), _gate_dense(w["e_gate_x_w"][0])
    lru_w = (w["e_conv_w"][0], w["e_conv_b"], ga, w["e_gate_a_b"], gx, w["e_gate_x_b"], w["e_lru_lambda"])
    yb = _lru_fwd(p, *lru_w)
    ycat = jnp.concatenate([ya, yb], axis=1)
    x1 = _matmul(ycat, w["e_w_out"], "nn", "mm_e_out", add=x)
    w = {**w, **late_weights(x1)}
    x2, ffn0 = ffn_fwd(x1, 0)

    xn1 = _rms_fwd(x2, w["o_norm_g"], "rms_o")
    u = _matmul(xn1, w["o_w_in"], "nn", "mm_o_in")
    expand = jnp.kron(jnp.eye(S5_STATE, dtype=F32), jnp.ones((1, S5_GROUP), F32))
    disc_in = (w["o_A_re"][0], w["o_A_im"][0], w["o_log_dt"].reshape(S5_GROUPS, 1),
               w["o_B_re"][0].reshape(S5_GROUPS, -1), w["o_B_im"][0].reshape(S5_GROUPS, -1), expand)
    ab_re, ab_im, bb_re, bb_im = _s5_disc_fwd(*disc_in)
    s5_w = (w["o_D"], _s5_in_dense(bb_re), _s5_in_dense(bb_im), _s5_out_dense(w["o_C_re"][0]),
            _s5_out_dense(w["o_C_im"][0]), ab_re.reshape(_NS, 1, _S5_W), ab_im.reshape(_NS, 1, _S5_W))
    yact = _s5_fwd(u, *s5_w)
    zz = _matmul(yact, w["o_w_glu_t"], "nt", "mm_o_glu")
    x3 = _glu_fwd(x2, zz)
    x4, ffn1 = ffn_fwd(x3, 1)

    loss, g, gs["final_norm_g"] = _loss_head(x4, w["final_norm_g"].reshape(1, d_model), tgt)
    gs["final_norm_g"] = gs["final_norm_g"].reshape(d_model)

    g, up1, down1, dcw1, dcb1, dfn1 = ffn_bwd(g, ffn1, 1)
    dz = _glu_bwd(zz, g)
    dyact = _matmul(dz, w["o_w_glu_t"], "nn", "mm_o_dyact")
    d_glu = _matmul(dz, yact, "tn", "mm_o_dglu", out_dtype=BF16)
    du, gs["o_D"], dbbr, dbbi, dcdr, dcdi, dabr, dabi = _s5_bwd(u, *s5_w, dyact)
    gs["o_C_re"] = _s5_out_blocks(dcdr)[None]
    gs["o_C_im"] = _s5_out_blocks(dcdi)[None]
    cts = (dabr.reshape(S5_GROUPS, S5_STATE), dabi.reshape(S5_GROUPS, S5_STATE), _s5_in_blocks(dbbr),
           _s5_in_blocks(dbbi))
    da_re, da_im, dlog_dt, db_re, db_im = _s5_disc_bwd(*disc_in, cts)
    gs["o_A_re"], gs["o_A_im"], gs["o_log_dt"] = da_re[None], da_im[None], dlog_dt.reshape(1, S5_GROUPS)
    gs["o_B_re"] = db_re.reshape(w["o_B_re"].shape)
    gs["o_B_im"] = db_im.reshape(w["o_B_im"].shape)
    dxn = _matmul(du, w["o_w_in"], "nt", "mm_o_dxn")
    d_oin = _matmul(xn1, du, "tn", "mm_o_din", out_dtype=BF16)
    g, gs["o_norm_g"] = _rms_bwd(x2, w["o_norm_g"], dxn, g, "rms_o_bwd")
    g = send_grads("a", [("f_w_up", 1, up1), ("f_w_down", 1, down1), ("o_w_glu", 0, d_glu), ("o_w_in", 0, d_oin)], g)

    g, up0, down0, dcw0, dcb0, dfn0 = ffn_bwd(g, ffn0, 0)
    gs["f_conv_w"] = jnp.stack([dcw0, dcw1])
    gs["f_conv_b"] = jnp.concatenate([dcb0, dcb1], axis=0)
    gs["f_norm_g"] = jnp.concatenate([dfn0, dfn1], axis=0)

    dycat = _matmul(g, w["e_w_out"], "nt", "mm_e_dycat")
    d_eout = _matmul(ycat, g, "tn", "mm_e_dout", out_dtype=BF16)
    dycat = send_grads("b", [("f_w_up", 0, up0), ("f_w_down", 0, down0), ("e_w_out", 0, d_eout)], dycat)
    dy, dr1, dk1, dv1, dgate, gs["e_ln_w"], gs["e_ln_b"], drk = _rwkv_post_bwd(
        y, r, k2, v, gate, w["e_ln_w"], w["e_ln_b"], rk, gm, dycat)
    gs["e_r_k"] = drk.reshape(w["e_r_k"].shape)
    dr2, ddec, dk2, dzz, dbb, dv_pt = _wkv_bwd(r, dec, k2, z, b, v_exp, s_all, _expand_cols(dy, "wkv_expand_dy"))
    dpam, gs["e_w0"], dw2, gs["e_a0"], da2, dg2, gs["e_k_k"], gs["e_k_a"] = _rwkv_prep_bwd(
        pam, pw, gm, (dr2, ddec, dk2, _from_pt(dv_pt), dzz, dbb, dgate), (dr1, dk1, dv1))
    gs["e_w2"], gs["e_a2"], gs["e_g2"] = dw2[None], da2[None], dg2[None]
    dpa, gs["e_mu"] = _tshift_bwd(p, w["e_mu"], dpam)
    dbx, dbg, dcw, gs["e_conv_b"], dga, gs["e_gate_a_b"], dgx, gs["e_gate_x_b"], gs["e_lru_lambda"] = _lru_bwd(
        p, *lru_w, dycat)
    gs["e_conv_w"] = dcw[None]
    gs["e_gate_a_w"] = _gate_blocks(dga)[None]
    gs["e_gate_x_w"] = _gate_blocks(dgx)[None]
    dp = jnp.concatenate([dpa, dbx, dbg], axis=1)
    dxn = _matmul(dp, w["e_w_in_t"], "nn", "mm_e_dxn")
    d_ein = _matmul(dp, xn0, "tn", "mm_e_din", out_dtype=BF16)
    grad_x, gs["e_norm_g"] = _rms_bwd(x, w["e_norm_g"], dxn, g, "rms_e_bwd")
    grad_x = send_grads("c", [("e_w_in", 0, d_ein)], grad_x)
    return loss, grad_x, gs


CAST_ROWS = 256


def _cast_shard(w3, layer, transpose, chip, name):
    _, rows, cols = w3.shape
    tr = _tile(rows, (CAST_ROWS, 176, 128))

    def body(c_ref, w_ref, o_ref):
        v = w_ref[...]
        o_ref[...] = (v.T if transpose else v).astype(BF16)

    in_spec = pl.BlockSpec((None, tr, cols), lambda i, c: (layer, i, 0))
    if transpose:
        out_spec, shape = pl.BlockSpec((None, cols, tr), lambda i, c: (c[0], 0, i)), (cols, rows)
    else:
        out_spec, shape = pl.BlockSpec((None, tr, cols), lambda i, c: (c[0], i, 0)), (rows, cols)
    grid_spec = pltpu.PrefetchScalarGridSpec(num_scalar_prefetch=1, grid=(rows // tr,), in_specs=[in_spec],
                                             out_specs=out_spec)
    return pl.pallas_call(body, name=name, grid_spec=grid_spec,
                          out_shape=jax.ShapeDtypeStruct((N_CHIPS,) + shape, BF16),
                          compiler_params=_cparams(("parallel",), VMEM_MID))(chip, w3)


_ANY = pl.BlockSpec(memory_space=pl.ANY)


def _coords():
    return lax.axis_index("x"), lax.axis_index("y"), lax.axis_index("c")


def _flip(v, d):
    return 1 - v if d else v


_CHIP_RELS = ((1, 0), (0, 1), (1, 1))
_DEV_RELS = tuple((dx, dy, dc) for dx in (0, 1) for dy in (0, 1) for dc in (0, 1))[1:]


_HBM = pl.BlockSpec(memory_space=pltpu.HBM)
_SEM = pl.BlockSpec(memory_space=pltpu.SEMAPHORE)
_EFFECT = pltpu.SideEffectType.DATAFLOW_SIDE_EFFECTING


def _in_hbm(a):
    return pltpu.with_memory_space_constraint(a, pltpu.HBM)


def _gather_copies(bufs, send, recv, landed):
    x, y, c = _coords()
    me = 2 * x + y
    res = []
    for i, buf in enumerate(bufs):
        for j, (dx, dy) in enumerate(_CHIP_RELS):
            px, py = _flip(x, dx), _flip(y, dy)
            k = i * len(_CHIP_RELS) + j
            res.append(pltpu.make_async_remote_copy(
                src_ref=buf.at[me], dst_ref=buf.at[2 * px + py if landed else me], send_sem=send.at[k],
                recv_sem=recv.at[k], device_id=(px, py, c), device_id_type=MESH))
    return res


def _scatter_copies(srcs, lands, send, recv, landed):
    x, y, c = _coords()
    me = 4 * x + 2 * y + c
    res = []
    for i, (src, land) in enumerate(zip(srcs, lands)):
        for j, (dx, dy, dc) in enumerate(_DEV_RELS):
            peer = (_flip(x, dx), _flip(y, dy), _flip(c, dc))
            pid = 4 * peer[0] + 2 * peer[1] + peer[2]
            k = i * len(_DEV_RELS) + j
            res.append(pltpu.make_async_remote_copy(
                src_ref=src.at[pid], dst_ref=land.at[pid if landed else me], send_sem=send.at[k],
                recv_sem=recv.at[k], device_id=peer, device_id_type=MESH))
    return res


def _split_start(bufs, n_src, copies, n_rel, name, after):
    n = len(bufs)
    nk = n_src * n_rel

    def body(*refs):
        ins, send, recv, token = refs[:n], refs[n + 1 + n], refs[n + 2 + n], refs[-1]
        for cp in copies(ins, send, recv, False):
            cp.start()
        token[...] = jnp.zeros_like(token)

    res = pl.pallas_call(
        body, name=name, in_specs=[_HBM] * n + [_ANY],
        out_specs=[_HBM] * n + [_SEM, _SEM, pl.BlockSpec(memory_space=pltpu.VMEM)],
        out_shape=[pltpu.HBM(b.shape, b.dtype) for b in bufs]
        + [pltpu.SemaphoreType.DMA((nk,)), pltpu.SemaphoreType.DMA((nk,)), jax.ShapeDtypeStruct((8, LANES), F32)],
        input_output_aliases={i: i for i in range(n)},
        compiler_params=pltpu.CompilerParams(has_side_effects=_EFFECT))(*[_in_hbm(b) for b in bufs], after)
    return res[n], res[n + 1], list(res[:n]), res[n + 2]


def _split_wait(bufs, send, recv, copies, name, after):
    n = len(bufs)

    def body(*refs):
        ins, send_ref, recv_ref = refs[:n], refs[n], refs[n + 1]
        for cp in copies(ins, send_ref, recv_ref, True):
            cp.wait_send()
            cp.wait_recv()

    return pl.pallas_call(
        body, name=name, in_specs=[_HBM] * n + [_SEM, _SEM, _ANY], out_specs=[_HBM] * n,
        out_shape=[pltpu.HBM(b.shape, b.dtype) for b in bufs], input_output_aliases={i: i for i in range(n)},
        compiler_params=pltpu.CompilerParams(has_side_effects=_EFFECT))(*bufs, send, recv, after)


def _gather_start(bufs, name, after):
    return _split_start(bufs, len(bufs), _gather_copies, len(_CHIP_RELS), name, after)


def _gather_wait(bufs, send, recv, name, after):
    return _split_wait(bufs, send, recv, _gather_copies, name, after)


def _scatter_start(srcs, name, after):
    n = len(srcs)
    lands = [lax.empty(a.shape, a.dtype) for a in srcs]
    fn = lambda refs, send, recv, landed: _scatter_copies(refs[:n], refs[n:], send, recv, landed)
    send, recv, bufs, token = _split_start(list(srcs) + lands, n, fn, len(_DEV_RELS), name, after)
    return send, recv, bufs, token


def _scatter_wait(bufs, send, recv, name, after):
    n = len(bufs) // 2
    fn = lambda refs, s, r, landed: _scatter_copies(refs[:n], refs[n:], s, r, landed)
    res = _split_wait(bufs, send, recv, fn, name, after)
    return res[:n], res[n:]


def _sum_segments(src, land, me, name):
    nd, seg, cols = src.shape
    ts = _tile(seg, (256, 176, 128))

    def body(m_ref, *refs):
        o_ref = refs[-1]
        acc = refs[0][...].astype(F32)
        for r in refs[1:-1]:
            acc = acc + r[...].astype(F32)
        o_ref[...] = acc

    def peer(rel):
        bits = 4 * rel[0] + 2 * rel[1] + rel[2]
        return pl.BlockSpec((None, ts, cols), lambda i, m: (jnp.bitwise_xor(m[0], bits), i, 0))

    grid_spec = pltpu.PrefetchScalarGridSpec(
        num_scalar_prefetch=1, grid=(seg // ts,),
        in_specs=[pl.BlockSpec((None, ts, cols), lambda i, m: (m[0], i, 0))] + [peer(r) for r in _DEV_RELS],
        out_specs=pl.BlockSpec((None, ts, cols), lambda i, m: (m[1], i, 0)))
    return pl.pallas_call(body, name=name, grid_spec=grid_spec,
                          out_shape=jax.ShapeDtypeStruct((2, seg, cols), F32),
                          compiler_params=_cparams(("parallel",), VMEM_MID))(me, src, *[land] * len(_DEV_RELS))


def _exchange_sibling(arrs):
    n = len(arrs)

    def body(*refs):
        outs, (send, recv) = refs[n:2 * n], refs[2 * n:]
        x, y, c = _coords()
        sib = (x, y, 1 - c)
        sends, recvs = [], []
        for i in range(n):
            cp = pltpu.make_async_remote_copy(src_ref=outs[i].at[c], dst_ref=outs[i].at[c], send_sem=send.at[i],
                                              recv_sem=recv.at[i], device_id=sib, device_id_type=MESH)
            cp.start()
            sends.append(cp)
            recvs.append(pltpu.make_async_remote_copy(src_ref=outs[i].at[c], dst_ref=outs[i].at[1 - c],
                                                      send_sem=send.at[i], recv_sem=recv.at[i], device_id=sib,
                                                      device_id_type=MESH))
        for cp in recvs:
            cp.wait_recv()
        for cp in sends:
            cp.wait_send()

    return pl.pallas_call(
        body, name="exchange_sibling", in_specs=[_ANY] * n, out_specs=[_ANY] * n,
        out_shape=[jax.ShapeDtypeStruct(a.shape, a.dtype) for a in arrs],
        input_output_aliases={i: i for i in range(n)},
        scratch_shapes=[pltpu.SemaphoreType.DMA((n,)), pltpu.SemaphoreType.DMA((n,))])(*arrs)


def _allreduce_small(vec):
    nd, rows, lanes = vec.shape
    nr = len(_DEV_RELS)

    def body(in_ref, out_ref, stage, red, send, recv):
        x, y, c = _coords()
        me = 4 * x + 2 * y + c
        peers = []
        for dx, dy, dc in _DEV_RELS:
            peer = (_flip(x, dx), _flip(y, dy), _flip(c, dc))
            peers.append((peer, 4 * peer[0] + 2 * peer[1] + peer[2]))

        def copy(src, dst, k, peer):
            return pltpu.make_async_remote_copy(src_ref=src, dst_ref=dst, send_sem=send.at[k], recv_sem=recv.at[k],
                                                device_id=peer, device_id_type=MESH)

        first = [copy(in_ref.at[pid], stage.at[me], j, peer) for j, (peer, pid) in enumerate(peers)]
        for cp in first:
            cp.start()
        stage[me] = in_ref[me]
        for j, (peer, pid) in enumerate(peers):
            copy(in_ref.at[pid], stage.at[pid], j, peer).wait_recv()
        acc = stage[0]
        for d in range(1, nd):
            acc = acc + stage[d]
        red[...] = acc
        out_ref[me] = acc
        second = [copy(red, out_ref.at[me], nr + j, peer) for j, (peer, pid) in enumerate(peers)]
        for cp in second:
            cp.start()
        for j, (peer, pid) in enumerate(peers):
            copy(red, out_ref.at[pid], nr + j, peer).wait_recv()
        for cp in first + second:
            cp.wait_send()

    vm = pl.BlockSpec(memory_space=pltpu.VMEM)
    return pl.pallas_call(
        body, name="allreduce_small", in_specs=[vm], out_specs=vm,
        out_shape=jax.ShapeDtypeStruct(vec.shape, F32),
        scratch_shapes=[pltpu.VMEM(vec.shape, F32), pltpu.VMEM((rows, lanes), F32),
                        pltpu.SemaphoreType.DMA((2 * nr,)), pltpu.SemaphoreType.DMA((2 * nr,))],
        compiler_params=_cparams(None, VMEM_MID))(vec)


def _adam_math(w, g, m, v):
    m2 = ADAM_B1 * m + (1.0 - ADAM_B1) * g
    v2 = ADAM_B2 * v + (1.0 - ADAM_B2) * (g * g)
    m_hat = m2 / (1.0 - ADAM_B1 ** ADAM_STEP)
    v_hat = v2 / (1.0 - ADAM_B2 ** ADAM_STEP)
    return -ADAM_LR * (m_hat / (jnp.sqrt(v_hat) + ADAM_EPS) + ADAM_WD * w), m2, v2


def _adamw_big(w3, m3, v3, layer, g, transposed, name, prev=None):
    nl, rows, cols = w3.shape
    tr = 128 if transposed else _tile(rows, (256, 176, 128))

    def body(w_ref, m_ref, v_ref, g_ref, *rest):
        go_ref, d_ref, mo_ref, vo_ref = rest[-4:]
        g_val = g_ref[...].T if transposed else g_ref[...]
        go_ref[...] = g_val
        d_ref[...], mo_ref[...], vo_ref[...] = _adam_math(w_ref[...], g_val, m_ref[...], v_ref[...])

    wspec = pl.BlockSpec((None, tr, cols), lambda i: (layer, i, 0))
    gspec = pl.BlockSpec((cols, tr), lambda i: (0, i)) if transposed else pl.BlockSpec((tr, cols), lambda i: (i, 0))
    extra = [] if prev is None else list(prev)
    return pl.pallas_call(body, name=name, grid=(rows // tr,),
                          in_specs=[wspec, wspec, wspec, gspec] + [_ANY] * len(extra),
                          out_specs=[wspec] * 4, out_shape=[jax.ShapeDtypeStruct((nl, rows, cols), F32)] * 4,
                          input_output_aliases={4 + i: i for i in range(len(extra))},
                          compiler_params=_cparams(("parallel",), VMEM_MID))(w3, m3, v3, g, *extra)


def _adamw_small(w, g, m, v):
    rows = w.shape[0]
    tr = _tile(rows, (512, 256))

    def body(w_ref, g_ref, m_ref, v_ref, d_ref, mo_ref, vo_ref):
        d_ref[...], mo_ref[...], vo_ref[...] = _adam_math(w_ref[...], g_ref[...], m_ref[...], v_ref[...])

    spec = pl.BlockSpec((tr, LANES), lambda i: (i, 0))
    return pl.pallas_call(body, name="adamw_small", grid=(rows // tr,), in_specs=[spec] * 4, out_specs=[spec] * 3,
                          out_shape=[jax.ShapeDtypeStruct(w.shape, F32)] * 3,
                          compiler_params=_cparams(("parallel",)))(w, g, m, v)


PACK_ROWS = 8


def _packed_rows(shape):
    size = 1
    for d in shape:
        size *= d
    return -(-size // (PACK_ROWS * LANES)) * PACK_ROWS


def _pack(arrs, row_mult):
    parts = []
    for a in arrs:
        flat = a.reshape(-1).astype(F32)
        rows = _packed_rows(a.shape)
        parts.append(jnp.pad(flat, (0, rows * LANES - flat.shape[0])).reshape(rows, LANES))
    total = sum(p.shape[0] for p in parts)
    fill = -(-total // row_mult) * row_mult - total
    if fill:
        parts.append(jnp.zeros((fill, LANES), F32))
    return jnp.concatenate(parts, axis=0)


def _unpack(packed, shapes):
    out, off = [], 0
    for s in shapes:
        rows = _packed_rows(s)
        size = 1
        for d in s:
            size *= d
        out.append(packed[off:off + rows].reshape(-1)[:size].reshape(s))
        off += rows
    return out


_SMALL_REP = ("e_norm_g", "e_mu", "e_w0", "e_a0", "e_k_k", "e_k_a", "e_r_k", "e_ln_w", "e_ln_b", "e_conv_b",
              "e_gate_a_w", "e_gate_a_b", "e_gate_x_w", "e_gate_x_b", "e_lru_lambda", "o_A_re", "o_A_im", "o_log_dt",
              "o_B_re", "o_B_im", "o_C_re", "o_C_im", "f_norm_g", "f_conv_b", "final_norm_g")
_SMALL_SH = ("e_w2", "e_a2", "e_g2", "e_conv_w", "o_norm_g", "o_D", "f_conv_w")
_LARGE = (("e_w_in", True), ("e_w_out", False), ("o_w_in", False), ("o_w_glu", True), ("f_w_up", True),
        ("f_w_down", False))
_ORDER = ("e_norm_g", "e_w_in", "e_mu", "e_w0", "e_w2", "e_a0", "e_a2", "e_g2", "e_k_k", "e_k_a", "e_r_k", "e_ln_w",
          "e_ln_b", "e_conv_w", "e_conv_b", "e_gate_a_w", "e_gate_a_b", "e_gate_x_w", "e_gate_x_b", "e_lru_lambda",
          "e_w_out", "o_norm_g", "o_w_in", "o_A_re", "o_A_im", "o_log_dt", "o_B_re", "o_B_im", "o_C_re", "o_C_im",
          "o_D", "o_w_glu", "f_norm_g", "f_w_up", "f_conv_w", "f_conv_b", "f_w_down", "final_norm_g")
N_CHIPS = 4
N_DEV = 8


def _step(x, tgt, wts, ms, vs):
    xi, yi, ci = _coords()
    chip = 2 * xi + yi
    chip1 = chip.astype(jnp.int32).reshape(1)
    me2 = jnp.stack([4 * xi + 2 * yi + ci, ci]).astype(jnp.int32)
    by_cols = dict(_LARGE)

    bufs = {(name, l): _cast_shard(wts[name], l, by_cols[name], chip1, f"cast_{name}{l}")
            for name, _ in _LARGE for l in range(wts[name].shape[0])}
    sh_shapes = [wts[n].shape for n in _SMALL_SH]
    packed = _pack([wts[n] for n in _SMALL_SH], 8)
    small_buf = lax.dynamic_update_slice(jnp.zeros((N_CHIPS,) + packed.shape, F32), packed[None], (chip, 0, 0))
    early = [("e_w_in", 0), ("e_w_out", 0)]
    late = [k for k in bufs if k not in early]
    send, recv, thru, token = _gather_start([bufs[k] for k in early] + [small_buf], "gather_start_a", x)
    got = _gather_wait(thru, send, recv, "gather_wait_a", token)
    send_b, recv_b, thru_b, token = _gather_start([bufs[k] for k in late], "gather_start_b", got[0])
    x, _ = lax.optimization_barrier((x, token))

    def rows(g):
        return g.reshape(N_CHIPS * g.shape[1], g.shape[2])

    full = {n: wts[n] for n in _SMALL_REP}
    full["e_w_in_t"], full["e_w_out"] = rows(got[0]), rows(got[1])
    per_chip = [_unpack(got[2][k], sh_shapes) for k in range(N_CHIPS)]
    for i, n in enumerate(_SMALL_SH):
        full[n] = jnp.concatenate([per_chip[k][i] for k in range(N_CHIPS)], axis=-1)

    def late_weights(after):
        res = dict(zip(late, _gather_wait(thru_b, send_b, recv_b, "gather_wait_b", after)))
        return {"o_w_in": rows(res[("o_w_in", 0)]), "o_w_glu_t": rows(res[("o_w_glu", 0)]),
                "f_w_up_t": [rows(res[("f_w_up", l)]) for l in range(2)],
                "f_w_down": [rows(res[("f_w_down", l)]) for l in range(2)]}

    pending = []

    def send_grads(tag, items, carry):
        srcs = [g.reshape(N_DEV, g.shape[0] // N_DEV, g.shape[1]) for _, _, g in items]
        s_sem, r_sem, both, tok = _scatter_start(srcs, f"scatter_start_{tag}", carry)
        pending.append((tag, [(name, l) for name, l, _ in items], s_sem, r_sem, both))
        carry, _ = lax.optimization_barrier((carry, tok))
        return carry

    loss, grad_x, gs = _local_step(x, tgt, full, late_weights, send_grads)

    final = {}
    small = _SMALL_REP + _SMALL_SH
    shapes = [gs[n].shape for n in small]
    red = _allreduce_small(_pack([gs[n] for n in small], 8 * N_DEV).reshape(N_DEV, -1, LANES))
    tot = dict(zip(small, _unpack(red.reshape(-1, LANES), shapes)))
    for n in _SMALL_SH:
        width = wts[n].shape[-1]
        tot[n] = lax.dynamic_slice_in_dim(tot[n], chip * width, width, axis=tot[n].ndim - 1)
    loc_shapes = [wts[n].shape for n in small]
    pk = lambda d: _pack([d[n] for n in small], 256)
    delta, new_m, new_v = _adamw_small(pk(wts), pk(tot), pk(ms), pk(vs))
    for n, g, d, m2, v2 in zip(small, [tot[n] for n in small], _unpack(delta, loc_shapes), _unpack(new_m, loc_shapes),
                               _unpack(new_v, loc_shapes)):
        final[n] = [g.reshape(wts[n].shape), d, m2, v2]

    halves, keys = [], []
    for tag, names, s_sem, r_sem, both in pending:
        srcs, lands = _scatter_wait(both, s_sem, r_sem, f"scatter_wait_{tag}", new_v)
        for (name, l), src, land in zip(names, srcs, lands):
            halves.append(_sum_segments(src, land, me2, f"sum_{name}{l}"))
            keys.append((name, l))
    shards = _exchange_sibling(halves)
    for s, (name, l) in zip(shards, keys):
        final[name] = _adamw_big(wts[name], ms[name], vs[name], l, s.reshape(2 * s.shape[1], s.shape[2]),
                                 by_cols[name], f"adamw_{name}{l}", prev=final.get(name))

    loss = lax.psum(loss[0, 0], ("x", "y", "c"))
    res = [loss, grad_x[None]]
    for k in range(4):
        res += [final[n][k] for n in _ORDER]
    return tuple(res)


def kernel(x, e_norm_g, e_w_in, e_mu, e_w0, e_w2, e_a0, e_a2, e_g2, e_k_k, e_k_a, e_r_k, e_ln_w, e_ln_b, e_conv_w, e_conv_b, e_gate_a_w, e_gate_a_b, e_gate_x_w, e_gate_x_b, e_lru_lambda, e_w_out, o_norm_g, o_w_in, o_A_re, o_A_im, o_log_dt, o_B_re, o_B_im, o_C_re, o_C_im, o_D, o_w_glu, f_norm_g, f_w_up, f_conv_w, f_conv_b, f_w_down, final_norm_g, loss_target, m_e_norm_g, m_e_w_in, m_e_mu, m_e_w0, m_e_w2, m_e_a0, m_e_a2, m_e_g2, m_e_k_k, m_e_k_a, m_e_r_k, m_e_ln_w, m_e_ln_b, m_e_conv_w, m_e_conv_b, m_e_gate_a_w, m_e_gate_a_b, m_e_gate_x_w, m_e_gate_x_b, m_e_lru_lambda, m_e_w_out, m_o_norm_g, m_o_w_in, m_o_A_re, m_o_A_im, m_o_log_dt, m_o_B_re, m_o_B_im, m_o_C_re, m_o_C_im, m_o_D, m_o_w_glu, m_f_norm_g, m_f_w_up, m_f_conv_w, m_f_conv_b, m_f_w_down, m_final_norm_g, v_e_norm_g, v_e_w_in, v_e_mu, v_e_w0, v_e_w2, v_e_a0, v_e_a2, v_e_g2, v_e_k_k, v_e_k_a, v_e_r_k, v_e_ln_w, v_e_ln_b, v_e_conv_w, v_e_conv_b, v_e_gate_a_w, v_e_gate_a_b, v_e_gate_x_w, v_e_gate_x_b, v_e_lru_lambda, v_e_w_out, v_o_norm_g, v_o_w_in, v_o_A_re, v_o_A_im, v_o_log_dt, v_o_B_re, v_o_B_im, v_o_C_re, v_o_C_im, v_o_D, v_o_w_glu, v_f_norm_g, v_f_w_up, v_f_conv_w, v_f_conv_b, v_f_w_down, v_final_norm_g):
    args = locals()
    wts = {n: args[n] for n in _ORDER}
    ms = {n: args["m_" + n] for n in _ORDER}
    vs = {n: args["v_" + n] for n in _ORDER}
    return _step(x[0], loss_target[0], wts, ms, vs)
```

```python
import functools

import jax
import jax.numpy as jnp
from jax import lax
from jax.experimental import pallas as pl
from jax.experimental.pallas import tpu as pltpu

F32 = jnp.float32
BF16 = jnp.bfloat16
MESH = pl.DeviceIdType.MESH

HEAD = 64
RW = 512
N_HEADS = RW // HEAD
LRU_W = 512
SHIFT_COLS = 1792
W_LORA, A_LORA, G_LORA = 64, 64, 128
S5_GROUPS, S5_GROUP, S5_STATE = 64, 16, 64
D_FF = 2816
NORM_EPS = 1e-6
GN_EPS = 64e-5
LRU_C = 8.0
ADAM_LR, ADAM_B1, ADAM_B2, ADAM_EPS, ADAM_WD, ADAM_STEP = 0.001, 0.9, 0.999, 1e-08, 0.01, 10

VMEM_BIG = 56 * 1024 * 1024
VMEM_MID = 40 * 1024 * 1024
LANES = 128
PT = 16
WKV_CHUNK = 32
S5_SLAB = 128


def _cparams(sem=None, vmem=None):
    kw = {}
    if sem is not None:
        kw["dimension_semantics"] = sem
    if vmem is not None:
        kw["vmem_limit_bytes"] = vmem
    return pltpu.CompilerParams(**kw)


def _tile(dim, cands):
    for c in cands:
        if dim % c == 0:
            return c
    return dim


def _full(shape):
    n = len(shape)
    return pl.BlockSpec(shape, lambda *_: (0,) * n)


_TILES = (2816, 2048, 1408, 1024, 512, 256, 128)
MM_BUDGET = 36 * 1024 * 1024
VMEM_SLACK = 12 * 1024 * 1024


def _mm_tiles(m, n, k, size_a, size_b, size_o, has_add):
    best = None
    for tm in _TILES:
        for tk in _TILES:
            for tn in _TILES:
                if m % tm or n % tn or k % tk:
                    continue
                need = 2 * (tm * tk * size_a + tk * tn * size_b + tm * tn * size_o) + tm * tn * 4 * (1 + 2 * has_add)
                if k > tk:
                    need += tm * tn * 4
                if need <= MM_BUDGET:
                    cand = (tm, tk, tn)
                    if best is None or cand > best[0]:
                        best = (cand, need)
    (tm, tk, tn), need = best
    return tm, tn, tk, need


def _matmul(a, b, mode, name, out_dtype=F32, add=None):
    if mode == "nn":
        (m, k), (k2, n) = a.shape, b.shape
    elif mode == "nt":
        (m, k), (n, k2) = a.shape, b.shape
    else:
        (k, m), (k2, n) = a.shape, b.shape
    assert k == k2, (a.shape, b.shape, mode)
    tm, tn, tk, need = _mm_tiles(m, n, k, a.dtype.itemsize, b.dtype.itemsize, jnp.dtype(out_dtype).itemsize,
                                 add is not None)
    nk = k // tk
    dims = {"nn": (((1,), (0,)), ((), ())), "nt": (((1,), (1,)), ((), ())), "tn": (((0,), (0,)), ((), ()))}[mode]

    def body(*refs):
        a_ref, b_ref = refs[:2]
        add_ref = refs[2] if add is not None else None
        o_ref = refs[3] if add is not None else refs[2]
        part = lax.dot_general(a_ref[...].astype(BF16), b_ref[...].astype(BF16), dims, preferred_element_type=F32)

        def finish(r):
            if add_ref is not None:
                r = r + add_ref[...]
            o_ref[...] = r.astype(o_ref.dtype)

        if nk == 1:
            finish(part)
            return
        acc = refs[-1]
        kk = pl.program_id(2)

        @pl.when(kk == 0)
        def _():
            acc[...] = part

        @pl.when(kk > 0)
        def _():
            acc[...] += part

        @pl.when(kk == nk - 1)
        def _():
            finish(acc[...])

    if mode == "nn":
        a_spec = pl.BlockSpec((tm, tk), lambda i, j, kk: (i, kk))
        b_spec = pl.BlockSpec((tk, tn), lambda i, j, kk: (kk, j))
    elif mode == "nt":
        a_spec = pl.BlockSpec((tm, tk), lambda i, j, kk: (i, kk))
        b_spec = pl.BlockSpec((tn, tk), lambda i, j, kk: (j, kk))
    else:
        a_spec = pl.BlockSpec((tk, tm), lambda i, j, kk: (kk, i))
        b_spec = pl.BlockSpec((tk, tn), lambda i, j, kk: (kk, j))
    o_spec = pl.BlockSpec((tm, tn), lambda i, j, kk: (i, j))
    in_specs = [a_spec, b_spec] + ([o_spec] if add is not None else [])
    args = (a, b) + ((add,) if add is not None else ())
    return pl.pallas_call(
        body, name=name, grid=(m // tm, n // tn, nk),
        in_specs=in_specs, out_specs=o_spec,
        out_shape=jax.ShapeDtypeStruct((m, n), out_dtype),
        scratch_shapes=[pltpu.VMEM((tm, tn), F32)] if nk > 1 else [],
        compiler_params=_cparams(("parallel", "parallel", "arbitrary"), min(VMEM_BIG, need + VMEM_SLACK)),
    )(*args)


TOK = 256


def _rms(x, g):
    return x * lax.rsqrt(jnp.mean(x * x, axis=-1, keepdims=True) + NORM_EPS) * g


def _rms_fwd(x, g, name):
    t, d = x.shape

    def body(x_ref, g_ref, o_ref):
        o_ref[...] = _rms(x_ref[...], g_ref[...]).astype(BF16)

    row = pl.BlockSpec((TOK, d), lambda i: (i, 0))
    return pl.pallas_call(body, name=name, grid=(t // TOK,), in_specs=[row, _full((1, d))], out_specs=row,
                          out_shape=jax.ShapeDtypeStruct((t, d), BF16),
                          compiler_params=_cparams(("parallel",)))(x, g)


def _rms_bwd(x, g, dxn, res, name):
    t, d = x.shape

    def body(x_ref, g_ref, d_ref, res_ref, dx_ref, dg_ref):
        _, vjp = jax.vjp(_rms, x_ref[...], g_ref[...])
        dx, dg = vjp(d_ref[...].astype(F32))
        dx_ref[...] = dx + res_ref[...]

        @pl.when(pl.program_id(0) == 0)
        def _():
            dg_ref[...] = jnp.zeros_like(dg_ref)

        dg_ref[...] += dg

    row = pl.BlockSpec((TOK, d), lambda i: (i, 0))
    return pl.pallas_call(body, name=name, grid=(t // TOK,), in_specs=[row, _full((1, d)), row, row],
                          out_specs=[row, _full((1, d))],
                          out_shape=[jax.ShapeDtypeStruct((t, d), F32), jax.ShapeDtypeStruct((1, d), F32)],
                          compiler_params=_cparams(("arbitrary",)))(x, g, dxn, res)


def _loss_head(x, g, tgt):
    t, d = x.shape

    def body(x_ref, g_ref, t_ref, l_ref, dx_ref, dg_ref):
        tg = t_ref[...]

        def fn(xv, gv):
            err = _rms(xv, gv) - tg
            per_tok = jnp.mean(err * err, axis=-1, keepdims=True)
            return 0.5 * jnp.sum(per_tok, axis=0, keepdims=True)

        l, vjp = jax.vjp(fn, x_ref[...], g_ref[...])
        dx, dg = vjp(jnp.ones((1, 1), F32))
        dx_ref[...] = dx

        @pl.when(pl.program_id(0) == 0)
        def _():
            dg_ref[...] = jnp.zeros_like(dg_ref)
            l_ref[...] = jnp.zeros_like(l_ref)

        dg_ref[...] += dg
        l_ref[...] += jnp.broadcast_to(l, l_ref.shape)

    row = pl.BlockSpec((TOK, d), lambda i: (i, 0))
    return pl.pallas_call(body, name="loss_head", grid=(t // TOK,), in_specs=[row, _full((1, d)), row],
                          out_specs=[_full((1, LANES)), row, _full((1, d))],
                          out_shape=[jax.ShapeDtypeStruct((1, LANES), F32), jax.ShapeDtypeStruct((t, d), F32),
                                     jax.ShapeDtypeStruct((1, d), F32)],
                          compiler_params=_cparams(("arbitrary",)))(x, g, tgt)


def _glu_fwd(x, z):
    t, d = x.shape

    def body(x_ref, v_ref, g_ref, o_ref):
        o_ref[...] = x_ref[...] + v_ref[...] * jax.nn.sigmoid(g_ref[...])

    row = pl.BlockSpec((TOK, d), lambda i: (i, 0))
    gate = pl.BlockSpec((TOK, d), lambda i: (i, 1))
    return pl.pallas_call(body, name="glu_fwd", grid=(t // TOK,), in_specs=[row, row, gate], out_specs=row,
                          out_shape=jax.ShapeDtypeStruct((t, d), F32),
                          compiler_params=_cparams(("parallel",)))(x, z, z)


def _glu_bwd(z, g):
    t, d = g.shape

    def body(v_ref, g_ref, d_ref, o_ref):
        s = jax.nn.sigmoid(g_ref[...])
        dy = d_ref[...]
        o_ref[:, :d] = (dy * s).astype(BF16)
        o_ref[:, d:] = (dy * v_ref[...] * s * (1.0 - s)).astype(BF16)

    row = pl.BlockSpec((TOK, d), lambda i: (i, 0))
    gate = pl.BlockSpec((TOK, d), lambda i: (i, 1))
    return pl.pallas_call(body, name="glu_bwd", grid=(t // TOK,), in_specs=[row, gate, row],
                          out_specs=pl.BlockSpec((TOK, 2 * d), lambda i: (i, 0)),
                          out_shape=jax.ShapeDtypeStruct((t, 2 * d), BF16),
                          compiler_params=_cparams(("parallel",)))(z, z, g)


def _shift_down(x, d):
    row = lax.broadcasted_iota(jnp.int32, x.shape, 0)
    return jnp.where(row < d, 0.0, pltpu.roll(x, d, 0))


def _shift_up(x, d):
    n = x.shape[0]
    row = lax.broadcasted_iota(jnp.int32, x.shape, 0)
    return jnp.where(row >= n - d, 0.0, pltpu.roll(x, n - d, 0))


def _make_sd():
    @functools.partial(jax.custom_vjp, nondiff_argnums=(1,))
    def sd(x, d):
        return _shift_down(x, d)

    def fwd(x, d):
        return _shift_down(x, d), None

    def bwd(d, _, g):
        return (_shift_up(g, d),)

    sd.defvjp(fwd, bwd)
    return sd


def _lin_scan(a, u, reverse=False):
    n = a.shape[0]
    row = lax.broadcasted_iota(jnp.int32, a.shape, 0)
    d = 1
    while d < n:
        if reverse:
            keep = row < n - d
            a_s, u_s = pltpu.roll(a, n - d, 0), pltpu.roll(u, n - d, 0)
        else:
            keep = row >= d
            a_s, u_s = pltpu.roll(a, d, 0), pltpu.roll(u, d, 0)
        u = u + a * jnp.where(keep, u_s, 0.0)
        a = a * jnp.where(keep, a_s, 1.0)
        d *= 2
    return u


def _make_scan():
    @jax.custom_vjp
    def scan(a, u):
        return _lin_scan(a, u)

    def fwd(a, u):
        h = _lin_scan(a, u)
        return h, (a, h)

    def bwd(res, dh):
        a, h = res
        g = _lin_scan(_shift_up(a, 1), dh, reverse=True)
        return g * _shift_down(h, 1), g

    scan.defvjp(fwd, bwd)
    return scan


def _acc_out(ref, val):
    @pl.when(pl.program_id(0) == 0)
    def _():
        ref[...] = jnp.zeros_like(ref)

    ref[...] += val


FFN_CW = 128


def _ffn_fn(hg, hv, wg, wv, bg, bv, sd):
    cg = wg[0:1] * sd(hg, 2) + wg[1:2] * sd(hg, 1) + wg[2:3] * hg + bg
    cv = wv[0:1] * sd(hv, 2) + wv[1:2] * sd(hv, 1) + wv[2:3] * hv + bv
    return jax.nn.silu(cg) * cv


def _ffn_specs(t):
    nb = D_FF // FFN_CW
    col = lambda r, off: pl.BlockSpec((r, FFN_CW), lambda j: (0, j + off))
    return nb, [col(t, 0), col(t, nb), col(3, 0), col(3, nb), col(1, 0), col(1, nb)], col


def _ffn_mid_fwd(h, cw, cb, name):
    t = h.shape[0]
    nb, in_specs, col = _ffn_specs(t)

    def body(hg, hv, wg, wv, bg, bv, o_ref):
        o_ref[...] = _ffn_fn(hg[...], hv[...], wg[...], wv[...], bg[...], bv[...], _shift_down).astype(BF16)

    return pl.pallas_call(body, name=name, grid=(nb,), in_specs=in_specs, out_specs=col(t, 0),
                          out_shape=jax.ShapeDtypeStruct((t, D_FF), BF16),
                          compiler_params=_cparams(("parallel",), VMEM_MID))(h, h, cw, cw, cb, cb)


def _ffn_mid_bwd(h, cw, cb, dact, name):
    t = h.shape[0]
    nb, in_specs, col = _ffn_specs(t)

    def body(hg, hv, wg, wv, bg, bv, d_ref, dhg, dhv, dwg, dwv, dbg, dbv):
        fn = functools.partial(_ffn_fn, sd=_make_sd())
        _, vjp = jax.vjp(fn, hg[...], hv[...], wg[...], wv[...], bg[...], bv[...])
        g = vjp(d_ref[...])
        dhg[...] = g[0].astype(BF16)
        dhv[...] = g[1].astype(BF16)
        dwg[...], dwv[...], dbg[...], dbv[...] = g[2], g[3], g[4], g[5]

    big = jax.ShapeDtypeStruct((t, D_FF), BF16)
    w3 = jax.ShapeDtypeStruct((3, D_FF), F32)
    b1 = jax.ShapeDtypeStruct((1, D_FF), F32)
    return pl.pallas_call(body, name=name, grid=(nb,), in_specs=in_specs + [col(t, 0)],
                          out_specs=[col(t, 0), col(t, 0), col(3, 0), col(3, 0), col(1, 0), col(1, 0)],
                          out_shape=[big, big, w3, w3, b1, b1],
                          compiler_params=_cparams(("parallel",), VMEM_BIG))(h, h, cw, cw, cb, cb, dact)


TS_CW = 256


def _tshift_fn(p, mu, sd):
    return p + mu * (sd(p, 1) - p)


def _tshift_fwd(p, mu):
    t = p.shape[0]
    col = lambda r: pl.BlockSpec((r, TS_CW), lambda j: (0, j))

    def body(p_ref, mu_ref, o_ref):
        o_ref[...] = _tshift_fn(p_ref[...], mu_ref[...], _shift_down)

    return pl.pallas_call(body, name="tshift_fwd", grid=(SHIFT_COLS // TS_CW,), in_specs=[col(t), col(1)],
                          out_specs=col(t), out_shape=jax.ShapeDtypeStruct((t, SHIFT_COLS), F32),
                          compiler_params=_cparams(("parallel",), VMEM_MID))(p, mu)


def _tshift_bwd(p, mu, dpam):
    t = p.shape[0]
    col = lambda r: pl.BlockSpec((r, TS_CW), lambda j: (0, j))

    def body(p_ref, mu_ref, d_ref, dp_ref, dmu_ref):
        _, vjp = jax.vjp(functools.partial(_tshift_fn, sd=_make_sd()), p_ref[...], mu_ref[...])
        dp, dmu = vjp(d_ref[...])
        dp_ref[...] = dp.astype(BF16)
        dmu_ref[...] = dmu

    return pl.pallas_call(body, name="tshift_bwd", grid=(SHIFT_COLS // TS_CW,), in_specs=[col(t), col(1), col(t)],
                          out_specs=[col(t), col(1)],
                          out_shape=[jax.ShapeDtypeStruct((t, SHIFT_COLS), BF16),
                                     jax.ShapeDtypeStruct((1, SHIFT_COLS), F32)],
                          compiler_params=_cparams(("parallel",), VMEM_MID))(p, mu, dpam)


_HI = lax.Precision.HIGHEST
_O = (0, RW, 2 * RW, 3 * RW, 3 * RW + W_LORA, 3 * RW + W_LORA + A_LORA, SHIFT_COLS)


def _dot16(a, b, dims=(((1,), (0,)), ((), ()))):
    return lax.dot_general(a.astype(BF16), b.astype(BF16), dims, preferred_element_type=F32)


def _make_dot16():
    @jax.custom_vjp
    def dot(a, b):
        return _dot16(a, b)

    def fwd(a, b):
        return _dot16(a, b), (a, b)

    def bwd(res, g):
        a, b = res
        return _dot16(g, b, (((1,), (1,)), ((), ()))), _dot16(a, g, (((0,), (0,)), ((), ())))

    dot.defvjp(fwd, bwd)
    return dot


def _seg(x, gm):
    return jnp.dot(x, gm, precision=_HI)


def _prep_fn(r, k, v, wd, ad, gd, w0, w2, a0, a2, g2, k_k, k_a, gm, dot):
    w_log = -jax.nn.softplus(-(w0 + dot(jnp.tanh(wd), w2))) - 0.5
    decay = jnp.exp(-jnp.exp(w_log))
    a = jax.nn.sigmoid(a0 + dot(ad, a2))
    g = dot(jax.nn.sigmoid(gd), g2)
    kk = k * k_k
    kk = kk / jnp.maximum(jnp.sqrt(_seg(kk * kk, gm)), 1e-12)
    k2 = k * (1.0 + (a - 1.0) * k_a)
    return r, decay, k2, v, -kk, kk * a, g


_PREP_W = ("w0", "w2", "a0", "a2", "g2", "k_k", "k_a")


def _prep_wspecs(w):
    return [_full(w[n].shape) for n in _PREP_W] + [_full((RW, RW))]


def _rwkv_prep_fwd(pam, w, gm):
    t = pam.shape[0]

    def body(p_ref, *refs):
        wr, outs = refs[:8], refs[8:]
        pieces = [p_ref[:, _O[i]:_O[i + 1]] for i in range(6)]
        res = _prep_fn(*pieces, *[x[...] for x in wr], _dot16)
        for o, val in zip(outs, res):
            o[...] = val

    row = lambda c: pl.BlockSpec((TOK, c), lambda i: (i, 0))
    return pl.pallas_call(body, name="rwkv_prep_fwd", grid=(t // TOK,),
                          in_specs=[row(SHIFT_COLS)] + _prep_wspecs(w), out_specs=[row(RW)] * 7,
                          out_shape=[jax.ShapeDtypeStruct((t, RW), F32)] * 7,
                          compiler_params=_cparams(("parallel",), VMEM_MID))(pam, *[w[n] for n in _PREP_W], gm)


def _rwkv_prep_bwd(pam, w, gm, cts, more):
    t = pam.shape[0]

    def body(p_ref, *refs):
        wr, ct, ex, dp_ref, dws = refs[:8], refs[8:15], refs[15:18], refs[18], refs[19:]
        pieces = [p_ref[:, _O[i]:_O[i + 1]] for i in range(6)]
        fn = lambda *a: _prep_fn(*a, wr[7][...], _make_dot16())
        _, vjp = jax.vjp(fn, *pieces, *[x[...] for x in wr[:7]])
        c = [x[...] for x in ct]
        c[0] = c[0] + ex[0][...]
        c[2] = c[2] + ex[1][...]
        c[3] = c[3] + ex[2][...]
        g = vjp(tuple(c))
        for i in range(6):
            dp_ref[:, _O[i]:_O[i + 1]] = g[i]
        for o, val in zip(dws, g[6:]):
            _acc_out(o, val)

    row = lambda c: pl.BlockSpec((TOK, c), lambda i: (i, 0))
    return pl.pallas_call(body, name="rwkv_prep_bwd", grid=(t // TOK,),
                          in_specs=[row(SHIFT_COLS)] + _prep_wspecs(w) + [row(RW)] * 10,
                          out_specs=[row(SHIFT_COLS)] + [_full(w[n].shape) for n in _PREP_W],
                          out_shape=[jax.ShapeDtypeStruct((t, SHIFT_COLS), F32)]
                          + [jax.ShapeDtypeStruct(w[n].shape, F32) for n in _PREP_W],
                          compiler_params=_cparams(("arbitrary",), VMEM_MID))(
                              pam, *[w[n] for n in _PREP_W], gm, *cts, *more)


def _post_fn(y, r, k2, v, g, ln_w, ln_b, r_k, gm):
    inv = 1.0 / HEAD
    d = y - _seg(y, gm) * inv
    yn = d * lax.rsqrt(_seg(d * d, gm) * inv + GN_EPS) * ln_w + ln_b
    bonus = _seg(r * k2 * r_k, gm) * v
    return (yn + bonus) * g


def _rwkv_post_fwd(y, r, k2, v, g, ln_w, ln_b, r_k, gm):
    t = y.shape[0]

    def body(*refs):
        o_ref = refs[-1]
        o_ref[...] = _post_fn(*[x[...] for x in refs[:-1]]).astype(BF16)

    row = pl.BlockSpec((TOK, RW), lambda i: (i, 0))
    return pl.pallas_call(body, name="rwkv_post_fwd", grid=(t // TOK,),
                          in_specs=[row] * 5 + [_full((1, RW))] * 3 + [_full((RW, RW))], out_specs=row,
                          out_shape=jax.ShapeDtypeStruct((t, RW), BF16),
                          compiler_params=_cparams(("parallel",), VMEM_MID))(y, r, k2, v, g, ln_w, ln_b, r_k, gm)


def _rwkv_post_bwd(y, r, k2, v, g, ln_w, ln_b, r_k, gm, dya):
    t = y.shape[0]

    def body(*refs):
        ins, gm_ref, d_ref, outs = refs[:8], refs[8], refs[9], refs[10:]
        fn = lambda *a: _post_fn(*a, gm_ref[...])
        _, vjp = jax.vjp(fn, *[x[...] for x in ins])
        gr = vjp(d_ref[...])
        for o, val in zip(outs[:5], gr[:5]):
            o[...] = val
        for o, val in zip(outs[5:], gr[5:]):
            _acc_out(o, val)

    row = pl.BlockSpec((TOK, RW), lambda i: (i, 0))
    vec = _full((1, RW))
    return pl.pallas_call(body, name="rwkv_post_bwd", grid=(t // TOK,),
                          in_specs=[row] * 5 + [vec] * 3 + [_full((RW, RW)), row],
                          out_specs=[row] * 5 + [vec] * 3,
                          out_shape=[jax.ShapeDtypeStruct((t, RW), F32)] * 5 + [jax.ShapeDtypeStruct((1, RW), F32)] * 3,
                          compiler_params=_cparams(("arbitrary",), VMEM_MID))(y, r, k2, v, g, ln_w, ln_b, r_k, gm, dya)


def _from_pt(x):
    n = x.shape[0]
    return x.reshape(n, HEAD, N_HEADS, PT).transpose(0, 3, 2, 1).reshape(n * PT, N_HEADS * HEAD)


def _lane_sum(x):
    return jnp.sum(x, axis=-1, keepdims=True)


def _pair_consts():
    lane = lax.broadcasted_iota(jnp.int32, (HEAD, LANES), 1)
    return lane, lane < HEAD


def _seg_sum_pair(x, first):
    return jnp.where(first, _lane_sum(jnp.where(first, x, 0.0)), _lane_sum(jnp.where(first, 0.0, x)))


def _to_pt(x):
    t = x.shape[0]
    return x.reshape(t // PT, PT, N_HEADS, HEAD).transpose(0, 3, 2, 1).reshape(t // PT, HEAD, N_HEADS * PT)


def _expand_cols(x, name):
    t = x.shape[0]
    tiles = WKV_CHUNK // PT

    def body(x_ref, o_ref):
        _, first = _pair_consts()
        for tl in range(tiles):
            tile = x_ref[tl]
            for j in range(PT):
                for p in range(N_HEADS // 2):
                    src = jnp.where(first, (2 * p) * PT + j, (2 * p + 1) * PT + j)
                    o_ref[tl * PT + j, :, p * LANES:(p + 1) * LANES] = jnp.take_along_axis(tile, src, axis=1)

    return pl.pallas_call(
        body, name=name, grid=(t // WKV_CHUNK,),
        in_specs=[pl.BlockSpec((tiles, HEAD, LANES), lambda i: (i, 0, 0))],
        out_specs=pl.BlockSpec((WKV_CHUNK, HEAD, RW), lambda i: (i, 0, 0)),
        out_shape=jax.ShapeDtypeStruct((t, HEAD, RW), F32),
        compiler_params=_cparams(("parallel",), VMEM_MID))(_to_pt(x))


def _wkv_fwd(w, k, z, b, v_exp):
    t = w.shape[0]
    nc = t // WKV_CHUNK
    pairs = N_HEADS // 2

    def body(w_ref, k_ref, z_ref, b_ref, v_ref, s_all, s_ref):
        @pl.when(pl.program_id(0) == 0)
        def _():
            s_ref[...] = jnp.zeros_like(s_ref)

        _, first = _pair_consts()

        def group(gi, carry):
            base = pl.multiple_of(gi * 8, 8)
            rows = [ref[pl.ds(base, 8), :] for ref in (w_ref, k_ref, z_ref, b_ref)]
            s = [s_ref[:, p * LANES:(p + 1) * LANES] for p in range(pairs)]
            for jj in range(8):
                for p in range(pairs):
                    cs = slice(p * LANES, (p + 1) * LANES)
                    wr, kr, zr, br = [x[jj:jj + 1, cs] for x in rows]
                    s_all[base + jj, :, cs] = s[p]
                    sa = _seg_sum_pair(s[p] * zr, first)
                    s[p] = s[p] * wr + sa * br + v_ref[base + jj, :, cs] * kr
            for p in range(pairs):
                s_ref[:, p * LANES:(p + 1) * LANES] = s[p]
            return carry

        lax.fori_loop(0, WKV_CHUNK // 8, group, 0)

    row = pl.BlockSpec((WKV_CHUNK, RW), lambda i: (i, 0))
    big = pl.BlockSpec((WKV_CHUNK, HEAD, RW), lambda i: (i, 0, 0))
    return pl.pallas_call(
        body, name="wkv_fwd", grid=(nc,), in_specs=[row] * 4 + [big], out_specs=[big, _full((HEAD, RW))],
        out_shape=[jax.ShapeDtypeStruct((t, HEAD, RW), F32), jax.ShapeDtypeStruct((HEAD, RW), F32)],
        compiler_params=_cparams(("arbitrary",), VMEM_MID))(w, k, z, b, v_exp)


def _wkv_out(r, s_all, s_last):
    t = r.shape[0]
    nc = t // WKV_CHUNK
    tiles = WKV_CHUNK // PT
    pairs = N_HEADS // 2

    def body(r_ref, s_ref, nxt_ref, last_ref, y_ref):
        lane, first = _pair_consts()
        after = jnp.where(pl.program_id(0) == nc - 1, last_ref[...], nxt_ref[0])
        for tl in range(tiles):
            ytile = jnp.zeros((HEAD, LANES), F32)
            for g in range(PT // 8):
                rows = r_ref[tl * PT + g * 8:tl * PT + g * 8 + 8, :]
                for jj in range(8):
                    tt = tl * PT + g * 8 + jj
                    j = g * 8 + jj
                    for p in range(pairs):
                        cs = slice(p * LANES, (p + 1) * LANES)
                        s = s_ref[tt + 1, :, cs] if tt + 1 < WKV_CHUNK else after[:, cs]
                        pr = s * rows[jj:jj + 1, cs]
                        y0 = _lane_sum(jnp.where(first, pr, 0.0))
                        y1 = _lane_sum(jnp.where(first, 0.0, pr))
                        ytile = jnp.where(lane == (2 * p) * PT + j, y0, ytile)
                        ytile = jnp.where(lane == (2 * p + 1) * PT + j, y1, ytile)
            y_ref[tl] = ytile

    row = pl.BlockSpec((WKV_CHUNK, RW), lambda i: (i, 0))
    pt = pl.BlockSpec((tiles, HEAD, LANES), lambda i: (i, 0, 0))
    big = pl.BlockSpec((WKV_CHUNK, HEAD, RW), lambda i: (i, 0, 0))
    nxt = pl.BlockSpec((1, HEAD, RW), lambda i: (jnp.minimum((i + 1) * WKV_CHUNK, t - 1), 0, 0))
    return pl.pallas_call(
        body, name="wkv_out", grid=(nc,), in_specs=[row, big, nxt, _full((HEAD, RW))], out_specs=pt,
        out_shape=jax.ShapeDtypeStruct((t // PT, HEAD, LANES), F32),
        compiler_params=_cparams(("parallel",), VMEM_MID))(r, s_all, s_all, s_last)


def _wkv_bwd(r, w, k, z, b, v_exp, s_all, dy_exp):
    t = r.shape[0]
    nc = t // WKV_CHUNK
    tiles = WKV_CHUNK // PT
    pairs = N_HEADS // 2

    def body(r_ref, w_ref, k_ref, z_ref, b_ref, v_ref, s_all_ref, dy_ref,
             dr_ref, dw_ref, dk_ref, dz_ref, db_ref, dv_ref, ds_ref):
        @pl.when(pl.program_id(0) == 0)
        def _():
            ds_ref[...] = jnp.zeros_like(ds_ref)

        lane, first = _pair_consts()
        col_sum = lambda x: jnp.sum(x, axis=0, keepdims=True)
        row8 = lax.broadcasted_iota(jnp.int32, (8, LANES), 0)
        for tl in reversed(range(tiles)):
            def group(gg, dvtile):
                gi = PT // 8 - 1 - gg
                base = pl.multiple_of(tl * PT + gi * 8, 8)
                rows = [ref[pl.ds(base, 8), :] for ref in (r_ref, w_ref, k_ref, z_ref, b_ref)]
                outs = (dr_ref, dw_ref, dk_ref, dz_ref, db_ref)
                tiles8 = {(id(o), p): jnp.zeros((8, LANES), F32) for o in outs for p in range(pairs)}
                ds = [ds_ref[:, p * LANES:(p + 1) * LANES] for p in range(pairs)]
                for jj in reversed(range(8)):
                    j = gi * 8 + jj
                    for p in range(pairs):
                        cs = slice(p * LANES, (p + 1) * LANES)

                        def put(ref, val, p=p, jj=jj):
                            tiles8[(id(ref), p)] = jnp.where(row8 == jj, val, tiles8[(id(ref), p)])

                        rr, wr, kr, zr, br = [x[jj:jj + 1, cs] for x in rows]
                        sp = s_all_ref[base + jj, :, cs]
                        vc = v_ref[base + jj, :, cs]
                        dyc = dy_ref[base + jj, :, cs]
                        sa = _seg_sum_pair(sp * zr, first)
                        st = sp * wr + sa * br + vc * kr
                        d = ds[p] + dyc * rr
                        put(dr_ref, col_sum(st * dyc))
                        dvk = d * kr
                        dv0 = _lane_sum(jnp.where(first, dvk, 0.0))
                        dv1 = _lane_sum(jnp.where(first, 0.0, dvk))
                        dvtile = jnp.where(lane == (2 * p) * PT + j, dv0, dvtile)
                        dvtile = jnp.where(lane == (2 * p + 1) * PT + j, dv1, dvtile)
                        put(dk_ref, col_sum(d * vc))
                        put(dw_ref, col_sum(sp * d))
                        u = _seg_sum_pair(d * br, first)
                        put(dz_ref, col_sum(sp * u))
                        put(db_ref, col_sum(d * sa))
                        ds[p] = d * wr + u * zr
                for p in range(pairs):
                    ds_ref[:, p * LANES:(p + 1) * LANES] = ds[p]
                for o in outs:
                    for p in range(pairs):
                        o[pl.ds(base, 8), p * LANES:(p + 1) * LANES] = tiles8[(id(o), p)]
                return dvtile

            dv_ref[tl] = lax.fori_loop(0, PT // 8, group, jnp.zeros((HEAD, LANES), F32))

    rev = lambda i: nc - 1 - i
    row = pl.BlockSpec((WKV_CHUNK, RW), lambda i: (rev(i), 0))
    pt = pl.BlockSpec((tiles, HEAD, LANES), lambda i: (rev(i), 0, 0))
    big = pl.BlockSpec((WKV_CHUNK, HEAD, RW), lambda i: (rev(i), 0, 0))
    return pl.pallas_call(
        body, name="wkv_bwd", grid=(nc,), in_specs=[row] * 5 + [big, big, big], out_specs=[row] * 5 + [pt],
        out_shape=[jax.ShapeDtypeStruct((t, RW), F32)] * 5 + [jax.ShapeDtypeStruct((t // PT, HEAD, LANES), F32)],
        scratch_shapes=[pltpu.VMEM((HEAD, RW), F32)],
        compiler_params=_cparams(("arbitrary",), VMEM_BIG))(r, w, k, z, b, v_exp, s_all, dy_exp)


LRU_CW = 128
_BX0 = SHIFT_COLS // LRU_CW
_BG0 = (SHIFT_COLS + LRU_W) // LRU_CW


def _lru_fn(bx, bg, cw, cb, ga, ba, gx, bxb, lam, sd, scan, dot):
    xc = cw[0:1] * sd(bx, 3) + cw[1:2] * sd(bx, 2) + cw[2:3] * sd(bx, 1) + cw[3:4] * bx + cb
    gr = jax.nn.sigmoid(dot(xc, ga) + ba)
    gi = jax.nn.sigmoid(dot(xc, gx) + bxb)
    log_a = -LRU_C * gr * jax.nn.softplus(-lam)
    a = jnp.exp(log_a)
    mult = jnp.sqrt(-jnp.tanh(log_a) * (jnp.exp(2.0 * log_a) + 1.0))
    return scan(a, xc * gi * mult) * jax.nn.gelu(bg)


def _lru_specs(t):
    col = lambda r, off=0: pl.BlockSpec((r, LRU_CW), lambda j: (0, j + off))
    diag = pl.BlockSpec((LRU_CW, LRU_CW), lambda j: (j, j))
    return col, [col(t, _BX0), col(t, _BG0), col(4), col(1), diag, col(1), diag, col(1), col(1)]


def _lru_fwd(p, cw, cb, ga, ba, gx, bxb, lam):
    t = p.shape[0]
    col, in_specs = _lru_specs(t)

    def body(*refs):
        o_ref = refs[-1]
        o_ref[...] = _lru_fn(*[x[...] for x in refs[:-1]], _shift_down, _lin_scan, _dot16).astype(BF16)

    return pl.pallas_call(body, name="lru_fwd", grid=(LRU_W // LRU_CW,), in_specs=in_specs, out_specs=col(t),
                          out_shape=jax.ShapeDtypeStruct((t, LRU_W), BF16),
                          compiler_params=_cparams(("parallel",), VMEM_MID))(p, p, cw, cb, ga, ba, gx, bxb, lam)


def _lru_bwd(p, cw, cb, ga, ba, gx, bxb, lam, dyb):
    t = p.shape[0]
    col, in_specs = _lru_specs(t)

    def body(*refs):
        ins, d_ref, outs = refs[:9], refs[9], refs[10:]
        fn = functools.partial(_lru_fn, sd=_make_sd(), scan=_make_scan(), dot=_make_dot16())
        _, vjp = jax.vjp(fn, *[x[...] for x in ins])
        g = vjp(d_ref[...])
        outs[0][...] = g[0].astype(BF16)
        outs[1][...] = g[1].astype(BF16)
        for o, val in zip(outs[2:], g[2:]):
            o[...] = val

    sq = pl.BlockSpec((LRU_CW, LRU_CW), lambda j: (j, 0))
    act = jax.ShapeDtypeStruct((t, LRU_W), BF16)
    vec = jax.ShapeDtypeStruct((1, LRU_W), F32)
    sqs = jax.ShapeDtypeStruct((LRU_W, LRU_CW), F32)
    return pl.pallas_call(body, name="lru_bwd", grid=(LRU_W // LRU_CW,), in_specs=in_specs + [col(t, RW // LRU_CW)],
                          out_specs=[col(t), col(t), col(4), col(1), sq, col(1), sq, col(1), col(1)],
                          out_shape=[act, act, jax.ShapeDtypeStruct((4, LRU_W), F32), vec, sqs, vec, sqs, vec, vec],
                          compiler_params=_cparams(("parallel",), VMEM_BIG))(p, p, cw, cb, ga, ba, gx, bxb, lam, dyb)


def _s5_disc_fn(a_re, a_im, log_dt, b_re, b_im, e):
    lam_re = jnp.minimum(a_re, -1e-4)
    lam_im = a_im
    dt = jnp.exp(log_dt)
    mag = jnp.exp(lam_re * dt)
    ab_re = mag * jnp.cos(lam_im * dt)
    ab_im = mag * jnp.sin(lam_im * dt)
    den = lam_re * lam_re + lam_im * lam_im
    zr = ab_re - 1.0
    q_re = jnp.dot((zr * lam_re + ab_im * lam_im) / den, e, precision=_HI)
    q_im = jnp.dot((ab_im * lam_re - zr * lam_im) / den, e, precision=_HI)
    return ab_re, ab_im, q_re * b_re - q_im * b_im, q_re * b_im + q_im * b_re


def _s5_disc_fwd(a_re, a_im, log_dt, b_re, b_im, e):
    def body(*refs):
        res = _s5_disc_fn(*[x[...] for x in refs[:6]])
        for o, val in zip(refs[6:], res):
            o[...] = val

    small = jax.ShapeDtypeStruct(a_re.shape, F32)
    wide = jax.ShapeDtypeStruct(b_re.shape, F32)
    return pl.pallas_call(body, name="s5_disc_fwd", out_shape=[small, small, wide, wide])(
        a_re, a_im, log_dt, b_re, b_im, e)


def _s5_disc_bwd(a_re, a_im, log_dt, b_re, b_im, e, cts):
    def body(*refs):
        ins, e_ref, ct, outs = refs[:5], refs[5], refs[6:10], refs[10:]
        _, vjp = jax.vjp(lambda *a: _s5_disc_fn(*a, e_ref[...]), *[x[...] for x in ins])
        for o, val in zip(outs, vjp(tuple(c[...] for c in ct))):
            o[...] = val

    shapes = [jax.ShapeDtypeStruct(x.shape, F32) for x in (a_re, a_im, log_dt, b_re, b_im)]
    return pl.pallas_call(body, name="s5_disc_bwd", out_shape=shapes)(a_re, a_im, log_dt, b_re, b_im, e, *cts)


def _cmul(a, b):
    return a[0] * b[0] - a[1] * b[1], a[0] * b[1] + a[1] * b[0]


def _s5_scan(sr, si, ab, reverse):
    n_tiles = sr.shape[0] // 8
    width = sr.shape[1]
    row8 = lax.broadcasted_iota(jnp.int32, (8, width), 0)
    p1 = ab
    p2 = _cmul(p1, p1)
    p4 = _cmul(p2, p2)
    pw = [p1]
    for _ in range(7):
        pw.append(_cmul(pw[-1], p1))
    cr = jnp.zeros((8, width), F32)
    ci = jnp.zeros((8, width), F32)
    for j in range(8):
        e = pw[7 - j] if reverse else pw[j]
        cr = jnp.where(row8 == j, e[0], cr)
        ci = jnp.where(row8 == j, e[1], ci)

    levels = []
    for d, q in ((1, p1), (2, p2), (4, p4)):
        keep = row8 < 8 - d if reverse else row8 >= d
        levels.append((d, (jnp.where(keep, q[0], 0.0), jnp.where(keep, q[1], 0.0))))

    def tile(i, carry):
        idx = n_tiles - 1 - i if reverse else i
        base = pl.multiple_of(idx * 8, 8)
        x = (sr[pl.ds(base, 8), :], si[pl.ds(base, 8), :])
        for d, q in levels:
            amt = 8 - d if reverse else d
            m = _cmul(q, (pltpu.roll(x[0], amt, 0), pltpu.roll(x[1], amt, 0)))
            x = (x[0] + m[0], x[1] + m[1])
        m = _cmul((cr, ci), carry)
        x = (x[0] + m[0], x[1] + m[1])
        sr[pl.ds(base, 8), :] = x[0]
        si[pl.ds(base, 8), :] = x[1]
        edge = slice(0, 1) if reverse else slice(7, 8)
        return x[0][edge], x[1][edge]

    zero = jnp.zeros((1, width), F32)
    lax.fori_loop(0, n_tiles, tile, (zero, zero))


_S5_W = S5_SLAB // S5_GROUP * S5_STATE


def _s5_specs(t):
    col = lambda r: pl.BlockSpec((r, S5_SLAB), lambda j: (0, j))
    bb = pl.BlockSpec((None, S5_SLAB, _S5_W), lambda j: (j, 0, 0))
    cd = pl.BlockSpec((None, _S5_W, S5_SLAB), lambda j: (j, 0, 0))
    ab = pl.BlockSpec((None, 1, _S5_W), lambda j: (j, 0, 0))
    return col, bb, cd, ab


def _s5_fwd(u, dvec, bbr, bbi, cdr, cdi, abr, abi):
    t, width = u.shape
    col, bb, cd, ab = _s5_specs(t)

    def body(u_ref, d_ref, bbr_ref, bbi_ref, cdr_ref, cdi_ref, abr_ref, abi_ref, o_ref, sr, si):
        uv = u_ref[...]
        sr[...] = _dot16(uv, bbr_ref[...])
        si[...] = _dot16(uv, bbi_ref[...])
        _s5_scan(sr, si, (abr_ref[...], abi_ref[...]), False)
        y = _dot16(sr[...], cdr_ref[...]) - _dot16(si[...], cdi_ref[...])
        o_ref[...] = jax.nn.gelu(y + d_ref[...] * uv).astype(BF16)

    return pl.pallas_call(body, name="s5_fwd", grid=(width // S5_SLAB,),
                          in_specs=[col(t), col(1), bb, bb, cd, cd, ab, ab], out_specs=col(t),
                          out_shape=jax.ShapeDtypeStruct((t, width), BF16),
                          scratch_shapes=[pltpu.VMEM((t, _S5_W), F32)] * 2,
                          compiler_params=_cparams(("parallel",), VMEM_BIG))(u, dvec, bbr, bbi, cdr, cdi, abr, abi)


def _s5_bwd(u, dvec, bbr, bbi, cdr, cdi, abr, abi, dyact):
    t, width = u.shape
    col, bb, cd, ab = _s5_specs(t)
    ns = width // S5_SLAB
    tn = (((0,), (0,)), ((), ()))
    nt = (((1,), (1,)), ((), ()))

    def body(u_ref, d_ref, bbr_ref, bbi_ref, cdr_ref, cdi_ref, abr_ref, abi_ref, dy_ref,
             du_ref, dd_ref, dbbr_ref, dbbi_ref, dcdr_ref, dcdi_ref, dabr_ref, dabi_ref, sr, si, gr, gi):
        uv = u_ref[...]
        dv = d_ref[...]
        abv = (abr_ref[...], abi_ref[...])
        sr[...] = _dot16(uv, bbr_ref[...])
        si[...] = _dot16(uv, bbi_ref[...])
        _s5_scan(sr, si, abv, False)
        y = _dot16(sr[...], cdr_ref[...]) - _dot16(si[...], cdi_ref[...])
        _, vjp = jax.vjp(jax.nn.gelu, y + dv * uv)
        (dpre,) = vjp(dy_ref[...].astype(F32))
        dd_ref[...] = jnp.sum(dpre * uv, axis=0, keepdims=True)
        dcdr_ref[...] = _dot16(sr[...], dpre, tn)
        dcdi_ref[...] = -_dot16(si[...], dpre, tn)
        gr[...] = _dot16(dpre, cdr_ref[...], nt)
        gi[...] = -_dot16(dpre, cdi_ref[...], nt)
        _s5_scan(gr, gi, (abv[0], -abv[1]), True)

        row8 = lax.broadcasted_iota(jnp.int32, (8, _S5_W), 0)

        def tile(i, carry):
            acc_r, acc_i, last_r, last_i = carry
            base = pl.multiple_of(i * 8, 8)
            s_r, s_i = sr[pl.ds(base, 8), :], si[pl.ds(base, 8), :]
            g_r, g_i = gr[pl.ds(base, 8), :], gi[pl.ds(base, 8), :]
            p_r = jnp.where(row8 == 0, last_r, pltpu.roll(s_r, 1, 0))
            p_i = jnp.where(row8 == 0, last_i, pltpu.roll(s_i, 1, 0))
            acc_r = acc_r + jnp.sum(g_r * p_r + g_i * p_i, axis=0, keepdims=True)
            acc_i = acc_i + jnp.sum(g_i * p_r - g_r * p_i, axis=0, keepdims=True)
            return acc_r, acc_i, s_r[7:8], s_i[7:8]

        zero = jnp.zeros((1, _S5_W), F32)
        acc_r, acc_i, _, _ = lax.fori_loop(0, t // 8, tile, (zero, zero, zero, zero))
        dabr_ref[...] = acc_r
        dabi_ref[...] = acc_i
        du_ref[...] = dpre * dv + _dot16(gr[...], bbr_ref[...], nt) + _dot16(gi[...], bbi_ref[...], nt)
        dbbr_ref[...] = _dot16(uv, gr[...], tn)
        dbbi_ref[...] = _dot16(uv, gi[...], tn)

    sds = jax.ShapeDtypeStruct
    return pl.pallas_call(
        body, name="s5_bwd", grid=(ns,), in_specs=[col(t), col(1), bb, bb, cd, cd, ab, ab, col(t)],
        out_specs=[col(t), col(1), bb, bb, cd, cd, ab, ab],
        out_shape=[sds((t, width), F32), sds((1, width), F32), sds((ns, S5_SLAB, _S5_W), F32),
                   sds((ns, S5_SLAB, _S5_W), F32), sds((ns, _S5_W, S5_SLAB), F32), sds((ns, _S5_W, S5_SLAB), F32),
                   sds((ns, 1, _S5_W), F32), sds((ns, 1, _S5_W), F32)],
        scratch_shapes=[pltpu.VMEM((t, _S5_W), F32)] * 4,
        compiler_params=_cparams(("parallel",), VMEM_BIG))(u, dvec, bbr, bbi, cdr, cdi, abr, abi, dyact)


def _gate_dense(w):
    h = w.shape[0]
    return jnp.einsum("hij,hg->higj", w, jnp.eye(h, dtype=F32)).reshape(h * HEAD, h * HEAD)


def _gate_blocks(d):
    x = d.reshape(LRU_W // LRU_CW, 2, HEAD, 2, HEAD)
    return jnp.einsum("tgihj,gh->tgij", x, jnp.eye(2, dtype=F32)).reshape(LRU_W // HEAD, HEAD, HEAD)


_GPS = S5_SLAB // S5_GROUP
_NS = S5_GROUPS // _GPS


def _s5_in_dense(bb):
    x = bb.reshape(_NS, _GPS, S5_STATE, S5_GROUP)
    return jnp.einsum("sgnc,gh->sgchn", x, jnp.eye(_GPS, dtype=F32)).reshape(_NS, S5_SLAB, _S5_W)


def _s5_in_blocks(d):
    x = d.reshape(_NS, _GPS, S5_GROUP, _GPS, S5_STATE)
    return jnp.einsum("sgchn,gh->sgnc", x, jnp.eye(_GPS, dtype=F32)).reshape(S5_GROUPS, S5_STATE * S5_GROUP)


def _s5_out_dense(c):
    x = c.reshape(_NS, _GPS, S5_GROUP, S5_STATE)
    return jnp.einsum("sgcn,gh->shngc", x, jnp.eye(_GPS, dtype=F32)).reshape(_NS, _S5_W, S5_SLAB)


def _s5_out_blocks(d):
    x = d.reshape(_NS, _GPS, S5_STATE, _GPS, S5_GROUP)
    return jnp.einsum("shngc,gh->sgcn", x, jnp.eye(_GPS, dtype=F32)).reshape(S5_GROUPS, S5_GROUP, S5_STATE)


def _local_step(x, tgt, w, late_weights, send_grads):
    d_model = x.shape[1]
    gs = {}
    gm = jnp.kron(jnp.eye(N_HEADS, dtype=F32), jnp.ones((HEAD, HEAD), F32))
    n_layers = w["f_norm_g"].shape[0]

    def ffn_fwd(xin, l):
        xn = _rms_fwd(xin, w["f_norm_g"][l:l + 1], f"rms_f{l}")
        h = _matmul(xn, w["f_w_up_t"][l], "nt", f"mm_f{l}_up")
        act = _ffn_mid_fwd(h, w["f_conv_w"][l], w["f_conv_b"][l:l + 1], f"ffn_mid_fwd{l}")
        return _matmul(act, w["f_w_down"][l], "nn", f"mm_f{l}_down", add=xin), (xin, xn, h, act)

    def ffn_bwd(g, saved, l):
        xin, xn, h, act = saved
        dact = _matmul(g, w["f_w_down"][l], "nt", f"mm_f{l}_dact")
        d_down = _matmul(act, g, "tn", f"mm_f{l}_ddown", out_dtype=BF16)
        dhg, dhv, dwg, dwv, dbg, dbv = _ffn_mid_bwd(h, w["f_conv_w"][l], w["f_conv_b"][l:l + 1], dact,
                                                    f"ffn_mid_bwd{l}")
        dh = jnp.concatenate([dhg, dhv], axis=1)
        dxn = _matmul(dh, w["f_w_up_t"][l], "nn", f"mm_f{l}_dxn")
        d_up = _matmul(dh, xn, "tn", f"mm_f{l}_dup", out_dtype=BF16)
        dx, dgn = _rms_bwd(xin, w["f_norm_g"][l:l + 1], dxn, g, f"rms_f{l}_bwd")
        return dx, d_up, d_down, jnp.concatenate([dwg, dwv], axis=1), jnp.concatenate([dbg, dbv], axis=1), dgn

    xn0 = _rms_fwd(x, w["e_norm_g"], "rms_e")
    p = _matmul(xn0, w["e_w_in_t"], "nt", "mm_e_in")
    pam = _tshift_fwd(p, w["e_mu"])
    pw = dict(w0=w["e_w0"], w2=w["e_w2"][0], a0=w["e_a0"], a2=w["e_a2"][0], g2=w["e_g2"][0],
              k_k=w["e_k_k"], k_a=w["e_k_a"])
    r, dec, k2, v, z, b, gate = _rwkv_prep_fwd(pam, pw, gm)
    v_exp = _expand_cols(v, "wkv_expand_v")
    s_all, s_last = _wkv_fwd(dec, k2, z, b, v_exp)
    y_pt = _wkv_out(r, s_all, s_last)
    y = _from_pt(y_pt)
    rk = w["e_r_k"].reshape(1, RW)
    ya = _rwkv_post_fwd(y, r, k2, v, gate, w["e_ln_w"], w["e_ln_b"], rk, gm)
    ga, gx = _gate_dense(w["e_gate_a_w"][0]), _gate_dense(w["e_gate_x_w"][0])
    lru_w = (w["e_conv_w"][0], w["e_conv_b"], ga, w["e_gate_a_b"], gx, w["e_gate_x_b"], w["e_lru_lambda"])
    yb = _lru_fwd(p, *lru_w)
    ycat = jnp.concatenate([ya, yb], axis=1)
    x1 = _matmul(ycat, w["e_w_out"], "nn", "mm_e_out", add=x)
    w = {**w, **late_weights(x1)}
    x2, ffn0 = ffn_fwd(x1, 0)

    xn1 = _rms_fwd(x2, w["o_norm_g"], "rms_o")
    u = _matmul(xn1, w["o_w_in"], "nn", "mm_o_in")
    expand = jnp.kron(jnp.eye(S5_STATE, dtype=F32), jnp.ones((1, S5_GROUP), F32))
    disc_in = (w["o_A_re"][0], w["o_A_im"][0], w["o_log_dt"].reshape(S5_GROUPS, 1),
               w["o_B_re"][0].reshape(S5_GROUPS, -1), w["o_B_im"][0].reshape(S5_GROUPS, -1), expand)
    ab_re, ab_im, bb_re, bb_im = _s5_disc_fwd(*disc_in)
    s5_w = (w["o_D"], _s5_in_dense(bb_re), _s5_in_dense(bb_im), _s5_out_dense(w["o_C_re"][0]),
            _s5_out_dense(w["o_C_im"][0]), ab_re.reshape(_NS, 1, _S5_W), ab_im.reshape(_NS, 1, _S5_W))
    yact = _s5_fwd(u, *s5_w)
    zz = _matmul(yact, w["o_w_glu_t"], "nt", "mm_o_glu")
    x3 = _glu_fwd(x2, zz)
    x4, ffn1 = ffn_fwd(x3, 1)

    loss, g, gs["final_norm_g"] = _loss_head(x4, w["final_norm_g"].reshape(1, d_model), tgt)
    gs["final_norm_g"] = gs["final_norm_g"].reshape(d_model)

    g, up1, down1, dcw1, dcb1, dfn1 = ffn_bwd(g, ffn1, 1)
    dz = _glu_bwd(zz, g)
    dyact = _matmul(dz, w["o_w_glu_t"], "nn", "mm_o_dyact")
    d_glu = _matmul(dz, yact, "tn", "mm_o_dglu", out_dtype=BF16)
    du, gs["o_D"], dbbr, dbbi, dcdr, dcdi, dabr, dabi = _s5_bwd(u, *s5_w, dyact)
    gs["o_C_re"] = _s5_out_blocks(dcdr)[None]
    gs["o_C_im"] = _s5_out_blocks(dcdi)[None]
    cts = (dabr.reshape(S5_GROUPS, S5_STATE), dabi.reshape(S5_GROUPS, S5_STATE), _s5_in_blocks(dbbr),
           _s5_in_blocks(dbbi))
    da_re, da_im, dlog_dt, db_re, db_im = _s5_disc_bwd(*disc_in, cts)
    gs["o_A_re"], gs["o_A_im"], gs["o_log_dt"] = da_re[None], da_im[None], dlog_dt.reshape(1, S5_GROUPS)
    gs["o_B_re"] = db_re.reshape(w["o_B_re"].shape)
    gs["o_B_im"] = db_im.reshape(w["o_B_im"].shape)
    dxn = _matmul(du, w["o_w_in"], "nt", "mm_o_dxn")
    d_oin = _matmul(xn1, du, "tn", "mm_o_din", out_dtype=BF16)
    g, gs["o_norm_g"] = _rms_bwd(x2, w["o_norm_g"], dxn, g, "rms_o_bwd")
    g = send_grads("a", [("f_w_up", 1, up1), ("f_w_down", 1, down1), ("o_w_glu", 0, d_glu), ("o_w_in", 0, d_oin)], g)

    g, up0, down0, dcw0, dcb0, dfn0 = ffn_bwd(g, ffn0, 0)
    gs["f_conv_w"] = jnp.stack([dcw0, dcw1])
    gs["f_conv_b"] = jnp.concatenate([dcb0, dcb1], axis=0)
    gs["f_norm_g"] = jnp.concatenate([dfn0, dfn1], axis=0)

    dycat = _matmul(g, w["e_w_out"], "nt", "mm_e_dycat")
    d_eout = _matmul(ycat, g, "tn", "mm_e_dout", out_dtype=BF16)
    dycat = send_grads("b", [("f_w_up", 0, up0), ("f_w_down", 0, down0), ("e_w_out", 0, d_eout)], dycat)
    dy, dr1, dk1, dv1, dgate, gs["e_ln_w"], gs["e_ln_b"], drk = _rwkv_post_bwd(
        y, r, k2, v, gate, w["e_ln_w"], w["e_ln_b"], rk, gm, dycat)
    gs["e_r_k"] = drk.reshape(w["e_r_k"].shape)
    dr2, ddec, dk2, dzz, dbb, dv_pt = _wkv_bwd(r, dec, k2, z, b, v_exp, s_all, _expand_cols(dy, "wkv_expand_dy"))
    dpam, gs["e_w0"], dw2, gs["e_a0"], da2, dg2, gs["e_k_k"], gs["e_k_a"] = _rwkv_prep_bwd(
        pam, pw, gm, (dr2, ddec, dk2, _from_pt(dv_pt), dzz, dbb, dgate), (dr1, dk1, dv1))
    gs["e_w2"], gs["e_a2"], gs["e_g2"] = dw2[None], da2[None], dg2[None]
    dpa, gs["e_mu"] = _tshift_bwd(p, w["e_mu"], dpam)
    dbx, dbg, dcw, gs["e_conv_b"], dga, gs["e_gate_a_b"], dgx, gs["e_gate_x_b"], gs["e_lru_lambda"] = _lru_bwd(
        p, *lru_w, dycat)
    gs["e_conv_w"] = dcw[None]
    gs["e_gate_a_w"] = _gate_blocks(dga)[None]
    gs["e_gate_x_w"] = _gate_blocks(dgx)[None]
    dp = jnp.concatenate([dpa, dbx, dbg], axis=1)
    dxn = _matmul(dp, w["e_w_in_t"], "nn", "mm_e_dxn")
    d_ein = _matmul(dp, xn0, "tn", "mm_e_din", out_dtype=BF16)
    grad_x, gs["e_norm_g"] = _rms_bwd(x, w["e_norm_g"], dxn, g, "rms_e_bwd")
    grad_x = send_grads("c", [("e_w_in", 0, d_ein)], grad_x)
    return loss, grad_x, gs


CAST_ROWS = 256


def _cast_shard(w3, layer, transpose, chip, name):
    _, rows, cols = w3.shape
    tr = _tile(rows, (CAST_ROWS, 176, 128))

    def body(c_ref, w_ref, o_ref):
        v = w_ref[...]
        o_ref[...] = (v.T if transpose else v).astype(BF16)

    in_spec = pl.BlockSpec((None, tr, cols), lambda i, c: (layer, i, 0))
    if transpose:
        out_spec, shape = pl.BlockSpec((None, cols, tr), lambda i, c: (c[0], 0, i)), (cols, rows)
    else:
        out_spec, shape = pl.BlockSpec((None, tr, cols), lambda i, c: (c[0], i, 0)), (rows, cols)
    grid_spec = pltpu.PrefetchScalarGridSpec(num_scalar_prefetch=1, grid=(rows // tr,), in_specs=[in_spec],
                                             out_specs=out_spec)
    return pl.pallas_call(body, name=name, grid_spec=grid_spec,
                          out_shape=jax.ShapeDtypeStruct((N_CHIPS,) + shape, BF16),
                          compiler_params=_cparams(("parallel",), VMEM_MID))(chip, w3)


_ANY = pl.BlockSpec(memory_space=pl.ANY)


def _coords():
    return lax.axis_index("x"), lax.axis_index("y"), lax.axis_index("c")


def _flip(v, d):
    return 1 - v if d else v


_CHIP_RELS = ((1, 0), (0, 1), (1, 1))
_DEV_RELS = tuple((dx, dy, dc) for dx in (0, 1) for dy in (0, 1) for dc in (0, 1))[1:]


_HBM = pl.BlockSpec(memory_space=pltpu.HBM)
_SEM = pl.BlockSpec(memory_space=pltpu.SEMAPHORE)
_EFFECT = pltpu.SideEffectType.DATAFLOW_SIDE_EFFECTING


def _in_hbm(a):
    return pltpu.with_memory_space_constraint(a, pltpu.HBM)


def _gather_copies(bufs, send, recv, landed):
    x, y, c = _coords()
    me = 2 * x + y
    res = []
    for i, buf in enumerate(bufs):
        for j, (dx, dy) in enumerate(_CHIP_RELS):
            px, py = _flip(x, dx), _flip(y, dy)
            k = i * len(_CHIP_RELS) + j
            res.append(pltpu.make_async_remote_copy(
                src_ref=buf.at[me], dst_ref=buf.at[2 * px + py if landed else me], send_sem=send.at[k],
                recv_sem=recv.at[k], device_id=(px, py, c), device_id_type=MESH))
    return res


def _scatter_copies(srcs, lands, send, recv, landed):
    x, y, c = _coords()
    me = 4 * x + 2 * y + c
    res = []
    for i, (src, land) in enumerate(zip(srcs, lands)):
        for j, (dx, dy, dc) in enumerate(_DEV_RELS):
            peer = (_flip(x, dx), _flip(y, dy), _flip(c, dc))
            pid = 4 * peer[0] + 2 * peer[1] + peer[2]
            k = i * len(_DEV_RELS) + j
            res.append(pltpu.make_async_remote_copy(
                src_ref=src.at[pid], dst_ref=land.at[pid if landed else me], send_sem=send.at[k],
                recv_sem=recv.at[k], device_id=peer, device_id_type=MESH))
    return res


def _split_start(bufs, n_src, copies, n_rel, name, after):
    n = len(bufs)
    nk = n_src * n_rel

    def body(*refs):
        ins, send, recv, token = refs[:n], refs[n + 1 + n], refs[n + 2 + n], refs[-1]
        for cp in copies(ins, send, recv, False):
            cp.start()
        token[...] = jnp.zeros_like(token)

    res = pl.pallas_call(
        body, name=name, in_specs=[_HBM] * n + [_ANY],
        out_specs=[_HBM] * n + [_SEM, _SEM, pl.BlockSpec(memory_space=pltpu.VMEM)],
        out_shape=[pltpu.HBM(b.shape, b.dtype) for b in bufs]
        + [pltpu.SemaphoreType.DMA((nk,)), pltpu.SemaphoreType.DMA((nk,)), jax.ShapeDtypeStruct((8, LANES), F32)],
        input_output_aliases={i: i for i in range(n)},
        compiler_params=pltpu.CompilerParams(has_side_effects=_EFFECT))(*[_in_hbm(b) for b in bufs], after)
    return res[n], res[n + 1], list(res[:n]), res[n + 2]


def _split_wait(bufs, send, recv, copies, name, after):
    n = len(bufs)

    def body(*refs):
        ins, send_ref, recv_ref = refs[:n], refs[n], refs[n + 1]
        for cp in copies(ins, send_ref, recv_ref, True):
            cp.wait_send()
            cp.wait_recv()

    return pl.pallas_call(
        body, name=name, in_specs=[_HBM] * n + [_SEM, _SEM, _ANY], out_specs=[_HBM] * n,
        out_shape=[pltpu.HBM(b.shape, b.dtype) for b in bufs], input_output_aliases={i: i for i in range(n)},
        compiler_params=pltpu.CompilerParams(has_side_effects=_EFFECT))(*bufs, send, recv, after)


def _gather_start(bufs, name, after):
    return _split_start(bufs, len(bufs), _gather_copies, len(_CHIP_RELS), name, after)


def _gather_wait(bufs, send, recv, name, after):
    return _split_wait(bufs, send, recv, _gather_copies, name, after)


def _scatter_start(srcs, name, after):
    n = len(srcs)
    lands = [lax.empty(a.shape, a.dtype) for a in srcs]
    fn = lambda refs, send, recv, landed: _scatter_copies(refs[:n], refs[n:], send, recv, landed)
    send, recv, bufs, token = _split_start(list(srcs) + lands, n, fn, len(_DEV_RELS), name, after)
    return send, recv, bufs, token


def _scatter_wait(bufs, send, recv, name, after):
    n = len(bufs) // 2
    fn = lambda refs, s, r, landed: _scatter_copies(refs[:n], refs[n:], s, r, landed)
    res = _split_wait(bufs, send, recv, fn, name, after)
    return res[:n], res[n:]


def _sum_segments(src, land, me, name):
    nd, seg, cols = src.shape
    ts = _tile(seg, (256, 176, 128))

    def body(m_ref, *refs):
        o_ref = refs[-1]
        acc = refs[0][...].astype(F32)
        for r in refs[1:-1]:
            acc = acc + r[...].astype(F32)
        o_ref[...] = acc

    def peer(rel):
        bits = 4 * rel[0] + 2 * rel[1] + rel[2]
        return pl.BlockSpec((None, ts, cols), lambda i, m: (jnp.bitwise_xor(m[0], bits), i, 0))

    grid_spec = pltpu.PrefetchScalarGridSpec(
        num_scalar_prefetch=1, grid=(seg // ts,),
        in_specs=[pl.BlockSpec((None, ts, cols), lambda i, m: (m[0], i, 0))] + [peer(r) for r in _DEV_RELS],
        out_specs=pl.BlockSpec((None, ts, cols), lambda i, m: (m[1], i, 0)))
    return pl.pallas_call(body, name=name, grid_spec=grid_spec,
                          out_shape=jax.ShapeDtypeStruct((2, seg, cols), F32),
                          compiler_params=_cparams(("parallel",), VMEM_MID))(me, src, *[land] * len(_DEV_RELS))


def _exchange_sibling(arrs):
    n = len(arrs)

    def body(*refs):
        outs, (send, recv) = refs[n:2 * n], refs[2 * n:]
        x, y, c = _coords()
        sib = (x, y, 1 - c)
        sends, recvs = [], []
        for i in range(n):
            cp = pltpu.make_async_remote_copy(src_ref=outs[i].at[c], dst_ref=outs[i].at[c], send_sem=send.at[i],
                                              recv_sem=recv.at[i], device_id=sib, device_id_type=MESH)
            cp.start()
            sends.append(cp)
            recvs.append(pltpu.make_async_remote_copy(src_ref=outs[i].at[c], dst_ref=outs[i].at[1 - c],
                                                      send_sem=send.at[i], recv_sem=recv.at[i], device_id=sib,
                                                      device_id_type=MESH))
        for cp in recvs:
            cp.wait_recv()
        for cp in sends:
            cp.wait_send()

    return pl.pallas_call(
        body, name="exchange_sibling", in_specs=[_ANY] * n, out_specs=[_ANY] * n,
        out_shape=[jax.ShapeDtypeStruct(a.shape, a.dtype) for a in arrs],
        input_output_aliases={i: i for i in range(n)},
        scratch_shapes=[pltpu.SemaphoreType.DMA((n,)), pltpu.SemaphoreType.DMA((n,))])(*arrs)


def _allreduce_small(vec):
    nd, rows, lanes = vec.shape
    nr = len(_DEV_RELS)

    def body(in_ref, out_ref, stage, red, send, recv):
        x, y, c = _coords()
        me = 4 * x + 2 * y + c
        peers = []
        for dx, dy, dc in _DEV_RELS:
            peer = (_flip(x, dx), _flip(y, dy), _flip(c, dc))
            peers.append((peer, 4 * peer[0] + 2 * peer[1] + peer[2]))

        def copy(src, dst, k, peer):
            return pltpu.make_async_remote_copy(src_ref=src, dst_ref=dst, send_sem=send.at[k], recv_sem=recv.at[k],
                                                device_id=peer, device_id_type=MESH)

        first = [copy(in_ref.at[pid], stage.at[me], j, peer) for j, (peer, pid) in enumerate(peers)]
        for cp in first:
            cp.start()
        stage[me] = in_ref[me]
        for j, (peer, pid) in enumerate(peers):
            copy(in_ref.at[pid], stage.at[pid], j, peer).wait_recv()
        acc = stage[0]
        for d in range(1, nd):
            acc = acc + stage[d]
        red[...] = acc
        out_ref[me] = acc
        second = [copy(red, out_ref.at[me], nr + j, peer) for j, (peer, pid) in enumerate(peers)]
        for cp in second:
            cp.start()
        for j, (peer, pid) in enumerate(peers):
            copy(red, out_ref.at[pid], nr + j, peer).wait_recv()
        for cp in first + second:
            cp.wait_send()

    vm = pl.BlockSpec(memory_space=pltpu.VMEM)
    return pl.pallas_call(
        body, name="allreduce_small", in_specs=[vm], out_specs=vm,
        out_shape=jax.ShapeDtypeStruct(vec.shape, F32),
        scratch_shapes=[pltpu.VMEM(vec.shape, F32), pltpu.VMEM((rows, lanes), F32),
                        pltpu.SemaphoreType.DMA((2 * nr,)), pltpu.SemaphoreType.DMA((2 * nr,))],
        compiler_params=_cparams(None, VMEM_MID))(vec)


def _adam_math(w, g, m, v):
    m2 = ADAM_B1 * m + (1.0 - ADAM_B1) * g
    v2 = ADAM_B2 * v + (1.0 - ADAM_B2) * (g * g)
    m_hat = m2 / (1.0 - ADAM_B1 ** ADAM_STEP)
    v_hat = v2 / (1.0 - ADAM_B2 ** ADAM_STEP)
    return -ADAM_LR * (m_hat / (jnp.sqrt(v_hat) + ADAM_EPS) + ADAM_WD * w), m2, v2


def _adamw_big(w3, m3, v3, layer, g, transposed, name, prev=None):
    nl, rows, cols = w3.shape
    tr = 128 if transposed else _tile(rows, (256, 176, 128))

    def body(w_ref, m_ref, v_ref, g_ref, *rest):
        go_ref, d_ref, mo_ref, vo_ref = rest[-4:]
        g_val = g_ref[...].T if transposed else g_ref[...]
        go_ref[...] = g_val
        d_ref[...], mo_ref[...], vo_ref[...] = _adam_math(w_ref[...], g_val, m_ref[...], v_ref[...])

    wspec = pl.BlockSpec((None, tr, cols), lambda i: (layer, i, 0))
    gspec = pl.BlockSpec((cols, tr), lambda i: (0, i)) if transposed else pl.BlockSpec((tr, cols), lambda i: (i, 0))
    extra = [] if prev is None else list(prev)
    return pl.pallas_call(body, name=name, grid=(rows // tr,),
                          in_specs=[wspec, wspec, wspec, gspec] + [_ANY] * len(extra),
                          out_specs=[wspec] * 4, out_shape=[jax.ShapeDtypeStruct((nl, rows, cols), F32)] * 4,
                          input_output_aliases={4 + i: i for i in range(len(extra))},
                          compiler_params=_cparams(("parallel",), VMEM_MID))(w3, m3, v3, g, *extra)


def _adamw_small(w, g, m, v):
    rows = w.shape[0]
    tr = _tile(rows, (512, 256))

    def body(w_ref, g_ref, m_ref, v_ref, d_ref, mo_ref, vo_ref):
        d_ref[...], mo_ref[...], vo_ref[...] = _adam_math(w_ref[...], g_ref[...], m_ref[...], v_ref[...])

    spec = pl.BlockSpec((tr, LANES), lambda i: (i, 0))
    return pl.pallas_call(body, name="adamw_small", grid=(rows // tr,), in_specs=[spec] * 4, out_specs=[spec] * 3,
                          out_shape=[jax.ShapeDtypeStruct(w.shape, F32)] * 3,
                          compiler_params=_cparams(("parallel",)))(w, g, m, v)


PACK_ROWS = 8


def _packed_rows(shape):
    size = 1
    for d in shape:
        size *= d
    return -(-size // (PACK_ROWS * LANES)) * PACK_ROWS


def _pack(arrs, row_mult):
    parts = []
    for a in arrs:
        flat = a.reshape(-1).astype(F32)
        rows = _packed_rows(a.shape)
        parts.append(jnp.pad(flat, (0, rows * LANES - flat.shape[0])).reshape(rows, LANES))
    total = sum(p.shape[0] for p in parts)
    fill = -(-total // row_mult) * row_mult - total
    if fill:
        parts.append(jnp.zeros((fill, LANES), F32))
    return jnp.concatenate(parts, axis=0)


def _unpack(packed, shapes):
    out, off = [], 0
    for s in shapes:
        rows = _packed_rows(s)
        size = 1
        for d in s:
            size *= d
        out.append(packed[off:off + rows].reshape(-1)[:size].reshape(s))
        off += rows
    return out


_SMALL_REP = ("e_norm_g", "e_mu", "e_w0", "e_a0", "e_k_k", "e_k_a", "e_r_k", "e_ln_w", "e_ln_b", "e_conv_b",
              "e_gate_a_w", "e_gate_a_b", "e_gate_x_w", "e_gate_x_b", "e_lru_lambda", "o_A_re", "o_A_im", "o_log_dt",
              "o_B_re", "o_B_im", "o_C_re", "o_C_im", "f_norm_g", "f_conv_b", "final_norm_g")
_SMALL_SH = ("e_w2", "e_a2", "e_g2", "e_conv_w", "o_norm_g", "o_D", "f_conv_w")
_LARGE = (("e_w_in", True), ("e_w_out", False), ("o_w_in", False), ("o_w_glu", True), ("f_w_up", True),
        ("f_w_down", False))
_ORDER = ("e_norm_g", "e_w_in", "e_mu", "e_w0", "e_w2", "e_a0", "e_a2", "e_g2", "e_k_k", "e_k_a", "e_r_k", "e_ln_w",
          "e_ln_b", "e_conv_w", "e_conv_b", "e_gate_a_w", "e_gate_a_b", "e_gate_x_w", "e_gate_x_b", "e_lru_lambda",
          "e_w_out", "o_norm_g", "o_w_in", "o_A_re", "o_A_im", "o_log_dt", "o_B_re", "o_B_im", "o_C_re", "o_C_im",
          "o_D", "o_w_glu", "f_norm_g", "f_w_up", "f_conv_w", "f_conv_b", "f_w_down", "final_norm_g")
N_CHIPS = 4
N_DEV = 8


def _step(x, tgt, wts, ms, vs):
    xi, yi, ci = _coords()
    chip = 2 * xi + yi
    chip1 = chip.astype(jnp.int32).reshape(1)
    me2 = jnp.stack([4 * xi + 2 * yi + ci, ci]).astype(jnp.int32)
    by_cols = dict(_LARGE)

    bufs = {(name, l): _cast_shard(wts[name], l, by_cols[name], chip1, f"cast_{name}{l}")
            for name, _ in _LARGE for l in range(wts[name].shape[0])}
    sh_shapes = [wts[n].shape for n in _SMALL_SH]
    packed = _pack([wts[n] for n in _SMALL_SH], 8)
    small_buf = lax.dynamic_update_slice(jnp.zeros((N_CHIPS,) + packed.shape, F32), packed[None], (chip, 0, 0))
    early = [("e_w_in", 0), ("e_w_out", 0)]
    late = [k for k in bufs if k not in early]
    send, recv, thru, token = _gather_start([bufs[k] for k in early] + [small_buf], "gather_start_a", x)
    got = _gather_wait(thru, send, recv, "gather_wait_a", token)
    send_b, recv_b, thru_b, token = _gather_start([bufs[k] for k in late], "gather_start_b", got[0])
    x, _ = lax.optimization_barrier((x, token))

    def rows(g):
        return g.reshape(N_CHIPS * g.shape[1], g.shape[2])

    full = {n: wts[n] for n in _SMALL_REP}
    full["e_w_in_t"], full["e_w_out"] = rows(got[0]), rows(got[1])
    per_chip = [_unpack(got[2][k], sh_shapes) for k in range(N_CHIPS)]
    for i, n in enumerate(_SMALL_SH):
        full[n] = jnp.concatenate([per_chip[k][i] for k in range(N_CHIPS)], axis=-1)

    def late_weights(after):
        res = dict(zip(late, _gather_wait(thru_b, send_b, recv_b, "gather_wait_b", after)))
        return {"o_w_in": rows(res[("o_w_in", 0)]), "o_w_glu_t": rows(res[("o_w_glu", 0)]),
                "f_w_up_t": [rows(res[("f_w_up", l)]) for l in range(2)],
                "f_w_down": [rows(res[("f_w_down", l)]) for l in range(2)]}

    pending = []

    def send_grads(tag, items, carry):
        srcs = [g.reshape(N_DEV, g.shape[0] // N_DEV, g.shape[1]) for _, _, g in items]
        s_sem, r_sem, both, tok = _scatter_start(srcs, f"scatter_start_{tag}", carry)
        pending.append((tag, [(name, l) for name, l, _ in items], s_sem, r_sem, both))
        carry, _ = lax.optimization_barrier((carry, tok))
        return carry

    loss, grad_x, gs = _local_step(x, tgt, full, late_weights, send_grads)

    final = {}
    small = _SMALL_REP + _SMALL_SH
    shapes = [gs[n].shape for n in small]
    red = _allreduce_small(_pack([gs[n] for n in small], 8 * N_DEV).reshape(N_DEV, -1, LANES))
    tot = dict(zip(small, _unpack(red.reshape(-1, LANES), shapes)))
    for n in _SMALL_SH:
        width = wts[n].shape[-1]
        tot[n] = lax.dynamic_slice_in_dim(tot[n], chip * width, width, axis=tot[n].ndim - 1)
    loc_shapes = [wts[n].shape for n in small]
    pk = lambda d: _pack([d[n] for n in small], 256)
    delta, new_m, new_v = _adamw_small(pk(wts), pk(tot), pk(ms), pk(vs))
    for n, g, d, m2, v2 in zip(small, [tot[n] for n in small], _unpack(delta, loc_shapes), _unpack(new_m, loc_shapes),
                               _unpack(new_v, loc_shapes)):
        final[n] = [g.reshape(wts[n].shape), d, m2, v2]

    halves, keys = [], []
    for tag, names, s_sem, r_sem, both in pending:
        srcs, lands = _scatter_wait(both, s_sem, r_sem, f"scatter_wait_{tag}", new_v)
        for (name, l), src, land in zip(names, srcs, lands):
            halves.append(_sum_segments(src, land, me2, f"sum_{name}{l}"))
            keys.append((name, l))
    shards = _exchange_sibling(halves)
    for s, (name, l) in zip(shards, keys):
        final[name] = _adamw_big(wts[name], ms[name], vs[name], l, s.reshape(2 * s.shape[1], s.shape[2]),
                                 by_cols[name], f"adamw_{name}{l}", prev=final.get(name))

    loss = lax.psum(loss[0, 0], ("x", "y", "c"))
    res = [loss, grad_x[None]]
    for k in range(4):
        res += [final[n][k] for n in _ORDER]
    return tuple(res)


def kernel(x, e_norm_g, e_w_in, e_mu, e_w0, e_w2, e_a0, e_a2, e_g2, e_k_k, e_k_a, e_r_k, e_ln_w, e_ln_b, e_conv_w, e_conv_b, e_gate_a_w, e_gate_a_b, e_gate_x_w, e_gate_x_b, e_lru_lambda, e_w_out, o_norm_g, o_w_in, o_A_re, o_A_im, o_log_dt, o_B_re, o_B_im, o_C_re, o_C_im, o_D, o_w_glu, f_norm_g, f_w_up, f_conv_w, f_conv_b, f_w_down, final_norm_g, loss_target, m_e_norm_g, m_e_w_in, m_e_mu, m_e_w0, m_e_w2, m_e_a0, m_e_a2, m_e_g2, m_e_k_k, m_e_k_a, m_e_r_k, m_e_ln_w, m_e_ln_b, m_e_conv_w, m_e_conv_b, m_e_gate_a_w, m_e_gate_a_b, m_e_gate_x_w, m_e_gate_x_b, m_e_lru_lambda, m_e_w_out, m_o_norm_g, m_o_w_in, m_o_A_re, m_o_A_im, m_o_log_dt, m_o_B_re, m_o_B_im, m_o_C_re, m_o_C_im, m_o_D, m_o_w_glu, m_f_norm_g, m_f_w_up, m_f_conv_w, m_f_conv_b, m_f_w_down, m_final_norm_g, v_e_norm_g, v_e_w_in, v_e_mu, v_e_w0, v_e_w2, v_e_a0, v_e_a2, v_e_g2, v_e_k_k, v_e_k_a, v_e_r_k, v_e_ln_w, v_e_ln_b, v_e_conv_w, v_e_conv_b, v_e_gate_a_w, v_e_gate_a_b, v_e_gate_x_w, v_e_gate_x_b, v_e_lru_lambda, v_e_w_out, v_o_norm_g, v_o_w_in, v_o_A_re, v_o_A_im, v_o_log_dt, v_o_B_re, v_o_B_im, v_o_C_re, v_o_C_im, v_o_D, v_o_w_glu, v_f_norm_g, v_f_w_up, v_f_conv_w, v_f_conv_b, v_f_w_down, v_final_norm_g):
    args = locals()
    wts = {n: args[n] for n in _ORDER}
    ms = {n: args["m_" + n] for n in _ORDER}
    vs = {n: args["v_" + n] for n in _ORDER}
    return _step(x[0], loss_target[0], wts, ms, vs)
```

```python
import functools

import jax
import jax.numpy as jnp
from jax import lax
from jax.experimental import pallas as pl
from jax.experimental.pallas import tpu as pltpu

F32 = jnp.float32
BF16 = jnp.bfloat16
MESH = pl.DeviceIdType.MESH

HEAD = 64
RW = 512
N_HEADS = RW // HEAD
LRU_W = 512
SHIFT_COLS = 1792
W_LORA, A_LORA, G_LORA = 64, 64, 128
S5_GROUPS, S5_GROUP, S5_STATE = 64, 16, 64
D_FF = 2816
NORM_EPS = 1e-6
GN_EPS = 64e-5
LRU_C = 8.0
ADAM_LR, ADAM_B1, ADAM_B2, ADAM_EPS, ADAM_WD, ADAM_STEP = 0.001, 0.9, 0.999, 1e-08, 0.01, 10

VMEM_BIG = 56 * 1024 * 1024
VMEM_MID = 40 * 1024 * 1024
LANES = 128
PT = 16
WKV_CHUNK = 32
S5_SLAB = 128


def _cparams(sem=None, vmem=None):
    kw = {}
    if sem is not None:
        kw["dimension_semantics"] = sem
    if vmem is not None:
        kw["vmem_limit_bytes"] = vmem
    return pltpu.CompilerParams(**kw)


def _tile(dim, cands):
    for c in cands:
        if dim % c == 0:
            return c
    return dim


def _full(shape):
    n = len(shape)
    return pl.BlockSpec(shape, lambda *_: (0,) * n)


_TILES = (2816, 2048, 1408, 1024, 512, 256, 128)
MM_BUDGET = 36 * 1024 * 1024
VMEM_SLACK = 12 * 1024 * 1024


def _mm_tiles(m, n, k, size_a, size_b, size_o, has_add):
    best = None
    for tm in _TILES:
        for tk in _TILES:
            for tn in _TILES:
                if m % tm or n % tn or k % tk:
                    continue
                need = 2 * (tm * tk * size_a + tk * tn * size_b + tm * tn * size_o) + tm * tn * 4 * (1 + 2 * has_add)
                if k > tk:
                    need += tm * tn * 4
                if need <= MM_BUDGET:
                    cand = (tm, tk, tn)
                    if best is None or cand > best[0]:
                        best = (cand, need)
    (tm, tk, tn), need = best
    return tm, tn, tk, need


def _matmul(a, b, mode, name, out_dtype=F32, add=None):
    if mode == "nn":
        (m, k), (k2, n) = a.shape, b.shape
    elif mode == "nt":
        (m, k), (n, k2) = a.shape, b.shape
    else:
        (k, m), (k2, n) = a.shape, b.shape
    assert k == k2, (a.shape, b.shape, mode)
    tm, tn, tk, need = _mm_tiles(m, n, k, a.dtype.itemsize, b.dtype.itemsize, jnp.dtype(out_dtype).itemsize,
                                 add is not None)
    nk = k // tk
    dims = {"nn": (((1,), (0,)), ((), ())), "nt": (((1,), (1,)), ((), ())), "tn": (((0,), (0,)), ((), ()))}[mode]

    def body(*refs):
        a_ref, b_ref = refs[:2]
        add_ref = refs[2] if add is not None else None
        o_ref = refs[3] if add is not None else refs[2]
        part = lax.dot_general(a_ref[...].astype(BF16), b_ref[...].astype(BF16), dims, preferred_element_type=F32)

        def finish(r):
            if add_ref is not None:
                r = r + add_ref[...]
            o_ref[...] = r.astype(o_ref.dtype)

        if nk == 1:
            finish(part)
            return
        acc = refs[-1]
        kk = pl.program_id(2)

        @pl.when(kk == 0)
        def _():
            acc[...] = part

        @pl.when(kk > 0)
        def _():
            acc[...] += part

        @pl.when(kk == nk - 1)
        def _():
            finish(acc[...])

    if mode == "nn":
        a_spec = pl.BlockSpec((tm, tk), lambda i, j, kk: (i, kk))
        b_spec = pl.BlockSpec((tk, tn), lambda i, j, kk: (kk, j))
    elif mode == "nt":
        a_spec = pl.BlockSpec((tm, tk), lambda i, j, kk: (i, kk))
        b_spec = pl.BlockSpec((tn, tk), lambda i, j, kk: (j, kk))
    else:
        a_spec = pl.BlockSpec((tk, tm), lambda i, j, kk: (kk, i))
        b_spec = pl.BlockSpec((tk, tn), lambda i, j, kk: (kk, j))
    o_spec = pl.BlockSpec((tm, tn), lambda i, j, kk: (i, j))
    in_specs = [a_spec, b_spec] + ([o_spec] if add is not None else [])
    args = (a, b) + ((add,) if add is not None else ())
    return pl.pallas_call(
        body, name=name, grid=(m // tm, n // tn, nk),
        in_specs=in_specs, out_specs=o_spec,
        out_shape=jax.ShapeDtypeStruct((m, n), out_dtype),
        scratch_shapes=[pltpu.VMEM((tm, tn), F32)] if nk > 1 else [],
        compiler_params=_cparams(("parallel", "parallel", "arbitrary"), min(VMEM_BIG, need + VMEM_SLACK)),
    )(*args)


TOK = 256


def _rms(x, g):
    return x * lax.rsqrt(jnp.mean(x * x, axis=-1, keepdims=True) + NORM_EPS) * g


def _rms_fwd(x, g, name):
    t, d = x.shape

    def body(x_ref, g_ref, o_ref):
        o_ref[...] = _rms(x_ref[...], g_ref[...]).astype(BF16)

    row = pl.BlockSpec((TOK, d), lambda i: (i, 0))
    return pl.pallas_call(body, name=name, grid=(t // TOK,), in_specs=[row, _full((1, d))], out_specs=row,
                          out_shape=jax.ShapeDtypeStruct((t, d), BF16),
                          compiler_params=_cparams(("parallel",)))(x, g)


def _rms_bwd(x, g, dxn, res, name):
    t, d = x.shape

    def body(x_ref, g_ref, d_ref, res_ref, dx_ref, dg_ref):
        _, vjp = jax.vjp(_rms, x_ref[...], g_ref[...])
        dx, dg = vjp(d_ref[...].astype(F32))
        dx_ref[...] = dx + res_ref[...]

        @pl.when(pl.program_id(0) == 0)
        def _():
            dg_ref[...] = jnp.zeros_like(dg_ref)

        dg_ref[...] += dg

    row = pl.BlockSpec((TOK, d), lambda i: (i, 0))
    return pl.pallas_call(body, name=name, grid=(t // TOK,), in_specs=[row, _full((1, d)), row, row],
                          out_specs=[row, _full((1, d))],
                          out_shape=[jax.ShapeDtypeStruct((t, d), F32), jax.ShapeDtypeStruct((1, d), F32)],
                          compiler_params=_cparams(("arbitrary",)))(x, g, dxn, res)


def _loss_head(x, g, tgt):
    t, d = x.shape

    def body(x_ref, g_ref, t_ref, l_ref, dx_ref, dg_ref):
        tg = t_ref[...]

        def fn(xv, gv):
            err = _rms(xv, gv) - tg
            per_tok = jnp.mean(err * err, axis=-1, keepdims=True)
            return 0.5 * jnp.sum(per_tok, axis=0, keepdims=True)

        l, vjp = jax.vjp(fn, x_ref[...], g_ref[...])
        dx, dg = vjp(jnp.ones((1, 1), F32))
        dx_ref[...] = dx

        @pl.when(pl.program_id(0) == 0)
        def _():
            dg_ref[...] = jnp.zeros_like(dg_ref)
            l_ref[...] = jnp.zeros_like(l_ref)

        dg_ref[...] += dg
        l_ref[...] += jnp.broadcast_to(l, l_ref.shape)

    row = pl.BlockSpec((TOK, d), lambda i: (i, 0))
    return pl.pallas_call(body, name="loss_head", grid=(t // TOK,), in_specs=[row, _full((1, d)), row],
                          out_specs=[_full((1, LANES)), row, _full((1, d))],
                          out_shape=[jax.ShapeDtypeStruct((1, LANES), F32), jax.ShapeDtypeStruct((t, d), F32),
                                     jax.ShapeDtypeStruct((1, d), F32)],
                          compiler_params=_cparams(("arbitrary",)))(x, g, tgt)


def _glu_fwd(x, z):
    t, d = x.shape

    def body(x_ref, v_ref, g_ref, o_ref):
        o_ref[...] = x_ref[...] + v_ref[...] * jax.nn.sigmoid(g_ref[...])

    row = pl.BlockSpec((TOK, d), lambda i: (i, 0))
    gate = pl.BlockSpec((TOK, d), lambda i: (i, 1))
    return pl.pallas_call(body, name="glu_fwd", grid=(t // TOK,), in_specs=[row, row, gate], out_specs=row,
                          out_shape=jax.ShapeDtypeStruct((t, d), F32),
                          compiler_params=_cparams(("parallel",)))(x, z, z)


def _glu_bwd(z, g):
    t, d = g.shape

    def body(v_ref, g_ref, d_ref, o_ref):
        s = jax.nn.sigmoid(g_ref[...])
        dy = d_ref[...]
        o_ref[:, :d] = (dy * s).astype(BF16)
        o_ref[:, d:] = (dy * v_ref[...] * s * (1.0 - s)).astype(BF16)

    row = pl.BlockSpec((TOK, d), lambda i: (i, 0))
    gate = pl.BlockSpec((TOK, d), lambda i: (i, 1))
    return pl.pallas_call(body, name="glu_bwd", grid=(t // TOK,), in_specs=[row, gate, row],
                          out_specs=pl.BlockSpec((TOK, 2 * d), lambda i: (i, 0)),
                          out_shape=jax.ShapeDtypeStruct((t, 2 * d), BF16),
                          compiler_params=_cparams(("parallel",)))(z, z, g)


def _shift_down(x, d):
    row = lax.broadcasted_iota(jnp.int32, x.shape, 0)
    return jnp.where(row < d, 0.0, pltpu.roll(x, d, 0))


def _shift_up(x, d):
    n = x.shape[0]
    row = lax.broadcasted_iota(jnp.int32, x.shape, 0)
    return jnp.where(row >= n - d, 0.0, pltpu.roll(x, n - d, 0))


def _make_sd():
    @functools.partial(jax.custom_vjp, nondiff_argnums=(1,))
    def sd(x, d):
        return _shift_down(x, d)

    def fwd(x, d):
        return _shift_down(x, d), None

    def bwd(d, _, g):
        return (_shift_up(g, d),)

    sd.defvjp(fwd, bwd)
    return sd


def _lin_scan(a, u, reverse=False):
    n = a.shape[0]
    row = lax.broadcasted_iota(jnp.int32, a.shape, 0)
    d = 1
    while d < n:
        if reverse:
            keep = row < n - d
            a_s, u_s = pltpu.roll(a, n - d, 0), pltpu.roll(u, n - d, 0)
        else:
            keep = row >= d
            a_s, u_s = pltpu.roll(a, d, 0), pltpu.roll(u, d, 0)
        u = u + a * jnp.where(keep, u_s, 0.0)
        a = a * jnp.where(keep, a_s, 1.0)
        d *= 2
    return u


def _make_scan():
    @jax.custom_vjp
    def scan(a, u):
        return _lin_scan(a, u)

    def fwd(a, u):
        h = _lin_scan(a, u)
        return h, (a, h)

    def bwd(res, dh):
        a, h = res
        g = _lin_scan(_shift_up(a, 1), dh, reverse=True)
        return g * _shift_down(h, 1), g

    scan.defvjp(fwd, bwd)
    return scan


def _acc_out(ref, val):
    @pl.when(pl.program_id(0) == 0)
    def _():
        ref[...] = jnp.zeros_like(ref)

    ref[...] += val


FFN_CW = 128


def _ffn_fn(hg, hv, wg, wv, bg, bv, sd):
    cg = wg[0:1] * sd(hg, 2) + wg[1:2] * sd(hg, 1) + wg[2:3] * hg + bg
    cv = wv[0:1] * sd(hv, 2) + wv[1:2] * sd(hv, 1) + wv[2:3] * hv + bv
    return jax.nn.silu(cg) * cv


def _ffn_specs(t):
    nb = D_FF // FFN_CW
    col = lambda r, off: pl.BlockSpec((r, FFN_CW), lambda j: (0, j + off))
    return nb, [col(t, 0), col(t, nb), col(3, 0), col(3, nb), col(1, 0), col(1, nb)], col


def _ffn_mid_fwd(h, cw, cb, name):
    t = h.shape[0]
    nb, in_specs, col = _ffn_specs(t)

    def body(hg, hv, wg, wv, bg, bv, o_ref):
        o_ref[...] = _ffn_fn(hg[...], hv[...], wg[...], wv[...], bg[...], bv[...], _shift_down).astype(BF16)

    return pl.pallas_call(body, name=name, grid=(nb,), in_specs=in_specs, out_specs=col(t, 0),
                          out_shape=jax.ShapeDtypeStruct((t, D_FF), BF16),
                          compiler_params=_cparams(("parallel",), VMEM_MID))(h, h, cw, cw, cb, cb)


def _ffn_mid_bwd(h, cw, cb, dact, name):
    t = h.shape[0]
    nb, in_specs, col = _ffn_specs(t)

    def body(hg, hv, wg, wv, bg, bv, d_ref, dhg, dhv, dwg, dwv, dbg, dbv):
        fn = functools.partial(_ffn_fn, sd=_make_sd())
        _, vjp = jax.vjp(fn, hg[...], hv[...], wg[...], wv[...], bg[...], bv[...])
        g = vjp(d_ref[...])
        dhg[...] = g[0].astype(BF16)
        dhv[...] = g[1].astype(BF16)
        dwg[...], dwv[...], dbg[...], dbv[...] = g[2], g[3], g[4], g[5]

    big = jax.ShapeDtypeStruct((t, D_FF), BF16)
    w3 = jax.ShapeDtypeStruct((3, D_FF), F32)
    b1 = jax.ShapeDtypeStruct((1, D_FF), F32)
    return pl.pallas_call(body, name=name, grid=(nb,), in_specs=in_specs + [col(t, 0)],
                          out_specs=[col(t, 0), col(t, 0), col(3, 0), col(3, 0), col(1, 0), col(1, 0)],
                          out_shape=[big, big, w3, w3, b1, b1],
                          compiler_params=_cparams(("parallel",), VMEM_BIG))(h, h, cw, cw, cb, cb, dact)


TS_CW = 256


def _tshift_fn(p, mu, sd):
    return p + mu * (sd(p, 1) - p)


def _tshift_fwd(p, mu):
    t = p.shape[0]
    col = lambda r: pl.BlockSpec((r, TS_CW), lambda j: (0, j))

    def body(p_ref, mu_ref, o_ref):
        o_ref[...] = _tshift_fn(p_ref[...], mu_ref[...], _shift_down)

    return pl.pallas_call(body, name="tshift_fwd", grid=(SHIFT_COLS // TS_CW,), in_specs=[col(t), col(1)],
                          out_specs=col(t), out_shape=jax.ShapeDtypeStruct((t, SHIFT_COLS), F32),
                          compiler_params=_cparams(("parallel",), VMEM_MID))(p, mu)


def _tshift_bwd(p, mu, dpam):
    t = p.shape[0]
    col = lambda r: pl.BlockSpec((r, TS_CW), lambda j: (0, j))

    def body(p_ref, mu_ref, d_ref, dp_ref, dmu_ref):
        _, vjp = jax.vjp(functools.partial(_tshift_fn, sd=_make_sd()), p_ref[...], mu_ref[...])
        dp, dmu = vjp(d_ref[...])
        dp_ref[...] = dp.astype(BF16)
        dmu_ref[...] = dmu

    return pl.pallas_call(body, name="tshift_bwd", grid=(SHIFT_COLS // TS_CW,), in_specs=[col(t), col(1), col(t)],
                          out_specs=[col(t), col(1)],
                          out_shape=[jax.ShapeDtypeStruct((t, SHIFT_COLS), BF16),
                                     jax.ShapeDtypeStruct((1, SHIFT_COLS), F32)],
                          compiler_params=_cparams(("parallel",), VMEM_MID))(p, mu, dpam)


_HI = lax.Precision.HIGHEST
_O = (0, RW, 2 * RW, 3 * RW, 3 * RW + W_LORA, 3 * RW + W_LORA + A_LORA, SHIFT_COLS)


def _dot16(a, b, dims=(((1,), (0,)), ((), ()))):
    return lax.dot_general(a.astype(BF16), b.astype(BF16), dims, preferred_element_type=F32)


def _make_dot16():
    @jax.custom_vjp
    def dot(a, b):
        return _dot16(a, b)

    def fwd(a, b):
        return _dot16(a, b), (a, b)

    def bwd(res, g):
        a, b = res
        return _dot16(g, b, (((1,), (1,)), ((), ()))), _dot16(a, g, (((0,), (0,)), ((), ())))

    dot.defvjp(fwd, bwd)
    return dot


def _seg(x, gm):
    return jnp.dot(x, gm, precision=_HI)


def _prep_fn(r, k, v, wd, ad, gd, w0, w2, a0, a2, g2, k_k, k_a, gm, dot):
    w_log = -jax.nn.softplus(-(w0 + dot(jnp.tanh(wd), w2))) - 0.5
    decay = jnp.exp(-jnp.exp(w_log))
    a = jax.nn.sigmoid(a0 + dot(ad, a2))
    g = dot(jax.nn.sigmoid(gd), g2)
    kk = k * k_k
    kk = kk / jnp.maximum(jnp.sqrt(_seg(kk * kk, gm)), 1e-12)
    k2 = k * (1.0 + (a - 1.0) * k_a)
    return r, decay, k2, v, -kk, kk * a, g


_PREP_W = ("w0", "w2", "a0", "a2", "g2", "k_k", "k_a")


def _prep_wspecs(w):
    return [_full(w[n].shape) for n in _PREP_W] + [_full((RW, RW))]


def _rwkv_prep_fwd(pam, w, gm):
    t = pam.shape[0]

    def body(p_ref, *refs):
        wr, outs = refs[:8], refs[8:]
        pieces = [p_ref[:, _O[i]:_O[i + 1]] for i in range(6)]
        res = _prep_fn(*pieces, *[x[...] for x in wr], _dot16)
        for o, val in zip(outs, res):
            o[...] = val

    row = lambda c: pl.BlockSpec((TOK, c), lambda i: (i, 0))
    return pl.pallas_call(body, name="rwkv_prep_fwd", grid=(t // TOK,),
                          in_specs=[row(SHIFT_COLS)] + _prep_wspecs(w), out_specs=[row(RW)] * 7,
                          out_shape=[jax.ShapeDtypeStruct((t, RW), F32)] * 7,
                          compiler_params=_cparams(("parallel",), VMEM_MID))(pam, *[w[n] for n in _PREP_W], gm)


def _rwkv_prep_bwd(pam, w, gm, cts, more):
    t = pam.shape[0]

    def body(p_ref, *refs):
        wr, ct, ex, dp_ref, dws = refs[:8], refs[8:15], refs[15:18], refs[18], refs[19:]
        pieces = [p_ref[:, _O[i]:_O[i + 1]] for i in range(6)]
        fn = lambda *a: _prep_fn(*a, wr[7][...], _make_dot16())
        _, vjp = jax.vjp(fn, *pieces, *[x[...] for x in wr[:7]])
        c = [x[...] for x in ct]
        c[0] = c[0] + ex[0][...]
        c[2] = c[2] + ex[1][...]
        c[3] = c[3] + ex[2][...]
        g = vjp(tuple(c))
        for i in range(6):
            dp_ref[:, _O[i]:_O[i + 1]] = g[i]
        for o, val in zip(dws, g[6:]):
            _acc_out(o, val)

    row = lambda c: pl.BlockSpec((TOK, c), lambda i: (i, 0))
    return pl.pallas_call(body, name="rwkv_prep_bwd", grid=(t // TOK,),
                          in_specs=[row(SHIFT_COLS)] + _prep_wspecs(w) + [row(RW)] * 10,
                          out_specs=[row(SHIFT_COLS)] + [_full(w[n].shape) for n in _PREP_W],
                          out_shape=[jax.ShapeDtypeStruct((t, SHIFT_COLS), F32)]
                          + [jax.ShapeDtypeStruct(w[n].shape, F32) for n in _PREP_W],
                          compiler_params=_cparams(("arbitrary",), VMEM_MID))(
                              pam, *[w[n] for n in _PREP_W], gm, *cts, *more)


def _post_fn(y, r, k2, v, g, ln_w, ln_b, r_k, gm):
    inv = 1.0 / HEAD
    d = y - _seg(y, gm) * inv
    yn = d * lax.rsqrt(_seg(d * d, gm) * inv + GN_EPS) * ln_w + ln_b
    bonus = _seg(r * k2 * r_k, gm) * v
    return (yn + bonus) * g


def _rwkv_post_fwd(y, r, k2, v, g, ln_w, ln_b, r_k, gm):
    t = y.shape[0]

    def body(*refs):
        o_ref = refs[-1]
        o_ref[...] = _post_fn(*[x[...] for x in refs[:-1]]).astype(BF16)

    row = pl.BlockSpec((TOK, RW), lambda i: (i, 0))
    return pl.pallas_call(body, name="rwkv_post_fwd", grid=(t // TOK,),
                          in_specs=[row] * 5 + [_full((1, RW))] * 3 + [_full((RW, RW))], out_specs=row,
                          out_shape=jax.ShapeDtypeStruct((t, RW), BF16),
                          compiler_params=_cparams(("parallel",), VMEM_MID))(y, r, k2, v, g, ln_w, ln_b, r_k, gm)


def _rwkv_post_bwd(y, r, k2, v, g, ln_w, ln_b, r_k, gm, dya):
    t = y.shape[0]

    def body(*refs):
        ins, gm_ref, d_ref, outs = refs[:8], refs[8], refs[9], refs[10:]
        fn = lambda *a: _post_fn(*a, gm_ref[...])
        _, vjp = jax.vjp(fn, *[x[...] for x in ins])
        gr = vjp(d_ref[...])
        for o, val in zip(outs[:5], gr[:5]):
            o[...] = val
        for o, val in zip(outs[5:], gr[5:]):
            _acc_out(o, val)

    row = pl.BlockSpec((TOK, RW), lambda i: (i, 0))
    vec = _full((1, RW))
    return pl.pallas_call(body, name="rwkv_post_bwd", grid=(t // TOK,),
                          in_specs=[row] * 5 + [vec] * 3 + [_full((RW, RW)), row],
                          out_specs=[row] * 5 + [vec] * 3,
                          out_shape=[jax.ShapeDtypeStruct((t, RW), F32)] * 5 + [jax.ShapeDtypeStruct((1, RW), F32)] * 3,
                          compiler_params=_cparams(("arbitrary",), VMEM_MID))(y, r, k2, v, g, ln_w, ln_b, r_k, gm, dya)


def _from_pt(x):
    n = x.shape[0]
    return x.reshape(n, HEAD, N_HEADS, PT).transpose(0, 3, 2, 1).reshape(n * PT, N_HEADS * HEAD)


def _lane_sum(x):
    return jnp.sum(x, axis=-1, keepdims=True)


def _pair_consts():
    lane = lax.broadcasted_iota(jnp.int32, (HEAD, LANES), 1)
    return lane, lane < HEAD


def _seg_sum_pair(x, first):
    return jnp.where(first, _lane_sum(jnp.where(first, x, 0.0)), _lane_sum(jnp.where(first, 0.0, x)))


def _to_pt(x):
    t = x.shape[0]
    return x.reshape(t // PT, PT, N_HEADS, HEAD).transpose(0, 3, 2, 1).reshape(t // PT, HEAD, N_HEADS * PT)


def _expand_cols(x, name):
    t = x.shape[0]
    tiles = WKV_CHUNK // PT

    def body(x_ref, o_ref):
        _, first = _pair_consts()
        for tl in range(tiles):
            tile = x_ref[tl]
            for j in range(PT):
                for p in range(N_HEADS // 2):
                    src = jnp.where(first, (2 * p) * PT + j, (2 * p + 1) * PT + j)
                    o_ref[tl * PT + j, :, p * LANES:(p + 1) * LANES] = jnp.take_along_axis(tile, src, axis=1)

    return pl.pallas_call(
        body, name=name, grid=(t // WKV_CHUNK,),
        in_specs=[pl.BlockSpec((tiles, HEAD, LANES), lambda i: (i, 0, 0))],
        out_specs=pl.BlockSpec((WKV_CHUNK, HEAD, RW), lambda i: (i, 0, 0)),
        out_shape=jax.ShapeDtypeStruct((t, HEAD, RW), F32),
        compiler_params=_cparams(("parallel",), VMEM_MID))(_to_pt(x))


def _wkv_fwd(w, k, z, b, v_exp):
    t = w.shape[0]
    nc = t // WKV_CHUNK
    pairs = N_HEADS // 2

    def body(w_ref, k_ref, z_ref, b_ref, v_ref, s_all, s_ref):
        @pl.when(pl.program_id(0) == 0)
        def _():
            s_ref[...] = jnp.zeros_like(s_ref)

        _, first = _pair_consts()

        def group(gi, carry):
            base = pl.multiple_of(gi * 8, 8)
            rows = [ref[pl.ds(base, 8), :] for ref in (w_ref, k_ref, z_ref, b_ref)]
            s = [s_ref[:, p * LANES:(p + 1) * LANES] for p in range(pairs)]
            for jj in range(8):
                for p in range(pairs):
                    cs = slice(p * LANES, (p + 1) * LANES)
                    wr, kr, zr, br = [x[jj:jj + 1, cs] for x in rows]
                    s_all[base + jj, :, cs] = s[p]
                    sa = _seg_sum_pair(s[p] * zr, first)
                    s[p] = s[p] * wr + sa * br + v_ref[base + jj, :, cs] * kr
            for p in range(pairs):
                s_ref[:, p * LANES:(p + 1) * LANES] = s[p]
            return carry

        lax.fori_loop(0, WKV_CHUNK // 8, group, 0)

    row = pl.BlockSpec((WKV_CHUNK, RW), lambda i: (i, 0))
    big = pl.BlockSpec((WKV_CHUNK, HEAD, RW), lambda i: (i, 0, 0))
    return pl.pallas_call(
        body, name="wkv_fwd", grid=(nc,), in_specs=[row] * 4 + [big], out_specs=[big, _full((HEAD, RW))],
        out_shape=[jax.ShapeDtypeStruct((t, HEAD, RW), F32), jax.ShapeDtypeStruct((HEAD, RW), F32)],
        compiler_params=_cparams(("arbitrary",), VMEM_MID))(w, k, z, b, v_exp)


def _wkv_out(r, s_all, s_last):
    t = r.shape[0]
    nc = t // WKV_CHUNK
    tiles = WKV_CHUNK // PT
    pairs = N_HEADS // 2

    def body(r_ref, s_ref, nxt_ref, last_ref, y_ref):
        lane, first = _pair_consts()
        after = jnp.where(pl.program_id(0) == nc - 1, last_ref[...], nxt_ref[0])
        for tl in range(tiles):
            ytile = jnp.zeros((HEAD, LANES), F32)
            for g in range(PT // 8):
                rows = r_ref[tl * PT + g * 8:tl * PT + g * 8 + 8, :]
                for jj in range(8):
                    tt = tl * PT + g * 8 + jj
                    j = g * 8 + jj
                    for p in range(pairs):
                        cs = slice(p * LANES, (p + 1) * LANES)
                        s = s_ref[tt + 1, :, cs] if tt + 1 < WKV_CHUNK else after[:, cs]
                        pr = s * rows[jj:jj + 1, cs]
                        y0 = _lane_sum(jnp.where(first, pr, 0.0))
                        y1 = _lane_sum(jnp.where(first, 0.0, pr))
                        ytile = jnp.where(lane == (2 * p) * PT + j, y0, ytile)
                        ytile = jnp.where(lane == (2 * p + 1) * PT + j, y1, ytile)
            y_ref[tl] = ytile

    row = pl.BlockSpec((WKV_CHUNK, RW), lambda i: (i, 0))
    pt = pl.BlockSpec((tiles, HEAD, LANES), lambda i: (i, 0, 0))
    big = pl.BlockSpec((WKV_CHUNK, HEAD, RW), lambda i: (i, 0, 0))
    nxt = pl.BlockSpec((1, HEAD, RW), lambda i: (jnp.minimum((i + 1) * WKV_CHUNK, t - 1), 0, 0))
    return pl.pallas_call(
        body, name="wkv_out", grid=(nc,), in_specs=[row, big, nxt, _full((HEAD, RW))], out_specs=pt,
        out_shape=jax.ShapeDtypeStruct((t // PT, HEAD, LANES), F32),
        compiler_params=_cparams(("parallel",), VMEM_MID))(r, s_all, s_all, s_last)


def _wkv_bwd(r, w, k, z, b, v_exp, s_all, dy_exp):
    t = r.shape[0]
    nc = t // WKV_CHUNK
    tiles = WKV_CHUNK // PT
    pairs = N_HEADS // 2

    def body(r_ref, w_ref, k_ref, z_ref, b_ref, v_ref, s_all_ref, dy_ref,
             dr_ref, dw_ref, dk_ref, dz_ref, db_ref, dv_ref, ds_ref):
        @pl.when(pl.program_id(0) == 0)
        def _():
            ds_ref[...] = jnp.zeros_like(ds_ref)

        lane, first = _pair_consts()
        col_sum = lambda x: jnp.sum(x, axis=0, keepdims=True)
        row8 = lax.broadcasted_iota(jnp.int32, (8, LANES), 0)
        for tl in reversed(range(tiles)):
            def group(gg, dvtile):
                gi = PT // 8 - 1 - gg
                base = pl.multiple_of(tl * PT + gi * 8, 8)
                rows = [ref[pl.ds(base, 8), :] for ref in (r_ref, w_ref, k_ref, z_ref, b_ref)]
                outs = (dr_ref, dw_ref, dk_ref, dz_ref, db_ref)
                tiles8 = {(id(o), p): jnp.zeros((8, LANES), F32) for o in outs for p in range(pairs)}
                ds = [ds_ref[:, p * LANES:(p + 1) * LANES] for p in range(pairs)]
                for jj in reversed(range(8)):
                    j = gi * 8 + jj
                    for p in range(pairs):
                        cs = slice(p * LANES, (p + 1) * LANES)

                        def put(ref, val, p=p, jj=jj):
                            tiles8[(id(ref), p)] = jnp.where(row8 == jj, val, tiles8[(id(ref), p)])

                        rr, wr, kr, zr, br = [x[jj:jj + 1, cs] for x in rows]
                        sp = s_all_ref[base + jj, :, cs]
                        vc = v_ref[base + jj, :, cs]
                        dyc = dy_ref[base + jj, :, cs]
                        sa = _seg_sum_pair(sp * zr, first)
                        st = sp * wr + sa * br + vc * kr
                        d = ds[p] + dyc * rr
                        put(dr_ref, col_sum(st * dyc))
                        dvk = d * kr
                        dv0 = _lane_sum(jnp.where(first, dvk, 0.0))
                        dv1 = _lane_sum(jnp.where(first, 0.0, dvk))
                        dvtile = jnp.where(lane == (2 * p) * PT + j, dv0, dvtile)
                        dvtile = jnp.where(lane == (2 * p + 1) * PT + j, dv1, dvtile)
                        put(dk_ref, col_sum(d * vc))
                        put(dw_ref, col_sum(sp * d))
                        u = _seg_sum_pair(d * br, first)
                        put(dz_ref, col_sum(sp * u))
                        put(db_ref, col_sum(d * sa))
                        ds[p] = d * wr + u * zr
                for p in range(pairs):
                    ds_ref[:, p * LANES:(p + 1) * LANES] = ds[p]
                for o in outs:
                    for p in range(pairs):
                        o[pl.ds(base, 8), p * LANES:(p + 1) * LANES] = tiles8[(id(o), p)]
                return dvtile

            dv_ref[tl] = lax.fori_loop(0, PT // 8, group, jnp.zeros((HEAD, LANES), F32))

    rev = lambda i: nc - 1 - i
    row = pl.BlockSpec((WKV_CHUNK, RW), lambda i: (rev(i), 0))
    pt = pl.BlockSpec((tiles, HEAD, LANES), lambda i: (rev(i), 0, 0))
    big = pl.BlockSpec((WKV_CHUNK, HEAD, RW), lambda i: (rev(i), 0, 0))
    return pl.pallas_call(
        body, name="wkv_bwd", grid=(nc,), in_specs=[row] * 5 + [big, big, big], out_specs=[row] * 5 + [pt],
        out_shape=[jax.ShapeDtypeStruct((t, RW), F32)] * 5 + [jax.ShapeDtypeStruct((t // PT, HEAD, LANES), F32)],
        scratch_shapes=[pltpu.VMEM((HEAD, RW), F32)],
        compiler_params=_cparams(("arbitrary",), VMEM_BIG))(r, w, k, z, b, v_exp, s_all, dy_exp)


LRU_CW = 128
_BX0 = SHIFT_COLS // LRU_CW
_BG0 = (SHIFT_COLS + LRU_W) // LRU_CW


def _lru_fn(bx, bg, cw, cb, ga, ba, gx, bxb, lam, sd, scan, dot):
    xc = cw[0:1] * sd(bx, 3) + cw[1:2] * sd(bx, 2) + cw[2:3] * sd(bx, 1) + cw[3:4] * bx + cb
    gr = jax.nn.sigmoid(dot(xc, ga) + ba)
    gi = jax.nn.sigmoid(dot(xc, gx) + bxb)
    log_a = -LRU_C * gr * jax.nn.softplus(-lam)
    a = jnp.exp(log_a)
    mult = jnp.sqrt(-jnp.tanh(log_a) * (jnp.exp(2.0 * log_a) + 1.0))
    return scan(a, xc * gi * mult) * jax.nn.gelu(bg)


def _lru_specs(t):
    col = lambda r, off=0: pl.BlockSpec((r, LRU_CW), lambda j: (0, j + off))
    diag = pl.BlockSpec((LRU_CW, LRU_CW), lambda j: (j, j))
    return col, [col(t, _BX0), col(t, _BG0), col(4), col(1), diag, col(1), diag, col(1), col(1)]


def _lru_fwd(p, cw, cb, ga, ba, gx, bxb, lam):
    t = p.shape[0]
    col, in_specs = _lru_specs(t)

    def body(*refs):
        o_ref = refs[-1]
        o_ref[...] = _lru_fn(*[x[...] for x in refs[:-1]], _shift_down, _lin_scan, _dot16).astype(BF16)

    return pl.pallas_call(body, name="lru_fwd", grid=(LRU_W // LRU_CW,), in_specs=in_specs, out_specs=col(t),
                          out_shape=jax.ShapeDtypeStruct((t, LRU_W), BF16),
                          compiler_params=_cparams(("parallel",), VMEM_MID))(p, p, cw, cb, ga, ba, gx, bxb, lam)


def _lru_bwd(p, cw, cb, ga, ba, gx, bxb, lam, dyb):
    t = p.shape[0]
    col, in_specs = _lru_specs(t)

    def body(*refs):
        ins, d_ref, outs = refs[:9], refs[9], refs[10:]
        fn = functools.partial(_lru_fn, sd=_make_sd(), scan=_make_scan(), dot=_make_dot16())
        _, vjp = jax.vjp(fn, *[x[...] for x in ins])
        g = vjp(d_ref[...])
        outs[0][...] = g[0].astype(BF16)
        outs[1][...] = g[1].astype(BF16)
        for o, val in zip(outs[2:], g[2:]):
            o[...] = val

    sq = pl.BlockSpec((LRU_CW, LRU_CW), lambda j: (j, 0))
    act = jax.ShapeDtypeStruct((t, LRU_W), BF16)
    vec = jax.ShapeDtypeStruct((1, LRU_W), F32)
    sqs = jax.ShapeDtypeStruct((LRU_W, LRU_CW), F32)
    return pl.pallas_call(body, name="lru_bwd", grid=(LRU_W // LRU_CW,), in_specs=in_specs + [col(t, RW // LRU_CW)],
                          out_specs=[col(t), col(t), col(4), col(1), sq, col(1), sq, col(1), col(1)],
                          out_shape=[act, act, jax.ShapeDtypeStruct((4, LRU_W), F32), vec, sqs, vec, sqs, vec, vec],
                          compiler_params=_cparams(("parallel",), VMEM_BIG))(p, p, cw, cb, ga, ba, gx, bxb, lam, dyb)


def _s5_disc_fn(a_re, a_im, log_dt, b_re, b_im, e):
    lam_re = jnp.minimum(a_re, -1e-4)
    lam_im = a_im
    dt = jnp.exp(log_dt)
    mag = jnp.exp(lam_re * dt)
    ab_re = mag * jnp.cos(lam_im * dt)
    ab_im = mag * jnp.sin(lam_im * dt)
    den = lam_re * lam_re + lam_im * lam_im
    zr = ab_re - 1.0
    q_re = jnp.dot((zr * lam_re + ab_im * lam_im) / den, e, precision=_HI)
    q_im = jnp.dot((ab_im * lam_re - zr * lam_im) / den, e, precision=_HI)
    return ab_re, ab_im, q_re * b_re - q_im * b_im, q_re * b_im + q_im * b_re


def _s5_disc_fwd(a_re, a_im, log_dt, b_re, b_im, e):
    def body(*refs):
        res = _s5_disc_fn(*[x[...] for x in refs[:6]])
        for o, val in zip(refs[6:], res):
            o[...] = val

    small = jax.ShapeDtypeStruct(a_re.shape, F32)
    wide = jax.ShapeDtypeStruct(b_re.shape, F32)
    return pl.pallas_call(body, name="s5_disc_fwd", out_shape=[small, small, wide, wide])(
        a_re, a_im, log_dt, b_re, b_im, e)


def _s5_disc_bwd(a_re, a_im, log_dt, b_re, b_im, e, cts):
    def body(*refs):
        ins, e_ref, ct, outs = refs[:5], refs[5], refs[6:10], refs[10:]
        _, vjp = jax.vjp(lambda *a: _s5_disc_fn(*a, e_ref[...]), *[x[...] for x in ins])
        for o, val in zip(outs, vjp(tuple(c[...] for c in ct))):
            o[...] = val

    shapes = [jax.ShapeDtypeStruct(x.shape, F32) for x in (a_re, a_im, log_dt, b_re, b_im)]
    return pl.pallas_call(body, name="s5_disc_bwd", out_shape=shapes)(a_re, a_im, log_dt, b_re, b_im, e, *cts)


def _cmul(a, b):
    return a[0] * b[0] - a[1] * b[1], a[0] * b[1] + a[1] * b[0]


def _s5_scan(sr, si, ab, reverse):
    n_tiles = sr.shape[0] // 8
    width = sr.shape[1]
    row8 = lax.broadcasted_iota(jnp.int32, (8, width), 0)
    p1 = ab
    p2 = _cmul(p1, p1)
    p4 = _cmul(p2, p2)
    pw = [p1]
    for _ in range(7):
        pw.append(_cmul(pw[-1], p1))
    cr = jnp.zeros((8, width), F32)
    ci = jnp.zeros((8, width), F32)
    for j in range(8):
        e = pw[7 - j] if reverse else pw[j]
        cr = jnp.where(row8 == j, e[0], cr)
        ci = jnp.where(row8 == j, e[1], ci)

    levels = []
    for d, q in ((1, p1), (2, p2), (4, p4)):
        keep = row8 < 8 - d if reverse else row8 >= d
        levels.append((d, (jnp.where(keep, q[0], 0.0), jnp.where(keep, q[1], 0.0))))

    def tile(i, carry):
        idx = n_tiles - 1 - i if reverse else i
        base = pl.multiple_of(idx * 8, 8)
        x = (sr[pl.ds(base, 8), :], si[pl.ds(base, 8), :])
        for d, q in levels:
            amt = 8 - d if reverse else d
            m = _cmul(q, (pltpu.roll(x[0], amt, 0), pltpu.roll(x[1], amt, 0)))
            x = (x[0] + m[0], x[1] + m[1])
        m = _cmul((cr, ci), carry)
        x = (x[0] + m[0], x[1] + m[1])
        sr[pl.ds(base, 8), :] = x[0]
        si[pl.ds(base, 8), :] = x[1]
        edge = slice(0, 1) if reverse else slice(7, 8)
        return x[0][edge], x[1][edge]

    zero = jnp.zeros((1, width), F32)
    lax.fori_loop(0, n_tiles, tile, (zero, zero))


_S5_W = S5_SLAB // S5_GROUP * S5_STATE


def _s5_specs(t):
    col = lambda r: pl.BlockSpec((r, S5_SLAB), lambda j: (0, j))
    bb = pl.BlockSpec((None, S5_SLAB, _S5_W), lambda j: (j, 0, 0))
    cd = pl.BlockSpec((None, _S5_W, S5_SLAB), lambda j: (j, 0, 0))
    ab = pl.BlockSpec((None, 1, _S5_W), lambda j: (j, 0, 0))
    return col, bb, cd, ab


def _s5_fwd(u, dvec, bbr, bbi, cdr, cdi, abr, abi):
    t, width = u.shape
    col, bb, cd, ab = _s5_specs(t)

    def body(u_ref, d_ref, bbr_ref, bbi_ref, cdr_ref, cdi_ref, abr_ref, abi_ref, o_ref, sr, si):
        uv = u_ref[...]
        sr[...] = _dot16(uv, bbr_ref[...])
        si[...] = _dot16(uv, bbi_ref[...])
        _s5_scan(sr, si, (abr_ref[...], abi_ref[...]), False)
        y = _dot16(sr[...], cdr_ref[...]) - _dot16(si[...], cdi_ref[...])
        o_ref[...] = jax.nn.gelu(y + d_ref[...] * uv).astype(BF16)

    return pl.pallas_call(body, name="s5_fwd", grid=(width // S5_SLAB,),
                          in_specs=[col(t), col(1), bb, bb, cd, cd, ab, ab], out_specs=col(t),
                          out_shape=jax.ShapeDtypeStruct((t, width), BF16),
                          scratch_shapes=[pltpu.VMEM((t, _S5_W), F32)] * 2,
                          compiler_params=_cparams(("parallel",), VMEM_BIG))(u, dvec, bbr, bbi, cdr, cdi, abr, abi)


def _s5_bwd(u, dvec, bbr, bbi, cdr, cdi, abr, abi, dyact):
    t, width = u.shape
    col, bb, cd, ab = _s5_specs(t)
    ns = width // S5_SLAB
    tn = (((0,), (0,)), ((), ()))
    nt = (((1,), (1,)), ((), ()))

    def body(u_ref, d_ref, bbr_ref, bbi_ref, cdr_ref, cdi_ref, abr_ref, abi_ref, dy_ref,
             du_ref, dd_ref, dbbr_ref, dbbi_ref, dcdr_ref, dcdi_ref, dabr_ref, dabi_ref, sr, si, gr, gi):
        uv = u_ref[...]
        dv = d_ref[...]
        abv = (abr_ref[...], abi_ref[...])
        sr[...] = _dot16(uv, bbr_ref[...])
        si[...] = _dot16(uv, bbi_ref[...])
        _s5_scan(sr, si, abv, False)
        y = _dot16(sr[...], cdr_ref[...]) - _dot16(si[...], cdi_ref[...])
        _, vjp = jax.vjp(jax.nn.gelu, y + dv * uv)
        (dpre,) = vjp(dy_ref[...].astype(F32))
        dd_ref[...] = jnp.sum(dpre * uv, axis=0, keepdims=True)
        dcdr_ref[...] = _dot16(sr[...], dpre, tn)
        dcdi_ref[...] = -_dot16(si[...], dpre, tn)
        gr[...] = _dot16(dpre, cdr_ref[...], nt)
        gi[...] = -_dot16(dpre, cdi_ref[...], nt)
        _s5_scan(gr, gi, (abv[0], -abv[1]), True)

        row8 = lax.broadcasted_iota(jnp.int32, (8, _S5_W), 0)

        def tile(i, carry):
            acc_r, acc_i, last_r, last_i = carry
            base = pl.multiple_of(i * 8, 8)
            s_r, s_i = sr[pl.ds(base, 8), :], si[pl.ds(base, 8), :]
            g_r, g_i = gr[pl.ds(base, 8), :], gi[pl.ds(base, 8), :]
            p_r = jnp.where(row8 == 0, last_r, pltpu.roll(s_r, 1, 0))
            p_i = jnp.where(row8 == 0, last_i, pltpu.roll(s_i, 1, 0))
            acc_r = acc_r + jnp.sum(g_r * p_r + g_i * p_i, axis=0, keepdims=True)
            acc_i = acc_i + jnp.sum(g_i * p_r - g_r * p_i, axis=0, keepdims=True)
            return acc_r, acc_i, s_r[7:8], s_i[7:8]

        zero = jnp.zeros((1, _S5_W), F32)
        acc_r, acc_i, _, _ = lax.fori_loop(0, t // 8, tile, (zero, zero, zero, zero))
        dabr_ref[...] = acc_r
        dabi_ref[...] = acc_i
        du_ref[...] = dpre * dv + _dot16(gr[...], bbr_ref[...], nt) + _dot16(gi[...], bbi_ref[...], nt)
        dbbr_ref[...] = _dot16(uv, gr[...], tn)
        dbbi_ref[...] = _dot16(uv, gi[...], tn)

    sds = jax.ShapeDtypeStruct
    return pl.pallas_call(
        body, name="s5_bwd", grid=(ns,), in_specs=[col(t), col(1), bb, bb, cd, cd, ab, ab, col(t)],
        out_specs=[col(t), col(1), bb, bb, cd, cd, ab, ab],
        out_shape=[sds((t, width), F32), sds((1, width), F32), sds((ns, S5_SLAB, _S5_W), F32),
                   sds((ns, S5_SLAB, _S5_W), F32), sds((ns, _S5_W, S5_SLAB), F32), sds((ns, _S5_W, S5_SLAB), F32),
                   sds((ns, 1, _S5_W), F32), sds((ns, 1, _S5_W), F32)],
        scratch_shapes=[pltpu.VMEM((t, _S5_W), F32)] * 4,
        compiler_params=_cparams(("parallel",), VMEM_BIG))(u, dvec, bbr, bbi, cdr, cdi, abr, abi, dyact)


def _gate_dense(w):
    h = w.shape[0]
    return jnp.einsum("hij,hg->higj", w, jnp.eye(h, dtype=F32)).reshape(h * HEAD, h * HEAD)


def _gate_blocks(d):
    x = d.reshape(LRU_W // LRU_CW, 2, HEAD, 2, HEAD)
    return jnp.einsum("tgihj,gh->tgij", x, jnp.eye(2, dtype=F32)).reshape(LRU_W // HEAD, HEAD, HEAD)


_GPS = S5_SLAB // S5_GROUP
_NS = S5_GROUPS // _GPS


def _s5_in_dense(bb):
    x = bb.reshape(_NS, _GPS, S5_STATE, S5_GROUP)
    return jnp.einsum("sgnc,gh->sgchn", x, jnp.eye(_GPS, dtype=F32)).reshape(_NS, S5_SLAB, _S5_W)


def _s5_in_blocks(d):
    x = d.reshape(_NS, _GPS, S5_GROUP, _GPS, S5_STATE)
    return jnp.einsum("sgchn,gh->sgnc", x, jnp.eye(_GPS, dtype=F32)).reshape(S5_GROUPS, S5_STATE * S5_GROUP)


def _s5_out_dense(c):
    x = c.reshape(_NS, _GPS, S5_GROUP, S5_STATE)
    return jnp.einsum("sgcn,gh->shngc", x, jnp.eye(_GPS, dtype=F32)).reshape(_NS, _S5_W, S5_SLAB)


def _s5_out_blocks(d):
    x = d.reshape(_NS, _GPS, S5_STATE, _GPS, S5_GROUP)
    return jnp.einsum("shngc,gh->sgcn", x, jnp.eye(_GPS, dtype=F32)).reshape(S5_GROUPS, S5_GROUP, S5_STATE)


def _local_step(x, tgt, w, late_weights, send_grads):
    d_model = x.shape[1]
    gs = {}
    gm = jnp.kron(jnp.eye(N_HEADS, dtype=F32), jnp.ones((HEAD, HEAD), F32))
    n_layers = w["f_norm_g"].shape[0]

    def ffn_fwd(xin, l):
        xn = _rms_fwd(xin, w["f_norm_g"][l:l + 1], f"rms_f{l}")
        h = _matmul(xn, w["f_w_up_t"][l], "nt", f"mm_f{l}_up")
        act = _ffn_mid_fwd(h, w["f_conv_w"][l], w["f_conv_b"][l:l + 1], f"ffn_mid_fwd{l}")
        return _matmul(act, w["f_w_down"][l], "nn", f"mm_f{l}_down", add=xin), (xin, xn, h, act)

    def ffn_bwd(g, saved, l):
        xin, xn, h, act = saved
        dact = _matmul(g, w["f_w_down"][l], "nt", f"mm_f{l}_dact")
        d_down = _matmul(act, g, "tn", f"mm_f{l}_ddown", out_dtype=BF16)
        dhg, dhv, dwg, dwv, dbg, dbv = _ffn_mid_bwd(h, w["f_conv_w"][l], w["f_conv_b"][l:l + 1], dact,
                                                    f"ffn_mid_bwd{l}")
        dh = jnp.concatenate([dhg, dhv], axis=1)
        dxn = _matmul(dh, w["f_w_up_t"][l], "nn", f"mm_f{l}_dxn")
        d_up = _matmul(dh, xn, "tn", f"mm_f{l}_dup", out_dtype=BF16)
        dx, dgn = _rms_bwd(xin, w["f_norm_g"][l:l + 1], dxn, g, f"rms_f{l}_bwd")
        return dx, d_up, d_down, jnp.concatenate([dwg, dwv], axis=1), jnp.concatenate([dbg, dbv], axis=1), dgn

    xn0 = _rms_fwd(x, w["e_norm_g"], "rms_e")
    p = _matmul(xn0, w["e_w_in_t"], "nt", "mm_e_in")
    pam = _tshift_fwd(p, w["e_mu"])
    pw = dict(w0=w["e_w0"], w2=w["e_w2"][0], a0=w["e_a0"], a2=w["e_a2"][0], g2=w["e_g2"][0],
              k_k=w["e_k_k"], k_a=w["e_k_a"])
    r, dec, k2, v, z, b, gate = _rwkv_prep_fwd(pam, pw, gm)
    v_exp = _expand_cols(v, "wkv_expand_v")
    s_all, s_last = _wkv_fwd(dec, k2, z, b, v_exp)
    y_pt = _wkv_out(r, s_all, s_last)
    y = _from_pt(y_pt)
    rk = w["e_r_k"].reshape(1, RW)
    ya = _rwkv_post_fwd(y, r, k2, v, gate, w["e_ln_w"], w["e_ln_b"], rk, gm)
    ga, gx = _gate_dense(w["e_gate_a_w"][0]), _gate_dense(w["e_gate_x_w"][0])
    lru_w = (w["e_conv_w"][0], w["e_conv_b"], ga, w["e_gate_a_b"], gx, w["e_gate_x_b"], w["e_lru_lambda"])
    yb = _lru_fwd(p, *lru_w)
    ycat = jnp.concatenate([ya, yb], axis=1)
    x1 = _matmul(ycat, w["e_w_out"], "nn", "mm_e_out", add=x)
    w = {**w, **late_weights(x1)}
    x2, ffn0 = ffn_fwd(x1, 0)

    xn1 = _rms_fwd(x2, w["o_norm_g"], "rms_o")
    u = _matmul(xn1, w["o_w_in"], "nn", "mm_o_in")
    expand = jnp.kron(jnp.eye(S5_STATE, dtype=F32), jnp.ones((1, S5_GROUP), F32))
    disc_in = (w["o_A_re"][0], w["o_A_im"][0], w["o_log_dt"].reshape(S5_GROUPS, 1),
               w["o_B_re"][0].reshape(S5_GROUPS, -1), w["o_B_im"][0].reshape(S5_GROUPS, -1), expand)
    ab_re, ab_im, bb_re, bb_im = _s5_disc_fwd(*disc_in)
    s5_w = (w["o_D"], _s5_in_dense(bb_re), _s5_in_dense(bb_im), _s5_out_dense(w["o_C_re"][0]),
            _s5_out_dense(w["o_C_im"][0]), ab_re.reshape(_NS, 1, _S5_W), ab_im.reshape(_NS, 1, _S5_W))
    yact = _s5_fwd(u, *s5_w)
    zz = _matmul(yact, w["o_w_glu_t"], "nt", "mm_o_glu")
    x3 = _glu_fwd(x2, zz)
    x4, ffn1 = ffn_fwd(x3, 1)

    loss, g, gs["final_norm_g", 0] = _loss_head(x4, w["final_norm_g"].reshape(1, d_model), tgt)

    g, up1, down1, dcw1, dcb1, dfn1 = ffn_bwd(g, ffn1, 1)
    dz = _glu_bwd(zz, g)
    dyact = _matmul(dz, w["o_w_glu_t"], "nn", "mm_o_dyact")
    d_glu = _matmul(dz, yact, "tn", "mm_o_dglu", out_dtype=BF16)
    du, gs["o_D", 0], dbbr, dbbi, dcdr, dcdi, dabr, dabi = _s5_bwd(u, *s5_w, dyact)
    gs["o_C_re", 0] = _s5_out_blocks(dcdr).reshape(S5_GROUPS * S5_GROUP, S5_STATE)
    gs["o_C_im", 0] = _s5_out_blocks(dcdi).reshape(S5_GROUPS * S5_GROUP, S5_STATE)
    cts = (dabr.reshape(S5_GROUPS, S5_STATE), dabi.reshape(S5_GROUPS, S5_STATE), _s5_in_blocks(dbbr),
           _s5_in_blocks(dbbi))
    gs["o_A_re", 0], gs["o_A_im", 0], dlog_dt, gs["o_B_re", 0], gs["o_B_im", 0] = _s5_disc_bwd(*disc_in, cts)
    gs["o_log_dt", 0] = dlog_dt.reshape(1, S5_GROUPS)
    dxn = _matmul(du, w["o_w_in"], "nt", "mm_o_dxn")
    d_oin = _matmul(xn1, du, "tn", "mm_o_din", out_dtype=BF16)
    g, gs["o_norm_g", 0] = _rms_bwd(x2, w["o_norm_g"], dxn, g, "rms_o_bwd")
    g = send_grads("a", [("f_w_up", 1, up1), ("f_w_down", 1, down1), ("o_w_glu", 0, d_glu), ("o_w_in", 0, d_oin)], g)

    g, up0, down0, dcw0, dcb0, dfn0 = ffn_bwd(g, ffn0, 0)
    gs["f_conv_w", 0], gs["f_conv_w", 3] = dcw0, dcw1
    gs["f_conv_b", 0], gs["f_conv_b", 1] = dcb0, dcb1
    gs["f_norm_g", 0], gs["f_norm_g", 1] = dfn0, dfn1

    dycat = _matmul(g, w["e_w_out"], "nt", "mm_e_dycat")
    d_eout = _matmul(ycat, g, "tn", "mm_e_dout", out_dtype=BF16)
    dycat = send_grads("b", [("f_w_up", 0, up0), ("f_w_down", 0, down0), ("e_w_out", 0, d_eout)], dycat)
    dy, dr1, dk1, dv1, dgate, gs["e_ln_w", 0], gs["e_ln_b", 0], gs["e_r_k", 0] = _rwkv_post_bwd(
        y, r, k2, v, gate, w["e_ln_w"], w["e_ln_b"], rk, gm, dycat)
    dr2, ddec, dk2, dzz, dbb, dv_pt = _wkv_bwd(r, dec, k2, z, b, v_exp, s_all, _expand_cols(dy, "wkv_expand_dy"))
    (dpam, gs["e_w0", 0], gs["e_w2", 0], gs["e_a0", 0], gs["e_a2", 0], gs["e_g2", 0], gs["e_k_k", 0],
     gs["e_k_a", 0]) = _rwkv_prep_bwd(pam, pw, gm, (dr2, ddec, dk2, _from_pt(dv_pt), dzz, dbb, dgate), (dr1, dk1, dv1))
    dpa, gs["e_mu", 0] = _tshift_bwd(p, w["e_mu"], dpam)
    (dbx, dbg, gs["e_conv_w", 0], gs["e_conv_b", 0], dga, gs["e_gate_a_b", 0], dgx, gs["e_gate_x_b", 0],
     gs["e_lru_lambda", 0]) = _lru_bwd(p, *lru_w, dycat)
    gs["e_gate_a_w", 0] = _gate_blocks(dga).reshape(LRU_W, HEAD)
    gs["e_gate_x_w", 0] = _gate_blocks(dgx).reshape(LRU_W, HEAD)
    dp = jnp.concatenate([dpa, dbx, dbg], axis=1)
    dxn = _matmul(dp, w["e_w_in_t"], "nn", "mm_e_dxn")
    d_ein = _matmul(dp, xn0, "tn", "mm_e_din", out_dtype=BF16)
    grad_x, gs["e_norm_g", 0] = _rms_bwd(x, w["e_norm_g"], dxn, g, "rms_e_bwd")
    grad_x = send_grads("c", [("e_w_in", 0, d_ein)], grad_x)
    return loss, grad_x, gs


CAST_ROWS = 256


def _cast_shard(w3, layer, transpose, chip, name):
    _, rows, cols = w3.shape
    tr = _tile(rows, (CAST_ROWS, 176, 128))

    def body(c_ref, w_ref, o_ref):
        v = w_ref[...]
        o_ref[...] = (v.T if transpose else v).astype(BF16)

    in_spec = pl.BlockSpec((None, tr, cols), lambda i, c: (layer, i, 0))
    if transpose:
        out_spec, shape = pl.BlockSpec((None, cols, tr), lambda i, c: (c[0], 0, i)), (cols, rows)
    else:
        out_spec, shape = pl.BlockSpec((None, tr, cols), lambda i, c: (c[0], i, 0)), (rows, cols)
    grid_spec = pltpu.PrefetchScalarGridSpec(num_scalar_prefetch=1, grid=(rows // tr,), in_specs=[in_spec],
                                             out_specs=out_spec)
    return pl.pallas_call(body, name=name, grid_spec=grid_spec,
                          out_shape=jax.ShapeDtypeStruct((N_CHIPS,) + shape, BF16),
                          compiler_params=_cparams(("parallel",), VMEM_MID))(chip, w3)


_ANY = pl.BlockSpec(memory_space=pl.ANY)


def _coords():
    return lax.axis_index("x"), lax.axis_index("y"), lax.axis_index("c")


def _flip(v, d):
    return 1 - v if d else v


_CHIP_RELS = ((1, 0), (0, 1), (1, 1))
_DEV_RELS = tuple((dx, dy, dc) for dx in (0, 1) for dy in (0, 1) for dc in (0, 1))[1:]


_HBM = pl.BlockSpec(memory_space=pltpu.HBM)
_SEM = pl.BlockSpec(memory_space=pltpu.SEMAPHORE)
_EFFECT = pltpu.SideEffectType.DATAFLOW_SIDE_EFFECTING


def _in_hbm(a):
    return pltpu.with_memory_space_constraint(a, pltpu.HBM)


def _gather_copies(bufs, send, recv, landed):
    x, y, c = _coords()
    me = 2 * x + y
    res = []
    for i, buf in enumerate(bufs):
        for j, (dx, dy) in enumerate(_CHIP_RELS):
            px, py = _flip(x, dx), _flip(y, dy)
            k = i * len(_CHIP_RELS) + j
            res.append(pltpu.make_async_remote_copy(
                src_ref=buf.at[me], dst_ref=buf.at[2 * px + py if landed else me], send_sem=send.at[k],
                recv_sem=recv.at[k], device_id=(px, py, c), device_id_type=MESH))
    return res


def _scatter_copies(srcs, lands, send, recv, landed):
    x, y, c = _coords()
    me = 4 * x + 2 * y + c
    res = []
    for i, (src, land) in enumerate(zip(srcs, lands)):
        for j, (dx, dy, dc) in enumerate(_DEV_RELS):
            peer = (_flip(x, dx), _flip(y, dy), _flip(c, dc))
            pid = 4 * peer[0] + 2 * peer[1] + peer[2]
            k = i * len(_DEV_RELS) + j
            res.append(pltpu.make_async_remote_copy(
                src_ref=src.at[pid], dst_ref=land.at[pid if landed else me], send_sem=send.at[k],
                recv_sem=recv.at[k], device_id=peer, device_id_type=MESH))
    return res


def _split_start(bufs, n_src, copies, n_rel, name, after):
    n = len(bufs)
    nk = n_src * n_rel

    def body(*refs):
        ins, send, recv, token = refs[:n], refs[n + 1 + n], refs[n + 2 + n], refs[-1]
        for cp in copies(ins, send, recv, False):
            cp.start()
        token[...] = jnp.zeros_like(token)

    res = pl.pallas_call(
        body, name=name, in_specs=[_HBM] * n + [_ANY],
        out_specs=[_HBM] * n + [_SEM, _SEM, pl.BlockSpec(memory_space=pltpu.VMEM)],
        out_shape=[pltpu.HBM(b.shape, b.dtype) for b in bufs]
        + [pltpu.SemaphoreType.DMA((nk,)), pltpu.SemaphoreType.DMA((nk,)), jax.ShapeDtypeStruct((8, LANES), F32)],
        input_output_aliases={i: i for i in range(n)},
        compiler_params=pltpu.CompilerParams(has_side_effects=_EFFECT))(*[_in_hbm(b) for b in bufs], after)
    return res[n], res[n + 1], list(res[:n]), res[n + 2]


def _split_wait(bufs, send, recv, copies, name, after):
    n = len(bufs)

    def body(*refs):
        ins, send_ref, recv_ref = refs[:n], refs[n], refs[n + 1]
        for cp in copies(ins, send_ref, recv_ref, True):
            cp.wait_send()
            cp.wait_recv()

    return pl.pallas_call(
        body, name=name, in_specs=[_HBM] * n + [_SEM, _SEM, _ANY], out_specs=[_HBM] * n,
        out_shape=[pltpu.HBM(b.shape, b.dtype) for b in bufs], input_output_aliases={i: i for i in range(n)},
        compiler_params=pltpu.CompilerParams(has_side_effects=_EFFECT))(*bufs, send, recv, after)


def _gather_start(bufs, name, after):
    return _split_start(bufs, len(bufs), _gather_copies, len(_CHIP_RELS), name, after)


def _gather_wait(bufs, send, recv, name, after):
    return _split_wait(bufs, send, recv, _gather_copies, name, after)


def _scatter_start(srcs, name, after):
    n = len(srcs)
    lands = [lax.empty(a.shape, a.dtype) for a in srcs]
    fn = lambda refs, send, recv, landed: _scatter_copies(refs[:n], refs[n:], send, recv, landed)
    send, recv, bufs, token = _split_start(list(srcs) + lands, n, fn, len(_DEV_RELS), name, after)
    return send, recv, bufs, token


def _scatter_wait(bufs, send, recv, name, after):
    n = len(bufs) // 2
    fn = lambda refs, s, r, landed: _scatter_copies(refs[:n], refs[n:], s, r, landed)
    res = _split_wait(bufs, send, recv, fn, name, after)
    return res[:n], res[n:]


def _sum_segments(src, land, me, name):
    nd, seg, cols = src.shape
    ts = _tile(seg, (256, 176, 128))

    def body(m_ref, *refs):
        o_ref = refs[-1]
        acc = refs[0][...].astype(F32)
        for r in refs[1:-1]:
            acc = acc + r[...].astype(F32)
        o_ref[...] = acc

    def peer(rel):
        bits = 4 * rel[0] + 2 * rel[1] + rel[2]
        return pl.BlockSpec((None, ts, cols), lambda i, m: (jnp.bitwise_xor(m[0], bits), i, 0))

    grid_spec = pltpu.PrefetchScalarGridSpec(
        num_scalar_prefetch=1, grid=(seg // ts,),
        in_specs=[pl.BlockSpec((None, ts, cols), lambda i, m: (m[0], i, 0))] + [peer(r) for r in _DEV_RELS],
        out_specs=pl.BlockSpec((None, ts, cols), lambda i, m: (m[1], i, 0)))
    return pl.pallas_call(body, name=name, grid_spec=grid_spec,
                          out_shape=jax.ShapeDtypeStruct((2, seg, cols), F32),
                          compiler_params=_cparams(("parallel",), VMEM_MID))(me, src, *[land] * len(_DEV_RELS))


def _exchange_sibling(arrs):
    n = len(arrs)

    def body(*refs):
        outs, (send, recv) = refs[n:2 * n], refs[2 * n:]
        x, y, c = _coords()
        sib = (x, y, 1 - c)
        sends, recvs = [], []
        for i in range(n):
            cp = pltpu.make_async_remote_copy(src_ref=outs[i].at[c], dst_ref=outs[i].at[c], send_sem=send.at[i],
                                              recv_sem=recv.at[i], device_id=sib, device_id_type=MESH)
            cp.start()
            sends.append(cp)
            recvs.append(pltpu.make_async_remote_copy(src_ref=outs[i].at[c], dst_ref=outs[i].at[1 - c],
                                                      send_sem=send.at[i], recv_sem=recv.at[i], device_id=sib,
                                                      device_id_type=MESH))
        for cp in recvs:
            cp.wait_recv()
        for cp in sends:
            cp.wait_send()

    return pl.pallas_call(
        body, name="exchange_sibling", in_specs=[_ANY] * n, out_specs=[_ANY] * n,
        out_shape=[jax.ShapeDtypeStruct(a.shape, a.dtype) for a in arrs],
        input_output_aliases={i: i for i in range(n)},
        scratch_shapes=[pltpu.SemaphoreType.DMA((n,)), pltpu.SemaphoreType.DMA((n,))])(*arrs)


def _allreduce_small(vec):
    nd, rows, lanes = vec.shape
    nr = len(_DEV_RELS)

    def body(in_ref, out_ref, stage, red, send, recv):
        x, y, c = _coords()
        me = 4 * x + 2 * y + c
        peers = []
        for dx, dy, dc in _DEV_RELS:
            peer = (_flip(x, dx), _flip(y, dy), _flip(c, dc))
            peers.append((peer, 4 * peer[0] + 2 * peer[1] + peer[2]))

        def copy(src, dst, k, peer):
            return pltpu.make_async_remote_copy(src_ref=src, dst_ref=dst, send_sem=send.at[k], recv_sem=recv.at[k],
                                                device_id=peer, device_id_type=MESH)

        first = [copy(in_ref.at[pid], stage.at[me], j, peer) for j, (peer, pid) in enumerate(peers)]
        for cp in first:
            cp.start()
        stage[me] = in_ref[me]
        for j, (peer, pid) in enumerate(peers):
            copy(in_ref.at[pid], stage.at[pid], j, peer).wait_recv()
        acc = stage[0]
        for d in range(1, nd):
            acc = acc + stage[d]
        red[...] = acc
        out_ref[me] = acc
        second = [copy(red, out_ref.at[me], nr + j, peer) for j, (peer, pid) in enumerate(peers)]
        for cp in second:
            cp.start()
        for j, (peer, pid) in enumerate(peers):
            copy(red, out_ref.at[pid], nr + j, peer).wait_recv()
        for cp in first + second:
            cp.wait_send()

    vm = pl.BlockSpec(memory_space=pltpu.VMEM)
    return pl.pallas_call(
        body, name="allreduce_small", in_specs=[vm], out_specs=vm,
        out_shape=jax.ShapeDtypeStruct(vec.shape, F32),
        scratch_shapes=[pltpu.VMEM(vec.shape, F32), pltpu.VMEM((rows, lanes), F32),
                        pltpu.SemaphoreType.DMA((2 * nr,)), pltpu.SemaphoreType.DMA((2 * nr,))],
        compiler_params=_cparams(None, VMEM_MID))(vec)


def _adam_math(w, g, m, v):
    m2 = ADAM_B1 * m + (1.0 - ADAM_B1) * g
    v2 = ADAM_B2 * v + (1.0 - ADAM_B2) * (g * g)
    m_hat = m2 / (1.0 - ADAM_B1 ** ADAM_STEP)
    v_hat = v2 / (1.0 - ADAM_B2 ** ADAM_STEP)
    return -ADAM_LR * (m_hat / (jnp.sqrt(v_hat) + ADAM_EPS) + ADAM_WD * w), m2, v2


def _adamw_big(w3, m3, v3, layer, g, transposed, name, prev=None):
    nl, rows, cols = w3.shape
    tr = 128 if transposed else _tile(rows, (256, 176, 128))

    def body(w_ref, m_ref, v_ref, g_ref, *rest):
        go_ref, d_ref, mo_ref, vo_ref = rest[-4:]
        g_val = g_ref[...].T if transposed else g_ref[...]
        go_ref[...] = g_val
        d_ref[...], mo_ref[...], vo_ref[...] = _adam_math(w_ref[...], g_val, m_ref[...], v_ref[...])

    wspec = pl.BlockSpec((None, tr, cols), lambda i: (layer, i, 0))
    gspec = pl.BlockSpec((cols, tr), lambda i: (0, i)) if transposed else pl.BlockSpec((tr, cols), lambda i: (i, 0))
    extra = [] if prev is None else list(prev)
    return pl.pallas_call(body, name=name, grid=(rows // tr,),
                          in_specs=[wspec, wspec, wspec, gspec] + [_ANY] * len(extra),
                          out_specs=[wspec] * 4, out_shape=[jax.ShapeDtypeStruct((nl, rows, cols), F32)] * 4,
                          input_output_aliases={4 + i: i for i in range(len(extra))},
                          compiler_params=_cparams(("parallel",), VMEM_MID))(w3, m3, v3, g, *extra)


_SMALL = (
    ("e_norm_g", (1, 1024), None), ("e_mu", (1, SHIFT_COLS), None), ("e_w0", (1, RW), None),
    ("e_w2", (W_LORA, RW), 128), ("e_a0", (1, RW), None), ("e_a2", (A_LORA, RW), 128), ("e_g2", (G_LORA, RW), 128),
    ("e_k_k", (1, RW), None), ("e_k_a", (1, RW), None), ("e_r_k", (1, RW), None), ("e_ln_w", (1, RW), None),
    ("e_ln_b", (1, RW), None), ("e_conv_w", (4, LRU_W), 128), ("e_conv_b", (1, LRU_W), None),
    ("e_gate_a_w", (LRU_W, HEAD), None), ("e_gate_a_b", (1, LRU_W), None), ("e_gate_x_w", (LRU_W, HEAD), None),
    ("e_gate_x_b", (1, LRU_W), None), ("e_lru_lambda", (1, LRU_W), None), ("o_norm_g", (1, 1024), 256),
    ("o_A_re", (S5_GROUPS, S5_STATE), None), ("o_A_im", (S5_GROUPS, S5_STATE), None), ("o_log_dt", (1, S5_GROUPS), None),
    ("o_B_re", (S5_GROUPS, S5_STATE * S5_GROUP), None), ("o_B_im", (S5_GROUPS, S5_STATE * S5_GROUP), None),
    ("o_C_re", (S5_GROUPS * S5_GROUP, S5_STATE), None), ("o_C_im", (S5_GROUPS * S5_GROUP, S5_STATE), None),
    ("o_D", (1, 1024), 256), ("f_norm_g", (2, 1024), None), ("f_conv_w", (6, 2 * D_FF), 2 * D_FF // 4),
    ("f_conv_b", (2, 2 * D_FF), None), ("final_norm_g", (1, 1024), None))
_PIECES = {"f_norm_g": ((0, 1), (1, 1)), "f_conv_b": ((0, 1), (1, 1)), "f_conv_w": ((0, 3), (3, 3))}


def _ceil_to(n, m):
    return -(-n // m) * m


def _small_layout():
    groups = {}
    for name, (rows, cols), _ in _SMALL:
        for first, r in _PIECES.get(name, ((0, rows),)):
            groups.setdefault(cols, []).append((name, first, r))
    layout, off = {}, 0
    for cols, items in groups.items():
        r0, placed = 0, []
        for name, first, r in items:
            if r >= 8 or r0 % 8 + r > 8:
                r0 = _ceil_to(r0, 8)
            placed.append((name, first, r, r0))
            r0 += r
        rpad = _ceil_to(r0, 8)
        for name, first, r, at in placed:
            layout[name, first] = (off, rpad, at, r, cols)
        off += -(-cols // LANES) * rpad
    return layout, _ceil_to(off, 8 * N_DEV)


def _small_pack(gs):
    layout, total = _small_layout()
    keys = list(layout)

    def body(*refs):
        out = refs[-1]
        out[...] = jnp.zeros_like(out)
        for key, g_ref in zip(keys, refs[:-1]):
            off, rpad, at, r, cols = layout[key]
            for j in range(-(-cols // LANES)):
                cw = min(LANES, cols - j * LANES)
                out[off + j * rpad + at:off + j * rpad + at + r, 0:cw] = g_ref[:, j * LANES:j * LANES + cw]

    return pl.pallas_call(body, name="small_pack", out_shape=jax.ShapeDtypeStruct((total, LANES), F32),
                          compiler_params=_cparams(None, VMEM_MID))(*[gs[k] for k in keys])


def _adamw_small(red, chip, wts, ms, vs):
    layout, _ = _small_layout()
    names = [n for n, _, _ in _SMALL]
    n = len(names)

    def body(chip_ref, red_ref, *refs):
        ins, outs = refs[:3 * n], refs[3 * n:]
        c = chip_ref[0]
        for i, (name, (rows, cols), loc) in enumerate(_SMALL):
            w_ref, m_ref, v_ref = ins[3 * i:3 * i + 3]
            o_refs = outs[4 * i:4 * i + 4]
            width = cols if loc is None else loc
            for first, r in _PIECES.get(name, ((0, rows),)):
                off, rpad, at, _, _ = layout[name, first]
                for j in range(-(-width // LANES)):
                    cw = min(LANES, width - j * LANES)
                    if loc is None:
                        start = off + j * rpad + at
                        g = red_ref[start:start + r, 0:cw]
                    else:
                        blk = c * (loc // LANES) + j
                        if r >= 8:
                            g = red_ref[pl.ds(pl.multiple_of(off + at + blk * rpad, 8), r), 0:cw]
                        else:
                            tile = red_ref[pl.ds(pl.multiple_of(off + at // 8 * 8 + blk * rpad, 8), 8), 0:cw]
                            g = tile[at % 8:at % 8 + r]
                    rs, cs = slice(first, first + r), slice(j * LANES, j * LANES + cw)
                    d, m2, v2 = _adam_math(w_ref[rs, cs], g, m_ref[rs, cs], v_ref[rs, cs])
                    for o, val in zip(o_refs, (g, d, m2, v2)):
                        o[rs, cs] = val

    args, shapes = [], []
    for name in names:
        args += [wts[name], ms[name], vs[name]]
        shapes += [jax.ShapeDtypeStruct(wts[name].shape, F32)] * 4
    vm = pl.BlockSpec(memory_space=pltpu.VMEM)
    res = pl.pallas_call(body, name="adamw_small",
                         in_specs=[pl.BlockSpec(memory_space=pltpu.SMEM), vm] + [vm] * (3 * n),
                         out_specs=[vm] * (4 * n), out_shape=shapes,
                         compiler_params=_cparams(None, VMEM_BIG))(chip, red, *args)
    return {name: res[4 * i:4 * i + 4] for i, name in enumerate(names)}


PACK_ROWS = 8


def _packed_rows(shape):
    size = 1
    for d in shape:
        size *= d
    return -(-size // (PACK_ROWS * LANES)) * PACK_ROWS


def _pack(arrs, row_mult):
    parts = []
    for a in arrs:
        flat = a.reshape(-1).astype(F32)
        rows = _packed_rows(a.shape)
        parts.append(jnp.pad(flat, (0, rows * LANES - flat.shape[0])).reshape(rows, LANES))
    total = sum(p.shape[0] for p in parts)
    fill = -(-total // row_mult) * row_mult - total
    if fill:
        parts.append(jnp.zeros((fill, LANES), F32))
    return jnp.concatenate(parts, axis=0)


def _unpack(packed, shapes):
    out, off = [], 0
    for s in shapes:
        rows = _packed_rows(s)
        size = 1
        for d in s:
            size *= d
        out.append(packed[off:off + rows].reshape(-1)[:size].reshape(s))
        off += rows
    return out


_SMALL_SH = ("e_w2", "e_a2", "e_g2", "e_conv_w", "o_norm_g", "o_D", "f_conv_w")
_LARGE = (("e_w_in", True), ("e_w_out", False), ("o_w_in", False), ("o_w_glu", True), ("f_w_up", True),
        ("f_w_down", False))
_ORDER = ("e_norm_g", "e_w_in", "e_mu", "e_w0", "e_w2", "e_a0", "e_a2", "e_g2", "e_k_k", "e_k_a", "e_r_k", "e_ln_w",
          "e_ln_b", "e_conv_w", "e_conv_b", "e_gate_a_w", "e_gate_a_b", "e_gate_x_w", "e_gate_x_b", "e_lru_lambda",
          "e_w_out", "o_norm_g", "o_w_in", "o_A_re", "o_A_im", "o_log_dt", "o_B_re", "o_B_im", "o_C_re", "o_C_im",
          "o_D", "o_w_glu", "f_norm_g", "f_w_up", "f_conv_w", "f_conv_b", "f_w_down", "final_norm_g")
N_CHIPS = 4
N_DEV = 8


def _step(x, tgt, wts, ms, vs):
    xi, yi, ci = _coords()
    chip = 2 * xi + yi
    chip1 = chip.astype(jnp.int32).reshape(1)
    me2 = jnp.stack([4 * xi + 2 * yi + ci, ci]).astype(jnp.int32)
    by_cols = dict(_LARGE)

    bufs = {(name, l): _cast_shard(wts[name], l, by_cols[name], chip1, f"cast_{name}{l}")
            for name, _ in _LARGE for l in range(wts[name].shape[0])}
    sh_shapes = [wts[n].shape for n in _SMALL_SH]
    packed = _pack([wts[n] for n in _SMALL_SH], 8)
    small_buf = lax.dynamic_update_slice(jnp.zeros((N_CHIPS,) + packed.shape, F32), packed[None], (chip, 0, 0))
    early = [("e_w_in", 0), ("e_w_out", 0)]
    late = [k for k in bufs if k not in early]
    send, recv, thru, token = _gather_start([bufs[k] for k in early] + [small_buf], "gather_start_a", x)
    got = _gather_wait(thru, send, recv, "gather_wait_a", token)
    send_b, recv_b, thru_b, token = _gather_start([bufs[k] for k in late], "gather_start_b", got[0])
    x, _ = lax.optimization_barrier((x, token))

    def rows(g):
        return g.reshape(N_CHIPS * g.shape[1], g.shape[2])

    full = {n: wts[n] for n, _, loc in _SMALL if loc is None}
    full["e_w_in_t"], full["e_w_out"] = rows(got[0]), rows(got[1])
    per_chip = [_unpack(got[2][k], sh_shapes) for k in range(N_CHIPS)]
    for i, n in enumerate(_SMALL_SH):
        full[n] = jnp.concatenate([per_chip[k][i] for k in range(N_CHIPS)], axis=-1)

    def late_weights(after):
        res = dict(zip(late, _gather_wait(thru_b, send_b, recv_b, "gather_wait_b", after)))
        return {"o_w_in": rows(res[("o_w_in", 0)]), "o_w_glu_t": rows(res[("o_w_glu", 0)]),
                "f_w_up_t": [rows(res[("f_w_up", l)]) for l in range(2)],
                "f_w_down": [rows(res[("f_w_down", l)]) for l in range(2)]}

    pending = []

    def send_grads(tag, items, carry):
        srcs = [g.reshape(N_DEV, g.shape[0] // N_DEV, g.shape[1]) for _, _, g in items]
        s_sem, r_sem, both, tok = _scatter_start(srcs, f"scatter_start_{tag}", carry)
        pending.append((tag, [(name, l) for name, l, _ in items], s_sem, r_sem, both))
        carry, _ = lax.optimization_barrier((carry, tok))
        return carry

    loss, grad_x, gs = _local_step(x, tgt, full, late_weights, send_grads)

    final = {}
    red = _allreduce_small(_small_pack(gs).reshape(N_DEV, -1, LANES)).reshape(-1, LANES)
    view = {name: (rows, cols if loc is None else loc) for name, (rows, cols), loc in _SMALL}
    as2d = lambda d: {name: d[name].reshape(view[name]) for name in view}
    small = _adamw_small(red, chip1, as2d(wts), as2d(ms), as2d(vs))
    for name, res in small.items():
        final[name] = [r.reshape(wts[name].shape) for r in res]
    new_v = small["final_norm_g"][3]

    halves, keys = [], []
    for tag, names, s_sem, r_sem, both in pending:
        srcs, lands = _scatter_wait(both, s_sem, r_sem, f"scatter_wait_{tag}", new_v)
        for (name, l), src, land in zip(names, srcs, lands):
            halves.append(_sum_segments(src, land, me2, f"sum_{name}{l}"))
            keys.append((name, l))
    shards = _exchange_sibling(halves)
    for s, (name, l) in zip(shards, keys):
        final[name] = _adamw_big(wts[name], ms[name], vs[name], l, s.reshape(2 * s.shape[1], s.shape[2]),
                                 by_cols[name], f"adamw_{name}{l}", prev=final.get(name))

    loss = lax.psum(loss[0, 0], ("x", "y", "c"))
    res = [loss, grad_x[None]]
    for k in range(4):
        res += [final[n][k] for n in _ORDER]
    return tuple(res)


def kernel(x, e_norm_g, e_w_in, e_mu, e_w0, e_w2, e_a0, e_a2, e_g2, e_k_k, e_k_a, e_r_k, e_ln_w, e_ln_b, e_conv_w, e_conv_b, e_gate_a_w, e_gate_a_b, e_gate_x_w, e_gate_x_b, e_lru_lambda, e_w_out, o_norm_g, o_w_in, o_A_re, o_A_im, o_log_dt, o_B_re, o_B_im, o_C_re, o_C_im, o_D, o_w_glu, f_norm_g, f_w_up, f_conv_w, f_conv_b, f_w_down, final_norm_g, loss_target, m_e_norm_g, m_e_w_in, m_e_mu, m_e_w0, m_e_w2, m_e_a0, m_e_a2, m_e_g2, m_e_k_k, m_e_k_a, m_e_r_k, m_e_ln_w, m_e_ln_b, m_e_conv_w, m_e_conv_b, m_e_gate_a_w, m_e_gate_a_b, m_e_gate_x_w, m_e_gate_x_b, m_e_lru_lambda, m_e_w_out, m_o_norm_g, m_o_w_in, m_o_A_re, m_o_A_im, m_o_log_dt, m_o_B_re, m_o_B_im, m_o_C_re, m_o_C_im, m_o_D, m_o_w_glu, m_f_norm_g, m_f_w_up, m_f_conv_w, m_f_conv_b, m_f_w_down, m_final_norm_g, v_e_norm_g, v_e_w_in, v_e_mu, v_e_w0, v_e_w2, v_e_a0, v_e_a2, v_e_g2, v_e_k_k, v_e_k_a, v_e_r_k, v_e_ln_w, v_e_ln_b, v_e_conv_w, v_e_conv_b, v_e_gate_a_w, v_e_gate_a_b, v_e_gate_x_w, v_e_gate_x_b, v_e_lru_lambda, v_e_w_out, v_o_norm_g, v_o_w_in, v_o_A_re, v_o_A_im, v_o_log_dt, v_o_B_re, v_o_B_im, v_o_C_re, v_o_C_im, v_o_D, v_o_w_glu, v_f_norm_g, v_f_w_up, v_f_conv_w, v_f_conv_b, v_f_w_down, v_final_norm_g):
    args = locals()
    wts = {n: args[n] for n in _ORDER}
    ms = {n: args["m_" + n] for n in _ORDER}
    vs = {n: args["v_" + n] for n in _ORDER}
    return _step(x[0], loss_target[0], wts, ms, vs)
```

```python
import functools

import jax
import jax.numpy as jnp
from jax import lax
from jax.experimental import pallas as pl
from jax.experimental.pallas import tpu as pltpu

F32 = jnp.float32
BF16 = jnp.bfloat16
MESH = pl.DeviceIdType.MESH

HEAD = 64
RW = 512
N_HEADS = RW // HEAD
LRU_W = 512
SHIFT_COLS = 1792
W_LORA, A_LORA, G_LORA = 64, 64, 128
S5_GROUPS, S5_GROUP, S5_STATE = 64, 16, 64
D_FF = 2816
NORM_EPS = 1e-6
GN_EPS = 64e-5
LRU_C = 8.0
ADAM_LR, ADAM_B1, ADAM_B2, ADAM_EPS, ADAM_WD, ADAM_STEP = 0.001, 0.9, 0.999, 1e-08, 0.01, 10

VMEM_BIG = 56 * 1024 * 1024
VMEM_MID = 40 * 1024 * 1024
LANES = 128
PT = 16
WKV_CHUNK = 32
S5_SLAB = 128


def _cparams(sem=None, vmem=None):
    kw = {}
    if sem is not None:
        kw["dimension_semantics"] = sem
    if vmem is not None:
        kw["vmem_limit_bytes"] = vmem
    return pltpu.CompilerParams(**kw)


def _tile(dim, cands):
    for c in cands:
        if dim % c == 0:
            return c
    return dim


def _full(shape):
    n = len(shape)
    return pl.BlockSpec(shape, lambda *_: (0,) * n)


_TILES = (2816, 2048, 1408, 1024, 512, 256, 128)
MM_BUDGET = 36 * 1024 * 1024
VMEM_SLACK = 12 * 1024 * 1024


MXU_FLOPS = 9.0e14
HBM_BYTES = 3.3e12
STEP_SECONDS = 0.35e-6


def _mm_tiles(m, n, k, size_a, size_b, size_o, has_add):
    best = None
    for tm in _TILES:
        for tk in _TILES:
            for tn in _TILES:
                if m % tm or n % tn or k % tk:
                    continue
                need = 2 * (tm * tk * size_a + tk * tn * size_b + tm * tn * size_o) + tm * tn * 4 * (1 + 2 * has_add)
                if k > tk:
                    need += tm * tn * 4
                if need > MM_BUDGET:
                    continue
                steps = (m // tm) * (n // tn) * (k // tk)
                a_reads = n // tn if k > tk else 1
                moved = (m * k * size_a * a_reads + k * n * size_b * (m // tm) + m * n * (size_o + 4 * has_add))
                cost = max(2.0 * m * n * k / MXU_FLOPS, moved / HBM_BYTES) + steps * STEP_SECONDS
                cand = (-cost, tk, tm, tn)
                if best is None or cand > best[0]:
                    best = (cand, need)
    (_, tk, tm, tn), need = best
    return tm, tn, tk, need


def _matmul(a, b, mode, name, out_dtype=F32, add=None):
    if mode == "nn":
        (m, k), (k2, n) = a.shape, b.shape
    elif mode == "nt":
        (m, k), (n, k2) = a.shape, b.shape
    else:
        (k, m), (k2, n) = a.shape, b.shape
    assert k == k2, (a.shape, b.shape, mode)
    tm, tn, tk, need = _mm_tiles(m, n, k, a.dtype.itemsize, b.dtype.itemsize, jnp.dtype(out_dtype).itemsize,
                                 add is not None)
    nk = k // tk
    dims = {"nn": (((1,), (0,)), ((), ())), "nt": (((1,), (1,)), ((), ())), "tn": (((0,), (0,)), ((), ()))}[mode]

    def body(*refs):
        a_ref, b_ref = refs[:2]
        add_ref = refs[2] if add is not None else None
        o_ref = refs[3] if add is not None else refs[2]
        part = lax.dot_general(a_ref[...].astype(BF16), b_ref[...].astype(BF16), dims, preferred_element_type=F32)

        def finish(r):
            if add_ref is not None:
                r = r + add_ref[...]
            o_ref[...] = r.astype(o_ref.dtype)

        if nk == 1:
            finish(part)
            return
        acc = refs[-1]
        kk = pl.program_id(2)

        @pl.when(kk == 0)
        def _():
            acc[...] = part

        @pl.when(kk > 0)
        def _():
            acc[...] += part

        @pl.when(kk == nk - 1)
        def _():
            finish(acc[...])

    if mode == "nn":
        a_spec = pl.BlockSpec((tm, tk), lambda i, j, kk: (i, kk))
        b_spec = pl.BlockSpec((tk, tn), lambda i, j, kk: (kk, j))
    elif mode == "nt":
        a_spec = pl.BlockSpec((tm, tk), lambda i, j, kk: (i, kk))
        b_spec = pl.BlockSpec((tn, tk), lambda i, j, kk: (j, kk))
    else:
        a_spec = pl.BlockSpec((tk, tm), lambda i, j, kk: (kk, i))
        b_spec = pl.BlockSpec((tk, tn), lambda i, j, kk: (kk, j))
    o_spec = pl.BlockSpec((tm, tn), lambda i, j, kk: (i, j))
    in_specs = [a_spec, b_spec] + ([o_spec] if add is not None else [])
    args = (a, b) + ((add,) if add is not None else ())
    return pl.pallas_call(
        body, name=name, grid=(m // tm, n // tn, nk),
        in_specs=in_specs, out_specs=o_spec,
        out_shape=jax.ShapeDtypeStruct((m, n), out_dtype),
        scratch_shapes=[pltpu.VMEM((tm, tn), F32)] if nk > 1 else [],
        compiler_params=_cparams(("parallel", "parallel", "arbitrary"), min(VMEM_BIG, need + VMEM_SLACK)),
    )(*args)


TOK = 256


def _rms(x, g):
    return x * lax.rsqrt(jnp.mean(x * x, axis=-1, keepdims=True) + NORM_EPS) * g


def _rms_fwd(x, g, name):
    t, d = x.shape

    def body(x_ref, g_ref, o_ref):
        o_ref[...] = _rms(x_ref[...], g_ref[...]).astype(BF16)

    row = pl.BlockSpec((TOK, d), lambda i: (i, 0))
    return pl.pallas_call(body, name=name, grid=(t // TOK,), in_specs=[row, _full((1, d))], out_specs=row,
                          out_shape=jax.ShapeDtypeStruct((t, d), BF16),
                          compiler_params=_cparams(("parallel",)))(x, g)


def _rms_bwd(x, g, dxn, res, name):
    t, d = x.shape

    def body(x_ref, g_ref, d_ref, res_ref, dx_ref, dg_ref):
        _, vjp = jax.vjp(_rms, x_ref[...], g_ref[...])
        dx, dg = vjp(d_ref[...].astype(F32))
        dx_ref[...] = dx + res_ref[...]

        @pl.when(pl.program_id(0) == 0)
        def _():
            dg_ref[...] = jnp.zeros_like(dg_ref)

        dg_ref[...] += dg

    row = pl.BlockSpec((TOK, d), lambda i: (i, 0))
    return pl.pallas_call(body, name=name, grid=(t // TOK,), in_specs=[row, _full((1, d)), row, row],
                          out_specs=[row, _full((1, d))],
                          out_shape=[jax.ShapeDtypeStruct((t, d), F32), jax.ShapeDtypeStruct((1, d), F32)],
                          compiler_params=_cparams(("arbitrary",)))(x, g, dxn, res)


def _loss_head(x, g, tgt):
    t, d = x.shape

    def body(x_ref, g_ref, t_ref, l_ref, dx_ref, dg_ref):
        tg = t_ref[...]

        def fn(xv, gv):
            err = _rms(xv, gv) - tg
            per_tok = jnp.mean(err * err, axis=-1, keepdims=True)
            return 0.5 * jnp.sum(per_tok, axis=0, keepdims=True)

        l, vjp = jax.vjp(fn, x_ref[...], g_ref[...])
        dx, dg = vjp(jnp.ones((1, 1), F32))
        dx_ref[...] = dx

        @pl.when(pl.program_id(0) == 0)
        def _():
            dg_ref[...] = jnp.zeros_like(dg_ref)
            l_ref[...] = jnp.zeros_like(l_ref)

        dg_ref[...] += dg
        l_ref[...] += jnp.broadcast_to(l, l_ref.shape)

    row = pl.BlockSpec((TOK, d), lambda i: (i, 0))
    return pl.pallas_call(body, name="loss_head", grid=(t // TOK,), in_specs=[row, _full((1, d)), row],
                          out_specs=[_full((1, LANES)), row, _full((1, d))],
                          out_shape=[jax.ShapeDtypeStruct((1, LANES), F32), jax.ShapeDtypeStruct((t, d), F32),
                                     jax.ShapeDtypeStruct((1, d), F32)],
                          compiler_params=_cparams(("arbitrary",)))(x, g, tgt)


def _glu_fwd(x, z):
    t, d = x.shape

    def body(x_ref, v_ref, g_ref, o_ref):
        o_ref[...] = x_ref[...] + v_ref[...] * jax.nn.sigmoid(g_ref[...])

    row = pl.BlockSpec((TOK, d), lambda i: (i, 0))
    gate = pl.BlockSpec((TOK, d), lambda i: (i, 1))
    return pl.pallas_call(body, name="glu_fwd", grid=(t // TOK,), in_specs=[row, row, gate], out_specs=row,
                          out_shape=jax.ShapeDtypeStruct((t, d), F32),
                          compiler_params=_cparams(("parallel",)))(x, z, z)


def _glu_bwd(z, g):
    t, d = g.shape

    def body(v_ref, g_ref, d_ref, o_ref):
        s = jax.nn.sigmoid(g_ref[...])
        dy = d_ref[...]
        o_ref[:, :d] = (dy * s).astype(BF16)
        o_ref[:, d:] = (dy * v_ref[...] * s * (1.0 - s)).astype(BF16)

    row = pl.BlockSpec((TOK, d), lambda i: (i, 0))
    gate = pl.BlockSpec((TOK, d), lambda i: (i, 1))
    return pl.pallas_call(body, name="glu_bwd", grid=(t // TOK,), in_specs=[row, gate, row],
                          out_specs=pl.BlockSpec((TOK, 2 * d), lambda i: (i, 0)),
                          out_shape=jax.ShapeDtypeStruct((t, 2 * d), BF16),
                          compiler_params=_cparams(("parallel",)))(z, z, g)


def _shift_down(x, d):
    row = lax.broadcasted_iota(jnp.int32, x.shape, 0)
    return jnp.where(row < d, 0.0, pltpu.roll(x, d, 0))


def _shift_up(x, d):
    n = x.shape[0]
    row = lax.broadcasted_iota(jnp.int32, x.shape, 0)
    return jnp.where(row >= n - d, 0.0, pltpu.roll(x, n - d, 0))


def _make_sd():
    @functools.partial(jax.custom_vjp, nondiff_argnums=(1,))
    def sd(x, d):
        return _shift_down(x, d)

    def fwd(x, d):
        return _shift_down(x, d), None

    def bwd(d, _, g):
        return (_shift_up(g, d),)

    sd.defvjp(fwd, bwd)
    return sd


def _lin_scan(a, u, reverse=False):
    n = a.shape[0]
    row = lax.broadcasted_iota(jnp.int32, a.shape, 0)
    d = 1
    while d < n:
        if reverse:
            keep = row < n - d
            a_s, u_s = pltpu.roll(a, n - d, 0), pltpu.roll(u, n - d, 0)
        else:
            keep = row >= d
            a_s, u_s = pltpu.roll(a, d, 0), pltpu.roll(u, d, 0)
        u = u + a * jnp.where(keep, u_s, 0.0)
        a = a * jnp.where(keep, a_s, 1.0)
        d *= 2
    return u


def _make_scan():
    @jax.custom_vjp
    def scan(a, u):
        return _lin_scan(a, u)

    def fwd(a, u):
        h = _lin_scan(a, u)
        return h, (a, h)

    def bwd(res, dh):
        a, h = res
        g = _lin_scan(_shift_up(a, 1), dh, reverse=True)
        return g * _shift_down(h, 1), g

    scan.defvjp(fwd, bwd)
    return scan


def _acc_out(ref, val):
    @pl.when(pl.program_id(0) == 0)
    def _():
        ref[...] = jnp.zeros_like(ref)

    ref[...] += val


FFN_CW = 128


def _ffn_fn(hg, hv, wg, wv, bg, bv, sd):
    cg = wg[0:1] * sd(hg, 2) + wg[1:2] * sd(hg, 1) + wg[2:3] * hg + bg
    cv = wv[0:1] * sd(hv, 2) + wv[1:2] * sd(hv, 1) + wv[2:3] * hv + bv
    return jax.nn.silu(cg) * cv


def _ffn_specs(t):
    nb = D_FF // FFN_CW
    col = lambda r, off: pl.BlockSpec((r, FFN_CW), lambda j: (0, j + off))
    return nb, [col(t, 0), col(t, nb), col(3, 0), col(3, nb), col(1, 0), col(1, nb)], col


def _ffn_mid_fwd(h, cw, cb, name):
    t = h.shape[0]
    nb, in_specs, col = _ffn_specs(t)

    def body(hg, hv, wg, wv, bg, bv, o_ref):
        o_ref[...] = _ffn_fn(hg[...], hv[...], wg[...], wv[...], bg[...], bv[...], _shift_down).astype(BF16)

    return pl.pallas_call(body, name=name, grid=(nb,), in_specs=in_specs, out_specs=col(t, 0),
                          out_shape=jax.ShapeDtypeStruct((t, D_FF), BF16),
                          compiler_params=_cparams(("parallel",), VMEM_MID))(h, h, cw, cw, cb, cb)


def _ffn_mid_bwd(h, cw, cb, dact, name):
    t = h.shape[0]
    nb, in_specs, col = _ffn_specs(t)

    def body(hg, hv, wg, wv, bg, bv, d_ref, dhg, dhv, dwg, dwv, dbg, dbv):
        fn = functools.partial(_ffn_fn, sd=_make_sd())
        _, vjp = jax.vjp(fn, hg[...], hv[...], wg[...], wv[...], bg[...], bv[...])
        g = vjp(d_ref[...])
        dhg[...] = g[0].astype(BF16)
        dhv[...] = g[1].astype(BF16)
        dwg[...], dwv[...], dbg[...], dbv[...] = g[2], g[3], g[4], g[5]

    big = jax.ShapeDtypeStruct((t, D_FF), BF16)
    w3 = jax.ShapeDtypeStruct((3, D_FF), F32)
    b1 = jax.ShapeDtypeStruct((1, D_FF), F32)
    return pl.pallas_call(body, name=name, grid=(nb,), in_specs=in_specs + [col(t, 0)],
                          out_specs=[col(t, 0), col(t, 0), col(3, 0), col(3, 0), col(1, 0), col(1, 0)],
                          out_shape=[big, big, w3, w3, b1, b1],
                          compiler_params=_cparams(("parallel",), VMEM_BIG))(h, h, cw, cw, cb, cb, dact)


TS_CW = 256


def _tshift_fn(p, mu, sd):
    return p + mu * (sd(p, 1) - p)


def _tshift_fwd(p, mu):
    t = p.shape[0]
    col = lambda r: pl.BlockSpec((r, TS_CW), lambda j: (0, j))

    def body(p_ref, mu_ref, o_ref):
        o_ref[...] = _tshift_fn(p_ref[...], mu_ref[...], _shift_down)

    return pl.pallas_call(body, name="tshift_fwd", grid=(SHIFT_COLS // TS_CW,), in_specs=[col(t), col(1)],
                          out_specs=col(t), out_shape=jax.ShapeDtypeStruct((t, SHIFT_COLS), F32),
                          compiler_params=_cparams(("parallel",), VMEM_MID))(p, mu)


def _tshift_bwd(p, mu, dpam):
    t = p.shape[0]
    col = lambda r: pl.BlockSpec((r, TS_CW), lambda j: (0, j))

    def body(p_ref, mu_ref, d_ref, dp_ref, dmu_ref):
        _, vjp = jax.vjp(functools.partial(_tshift_fn, sd=_make_sd()), p_ref[...], mu_ref[...])
        dp, dmu = vjp(d_ref[...])
        dp_ref[...] = dp.astype(BF16)
        dmu_ref[...] = dmu

    return pl.pallas_call(body, name="tshift_bwd", grid=(SHIFT_COLS // TS_CW,), in_specs=[col(t), col(1), col(t)],
                          out_specs=[col(t), col(1)],
                          out_shape=[jax.ShapeDtypeStruct((t, SHIFT_COLS), BF16),
                                     jax.ShapeDtypeStruct((1, SHIFT_COLS), F32)],
                          compiler_params=_cparams(("parallel",), VMEM_MID))(p, mu, dpam)


_HI = lax.Precision.HIGHEST
_O = (0, RW, 2 * RW, 3 * RW, 3 * RW + W_LORA, 3 * RW + W_LORA + A_LORA, SHIFT_COLS)


def _dot16(a, b, dims=(((1,), (0,)), ((), ()))):
    return lax.dot_general(a.astype(BF16), b.astype(BF16), dims, preferred_element_type=F32)


def _make_dot16():
    @jax.custom_vjp
    def dot(a, b):
        return _dot16(a, b)

    def fwd(a, b):
        return _dot16(a, b), (a, b)

    def bwd(res, g):
        a, b = res
        return _dot16(g, b, (((1,), (1,)), ((), ()))), _dot16(a, g, (((0,), (0,)), ((), ())))

    dot.defvjp(fwd, bwd)
    return dot


def _seg(x, gm):
    return jnp.dot(x, gm, precision=_HI)


def _prep_fn(r, k, v, wd, ad, gd, w0, w2, a0, a2, g2, k_k, k_a, gm, dot):
    w_log = -jax.nn.softplus(-(w0 + dot(jnp.tanh(wd), w2))) - 0.5
    decay = jnp.exp(-jnp.exp(w_log))
    a = jax.nn.sigmoid(a0 + dot(ad, a2))
    g = dot(jax.nn.sigmoid(gd), g2)
    kk = k * k_k
    kk = kk / jnp.maximum(jnp.sqrt(_seg(kk * kk, gm)), 1e-12)
    k2 = k * (1.0 + (a - 1.0) * k_a)
    return r, decay, k2, v, -kk, kk * a, g


_PREP_W = ("w0", "w2", "a0", "a2", "g2", "k_k", "k_a")


def _prep_wspecs(w):
    return [_full(w[n].shape) for n in _PREP_W] + [_full((RW, RW))]


def _rwkv_prep_fwd(pam, w, gm):
    t = pam.shape[0]

    def body(p_ref, *refs):
        wr, outs = refs[:8], refs[8:]
        pieces = [p_ref[:, _O[i]:_O[i + 1]] for i in range(6)]
        res = _prep_fn(*pieces, *[x[...] for x in wr], _dot16)
        for o, val in zip(outs, res):
            o[...] = val

    row = lambda c: pl.BlockSpec((TOK, c), lambda i: (i, 0))
    return pl.pallas_call(body, name="rwkv_prep_fwd", grid=(t // TOK,),
                          in_specs=[row(SHIFT_COLS)] + _prep_wspecs(w), out_specs=[row(RW)] * 7,
                          out_shape=[jax.ShapeDtypeStruct((t, RW), F32)] * 7,
                          compiler_params=_cparams(("parallel",), VMEM_MID))(pam, *[w[n] for n in _PREP_W], gm)


def _rwkv_prep_bwd(pam, w, gm, cts, more):
    t = pam.shape[0]

    def body(p_ref, *refs):
        wr, ct, ex, dp_ref, dws = refs[:8], refs[8:15], refs[15:18], refs[18], refs[19:]
        pieces = [p_ref[:, _O[i]:_O[i + 1]] for i in range(6)]
        fn = lambda *a: _prep_fn(*a, wr[7][...], _make_dot16())
        _, vjp = jax.vjp(fn, *pieces, *[x[...] for x in wr[:7]])
        c = [x[...] for x in ct]
        c[0] = c[0] + ex[0][...]
        c[2] = c[2] + ex[1][...]
        c[3] = c[3] + ex[2][...]
        g = vjp(tuple(c))
        for i in range(6):
            dp_ref[:, _O[i]:_O[i + 1]] = g[i]
        for o, val in zip(dws, g[6:]):
            _acc_out(o, val)

    row = lambda c: pl.BlockSpec((TOK, c), lambda i: (i, 0))
    return pl.pallas_call(body, name="rwkv_prep_bwd", grid=(t // TOK,),
                          in_specs=[row(SHIFT_COLS)] + _prep_wspecs(w) + [row(RW)] * 10,
                          out_specs=[row(SHIFT_COLS)] + [_full(w[n].shape) for n in _PREP_W],
                          out_shape=[jax.ShapeDtypeStruct((t, SHIFT_COLS), F32)]
                          + [jax.ShapeDtypeStruct(w[n].shape, F32) for n in _PREP_W],
                          compiler_params=_cparams(("arbitrary",), VMEM_MID))(
                              pam, *[w[n] for n in _PREP_W], gm, *cts, *more)


def _post_fn(y, r, k2, v, g, ln_w, ln_b, r_k, gm):
    inv = 1.0 / HEAD
    d = y - _seg(y, gm) * inv
    yn = d * lax.rsqrt(_seg(d * d, gm) * inv + GN_EPS) * ln_w + ln_b
    bonus = _seg(r * k2 * r_k, gm) * v
    return (yn + bonus) * g


def _rwkv_post_fwd(y, r, k2, v, g, ln_w, ln_b, r_k, gm):
    t = y.shape[0]

    def body(*refs):
        o_ref = refs[-1]
        o_ref[...] = _post_fn(*[x[...] for x in refs[:-1]]).astype(BF16)

    row = pl.BlockSpec((TOK, RW), lambda i: (i, 0))
    return pl.pallas_call(body, name="rwkv_post_fwd", grid=(t // TOK,),
                          in_specs=[row] * 5 + [_full((1, RW))] * 3 + [_full((RW, RW))], out_specs=row,
                          out_shape=jax.ShapeDtypeStruct((t, RW), BF16),
                          compiler_params=_cparams(("parallel",), VMEM_MID))(y, r, k2, v, g, ln_w, ln_b, r_k, gm)


def _rwkv_post_bwd(y, r, k2, v, g, ln_w, ln_b, r_k, gm, dya):
    t = y.shape[0]

    def body(*refs):
        ins, gm_ref, d_ref, outs = refs[:8], refs[8], refs[9], refs[10:]
        fn = lambda *a: _post_fn(*a, gm_ref[...])
        _, vjp = jax.vjp(fn, *[x[...] for x in ins])
        gr = vjp(d_ref[...])
        for o, val in zip(outs[:5], gr[:5]):
            o[...] = val
        for o, val in zip(outs[5:], gr[5:]):
            _acc_out(o, val)

    row = pl.BlockSpec((TOK, RW), lambda i: (i, 0))
    vec = _full((1, RW))
    return pl.pallas_call(body, name="rwkv_post_bwd", grid=(t // TOK,),
                          in_specs=[row] * 5 + [vec] * 3 + [_full((RW, RW)), row],
                          out_specs=[row] * 5 + [vec] * 3,
                          out_shape=[jax.ShapeDtypeStruct((t, RW), F32)] * 5 + [jax.ShapeDtypeStruct((1, RW), F32)] * 3,
                          compiler_params=_cparams(("arbitrary",), VMEM_MID))(y, r, k2, v, g, ln_w, ln_b, r_k, gm, dya)


def _from_pt(x):
    n = x.shape[0]
    return x.reshape(n, HEAD, N_HEADS, PT).transpose(0, 3, 2, 1).reshape(n * PT, N_HEADS * HEAD)


def _lane_sum(x):
    return jnp.sum(x, axis=-1, keepdims=True)


def _pair_consts():
    lane = lax.broadcasted_iota(jnp.int32, (HEAD, LANES), 1)
    return lane, lane < HEAD


def _seg_sum_pair(x, first):
    return jnp.where(first, _lane_sum(jnp.where(first, x, 0.0)), _lane_sum(jnp.where(first, 0.0, x)))


def _to_pt(x):
    t = x.shape[0]
    return x.reshape(t // PT, PT, N_HEADS, HEAD).transpose(0, 3, 2, 1).reshape(t // PT, HEAD, N_HEADS * PT)


def _expand_cols(x, name):
    t = x.shape[0]
    tiles = WKV_CHUNK // PT

    def body(x_ref, o_ref):
        _, first = _pair_consts()
        for tl in range(tiles):
            tile = x_ref[tl]
            for j in range(PT):
                for p in range(N_HEADS // 2):
                    src = jnp.where(first, (2 * p) * PT + j, (2 * p + 1) * PT + j)
                    o_ref[tl * PT + j, :, p * LANES:(p + 1) * LANES] = jnp.take_along_axis(tile, src, axis=1)

    return pl.pallas_call(
        body, name=name, grid=(t // WKV_CHUNK,),
        in_specs=[pl.BlockSpec((tiles, HEAD, LANES), lambda i: (i, 0, 0))],
        out_specs=pl.BlockSpec((WKV_CHUNK, HEAD, RW), lambda i: (i, 0, 0)),
        out_shape=jax.ShapeDtypeStruct((t, HEAD, RW), F32),
        compiler_params=_cparams(("parallel",), VMEM_MID))(_to_pt(x))


def _wkv_fwd(w, k, z, b, v_exp):
    t = w.shape[0]
    nc = t // WKV_CHUNK
    pairs = N_HEADS // 2

    def body(w_ref, k_ref, z_ref, b_ref, v_ref, s_all, s_ref):
        @pl.when(pl.program_id(0) == 0)
        def _():
            s_ref[...] = jnp.zeros_like(s_ref)

        _, first = _pair_consts()

        def group(gi, carry):
            base = pl.multiple_of(gi * 8, 8)
            rows = [ref[pl.ds(base, 8), :] for ref in (w_ref, k_ref, z_ref, b_ref)]
            s = [s_ref[:, p * LANES:(p + 1) * LANES] for p in range(pairs)]
            for jj in range(8):
                for p in range(pairs):
                    cs = slice(p * LANES, (p + 1) * LANES)
                    wr, kr, zr, br = [x[jj:jj + 1, cs] for x in rows]
                    s_all[base + jj, :, cs] = s[p]
                    sa = _seg_sum_pair(s[p] * zr, first)
                    s[p] = s[p] * wr + sa * br + v_ref[base + jj, :, cs] * kr
            for p in range(pairs):
                s_ref[:, p * LANES:(p + 1) * LANES] = s[p]
            return carry

        lax.fori_loop(0, WKV_CHUNK // 8, group, 0)

    row = pl.BlockSpec((WKV_CHUNK, RW), lambda i: (i, 0))
    big = pl.BlockSpec((WKV_CHUNK, HEAD, RW), lambda i: (i, 0, 0))
    return pl.pallas_call(
        body, name="wkv_fwd", grid=(nc,), in_specs=[row] * 4 + [big], out_specs=[big, _full((HEAD, RW))],
        out_shape=[jax.ShapeDtypeStruct((t, HEAD, RW), F32), jax.ShapeDtypeStruct((HEAD, RW), F32)],
        compiler_params=_cparams(("arbitrary",), VMEM_MID))(w, k, z, b, v_exp)


def _wkv_out(r, s_all, s_last):
    t = r.shape[0]
    nc = t // WKV_CHUNK
    tiles = WKV_CHUNK // PT
    pairs = N_HEADS // 2

    def body(r_ref, s_ref, nxt_ref, last_ref, y_ref):
        lane, first = _pair_consts()
        after = jnp.where(pl.program_id(0) == nc - 1, last_ref[...], nxt_ref[0])
        for tl in range(tiles):
            ytile = jnp.zeros((HEAD, LANES), F32)
            for g in range(PT // 8):
                rows = r_ref[tl * PT + g * 8:tl * PT + g * 8 + 8, :]
                for jj in range(8):
                    tt = tl * PT + g * 8 + jj
                    j = g * 8 + jj
                    for p in range(pairs):
                        cs = slice(p * LANES, (p + 1) * LANES)
                        s = s_ref[tt + 1, :, cs] if tt + 1 < WKV_CHUNK else after[:, cs]
                        pr = s * rows[jj:jj + 1, cs]
                        y0 = _lane_sum(jnp.where(first, pr, 0.0))
                        y1 = _lane_sum(jnp.where(first, 0.0, pr))
                        ytile = jnp.where(lane == (2 * p) * PT + j, y0, ytile)
                        ytile = jnp.where(lane == (2 * p + 1) * PT + j, y1, ytile)
            y_ref[tl] = ytile

    row = pl.BlockSpec((WKV_CHUNK, RW), lambda i: (i, 0))
    pt = pl.BlockSpec((tiles, HEAD, LANES), lambda i: (i, 0, 0))
    big = pl.BlockSpec((WKV_CHUNK, HEAD, RW), lambda i: (i, 0, 0))
    nxt = pl.BlockSpec((1, HEAD, RW), lambda i: (jnp.minimum((i + 1) * WKV_CHUNK, t - 1), 0, 0))
    return pl.pallas_call(
        body, name="wkv_out", grid=(nc,), in_specs=[row, big, nxt, _full((HEAD, RW))], out_specs=pt,
        out_shape=jax.ShapeDtypeStruct((t // PT, HEAD, LANES), F32),
        compiler_params=_cparams(("parallel",), VMEM_MID))(r, s_all, s_all, s_last)


def _wkv_bwd(r, w, k, z, b, v_exp, s_all, dy_exp):
    t = r.shape[0]
    nc = t // WKV_CHUNK
    tiles = WKV_CHUNK // PT
    pairs = N_HEADS // 2

    def body(r_ref, w_ref, k_ref, z_ref, b_ref, v_ref, s_all_ref, dy_ref,
             dr_ref, dw_ref, dk_ref, dz_ref, db_ref, dv_ref, ds_ref):
        @pl.when(pl.program_id(0) == 0)
        def _():
            ds_ref[...] = jnp.zeros_like(ds_ref)

        lane, first = _pair_consts()
        col_sum = lambda x: jnp.sum(x, axis=0, keepdims=True)
        row8 = lax.broadcasted_iota(jnp.int32, (8, LANES), 0)
        for tl in reversed(range(tiles)):
            def group(gg, dvtile):
                gi = PT // 8 - 1 - gg
                base = pl.multiple_of(tl * PT + gi * 8, 8)
                rows = [ref[pl.ds(base, 8), :] for ref in (r_ref, w_ref, k_ref, z_ref, b_ref)]
                outs = (dr_ref, dw_ref, dk_ref, dz_ref, db_ref)
                tiles8 = {(id(o), p): jnp.zeros((8, LANES), F32) for o in outs for p in range(pairs)}
                ds = [ds_ref[:, p * LANES:(p + 1) * LANES] for p in range(pairs)]
                for jj in reversed(range(8)):
                    j = gi * 8 + jj
                    for p in range(pairs):
                        cs = slice(p * LANES, (p + 1) * LANES)

                        def put(ref, val, p=p, jj=jj):
                            tiles8[(id(ref), p)] = jnp.where(row8 == jj, val, tiles8[(id(ref), p)])

                        rr, wr, kr, zr, br = [x[jj:jj + 1, cs] for x in rows]
                        sp = s_all_ref[base + jj, :, cs]
                        vc = v_ref[base + jj, :, cs]
                        dyc = dy_ref[base + jj, :, cs]
                        sa = _seg_sum_pair(sp * zr, first)
                        st = sp * wr + sa * br + vc * kr
                        d = ds[p] + dyc * rr
                        put(dr_ref, col_sum(st * dyc))
                        dvk = d * kr
                        dv0 = _lane_sum(jnp.where(first, dvk, 0.0))
                        dv1 = _lane_sum(jnp.where(first, 0.0, dvk))
                        dvtile = jnp.where(lane == (2 * p) * PT + j, dv0, dvtile)
                        dvtile = jnp.where(lane == (2 * p + 1) * PT + j, dv1, dvtile)
                        put(dk_ref, col_sum(d * vc))
                        put(dw_ref, col_sum(sp * d))
                        u = _seg_sum_pair(d * br, first)
                        put(dz_ref, col_sum(sp * u))
                        put(db_ref, col_sum(d * sa))
                        ds[p] = d * wr + u * zr
                for p in range(pairs):
                    ds_ref[:, p * LANES:(p + 1) * LANES] = ds[p]
                for o in outs:
                    for p in range(pairs):
                        o[pl.ds(base, 8), p * LANES:(p + 1) * LANES] = tiles8[(id(o), p)]
                return dvtile

            dv_ref[tl] = lax.fori_loop(0, PT // 8, group, jnp.zeros((HEAD, LANES), F32))

    rev = lambda i: nc - 1 - i
    row = pl.BlockSpec((WKV_CHUNK, RW), lambda i: (rev(i), 0))
    pt = pl.BlockSpec((tiles, HEAD, LANES), lambda i: (rev(i), 0, 0))
    big = pl.BlockSpec((WKV_CHUNK, HEAD, RW), lambda i: (rev(i), 0, 0))
    return pl.pallas_call(
        body, name="wkv_bwd", grid=(nc,), in_specs=[row] * 5 + [big, big, big], out_specs=[row] * 5 + [pt],
        out_shape=[jax.ShapeDtypeStruct((t, RW), F32)] * 5 + [jax.ShapeDtypeStruct((t // PT, HEAD, LANES), F32)],
        scratch_shapes=[pltpu.VMEM((HEAD, RW), F32)],
        compiler_params=_cparams(("arbitrary",), VMEM_BIG))(r, w, k, z, b, v_exp, s_all, dy_exp)


LRU_CW = 128
_BX0 = SHIFT_COLS // LRU_CW
_BG0 = (SHIFT_COLS + LRU_W) // LRU_CW


def _lru_fn(bx, bg, cw, cb, ga, ba, gx, bxb, lam, sd, scan, dot):
    xc = cw[0:1] * sd(bx, 3) + cw[1:2] * sd(bx, 2) + cw[2:3] * sd(bx, 1) + cw[3:4] * bx + cb
    gr = jax.nn.sigmoid(dot(xc, ga) + ba)
    gi = jax.nn.sigmoid(dot(xc, gx) + bxb)
    log_a = -LRU_C * gr * jax.nn.softplus(-lam)
    a = jnp.exp(log_a)
    mult = jnp.sqrt(-jnp.tanh(log_a) * (jnp.exp(2.0 * log_a) + 1.0))
    return scan(a, xc * gi * mult) * jax.nn.gelu(bg)


def _lru_specs(t):
    col = lambda r, off=0: pl.BlockSpec((r, LRU_CW), lambda j: (0, j + off))
    diag = pl.BlockSpec((LRU_CW, LRU_CW), lambda j: (j, j))
    return col, [col(t, _BX0), col(t, _BG0), col(4), col(1), diag, col(1), diag, col(1), col(1)]


def _lru_fwd(p, cw, cb, ga, ba, gx, bxb, lam):
    t = p.shape[0]
    col, in_specs = _lru_specs(t)

    def body(*refs):
        o_ref = refs[-1]
        o_ref[...] = _lru_fn(*[x[...] for x in refs[:-1]], _shift_down, _lin_scan, _dot16).astype(BF16)

    return pl.pallas_call(body, name="lru_fwd", grid=(LRU_W // LRU_CW,), in_specs=in_specs, out_specs=col(t),
                          out_shape=jax.ShapeDtypeStruct((t, LRU_W), BF16),
                          compiler_params=_cparams(("parallel",), VMEM_MID))(p, p, cw, cb, ga, ba, gx, bxb, lam)


def _lru_bwd(p, cw, cb, ga, ba, gx, bxb, lam, dyb):
    t = p.shape[0]
    col, in_specs = _lru_specs(t)

    def body(*refs):
        ins, d_ref, outs = refs[:9], refs[9], refs[10:]
        fn = functools.partial(_lru_fn, sd=_make_sd(), scan=_make_scan(), dot=_make_dot16())
        _, vjp = jax.vjp(fn, *[x[...] for x in ins])
        g = vjp(d_ref[...])
        outs[0][...] = g[0].astype(BF16)
        outs[1][...] = g[1].astype(BF16)
        for o, val in zip(outs[2:], g[2:]):
            o[...] = val

    sq = pl.BlockSpec((LRU_CW, LRU_CW), lambda j: (j, 0))
    act = jax.ShapeDtypeStruct((t, LRU_W), BF16)
    vec = jax.ShapeDtypeStruct((1, LRU_W), F32)
    sqs = jax.ShapeDtypeStruct((LRU_W, LRU_CW), F32)
    return pl.pallas_call(body, name="lru_bwd", grid=(LRU_W // LRU_CW,), in_specs=in_specs + [col(t, RW // LRU_CW)],
                          out_specs=[col(t), col(t), col(4), col(1), sq, col(1), sq, col(1), col(1)],
                          out_shape=[act, act, jax.ShapeDtypeStruct((4, LRU_W), F32), vec, sqs, vec, sqs, vec, vec],
                          compiler_params=_cparams(("parallel",), VMEM_BIG))(p, p, cw, cb, ga, ba, gx, bxb, lam, dyb)


def _s5_disc_fn(a_re, a_im, log_dt, b_re, b_im, e):
    lam_re = jnp.minimum(a_re, -1e-4)
    lam_im = a_im
    dt = jnp.exp(log_dt)
    mag = jnp.exp(lam_re * dt)
    ab_re = mag * jnp.cos(lam_im * dt)
    ab_im = mag * jnp.sin(lam_im * dt)
    den = lam_re * lam_re + lam_im * lam_im
    zr = ab_re - 1.0
    q_re = jnp.dot((zr * lam_re + ab_im * lam_im) / den, e, precision=_HI)
    q_im = jnp.dot((ab_im * lam_re - zr * lam_im) / den, e, precision=_HI)
    return ab_re, ab_im, q_re * b_re - q_im * b_im, q_re * b_im + q_im * b_re


def _s5_disc_fwd(a_re, a_im, log_dt, b_re, b_im, e):
    def body(*refs):
        res = _s5_disc_fn(*[x[...] for x in refs[:6]])
        for o, val in zip(refs[6:], res):
            o[...] = val

    small = jax.ShapeDtypeStruct(a_re.shape, F32)
    wide = jax.ShapeDtypeStruct(b_re.shape, F32)
    return pl.pallas_call(body, name="s5_disc_fwd", out_shape=[small, small, wide, wide])(
        a_re, a_im, log_dt, b_re, b_im, e)


def _s5_disc_bwd(a_re, a_im, log_dt, b_re, b_im, e, cts):
    def body(*refs):
        ins, e_ref, ct, outs = refs[:5], refs[5], refs[6:10], refs[10:]
        _, vjp = jax.vjp(lambda *a: _s5_disc_fn(*a, e_ref[...]), *[x[...] for x in ins])
        for o, val in zip(outs, vjp(tuple(c[...] for c in ct))):
            o[...] = val

    shapes = [jax.ShapeDtypeStruct(x.shape, F32) for x in (a_re, a_im, log_dt, b_re, b_im)]
    return pl.pallas_call(body, name="s5_disc_bwd", out_shape=shapes)(a_re, a_im, log_dt, b_re, b_im, e, *cts)


def _cmul(a, b):
    return a[0] * b[0] - a[1] * b[1], a[0] * b[1] + a[1] * b[0]


def _s5_scan(sr, si, ab, reverse):
    n_tiles = sr.shape[0] // 8
    width = sr.shape[1]
    row8 = lax.broadcasted_iota(jnp.int32, (8, width), 0)
    p1 = ab
    p2 = _cmul(p1, p1)
    p4 = _cmul(p2, p2)
    pw = [p1]
    for _ in range(7):
        pw.append(_cmul(pw[-1], p1))
    cr = jnp.zeros((8, width), F32)
    ci = jnp.zeros((8, width), F32)
    for j in range(8):
        e = pw[7 - j] if reverse else pw[j]
        cr = jnp.where(row8 == j, e[0], cr)
        ci = jnp.where(row8 == j, e[1], ci)

    levels = []
    for d, q in ((1, p1), (2, p2), (4, p4)):
        keep = row8 < 8 - d if reverse else row8 >= d
        levels.append((d, (jnp.where(keep, q[0], 0.0), jnp.where(keep, q[1], 0.0))))

    def tile(i, carry):
        idx = n_tiles - 1 - i if reverse else i
        base = pl.multiple_of(idx * 8, 8)
        x = (sr[pl.ds(base, 8), :], si[pl.ds(base, 8), :])
        for d, q in levels:
            amt = 8 - d if reverse else d
            m = _cmul(q, (pltpu.roll(x[0], amt, 0), pltpu.roll(x[1], amt, 0)))
            x = (x[0] + m[0], x[1] + m[1])
        m = _cmul((cr, ci), carry)
        x = (x[0] + m[0], x[1] + m[1])
        sr[pl.ds(base, 8), :] = x[0]
        si[pl.ds(base, 8), :] = x[1]
        edge = slice(0, 1) if reverse else slice(7, 8)
        return x[0][edge], x[1][edge]

    zero = jnp.zeros((1, width), F32)
    lax.fori_loop(0, n_tiles, tile, (zero, zero))


_S5_W = S5_SLAB // S5_GROUP * S5_STATE


def _s5_specs(t):
    col = lambda r: pl.BlockSpec((r, S5_SLAB), lambda j: (0, j))
    bb = pl.BlockSpec((None, S5_SLAB, _S5_W), lambda j: (j, 0, 0))
    cd = pl.BlockSpec((None, _S5_W, S5_SLAB), lambda j: (j, 0, 0))
    ab = pl.BlockSpec((None, 1, _S5_W), lambda j: (j, 0, 0))
    return col, bb, cd, ab


def _s5_fwd(u, dvec, bbr, bbi, cdr, cdi, abr, abi):
    t, width = u.shape
    col, bb, cd, ab = _s5_specs(t)

    def body(u_ref, d_ref, bbr_ref, bbi_ref, cdr_ref, cdi_ref, abr_ref, abi_ref, o_ref, sr, si):
        uv = u_ref[...]
        sr[...] = _dot16(uv, bbr_ref[...])
        si[...] = _dot16(uv, bbi_ref[...])
        _s5_scan(sr, si, (abr_ref[...], abi_ref[...]), False)
        y = _dot16(sr[...], cdr_ref[...]) - _dot16(si[...], cdi_ref[...])
        o_ref[...] = jax.nn.gelu(y + d_ref[...] * uv).astype(BF16)

    return pl.pallas_call(body, name="s5_fwd", grid=(width // S5_SLAB,),
                          in_specs=[col(t), col(1), bb, bb, cd, cd, ab, ab], out_specs=col(t),
                          out_shape=jax.ShapeDtypeStruct((t, width), BF16),
                          scratch_shapes=[pltpu.VMEM((t, _S5_W), F32)] * 2,
                          compiler_params=_cparams(("parallel",), VMEM_BIG))(u, dvec, bbr, bbi, cdr, cdi, abr, abi)


def _s5_bwd(u, dvec, bbr, bbi, cdr, cdi, abr, abi, dyact):
    t, width = u.shape
    col, bb, cd, ab = _s5_specs(t)
    ns = width // S5_SLAB
    tn = (((0,), (0,)), ((), ()))
    nt = (((1,), (1,)), ((), ()))

    def body(u_ref, d_ref, bbr_ref, bbi_ref, cdr_ref, cdi_ref, abr_ref, abi_ref, dy_ref,
             du_ref, dd_ref, dbbr_ref, dbbi_ref, dcdr_ref, dcdi_ref, dabr_ref, dabi_ref, sr, si, gr, gi):
        uv = u_ref[...]
        dv = d_ref[...]
        abv = (abr_ref[...], abi_ref[...])
        sr[...] = _dot16(uv, bbr_ref[...])
        si[...] = _dot16(uv, bbi_ref[...])
        _s5_scan(sr, si, abv, False)
        y = _dot16(sr[...], cdr_ref[...]) - _dot16(si[...], cdi_ref[...])
        _, vjp = jax.vjp(jax.nn.gelu, y + dv * uv)
        (dpre,) = vjp(dy_ref[...].astype(F32))
        dd_ref[...] = jnp.sum(dpre * uv, axis=0, keepdims=True)
        dcdr_ref[...] = _dot16(sr[...], dpre, tn)
        dcdi_ref[...] = -_dot16(si[...], dpre, tn)
        gr[...] = _dot16(dpre, cdr_ref[...], nt)
        gi[...] = -_dot16(dpre, cdi_ref[...], nt)
        _s5_scan(gr, gi, (abv[0], -abv[1]), True)

        row8 = lax.broadcasted_iota(jnp.int32, (8, _S5_W), 0)

        def tile(i, carry):
            acc_r, acc_i, last_r, last_i = carry
            base = pl.multiple_of(i * 8, 8)
            s_r, s_i = sr[pl.ds(base, 8), :], si[pl.ds(base, 8), :]
            g_r, g_i = gr[pl.ds(base, 8), :], gi[pl.ds(base, 8), :]
            p_r = jnp.where(row8 == 0, last_r, pltpu.roll(s_r, 1, 0))
            p_i = jnp.where(row8 == 0, last_i, pltpu.roll(s_i, 1, 0))
            acc_r = acc_r + jnp.sum(g_r * p_r + g_i * p_i, axis=0, keepdims=True)
            acc_i = acc_i + jnp.sum(g_i * p_r - g_r * p_i, axis=0, keepdims=True)
            return acc_r, acc_i, s_r[7:8], s_i[7:8]

        zero = jnp.zeros((1, _S5_W), F32)
        acc_r, acc_i, _, _ = lax.fori_loop(0, t // 8, tile, (zero, zero, zero, zero))
        dabr_ref[...] = acc_r
        dabi_ref[...] = acc_i
        du_ref[...] = dpre * dv + _dot16(gr[...], bbr_ref[...], nt) + _dot16(gi[...], bbi_ref[...], nt)
        dbbr_ref[...] = _dot16(uv, gr[...], tn)
        dbbi_ref[...] = _dot16(uv, gi[...], tn)

    sds = jax.ShapeDtypeStruct
    return pl.pallas_call(
        body, name="s5_bwd", grid=(ns,), in_specs=[col(t), col(1), bb, bb, cd, cd, ab, ab, col(t)],
        out_specs=[col(t), col(1), bb, bb, cd, cd, ab, ab],
        out_shape=[sds((t, width), F32), sds((1, width), F32), sds((ns, S5_SLAB, _S5_W), F32),
                   sds((ns, S5_SLAB, _S5_W), F32), sds((ns, _S5_W, S5_SLAB), F32), sds((ns, _S5_W, S5_SLAB), F32),
                   sds((ns, 1, _S5_W), F32), sds((ns, 1, _S5_W), F32)],
        scratch_shapes=[pltpu.VMEM((t, _S5_W), F32)] * 4,
        compiler_params=_cparams(("parallel",), VMEM_BIG))(u, dvec, bbr, bbi, cdr, cdi, abr, abi, dyact)


def _gate_dense(w):
    h = w.shape[0]
    return jnp.einsum("hij,hg->higj", w, jnp.eye(h, dtype=F32)).reshape(h * HEAD, h * HEAD)


def _gate_blocks(d):
    x = d.reshape(LRU_W // LRU_CW, 2, HEAD, 2, HEAD)
    return jnp.einsum("tgihj,gh->tgij", x, jnp.eye(2, dtype=F32)).reshape(LRU_W // HEAD, HEAD, HEAD)


_GPS = S5_SLAB // S5_GROUP
_NS = S5_GROUPS // _GPS


def _s5_in_dense(bb):
    x = bb.reshape(_NS, _GPS, S5_STATE, S5_GROUP)
    return jnp.einsum("sgnc,gh->sgchn", x, jnp.eye(_GPS, dtype=F32)).reshape(_NS, S5_SLAB, _S5_W)


def _s5_in_blocks(d):
    x = d.reshape(_NS, _GPS, S5_GROUP, _GPS, S5_STATE)
    return jnp.einsum("sgchn,gh->sgnc", x, jnp.eye(_GPS, dtype=F32)).reshape(S5_GROUPS, S5_STATE * S5_GROUP)


def _s5_out_dense(c):
    x = c.reshape(_NS, _GPS, S5_GROUP, S5_STATE)
    return jnp.einsum("sgcn,gh->shngc", x, jnp.eye(_GPS, dtype=F32)).reshape(_NS, _S5_W, S5_SLAB)


def _s5_out_blocks(d):
    x = d.reshape(_NS, _GPS, S5_STATE, _GPS, S5_GROUP)
    return jnp.einsum("shngc,gh->sgcn", x, jnp.eye(_GPS, dtype=F32)).reshape(S5_GROUPS, S5_GROUP, S5_STATE)


def _local_step(x, tgt, w, late_weights, send_grads):
    d_model = x.shape[1]
    gs = {}
    gm = jnp.kron(jnp.eye(N_HEADS, dtype=F32), jnp.ones((HEAD, HEAD), F32))
    n_layers = w["f_norm_g"].shape[0]

    def ffn_fwd(xin, l):
        xn = _rms_fwd(xin, w["f_norm_g"][l:l + 1], f"rms_f{l}")
        h = _matmul(xn, w["f_w_up_t"][l], "nt", f"mm_f{l}_up")
        act = _ffn_mid_fwd(h, w["f_conv_w"][l], w["f_conv_b"][l:l + 1], f"ffn_mid_fwd{l}")
        return _matmul(act, w["f_w_down"][l], "nn", f"mm_f{l}_down", add=xin), (xin, xn, h, act)

    def ffn_bwd(g, saved, l):
        xin, xn, h, act = saved
        dact = _matmul(g, w["f_w_down"][l], "nt", f"mm_f{l}_dact")
        d_down = _matmul(act, g, "tn", f"mm_f{l}_ddown", out_dtype=BF16)
        dhg, dhv, dwg, dwv, dbg, dbv = _ffn_mid_bwd(h, w["f_conv_w"][l], w["f_conv_b"][l:l + 1], dact,
                                                    f"ffn_mid_bwd{l}")
        dh = jnp.concatenate([dhg, dhv], axis=1)
        dxn = _matmul(dh, w["f_w_up_t"][l], "nn", f"mm_f{l}_dxn")
        d_up = _matmul(dh, xn, "tn", f"mm_f{l}_dup", out_dtype=BF16)
        dx, dgn = _rms_bwd(xin, w["f_norm_g"][l:l + 1], dxn, g, f"rms_f{l}_bwd")
        return dx, d_up, d_down, jnp.concatenate([dwg, dwv], axis=1), jnp.concatenate([dbg, dbv], axis=1), dgn

    xn0 = _rms_fwd(x, w["e_norm_g"], "rms_e")
    p = _matmul(xn0, w["e_w_in_t"], "nt", "mm_e_in")
    pam = _tshift_fwd(p, w["e_mu"])
    pw = dict(w0=w["e_w0"], w2=w["e_w2"][0], a0=w["e_a0"], a2=w["e_a2"][0], g2=w["e_g2"][0],
              k_k=w["e_k_k"], k_a=w["e_k_a"])
    r, dec, k2, v, z, b, gate = _rwkv_prep_fwd(pam, pw, gm)
    v_exp = _expand_cols(v, "wkv_expand_v")
    s_all, s_last = _wkv_fwd(dec, k2, z, b, v_exp)
    y_pt = _wkv_out(r, s_all, s_last)
    y = _from_pt(y_pt)
    rk = w["e_r_k"].reshape(1, RW)
    ya = _rwkv_post_fwd(y, r, k2, v, gate, w["e_ln_w"], w["e_ln_b"], rk, gm)
    ga, gx = _gate_dense(w["e_gate_a_w"][0]), _gate_dense(w["e_gate_x_w"][0])
    lru_w = (w["e_conv_w"][0], w["e_conv_b"], ga, w["e_gate_a_b"], gx, w["e_gate_x_b"], w["e_lru_lambda"])
    yb = _lru_fwd(p, *lru_w)
    ycat = jnp.concatenate([ya, yb], axis=1)
    x1 = _matmul(ycat, w["e_w_out"], "nn", "mm_e_out", add=x)
    w = {**w, **late_weights(x1)}
    x2, ffn0 = ffn_fwd(x1, 0)

    xn1 = _rms_fwd(x2, w["o_norm_g"], "rms_o")
    u = _matmul(xn1, w["o_w_in"], "nn", "mm_o_in")
    expand = jnp.kron(jnp.eye(S5_STATE, dtype=F32), jnp.ones((1, S5_GROUP), F32))
    disc_in = (w["o_A_re"][0], w["o_A_im"][0], w["o_log_dt"].reshape(S5_GROUPS, 1),
               w["o_B_re"][0].reshape(S5_GROUPS, -1), w["o_B_im"][0].reshape(S5_GROUPS, -1), expand)
    ab_re, ab_im, bb_re, bb_im = _s5_disc_fwd(*disc_in)
    s5_w = (w["o_D"], _s5_in_dense(bb_re), _s5_in_dense(bb_im), _s5_out_dense(w["o_C_re"][0]),
            _s5_out_dense(w["o_C_im"][0]), ab_re.reshape(_NS, 1, _S5_W), ab_im.reshape(_NS, 1, _S5_W))
    yact = _s5_fwd(u, *s5_w)
    zz = _matmul(yact, w["o_w_glu_t"], "nt", "mm_o_glu")
    x3 = _glu_fwd(x2, zz)
    x4, ffn1 = ffn_fwd(x3, 1)

    loss, g, gs["final_norm_g", 0] = _loss_head(x4, w["final_norm_g"].reshape(1, d_model), tgt)

    g, up1, down1, dcw1, dcb1, dfn1 = ffn_bwd(g, ffn1, 1)
    dz = _glu_bwd(zz, g)
    dyact = _matmul(dz, w["o_w_glu_t"], "nn", "mm_o_dyact")
    d_glu = _matmul(dz, yact, "tn", "mm_o_dglu", out_dtype=BF16)
    du, gs["o_D", 0], dbbr, dbbi, dcdr, dcdi, dabr, dabi = _s5_bwd(u, *s5_w, dyact)
    gs["o_C_re", 0] = _s5_out_blocks(dcdr).reshape(S5_GROUPS * S5_GROUP, S5_STATE)
    gs["o_C_im", 0] = _s5_out_blocks(dcdi).reshape(S5_GROUPS * S5_GROUP, S5_STATE)
    cts = (dabr.reshape(S5_GROUPS, S5_STATE), dabi.reshape(S5_GROUPS, S5_STATE), _s5_in_blocks(dbbr),
           _s5_in_blocks(dbbi))
    gs["o_A_re", 0], gs["o_A_im", 0], dlog_dt, gs["o_B_re", 0], gs["o_B_im", 0] = _s5_disc_bwd(*disc_in, cts)
    gs["o_log_dt", 0] = dlog_dt.reshape(1, S5_GROUPS)
    dxn = _matmul(du, w["o_w_in"], "nt", "mm_o_dxn")
    d_oin = _matmul(xn1, du, "tn", "mm_o_din", out_dtype=BF16)
    g, gs["o_norm_g", 0] = _rms_bwd(x2, w["o_norm_g"], dxn, g, "rms_o_bwd")
    g = send_grads("a", [("f_w_up", 1, up1), ("f_w_down", 1, down1), ("o_w_glu", 0, d_glu), ("o_w_in", 0, d_oin)], g)

    g, up0, down0, dcw0, dcb0, dfn0 = ffn_bwd(g, ffn0, 0)
    gs["f_conv_w", 0], gs["f_conv_w", 3] = dcw0, dcw1
    gs["f_conv_b", 0], gs["f_conv_b", 1] = dcb0, dcb1
    gs["f_norm_g", 0], gs["f_norm_g", 1] = dfn0, dfn1

    dycat = _matmul(g, w["e_w_out"], "nt", "mm_e_dycat")
    d_eout = _matmul(ycat, g, "tn", "mm_e_dout", out_dtype=BF16)
    dycat = send_grads("b", [("f_w_up", 0, up0), ("f_w_down", 0, down0), ("e_w_out", 0, d_eout)], dycat)
    dy, dr1, dk1, dv1, dgate, gs["e_ln_w", 0], gs["e_ln_b", 0], gs["e_r_k", 0] = _rwkv_post_bwd(
        y, r, k2, v, gate, w["e_ln_w"], w["e_ln_b"], rk, gm, dycat)
    dr2, ddec, dk2, dzz, dbb, dv_pt = _wkv_bwd(r, dec, k2, z, b, v_exp, s_all, _expand_cols(dy, "wkv_expand_dy"))
    (dpam, gs["e_w0", 0], gs["e_w2", 0], gs["e_a0", 0], gs["e_a2", 0], gs["e_g2", 0], gs["e_k_k", 0],
     gs["e_k_a", 0]) = _rwkv_prep_bwd(pam, pw, gm, (dr2, ddec, dk2, _from_pt(dv_pt), dzz, dbb, dgate), (dr1, dk1, dv1))
    dpa, gs["e_mu", 0] = _tshift_bwd(p, w["e_mu"], dpam)
    (dbx, dbg, gs["e_conv_w", 0], gs["e_conv_b", 0], dga, gs["e_gate_a_b", 0], dgx, gs["e_gate_x_b", 0],
     gs["e_lru_lambda", 0]) = _lru_bwd(p, *lru_w, dycat)
    gs["e_gate_a_w", 0] = _gate_blocks(dga).reshape(LRU_W, HEAD)
    gs["e_gate_x_w", 0] = _gate_blocks(dgx).reshape(LRU_W, HEAD)
    dp = jnp.concatenate([dpa, dbx, dbg], axis=1)
    dxn = _matmul(dp, w["e_w_in_t"], "nn", "mm_e_dxn")
    d_ein = _matmul(dp, xn0, "tn", "mm_e_din", out_dtype=BF16)
    grad_x, gs["e_norm_g", 0] = _rms_bwd(x, w["e_norm_g"], dxn, g, "rms_e_bwd")
    grad_x = send_grads("c", [("e_w_in", 0, d_ein)], grad_x)
    return loss, grad_x, gs


CAST_ROWS = 256


def _cast_shard(w3, layer, transpose, chip, name):
    _, rows, cols = w3.shape
    tr = _tile(rows, (CAST_ROWS, 176, 128))

    def body(c_ref, w_ref, o_ref):
        v = w_ref[...]
        o_ref[...] = (v.T if transpose else v).astype(BF16)

    in_spec = pl.BlockSpec((None, tr, cols), lambda i, c: (layer, i, 0))
    if transpose:
        out_spec, shape = pl.BlockSpec((None, cols, tr), lambda i, c: (c[0], 0, i)), (cols, rows)
    else:
        out_spec, shape = pl.BlockSpec((None, tr, cols), lambda i, c: (c[0], i, 0)), (rows, cols)
    grid_spec = pltpu.PrefetchScalarGridSpec(num_scalar_prefetch=1, grid=(rows // tr,), in_specs=[in_spec],
                                             out_specs=out_spec)
    return pl.pallas_call(body, name=name, grid_spec=grid_spec,
                          out_shape=jax.ShapeDtypeStruct((N_CHIPS,) + shape, BF16),
                          compiler_params=_cparams(("parallel",), VMEM_MID))(chip, w3)


_ANY = pl.BlockSpec(memory_space=pl.ANY)


def _coords():
    return lax.axis_index("x"), lax.axis_index("y"), lax.axis_index("c")


def _flip(v, d):
    return 1 - v if d else v


_CHIP_RELS = ((1, 0), (0, 1), (1, 1))
_DEV_RELS = tuple((dx, dy, dc) for dx in (0, 1) for dy in (0, 1) for dc in (0, 1))[1:]


_HBM = pl.BlockSpec(memory_space=pltpu.HBM)
_SEM = pl.BlockSpec(memory_space=pltpu.SEMAPHORE)
_EFFECT = pltpu.SideEffectType.DATAFLOW_SIDE_EFFECTING


def _in_hbm(a):
    return pltpu.with_memory_space_constraint(a, pltpu.HBM)


def _gather_copies(bufs, send, recv, landed):
    x, y, c = _coords()
    me = 2 * x + y
    res = []
    for i, buf in enumerate(bufs):
        for j, (dx, dy) in enumerate(_CHIP_RELS):
            px, py = _flip(x, dx), _flip(y, dy)
            k = i * len(_CHIP_RELS) + j
            res.append(pltpu.make_async_remote_copy(
                src_ref=buf.at[me], dst_ref=buf.at[2 * px + py if landed else me], send_sem=send.at[k],
                recv_sem=recv.at[k], device_id=(px, py, c), device_id_type=MESH))
    return res


def _scatter_copies(srcs, lands, send, recv, landed):
    x, y, c = _coords()
    me = 4 * x + 2 * y + c
    res = []
    for i, (src, land) in enumerate(zip(srcs, lands)):
        for j, (dx, dy, dc) in enumerate(_DEV_RELS):
            peer = (_flip(x, dx), _flip(y, dy), _flip(c, dc))
            pid = 4 * peer[0] + 2 * peer[1] + peer[2]
            k = i * len(_DEV_RELS) + j
            res.append(pltpu.make_async_remote_copy(
                src_ref=src.at[pid], dst_ref=land.at[pid if landed else me], send_sem=send.at[k],
                recv_sem=recv.at[k], device_id=peer, device_id_type=MESH))
    return res


def _split_start(bufs, n_src, copies, n_rel, name, after):
    n = len(bufs)
    nk = n_src * n_rel

    def body(*refs):
        ins, send, recv, token = refs[:n], refs[n + 1 + n], refs[n + 2 + n], refs[-1]
        for cp in copies(ins, send, recv, False):
            cp.start()
        token[...] = jnp.zeros_like(token)

    res = pl.pallas_call(
        body, name=name, in_specs=[_HBM] * n + [_ANY],
        out_specs=[_HBM] * n + [_SEM, _SEM, pl.BlockSpec(memory_space=pltpu.VMEM)],
        out_shape=[pltpu.HBM(b.shape, b.dtype) for b in bufs]
        + [pltpu.SemaphoreType.DMA((nk,)), pltpu.SemaphoreType.DMA((nk,)), jax.ShapeDtypeStruct((8, LANES), F32)],
        input_output_aliases={i: i for i in range(n)},
        compiler_params=pltpu.CompilerParams(has_side_effects=_EFFECT))(*[_in_hbm(b) for b in bufs], after)
    return res[n], res[n + 1], list(res[:n]), res[n + 2]


def _split_wait(bufs, send, recv, copies, name, after):
    n = len(bufs)

    def body(*refs):
        ins, send_ref, recv_ref = refs[:n], refs[n], refs[n + 1]
        for cp in copies(ins, send_ref, recv_ref, True):
            cp.wait_send()
            cp.wait_recv()

    return pl.pallas_call(
        body, name=name, in_specs=[_HBM] * n + [_SEM, _SEM, _ANY], out_specs=[_HBM] * n,
        out_shape=[pltpu.HBM(b.shape, b.dtype) for b in bufs], input_output_aliases={i: i for i in range(n)},
        compiler_params=pltpu.CompilerParams(has_side_effects=_EFFECT))(*bufs, send, recv, after)


def _gather_start(bufs, name, after):
    return _split_start(bufs, len(bufs), _gather_copies, len(_CHIP_RELS), name, after)


def _gather_wait(bufs, send, recv, name, after):
    return _split_wait(bufs, send, recv, _gather_copies, name, after)


def _scatter_start(srcs, name, after):
    n = len(srcs)
    lands = [lax.empty(a.shape, a.dtype) for a in srcs]
    fn = lambda refs, send, recv, landed: _scatter_copies(refs[:n], refs[n:], send, recv, landed)
    send, recv, bufs, token = _split_start(list(srcs) + lands, n, fn, len(_DEV_RELS), name, after)
    return send, recv, bufs, token


def _scatter_wait(bufs, send, recv, name, after):
    n = len(bufs) // 2
    fn = lambda refs, s, r, landed: _scatter_copies(refs[:n], refs[n:], s, r, landed)
    res = _split_wait(bufs, send, recv, fn, name, after)
    return res[:n], res[n:]


def _sum_segments(src, land, me, name):
    nd, seg, cols = src.shape
    ts = _tile(seg, (256, 176, 128))

    def body(m_ref, *refs):
        o_ref = refs[-1]
        acc = refs[0][...].astype(F32)
        for r in refs[1:-1]:
            acc = acc + r[...].astype(F32)
        o_ref[...] = acc

    def peer(rel):
        bits = 4 * rel[0] + 2 * rel[1] + rel[2]
        return pl.BlockSpec((None, ts, cols), lambda i, m: (jnp.bitwise_xor(m[0], bits), i, 0))

    grid_spec = pltpu.PrefetchScalarGridSpec(
        num_scalar_prefetch=1, grid=(seg // ts,),
        in_specs=[pl.BlockSpec((None, ts, cols), lambda i, m: (m[0], i, 0))] + [peer(r) for r in _DEV_RELS],
        out_specs=pl.BlockSpec((None, ts, cols), lambda i, m: (m[1], i, 0)))
    return pl.pallas_call(body, name=name, grid_spec=grid_spec,
                          out_shape=jax.ShapeDtypeStruct((2, seg, cols), F32),
                          compiler_params=_cparams(("parallel",), VMEM_MID))(me, src, *[land] * len(_DEV_RELS))


def _exchange_sibling(arrs):
    n = len(arrs)

    def body(*refs):
        outs, (send, recv) = refs[n:2 * n], refs[2 * n:]
        x, y, c = _coords()
        sib = (x, y, 1 - c)
        sends, recvs = [], []
        for i in range(n):
            cp = pltpu.make_async_remote_copy(src_ref=outs[i].at[c], dst_ref=outs[i].at[c], send_sem=send.at[i],
                                              recv_sem=recv.at[i], device_id=sib, device_id_type=MESH)
            cp.start()
            sends.append(cp)
            recvs.append(pltpu.make_async_remote_copy(src_ref=outs[i].at[c], dst_ref=outs[i].at[1 - c],
                                                      send_sem=send.at[i], recv_sem=recv.at[i], device_id=sib,
                                                      device_id_type=MESH))
        for cp in recvs:
            cp.wait_recv()
        for cp in sends:
            cp.wait_send()

    return pl.pallas_call(
        body, name="exchange_sibling", in_specs=[_ANY] * n, out_specs=[_ANY] * n,
        out_shape=[jax.ShapeDtypeStruct(a.shape, a.dtype) for a in arrs],
        input_output_aliases={i: i for i in range(n)},
        scratch_shapes=[pltpu.SemaphoreType.DMA((n,)), pltpu.SemaphoreType.DMA((n,))])(*arrs)


def _allreduce_small(vec):
    nd, rows, lanes = vec.shape
    nr = len(_DEV_RELS)

    def body(in_ref, out_ref, stage, red, send, recv):
        x, y, c = _coords()
        me = 4 * x + 2 * y + c
        peers = []
        for dx, dy, dc in _DEV_RELS:
            peer = (_flip(x, dx), _flip(y, dy), _flip(c, dc))
            peers.append((peer, 4 * peer[0] + 2 * peer[1] + peer[2]))

        def copy(src, dst, k, peer):
            return pltpu.make_async_remote_copy(src_ref=src, dst_ref=dst, send_sem=send.at[k], recv_sem=recv.at[k],
                                                device_id=peer, device_id_type=MESH)

        first = [copy(in_ref.at[pid], stage.at[me], j, peer) for j, (peer, pid) in enumerate(peers)]
        for cp in first:
            cp.start()
        stage[me] = in_ref[me]
        for j, (peer, pid) in enumerate(peers):
            copy(in_ref.at[pid], stage.at[pid], j, peer).wait_recv()
        acc = stage[0]
        for d in range(1, nd):
            acc = acc + stage[d]
        red[...] = acc
        out_ref[me] = acc
        second = [copy(red, out_ref.at[me], nr + j, peer) for j, (peer, pid) in enumerate(peers)]
        for cp in second:
            cp.start()
        for j, (peer, pid) in enumerate(peers):
            copy(red, out_ref.at[pid], nr + j, peer).wait_recv()
        for cp in first + second:
            cp.wait_send()

    vm = pl.BlockSpec(memory_space=pltpu.VMEM)
    return pl.pallas_call(
        body, name="allreduce_small", in_specs=[vm], out_specs=vm,
        out_shape=jax.ShapeDtypeStruct(vec.shape, F32),
        scratch_shapes=[pltpu.VMEM(vec.shape, F32), pltpu.VMEM((rows, lanes), F32),
                        pltpu.SemaphoreType.DMA((2 * nr,)), pltpu.SemaphoreType.DMA((2 * nr,))],
        compiler_params=_cparams(None, VMEM_MID))(vec)


def _adam_math(w, g, m, v):
    m2 = ADAM_B1 * m + (1.0 - ADAM_B1) * g
    v2 = ADAM_B2 * v + (1.0 - ADAM_B2) * (g * g)
    m_hat = m2 / (1.0 - ADAM_B1 ** ADAM_STEP)
    v_hat = v2 / (1.0 - ADAM_B2 ** ADAM_STEP)
    return -ADAM_LR * (m_hat / (jnp.sqrt(v_hat) + ADAM_EPS) + ADAM_WD * w), m2, v2


def _adamw_big(w3, m3, v3, layer, g, transposed, name, prev=None):
    nl, rows, cols = w3.shape
    tr = 128 if transposed else _tile(rows, (256, 176, 128))

    def body(w_ref, m_ref, v_ref, g_ref, *rest):
        go_ref, d_ref, mo_ref, vo_ref = rest[-4:]
        g_val = g_ref[...].T if transposed else g_ref[...]
        go_ref[...] = g_val
        d_ref[...], mo_ref[...], vo_ref[...] = _adam_math(w_ref[...], g_val, m_ref[...], v_ref[...])

    wspec = pl.BlockSpec((None, tr, cols), lambda i: (layer, i, 0))
    gspec = pl.BlockSpec((cols, tr), lambda i: (0, i)) if transposed else pl.BlockSpec((tr, cols), lambda i: (i, 0))
    extra = [] if prev is None else list(prev)
    return pl.pallas_call(body, name=name, grid=(rows // tr,),
                          in_specs=[wspec, wspec, wspec, gspec] + [_ANY] * len(extra),
                          out_specs=[wspec] * 4, out_shape=[jax.ShapeDtypeStruct((nl, rows, cols), F32)] * 4,
                          input_output_aliases={4 + i: i for i in range(len(extra))},
                          compiler_params=_cparams(("parallel",), VMEM_MID))(w3, m3, v3, g, *extra)


_SMALL = (
    ("e_norm_g", (1, 1024), None), ("e_mu", (1, SHIFT_COLS), None), ("e_w0", (1, RW), None),
    ("e_w2", (W_LORA, RW), 128), ("e_a0", (1, RW), None), ("e_a2", (A_LORA, RW), 128), ("e_g2", (G_LORA, RW), 128),
    ("e_k_k", (1, RW), None), ("e_k_a", (1, RW), None), ("e_r_k", (1, RW), None), ("e_ln_w", (1, RW), None),
    ("e_ln_b", (1, RW), None), ("e_conv_w", (4, LRU_W), 128), ("e_conv_b", (1, LRU_W), None),
    ("e_gate_a_w", (LRU_W, HEAD), None), ("e_gate_a_b", (1, LRU_W), None), ("e_gate_x_w", (LRU_W, HEAD), None),
    ("e_gate_x_b", (1, LRU_W), None), ("e_lru_lambda", (1, LRU_W), None), ("o_norm_g", (1, 1024), 256),
    ("o_A_re", (S5_GROUPS, S5_STATE), None), ("o_A_im", (S5_GROUPS, S5_STATE), None), ("o_log_dt", (1, S5_GROUPS), None),
    ("o_B_re", (S5_GROUPS, S5_STATE * S5_GROUP), None), ("o_B_im", (S5_GROUPS, S5_STATE * S5_GROUP), None),
    ("o_C_re", (S5_GROUPS * S5_GROUP, S5_STATE), None), ("o_C_im", (S5_GROUPS * S5_GROUP, S5_STATE), None),
    ("o_D", (1, 1024), 256), ("f_norm_g", (2, 1024), None), ("f_conv_w", (6, 2 * D_FF), 2 * D_FF // 4),
    ("f_conv_b", (2, 2 * D_FF), None), ("final_norm_g", (1, 1024), None))
_PIECES = {"f_norm_g": ((0, 1), (1, 1)), "f_conv_b": ((0, 1), (1, 1)), "f_conv_w": ((0, 3), (3, 3))}


def _ceil_to(n, m):
    return -(-n // m) * m


def _small_layout():
    groups = {}
    for name, (rows, cols), _ in _SMALL:
        for first, r in _PIECES.get(name, ((0, rows),)):
            groups.setdefault(cols, []).append((name, first, r))
    layout, off = {}, 0
    for cols, items in groups.items():
        stacks = [0, 0] if 2 * cols <= LANES else [0]
        placed = []
        for name, first, r in sorted(items, key=lambda it: -it[2]):
            half = stacks.index(min(stacks))
            r0 = stacks[half]
            if r >= 8 or r0 % 8 + r > 8:
                r0 = _ceil_to(r0, 8)
            placed.append((name, first, r, r0, half * (LANES // 2)))
            stacks[half] = r0 + r
        rpad = _ceil_to(max(stacks), 8)
        for name, first, r, at, lane in placed:
            layout[name, first] = (off, rpad, at, r, cols, lane)
        off += -(-cols // LANES) * rpad
    return layout, _ceil_to(off, 8 * N_DEV)


def _small_pack(gs):
    layout, total = _small_layout()
    keys = list(layout)

    def body(*refs):
        out = refs[-1]
        out[...] = jnp.zeros_like(out)
        for key, g_ref in zip(keys, refs[:-1]):
            off, rpad, at, r, cols, lane = layout[key]
            for j in range(-(-cols // LANES)):
                cw = min(LANES, cols - j * LANES)
                out[off + j * rpad + at:off + j * rpad + at + r, lane:lane + cw] = g_ref[:, j * LANES:j * LANES + cw]

    return pl.pallas_call(body, name="small_pack", out_shape=jax.ShapeDtypeStruct((total, LANES), F32),
                          compiler_params=_cparams(None, VMEM_MID))(*[gs[k] for k in keys])


def _adamw_small(red, chip, wts, ms, vs):
    layout, _ = _small_layout()
    names = [n for n, _, _ in _SMALL]
    n = len(names)

    def body(chip_ref, red_ref, *refs):
        ins, outs = refs[:3 * n], refs[3 * n:]
        c = chip_ref[0]
        for i, (name, (rows, cols), loc) in enumerate(_SMALL):
            w_ref, m_ref, v_ref = ins[3 * i:3 * i + 3]
            o_refs = outs[4 * i:4 * i + 4]
            width = cols if loc is None else loc
            for first, r in _PIECES.get(name, ((0, rows),)):
                off, rpad, at, _, _, lane = layout[name, first]
                for j in range(-(-width // LANES)):
                    cw = min(LANES, width - j * LANES)
                    ls = slice(lane, lane + cw)
                    if loc is None:
                        start = off + j * rpad + at
                        g = red_ref[start:start + r, ls]
                    else:
                        blk = c * (loc // LANES) + j
                        if r >= 8:
                            g = red_ref[pl.ds(pl.multiple_of(off + at + blk * rpad, 8), r), ls]
                        else:
                            tile = red_ref[pl.ds(pl.multiple_of(off + at // 8 * 8 + blk * rpad, 8), 8), ls]
                            g = tile[at % 8:at % 8 + r]
                    rs, cs = slice(first, first + r), slice(j * LANES, j * LANES + cw)
                    d, m2, v2 = _adam_math(w_ref[rs, cs], g, m_ref[rs, cs], v_ref[rs, cs])
                    for o, val in zip(o_refs, (g, d, m2, v2)):
                        o[rs, cs] = val

    args, shapes = [], []
    for name in names:
        args += [wts[name], ms[name], vs[name]]
        shapes += [jax.ShapeDtypeStruct(wts[name].shape, F32)] * 4
    vm = pl.BlockSpec(memory_space=pltpu.VMEM)
    res = pl.pallas_call(body, name="adamw_small",
                         in_specs=[pl.BlockSpec(memory_space=pltpu.SMEM), vm] + [vm] * (3 * n),
                         out_specs=[vm] * (4 * n), out_shape=shapes,
                         compiler_params=_cparams(None, VMEM_BIG))(chip, red, *args)
    return {name: res[4 * i:4 * i + 4] for i, name in enumerate(names)}


PACK_ROWS = 8


def _packed_rows(shape):
    size = 1
    for d in shape:
        size *= d
    return -(-size // (PACK_ROWS * LANES)) * PACK_ROWS


def _pack(arrs, row_mult):
    parts = []
    for a in arrs:
        flat = a.reshape(-1).astype(F32)
        rows = _packed_rows(a.shape)
        parts.append(jnp.pad(flat, (0, rows * LANES - flat.shape[0])).reshape(rows, LANES))
    total = sum(p.shape[0] for p in parts)
    fill = -(-total // row_mult) * row_mult - total
    if fill:
        parts.append(jnp.zeros((fill, LANES), F32))
    return jnp.concatenate(parts, axis=0)


def _unpack(packed, shapes):
    out, off = [], 0
    for s in shapes:
        rows = _packed_rows(s)
        size = 1
        for d in s:
            size *= d
        out.append(packed[off:off + rows].reshape(-1)[:size].reshape(s))
        off += rows
    return out


_SMALL_SH = ("e_w2", "e_a2", "e_g2", "e_conv_w", "o_norm_g", "o_D", "f_conv_w")
_LARGE = (("e_w_in", True), ("e_w_out", False), ("o_w_in", False), ("o_w_glu", True), ("f_w_up", True),
        ("f_w_down", False))
_ORDER = ("e_norm_g", "e_w_in", "e_mu", "e_w0", "e_w2", "e_a0", "e_a2", "e_g2", "e_k_k", "e_k_a", "e_r_k", "e_ln_w",
          "e_ln_b", "e_conv_w", "e_conv_b", "e_gate_a_w", "e_gate_a_b", "e_gate_x_w", "e_gate_x_b", "e_lru_lambda",
          "e_w_out", "o_norm_g", "o_w_in", "o_A_re", "o_A_im", "o_log_dt", "o_B_re", "o_B_im", "o_C_re", "o_C_im",
          "o_D", "o_w_glu", "f_norm_g", "f_w_up", "f_conv_w", "f_conv_b", "f_w_down", "final_norm_g")
N_CHIPS = 4
N_DEV = 8


def _step(x, tgt, wts, ms, vs):
    xi, yi, ci = _coords()
    chip = 2 * xi + yi
    chip1 = chip.astype(jnp.int32).reshape(1)
    me2 = jnp.stack([4 * xi + 2 * yi + ci, ci]).astype(jnp.int32)
    by_cols = dict(_LARGE)

    bufs = {(name, l): _cast_shard(wts[name], l, by_cols[name], chip1, f"cast_{name}{l}")
            for name, _ in _LARGE for l in range(wts[name].shape[0])}
    sh_shapes = [wts[n].shape for n in _SMALL_SH]
    packed = _pack([wts[n] for n in _SMALL_SH], 8)
    small_buf = lax.dynamic_update_slice(jnp.zeros((N_CHIPS,) + packed.shape, F32), packed[None], (chip, 0, 0))
    early = [("e_w_in", 0), ("e_w_out", 0)]
    late = [k for k in bufs if k not in early]
    send, recv, thru, token = _gather_start([bufs[k] for k in early] + [small_buf], "gather_start_a", x)
    got = _gather_wait(thru, send, recv, "gather_wait_a", token)
    send_b, recv_b, thru_b, token = _gather_start([bufs[k] for k in late], "gather_start_b", got[0])
    x, _ = lax.optimization_barrier((x, token))

    def rows(g):
        return g.reshape(N_CHIPS * g.shape[1], g.shape[2])

    full = {n: wts[n] for n, _, loc in _SMALL if loc is None}
    full["e_w_in_t"], full["e_w_out"] = rows(got[0]), rows(got[1])
    per_chip = [_unpack(got[2][k], sh_shapes) for k in range(N_CHIPS)]
    for i, n in enumerate(_SMALL_SH):
        full[n] = jnp.concatenate([per_chip[k][i] for k in range(N_CHIPS)], axis=-1)

    def late_weights(after):
        res = dict(zip(late, _gather_wait(thru_b, send_b, recv_b, "gather_wait_b", after)))
        return {"o_w_in": rows(res[("o_w_in", 0)]), "o_w_glu_t": rows(res[("o_w_glu", 0)]),
                "f_w_up_t": [rows(res[("f_w_up", l)]) for l in range(2)],
                "f_w_down": [rows(res[("f_w_down", l)]) for l in range(2)]}

    pending = []

    def send_grads(tag, items, carry):
        srcs = [g.reshape(N_DEV, g.shape[0] // N_DEV, g.shape[1]) for _, _, g in items]
        s_sem, r_sem, both, tok = _scatter_start(srcs, f"scatter_start_{tag}", carry)
        pending.append((tag, [(name, l) for name, l, _ in items], s_sem, r_sem, both))
        carry, _ = lax.optimization_barrier((carry, tok))
        return carry

    loss, grad_x, gs = _local_step(x, tgt, full, late_weights, send_grads)

    final = {}
    red = _allreduce_small(_small_pack(gs).reshape(N_DEV, -1, LANES)).reshape(-1, LANES)
    view = {name: (rows, cols if loc is None else loc) for name, (rows, cols), loc in _SMALL}
    as2d = lambda d: {name: d[name].reshape(view[name]) for name in view}
    small = _adamw_small(red, chip1, as2d(wts), as2d(ms), as2d(vs))
    for name, res in small.items():
        final[name] = [r.reshape(wts[name].shape) for r in res]
    new_v = small["final_norm_g"][3]

    halves, keys = [], []
    for tag, names, s_sem, r_sem, both in pending:
        srcs, lands = _scatter_wait(both, s_sem, r_sem, f"scatter_wait_{tag}", new_v)
        for (name, l), src, land in zip(names, srcs, lands):
            halves.append(_sum_segments(src, land, me2, f"sum_{name}{l}"))
            keys.append((name, l))
    shards = _exchange_sibling(halves)
    for s, (name, l) in zip(shards, keys):
        final[name] = _adamw_big(wts[name], ms[name], vs[name], l, s.reshape(2 * s.shape[1], s.shape[2]),
                                 by_cols[name], f"adamw_{name}{l}", prev=final.get(name))

    loss = lax.psum(loss[0, 0], ("x", "y", "c"))
    res = [loss, grad_x[None]]
    for k in range(4):
        res += [final[n][k] for n in _ORDER]
    return tuple(res)


def kernel(x, e_norm_g, e_w_in, e_mu, e_w0, e_w2, e_a0, e_a2, e_g2, e_k_k, e_k_a, e_r_k, e_ln_w, e_ln_b, e_conv_w, e_conv_b, e_gate_a_w, e_gate_a_b, e_gate_x_w, e_gate_x_b, e_lru_lambda, e_w_out, o_norm_g, o_w_in, o_A_re, o_A_im, o_log_dt, o_B_re, o_B_im, o_C_re, o_C_im, o_D, o_w_glu, f_norm_g, f_w_up, f_conv_w, f_conv_b, f_w_down, final_norm_g, loss_target, m_e_norm_g, m_e_w_in, m_e_mu, m_e_w0, m_e_w2, m_e_a0, m_e_a2, m_e_g2, m_e_k_k, m_e_k_a, m_e_r_k, m_e_ln_w, m_e_ln_b, m_e_conv_w, m_e_conv_b, m_e_gate_a_w, m_e_gate_a_b, m_e_gate_x_w, m_e_gate_x_b, m_e_lru_lambda, m_e_w_out, m_o_norm_g, m_o_w_in, m_o_A_re, m_o_A_im, m_o_log_dt, m_o_B_re, m_o_B_im, m_o_C_re, m_o_C_im, m_o_D, m_o_w_glu, m_f_norm_g, m_f_w_up, m_f_conv_w, m_f_conv_b, m_f_w_down, m_final_norm_g, v_e_norm_g, v_e_w_in, v_e_mu, v_e_w0, v_e_w2, v_e_a0, v_e_a2, v_e_g2, v_e_k_k, v_e_k_a, v_e_r_k, v_e_ln_w, v_e_ln_b, v_e_conv_w, v_e_conv_b, v_e_gate_a_w, v_e_gate_a_b, v_e_gate_x_w, v_e_gate_x_b, v_e_lru_lambda, v_e_w_out, v_o_norm_g, v_o_w_in, v_o_A_re, v_o_A_im, v_o_log_dt, v_o_B_re, v_o_B_im, v_o_C_re, v_o_C_im, v_o_D, v_o_w_glu, v_f_norm_g, v_f_w_up, v_f_conv_w, v_f_conv_b, v_f_w_down, v_final_norm_g):
    args = locals()
    wts = {n: args[n] for n in _ORDER}
    ms = {n: args["m_" + n] for n in _ORDER}
    vs = {n: args["v_" + n] for n in _ORDER}
    return _step(x[0], loss_target[0], wts, ms, vs)
```

```python
import functools

import jax
import jax.numpy as jnp
from jax import lax
from jax.experimental import pallas as pl
from jax.experimental.pallas import tpu as pltpu

F32 = jnp.float32
BF16 = jnp.bfloat16
MESH = pl.DeviceIdType.MESH

HEAD = 64
RW = 512
N_HEADS = RW // HEAD
LRU_W = 512
SHIFT_COLS = 1792
W_LORA, A_LORA, G_LORA = 64, 64, 128
S5_GROUPS, S5_GROUP, S5_STATE = 64, 16, 64
D_FF = 2816
NORM_EPS = 1e-6
GN_EPS = 64e-5
LRU_C = 8.0
ADAM_LR, ADAM_B1, ADAM_B2, ADAM_EPS, ADAM_WD, ADAM_STEP = 0.001, 0.9, 0.999, 1e-08, 0.01, 10

VMEM_BIG = 56 * 1024 * 1024
VMEM_MID = 40 * 1024 * 1024
LANES = 128
PT = 16
WKV_CHUNK = 32
S5_SLAB = 128


def _blocked(*args, **kw):
    call = pl.pallas_call(*args, **kw)

    def run(*ops):
        return call(*[pltpu.with_memory_space_constraint(a, pltpu.HBM) if a.ndim >= 2 else a for a in ops])

    return run


def _cparams(sem=None, vmem=None):
    kw = {}
    if sem is not None:
        kw["dimension_semantics"] = sem
    if vmem is not None:
        kw["vmem_limit_bytes"] = vmem
    return pltpu.CompilerParams(**kw)


def _tile(dim, cands):
    for c in cands:
        if dim % c == 0:
            return c
    return dim


def _full(shape):
    n = len(shape)
    return pl.BlockSpec(shape, lambda *_: (0,) * n)


_TILES = (2816, 2048, 1408, 1024, 512, 256, 128)
MM_BUDGET = 36 * 1024 * 1024
VMEM_SLACK = 12 * 1024 * 1024


MXU_FLOPS = 9.0e14
HBM_BYTES = 3.3e12
STEP_SECONDS = 0.35e-6


def _mm_tiles(m, n, k, size_a, size_b, size_o, has_add):
    best = None
    for tm in _TILES:
        for tk in _TILES:
            for tn in _TILES:
                if m % tm or n % tn or k % tk:
                    continue
                need = 2 * (tm * tk * size_a + tk * tn * size_b + tm * tn * size_o) + tm * tn * 4 * (1 + 2 * has_add)
                if k > tk:
                    need += tm * tn * 4
                if need > MM_BUDGET:
                    continue
                steps = (m // tm) * (n // tn) * (k // tk)
                a_reads = n // tn if k > tk else 1
                moved = (m * k * size_a * a_reads + k * n * size_b * (m // tm) + m * n * (size_o + 4 * has_add))
                cost = max(2.0 * m * n * k / MXU_FLOPS, moved / HBM_BYTES) + steps * STEP_SECONDS
                cand = (-cost, tk, tm, tn)
                if best is None or cand > best[0]:
                    best = (cand, need)
    (_, tk, tm, tn), need = best
    return tm, tn, tk, need


def _matmul(a, b, mode, name, out_dtype=F32, add=None):
    if mode == "nn":
        (m, k), (k2, n) = a.shape, b.shape
    elif mode == "nt":
        (m, k), (n, k2) = a.shape, b.shape
    else:
        (k, m), (k2, n) = a.shape, b.shape
    assert k == k2, (a.shape, b.shape, mode)
    tm, tn, tk, need = _mm_tiles(m, n, k, a.dtype.itemsize, b.dtype.itemsize, jnp.dtype(out_dtype).itemsize,
                                 add is not None)
    nk = k // tk
    dims = {"nn": (((1,), (0,)), ((), ())), "nt": (((1,), (1,)), ((), ())), "tn": (((0,), (0,)), ((), ()))}[mode]

    def body(*refs):
        a_ref, b_ref = refs[:2]
        add_ref = refs[2] if add is not None else None
        o_ref = refs[3] if add is not None else refs[2]
        part = lax.dot_general(a_ref[...].astype(BF16), b_ref[...].astype(BF16), dims, preferred_element_type=F32)

        def finish(r):
            if add_ref is not None:
                r = r + add_ref[...]
            o_ref[...] = r.astype(o_ref.dtype)

        if nk == 1:
            finish(part)
            return
        acc = refs[-1]
        kk = pl.program_id(2)

        @pl.when(kk == 0)
        def _():
            acc[...] = part

        @pl.when(kk > 0)
        def _():
            acc[...] += part

        @pl.when(kk == nk - 1)
        def _():
            finish(acc[...])

    if mode == "nn":
        a_spec = pl.BlockSpec((tm, tk), lambda i, j, kk: (i, kk))
        b_spec = pl.BlockSpec((tk, tn), lambda i, j, kk: (kk, j))
    elif mode == "nt":
        a_spec = pl.BlockSpec((tm, tk), lambda i, j, kk: (i, kk))
        b_spec = pl.BlockSpec((tn, tk), lambda i, j, kk: (j, kk))
    else:
        a_spec = pl.BlockSpec((tk, tm), lambda i, j, kk: (kk, i))
        b_spec = pl.BlockSpec((tk, tn), lambda i, j, kk: (kk, j))
    o_spec = pl.BlockSpec((tm, tn), lambda i, j, kk: (i, j))
    in_specs = [a_spec, b_spec] + ([o_spec] if add is not None else [])
    args = (a, b) + ((add,) if add is not None else ())
    return _blocked(
        body, name=name, grid=(m // tm, n // tn, nk),
        in_specs=in_specs, out_specs=o_spec,
        out_shape=jax.ShapeDtypeStruct((m, n), out_dtype),
        scratch_shapes=[pltpu.VMEM((tm, tn), F32)] if nk > 1 else [],
        compiler_params=_cparams(("parallel", "parallel", "arbitrary"), min(VMEM_BIG, need + VMEM_SLACK)),
    )(*args)


TOK = 256


def _rms(x, g):
    return x * lax.rsqrt(jnp.mean(x * x, axis=-1, keepdims=True) + NORM_EPS) * g


def _rms_fwd(x, g, name):
    t, d = x.shape

    def body(x_ref, g_ref, o_ref):
        o_ref[...] = _rms(x_ref[...], g_ref[...]).astype(BF16)

    row = pl.BlockSpec((TOK, d), lambda i: (i, 0))
    return _blocked(body, name=name, grid=(t // TOK,), in_specs=[row, _full((1, d))], out_specs=row,
                          out_shape=jax.ShapeDtypeStruct((t, d), BF16),
                          compiler_params=_cparams(("parallel",)))(x, g)


def _rms_bwd(x, g, dxn, res, name):
    t, d = x.shape

    def body(x_ref, g_ref, d_ref, res_ref, dx_ref, dg_ref):
        _, vjp = jax.vjp(_rms, x_ref[...], g_ref[...])
        dx, dg = vjp(d_ref[...].astype(F32))
        dx_ref[...] = dx + res_ref[...]

        @pl.when(pl.program_id(0) == 0)
        def _():
            dg_ref[...] = jnp.zeros_like(dg_ref)

        dg_ref[...] += dg

    row = pl.BlockSpec((TOK, d), lambda i: (i, 0))
    return _blocked(body, name=name, grid=(t // TOK,), in_specs=[row, _full((1, d)), row, row],
                          out_specs=[row, _full((1, d))],
                          out_shape=[jax.ShapeDtypeStruct((t, d), F32), jax.ShapeDtypeStruct((1, d), F32)],
                          compiler_params=_cparams(("arbitrary",)))(x, g, dxn, res)


def _loss_head(x, g, tgt):
    t, d = x.shape

    def body(x_ref, g_ref, t_ref, l_ref, dx_ref, dg_ref):
        tg = t_ref[...]

        def fn(xv, gv):
            err = _rms(xv, gv) - tg
            per_tok = jnp.mean(err * err, axis=-1, keepdims=True)
            return 0.5 * jnp.sum(per_tok, axis=0, keepdims=True)

        l, vjp = jax.vjp(fn, x_ref[...], g_ref[...])
        dx, dg = vjp(jnp.ones((1, 1), F32))
        dx_ref[...] = dx

        @pl.when(pl.program_id(0) == 0)
        def _():
            dg_ref[...] = jnp.zeros_like(dg_ref)
            l_ref[...] = jnp.zeros_like(l_ref)

        dg_ref[...] += dg
        l_ref[...] += jnp.broadcast_to(l, l_ref.shape)

    row = pl.BlockSpec((TOK, d), lambda i: (i, 0))
    return _blocked(body, name="loss_head", grid=(t // TOK,), in_specs=[row, _full((1, d)), row],
                          out_specs=[_full((1, LANES)), row, _full((1, d))],
                          out_shape=[jax.ShapeDtypeStruct((1, LANES), F32), jax.ShapeDtypeStruct((t, d), F32),
                                     jax.ShapeDtypeStruct((1, d), F32)],
                          compiler_params=_cparams(("arbitrary",)))(x, g, tgt)


def _glu_fwd(x, z):
    t, d = x.shape

    def body(x_ref, v_ref, g_ref, o_ref):
        o_ref[...] = x_ref[...] + v_ref[...] * jax.nn.sigmoid(g_ref[...])

    row = pl.BlockSpec((TOK, d), lambda i: (i, 0))
    gate = pl.BlockSpec((TOK, d), lambda i: (i, 1))
    return _blocked(body, name="glu_fwd", grid=(t // TOK,), in_specs=[row, row, gate], out_specs=row,
                          out_shape=jax.ShapeDtypeStruct((t, d), F32),
                          compiler_params=_cparams(("parallel",)))(x, z, z)


def _glu_bwd(z, g):
    t, d = g.shape

    def body(v_ref, g_ref, d_ref, o_ref):
        s = jax.nn.sigmoid(g_ref[...])
        dy = d_ref[...]
        o_ref[:, :d] = (dy * s).astype(BF16)
        o_ref[:, d:] = (dy * v_ref[...] * s * (1.0 - s)).astype(BF16)

    row = pl.BlockSpec((TOK, d), lambda i: (i, 0))
    gate = pl.BlockSpec((TOK, d), lambda i: (i, 1))
    return _blocked(body, name="glu_bwd", grid=(t // TOK,), in_specs=[row, gate, row],
                          out_specs=pl.BlockSpec((TOK, 2 * d), lambda i: (i, 0)),
                          out_shape=jax.ShapeDtypeStruct((t, 2 * d), BF16),
                          compiler_params=_cparams(("parallel",)))(z, z, g)


def _shift_down(x, d):
    row = lax.broadcasted_iota(jnp.int32, x.shape, 0)
    return jnp.where(row < d, 0.0, pltpu.roll(x, d, 0))


def _shift_up(x, d):
    n = x.shape[0]
    row = lax.broadcasted_iota(jnp.int32, x.shape, 0)
    return jnp.where(row >= n - d, 0.0, pltpu.roll(x, n - d, 0))


def _make_sd():
    @functools.partial(jax.custom_vjp, nondiff_argnums=(1,))
    def sd(x, d):
        return _shift_down(x, d)

    def fwd(x, d):
        return _shift_down(x, d), None

    def bwd(d, _, g):
        return (_shift_up(g, d),)

    sd.defvjp(fwd, bwd)
    return sd


def _lin_scan(a, u, reverse=False):
    n = a.shape[0]
    row = lax.broadcasted_iota(jnp.int32, a.shape, 0)
    d = 1
    while d < n:
        if reverse:
            keep = row < n - d
            a_s, u_s = pltpu.roll(a, n - d, 0), pltpu.roll(u, n - d, 0)
        else:
            keep = row >= d
            a_s, u_s = pltpu.roll(a, d, 0), pltpu.roll(u, d, 0)
        u = u + a * jnp.where(keep, u_s, 0.0)
        a = a * jnp.where(keep, a_s, 1.0)
        d *= 2
    return u


def _make_scan():
    @jax.custom_vjp
    def scan(a, u):
        return _lin_scan(a, u)

    def fwd(a, u):
        h = _lin_scan(a, u)
        return h, (a, h)

    def bwd(res, dh):
        a, h = res
        g = _lin_scan(_shift_up(a, 1), dh, reverse=True)
        return g * _shift_down(h, 1), g

    scan.defvjp(fwd, bwd)
    return scan


def _acc_out(ref, val):
    @pl.when(pl.program_id(0) == 0)
    def _():
        ref[...] = jnp.zeros_like(ref)

    ref[...] += val


FFN_CW = 128


def _ffn_fn(hg, hv, wg, wv, bg, bv, sd):
    cg = wg[0:1] * sd(hg, 2) + wg[1:2] * sd(hg, 1) + wg[2:3] * hg + bg
    cv = wv[0:1] * sd(hv, 2) + wv[1:2] * sd(hv, 1) + wv[2:3] * hv + bv
    return jax.nn.silu(cg) * cv


def _ffn_specs(t):
    nb = D_FF // FFN_CW
    col = lambda r, off: pl.BlockSpec((r, FFN_CW), lambda j: (0, j + off))
    return nb, [col(t, 0), col(t, nb), col(3, 0), col(3, nb), col(1, 0), col(1, nb)], col


def _ffn_mid_fwd(h, cw, cb, name):
    t = h.shape[0]
    nb, in_specs, col = _ffn_specs(t)

    def body(hg, hv, wg, wv, bg, bv, o_ref):
        o_ref[...] = _ffn_fn(hg[...], hv[...], wg[...], wv[...], bg[...], bv[...], _shift_down).astype(BF16)

    return _blocked(body, name=name, grid=(nb,), in_specs=in_specs, out_specs=col(t, 0),
                          out_shape=jax.ShapeDtypeStruct((t, D_FF), BF16),
                          compiler_params=_cparams(("parallel",), VMEM_MID))(h, h, cw, cw, cb, cb)


def _ffn_mid_bwd(h, cw, cb, dact, name):
    t = h.shape[0]
    nb, in_specs, col = _ffn_specs(t)

    def body(hg, hv, wg, wv, bg, bv, d_ref, dhg, dhv, dwg, dwv, dbg, dbv):
        fn = functools.partial(_ffn_fn, sd=_make_sd())
        _, vjp = jax.vjp(fn, hg[...], hv[...], wg[...], wv[...], bg[...], bv[...])
        g = vjp(d_ref[...])
        dhg[...] = g[0].astype(BF16)
        dhv[...] = g[1].astype(BF16)
        dwg[...], dwv[...], dbg[...], dbv[...] = g[2], g[3], g[4], g[5]

    big = jax.ShapeDtypeStruct((t, D_FF), BF16)
    w3 = jax.ShapeDtypeStruct((3, D_FF), F32)
    b1 = jax.ShapeDtypeStruct((1, D_FF), F32)
    return _blocked(body, name=name, grid=(nb,), in_specs=in_specs + [col(t, 0)],
                          out_specs=[col(t, 0), col(t, 0), col(3, 0), col(3, 0), col(1, 0), col(1, 0)],
                          out_shape=[big, big, w3, w3, b1, b1],
                          compiler_params=_cparams(("parallel",), VMEM_BIG))(h, h, cw, cw, cb, cb, dact)


TS_CW = 256


def _tshift_fn(p, mu, sd):
    return p + mu * (sd(p, 1) - p)


def _tshift_fwd(p, mu):
    t = p.shape[0]
    col = lambda r: pl.BlockSpec((r, TS_CW), lambda j: (0, j))

    def body(p_ref, mu_ref, o_ref):
        o_ref[...] = _tshift_fn(p_ref[...], mu_ref[...], _shift_down)

    return _blocked(body, name="tshift_fwd", grid=(SHIFT_COLS // TS_CW,), in_specs=[col(t), col(1)],
                          out_specs=col(t), out_shape=jax.ShapeDtypeStruct((t, SHIFT_COLS), F32),
                          compiler_params=_cparams(("parallel",), VMEM_MID))(p, mu)


def _tshift_bwd(p, mu, dpam):
    t = p.shape[0]
    col = lambda r: pl.BlockSpec((r, TS_CW), lambda j: (0, j))

    def body(p_ref, mu_ref, d_ref, dp_ref, dmu_ref):
        _, vjp = jax.vjp(functools.partial(_tshift_fn, sd=_make_sd()), p_ref[...], mu_ref[...])
        dp, dmu = vjp(d_ref[...])
        dp_ref[...] = dp.astype(BF16)
        dmu_ref[...] = dmu

    return _blocked(body, name="tshift_bwd", grid=(SHIFT_COLS // TS_CW,), in_specs=[col(t), col(1), col(t)],
                          out_specs=[col(t), col(1)],
                          out_shape=[jax.ShapeDtypeStruct((t, SHIFT_COLS), BF16),
                                     jax.ShapeDtypeStruct((1, SHIFT_COLS), F32)],
                          compiler_params=_cparams(("parallel",), VMEM_MID))(p, mu, dpam)


_HI = lax.Precision.HIGHEST
_O = (0, RW, 2 * RW, 3 * RW, 3 * RW + W_LORA, 3 * RW + W_LORA + A_LORA, SHIFT_COLS)


def _dot16(a, b, dims=(((1,), (0,)), ((), ()))):
    return lax.dot_general(a.astype(BF16), b.astype(BF16), dims, preferred_element_type=F32)


def _make_dot16():
    @jax.custom_vjp
    def dot(a, b):
        return _dot16(a, b)

    def fwd(a, b):
        return _dot16(a, b), (a, b)

    def bwd(res, g):
        a, b = res
        return _dot16(g, b, (((1,), (1,)), ((), ()))), _dot16(a, g, (((0,), (0,)), ((), ())))

    dot.defvjp(fwd, bwd)
    return dot


def _seg(x, gm):
    return jnp.dot(x, gm, precision=_HI)


def _prep_fn(r, k, v, wd, ad, gd, w0, w2, a0, a2, g2, k_k, k_a, gm, dot):
    w_log = -jax.nn.softplus(-(w0 + dot(jnp.tanh(wd), w2))) - 0.5
    decay = jnp.exp(-jnp.exp(w_log))
    a = jax.nn.sigmoid(a0 + dot(ad, a2))
    g = dot(jax.nn.sigmoid(gd), g2)
    kk = k * k_k
    kk = kk / jnp.maximum(jnp.sqrt(_seg(kk * kk, gm)), 1e-12)
    k2 = k * (1.0 + (a - 1.0) * k_a)
    return r, decay, k2, v, -kk, kk * a, g


_PREP_W = ("w0", "w2", "a0", "a2", "g2", "k_k", "k_a")


def _prep_wspecs(w):
    return [_full(w[n].shape) for n in _PREP_W] + [_full((RW, RW))]


def _rwkv_prep_fwd(pam, w, gm):
    t = pam.shape[0]

    def body(p_ref, *refs):
        wr, outs = refs[:8], refs[8:]
        pieces = [p_ref[:, _O[i]:_O[i + 1]] for i in range(6)]
        res = _prep_fn(*pieces, *[x[...] for x in wr], _dot16)
        for o, val in zip(outs, res):
            o[...] = val

    row = lambda c: pl.BlockSpec((TOK, c), lambda i: (i, 0))
    return _blocked(body, name="rwkv_prep_fwd", grid=(t // TOK,),
                          in_specs=[row(SHIFT_COLS)] + _prep_wspecs(w), out_specs=[row(RW)] * 7,
                          out_shape=[jax.ShapeDtypeStruct((t, RW), F32)] * 7,
                          compiler_params=_cparams(("parallel",), VMEM_MID))(pam, *[w[n] for n in _PREP_W], gm)


def _rwkv_prep_bwd(pam, w, gm, cts, more):
    t = pam.shape[0]

    def body(p_ref, *refs):
        wr, ct, ex, dp_ref, dws = refs[:8], refs[8:15], refs[15:18], refs[18], refs[19:]
        pieces = [p_ref[:, _O[i]:_O[i + 1]] for i in range(6)]
        fn = lambda *a: _prep_fn(*a, wr[7][...], _make_dot16())
        _, vjp = jax.vjp(fn, *pieces, *[x[...] for x in wr[:7]])
        c = [x[...] for x in ct]
        c[0] = c[0] + ex[0][...]
        c[2] = c[2] + ex[1][...]
        c[3] = c[3] + ex[2][...]
        g = vjp(tuple(c))
        for i in range(6):
            dp_ref[:, _O[i]:_O[i + 1]] = g[i]
        for o, val in zip(dws, g[6:]):
            _acc_out(o, val)

    row = lambda c: pl.BlockSpec((TOK, c), lambda i: (i, 0))
    return _blocked(body, name="rwkv_prep_bwd", grid=(t // TOK,),
                          in_specs=[row(SHIFT_COLS)] + _prep_wspecs(w) + [row(RW)] * 10,
                          out_specs=[row(SHIFT_COLS)] + [_full(w[n].shape) for n in _PREP_W],
                          out_shape=[jax.ShapeDtypeStruct((t, SHIFT_COLS), F32)]
                          + [jax.ShapeDtypeStruct(w[n].shape, F32) for n in _PREP_W],
                          compiler_params=_cparams(("arbitrary",), VMEM_MID))(
                              pam, *[w[n] for n in _PREP_W], gm, *cts, *more)


def _post_fn(y, r, k2, v, g, ln_w, ln_b, r_k, gm):
    inv = 1.0 / HEAD
    d = y - _seg(y, gm) * inv
    yn = d * lax.rsqrt(_seg(d * d, gm) * inv + GN_EPS) * ln_w + ln_b
    bonus = _seg(r * k2 * r_k, gm) * v
    return (yn + bonus) * g


def _rwkv_post_fwd(y, r, k2, v, g, ln_w, ln_b, r_k, gm):
    t = y.shape[0]

    def body(*refs):
        o_ref = refs[-1]
        o_ref[...] = _post_fn(*[x[...] for x in refs[:-1]]).astype(BF16)

    row = pl.BlockSpec((TOK, RW), lambda i: (i, 0))
    return _blocked(body, name="rwkv_post_fwd", grid=(t // TOK,),
                          in_specs=[row] * 5 + [_full((1, RW))] * 3 + [_full((RW, RW))], out_specs=row,
                          out_shape=jax.ShapeDtypeStruct((t, RW), BF16),
                          compiler_params=_cparams(("parallel",), VMEM_MID))(y, r, k2, v, g, ln_w, ln_b, r_k, gm)


def _rwkv_post_bwd(y, r, k2, v, g, ln_w, ln_b, r_k, gm, dya):
    t = y.shape[0]

    def body(*refs):
        ins, gm_ref, d_ref, outs = refs[:8], refs[8], refs[9], refs[10:]
        fn = lambda *a: _post_fn(*a, gm_ref[...])
        _, vjp = jax.vjp(fn, *[x[...] for x in ins])
        gr = vjp(d_ref[...])
        for o, val in zip(outs[:5], gr[:5]):
            o[...] = val
        for o, val in zip(outs[5:], gr[5:]):
            _acc_out(o, val)

    row = pl.BlockSpec((TOK, RW), lambda i: (i, 0))
    vec = _full((1, RW))
    return _blocked(body, name="rwkv_post_bwd", grid=(t // TOK,),
                          in_specs=[row] * 5 + [vec] * 3 + [_full((RW, RW)), row],
                          out_specs=[row] * 5 + [vec] * 3,
                          out_shape=[jax.ShapeDtypeStruct((t, RW), F32)] * 5 + [jax.ShapeDtypeStruct((1, RW), F32)] * 3,
                          compiler_params=_cparams(("arbitrary",), VMEM_MID))(y, r, k2, v, g, ln_w, ln_b, r_k, gm, dya)


def _from_pt(x):
    n = x.shape[0]
    return x.reshape(n, HEAD, N_HEADS, PT).transpose(0, 3, 2, 1).reshape(n * PT, N_HEADS * HEAD)


def _lane_sum(x):
    return jnp.sum(x, axis=-1, keepdims=True)


def _pair_consts():
    lane = lax.broadcasted_iota(jnp.int32, (HEAD, LANES), 1)
    return lane, lane < HEAD


def _seg_sum_pair(x, first):
    return jnp.where(first, _lane_sum(jnp.where(first, x, 0.0)), _lane_sum(jnp.where(first, 0.0, x)))


def _to_pt(x):
    t = x.shape[0]
    return x.reshape(t // PT, PT, N_HEADS, HEAD).transpose(0, 3, 2, 1).reshape(t // PT, HEAD, N_HEADS * PT)


def _expand_cols(x, name):
    t = x.shape[0]
    tiles = WKV_CHUNK // PT

    def body(x_ref, o_ref):
        _, first = _pair_consts()
        for tl in range(tiles):
            tile = x_ref[tl]
            for j in range(PT):
                for p in range(N_HEADS // 2):
                    src = jnp.where(first, (2 * p) * PT + j, (2 * p + 1) * PT + j)
                    o_ref[tl * PT + j, :, p * LANES:(p + 1) * LANES] = jnp.take_along_axis(tile, src, axis=1)

    return _blocked(
        body, name=name, grid=(t // WKV_CHUNK,),
        in_specs=[pl.BlockSpec((tiles, HEAD, LANES), lambda i: (i, 0, 0))],
        out_specs=pl.BlockSpec((WKV_CHUNK, HEAD, RW), lambda i: (i, 0, 0)),
        out_shape=jax.ShapeDtypeStruct((t, HEAD, RW), F32),
        compiler_params=_cparams(("parallel",), VMEM_MID))(_to_pt(x))


def _wkv_fwd(w, k, z, b, v_exp):
    t = w.shape[0]
    nc = t // WKV_CHUNK
    pairs = N_HEADS // 2

    def body(w_ref, k_ref, z_ref, b_ref, v_ref, s_all, s_ref):
        @pl.when(pl.program_id(0) == 0)
        def _():
            s_ref[...] = jnp.zeros_like(s_ref)

        _, first = _pair_consts()

        def group(gi, carry):
            base = pl.multiple_of(gi * 8, 8)
            rows = [ref[pl.ds(base, 8), :] for ref in (w_ref, k_ref, z_ref, b_ref)]
            s = [s_ref[:, p * LANES:(p + 1) * LANES] for p in range(pairs)]
            for jj in range(8):
                for p in range(pairs):
                    cs = slice(p * LANES, (p + 1) * LANES)
                    wr, kr, zr, br = [x[jj:jj + 1, cs] for x in rows]
                    s_all[base + jj, :, cs] = s[p]
                    sa = _seg_sum_pair(s[p] * zr, first)
                    s[p] = s[p] * wr + sa * br + v_ref[base + jj, :, cs] * kr
            for p in range(pairs):
                s_ref[:, p * LANES:(p + 1) * LANES] = s[p]
            return carry

        lax.fori_loop(0, WKV_CHUNK // 8, group, 0)

    row = pl.BlockSpec((WKV_CHUNK, RW), lambda i: (i, 0))
    big = pl.BlockSpec((WKV_CHUNK, HEAD, RW), lambda i: (i, 0, 0))
    return _blocked(
        body, name="wkv_fwd", grid=(nc,), in_specs=[row] * 4 + [big], out_specs=[big, _full((HEAD, RW))],
        out_shape=[jax.ShapeDtypeStruct((t, HEAD, RW), F32), jax.ShapeDtypeStruct((HEAD, RW), F32)],
        compiler_params=_cparams(("arbitrary",), VMEM_MID))(w, k, z, b, v_exp)


def _wkv_out(r, s_all, s_last):
    t = r.shape[0]
    nc = t // WKV_CHUNK
    tiles = WKV_CHUNK // PT
    pairs = N_HEADS // 2

    def body(r_ref, s_ref, nxt_ref, last_ref, y_ref):
        lane, first = _pair_consts()
        after = jnp.where(pl.program_id(0) == nc - 1, last_ref[...], nxt_ref[0])
        for tl in range(tiles):
            ytile = jnp.zeros((HEAD, LANES), F32)
            for g in range(PT // 8):
                rows = r_ref[tl * PT + g * 8:tl * PT + g * 8 + 8, :]
                for jj in range(8):
                    tt = tl * PT + g * 8 + jj
                    j = g * 8 + jj
                    for p in range(pairs):
                        cs = slice(p * LANES, (p + 1) * LANES)
                        s = s_ref[tt + 1, :, cs] if tt + 1 < WKV_CHUNK else after[:, cs]
                        pr = s * rows[jj:jj + 1, cs]
                        y0 = _lane_sum(jnp.where(first, pr, 0.0))
                        y1 = _lane_sum(jnp.where(first, 0.0, pr))
                        ytile = jnp.where(lane == (2 * p) * PT + j, y0, ytile)
                        ytile = jnp.where(lane == (2 * p + 1) * PT + j, y1, ytile)
            y_ref[tl] = ytile

    row = pl.BlockSpec((WKV_CHUNK, RW), lambda i: (i, 0))
    pt = pl.BlockSpec((tiles, HEAD, LANES), lambda i: (i, 0, 0))
    big = pl.BlockSpec((WKV_CHUNK, HEAD, RW), lambda i: (i, 0, 0))
    nxt = pl.BlockSpec((1, HEAD, RW), lambda i: (jnp.minimum((i + 1) * WKV_CHUNK, t - 1), 0, 0))
    return _blocked(
        body, name="wkv_out", grid=(nc,), in_specs=[row, big, nxt, _full((HEAD, RW))], out_specs=pt,
        out_shape=jax.ShapeDtypeStruct((t // PT, HEAD, LANES), F32),
        compiler_params=_cparams(("parallel",), VMEM_MID))(r, s_all, s_all, s_last)


def _wkv_bwd(r, w, k, z, b, v_exp, s_all, dy_exp):
    t = r.shape[0]
    nc = t // WKV_CHUNK
    tiles = WKV_CHUNK // PT
    pairs = N_HEADS // 2

    def body(r_ref, w_ref, k_ref, z_ref, b_ref, v_ref, s_all_ref, dy_ref,
             dr_ref, dw_ref, dk_ref, dz_ref, db_ref, dv_ref, ds_ref):
        @pl.when(pl.program_id(0) == 0)
        def _():
            ds_ref[...] = jnp.zeros_like(ds_ref)

        lane, first = _pair_consts()
        col_sum = lambda x: jnp.sum(x, axis=0, keepdims=True)
        row8 = lax.broadcasted_iota(jnp.int32, (8, LANES), 0)
        for tl in reversed(range(tiles)):
            def group(gg, dvtile):
                gi = PT // 8 - 1 - gg
                base = pl.multiple_of(tl * PT + gi * 8, 8)
                rows = [ref[pl.ds(base, 8), :] for ref in (r_ref, w_ref, k_ref, z_ref, b_ref)]
                outs = (dr_ref, dw_ref, dk_ref, dz_ref, db_ref)
                tiles8 = {(id(o), p): jnp.zeros((8, LANES), F32) for o in outs for p in range(pairs)}
                ds = [ds_ref[:, p * LANES:(p + 1) * LANES] for p in range(pairs)]
                for jj in reversed(range(8)):
                    j = gi * 8 + jj
                    for p in range(pairs):
                        cs = slice(p * LANES, (p + 1) * LANES)

                        def put(ref, val, p=p, jj=jj):
                            tiles8[(id(ref), p)] = jnp.where(row8 == jj, val, tiles8[(id(ref), p)])

                        rr, wr, kr, zr, br = [x[jj:jj + 1, cs] for x in rows]
                        sp = s_all_ref[base + jj, :, cs]
                        vc = v_ref[base + jj, :, cs]
                        dyc = dy_ref[base + jj, :, cs]
                        sa = _seg_sum_pair(sp * zr, first)
                        st = sp * wr + sa * br + vc * kr
                        d = ds[p] + dyc * rr
                        put(dr_ref, col_sum(st * dyc))
                        dvk = d * kr
                        dv0 = _lane_sum(jnp.where(first, dvk, 0.0))
                        dv1 = _lane_sum(jnp.where(first, 0.0, dvk))
                        dvtile = jnp.where(lane == (2 * p) * PT + j, dv0, dvtile)
                        dvtile = jnp.where(lane == (2 * p + 1) * PT + j, dv1, dvtile)
                        put(dk_ref, col_sum(d * vc))
                        put(dw_ref, col_sum(sp * d))
                        u = _seg_sum_pair(d * br, first)
                        put(dz_ref, col_sum(sp * u))
                        put(db_ref, col_sum(d * sa))
                        ds[p] = d * wr + u * zr
                for p in range(pairs):
                    ds_ref[:, p * LANES:(p + 1) * LANES] = ds[p]
                for o in outs:
                    for p in range(pairs):
                        o[pl.ds(base, 8), p * LANES:(p + 1) * LANES] = tiles8[(id(o), p)]
                return dvtile

            dv_ref[tl] = lax.fori_loop(0, PT // 8, group, jnp.zeros((HEAD, LANES), F32))

    rev = lambda i: nc - 1 - i
    row = pl.BlockSpec((WKV_CHUNK, RW), lambda i: (rev(i), 0))
    pt = pl.BlockSpec((tiles, HEAD, LANES), lambda i: (rev(i), 0, 0))
    big = pl.BlockSpec((WKV_CHUNK, HEAD, RW), lambda i: (rev(i), 0, 0))
    return _blocked(
        body, name="wkv_bwd", grid=(nc,), in_specs=[row] * 5 + [big, big, big], out_specs=[row] * 5 + [pt],
        out_shape=[jax.ShapeDtypeStruct((t, RW), F32)] * 5 + [jax.ShapeDtypeStruct((t // PT, HEAD, LANES), F32)],
        scratch_shapes=[pltpu.VMEM((HEAD, RW), F32)],
        compiler_params=_cparams(("arbitrary",), VMEM_BIG))(r, w, k, z, b, v_exp, s_all, dy_exp)


LRU_CW = 128
_BX0 = SHIFT_COLS // LRU_CW
_BG0 = (SHIFT_COLS + LRU_W) // LRU_CW


def _lru_fn(bx, bg, cw, cb, ga, ba, gx, bxb, lam, sd, scan, dot):
    xc = cw[0:1] * sd(bx, 3) + cw[1:2] * sd(bx, 2) + cw[2:3] * sd(bx, 1) + cw[3:4] * bx + cb
    gr = jax.nn.sigmoid(dot(xc, ga) + ba)
    gi = jax.nn.sigmoid(dot(xc, gx) + bxb)
    log_a = -LRU_C * gr * jax.nn.softplus(-lam)
    a = jnp.exp(log_a)
    mult = jnp.sqrt(-jnp.tanh(log_a) * (jnp.exp(2.0 * log_a) + 1.0))
    return scan(a, xc * gi * mult) * jax.nn.gelu(bg)


def _lru_specs(t):
    col = lambda r, off=0: pl.BlockSpec((r, LRU_CW), lambda j: (0, j + off))
    diag = pl.BlockSpec((LRU_CW, LRU_CW), lambda j: (j, j))
    return col, [col(t, _BX0), col(t, _BG0), col(4), col(1), diag, col(1), diag, col(1), col(1)]


def _lru_fwd(p, cw, cb, ga, ba, gx, bxb, lam):
    t = p.shape[0]
    col, in_specs = _lru_specs(t)

    def body(*refs):
        o_ref = refs[-1]
        o_ref[...] = _lru_fn(*[x[...] for x in refs[:-1]], _shift_down, _lin_scan, _dot16).astype(BF16)

    return _blocked(body, name="lru_fwd", grid=(LRU_W // LRU_CW,), in_specs=in_specs, out_specs=col(t),
                          out_shape=jax.ShapeDtypeStruct((t, LRU_W), BF16),
                          compiler_params=_cparams(("parallel",), VMEM_MID))(p, p, cw, cb, ga, ba, gx, bxb, lam)


def _lru_bwd(p, cw, cb, ga, ba, gx, bxb, lam, dyb):
    t = p.shape[0]
    col, in_specs = _lru_specs(t)

    def body(*refs):
        ins, d_ref, outs = refs[:9], refs[9], refs[10:]
        fn = functools.partial(_lru_fn, sd=_make_sd(), scan=_make_scan(), dot=_make_dot16())
        _, vjp = jax.vjp(fn, *[x[...] for x in ins])
        g = vjp(d_ref[...])
        outs[0][...] = g[0].astype(BF16)
        outs[1][...] = g[1].astype(BF16)
        for o, val in zip(outs[2:], g[2:]):
            o[...] = val

    sq = pl.BlockSpec((LRU_CW, LRU_CW), lambda j: (j, 0))
    act = jax.ShapeDtypeStruct((t, LRU_W), BF16)
    vec = jax.ShapeDtypeStruct((1, LRU_W), F32)
    sqs = jax.ShapeDtypeStruct((LRU_W, LRU_CW), F32)
    return _blocked(body, name="lru_bwd", grid=(LRU_W // LRU_CW,), in_specs=in_specs + [col(t, RW // LRU_CW)],
                          out_specs=[col(t), col(t), col(4), col(1), sq, col(1), sq, col(1), col(1)],
                          out_shape=[act, act, jax.ShapeDtypeStruct((4, LRU_W), F32), vec, sqs, vec, sqs, vec, vec],
                          compiler_params=_cparams(("parallel",), VMEM_BIG))(p, p, cw, cb, ga, ba, gx, bxb, lam, dyb)


def _s5_disc_fn(a_re, a_im, log_dt, b_re, b_im, e):
    lam_re = jnp.minimum(a_re, -1e-4)
    lam_im = a_im
    dt = jnp.exp(log_dt)
    mag = jnp.exp(lam_re * dt)
    ab_re = mag * jnp.cos(lam_im * dt)
    ab_im = mag * jnp.sin(lam_im * dt)
    den = lam_re * lam_re + lam_im * lam_im
    zr = ab_re - 1.0
    q_re = jnp.dot((zr * lam_re + ab_im * lam_im) / den, e, precision=_HI)
    q_im = jnp.dot((ab_im * lam_re - zr * lam_im) / den, e, precision=_HI)
    return ab_re, ab_im, q_re * b_re - q_im * b_im, q_re * b_im + q_im * b_re


def _s5_disc_fwd(a_re, a_im, log_dt, b_re, b_im, e):
    def body(*refs):
        res = _s5_disc_fn(*[x[...] for x in refs[:6]])
        for o, val in zip(refs[6:], res):
            o[...] = val

    small = jax.ShapeDtypeStruct(a_re.shape, F32)
    wide = jax.ShapeDtypeStruct(b_re.shape, F32)
    return pl.pallas_call(body, name="s5_disc_fwd", out_shape=[small, small, wide, wide])(
        a_re, a_im, log_dt, b_re, b_im, e)


def _s5_disc_bwd(a_re, a_im, log_dt, b_re, b_im, e, cts):
    def body(*refs):
        ins, e_ref, ct, outs = refs[:5], refs[5], refs[6:10], refs[10:]
        _, vjp = jax.vjp(lambda *a: _s5_disc_fn(*a, e_ref[...]), *[x[...] for x in ins])
        for o, val in zip(outs, vjp(tuple(c[...] for c in ct))):
            o[...] = val

    shapes = [jax.ShapeDtypeStruct(x.shape, F32) for x in (a_re, a_im, log_dt, b_re, b_im)]
    return pl.pallas_call(body, name="s5_disc_bwd", out_shape=shapes)(a_re, a_im, log_dt, b_re, b_im, e, *cts)


def _cmul(a, b):
    return a[0] * b[0] - a[1] * b[1], a[0] * b[1] + a[1] * b[0]


def _s5_scan(sr, si, ab, reverse):
    n_tiles = sr.shape[0] // 8
    width = sr.shape[1]
    row8 = lax.broadcasted_iota(jnp.int32, (8, width), 0)
    p1 = ab
    p2 = _cmul(p1, p1)
    p4 = _cmul(p2, p2)
    pw = [p1]
    for _ in range(7):
        pw.append(_cmul(pw[-1], p1))
    cr = jnp.zeros((8, width), F32)
    ci = jnp.zeros((8, width), F32)
    for j in range(8):
        e = pw[7 - j] if reverse else pw[j]
        cr = jnp.where(row8 == j, e[0], cr)
        ci = jnp.where(row8 == j, e[1], ci)

    levels = []
    for d, q in ((1, p1), (2, p2), (4, p4)):
        keep = row8 < 8 - d if reverse else row8 >= d
        levels.append((d, (jnp.where(keep, q[0], 0.0), jnp.where(keep, q[1], 0.0))))

    def tile(i, carry):
        idx = n_tiles - 1 - i if reverse else i
        base = pl.multiple_of(idx * 8, 8)
        x = (sr[pl.ds(base, 8), :], si[pl.ds(base, 8), :])
        for d, q in levels:
            amt = 8 - d if reverse else d
            m = _cmul(q, (pltpu.roll(x[0], amt, 0), pltpu.roll(x[1], amt, 0)))
            x = (x[0] + m[0], x[1] + m[1])
        m = _cmul((cr, ci), carry)
        x = (x[0] + m[0], x[1] + m[1])
        sr[pl.ds(base, 8), :] = x[0]
        si[pl.ds(base, 8), :] = x[1]
        edge = slice(0, 1) if reverse else slice(7, 8)
        return x[0][edge], x[1][edge]

    zero = jnp.zeros((1, width), F32)
    lax.fori_loop(0, n_tiles, tile, (zero, zero))


_S5_W = S5_SLAB // S5_GROUP * S5_STATE


def _s5_specs(t):
    col = lambda r: pl.BlockSpec((r, S5_SLAB), lambda j: (0, j))
    bb = pl.BlockSpec((None, S5_SLAB, _S5_W), lambda j: (j, 0, 0))
    cd = pl.BlockSpec((None, _S5_W, S5_SLAB), lambda j: (j, 0, 0))
    ab = pl.BlockSpec((None, 1, _S5_W), lambda j: (j, 0, 0))
    return col, bb, cd, ab


def _s5_fwd(u, dvec, bbr, bbi, cdr, cdi, abr, abi):
    t, width = u.shape
    col, bb, cd, ab = _s5_specs(t)

    def body(u_ref, d_ref, bbr_ref, bbi_ref, cdr_ref, cdi_ref, abr_ref, abi_ref, o_ref, sr, si):
        uv = u_ref[...]
        sr[...] = _dot16(uv, bbr_ref[...])
        si[...] = _dot16(uv, bbi_ref[...])
        _s5_scan(sr, si, (abr_ref[...], abi_ref[...]), False)
        y = _dot16(sr[...], cdr_ref[...]) - _dot16(si[...], cdi_ref[...])
        o_ref[...] = jax.nn.gelu(y + d_ref[...] * uv).astype(BF16)

    return _blocked(body, name="s5_fwd", grid=(width // S5_SLAB,),
                          in_specs=[col(t), col(1), bb, bb, cd, cd, ab, ab], out_specs=col(t),
                          out_shape=jax.ShapeDtypeStruct((t, width), BF16),
                          scratch_shapes=[pltpu.VMEM((t, _S5_W), F32)] * 2,
                          compiler_params=_cparams(("parallel",), VMEM_BIG))(u, dvec, bbr, bbi, cdr, cdi, abr, abi)


def _s5_bwd(u, dvec, bbr, bbi, cdr, cdi, abr, abi, dyact):
    t, width = u.shape
    col, bb, cd, ab = _s5_specs(t)
    ns = width // S5_SLAB
    tn = (((0,), (0,)), ((), ()))
    nt = (((1,), (1,)), ((), ()))

    def body(u_ref, d_ref, bbr_ref, bbi_ref, cdr_ref, cdi_ref, abr_ref, abi_ref, dy_ref,
             du_ref, dd_ref, dbbr_ref, dbbi_ref, dcdr_ref, dcdi_ref, dabr_ref, dabi_ref, sr, si, gr, gi):
        uv = u_ref[...]
        dv = d_ref[...]
        abv = (abr_ref[...], abi_ref[...])
        sr[...] = _dot16(uv, bbr_ref[...])
        si[...] = _dot16(uv, bbi_ref[...])
        _s5_scan(sr, si, abv, False)
        y = _dot16(sr[...], cdr_ref[...]) - _dot16(si[...], cdi_ref[...])
        _, vjp = jax.vjp(jax.nn.gelu, y + dv * uv)
        (dpre,) = vjp(dy_ref[...].astype(F32))
        dd_ref[...] = jnp.sum(dpre * uv, axis=0, keepdims=True)
        dcdr_ref[...] = _dot16(sr[...], dpre, tn)
        dcdi_ref[...] = -_dot16(si[...], dpre, tn)
        gr[...] = _dot16(dpre, cdr_ref[...], nt)
        gi[...] = -_dot16(dpre, cdi_ref[...], nt)
        _s5_scan(gr, gi, (abv[0], -abv[1]), True)

        row8 = lax.broadcasted_iota(jnp.int32, (8, _S5_W), 0)

        def tile(i, carry):
            acc_r, acc_i, last_r, last_i = carry
            base = pl.multiple_of(i * 8, 8)
            s_r, s_i = sr[pl.ds(base, 8), :], si[pl.ds(base, 8), :]
            g_r, g_i = gr[pl.ds(base, 8), :], gi[pl.ds(base, 8), :]
            p_r = jnp.where(row8 == 0, last_r, pltpu.roll(s_r, 1, 0))
            p_i = jnp.where(row8 == 0, last_i, pltpu.roll(s_i, 1, 0))
            acc_r = acc_r + jnp.sum(g_r * p_r + g_i * p_i, axis=0, keepdims=True)
            acc_i = acc_i + jnp.sum(g_i * p_r - g_r * p_i, axis=0, keepdims=True)
            return acc_r, acc_i, s_r[7:8], s_i[7:8]

        zero = jnp.zeros((1, _S5_W), F32)
        acc_r, acc_i, _, _ = lax.fori_loop(0, t // 8, tile, (zero, zero, zero, zero))
        dabr_ref[...] = acc_r
        dabi_ref[...] = acc_i
        du_ref[...] = dpre * dv + _dot16(gr[...], bbr_ref[...], nt) + _dot16(gi[...], bbi_ref[...], nt)
        dbbr_ref[...] = _dot16(uv, gr[...], tn)
        dbbi_ref[...] = _dot16(uv, gi[...], tn)

    sds = jax.ShapeDtypeStruct
    return _blocked(
        body, name="s5_bwd", grid=(ns,), in_specs=[col(t), col(1), bb, bb, cd, cd, ab, ab, col(t)],
        out_specs=[col(t), col(1), bb, bb, cd, cd, ab, ab],
        out_shape=[sds((t, width), F32), sds((1, width), F32), sds((ns, S5_SLAB, _S5_W), F32),
                   sds((ns, S5_SLAB, _S5_W), F32), sds((ns, _S5_W, S5_SLAB), F32), sds((ns, _S5_W, S5_SLAB), F32),
                   sds((ns, 1, _S5_W), F32), sds((ns, 1, _S5_W), F32)],
        scratch_shapes=[pltpu.VMEM((t, _S5_W), F32)] * 4,
        compiler_params=_cparams(("parallel",), VMEM_BIG))(u, dvec, bbr, bbi, cdr, cdi, abr, abi, dyact)


def _gate_dense(w):
    h = w.shape[0]
    return jnp.einsum("hij,hg->higj", w, jnp.eye(h, dtype=F32)).reshape(h * HEAD, h * HEAD)


def _gate_blocks(d):
    x = d.reshape(LRU_W // LRU_CW, 2, HEAD, 2, HEAD)
    return jnp.einsum("tgihj,gh->tgij", x, jnp.eye(2, dtype=F32)).reshape(LRU_W // HEAD, HEAD, HEAD)


_GPS = S5_SLAB // S5_GROUP
_NS = S5_GROUPS // _GPS


def _s5_in_dense(bb):
    x = bb.reshape(_NS, _GPS, S5_STATE, S5_GROUP)
    return jnp.einsum("sgnc,gh->sgchn", x, jnp.eye(_GPS, dtype=F32)).reshape(_NS, S5_SLAB, _S5_W)


def _s5_in_blocks(d):
    x = d.reshape(_NS, _GPS, S5_GROUP, _GPS, S5_STATE)
    return jnp.einsum("sgchn,gh->sgnc", x, jnp.eye(_GPS, dtype=F32)).reshape(S5_GROUPS, S5_STATE * S5_GROUP)


def _s5_out_dense(c):
    x = c.reshape(_NS, _GPS, S5_GROUP, S5_STATE)
    return jnp.einsum("sgcn,gh->shngc", x, jnp.eye(_GPS, dtype=F32)).reshape(_NS, _S5_W, S5_SLAB)


def _s5_out_blocks(d):
    x = d.reshape(_NS, _GPS, S5_STATE, _GPS, S5_GROUP)
    return jnp.einsum("shngc,gh->sgcn", x, jnp.eye(_GPS, dtype=F32)).reshape(S5_GROUPS, S5_GROUP, S5_STATE)


def _local_step(x, tgt, w, late_weights, send_grads):
    d_model = x.shape[1]
    gs = {}
    gm = jnp.kron(jnp.eye(N_HEADS, dtype=F32), jnp.ones((HEAD, HEAD), F32))
    n_layers = w["f_norm_g"].shape[0]

    def ffn_fwd(xin, l):
        xn = _rms_fwd(xin, w["f_norm_g"][l:l + 1], f"rms_f{l}")
        h = _matmul(xn, w["f_w_up_t"][l], "nt", f"mm_f{l}_up")
        act = _ffn_mid_fwd(h, w["f_conv_w"][l], w["f_conv_b"][l:l + 1], f"ffn_mid_fwd{l}")
        return _matmul(act, w["f_w_down"][l], "nn", f"mm_f{l}_down", add=xin), (xin, xn, h, act)

    def ffn_bwd(g, saved, l):
        xin, xn, h, act = saved
        dact = _matmul(g, w["f_w_down"][l], "nt", f"mm_f{l}_dact")
        d_down = _matmul(act, g, "tn", f"mm_f{l}_ddown", out_dtype=BF16)
        dhg, dhv, dwg, dwv, dbg, dbv = _ffn_mid_bwd(h, w["f_conv_w"][l], w["f_conv_b"][l:l + 1], dact,
                                                    f"ffn_mid_bwd{l}")
        dh = jnp.concatenate([dhg, dhv], axis=1)
        dxn = _matmul(dh, w["f_w_up_t"][l], "nn", f"mm_f{l}_dxn")
        d_up = _matmul(dh, xn, "tn", f"mm_f{l}_dup", out_dtype=BF16)
        dx, dgn = _rms_bwd(xin, w["f_norm_g"][l:l + 1], dxn, g, f"rms_f{l}_bwd")
        return dx, d_up, d_down, jnp.concatenate([dwg, dwv], axis=1), jnp.concatenate([dbg, dbv], axis=1), dgn

    xn0 = _rms_fwd(x, w["e_norm_g"], "rms_e")
    p = _matmul(xn0, w["e_w_in_t"], "nt", "mm_e_in")
    pam = _tshift_fwd(p, w["e_mu"])
    pw = dict(w0=w["e_w0"], w2=w["e_w2"][0], a0=w["e_a0"], a2=w["e_a2"][0], g2=w["e_g2"][0],
              k_k=w["e_k_k"], k_a=w["e_k_a"])
    r, dec, k2, v, z, b, gate = _rwkv_prep_fwd(pam, pw, gm)
    v_exp = _expand_cols(v, "wkv_expand_v")
    s_all, s_last = _wkv_fwd(dec, k2, z, b, v_exp)
    y_pt = _wkv_out(r, s_all, s_last)
    y = _from_pt(y_pt)
    rk = w["e_r_k"].reshape(1, RW)
    ya = _rwkv_post_fwd(y, r, k2, v, gate, w["e_ln_w"], w["e_ln_b"], rk, gm)
    ga, gx = _gate_dense(w["e_gate_a_w"][0]), _gate_dense(w["e_gate_x_w"][0])
    lru_w = (w["e_conv_w"][0], w["e_conv_b"], ga, w["e_gate_a_b"], gx, w["e_gate_x_b"], w["e_lru_lambda"])
    yb = _lru_fwd(p, *lru_w)
    ycat = jnp.concatenate([ya, yb], axis=1)
    x1 = _matmul(ycat, w["e_w_out"], "nn", "mm_e_out", add=x)
    w = {**w, **late_weights(x1)}
    x2, ffn0 = ffn_fwd(x1, 0)

    xn1 = _rms_fwd(x2, w["o_norm_g"], "rms_o")
    u = _matmul(xn1, w["o_w_in"], "nn", "mm_o_in")
    expand = jnp.kron(jnp.eye(S5_STATE, dtype=F32), jnp.ones((1, S5_GROUP), F32))
    disc_in = (w["o_A_re"][0], w["o_A_im"][0], w["o_log_dt"].reshape(S5_GROUPS, 1),
               w["o_B_re"][0].reshape(S5_GROUPS, -1), w["o_B_im"][0].reshape(S5_GROUPS, -1), expand)
    ab_re, ab_im, bb_re, bb_im = _s5_disc_fwd(*disc_in)
    s5_w = (w["o_D"], _s5_in_dense(bb_re), _s5_in_dense(bb_im), _s5_out_dense(w["o_C_re"][0]),
            _s5_out_dense(w["o_C_im"][0]), ab_re.reshape(_NS, 1, _S5_W), ab_im.reshape(_NS, 1, _S5_W))
    yact = _s5_fwd(u, *s5_w)
    zz = _matmul(yact, w["o_w_glu_t"], "nt", "mm_o_glu")
    x3 = _glu_fwd(x2, zz)
    x4, ffn1 = ffn_fwd(x3, 1)

    loss, g, gs["final_norm_g", 0] = _loss_head(x4, w["final_norm_g"].reshape(1, d_model), tgt)

    g, up1, down1, dcw1, dcb1, dfn1 = ffn_bwd(g, ffn1, 1)
    dz = _glu_bwd(zz, g)
    dyact = _matmul(dz, w["o_w_glu_t"], "nn", "mm_o_dyact")
    d_glu = _matmul(dz, yact, "tn", "mm_o_dglu", out_dtype=BF16)
    du, gs["o_D", 0], dbbr, dbbi, dcdr, dcdi, dabr, dabi = _s5_bwd(u, *s5_w, dyact)
    gs["o_C_re", 0] = _s5_out_blocks(dcdr).reshape(S5_GROUPS * S5_GROUP, S5_STATE)
    gs["o_C_im", 0] = _s5_out_blocks(dcdi).reshape(S5_GROUPS * S5_GROUP, S5_STATE)
    cts = (dabr.reshape(S5_GROUPS, S5_STATE), dabi.reshape(S5_GROUPS, S5_STATE), _s5_in_blocks(dbbr),
           _s5_in_blocks(dbbi))
    gs["o_A_re", 0], gs["o_A_im", 0], dlog_dt, gs["o_B_re", 0], gs["o_B_im", 0] = _s5_disc_bwd(*disc_in, cts)
    gs["o_log_dt", 0] = dlog_dt.reshape(1, S5_GROUPS)
    dxn = _matmul(du, w["o_w_in"], "nt", "mm_o_dxn")
    d_oin = _matmul(xn1, du, "tn", "mm_o_din", out_dtype=BF16)
    g, gs["o_norm_g", 0] = _rms_bwd(x2, w["o_norm_g"], dxn, g, "rms_o_bwd")
    g = send_grads("a", [("f_w_up", 1, up1), ("f_w_down", 1, down1), ("o_w_glu", 0, d_glu), ("o_w_in", 0, d_oin)], g)

    g, up0, down0, dcw0, dcb0, dfn0 = ffn_bwd(g, ffn0, 0)
    gs["f_conv_w", 0], gs["f_conv_w", 3] = dcw0, dcw1
    gs["f_conv_b", 0], gs["f_conv_b", 1] = dcb0, dcb1
    gs["f_norm_g", 0], gs["f_norm_g", 1] = dfn0, dfn1

    dycat = _matmul(g, w["e_w_out"], "nt", "mm_e_dycat")
    d_eout = _matmul(ycat, g, "tn", "mm_e_dout", out_dtype=BF16)
    dycat = send_grads("b", [("f_w_up", 0, up0), ("f_w_down", 0, down0), ("e_w_out", 0, d_eout)], dycat)
    dy, dr1, dk1, dv1, dgate, gs["e_ln_w", 0], gs["e_ln_b", 0], gs["e_r_k", 0] = _rwkv_post_bwd(
        y, r, k2, v, gate, w["e_ln_w"], w["e_ln_b"], rk, gm, dycat)
    dr2, ddec, dk2, dzz, dbb, dv_pt = _wkv_bwd(r, dec, k2, z, b, v_exp, s_all, _expand_cols(dy, "wkv_expand_dy"))
    (dpam, gs["e_w0", 0], gs["e_w2", 0], gs["e_a0", 0], gs["e_a2", 0], gs["e_g2", 0], gs["e_k_k", 0],
     gs["e_k_a", 0]) = _rwkv_prep_bwd(pam, pw, gm, (dr2, ddec, dk2, _from_pt(dv_pt), dzz, dbb, dgate), (dr1, dk1, dv1))
    dpa, gs["e_mu", 0] = _tshift_bwd(p, w["e_mu"], dpam)
    (dbx, dbg, gs["e_conv_w", 0], gs["e_conv_b", 0], dga, gs["e_gate_a_b", 0], dgx, gs["e_gate_x_b", 0],
     gs["e_lru_lambda", 0]) = _lru_bwd(p, *lru_w, dycat)
    gs["e_gate_a_w", 0] = _gate_blocks(dga).reshape(LRU_W, HEAD)
    gs["e_gate_x_w", 0] = _gate_blocks(dgx).reshape(LRU_W, HEAD)
    dp = jnp.concatenate([dpa, dbx, dbg], axis=1)
    dxn = _matmul(dp, w["e_w_in_t"], "nn", "mm_e_dxn")
    d_ein = _matmul(dp, xn0, "tn", "mm_e_din", out_dtype=BF16)
    grad_x, gs["e_norm_g", 0] = _rms_bwd(x, w["e_norm_g"], dxn, g, "rms_e_bwd")
    grad_x = send_grads("c", [("e_w_in", 0, d_ein)], grad_x)
    return loss, grad_x, gs


CAST_ROWS = 256


def _cast_shard(w3, layer, transpose, chip, name):
    _, rows, cols = w3.shape
    tr = _tile(rows, (CAST_ROWS, 176, 128))

    def body(c_ref, w_ref, o_ref):
        v = w_ref[...]
        o_ref[...] = (v.T if transpose else v).astype(BF16)

    in_spec = pl.BlockSpec((None, tr, cols), lambda i, c: (layer, i, 0))
    if transpose:
        out_spec, shape = pl.BlockSpec((None, cols, tr), lambda i, c: (c[0], 0, i)), (cols, rows)
    else:
        out_spec, shape = pl.BlockSpec((None, tr, cols), lambda i, c: (c[0], i, 0)), (rows, cols)
    grid_spec = pltpu.PrefetchScalarGridSpec(num_scalar_prefetch=1, grid=(rows // tr,), in_specs=[in_spec],
                                             out_specs=out_spec)
    return _blocked(body, name=name, grid_spec=grid_spec,
                          out_shape=jax.ShapeDtypeStruct((N_CHIPS,) + shape, BF16),
                          compiler_params=_cparams(("parallel",), VMEM_MID))(chip, w3)


_ANY = pl.BlockSpec(memory_space=pl.ANY)


def _coords():
    return lax.axis_index("x"), lax.axis_index("y"), lax.axis_index("c")


def _flip(v, d):
    return 1 - v if d else v


_CHIP_RELS = ((1, 0), (0, 1), (1, 1))
_DEV_RELS = tuple((dx, dy, dc) for dx in (0, 1) for dy in (0, 1) for dc in (0, 1))[1:]


_HBM = pl.BlockSpec(memory_space=pltpu.HBM)
_SEM = pl.BlockSpec(memory_space=pltpu.SEMAPHORE)
_EFFECT = pltpu.SideEffectType.DATAFLOW_SIDE_EFFECTING


def _in_hbm(a):
    return pltpu.with_memory_space_constraint(a, pltpu.HBM)


def _gather_copies(bufs, send, recv, landed):
    x, y, c = _coords()
    me = 2 * x + y
    res = []
    for i, buf in enumerate(bufs):
        for j, (dx, dy) in enumerate(_CHIP_RELS):
            px, py = _flip(x, dx), _flip(y, dy)
            k = i * len(_CHIP_RELS) + j
            res.append(pltpu.make_async_remote_copy(
                src_ref=buf.at[me], dst_ref=buf.at[2 * px + py if landed else me], send_sem=send.at[k],
                recv_sem=recv.at[k], device_id=(px, py, c), device_id_type=MESH))
    return res


def _scatter_copies(srcs, lands, send, recv, landed):
    x, y, c = _coords()
    me = 4 * x + 2 * y + c
    res = []
    for i, (src, land) in enumerate(zip(srcs, lands)):
        for j, (dx, dy, dc) in enumerate(_DEV_RELS):
            peer = (_flip(x, dx), _flip(y, dy), _flip(c, dc))
            pid = 4 * peer[0] + 2 * peer[1] + peer[2]
            k = i * len(_DEV_RELS) + j
            res.append(pltpu.make_async_remote_copy(
                src_ref=src.at[pid], dst_ref=land.at[pid if landed else me], send_sem=send.at[k],
                recv_sem=recv.at[k], device_id=peer, device_id_type=MESH))
    return res


def _split_start(bufs, n_src, copies, n_rel, name, after):
    n = len(bufs)
    nk = n_src * n_rel

    def body(*refs):
        ins, send, recv, token = refs[:n], refs[n + 1 + n], refs[n + 2 + n], refs[-1]
        for cp in copies(ins, send, recv, False):
            cp.start()
        token[...] = jnp.zeros_like(token)

    res = pl.pallas_call(
        body, name=name, in_specs=[_HBM] * n + [_ANY],
        out_specs=[_HBM] * n + [_SEM, _SEM, pl.BlockSpec(memory_space=pltpu.VMEM)],
        out_shape=[pltpu.HBM(b.shape, b.dtype) for b in bufs]
        + [pltpu.SemaphoreType.DMA((nk,)), pltpu.SemaphoreType.DMA((nk,)), jax.ShapeDtypeStruct((8, LANES), F32)],
        input_output_aliases={i: i for i in range(n)},
        compiler_params=pltpu.CompilerParams(has_side_effects=_EFFECT))(*[_in_hbm(b) for b in bufs], after)
    return res[n], res[n + 1], list(res[:n]), res[n + 2]


def _split_wait(bufs, send, recv, copies, name, after):
    n = len(bufs)

    def body(*refs):
        ins, send_ref, recv_ref = refs[:n], refs[n], refs[n + 1]
        for cp in copies(ins, send_ref, recv_ref, True):
            cp.wait_send()
            cp.wait_recv()

    return pl.pallas_call(
        body, name=name, in_specs=[_HBM] * n + [_SEM, _SEM, _ANY], out_specs=[_HBM] * n,
        out_shape=[pltpu.HBM(b.shape, b.dtype) for b in bufs], input_output_aliases={i: i for i in range(n)},
        compiler_params=pltpu.CompilerParams(has_side_effects=_EFFECT))(*bufs, send, recv, after)


def _gather_start(bufs, name, after):
    return _split_start(bufs, len(bufs), _gather_copies, len(_CHIP_RELS), name, after)


def _gather_wait(bufs, send, recv, name, after):
    return _split_wait(bufs, send, recv, _gather_copies, name, after)


def _scatter_start(srcs, name, after):
    n = len(srcs)
    lands = [lax.empty(a.shape, a.dtype) for a in srcs]
    fn = lambda refs, send, recv, landed: _scatter_copies(refs[:n], refs[n:], send, recv, landed)
    send, recv, bufs, token = _split_start(list(srcs) + lands, n, fn, len(_DEV_RELS), name, after)
    return send, recv, bufs, token


def _scatter_wait(bufs, send, recv, name, after):
    n = len(bufs) // 2
    fn = lambda refs, s, r, landed: _scatter_copies(refs[:n], refs[n:], s, r, landed)
    res = _split_wait(bufs, send, recv, fn, name, after)
    return res[:n], res[n:]


def _sum_segments(src, land, me, name):
    nd, seg, cols = src.shape
    ts = _tile(seg, (256, 176, 128))

    def body(m_ref, *refs):
        o_ref = refs[-1]
        acc = refs[0][...].astype(F32)
        for r in refs[1:-1]:
            acc = acc + r[...].astype(F32)
        o_ref[...] = acc

    def peer(rel):
        bits = 4 * rel[0] + 2 * rel[1] + rel[2]
        return pl.BlockSpec((None, ts, cols), lambda i, m: (jnp.bitwise_xor(m[0], bits), i, 0))

    grid_spec = pltpu.PrefetchScalarGridSpec(
        num_scalar_prefetch=1, grid=(seg // ts,),
        in_specs=[pl.BlockSpec((None, ts, cols), lambda i, m: (m[0], i, 0))] + [peer(r) for r in _DEV_RELS],
        out_specs=pl.BlockSpec((None, ts, cols), lambda i, m: (m[1], i, 0)))
    return _blocked(body, name=name, grid_spec=grid_spec,
                          out_shape=jax.ShapeDtypeStruct((2, seg, cols), F32),
                          compiler_params=_cparams(("parallel",), VMEM_MID))(me, src, *[land] * len(_DEV_RELS))


def _exchange_sibling(arrs):
    n = len(arrs)

    def body(*refs):
        outs, (send, recv) = refs[n:2 * n], refs[2 * n:]
        x, y, c = _coords()
        sib = (x, y, 1 - c)
        sends, recvs = [], []
        for i in range(n):
            cp = pltpu.make_async_remote_copy(src_ref=outs[i].at[c], dst_ref=outs[i].at[c], send_sem=send.at[i],
                                              recv_sem=recv.at[i], device_id=sib, device_id_type=MESH)
            cp.start()
            sends.append(cp)
            recvs.append(pltpu.make_async_remote_copy(src_ref=outs[i].at[c], dst_ref=outs[i].at[1 - c],
                                                      send_sem=send.at[i], recv_sem=recv.at[i], device_id=sib,
                                                      device_id_type=MESH))
        for cp in recvs:
            cp.wait_recv()
        for cp in sends:
            cp.wait_send()

    return pl.pallas_call(
        body, name="exchange_sibling", in_specs=[_ANY] * n, out_specs=[_ANY] * n,
        out_shape=[jax.ShapeDtypeStruct(a.shape, a.dtype) for a in arrs],
        input_output_aliases={i: i for i in range(n)},
        scratch_shapes=[pltpu.SemaphoreType.DMA((n,)), pltpu.SemaphoreType.DMA((n,))])(*arrs)


def _allreduce_small(vec):
    nd, rows, lanes = vec.shape
    nr = len(_DEV_RELS)

    def body(in_ref, out_ref, stage, red, send, recv):
        x, y, c = _coords()
        me = 4 * x + 2 * y + c
        peers = []
        for dx, dy, dc in _DEV_RELS:
            peer = (_flip(x, dx), _flip(y, dy), _flip(c, dc))
            peers.append((peer, 4 * peer[0] + 2 * peer[1] + peer[2]))

        def copy(src, dst, k, peer):
            return pltpu.make_async_remote_copy(src_ref=src, dst_ref=dst, send_sem=send.at[k], recv_sem=recv.at[k],
                                                device_id=peer, device_id_type=MESH)

        first = [copy(in_ref.at[pid], stage.at[me], j, peer) for j, (peer, pid) in enumerate(peers)]
        for cp in first:
            cp.start()
        stage[me] = in_ref[me]
        for j, (peer, pid) in enumerate(peers):
            copy(in_ref.at[pid], stage.at[pid], j, peer).wait_recv()
        acc = stage[0]
        for d in range(1, nd):
            acc = acc + stage[d]
        red[...] = acc
        out_ref[me] = acc
        second = [copy(red, out_ref.at[me], nr + j, peer) for j, (peer, pid) in enumerate(peers)]
        for cp in second:
            cp.start()
        for j, (peer, pid) in enumerate(peers):
            copy(red, out_ref.at[pid], nr + j, peer).wait_recv()
        for cp in first + second:
            cp.wait_send()

    vm = pl.BlockSpec(memory_space=pltpu.VMEM)
    return pl.pallas_call(
        body, name="allreduce_small", in_specs=[vm], out_specs=vm,
        out_shape=jax.ShapeDtypeStruct(vec.shape, F32),
        scratch_shapes=[pltpu.VMEM(vec.shape, F32), pltpu.VMEM((rows, lanes), F32),
                        pltpu.SemaphoreType.DMA((2 * nr,)), pltpu.SemaphoreType.DMA((2 * nr,))],
        compiler_params=_cparams(None, VMEM_MID))(vec)


def _adam_math(w, g, m, v):
    m2 = ADAM_B1 * m + (1.0 - ADAM_B1) * g
    v2 = ADAM_B2 * v + (1.0 - ADAM_B2) * (g * g)
    m_hat = m2 / (1.0 - ADAM_B1 ** ADAM_STEP)
    v_hat = v2 / (1.0 - ADAM_B2 ** ADAM_STEP)
    return -ADAM_LR * (m_hat / (jnp.sqrt(v_hat) + ADAM_EPS) + ADAM_WD * w), m2, v2


def _adamw_big(w3, m3, v3, layer, g, transposed, name, prev=None):
    nl, rows, cols = w3.shape
    tr = 128 if transposed else _tile(rows, (256, 176, 128))

    def body(w_ref, m_ref, v_ref, g_ref, *rest):
        go_ref, d_ref, mo_ref, vo_ref = rest[-4:]
        g_val = g_ref[...].T if transposed else g_ref[...]
        go_ref[...] = g_val
        d_ref[...], mo_ref[...], vo_ref[...] = _adam_math(w_ref[...], g_val, m_ref[...], v_ref[...])

    wspec = pl.BlockSpec((None, tr, cols), lambda i: (layer, i, 0))
    gspec = pl.BlockSpec((cols, tr), lambda i: (0, i)) if transposed else pl.BlockSpec((tr, cols), lambda i: (i, 0))
    extra = [] if prev is None else list(prev)
    return _blocked(body, name=name, grid=(rows // tr,),
                          in_specs=[wspec, wspec, wspec, gspec] + [_ANY] * len(extra),
                          out_specs=[wspec] * 4, out_shape=[jax.ShapeDtypeStruct((nl, rows, cols), F32)] * 4,
                          input_output_aliases={4 + i: i for i in range(len(extra))},
                          compiler_params=_cparams(("parallel",), VMEM_MID))(w3, m3, v3, g, *extra)


_SMALL = (
    ("e_norm_g", (1, 1024), None), ("e_mu", (1, SHIFT_COLS), None), ("e_w0", (1, RW), None),
    ("e_w2", (W_LORA, RW), 128), ("e_a0", (1, RW), None), ("e_a2", (A_LORA, RW), 128), ("e_g2", (G_LORA, RW), 128),
    ("e_k_k", (1, RW), None), ("e_k_a", (1, RW), None), ("e_r_k", (1, RW), None), ("e_ln_w", (1, RW), None),
    ("e_ln_b", (1, RW), None), ("e_conv_w", (4, LRU_W), 128), ("e_conv_b", (1, LRU_W), None),
    ("e_gate_a_w", (LRU_W, HEAD), None), ("e_gate_a_b", (1, LRU_W), None), ("e_gate_x_w", (LRU_W, HEAD), None),
    ("e_gate_x_b", (1, LRU_W), None), ("e_lru_lambda", (1, LRU_W), None), ("o_norm_g", (1, 1024), 256),
    ("o_A_re", (S5_GROUPS, S5_STATE), None), ("o_A_im", (S5_GROUPS, S5_STATE), None), ("o_log_dt", (1, S5_GROUPS), None),
    ("o_B_re", (S5_GROUPS, S5_STATE * S5_GROUP), None), ("o_B_im", (S5_GROUPS, S5_STATE * S5_GROUP), None),
    ("o_C_re", (S5_GROUPS * S5_GROUP, S5_STATE), None), ("o_C_im", (S5_GROUPS * S5_GROUP, S5_STATE), None),
    ("o_D", (1, 1024), 256), ("f_norm_g", (2, 1024), None), ("f_conv_w", (6, 2 * D_FF), 2 * D_FF // 4),
    ("f_conv_b", (2, 2 * D_FF), None), ("final_norm_g", (1, 1024), None))
_PIECES = {"f_norm_g": ((0, 1), (1, 1)), "f_conv_b": ((0, 1), (1, 1)), "f_conv_w": ((0, 3), (3, 3))}


def _ceil_to(n, m):
    return -(-n // m) * m


def _small_layout():
    groups = {}
    for name, (rows, cols), _ in _SMALL:
        for first, r in _PIECES.get(name, ((0, rows),)):
            groups.setdefault(cols, []).append((name, first, r))
    layout, off = {}, 0
    for cols, items in groups.items():
        stacks = [0, 0] if 2 * cols <= LANES else [0]
        placed = []
        for name, first, r in sorted(items, key=lambda it: -it[2]):
            half = stacks.index(min(stacks))
            r0 = stacks[half]
            if r >= 8 or r0 % 8 + r > 8:
                r0 = _ceil_to(r0, 8)
            placed.append((name, first, r, r0, half * (LANES // 2)))
            stacks[half] = r0 + r
        rpad = _ceil_to(max(stacks), 8)
        for name, first, r, at, lane in placed:
            layout[name, first] = (off, rpad, at, r, cols, lane)
        off += -(-cols // LANES) * rpad
    return layout, _ceil_to(off, 8 * N_DEV)


def _small_pack(gs):
    layout, total = _small_layout()
    keys = list(layout)

    def body(*refs):
        out = refs[-1]
        out[...] = jnp.zeros_like(out)
        for key, g_ref in zip(keys, refs[:-1]):
            off, rpad, at, r, cols, lane = layout[key]
            for j in range(-(-cols // LANES)):
                cw = min(LANES, cols - j * LANES)
                out[off + j * rpad + at:off + j * rpad + at + r, lane:lane + cw] = g_ref[:, j * LANES:j * LANES + cw]

    return pl.pallas_call(body, name="small_pack", out_shape=jax.ShapeDtypeStruct((total, LANES), F32),
                          compiler_params=_cparams(None, VMEM_MID))(*[gs[k] for k in keys])


def _adamw_small(red, chip, wts, ms, vs):
    layout, _ = _small_layout()
    names = [n for n, _, _ in _SMALL]
    n = len(names)

    def body(chip_ref, red_ref, *refs):
        ins, outs = refs[:3 * n], refs[3 * n:]
        c = chip_ref[0]
        for i, (name, (rows, cols), loc) in enumerate(_SMALL):
            w_ref, m_ref, v_ref = ins[3 * i:3 * i + 3]
            o_refs = outs[4 * i:4 * i + 4]
            width = cols if loc is None else loc
            for first, r in _PIECES.get(name, ((0, rows),)):
                off, rpad, at, _, _, lane = layout[name, first]
                for j in range(-(-width // LANES)):
                    cw = min(LANES, width - j * LANES)
                    ls = slice(lane, lane + cw)
                    if loc is None:
                        start = off + j * rpad + at
                        g = red_ref[start:start + r, ls]
                    else:
                        blk = c * (loc // LANES) + j
                        if r >= 8:
                            g = red_ref[pl.ds(pl.multiple_of(off + at + blk * rpad, 8), r), ls]
                        else:
                            tile = red_ref[pl.ds(pl.multiple_of(off + at // 8 * 8 + blk * rpad, 8), 8), ls]
                            g = tile[at % 8:at % 8 + r]
                    rs, cs = slice(first, first + r), slice(j * LANES, j * LANES + cw)
                    d, m2, v2 = _adam_math(w_ref[rs, cs], g, m_ref[rs, cs], v_ref[rs, cs])
                    for o, val in zip(o_refs, (g, d, m2, v2)):
                        o[rs, cs] = val

    args, shapes = [], []
    for name in names:
        args += [wts[name], ms[name], vs[name]]
        shapes += [jax.ShapeDtypeStruct(wts[name].shape, F32)] * 4
    vm = pl.BlockSpec(memory_space=pltpu.VMEM)
    res = pl.pallas_call(body, name="adamw_small",
                         in_specs=[pl.BlockSpec(memory_space=pltpu.SMEM), vm] + [vm] * (3 * n),
                         out_specs=[vm] * (4 * n), out_shape=shapes,
                         compiler_params=_cparams(None, VMEM_BIG))(chip, red, *args)
    return {name: res[4 * i:4 * i + 4] for i, name in enumerate(names)}


PACK_ROWS = 8


def _packed_rows(shape):
    size = 1
    for d in shape:
        size *= d
    return -(-size // (PACK_ROWS * LANES)) * PACK_ROWS


def _pack(arrs, row_mult):
    parts = []
    for a in arrs:
        flat = a.reshape(-1).astype(F32)
        rows = _packed_rows(a.shape)
        parts.append(jnp.pad(flat, (0, rows * LANES - flat.shape[0])).reshape(rows, LANES))
    total = sum(p.shape[0] for p in parts)
    fill = -(-total // row_mult) * row_mult - total
    if fill:
        parts.append(jnp.zeros((fill, LANES), F32))
    return jnp.concatenate(parts, axis=0)


def _unpack(packed, shapes):
    out, off = [], 0
    for s in shapes:
        rows = _packed_rows(s)
        size = 1
        for d in s:
            size *= d
        out.append(packed[off:off + rows].reshape(-1)[:size].reshape(s))
        off += rows
    return out


_SMALL_SH = ("e_w2", "e_a2", "e_g2", "e_conv_w", "o_norm_g", "o_D", "f_conv_w")
_LARGE = (("e_w_in", True), ("e_w_out", False), ("o_w_in", False), ("o_w_glu", True), ("f_w_up", True),
        ("f_w_down", False))
_ORDER = ("e_norm_g", "e_w_in", "e_mu", "e_w0", "e_w2", "e_a0", "e_a2", "e_g2", "e_k_k", "e_k_a", "e_r_k", "e_ln_w",
          "e_ln_b", "e_conv_w", "e_conv_b", "e_gate_a_w", "e_gate_a_b", "e_gate_x_w", "e_gate_x_b", "e_lru_lambda",
          "e_w_out", "o_norm_g", "o_w_in", "o_A_re", "o_A_im", "o_log_dt", "o_B_re", "o_B_im", "o_C_re", "o_C_im",
          "o_D", "o_w_glu", "f_norm_g", "f_w_up", "f_conv_w", "f_conv_b", "f_w_down", "final_norm_g")
N_CHIPS = 4
N_DEV = 8


def _step(x, tgt, wts, ms, vs):
    xi, yi, ci = _coords()
    chip = 2 * xi + yi
    chip1 = chip.astype(jnp.int32).reshape(1)
    me2 = jnp.stack([4 * xi + 2 * yi + ci, ci]).astype(jnp.int32)
    by_cols = dict(_LARGE)

    bufs = {(name, l): _cast_shard(wts[name], l, by_cols[name], chip1, f"cast_{name}{l}")
            for name, _ in _LARGE for l in range(wts[name].shape[0])}
    sh_shapes = [wts[n].shape for n in _SMALL_SH]
    packed = _pack([wts[n] for n in _SMALL_SH], 8)
    small_buf = lax.dynamic_update_slice(jnp.zeros((N_CHIPS,) + packed.shape, F32), packed[None], (chip, 0, 0))
    early = [("e_w_in", 0), ("e_w_out", 0)]
    late = [k for k in bufs if k not in early]
    send, recv, thru, token = _gather_start([bufs[k] for k in early] + [small_buf], "gather_start_a", x)
    got = _gather_wait(thru, send, recv, "gather_wait_a", token)
    send_b, recv_b, thru_b, token = _gather_start([bufs[k] for k in late], "gather_start_b", got[0])
    x, _ = lax.optimization_barrier((x, token))

    def rows(g):
        return g.reshape(N_CHIPS * g.shape[1], g.shape[2])

    full = {n: wts[n] for n, _, loc in _SMALL if loc is None}
    full["e_w_in_t"], full["e_w_out"] = rows(got[0]), rows(got[1])
    per_chip = [_unpack(got[2][k], sh_shapes) for k in range(N_CHIPS)]
    for i, n in enumerate(_SMALL_SH):
        full[n] = jnp.concatenate([per_chip[k][i] for k in range(N_CHIPS)], axis=-1)

    def late_weights(after):
        res = dict(zip(late, _gather_wait(thru_b, send_b, recv_b, "gather_wait_b", after)))
        return {"o_w_in": rows(res[("o_w_in", 0)]), "o_w_glu_t": rows(res[("o_w_glu", 0)]),
                "f_w_up_t": [rows(res[("f_w_up", l)]) for l in range(2)],
                "f_w_down": [rows(res[("f_w_down", l)]) for l in range(2)]}

    pending = []

    def send_grads(tag, items, carry):
        srcs = [g.reshape(N_DEV, g.shape[0] // N_DEV, g.shape[1]) for _, _, g in items]
        s_sem, r_sem, both, tok = _scatter_start(srcs, f"scatter_start_{tag}", carry)
        pending.append((tag, [(name, l) for name, l, _ in items], s_sem, r_sem, both))
        carry, _ = lax.optimization_barrier((carry, tok))
        return carry

    loss, grad_x, gs = _local_step(x, tgt, full, late_weights, send_grads)

    final = {}
    red = _allreduce_small(_small_pack(gs).reshape(N_DEV, -1, LANES)).reshape(-1, LANES)
    view = {name: (rows, cols if loc is None else loc) for name, (rows, cols), loc in _SMALL}
    as2d = lambda d: {name: d[name].reshape(view[name]) for name in view}
    small = _adamw_small(red, chip1, as2d(wts), as2d(ms), as2d(vs))
    for name, res in small.items():
        final[name] = [r.reshape(wts[name].shape) for r in res]
    new_v = small["final_norm_g"][3]

    halves, keys = [], []
    for tag, names, s_sem, r_sem, both in pending:
        srcs, lands = _scatter_wait(both, s_sem, r_sem, f"scatter_wait_{tag}", new_v)
        for (name, l), src, land in zip(names, srcs, lands):
            halves.append(_sum_segments(src, land, me2, f"sum_{name}{l}"))
            keys.append((name, l))
    shards = _exchange_sibling(halves)
    for s, (name, l) in zip(shards, keys):
        final[name] = _adamw_big(wts[name], ms[name], vs[name], l, s.reshape(2 * s.shape[1], s.shape[2]),
                                 by_cols[name], f"adamw_{name}{l}", prev=final.get(name))

    loss = lax.psum(loss[0, 0], ("x", "y", "c"))
    res = [loss, grad_x[None]]
    for k in range(4):
        res += [final[n][k] for n in _ORDER]
    return tuple(res)


def kernel(x, e_norm_g, e_w_in, e_mu, e_w0, e_w2, e_a0, e_a2, e_g2, e_k_k, e_k_a, e_r_k, e_ln_w, e_ln_b, e_conv_w, e_conv_b, e_gate_a_w, e_gate_a_b, e_gate_x_w, e_gate_x_b, e_lru_lambda, e_w_out, o_norm_g, o_w_in, o_A_re, o_A_im, o_log_dt, o_B_re, o_B_im, o_C_re, o_C_im, o_D, o_w_glu, f_norm_g, f_w_up, f_conv_w, f_conv_b, f_w_down, final_norm_g, loss_target, m_e_norm_g, m_e_w_in, m_e_mu, m_e_w0, m_e_w2, m_e_a0, m_e_a2, m_e_g2, m_e_k_k, m_e_k_a, m_e_r_k, m_e_ln_w, m_e_ln_b, m_e_conv_w, m_e_conv_b, m_e_gate_a_w, m_e_gate_a_b, m_e_gate_x_w, m_e_gate_x_b, m_e_lru_lambda, m_e_w_out, m_o_norm_g, m_o_w_in, m_o_A_re, m_o_A_im, m_o_log_dt, m_o_B_re, m_o_B_im, m_o_C_re, m_o_C_im, m_o_D, m_o_w_glu, m_f_norm_g, m_f_w_up, m_f_conv_w, m_f_conv_b, m_f_w_down, m_final_norm_g, v_e_norm_g, v_e_w_in, v_e_mu, v_e_w0, v_e_w2, v_e_a0, v_e_a2, v_e_g2, v_e_k_k, v_e_k_a, v_e_r_k, v_e_ln_w, v_e_ln_b, v_e_conv_w, v_e_conv_b, v_e_gate_a_w, v_e_gate_a_b, v_e_gate_x_w, v_e_gate_x_b, v_e_lru_lambda, v_e_w_out, v_o_norm_g, v_o_w_in, v_o_A_re, v_o_A_im, v_o_log_dt, v_o_B_re, v_o_B_im, v_o_C_re, v_o_C_im, v_o_D, v_o_w_glu, v_f_norm_g, v_f_w_up, v_f_conv_w, v_f_conv_b, v_f_w_down, v_final_norm_g):
    args = locals()
    wts = {n: args[n] for n in _ORDER}
    ms = {n: args["m_" + n] for n in _ORDER}
    vs = {n: args["v_" + n] for n in _ORDER}
    return _step(x[0], loss_target[0], wts, ms, vs)
```

```python
import functools

import jax
import jax.numpy as jnp
from jax import lax
from jax.experimental import pallas as pl
from jax.experimental.pallas import tpu as pltpu

F32 = jnp.float32
BF16 = jnp.bfloat16
MESH = pl.DeviceIdType.MESH

HEAD = 64
RW = 512
N_HEADS = RW // HEAD
LRU_W = 512
SHIFT_COLS = 1792
W_LORA, A_LORA, G_LORA = 64, 64, 128
S5_GROUPS, S5_GROUP, S5_STATE = 64, 16, 64
D_FF = 2816
NORM_EPS = 1e-6
GN_EPS = 64e-5
LRU_C = 8.0
ADAM_LR, ADAM_B1, ADAM_B2, ADAM_EPS, ADAM_WD, ADAM_STEP = 0.001, 0.9, 0.999, 1e-08, 0.01, 10

VMEM_BIG = 56 * 1024 * 1024
VMEM_MID = 40 * 1024 * 1024
LANES = 128
PT = 16
WKV_CHUNK = 32
S5_SLAB = 128


def _blocked(*args, **kw):
    call = pl.pallas_call(*args, **kw)

    def run(*ops):
        return call(*[pltpu.with_memory_space_constraint(a, pltpu.HBM) if a.ndim >= 2 else a for a in ops])

    return run


def _cparams(sem=None, vmem=None):
    kw = {}
    if sem is not None:
        kw["dimension_semantics"] = sem
    if vmem is not None:
        kw["vmem_limit_bytes"] = vmem
    return pltpu.CompilerParams(**kw)


def _tile(dim, cands):
    for c in cands:
        if dim % c == 0:
            return c
    return dim


def _full(shape):
    n = len(shape)
    return pl.BlockSpec(shape, lambda *_: (0,) * n)


_TILES = (2816, 2048, 1408, 1024, 512, 256, 128)
MM_BUDGET = 36 * 1024 * 1024
VMEM_SLACK = 12 * 1024 * 1024


MXU_FLOPS = 9.0e14
HBM_BYTES = 3.3e12
STEP_SECONDS = 0.35e-6


def _mm_tiles(m, n, k, size_a, size_b, size_o, has_add):
    best = None
    for tm in _TILES:
        for tk in _TILES:
            for tn in _TILES:
                if m % tm or n % tn or k % tk:
                    continue
                need = 2 * (tm * tk * size_a + tk * tn * size_b + tm * tn * size_o) + tm * tn * 4 * (1 + 2 * has_add)
                if k > tk:
                    need += tm * tn * 4
                if need > MM_BUDGET:
                    continue
                steps = (m // tm) * (n // tn) * (k // tk)
                a_reads = n // tn if k > tk else 1
                moved = (m * k * size_a * a_reads + k * n * size_b * (m // tm) + m * n * (size_o + 4 * has_add))
                cost = max(2.0 * m * n * k / MXU_FLOPS, moved / HBM_BYTES) + steps * STEP_SECONDS
                cand = (-cost, tk, tm, tn)
                if best is None or cand > best[0]:
                    best = (cand, need)
    (_, tk, tm, tn), need = best
    return tm, tn, tk, need


def _matmul(a, b, mode, name, out_dtype=F32, add=None):
    if mode == "nn":
        (m, k), (k2, n) = a.shape, b.shape
    elif mode == "nt":
        (m, k), (n, k2) = a.shape, b.shape
    else:
        (k, m), (k2, n) = a.shape, b.shape
    assert k == k2, (a.shape, b.shape, mode)
    tm, tn, tk, need = _mm_tiles(m, n, k, a.dtype.itemsize, b.dtype.itemsize, jnp.dtype(out_dtype).itemsize,
                                 add is not None)
    nk = k // tk
    dims = {"nn": (((1,), (0,)), ((), ())), "nt": (((1,), (1,)), ((), ())), "tn": (((0,), (0,)), ((), ()))}[mode]

    def body(*refs):
        a_ref, b_ref = refs[:2]
        add_ref = refs[2] if add is not None else None
        o_ref = refs[3] if add is not None else refs[2]
        part = lax.dot_general(a_ref[...].astype(BF16), b_ref[...].astype(BF16), dims, preferred_element_type=F32)

        def finish(r):
            if add_ref is not None:
                r = r + add_ref[...]
            o_ref[...] = r.astype(o_ref.dtype)

        if nk == 1:
            finish(part)
            return
        acc = refs[-1]
        kk = pl.program_id(2)

        @pl.when(kk == 0)
        def _():
            acc[...] = part

        @pl.when(kk > 0)
        def _():
            acc[...] += part

        @pl.when(kk == nk - 1)
        def _():
            finish(acc[...])

    if mode == "nn":
        a_spec = pl.BlockSpec((tm, tk), lambda i, j, kk: (i, kk))
        b_spec = pl.BlockSpec((tk, tn), lambda i, j, kk: (kk, j))
    elif mode == "nt":
        a_spec = pl.BlockSpec((tm, tk), lambda i, j, kk: (i, kk))
        b_spec = pl.BlockSpec((tn, tk), lambda i, j, kk: (j, kk))
    else:
        a_spec = pl.BlockSpec((tk, tm), lambda i, j, kk: (kk, i))
        b_spec = pl.BlockSpec((tk, tn), lambda i, j, kk: (kk, j))
    o_spec = pl.BlockSpec((tm, tn), lambda i, j, kk: (i, j))
    in_specs = [a_spec, b_spec] + ([o_spec] if add is not None else [])
    args = (a, b) + ((add,) if add is not None else ())
    return _blocked(
        body, name=name, grid=(m // tm, n // tn, nk),
        in_specs=in_specs, out_specs=o_spec,
        out_shape=jax.ShapeDtypeStruct((m, n), out_dtype),
        scratch_shapes=[pltpu.VMEM((tm, tn), F32)] if nk > 1 else [],
        compiler_params=_cparams(("parallel", "parallel", "arbitrary"), min(VMEM_BIG, need + VMEM_SLACK)),
    )(*args)


TOK = 256


def _rms(x, g):
    return x * lax.rsqrt(jnp.mean(x * x, axis=-1, keepdims=True) + NORM_EPS) * g


def _rms_fwd(x, g, name):
    t, d = x.shape

    def body(x_ref, g_ref, o_ref):
        o_ref[...] = _rms(x_ref[...], g_ref[...]).astype(BF16)

    row = pl.BlockSpec((TOK, d), lambda i: (i, 0))
    return _blocked(body, name=name, grid=(t // TOK,), in_specs=[row, _full((1, d))], out_specs=row,
                          out_shape=jax.ShapeDtypeStruct((t, d), BF16),
                          compiler_params=_cparams(("parallel",)))(x, g)


def _rms_bwd(x, g, dxn, res, name):
    t, d = x.shape

    def body(x_ref, g_ref, d_ref, res_ref, dx_ref, dg_ref):
        _, vjp = jax.vjp(_rms, x_ref[...], g_ref[...])
        dx, dg = vjp(d_ref[...].astype(F32))
        dx_ref[...] = dx + res_ref[...]

        @pl.when(pl.program_id(0) == 0)
        def _():
            dg_ref[...] = jnp.zeros_like(dg_ref)

        dg_ref[...] += dg

    row = pl.BlockSpec((TOK, d), lambda i: (i, 0))
    return _blocked(body, name=name, grid=(t // TOK,), in_specs=[row, _full((1, d)), row, row],
                          out_specs=[row, _full((1, d))],
                          out_shape=[jax.ShapeDtypeStruct((t, d), F32), jax.ShapeDtypeStruct((1, d), F32)],
                          compiler_params=_cparams(("arbitrary",)))(x, g, dxn, res)


def _loss_head(x, g, tgt):
    t, d = x.shape

    def body(x_ref, g_ref, t_ref, l_ref, dx_ref, dg_ref):
        tg = t_ref[...]

        def fn(xv, gv):
            err = _rms(xv, gv) - tg
            per_tok = jnp.mean(err * err, axis=-1, keepdims=True)
            return 0.5 * jnp.sum(per_tok, axis=0, keepdims=True)

        l, vjp = jax.vjp(fn, x_ref[...], g_ref[...])
        dx, dg = vjp(jnp.ones((1, 1), F32))
        dx_ref[...] = dx

        @pl.when(pl.program_id(0) == 0)
        def _():
            dg_ref[...] = jnp.zeros_like(dg_ref)
            l_ref[...] = jnp.zeros_like(l_ref)

        dg_ref[...] += dg
        l_ref[...] += jnp.broadcast_to(l, l_ref.shape)

    row = pl.BlockSpec((TOK, d), lambda i: (i, 0))
    return _blocked(body, name="loss_head", grid=(t // TOK,), in_specs=[row, _full((1, d)), row],
                          out_specs=[_full((1, LANES)), row, _full((1, d))],
                          out_shape=[jax.ShapeDtypeStruct((1, LANES), F32), jax.ShapeDtypeStruct((t, d), F32),
                                     jax.ShapeDtypeStruct((1, d), F32)],
                          compiler_params=_cparams(("arbitrary",)))(x, g, tgt)


def _glu_fwd(x, z):
    t, d = x.shape

    def body(x_ref, v_ref, g_ref, o_ref):
        o_ref[...] = x_ref[...] + v_ref[...] * jax.nn.sigmoid(g_ref[...])

    row = pl.BlockSpec((TOK, d), lambda i: (i, 0))
    gate = pl.BlockSpec((TOK, d), lambda i: (i, 1))
    return _blocked(body, name="glu_fwd", grid=(t // TOK,), in_specs=[row, row, gate], out_specs=row,
                          out_shape=jax.ShapeDtypeStruct((t, d), F32),
                          compiler_params=_cparams(("parallel",)))(x, z, z)


def _glu_bwd(z, g):
    t, d = g.shape

    def body(v_ref, g_ref, d_ref, o_ref):
        s = jax.nn.sigmoid(g_ref[...])
        dy = d_ref[...]
        o_ref[:, :d] = (dy * s).astype(BF16)
        o_ref[:, d:] = (dy * v_ref[...] * s * (1.0 - s)).astype(BF16)

    row = pl.BlockSpec((TOK, d), lambda i: (i, 0))
    gate = pl.BlockSpec((TOK, d), lambda i: (i, 1))
    return _blocked(body, name="glu_bwd", grid=(t // TOK,), in_specs=[row, gate, row],
                          out_specs=pl.BlockSpec((TOK, 2 * d), lambda i: (i, 0)),
                          out_shape=jax.ShapeDtypeStruct((t, 2 * d), BF16),
                          compiler_params=_cparams(("parallel",)))(z, z, g)


def _shift_down(x, d):
    row = lax.broadcasted_iota(jnp.int32, x.shape, 0)
    return jnp.where(row < d, 0.0, pltpu.roll(x, d, 0))


def _shift_up(x, d):
    n = x.shape[0]
    row = lax.broadcasted_iota(jnp.int32, x.shape, 0)
    return jnp.where(row >= n - d, 0.0, pltpu.roll(x, n - d, 0))


def _make_sd():
    @functools.partial(jax.custom_vjp, nondiff_argnums=(1,))
    def sd(x, d):
        return _shift_down(x, d)

    def fwd(x, d):
        return _shift_down(x, d), None

    def bwd(d, _, g):
        return (_shift_up(g, d),)

    sd.defvjp(fwd, bwd)
    return sd


def _lin_scan(a, u, reverse=False):
    n = a.shape[0]
    row = lax.broadcasted_iota(jnp.int32, a.shape, 0)
    d = 1
    while d < n:
        if reverse:
            keep = row < n - d
            a_s, u_s = pltpu.roll(a, n - d, 0), pltpu.roll(u, n - d, 0)
        else:
            keep = row >= d
            a_s, u_s = pltpu.roll(a, d, 0), pltpu.roll(u, d, 0)
        u = u + a * jnp.where(keep, u_s, 0.0)
        a = a * jnp.where(keep, a_s, 1.0)
        d *= 2
    return u


def _make_scan():
    @jax.custom_vjp
    def scan(a, u):
        return _lin_scan(a, u)

    def fwd(a, u):
        h = _lin_scan(a, u)
        return h, (a, h)

    def bwd(res, dh):
        a, h = res
        g = _lin_scan(_shift_up(a, 1), dh, reverse=True)
        return g * _shift_down(h, 1), g

    scan.defvjp(fwd, bwd)
    return scan


def _acc_out(ref, val):
    @pl.when(pl.program_id(0) == 0)
    def _():
        ref[...] = jnp.zeros_like(ref)

    ref[...] += val


FFN_CW = 128


def _ffn_fn(hg, hv, wg, wv, bg, bv, sd):
    cg = wg[0:1] * sd(hg, 2) + wg[1:2] * sd(hg, 1) + wg[2:3] * hg + bg
    cv = wv[0:1] * sd(hv, 2) + wv[1:2] * sd(hv, 1) + wv[2:3] * hv + bv
    return jax.nn.silu(cg) * cv


def _ffn_specs(t):
    nb = D_FF // FFN_CW
    col = lambda r, off: pl.BlockSpec((r, FFN_CW), lambda j: (0, j + off))
    return nb, [col(t, 0), col(t, nb), col(3, 0), col(3, nb), col(1, 0), col(1, nb)], col


def _ffn_mid_fwd(h, cw, cb, name):
    t = h.shape[0]
    nb, in_specs, col = _ffn_specs(t)

    def body(hg, hv, wg, wv, bg, bv, o_ref):
        o_ref[...] = _ffn_fn(hg[...], hv[...], wg[...], wv[...], bg[...], bv[...], _shift_down).astype(BF16)

    return _blocked(body, name=name, grid=(nb,), in_specs=in_specs, out_specs=col(t, 0),
                          out_shape=jax.ShapeDtypeStruct((t, D_FF), BF16),
                          compiler_params=_cparams(("parallel",), VMEM_MID))(h, h, cw, cw, cb, cb)


def _ffn_mid_bwd(h, cw, cb, dact, name):
    t = h.shape[0]
    nb, in_specs, col = _ffn_specs(t)

    def body(hg, hv, wg, wv, bg, bv, d_ref, dhg, dhv, dwg, dwv, dbg, dbv):
        fn = functools.partial(_ffn_fn, sd=_make_sd())
        _, vjp = jax.vjp(fn, hg[...], hv[...], wg[...], wv[...], bg[...], bv[...])
        g = vjp(d_ref[...])
        dhg[...] = g[0].astype(BF16)
        dhv[...] = g[1].astype(BF16)
        dwg[...], dwv[...], dbg[...], dbv[...] = g[2], g[3], g[4], g[5]

    big = jax.ShapeDtypeStruct((t, D_FF), BF16)
    w3 = jax.ShapeDtypeStruct((3, D_FF), F32)
    b1 = jax.ShapeDtypeStruct((1, D_FF), F32)
    return _blocked(body, name=name, grid=(nb,), in_specs=in_specs + [col(t, 0)],
                          out_specs=[col(t, 0), col(t, 0), col(3, 0), col(3, 0), col(1, 0), col(1, 0)],
                          out_shape=[big, big, w3, w3, b1, b1],
                          compiler_params=_cparams(("parallel",), VMEM_BIG))(h, h, cw, cw, cb, cb, dact)


TS_CW = 256


def _tshift_fn(p, mu, sd):
    return p + mu * (sd(p, 1) - p)


def _tshift_fwd(p, mu):
    t = p.shape[0]
    col = lambda r: pl.BlockSpec((r, TS_CW), lambda j: (0, j))

    def body(p_ref, mu_ref, o_ref):
        o_ref[...] = _tshift_fn(p_ref[...], mu_ref[...], _shift_down)

    return _blocked(body, name="tshift_fwd", grid=(SHIFT_COLS // TS_CW,), in_specs=[col(t), col(1)],
                          out_specs=col(t), out_shape=jax.ShapeDtypeStruct((t, SHIFT_COLS), F32),
                          compiler_params=_cparams(("parallel",), VMEM_MID))(p, mu)


def _tshift_bwd(p, mu, dpam):
    t = p.shape[0]
    col = lambda r: pl.BlockSpec((r, TS_CW), lambda j: (0, j))

    def body(p_ref, mu_ref, d_ref, dp_ref, dmu_ref):
        _, vjp = jax.vjp(functools.partial(_tshift_fn, sd=_make_sd()), p_ref[...], mu_ref[...])
        dp, dmu = vjp(d_ref[...])
        dp_ref[...] = dp.astype(BF16)
        dmu_ref[...] = dmu

    return _blocked(body, name="tshift_bwd", grid=(SHIFT_COLS // TS_CW,), in_specs=[col(t), col(1), col(t)],
                          out_specs=[col(t), col(1)],
                          out_shape=[jax.ShapeDtypeStruct((t, SHIFT_COLS), BF16),
                                     jax.ShapeDtypeStruct((1, SHIFT_COLS), F32)],
                          compiler_params=_cparams(("parallel",), VMEM_MID))(p, mu, dpam)


_HI = lax.Precision.HIGHEST
_O = (0, RW, 2 * RW, 3 * RW, 3 * RW + W_LORA, 3 * RW + W_LORA + A_LORA, SHIFT_COLS)


def _dot16(a, b, dims=(((1,), (0,)), ((), ()))):
    return lax.dot_general(a.astype(BF16), b.astype(BF16), dims, preferred_element_type=F32)


def _make_dot16():
    @jax.custom_vjp
    def dot(a, b):
        return _dot16(a, b)

    def fwd(a, b):
        return _dot16(a, b), (a, b)

    def bwd(res, g):
        a, b = res
        return _dot16(g, b, (((1,), (1,)), ((), ()))), _dot16(a, g, (((0,), (0,)), ((), ())))

    dot.defvjp(fwd, bwd)
    return dot


def _seg(x, gm):
    return jnp.dot(x, gm, precision=_HI)


def _prep_fn(r, k, v, wd, ad, gd, w0, w2, a0, a2, g2, k_k, k_a, gm, dot):
    w_log = -jax.nn.softplus(-(w0 + dot(jnp.tanh(wd), w2))) - 0.5
    decay = jnp.exp(-jnp.exp(w_log))
    a = jax.nn.sigmoid(a0 + dot(ad, a2))
    g = dot(jax.nn.sigmoid(gd), g2)
    kk = k * k_k
    kk = kk / jnp.maximum(jnp.sqrt(_seg(kk * kk, gm)), 1e-12)
    k2 = k * (1.0 + (a - 1.0) * k_a)
    return r, decay, k2, v, -kk, kk * a, g


_PREP_W = ("w0", "w2", "a0", "a2", "g2", "k_k", "k_a")


def _prep_wspecs(w):
    return [_full(w[n].shape) for n in _PREP_W] + [_full((RW, RW))]


def _rwkv_prep_fwd(pam, w, gm):
    t = pam.shape[0]

    def body(p_ref, *refs):
        wr, outs = refs[:8], refs[8:]
        pieces = [p_ref[:, _O[i]:_O[i + 1]] for i in range(6)]
        res = _prep_fn(*pieces, *[x[...] for x in wr], _dot16)
        for o, val in zip(outs, res):
            o[...] = val

    row = lambda c: pl.BlockSpec((TOK, c), lambda i: (i, 0))
    return _blocked(body, name="rwkv_prep_fwd", grid=(t // TOK,),
                          in_specs=[row(SHIFT_COLS)] + _prep_wspecs(w), out_specs=[row(RW)] * 7,
                          out_shape=[jax.ShapeDtypeStruct((t, RW), F32)] * 7,
                          compiler_params=_cparams(("parallel",), VMEM_MID))(pam, *[w[n] for n in _PREP_W], gm)


def _rwkv_prep_bwd(pam, w, gm, cts, more):
    t = pam.shape[0]

    def body(p_ref, *refs):
        wr, ct, ex, dp_ref, dws = refs[:8], refs[8:15], refs[15:18], refs[18], refs[19:]
        pieces = [p_ref[:, _O[i]:_O[i + 1]] for i in range(6)]
        fn = lambda *a: _prep_fn(*a, wr[7][...], _make_dot16())
        _, vjp = jax.vjp(fn, *pieces, *[x[...] for x in wr[:7]])
        c = [x[...] for x in ct]
        c[0] = c[0] + ex[0][...]
        c[2] = c[2] + ex[1][...]
        c[3] = c[3] + ex[2][...]
        g = vjp(tuple(c))
        for i in range(6):
            dp_ref[:, _O[i]:_O[i + 1]] = g[i]
        for o, val in zip(dws, g[6:]):
            _acc_out(o, val)

    row = lambda c: pl.BlockSpec((TOK, c), lambda i: (i, 0))
    return _blocked(body, name="rwkv_prep_bwd", grid=(t // TOK,),
                          in_specs=[row(SHIFT_COLS)] + _prep_wspecs(w) + [row(RW)] * 10,
                          out_specs=[row(SHIFT_COLS)] + [_full(w[n].shape) for n in _PREP_W],
                          out_shape=[jax.ShapeDtypeStruct((t, SHIFT_COLS), F32)]
                          + [jax.ShapeDtypeStruct(w[n].shape, F32) for n in _PREP_W],
                          compiler_params=_cparams(("arbitrary",), VMEM_MID))(
                              pam, *[w[n] for n in _PREP_W], gm, *cts, *more)


def _post_fn(y, r, k2, v, g, ln_w, ln_b, r_k, gm):
    inv = 1.0 / HEAD
    d = y - _seg(y, gm) * inv
    yn = d * lax.rsqrt(_seg(d * d, gm) * inv + GN_EPS) * ln_w + ln_b
    bonus = _seg(r * k2 * r_k, gm) * v
    return (yn + bonus) * g


def _rwkv_post_fwd(y, r, k2, v, g, ln_w, ln_b, r_k, gm):
    t = y.shape[0]

    def body(*refs):
        o_ref = refs[-1]
        o_ref[...] = _post_fn(*[x[...] for x in refs[:-1]]).astype(BF16)

    row = pl.BlockSpec((TOK, RW), lambda i: (i, 0))
    return _blocked(body, name="rwkv_post_fwd", grid=(t // TOK,),
                          in_specs=[row] * 5 + [_full((1, RW))] * 3 + [_full((RW, RW))], out_specs=row,
                          out_shape=jax.ShapeDtypeStruct((t, RW), BF16),
                          compiler_params=_cparams(("parallel",), VMEM_MID))(y, r, k2, v, g, ln_w, ln_b, r_k, gm)


def _rwkv_post_bwd(y, r, k2, v, g, ln_w, ln_b, r_k, gm, dya):
    t = y.shape[0]

    def body(*refs):
        ins, gm_ref, d_ref, outs = refs[:8], refs[8], refs[9], refs[10:]
        fn = lambda *a: _post_fn(*a, gm_ref[...])
        _, vjp = jax.vjp(fn, *[x[...] for x in ins])
        gr = vjp(d_ref[...])
        for o, val in zip(outs[:5], gr[:5]):
            o[...] = val
        for o, val in zip(outs[5:], gr[5:]):
            _acc_out(o, val)

    row = pl.BlockSpec((TOK, RW), lambda i: (i, 0))
    vec = _full((1, RW))
    return _blocked(body, name="rwkv_post_bwd", grid=(t // TOK,),
                          in_specs=[row] * 5 + [vec] * 3 + [_full((RW, RW)), row],
                          out_specs=[row] * 5 + [vec] * 3,
                          out_shape=[jax.ShapeDtypeStruct((t, RW), F32)] * 5 + [jax.ShapeDtypeStruct((1, RW), F32)] * 3,
                          compiler_params=_cparams(("arbitrary",), VMEM_MID))(y, r, k2, v, g, ln_w, ln_b, r_k, gm, dya)


def _from_pt(x):
    n = x.shape[0]
    return x.reshape(n, HEAD, N_HEADS, PT).transpose(0, 3, 2, 1).reshape(n * PT, N_HEADS * HEAD)


def _lane_sum(x):
    return jnp.sum(x, axis=-1, keepdims=True)


def _pair_consts():
    lane = lax.broadcasted_iota(jnp.int32, (HEAD, LANES), 1)
    return lane, lane < HEAD


def _seg_sum_pair(x, first):
    return jnp.where(first, _lane_sum(jnp.where(first, x, 0.0)), _lane_sum(jnp.where(first, 0.0, x)))


def _to_pt(x):
    t = x.shape[0]
    return x.reshape(t // PT, PT, N_HEADS, HEAD).transpose(0, 3, 2, 1).reshape(t // PT, HEAD, N_HEADS * PT)


def _expand_cols(x, name):
    t = x.shape[0]
    tiles = WKV_CHUNK // PT

    def body(x_ref, o_ref):
        _, first = _pair_consts()
        for tl in range(tiles):
            tile = x_ref[tl]
            for j in range(PT):
                for p in range(N_HEADS // 2):
                    src = jnp.where(first, (2 * p) * PT + j, (2 * p + 1) * PT + j)
                    o_ref[tl * PT + j, :, p * LANES:(p + 1) * LANES] = jnp.take_along_axis(tile, src, axis=1)

    return _blocked(
        body, name=name, grid=(t // WKV_CHUNK,),
        in_specs=[pl.BlockSpec((tiles, HEAD, LANES), lambda i: (i, 0, 0))],
        out_specs=pl.BlockSpec((WKV_CHUNK, HEAD, RW), lambda i: (i, 0, 0)),
        out_shape=jax.ShapeDtypeStruct((t, HEAD, RW), F32),
        compiler_params=_cparams(("parallel",), VMEM_MID))(_to_pt(x))


def _wkv_fwd(w, k, z, b, v_exp):
    t = w.shape[0]
    nc = t // WKV_CHUNK
    pairs = N_HEADS // 2

    def body(w_ref, k_ref, z_ref, b_ref, v_ref, s_all, s_ref):
        @pl.when(pl.program_id(0) == 0)
        def _():
            s_ref[...] = jnp.zeros_like(s_ref)

        _, first = _pair_consts()

        def group(gi, carry):
            base = pl.multiple_of(gi * 8, 8)
            rows = [ref[pl.ds(base, 8), :] for ref in (w_ref, k_ref, z_ref, b_ref)]
            s = [s_ref[:, p * LANES:(p + 1) * LANES] for p in range(pairs)]
            for jj in range(8):
                for p in range(pairs):
                    cs = slice(p * LANES, (p + 1) * LANES)
                    wr, kr, zr, br = [x[jj:jj + 1, cs] for x in rows]
                    s_all[base + jj, :, cs] = s[p]
                    sa = _seg_sum_pair(s[p] * zr, first)
                    s[p] = s[p] * wr + sa * br + v_ref[base + jj, :, cs] * kr
            for p in range(pairs):
                s_ref[:, p * LANES:(p + 1) * LANES] = s[p]
            return carry

        lax.fori_loop(0, WKV_CHUNK // 8, group, 0)

    row = pl.BlockSpec((WKV_CHUNK, RW), lambda i: (i, 0))
    big = pl.BlockSpec((WKV_CHUNK, HEAD, RW), lambda i: (i, 0, 0))
    return _blocked(
        body, name="wkv_fwd", grid=(nc,), in_specs=[row] * 4 + [big], out_specs=[big, _full((HEAD, RW))],
        out_shape=[jax.ShapeDtypeStruct((t, HEAD, RW), F32), jax.ShapeDtypeStruct((HEAD, RW), F32)],
        compiler_params=_cparams(("arbitrary",), VMEM_MID))(w, k, z, b, v_exp)


def _wkv_out(r, s_all, s_last):
    t = r.shape[0]
    nc = t // WKV_CHUNK
    tiles = WKV_CHUNK // PT
    pairs = N_HEADS // 2

    def body(r_ref, s_ref, nxt_ref, last_ref, y_ref):
        lane, first = _pair_consts()
        after = jnp.where(pl.program_id(0) == nc - 1, last_ref[...], nxt_ref[0])
        for tl in range(tiles):
            ytile = jnp.zeros((HEAD, LANES), F32)
            for g in range(PT // 8):
                rows = r_ref[tl * PT + g * 8:tl * PT + g * 8 + 8, :]
                for jj in range(8):
                    tt = tl * PT + g * 8 + jj
                    j = g * 8 + jj
                    for p in range(pairs):
                        cs = slice(p * LANES, (p + 1) * LANES)
                        s = s_ref[tt + 1, :, cs] if tt + 1 < WKV_CHUNK else after[:, cs]
                        pr = s * rows[jj:jj + 1, cs]
                        y0 = _lane_sum(jnp.where(first, pr, 0.0))
                        y1 = _lane_sum(jnp.where(first, 0.0, pr))
                        ytile = jnp.where(lane == (2 * p) * PT + j, y0, ytile)
                        ytile = jnp.where(lane == (2 * p + 1) * PT + j, y1, ytile)
            y_ref[tl] = ytile

    row = pl.BlockSpec((WKV_CHUNK, RW), lambda i: (i, 0))
    pt = pl.BlockSpec((tiles, HEAD, LANES), lambda i: (i, 0, 0))
    big = pl.BlockSpec((WKV_CHUNK, HEAD, RW), lambda i: (i, 0, 0))
    nxt = pl.BlockSpec((1, HEAD, RW), lambda i: (jnp.minimum((i + 1) * WKV_CHUNK, t - 1), 0, 0))
    return _blocked(
        body, name="wkv_out", grid=(nc,), in_specs=[row, big, nxt, _full((HEAD, RW))], out_specs=pt,
        out_shape=jax.ShapeDtypeStruct((t // PT, HEAD, LANES), F32),
        compiler_params=_cparams(("parallel",), VMEM_MID))(r, s_all, s_all, s_last)


def _wkv_bwd(r, w, k, z, b, v_exp, s_all, dy_exp):
    t = r.shape[0]
    nc = t // WKV_CHUNK
    tiles = WKV_CHUNK // PT
    pairs = N_HEADS // 2

    def body(r_ref, w_ref, k_ref, z_ref, b_ref, v_ref, s_all_ref, dy_ref,
             dr_ref, dw_ref, dk_ref, dz_ref, db_ref, dv_ref, ds_ref):
        @pl.when(pl.program_id(0) == 0)
        def _():
            ds_ref[...] = jnp.zeros_like(ds_ref)

        lane, first = _pair_consts()
        col_sum = lambda x: jnp.sum(x, axis=0, keepdims=True)
        row8 = lax.broadcasted_iota(jnp.int32, (8, LANES), 0)
        for tl in reversed(range(tiles)):
            def group(gg, dvtile):
                gi = PT // 8 - 1 - gg
                base = pl.multiple_of(tl * PT + gi * 8, 8)
                rows = [ref[pl.ds(base, 8), :] for ref in (r_ref, w_ref, k_ref, z_ref, b_ref)]
                outs = (dr_ref, dw_ref, dk_ref, dz_ref, db_ref)
                tiles8 = {(id(o), p): jnp.zeros((8, LANES), F32) for o in outs for p in range(pairs)}
                ds = [ds_ref[:, p * LANES:(p + 1) * LANES] for p in range(pairs)]
                for jj in reversed(range(8)):
                    j = gi * 8 + jj
                    for p in range(pairs):
                        cs = slice(p * LANES, (p + 1) * LANES)

                        def put(ref, val, p=p, jj=jj):
                            tiles8[(id(ref), p)] = jnp.where(row8 == jj, val, tiles8[(id(ref), p)])

                        rr, wr, kr, zr, br = [x[jj:jj + 1, cs] for x in rows]
                        sp = s_all_ref[base + jj, :, cs]
                        vc = v_ref[base + jj, :, cs]
                        dyc = dy_ref[base + jj, :, cs]
                        sa = _seg_sum_pair(sp * zr, first)
                        st = sp * wr + sa * br + vc * kr
                        d = ds[p] + dyc * rr
                        put(dr_ref, col_sum(st * dyc))
                        dvk = d * kr
                        dv0 = _lane_sum(jnp.where(first, dvk, 0.0))
                        dv1 = _lane_sum(jnp.where(first, 0.0, dvk))
                        dvtile = jnp.where(lane == (2 * p) * PT + j, dv0, dvtile)
                        dvtile = jnp.where(lane == (2 * p + 1) * PT + j, dv1, dvtile)
                        put(dk_ref, col_sum(d * vc))
                        put(dw_ref, col_sum(sp * d))
                        u = _seg_sum_pair(d * br, first)
                        put(dz_ref, col_sum(sp * u))
                        put(db_ref, col_sum(d * sa))
                        ds[p] = d * wr + u * zr
                for p in range(pairs):
                    ds_ref[:, p * LANES:(p + 1) * LANES] = ds[p]
                for o in outs:
                    for p in range(pairs):
                        o[pl.ds(base, 8), p * LANES:(p + 1) * LANES] = tiles8[(id(o), p)]
                return dvtile

            dv_ref[tl] = lax.fori_loop(0, PT // 8, group, jnp.zeros((HEAD, LANES), F32))

    rev = lambda i: nc - 1 - i
    row = pl.BlockSpec((WKV_CHUNK, RW), lambda i: (rev(i), 0))
    pt = pl.BlockSpec((tiles, HEAD, LANES), lambda i: (rev(i), 0, 0))
    big = pl.BlockSpec((WKV_CHUNK, HEAD, RW), lambda i: (rev(i), 0, 0))
    return _blocked(
        body, name="wkv_bwd", grid=(nc,), in_specs=[row] * 5 + [big, big, big], out_specs=[row] * 5 + [pt],
        out_shape=[jax.ShapeDtypeStruct((t, RW), F32)] * 5 + [jax.ShapeDtypeStruct((t // PT, HEAD, LANES), F32)],
        scratch_shapes=[pltpu.VMEM((HEAD, RW), F32)],
        compiler_params=_cparams(("arbitrary",), VMEM_BIG))(r, w, k, z, b, v_exp, s_all, dy_exp)


LRU_CW = 128
_BX0 = SHIFT_COLS // LRU_CW
_BG0 = (SHIFT_COLS + LRU_W) // LRU_CW


def _lru_fn(bx, bg, cw, cb, ga, ba, gx, bxb, lam, sd, scan, dot):
    xc = cw[0:1] * sd(bx, 3) + cw[1:2] * sd(bx, 2) + cw[2:3] * sd(bx, 1) + cw[3:4] * bx + cb
    gr = jax.nn.sigmoid(dot(xc, ga) + ba)
    gi = jax.nn.sigmoid(dot(xc, gx) + bxb)
    log_a = -LRU_C * gr * jax.nn.softplus(-lam)
    a = jnp.exp(log_a)
    mult = jnp.sqrt(-jnp.tanh(log_a) * (jnp.exp(2.0 * log_a) + 1.0))
    return scan(a, xc * gi * mult) * jax.nn.gelu(bg)


def _lru_specs(t):
    col = lambda r, off=0: pl.BlockSpec((r, LRU_CW), lambda j: (0, j + off))
    diag = pl.BlockSpec((LRU_CW, LRU_CW), lambda j: (j, j))
    return col, [col(t, _BX0), col(t, _BG0), col(4), col(1), diag, col(1), diag, col(1), col(1)]


def _lru_fwd(p, cw, cb, ga, ba, gx, bxb, lam):
    t = p.shape[0]
    col, in_specs = _lru_specs(t)

    def body(*refs):
        o_ref = refs[-1]
        o_ref[...] = _lru_fn(*[x[...] for x in refs[:-1]], _shift_down, _lin_scan, _dot16).astype(BF16)

    return _blocked(body, name="lru_fwd", grid=(LRU_W // LRU_CW,), in_specs=in_specs, out_specs=col(t),
                          out_shape=jax.ShapeDtypeStruct((t, LRU_W), BF16),
                          compiler_params=_cparams(("parallel",), VMEM_MID))(p, p, cw, cb, ga, ba, gx, bxb, lam)


def _lru_bwd(p, cw, cb, ga, ba, gx, bxb, lam, dyb):
    t = p.shape[0]
    col, in_specs = _lru_specs(t)

    def body(*refs):
        ins, d_ref, outs = refs[:9], refs[9], refs[10:]
        fn = functools.partial(_lru_fn, sd=_make_sd(), scan=_make_scan(), dot=_make_dot16())
        _, vjp = jax.vjp(fn, *[x[...] for x in ins])
        g = vjp(d_ref[...])
        outs[0][...] = g[0].astype(BF16)
        outs[1][...] = g[1].astype(BF16)
        for o, val in zip(outs[2:], g[2:]):
            o[...] = val

    sq = pl.BlockSpec((LRU_CW, LRU_CW), lambda j: (j, 0))
    act = jax.ShapeDtypeStruct((t, LRU_W), BF16)
    vec = jax.ShapeDtypeStruct((1, LRU_W), F32)
    sqs = jax.ShapeDtypeStruct((LRU_W, LRU_CW), F32)
    return _blocked(body, name="lru_bwd", grid=(LRU_W // LRU_CW,), in_specs=in_specs + [col(t, RW // LRU_CW)],
                          out_specs=[col(t), col(t), col(4), col(1), sq, col(1), sq, col(1), col(1)],
                          out_shape=[act, act, jax.ShapeDtypeStruct((4, LRU_W), F32), vec, sqs, vec, sqs, vec, vec],
                          compiler_params=_cparams(("parallel",), VMEM_BIG))(p, p, cw, cb, ga, ba, gx, bxb, lam, dyb)


def _s5_disc_fn(a_re, a_im, log_dt, b_re, b_im, e):
    lam_re = jnp.minimum(a_re, -1e-4)
    lam_im = a_im
    dt = jnp.exp(log_dt)
    mag = jnp.exp(lam_re * dt)
    ab_re = mag * jnp.cos(lam_im * dt)
    ab_im = mag * jnp.sin(lam_im * dt)
    den = lam_re * lam_re + lam_im * lam_im
    zr = ab_re - 1.0
    q_re = jnp.dot((zr * lam_re + ab_im * lam_im) / den, e, precision=_HI)
    q_im = jnp.dot((ab_im * lam_re - zr * lam_im) / den, e, precision=_HI)
    return ab_re, ab_im, q_re * b_re - q_im * b_im, q_re * b_im + q_im * b_re


def _s5_disc_fwd(a_re, a_im, log_dt, b_re, b_im, e):
    def body(*refs):
        res = _s5_disc_fn(*[x[...] for x in refs[:6]])
        for o, val in zip(refs[6:], res):
            o[...] = val

    small = jax.ShapeDtypeStruct(a_re.shape, F32)
    wide = jax.ShapeDtypeStruct(b_re.shape, F32)
    return pl.pallas_call(body, name="s5_disc_fwd", out_shape=[small, small, wide, wide])(
        a_re, a_im, log_dt, b_re, b_im, e)


def _s5_disc_bwd(a_re, a_im, log_dt, b_re, b_im, e, cts):
    def body(*refs):
        ins, e_ref, ct, outs = refs[:5], refs[5], refs[6:10], refs[10:]
        _, vjp = jax.vjp(lambda *a: _s5_disc_fn(*a, e_ref[...]), *[x[...] for x in ins])
        for o, val in zip(outs, vjp(tuple(c[...] for c in ct))):
            o[...] = val

    shapes = [jax.ShapeDtypeStruct(x.shape, F32) for x in (a_re, a_im, log_dt, b_re, b_im)]
    return pl.pallas_call(body, name="s5_disc_bwd", out_shape=shapes)(a_re, a_im, log_dt, b_re, b_im, e, *cts)


def _cmul(a, b):
    return a[0] * b[0] - a[1] * b[1], a[0] * b[1] + a[1] * b[0]


def _s5_scan(sr, si, ab, reverse):
    n_tiles = sr.shape[0] // 8
    width = sr.shape[1]
    row8 = lax.broadcasted_iota(jnp.int32, (8, width), 0)
    p1 = ab
    p2 = _cmul(p1, p1)
    p4 = _cmul(p2, p2)
    pw = [p1]
    for _ in range(7):
        pw.append(_cmul(pw[-1], p1))
    cr = jnp.zeros((8, width), F32)
    ci = jnp.zeros((8, width), F32)
    for j in range(8):
        e = pw[7 - j] if reverse else pw[j]
        cr = jnp.where(row8 == j, e[0], cr)
        ci = jnp.where(row8 == j, e[1], ci)

    levels = []
    for d, q in ((1, p1), (2, p2), (4, p4)):
        keep = row8 < 8 - d if reverse else row8 >= d
        levels.append((d, (jnp.where(keep, q[0], 0.0), jnp.where(keep, q[1], 0.0))))

    def tile(i, carry):
        idx = n_tiles - 1 - i if reverse else i
        base = pl.multiple_of(idx * 8, 8)
        x = (sr[pl.ds(base, 8), :], si[pl.ds(base, 8), :])
        for d, q in levels:
            amt = 8 - d if reverse else d
            m = _cmul(q, (pltpu.roll(x[0], amt, 0), pltpu.roll(x[1], amt, 0)))
            x = (x[0] + m[0], x[1] + m[1])
        m = _cmul((cr, ci), carry)
        x = (x[0] + m[0], x[1] + m[1])
        sr[pl.ds(base, 8), :] = x[0]
        si[pl.ds(base, 8), :] = x[1]
        edge = slice(0, 1) if reverse else slice(7, 8)
        return x[0][edge], x[1][edge]

    zero = jnp.zeros((1, width), F32)
    lax.fori_loop(0, n_tiles, tile, (zero, zero))


_S5_W = S5_SLAB // S5_GROUP * S5_STATE


def _s5_specs(t):
    col = lambda r: pl.BlockSpec((r, S5_SLAB), lambda j: (0, j))
    bb = pl.BlockSpec((None, S5_SLAB, _S5_W), lambda j: (j, 0, 0))
    cd = pl.BlockSpec((None, _S5_W, S5_SLAB), lambda j: (j, 0, 0))
    ab = pl.BlockSpec((None, 1, _S5_W), lambda j: (j, 0, 0))
    return col, bb, cd, ab


def _s5_fwd(u, dvec, bbr, bbi, cdr, cdi, abr, abi):
    t, width = u.shape
    col, bb, cd, ab = _s5_specs(t)

    def body(u_ref, d_ref, bbr_ref, bbi_ref, cdr_ref, cdi_ref, abr_ref, abi_ref, o_ref, sr, si):
        uv = u_ref[...]
        sr[...] = _dot16(uv, bbr_ref[...])
        si[...] = _dot16(uv, bbi_ref[...])
        _s5_scan(sr, si, (abr_ref[...], abi_ref[...]), False)
        y = _dot16(sr[...], cdr_ref[...]) - _dot16(si[...], cdi_ref[...])
        o_ref[...] = jax.nn.gelu(y + d_ref[...] * uv).astype(BF16)

    return _blocked(body, name="s5_fwd", grid=(width // S5_SLAB,),
                          in_specs=[col(t), col(1), bb, bb, cd, cd, ab, ab], out_specs=col(t),
                          out_shape=jax.ShapeDtypeStruct((t, width), BF16),
                          scratch_shapes=[pltpu.VMEM((t, _S5_W), F32)] * 2,
                          compiler_params=_cparams(("parallel",), VMEM_BIG))(u, dvec, bbr, bbi, cdr, cdi, abr, abi)


def _s5_bwd(u, dvec, bbr, bbi, cdr, cdi, abr, abi, dyact):
    t, width = u.shape
    col, bb, cd, ab = _s5_specs(t)
    ns = width // S5_SLAB
    tn = (((0,), (0,)), ((), ()))
    nt = (((1,), (1,)), ((), ()))

    def body(u_ref, d_ref, bbr_ref, bbi_ref, cdr_ref, cdi_ref, abr_ref, abi_ref, dy_ref,
             du_ref, dd_ref, dbbr_ref, dbbi_ref, dcdr_ref, dcdi_ref, dabr_ref, dabi_ref, sr, si, gr, gi):
        uv = u_ref[...]
        dv = d_ref[...]
        abv = (abr_ref[...], abi_ref[...])
        sr[...] = _dot16(uv, bbr_ref[...])
        si[...] = _dot16(uv, bbi_ref[...])
        _s5_scan(sr, si, abv, False)
        y = _dot16(sr[...], cdr_ref[...]) - _dot16(si[...], cdi_ref[...])
        _, vjp = jax.vjp(jax.nn.gelu, y + dv * uv)
        (dpre,) = vjp(dy_ref[...].astype(F32))
        dd_ref[...] = jnp.sum(dpre * uv, axis=0, keepdims=True)
        dcdr_ref[...] = _dot16(sr[...], dpre, tn)
        dcdi_ref[...] = -_dot16(si[...], dpre, tn)
        gr[...] = _dot16(dpre, cdr_ref[...], nt)
        gi[...] = -_dot16(dpre, cdi_ref[...], nt)
        _s5_scan(gr, gi, (abv[0], -abv[1]), True)

        row8 = lax.broadcasted_iota(jnp.int32, (8, _S5_W), 0)

        def tile(i, carry):
            acc_r, acc_i, last_r, last_i = carry
            base = pl.multiple_of(i * 8, 8)
            s_r, s_i = sr[pl.ds(base, 8), :], si[pl.ds(base, 8), :]
            g_r, g_i = gr[pl.ds(base, 8), :], gi[pl.ds(base, 8), :]
            p_r = jnp.where(row8 == 0, last_r, pltpu.roll(s_r, 1, 0))
            p_i = jnp.where(row8 == 0, last_i, pltpu.roll(s_i, 1, 0))
            acc_r = acc_r + jnp.sum(g_r * p_r + g_i * p_i, axis=0, keepdims=True)
            acc_i = acc_i + jnp.sum(g_i * p_r - g_r * p_i, axis=0, keepdims=True)
            return acc_r, acc_i, s_r[7:8], s_i[7:8]

        zero = jnp.zeros((1, _S5_W), F32)
        acc_r, acc_i, _, _ = lax.fori_loop(0, t // 8, tile, (zero, zero, zero, zero))
        dabr_ref[...] = acc_r
        dabi_ref[...] = acc_i
        du_ref[...] = dpre * dv + _dot16(gr[...], bbr_ref[...], nt) + _dot16(gi[...], bbi_ref[...], nt)
        dbbr_ref[...] = _dot16(uv, gr[...], tn)
        dbbi_ref[...] = _dot16(uv, gi[...], tn)

    sds = jax.ShapeDtypeStruct
    return _blocked(
        body, name="s5_bwd", grid=(ns,), in_specs=[col(t), col(1), bb, bb, cd, cd, ab, ab, col(t)],
        out_specs=[col(t), col(1), bb, bb, cd, cd, ab, ab],
        out_shape=[sds((t, width), F32), sds((1, width), F32), sds((ns, S5_SLAB, _S5_W), F32),
                   sds((ns, S5_SLAB, _S5_W), F32), sds((ns, _S5_W, S5_SLAB), F32), sds((ns, _S5_W, S5_SLAB), F32),
                   sds((ns, 1, _S5_W), F32), sds((ns, 1, _S5_W), F32)],
        scratch_shapes=[pltpu.VMEM((t, _S5_W), F32)] * 4,
        compiler_params=_cparams(("parallel",), VMEM_BIG))(u, dvec, bbr, bbi, cdr, cdi, abr, abi, dyact)


def _gate_dense(w):
    h = w.shape[0]
    return jnp.einsum("hij,hg->higj", w, jnp.eye(h, dtype=F32)).reshape(h * HEAD, h * HEAD)


def _gate_blocks(d):
    x = d.reshape(LRU_W // LRU_CW, 2, HEAD, 2, HEAD)
    return jnp.einsum("tgihj,gh->tgij", x, jnp.eye(2, dtype=F32)).reshape(LRU_W // HEAD, HEAD, HEAD)


_GPS = S5_SLAB // S5_GROUP
_NS = S5_GROUPS // _GPS


def _s5_in_dense(bb):
    x = bb.reshape(_NS, _GPS, S5_STATE, S5_GROUP)
    return jnp.einsum("sgnc,gh->sgchn", x, jnp.eye(_GPS, dtype=F32)).reshape(_NS, S5_SLAB, _S5_W)


def _s5_in_blocks(d):
    x = d.reshape(_NS, _GPS, S5_GROUP, _GPS, S5_STATE)
    return jnp.einsum("sgchn,gh->sgnc", x, jnp.eye(_GPS, dtype=F32)).reshape(S5_GROUPS, S5_STATE * S5_GROUP)


def _s5_out_dense(c):
    x = c.reshape(_NS, _GPS, S5_GROUP, S5_STATE)
    return jnp.einsum("sgcn,gh->shngc", x, jnp.eye(_GPS, dtype=F32)).reshape(_NS, _S5_W, S5_SLAB)


def _s5_out_blocks(d):
    x = d.reshape(_NS, _GPS, S5_STATE, _GPS, S5_GROUP)
    return jnp.einsum("shngc,gh->sgcn", x, jnp.eye(_GPS, dtype=F32)).reshape(S5_GROUPS, S5_GROUP, S5_STATE)


def _local_step(x, tgt, w, late_weights, send_grads):
    d_model = x.shape[1]
    gs = {}
    gm = jnp.kron(jnp.eye(N_HEADS, dtype=F32), jnp.ones((HEAD, HEAD), F32))
    n_layers = w["f_norm_g"].shape[0]

    def ffn_fwd(xin, l):
        xn = _rms_fwd(xin, w["f_norm_g"][l:l + 1], f"rms_f{l}")
        h = _matmul(xn, w["f_w_up_t"][l], "nt", f"mm_f{l}_up")
        act = _ffn_mid_fwd(h, w["f_conv_w"][l], w["f_conv_b"][l:l + 1], f"ffn_mid_fwd{l}")
        return _matmul(act, w["f_w_down"][l], "nn", f"mm_f{l}_down", add=xin), (xin, xn, h, act)

    def ffn_bwd(g, saved, l):
        xin, xn, h, act = saved
        dact = _matmul(g, w["f_w_down"][l], "nt", f"mm_f{l}_dact")
        d_down = _matmul(act, g, "tn", f"mm_f{l}_ddown", out_dtype=BF16)
        dhg, dhv, dwg, dwv, dbg, dbv = _ffn_mid_bwd(h, w["f_conv_w"][l], w["f_conv_b"][l:l + 1], dact,
                                                    f"ffn_mid_bwd{l}")
        dh = jnp.concatenate([dhg, dhv], axis=1)
        dxn = _matmul(dh, w["f_w_up_t"][l], "nn", f"mm_f{l}_dxn")
        d_up = _matmul(dh, xn, "tn", f"mm_f{l}_dup", out_dtype=BF16)
        dx, dgn = _rms_bwd(xin, w["f_norm_g"][l:l + 1], dxn, g, f"rms_f{l}_bwd")
        return dx, d_up, d_down, jnp.concatenate([dwg, dwv], axis=1), jnp.concatenate([dbg, dbv], axis=1), dgn

    xn0 = _rms_fwd(x, w["e_norm_g"], "rms_e")
    p = _matmul(xn0, w["e_w_in_t"], "nt", "mm_e_in")
    pam = _tshift_fwd(p, w["e_mu"])
    pw = dict(w0=w["e_w0"], w2=w["e_w2"][0], a0=w["e_a0"], a2=w["e_a2"][0], g2=w["e_g2"][0],
              k_k=w["e_k_k"], k_a=w["e_k_a"])
    r, dec, k2, v, z, b, gate = _rwkv_prep_fwd(pam, pw, gm)
    v_exp = _expand_cols(v, "wkv_expand_v")
    s_all, s_last = _wkv_fwd(dec, k2, z, b, v_exp)
    y_pt = _wkv_out(r, s_all, s_last)
    y = _from_pt(y_pt)
    rk = w["e_r_k"].reshape(1, RW)
    ya = _rwkv_post_fwd(y, r, k2, v, gate, w["e_ln_w"], w["e_ln_b"], rk, gm)
    ga, gx = _gate_dense(w["e_gate_a_w"][0]), _gate_dense(w["e_gate_x_w"][0])
    lru_w = (w["e_conv_w"][0], w["e_conv_b"], ga, w["e_gate_a_b"], gx, w["e_gate_x_b"], w["e_lru_lambda"])
    yb = _lru_fwd(p, *lru_w)
    ycat = jnp.concatenate([ya, yb], axis=1)
    w = {**w, **late_weights(ycat)}
    x1 = _matmul(ycat, w["e_w_out"], "nn", "mm_e_out", add=x)
    x2, ffn0 = ffn_fwd(x1, 0)

    xn1 = _rms_fwd(x2, w["o_norm_g"], "rms_o")
    u = _matmul(xn1, w["o_w_in"], "nn", "mm_o_in")
    expand = jnp.kron(jnp.eye(S5_STATE, dtype=F32), jnp.ones((1, S5_GROUP), F32))
    disc_in = (w["o_A_re"][0], w["o_A_im"][0], w["o_log_dt"].reshape(S5_GROUPS, 1),
               w["o_B_re"][0].reshape(S5_GROUPS, -1), w["o_B_im"][0].reshape(S5_GROUPS, -1), expand)
    ab_re, ab_im, bb_re, bb_im = _s5_disc_fwd(*disc_in)
    s5_w = (w["o_D"], _s5_in_dense(bb_re), _s5_in_dense(bb_im), _s5_out_dense(w["o_C_re"][0]),
            _s5_out_dense(w["o_C_im"][0]), ab_re.reshape(_NS, 1, _S5_W), ab_im.reshape(_NS, 1, _S5_W))
    yact = _s5_fwd(u, *s5_w)
    zz = _matmul(yact, w["o_w_glu_t"], "nt", "mm_o_glu")
    x3 = _glu_fwd(x2, zz)
    x4, ffn1 = ffn_fwd(x3, 1)

    loss, g, gs["final_norm_g", 0] = _loss_head(x4, w["final_norm_g"].reshape(1, d_model), tgt)

    g, up1, down1, dcw1, dcb1, dfn1 = ffn_bwd(g, ffn1, 1)
    dz = _glu_bwd(zz, g)
    dyact = _matmul(dz, w["o_w_glu_t"], "nn", "mm_o_dyact")
    d_glu = _matmul(dz, yact, "tn", "mm_o_dglu", out_dtype=BF16)
    du, gs["o_D", 0], dbbr, dbbi, dcdr, dcdi, dabr, dabi = _s5_bwd(u, *s5_w, dyact)
    gs["o_C_re", 0] = _s5_out_blocks(dcdr).reshape(S5_GROUPS * S5_GROUP, S5_STATE)
    gs["o_C_im", 0] = _s5_out_blocks(dcdi).reshape(S5_GROUPS * S5_GROUP, S5_STATE)
    cts = (dabr.reshape(S5_GROUPS, S5_STATE), dabi.reshape(S5_GROUPS, S5_STATE), _s5_in_blocks(dbbr),
           _s5_in_blocks(dbbi))
    gs["o_A_re", 0], gs["o_A_im", 0], dlog_dt, gs["o_B_re", 0], gs["o_B_im", 0] = _s5_disc_bwd(*disc_in, cts)
    gs["o_log_dt", 0] = dlog_dt.reshape(1, S5_GROUPS)
    dxn = _matmul(du, w["o_w_in"], "nt", "mm_o_dxn")
    d_oin = _matmul(xn1, du, "tn", "mm_o_din", out_dtype=BF16)
    g, gs["o_norm_g", 0] = _rms_bwd(x2, w["o_norm_g"], dxn, g, "rms_o_bwd")
    g = send_grads("a", [("f_w_up", 1, up1), ("f_w_down", 1, down1), ("o_w_glu", 0, d_glu), ("o_w_in", 0, d_oin)], g)

    g, up0, down0, dcw0, dcb0, dfn0 = ffn_bwd(g, ffn0, 0)
    gs["f_conv_w", 0], gs["f_conv_w", 3] = dcw0, dcw1
    gs["f_conv_b", 0], gs["f_conv_b", 1] = dcb0, dcb1
    gs["f_norm_g", 0], gs["f_norm_g", 1] = dfn0, dfn1

    dycat = _matmul(g, w["e_w_out"], "nt", "mm_e_dycat")
    d_eout = _matmul(ycat, g, "tn", "mm_e_dout", out_dtype=BF16)
    dycat = send_grads("b", [("f_w_up", 0, up0), ("f_w_down", 0, down0), ("e_w_out", 0, d_eout)], dycat)
    dy, dr1, dk1, dv1, dgate, gs["e_ln_w", 0], gs["e_ln_b", 0], gs["e_r_k", 0] = _rwkv_post_bwd(
        y, r, k2, v, gate, w["e_ln_w"], w["e_ln_b"], rk, gm, dycat)
    dr2, ddec, dk2, dzz, dbb, dv_pt = _wkv_bwd(r, dec, k2, z, b, v_exp, s_all, _expand_cols(dy, "wkv_expand_dy"))
    (dpam, gs["e_w0", 0], gs["e_w2", 0], gs["e_a0", 0], gs["e_a2", 0], gs["e_g2", 0], gs["e_k_k", 0],
     gs["e_k_a", 0]) = _rwkv_prep_bwd(pam, pw, gm, (dr2, ddec, dk2, _from_pt(dv_pt), dzz, dbb, dgate), (dr1, dk1, dv1))
    dpa, gs["e_mu", 0] = _tshift_bwd(p, w["e_mu"], dpam)
    (dbx, dbg, gs["e_conv_w", 0], gs["e_conv_b", 0], dga, gs["e_gate_a_b", 0], dgx, gs["e_gate_x_b", 0],
     gs["e_lru_lambda", 0]) = _lru_bwd(p, *lru_w, dycat)
    gs["e_gate_a_w", 0] = _gate_blocks(dga).reshape(LRU_W, HEAD)
    gs["e_gate_x_w", 0] = _gate_blocks(dgx).reshape(LRU_W, HEAD)
    dp = jnp.concatenate([dpa, dbx, dbg], axis=1)
    d_ein = _matmul(dp, xn0, "tn", "mm_e_din", out_dtype=BF16)
    dp = send_grads("c", [("e_w_in", 0, d_ein)], dp)
    dxn = _matmul(dp, w["e_w_in_t"], "nn", "mm_e_dxn")
    grad_x, gs["e_norm_g", 0] = _rms_bwd(x, w["e_norm_g"], dxn, g, "rms_e_bwd")
    return loss, grad_x, gs


CAST_ROWS = 256


def _cast_shard(w3, layer, transpose, chip, name):
    _, rows, cols = w3.shape
    tr = _tile(rows, (CAST_ROWS, 176, 128))

    def body(c_ref, w_ref, o_ref):
        v = w_ref[...]
        o_ref[...] = (v.T if transpose else v).astype(BF16)

    in_spec = pl.BlockSpec((None, tr, cols), lambda i, c: (layer, i, 0))
    if transpose:
        out_spec, shape = pl.BlockSpec((None, cols, tr), lambda i, c: (c[0], 0, i)), (cols, rows)
    else:
        out_spec, shape = pl.BlockSpec((None, tr, cols), lambda i, c: (c[0], i, 0)), (rows, cols)
    grid_spec = pltpu.PrefetchScalarGridSpec(num_scalar_prefetch=1, grid=(rows // tr,), in_specs=[in_spec],
                                             out_specs=out_spec)
    return _blocked(body, name=name, grid_spec=grid_spec,
                          out_shape=jax.ShapeDtypeStruct((N_CHIPS,) + shape, BF16),
                          compiler_params=_cparams(("parallel",), VMEM_MID))(chip, w3)


_ANY = pl.BlockSpec(memory_space=pl.ANY)


def _coords():
    return lax.axis_index("x"), lax.axis_index("y"), lax.axis_index("c")


def _flip(v, d):
    return 1 - v if d else v


_CHIP_RELS = ((1, 0), (0, 1), (1, 1))
_DEV_RELS = tuple((dx, dy, dc) for dx in (0, 1) for dy in (0, 1) for dc in (0, 1))[1:]


_HBM = pl.BlockSpec(memory_space=pltpu.HBM)
_SEM = pl.BlockSpec(memory_space=pltpu.SEMAPHORE)
_EFFECT = pltpu.SideEffectType.DATAFLOW_SIDE_EFFECTING


def _in_hbm(a):
    return pltpu.with_memory_space_constraint(a, pltpu.HBM)


def _gather_copies(bufs, send, recv, landed):
    x, y, c = _coords()
    me = 2 * x + y
    res = []
    for i, buf in enumerate(bufs):
        for j, (dx, dy) in enumerate(_CHIP_RELS):
            px, py = _flip(x, dx), _flip(y, dy)
            k = i * len(_CHIP_RELS) + j
            res.append(pltpu.make_async_remote_copy(
                src_ref=buf.at[me], dst_ref=buf.at[2 * px + py if landed else me], send_sem=send.at[k],
                recv_sem=recv.at[k], device_id=(px, py, c), device_id_type=MESH))
    return res


def _scatter_copies(srcs, lands, send, recv, landed):
    x, y, c = _coords()
    me = 4 * x + 2 * y + c
    res = []
    for i, (src, land) in enumerate(zip(srcs, lands)):
        for j, (dx, dy, dc) in enumerate(_DEV_RELS):
            peer = (_flip(x, dx), _flip(y, dy), _flip(c, dc))
            pid = 4 * peer[0] + 2 * peer[1] + peer[2]
            k = i * len(_DEV_RELS) + j
            res.append(pltpu.make_async_remote_copy(
                src_ref=src.at[pid], dst_ref=land.at[pid if landed else me], send_sem=send.at[k],
                recv_sem=recv.at[k], device_id=peer, device_id_type=MESH))
    return res


def _split_start(bufs, n_src, copies, n_rel, name, after):
    n = len(bufs)
    nk = n_src * n_rel

    def body(*refs):
        ins, send, recv, token = refs[:n], refs[n + 1 + n], refs[n + 2 + n], refs[-1]
        for cp in copies(ins, send, recv, False):
            cp.start()
        token[...] = jnp.zeros_like(token)

    res = pl.pallas_call(
        body, name=name, in_specs=[_HBM] * n + [_ANY],
        out_specs=[_HBM] * n + [_SEM, _SEM, pl.BlockSpec(memory_space=pltpu.VMEM)],
        out_shape=[pltpu.HBM(b.shape, b.dtype) for b in bufs]
        + [pltpu.SemaphoreType.DMA((nk,)), pltpu.SemaphoreType.DMA((nk,)), jax.ShapeDtypeStruct((8, LANES), F32)],
        input_output_aliases={i: i for i in range(n)},
        compiler_params=pltpu.CompilerParams(has_side_effects=_EFFECT))(*[_in_hbm(b) for b in bufs], after)
    return res[n], res[n + 1], list(res[:n]), res[n + 2]


def _split_wait(bufs, send, recv, copies, name, after):
    n = len(bufs)

    def body(*refs):
        ins, send_ref, recv_ref = refs[:n], refs[n], refs[n + 1]
        for cp in copies(ins, send_ref, recv_ref, True):
            cp.wait_send()
            cp.wait_recv()

    return pl.pallas_call(
        body, name=name, in_specs=[_HBM] * n + [_SEM, _SEM, _ANY], out_specs=[_HBM] * n,
        out_shape=[pltpu.HBM(b.shape, b.dtype) for b in bufs], input_output_aliases={i: i for i in range(n)},
        compiler_params=pltpu.CompilerParams(has_side_effects=_EFFECT))(*bufs, send, recv, after)


def _gather_start(bufs, name, after):
    return _split_start(bufs, len(bufs), _gather_copies, len(_CHIP_RELS), name, after)


def _gather_wait(bufs, send, recv, name, after):
    return _split_wait(bufs, send, recv, _gather_copies, name, after)


def _scatter_start(srcs, name, after):
    n = len(srcs)
    lands = [lax.empty(a.shape, a.dtype) for a in srcs]
    fn = lambda refs, send, recv, landed: _scatter_copies(refs[:n], refs[n:], send, recv, landed)
    send, recv, bufs, token = _split_start(list(srcs) + lands, n, fn, len(_DEV_RELS), name, after)
    return send, recv, bufs, token


def _scatter_wait(bufs, send, recv, name, after):
    n = len(bufs) // 2
    fn = lambda refs, s, r, landed: _scatter_copies(refs[:n], refs[n:], s, r, landed)
    res = _split_wait(bufs, send, recv, fn, name, after)
    return res[:n], res[n:]


def _sum_segments(src, land, me, name):
    nd, seg, cols = src.shape
    ts = _tile(seg, (256, 176, 128))

    def body(m_ref, *refs):
        o_ref = refs[-1]
        acc = refs[0][...].astype(F32)
        for r in refs[1:-1]:
            acc = acc + r[...].astype(F32)
        o_ref[...] = acc

    def peer(rel):
        bits = 4 * rel[0] + 2 * rel[1] + rel[2]
        return pl.BlockSpec((None, ts, cols), lambda i, m: (jnp.bitwise_xor(m[0], bits), i, 0))

    grid_spec = pltpu.PrefetchScalarGridSpec(
        num_scalar_prefetch=1, grid=(seg // ts,),
        in_specs=[pl.BlockSpec((None, ts, cols), lambda i, m: (m[0], i, 0))] + [peer(r) for r in _DEV_RELS],
        out_specs=pl.BlockSpec((None, ts, cols), lambda i, m: (m[1], i, 0)))
    return _blocked(body, name=name, grid_spec=grid_spec,
                          out_shape=jax.ShapeDtypeStruct((2, seg, cols), F32),
                          compiler_params=_cparams(("parallel",), VMEM_MID))(me, src, *[land] * len(_DEV_RELS))


def _exchange_sibling(arrs):
    n = len(arrs)

    def body(*refs):
        outs, (send, recv) = refs[n:2 * n], refs[2 * n:]
        x, y, c = _coords()
        sib = (x, y, 1 - c)
        sends, recvs = [], []
        for i in range(n):
            cp = pltpu.make_async_remote_copy(src_ref=outs[i].at[c], dst_ref=outs[i].at[c], send_sem=send.at[i],
                                              recv_sem=recv.at[i], device_id=sib, device_id_type=MESH)
            cp.start()
            sends.append(cp)
            recvs.append(pltpu.make_async_remote_copy(src_ref=outs[i].at[c], dst_ref=outs[i].at[1 - c],
                                                      send_sem=send.at[i], recv_sem=recv.at[i], device_id=sib,
                                                      device_id_type=MESH))
        for cp in recvs:
            cp.wait_recv()
        for cp in sends:
            cp.wait_send()

    return pl.pallas_call(
        body, name="exchange_sibling", in_specs=[_ANY] * n, out_specs=[_ANY] * n,
        out_shape=[jax.ShapeDtypeStruct(a.shape, a.dtype) for a in arrs],
        input_output_aliases={i: i for i in range(n)},
        scratch_shapes=[pltpu.SemaphoreType.DMA((n,)), pltpu.SemaphoreType.DMA((n,))])(*arrs)


def _allreduce_small(vec):
    nd, rows, lanes = vec.shape
    nr = len(_DEV_RELS)

    def body(in_ref, out_ref, stage, red, send, recv):
        x, y, c = _coords()
        me = 4 * x + 2 * y + c
        peers = []
        for dx, dy, dc in _DEV_RELS:
            peer = (_flip(x, dx), _flip(y, dy), _flip(c, dc))
            peers.append((peer, 4 * peer[0] + 2 * peer[1] + peer[2]))

        def copy(src, dst, k, peer):
            return pltpu.make_async_remote_copy(src_ref=src, dst_ref=dst, send_sem=send.at[k], recv_sem=recv.at[k],
                                                device_id=peer, device_id_type=MESH)

        first = [copy(in_ref.at[pid], stage.at[me], j, peer) for j, (peer, pid) in enumerate(peers)]
        for cp in first:
            cp.start()
        stage[me] = in_ref[me]
        for j, (peer, pid) in enumerate(peers):
            copy(in_ref.at[pid], stage.at[pid], j, peer).wait_recv()
        acc = stage[0]
        for d in range(1, nd):
            acc = acc + stage[d]
        red[...] = acc
        out_ref[me] = acc
        second = [copy(red, out_ref.at[me], nr + j, peer) for j, (peer, pid) in enumerate(peers)]
        for cp in second:
            cp.start()
        for j, (peer, pid) in enumerate(peers):
            copy(red, out_ref.at[pid], nr + j, peer).wait_recv()
        for cp in first + second:
            cp.wait_send()

    vm = pl.BlockSpec(memory_space=pltpu.VMEM)
    return pl.pallas_call(
        body, name="allreduce_small", in_specs=[vm], out_specs=vm,
        out_shape=jax.ShapeDtypeStruct(vec.shape, F32),
        scratch_shapes=[pltpu.VMEM(vec.shape, F32), pltpu.VMEM((rows, lanes), F32),
                        pltpu.SemaphoreType.DMA((2 * nr,)), pltpu.SemaphoreType.DMA((2 * nr,))],
        compiler_params=_cparams(None, VMEM_MID))(vec)


def _adam_math(w, g, m, v):
    m2 = ADAM_B1 * m + (1.0 - ADAM_B1) * g
    v2 = ADAM_B2 * v + (1.0 - ADAM_B2) * (g * g)
    m_hat = m2 / (1.0 - ADAM_B1 ** ADAM_STEP)
    v_hat = v2 / (1.0 - ADAM_B2 ** ADAM_STEP)
    return -ADAM_LR * (m_hat / (jnp.sqrt(v_hat) + ADAM_EPS) + ADAM_WD * w), m2, v2


def _adamw_big(w3, m3, v3, layer, g, transposed, name, prev=None):
    nl, rows, cols = w3.shape
    tr = 128 if transposed else _tile(rows, (256, 176, 128))

    def body(w_ref, m_ref, v_ref, g_ref, *rest):
        go_ref, d_ref, mo_ref, vo_ref = rest[-4:]
        g_val = g_ref[...].T if transposed else g_ref[...]
        go_ref[...] = g_val
        d_ref[...], mo_ref[...], vo_ref[...] = _adam_math(w_ref[...], g_val, m_ref[...], v_ref[...])

    wspec = pl.BlockSpec((None, tr, cols), lambda i: (layer, i, 0))
    gspec = pl.BlockSpec((cols, tr), lambda i: (0, i)) if transposed else pl.BlockSpec((tr, cols), lambda i: (i, 0))
    extra = [] if prev is None else list(prev)
    return _blocked(body, name=name, grid=(rows // tr,),
                          in_specs=[wspec, wspec, wspec, gspec] + [_ANY] * len(extra),
                          out_specs=[wspec] * 4, out_shape=[jax.ShapeDtypeStruct((nl, rows, cols), F32)] * 4,
                          input_output_aliases={4 + i: i for i in range(len(extra))},
                          compiler_params=_cparams(("parallel",), VMEM_MID))(w3, m3, v3, g, *extra)


_SMALL = (
    ("e_norm_g", (1, 1024), None), ("e_mu", (1, SHIFT_COLS), None), ("e_w0", (1, RW), None),
    ("e_w2", (W_LORA, RW), 128), ("e_a0", (1, RW), None), ("e_a2", (A_LORA, RW), 128), ("e_g2", (G_LORA, RW), 128),
    ("e_k_k", (1, RW), None), ("e_k_a", (1, RW), None), ("e_r_k", (1, RW), None), ("e_ln_w", (1, RW), None),
    ("e_ln_b", (1, RW), None), ("e_conv_w", (4, LRU_W), 128), ("e_conv_b", (1, LRU_W), None),
    ("e_gate_a_w", (LRU_W, HEAD), None), ("e_gate_a_b", (1, LRU_W), None), ("e_gate_x_w", (LRU_W, HEAD), None),
    ("e_gate_x_b", (1, LRU_W), None), ("e_lru_lambda", (1, LRU_W), None), ("o_norm_g", (1, 1024), 256),
    ("o_A_re", (S5_GROUPS, S5_STATE), None), ("o_A_im", (S5_GROUPS, S5_STATE), None), ("o_log_dt", (1, S5_GROUPS), None),
    ("o_B_re", (S5_GROUPS, S5_STATE * S5_GROUP), None), ("o_B_im", (S5_GROUPS, S5_STATE * S5_GROUP), None),
    ("o_C_re", (S5_GROUPS * S5_GROUP, S5_STATE), None), ("o_C_im", (S5_GROUPS * S5_GROUP, S5_STATE), None),
    ("o_D", (1, 1024), 256), ("f_norm_g", (2, 1024), None), ("f_conv_w", (6, 2 * D_FF), 2 * D_FF // 4),
    ("f_conv_b", (2, 2 * D_FF), None), ("final_norm_g", (1, 1024), None))
_PIECES = {"f_norm_g": ((0, 1), (1, 1)), "f_conv_b": ((0, 1), (1, 1)), "f_conv_w": ((0, 3), (3, 3))}


def _ceil_to(n, m):
    return -(-n // m) * m


def _small_layout():
    groups = {}
    for name, (rows, cols), _ in _SMALL:
        for first, r in _PIECES.get(name, ((0, rows),)):
            groups.setdefault(cols, []).append((name, first, r))
    layout, off = {}, 0
    for cols, items in groups.items():
        stacks = [0, 0] if 2 * cols <= LANES else [0]
        placed = []
        for name, first, r in sorted(items, key=lambda it: -it[2]):
            half = stacks.index(min(stacks))
            r0 = stacks[half]
            if r >= 8 or r0 % 8 + r > 8:
                r0 = _ceil_to(r0, 8)
            placed.append((name, first, r, r0, half * (LANES // 2)))
            stacks[half] = r0 + r
        rpad = _ceil_to(max(stacks), 8)
        for name, first, r, at, lane in placed:
            layout[name, first] = (off, rpad, at, r, cols, lane)
        off += -(-cols // LANES) * rpad
    return layout, _ceil_to(off, 8 * N_DEV)


def _small_pack(gs):
    layout, total = _small_layout()
    keys = list(layout)

    def body(*refs):
        out = refs[-1]
        out[...] = jnp.zeros_like(out)
        for key, g_ref in zip(keys, refs[:-1]):
            off, rpad, at, r, cols, lane = layout[key]
            for j in range(-(-cols // LANES)):
                cw = min(LANES, cols - j * LANES)
                out[off + j * rpad + at:off + j * rpad + at + r, lane:lane + cw] = g_ref[:, j * LANES:j * LANES + cw]

    return pl.pallas_call(body, name="small_pack", out_shape=jax.ShapeDtypeStruct((total, LANES), F32),
                          compiler_params=_cparams(None, VMEM_MID))(*[gs[k] for k in keys])


def _adamw_small(red, chip, wts, ms, vs):
    layout, _ = _small_layout()
    names = [n for n, _, _ in _SMALL]
    n = len(names)

    def body(chip_ref, red_ref, *refs):
        ins, outs = refs[:3 * n], refs[3 * n:]
        c = chip_ref[0]
        for i, (name, (rows, cols), loc) in enumerate(_SMALL):
            w_ref, m_ref, v_ref = ins[3 * i:3 * i + 3]
            o_refs = outs[4 * i:4 * i + 4]
            width = cols if loc is None else loc
            for first, r in _PIECES.get(name, ((0, rows),)):
                off, rpad, at, _, _, lane = layout[name, first]
                for j in range(-(-width // LANES)):
                    cw = min(LANES, width - j * LANES)
                    ls = slice(lane, lane + cw)
                    if loc is None:
                        start = off + j * rpad + at
                        g = red_ref[start:start + r, ls]
                    else:
                        blk = c * (loc // LANES) + j
                        if r >= 8:
                            g = red_ref[pl.ds(pl.multiple_of(off + at + blk * rpad, 8), r), ls]
                        else:
                            tile = red_ref[pl.ds(pl.multiple_of(off + at // 8 * 8 + blk * rpad, 8), 8), ls]
                            g = tile[at % 8:at % 8 + r]
                    rs, cs = slice(first, first + r), slice(j * LANES, j * LANES + cw)
                    d, m2, v2 = _adam_math(w_ref[rs, cs], g, m_ref[rs, cs], v_ref[rs, cs])
                    for o, val in zip(o_refs, (g, d, m2, v2)):
                        o[rs, cs] = val

    args, shapes = [], []
    for name in names:
        args += [wts[name], ms[name], vs[name]]
        shapes += [jax.ShapeDtypeStruct(wts[name].shape, F32)] * 4
    vm = pl.BlockSpec(memory_space=pltpu.VMEM)
    res = pl.pallas_call(body, name="adamw_small",
                         in_specs=[pl.BlockSpec(memory_space=pltpu.SMEM), vm] + [vm] * (3 * n),
                         out_specs=[vm] * (4 * n), out_shape=shapes,
                         compiler_params=_cparams(None, VMEM_BIG))(chip, red, *args)
    return {name: res[4 * i:4 * i + 4] for i, name in enumerate(names)}


PACK_ROWS = 8


def _packed_rows(shape):
    size = 1
    for d in shape:
        size *= d
    return -(-size // (PACK_ROWS * LANES)) * PACK_ROWS


def _pack(arrs, row_mult):
    parts = []
    for a in arrs:
        flat = a.reshape(-1).astype(F32)
        rows = _packed_rows(a.shape)
        parts.append(jnp.pad(flat, (0, rows * LANES - flat.shape[0])).reshape(rows, LANES))
    total = sum(p.shape[0] for p in parts)
    fill = -(-total // row_mult) * row_mult - total
    if fill:
        parts.append(jnp.zeros((fill, LANES), F32))
    return jnp.concatenate(parts, axis=0)


def _unpack(packed, shapes):
    out, off = [], 0
    for s in shapes:
        rows = _packed_rows(s)
        size = 1
        for d in s:
            size *= d
        out.append(packed[off:off + rows].reshape(-1)[:size].reshape(s))
        off += rows
    return out


_SMALL_SH = ("e_w2", "e_a2", "e_g2", "e_conv_w", "o_norm_g", "o_D", "f_conv_w")
_LARGE = (("e_w_in", True), ("e_w_out", False), ("o_w_in", False), ("o_w_glu", True), ("f_w_up", True),
        ("f_w_down", False))
_ORDER = ("e_norm_g", "e_w_in", "e_mu", "e_w0", "e_w2", "e_a0", "e_a2", "e_g2", "e_k_k", "e_k_a", "e_r_k", "e_ln_w",
          "e_ln_b", "e_conv_w", "e_conv_b", "e_gate_a_w", "e_gate_a_b", "e_gate_x_w", "e_gate_x_b", "e_lru_lambda",
          "e_w_out", "o_norm_g", "o_w_in", "o_A_re", "o_A_im", "o_log_dt", "o_B_re", "o_B_im", "o_C_re", "o_C_im",
          "o_D", "o_w_glu", "f_norm_g", "f_w_up", "f_conv_w", "f_conv_b", "f_w_down", "final_norm_g")
N_CHIPS = 4
N_DEV = 8


def _step(x, tgt, wts, ms, vs):
    xi, yi, ci = _coords()
    chip = 2 * xi + yi
    chip1 = chip.astype(jnp.int32).reshape(1)
    me2 = jnp.stack([4 * xi + 2 * yi + ci, ci]).astype(jnp.int32)
    by_cols = dict(_LARGE)

    bufs = {(name, l): _cast_shard(wts[name], l, by_cols[name], chip1, f"cast_{name}{l}")
            for name, _ in _LARGE for l in range(wts[name].shape[0])}
    sh_shapes = [wts[n].shape for n in _SMALL_SH]
    packed = _pack([wts[n] for n in _SMALL_SH], 8)
    small_buf = lax.dynamic_update_slice(jnp.zeros((N_CHIPS,) + packed.shape, F32), packed[None], (chip, 0, 0))
    early = [("e_w_in", 0)]
    late = [k for k in bufs if k not in early]
    send, recv, thru, token = _gather_start([bufs[k] for k in early] + [small_buf], "gather_start_a", x)
    got = _gather_wait(thru, send, recv, "gather_wait_a", token)
    send_b, recv_b, thru_b, token = _gather_start([bufs[k] for k in late], "gather_start_b", got[0])
    x, _ = lax.optimization_barrier((x, token))

    def rows(g):
        return g.reshape(N_CHIPS * g.shape[1], g.shape[2])

    full = {n: wts[n] for n, _, loc in _SMALL if loc is None}
    full["e_w_in_t"] = rows(got[0])
    per_chip = [_unpack(got[1][k], sh_shapes) for k in range(N_CHIPS)]
    for i, n in enumerate(_SMALL_SH):
        full[n] = jnp.concatenate([per_chip[k][i] for k in range(N_CHIPS)], axis=-1)

    def late_weights(after):
        res = dict(zip(late, _gather_wait(thru_b, send_b, recv_b, "gather_wait_b", after)))
        return {"e_w_out": rows(res[("e_w_out", 0)]), "o_w_in": rows(res[("o_w_in", 0)]),
                "o_w_glu_t": rows(res[("o_w_glu", 0)]),
                "f_w_up_t": [rows(res[("f_w_up", l)]) for l in range(2)],
                "f_w_down": [rows(res[("f_w_down", l)]) for l in range(2)]}

    pending = []

    def send_grads(tag, items, carry):
        srcs = [g.reshape(N_DEV, g.shape[0] // N_DEV, g.shape[1]) for _, _, g in items]
        s_sem, r_sem, both, tok = _scatter_start(srcs, f"scatter_start_{tag}", carry)
        pending.append((tag, [(name, l) for name, l, _ in items], s_sem, r_sem, both))
        carry, _ = lax.optimization_barrier((carry, tok))
        return carry

    loss, grad_x, gs = _local_step(x, tgt, full, late_weights, send_grads)

    final = {}
    red = _allreduce_small(_small_pack(gs).reshape(N_DEV, -1, LANES)).reshape(-1, LANES)
    view = {name: (rows, cols if loc is None else loc) for name, (rows, cols), loc in _SMALL}
    as2d = lambda d: {name: d[name].reshape(view[name]) for name in view}
    small = _adamw_small(red, chip1, as2d(wts), as2d(ms), as2d(vs))
    for name, res in small.items():
        final[name] = [r.reshape(wts[name].shape) for r in res]
    new_v = small["final_norm_g"][3]

    halves, keys = [], []
    for tag, names, s_sem, r_sem, both in pending:
        srcs, lands = _scatter_wait(both, s_sem, r_sem, f"scatter_wait_{tag}", new_v)
        for (name, l), src, land in zip(names, srcs, lands):
            halves.append(_sum_segments(src, land, me2, f"sum_{name}{l}"))
            keys.append((name, l))
    shards = _exchange_sibling(halves)
    for s, (name, l) in zip(shards, keys):
        final[name] = _adamw_big(wts[name], ms[name], vs[name], l, s.reshape(2 * s.shape[1], s.shape[2]),
                                 by_cols[name], f"adamw_{name}{l}", prev=final.get(name))

    loss = lax.psum(loss[0, 0], ("x", "y", "c"))
    res = [loss, grad_x[None]]
    for k in range(4):
        res += [final[n][k] for n in _ORDER]
    return tuple(res)


def kernel(x, e_norm_g, e_w_in, e_mu, e_w0, e_w2, e_a0, e_a2, e_g2, e_k_k, e_k_a, e_r_k, e_ln_w, e_ln_b, e_conv_w, e_conv_b, e_gate_a_w, e_gate_a_b, e_gate_x_w, e_gate_x_b, e_lru_lambda, e_w_out, o_norm_g, o_w_in, o_A_re, o_A_im, o_log_dt, o_B_re, o_B_im, o_C_re, o_C_im, o_D, o_w_glu, f_norm_g, f_w_up, f_conv_w, f_conv_b, f_w_down, final_norm_g, loss_target, m_e_norm_g, m_e_w_in, m_e_mu, m_e_w0, m_e_w2, m_e_a0, m_e_a2, m_e_g2, m_e_k_k, m_e_k_a, m_e_r_k, m_e_ln_w, m_e_ln_b, m_e_conv_w, m_e_conv_b, m_e_gate_a_w, m_e_gate_a_b, m_e_gate_x_w, m_e_gate_x_b, m_e_lru_lambda, m_e_w_out, m_o_norm_g, m_o_w_in, m_o_A_re, m_o_A_im, m_o_log_dt, m_o_B_re, m_o_B_im, m_o_C_re, m_o_C_im, m_o_D, m_o_w_glu, m_f_norm_g, m_f_w_up, m_f_conv_w, m_f_conv_b, m_f_w_down, m_final_norm_g, v_e_norm_g, v_e_w_in, v_e_mu, v_e_w0, v_e_w2, v_e_a0, v_e_a2, v_e_g2, v_e_k_k, v_e_k_a, v_e_r_k, v_e_ln_w, v_e_ln_b, v_e_conv_w, v_e_conv_b, v_e_gate_a_w, v_e_gate_a_b, v_e_gate_x_w, v_e_gate_x_b, v_e_lru_lambda, v_e_w_out, v_o_norm_g, v_o_w_in, v_o_A_re, v_o_A_im, v_o_log_dt, v_o_B_re, v_o_B_im, v_o_C_re, v_o_C_im, v_o_D, v_o_w_glu, v_f_norm_g, v_f_w_up, v_f_conv_w, v_f_conv_b, v_f_w_down, v_final_norm_g):
    args = locals()
    wts = {n: args[n] for n in _ORDER}
    ms = {n: args["m_" + n] for n in _ORDER}
    vs = {n: args["v_" + n] for n in _ORDER}
    return _step(x[0], loss_target[0], wts, ms, vs)
```

```python
import functools

import jax
import jax.numpy as jnp
from jax import lax
from jax.experimental import pallas as pl
from jax.experimental.pallas import tpu as pltpu

F32 = jnp.float32
BF16 = jnp.bfloat16
MESH = pl.DeviceIdType.MESH

HEAD = 64
RW = 512
N_HEADS = RW // HEAD
LRU_W = 512
SHIFT_COLS = 1792
W_LORA, A_LORA, G_LORA = 64, 64, 128
S5_GROUPS, S5_GROUP, S5_STATE = 64, 16, 64
D_FF = 2816
NORM_EPS = 1e-6
GN_EPS = 64e-5
LRU_C = 8.0
ADAM_LR, ADAM_B1, ADAM_B2, ADAM_EPS, ADAM_WD, ADAM_STEP = 0.001, 0.9, 0.999, 1e-08, 0.01, 10

VMEM_BIG = 56 * 1024 * 1024
VMEM_MID = 40 * 1024 * 1024
LANES = 128
PT = 16
WKV_CHUNK = 32
S5_SLAB = 128


def _blocked(*args, **kw):
    call = pl.pallas_call(*args, **kw)

    def run(*ops):
        return call(*[pltpu.with_memory_space_constraint(a, pltpu.HBM) if a.ndim >= 2 else a for a in ops])

    return run


def _cparams(sem=None, vmem=None):
    kw = {}
    if sem is not None:
        kw["dimension_semantics"] = sem
    if vmem is not None:
        kw["vmem_limit_bytes"] = vmem
    return pltpu.CompilerParams(**kw)


def _tile(dim, cands):
    for c in cands:
        if dim % c == 0:
            return c
    return dim


def _full(shape):
    n = len(shape)
    return pl.BlockSpec(shape, lambda *_: (0,) * n)


_TILES = (2816, 2048, 1408, 1024, 512, 256, 128)
MM_BUDGET = 36 * 1024 * 1024
VMEM_SLACK = 12 * 1024 * 1024


MXU_FLOPS = 9.0e14
HBM_BYTES = 3.3e12
STEP_SECONDS = 0.35e-6


def _mm_tiles(m, n, k, size_a, size_b, size_o, has_add):
    best = None
    for tm in _TILES:
        for tk in _TILES:
            for tn in _TILES:
                if m % tm or n % tn or k % tk:
                    continue
                need = 2 * (tm * tk * size_a + tk * tn * size_b + tm * tn * size_o) + tm * tn * 4 * (1 + 2 * has_add)
                if k > tk:
                    need += tm * tn * 4
                if need > MM_BUDGET:
                    continue
                steps = (m // tm) * (n // tn) * (k // tk)
                a_reads = n // tn if k > tk else 1
                moved = (m * k * size_a * a_reads + k * n * size_b * (m // tm) + m * n * (size_o + 4 * has_add))
                cost = max(2.0 * m * n * k / MXU_FLOPS, moved / HBM_BYTES) + steps * STEP_SECONDS
                cand = (-cost, tk, tm, tn)
                if best is None or cand > best[0]:
                    best = (cand, need)
    (_, tk, tm, tn), need = best
    return tm, tn, tk, need


def _matmul(a, b, mode, name, out_dtype=F32, add=None):
    if mode == "nn":
        (m, k), (k2, n) = a.shape, b.shape
    elif mode == "nt":
        (m, k), (n, k2) = a.shape, b.shape
    else:
        (k, m), (k2, n) = a.shape, b.shape
    assert k == k2, (a.shape, b.shape, mode)
    tm, tn, tk, need = _mm_tiles(m, n, k, a.dtype.itemsize, b.dtype.itemsize, jnp.dtype(out_dtype).itemsize,
                                 add is not None)
    nk = k // tk
    dims = {"nn": (((1,), (0,)), ((), ())), "nt": (((1,), (1,)), ((), ())), "tn": (((0,), (0,)), ((), ()))}[mode]

    def body(*refs):
        a_ref, b_ref = refs[:2]
        add_ref = refs[2] if add is not None else None
        o_ref = refs[3] if add is not None else refs[2]
        part = lax.dot_general(a_ref[...].astype(BF16), b_ref[...].astype(BF16), dims, preferred_element_type=F32)

        def finish(r):
            if add_ref is not None:
                r = r + add_ref[...]
            o_ref[...] = r.astype(o_ref.dtype)

        if nk == 1:
            finish(part)
            return
        acc = refs[-1]
        kk = pl.program_id(2)

        @pl.when(kk == 0)
        def _():
            acc[...] = part

        @pl.when(kk > 0)
        def _():
            acc[...] += part

        @pl.when(kk == nk - 1)
        def _():
            finish(acc[...])

    if mode == "nn":
        a_spec = pl.BlockSpec((tm, tk), lambda i, j, kk: (i, kk))
        b_spec = pl.BlockSpec((tk, tn), lambda i, j, kk: (kk, j))
    elif mode == "nt":
        a_spec = pl.BlockSpec((tm, tk), lambda i, j, kk: (i, kk))
        b_spec = pl.BlockSpec((tn, tk), lambda i, j, kk: (j, kk))
    else:
        a_spec = pl.BlockSpec((tk, tm), lambda i, j, kk: (kk, i))
        b_spec = pl.BlockSpec((tk, tn), lambda i, j, kk: (kk, j))
    o_spec = pl.BlockSpec((tm, tn), lambda i, j, kk: (i, j))
    in_specs = [a_spec, b_spec] + ([o_spec] if add is not None else [])
    args = (a, b) + ((add,) if add is not None else ())
    return _blocked(
        body, name=name, grid=(m // tm, n // tn, nk),
        in_specs=in_specs, out_specs=o_spec,
        out_shape=jax.ShapeDtypeStruct((m, n), out_dtype),
        scratch_shapes=[pltpu.VMEM((tm, tn), F32)] if nk > 1 else [],
        compiler_params=_cparams(("parallel", "parallel", "arbitrary"), min(VMEM_BIG, need + VMEM_SLACK)),
    )(*args)


TOK = 256


def _rms(x, g):
    return x * lax.rsqrt(jnp.mean(x * x, axis=-1, keepdims=True) + NORM_EPS) * g


def _rms_fwd(x, g, name):
    t, d = x.shape

    def body(x_ref, g_ref, o_ref):
        o_ref[...] = _rms(x_ref[...], g_ref[...]).astype(BF16)

    row = pl.BlockSpec((TOK, d), lambda i: (i, 0))
    return _blocked(body, name=name, grid=(t // TOK,), in_specs=[row, _full((1, d))], out_specs=row,
                          out_shape=jax.ShapeDtypeStruct((t, d), BF16),
                          compiler_params=_cparams(("parallel",)))(x, g)


def _rms_bwd(x, g, dxn, res, name):
    t, d = x.shape

    def body(x_ref, g_ref, d_ref, res_ref, dx_ref, dg_ref):
        _, vjp = jax.vjp(_rms, x_ref[...], g_ref[...])
        dx, dg = vjp(d_ref[...].astype(F32))
        dx_ref[...] = dx + res_ref[...]

        @pl.when(pl.program_id(0) == 0)
        def _():
            dg_ref[...] = jnp.zeros_like(dg_ref)

        dg_ref[...] += dg

    row = pl.BlockSpec((TOK, d), lambda i: (i, 0))
    return _blocked(body, name=name, grid=(t // TOK,), in_specs=[row, _full((1, d)), row, row],
                          out_specs=[row, _full((1, d))],
                          out_shape=[jax.ShapeDtypeStruct((t, d), F32), jax.ShapeDtypeStruct((1, d), F32)],
                          compiler_params=_cparams(("arbitrary",)))(x, g, dxn, res)


def _loss_head(x, g, tgt):
    t, d = x.shape

    def body(x_ref, g_ref, t_ref, l_ref, dx_ref, dg_ref):
        tg = t_ref[...]

        def fn(xv, gv):
            err = _rms(xv, gv) - tg
            per_tok = jnp.mean(err * err, axis=-1, keepdims=True)
            return 0.5 * jnp.sum(per_tok, axis=0, keepdims=True)

        l, vjp = jax.vjp(fn, x_ref[...], g_ref[...])
        dx, dg = vjp(jnp.ones((1, 1), F32))
        dx_ref[...] = dx

        @pl.when(pl.program_id(0) == 0)
        def _():
            dg_ref[...] = jnp.zeros_like(dg_ref)
            l_ref[...] = jnp.zeros_like(l_ref)

        dg_ref[...] += dg
        l_ref[...] += jnp.broadcast_to(l, l_ref.shape)

    row = pl.BlockSpec((TOK, d), lambda i: (i, 0))
    return _blocked(body, name="loss_head", grid=(t // TOK,), in_specs=[row, _full((1, d)), row],
                          out_specs=[_full((1, LANES)), row, _full((1, d))],
                          out_shape=[jax.ShapeDtypeStruct((1, LANES), F32), jax.ShapeDtypeStruct((t, d), F32),
                                     jax.ShapeDtypeStruct((1, d), F32)],
                          compiler_params=_cparams(("arbitrary",)))(x, g, tgt)


def _glu_fwd(x, z):
    t, d = x.shape

    def body(x_ref, v_ref, g_ref, o_ref):
        o_ref[...] = x_ref[...] + v_ref[...] * jax.nn.sigmoid(g_ref[...])

    row = pl.BlockSpec((TOK, d), lambda i: (i, 0))
    gate = pl.BlockSpec((TOK, d), lambda i: (i, 1))
    return _blocked(body, name="glu_fwd", grid=(t // TOK,), in_specs=[row, row, gate], out_specs=row,
                          out_shape=jax.ShapeDtypeStruct((t, d), F32),
                          compiler_params=_cparams(("parallel",)))(x, z, z)


def _glu_bwd(z, g):
    t, d = g.shape

    def body(v_ref, g_ref, d_ref, o_ref):
        s = jax.nn.sigmoid(g_ref[...])
        dy = d_ref[...]
        o_ref[:, :d] = (dy * s).astype(BF16)
        o_ref[:, d:] = (dy * v_ref[...] * s * (1.0 - s)).astype(BF16)

    row = pl.BlockSpec((TOK, d), lambda i: (i, 0))
    gate = pl.BlockSpec((TOK, d), lambda i: (i, 1))
    return _blocked(body, name="glu_bwd", grid=(t // TOK,), in_specs=[row, gate, row],
                          out_specs=pl.BlockSpec((TOK, 2 * d), lambda i: (i, 0)),
                          out_shape=jax.ShapeDtypeStruct((t, 2 * d), BF16),
                          compiler_params=_cparams(("parallel",)))(z, z, g)


def _shift_down(x, d):
    row = lax.broadcasted_iota(jnp.int32, x.shape, 0)
    return jnp.where(row < d, 0.0, pltpu.roll(x, d, 0))


def _shift_up(x, d):
    n = x.shape[0]
    row = lax.broadcasted_iota(jnp.int32, x.shape, 0)
    return jnp.where(row >= n - d, 0.0, pltpu.roll(x, n - d, 0))


def _make_sd():
    @functools.partial(jax.custom_vjp, nondiff_argnums=(1,))
    def sd(x, d):
        return _shift_down(x, d)

    def fwd(x, d):
        return _shift_down(x, d), None

    def bwd(d, _, g):
        return (_shift_up(g, d),)

    sd.defvjp(fwd, bwd)
    return sd


def _lin_scan(a, u, reverse=False):
    n = a.shape[0]
    row = lax.broadcasted_iota(jnp.int32, a.shape, 0)
    d = 1
    while d < n:
        if reverse:
            keep = row < n - d
            a_s, u_s = pltpu.roll(a, n - d, 0), pltpu.roll(u, n - d, 0)
        else:
            keep = row >= d
            a_s, u_s = pltpu.roll(a, d, 0), pltpu.roll(u, d, 0)
        u = u + a * jnp.where(keep, u_s, 0.0)
        a = a * jnp.where(keep, a_s, 1.0)
        d *= 2
    return u


def _make_scan():
    @jax.custom_vjp
    def scan(a, u):
        return _lin_scan(a, u)

    def fwd(a, u):
        h = _lin_scan(a, u)
        return h, (a, h)

    def bwd(res, dh):
        a, h = res
        g = _lin_scan(_shift_up(a, 1), dh, reverse=True)
        return g * _shift_down(h, 1), g

    scan.defvjp(fwd, bwd)
    return scan


def _acc_out(ref, val):
    @pl.when(pl.program_id(0) == 0)
    def _():
        ref[...] = jnp.zeros_like(ref)

    ref[...] += val


FFN_CW = 128


def _ffn_fn(hg, hv, wg, wv, bg, bv, sd):
    cg = wg[0:1] * sd(hg, 2) + wg[1:2] * sd(hg, 1) + wg[2:3] * hg + bg
    cv = wv[0:1] * sd(hv, 2) + wv[1:2] * sd(hv, 1) + wv[2:3] * hv + bv
    return jax.nn.silu(cg) * cv


def _ffn_specs(t):
    nb = D_FF // FFN_CW
    col = lambda r, off: pl.BlockSpec((r, FFN_CW), lambda j: (0, j + off))
    return nb, [col(t, 0), col(t, nb), col(3, 0), col(3, nb), col(1, 0), col(1, nb)], col


def _ffn_mid_fwd(h, cw, cb, name):
    t = h.shape[0]
    nb, in_specs, col = _ffn_specs(t)

    def body(hg, hv, wg, wv, bg, bv, o_ref):
        o_ref[...] = _ffn_fn(hg[...], hv[...], wg[...], wv[...], bg[...], bv[...], _shift_down).astype(BF16)

    return _blocked(body, name=name, grid=(nb,), in_specs=in_specs, out_specs=col(t, 0),
                          out_shape=jax.ShapeDtypeStruct((t, D_FF), BF16),
                          compiler_params=_cparams(("parallel",), VMEM_MID))(h, h, cw, cw, cb, cb)


def _ffn_mid_bwd(h, cw, cb, dact, name):
    t = h.shape[0]
    nb, in_specs, col = _ffn_specs(t)

    def body(hg, hv, wg, wv, bg, bv, d_ref, dhg, dhv, dwg, dwv, dbg, dbv):
        fn = functools.partial(_ffn_fn, sd=_make_sd())
        _, vjp = jax.vjp(fn, hg[...], hv[...], wg[...], wv[...], bg[...], bv[...])
        g = vjp(d_ref[...])
        dhg[...] = g[0].astype(BF16)
        dhv[...] = g[1].astype(BF16)
        dwg[...], dwv[...], dbg[...], dbv[...] = g[2], g[3], g[4], g[5]

    big = jax.ShapeDtypeStruct((t, D_FF), BF16)
    w3 = jax.ShapeDtypeStruct((3, D_FF), F32)
    b1 = jax.ShapeDtypeStruct((1, D_FF), F32)
    return _blocked(body, name=name, grid=(nb,), in_specs=in_specs + [col(t, 0)],
                          out_specs=[col(t, 0), col(t, 0), col(3, 0), col(3, 0), col(1, 0), col(1, 0)],
                          out_shape=[big, big, w3, w3, b1, b1],
                          compiler_params=_cparams(("parallel",), VMEM_BIG))(h, h, cw, cw, cb, cb, dact)


TS_CW = 256


def _tshift_fn(p, mu, sd):
    return p + mu * (sd(p, 1) - p)


def _tshift_fwd(p, mu):
    t = p.shape[0]
    col = lambda r: pl.BlockSpec((r, TS_CW), lambda j: (0, j))

    def body(p_ref, mu_ref, o_ref):
        o_ref[...] = _tshift_fn(p_ref[...], mu_ref[...], _shift_down)

    return _blocked(body, name="tshift_fwd", grid=(SHIFT_COLS // TS_CW,), in_specs=[col(t), col(1)],
                          out_specs=col(t), out_shape=jax.ShapeDtypeStruct((t, SHIFT_COLS), F32),
                          compiler_params=_cparams(("parallel",), VMEM_MID))(p, mu)


def _tshift_bwd(p, mu, dpam):
    t = p.shape[0]
    col = lambda r: pl.BlockSpec((r, TS_CW), lambda j: (0, j))

    def body(p_ref, mu_ref, d_ref, dp_ref, dmu_ref):
        _, vjp = jax.vjp(functools.partial(_tshift_fn, sd=_make_sd()), p_ref[...], mu_ref[...])
        dp, dmu = vjp(d_ref[...])
        dp_ref[...] = dp.astype(BF16)
        dmu_ref[...] = dmu

    return _blocked(body, name="tshift_bwd", grid=(SHIFT_COLS // TS_CW,), in_specs=[col(t), col(1), col(t)],
                          out_specs=[col(t), col(1)],
                          out_shape=[jax.ShapeDtypeStruct((t, SHIFT_COLS), BF16),
                                     jax.ShapeDtypeStruct((1, SHIFT_COLS), F32)],
                          compiler_params=_cparams(("parallel",), VMEM_MID))(p, mu, dpam)


_HI = lax.Precision.HIGHEST
_O = (0, RW, 2 * RW, 3 * RW, 3 * RW + W_LORA, 3 * RW + W_LORA + A_LORA, SHIFT_COLS)


def _dot16(a, b, dims=(((1,), (0,)), ((), ()))):
    return lax.dot_general(a.astype(BF16), b.astype(BF16), dims, preferred_element_type=F32)


def _make_dot16():
    @jax.custom_vjp
    def dot(a, b):
        return _dot16(a, b)

    def fwd(a, b):
        return _dot16(a, b), (a, b)

    def bwd(res, g):
        a, b = res
        return _dot16(g, b, (((1,), (1,)), ((), ()))), _dot16(a, g, (((0,), (0,)), ((), ())))

    dot.defvjp(fwd, bwd)
    return dot


def _seg(x):
    first = lax.broadcasted_iota(jnp.int32, (x.shape[0], LANES), 1) < HEAD
    parts = []
    for p in range(x.shape[1] // LANES):
        xp = x[:, p * LANES:(p + 1) * LANES]
        s0 = jnp.sum(jnp.where(first, xp, 0.0), axis=-1, keepdims=True)
        s1 = jnp.sum(jnp.where(first, 0.0, xp), axis=-1, keepdims=True)
        parts.append(jnp.where(first, s0, s1))
    return jnp.concatenate(parts, axis=1)


def _prep_fn(r, k, v, wd, ad, gd, w0, w2, a0, a2, g2, k_k, k_a, dot):
    w_log = -jax.nn.softplus(-(w0 + dot(jnp.tanh(wd), w2))) - 0.5
    decay = jnp.exp(-jnp.exp(w_log))
    a = jax.nn.sigmoid(a0 + dot(ad, a2))
    g = dot(jax.nn.sigmoid(gd), g2)
    kk = k * k_k
    kk = kk / jnp.maximum(jnp.sqrt(_seg(kk * kk)), 1e-12)
    k2 = k * (1.0 + (a - 1.0) * k_a)
    return r, decay, k2, v, -kk, kk * a, g


_PREP_W = ("w0", "w2", "a0", "a2", "g2", "k_k", "k_a")


def _prep_wspecs(w):
    return [_full(w[n].shape) for n in _PREP_W]


def _rwkv_prep_fwd(pam, w):
    t = pam.shape[0]

    def body(p_ref, *refs):
        wr, outs = refs[:7], refs[7:]
        pieces = [p_ref[:, _O[i]:_O[i + 1]] for i in range(6)]
        res = _prep_fn(*pieces, *[x[...] for x in wr], _dot16)
        for o, val in zip(outs, res):
            o[...] = val

    row = lambda c: pl.BlockSpec((TOK, c), lambda i: (i, 0))
    return _blocked(body, name="rwkv_prep_fwd", grid=(t // TOK,),
                          in_specs=[row(SHIFT_COLS)] + _prep_wspecs(w), out_specs=[row(RW)] * 7,
                          out_shape=[jax.ShapeDtypeStruct((t, RW), F32)] * 7,
                          compiler_params=_cparams(("parallel",), VMEM_MID))(pam, *[w[n] for n in _PREP_W])


def _rwkv_prep_bwd(pam, w, cts, more):
    t = pam.shape[0]

    def body(p_ref, *refs):
        wr, ct, ex, dp_ref, dws = refs[:7], refs[7:14], refs[14:17], refs[17], refs[18:]
        pieces = [p_ref[:, _O[i]:_O[i + 1]] for i in range(6)]
        fn = lambda *a: _prep_fn(*a, _make_dot16())
        _, vjp = jax.vjp(fn, *pieces, *[x[...] for x in wr])
        c = [x[...] for x in ct]
        c[0] = c[0] + ex[0][...]
        c[2] = c[2] + ex[1][...]
        c[3] = c[3] + ex[2][...]
        g = vjp(tuple(c))
        for i in range(6):
            dp_ref[:, _O[i]:_O[i + 1]] = g[i]
        for o, val in zip(dws, g[6:]):
            _acc_out(o, val)

    row = lambda c: pl.BlockSpec((TOK, c), lambda i: (i, 0))
    return _blocked(body, name="rwkv_prep_bwd", grid=(t // TOK,),
                          in_specs=[row(SHIFT_COLS)] + _prep_wspecs(w) + [row(RW)] * 10,
                          out_specs=[row(SHIFT_COLS)] + [_full(w[n].shape) for n in _PREP_W],
                          out_shape=[jax.ShapeDtypeStruct((t, SHIFT_COLS), F32)]
                          + [jax.ShapeDtypeStruct(w[n].shape, F32) for n in _PREP_W],
                          compiler_params=_cparams(("arbitrary",), VMEM_MID))(
                              pam, *[w[n] for n in _PREP_W], *cts, *more)


def _post_fn(y, r, k2, v, g, ln_w, ln_b, r_k):
    inv = 1.0 / HEAD
    d = y - _seg(y) * inv
    yn = d * lax.rsqrt(_seg(d * d) * inv + GN_EPS) * ln_w + ln_b
    bonus = _seg(r * k2 * r_k) * v
    return (yn + bonus) * g


def _rwkv_post_fwd(y, r, k2, v, g, ln_w, ln_b, r_k):
    t = y.shape[0]

    def body(*refs):
        o_ref = refs[-1]
        o_ref[...] = _post_fn(*[x[...] for x in refs[:-1]]).astype(BF16)

    row = pl.BlockSpec((TOK, RW), lambda i: (i, 0))
    return _blocked(body, name="rwkv_post_fwd", grid=(t // TOK,),
                          in_specs=[row] * 5 + [_full((1, RW))] * 3, out_specs=row,
                          out_shape=jax.ShapeDtypeStruct((t, RW), BF16),
                          compiler_params=_cparams(("parallel",), VMEM_MID))(y, r, k2, v, g, ln_w, ln_b, r_k)


def _rwkv_post_bwd(y, r, k2, v, g, ln_w, ln_b, r_k, dya):
    t = y.shape[0]

    def body(*refs):
        ins, d_ref, outs = refs[:8], refs[8], refs[9:]
        _, vjp = jax.vjp(_post_fn, *[x[...] for x in ins])
        gr = vjp(d_ref[...])
        for o, val in zip(outs[:5], gr[:5]):
            o[...] = val
        for o, val in zip(outs[5:], gr[5:]):
            _acc_out(o, val)

    row = pl.BlockSpec((TOK, RW), lambda i: (i, 0))
    vec = _full((1, RW))
    return _blocked(body, name="rwkv_post_bwd", grid=(t // TOK,),
                          in_specs=[row] * 5 + [vec] * 3 + [row],
                          out_specs=[row] * 5 + [vec] * 3,
                          out_shape=[jax.ShapeDtypeStruct((t, RW), F32)] * 5 + [jax.ShapeDtypeStruct((1, RW), F32)] * 3,
                          compiler_params=_cparams(("arbitrary",), VMEM_MID))(y, r, k2, v, g, ln_w, ln_b, r_k, dya)


def _from_pt(x):
    n = x.shape[0]
    return x.reshape(n, HEAD, N_HEADS, PT).transpose(0, 3, 2, 1).reshape(n * PT, N_HEADS * HEAD)


def _lane_sum(x):
    return jnp.sum(x, axis=-1, keepdims=True)


def _pair_consts():
    lane = lax.broadcasted_iota(jnp.int32, (HEAD, LANES), 1)
    return lane, lane < HEAD


def _seg_sum_pair(x, first):
    return jnp.where(first, _lane_sum(jnp.where(first, x, 0.0)), _lane_sum(jnp.where(first, 0.0, x)))


def _to_pt(x):
    t = x.shape[0]
    return x.reshape(t // PT, PT, N_HEADS, HEAD).transpose(0, 3, 2, 1).reshape(t // PT, HEAD, N_HEADS * PT)


def _expand_cols(x, name):
    t = x.shape[0]
    tiles = WKV_CHUNK // PT

    def body(x_ref, o_ref):
        _, first = _pair_consts()
        for tl in range(tiles):
            tile = x_ref[tl]
            for j in range(PT):
                for p in range(N_HEADS // 2):
                    src = jnp.where(first, (2 * p) * PT + j, (2 * p + 1) * PT + j)
                    o_ref[tl * PT + j, :, p * LANES:(p + 1) * LANES] = jnp.take_along_axis(tile, src, axis=1)

    return _blocked(
        body, name=name, grid=(t // WKV_CHUNK,),
        in_specs=[pl.BlockSpec((tiles, HEAD, LANES), lambda i: (i, 0, 0))],
        out_specs=pl.BlockSpec((WKV_CHUNK, HEAD, RW), lambda i: (i, 0, 0)),
        out_shape=jax.ShapeDtypeStruct((t, HEAD, RW), F32),
        compiler_params=_cparams(("parallel",), VMEM_MID))(_to_pt(x))


def _wkv_fwd(w, k, z, b, v_exp):
    t = w.shape[0]
    nc = t // WKV_CHUNK
    pairs = N_HEADS // 2

    def body(w_ref, k_ref, z_ref, b_ref, v_ref, s_all, s_ref):
        @pl.when(pl.program_id(0) == 0)
        def _():
            s_ref[...] = jnp.zeros_like(s_ref)

        _, first = _pair_consts()

        def group(gi, carry):
            base = pl.multiple_of(gi * 8, 8)
            rows = [ref[pl.ds(base, 8), :] for ref in (w_ref, k_ref, z_ref, b_ref)]
            s = [s_ref[:, p * LANES:(p + 1) * LANES] for p in range(pairs)]
            for jj in range(8):
                for p in range(pairs):
                    cs = slice(p * LANES, (p + 1) * LANES)
                    wr, kr, zr, br = [x[jj:jj + 1, cs] for x in rows]
                    s_all[base + jj, :, cs] = s[p]
                    sa = _seg_sum_pair(s[p] * zr, first)
                    s[p] = s[p] * wr + sa * br + v_ref[base + jj, :, cs] * kr
            for p in range(pairs):
                s_ref[:, p * LANES:(p + 1) * LANES] = s[p]
            return carry

        lax.fori_loop(0, WKV_CHUNK // 8, group, 0)

    row = pl.BlockSpec((WKV_CHUNK, RW), lambda i: (i, 0))
    big = pl.BlockSpec((WKV_CHUNK, HEAD, RW), lambda i: (i, 0, 0))
    return _blocked(
        body, name="wkv_fwd", grid=(nc,), in_specs=[row] * 4 + [big], out_specs=[big, _full((HEAD, RW))],
        out_shape=[jax.ShapeDtypeStruct((t, HEAD, RW), F32), jax.ShapeDtypeStruct((HEAD, RW), F32)],
        compiler_params=_cparams(("arbitrary",), VMEM_MID))(w, k, z, b, v_exp)


def _wkv_out(r, s_all, s_last):
    t = r.shape[0]
    nc = t // WKV_CHUNK
    tiles = WKV_CHUNK // PT
    pairs = N_HEADS // 2

    def body(r_ref, s_ref, nxt_ref, last_ref, y_ref):
        lane, first = _pair_consts()
        after = jnp.where(pl.program_id(0) == nc - 1, last_ref[...], nxt_ref[0])
        for tl in range(tiles):
            ytile = jnp.zeros((HEAD, LANES), F32)
            for g in range(PT // 8):
                rows = r_ref[tl * PT + g * 8:tl * PT + g * 8 + 8, :]
                for jj in range(8):
                    tt = tl * PT + g * 8 + jj
                    j = g * 8 + jj
                    for p in range(pairs):
                        cs = slice(p * LANES, (p + 1) * LANES)
                        s = s_ref[tt + 1, :, cs] if tt + 1 < WKV_CHUNK else after[:, cs]
                        pr = s * rows[jj:jj + 1, cs]
                        y0 = _lane_sum(jnp.where(first, pr, 0.0))
                        y1 = _lane_sum(jnp.where(first, 0.0, pr))
                        ytile = jnp.where(lane == (2 * p) * PT + j, y0, ytile)
                        ytile = jnp.where(lane == (2 * p + 1) * PT + j, y1, ytile)
            y_ref[tl] = ytile

    row = pl.BlockSpec((WKV_CHUNK, RW), lambda i: (i, 0))
    pt = pl.BlockSpec((tiles, HEAD, LANES), lambda i: (i, 0, 0))
    big = pl.BlockSpec((WKV_CHUNK, HEAD, RW), lambda i: (i, 0, 0))
    nxt = pl.BlockSpec((1, HEAD, RW), lambda i: (jnp.minimum((i + 1) * WKV_CHUNK, t - 1), 0, 0))
    return _blocked(
        body, name="wkv_out", grid=(nc,), in_specs=[row, big, nxt, _full((HEAD, RW))], out_specs=pt,
        out_shape=jax.ShapeDtypeStruct((t // PT, HEAD, LANES), F32),
        compiler_params=_cparams(("parallel",), VMEM_MID))(r, s_all, s_all, s_last)


def _wkv_bwd(r, w, k, z, b, v_exp, s_all, dy_exp):
    t = r.shape[0]
    nc = t // WKV_CHUNK
    tiles = WKV_CHUNK // PT
    pairs = N_HEADS // 2

    def body(r_ref, w_ref, k_ref, z_ref, b_ref, v_ref, s_all_ref, dy_ref,
             dr_ref, dw_ref, dk_ref, dz_ref, db_ref, dv_ref, ds_ref):
        @pl.when(pl.program_id(0) == 0)
        def _():
            ds_ref[...] = jnp.zeros_like(ds_ref)

        lane, first = _pair_consts()
        col_sum = lambda x: jnp.sum(x, axis=0, keepdims=True)
        row8 = lax.broadcasted_iota(jnp.int32, (8, LANES), 0)
        for tl in reversed(range(tiles)):
            def group(gg, dvtile):
                gi = PT // 8 - 1 - gg
                base = pl.multiple_of(tl * PT + gi * 8, 8)
                rows = [ref[pl.ds(base, 8), :] for ref in (r_ref, w_ref, k_ref, z_ref, b_ref)]
                outs = (dr_ref, dw_ref, dk_ref, dz_ref, db_ref)
                tiles8 = {(id(o), p): jnp.zeros((8, LANES), F32) for o in outs for p in range(pairs)}
                ds = [ds_ref[:, p * LANES:(p + 1) * LANES] for p in range(pairs)]
                for jj in reversed(range(8)):
                    j = gi * 8 + jj
                    for p in range(pairs):
                        cs = slice(p * LANES, (p + 1) * LANES)

                        def put(ref, val, p=p, jj=jj):
                            tiles8[(id(ref), p)] = jnp.where(row8 == jj, val, tiles8[(id(ref), p)])

                        rr, wr, kr, zr, br = [x[jj:jj + 1, cs] for x in rows]
                        sp = s_all_ref[base + jj, :, cs]
                        vc = v_ref[base + jj, :, cs]
                        dyc = dy_ref[base + jj, :, cs]
                        sa = _seg_sum_pair(sp * zr, first)
                        st = sp * wr + sa * br + vc * kr
                        d = ds[p] + dyc * rr
                        put(dr_ref, col_sum(st * dyc))
                        dvk = d * kr
                        dv0 = _lane_sum(jnp.where(first, dvk, 0.0))
                        dv1 = _lane_sum(jnp.where(first, 0.0, dvk))
                        dvtile = jnp.where(lane == (2 * p) * PT + j, dv0, dvtile)
                        dvtile = jnp.where(lane == (2 * p + 1) * PT + j, dv1, dvtile)
                        put(dk_ref, col_sum(d * vc))
                        put(dw_ref, col_sum(sp * d))
                        u = _seg_sum_pair(d * br, first)
                        put(dz_ref, col_sum(sp * u))
                        put(db_ref, col_sum(d * sa))
                        ds[p] = d * wr + u * zr
                for p in range(pairs):
                    ds_ref[:, p * LANES:(p + 1) * LANES] = ds[p]
                for o in outs:
                    for p in range(pairs):
                        o[pl.ds(base, 8), p * LANES:(p + 1) * LANES] = tiles8[(id(o), p)]
                return dvtile

            dv_ref[tl] = lax.fori_loop(0, PT // 8, group, jnp.zeros((HEAD, LANES), F32))

    rev = lambda i: nc - 1 - i
    row = pl.BlockSpec((WKV_CHUNK, RW), lambda i: (rev(i), 0))
    pt = pl.BlockSpec((tiles, HEAD, LANES), lambda i: (rev(i), 0, 0))
    big = pl.BlockSpec((WKV_CHUNK, HEAD, RW), lambda i: (rev(i), 0, 0))
    return _blocked(
        body, name="wkv_bwd", grid=(nc,), in_specs=[row] * 5 + [big, big, big], out_specs=[row] * 5 + [pt],
        out_shape=[jax.ShapeDtypeStruct((t, RW), F32)] * 5 + [jax.ShapeDtypeStruct((t // PT, HEAD, LANES), F32)],
        scratch_shapes=[pltpu.VMEM((HEAD, RW), F32)],
        compiler_params=_cparams(("arbitrary",), VMEM_BIG))(r, w, k, z, b, v_exp, s_all, dy_exp)


LRU_CW = 128
_BX0 = SHIFT_COLS // LRU_CW
_BG0 = (SHIFT_COLS + LRU_W) // LRU_CW


def _lru_fn(bx, bg, cw, cb, ga, ba, gx, bxb, lam, sd, scan, dot):
    xc = cw[0:1] * sd(bx, 3) + cw[1:2] * sd(bx, 2) + cw[2:3] * sd(bx, 1) + cw[3:4] * bx + cb
    gr = jax.nn.sigmoid(dot(xc, ga) + ba)
    gi = jax.nn.sigmoid(dot(xc, gx) + bxb)
    log_a = -LRU_C * gr * jax.nn.softplus(-lam)
    a = jnp.exp(log_a)
    mult = jnp.sqrt(-jnp.tanh(log_a) * (jnp.exp(2.0 * log_a) + 1.0))
    return scan(a, xc * gi * mult) * jax.nn.gelu(bg)


def _lru_specs(t):
    col = lambda r, off=0: pl.BlockSpec((r, LRU_CW), lambda j: (0, j + off))
    diag = pl.BlockSpec((LRU_CW, LRU_CW), lambda j: (j, j))
    return col, [col(t, _BX0), col(t, _BG0), col(4), col(1), diag, col(1), diag, col(1), col(1)]


def _lru_fwd(p, cw, cb, ga, ba, gx, bxb, lam):
    t = p.shape[0]
    col, in_specs = _lru_specs(t)

    def body(*refs):
        o_ref = refs[-1]
        o_ref[...] = _lru_fn(*[x[...] for x in refs[:-1]], _shift_down, _lin_scan, _dot16).astype(BF16)

    return _blocked(body, name="lru_fwd", grid=(LRU_W // LRU_CW,), in_specs=in_specs, out_specs=col(t),
                          out_shape=jax.ShapeDtypeStruct((t, LRU_W), BF16),
                          compiler_params=_cparams(("parallel",), VMEM_MID))(p, p, cw, cb, ga, ba, gx, bxb, lam)


def _lru_bwd(p, cw, cb, ga, ba, gx, bxb, lam, dyb):
    t = p.shape[0]
    col, in_specs = _lru_specs(t)

    def body(*refs):
        ins, d_ref, outs = refs[:9], refs[9], refs[10:]
        fn = functools.partial(_lru_fn, sd=_make_sd(), scan=_make_scan(), dot=_make_dot16())
        _, vjp = jax.vjp(fn, *[x[...] for x in ins])
        g = vjp(d_ref[...])
        outs[0][...] = g[0].astype(BF16)
        outs[1][...] = g[1].astype(BF16)
        for o, val in zip(outs[2:], g[2:]):
            o[...] = val

    sq = pl.BlockSpec((LRU_CW, LRU_CW), lambda j: (j, 0))
    act = jax.ShapeDtypeStruct((t, LRU_W), BF16)
    vec = jax.ShapeDtypeStruct((1, LRU_W), F32)
    sqs = jax.ShapeDtypeStruct((LRU_W, LRU_CW), F32)
    return _blocked(body, name="lru_bwd", grid=(LRU_W // LRU_CW,), in_specs=in_specs + [col(t, RW // LRU_CW)],
                          out_specs=[col(t), col(t), col(4), col(1), sq, col(1), sq, col(1), col(1)],
                          out_shape=[act, act, jax.ShapeDtypeStruct((4, LRU_W), F32), vec, sqs, vec, sqs, vec, vec],
                          compiler_params=_cparams(("parallel",), VMEM_BIG))(p, p, cw, cb, ga, ba, gx, bxb, lam, dyb)


def _s5_disc_fn(a_re, a_im, log_dt, b_re, b_im, e):
    lam_re = jnp.minimum(a_re, -1e-4)
    lam_im = a_im
    dt = jnp.exp(log_dt)
    mag = jnp.exp(lam_re * dt)
    ab_re = mag * jnp.cos(lam_im * dt)
    ab_im = mag * jnp.sin(lam_im * dt)
    den = lam_re * lam_re + lam_im * lam_im
    zr = ab_re - 1.0
    q_re = jnp.dot((zr * lam_re + ab_im * lam_im) / den, e, precision=_HI)
    q_im = jnp.dot((ab_im * lam_re - zr * lam_im) / den, e, precision=_HI)
    return ab_re, ab_im, q_re * b_re - q_im * b_im, q_re * b_im + q_im * b_re


def _s5_disc_fwd(a_re, a_im, log_dt, b_re, b_im, e):
    def body(*refs):
        res = _s5_disc_fn(*[x[...] for x in refs[:6]])
        for o, val in zip(refs[6:], res):
            o[...] = val

    small = jax.ShapeDtypeStruct(a_re.shape, F32)
    wide = jax.ShapeDtypeStruct(b_re.shape, F32)
    return pl.pallas_call(body, name="s5_disc_fwd", out_shape=[small, small, wide, wide])(
        a_re, a_im, log_dt, b_re, b_im, e)


def _s5_disc_bwd(a_re, a_im, log_dt, b_re, b_im, e, cts):
    def body(*refs):
        ins, e_ref, ct, outs = refs[:5], refs[5], refs[6:10], refs[10:]
        _, vjp = jax.vjp(lambda *a: _s5_disc_fn(*a, e_ref[...]), *[x[...] for x in ins])
        for o, val in zip(outs, vjp(tuple(c[...] for c in ct))):
            o[...] = val

    shapes = [jax.ShapeDtypeStruct(x.shape, F32) for x in (a_re, a_im, log_dt, b_re, b_im)]
    return pl.pallas_call(body, name="s5_disc_bwd", out_shape=shapes)(a_re, a_im, log_dt, b_re, b_im, e, *cts)


def _cmul(a, b):
    return a[0] * b[0] - a[1] * b[1], a[0] * b[1] + a[1] * b[0]


def _s5_scan(sr, si, ab, reverse):
    n_tiles = sr.shape[0] // 8
    width = sr.shape[1]
    row8 = lax.broadcasted_iota(jnp.int32, (8, width), 0)
    p1 = ab
    p2 = _cmul(p1, p1)
    p4 = _cmul(p2, p2)
    pw = [p1]
    for _ in range(7):
        pw.append(_cmul(pw[-1], p1))
    cr = jnp.zeros((8, width), F32)
    ci = jnp.zeros((8, width), F32)
    for j in range(8):
        e = pw[7 - j] if reverse else pw[j]
        cr = jnp.where(row8 == j, e[0], cr)
        ci = jnp.where(row8 == j, e[1], ci)

    levels = []
    for d, q in ((1, p1), (2, p2), (4, p4)):
        keep = row8 < 8 - d if reverse else row8 >= d
        levels.append((d, (jnp.where(keep, q[0], 0.0), jnp.where(keep, q[1], 0.0))))

    def tile(i, carry):
        idx = n_tiles - 1 - i if reverse else i
        base = pl.multiple_of(idx * 8, 8)
        x = (sr[pl.ds(base, 8), :], si[pl.ds(base, 8), :])
        for d, q in levels:
            amt = 8 - d if reverse else d
            m = _cmul(q, (pltpu.roll(x[0], amt, 0), pltpu.roll(x[1], amt, 0)))
            x = (x[0] + m[0], x[1] + m[1])
        m = _cmul((cr, ci), carry)
        x = (x[0] + m[0], x[1] + m[1])
        sr[pl.ds(base, 8), :] = x[0]
        si[pl.ds(base, 8), :] = x[1]
        edge = slice(0, 1) if reverse else slice(7, 8)
        return x[0][edge], x[1][edge]

    zero = jnp.zeros((1, width), F32)
    lax.fori_loop(0, n_tiles, tile, (zero, zero))


_S5_W = S5_SLAB // S5_GROUP * S5_STATE


def _s5_specs(t):
    col = lambda r: pl.BlockSpec((r, S5_SLAB), lambda j: (0, j))
    bb = pl.BlockSpec((None, S5_SLAB, _S5_W), lambda j: (j, 0, 0))
    cd = pl.BlockSpec((None, _S5_W, S5_SLAB), lambda j: (j, 0, 0))
    ab = pl.BlockSpec((None, 1, _S5_W), lambda j: (j, 0, 0))
    return col, bb, cd, ab


def _s5_fwd(u, dvec, bbr, bbi, cdr, cdi, abr, abi):
    t, width = u.shape
    col, bb, cd, ab = _s5_specs(t)

    def body(u_ref, d_ref, bbr_ref, bbi_ref, cdr_ref, cdi_ref, abr_ref, abi_ref, o_ref, sr, si):
        uv = u_ref[...]
        sr[...] = _dot16(uv, bbr_ref[...])
        si[...] = _dot16(uv, bbi_ref[...])
        _s5_scan(sr, si, (abr_ref[...], abi_ref[...]), False)
        y = _dot16(sr[...], cdr_ref[...]) - _dot16(si[...], cdi_ref[...])
        o_ref[...] = jax.nn.gelu(y + d_ref[...] * uv).astype(BF16)

    return _blocked(body, name="s5_fwd", grid=(width // S5_SLAB,),
                          in_specs=[col(t), col(1), bb, bb, cd, cd, ab, ab], out_specs=col(t),
                          out_shape=jax.ShapeDtypeStruct((t, width), BF16),
                          scratch_shapes=[pltpu.VMEM((t, _S5_W), F32)] * 2,
                          compiler_params=_cparams(("parallel",), VMEM_BIG))(u, dvec, bbr, bbi, cdr, cdi, abr, abi)


def _s5_bwd(u, dvec, bbr, bbi, cdr, cdi, abr, abi, dyact):
    t, width = u.shape
    col, bb, cd, ab = _s5_specs(t)
    ns = width // S5_SLAB
    tn = (((0,), (0,)), ((), ()))
    nt = (((1,), (1,)), ((), ()))

    def body(u_ref, d_ref, bbr_ref, bbi_ref, cdr_ref, cdi_ref, abr_ref, abi_ref, dy_ref,
             du_ref, dd_ref, dbbr_ref, dbbi_ref, dcdr_ref, dcdi_ref, dabr_ref, dabi_ref, sr, si, gr, gi):
        uv = u_ref[...]
        dv = d_ref[...]
        abv = (abr_ref[...], abi_ref[...])
        sr[...] = _dot16(uv, bbr_ref[...])
        si[...] = _dot16(uv, bbi_ref[...])
        _s5_scan(sr, si, abv, False)
        y = _dot16(sr[...], cdr_ref[...]) - _dot16(si[...], cdi_ref[...])
        _, vjp = jax.vjp(jax.nn.gelu, y + dv * uv)
        (dpre,) = vjp(dy_ref[...].astype(F32))
        dd_ref[...] = jnp.sum(dpre * uv, axis=0, keepdims=True)
        dcdr_ref[...] = _dot16(sr[...], dpre, tn)
        dcdi_ref[...] = -_dot16(si[...], dpre, tn)
        gr[...] = _dot16(dpre, cdr_ref[...], nt)
        gi[...] = -_dot16(dpre, cdi_ref[...], nt)
        _s5_scan(gr, gi, (abv[0], -abv[1]), True)

        row8 = lax.broadcasted_iota(jnp.int32, (8, _S5_W), 0)

        def tile(i, carry):
            acc_r, acc_i, last_r, last_i = carry
            base = pl.multiple_of(i * 8, 8)
            s_r, s_i = sr[pl.ds(base, 8), :], si[pl.ds(base, 8), :]
            g_r, g_i = gr[pl.ds(base, 8), :], gi[pl.ds(base, 8), :]
            p_r = jnp.where(row8 == 0, last_r, pltpu.roll(s_r, 1, 0))
            p_i = jnp.where(row8 == 0, last_i, pltpu.roll(s_i, 1, 0))
            acc_r = acc_r + jnp.sum(g_r * p_r + g_i * p_i, axis=0, keepdims=True)
            acc_i = acc_i + jnp.sum(g_i * p_r - g_r * p_i, axis=0, keepdims=True)
            return acc_r, acc_i, s_r[7:8], s_i[7:8]

        zero = jnp.zeros((1, _S5_W), F32)
        acc_r, acc_i, _, _ = lax.fori_loop(0, t // 8, tile, (zero, zero, zero, zero))
        dabr_ref[...] = acc_r
        dabi_ref[...] = acc_i
        du_ref[...] = dpre * dv + _dot16(gr[...], bbr_ref[...], nt) + _dot16(gi[...], bbi_ref[...], nt)
        dbbr_ref[...] = _dot16(uv, gr[...], tn)
        dbbi_ref[...] = _dot16(uv, gi[...], tn)

    sds = jax.ShapeDtypeStruct
    return _blocked(
        body, name="s5_bwd", grid=(ns,), in_specs=[col(t), col(1), bb, bb, cd, cd, ab, ab, col(t)],
        out_specs=[col(t), col(1), bb, bb, cd, cd, ab, ab],
        out_shape=[sds((t, width), F32), sds((1, width), F32), sds((ns, S5_SLAB, _S5_W), F32),
                   sds((ns, S5_SLAB, _S5_W), F32), sds((ns, _S5_W, S5_SLAB), F32), sds((ns, _S5_W, S5_SLAB), F32),
                   sds((ns, 1, _S5_W), F32), sds((ns, 1, _S5_W), F32)],
        scratch_shapes=[pltpu.VMEM((t, _S5_W), F32)] * 4,
        compiler_params=_cparams(("parallel",), VMEM_BIG))(u, dvec, bbr, bbi, cdr, cdi, abr, abi, dyact)


def _gate_dense(w):
    h = w.shape[0]
    return jnp.einsum("hij,hg->higj", w, jnp.eye(h, dtype=F32)).reshape(h * HEAD, h * HEAD)


def _gate_blocks(d):
    x = d.reshape(LRU_W // LRU_CW, 2, HEAD, 2, HEAD)
    return jnp.einsum("tgihj,gh->tgij", x, jnp.eye(2, dtype=F32)).reshape(LRU_W // HEAD, HEAD, HEAD)


_GPS = S5_SLAB // S5_GROUP
_NS = S5_GROUPS // _GPS


def _s5_in_dense(bb):
    x = bb.reshape(_NS, _GPS, S5_STATE, S5_GROUP)
    return jnp.einsum("sgnc,gh->sgchn", x, jnp.eye(_GPS, dtype=F32)).reshape(_NS, S5_SLAB, _S5_W)


def _s5_in_blocks(d):
    x = d.reshape(_NS, _GPS, S5_GROUP, _GPS, S5_STATE)
    return jnp.einsum("sgchn,gh->sgnc", x, jnp.eye(_GPS, dtype=F32)).reshape(S5_GROUPS, S5_STATE * S5_GROUP)


def _s5_out_dense(c):
    x = c.reshape(_NS, _GPS, S5_GROUP, S5_STATE)
    return jnp.einsum("sgcn,gh->shngc", x, jnp.eye(_GPS, dtype=F32)).reshape(_NS, _S5_W, S5_SLAB)


def _s5_out_blocks(d):
    x = d.reshape(_NS, _GPS, S5_STATE, _GPS, S5_GROUP)
    return jnp.einsum("shngc,gh->sgcn", x, jnp.eye(_GPS, dtype=F32)).reshape(S5_GROUPS, S5_GROUP, S5_STATE)


def _local_step(x, tgt, w, late_weights, send_grads):
    d_model = x.shape[1]
    gs = {}
    n_layers = w["f_norm_g"].shape[0]

    def ffn_fwd(xin, l):
        xn = _rms_fwd(xin, w["f_norm_g"][l:l + 1], f"rms_f{l}")
        h = _matmul(xn, w["f_w_up_t"][l], "nt", f"mm_f{l}_up")
        act = _ffn_mid_fwd(h, w["f_conv_w"][l], w["f_conv_b"][l:l + 1], f"ffn_mid_fwd{l}")
        return _matmul(act, w["f_w_down"][l], "nn", f"mm_f{l}_down", add=xin), (xin, xn, h, act)

    def ffn_bwd(g, saved, l):
        xin, xn, h, act = saved
        dact = _matmul(g, w["f_w_down"][l], "nt", f"mm_f{l}_dact")
        d_down = _matmul(act, g, "tn", f"mm_f{l}_ddown", out_dtype=BF16)
        dhg, dhv, dwg, dwv, dbg, dbv = _ffn_mid_bwd(h, w["f_conv_w"][l], w["f_conv_b"][l:l + 1], dact,
                                                    f"ffn_mid_bwd{l}")
        dh = jnp.concatenate([dhg, dhv], axis=1)
        dxn = _matmul(dh, w["f_w_up_t"][l], "nn", f"mm_f{l}_dxn")
        d_up = _matmul(dh, xn, "tn", f"mm_f{l}_dup", out_dtype=BF16)
        dx, dgn = _rms_bwd(xin, w["f_norm_g"][l:l + 1], dxn, g, f"rms_f{l}_bwd")
        return dx, d_up, d_down, jnp.concatenate([dwg, dwv], axis=1), jnp.concatenate([dbg, dbv], axis=1), dgn

    xn0 = _rms_fwd(x, w["e_norm_g"], "rms_e")
    p = _matmul(xn0, w["e_w_in_t"], "nt", "mm_e_in")
    pam = _tshift_fwd(p, w["e_mu"])
    pw = dict(w0=w["e_w0"], w2=w["e_w2"][0], a0=w["e_a0"], a2=w["e_a2"][0], g2=w["e_g2"][0],
              k_k=w["e_k_k"], k_a=w["e_k_a"])
    r, dec, k2, v, z, b, gate = _rwkv_prep_fwd(pam, pw)
    v_exp = _expand_cols(v, "wkv_expand_v")
    s_all, s_last = _wkv_fwd(dec, k2, z, b, v_exp)
    y_pt = _wkv_out(r, s_all, s_last)
    y = _from_pt(y_pt)
    rk = w["e_r_k"].reshape(1, RW)
    ya = _rwkv_post_fwd(y, r, k2, v, gate, w["e_ln_w"], w["e_ln_b"], rk)
    ga, gx = _gate_dense(w["e_gate_a_w"][0]), _gate_dense(w["e_gate_x_w"][0])
    lru_w = (w["e_conv_w"][0], w["e_conv_b"], ga, w["e_gate_a_b"], gx, w["e_gate_x_b"], w["e_lru_lambda"])
    yb = _lru_fwd(p, *lru_w)
    ycat = jnp.concatenate([ya, yb], axis=1)
    w = {**w, **late_weights(ycat)}
    x1 = _matmul(ycat, w["e_w_out"], "nn", "mm_e_out", add=x)
    x2, ffn0 = ffn_fwd(x1, 0)

    xn1 = _rms_fwd(x2, w["o_norm_g"], "rms_o")
    u = _matmul(xn1, w["o_w_in"], "nn", "mm_o_in")
    expand = jnp.kron(jnp.eye(S5_STATE, dtype=F32), jnp.ones((1, S5_GROUP), F32))
    disc_in = (w["o_A_re"][0], w["o_A_im"][0], w["o_log_dt"].reshape(S5_GROUPS, 1),
               w["o_B_re"][0].reshape(S5_GROUPS, -1), w["o_B_im"][0].reshape(S5_GROUPS, -1), expand)
    ab_re, ab_im, bb_re, bb_im = _s5_disc_fwd(*disc_in)
    s5_w = (w["o_D"], _s5_in_dense(bb_re), _s5_in_dense(bb_im), _s5_out_dense(w["o_C_re"][0]),
            _s5_out_dense(w["o_C_im"][0]), ab_re.reshape(_NS, 1, _S5_W), ab_im.reshape(_NS, 1, _S5_W))
    yact = _s5_fwd(u, *s5_w)
    zz = _matmul(yact, w["o_w_glu_t"], "nt", "mm_o_glu")
    x3 = _glu_fwd(x2, zz)
    x4, ffn1 = ffn_fwd(x3, 1)

    loss, g, gs["final_norm_g", 0] = _loss_head(x4, w["final_norm_g"].reshape(1, d_model), tgt)

    g, up1, down1, dcw1, dcb1, dfn1 = ffn_bwd(g, ffn1, 1)
    dz = _glu_bwd(zz, g)
    dyact = _matmul(dz, w["o_w_glu_t"], "nn", "mm_o_dyact")
    d_glu = _matmul(dz, yact, "tn", "mm_o_dglu", out_dtype=BF16)
    du, gs["o_D", 0], dbbr, dbbi, dcdr, dcdi, dabr, dabi = _s5_bwd(u, *s5_w, dyact)
    gs["o_C_re", 0] = _s5_out_blocks(dcdr).reshape(S5_GROUPS * S5_GROUP, S5_STATE)
    gs["o_C_im", 0] = _s5_out_blocks(dcdi).reshape(S5_GROUPS * S5_GROUP, S5_STATE)
    cts = (dabr.reshape(S5_GROUPS, S5_STATE), dabi.reshape(S5_GROUPS, S5_STATE), _s5_in_blocks(dbbr),
           _s5_in_blocks(dbbi))
    gs["o_A_re", 0], gs["o_A_im", 0], dlog_dt, gs["o_B_re", 0], gs["o_B_im", 0] = _s5_disc_bwd(*disc_in, cts)
    gs["o_log_dt", 0] = dlog_dt.reshape(1, S5_GROUPS)
    dxn = _matmul(du, w["o_w_in"], "nt", "mm_o_dxn")
    d_oin = _matmul(xn1, du, "tn", "mm_o_din", out_dtype=BF16)
    g, gs["o_norm_g", 0] = _rms_bwd(x2, w["o_norm_g"], dxn, g, "rms_o_bwd")
    g = send_grads("a", [("f_w_up", 1, up1), ("f_w_down", 1, down1), ("o_w_glu", 0, d_glu), ("o_w_in", 0, d_oin)], g)

    g, up0, down0, dcw0, dcb0, dfn0 = ffn_bwd(g, ffn0, 0)
    gs["f_conv_w", 0], gs["f_conv_w", 3] = dcw0, dcw1
    gs["f_conv_b", 0], gs["f_conv_b", 1] = dcb0, dcb1
    gs["f_norm_g", 0], gs["f_norm_g", 1] = dfn0, dfn1

    dycat = _matmul(g, w["e_w_out"], "nt", "mm_e_dycat")
    d_eout = _matmul(ycat, g, "tn", "mm_e_dout", out_dtype=BF16)
    dycat = send_grads("b", [("f_w_up", 0, up0), ("f_w_down", 0, down0), ("e_w_out", 0, d_eout)], dycat)
    dy, dr1, dk1, dv1, dgate, gs["e_ln_w", 0], gs["e_ln_b", 0], gs["e_r_k", 0] = _rwkv_post_bwd(
        y, r, k2, v, gate, w["e_ln_w"], w["e_ln_b"], rk, dycat)
    dr2, ddec, dk2, dzz, dbb, dv_pt = _wkv_bwd(r, dec, k2, z, b, v_exp, s_all, _expand_cols(dy, "wkv_expand_dy"))
    (dpam, gs["e_w0", 0], gs["e_w2", 0], gs["e_a0", 0], gs["e_a2", 0], gs["e_g2", 0], gs["e_k_k", 0],
     gs["e_k_a", 0]) = _rwkv_prep_bwd(pam, pw, (dr2, ddec, dk2, _from_pt(dv_pt), dzz, dbb, dgate), (dr1, dk1, dv1))
    dpa, gs["e_mu", 0] = _tshift_bwd(p, w["e_mu"], dpam)
    (dbx, dbg, gs["e_conv_w", 0], gs["e_conv_b", 0], dga, gs["e_gate_a_b", 0], dgx, gs["e_gate_x_b", 0],
     gs["e_lru_lambda", 0]) = _lru_bwd(p, *lru_w, dycat)
    gs["e_gate_a_w", 0] = _gate_blocks(dga).reshape(LRU_W, HEAD)
    gs["e_gate_x_w", 0] = _gate_blocks(dgx).reshape(LRU_W, HEAD)
    dp = jnp.concatenate([dpa, dbx, dbg], axis=1)
    d_ein = _matmul(dp, xn0, "tn", "mm_e_din", out_dtype=BF16)
    dp = send_grads("c", [("e_w_in", 0, d_ein)], dp)
    dxn = _matmul(dp, w["e_w_in_t"], "nn", "mm_e_dxn")
    grad_x, gs["e_norm_g", 0] = _rms_bwd(x, w["e_norm_g"], dxn, g, "rms_e_bwd")
    return loss, grad_x, gs


CAST_ROWS = 256


def _cast_shard(w3, layer, transpose, chip, name):
    _, rows, cols = w3.shape
    tr = _tile(rows, (CAST_ROWS, 176, 128))

    def body(c_ref, w_ref, o_ref):
        v = w_ref[...]
        o_ref[...] = (v.T if transpose else v).astype(BF16)

    in_spec = pl.BlockSpec((None, tr, cols), lambda i, c: (layer, i, 0))
    if transpose:
        out_spec, shape = pl.BlockSpec((None, cols, tr), lambda i, c: (c[0], 0, i)), (cols, rows)
    else:
        out_spec, shape = pl.BlockSpec((None, tr, cols), lambda i, c: (c[0], i, 0)), (rows, cols)
    grid_spec = pltpu.PrefetchScalarGridSpec(num_scalar_prefetch=1, grid=(rows // tr,), in_specs=[in_spec],
                                             out_specs=out_spec)
    return _blocked(body, name=name, grid_spec=grid_spec,
                          out_shape=jax.ShapeDtypeStruct((N_CHIPS,) + shape, BF16),
                          compiler_params=_cparams(("parallel",), VMEM_MID))(chip, w3)


_ANY = pl.BlockSpec(memory_space=pl.ANY)


def _coords():
    return lax.axis_index("x"), lax.axis_index("y"), lax.axis_index("c")


def _flip(v, d):
    return 1 - v if d else v


_CHIP_RELS = ((1, 0), (0, 1), (1, 1))
_DEV_RELS = tuple((dx, dy, dc) for dx in (0, 1) for dy in (0, 1) for dc in (0, 1))[1:]


_HBM = pl.BlockSpec(memory_space=pltpu.HBM)
_SEM = pl.BlockSpec(memory_space=pltpu.SEMAPHORE)
_EFFECT = pltpu.SideEffectType.DATAFLOW_SIDE_EFFECTING


def _in_hbm(a):
    return pltpu.with_memory_space_constraint(a, pltpu.HBM)


def _gather_copies(bufs, send, recv, landed):
    x, y, c = _coords()
    me = 2 * x + y
    res = []
    for i, buf in enumerate(bufs):
        for j, (dx, dy) in enumerate(_CHIP_RELS):
            px, py = _flip(x, dx), _flip(y, dy)
            k = i * len(_CHIP_RELS) + j
            res.append(pltpu.make_async_remote_copy(
                src_ref=buf.at[me], dst_ref=buf.at[2 * px + py if landed else me], send_sem=send.at[k],
                recv_sem=recv.at[k], device_id=(px, py, c), device_id_type=MESH))
    return res


def _scatter_copies(srcs, lands, send, recv, landed):
    x, y, c = _coords()
    me = 4 * x + 2 * y + c
    res = []
    for i, (src, land) in enumerate(zip(srcs, lands)):
        for j, (dx, dy, dc) in enumerate(_DEV_RELS):
            peer = (_flip(x, dx), _flip(y, dy), _flip(c, dc))
            pid = 4 * peer[0] + 2 * peer[1] + peer[2]
            k = i * len(_DEV_RELS) + j
            res.append(pltpu.make_async_remote_copy(
                src_ref=src.at[pid], dst_ref=land.at[pid if landed else me], send_sem=send.at[k],
                recv_sem=recv.at[k], device_id=peer, device_id_type=MESH))
    return res


def _split_start(bufs, n_src, copies, n_rel, name, after):
    n = len(bufs)
    nk = n_src * n_rel

    def body(*refs):
        ins, send, recv, token = refs[:n], refs[n + 1 + n], refs[n + 2 + n], refs[-1]
        for cp in copies(ins, send, recv, False):
            cp.start()
        token[...] = jnp.zeros_like(token)

    res = pl.pallas_call(
        body, name=name, in_specs=[_HBM] * n + [_ANY],
        out_specs=[_HBM] * n + [_SEM, _SEM, pl.BlockSpec(memory_space=pltpu.VMEM)],
        out_shape=[pltpu.HBM(b.shape, b.dtype) for b in bufs]
        + [pltpu.SemaphoreType.DMA((nk,)), pltpu.SemaphoreType.DMA((nk,)), jax.ShapeDtypeStruct((8, LANES), F32)],
        input_output_aliases={i: i for i in range(n)},
        compiler_params=pltpu.CompilerParams(has_side_effects=_EFFECT))(*[_in_hbm(b) for b in bufs], after)
    return res[n], res[n + 1], list(res[:n]), res[n + 2]


def _split_wait(bufs, send, recv, copies, name, after):
    n = len(bufs)

    def body(*refs):
        ins, send_ref, recv_ref = refs[:n], refs[n], refs[n + 1]
        for cp in copies(ins, send_ref, recv_ref, True):
            cp.wait_send()
            cp.wait_recv()

    return pl.pallas_call(
        body, name=name, in_specs=[_HBM] * n + [_SEM, _SEM, _ANY], out_specs=[_HBM] * n,
        out_shape=[pltpu.HBM(b.shape, b.dtype) for b in bufs], input_output_aliases={i: i for i in range(n)},
        compiler_params=pltpu.CompilerParams(has_side_effects=_EFFECT))(*bufs, send, recv, after)


def _gather_start(bufs, name, after):
    return _split_start(bufs, len(bufs), _gather_copies, len(_CHIP_RELS), name, after)


def _gather_wait(bufs, send, recv, name, after):
    return _split_wait(bufs, send, recv, _gather_copies, name, after)


def _scatter_start(srcs, name, after):
    n = len(srcs)
    lands = [lax.empty(a.shape, a.dtype) for a in srcs]
    fn = lambda refs, send, recv, landed: _scatter_copies(refs[:n], refs[n:], send, recv, landed)
    send, recv, bufs, token = _split_start(list(srcs) + lands, n, fn, len(_DEV_RELS), name, after)
    return send, recv, bufs, token


def _scatter_wait(bufs, send, recv, name, after):
    n = len(bufs) // 2
    fn = lambda refs, s, r, landed: _scatter_copies(refs[:n], refs[n:], s, r, landed)
    res = _split_wait(bufs, send, recv, fn, name, after)
    return res[:n], res[n:]


def _sum_segments(src, land, me, name):
    nd, seg, cols = src.shape
    ts = _tile(seg, (256, 176, 128))

    def body(m_ref, *refs):
        o_ref = refs[-1]
        acc = refs[0][...].astype(F32)
        for r in refs[1:-1]:
            acc = acc + r[...].astype(F32)
        o_ref[...] = acc

    def peer(rel):
        bits = 4 * rel[0] + 2 * rel[1] + rel[2]
        return pl.BlockSpec((None, ts, cols), lambda i, m: (jnp.bitwise_xor(m[0], bits), i, 0))

    grid_spec = pltpu.PrefetchScalarGridSpec(
        num_scalar_prefetch=1, grid=(seg // ts,),
        in_specs=[pl.BlockSpec((None, ts, cols), lambda i, m: (m[0], i, 0))] + [peer(r) for r in _DEV_RELS],
        out_specs=pl.BlockSpec((None, ts, cols), lambda i, m: (m[1], i, 0)))
    return _blocked(body, name=name, grid_spec=grid_spec,
                          out_shape=jax.ShapeDtypeStruct((2, seg, cols), F32),
                          compiler_params=_cparams(("parallel",), VMEM_MID))(me, src, *[land] * len(_DEV_RELS))


def _exchange_sibling(arrs):
    n = len(arrs)

    def body(*refs):
        outs, (send, recv) = refs[n:2 * n], refs[2 * n:]
        x, y, c = _coords()
        sib = (x, y, 1 - c)
        sends, recvs = [], []
        for i in range(n):
            cp = pltpu.make_async_remote_copy(src_ref=outs[i].at[c], dst_ref=outs[i].at[c], send_sem=send.at[i],
                                              recv_sem=recv.at[i], device_id=sib, device_id_type=MESH)
            cp.start()
            sends.append(cp)
            recvs.append(pltpu.make_async_remote_copy(src_ref=outs[i].at[c], dst_ref=outs[i].at[1 - c],
                                                      send_sem=send.at[i], recv_sem=recv.at[i], device_id=sib,
                                                      device_id_type=MESH))
        for cp in recvs:
            cp.wait_recv()
        for cp in sends:
            cp.wait_send()

    return pl.pallas_call(
        body, name="exchange_sibling", in_specs=[_ANY] * n, out_specs=[_ANY] * n,
        out_shape=[jax.ShapeDtypeStruct(a.shape, a.dtype) for a in arrs],
        input_output_aliases={i: i for i in range(n)},
        scratch_shapes=[pltpu.SemaphoreType.DMA((n,)), pltpu.SemaphoreType.DMA((n,))])(*arrs)


def _allreduce_small(vec):
    nd, rows, lanes = vec.shape
    nr = len(_DEV_RELS)

    def body(in_ref, out_ref, stage, red, send, recv):
        x, y, c = _coords()
        me = 4 * x + 2 * y + c
        peers = []
        for dx, dy, dc in _DEV_RELS:
            peer = (_flip(x, dx), _flip(y, dy), _flip(c, dc))
            peers.append((peer, 4 * peer[0] + 2 * peer[1] + peer[2]))

        def copy(src, dst, k, peer):
            return pltpu.make_async_remote_copy(src_ref=src, dst_ref=dst, send_sem=send.at[k], recv_sem=recv.at[k],
                                                device_id=peer, device_id_type=MESH)

        first = [copy(in_ref.at[pid], stage.at[me], j, peer) for j, (peer, pid) in enumerate(peers)]
        for cp in first:
            cp.start()
        stage[me] = in_ref[me]
        for j, (peer, pid) in enumerate(peers):
            copy(in_ref.at[pid], stage.at[pid], j, peer).wait_recv()
        acc = stage[0]
        for d in range(1, nd):
            acc = acc + stage[d]
        red[...] = acc
        out_ref[me] = acc
        second = [copy(red, out_ref.at[me], nr + j, peer) for j, (peer, pid) in enumerate(peers)]
        for cp in second:
            cp.start()
        for j, (peer, pid) in enumerate(peers):
            copy(red, out_ref.at[pid], nr + j, peer).wait_recv()
        for cp in first + second:
            cp.wait_send()

    vm = pl.BlockSpec(memory_space=pltpu.VMEM)
    return pl.pallas_call(
        body, name="allreduce_small", in_specs=[vm], out_specs=vm,
        out_shape=jax.ShapeDtypeStruct(vec.shape, F32),
        scratch_shapes=[pltpu.VMEM(vec.shape, F32), pltpu.VMEM((rows, lanes), F32),
                        pltpu.SemaphoreType.DMA((2 * nr,)), pltpu.SemaphoreType.DMA((2 * nr,))],
        compiler_params=_cparams(None, VMEM_MID))(vec)


def _adam_math(w, g, m, v):
    m2 = ADAM_B1 * m + (1.0 - ADAM_B1) * g
    v2 = ADAM_B2 * v + (1.0 - ADAM_B2) * (g * g)
    m_hat = m2 / (1.0 - ADAM_B1 ** ADAM_STEP)
    v_hat = v2 / (1.0 - ADAM_B2 ** ADAM_STEP)
    return -ADAM_LR * (m_hat / (jnp.sqrt(v_hat) + ADAM_EPS) + ADAM_WD * w), m2, v2


def _adamw_big(w3, m3, v3, layer, g, transposed, name, prev=None):
    nl, rows, cols = w3.shape
    tr = 128 if transposed else _tile(rows, (256, 176, 128))

    def body(w_ref, m_ref, v_ref, g_ref, *rest):
        go_ref, d_ref, mo_ref, vo_ref = rest[-4:]
        g_val = g_ref[...].T if transposed else g_ref[...]
        go_ref[...] = g_val
        d_ref[...], mo_ref[...], vo_ref[...] = _adam_math(w_ref[...], g_val, m_ref[...], v_ref[...])

    wspec = pl.BlockSpec((None, tr, cols), lambda i: (layer, i, 0))
    gspec = pl.BlockSpec((cols, tr), lambda i: (0, i)) if transposed else pl.BlockSpec((tr, cols), lambda i: (i, 0))
    extra = [] if prev is None else list(prev)
    return _blocked(body, name=name, grid=(rows // tr,),
                          in_specs=[wspec, wspec, wspec, gspec] + [_ANY] * len(extra),
                          out_specs=[wspec] * 4, out_shape=[jax.ShapeDtypeStruct((nl, rows, cols), F32)] * 4,
                          input_output_aliases={4 + i: i for i in range(len(extra))},
                          compiler_params=_cparams(("parallel",), VMEM_MID))(w3, m3, v3, g, *extra)


_SMALL = (
    ("e_norm_g", (1, 1024), None), ("e_mu", (1, SHIFT_COLS), None), ("e_w0", (1, RW), None),
    ("e_w2", (W_LORA, RW), 128), ("e_a0", (1, RW), None), ("e_a2", (A_LORA, RW), 128), ("e_g2", (G_LORA, RW), 128),
    ("e_k_k", (1, RW), None), ("e_k_a", (1, RW), None), ("e_r_k", (1, RW), None), ("e_ln_w", (1, RW), None),
    ("e_ln_b", (1, RW), None), ("e_conv_w", (4, LRU_W), 128), ("e_conv_b", (1, LRU_W), None),
    ("e_gate_a_w", (LRU_W, HEAD), None), ("e_gate_a_b", (1, LRU_W), None), ("e_gate_x_w", (LRU_W, HEAD), None),
    ("e_gate_x_b", (1, LRU_W), None), ("e_lru_lambda", (1, LRU_W), None), ("o_norm_g", (1, 1024), 256),
    ("o_A_re", (S5_GROUPS, S5_STATE), None), ("o_A_im", (S5_GROUPS, S5_STATE), None), ("o_log_dt", (1, S5_GROUPS), None),
    ("o_B_re", (S5_GROUPS, S5_STATE * S5_GROUP), None), ("o_B_im", (S5_GROUPS, S5_STATE * S5_GROUP), None),
    ("o_C_re", (S5_GROUPS * S5_GROUP, S5_STATE), None), ("o_C_im", (S5_GROUPS * S5_GROUP, S5_STATE), None),
    ("o_D", (1, 1024), 256), ("f_norm_g", (2, 1024), None), ("f_conv_w", (6, 2 * D_FF), 2 * D_FF // 4),
    ("f_conv_b", (2, 2 * D_FF), None), ("final_norm_g", (1, 1024), None))
_PIECES = {"f_norm_g": ((0, 1), (1, 1)), "f_conv_b": ((0, 1), (1, 1)), "f_conv_w": ((0, 3), (3, 3))}


def _ceil_to(n, m):
    return -(-n // m) * m


def _small_layout():
    groups = {}
    for name, (rows, cols), _ in _SMALL:
        for first, r in _PIECES.get(name, ((0, rows),)):
            groups.setdefault(cols, []).append((name, first, r))
    layout, off = {}, 0
    for cols, items in groups.items():
        stacks = [0, 0] if 2 * cols <= LANES else [0]
        placed = []
        for name, first, r in sorted(items, key=lambda it: -it[2]):
            half = stacks.index(min(stacks))
            r0 = stacks[half]
            if r >= 8 or r0 % 8 + r > 8:
                r0 = _ceil_to(r0, 8)
            placed.append((name, first, r, r0, half * (LANES // 2)))
            stacks[half] = r0 + r
        rpad = _ceil_to(max(stacks), 8)
        for name, first, r, at, lane in placed:
            layout[name, first] = (off, rpad, at, r, cols, lane)
        off += -(-cols // LANES) * rpad
    return layout, _ceil_to(off, 8 * N_DEV)


def _small_pack(gs):
    layout, total = _small_layout()
    keys = list(layout)

    def body(*refs):
        out = refs[-1]
        out[...] = jnp.zeros_like(out)
        for key, g_ref in zip(keys, refs[:-1]):
            off, rpad, at, r, cols, lane = layout[key]
            for j in range(-(-cols // LANES)):
                cw = min(LANES, cols - j * LANES)
                out[off + j * rpad + at:off + j * rpad + at + r, lane:lane + cw] = g_ref[:, j * LANES:j * LANES + cw]

    return pl.pallas_call(body, name="small_pack", out_shape=jax.ShapeDtypeStruct((total, LANES), F32),
                          compiler_params=_cparams(None, VMEM_MID))(*[gs[k] for k in keys])


def _adamw_small(red, chip, wts, ms, vs):
    layout, _ = _small_layout()
    names = [n for n, _, _ in _SMALL]
    n = len(names)

    def body(chip_ref, red_ref, *refs):
        ins, outs = refs[:3 * n], refs[3 * n:]
        c = chip_ref[0]
        for i, (name, (rows, cols), loc) in enumerate(_SMALL):
            w_ref, m_ref, v_ref = ins[3 * i:3 * i + 3]
            o_refs = outs[4 * i:4 * i + 4]
            width = cols if loc is None else loc
            for first, r in _PIECES.get(name, ((0, rows),)):
                off, rpad, at, _, _, lane = layout[name, first]
                for j in range(-(-width // LANES)):
                    cw = min(LANES, width - j * LANES)
                    ls = slice(lane, lane + cw)
                    if loc is None:
                        start = off + j * rpad + at
                        g = red_ref[start:start + r, ls]
                    else:
                        blk = c * (loc // LANES) + j
                        if r >= 8:
                            g = red_ref[pl.ds(pl.multiple_of(off + at + blk * rpad, 8), r), ls]
                        else:
                            tile = red_ref[pl.ds(pl.multiple_of(off + at // 8 * 8 + blk * rpad, 8), 8), ls]
                            g = tile[at % 8:at % 8 + r]
                    rs, cs = slice(first, first + r), slice(j * LANES, j * LANES + cw)
                    d, m2, v2 = _adam_math(w_ref[rs, cs], g, m_ref[rs, cs], v_ref[rs, cs])
                    for o, val in zip(o_refs, (g, d, m2, v2)):
                        o[rs, cs] = val

    args, shapes = [], []
    for name in names:
        args += [wts[name], ms[name], vs[name]]
        shapes += [jax.ShapeDtypeStruct(wts[name].shape, F32)] * 4
    vm = pl.BlockSpec(memory_space=pltpu.VMEM)
    res = pl.pallas_call(body, name="adamw_small",
                         in_specs=[pl.BlockSpec(memory_space=pltpu.SMEM), vm] + [vm] * (3 * n),
                         out_specs=[vm] * (4 * n), out_shape=shapes,
                         compiler_params=_cparams(None, VMEM_BIG))(chip, red, *args)
    return {name: res[4 * i:4 * i + 4] for i, name in enumerate(names)}


PACK_ROWS = 8


def _packed_rows(shape):
    size = 1
    for d in shape:
        size *= d
    return -(-size // (PACK_ROWS * LANES)) * PACK_ROWS


def _pack(arrs, row_mult):
    parts = []
    for a in arrs:
        flat = a.reshape(-1).astype(F32)
        rows = _packed_rows(a.shape)
        parts.append(jnp.pad(flat, (0, rows * LANES - flat.shape[0])).reshape(rows, LANES))
    total = sum(p.shape[0] for p in parts)
    fill = -(-total // row_mult) * row_mult - total
    if fill:
        parts.append(jnp.zeros((fill, LANES), F32))
    return jnp.concatenate(parts, axis=0)


def _unpack(packed, shapes):
    out, off = [], 0
    for s in shapes:
        rows = _packed_rows(s)
        size = 1
        for d in s:
            size *= d
        out.append(packed[off:off + rows].reshape(-1)[:size].reshape(s))
        off += rows
    return out


_SMALL_SH = ("e_w2", "e_a2", "e_g2", "e_conv_w", "o_norm_g", "o_D", "f_conv_w")
_LARGE = (("e_w_in", True), ("e_w_out", False), ("o_w_in", False), ("o_w_glu", True), ("f_w_up", True),
        ("f_w_down", False))
_ORDER = ("e_norm_g", "e_w_in", "e_mu", "e_w0", "e_w2", "e_a0", "e_a2", "e_g2", "e_k_k", "e_k_a", "e_r_k", "e_ln_w",
          "e_ln_b", "e_conv_w", "e_conv_b", "e_gate_a_w", "e_gate_a_b", "e_gate_x_w", "e_gate_x_b", "e_lru_lambda",
          "e_w_out", "o_norm_g", "o_w_in", "o_A_re", "o_A_im", "o_log_dt", "o_B_re", "o_B_im", "o_C_re", "o_C_im",
          "o_D", "o_w_glu", "f_norm_g", "f_w_up", "f_conv_w", "f_conv_b", "f_w_down", "final_norm_g")
N_CHIPS = 4
N_DEV = 8


def _step(x, tgt, wts, ms, vs):
    xi, yi, ci = _coords()
    chip = 2 * xi + yi
    chip1 = chip.astype(jnp.int32).reshape(1)
    me2 = jnp.stack([4 * xi + 2 * yi + ci, ci]).astype(jnp.int32)
    by_cols = dict(_LARGE)

    bufs = {(name, l): _cast_shard(wts[name], l, by_cols[name], chip1, f"cast_{name}{l}")
            for name, _ in _LARGE for l in range(wts[name].shape[0])}
    sh_shapes = [wts[n].shape for n in _SMALL_SH]
    packed = _pack([wts[n] for n in _SMALL_SH], 8)
    small_buf = lax.dynamic_update_slice(jnp.zeros((N_CHIPS,) + packed.shape, F32), packed[None], (chip, 0, 0))
    early = [("e_w_in", 0)]
    late = [k for k in bufs if k not in early]
    send, recv, thru, token = _gather_start([bufs[k] for k in early] + [small_buf], "gather_start_a", x)
    got = _gather_wait(thru, send, recv, "gather_wait_a", token)
    send_b, recv_b, thru_b, token = _gather_start([bufs[k] for k in late], "gather_start_b", got[0])
    x, _ = lax.optimization_barrier((x, token))

    def rows(g):
        return g.reshape(N_CHIPS * g.shape[1], g.shape[2])

    full = {n: wts[n] for n, _, loc in _SMALL if loc is None}
    full["e_w_in_t"] = rows(got[0])
    per_chip = [_unpack(got[1][k], sh_shapes) for k in range(N_CHIPS)]
    for i, n in enumerate(_SMALL_SH):
        full[n] = jnp.concatenate([per_chip[k][i] for k in range(N_CHIPS)], axis=-1)

    def late_weights(after):
        res = dict(zip(late, _gather_wait(thru_b, send_b, recv_b, "gather_wait_b", after)))
        return {"e_w_out": rows(res[("e_w_out", 0)]), "o_w_in": rows(res[("o_w_in", 0)]),
                "o_w_glu_t": rows(res[("o_w_glu", 0)]),
                "f_w_up_t": [rows(res[("f_w_up", l)]) for l in range(2)],
                "f_w_down": [rows(res[("f_w_down", l)]) for l in range(2)]}

    pending = []

    def send_grads(tag, items, carry):
        srcs = [g.reshape(N_DEV, g.shape[0] // N_DEV, g.shape[1]) for _, _, g in items]
        s_sem, r_sem, both, tok = _scatter_start(srcs, f"scatter_start_{tag}", carry)
        pending.append((tag, [(name, l) for name, l, _ in items], s_sem, r_sem, both))
        carry, _ = lax.optimization_barrier((carry, tok))
        return carry

    loss, grad_x, gs = _local_step(x, tgt, full, late_weights, send_grads)

    final = {}
    red = _allreduce_small(_small_pack(gs).reshape(N_DEV, -1, LANES)).reshape(-1, LANES)
    view = {name: (rows, cols if loc is None else loc) for name, (rows, cols), loc in _SMALL}
    as2d = lambda d: {name: d[name].reshape(view[name]) for name in view}
    small = _adamw_small(red, chip1, as2d(wts), as2d(ms), as2d(vs))
    for name, res in small.items():
        final[name] = [r.reshape(wts[name].shape) for r in res]
    new_v = small["final_norm_g"][3]

    halves, keys = [], []
    for tag, names, s_sem, r_sem, both in pending:
        srcs, lands = _scatter_wait(both, s_sem, r_sem, f"scatter_wait_{tag}", new_v)
        for (name, l), src, land in zip(names, srcs, lands):
            halves.append(_sum_segments(src, land, me2, f"sum_{name}{l}"))
            keys.append((name, l))
    shards = _exchange_sibling(halves)
    for s, (name, l) in zip(shards, keys):
        final[name] = _adamw_big(wts[name], ms[name], vs[name], l, s.reshape(2 * s.shape[1], s.shape[2]),
                                 by_cols[name], f"adamw_{name}{l}", prev=final.get(name))

    loss = lax.psum(loss[0, 0], ("x", "y", "c"))
    res = [loss, grad_x[None]]
    for k in range(4):
        res += [final[n][k] for n in _ORDER]
    return tuple(res)


def kernel(x, e_norm_g, e_w_in, e_mu, e_w0, e_w2, e_a0, e_a2, e_g2, e_k_k, e_k_a, e_r_k, e_ln_w, e_ln_b, e_conv_w, e_conv_b, e_gate_a_w, e_gate_a_b, e_gate_x_w, e_gate_x_b, e_lru_lambda, e_w_out, o_norm_g, o_w_in, o_A_re, o_A_im, o_log_dt, o_B_re, o_B_im, o_C_re, o_C_im, o_D, o_w_glu, f_norm_g, f_w_up, f_conv_w, f_conv_b, f_w_down, final_norm_g, loss_target, m_e_norm_g, m_e_w_in, m_e_mu, m_e_w0, m_e_w2, m_e_a0, m_e_a2, m_e_g2, m_e_k_k, m_e_k_a, m_e_r_k, m_e_ln_w, m_e_ln_b, m_e_conv_w, m_e_conv_b, m_e_gate_a_w, m_e_gate_a_b, m_e_gate_x_w, m_e_gate_x_b, m_e_lru_lambda, m_e_w_out, m_o_norm_g, m_o_w_in, m_o_A_re, m_o_A_im, m_o_log_dt, m_o_B_re, m_o_B_im, m_o_C_re, m_o_C_im, m_o_D, m_o_w_glu, m_f_norm_g, m_f_w_up, m_f_conv_w, m_f_conv_b, m_f_w_down, m_final_norm_g, v_e_norm_g, v_e_w_in, v_e_mu, v_e_w0, v_e_w2, v_e_a0, v_e_a2, v_e_g2, v_e_k_k, v_e_k_a, v_e_r_k, v_e_ln_w, v_e_ln_b, v_e_conv_w, v_e_conv_b, v_e_gate_a_w, v_e_gate_a_b, v_e_gate_x_w, v_e_gate_x_b, v_e_lru_lambda, v_e_w_out, v_o_norm_g, v_o_w_in, v_o_A_re, v_o_A_im, v_o_log_dt, v_o_B_re, v_o_B_im, v_o_C_re, v_o_C_im, v_o_D, v_o_w_glu, v_f_norm_g, v_f_w_up, v_f_conv_w, v_f_conv_b, v_f_w_down, v_final_norm_g):
    args = locals()
    wts = {n: args[n] for n in _ORDER}
    ms = {n: args["m_" + n] for n in _ORDER}
    vs = {n: args["v_" + n] for n in _ORDER}
    return _step(x[0], loss_target[0], wts, ms, vs)
```

```python
import functools

import jax
import jax.numpy as jnp
from jax import lax
from jax.experimental import pallas as pl
from jax.experimental.pallas import tpu as pltpu

F32 = jnp.float32
BF16 = jnp.bfloat16
MESH = pl.DeviceIdType.MESH

D_MODEL = 1024
HEAD = 64
RW = 512
N_HEADS = RW // HEAD
LRU_W = 512
SHIFT_COLS = 1792
W_LORA, A_LORA, G_LORA = 64, 64, 128
S5_GROUPS, S5_GROUP, S5_STATE = 64, 16, 64
D_FF = 2816
NORM_EPS = 1e-6
GN_EPS = 64e-5
LRU_C = 8.0
ADAM_LR, ADAM_B1, ADAM_B2, ADAM_EPS, ADAM_WD, ADAM_STEP = 0.001, 0.9, 0.999, 1e-08, 0.01, 10

VMEM_BIG = 56 * 1024 * 1024
VMEM_MID = 40 * 1024 * 1024
LANES = 128
PT = 16
WKV_CHUNK = 32
S5_SLAB = 128


def _blocked(*args, **kw):
    call = pl.pallas_call(*args, **kw)

    def run(*ops):
        return call(*[pltpu.with_memory_space_constraint(a, pltpu.HBM) if a.ndim >= 2 else a for a in ops])

    return run


def _cparams(sem=None, vmem=None):
    kw = {}
    if sem is not None:
        kw["dimension_semantics"] = sem
    if vmem is not None:
        kw["vmem_limit_bytes"] = vmem
    return pltpu.CompilerParams(**kw)


def _tile(dim, cands):
    for c in cands:
        if dim % c == 0:
            return c
    return dim


def _full(shape):
    n = len(shape)
    return pl.BlockSpec(shape, lambda *_: (0,) * n)


_TILES = (2816, 2048, 1408, 1024, 512, 256, 128)
MM_BUDGET = 36 * 1024 * 1024
VMEM_SLACK = 12 * 1024 * 1024


MXU_FLOPS = 9.0e14
HBM_BYTES = 3.3e12
STEP_SECONDS = 0.35e-6


def _mm_tiles(m, n, k, size_a, size_b, size_o, has_add, parts=1, tk_only=None, tm_max=None):
    best = None
    for tm in _TILES:
        for tk in _TILES:
            for tn in _TILES:
                if m % tm or n % tn or k % tk or (tk_only and tk != tk_only) or (tm_max and tm_max % tm):
                    continue
                need = (2 * (parts * tm * tk * size_a + tk * tn * size_b + tm * tn * size_o)
                        + tm * tn * 4 * (1 + 2 * has_add))
                if k > tk:
                    need += tm * tn * 4
                if need > MM_BUDGET:
                    continue
                steps = (m // tm) * (n // tn) * (k // tk)
                a_reads = n // tn if k > tk else 1
                moved = (m * k * size_a * a_reads + k * n * size_b * (m // tm) + m * n * (size_o + 4 * has_add))
                cost = max(2.0 * m * n * k / MXU_FLOPS, moved / HBM_BYTES) + steps * STEP_SECONDS
                cand = (-cost, tk, tm, tn)
                if best is None or cand > best[0]:
                    best = (cand, need)
    (_, tk, tm, tn), need = best
    return tm, tn, tk, need


def _matmul(a, b, mode, name, out_dtype=F32, add=None):
    parts = a if isinstance(a, tuple) else (a,)
    na = len(parts)
    wide = parts[0].shape[1]
    if mode == "nn":
        (m, k), (k2, n) = (parts[0].shape[0], na * wide), b.shape
    elif mode == "nt":
        (m, k), (n, k2) = (parts[0].shape[0], na * wide), b.shape
    else:
        (k, m), (k2, n) = (parts[0].shape[0], na * wide), b.shape
    assert k == k2, (parts[0].shape, b.shape, mode)
    split = {} if na == 1 else ({"tm_max": wide} if mode == "tn" else {"tk_only": wide})
    tm, tn, tk, need = _mm_tiles(m, n, k, parts[0].dtype.itemsize, b.dtype.itemsize, jnp.dtype(out_dtype).itemsize,
                                 add is not None, parts=na, **split)
    nk = k // tk
    per_part = wide // (tm if mode == "tn" else tk)
    dims = {"nn": (((1,), (0,)), ((), ())), "nt": (((1,), (1,)), ((), ())), "tn": (((0,), (0,)), ((), ()))}[mode]

    def body(*refs):
        a_refs, b_ref = refs[:na], refs[na]
        add_ref = refs[na + 1] if add is not None else None
        o_ref = refs[na + 2] if add is not None else refs[na + 1]
        kk = pl.program_id(2)

        def finish(r):
            if add_ref is not None:
                r = r + add_ref[...]
            o_ref[...] = r.astype(o_ref.dtype)

        def use(a_ref):
            part = lax.dot_general(a_ref[...].astype(BF16), b_ref[...].astype(BF16), dims, preferred_element_type=F32)
            if nk == 1:
                finish(part)
                return
            acc = refs[-1]

            @pl.when(kk == 0)
            def _():
                acc[...] = part

            @pl.when(kk > 0)
            def _():
                acc[...] += part

            @pl.when(kk == nk - 1)
            def _():
                finish(acc[...])

        if na == 1:
            use(a_refs[0])
        else:
            which = (pl.program_id(0) if mode == "tn" else kk) // per_part
            for p in range(na):
                pl.when(which == p)(functools.partial(use, a_refs[p]))

    def a_spec(p):
        def along(pos):
            return jnp.clip(pos - p * per_part, 0, per_part - 1) if na > 1 else pos
        if mode == "tn":
            return pl.BlockSpec((tk, tm), lambda i, j, kk: (kk, along(i)))
        return pl.BlockSpec((tm, tk), lambda i, j, kk: (i, along(kk)))

    if mode == "nn":
        b_spec = pl.BlockSpec((tk, tn), lambda i, j, kk: (kk, j))
    elif mode == "nt":
        b_spec = pl.BlockSpec((tn, tk), lambda i, j, kk: (j, kk))
    else:
        b_spec = pl.BlockSpec((tk, tn), lambda i, j, kk: (kk, j))
    o_spec = pl.BlockSpec((tm, tn), lambda i, j, kk: (i, j))
    in_specs = [a_spec(p) for p in range(na)] + [b_spec] + ([o_spec] if add is not None else [])
    args = parts + (b,) + ((add,) if add is not None else ())
    return _blocked(
        body, name=name, grid=(m // tm, n // tn, nk),
        in_specs=in_specs, out_specs=o_spec,
        out_shape=jax.ShapeDtypeStruct((m, n), out_dtype),
        scratch_shapes=[pltpu.VMEM((tm, tn), F32)] if nk > 1 else [],
        compiler_params=_cparams(("parallel", "parallel", "arbitrary"), min(VMEM_BIG, need + VMEM_SLACK)),
    )(*args)


TOK = 256


def _rms(x, g):
    return x * lax.rsqrt(jnp.mean(x * x, axis=-1, keepdims=True) + NORM_EPS) * g


def _rms_fwd(x, g, name):
    t, d = x.shape

    def body(x_ref, g_ref, o_ref):
        o_ref[...] = _rms(x_ref[...], g_ref[...]).astype(BF16)

    row = pl.BlockSpec((TOK, d), lambda i: (i, 0))
    return _blocked(body, name=name, grid=(t // TOK,), in_specs=[row, _full((1, d))], out_specs=row,
                          out_shape=jax.ShapeDtypeStruct((t, d), BF16),
                          compiler_params=_cparams(("parallel",)))(x, g)


def _rms_bwd(x, g, dxn, res, name):
    t, d = x.shape

    def body(x_ref, g_ref, d_ref, res_ref, dx_ref, dg_ref):
        _, vjp = jax.vjp(_rms, x_ref[...], g_ref[...])
        dx, dg = vjp(d_ref[...].astype(F32))
        dx_ref[...] = dx + res_ref[...]

        @pl.when(pl.program_id(0) == 0)
        def _():
            dg_ref[...] = jnp.zeros_like(dg_ref)

        dg_ref[...] += dg

    row = pl.BlockSpec((TOK, d), lambda i: (i, 0))
    return _blocked(body, name=name, grid=(t // TOK,), in_specs=[row, _full((1, d)), row, row],
                          out_specs=[row, _full((1, d))],
                          out_shape=[jax.ShapeDtypeStruct((t, d), F32), jax.ShapeDtypeStruct((1, d), F32)],
                          compiler_params=_cparams(("arbitrary",)))(x, g, dxn, res)


def _loss_head(x, g, tgt):
    t, d = x.shape

    def body(x_ref, g_ref, t_ref, l_ref, dx_ref, dg_ref):
        tg = t_ref[...]

        def fn(xv, gv):
            err = _rms(xv, gv) - tg
            per_tok = jnp.mean(err * err, axis=-1, keepdims=True)
            return 0.5 * jnp.sum(per_tok, axis=0, keepdims=True)

        l, vjp = jax.vjp(fn, x_ref[...], g_ref[...])
        dx, dg = vjp(jnp.ones((1, 1), F32))
        dx_ref[...] = dx

        @pl.when(pl.program_id(0) == 0)
        def _():
            dg_ref[...] = jnp.zeros_like(dg_ref)
            l_ref[...] = jnp.zeros_like(l_ref)

        dg_ref[...] += dg
        l_ref[...] += jnp.broadcast_to(l, l_ref.shape)

    row = pl.BlockSpec((TOK, d), lambda i: (i, 0))
    return _blocked(body, name="loss_head", grid=(t // TOK,), in_specs=[row, _full((1, d)), row],
                          out_specs=[_full((1, LANES)), row, _full((1, d))],
                          out_shape=[jax.ShapeDtypeStruct((1, LANES), F32), jax.ShapeDtypeStruct((t, d), F32),
                                     jax.ShapeDtypeStruct((1, d), F32)],
                          compiler_params=_cparams(("arbitrary",)))(x, g, tgt)


def _glu_fwd(x, z):
    t, d = x.shape

    def body(x_ref, v_ref, g_ref, o_ref):
        o_ref[...] = x_ref[...] + v_ref[...] * jax.nn.sigmoid(g_ref[...])

    row = pl.BlockSpec((TOK, d), lambda i: (i, 0))
    gate = pl.BlockSpec((TOK, d), lambda i: (i, 1))
    return _blocked(body, name="glu_fwd", grid=(t // TOK,), in_specs=[row, row, gate], out_specs=row,
                          out_shape=jax.ShapeDtypeStruct((t, d), F32),
                          compiler_params=_cparams(("parallel",)))(x, z, z)


def _glu_bwd(z, g):
    t, d = g.shape

    def body(v_ref, g_ref, d_ref, o_ref):
        s = jax.nn.sigmoid(g_ref[...])
        dy = d_ref[...]
        o_ref[:, :d] = (dy * s).astype(BF16)
        o_ref[:, d:] = (dy * v_ref[...] * s * (1.0 - s)).astype(BF16)

    row = pl.BlockSpec((TOK, d), lambda i: (i, 0))
    gate = pl.BlockSpec((TOK, d), lambda i: (i, 1))
    return _blocked(body, name="glu_bwd", grid=(t // TOK,), in_specs=[row, gate, row],
                          out_specs=pl.BlockSpec((TOK, 2 * d), lambda i: (i, 0)),
                          out_shape=jax.ShapeDtypeStruct((t, 2 * d), BF16),
                          compiler_params=_cparams(("parallel",)))(z, z, g)


def _shift_down(x, d):
    row = lax.broadcasted_iota(jnp.int32, x.shape, 0)
    return jnp.where(row < d, 0.0, pltpu.roll(x, d, 0))


def _shift_up(x, d):
    n = x.shape[0]
    row = lax.broadcasted_iota(jnp.int32, x.shape, 0)
    return jnp.where(row >= n - d, 0.0, pltpu.roll(x, n - d, 0))


def _make_sd():
    @functools.partial(jax.custom_vjp, nondiff_argnums=(1,))
    def sd(x, d):
        return _shift_down(x, d)

    def fwd(x, d):
        return _shift_down(x, d), None

    def bwd(d, _, g):
        return (_shift_up(g, d),)

    sd.defvjp(fwd, bwd)
    return sd


def _lin_scan(a, u, reverse=False):
    n = a.shape[0]
    row = lax.broadcasted_iota(jnp.int32, a.shape, 0)
    d = 1
    while d < n:
        if reverse:
            keep = row < n - d
            a_s, u_s = pltpu.roll(a, n - d, 0), pltpu.roll(u, n - d, 0)
        else:
            keep = row >= d
            a_s, u_s = pltpu.roll(a, d, 0), pltpu.roll(u, d, 0)
        u = u + a * jnp.where(keep, u_s, 0.0)
        a = a * jnp.where(keep, a_s, 1.0)
        d *= 2
    return u


def _make_scan():
    @jax.custom_vjp
    def scan(a, u):
        return _lin_scan(a, u)

    def fwd(a, u):
        h = _lin_scan(a, u)
        return h, (a, h)

    def bwd(res, dh):
        a, h = res
        g = _lin_scan(_shift_up(a, 1), dh, reverse=True)
        return g * _shift_down(h, 1), g

    scan.defvjp(fwd, bwd)
    return scan


def _acc_out(ref, val):
    @pl.when(pl.program_id(0) == 0)
    def _():
        ref[...] = jnp.zeros_like(ref)

    ref[...] += val


FFN_CW = 128


def _ffn_fn(hg, hv, wg, wv, bg, bv, sd):
    cg = wg[0:1] * sd(hg, 2) + wg[1:2] * sd(hg, 1) + wg[2:3] * hg + bg
    cv = wv[0:1] * sd(hv, 2) + wv[1:2] * sd(hv, 1) + wv[2:3] * hv + bv
    return jax.nn.silu(cg) * cv


def _ffn_specs(t):
    nb = D_FF // FFN_CW
    col = lambda r, off: pl.BlockSpec((r, FFN_CW), lambda j: (0, j + off))
    return nb, [col(t, 0), col(t, nb), col(3, 0), col(3, nb), col(1, 0), col(1, nb)], col


def _ffn_mid_fwd(h, cw, cb, name):
    t = h.shape[0]
    nb, in_specs, col = _ffn_specs(t)

    def body(hg, hv, wg, wv, bg, bv, o_ref):
        o_ref[...] = _ffn_fn(hg[...], hv[...], wg[...], wv[...], bg[...], bv[...], _shift_down).astype(BF16)

    return _blocked(body, name=name, grid=(nb,), in_specs=in_specs, out_specs=col(t, 0),
                          out_shape=jax.ShapeDtypeStruct((t, D_FF), BF16),
                          compiler_params=_cparams(("parallel",), VMEM_MID))(h, h, cw, cw, cb, cb)


def _ffn_mid_bwd(h, cw, cb, dact, name):
    t = h.shape[0]
    nb, in_specs, col = _ffn_specs(t)

    def body(hg, hv, wg, wv, bg, bv, d_ref, dhg, dhv, dwg, dwv, dbg, dbv):
        fn = functools.partial(_ffn_fn, sd=_make_sd())
        _, vjp = jax.vjp(fn, hg[...], hv[...], wg[...], wv[...], bg[...], bv[...])
        g = vjp(d_ref[...])
        dhg[...] = g[0].astype(BF16)
        dhv[...] = g[1].astype(BF16)
        dwg[...], dwv[...], dbg[...], dbv[...] = g[2], g[3], g[4], g[5]

    big = jax.ShapeDtypeStruct((t, D_FF), BF16)
    w3 = jax.ShapeDtypeStruct((3, D_FF), F32)
    b1 = jax.ShapeDtypeStruct((1, D_FF), F32)
    return _blocked(body, name=name, grid=(nb,), in_specs=in_specs + [col(t, 0)],
                          out_specs=[col(t, 0), col(t, 0), col(3, 0), col(3, 0), col(1, 0), col(1, 0)],
                          out_shape=[big, big, w3, w3, b1, b1],
                          compiler_params=_cparams(("parallel",), VMEM_BIG))(h, h, cw, cw, cb, cb, dact)


TS_CW = 256


def _tshift_fn(p, mu, sd):
    return p + mu * (sd(p, 1) - p)


def _tshift_fwd(p, mu):
    t = p.shape[0]
    col = lambda r: pl.BlockSpec((r, TS_CW), lambda j: (0, j))

    def body(p_ref, mu_ref, o_ref):
        o_ref[...] = _tshift_fn(p_ref[...], mu_ref[...], _shift_down)

    return _blocked(body, name="tshift_fwd", grid=(SHIFT_COLS // TS_CW,), in_specs=[col(t), col(1)],
                          out_specs=col(t), out_shape=jax.ShapeDtypeStruct((t, SHIFT_COLS), F32),
                          compiler_params=_cparams(("parallel",), VMEM_MID))(p, mu)


def _tshift_bwd(p, mu, dpam):
    t = p.shape[0]
    col = lambda r: pl.BlockSpec((r, TS_CW), lambda j: (0, j))

    def body(p_ref, mu_ref, d_ref, dp_ref, dmu_ref):
        _, vjp = jax.vjp(functools.partial(_tshift_fn, sd=_make_sd()), p_ref[...], mu_ref[...])
        dp, dmu = vjp(d_ref[...])
        dp_ref[...] = dp.astype(BF16)
        dmu_ref[...] = dmu

    return _blocked(body, name="tshift_bwd", grid=(SHIFT_COLS // TS_CW,), in_specs=[col(t), col(1), col(t)],
                          out_specs=[col(t), col(1)],
                          out_shape=[jax.ShapeDtypeStruct((t, SHIFT_COLS), BF16),
                                     jax.ShapeDtypeStruct((1, SHIFT_COLS), F32)],
                          compiler_params=_cparams(("parallel",), VMEM_MID))(p, mu, dpam)


_HI = lax.Precision.HIGHEST
_O = (0, RW, 2 * RW, 3 * RW, 3 * RW + W_LORA, 3 * RW + W_LORA + A_LORA, SHIFT_COLS)


def _dot16(a, b, dims=(((1,), (0,)), ((), ()))):
    return lax.dot_general(a.astype(BF16), b.astype(BF16), dims, preferred_element_type=F32)


def _make_dot16():
    @jax.custom_vjp
    def dot(a, b):
        return _dot16(a, b)

    def fwd(a, b):
        return _dot16(a, b), (a, b)

    def bwd(res, g):
        a, b = res
        return _dot16(g, b, (((1,), (1,)), ((), ()))), _dot16(a, g, (((0,), (0,)), ((), ())))

    dot.defvjp(fwd, bwd)
    return dot


def _seg(x):
    first = lax.broadcasted_iota(jnp.int32, (x.shape[0], LANES), 1) < HEAD
    parts = []
    for p in range(x.shape[1] // LANES):
        xp = x[:, p * LANES:(p + 1) * LANES]
        s0 = jnp.sum(jnp.where(first, xp, 0.0), axis=-1, keepdims=True)
        s1 = jnp.sum(jnp.where(first, 0.0, xp), axis=-1, keepdims=True)
        parts.append(jnp.where(first, s0, s1))
    return jnp.concatenate(parts, axis=1)


def _prep_fn(r, k, v, wd, ad, gd, w0, w2, a0, a2, g2, k_k, k_a, dot):
    w_log = -jax.nn.softplus(-(w0 + dot(jnp.tanh(wd), w2))) - 0.5
    decay = jnp.exp(-jnp.exp(w_log))
    a = jax.nn.sigmoid(a0 + dot(ad, a2))
    g = dot(jax.nn.sigmoid(gd), g2)
    kk = k * k_k
    kk = kk / jnp.maximum(jnp.sqrt(_seg(kk * kk)), 1e-12)
    k2 = k * (1.0 + (a - 1.0) * k_a)
    return r, decay, k2, v, -kk, kk * a, g


_PREP_W = ("w0", "w2", "a0", "a2", "g2", "k_k", "k_a")


def _prep_wspecs(w):
    return [_full(w[n].shape) for n in _PREP_W]


def _rwkv_prep_fwd(pam, w):
    t = pam.shape[0]

    def body(p_ref, *refs):
        wr, outs = refs[:7], refs[7:]
        pieces = [p_ref[:, _O[i]:_O[i + 1]] for i in range(6)]
        res = _prep_fn(*pieces, *[x[...] for x in wr], _dot16)
        for o, val in zip(outs, res):
            o[...] = val

    row = lambda c: pl.BlockSpec((TOK, c), lambda i: (i, 0))
    return _blocked(body, name="rwkv_prep_fwd", grid=(t // TOK,),
                          in_specs=[row(SHIFT_COLS)] + _prep_wspecs(w), out_specs=[row(RW)] * 7,
                          out_shape=[jax.ShapeDtypeStruct((t, RW), F32)] * 7,
                          compiler_params=_cparams(("parallel",), VMEM_MID))(pam, *[w[n] for n in _PREP_W])


def _rwkv_prep_bwd(pam, w, cts, more):
    t = pam.shape[0]

    def body(p_ref, *refs):
        wr, ct, ex, dp_ref, dws = refs[:7], refs[7:14], refs[14:17], refs[17], refs[18:]
        pieces = [p_ref[:, _O[i]:_O[i + 1]] for i in range(6)]
        fn = lambda *a: _prep_fn(*a, _make_dot16())
        _, vjp = jax.vjp(fn, *pieces, *[x[...] for x in wr])
        c = [x[...] for x in ct]
        c[0] = c[0] + ex[0][...]
        c[2] = c[2] + ex[1][...]
        c[3] = c[3] + ex[2][...]
        g = vjp(tuple(c))
        for i in range(6):
            dp_ref[:, _O[i]:_O[i + 1]] = g[i]
        for o, val in zip(dws, g[6:]):
            _acc_out(o, val)

    row = lambda c: pl.BlockSpec((TOK, c), lambda i: (i, 0))
    return _blocked(body, name="rwkv_prep_bwd", grid=(t // TOK,),
                          in_specs=[row(SHIFT_COLS)] + _prep_wspecs(w) + [row(RW)] * 10,
                          out_specs=[row(SHIFT_COLS)] + [_full(w[n].shape) for n in _PREP_W],
                          out_shape=[jax.ShapeDtypeStruct((t, SHIFT_COLS), F32)]
                          + [jax.ShapeDtypeStruct(w[n].shape, F32) for n in _PREP_W],
                          compiler_params=_cparams(("arbitrary",), VMEM_MID))(
                              pam, *[w[n] for n in _PREP_W], *cts, *more)


def _post_fn(y, r, k2, v, g, ln_w, ln_b, r_k):
    inv = 1.0 / HEAD
    d = y - _seg(y) * inv
    yn = d * lax.rsqrt(_seg(d * d) * inv + GN_EPS) * ln_w + ln_b
    bonus = _seg(r * k2 * r_k) * v
    return (yn + bonus) * g


def _rwkv_post_fwd(y, r, k2, v, g, ln_w, ln_b, r_k):
    t = y.shape[0]

    def body(*refs):
        o_ref = refs[-1]
        o_ref[...] = _post_fn(*[x[...] for x in refs[:-1]]).astype(BF16)

    row = pl.BlockSpec((TOK, RW), lambda i: (i, 0))
    return _blocked(body, name="rwkv_post_fwd", grid=(t // TOK,),
                          in_specs=[row] * 5 + [_full((1, RW))] * 3, out_specs=row,
                          out_shape=jax.ShapeDtypeStruct((t, RW), BF16),
                          compiler_params=_cparams(("parallel",), VMEM_MID))(y, r, k2, v, g, ln_w, ln_b, r_k)


def _rwkv_post_bwd(y, r, k2, v, g, ln_w, ln_b, r_k, dya):
    t = y.shape[0]

    def body(*refs):
        ins, d_ref, outs = refs[:8], refs[8], refs[9:]
        _, vjp = jax.vjp(_post_fn, *[x[...] for x in ins])
        gr = vjp(d_ref[...])
        for o, val in zip(outs[:5], gr[:5]):
            o[...] = val
        for o, val in zip(outs[5:], gr[5:]):
            _acc_out(o, val)

    row = pl.BlockSpec((TOK, RW), lambda i: (i, 0))
    vec = _full((1, RW))
    return _blocked(body, name="rwkv_post_bwd", grid=(t // TOK,),
                          in_specs=[row] * 5 + [vec] * 3 + [row],
                          out_specs=[row] * 5 + [vec] * 3,
                          out_shape=[jax.ShapeDtypeStruct((t, RW), F32)] * 5 + [jax.ShapeDtypeStruct((1, RW), F32)] * 3,
                          compiler_params=_cparams(("arbitrary",), VMEM_MID))(y, r, k2, v, g, ln_w, ln_b, r_k, dya)


def _from_pt(x):
    n = x.shape[0]
    return x.reshape(n, HEAD, N_HEADS, PT).transpose(0, 3, 2, 1).reshape(n * PT, N_HEADS * HEAD)


def _lane_sum(x):
    return jnp.sum(x, axis=-1, keepdims=True)


def _pair_consts():
    lane = lax.broadcasted_iota(jnp.int32, (HEAD, LANES), 1)
    return lane, lane < HEAD


def _seg_sum_pair(x, first):
    return jnp.where(first, _lane_sum(jnp.where(first, x, 0.0)), _lane_sum(jnp.where(first, 0.0, x)))


def _to_pt(x):
    t = x.shape[0]
    return x.reshape(t // PT, PT, N_HEADS, HEAD).transpose(0, 3, 2, 1).reshape(t // PT, HEAD, N_HEADS * PT)


def _expand_cols(x, name):
    t = x.shape[0]
    tiles = WKV_CHUNK // PT

    def body(x_ref, o_ref):
        _, first = _pair_consts()
        for tl in range(tiles):
            tile = x_ref[tl]
            for j in range(PT):
                for p in range(N_HEADS // 2):
                    src = jnp.where(first, (2 * p) * PT + j, (2 * p + 1) * PT + j)
                    o_ref[tl * PT + j, :, p * LANES:(p + 1) * LANES] = jnp.take_along_axis(tile, src, axis=1)

    return _blocked(
        body, name=name, grid=(t // WKV_CHUNK,),
        in_specs=[pl.BlockSpec((tiles, HEAD, LANES), lambda i: (i, 0, 0))],
        out_specs=pl.BlockSpec((WKV_CHUNK, HEAD, RW), lambda i: (i, 0, 0)),
        out_shape=jax.ShapeDtypeStruct((t, HEAD, RW), F32),
        compiler_params=_cparams(("parallel",), VMEM_MID))(_to_pt(x))


def _wkv_fwd(w, k, z, b, v_exp):
    t = w.shape[0]
    nc = t // WKV_CHUNK
    pairs = N_HEADS // 2

    def body(w_ref, k_ref, z_ref, b_ref, v_ref, s_all, s_ref):
        @pl.when(pl.program_id(0) == 0)
        def _():
            s_ref[...] = jnp.zeros_like(s_ref)

        _, first = _pair_consts()

        def group(gi, carry):
            base = pl.multiple_of(gi * 8, 8)
            rows = [ref[pl.ds(base, 8), :] for ref in (w_ref, k_ref, z_ref, b_ref)]
            s = [s_ref[:, p * LANES:(p + 1) * LANES] for p in range(pairs)]
            for jj in range(8):
                for p in range(pairs):
                    cs = slice(p * LANES, (p + 1) * LANES)
                    wr, kr, zr, br = [x[jj:jj + 1, cs] for x in rows]
                    s_all[base + jj, :, cs] = s[p]
                    sa = _seg_sum_pair(s[p] * zr, first)
                    s[p] = s[p] * wr + sa * br + v_ref[base + jj, :, cs] * kr
            for p in range(pairs):
                s_ref[:, p * LANES:(p + 1) * LANES] = s[p]
            return carry

        lax.fori_loop(0, WKV_CHUNK // 8, group, 0)

    row = pl.BlockSpec((WKV_CHUNK, RW), lambda i: (i, 0))
    big = pl.BlockSpec((WKV_CHUNK, HEAD, RW), lambda i: (i, 0, 0))
    return _blocked(
        body, name="wkv_fwd", grid=(nc,), in_specs=[row] * 4 + [big], out_specs=[big, _full((HEAD, RW))],
        out_shape=[jax.ShapeDtypeStruct((t, HEAD, RW), F32), jax.ShapeDtypeStruct((HEAD, RW), F32)],
        compiler_params=_cparams(("arbitrary",), VMEM_MID))(w, k, z, b, v_exp)


def _wkv_out(r, s_all, s_last):
    t = r.shape[0]
    nc = t // WKV_CHUNK
    tiles = WKV_CHUNK // PT
    pairs = N_HEADS // 2

    def body(r_ref, s_ref, nxt_ref, last_ref, y_ref):
        lane, first = _pair_consts()
        after = jnp.where(pl.program_id(0) == nc - 1, last_ref[...], nxt_ref[0])
        for tl in range(tiles):
            ytile = jnp.zeros((HEAD, LANES), F32)
            for g in range(PT // 8):
                rows = r_ref[tl * PT + g * 8:tl * PT + g * 8 + 8, :]
                for jj in range(8):
                    tt = tl * PT + g * 8 + jj
                    j = g * 8 + jj
                    for p in range(pairs):
                        cs = slice(p * LANES, (p + 1) * LANES)
                        s = s_ref[tt + 1, :, cs] if tt + 1 < WKV_CHUNK else after[:, cs]
                        pr = s * rows[jj:jj + 1, cs]
                        y0 = _lane_sum(jnp.where(first, pr, 0.0))
                        y1 = _lane_sum(jnp.where(first, 0.0, pr))
                        ytile = jnp.where(lane == (2 * p) * PT + j, y0, ytile)
                        ytile = jnp.where(lane == (2 * p + 1) * PT + j, y1, ytile)
            y_ref[tl] = ytile

    row = pl.BlockSpec((WKV_CHUNK, RW), lambda i: (i, 0))
    pt = pl.BlockSpec((tiles, HEAD, LANES), lambda i: (i, 0, 0))
    big = pl.BlockSpec((WKV_CHUNK, HEAD, RW), lambda i: (i, 0, 0))
    nxt = pl.BlockSpec((1, HEAD, RW), lambda i: (jnp.minimum((i + 1) * WKV_CHUNK, t - 1), 0, 0))
    return _blocked(
        body, name="wkv_out", grid=(nc,), in_specs=[row, big, nxt, _full((HEAD, RW))], out_specs=pt,
        out_shape=jax.ShapeDtypeStruct((t // PT, HEAD, LANES), F32),
        compiler_params=_cparams(("parallel",), VMEM_MID))(r, s_all, s_all, s_last)


def _wkv_bwd(r, w, k, z, b, v_exp, s_all, dy_exp):
    t = r.shape[0]
    nc = t // WKV_CHUNK
    tiles = WKV_CHUNK // PT
    pairs = N_HEADS // 2

    def body(r_ref, w_ref, k_ref, z_ref, b_ref, v_ref, s_all_ref, dy_ref,
             dr_ref, dw_ref, dk_ref, dz_ref, db_ref, dv_ref, ds_ref):
        @pl.when(pl.program_id(0) == 0)
        def _():
            ds_ref[...] = jnp.zeros_like(ds_ref)

        lane, first = _pair_consts()
        col_sum = lambda x: jnp.sum(x, axis=0, keepdims=True)
        row8 = lax.broadcasted_iota(jnp.int32, (8, LANES), 0)
        for tl in reversed(range(tiles)):
            def group(gg, dvtile):
                gi = PT // 8 - 1 - gg
                base = pl.multiple_of(tl * PT + gi * 8, 8)
                rows = [ref[pl.ds(base, 8), :] for ref in (r_ref, w_ref, k_ref, z_ref, b_ref)]
                outs = (dr_ref, dw_ref, dk_ref, dz_ref, db_ref)
                tiles8 = {(id(o), p): jnp.zeros((8, LANES), F32) for o in outs for p in range(pairs)}
                ds = [ds_ref[:, p * LANES:(p + 1) * LANES] for p in range(pairs)]
                for jj in reversed(range(8)):
                    j = gi * 8 + jj
                    for p in range(pairs):
                        cs = slice(p * LANES, (p + 1) * LANES)

                        def put(ref, val, p=p, jj=jj):
                            tiles8[(id(ref), p)] = jnp.where(row8 == jj, val, tiles8[(id(ref), p)])

                        rr, wr, kr, zr, br = [x[jj:jj + 1, cs] for x in rows]
                        sp = s_all_ref[base + jj, :, cs]
                        vc = v_ref[base + jj, :, cs]
                        dyc = dy_ref[base + jj, :, cs]
                        sa = _seg_sum_pair(sp * zr, first)
                        st = sp * wr + sa * br + vc * kr
                        d = ds[p] + dyc * rr
                        put(dr_ref, col_sum(st * dyc))
                        dvk = d * kr
                        dv0 = _lane_sum(jnp.where(first, dvk, 0.0))
                        dv1 = _lane_sum(jnp.where(first, 0.0, dvk))
                        dvtile = jnp.where(lane == (2 * p) * PT + j, dv0, dvtile)
                        dvtile = jnp.where(lane == (2 * p + 1) * PT + j, dv1, dvtile)
                        put(dk_ref, col_sum(d * vc))
                        put(dw_ref, col_sum(sp * d))
                        u = _seg_sum_pair(d * br, first)
                        put(dz_ref, col_sum(sp * u))
                        put(db_ref, col_sum(d * sa))
                        ds[p] = d * wr + u * zr
                for p in range(pairs):
                    ds_ref[:, p * LANES:(p + 1) * LANES] = ds[p]
                for o in outs:
                    for p in range(pairs):
                        o[pl.ds(base, 8), p * LANES:(p + 1) * LANES] = tiles8[(id(o), p)]
                return dvtile

            dv_ref[tl] = lax.fori_loop(0, PT // 8, group, jnp.zeros((HEAD, LANES), F32))

    rev = lambda i: nc - 1 - i
    row = pl.BlockSpec((WKV_CHUNK, RW), lambda i: (rev(i), 0))
    pt = pl.BlockSpec((tiles, HEAD, LANES), lambda i: (rev(i), 0, 0))
    big = pl.BlockSpec((WKV_CHUNK, HEAD, RW), lambda i: (rev(i), 0, 0))
    return _blocked(
        body, name="wkv_bwd", grid=(nc,), in_specs=[row] * 5 + [big, big, big], out_specs=[row] * 5 + [pt],
        out_shape=[jax.ShapeDtypeStruct((t, RW), F32)] * 5 + [jax.ShapeDtypeStruct((t // PT, HEAD, LANES), F32)],
        scratch_shapes=[pltpu.VMEM((HEAD, RW), F32)],
        compiler_params=_cparams(("arbitrary",), VMEM_BIG))(r, w, k, z, b, v_exp, s_all, dy_exp)


LRU_CW = 128
_BX0 = SHIFT_COLS // LRU_CW
_BG0 = (SHIFT_COLS + LRU_W) // LRU_CW


def _lru_fn(bx, bg, cw, cb, ga, ba, gx, bxb, lam, sd, scan, dot):
    xc = cw[0:1] * sd(bx, 3) + cw[1:2] * sd(bx, 2) + cw[2:3] * sd(bx, 1) + cw[3:4] * bx + cb
    gr = jax.nn.sigmoid(dot(xc, ga) + ba)
    gi = jax.nn.sigmoid(dot(xc, gx) + bxb)
    log_a = -LRU_C * gr * jax.nn.softplus(-lam)
    a = jnp.exp(log_a)
    mult = jnp.sqrt(-jnp.tanh(log_a) * (jnp.exp(2.0 * log_a) + 1.0))
    return scan(a, xc * gi * mult) * jax.nn.gelu(bg)


def _lru_specs(t):
    col = lambda r, off=0: pl.BlockSpec((r, LRU_CW), lambda j: (0, j + off))
    diag = pl.BlockSpec((LRU_CW, LRU_CW), lambda j: (j, j))
    return col, [col(t, _BX0), col(t, _BG0), col(4), col(1), diag, col(1), diag, col(1), col(1)]


def _lru_fwd(p, cw, cb, ga, ba, gx, bxb, lam):
    t = p.shape[0]
    col, in_specs = _lru_specs(t)

    def body(*refs):
        o_ref = refs[-1]
        o_ref[...] = _lru_fn(*[x[...] for x in refs[:-1]], _shift_down, _lin_scan, _dot16).astype(BF16)

    return _blocked(body, name="lru_fwd", grid=(LRU_W // LRU_CW,), in_specs=in_specs, out_specs=col(t),
                          out_shape=jax.ShapeDtypeStruct((t, LRU_W), BF16),
                          compiler_params=_cparams(("parallel",), VMEM_MID))(p, p, cw, cb, ga, ba, gx, bxb, lam)


def _lru_bwd(p, cw, cb, ga, ba, gx, bxb, lam, dyb):
    t = p.shape[0]
    col, in_specs = _lru_specs(t)

    def body(*refs):
        ins, d_ref, outs = refs[:9], refs[9], refs[10:]
        fn = functools.partial(_lru_fn, sd=_make_sd(), scan=_make_scan(), dot=_make_dot16())
        _, vjp = jax.vjp(fn, *[x[...] for x in ins])
        g = vjp(d_ref[...])
        outs[0][...] = g[0].astype(BF16)
        outs[1][...] = g[1].astype(BF16)
        for o, val in zip(outs[2:], g[2:]):
            o[...] = val

    sq = pl.BlockSpec((LRU_CW, LRU_CW), lambda j: (j, 0))
    act = jax.ShapeDtypeStruct((t, LRU_W), BF16)
    vec = jax.ShapeDtypeStruct((1, LRU_W), F32)
    sqs = jax.ShapeDtypeStruct((LRU_W, LRU_CW), F32)
    return _blocked(body, name="lru_bwd", grid=(LRU_W // LRU_CW,), in_specs=in_specs + [col(t, RW // LRU_CW)],
                          out_specs=[col(t), col(t), col(4), col(1), sq, col(1), sq, col(1), col(1)],
                          out_shape=[act, act, jax.ShapeDtypeStruct((4, LRU_W), F32), vec, sqs, vec, sqs, vec, vec],
                          compiler_params=_cparams(("parallel",), VMEM_BIG))(p, p, cw, cb, ga, ba, gx, bxb, lam, dyb)


def _s5_disc_fn(a_re, a_im, log_dt, b_re, b_im, e):
    lam_re = jnp.minimum(a_re, -1e-4)
    lam_im = a_im
    dt = jnp.exp(log_dt)
    mag = jnp.exp(lam_re * dt)
    ab_re = mag * jnp.cos(lam_im * dt)
    ab_im = mag * jnp.sin(lam_im * dt)
    den = lam_re * lam_re + lam_im * lam_im
    zr = ab_re - 1.0
    q_re = jnp.dot((zr * lam_re + ab_im * lam_im) / den, e, precision=_HI)
    q_im = jnp.dot((ab_im * lam_re - zr * lam_im) / den, e, precision=_HI)
    return ab_re, ab_im, q_re * b_re - q_im * b_im, q_re * b_im + q_im * b_re


def _s5_disc_fwd(a_re, a_im, log_dt, b_re, b_im, e):
    def body(*refs):
        res = _s5_disc_fn(*[x[...] for x in refs[:6]])
        for o, val in zip(refs[6:], res):
            o[...] = val

    small = jax.ShapeDtypeStruct(a_re.shape, F32)
    wide = jax.ShapeDtypeStruct(b_re.shape, F32)
    return pl.pallas_call(body, name="s5_disc_fwd", out_shape=[small, small, wide, wide])(
        a_re, a_im, log_dt, b_re, b_im, e)


def _s5_disc_bwd(a_re, a_im, log_dt, b_re, b_im, e, cts):
    def body(*refs):
        ins, e_ref, ct, outs = refs[:5], refs[5], refs[6:10], refs[10:]
        _, vjp = jax.vjp(lambda *a: _s5_disc_fn(*a, e_ref[...]), *[x[...] for x in ins])
        for o, val in zip(outs, vjp(tuple(c[...] for c in ct))):
            o[...] = val

    shapes = [jax.ShapeDtypeStruct(x.shape, F32) for x in (a_re, a_im, log_dt, b_re, b_im)]
    return pl.pallas_call(body, name="s5_disc_bwd", out_shape=shapes)(a_re, a_im, log_dt, b_re, b_im, e, *cts)


def _cmul(a, b):
    return a[0] * b[0] - a[1] * b[1], a[0] * b[1] + a[1] * b[0]


def _s5_scan(sr, si, ab, reverse):
    n_tiles = sr.shape[0] // 8
    width = sr.shape[1]
    row8 = lax.broadcasted_iota(jnp.int32, (8, width), 0)
    p1 = ab
    p2 = _cmul(p1, p1)
    p4 = _cmul(p2, p2)
    pw = [p1]
    for _ in range(7):
        pw.append(_cmul(pw[-1], p1))
    cr = jnp.zeros((8, width), F32)
    ci = jnp.zeros((8, width), F32)
    for j in range(8):
        e = pw[7 - j] if reverse else pw[j]
        cr = jnp.where(row8 == j, e[0], cr)
        ci = jnp.where(row8 == j, e[1], ci)

    levels = []
    for d, q in ((1, p1), (2, p2), (4, p4)):
        keep = row8 < 8 - d if reverse else row8 >= d
        levels.append((d, (jnp.where(keep, q[0], 0.0), jnp.where(keep, q[1], 0.0))))

    def tile(i, carry):
        idx = n_tiles - 1 - i if reverse else i
        base = pl.multiple_of(idx * 8, 8)
        x = (sr[pl.ds(base, 8), :], si[pl.ds(base, 8), :])
        for d, q in levels:
            amt = 8 - d if reverse else d
            m = _cmul(q, (pltpu.roll(x[0], amt, 0), pltpu.roll(x[1], amt, 0)))
            x = (x[0] + m[0], x[1] + m[1])
        m = _cmul((cr, ci), carry)
        x = (x[0] + m[0], x[1] + m[1])
        sr[pl.ds(base, 8), :] = x[0]
        si[pl.ds(base, 8), :] = x[1]
        edge = slice(0, 1) if reverse else slice(7, 8)
        return x[0][edge], x[1][edge]

    zero = jnp.zeros((1, width), F32)
    lax.fori_loop(0, n_tiles, tile, (zero, zero))


_S5_W = S5_SLAB // S5_GROUP * S5_STATE


def _s5_specs(t):
    col = lambda r: pl.BlockSpec((r, S5_SLAB), lambda j: (0, j))
    bb = pl.BlockSpec((None, S5_SLAB, _S5_W), lambda j: (j, 0, 0))
    cd = pl.BlockSpec((None, _S5_W, S5_SLAB), lambda j: (j, 0, 0))
    ab = pl.BlockSpec((None, 1, _S5_W), lambda j: (j, 0, 0))
    return col, bb, cd, ab


def _s5_fwd(u, dvec, bbr, bbi, cdr, cdi, abr, abi):
    t, width = u.shape
    col, bb, cd, ab = _s5_specs(t)

    def body(u_ref, d_ref, bbr_ref, bbi_ref, cdr_ref, cdi_ref, abr_ref, abi_ref, o_ref, sr, si):
        uv = u_ref[...]
        sr[...] = _dot16(uv, bbr_ref[...])
        si[...] = _dot16(uv, bbi_ref[...])
        _s5_scan(sr, si, (abr_ref[...], abi_ref[...]), False)
        y = _dot16(sr[...], cdr_ref[...]) - _dot16(si[...], cdi_ref[...])
        o_ref[...] = jax.nn.gelu(y + d_ref[...] * uv).astype(BF16)

    return _blocked(body, name="s5_fwd", grid=(width // S5_SLAB,),
                          in_specs=[col(t), col(1), bb, bb, cd, cd, ab, ab], out_specs=col(t),
                          out_shape=jax.ShapeDtypeStruct((t, width), BF16),
                          scratch_shapes=[pltpu.VMEM((t, _S5_W), F32)] * 2,
                          compiler_params=_cparams(("parallel",), VMEM_BIG))(u, dvec, bbr, bbi, cdr, cdi, abr, abi)


def _s5_bwd(u, dvec, bbr, bbi, cdr, cdi, abr, abi, dyact):
    t, width = u.shape
    col, bb, cd, ab = _s5_specs(t)
    ns = width // S5_SLAB
    tn = (((0,), (0,)), ((), ()))
    nt = (((1,), (1,)), ((), ()))

    def body(u_ref, d_ref, bbr_ref, bbi_ref, cdr_ref, cdi_ref, abr_ref, abi_ref, dy_ref,
             du_ref, dd_ref, dbbr_ref, dbbi_ref, dcdr_ref, dcdi_ref, dabr_ref, dabi_ref, sr, si, gr, gi):
        uv = u_ref[...]
        dv = d_ref[...]
        abv = (abr_ref[...], abi_ref[...])
        sr[...] = _dot16(uv, bbr_ref[...])
        si[...] = _dot16(uv, bbi_ref[...])
        _s5_scan(sr, si, abv, False)
        y = _dot16(sr[...], cdr_ref[...]) - _dot16(si[...], cdi_ref[...])
        _, vjp = jax.vjp(jax.nn.gelu, y + dv * uv)
        (dpre,) = vjp(dy_ref[...].astype(F32))
        dd_ref[...] = jnp.sum(dpre * uv, axis=0, keepdims=True)
        dcdr_ref[...] = _dot16(sr[...], dpre, tn)
        dcdi_ref[...] = -_dot16(si[...], dpre, tn)
        gr[...] = _dot16(dpre, cdr_ref[...], nt)
        gi[...] = -_dot16(dpre, cdi_ref[...], nt)
        _s5_scan(gr, gi, (abv[0], -abv[1]), True)

        row8 = lax.broadcasted_iota(jnp.int32, (8, _S5_W), 0)

        def tile(i, carry):
            acc_r, acc_i, last_r, last_i = carry
            base = pl.multiple_of(i * 8, 8)
            s_r, s_i = sr[pl.ds(base, 8), :], si[pl.ds(base, 8), :]
            g_r, g_i = gr[pl.ds(base, 8), :], gi[pl.ds(base, 8), :]
            p_r = jnp.where(row8 == 0, last_r, pltpu.roll(s_r, 1, 0))
            p_i = jnp.where(row8 == 0, last_i, pltpu.roll(s_i, 1, 0))
            acc_r = acc_r + jnp.sum(g_r * p_r + g_i * p_i, axis=0, keepdims=True)
            acc_i = acc_i + jnp.sum(g_i * p_r - g_r * p_i, axis=0, keepdims=True)
            return acc_r, acc_i, s_r[7:8], s_i[7:8]

        zero = jnp.zeros((1, _S5_W), F32)
        acc_r, acc_i, _, _ = lax.fori_loop(0, t // 8, tile, (zero, zero, zero, zero))
        dabr_ref[...] = acc_r
        dabi_ref[...] = acc_i
        du_ref[...] = dpre * dv + _dot16(gr[...], bbr_ref[...], nt) + _dot16(gi[...], bbi_ref[...], nt)
        dbbr_ref[...] = _dot16(uv, gr[...], tn)
        dbbi_ref[...] = _dot16(uv, gi[...], tn)

    sds = jax.ShapeDtypeStruct
    return _blocked(
        body, name="s5_bwd", grid=(ns,), in_specs=[col(t), col(1), bb, bb, cd, cd, ab, ab, col(t)],
        out_specs=[col(t), col(1), bb, bb, cd, cd, ab, ab],
        out_shape=[sds((t, width), F32), sds((1, width), F32), sds((ns, S5_SLAB, _S5_W), F32),
                   sds((ns, S5_SLAB, _S5_W), F32), sds((ns, _S5_W, S5_SLAB), F32), sds((ns, _S5_W, S5_SLAB), F32),
                   sds((ns, 1, _S5_W), F32), sds((ns, 1, _S5_W), F32)],
        scratch_shapes=[pltpu.VMEM((t, _S5_W), F32)] * 4,
        compiler_params=_cparams(("parallel",), VMEM_BIG))(u, dvec, bbr, bbi, cdr, cdi, abr, abi, dyact)


def _gate_dense(w):
    h = w.shape[0]
    return jnp.einsum("hij,hg->higj", w, jnp.eye(h, dtype=F32)).reshape(h * HEAD, h * HEAD)


def _gate_blocks(d):
    x = d.reshape(LRU_W // LRU_CW, 2, HEAD, 2, HEAD)
    return jnp.einsum("tgihj,gh->tgij", x, jnp.eye(2, dtype=F32)).reshape(LRU_W // HEAD, HEAD, HEAD)


_GPS = S5_SLAB // S5_GROUP
_NS = S5_GROUPS // _GPS


def _s5_in_dense(bb):
    x = bb.reshape(_NS, _GPS, S5_STATE, S5_GROUP)
    return jnp.einsum("sgnc,gh->sgchn", x, jnp.eye(_GPS, dtype=F32)).reshape(_NS, S5_SLAB, _S5_W)


def _s5_in_blocks(d):
    x = d.reshape(_NS, _GPS, S5_GROUP, _GPS, S5_STATE)
    return jnp.einsum("sgchn,gh->sgnc", x, jnp.eye(_GPS, dtype=F32)).reshape(S5_GROUPS, S5_STATE * S5_GROUP)


def _s5_out_dense(c):
    x = c.reshape(_NS, _GPS, S5_GROUP, S5_STATE)
    return jnp.einsum("sgcn,gh->shngc", x, jnp.eye(_GPS, dtype=F32)).reshape(_NS, _S5_W, S5_SLAB)


def _s5_out_blocks(d):
    x = d.reshape(_NS, _GPS, S5_STATE, _GPS, S5_GROUP)
    return jnp.einsum("shngc,gh->sgcn", x, jnp.eye(_GPS, dtype=F32)).reshape(S5_GROUPS, S5_GROUP, S5_STATE)


def _local_step(x, tgt, w, late_weights, send_grads):
    d_model = x.shape[1]
    gs = {}
    n_layers = w["f_norm_g"].shape[0]

    def ffn_fwd(xin, l):
        xn = _rms_fwd(xin, w["f_norm_g"][l:l + 1], f"rms_f{l}")
        h = _matmul(xn, w["f_w_up_t"][l], "nt", f"mm_f{l}_up")
        act = _ffn_mid_fwd(h, w["f_conv_w"][l], w["f_conv_b"][l:l + 1], f"ffn_mid_fwd{l}")
        return _matmul(act, w["f_w_down"][l], "nn", f"mm_f{l}_down", add=xin), (xin, xn, h, act)

    def ffn_bwd(g, saved, l):
        xin, xn, h, act = saved
        dact = _matmul(g, w["f_w_down"][l], "nt", f"mm_f{l}_dact")
        d_down = _matmul(act, g, "tn", f"mm_f{l}_ddown", out_dtype=BF16)
        dhg, dhv, dwg, dwv, dbg, dbv = _ffn_mid_bwd(h, w["f_conv_w"][l], w["f_conv_b"][l:l + 1], dact,
                                                    f"ffn_mid_bwd{l}")
        dxn = _matmul((dhg, dhv), w["f_w_up_t"][l], "nn", f"mm_f{l}_dxn")
        d_up = _matmul((dhg, dhv), xn, "tn", f"mm_f{l}_dup", out_dtype=BF16)
        dx, dgn = _rms_bwd(xin, w["f_norm_g"][l:l + 1], dxn, g, f"rms_f{l}_bwd")
        return dx, d_up, d_down, jnp.concatenate([dwg, dwv], axis=1), jnp.concatenate([dbg, dbv], axis=1), dgn

    xn0 = _rms_fwd(x, w["e_norm_g"], "rms_e")
    p = _matmul(xn0, w["e_w_in_t"], "nt", "mm_e_in")
    pam = _tshift_fwd(p, w["e_mu"])
    pw = dict(w0=w["e_w0"], w2=w["e_w2"][0], a0=w["e_a0"], a2=w["e_a2"][0], g2=w["e_g2"][0],
              k_k=w["e_k_k"], k_a=w["e_k_a"])
    r, dec, k2, v, z, b, gate = _rwkv_prep_fwd(pam, pw)
    v_exp = _expand_cols(v, "wkv_expand_v")
    s_all, s_last = _wkv_fwd(dec, k2, z, b, v_exp)
    y_pt = _wkv_out(r, s_all, s_last)
    y = _from_pt(y_pt)
    rk = w["e_r_k"].reshape(1, RW)
    ya = _rwkv_post_fwd(y, r, k2, v, gate, w["e_ln_w"], w["e_ln_b"], rk)
    ga, gx = _gate_dense(w["e_gate_a_w"][0]), _gate_dense(w["e_gate_x_w"][0])
    lru_w = (w["e_conv_w"][0], w["e_conv_b"], ga, w["e_gate_a_b"], gx, w["e_gate_x_b"], w["e_lru_lambda"])
    yb = _lru_fwd(p, *lru_w)
    ycat = jnp.concatenate([ya, yb], axis=1)
    w = {**w, **late_weights(ycat)}
    x1 = _matmul(ycat, w["e_w_out"], "nn", "mm_e_out", add=x)
    x2, ffn0 = ffn_fwd(x1, 0)

    xn1 = _rms_fwd(x2, w["o_norm_g"], "rms_o")
    u = _matmul(xn1, w["o_w_in"], "nn", "mm_o_in")
    expand = jnp.kron(jnp.eye(S5_STATE, dtype=F32), jnp.ones((1, S5_GROUP), F32))
    disc_in = (w["o_A_re"][0], w["o_A_im"][0], w["o_log_dt"].reshape(S5_GROUPS, 1),
               w["o_B_re"][0].reshape(S5_GROUPS, -1), w["o_B_im"][0].reshape(S5_GROUPS, -1), expand)
    ab_re, ab_im, bb_re, bb_im = _s5_disc_fwd(*disc_in)
    s5_w = (w["o_D"], _s5_in_dense(bb_re), _s5_in_dense(bb_im), _s5_out_dense(w["o_C_re"][0]),
            _s5_out_dense(w["o_C_im"][0]), ab_re.reshape(_NS, 1, _S5_W), ab_im.reshape(_NS, 1, _S5_W))
    yact = _s5_fwd(u, *s5_w)
    zz = _matmul(yact, w["o_w_glu_t"], "nt", "mm_o_glu")
    x3 = _glu_fwd(x2, zz)
    x4, ffn1 = ffn_fwd(x3, 1)

    loss, g, gs["final_norm_g", 0] = _loss_head(x4, w["final_norm_g"].reshape(1, d_model), tgt)

    g, up1, down1, dcw1, dcb1, dfn1 = ffn_bwd(g, ffn1, 1)
    dz = _glu_bwd(zz, g)
    dyact = _matmul(dz, w["o_w_glu_t"], "nn", "mm_o_dyact")
    d_glu = _matmul(dz, yact, "tn", "mm_o_dglu", out_dtype=BF16)
    du, gs["o_D", 0], dbbr, dbbi, dcdr, dcdi, dabr, dabi = _s5_bwd(u, *s5_w, dyact)
    gs["o_C_re", 0] = _s5_out_blocks(dcdr).reshape(S5_GROUPS * S5_GROUP, S5_STATE)
    gs["o_C_im", 0] = _s5_out_blocks(dcdi).reshape(S5_GROUPS * S5_GROUP, S5_STATE)
    cts = (dabr.reshape(S5_GROUPS, S5_STATE), dabi.reshape(S5_GROUPS, S5_STATE), _s5_in_blocks(dbbr),
           _s5_in_blocks(dbbi))
    gs["o_A_re", 0], gs["o_A_im", 0], dlog_dt, gs["o_B_re", 0], gs["o_B_im", 0] = _s5_disc_bwd(*disc_in, cts)
    gs["o_log_dt", 0] = dlog_dt.reshape(1, S5_GROUPS)
    dxn = _matmul(du, w["o_w_in"], "nt", "mm_o_dxn")
    d_oin = _matmul(xn1, du, "tn", "mm_o_din", out_dtype=BF16)
    g, gs["o_norm_g", 0] = _rms_bwd(x2, w["o_norm_g"], dxn, g, "rms_o_bwd")
    g = send_grads("a", [("f_w_up", 1, up1), ("f_w_down", 1, down1), ("o_w_glu", 0, d_glu), ("o_w_in", 0, d_oin)], g)

    g, up0, down0, dcw0, dcb0, dfn0 = ffn_bwd(g, ffn0, 0)
    gs["f_conv_w", 0], gs["f_conv_w", 3] = dcw0, dcw1
    gs["f_conv_b", 0], gs["f_conv_b", 1] = dcb0, dcb1
    gs["f_norm_g", 0], gs["f_norm_g", 1] = dfn0, dfn1

    dycat = _matmul(g, w["e_w_out"], "nt", "mm_e_dycat")
    d_eout = _matmul(ycat, g, "tn", "mm_e_dout", out_dtype=BF16)
    dycat = send_grads("b", [("f_w_up", 0, up0), ("f_w_down", 0, down0), ("e_w_out", 0, d_eout)], dycat)
    dy, dr1, dk1, dv1, dgate, gs["e_ln_w", 0], gs["e_ln_b", 0], gs["e_r_k", 0] = _rwkv_post_bwd(
        y, r, k2, v, gate, w["e_ln_w"], w["e_ln_b"], rk, dycat)
    dr2, ddec, dk2, dzz, dbb, dv_pt = _wkv_bwd(r, dec, k2, z, b, v_exp, s_all, _expand_cols(dy, "wkv_expand_dy"))
    (dpam, gs["e_w0", 0], gs["e_w2", 0], gs["e_a0", 0], gs["e_a2", 0], gs["e_g2", 0], gs["e_k_k", 0],
     gs["e_k_a", 0]) = _rwkv_prep_bwd(pam, pw, (dr2, ddec, dk2, _from_pt(dv_pt), dzz, dbb, dgate), (dr1, dk1, dv1))
    dpa, gs["e_mu", 0] = _tshift_bwd(p, w["e_mu"], dpam)
    (dbx, dbg, gs["e_conv_w", 0], gs["e_conv_b", 0], dga, gs["e_gate_a_b", 0], dgx, gs["e_gate_x_b", 0],
     gs["e_lru_lambda", 0]) = _lru_bwd(p, *lru_w, dycat)
    gs["e_gate_a_w", 0] = _gate_blocks(dga).reshape(LRU_W, HEAD)
    gs["e_gate_x_w", 0] = _gate_blocks(dgx).reshape(LRU_W, HEAD)
    dp = jnp.concatenate([dpa, dbx, dbg], axis=1)
    d_ein = _matmul(dp, xn0, "tn", "mm_e_din", out_dtype=BF16)
    dp = send_grads("c", [("e_w_in", 0, d_ein)], dp)
    dxn = _matmul(dp, w["e_w_in_t"], "nn", "mm_e_dxn")
    grad_x, gs["e_norm_g", 0] = _rms_bwd(x, w["e_norm_g"], dxn, g, "rms_e_bwd")
    return loss, grad_x, gs


CAST_ROWS = 256


def _cast_shard(w3, layer, transpose, chip, name):
    _, rows, cols = w3.shape
    tr = _tile(rows, (CAST_ROWS, 176, 128))

    def body(c_ref, w_ref, o_ref):
        v = w_ref[...]
        o_ref[...] = (v.T if transpose else v).astype(BF16)

    in_spec = pl.BlockSpec((None, tr, cols), lambda i, c: (layer, i, 0))
    if transpose:
        out_spec, shape = pl.BlockSpec((None, cols, tr), lambda i, c: (c[0], 0, i)), (cols, rows)
    else:
        out_spec, shape = pl.BlockSpec((None, tr, cols), lambda i, c: (c[0], i, 0)), (rows, cols)
    grid_spec = pltpu.PrefetchScalarGridSpec(num_scalar_prefetch=1, grid=(rows // tr,), in_specs=[in_spec],
                                             out_specs=out_spec)
    return _blocked(body, name=name, grid_spec=grid_spec,
                          out_shape=jax.ShapeDtypeStruct((N_CHIPS,) + shape, BF16),
                          compiler_params=_cparams(("parallel",), VMEM_MID))(chip, w3)


_ANY = pl.BlockSpec(memory_space=pl.ANY)


def _coords():
    return lax.axis_index("x"), lax.axis_index("y"), lax.axis_index("c")


def _flip(v, d):
    return 1 - v if d else v


_CHIP_RELS = ((1, 0), (0, 1), (1, 1))
_DEV_RELS = tuple((dx, dy, dc) for dx in (0, 1) for dy in (0, 1) for dc in (0, 1))[1:]


_HBM = pl.BlockSpec(memory_space=pltpu.HBM)
_SEM = pl.BlockSpec(memory_space=pltpu.SEMAPHORE)
_EFFECT = pltpu.SideEffectType.DATAFLOW_SIDE_EFFECTING


def _in_hbm(a):
    return pltpu.with_memory_space_constraint(a, pltpu.HBM)


def _gather_copies(bufs, send, recv, landed):
    x, y, c = _coords()
    me = 2 * x + y
    res = []
    for i, buf in enumerate(bufs):
        for j, (dx, dy) in enumerate(_CHIP_RELS):
            px, py = _flip(x, dx), _flip(y, dy)
            k = i * len(_CHIP_RELS) + j
            res.append(pltpu.make_async_remote_copy(
                src_ref=buf.at[me], dst_ref=buf.at[2 * px + py if landed else me], send_sem=send.at[k],
                recv_sem=recv.at[k], device_id=(px, py, c), device_id_type=MESH))
    return res


def _scatter_copies(srcs, lands, send, recv, landed):
    x, y, c = _coords()
    me = 4 * x + 2 * y + c
    res = []
    for i, (src, land) in enumerate(zip(srcs, lands)):
        for j, (dx, dy, dc) in enumerate(_DEV_RELS):
            peer = (_flip(x, dx), _flip(y, dy), _flip(c, dc))
            pid = 4 * peer[0] + 2 * peer[1] + peer[2]
            k = i * len(_DEV_RELS) + j
            res.append(pltpu.make_async_remote_copy(
                src_ref=src.at[pid], dst_ref=land.at[pid if landed else me], send_sem=send.at[k],
                recv_sem=recv.at[k], device_id=peer, device_id_type=MESH))
    return res


def _split_start(bufs, n_src, copies, n_rel, name, after):
    n = len(bufs)
    nk = n_src * n_rel

    def body(*refs):
        ins, send, recv, token = refs[:n], refs[n + 1 + n], refs[n + 2 + n], refs[-1]
        for cp in copies(ins, send, recv, False):
            cp.start()
        token[...] = jnp.zeros_like(token)

    res = pl.pallas_call(
        body, name=name, in_specs=[_HBM] * n + [_ANY],
        out_specs=[_HBM] * n + [_SEM, _SEM, pl.BlockSpec(memory_space=pltpu.VMEM)],
        out_shape=[pltpu.HBM(b.shape, b.dtype) for b in bufs]
        + [pltpu.SemaphoreType.DMA((nk,)), pltpu.SemaphoreType.DMA((nk,)), jax.ShapeDtypeStruct((8, LANES), F32)],
        input_output_aliases={i: i for i in range(n)},
        compiler_params=pltpu.CompilerParams(has_side_effects=_EFFECT))(*[_in_hbm(b) for b in bufs], after)
    return res[n], res[n + 1], list(res[:n]), res[n + 2]


def _split_wait(bufs, send, recv, copies, name, after):
    n = len(bufs)

    def body(*refs):
        ins, send_ref, recv_ref = refs[:n], refs[n], refs[n + 1]
        for cp in copies(ins, send_ref, recv_ref, True):
            cp.wait_send()
            cp.wait_recv()

    return pl.pallas_call(
        body, name=name, in_specs=[_HBM] * n + [_SEM, _SEM, _ANY], out_specs=[_HBM] * n,
        out_shape=[pltpu.HBM(b.shape, b.dtype) for b in bufs], input_output_aliases={i: i for i in range(n)},
        compiler_params=pltpu.CompilerParams(has_side_effects=_EFFECT))(*bufs, send, recv, after)


def _gather_start(bufs, name, after):
    return _split_start(bufs, len(bufs), _gather_copies, len(_CHIP_RELS), name, after)


def _gather_wait(bufs, send, recv, name, after):
    return _split_wait(bufs, send, recv, _gather_copies, name, after)


def _scatter_start(srcs, name, after):
    n = len(srcs)
    lands = [lax.empty(a.shape, a.dtype) for a in srcs]
    fn = lambda refs, send, recv, landed: _scatter_copies(refs[:n], refs[n:], send, recv, landed)
    send, recv, bufs, token = _split_start(list(srcs) + lands, n, fn, len(_DEV_RELS), name, after)
    return send, recv, bufs, token


def _scatter_wait(bufs, send, recv, name, after):
    n = len(bufs) // 2
    fn = lambda refs, s, r, landed: _scatter_copies(refs[:n], refs[n:], s, r, landed)
    res = _split_wait(bufs, send, recv, fn, name, after)
    return res[:n], res[n:]


def _sum_segments(src, land, me, name):
    nd, seg, cols = src.shape
    ts = _tile(seg, (256, 176, 128))

    def body(m_ref, *refs):
        o_ref = refs[-1]
        acc = refs[0][...].astype(F32)
        for r in refs[1:-1]:
            acc = acc + r[...].astype(F32)
        o_ref[...] = acc

    def peer(rel):
        bits = 4 * rel[0] + 2 * rel[1] + rel[2]
        return pl.BlockSpec((None, ts, cols), lambda i, m: (jnp.bitwise_xor(m[0], bits), i, 0))

    grid_spec = pltpu.PrefetchScalarGridSpec(
        num_scalar_prefetch=1, grid=(seg // ts,),
        in_specs=[pl.BlockSpec((None, ts, cols), lambda i, m: (m[0], i, 0))] + [peer(r) for r in _DEV_RELS],
        out_specs=pl.BlockSpec((None, ts, cols), lambda i, m: (m[1], i, 0)))
    return _blocked(body, name=name, grid_spec=grid_spec,
                          out_shape=jax.ShapeDtypeStruct((2, seg, cols), F32),
                          compiler_params=_cparams(("parallel",), VMEM_MID))(me, src, *[land] * len(_DEV_RELS))


def _exchange_sibling(arrs):
    n = len(arrs)

    def body(*refs):
        outs, (send, recv) = refs[n:2 * n], refs[2 * n:]
        x, y, c = _coords()
        sib = (x, y, 1 - c)
        sends, recvs = [], []
        for i in range(n):
            cp = pltpu.make_async_remote_copy(src_ref=outs[i].at[c], dst_ref=outs[i].at[c], send_sem=send.at[i],
                                              recv_sem=recv.at[i], device_id=sib, device_id_type=MESH)
            cp.start()
            sends.append(cp)
            recvs.append(pltpu.make_async_remote_copy(src_ref=outs[i].at[c], dst_ref=outs[i].at[1 - c],
                                                      send_sem=send.at[i], recv_sem=recv.at[i], device_id=sib,
                                                      device_id_type=MESH))
        for cp in recvs:
            cp.wait_recv()
        for cp in sends:
            cp.wait_send()

    return pl.pallas_call(
        body, name="exchange_sibling", in_specs=[_ANY] * n, out_specs=[_ANY] * n,
        out_shape=[jax.ShapeDtypeStruct(a.shape, a.dtype) for a in arrs],
        input_output_aliases={i: i for i in range(n)},
        scratch_shapes=[pltpu.SemaphoreType.DMA((n,)), pltpu.SemaphoreType.DMA((n,))])(*arrs)


def _allreduce_small(vec):
    nd, rows, lanes = vec.shape
    nr = len(_DEV_RELS)

    def body(in_ref, out_ref, stage, red, send, recv):
        x, y, c = _coords()
        me = 4 * x + 2 * y + c
        peers = []
        for dx, dy, dc in _DEV_RELS:
            peer = (_flip(x, dx), _flip(y, dy), _flip(c, dc))
            peers.append((peer, 4 * peer[0] + 2 * peer[1] + peer[2]))

        def copy(src, dst, k, peer):
            return pltpu.make_async_remote_copy(src_ref=src, dst_ref=dst, send_sem=send.at[k], recv_sem=recv.at[k],
                                                device_id=peer, device_id_type=MESH)

        first = [copy(in_ref.at[pid], stage.at[me], j, peer) for j, (peer, pid) in enumerate(peers)]
        for cp in first:
            cp.start()
        stage[me] = in_ref[me]
        for j, (peer, pid) in enumerate(peers):
            copy(in_ref.at[pid], stage.at[pid], j, peer).wait_recv()
        acc = stage[0]
        for d in range(1, nd):
            acc = acc + stage[d]
        red[...] = acc
        out_ref[me] = acc
        second = [copy(red, out_ref.at[me], nr + j, peer) for j, (peer, pid) in enumerate(peers)]
        for cp in second:
            cp.start()
        for j, (peer, pid) in enumerate(peers):
            copy(red, out_ref.at[pid], nr + j, peer).wait_recv()
        for cp in first + second:
            cp.wait_send()

    vm = pl.BlockSpec(memory_space=pltpu.VMEM)
    return pl.pallas_call(
        body, name="allreduce_small", in_specs=[vm], out_specs=vm,
        out_shape=jax.ShapeDtypeStruct(vec.shape, F32),
        scratch_shapes=[pltpu.VMEM(vec.shape, F32), pltpu.VMEM((rows, lanes), F32),
                        pltpu.SemaphoreType.DMA((2 * nr,)), pltpu.SemaphoreType.DMA((2 * nr,))],
        compiler_params=_cparams(None, VMEM_MID))(vec)


def _adam_math(w, g, m, v):
    m2 = ADAM_B1 * m + (1.0 - ADAM_B1) * g
    v2 = ADAM_B2 * v + (1.0 - ADAM_B2) * (g * g)
    m_hat = m2 / (1.0 - ADAM_B1 ** ADAM_STEP)
    v_hat = v2 / (1.0 - ADAM_B2 ** ADAM_STEP)
    return -ADAM_LR * (m_hat / (jnp.sqrt(v_hat) + ADAM_EPS) + ADAM_WD * w), m2, v2


def _adamw_big(w3, m3, v3, layer, g, transposed, name, prev=None):
    nl, rows, cols = w3.shape
    tr = 128 if transposed else _tile(rows, (256, 176, 128))

    def body(w_ref, m_ref, v_ref, g_ref, *rest):
        go_ref, d_ref, mo_ref, vo_ref = rest[-4:]
        g_val = g_ref[...].T if transposed else g_ref[...]
        go_ref[...] = g_val
        d_ref[...], mo_ref[...], vo_ref[...] = _adam_math(w_ref[...], g_val, m_ref[...], v_ref[...])

    wspec = pl.BlockSpec((None, tr, cols), lambda i: (layer, i, 0))
    gspec = pl.BlockSpec((cols, tr), lambda i: (0, i)) if transposed else pl.BlockSpec((tr, cols), lambda i: (i, 0))
    extra = [] if prev is None else list(prev)
    return _blocked(body, name=name, grid=(rows // tr,),
                          in_specs=[wspec, wspec, wspec, gspec] + [_ANY] * len(extra),
                          out_specs=[wspec] * 4, out_shape=[jax.ShapeDtypeStruct((nl, rows, cols), F32)] * 4,
                          input_output_aliases={4 + i: i for i in range(len(extra))},
                          compiler_params=_cparams(("parallel",), VMEM_MID))(w3, m3, v3, g, *extra)


_SMALL = (
    ("e_norm_g", (1, D_MODEL), None), ("e_mu", (1, SHIFT_COLS), None), ("e_w0", (1, RW), None),
    ("e_w2", (W_LORA, RW), RW // 4), ("e_a0", (1, RW), None), ("e_a2", (A_LORA, RW), RW // 4),
    ("e_g2", (G_LORA, RW), RW // 4), ("e_k_k", (1, RW), None), ("e_k_a", (1, RW), None), ("e_r_k", (1, RW), None),
    ("e_ln_w", (1, RW), None), ("e_ln_b", (1, RW), None), ("e_conv_w", (4, LRU_W), LRU_W // 4),
    ("e_conv_b", (1, LRU_W), None), ("e_gate_a_w", (LRU_W, HEAD), None), ("e_gate_a_b", (1, LRU_W), None),
    ("e_gate_x_w", (LRU_W, HEAD), None), ("e_gate_x_b", (1, LRU_W), None), ("e_lru_lambda", (1, LRU_W), None),
    ("o_norm_g", (1, D_MODEL), D_MODEL // 4), ("o_A_re", (S5_GROUPS, S5_STATE), None),
    ("o_A_im", (S5_GROUPS, S5_STATE), None), ("o_log_dt", (1, S5_GROUPS), None),
    ("o_B_re", (S5_GROUPS, S5_STATE * S5_GROUP), None), ("o_B_im", (S5_GROUPS, S5_STATE * S5_GROUP), None),
    ("o_C_re", (S5_GROUPS * S5_GROUP, S5_STATE), None), ("o_C_im", (S5_GROUPS * S5_GROUP, S5_STATE), None),
    ("o_D", (1, D_MODEL), D_MODEL // 4), ("f_norm_g", (2, D_MODEL), None),
    ("f_conv_w", (6, 2 * D_FF), 2 * D_FF // 4), ("f_conv_b", (2, 2 * D_FF), None),
    ("final_norm_g", (1, D_MODEL), None))
_PIECES = {"f_norm_g": ((0, 1), (1, 1)), "f_conv_b": ((0, 1), (1, 1)), "f_conv_w": ((0, 3), (3, 3))}


def _ceil_to(n, m):
    return -(-n // m) * m


def _small_layout():
    groups = {}
    for name, (rows, cols), _ in _SMALL:
        for first, r in _PIECES.get(name, ((0, rows),)):
            groups.setdefault(cols, []).append((name, first, r))
    layout, off = {}, 0
    for cols, items in groups.items():
        stacks = [0, 0] if 2 * cols <= LANES else [0]
        placed = []
        for name, first, r in sorted(items, key=lambda it: -it[2]):
            half = stacks.index(min(stacks))
            r0 = stacks[half]
            if r >= 8 or r0 % 8 + r > 8:
                r0 = _ceil_to(r0, 8)
            placed.append((name, first, r, r0, half * (LANES // 2)))
            stacks[half] = r0 + r
        rpad = _ceil_to(max(stacks), 8)
        for name, first, r, at, lane in placed:
            layout[name, first] = (off, rpad, at, r, cols, lane)
        off += -(-cols // LANES) * rpad
    return layout, _ceil_to(off, 8 * N_DEV)


def _small_pack(gs):
    layout, total = _small_layout()
    keys = list(layout)

    def body(*refs):
        out = refs[-1]
        out[...] = jnp.zeros_like(out)
        for key, g_ref in zip(keys, refs[:-1]):
            off, rpad, at, r, cols, lane = layout[key]
            for j in range(-(-cols // LANES)):
                cw = min(LANES, cols - j * LANES)
                out[off + j * rpad + at:off + j * rpad + at + r, lane:lane + cw] = g_ref[:, j * LANES:j * LANES + cw]

    return pl.pallas_call(body, name="small_pack", out_shape=jax.ShapeDtypeStruct((total, LANES), F32),
                          compiler_params=_cparams(None, VMEM_MID))(*[gs[k] for k in keys])


def _adamw_small(red, chip, wts, ms, vs):
    layout, _ = _small_layout()
    names = [n for n, _, _ in _SMALL]
    n = len(names)

    def body(chip_ref, red_ref, *refs):
        ins, outs = refs[:3 * n], refs[3 * n:]
        c = chip_ref[0]
        for i, (name, (rows, cols), loc) in enumerate(_SMALL):
            w_ref, m_ref, v_ref = ins[3 * i:3 * i + 3]
            o_refs = outs[4 * i:4 * i + 4]
            width = cols if loc is None else loc
            for first, r in _PIECES.get(name, ((0, rows),)):
                off, rpad, at, _, _, lane = layout[name, first]
                for j in range(-(-width // LANES)):
                    cw = min(LANES, width - j * LANES)
                    ls = slice(lane, lane + cw)
                    if loc is None:
                        start = off + j * rpad + at
                        g = red_ref[start:start + r, ls]
                    else:
                        blk = c * (loc // LANES) + j
                        if r >= 8:
                            g = red_ref[pl.ds(pl.multiple_of(off + at + blk * rpad, 8), r), ls]
                        else:
                            tile = red_ref[pl.ds(pl.multiple_of(off + at // 8 * 8 + blk * rpad, 8), 8), ls]
                            g = tile[at % 8:at % 8 + r]
                    rs, cs = slice(first, first + r), slice(j * LANES, j * LANES + cw)
                    d, m2, v2 = _adam_math(w_ref[rs, cs], g, m_ref[rs, cs], v_ref[rs, cs])
                    for o, val in zip(o_refs, (g, d, m2, v2)):
                        o[rs, cs] = val

    args, shapes = [], []
    for name in names:
        args += [wts[name], ms[name], vs[name]]
        shapes += [jax.ShapeDtypeStruct(wts[name].shape, F32)] * 4
    vm = pl.BlockSpec(memory_space=pltpu.VMEM)
    res = pl.pallas_call(body, name="adamw_small",
                         in_specs=[pl.BlockSpec(memory_space=pltpu.SMEM), vm] + [vm] * (3 * n),
                         out_specs=[vm] * (4 * n), out_shape=shapes,
                         compiler_params=_cparams(None, VMEM_BIG))(chip, red, *args)
    return {name: res[4 * i:4 * i + 4] for i, name in enumerate(names)}


PACK_ROWS = 8


def _packed_rows(shape):
    size = 1
    for d in shape:
        size *= d
    return -(-size // (PACK_ROWS * LANES)) * PACK_ROWS


def _pack(arrs, row_mult):
    parts = []
    for a in arrs:
        flat = a.reshape(-1).astype(F32)
        rows = _packed_rows(a.shape)
        parts.append(jnp.pad(flat, (0, rows * LANES - flat.shape[0])).reshape(rows, LANES))
    total = sum(p.shape[0] for p in parts)
    fill = -(-total // row_mult) * row_mult - total
    if fill:
        parts.append(jnp.zeros((fill, LANES), F32))
    return jnp.concatenate(parts, axis=0)


def _unpack(packed, shapes):
    out, off = [], 0
    for s in shapes:
        rows = _packed_rows(s)
        size = 1
        for d in s:
            size *= d
        out.append(packed[off:off + rows].reshape(-1)[:size].reshape(s))
        off += rows
    return out


_SMALL_SH = ("e_w2", "e_a2", "e_g2", "e_conv_w", "o_norm_g", "o_D", "f_conv_w")
_LARGE = (("e_w_in", True), ("e_w_out", False), ("o_w_in", False), ("o_w_glu", True), ("f_w_up", True),
        ("f_w_down", False))
_ORDER = ("e_norm_g", "e_w_in", "e_mu", "e_w0", "e_w2", "e_a0", "e_a2", "e_g2", "e_k_k", "e_k_a", "e_r_k", "e_ln_w",
          "e_ln_b", "e_conv_w", "e_conv_b", "e_gate_a_w", "e_gate_a_b", "e_gate_x_w", "e_gate_x_b", "e_lru_lambda",
          "e_w_out", "o_norm_g", "o_w_in", "o_A_re", "o_A_im", "o_log_dt", "o_B_re", "o_B_im", "o_C_re", "o_C_im",
          "o_D", "o_w_glu", "f_norm_g", "f_w_up", "f_conv_w", "f_conv_b", "f_w_down", "final_norm_g")
N_CHIPS = 4
N_DEV = 8


def _step(x, tgt, wts, ms, vs):
    xi, yi, ci = _coords()
    chip = 2 * xi + yi
    chip1 = chip.astype(jnp.int32).reshape(1)
    me2 = jnp.stack([4 * xi + 2 * yi + ci, ci]).astype(jnp.int32)
    by_cols = dict(_LARGE)

    bufs = {(name, l): _cast_shard(wts[name], l, by_cols[name], chip1, f"cast_{name}{l}")
            for name, _ in _LARGE for l in range(wts[name].shape[0])}
    sh_shapes = [wts[n].shape for n in _SMALL_SH]
    packed = _pack([wts[n] for n in _SMALL_SH], 8)
    small_buf = lax.dynamic_update_slice(jnp.zeros((N_CHIPS,) + packed.shape, F32), packed[None], (chip, 0, 0))
    early = [("e_w_in", 0)]
    late = [k for k in bufs if k not in early]
    send, recv, thru, token = _gather_start([bufs[k] for k in early] + [small_buf], "gather_start_a", x)
    got = _gather_wait(thru, send, recv, "gather_wait_a", token)
    send_b, recv_b, thru_b, token = _gather_start([bufs[k] for k in late], "gather_start_b", got[0])
    x, _ = lax.optimization_barrier((x, token))

    def rows(g):
        return g.reshape(N_CHIPS * g.shape[1], g.shape[2])

    full = {n: wts[n] for n, _, loc in _SMALL if loc is None}
    full["e_w_in_t"] = rows(got[0])
    per_chip = [_unpack(got[1][k], sh_shapes) for k in range(N_CHIPS)]
    for i, n in enumerate(_SMALL_SH):
        full[n] = jnp.concatenate([per_chip[k][i] for k in range(N_CHIPS)], axis=-1)

    def late_weights(after):
        res = dict(zip(late, _gather_wait(thru_b, send_b, recv_b, "gather_wait_b", after)))
        return {"e_w_out": rows(res[("e_w_out", 0)]), "o_w_in": rows(res[("o_w_in", 0)]),
                "o_w_glu_t": rows(res[("o_w_glu", 0)]),
                "f_w_up_t": [rows(res[("f_w_up", l)]) for l in range(2)],
                "f_w_down": [rows(res[("f_w_down", l)]) for l in range(2)]}

    pending = []

    def send_grads(tag, items, carry):
        srcs = [g.reshape(N_DEV, g.shape[0] // N_DEV, g.shape[1]) for _, _, g in items]
        s_sem, r_sem, both, tok = _scatter_start(srcs, f"scatter_start_{tag}", carry)
        pending.append((tag, [(name, l) for name, l, _ in items], s_sem, r_sem, both))
        carry, _ = lax.optimization_barrier((carry, tok))
        return carry

    loss, grad_x, gs = _local_step(x, tgt, full, late_weights, send_grads)

    final = {}
    red = _allreduce_small(_small_pack(gs).reshape(N_DEV, -1, LANES)).reshape(-1, LANES)
    view = {name: (rows, cols if loc is None else loc) for name, (rows, cols), loc in _SMALL}
    as2d = lambda d: {name: d[name].reshape(view[name]) for name in view}
    small = _adamw_small(red, chip1, as2d(wts), as2d(ms), as2d(vs))
    for name, res in small.items():
        final[name] = [r.reshape(wts[name].shape) for r in res]
    new_v = small["final_norm_g"][3]

    halves, keys = [], []
    for tag, names, s_sem, r_sem, both in pending:
        srcs, lands = _scatter_wait(both, s_sem, r_sem, f"scatter_wait_{tag}", new_v)
        for (name, l), src, land in zip(names, srcs, lands):
            halves.append(_sum_segments(src, land, me2, f"sum_{name}{l}"))
            keys.append((name, l))
    shards = _exchange_sibling(halves)
    for s, (name, l) in zip(shards, keys):
        final[name] = _adamw_big(wts[name], ms[name], vs[name], l, s.reshape(2 * s.shape[1], s.shape[2]),
                                 by_cols[name], f"adamw_{name}{l}", prev=final.get(name))

    loss = lax.psum(loss[0, 0], ("x", "y", "c"))
    res = [loss, grad_x[None]]
    for k in range(4):
        res += [final[n][k] for n in _ORDER]
    return tuple(res)


def kernel(x, e_norm_g, e_w_in, e_mu, e_w0, e_w2, e_a0, e_a2, e_g2, e_k_k, e_k_a, e_r_k, e_ln_w, e_ln_b, e_conv_w, e_conv_b, e_gate_a_w, e_gate_a_b, e_gate_x_w, e_gate_x_b, e_lru_lambda, e_w_out, o_norm_g, o_w_in, o_A_re, o_A_im, o_log_dt, o_B_re, o_B_im, o_C_re, o_C_im, o_D, o_w_glu, f_norm_g, f_w_up, f_conv_w, f_conv_b, f_w_down, final_norm_g, loss_target, m_e_norm_g, m_e_w_in, m_e_mu, m_e_w0, m_e_w2, m_e_a0, m_e_a2, m_e_g2, m_e_k_k, m_e_k_a, m_e_r_k, m_e_ln_w, m_e_ln_b, m_e_conv_w, m_e_conv_b, m_e_gate_a_w, m_e_gate_a_b, m_e_gate_x_w, m_e_gate_x_b, m_e_lru_lambda, m_e_w_out, m_o_norm_g, m_o_w_in, m_o_A_re, m_o_A_im, m_o_log_dt, m_o_B_re, m_o_B_im, m_o_C_re, m_o_C_im, m_o_D, m_o_w_glu, m_f_norm_g, m_f_w_up, m_f_conv_w, m_f_conv_b, m_f_w_down, m_final_norm_g, v_e_norm_g, v_e_w_in, v_e_mu, v_e_w0, v_e_w2, v_e_a0, v_e_a2, v_e_g2, v_e_k_k, v_e_k_a, v_e_r_k, v_e_ln_w, v_e_ln_b, v_e_conv_w, v_e_conv_b, v_e_gate_a_w, v_e_gate_a_b, v_e_gate_x_w, v_e_gate_x_b, v_e_lru_lambda, v_e_w_out, v_o_norm_g, v_o_w_in, v_o_A_re, v_o_A_im, v_o_log_dt, v_o_B_re, v_o_B_im, v_o_C_re, v_o_C_im, v_o_D, v_o_w_glu, v_f_norm_g, v_f_w_up, v_f_conv_w, v_f_conv_b, v_f_w_down, v_final_norm_g):
    args = locals()
    wts = {n: args[n] for n in _ORDER}
    ms = {n: args["m_" + n] for n in _ORDER}
    vs = {n: args["v_" + n] for n in _ORDER}
    return _step(x[0], loss_target[0], wts, ms, vs)
```

```python
import functools

import jax
import jax.numpy as jnp
from jax import lax
from jax.experimental import pallas as pl
from jax.experimental.pallas import tpu as pltpu

F32 = jnp.float32
BF16 = jnp.bfloat16
MESH = pl.DeviceIdType.MESH

D_MODEL = 1024
HEAD = 64
RW = 512
N_HEADS = RW // HEAD
LRU_W = 512
SHIFT_COLS = 1792
W_LORA, A_LORA, G_LORA = 64, 64, 128
S5_GROUPS, S5_GROUP, S5_STATE = 64, 16, 64
D_FF = 2816
NORM_EPS = 1e-6
GN_EPS = 64e-5
LRU_C = 8.0
ADAM_LR, ADAM_B1, ADAM_B2, ADAM_EPS, ADAM_WD, ADAM_STEP = 0.001, 0.9, 0.999, 1e-08, 0.01, 10

VMEM_BIG = 56 * 1024 * 1024
VMEM_MID = 40 * 1024 * 1024
LANES = 128
PT = 16
WKV_CHUNK = 32
S5_SLAB = 128


def _blocked(*args, **kw):
    call = pl.pallas_call(*args, **kw)

    def run(*ops):
        return call(*[pltpu.with_memory_space_constraint(a, pltpu.HBM) if a.ndim >= 2 else a for a in ops])

    return run


def _cparams(sem=None, vmem=None):
    kw = {}
    if sem is not None:
        kw["dimension_semantics"] = sem
    if vmem is not None:
        kw["vmem_limit_bytes"] = vmem
    return pltpu.CompilerParams(**kw)


def _tile(dim, cands):
    for c in cands:
        if dim % c == 0:
            return c
    return dim


def _full(shape):
    n = len(shape)
    return pl.BlockSpec(shape, lambda *_: (0,) * n)


_TILES = (2816, 2048, 1408, 1024, 512, 256, 128)
MM_BUDGET = 36 * 1024 * 1024
VMEM_SLACK = 12 * 1024 * 1024


MXU_FLOPS = 9.0e14
HBM_BYTES = 3.3e12
STEP_SECONDS = 0.35e-6


def _mm_tiles(m, n, k, size_a, size_b, size_o, has_add, parts=1, tk_only=None, tm_max=None):
    best = None
    for tm in _TILES:
        for tk in _TILES:
            for tn in _TILES:
                if m % tm or n % tn or k % tk or (tk_only and tk != tk_only) or (tm_max and tm_max % tm):
                    continue
                need = (2 * (parts * tm * tk * size_a + tk * tn * size_b + tm * tn * size_o)
                        + tm * tn * 4 * (1 + 2 * has_add))
                if k > tk:
                    need += tm * tn * 4
                if need > MM_BUDGET:
                    continue
                steps = (m // tm) * (n // tn) * (k // tk)
                a_reads = n // tn if k > tk else 1
                moved = (m * k * size_a * a_reads + k * n * size_b * (m // tm) + m * n * (size_o + 4 * has_add))
                cost = max(2.0 * m * n * k / MXU_FLOPS, moved / HBM_BYTES) + steps * STEP_SECONDS
                cand = (-cost, tk, tm, tn)
                if best is None or cand > best[0]:
                    best = (cand, need)
    (_, tk, tm, tn), need = best
    return tm, tn, tk, need


def _matmul(a, b, mode, name, out_dtype=F32, add=None):
    parts = a if isinstance(a, tuple) else (a,)
    na = len(parts)
    wide = parts[0].shape[1]
    if mode == "nn":
        (m, k), (k2, n) = (parts[0].shape[0], na * wide), b.shape
    elif mode == "nt":
        (m, k), (n, k2) = (parts[0].shape[0], na * wide), b.shape
    else:
        (k, m), (k2, n) = (parts[0].shape[0], na * wide), b.shape
    assert k == k2, (parts[0].shape, b.shape, mode)
    split = {} if na == 1 else ({"tm_max": wide} if mode == "tn" else {"tk_only": wide})
    tm, tn, tk, need = _mm_tiles(m, n, k, parts[0].dtype.itemsize, b.dtype.itemsize, jnp.dtype(out_dtype).itemsize,
                                 add is not None, parts=na, **split)
    nk = k // tk
    per_part = wide // (tm if mode == "tn" else tk)
    dims = {"nn": (((1,), (0,)), ((), ())), "nt": (((1,), (1,)), ((), ())), "tn": (((0,), (0,)), ((), ()))}[mode]

    def body(*refs):
        a_refs, b_ref = refs[:na], refs[na]
        add_ref = refs[na + 1] if add is not None else None
        o_ref = refs[na + 2] if add is not None else refs[na + 1]
        kk = pl.program_id(2)

        def finish(r):
            if add_ref is not None:
                r = r + add_ref[...]
            o_ref[...] = r.astype(o_ref.dtype)

        def use(a_ref):
            part = lax.dot_general(a_ref[...].astype(BF16), b_ref[...].astype(BF16), dims, preferred_element_type=F32)
            if nk == 1:
                finish(part)
                return
            acc = refs[-1]

            @pl.when(kk == 0)
            def _():
                acc[...] = part

            @pl.when(kk > 0)
            def _():
                acc[...] += part

            @pl.when(kk == nk - 1)
            def _():
                finish(acc[...])

        if na == 1:
            use(a_refs[0])
        else:
            which = (pl.program_id(0) if mode == "tn" else kk) // per_part
            for p in range(na):
                pl.when(which == p)(functools.partial(use, a_refs[p]))

    def a_spec(p):
        def along(pos):
            return jnp.clip(pos - p * per_part, 0, per_part - 1) if na > 1 else pos
        if mode == "tn":
            return pl.BlockSpec((tk, tm), lambda i, j, kk: (kk, along(i)))
        return pl.BlockSpec((tm, tk), lambda i, j, kk: (i, along(kk)))

    if mode == "nn":
        b_spec = pl.BlockSpec((tk, tn), lambda i, j, kk: (kk, j))
    elif mode == "nt":
        b_spec = pl.BlockSpec((tn, tk), lambda i, j, kk: (j, kk))
    else:
        b_spec = pl.BlockSpec((tk, tn), lambda i, j, kk: (kk, j))
    o_spec = pl.BlockSpec((tm, tn), lambda i, j, kk: (i, j))
    in_specs = [a_spec(p) for p in range(na)] + [b_spec] + ([o_spec] if add is not None else [])
    args = parts + (b,) + ((add,) if add is not None else ())
    return _blocked(
        body, name=name, grid=(m // tm, n // tn, nk),
        in_specs=in_specs, out_specs=o_spec,
        out_shape=jax.ShapeDtypeStruct((m, n), out_dtype),
        scratch_shapes=[pltpu.VMEM((tm, tn), F32)] if nk > 1 else [],
        compiler_params=_cparams(("parallel", "parallel", "arbitrary"), min(VMEM_BIG, need + VMEM_SLACK)),
    )(*args)


TOK = 256
ROWS = 512


def _rms(x, g):
    return x * lax.rsqrt(jnp.mean(x * x, axis=-1, keepdims=True) + NORM_EPS) * g


def _rms_fwd(x, g, name):
    t, d = x.shape

    def body(x_ref, g_ref, o_ref):
        o_ref[...] = _rms(x_ref[...], g_ref[...]).astype(BF16)

    row = pl.BlockSpec((ROWS, d), lambda i: (i, 0))
    return _blocked(body, name=name, grid=(t // ROWS,), in_specs=[row, _full((1, d))], out_specs=row,
                          out_shape=jax.ShapeDtypeStruct((t, d), BF16),
                          compiler_params=_cparams(("parallel",), VMEM_MID))(x, g)


def _rms_bwd(x, g, dxn, res, name):
    t, d = x.shape

    def body(x_ref, g_ref, d_ref, res_ref, dx_ref, dg_ref):
        _, vjp = jax.vjp(_rms, x_ref[...], g_ref[...])
        dx, dg = vjp(d_ref[...].astype(F32))
        dx_ref[...] = dx + res_ref[...]

        @pl.when(pl.program_id(0) == 0)
        def _():
            dg_ref[...] = jnp.zeros_like(dg_ref)

        dg_ref[...] += dg

    row = pl.BlockSpec((ROWS, d), lambda i: (i, 0))
    return _blocked(body, name=name, grid=(t // ROWS,), in_specs=[row, _full((1, d)), row, row],
                          out_specs=[row, _full((1, d))],
                          out_shape=[jax.ShapeDtypeStruct((t, d), F32), jax.ShapeDtypeStruct((1, d), F32)],
                          compiler_params=_cparams(("arbitrary",), VMEM_MID))(x, g, dxn, res)


def _loss_head(x, g, tgt):
    t, d = x.shape

    def body(x_ref, g_ref, t_ref, l_ref, dx_ref, dg_ref):
        tg = t_ref[...]

        def fn(xv, gv):
            err = _rms(xv, gv) - tg
            per_tok = jnp.mean(err * err, axis=-1, keepdims=True)
            return 0.5 * jnp.sum(per_tok, axis=0, keepdims=True)

        l, vjp = jax.vjp(fn, x_ref[...], g_ref[...])
        dx, dg = vjp(jnp.ones((1, 1), F32))
        dx_ref[...] = dx

        @pl.when(pl.program_id(0) == 0)
        def _():
            dg_ref[...] = jnp.zeros_like(dg_ref)
            l_ref[...] = jnp.zeros_like(l_ref)

        dg_ref[...] += dg
        l_ref[...] += jnp.broadcast_to(l, l_ref.shape)

    row = pl.BlockSpec((ROWS, d), lambda i: (i, 0))
    return _blocked(body, name="loss_head", grid=(t // ROWS,), in_specs=[row, _full((1, d)), row],
                          out_specs=[_full((1, LANES)), row, _full((1, d))],
                          out_shape=[jax.ShapeDtypeStruct((1, LANES), F32), jax.ShapeDtypeStruct((t, d), F32),
                                     jax.ShapeDtypeStruct((1, d), F32)],
                          compiler_params=_cparams(("arbitrary",), VMEM_MID))(x, g, tgt)


def _glu_fwd(x, z):
    t, d = x.shape

    def body(x_ref, v_ref, g_ref, o_ref):
        o_ref[...] = x_ref[...] + v_ref[...] * jax.nn.sigmoid(g_ref[...])

    row = pl.BlockSpec((ROWS, d), lambda i: (i, 0))
    gate = pl.BlockSpec((ROWS, d), lambda i: (i, 1))
    return _blocked(body, name="glu_fwd", grid=(t // ROWS,), in_specs=[row, row, gate], out_specs=row,
                          out_shape=jax.ShapeDtypeStruct((t, d), F32),
                          compiler_params=_cparams(("parallel",), VMEM_MID))(x, z, z)


def _glu_bwd(z, g):
    t, d = g.shape

    def body(v_ref, g_ref, d_ref, o_ref):
        s = jax.nn.sigmoid(g_ref[...])
        dy = d_ref[...]
        o_ref[:, :d] = (dy * s).astype(BF16)
        o_ref[:, d:] = (dy * v_ref[...] * s * (1.0 - s)).astype(BF16)

    row = pl.BlockSpec((ROWS, d), lambda i: (i, 0))
    gate = pl.BlockSpec((ROWS, d), lambda i: (i, 1))
    return _blocked(body, name="glu_bwd", grid=(t // ROWS,), in_specs=[row, gate, row],
                          out_specs=pl.BlockSpec((ROWS, 2 * d), lambda i: (i, 0)),
                          out_shape=jax.ShapeDtypeStruct((t, 2 * d), BF16),
                          compiler_params=_cparams(("parallel",), VMEM_MID))(z, z, g)


def _shift_down(x, d):
    row = lax.broadcasted_iota(jnp.int32, x.shape, 0)
    return jnp.where(row < d, 0.0, pltpu.roll(x, d, 0))


def _shift_up(x, d):
    n = x.shape[0]
    row = lax.broadcasted_iota(jnp.int32, x.shape, 0)
    return jnp.where(row >= n - d, 0.0, pltpu.roll(x, n - d, 0))


def _make_sd():
    @functools.partial(jax.custom_vjp, nondiff_argnums=(1,))
    def sd(x, d):
        return _shift_down(x, d)

    def fwd(x, d):
        return _shift_down(x, d), None

    def bwd(d, _, g):
        return (_shift_up(g, d),)

    sd.defvjp(fwd, bwd)
    return sd


def _lin_scan(a, u, reverse=False):
    n = a.shape[0]
    row = lax.broadcasted_iota(jnp.int32, a.shape, 0)
    d = 1
    while d < n:
        if reverse:
            keep = row < n - d
            a_s, u_s = pltpu.roll(a, n - d, 0), pltpu.roll(u, n - d, 0)
        else:
            keep = row >= d
            a_s, u_s = pltpu.roll(a, d, 0), pltpu.roll(u, d, 0)
        u = u + a * jnp.where(keep, u_s, 0.0)
        a = a * jnp.where(keep, a_s, 1.0)
        d *= 2
    return u


def _make_scan():
    @jax.custom_vjp
    def scan(a, u):
        return _lin_scan(a, u)

    def fwd(a, u):
        h = _lin_scan(a, u)
        return h, (a, h)

    def bwd(res, dh):
        a, h = res
        g = _lin_scan(_shift_up(a, 1), dh, reverse=True)
        return g * _shift_down(h, 1), g

    scan.defvjp(fwd, bwd)
    return scan


def _acc_out(ref, val):
    @pl.when(pl.program_id(0) == 0)
    def _():
        ref[...] = jnp.zeros_like(ref)

    ref[...] += val


FFN_CW = 128


def _ffn_fn(hg, hv, wg, wv, bg, bv, sd):
    cg = wg[0:1] * sd(hg, 2) + wg[1:2] * sd(hg, 1) + wg[2:3] * hg + bg
    cv = wv[0:1] * sd(hv, 2) + wv[1:2] * sd(hv, 1) + wv[2:3] * hv + bv
    return jax.nn.silu(cg) * cv


def _ffn_specs(t):
    nb = D_FF // FFN_CW
    col = lambda r, off: pl.BlockSpec((r, FFN_CW), lambda j: (0, j + off))
    return nb, [col(t, 0), col(t, nb), col(3, 0), col(3, nb), col(1, 0), col(1, nb)], col


def _ffn_mid_fwd(h, cw, cb, name):
    t = h.shape[0]
    nb, in_specs, col = _ffn_specs(t)

    def body(hg, hv, wg, wv, bg, bv, o_ref):
        o_ref[...] = _ffn_fn(hg[...], hv[...], wg[...], wv[...], bg[...], bv[...], _shift_down).astype(BF16)

    return _blocked(body, name=name, grid=(nb,), in_specs=in_specs, out_specs=col(t, 0),
                          out_shape=jax.ShapeDtypeStruct((t, D_FF), BF16),
                          compiler_params=_cparams(("parallel",), VMEM_MID))(h, h, cw, cw, cb, cb)


def _ffn_mid_bwd(h, cw, cb, dact, name):
    t = h.shape[0]
    nb, in_specs, col = _ffn_specs(t)

    def body(hg, hv, wg, wv, bg, bv, d_ref, dhg, dhv, dwg, dwv, dbg, dbv):
        fn = functools.partial(_ffn_fn, sd=_make_sd())
        _, vjp = jax.vjp(fn, hg[...], hv[...], wg[...], wv[...], bg[...], bv[...])
        g = vjp(d_ref[...])
        dhg[...] = g[0].astype(BF16)
        dhv[...] = g[1].astype(BF16)
        dwg[...], dwv[...], dbg[...], dbv[...] = g[2], g[3], g[4], g[5]

    big = jax.ShapeDtypeStruct((t, D_FF), BF16)
    w3 = jax.ShapeDtypeStruct((3, D_FF), F32)
    b1 = jax.ShapeDtypeStruct((1, D_FF), F32)
    return _blocked(body, name=name, grid=(nb,), in_specs=in_specs + [col(t, 0)],
                          out_specs=[col(t, 0), col(t, 0), col(3, 0), col(3, 0), col(1, 0), col(1, 0)],
                          out_shape=[big, big, w3, w3, b1, b1],
                          compiler_params=_cparams(("parallel",), VMEM_BIG))(h, h, cw, cw, cb, cb, dact)


TS_CW = 256


def _tshift_fn(p, mu, sd):
    return p + mu * (sd(p, 1) - p)


def _tshift_fwd(p, mu):
    t = p.shape[0]
    col = lambda r: pl.BlockSpec((r, TS_CW), lambda j: (0, j))

    def body(p_ref, mu_ref, o_ref):
        o_ref[...] = _tshift_fn(p_ref[...], mu_ref[...], _shift_down)

    return _blocked(body, name="tshift_fwd", grid=(SHIFT_COLS // TS_CW,), in_specs=[col(t), col(1)],
                          out_specs=col(t), out_shape=jax.ShapeDtypeStruct((t, SHIFT_COLS), F32),
                          compiler_params=_cparams(("parallel",), VMEM_MID))(p, mu)


def _tshift_bwd(p, mu, dpam):
    t = p.shape[0]
    col = lambda r: pl.BlockSpec((r, TS_CW), lambda j: (0, j))

    def body(p_ref, mu_ref, d_ref, dp_ref, dmu_ref):
        _, vjp = jax.vjp(functools.partial(_tshift_fn, sd=_make_sd()), p_ref[...], mu_ref[...])
        dp, dmu = vjp(d_ref[...])
        dp_ref[...] = dp.astype(BF16)
        dmu_ref[...] = dmu

    return _blocked(body, name="tshift_bwd", grid=(SHIFT_COLS // TS_CW,), in_specs=[col(t), col(1), col(t)],
                          out_specs=[col(t), col(1)],
                          out_shape=[jax.ShapeDtypeStruct((t, SHIFT_COLS), BF16),
                                     jax.ShapeDtypeStruct((1, SHIFT_COLS), F32)],
                          compiler_params=_cparams(("parallel",), VMEM_MID))(p, mu, dpam)


_HI = lax.Precision.HIGHEST
_O = (0, RW, 2 * RW, 3 * RW, 3 * RW + W_LORA, 3 * RW + W_LORA + A_LORA, SHIFT_COLS)


def _dot16(a, b, dims=(((1,), (0,)), ((), ()))):
    return lax.dot_general(a.astype(BF16), b.astype(BF16), dims, preferred_element_type=F32)


def _make_dot16():
    @jax.custom_vjp
    def dot(a, b):
        return _dot16(a, b)

    def fwd(a, b):
        return _dot16(a, b), (a, b)

    def bwd(res, g):
        a, b = res
        return _dot16(g, b, (((1,), (1,)), ((), ()))), _dot16(a, g, (((0,), (0,)), ((), ())))

    dot.defvjp(fwd, bwd)
    return dot


def _seg(x):
    first = lax.broadcasted_iota(jnp.int32, (x.shape[0], LANES), 1) < HEAD
    parts = []
    for p in range(x.shape[1] // LANES):
        xp = x[:, p * LANES:(p + 1) * LANES]
        s0 = jnp.sum(jnp.where(first, xp, 0.0), axis=-1, keepdims=True)
        s1 = jnp.sum(jnp.where(first, 0.0, xp), axis=-1, keepdims=True)
        parts.append(jnp.where(first, s0, s1))
    return jnp.concatenate(parts, axis=1)


def _prep_fn(r, k, v, wd, ad, gd, w0, w2, a0, a2, g2, k_k, k_a, dot):
    w_log = -jax.nn.softplus(-(w0 + dot(jnp.tanh(wd), w2))) - 0.5
    decay = jnp.exp(-jnp.exp(w_log))
    a = jax.nn.sigmoid(a0 + dot(ad, a2))
    g = dot(jax.nn.sigmoid(gd), g2)
    kk = k * k_k
    kk = kk / jnp.maximum(jnp.sqrt(_seg(kk * kk)), 1e-12)
    k2 = k * (1.0 + (a - 1.0) * k_a)
    return r, decay, k2, v, -kk, kk * a, g


_PREP_W = ("w0", "w2", "a0", "a2", "g2", "k_k", "k_a")


def _prep_wspecs(w):
    return [_full(w[n].shape) for n in _PREP_W]


def _rwkv_prep_fwd(pam, w):
    t = pam.shape[0]

    def body(p_ref, *refs):
        wr, outs = refs[:7], refs[7:]
        pieces = [p_ref[:, _O[i]:_O[i + 1]] for i in range(6)]
        res = _prep_fn(*pieces, *[x[...] for x in wr], _dot16)
        for o, val in zip(outs, res):
            o[...] = val

    row = lambda c: pl.BlockSpec((TOK, c), lambda i: (i, 0))
    return _blocked(body, name="rwkv_prep_fwd", grid=(t // TOK,),
                          in_specs=[row(SHIFT_COLS)] + _prep_wspecs(w), out_specs=[row(RW)] * 7,
                          out_shape=[jax.ShapeDtypeStruct((t, RW), F32)] * 7,
                          compiler_params=_cparams(("parallel",), VMEM_MID))(pam, *[w[n] for n in _PREP_W])


def _rwkv_prep_bwd(pam, w, cts, more):
    t = pam.shape[0]

    def body(p_ref, *refs):
        wr, ct, ex, dp_ref, dws = refs[:7], refs[7:14], refs[14:17], refs[17], refs[18:]
        pieces = [p_ref[:, _O[i]:_O[i + 1]] for i in range(6)]
        fn = lambda *a: _prep_fn(*a, _make_dot16())
        _, vjp = jax.vjp(fn, *pieces, *[x[...] for x in wr])
        c = [x[...] for x in ct]
        c[0] = c[0] + ex[0][...]
        c[2] = c[2] + ex[1][...]
        c[3] = c[3] + ex[2][...]
        g = vjp(tuple(c))
        for i in range(6):
            dp_ref[:, _O[i]:_O[i + 1]] = g[i]
        for o, val in zip(dws, g[6:]):
            _acc_out(o, val)

    row = lambda c: pl.BlockSpec((TOK, c), lambda i: (i, 0))
    return _blocked(body, name="rwkv_prep_bwd", grid=(t // TOK,),
                          in_specs=[row(SHIFT_COLS)] + _prep_wspecs(w) + [row(RW)] * 10,
                          out_specs=[row(SHIFT_COLS)] + [_full(w[n].shape) for n in _PREP_W],
                          out_shape=[jax.ShapeDtypeStruct((t, SHIFT_COLS), F32)]
                          + [jax.ShapeDtypeStruct(w[n].shape, F32) for n in _PREP_W],
                          compiler_params=_cparams(("arbitrary",), VMEM_MID))(
                              pam, *[w[n] for n in _PREP_W], *cts, *more)


def _post_fn(y, r, k2, v, g, ln_w, ln_b, r_k):
    inv = 1.0 / HEAD
    d = y - _seg(y) * inv
    yn = d * lax.rsqrt(_seg(d * d) * inv + GN_EPS) * ln_w + ln_b
    bonus = _seg(r * k2 * r_k) * v
    return (yn + bonus) * g


def _rwkv_post_fwd(y, r, k2, v, g, ln_w, ln_b, r_k):
    t = y.shape[0]

    def body(*refs):
        o_ref = refs[-1]
        o_ref[...] = _post_fn(*[x[...] for x in refs[:-1]]).astype(BF16)

    row = pl.BlockSpec((TOK, RW), lambda i: (i, 0))
    return _blocked(body, name="rwkv_post_fwd", grid=(t // TOK,),
                          in_specs=[row] * 5 + [_full((1, RW))] * 3, out_specs=row,
                          out_shape=jax.ShapeDtypeStruct((t, RW), BF16),
                          compiler_params=_cparams(("parallel",), VMEM_MID))(y, r, k2, v, g, ln_w, ln_b, r_k)


def _rwkv_post_bwd(y, r, k2, v, g, ln_w, ln_b, r_k, dya):
    t = y.shape[0]

    def body(*refs):
        ins, d_ref, outs = refs[:8], refs[8], refs[9:]
        _, vjp = jax.vjp(_post_fn, *[x[...] for x in ins])
        gr = vjp(d_ref[...])
        for o, val in zip(outs[:5], gr[:5]):
            o[...] = val
        for o, val in zip(outs[5:], gr[5:]):
            _acc_out(o, val)

    row = pl.BlockSpec((TOK, RW), lambda i: (i, 0))
    vec = _full((1, RW))
    return _blocked(body, name="rwkv_post_bwd", grid=(t // TOK,),
                          in_specs=[row] * 5 + [vec] * 3 + [row],
                          out_specs=[row] * 5 + [vec] * 3,
                          out_shape=[jax.ShapeDtypeStruct((t, RW), F32)] * 5 + [jax.ShapeDtypeStruct((1, RW), F32)] * 3,
                          compiler_params=_cparams(("arbitrary",), VMEM_MID))(y, r, k2, v, g, ln_w, ln_b, r_k, dya)


def _from_pt(x):
    n = x.shape[0]
    return x.reshape(n, HEAD, N_HEADS, PT).transpose(0, 3, 2, 1).reshape(n * PT, N_HEADS * HEAD)


def _lane_sum(x):
    return jnp.sum(x, axis=-1, keepdims=True)


def _pair_consts():
    lane = lax.broadcasted_iota(jnp.int32, (HEAD, LANES), 1)
    return lane, lane < HEAD


def _seg_sum_pair(x, first):
    return jnp.where(first, _lane_sum(jnp.where(first, x, 0.0)), _lane_sum(jnp.where(first, 0.0, x)))


def _to_pt(x):
    t = x.shape[0]
    return x.reshape(t // PT, PT, N_HEADS, HEAD).transpose(0, 3, 2, 1).reshape(t // PT, HEAD, N_HEADS * PT)


def _expand_cols(x, name):
    t = x.shape[0]
    tiles = WKV_CHUNK // PT

    def body(x_ref, o_ref):
        _, first = _pair_consts()
        for tl in range(tiles):
            tile = x_ref[tl]
            for j in range(PT):
                for p in range(N_HEADS // 2):
                    src = jnp.where(first, (2 * p) * PT + j, (2 * p + 1) * PT + j)
                    o_ref[tl * PT + j, :, p * LANES:(p + 1) * LANES] = jnp.take_along_axis(tile, src, axis=1)

    return _blocked(
        body, name=name, grid=(t // WKV_CHUNK,),
        in_specs=[pl.BlockSpec((tiles, HEAD, LANES), lambda i: (i, 0, 0))],
        out_specs=pl.BlockSpec((WKV_CHUNK, HEAD, RW), lambda i: (i, 0, 0)),
        out_shape=jax.ShapeDtypeStruct((t, HEAD, RW), F32),
        compiler_params=_cparams(("parallel",), VMEM_MID))(_to_pt(x))


def _wkv_fwd(w, k, z, b, v_exp):
    t = w.shape[0]
    nc = t // WKV_CHUNK
    pairs = N_HEADS // 2

    def body(w_ref, k_ref, z_ref, b_ref, v_ref, s_all, s_ref):
        @pl.when(pl.program_id(0) == 0)
        def _():
            s_ref[...] = jnp.zeros_like(s_ref)

        _, first = _pair_consts()

        def group(gi, carry):
            base = pl.multiple_of(gi * 8, 8)
            rows = [ref[pl.ds(base, 8), :] for ref in (w_ref, k_ref, z_ref, b_ref)]
            s = [s_ref[:, p * LANES:(p + 1) * LANES] for p in range(pairs)]
            for jj in range(8):
                for p in range(pairs):
                    cs = slice(p * LANES, (p + 1) * LANES)
                    wr, kr, zr, br = [x[jj:jj + 1, cs] for x in rows]
                    s_all[base + jj, :, cs] = s[p]
                    sa = _seg_sum_pair(s[p] * zr, first)
                    s[p] = s[p] * wr + sa * br + v_ref[base + jj, :, cs] * kr
            for p in range(pairs):
                s_ref[:, p * LANES:(p + 1) * LANES] = s[p]
            return carry

        lax.fori_loop(0, WKV_CHUNK // 8, group, 0)

    row = pl.BlockSpec((WKV_CHUNK, RW), lambda i: (i, 0))
    big = pl.BlockSpec((WKV_CHUNK, HEAD, RW), lambda i: (i, 0, 0))
    return _blocked(
        body, name="wkv_fwd", grid=(nc,), in_specs=[row] * 4 + [big], out_specs=[big, _full((HEAD, RW))],
        out_shape=[jax.ShapeDtypeStruct((t, HEAD, RW), F32), jax.ShapeDtypeStruct((HEAD, RW), F32)],
        compiler_params=_cparams(("arbitrary",), VMEM_MID))(w, k, z, b, v_exp)


def _wkv_out(r, s_all, s_last):
    t = r.shape[0]
    nc = t // WKV_CHUNK
    tiles = WKV_CHUNK // PT
    pairs = N_HEADS // 2

    def body(r_ref, s_ref, nxt_ref, last_ref, y_ref):
        lane, first = _pair_consts()
        after = jnp.where(pl.program_id(0) == nc - 1, last_ref[...], nxt_ref[0])
        for tl in range(tiles):
            ytile = jnp.zeros((HEAD, LANES), F32)
            for g in range(PT // 8):
                rows = r_ref[tl * PT + g * 8:tl * PT + g * 8 + 8, :]
                for jj in range(8):
                    tt = tl * PT + g * 8 + jj
                    j = g * 8 + jj
                    for p in range(pairs):
                        cs = slice(p * LANES, (p + 1) * LANES)
                        s = s_ref[tt + 1, :, cs] if tt + 1 < WKV_CHUNK else after[:, cs]
                        pr = s * rows[jj:jj + 1, cs]
                        y0 = _lane_sum(jnp.where(first, pr, 0.0))
                        y1 = _lane_sum(jnp.where(first, 0.0, pr))
                        ytile = jnp.where(lane == (2 * p) * PT + j, y0, ytile)
                        ytile = jnp.where(lane == (2 * p + 1) * PT + j, y1, ytile)
            y_ref[tl] = ytile

    row = pl.BlockSpec((WKV_CHUNK, RW), lambda i: (i, 0))
    pt = pl.BlockSpec((tiles, HEAD, LANES), lambda i: (i, 0, 0))
    big = pl.BlockSpec((WKV_CHUNK, HEAD, RW), lambda i: (i, 0, 0))
    nxt = pl.BlockSpec((1, HEAD, RW), lambda i: (jnp.minimum((i + 1) * WKV_CHUNK, t - 1), 0, 0))
    return _blocked(
        body, name="wkv_out", grid=(nc,), in_specs=[row, big, nxt, _full((HEAD, RW))], out_specs=pt,
        out_shape=jax.ShapeDtypeStruct((t // PT, HEAD, LANES), F32),
        compiler_params=_cparams(("parallel",), VMEM_MID))(r, s_all, s_all, s_last)


def _wkv_bwd(r, w, k, z, b, v_exp, s_all, dy_exp):
    t = r.shape[0]
    nc = t // WKV_CHUNK
    tiles = WKV_CHUNK // PT
    pairs = N_HEADS // 2

    def body(r_ref, w_ref, k_ref, z_ref, b_ref, v_ref, s_all_ref, dy_ref,
             dr_ref, dw_ref, dk_ref, dz_ref, db_ref, dv_ref, ds_ref):
        @pl.when(pl.program_id(0) == 0)
        def _():
            ds_ref[...] = jnp.zeros_like(ds_ref)

        lane, first = _pair_consts()
        col_sum = lambda x: jnp.sum(x, axis=0, keepdims=True)
        row8 = lax.broadcasted_iota(jnp.int32, (8, LANES), 0)
        for tl in reversed(range(tiles)):
            def group(gg, dvtile):
                gi = PT // 8 - 1 - gg
                base = pl.multiple_of(tl * PT + gi * 8, 8)
                rows = [ref[pl.ds(base, 8), :] for ref in (r_ref, w_ref, k_ref, z_ref, b_ref)]
                outs = (dr_ref, dw_ref, dk_ref, dz_ref, db_ref)
                tiles8 = {(id(o), p): jnp.zeros((8, LANES), F32) for o in outs for p in range(pairs)}
                ds = [ds_ref[:, p * LANES:(p + 1) * LANES] for p in range(pairs)]
                for jj in reversed(range(8)):
                    j = gi * 8 + jj
                    for p in range(pairs):
                        cs = slice(p * LANES, (p + 1) * LANES)

                        def put(ref, val, p=p, jj=jj):
                            tiles8[(id(ref), p)] = jnp.where(row8 == jj, val, tiles8[(id(ref), p)])

                        rr, wr, kr, zr, br = [x[jj:jj + 1, cs] for x in rows]
                        sp = s_all_ref[base + jj, :, cs]
                        vc = v_ref[base + jj, :, cs]
                        dyc = dy_ref[base + jj, :, cs]
                        sa = _seg_sum_pair(sp * zr, first)
                        st = sp * wr + sa * br + vc * kr
                        d = ds[p] + dyc * rr
                        put(dr_ref, col_sum(st * dyc))
                        dvk = d * kr
                        dv0 = _lane_sum(jnp.where(first, dvk, 0.0))
                        dv1 = _lane_sum(jnp.where(first, 0.0, dvk))
                        dvtile = jnp.where(lane == (2 * p) * PT + j, dv0, dvtile)
                        dvtile = jnp.where(lane == (2 * p + 1) * PT + j, dv1, dvtile)
                        put(dk_ref, col_sum(d * vc))
                        put(dw_ref, col_sum(sp * d))
                        u = _seg_sum_pair(d * br, first)
                        put(dz_ref, col_sum(sp * u))
                        put(db_ref, col_sum(d * sa))
                        ds[p] = d * wr + u * zr
                for p in range(pairs):
                    ds_ref[:, p * LANES:(p + 1) * LANES] = ds[p]
                for o in outs:
                    for p in range(pairs):
                        o[pl.ds(base, 8), p * LANES:(p + 1) * LANES] = tiles8[(id(o), p)]
                return dvtile

            dv_ref[tl] = lax.fori_loop(0, PT // 8, group, jnp.zeros((HEAD, LANES), F32))

    rev = lambda i: nc - 1 - i
    row = pl.BlockSpec((WKV_CHUNK, RW), lambda i: (rev(i), 0))
    pt = pl.BlockSpec((tiles, HEAD, LANES), lambda i: (rev(i), 0, 0))
    big = pl.BlockSpec((WKV_CHUNK, HEAD, RW), lambda i: (rev(i), 0, 0))
    return _blocked(
        body, name="wkv_bwd", grid=(nc,), in_specs=[row] * 5 + [big, big, big], out_specs=[row] * 5 + [pt],
        out_shape=[jax.ShapeDtypeStruct((t, RW), F32)] * 5 + [jax.ShapeDtypeStruct((t // PT, HEAD, LANES), F32)],
        scratch_shapes=[pltpu.VMEM((HEAD, RW), F32)],
        compiler_params=_cparams(("arbitrary",), VMEM_BIG))(r, w, k, z, b, v_exp, s_all, dy_exp)


LRU_CW = 128
_BX0 = SHIFT_COLS // LRU_CW
_BG0 = (SHIFT_COLS + LRU_W) // LRU_CW


def _lru_fn(bx, bg, cw, cb, ga, ba, gx, bxb, lam, sd, scan, dot):
    xc = cw[0:1] * sd(bx, 3) + cw[1:2] * sd(bx, 2) + cw[2:3] * sd(bx, 1) + cw[3:4] * bx + cb
    gr = jax.nn.sigmoid(dot(xc, ga) + ba)
    gi = jax.nn.sigmoid(dot(xc, gx) + bxb)
    log_a = -LRU_C * gr * jax.nn.softplus(-lam)
    a = jnp.exp(log_a)
    mult = jnp.sqrt(-jnp.tanh(log_a) * (jnp.exp(2.0 * log_a) + 1.0))
    return scan(a, xc * gi * mult) * jax.nn.gelu(bg)


def _lru_specs(t):
    col = lambda r, off=0: pl.BlockSpec((r, LRU_CW), lambda j: (0, j + off))
    diag = pl.BlockSpec((LRU_CW, LRU_CW), lambda j: (j, j))
    return col, [col(t, _BX0), col(t, _BG0), col(4), col(1), diag, col(1), diag, col(1), col(1)]


def _lru_fwd(p, cw, cb, ga, ba, gx, bxb, lam):
    t = p.shape[0]
    col, in_specs = _lru_specs(t)

    def body(*refs):
        o_ref = refs[-1]
        o_ref[...] = _lru_fn(*[x[...] for x in refs[:-1]], _shift_down, _lin_scan, _dot16).astype(BF16)

    return _blocked(body, name="lru_fwd", grid=(LRU_W // LRU_CW,), in_specs=in_specs, out_specs=col(t),
                          out_shape=jax.ShapeDtypeStruct((t, LRU_W), BF16),
                          compiler_params=_cparams(("parallel",), VMEM_MID))(p, p, cw, cb, ga, ba, gx, bxb, lam)


def _lru_bwd(p, cw, cb, ga, ba, gx, bxb, lam, dyb):
    t = p.shape[0]
    col, in_specs = _lru_specs(t)

    def body(*refs):
        ins, d_ref, outs = refs[:9], refs[9], refs[10:]
        fn = functools.partial(_lru_fn, sd=_make_sd(), scan=_make_scan(), dot=_make_dot16())
        _, vjp = jax.vjp(fn, *[x[...] for x in ins])
        g = vjp(d_ref[...])
        outs[0][...] = g[0].astype(BF16)
        outs[1][...] = g[1].astype(BF16)
        for o, val in zip(outs[2:], g[2:]):
            o[...] = val

    sq = pl.BlockSpec((LRU_CW, LRU_CW), lambda j: (j, 0))
    act = jax.ShapeDtypeStruct((t, LRU_W), BF16)
    vec = jax.ShapeDtypeStruct((1, LRU_W), F32)
    sqs = jax.ShapeDtypeStruct((LRU_W, LRU_CW), F32)
    return _blocked(body, name="lru_bwd", grid=(LRU_W // LRU_CW,), in_specs=in_specs + [col(t, RW // LRU_CW)],
                          out_specs=[col(t), col(t), col(4), col(1), sq, col(1), sq, col(1), col(1)],
                          out_shape=[act, act, jax.ShapeDtypeStruct((4, LRU_W), F32), vec, sqs, vec, sqs, vec, vec],
                          compiler_params=_cparams(("parallel",), VMEM_BIG))(p, p, cw, cb, ga, ba, gx, bxb, lam, dyb)


def _s5_disc_fn(a_re, a_im, log_dt, b_re, b_im, e):
    lam_re = jnp.minimum(a_re, -1e-4)
    lam_im = a_im
    dt = jnp.exp(log_dt)
    mag = jnp.exp(lam_re * dt)
    ab_re = mag * jnp.cos(lam_im * dt)
    ab_im = mag * jnp.sin(lam_im * dt)
    den = lam_re * lam_re + lam_im * lam_im
    zr = ab_re - 1.0
    q_re = jnp.dot((zr * lam_re + ab_im * lam_im) / den, e, precision=_HI)
    q_im = jnp.dot((ab_im * lam_re - zr * lam_im) / den, e, precision=_HI)
    return ab_re, ab_im, q_re * b_re - q_im * b_im, q_re * b_im + q_im * b_re


def _s5_disc_fwd(a_re, a_im, log_dt, b_re, b_im, e):
    def body(*refs):
        res = _s5_disc_fn(*[x[...] for x in refs[:6]])
        for o, val in zip(refs[6:], res):
            o[...] = val

    small = jax.ShapeDtypeStruct(a_re.shape, F32)
    wide = jax.ShapeDtypeStruct(b_re.shape, F32)
    return pl.pallas_call(body, name="s5_disc_fwd", out_shape=[small, small, wide, wide])(
        a_re, a_im, log_dt, b_re, b_im, e)


def _s5_disc_bwd(a_re, a_im, log_dt, b_re, b_im, e, cts):
    def body(*refs):
        ins, e_ref, ct, outs = refs[:5], refs[5], refs[6:10], refs[10:]
        _, vjp = jax.vjp(lambda *a: _s5_disc_fn(*a, e_ref[...]), *[x[...] for x in ins])
        for o, val in zip(outs, vjp(tuple(c[...] for c in ct))):
            o[...] = val

    shapes = [jax.ShapeDtypeStruct(x.shape, F32) for x in (a_re, a_im, log_dt, b_re, b_im)]
    return pl.pallas_call(body, name="s5_disc_bwd", out_shape=shapes)(a_re, a_im, log_dt, b_re, b_im, e, *cts)


def _cmul(a, b):
    return a[0] * b[0] - a[1] * b[1], a[0] * b[1] + a[1] * b[0]


def _s5_scan(sr, si, ab, reverse):
    n_tiles = sr.shape[0] // 8
    width = sr.shape[1]
    row8 = lax.broadcasted_iota(jnp.int32, (8, width), 0)
    p1 = ab
    p2 = _cmul(p1, p1)
    p4 = _cmul(p2, p2)
    pw = [p1]
    for _ in range(7):
        pw.append(_cmul(pw[-1], p1))
    cr = jnp.zeros((8, width), F32)
    ci = jnp.zeros((8, width), F32)
    for j in range(8):
        e = pw[7 - j] if reverse else pw[j]
        cr = jnp.where(row8 == j, e[0], cr)
        ci = jnp.where(row8 == j, e[1], ci)

    levels = []
    for d, q in ((1, p1), (2, p2), (4, p4)):
        keep = row8 < 8 - d if reverse else row8 >= d
        levels.append((d, (jnp.where(keep, q[0], 0.0), jnp.where(keep, q[1], 0.0))))

    def tile(i, carry):
        idx = n_tiles - 1 - i if reverse else i
        base = pl.multiple_of(idx * 8, 8)
        x = (sr[pl.ds(base, 8), :], si[pl.ds(base, 8), :])
        for d, q in levels:
            amt = 8 - d if reverse else d
            m = _cmul(q, (pltpu.roll(x[0], amt, 0), pltpu.roll(x[1], amt, 0)))
            x = (x[0] + m[0], x[1] + m[1])
        m = _cmul((cr, ci), carry)
        x = (x[0] + m[0], x[1] + m[1])
        sr[pl.ds(base, 8), :] = x[0]
        si[pl.ds(base, 8), :] = x[1]
        edge = slice(0, 1) if reverse else slice(7, 8)
        return x[0][edge], x[1][edge]

    zero = jnp.zeros((1, width), F32)
    lax.fori_loop(0, n_tiles, tile, (zero, zero))


_S5_W = S5_SLAB // S5_GROUP * S5_STATE


def _s5_specs(t):
    col = lambda r: pl.BlockSpec((r, S5_SLAB), lambda j: (0, j))
    bb = pl.BlockSpec((None, S5_SLAB, _S5_W), lambda j: (j, 0, 0))
    cd = pl.BlockSpec((None, _S5_W, S5_SLAB), lambda j: (j, 0, 0))
    ab = pl.BlockSpec((None, 1, _S5_W), lambda j: (j, 0, 0))
    return col, bb, cd, ab


def _s5_fwd(u, dvec, bbr, bbi, cdr, cdi, abr, abi):
    t, width = u.shape
    col, bb, cd, ab = _s5_specs(t)

    def body(u_ref, d_ref, bbr_ref, bbi_ref, cdr_ref, cdi_ref, abr_ref, abi_ref, o_ref, sr, si):
        uv = u_ref[...]
        sr[...] = _dot16(uv, bbr_ref[...])
        si[...] = _dot16(uv, bbi_ref[...])
        _s5_scan(sr, si, (abr_ref[...], abi_ref[...]), False)
        y = _dot16(sr[...], cdr_ref[...]) - _dot16(si[...], cdi_ref[...])
        o_ref[...] = jax.nn.gelu(y + d_ref[...] * uv).astype(BF16)

    return _blocked(body, name="s5_fwd", grid=(width // S5_SLAB,),
                          in_specs=[col(t), col(1), bb, bb, cd, cd, ab, ab], out_specs=col(t),
                          out_shape=jax.ShapeDtypeStruct((t, width), BF16),
                          scratch_shapes=[pltpu.VMEM((t, _S5_W), F32)] * 2,
                          compiler_params=_cparams(("parallel",), VMEM_BIG))(u, dvec, bbr, bbi, cdr, cdi, abr, abi)


def _s5_bwd(u, dvec, bbr, bbi, cdr, cdi, abr, abi, dyact):
    t, width = u.shape
    col, bb, cd, ab = _s5_specs(t)
    ns = width // S5_SLAB
    tn = (((0,), (0,)), ((), ()))
    nt = (((1,), (1,)), ((), ()))

    def body(u_ref, d_ref, bbr_ref, bbi_ref, cdr_ref, cdi_ref, abr_ref, abi_ref, dy_ref,
             du_ref, dd_ref, dbbr_ref, dbbi_ref, dcdr_ref, dcdi_ref, dabr_ref, dabi_ref, sr, si, gr, gi):
        uv = u_ref[...]
        dv = d_ref[...]
        abv = (abr_ref[...], abi_ref[...])
        sr[...] = _dot16(uv, bbr_ref[...])
        si[...] = _dot16(uv, bbi_ref[...])
        _s5_scan(sr, si, abv, False)
        y = _dot16(sr[...], cdr_ref[...]) - _dot16(si[...], cdi_ref[...])
        _, vjp = jax.vjp(jax.nn.gelu, y + dv * uv)
        (dpre,) = vjp(dy_ref[...].astype(F32))
        dd_ref[...] = jnp.sum(dpre * uv, axis=0, keepdims=True)
        dcdr_ref[...] = _dot16(sr[...], dpre, tn)
        dcdi_ref[...] = -_dot16(si[...], dpre, tn)
        gr[...] = _dot16(dpre, cdr_ref[...], nt)
        gi[...] = -_dot16(dpre, cdi_ref[...], nt)
        _s5_scan(gr, gi, (abv[0], -abv[1]), True)

        row8 = lax.broadcasted_iota(jnp.int32, (8, _S5_W), 0)

        def tile(i, carry):
            acc_r, acc_i, last_r, last_i = carry
            base = pl.multiple_of(i * 8, 8)
            s_r, s_i = sr[pl.ds(base, 8), :], si[pl.ds(base, 8), :]
            g_r, g_i = gr[pl.ds(base, 8), :], gi[pl.ds(base, 8), :]
            p_r = jnp.where(row8 == 0, last_r, pltpu.roll(s_r, 1, 0))
            p_i = jnp.where(row8 == 0, last_i, pltpu.roll(s_i, 1, 0))
            acc_r = acc_r + jnp.sum(g_r * p_r + g_i * p_i, axis=0, keepdims=True)
            acc_i = acc_i + jnp.sum(g_i * p_r - g_r * p_i, axis=0, keepdims=True)
            return acc_r, acc_i, s_r[7:8], s_i[7:8]

        zero = jnp.zeros((1, _S5_W), F32)
        acc_r, acc_i, _, _ = lax.fori_loop(0, t // 8, tile, (zero, zero, zero, zero))
        dabr_ref[...] = acc_r
        dabi_ref[...] = acc_i
        du_ref[...] = dpre * dv + _dot16(gr[...], bbr_ref[...], nt) + _dot16(gi[...], bbi_ref[...], nt)
        dbbr_ref[...] = _dot16(uv, gr[...], tn)
        dbbi_ref[...] = _dot16(uv, gi[...], tn)

    sds = jax.ShapeDtypeStruct
    return _blocked(
        body, name="s5_bwd", grid=(ns,), in_specs=[col(t), col(1), bb, bb, cd, cd, ab, ab, col(t)],
        out_specs=[col(t), col(1), bb, bb, cd, cd, ab, ab],
        out_shape=[sds((t, width), F32), sds((1, width), F32), sds((ns, S5_SLAB, _S5_W), F32),
                   sds((ns, S5_SLAB, _S5_W), F32), sds((ns, _S5_W, S5_SLAB), F32), sds((ns, _S5_W, S5_SLAB), F32),
                   sds((ns, 1, _S5_W), F32), sds((ns, 1, _S5_W), F32)],
        scratch_shapes=[pltpu.VMEM((t, _S5_W), F32)] * 4,
        compiler_params=_cparams(("parallel",), VMEM_BIG))(u, dvec, bbr, bbi, cdr, cdi, abr, abi, dyact)


def _gate_dense(w):
    h = w.shape[0]
    return jnp.einsum("hij,hg->higj", w, jnp.eye(h, dtype=F32)).reshape(h * HEAD, h * HEAD)


def _gate_blocks(d):
    x = d.reshape(LRU_W // LRU_CW, 2, HEAD, 2, HEAD)
    return jnp.einsum("tgihj,gh->tgij", x, jnp.eye(2, dtype=F32)).reshape(LRU_W // HEAD, HEAD, HEAD)


_GPS = S5_SLAB // S5_GROUP
_NS = S5_GROUPS // _GPS


def _s5_in_dense(bb):
    x = bb.reshape(_NS, _GPS, S5_STATE, S5_GROUP)
    return jnp.einsum("sgnc,gh->sgchn", x, jnp.eye(_GPS, dtype=F32)).reshape(_NS, S5_SLAB, _S5_W)


def _s5_in_blocks(d):
    x = d.reshape(_NS, _GPS, S5_GROUP, _GPS, S5_STATE)
    return jnp.einsum("sgchn,gh->sgnc", x, jnp.eye(_GPS, dtype=F32)).reshape(S5_GROUPS, S5_STATE * S5_GROUP)


def _s5_out_dense(c):
    x = c.reshape(_NS, _GPS, S5_GROUP, S5_STATE)
    return jnp.einsum("sgcn,gh->shngc", x, jnp.eye(_GPS, dtype=F32)).reshape(_NS, _S5_W, S5_SLAB)


def _s5_out_blocks(d):
    x = d.reshape(_NS, _GPS, S5_STATE, _GPS, S5_GROUP)
    return jnp.einsum("shngc,gh->sgcn", x, jnp.eye(_GPS, dtype=F32)).reshape(S5_GROUPS, S5_GROUP, S5_STATE)


def _local_step(x, tgt, w, late_weights, send_grads):
    d_model = x.shape[1]
    gs = {}
    n_layers = w["f_norm_g"].shape[0]

    def ffn_fwd(xin, l):
        xn = _rms_fwd(xin, w["f_norm_g"][l:l + 1], f"rms_f{l}")
        h = _matmul(xn, w["f_w_up_t"][l], "nt", f"mm_f{l}_up")
        act = _ffn_mid_fwd(h, w["f_conv_w"][l], w["f_conv_b"][l:l + 1], f"ffn_mid_fwd{l}")
        return _matmul(act, w["f_w_down"][l], "nn", f"mm_f{l}_down", add=xin), (xin, xn, h, act)

    def ffn_bwd(g, saved, l):
        xin, xn, h, act = saved
        dact = _matmul(g, w["f_w_down"][l], "nt", f"mm_f{l}_dact")
        d_down = _matmul(act, g, "tn", f"mm_f{l}_ddown", out_dtype=BF16)
        dhg, dhv, dwg, dwv, dbg, dbv = _ffn_mid_bwd(h, w["f_conv_w"][l], w["f_conv_b"][l:l + 1], dact,
                                                    f"ffn_mid_bwd{l}")
        dxn = _matmul((dhg, dhv), w["f_w_up_t"][l], "nn", f"mm_f{l}_dxn")
        d_up = _matmul((dhg, dhv), xn, "tn", f"mm_f{l}_dup", out_dtype=BF16)
        dx, dgn = _rms_bwd(xin, w["f_norm_g"][l:l + 1], dxn, g, f"rms_f{l}_bwd")
        return dx, d_up, d_down, jnp.concatenate([dwg, dwv], axis=1), jnp.concatenate([dbg, dbv], axis=1), dgn

    xn0 = _rms_fwd(x, w["e_norm_g"], "rms_e")
    p = _matmul(xn0, w["e_w_in_t"], "nt", "mm_e_in")
    pam = _tshift_fwd(p, w["e_mu"])
    pw = dict(w0=w["e_w0"], w2=w["e_w2"][0], a0=w["e_a0"], a2=w["e_a2"][0], g2=w["e_g2"][0],
              k_k=w["e_k_k"], k_a=w["e_k_a"])
    r, dec, k2, v, z, b, gate = _rwkv_prep_fwd(pam, pw)
    v_exp = _expand_cols(v, "wkv_expand_v")
    s_all, s_last = _wkv_fwd(dec, k2, z, b, v_exp)
    y_pt = _wkv_out(r, s_all, s_last)
    y = _from_pt(y_pt)
    rk = w["e_r_k"].reshape(1, RW)
    ya = _rwkv_post_fwd(y, r, k2, v, gate, w["e_ln_w"], w["e_ln_b"], rk)
    ga, gx = _gate_dense(w["e_gate_a_w"][0]), _gate_dense(w["e_gate_x_w"][0])
    lru_w = (w["e_conv_w"][0], w["e_conv_b"], ga, w["e_gate_a_b"], gx, w["e_gate_x_b"], w["e_lru_lambda"])
    yb = _lru_fwd(p, *lru_w)
    ycat = jnp.concatenate([ya, yb], axis=1)
    w = {**w, **late_weights(ycat)}
    x1 = _matmul(ycat, w["e_w_out"], "nn", "mm_e_out", add=x)
    x2, ffn0 = ffn_fwd(x1, 0)

    xn1 = _rms_fwd(x2, w["o_norm_g"], "rms_o")
    u = _matmul(xn1, w["o_w_in"], "nn", "mm_o_in")
    expand = jnp.kron(jnp.eye(S5_STATE, dtype=F32), jnp.ones((1, S5_GROUP), F32))
    disc_in = (w["o_A_re"][0], w["o_A_im"][0], w["o_log_dt"].reshape(S5_GROUPS, 1),
               w["o_B_re"][0].reshape(S5_GROUPS, -1), w["o_B_im"][0].reshape(S5_GROUPS, -1), expand)
    ab_re, ab_im, bb_re, bb_im = _s5_disc_fwd(*disc_in)
    s5_w = (w["o_D"], _s5_in_dense(bb_re), _s5_in_dense(bb_im), _s5_out_dense(w["o_C_re"][0]),
            _s5_out_dense(w["o_C_im"][0]), ab_re.reshape(_NS, 1, _S5_W), ab_im.reshape(_NS, 1, _S5_W))
    yact = _s5_fwd(u, *s5_w)
    zz = _matmul(yact, w["o_w_glu_t"], "nt", "mm_o_glu")
    x3 = _glu_fwd(x2, zz)
    x4, ffn1 = ffn_fwd(x3, 1)

    loss, g, gs["final_norm_g", 0] = _loss_head(x4, w["final_norm_g"].reshape(1, d_model), tgt)

    g, up1, down1, dcw1, dcb1, dfn1 = ffn_bwd(g, ffn1, 1)
    dz = _glu_bwd(zz, g)
    dyact = _matmul(dz, w["o_w_glu_t"], "nn", "mm_o_dyact")
    d_glu = _matmul(dz, yact, "tn", "mm_o_dglu", out_dtype=BF16)
    du, gs["o_D", 0], dbbr, dbbi, dcdr, dcdi, dabr, dabi = _s5_bwd(u, *s5_w, dyact)
    gs["o_C_re", 0] = _s5_out_blocks(dcdr).reshape(S5_GROUPS * S5_GROUP, S5_STATE)
    gs["o_C_im", 0] = _s5_out_blocks(dcdi).reshape(S5_GROUPS * S5_GROUP, S5_STATE)
    cts = (dabr.reshape(S5_GROUPS, S5_STATE), dabi.reshape(S5_GROUPS, S5_STATE), _s5_in_blocks(dbbr),
           _s5_in_blocks(dbbi))
    gs["o_A_re", 0], gs["o_A_im", 0], dlog_dt, gs["o_B_re", 0], gs["o_B_im", 0] = _s5_disc_bwd(*disc_in, cts)
    gs["o_log_dt", 0] = dlog_dt.reshape(1, S5_GROUPS)
    dxn = _matmul(du, w["o_w_in"], "nt", "mm_o_dxn")
    d_oin = _matmul(xn1, du, "tn", "mm_o_din", out_dtype=BF16)
    g, gs["o_norm_g", 0] = _rms_bwd(x2, w["o_norm_g"], dxn, g, "rms_o_bwd")
    g = send_grads("a", [("f_w_up", 1, up1), ("f_w_down", 1, down1), ("o_w_glu", 0, d_glu), ("o_w_in", 0, d_oin)], g)

    g, up0, down0, dcw0, dcb0, dfn0 = ffn_bwd(g, ffn0, 0)
    gs["f_conv_w", 0], gs["f_conv_w", 3] = dcw0, dcw1
    gs["f_conv_b", 0], gs["f_conv_b", 1] = dcb0, dcb1
    gs["f_norm_g", 0], gs["f_norm_g", 1] = dfn0, dfn1

    dycat = _matmul(g, w["e_w_out"], "nt", "mm_e_dycat")
    d_eout = _matmul(ycat, g, "tn", "mm_e_dout", out_dtype=BF16)
    dycat = send_grads("b", [("f_w_up", 0, up0), ("f_w_down", 0, down0), ("e_w_out", 0, d_eout)], dycat)
    dy, dr1, dk1, dv1, dgate, gs["e_ln_w", 0], gs["e_ln_b", 0], gs["e_r_k", 0] = _rwkv_post_bwd(
        y, r, k2, v, gate, w["e_ln_w"], w["e_ln_b"], rk, dycat)
    dr2, ddec, dk2, dzz, dbb, dv_pt = _wkv_bwd(r, dec, k2, z, b, v_exp, s_all, _expand_cols(dy, "wkv_expand_dy"))
    (dpam, gs["e_w0", 0], gs["e_w2", 0], gs["e_a0", 0], gs["e_a2", 0], gs["e_g2", 0], gs["e_k_k", 0],
     gs["e_k_a", 0]) = _rwkv_prep_bwd(pam, pw, (dr2, ddec, dk2, _from_pt(dv_pt), dzz, dbb, dgate), (dr1, dk1, dv1))
    dpa, gs["e_mu", 0] = _tshift_bwd(p, w["e_mu"], dpam)
    (dbx, dbg, gs["e_conv_w", 0], gs["e_conv_b", 0], dga, gs["e_gate_a_b", 0], dgx, gs["e_gate_x_b", 0],
     gs["e_lru_lambda", 0]) = _lru_bwd(p, *lru_w, dycat)
    gs["e_gate_a_w", 0] = _gate_blocks(dga).reshape(LRU_W, HEAD)
    gs["e_gate_x_w", 0] = _gate_blocks(dgx).reshape(LRU_W, HEAD)
    dp = jnp.concatenate([dpa, dbx, dbg], axis=1)
    d_ein = _matmul(dp, xn0, "tn", "mm_e_din", out_dtype=BF16)
    dp = send_grads("c", [("e_w_in", 0, d_ein)], dp)
    dxn = _matmul(dp, w["e_w_in_t"], "nn", "mm_e_dxn")
    grad_x, gs["e_norm_g", 0] = _rms_bwd(x, w["e_norm_g"], dxn, g, "rms_e_bwd")
    return loss, grad_x, gs


CAST_ROWS = 256


def _cast_shard(w3, layer, transpose, chip, name):
    _, rows, cols = w3.shape
    tr = _tile(rows, (CAST_ROWS, 176, 128))

    def body(c_ref, w_ref, o_ref):
        v = w_ref[...]
        o_ref[...] = (v.T if transpose else v).astype(BF16)

    in_spec = pl.BlockSpec((None, tr, cols), lambda i, c: (layer, i, 0))
    if transpose:
        out_spec, shape = pl.BlockSpec((None, cols, tr), lambda i, c: (c[0], 0, i)), (cols, rows)
    else:
        out_spec, shape = pl.BlockSpec((None, tr, cols), lambda i, c: (c[0], i, 0)), (rows, cols)
    grid_spec = pltpu.PrefetchScalarGridSpec(num_scalar_prefetch=1, grid=(rows // tr,), in_specs=[in_spec],
                                             out_specs=out_spec)
    return _blocked(body, name=name, grid_spec=grid_spec,
                          out_shape=jax.ShapeDtypeStruct((N_CHIPS,) + shape, BF16),
                          compiler_params=_cparams(("parallel",), VMEM_MID))(chip, w3)


_ANY = pl.BlockSpec(memory_space=pl.ANY)


def _coords():
    return lax.axis_index("x"), lax.axis_index("y"), lax.axis_index("c")


def _flip(v, d):
    return 1 - v if d else v


_CHIP_RELS = ((1, 0), (0, 1), (1, 1))
_DEV_RELS = tuple((dx, dy, dc) for dx in (0, 1) for dy in (0, 1) for dc in (0, 1))[1:]


_HBM = pl.BlockSpec(memory_space=pltpu.HBM)
_SEM = pl.BlockSpec(memory_space=pltpu.SEMAPHORE)
_EFFECT = pltpu.SideEffectType.DATAFLOW_SIDE_EFFECTING


def _in_hbm(a):
    return pltpu.with_memory_space_constraint(a, pltpu.HBM)


def _gather_copies(bufs, send, recv, landed):
    x, y, c = _coords()
    me = 2 * x + y
    res = []
    for i, buf in enumerate(bufs):
        for j, (dx, dy) in enumerate(_CHIP_RELS):
            px, py = _flip(x, dx), _flip(y, dy)
            k = i * len(_CHIP_RELS) + j
            res.append(pltpu.make_async_remote_copy(
                src_ref=buf.at[me], dst_ref=buf.at[2 * px + py if landed else me], send_sem=send.at[k],
                recv_sem=recv.at[k], device_id=(px, py, c), device_id_type=MESH))
    return res


def _scatter_copies(srcs, lands, send, recv, landed):
    x, y, c = _coords()
    me = 4 * x + 2 * y + c
    res = []
    for i, (src, land) in enumerate(zip(srcs, lands)):
        for j, (dx, dy, dc) in enumerate(_DEV_RELS):
            peer = (_flip(x, dx), _flip(y, dy), _flip(c, dc))
            pid = 4 * peer[0] + 2 * peer[1] + peer[2]
            k = i * len(_DEV_RELS) + j
            res.append(pltpu.make_async_remote_copy(
                src_ref=src.at[pid], dst_ref=land.at[pid if landed else me], send_sem=send.at[k],
                recv_sem=recv.at[k], device_id=peer, device_id_type=MESH))
    return res


def _split_start(bufs, n_src, copies, n_rel, name, after):
    n = len(bufs)
    nk = n_src * n_rel

    def body(*refs):
        ins, send, recv, token = refs[:n], refs[n + 1 + n], refs[n + 2 + n], refs[-1]
        for cp in copies(ins, send, recv, False):
            cp.start()
        token[...] = jnp.zeros_like(token)

    res = pl.pallas_call(
        body, name=name, in_specs=[_HBM] * n + [_ANY],
        out_specs=[_HBM] * n + [_SEM, _SEM, pl.BlockSpec(memory_space=pltpu.VMEM)],
        out_shape=[pltpu.HBM(b.shape, b.dtype) for b in bufs]
        + [pltpu.SemaphoreType.DMA((nk,)), pltpu.SemaphoreType.DMA((nk,)), jax.ShapeDtypeStruct((8, LANES), F32)],
        input_output_aliases={i: i for i in range(n)},
        compiler_params=pltpu.CompilerParams(has_side_effects=_EFFECT))(*[_in_hbm(b) for b in bufs], after)
    return res[n], res[n + 1], list(res[:n]), res[n + 2]


def _split_wait(bufs, send, recv, copies, name, after):
    n = len(bufs)

    def body(*refs):
        ins, send_ref, recv_ref = refs[:n], refs[n], refs[n + 1]
        for cp in copies(ins, send_ref, recv_ref, True):
            cp.wait_send()
            cp.wait_recv()

    return pl.pallas_call(
        body, name=name, in_specs=[_HBM] * n + [_SEM, _SEM, _ANY], out_specs=[_HBM] * n,
        out_shape=[pltpu.HBM(b.shape, b.dtype) for b in bufs], input_output_aliases={i: i for i in range(n)},
        compiler_params=pltpu.CompilerParams(has_side_effects=_EFFECT))(*bufs, send, recv, after)


def _gather_start(bufs, name, after):
    return _split_start(bufs, len(bufs), _gather_copies, len(_CHIP_RELS), name, after)


def _gather_wait(bufs, send, recv, name, after):
    return _split_wait(bufs, send, recv, _gather_copies, name, after)


def _scatter_start(srcs, name, after):
    n = len(srcs)
    lands = [lax.empty(a.shape, a.dtype) for a in srcs]
    fn = lambda refs, send, recv, landed: _scatter_copies(refs[:n], refs[n:], send, recv, landed)
    send, recv, bufs, token = _split_start(list(srcs) + lands, n, fn, len(_DEV_RELS), name, after)
    return send, recv, bufs, token


def _scatter_wait(bufs, send, recv, name, after):
    n = len(bufs) // 2
    fn = lambda refs, s, r, landed: _scatter_copies(refs[:n], refs[n:], s, r, landed)
    res = _split_wait(bufs, send, recv, fn, name, after)
    return res[:n], res[n:]


def _sum_segments(src, land, me, name):
    nd, seg, cols = src.shape
    ts = _tile(seg, (256, 176, 128))

    def body(m_ref, *refs):
        o_ref = refs[-1]
        acc = refs[0][...].astype(F32)
        for r in refs[1:-1]:
            acc = acc + r[...].astype(F32)
        o_ref[...] = acc

    def peer(rel):
        bits = 4 * rel[0] + 2 * rel[1] + rel[2]
        return pl.BlockSpec((None, ts, cols), lambda i, m: (jnp.bitwise_xor(m[0], bits), i, 0))

    grid_spec = pltpu.PrefetchScalarGridSpec(
        num_scalar_prefetch=1, grid=(seg // ts,),
        in_specs=[pl.BlockSpec((None, ts, cols), lambda i, m: (m[0], i, 0))] + [peer(r) for r in _DEV_RELS],
        out_specs=pl.BlockSpec((None, ts, cols), lambda i, m: (m[1], i, 0)))
    return _blocked(body, name=name, grid_spec=grid_spec,
                          out_shape=jax.ShapeDtypeStruct((2, seg, cols), F32),
                          compiler_params=_cparams(("parallel",), VMEM_MID))(me, src, *[land] * len(_DEV_RELS))


def _exchange_sibling(arrs):
    n = len(arrs)

    def body(*refs):
        outs, (send, recv) = refs[n:2 * n], refs[2 * n:]
        x, y, c = _coords()
        sib = (x, y, 1 - c)
        sends, recvs = [], []
        for i in range(n):
            cp = pltpu.make_async_remote_copy(src_ref=outs[i].at[c], dst_ref=outs[i].at[c], send_sem=send.at[i],
                                              recv_sem=recv.at[i], device_id=sib, device_id_type=MESH)
            cp.start()
            sends.append(cp)
            recvs.append(pltpu.make_async_remote_copy(src_ref=outs[i].at[c], dst_ref=outs[i].at[1 - c],
                                                      send_sem=send.at[i], recv_sem=recv.at[i], device_id=sib,
                                                      device_id_type=MESH))
        for cp in recvs:
            cp.wait_recv()
        for cp in sends:
            cp.wait_send()

    return pl.pallas_call(
        body, name="exchange_sibling", in_specs=[_ANY] * n, out_specs=[_ANY] * n,
        out_shape=[jax.ShapeDtypeStruct(a.shape, a.dtype) for a in arrs],
        input_output_aliases={i: i for i in range(n)},
        scratch_shapes=[pltpu.SemaphoreType.DMA((n,)), pltpu.SemaphoreType.DMA((n,))])(*arrs)


def _allreduce_small(vec):
    nd, rows, lanes = vec.shape
    nr = len(_DEV_RELS)

    def body(in_ref, out_ref, stage, red, send, recv):
        x, y, c = _coords()
        me = 4 * x + 2 * y + c
        peers = []
        for dx, dy, dc in _DEV_RELS:
            peer = (_flip(x, dx), _flip(y, dy), _flip(c, dc))
            peers.append((peer, 4 * peer[0] + 2 * peer[1] + peer[2]))

        def copy(src, dst, k, peer):
            return pltpu.make_async_remote_copy(src_ref=src, dst_ref=dst, send_sem=send.at[k], recv_sem=recv.at[k],
                                                device_id=peer, device_id_type=MESH)

        first = [copy(in_ref.at[pid], stage.at[me], j, peer) for j, (peer, pid) in enumerate(peers)]
        for cp in first:
            cp.start()
        stage[me] = in_ref[me]
        for j, (peer, pid) in enumerate(peers):
            copy(in_ref.at[pid], stage.at[pid], j, peer).wait_recv()
        acc = stage[0]
        for d in range(1, nd):
            acc = acc + stage[d]
        red[...] = acc
        out_ref[me] = acc
        second = [copy(red, out_ref.at[me], nr + j, peer) for j, (peer, pid) in enumerate(peers)]
        for cp in second:
            cp.start()
        for j, (peer, pid) in enumerate(peers):
            copy(red, out_ref.at[pid], nr + j, peer).wait_recv()
        for cp in first + second:
            cp.wait_send()

    vm = pl.BlockSpec(memory_space=pltpu.VMEM)
    return pl.pallas_call(
        body, name="allreduce_small", in_specs=[vm], out_specs=vm,
        out_shape=jax.ShapeDtypeStruct(vec.shape, F32),
        scratch_shapes=[pltpu.VMEM(vec.shape, F32), pltpu.VMEM((rows, lanes), F32),
                        pltpu.SemaphoreType.DMA((2 * nr,)), pltpu.SemaphoreType.DMA((2 * nr,))],
        compiler_params=_cparams(None, VMEM_MID))(vec)


def _adam_math(w, g, m, v):
    m2 = ADAM_B1 * m + (1.0 - ADAM_B1) * g
    v2 = ADAM_B2 * v + (1.0 - ADAM_B2) * (g * g)
    m_hat = m2 / (1.0 - ADAM_B1 ** ADAM_STEP)
    v_hat = v2 / (1.0 - ADAM_B2 ** ADAM_STEP)
    return -ADAM_LR * (m_hat / (jnp.sqrt(v_hat) + ADAM_EPS) + ADAM_WD * w), m2, v2


def _adamw_big(w3, m3, v3, layer, g, transposed, name, prev=None):
    nl, rows, cols = w3.shape
    tr = 128 if transposed else _tile(rows, (256, 176, 128))

    def body(w_ref, m_ref, v_ref, g_ref, *rest):
        go_ref, d_ref, mo_ref, vo_ref = rest[-4:]
        g_val = g_ref[...].T if transposed else g_ref[...]
        go_ref[...] = g_val
        d_ref[...], mo_ref[...], vo_ref[...] = _adam_math(w_ref[...], g_val, m_ref[...], v_ref[...])

    wspec = pl.BlockSpec((None, tr, cols), lambda i: (layer, i, 0))
    gspec = pl.BlockSpec((cols, tr), lambda i: (0, i)) if transposed else pl.BlockSpec((tr, cols), lambda i: (i, 0))
    extra = [] if prev is None else list(prev)
    return _blocked(body, name=name, grid=(rows // tr,),
                          in_specs=[wspec, wspec, wspec, gspec] + [_ANY] * len(extra),
                          out_specs=[wspec] * 4, out_shape=[jax.ShapeDtypeStruct((nl, rows, cols), F32)] * 4,
                          input_output_aliases={4 + i: i for i in range(len(extra))},
                          compiler_params=_cparams(("parallel",), VMEM_MID))(w3, m3, v3, g, *extra)


_SMALL = (
    ("e_norm_g", (1, D_MODEL), None), ("e_mu", (1, SHIFT_COLS), None), ("e_w0", (1, RW), None),
    ("e_w2", (W_LORA, RW), RW // 4), ("e_a0", (1, RW), None), ("e_a2", (A_LORA, RW), RW // 4),
    ("e_g2", (G_LORA, RW), RW // 4), ("e_k_k", (1, RW), None), ("e_k_a", (1, RW), None), ("e_r_k", (1, RW), None),
    ("e_ln_w", (1, RW), None), ("e_ln_b", (1, RW), None), ("e_conv_w", (4, LRU_W), LRU_W // 4),
    ("e_conv_b", (1, LRU_W), None), ("e_gate_a_w", (LRU_W, HEAD), None), ("e_gate_a_b", (1, LRU_W), None),
    ("e_gate_x_w", (LRU_W, HEAD), None), ("e_gate_x_b", (1, LRU_W), None), ("e_lru_lambda", (1, LRU_W), None),
    ("o_norm_g", (1, D_MODEL), D_MODEL // 4), ("o_A_re", (S5_GROUPS, S5_STATE), None),
    ("o_A_im", (S5_GROUPS, S5_STATE), None), ("o_log_dt", (1, S5_GROUPS), None),
    ("o_B_re", (S5_GROUPS, S5_STATE * S5_GROUP), None), ("o_B_im", (S5_GROUPS, S5_STATE * S5_GROUP), None),
    ("o_C_re", (S5_GROUPS * S5_GROUP, S5_STATE), None), ("o_C_im", (S5_GROUPS * S5_GROUP, S5_STATE), None),
    ("o_D", (1, D_MODEL), D_MODEL // 4), ("f_norm_g", (2, D_MODEL), None),
    ("f_conv_w", (6, 2 * D_FF), 2 * D_FF // 4), ("f_conv_b", (2, 2 * D_FF), None),
    ("final_norm_g", (1, D_MODEL), None))
_PIECES = {"f_norm_g": ((0, 1), (1, 1)), "f_conv_b": ((0, 1), (1, 1)), "f_conv_w": ((0, 3), (3, 3))}


def _ceil_to(n, m):
    return -(-n // m) * m


def _small_layout():
    groups = {}
    for name, (rows, cols), _ in _SMALL:
        for first, r in _PIECES.get(name, ((0, rows),)):
            groups.setdefault(cols, []).append((name, first, r))
    layout, off = {}, 0
    for cols, items in groups.items():
        stacks = [0, 0] if 2 * cols <= LANES else [0]
        placed = []
        for name, first, r in sorted(items, key=lambda it: -it[2]):
            half = stacks.index(min(stacks))
            r0 = stacks[half]
            if r >= 8 or r0 % 8 + r > 8:
                r0 = _ceil_to(r0, 8)
            placed.append((name, first, r, r0, half * (LANES // 2)))
            stacks[half] = r0 + r
        rpad = _ceil_to(max(stacks), 8)
        for name, first, r, at, lane in placed:
            layout[name, first] = (off, rpad, at, r, cols, lane)
        off += -(-cols // LANES) * rpad
    return layout, _ceil_to(off, 8 * N_DEV)


def _small_pack(gs):
    layout, total = _small_layout()
    keys = list(layout)

    def body(*refs):
        out = refs[-1]
        out[...] = jnp.zeros_like(out)
        for key, g_ref in zip(keys, refs[:-1]):
            off, rpad, at, r, cols, lane = layout[key]
            for j in range(-(-cols // LANES)):
                cw = min(LANES, cols - j * LANES)
                out[off + j * rpad + at:off + j * rpad + at + r, lane:lane + cw] = g_ref[:, j * LANES:j * LANES + cw]

    return pl.pallas_call(body, name="small_pack", out_shape=jax.ShapeDtypeStruct((total, LANES), F32),
                          compiler_params=_cparams(None, VMEM_MID))(*[gs[k] for k in keys])


def _adamw_small(red, chip, wts, ms, vs):
    layout, _ = _small_layout()
    names = [n for n, _, _ in _SMALL]
    n = len(names)

    def body(chip_ref, red_ref, *refs):
        ins, outs = refs[:3 * n], refs[3 * n:]
        c = chip_ref[0]
        for i, (name, (rows, cols), loc) in enumerate(_SMALL):
            w_ref, m_ref, v_ref = ins[3 * i:3 * i + 3]
            o_refs = outs[4 * i:4 * i + 4]
            width = cols if loc is None else loc
            for first, r in _PIECES.get(name, ((0, rows),)):
                off, rpad, at, _, _, lane = layout[name, first]
                for j in range(-(-width // LANES)):
                    cw = min(LANES, width - j * LANES)
                    ls = slice(lane, lane + cw)
                    if loc is None:
                        start = off + j * rpad + at
                        g = red_ref[start:start + r, ls]
                    else:
                        blk = c * (loc // LANES) + j
                        if r >= 8:
                            g = red_ref[pl.ds(pl.multiple_of(off + at + blk * rpad, 8), r), ls]
                        else:
                            tile = red_ref[pl.ds(pl.multiple_of(off + at // 8 * 8 + blk * rpad, 8), 8), ls]
                            g = tile[at % 8:at % 8 + r]
                    rs, cs = slice(first, first + r), slice(j * LANES, j * LANES + cw)
                    d, m2, v2 = _adam_math(w_ref[rs, cs], g, m_ref[rs, cs], v_ref[rs, cs])
                    for o, val in zip(o_refs, (g, d, m2, v2)):
                        o[rs, cs] = val

    args, shapes = [], []
    for name in names:
        args += [wts[name], ms[name], vs[name]]
        shapes += [jax.ShapeDtypeStruct(wts[name].shape, F32)] * 4
    vm = pl.BlockSpec(memory_space=pltpu.VMEM)
    res = pl.pallas_call(body, name="adamw_small",
                         in_specs=[pl.BlockSpec(memory_space=pltpu.SMEM), vm] + [vm] * (3 * n),
                         out_specs=[vm] * (4 * n), out_shape=shapes,
                         compiler_params=_cparams(None, VMEM_BIG))(chip, red, *args)
    return {name: res[4 * i:4 * i + 4] for i, name in enumerate(names)}


PACK_ROWS = 8


def _packed_rows(shape):
    size = 1
    for d in shape:
        size *= d
    return -(-size // (PACK_ROWS * LANES)) * PACK_ROWS


def _pack(arrs, row_mult):
    parts = []
    for a in arrs:
        flat = a.reshape(-1).astype(F32)
        rows = _packed_rows(a.shape)
        parts.append(jnp.pad(flat, (0, rows * LANES - flat.shape[0])).reshape(rows, LANES))
    total = sum(p.shape[0] for p in parts)
    fill = -(-total // row_mult) * row_mult - total
    if fill:
        parts.append(jnp.zeros((fill, LANES), F32))
    return jnp.concatenate(parts, axis=0)


def _unpack(packed, shapes):
    out, off = [], 0
    for s in shapes:
        rows = _packed_rows(s)
        size = 1
        for d in s:
            size *= d
        out.append(packed[off:off + rows].reshape(-1)[:size].reshape(s))
        off += rows
    return out


_SMALL_SH = ("e_w2", "e_a2", "e_g2", "e_conv_w", "o_norm_g", "o_D", "f_conv_w")
_LARGE = (("e_w_in", True), ("e_w_out", False), ("o_w_in", False), ("o_w_glu", True), ("f_w_up", True),
        ("f_w_down", False))
_ORDER = ("e_norm_g", "e_w_in", "e_mu", "e_w0", "e_w2", "e_a0", "e_a2", "e_g2", "e_k_k", "e_k_a", "e_r_k", "e_ln_w",
          "e_ln_b", "e_conv_w", "e_conv_b", "e_gate_a_w", "e_gate_a_b", "e_gate_x_w", "e_gate_x_b", "e_lru_lambda",
          "e_w_out", "o_norm_g", "o_w_in", "o_A_re", "o_A_im", "o_log_dt", "o_B_re", "o_B_im", "o_C_re", "o_C_im",
          "o_D", "o_w_glu", "f_norm_g", "f_w_up", "f_conv_w", "f_conv_b", "f_w_down", "final_norm_g")
N_CHIPS = 4
N_DEV = 8


def _step(x, tgt, wts, ms, vs):
    xi, yi, ci = _coords()
    chip = 2 * xi + yi
    chip1 = chip.astype(jnp.int32).reshape(1)
    me2 = jnp.stack([4 * xi + 2 * yi + ci, ci]).astype(jnp.int32)
    by_cols = dict(_LARGE)

    bufs = {(name, l): _cast_shard(wts[name], l, by_cols[name], chip1, f"cast_{name}{l}")
            for name, _ in _LARGE for l in range(wts[name].shape[0])}
    sh_shapes = [wts[n].shape for n in _SMALL_SH]
    packed = _pack([wts[n] for n in _SMALL_SH], 8)
    small_buf = lax.dynamic_update_slice(jnp.zeros((N_CHIPS,) + packed.shape, F32), packed[None], (chip, 0, 0))
    early = [("e_w_in", 0)]
    late = [k for k in bufs if k not in early]
    send, recv, thru, token = _gather_start([bufs[k] for k in early] + [small_buf], "gather_start_a", x)
    got = _gather_wait(thru, send, recv, "gather_wait_a", token)
    send_b, recv_b, thru_b, token = _gather_start([bufs[k] for k in late], "gather_start_b", got[0])
    x, _ = lax.optimization_barrier((x, token))

    def rows(g):
        return g.reshape(N_CHIPS * g.shape[1], g.shape[2])

    full = {n: wts[n] for n, _, loc in _SMALL if loc is None}
    full["e_w_in_t"] = rows(got[0])
    per_chip = [_unpack(got[1][k], sh_shapes) for k in range(N_CHIPS)]
    for i, n in enumerate(_SMALL_SH):
        full[n] = jnp.concatenate([per_chip[k][i] for k in range(N_CHIPS)], axis=-1)

    def late_weights(after):
        res = dict(zip(late, _gather_wait(thru_b, send_b, recv_b, "gather_wait_b", after)))
        return {"e_w_out": rows(res[("e_w_out", 0)]), "o_w_in": rows(res[("o_w_in", 0)]),
                "o_w_glu_t": rows(res[("o_w_glu", 0)]),
                "f_w_up_t": [rows(res[("f_w_up", l)]) for l in range(2)],
                "f_w_down": [rows(res[("f_w_down", l)]) for l in range(2)]}

    pending = []

    def send_grads(tag, items, carry):
        srcs = [g.reshape(N_DEV, g.shape[0] // N_DEV, g.shape[1]) for _, _, g in items]
        s_sem, r_sem, both, tok = _scatter_start(srcs, f"scatter_start_{tag}", carry)
        pending.append((tag, [(name, l) for name, l, _ in items], s_sem, r_sem, both))
        carry, _ = lax.optimization_barrier((carry, tok))
        return carry

    loss, grad_x, gs = _local_step(x, tgt, full, late_weights, send_grads)

    final = {}
    red = _allreduce_small(_small_pack(gs).reshape(N_DEV, -1, LANES)).reshape(-1, LANES)
    view = {name: (rows, cols if loc is None else loc) for name, (rows, cols), loc in _SMALL}
    as2d = lambda d: {name: d[name].reshape(view[name]) for name in view}
    small = _adamw_small(red, chip1, as2d(wts), as2d(ms), as2d(vs))
    for name, res in small.items():
        final[name] = [r.reshape(wts[name].shape) for r in res]
    new_v = small["final_norm_g"][3]

    halves, keys = [], []
    for tag, names, s_sem, r_sem, both in pending:
        srcs, lands = _scatter_wait(both, s_sem, r_sem, f"scatter_wait_{tag}", new_v)
        for (name, l), src, land in zip(names, srcs, lands):
            halves.append(_sum_segments(src, land, me2, f"sum_{name}{l}"))
            keys.append((name, l))
    shards = _exchange_sibling(halves)
    for s, (name, l) in zip(shards, keys):
        final[name] = _adamw_big(wts[name], ms[name], vs[name], l, s.reshape(2 * s.shape[1], s.shape[2]),
                                 by_cols[name], f"adamw_{name}{l}", prev=final.get(name))

    loss = lax.psum(loss[0, 0], ("x", "y", "c"))
    res = [loss, grad_x[None]]
    for k in range(4):
        res += [final[n][k] for n in _ORDER]
    return tuple(res)


def kernel(x, e_norm_g, e_w_in, e_mu, e_w0, e_w2, e_a0, e_a2, e_g2, e_k_k, e_k_a, e_r_k, e_ln_w, e_ln_b, e_conv_w, e_conv_b, e_gate_a_w, e_gate_a_b, e_gate_x_w, e_gate_x_b, e_lru_lambda, e_w_out, o_norm_g, o_w_in, o_A_re, o_A_im, o_log_dt, o_B_re, o_B_im, o_C_re, o_C_im, o_D, o_w_glu, f_norm_g, f_w_up, f_conv_w, f_conv_b, f_w_down, final_norm_g, loss_target, m_e_norm_g, m_e_w_in, m_e_mu, m_e_w0, m_e_w2, m_e_a0, m_e_a2, m_e_g2, m_e_k_k, m_e_k_a, m_e_r_k, m_e_ln_w, m_e_ln_b, m_e_conv_w, m_e_conv_b, m_e_gate_a_w, m_e_gate_a_b, m_e_gate_x_w, m_e_gate_x_b, m_e_lru_lambda, m_e_w_out, m_o_norm_g, m_o_w_in, m_o_A_re, m_o_A_im, m_o_log_dt, m_o_B_re, m_o_B_im, m_o_C_re, m_o_C_im, m_o_D, m_o_w_glu, m_f_norm_g, m_f_w_up, m_f_conv_w, m_f_conv_b, m_f_w_down, m_final_norm_g, v_e_norm_g, v_e_w_in, v_e_mu, v_e_w0, v_e_w2, v_e_a0, v_e_a2, v_e_g2, v_e_k_k, v_e_k_a, v_e_r_k, v_e_ln_w, v_e_ln_b, v_e_conv_w, v_e_conv_b, v_e_gate_a_w, v_e_gate_a_b, v_e_gate_x_w, v_e_gate_x_b, v_e_lru_lambda, v_e_w_out, v_o_norm_g, v_o_w_in, v_o_A_re, v_o_A_im, v_o_log_dt, v_o_B_re, v_o_B_im, v_o_C_re, v_o_C_im, v_o_D, v_o_w_glu, v_f_norm_g, v_f_w_up, v_f_conv_w, v_f_conv_b, v_f_w_down, v_final_norm_g):
    args = locals()
    wts = {n: args[n] for n in _ORDER}
    ms = {n: args["m_" + n] for n in _ORDER}
    vs = {n: args["v_" + n] for n in _ORDER}
    return _step(x[0], loss_target[0], wts, ms, vs)
```

```python
import functools

import jax
import jax.numpy as jnp
from jax import lax
from jax.experimental import pallas as pl
from jax.experimental.pallas import tpu as pltpu

F32 = jnp.float32
BF16 = jnp.bfloat16
MESH = pl.DeviceIdType.MESH

D_MODEL = 1024
HEAD = 64
RW = 512
N_HEADS = RW // HEAD
LRU_W = 512
SHIFT_COLS = 1792
W_LORA, A_LORA, G_LORA = 64, 64, 128
S5_GROUPS, S5_GROUP, S5_STATE = 64, 16, 64
D_FF = 2816
NORM_EPS = 1e-6
GN_EPS = 64e-5
LRU_C = 8.0
ADAM_LR, ADAM_B1, ADAM_B2, ADAM_EPS, ADAM_WD, ADAM_STEP = 0.001, 0.9, 0.999, 1e-08, 0.01, 10

VMEM_BIG = 56 * 1024 * 1024
VMEM_MID = 40 * 1024 * 1024
LANES = 128
PT = 16
WKV_CHUNK = 32
S5_SLAB = 128


def _blocked(*args, **kw):
    call = pl.pallas_call(*args, **kw)

    def run(*ops):
        return call(*[pltpu.with_memory_space_constraint(a, pltpu.HBM) if a.ndim >= 2 else a for a in ops])

    return run


def _cparams(sem=None, vmem=None):
    kw = {}
    if sem is not None:
        kw["dimension_semantics"] = sem
    if vmem is not None:
        kw["vmem_limit_bytes"] = vmem
    return pltpu.CompilerParams(**kw)


def _tile(dim, cands):
    for c in cands:
        if dim % c == 0:
            return c
    return dim


def _full(shape):
    n = len(shape)
    return pl.BlockSpec(shape, lambda *_: (0,) * n)


_TILES = (2816, 2048, 1408, 1024, 512, 256, 128)
MM_BUDGET = 36 * 1024 * 1024
VMEM_SLACK = 12 * 1024 * 1024


MXU_FLOPS = 9.0e14
HBM_BYTES = 3.3e12
STEP_SECONDS = 0.35e-6


def _mm_tiles(m, n, k, size_a, size_b, size_o, has_add, parts=1, tk_only=None, tm_max=None):
    best = None
    for tm in _TILES:
        for tk in _TILES:
            for tn in _TILES:
                if m % tm or n % tn or k % tk or (tk_only and tk != tk_only) or (tm_max and tm_max % tm):
                    continue
                need = (2 * (parts * tm * tk * size_a + tk * tn * size_b + tm * tn * size_o)
                        + tm * tn * 4 * (1 + 2 * has_add))
                if k > tk:
                    need += tm * tn * 4
                if need > MM_BUDGET:
                    continue
                steps = (m // tm) * (n // tn) * (k // tk)
                a_reads = n // tn if k > tk else 1
                moved = (m * k * size_a * a_reads + k * n * size_b * (m // tm) + m * n * (size_o + 4 * has_add))
                cost = max(2.0 * m * n * k / MXU_FLOPS, moved / HBM_BYTES) + steps * STEP_SECONDS
                cand = (-cost, tk, tm, tn)
                if best is None or cand > best[0]:
                    best = (cand, need)
    (_, tk, tm, tn), need = best
    return tm, tn, tk, need


def _matmul(a, b, mode, name, out_dtype=F32, add=None):
    parts = a if isinstance(a, tuple) else (a,)
    na = len(parts)
    wide = parts[0].shape[1]
    if mode == "nn":
        (m, k), (k2, n) = (parts[0].shape[0], na * wide), b.shape
    elif mode == "nt":
        (m, k), (n, k2) = (parts[0].shape[0], na * wide), b.shape
    else:
        (k, m), (k2, n) = (parts[0].shape[0], na * wide), b.shape
    assert k == k2, (parts[0].shape, b.shape, mode)
    split = {} if na == 1 else ({"tm_max": wide} if mode == "tn" else {"tk_only": wide})
    tm, tn, tk, need = _mm_tiles(m, n, k, parts[0].dtype.itemsize, b.dtype.itemsize, jnp.dtype(out_dtype).itemsize,
                                 add is not None, parts=na, **split)
    nk = k // tk
    per_part = wide // (tm if mode == "tn" else tk)
    dims = {"nn": (((1,), (0,)), ((), ())), "nt": (((1,), (1,)), ((), ())), "tn": (((0,), (0,)), ((), ()))}[mode]

    def body(*refs):
        a_refs, b_ref = refs[:na], refs[na]
        add_ref = refs[na + 1] if add is not None else None
        o_ref = refs[na + 2] if add is not None else refs[na + 1]
        kk = pl.program_id(2)

        def finish(r):
            if add_ref is not None:
                r = r + add_ref[...]
            o_ref[...] = r.astype(o_ref.dtype)

        def use(a_ref):
            part = lax.dot_general(a_ref[...].astype(BF16), b_ref[...].astype(BF16), dims, preferred_element_type=F32)
            if nk == 1:
                finish(part)
                return
            acc = refs[-1]

            @pl.when(kk == 0)
            def _():
                acc[...] = part

            @pl.when(kk > 0)
            def _():
                acc[...] += part

            @pl.when(kk == nk - 1)
            def _():
                finish(acc[...])

        if na == 1:
            use(a_refs[0])
        else:
            which = (pl.program_id(0) if mode == "tn" else kk) // per_part
            for p in range(na):
                pl.when(which == p)(functools.partial(use, a_refs[p]))

    def a_spec(p):
        def along(pos):
            return jnp.clip(pos - p * per_part, 0, per_part - 1) if na > 1 else pos
        if mode == "tn":
            return pl.BlockSpec((tk, tm), lambda i, j, kk: (kk, along(i)))
        return pl.BlockSpec((tm, tk), lambda i, j, kk: (i, along(kk)))

    if mode == "nn":
        b_spec = pl.BlockSpec((tk, tn), lambda i, j, kk: (kk, j))
    elif mode == "nt":
        b_spec = pl.BlockSpec((tn, tk), lambda i, j, kk: (j, kk))
    else:
        b_spec = pl.BlockSpec((tk, tn), lambda i, j, kk: (kk, j))
    o_spec = pl.BlockSpec((tm, tn), lambda i, j, kk: (i, j))
    in_specs = [a_spec(p) for p in range(na)] + [b_spec] + ([o_spec] if add is not None else [])
    args = parts + (b,) + ((add,) if add is not None else ())
    return _blocked(
        body, name=name, grid=(m // tm, n // tn, nk),
        in_specs=in_specs, out_specs=o_spec,
        out_shape=jax.ShapeDtypeStruct((m, n), out_dtype),
        scratch_shapes=[pltpu.VMEM((tm, tn), F32)] if nk > 1 else [],
        compiler_params=_cparams(("parallel", "parallel", "arbitrary"), min(VMEM_BIG, need + VMEM_SLACK)),
    )(*args)


TOK = 256
ROWS = 512


def _rms(x, g):
    return x * lax.rsqrt(jnp.mean(x * x, axis=-1, keepdims=True) + NORM_EPS) * g


def _rms_fwd(x, g, name):
    t, d = x.shape

    def body(x_ref, g_ref, o_ref):
        o_ref[...] = _rms(x_ref[...], g_ref[...]).astype(BF16)

    row = pl.BlockSpec((ROWS, d), lambda i: (i, 0))
    return _blocked(body, name=name, grid=(t // ROWS,), in_specs=[row, _full((1, d))], out_specs=row,
                          out_shape=jax.ShapeDtypeStruct((t, d), BF16),
                          compiler_params=_cparams(("parallel",), VMEM_MID))(x, g)


def _rms_bwd(x, g, dxn, res, name):
    t, d = x.shape

    def body(x_ref, g_ref, d_ref, res_ref, dx_ref, dg_ref):
        _, vjp = jax.vjp(_rms, x_ref[...], g_ref[...])
        dx, dg = vjp(d_ref[...].astype(F32))
        dx_ref[...] = dx + res_ref[...]

        @pl.when(pl.program_id(0) == 0)
        def _():
            dg_ref[...] = jnp.zeros_like(dg_ref)

        dg_ref[...] += dg

    row = pl.BlockSpec((ROWS, d), lambda i: (i, 0))
    return _blocked(body, name=name, grid=(t // ROWS,), in_specs=[row, _full((1, d)), row, row],
                          out_specs=[row, _full((1, d))],
                          out_shape=[jax.ShapeDtypeStruct((t, d), F32), jax.ShapeDtypeStruct((1, d), F32)],
                          compiler_params=_cparams(("arbitrary",), VMEM_MID))(x, g, dxn, res)


def _loss_head(x, g, tgt):
    t, d = x.shape

    def body(x_ref, g_ref, t_ref, l_ref, dx_ref, dg_ref):
        tg = t_ref[...]

        def fn(xv, gv):
            err = _rms(xv, gv) - tg
            per_tok = jnp.mean(err * err, axis=-1, keepdims=True)
            return 0.5 * jnp.sum(per_tok, axis=0, keepdims=True)

        l, vjp = jax.vjp(fn, x_ref[...], g_ref[...])
        dx, dg = vjp(jnp.ones((1, 1), F32))
        dx_ref[...] = dx

        @pl.when(pl.program_id(0) == 0)
        def _():
            dg_ref[...] = jnp.zeros_like(dg_ref)
            l_ref[...] = jnp.zeros_like(l_ref)

        dg_ref[...] += dg
        l_ref[...] += jnp.broadcast_to(l, l_ref.shape)

    row = pl.BlockSpec((ROWS, d), lambda i: (i, 0))
    return _blocked(body, name="loss_head", grid=(t // ROWS,), in_specs=[row, _full((1, d)), row],
                          out_specs=[_full((1, LANES)), row, _full((1, d))],
                          out_shape=[jax.ShapeDtypeStruct((1, LANES), F32), jax.ShapeDtypeStruct((t, d), F32),
                                     jax.ShapeDtypeStruct((1, d), F32)],
                          compiler_params=_cparams(("arbitrary",), VMEM_MID))(x, g, tgt)


def _glu_fwd(x, z):
    t, d = x.shape

    def body(x_ref, v_ref, g_ref, o_ref):
        o_ref[...] = x_ref[...] + v_ref[...] * jax.nn.sigmoid(g_ref[...])

    row = pl.BlockSpec((ROWS, d), lambda i: (i, 0))
    gate = pl.BlockSpec((ROWS, d), lambda i: (i, 1))
    return _blocked(body, name="glu_fwd", grid=(t // ROWS,), in_specs=[row, row, gate], out_specs=row,
                          out_shape=jax.ShapeDtypeStruct((t, d), F32),
                          compiler_params=_cparams(("parallel",), VMEM_MID))(x, z, z)


def _glu_bwd(z, g):
    t, d = g.shape

    def body(v_ref, g_ref, d_ref, o_ref):
        s = jax.nn.sigmoid(g_ref[...])
        dy = d_ref[...]
        o_ref[:, :d] = (dy * s).astype(BF16)
        o_ref[:, d:] = (dy * v_ref[...] * s * (1.0 - s)).astype(BF16)

    row = pl.BlockSpec((ROWS, d), lambda i: (i, 0))
    gate = pl.BlockSpec((ROWS, d), lambda i: (i, 1))
    return _blocked(body, name="glu_bwd", grid=(t // ROWS,), in_specs=[row, gate, row],
                          out_specs=pl.BlockSpec((ROWS, 2 * d), lambda i: (i, 0)),
                          out_shape=jax.ShapeDtypeStruct((t, 2 * d), BF16),
                          compiler_params=_cparams(("parallel",), VMEM_MID))(z, z, g)


def _shift_down(x, d):
    row = lax.broadcasted_iota(jnp.int32, x.shape, 0)
    return jnp.where(row < d, 0.0, pltpu.roll(x, d, 0))


def _shift_up(x, d):
    n = x.shape[0]
    row = lax.broadcasted_iota(jnp.int32, x.shape, 0)
    return jnp.where(row >= n - d, 0.0, pltpu.roll(x, n - d, 0))


def _make_sd():
    @functools.partial(jax.custom_vjp, nondiff_argnums=(1,))
    def sd(x, d):
        return _shift_down(x, d)

    def fwd(x, d):
        return _shift_down(x, d), None

    def bwd(d, _, g):
        return (_shift_up(g, d),)

    sd.defvjp(fwd, bwd)
    return sd


def _lin_scan(a, u, reverse=False):
    n = a.shape[0]
    row = lax.broadcasted_iota(jnp.int32, a.shape, 0)
    d = 1
    while d < n:
        if reverse:
            keep = row < n - d
            a_s, u_s = pltpu.roll(a, n - d, 0), pltpu.roll(u, n - d, 0)
        else:
            keep = row >= d
            a_s, u_s = pltpu.roll(a, d, 0), pltpu.roll(u, d, 0)
        u = u + a * jnp.where(keep, u_s, 0.0)
        a = a * jnp.where(keep, a_s, 1.0)
        d *= 2
    return u


def _make_scan():
    @jax.custom_vjp
    def scan(a, u):
        return _lin_scan(a, u)

    def fwd(a, u):
        h = _lin_scan(a, u)
        return h, (a, h)

    def bwd(res, dh):
        a, h = res
        g = _lin_scan(_shift_up(a, 1), dh, reverse=True)
        return g * _shift_down(h, 1), g

    scan.defvjp(fwd, bwd)
    return scan


def _acc_out(ref, val):
    @pl.when(pl.program_id(0) == 0)
    def _():
        ref[...] = jnp.zeros_like(ref)

    ref[...] += val


FFN_CW = 128


def _ffn_fn(hg, hv, wg, wv, bg, bv, sd):
    cg = wg[0:1] * sd(hg, 2) + wg[1:2] * sd(hg, 1) + wg[2:3] * hg + bg
    cv = wv[0:1] * sd(hv, 2) + wv[1:2] * sd(hv, 1) + wv[2:3] * hv + bv
    return jax.nn.silu(cg) * cv


def _ffn_specs(t):
    nb = D_FF // FFN_CW
    col = lambda r, off: pl.BlockSpec((r, FFN_CW), lambda j: (0, j + off))
    return nb, [col(t, 0), col(t, nb), col(3, 0), col(3, nb), col(1, 0), col(1, nb)], col


def _ffn_mid_fwd(h, cw, cb, name):
    t = h.shape[0]
    nb, in_specs, col = _ffn_specs(t)

    def body(hg, hv, wg, wv, bg, bv, o_ref):
        o_ref[...] = _ffn_fn(hg[...], hv[...], wg[...], wv[...], bg[...], bv[...], _shift_down).astype(BF16)

    return _blocked(body, name=name, grid=(nb,), in_specs=in_specs, out_specs=col(t, 0),
                          out_shape=jax.ShapeDtypeStruct((t, D_FF), BF16),
                          compiler_params=_cparams(("parallel",), VMEM_MID))(h, h, cw, cw, cb, cb)


def _ffn_mid_bwd(h, cw, cb, dact, name):
    t = h.shape[0]
    nb, in_specs, col = _ffn_specs(t)

    def body(hg, hv, wg, wv, bg, bv, d_ref, dhg, dhv, dwg, dwv, dbg, dbv):
        fn = functools.partial(_ffn_fn, sd=_make_sd())
        _, vjp = jax.vjp(fn, hg[...], hv[...], wg[...], wv[...], bg[...], bv[...])
        g = vjp(d_ref[...])
        dhg[...] = g[0].astype(BF16)
        dhv[...] = g[1].astype(BF16)
        dwg[...], dwv[...], dbg[...], dbv[...] = g[2], g[3], g[4], g[5]

    big = jax.ShapeDtypeStruct((t, D_FF), BF16)
    w3 = jax.ShapeDtypeStruct((3, D_FF), F32)
    b1 = jax.ShapeDtypeStruct((1, D_FF), F32)
    return _blocked(body, name=name, grid=(nb,), in_specs=in_specs + [col(t, 0)],
                          out_specs=[col(t, 0), col(t, 0), col(3, 0), col(3, 0), col(1, 0), col(1, 0)],
                          out_shape=[big, big, w3, w3, b1, b1],
                          compiler_params=_cparams(("parallel",), VMEM_BIG))(h, h, cw, cw, cb, cb, dact)


TS_CW = 256


def _tshift_fn(p, mu, sd):
    return p + mu * (sd(p, 1) - p)


def _tshift_fwd(p, mu):
    t = p.shape[0]
    col = lambda r: pl.BlockSpec((r, TS_CW), lambda j: (0, j))

    def body(p_ref, mu_ref, o_ref):
        o_ref[...] = _tshift_fn(p_ref[...], mu_ref[...], _shift_down)

    return _blocked(body, name="tshift_fwd", grid=(SHIFT_COLS // TS_CW,), in_specs=[col(t), col(1)],
                          out_specs=col(t), out_shape=jax.ShapeDtypeStruct((t, SHIFT_COLS), F32),
                          compiler_params=_cparams(("parallel",), VMEM_MID))(p, mu)


def _tshift_bwd(p, mu, dpam):
    t = p.shape[0]
    col = lambda r: pl.BlockSpec((r, TS_CW), lambda j: (0, j))

    def body(p_ref, mu_ref, d_ref, dp_ref, dmu_ref):
        _, vjp = jax.vjp(functools.partial(_tshift_fn, sd=_make_sd()), p_ref[...], mu_ref[...])
        dp, dmu = vjp(d_ref[...])
        dp_ref[...] = dp.astype(BF16)
        dmu_ref[...] = dmu

    return _blocked(body, name="tshift_bwd", grid=(SHIFT_COLS // TS_CW,), in_specs=[col(t), col(1), col(t)],
                          out_specs=[col(t), col(1)],
                          out_shape=[jax.ShapeDtypeStruct((t, SHIFT_COLS), BF16),
                                     jax.ShapeDtypeStruct((1, SHIFT_COLS), F32)],
                          compiler_params=_cparams(("parallel",), VMEM_MID))(p, mu, dpam)


_HI = lax.Precision.HIGHEST
_O = (0, RW, 2 * RW, 3 * RW, 3 * RW + W_LORA, 3 * RW + W_LORA + A_LORA, SHIFT_COLS)


def _dot16(a, b, dims=(((1,), (0,)), ((), ()))):
    return lax.dot_general(a.astype(BF16), b.astype(BF16), dims, preferred_element_type=F32)


def _make_dot16():
    @jax.custom_vjp
    def dot(a, b):
        return _dot16(a, b)

    def fwd(a, b):
        return _dot16(a, b), (a, b)

    def bwd(res, g):
        a, b = res
        return _dot16(g, b, (((1,), (1,)), ((), ()))), _dot16(a, g, (((0,), (0,)), ((), ())))

    dot.defvjp(fwd, bwd)
    return dot


def _seg(x):
    first = lax.broadcasted_iota(jnp.int32, (x.shape[0], LANES), 1) < HEAD
    parts = []
    for p in range(x.shape[1] // LANES):
        xp = x[:, p * LANES:(p + 1) * LANES]
        s0 = jnp.sum(jnp.where(first, xp, 0.0), axis=-1, keepdims=True)
        s1 = jnp.sum(jnp.where(first, 0.0, xp), axis=-1, keepdims=True)
        parts.append(jnp.where(first, s0, s1))
    return jnp.concatenate(parts, axis=1)


def _prep_fn(r, k, v, wd, ad, gd, w0, w2, a0, a2, g2, k_k, k_a, dot):
    w_log = -jax.nn.softplus(-(w0 + dot(jnp.tanh(wd), w2))) - 0.5
    decay = jnp.exp(-jnp.exp(w_log))
    a = jax.nn.sigmoid(a0 + dot(ad, a2))
    g = dot(jax.nn.sigmoid(gd), g2)
    kk = k * k_k
    kk = kk / jnp.maximum(jnp.sqrt(_seg(kk * kk)), 1e-12)
    k2 = k * (1.0 + (a - 1.0) * k_a)
    return r, decay, k2, v, -kk, kk * a, g


_PREP_W = ("w0", "w2", "a0", "a2", "g2", "k_k", "k_a")


def _prep_wspecs(w):
    return [_full(w[n].shape) for n in _PREP_W]


def _rwkv_prep_fwd(pam, w):
    t = pam.shape[0]

    def body(p_ref, *refs):
        wr, outs = refs[:7], refs[7:]
        pieces = [p_ref[:, _O[i]:_O[i + 1]] for i in range(6)]
        res = _prep_fn(*pieces, *[x[...] for x in wr], _dot16)
        for o, val in zip(outs, res):
            o[...] = val

    row = lambda c: pl.BlockSpec((TOK, c), lambda i: (i, 0))
    return _blocked(body, name="rwkv_prep_fwd", grid=(t // TOK,),
                          in_specs=[row(SHIFT_COLS)] + _prep_wspecs(w), out_specs=[row(RW)] * 7,
                          out_shape=[jax.ShapeDtypeStruct((t, RW), F32)] * 7,
                          compiler_params=_cparams(("parallel",), VMEM_MID))(pam, *[w[n] for n in _PREP_W])


def _rwkv_prep_bwd(pam, w, cts, more):
    t = pam.shape[0]

    def body(p_ref, *refs):
        wr, ct, ex, dp_ref, dws = refs[:7], refs[7:14], refs[14:17], refs[17], refs[18:]
        pieces = [p_ref[:, _O[i]:_O[i + 1]] for i in range(6)]
        fn = lambda *a: _prep_fn(*a, _make_dot16())
        _, vjp = jax.vjp(fn, *pieces, *[x[...] for x in wr])
        c = [x[...] for x in ct]
        c[0] = c[0] + ex[0][...]
        c[2] = c[2] + ex[1][...]
        c[3] = c[3] + ex[2][...]
        g = vjp(tuple(c))
        for i in range(6):
            dp_ref[:, _O[i]:_O[i + 1]] = g[i]
        for o, val in zip(dws, g[6:]):
            _acc_out(o, val)

    row = lambda c: pl.BlockSpec((TOK, c), lambda i: (i, 0))
    return _blocked(body, name="rwkv_prep_bwd", grid=(t // TOK,),
                          in_specs=[row(SHIFT_COLS)] + _prep_wspecs(w) + [row(RW)] * 10,
                          out_specs=[row(SHIFT_COLS)] + [_full(w[n].shape) for n in _PREP_W],
                          out_shape=[jax.ShapeDtypeStruct((t, SHIFT_COLS), F32)]
                          + [jax.ShapeDtypeStruct(w[n].shape, F32) for n in _PREP_W],
                          compiler_params=_cparams(("arbitrary",), VMEM_MID))(
                              pam, *[w[n] for n in _PREP_W], *cts, *more)


def _post_fn(y, r, k2, v, g, ln_w, ln_b, r_k):
    inv = 1.0 / HEAD
    d = y - _seg(y) * inv
    yn = d * lax.rsqrt(_seg(d * d) * inv + GN_EPS) * ln_w + ln_b
    bonus = _seg(r * k2 * r_k) * v
    return (yn + bonus) * g


def _rwkv_post_fwd(y, r, k2, v, g, ln_w, ln_b, r_k):
    t = y.shape[0]

    def body(*refs):
        o_ref = refs[-1]
        o_ref[...] = _post_fn(*[x[...] for x in refs[:-1]]).astype(BF16)

    row = pl.BlockSpec((TOK, RW), lambda i: (i, 0))
    return _blocked(body, name="rwkv_post_fwd", grid=(t // TOK,),
                          in_specs=[row] * 5 + [_full((1, RW))] * 3, out_specs=row,
                          out_shape=jax.ShapeDtypeStruct((t, RW), BF16),
                          compiler_params=_cparams(("parallel",), VMEM_MID))(y, r, k2, v, g, ln_w, ln_b, r_k)


def _rwkv_post_bwd(y, r, k2, v, g, ln_w, ln_b, r_k, dya):
    t = y.shape[0]

    def body(*refs):
        ins, d_ref, outs = refs[:8], refs[8], refs[9:]
        _, vjp = jax.vjp(_post_fn, *[x[...] for x in ins])
        gr = vjp(d_ref[...])
        for o, val in zip(outs[:5], gr[:5]):
            o[...] = val
        for o, val in zip(outs[5:], gr[5:]):
            _acc_out(o, val)

    row = pl.BlockSpec((TOK, RW), lambda i: (i, 0))
    vec = _full((1, RW))
    return _blocked(body, name="rwkv_post_bwd", grid=(t // TOK,),
                          in_specs=[row] * 5 + [vec] * 3 + [row],
                          out_specs=[row] * 5 + [vec] * 3,
                          out_shape=[jax.ShapeDtypeStruct((t, RW), F32)] * 5 + [jax.ShapeDtypeStruct((1, RW), F32)] * 3,
                          compiler_params=_cparams(("arbitrary",), VMEM_MID))(y, r, k2, v, g, ln_w, ln_b, r_k, dya)


def _from_pt(x):
    n = x.shape[0]
    return x.reshape(n, HEAD, N_HEADS, PT).transpose(0, 3, 2, 1).reshape(n * PT, N_HEADS * HEAD)


def _lane_sum(x):
    return jnp.sum(x, axis=-1, keepdims=True)


def _pair_consts():
    lane = lax.broadcasted_iota(jnp.int32, (HEAD, LANES), 1)
    return lane, lane < HEAD


def _seg_sum_pair(x, first):
    return jnp.where(first, _lane_sum(jnp.where(first, x, 0.0)), _lane_sum(jnp.where(first, 0.0, x)))


def _to_pt(x):
    t = x.shape[0]
    return x.reshape(t // PT, PT, N_HEADS, HEAD).transpose(0, 3, 2, 1).reshape(t // PT, HEAD, N_HEADS * PT)


def _expand_cols(x, name):
    t = x.shape[0]
    tiles = WKV_CHUNK // PT

    def body(x_ref, o_ref):
        _, first = _pair_consts()
        for tl in range(tiles):
            tile = x_ref[tl]
            for j in range(PT):
                for p in range(N_HEADS // 2):
                    src = jnp.where(first, (2 * p) * PT + j, (2 * p + 1) * PT + j)
                    o_ref[tl * PT + j, :, p * LANES:(p + 1) * LANES] = jnp.take_along_axis(tile, src, axis=1)

    return _blocked(
        body, name=name, grid=(t // WKV_CHUNK,),
        in_specs=[pl.BlockSpec((tiles, HEAD, LANES), lambda i: (i, 0, 0))],
        out_specs=pl.BlockSpec((WKV_CHUNK, HEAD, RW), lambda i: (i, 0, 0)),
        out_shape=jax.ShapeDtypeStruct((t, HEAD, RW), F32),
        compiler_params=_cparams(("parallel",), VMEM_MID))(_to_pt(x))


def _wkv_fwd(w, k, z, b, v_exp):
    t = w.shape[0]
    nc = t // WKV_CHUNK
    pairs = N_HEADS // 2

    def body(w_ref, k_ref, z_ref, b_ref, v_ref, s_all, s_ref):
        @pl.when(pl.program_id(0) == 0)
        def _():
            s_ref[...] = jnp.zeros_like(s_ref)

        _, first = _pair_consts()

        def group(gi, carry):
            base = pl.multiple_of(gi * 8, 8)
            rows = [ref[pl.ds(base, 8), :] for ref in (w_ref, k_ref, z_ref, b_ref)]
            s = [s_ref[:, p * LANES:(p + 1) * LANES] for p in range(pairs)]
            for jj in range(8):
                for p in range(pairs):
                    cs = slice(p * LANES, (p + 1) * LANES)
                    wr, kr, zr, br = [x[jj:jj + 1, cs] for x in rows]
                    s_all[base + jj, :, cs] = s[p]
                    sa = _seg_sum_pair(s[p] * zr, first)
                    s[p] = s[p] * wr + sa * br + v_ref[base + jj, :, cs] * kr
            for p in range(pairs):
                s_ref[:, p * LANES:(p + 1) * LANES] = s[p]
            return carry

        lax.fori_loop(0, WKV_CHUNK // 8, group, 0)

    row = pl.BlockSpec((WKV_CHUNK, RW), lambda i: (i, 0))
    big = pl.BlockSpec((WKV_CHUNK, HEAD, RW), lambda i: (i, 0, 0))
    return _blocked(
        body, name="wkv_fwd", grid=(nc,), in_specs=[row] * 4 + [big], out_specs=[big, _full((HEAD, RW))],
        out_shape=[jax.ShapeDtypeStruct((t, HEAD, RW), F32), jax.ShapeDtypeStruct((HEAD, RW), F32)],
        compiler_params=_cparams(("arbitrary",), VMEM_MID))(w, k, z, b, v_exp)


def _wkv_out(r, s_all, s_last):
    t = r.shape[0]
    nc = t // WKV_CHUNK
    tiles = WKV_CHUNK // PT
    pairs = N_HEADS // 2

    def body(r_ref, s_ref, nxt_ref, last_ref, y_ref):
        lane, first = _pair_consts()
        after = jnp.where(pl.program_id(0) == nc - 1, last_ref[...], nxt_ref[0])
        for tl in range(tiles):
            ytile = jnp.zeros((HEAD, LANES), F32)
            for g in range(PT // 8):
                rows = r_ref[tl * PT + g * 8:tl * PT + g * 8 + 8, :]
                for jj in range(8):
                    tt = tl * PT + g * 8 + jj
                    j = g * 8 + jj
                    for p in range(pairs):
                        cs = slice(p * LANES, (p + 1) * LANES)
                        s = s_ref[tt + 1, :, cs] if tt + 1 < WKV_CHUNK else after[:, cs]
                        pr = s * rows[jj:jj + 1, cs]
                        y0 = _lane_sum(jnp.where(first, pr, 0.0))
                        y1 = _lane_sum(jnp.where(first, 0.0, pr))
                        ytile = jnp.where(lane == (2 * p) * PT + j, y0, ytile)
                        ytile = jnp.where(lane == (2 * p + 1) * PT + j, y1, ytile)
            y_ref[tl] = ytile

    row = pl.BlockSpec((WKV_CHUNK, RW), lambda i: (i, 0))
    pt = pl.BlockSpec((tiles, HEAD, LANES), lambda i: (i, 0, 0))
    big = pl.BlockSpec((WKV_CHUNK, HEAD, RW), lambda i: (i, 0, 0))
    nxt = pl.BlockSpec((1, HEAD, RW), lambda i: (jnp.minimum((i + 1) * WKV_CHUNK, t - 1), 0, 0))
    return _blocked(
        body, name="wkv_out", grid=(nc,), in_specs=[row, big, nxt, _full((HEAD, RW))], out_specs=pt,
        out_shape=jax.ShapeDtypeStruct((t // PT, HEAD, LANES), F32),
        compiler_params=_cparams(("parallel",), VMEM_MID))(r, s_all, s_all, s_last)


def _wkv_bwd(r, w, k, z, b, v_exp, s_all, dy_exp):
    t = r.shape[0]
    nc = t // WKV_CHUNK
    tiles = WKV_CHUNK // PT
    pairs = N_HEADS // 2

    def body(r_ref, w_ref, k_ref, z_ref, b_ref, v_ref, s_all_ref, dy_ref,
             dr_ref, dw_ref, dk_ref, dz_ref, db_ref, dv_ref, ds_ref):
        @pl.when(pl.program_id(0) == 0)
        def _():
            ds_ref[...] = jnp.zeros_like(ds_ref)

        lane, first = _pair_consts()
        col_sum = lambda x: jnp.sum(x, axis=0, keepdims=True)
        row8 = lax.broadcasted_iota(jnp.int32, (8, LANES), 0)
        for tl in reversed(range(tiles)):
            def group(gg, dvtile):
                gi = PT // 8 - 1 - gg
                base = pl.multiple_of(tl * PT + gi * 8, 8)
                rows = [ref[pl.ds(base, 8), :] for ref in (r_ref, w_ref, k_ref, z_ref, b_ref)]
                outs = (dr_ref, dw_ref, dk_ref, dz_ref, db_ref)
                tiles8 = {(id(o), p): jnp.zeros((8, LANES), F32) for o in outs for p in range(pairs)}
                ds = [ds_ref[:, p * LANES:(p + 1) * LANES] for p in range(pairs)]
                for jj in reversed(range(8)):
                    j = gi * 8 + jj
                    for p in range(pairs):
                        cs = slice(p * LANES, (p + 1) * LANES)

                        def put(ref, val, p=p, jj=jj):
                            tiles8[(id(ref), p)] = jnp.where(row8 == jj, val, tiles8[(id(ref), p)])

                        rr, wr, kr, zr, br = [x[jj:jj + 1, cs] for x in rows]
                        sp = s_all_ref[base + jj, :, cs]
                        vc = v_ref[base + jj, :, cs]
                        dyc = dy_ref[base + jj, :, cs]
                        sa = _seg_sum_pair(sp * zr, first)
                        st = sp * wr + sa * br + vc * kr
                        d = ds[p] + dyc * rr
                        put(dr_ref, col_sum(st * dyc))
                        dvk = d * kr
                        dv0 = _lane_sum(jnp.where(first, dvk, 0.0))
                        dv1 = _lane_sum(jnp.where(first, 0.0, dvk))
                        dvtile = jnp.where(lane == (2 * p) * PT + j, dv0, dvtile)
                        dvtile = jnp.where(lane == (2 * p + 1) * PT + j, dv1, dvtile)
                        put(dk_ref, col_sum(d * vc))
                        put(dw_ref, col_sum(sp * d))
                        u = _seg_sum_pair(d * br, first)
                        put(dz_ref, col_sum(sp * u))
                        put(db_ref, col_sum(d * sa))
                        ds[p] = d * wr + u * zr
                for p in range(pairs):
                    ds_ref[:, p * LANES:(p + 1) * LANES] = ds[p]
                for o in outs:
                    for p in range(pairs):
                        o[pl.ds(base, 8), p * LANES:(p + 1) * LANES] = tiles8[(id(o), p)]
                return dvtile

            dv_ref[tl] = lax.fori_loop(0, PT // 8, group, jnp.zeros((HEAD, LANES), F32))

    rev = lambda i: nc - 1 - i
    row = pl.BlockSpec((WKV_CHUNK, RW), lambda i: (rev(i), 0))
    pt = pl.BlockSpec((tiles, HEAD, LANES), lambda i: (rev(i), 0, 0))
    big = pl.BlockSpec((WKV_CHUNK, HEAD, RW), lambda i: (rev(i), 0, 0))
    return _blocked(
        body, name="wkv_bwd", grid=(nc,), in_specs=[row] * 5 + [big, big, big], out_specs=[row] * 5 + [pt],
        out_shape=[jax.ShapeDtypeStruct((t, RW), F32)] * 5 + [jax.ShapeDtypeStruct((t // PT, HEAD, LANES), F32)],
        scratch_shapes=[pltpu.VMEM((HEAD, RW), F32)],
        compiler_params=_cparams(("arbitrary",), VMEM_BIG))(r, w, k, z, b, v_exp, s_all, dy_exp)


LRU_CW = 128
_BX0 = SHIFT_COLS // LRU_CW
_BG0 = (SHIFT_COLS + LRU_W) // LRU_CW


def _lru_fn(bx, bg, cw, cb, ga, ba, gx, bxb, lam, sd, scan, dot):
    xc = cw[0:1] * sd(bx, 3) + cw[1:2] * sd(bx, 2) + cw[2:3] * sd(bx, 1) + cw[3:4] * bx + cb
    gr = jax.nn.sigmoid(dot(xc, ga) + ba)
    gi = jax.nn.sigmoid(dot(xc, gx) + bxb)
    log_a = -LRU_C * gr * jax.nn.softplus(-lam)
    a = jnp.exp(log_a)
    mult = jnp.sqrt(-jnp.tanh(log_a) * (jnp.exp(2.0 * log_a) + 1.0))
    return scan(a, xc * gi * mult) * jax.nn.gelu(bg)


def _lru_specs(t):
    col = lambda r, off=0: pl.BlockSpec((r, LRU_CW), lambda j: (0, j + off))
    diag = pl.BlockSpec((LRU_CW, LRU_CW), lambda j: (j, j))
    return col, [col(t, _BX0), col(t, _BG0), col(4), col(1), diag, col(1), diag, col(1), col(1)]


def _lru_fwd(p, cw, cb, ga, ba, gx, bxb, lam):
    t = p.shape[0]
    col, in_specs = _lru_specs(t)

    def body(*refs):
        o_ref = refs[-1]
        o_ref[...] = _lru_fn(*[x[...] for x in refs[:-1]], _shift_down, _lin_scan, _dot16).astype(BF16)

    return _blocked(body, name="lru_fwd", grid=(LRU_W // LRU_CW,), in_specs=in_specs, out_specs=col(t),
                          out_shape=jax.ShapeDtypeStruct((t, LRU_W), BF16),
                          compiler_params=_cparams(("parallel",), VMEM_MID))(p, p, cw, cb, ga, ba, gx, bxb, lam)


def _lru_bwd(p, cw, cb, ga, ba, gx, bxb, lam, dyb):
    t = p.shape[0]
    col, in_specs = _lru_specs(t)

    def body(*refs):
        ins, d_ref, outs = refs[:9], refs[9], refs[10:]
        fn = functools.partial(_lru_fn, sd=_make_sd(), scan=_make_scan(), dot=_make_dot16())
        _, vjp = jax.vjp(fn, *[x[...] for x in ins])
        g = vjp(d_ref[...])
        outs[0][...] = g[0].astype(BF16)
        outs[1][...] = g[1].astype(BF16)
        for o, val in zip(outs[2:], g[2:]):
            o[...] = val

    sq = pl.BlockSpec((LRU_CW, LRU_CW), lambda j: (j, 0))
    act = jax.ShapeDtypeStruct((t, LRU_W), BF16)
    vec = jax.ShapeDtypeStruct((1, LRU_W), F32)
    sqs = jax.ShapeDtypeStruct((LRU_W, LRU_CW), F32)
    return _blocked(body, name="lru_bwd", grid=(LRU_W // LRU_CW,), in_specs=in_specs + [col(t, RW // LRU_CW)],
                          out_specs=[col(t), col(t), col(4), col(1), sq, col(1), sq, col(1), col(1)],
                          out_shape=[act, act, jax.ShapeDtypeStruct((4, LRU_W), F32), vec, sqs, vec, sqs, vec, vec],
                          compiler_params=_cparams(("parallel",), VMEM_BIG))(p, p, cw, cb, ga, ba, gx, bxb, lam, dyb)


def _s5_disc_fn(a_re, a_im, log_dt, b_re, b_im, e):
    lam_re = jnp.minimum(a_re, -1e-4)
    lam_im = a_im
    dt = jnp.exp(log_dt)
    mag = jnp.exp(lam_re * dt)
    ab_re = mag * jnp.cos(lam_im * dt)
    ab_im = mag * jnp.sin(lam_im * dt)
    den = lam_re * lam_re + lam_im * lam_im
    zr = ab_re - 1.0
    q_re = jnp.dot((zr * lam_re + ab_im * lam_im) / den, e, precision=_HI)
    q_im = jnp.dot((ab_im * lam_re - zr * lam_im) / den, e, precision=_HI)
    return ab_re, ab_im, q_re * b_re - q_im * b_im, q_re * b_im + q_im * b_re


def _s5_disc_fwd(a_re, a_im, log_dt, b_re, b_im, e):
    def body(*refs):
        res = _s5_disc_fn(*[x[...] for x in refs[:6]])
        for o, val in zip(refs[6:], res):
            o[...] = val

    small = jax.ShapeDtypeStruct(a_re.shape, F32)
    wide = jax.ShapeDtypeStruct(b_re.shape, F32)
    return pl.pallas_call(body, name="s5_disc_fwd", out_shape=[small, small, wide, wide])(
        a_re, a_im, log_dt, b_re, b_im, e)


def _s5_disc_bwd(a_re, a_im, log_dt, b_re, b_im, e, cts):
    def body(*refs):
        ins, e_ref, ct, outs = refs[:5], refs[5], refs[6:10], refs[10:]
        _, vjp = jax.vjp(lambda *a: _s5_disc_fn(*a, e_ref[...]), *[x[...] for x in ins])
        for o, val in zip(outs, vjp(tuple(c[...] for c in ct))):
            o[...] = val

    shapes = [jax.ShapeDtypeStruct(x.shape, F32) for x in (a_re, a_im, log_dt, b_re, b_im)]
    return pl.pallas_call(body, name="s5_disc_bwd", out_shape=shapes)(a_re, a_im, log_dt, b_re, b_im, e, *cts)


def _cmul(a, b):
    return a[0] * b[0] - a[1] * b[1], a[0] * b[1] + a[1] * b[0]


def _s5_scan(sr, si, ab, reverse):
    n_tiles = sr.shape[0] // 8
    width = sr.shape[1]
    row8 = lax.broadcasted_iota(jnp.int32, (8, width), 0)
    p1 = ab
    p2 = _cmul(p1, p1)
    p4 = _cmul(p2, p2)
    pw = [p1]
    for _ in range(7):
        pw.append(_cmul(pw[-1], p1))
    cr = jnp.zeros((8, width), F32)
    ci = jnp.zeros((8, width), F32)
    for j in range(8):
        e = pw[7 - j] if reverse else pw[j]
        cr = jnp.where(row8 == j, e[0], cr)
        ci = jnp.where(row8 == j, e[1], ci)

    levels = []
    for d, q in ((1, p1), (2, p2), (4, p4)):
        keep = row8 < 8 - d if reverse else row8 >= d
        levels.append((d, (jnp.where(keep, q[0], 0.0), jnp.where(keep, q[1], 0.0))))

    def tile(i, carry):
        idx = n_tiles - 1 - i if reverse else i
        base = pl.multiple_of(idx * 8, 8)
        x = (sr[pl.ds(base, 8), :], si[pl.ds(base, 8), :])
        for d, q in levels:
            amt = 8 - d if reverse else d
            m = _cmul(q, (pltpu.roll(x[0], amt, 0), pltpu.roll(x[1], amt, 0)))
            x = (x[0] + m[0], x[1] + m[1])
        m = _cmul((cr, ci), carry)
        x = (x[0] + m[0], x[1] + m[1])
        sr[pl.ds(base, 8), :] = x[0]
        si[pl.ds(base, 8), :] = x[1]
        edge = slice(0, 1) if reverse else slice(7, 8)
        return x[0][edge], x[1][edge]

    zero = jnp.zeros((1, width), F32)
    lax.fori_loop(0, n_tiles, tile, (zero, zero))


_S5_W = S5_SLAB // S5_GROUP * S5_STATE


def _s5_specs(t):
    col = lambda r: pl.BlockSpec((r, S5_SLAB), lambda j: (0, j))
    bb = pl.BlockSpec((None, S5_SLAB, _S5_W), lambda j: (j, 0, 0))
    cd = pl.BlockSpec((None, _S5_W, S5_SLAB), lambda j: (j, 0, 0))
    ab = pl.BlockSpec((None, 1, _S5_W), lambda j: (j, 0, 0))
    return col, bb, cd, ab


def _s5_fwd(u, dvec, bbr, bbi, cdr, cdi, abr, abi):
    t, width = u.shape
    col, bb, cd, ab = _s5_specs(t)

    def body(u_ref, d_ref, bbr_ref, bbi_ref, cdr_ref, cdi_ref, abr_ref, abi_ref, o_ref, sr, si):
        uv = u_ref[...]
        sr[...] = _dot16(uv, bbr_ref[...])
        si[...] = _dot16(uv, bbi_ref[...])
        _s5_scan(sr, si, (abr_ref[...], abi_ref[...]), False)
        y = _dot16(sr[...], cdr_ref[...]) - _dot16(si[...], cdi_ref[...])
        o_ref[...] = jax.nn.gelu(y + d_ref[...] * uv).astype(BF16)

    return _blocked(body, name="s5_fwd", grid=(width // S5_SLAB,),
                          in_specs=[col(t), col(1), bb, bb, cd, cd, ab, ab], out_specs=col(t),
                          out_shape=jax.ShapeDtypeStruct((t, width), BF16),
                          scratch_shapes=[pltpu.VMEM((t, _S5_W), F32)] * 2,
                          compiler_params=_cparams(("parallel",), VMEM_BIG))(u, dvec, bbr, bbi, cdr, cdi, abr, abi)


def _s5_bwd(u, dvec, bbr, bbi, cdr, cdi, abr, abi, dyact):
    t, width = u.shape
    col, bb, cd, ab = _s5_specs(t)
    ns = width // S5_SLAB
    tn = (((0,), (0,)), ((), ()))
    nt = (((1,), (1,)), ((), ()))

    def body(u_ref, d_ref, bbr_ref, bbi_ref, cdr_ref, cdi_ref, abr_ref, abi_ref, dy_ref,
             du_ref, dd_ref, dbbr_ref, dbbi_ref, dcdr_ref, dcdi_ref, dabr_ref, dabi_ref, sr, si, gr, gi):
        uv = u_ref[...]
        dv = d_ref[...]
        abv = (abr_ref[...], abi_ref[...])
        sr[...] = _dot16(uv, bbr_ref[...])
        si[...] = _dot16(uv, bbi_ref[...])
        _s5_scan(sr, si, abv, False)
        y = _dot16(sr[...], cdr_ref[...]) - _dot16(si[...], cdi_ref[...])
        _, vjp = jax.vjp(jax.nn.gelu, y + dv * uv)
        (dpre,) = vjp(dy_ref[...].astype(F32))
        dd_ref[...] = jnp.sum(dpre * uv, axis=0, keepdims=True)
        dcdr_ref[...] = _dot16(sr[...], dpre, tn)
        dcdi_ref[...] = -_dot16(si[...], dpre, tn)
        gr[...] = _dot16(dpre, cdr_ref[...], nt)
        gi[...] = -_dot16(dpre, cdi_ref[...], nt)
        _s5_scan(gr, gi, (abv[0], -abv[1]), True)

        row8 = lax.broadcasted_iota(jnp.int32, (8, _S5_W), 0)

        def tile(i, carry):
            acc_r, acc_i, last_r, last_i = carry
            base = pl.multiple_of(i * 8, 8)
            s_r, s_i = sr[pl.ds(base, 8), :], si[pl.ds(base, 8), :]
            g_r, g_i = gr[pl.ds(base, 8), :], gi[pl.ds(base, 8), :]
            p_r = jnp.where(row8 == 0, last_r, pltpu.roll(s_r, 1, 0))
            p_i = jnp.where(row8 == 0, last_i, pltpu.roll(s_i, 1, 0))
            acc_r = acc_r + jnp.sum(g_r * p_r + g_i * p_i, axis=0, keepdims=True)
            acc_i = acc_i + jnp.sum(g_i * p_r - g_r * p_i, axis=0, keepdims=True)
            return acc_r, acc_i, s_r[7:8], s_i[7:8]

        zero = jnp.zeros((1, _S5_W), F32)
        acc_r, acc_i, _, _ = lax.fori_loop(0, t // 8, tile, (zero, zero, zero, zero))
        dabr_ref[...] = acc_r
        dabi_ref[...] = acc_i
        du_ref[...] = dpre * dv + _dot16(gr[...], bbr_ref[...], nt) + _dot16(gi[...], bbi_ref[...], nt)
        dbbr_ref[...] = _dot16(uv, gr[...], tn)
        dbbi_ref[...] = _dot16(uv, gi[...], tn)

    sds = jax.ShapeDtypeStruct
    return _blocked(
        body, name="s5_bwd", grid=(ns,), in_specs=[col(t), col(1), bb, bb, cd, cd, ab, ab, col(t)],
        out_specs=[col(t), col(1), bb, bb, cd, cd, ab, ab],
        out_shape=[sds((t, width), F32), sds((1, width), F32), sds((ns, S5_SLAB, _S5_W), F32),
                   sds((ns, S5_SLAB, _S5_W), F32), sds((ns, _S5_W, S5_SLAB), F32), sds((ns, _S5_W, S5_SLAB), F32),
                   sds((ns, 1, _S5_W), F32), sds((ns, 1, _S5_W), F32)],
        scratch_shapes=[pltpu.VMEM((t, _S5_W), F32)] * 4,
        compiler_params=_cparams(("parallel",), VMEM_BIG))(u, dvec, bbr, bbi, cdr, cdi, abr, abi, dyact)


def _gate_dense(w):
    h = w.shape[0]
    return jnp.einsum("hij,hg->higj", w, jnp.eye(h, dtype=F32)).reshape(h * HEAD, h * HEAD)


def _gate_blocks(d):
    x = d.reshape(LRU_W // LRU_CW, 2, HEAD, 2, HEAD)
    return jnp.einsum("tgihj,gh->tgij", x, jnp.eye(2, dtype=F32)).reshape(LRU_W // HEAD, HEAD, HEAD)


_GPS = S5_SLAB // S5_GROUP
_NS = S5_GROUPS // _GPS


def _s5_in_dense(bb):
    x = bb.reshape(_NS, _GPS, S5_STATE, S5_GROUP)
    return jnp.einsum("sgnc,gh->sgchn", x, jnp.eye(_GPS, dtype=F32)).reshape(_NS, S5_SLAB, _S5_W)


def _s5_in_blocks(d):
    x = d.reshape(_NS, _GPS, S5_GROUP, _GPS, S5_STATE)
    return jnp.einsum("sgchn,gh->sgnc", x, jnp.eye(_GPS, dtype=F32)).reshape(S5_GROUPS, S5_STATE * S5_GROUP)


def _s5_out_dense(c):
    x = c.reshape(_NS, _GPS, S5_GROUP, S5_STATE)
    return jnp.einsum("sgcn,gh->shngc", x, jnp.eye(_GPS, dtype=F32)).reshape(_NS, _S5_W, S5_SLAB)


def _s5_out_blocks(d):
    x = d.reshape(_NS, _GPS, S5_STATE, _GPS, S5_GROUP)
    return jnp.einsum("shngc,gh->sgcn", x, jnp.eye(_GPS, dtype=F32)).reshape(S5_GROUPS, S5_GROUP, S5_STATE)


def _local_step(x, tgt, w, late_weights, send_grads):
    d_model = x.shape[1]
    gs = {}
    n_layers = w["f_norm_g"].shape[0]

    def ffn_fwd(xin, l):
        xn = _rms_fwd(xin, w["f_norm_g"][l:l + 1], f"rms_f{l}")
        h = _matmul(xn, w["f_w_up_t"][l], "nt", f"mm_f{l}_up")
        act = _ffn_mid_fwd(h, w["f_conv_w"][l], w["f_conv_b"][l:l + 1], f"ffn_mid_fwd{l}")
        return _matmul(act, w["f_w_down"][l], "nn", f"mm_f{l}_down", add=xin), (xin, xn, h, act)

    def ffn_bwd(g, saved, l):
        xin, xn, h, act = saved
        dact = _matmul(g, w["f_w_down"][l], "nt", f"mm_f{l}_dact")
        d_down = _matmul(act, g, "tn", f"mm_f{l}_ddown", out_dtype=BF16)
        dhg, dhv, dwg, dwv, dbg, dbv = _ffn_mid_bwd(h, w["f_conv_w"][l], w["f_conv_b"][l:l + 1], dact,
                                                    f"ffn_mid_bwd{l}")
        dxn = _matmul((dhg, dhv), w["f_w_up_t"][l], "nn", f"mm_f{l}_dxn")
        d_up = _matmul((dhg, dhv), xn, "tn", f"mm_f{l}_dup", out_dtype=BF16)
        dx, dgn = _rms_bwd(xin, w["f_norm_g"][l:l + 1], dxn, g, f"rms_f{l}_bwd")
        return dx, d_up, d_down, jnp.concatenate([dwg, dwv], axis=1), jnp.concatenate([dbg, dbv], axis=1), dgn

    xn0 = _rms_fwd(x, w["e_norm_g"], "rms_e")
    p = _matmul(xn0, w["e_w_in_t"], "nt", "mm_e_in")
    pam = _tshift_fwd(p, w["e_mu"])
    pw = dict(w0=w["e_w0"], w2=w["e_w2"][0], a0=w["e_a0"], a2=w["e_a2"][0], g2=w["e_g2"][0],
              k_k=w["e_k_k"], k_a=w["e_k_a"])
    r, dec, k2, v, z, b, gate = _rwkv_prep_fwd(pam, pw)
    v_exp = _expand_cols(v, "wkv_expand_v")
    s_all, s_last = _wkv_fwd(dec, k2, z, b, v_exp)
    y_pt = _wkv_out(r, s_all, s_last)
    y = _from_pt(y_pt)
    rk = w["e_r_k"].reshape(1, RW)
    ya = _rwkv_post_fwd(y, r, k2, v, gate, w["e_ln_w"], w["e_ln_b"], rk)
    ga, gx = _gate_dense(w["e_gate_a_w"][0]), _gate_dense(w["e_gate_x_w"][0])
    lru_w = (w["e_conv_w"][0], w["e_conv_b"], ga, w["e_gate_a_b"], gx, w["e_gate_x_b"], w["e_lru_lambda"])
    yb = _lru_fwd(p, *lru_w)
    ycat = jnp.concatenate([ya, yb], axis=1)
    w = {**w, **late_weights(ycat)}
    x1 = _matmul(ycat, w["e_w_out"], "nn", "mm_e_out", add=x)
    x2, ffn0 = ffn_fwd(x1, 0)

    xn1 = _rms_fwd(x2, w["o_norm_g"], "rms_o")
    u = _matmul(xn1, w["o_w_in"], "nn", "mm_o_in")
    expand = jnp.kron(jnp.eye(S5_STATE, dtype=F32), jnp.ones((1, S5_GROUP), F32))
    disc_in = (w["o_A_re"][0], w["o_A_im"][0], w["o_log_dt"].reshape(S5_GROUPS, 1),
               w["o_B_re"][0].reshape(S5_GROUPS, -1), w["o_B_im"][0].reshape(S5_GROUPS, -1), expand)
    ab_re, ab_im, bb_re, bb_im = _s5_disc_fwd(*disc_in)
    s5_w = (w["o_D"], _s5_in_dense(bb_re), _s5_in_dense(bb_im), _s5_out_dense(w["o_C_re"][0]),
            _s5_out_dense(w["o_C_im"][0]), ab_re.reshape(_NS, 1, _S5_W), ab_im.reshape(_NS, 1, _S5_W))
    yact = _s5_fwd(u, *s5_w)
    zz = _matmul(yact, w["o_w_glu_t"], "nt", "mm_o_glu")
    x3 = _glu_fwd(x2, zz)
    x4, ffn1 = ffn_fwd(x3, 1)

    loss, g, gs["final_norm_g", 0] = _loss_head(x4, w["final_norm_g"].reshape(1, d_model), tgt)

    g, up1, down1, dcw1, dcb1, dfn1 = ffn_bwd(g, ffn1, 1)
    dz = _glu_bwd(zz, g)
    dyact = _matmul(dz, w["o_w_glu_t"], "nn", "mm_o_dyact")
    d_glu = _matmul(dz, yact, "tn", "mm_o_dglu", out_dtype=BF16)
    du, gs["o_D", 0], dbbr, dbbi, dcdr, dcdi, dabr, dabi = _s5_bwd(u, *s5_w, dyact)
    gs["o_C_re", 0] = _s5_out_blocks(dcdr).reshape(S5_GROUPS * S5_GROUP, S5_STATE)
    gs["o_C_im", 0] = _s5_out_blocks(dcdi).reshape(S5_GROUPS * S5_GROUP, S5_STATE)
    cts = (dabr.reshape(S5_GROUPS, S5_STATE), dabi.reshape(S5_GROUPS, S5_STATE), _s5_in_blocks(dbbr),
           _s5_in_blocks(dbbi))
    gs["o_A_re", 0], gs["o_A_im", 0], dlog_dt, gs["o_B_re", 0], gs["o_B_im", 0] = _s5_disc_bwd(*disc_in, cts)
    gs["o_log_dt", 0] = dlog_dt.reshape(1, S5_GROUPS)
    dxn = _matmul(du, w["o_w_in"], "nt", "mm_o_dxn")
    d_oin = _matmul(xn1, du, "tn", "mm_o_din", out_dtype=BF16)
    g, gs["o_norm_g", 0] = _rms_bwd(x2, w["o_norm_g"], dxn, g, "rms_o_bwd")
    g = send_grads("a", [("f_w_up", 1, up1), ("f_w_down", 1, down1), ("o_w_glu", 0, d_glu), ("o_w_in", 0, d_oin)], g)

    g, up0, down0, dcw0, dcb0, dfn0 = ffn_bwd(g, ffn0, 0)
    gs["f_conv_w", 0], gs["f_conv_w", 3] = dcw0, dcw1
    gs["f_conv_b", 0], gs["f_conv_b", 1] = dcb0, dcb1
    gs["f_norm_g", 0], gs["f_norm_g", 1] = dfn0, dfn1

    dycat = _matmul(g, w["e_w_out"], "nt", "mm_e_dycat")
    d_eout = _matmul(ycat, g, "tn", "mm_e_dout", out_dtype=BF16)
    dycat = send_grads("b", [("f_w_up", 0, up0), ("f_w_down", 0, down0), ("e_w_out", 0, d_eout)], dycat)
    dy, dr1, dk1, dv1, dgate, gs["e_ln_w", 0], gs["e_ln_b", 0], gs["e_r_k", 0] = _rwkv_post_bwd(
        y, r, k2, v, gate, w["e_ln_w"], w["e_ln_b"], rk, dycat)
    dr2, ddec, dk2, dzz, dbb, dv_pt = _wkv_bwd(r, dec, k2, z, b, v_exp, s_all, _expand_cols(dy, "wkv_expand_dy"))
    (dpam, gs["e_w0", 0], gs["e_w2", 0], gs["e_a0", 0], gs["e_a2", 0], gs["e_g2", 0], gs["e_k_k", 0],
     gs["e_k_a", 0]) = _rwkv_prep_bwd(pam, pw, (dr2, ddec, dk2, _from_pt(dv_pt), dzz, dbb, dgate), (dr1, dk1, dv1))
    dpa, gs["e_mu", 0] = _tshift_bwd(p, w["e_mu"], dpam)
    (dbx, dbg, gs["e_conv_w", 0], gs["e_conv_b", 0], dga, gs["e_gate_a_b", 0], dgx, gs["e_gate_x_b", 0],
     gs["e_lru_lambda", 0]) = _lru_bwd(p, *lru_w, dycat)
    gs["e_gate_a_w", 0] = _gate_blocks(dga).reshape(LRU_W, HEAD)
    gs["e_gate_x_w", 0] = _gate_blocks(dgx).reshape(LRU_W, HEAD)
    dp = jnp.concatenate([dpa, dbx, dbg], axis=1)
    d_ein = _matmul(dp, xn0, "tn", "mm_e_din", out_dtype=BF16)
    dp = send_grads("c", [("e_w_in", 0, d_ein)], dp)
    dxn = _matmul(dp, w["e_w_in_t"], "nn", "mm_e_dxn")
    grad_x, gs["e_norm_g", 0] = _rms_bwd(x, w["e_norm_g"], dxn, g, "rms_e_bwd")
    return loss, grad_x, gs


CAST_ROWS = 256


def _cast_shard(w3, layer, transpose, chip, name):
    _, rows, cols = w3.shape
    tr = _tile(rows, (CAST_ROWS, 176, 128))

    def body(c_ref, w_ref, o_ref):
        v = w_ref[...]
        o_ref[...] = (v.T if transpose else v).astype(BF16)

    in_spec = pl.BlockSpec((None, tr, cols), lambda i, c: (layer, i, 0))
    if transpose:
        out_spec, shape = pl.BlockSpec((None, cols, tr), lambda i, c: (c[0], 0, i)), (cols, rows)
    else:
        out_spec, shape = pl.BlockSpec((None, tr, cols), lambda i, c: (c[0], i, 0)), (rows, cols)
    grid_spec = pltpu.PrefetchScalarGridSpec(num_scalar_prefetch=1, grid=(rows // tr,), in_specs=[in_spec],
                                             out_specs=out_spec)
    return _blocked(body, name=name, grid_spec=grid_spec,
                          out_shape=jax.ShapeDtypeStruct((N_CHIPS,) + shape, BF16),
                          compiler_params=_cparams(("parallel",), VMEM_MID))(chip, w3)


_ANY = pl.BlockSpec(memory_space=pl.ANY)


def _coords():
    return lax.axis_index("x"), lax.axis_index("y"), lax.axis_index("c")


def _flip(v, d):
    return 1 - v if d else v


_CHIP_RELS = ((1, 0), (0, 1), (1, 1))
_DEV_RELS = tuple((dx, dy, dc) for dx in (0, 1) for dy in (0, 1) for dc in (0, 1))[1:]


_HBM = pl.BlockSpec(memory_space=pltpu.HBM)
_SEM = pl.BlockSpec(memory_space=pltpu.SEMAPHORE)
_EFFECT = pltpu.SideEffectType.DATAFLOW_SIDE_EFFECTING


def _in_hbm(a):
    return pltpu.with_memory_space_constraint(a, pltpu.HBM)


def _gather_copies(bufs, send, recv, landed):
    x, y, c = _coords()
    me = 2 * x + y
    res = []
    for i, buf in enumerate(bufs):
        for j, (dx, dy) in enumerate(_CHIP_RELS):
            px, py = _flip(x, dx), _flip(y, dy)
            k = i * len(_CHIP_RELS) + j
            res.append(pltpu.make_async_remote_copy(
                src_ref=buf.at[me], dst_ref=buf.at[2 * px + py if landed else me], send_sem=send.at[k],
                recv_sem=recv.at[k], device_id=(px, py, c), device_id_type=MESH))
    return res


def _scatter_copies(srcs, lands, send, recv, landed):
    x, y, c = _coords()
    me = 4 * x + 2 * y + c
    res = []
    for i, (src, land) in enumerate(zip(srcs, lands)):
        for j, (dx, dy, dc) in enumerate(_DEV_RELS):
            peer = (_flip(x, dx), _flip(y, dy), _flip(c, dc))
            pid = 4 * peer[0] + 2 * peer[1] + peer[2]
            k = i * len(_DEV_RELS) + j
            res.append(pltpu.make_async_remote_copy(
                src_ref=src.at[pid], dst_ref=land.at[pid if landed else me], send_sem=send.at[k],
                recv_sem=recv.at[k], device_id=peer, device_id_type=MESH))
    return res


def _split_start(bufs, n_src, copies, n_rel, name, after):
    n = len(bufs)
    nk = n_src * n_rel

    def body(*refs):
        ins, send, recv, token = refs[:n], refs[n + 1 + n], refs[n + 2 + n], refs[-1]
        for cp in copies(ins, send, recv, False):
            cp.start()
        token[...] = jnp.zeros_like(token)

    res = pl.pallas_call(
        body, name=name, in_specs=[_HBM] * n + [_ANY],
        out_specs=[_HBM] * n + [_SEM, _SEM, pl.BlockSpec(memory_space=pltpu.VMEM)],
        out_shape=[pltpu.HBM(b.shape, b.dtype) for b in bufs]
        + [pltpu.SemaphoreType.DMA((nk,)), pltpu.SemaphoreType.DMA((nk,)), jax.ShapeDtypeStruct((8, LANES), F32)],
        input_output_aliases={i: i for i in range(n)},
        compiler_params=pltpu.CompilerParams(has_side_effects=_EFFECT))(*[_in_hbm(b) for b in bufs], after)
    return res[n], res[n + 1], list(res[:n]), res[n + 2]


def _split_wait(bufs, send, recv, copies, name, after):
    n = len(bufs)

    def body(*refs):
        ins, send_ref, recv_ref = refs[:n], refs[n], refs[n + 1]
        for cp in copies(ins, send_ref, recv_ref, True):
            cp.wait_send()
            cp.wait_recv()

    return pl.pallas_call(
        body, name=name, in_specs=[_HBM] * n + [_SEM, _SEM, _ANY], out_specs=[_HBM] * n,
        out_shape=[pltpu.HBM(b.shape, b.dtype) for b in bufs], input_output_aliases={i: i for i in range(n)},
        compiler_params=pltpu.CompilerParams(has_side_effects=_EFFECT))(*bufs, send, recv, after)


def _gather_start(bufs, name, after):
    return _split_start(bufs, len(bufs), _gather_copies, len(_CHIP_RELS), name, after)


def _gather_wait(bufs, send, recv, name, after):
    return _split_wait(bufs, send, recv, _gather_copies, name, after)


def _scatter_start(srcs, name, after):
    n = len(srcs)
    lands = [lax.empty(a.shape, a.dtype) for a in srcs]
    fn = lambda refs, send, recv, landed: _scatter_copies(refs[:n], refs[n:], send, recv, landed)
    send, recv, bufs, token = _split_start(list(srcs) + lands, n, fn, len(_DEV_RELS), name, after)
    return send, recv, bufs, token


def _scatter_wait(bufs, send, recv, name, after):
    n = len(bufs) // 2
    fn = lambda refs, s, r, landed: _scatter_copies(refs[:n], refs[n:], s, r, landed)
    res = _split_wait(bufs, send, recv, fn, name, after)
    return res[:n], res[n:]


def _sum_segments(src, land, me, name):
    nd, seg, cols = src.shape
    ts = _tile(seg, (256, 176, 128))

    def body(m_ref, *refs):
        o_ref = refs[-1]
        acc = refs[0][...].astype(F32)
        for r in refs[1:-1]:
            acc = acc + r[...].astype(F32)
        o_ref[...] = acc

    def peer(rel):
        bits = 4 * rel[0] + 2 * rel[1] + rel[2]
        return pl.BlockSpec((None, ts, cols), lambda i, m: (jnp.bitwise_xor(m[0], bits), i, 0))

    grid_spec = pltpu.PrefetchScalarGridSpec(
        num_scalar_prefetch=1, grid=(seg // ts,),
        in_specs=[pl.BlockSpec((None, ts, cols), lambda i, m: (m[0], i, 0))] + [peer(r) for r in _DEV_RELS],
        out_specs=pl.BlockSpec((None, ts, cols), lambda i, m: (m[1], i, 0)))
    return _blocked(body, name=name, grid_spec=grid_spec,
                          out_shape=jax.ShapeDtypeStruct((2, seg, cols), F32),
                          compiler_params=_cparams(("parallel",), VMEM_MID))(me, src, *[land] * len(_DEV_RELS))


def _exchange_sibling(arrs):
    n = len(arrs)

    def body(*refs):
        outs, (send, recv) = refs[n:2 * n], refs[2 * n:]
        x, y, c = _coords()
        sib = (x, y, 1 - c)
        sends, recvs = [], []
        for i in range(n):
            cp = pltpu.make_async_remote_copy(src_ref=outs[i].at[c], dst_ref=outs[i].at[c], send_sem=send.at[i],
                                              recv_sem=recv.at[i], device_id=sib, device_id_type=MESH)
            cp.start()
            sends.append(cp)
            recvs.append(pltpu.make_async_remote_copy(src_ref=outs[i].at[c], dst_ref=outs[i].at[1 - c],
                                                      send_sem=send.at[i], recv_sem=recv.at[i], device_id=sib,
                                                      device_id_type=MESH))
        for cp in recvs:
            cp.wait_recv()
        for cp in sends:
            cp.wait_send()

    return pl.pallas_call(
        body, name="exchange_sibling", in_specs=[_ANY] * n, out_specs=[_ANY] * n,
        out_shape=[jax.ShapeDtypeStruct(a.shape, a.dtype) for a in arrs],
        input_output_aliases={i: i for i in range(n)},
        scratch_shapes=[pltpu.SemaphoreType.DMA((n,)), pltpu.SemaphoreType.DMA((n,))])(*arrs)


def _allreduce_small(vec):
    _, nchips, seg, lanes = vec.shape
    nr = len(_CHIP_RELS)

    def body(in_ref, out_ref, from_sib, half, stage, red, send, recv):
        x, y, c = _coords()
        me = 2 * x + y
        sib = (x, y, 1 - c)
        chips = [(_flip(x, dx), _flip(y, dy)) for dx, dy in _CHIP_RELS]

        def copy(src, dst, k, peer):
            return pltpu.make_async_remote_copy(src_ref=src, dst_ref=dst, send_sem=send.at[k], recv_sem=recv.at[k],
                                                device_id=peer, device_id_type=MESH)

        to_sib = copy(in_ref.at[1 - c], from_sib, 0, sib)
        to_sib.start()
        to_sib.wait_recv()
        half[...] = in_ref[c] + from_sib[...]

        first = [copy(half.at[2 * px + py], stage.at[me], 1 + j, (px, py, c)) for j, (px, py) in enumerate(chips)]
        for cp in first:
            cp.start()
        stage[me] = half[me]
        for j, (px, py) in enumerate(chips):
            copy(half.at[2 * px + py], stage.at[2 * px + py], 1 + j, (px, py, c)).wait_recv()
        acc = stage[0]
        for k in range(1, nchips):
            acc = acc + stage[k]
        red[...] = acc
        out_ref[c, me] = acc

        second = [copy(red, out_ref.at[c, me], 1 + nr + j, (px, py, c)) for j, (px, py) in enumerate(chips)]
        for cp in second:
            cp.start()
        for j, (px, py) in enumerate(chips):
            copy(red, out_ref.at[c, 2 * px + py], 1 + nr + j, (px, py, c)).wait_recv()

        back = copy(out_ref.at[c], out_ref.at[c], 1 + 2 * nr, sib)
        back.start()
        copy(out_ref.at[c], out_ref.at[1 - c], 1 + 2 * nr, sib).wait_recv()
        for cp in [to_sib] + first + second + [back]:
            cp.wait_send()

    vm = pl.BlockSpec(memory_space=pltpu.VMEM)
    nsem = 2 + 2 * nr
    return pl.pallas_call(
        body, name="allreduce_small", in_specs=[vm], out_specs=vm,
        out_shape=jax.ShapeDtypeStruct(vec.shape, F32),
        scratch_shapes=[pltpu.VMEM((nchips, seg, lanes), F32), pltpu.VMEM((nchips, seg, lanes), F32),
                        pltpu.VMEM((nchips, seg, lanes), F32), pltpu.VMEM((seg, lanes), F32),
                        pltpu.SemaphoreType.DMA((nsem,)), pltpu.SemaphoreType.DMA((nsem,))],
        compiler_params=_cparams(None, VMEM_MID))(vec)


def _adam_math(w, g, m, v):
    m2 = ADAM_B1 * m + (1.0 - ADAM_B1) * g
    v2 = ADAM_B2 * v + (1.0 - ADAM_B2) * (g * g)
    m_hat = m2 / (1.0 - ADAM_B1 ** ADAM_STEP)
    v_hat = v2 / (1.0 - ADAM_B2 ** ADAM_STEP)
    return -ADAM_LR * (m_hat / (jnp.sqrt(v_hat) + ADAM_EPS) + ADAM_WD * w), m2, v2


def _adamw_big(w3, m3, v3, layer, g, transposed, name, prev=None):
    nl, rows, cols = w3.shape
    tr = 128 if transposed else _tile(rows, (256, 176, 128))

    def body(w_ref, m_ref, v_ref, g_ref, *rest):
        go_ref, d_ref, mo_ref, vo_ref = rest[-4:]
        g_val = g_ref[...].T if transposed else g_ref[...]
        go_ref[...] = g_val
        d_ref[...], mo_ref[...], vo_ref[...] = _adam_math(w_ref[...], g_val, m_ref[...], v_ref[...])

    wspec = pl.BlockSpec((None, tr, cols), lambda i: (layer, i, 0))
    gspec = pl.BlockSpec((cols, tr), lambda i: (0, i)) if transposed else pl.BlockSpec((tr, cols), lambda i: (i, 0))
    extra = [] if prev is None else list(prev)
    return _blocked(body, name=name, grid=(rows // tr,),
                          in_specs=[wspec, wspec, wspec, gspec] + [_ANY] * len(extra),
                          out_specs=[wspec] * 4, out_shape=[jax.ShapeDtypeStruct((nl, rows, cols), F32)] * 4,
                          input_output_aliases={4 + i: i for i in range(len(extra))},
                          compiler_params=_cparams(("parallel",), VMEM_MID))(w3, m3, v3, g, *extra)


_SMALL = (
    ("e_norm_g", (1, D_MODEL), None), ("e_mu", (1, SHIFT_COLS), None), ("e_w0", (1, RW), None),
    ("e_w2", (W_LORA, RW), RW // 4), ("e_a0", (1, RW), None), ("e_a2", (A_LORA, RW), RW // 4),
    ("e_g2", (G_LORA, RW), RW // 4), ("e_k_k", (1, RW), None), ("e_k_a", (1, RW), None), ("e_r_k", (1, RW), None),
    ("e_ln_w", (1, RW), None), ("e_ln_b", (1, RW), None), ("e_conv_w", (4, LRU_W), LRU_W // 4),
    ("e_conv_b", (1, LRU_W), None), ("e_gate_a_w", (LRU_W, HEAD), None), ("e_gate_a_b", (1, LRU_W), None),
    ("e_gate_x_w", (LRU_W, HEAD), None), ("e_gate_x_b", (1, LRU_W), None), ("e_lru_lambda", (1, LRU_W), None),
    ("o_norm_g", (1, D_MODEL), D_MODEL // 4), ("o_A_re", (S5_GROUPS, S5_STATE), None),
    ("o_A_im", (S5_GROUPS, S5_STATE), None), ("o_log_dt", (1, S5_GROUPS), None),
    ("o_B_re", (S5_GROUPS, S5_STATE * S5_GROUP), None), ("o_B_im", (S5_GROUPS, S5_STATE * S5_GROUP), None),
    ("o_C_re", (S5_GROUPS * S5_GROUP, S5_STATE), None), ("o_C_im", (S5_GROUPS * S5_GROUP, S5_STATE), None),
    ("o_D", (1, D_MODEL), D_MODEL // 4), ("f_norm_g", (2, D_MODEL), None),
    ("f_conv_w", (6, 2 * D_FF), 2 * D_FF // 4), ("f_conv_b", (2, 2 * D_FF), None),
    ("final_norm_g", (1, D_MODEL), None))
_PIECES = {"f_norm_g": ((0, 1), (1, 1)), "f_conv_b": ((0, 1), (1, 1)), "f_conv_w": ((0, 3), (3, 3))}


def _ceil_to(n, m):
    return -(-n // m) * m


def _small_layout():
    groups = {}
    for name, (rows, cols), _ in _SMALL:
        for first, r in _PIECES.get(name, ((0, rows),)):
            groups.setdefault(cols, []).append((name, first, r))
    layout, off = {}, 0
    for cols, items in groups.items():
        stacks = [0, 0] if 2 * cols <= LANES else [0]
        placed = []
        for name, first, r in sorted(items, key=lambda it: -it[2]):
            half = stacks.index(min(stacks))
            r0 = stacks[half]
            if r >= 8 or r0 % 8 + r > 8:
                r0 = _ceil_to(r0, 8)
            placed.append((name, first, r, r0, half * (LANES // 2)))
            stacks[half] = r0 + r
        rpad = _ceil_to(max(stacks), 8)
        for name, first, r, at, lane in placed:
            layout[name, first] = (off, rpad, at, r, cols, lane)
        off += -(-cols // LANES) * rpad
    return layout, _ceil_to(off, 8 * N_DEV)


def _small_pack(gs):
    layout, total = _small_layout()
    keys = list(layout)

    def body(*refs):
        out = refs[-1]
        out[...] = jnp.zeros_like(out)
        for key, g_ref in zip(keys, refs[:-1]):
            off, rpad, at, r, cols, lane = layout[key]
            for j in range(-(-cols // LANES)):
                cw = min(LANES, cols - j * LANES)
                out[off + j * rpad + at:off + j * rpad + at + r, lane:lane + cw] = g_ref[:, j * LANES:j * LANES + cw]

    return pl.pallas_call(body, name="small_pack", out_shape=jax.ShapeDtypeStruct((total, LANES), F32),
                          compiler_params=_cparams(None, VMEM_MID))(*[gs[k] for k in keys])


def _adamw_small(red, chip, wts, ms, vs):
    layout, _ = _small_layout()
    names = [n for n, _, _ in _SMALL]
    n = len(names)

    def body(chip_ref, red_ref, *refs):
        ins, outs = refs[:3 * n], refs[3 * n:]
        c = chip_ref[0]
        for i, (name, (rows, cols), loc) in enumerate(_SMALL):
            w_ref, m_ref, v_ref = ins[3 * i:3 * i + 3]
            o_refs = outs[4 * i:4 * i + 4]
            width = cols if loc is None else loc
            for first, r in _PIECES.get(name, ((0, rows),)):
                off, rpad, at, _, _, lane = layout[name, first]
                for j in range(-(-width // LANES)):
                    cw = min(LANES, width - j * LANES)
                    ls = slice(lane, lane + cw)
                    if loc is None:
                        start = off + j * rpad + at
                        g = red_ref[start:start + r, ls]
                    else:
                        blk = c * (loc // LANES) + j
                        if r >= 8:
                            g = red_ref[pl.ds(pl.multiple_of(off + at + blk * rpad, 8), r), ls]
                        else:
                            tile = red_ref[pl.ds(pl.multiple_of(off + at // 8 * 8 + blk * rpad, 8), 8), ls]
                            g = tile[at % 8:at % 8 + r]
                    rs, cs = slice(first, first + r), slice(j * LANES, j * LANES + cw)
                    d, m2, v2 = _adam_math(w_ref[rs, cs], g, m_ref[rs, cs], v_ref[rs, cs])
                    for o, val in zip(o_refs, (g, d, m2, v2)):
                        o[rs, cs] = val

    args, shapes = [], []
    for name in names:
        args += [wts[name], ms[name], vs[name]]
        shapes += [jax.ShapeDtypeStruct(wts[name].shape, F32)] * 4
    vm = pl.BlockSpec(memory_space=pltpu.VMEM)
    res = pl.pallas_call(body, name="adamw_small",
                         in_specs=[pl.BlockSpec(memory_space=pltpu.SMEM), vm] + [vm] * (3 * n),
                         out_specs=[vm] * (4 * n), out_shape=shapes,
                         compiler_params=_cparams(None, VMEM_BIG))(chip, red, *args)
    return {name: res[4 * i:4 * i + 4] for i, name in enumerate(names)}


PACK_ROWS = 8


def _packed_rows(shape):
    size = 1
    for d in shape:
        size *= d
    return -(-size // (PACK_ROWS * LANES)) * PACK_ROWS


def _pack(arrs, row_mult):
    parts = []
    for a in arrs:
        flat = a.reshape(-1).astype(F32)
        rows = _packed_rows(a.shape)
        parts.append(jnp.pad(flat, (0, rows * LANES - flat.shape[0])).reshape(rows, LANES))
    total = sum(p.shape[0] for p in parts)
    fill = -(-total // row_mult) * row_mult - total
    if fill:
        parts.append(jnp.zeros((fill, LANES), F32))
    return jnp.concatenate(parts, axis=0)


def _unpack(packed, shapes):
    out, off = [], 0
    for s in shapes:
        rows = _packed_rows(s)
        size = 1
        for d in s:
            size *= d
        out.append(packed[off:off + rows].reshape(-1)[:size].reshape(s))
        off += rows
    return out


_SMALL_SH = ("e_w2", "e_a2", "e_g2", "e_conv_w", "o_norm_g", "o_D", "f_conv_w")
_LARGE = (("e_w_in", True), ("e_w_out", False), ("o_w_in", False), ("o_w_glu", True), ("f_w_up", True),
        ("f_w_down", False))
_ORDER = ("e_norm_g", "e_w_in", "e_mu", "e_w0", "e_w2", "e_a0", "e_a2", "e_g2", "e_k_k", "e_k_a", "e_r_k", "e_ln_w",
          "e_ln_b", "e_conv_w", "e_conv_b", "e_gate_a_w", "e_gate_a_b", "e_gate_x_w", "e_gate_x_b", "e_lru_lambda",
          "e_w_out", "o_norm_g", "o_w_in", "o_A_re", "o_A_im", "o_log_dt", "o_B_re", "o_B_im", "o_C_re", "o_C_im",
          "o_D", "o_w_glu", "f_norm_g", "f_w_up", "f_conv_w", "f_conv_b", "f_w_down", "final_norm_g")
N_CHIPS = 4
N_DEV = 8


def _step(x, tgt, wts, ms, vs):
    xi, yi, ci = _coords()
    chip = 2 * xi + yi
    chip1 = chip.astype(jnp.int32).reshape(1)
    me2 = jnp.stack([4 * xi + 2 * yi + ci, ci]).astype(jnp.int32)
    by_cols = dict(_LARGE)

    bufs = {(name, l): _cast_shard(wts[name], l, by_cols[name], chip1, f"cast_{name}{l}")
            for name, _ in _LARGE for l in range(wts[name].shape[0])}
    sh_shapes = [wts[n].shape for n in _SMALL_SH]
    packed = _pack([wts[n] for n in _SMALL_SH], 8)
    small_buf = lax.dynamic_update_slice(jnp.zeros((N_CHIPS,) + packed.shape, F32), packed[None], (chip, 0, 0))
    early = [("e_w_in", 0)]
    late = [k for k in bufs if k not in early]
    send, recv, thru, token = _gather_start([bufs[k] for k in early] + [small_buf], "gather_start_a", x)
    got = _gather_wait(thru, send, recv, "gather_wait_a", token)
    send_b, recv_b, thru_b, token = _gather_start([bufs[k] for k in late], "gather_start_b", got[0])
    x, _ = lax.optimization_barrier((x, token))

    def rows(g):
        return g.reshape(N_CHIPS * g.shape[1], g.shape[2])

    full = {n: wts[n] for n, _, loc in _SMALL if loc is None}
    full["e_w_in_t"] = rows(got[0])
    per_chip = [_unpack(got[1][k], sh_shapes) for k in range(N_CHIPS)]
    for i, n in enumerate(_SMALL_SH):
        full[n] = jnp.concatenate([per_chip[k][i] for k in range(N_CHIPS)], axis=-1)

    def late_weights(after):
        res = dict(zip(late, _gather_wait(thru_b, send_b, recv_b, "gather_wait_b", after)))
        return {"e_w_out": rows(res[("e_w_out", 0)]), "o_w_in": rows(res[("o_w_in", 0)]),
                "o_w_glu_t": rows(res[("o_w_glu", 0)]),
                "f_w_up_t": [rows(res[("f_w_up", l)]) for l in range(2)],
                "f_w_down": [rows(res[("f_w_down", l)]) for l in range(2)]}

    pending = []

    def send_grads(tag, items, carry):
        srcs = [g.reshape(N_DEV, g.shape[0] // N_DEV, g.shape[1]) for _, _, g in items]
        s_sem, r_sem, both, tok = _scatter_start(srcs, f"scatter_start_{tag}", carry)
        pending.append((tag, [(name, l) for name, l, _ in items], s_sem, r_sem, both))
        carry, _ = lax.optimization_barrier((carry, tok))
        return carry

    loss, grad_x, gs = _local_step(x, tgt, full, late_weights, send_grads)

    final = {}
    red = _allreduce_small(_small_pack(gs).reshape(2, N_CHIPS, -1, LANES)).reshape(-1, LANES)
    view = {name: (rows, cols if loc is None else loc) for name, (rows, cols), loc in _SMALL}
    as2d = lambda d: {name: d[name].reshape(view[name]) for name in view}
    small = _adamw_small(red, chip1, as2d(wts), as2d(ms), as2d(vs))
    for name, res in small.items():
        final[name] = [r.reshape(wts[name].shape) for r in res]
    new_v = small["final_norm_g"][3]

    halves, keys = [], []
    for tag, names, s_sem, r_sem, both in pending:
        srcs, lands = _scatter_wait(both, s_sem, r_sem, f"scatter_wait_{tag}", new_v)
        for (name, l), src, land in zip(names, srcs, lands):
            halves.append(_sum_segments(src, land, me2, f"sum_{name}{l}"))
            keys.append((name, l))
    shards = _exchange_sibling(halves)
    for s, (name, l) in zip(shards, keys):
        final[name] = _adamw_big(wts[name], ms[name], vs[name], l, s.reshape(2 * s.shape[1], s.shape[2]),
                                 by_cols[name], f"adamw_{name}{l}", prev=final.get(name))

    loss = lax.psum(loss[0, 0], ("x", "y", "c"))
    res = [loss, grad_x[None]]
    for k in range(4):
        res += [final[n][k] for n in _ORDER]
    return tuple(res)


def kernel(x, e_norm_g, e_w_in, e_mu, e_w0, e_w2, e_a0, e_a2, e_g2, e_k_k, e_k_a, e_r_k, e_ln_w, e_ln_b, e_conv_w, e_conv_b, e_gate_a_w, e_gate_a_b, e_gate_x_w, e_gate_x_b, e_lru_lambda, e_w_out, o_norm_g, o_w_in, o_A_re, o_A_im, o_log_dt, o_B_re, o_B_im, o_C_re, o_C_im, o_D, o_w_glu, f_norm_g, f_w_up, f_conv_w, f_conv_b, f_w_down, final_norm_g, loss_target, m_e_norm_g, m_e_w_in, m_e_mu, m_e_w0, m_e_w2, m_e_a0, m_e_a2, m_e_g2, m_e_k_k, m_e_k_a, m_e_r_k, m_e_ln_w, m_e_ln_b, m_e_conv_w, m_e_conv_b, m_e_gate_a_w, m_e_gate_a_b, m_e_gate_x_w, m_e_gate_x_b, m_e_lru_lambda, m_e_w_out, m_o_norm_g, m_o_w_in, m_o_A_re, m_o_A_im, m_o_log_dt, m_o_B_re, m_o_B_im, m_o_C_re, m_o_C_im, m_o_D, m_o_w_glu, m_f_norm_g, m_f_w_up, m_f_conv_w, m_f_conv_b, m_f_w_down, m_final_norm_g, v_e_norm_g, v_e_w_in, v_e_mu, v_e_w0, v_e_w2, v_e_a0, v_e_a2, v_e_g2, v_e_k_k, v_e_k_a, v_e_r_k, v_e_ln_w, v_e_ln_b, v_e_conv_w, v_e_conv_b, v_e_gate_a_w, v_e_gate_a_b, v_e_gate_x_w, v_e_gate_x_b, v_e_lru_lambda, v_e_w_out, v_o_norm_g, v_o_w_in, v_o_A_re, v_o_A_im, v_o_log_dt, v_o_B_re, v_o_B_im, v_o_C_re, v_o_C_im, v_o_D, v_o_w_glu, v_f_norm_g, v_f_w_up, v_f_conv_w, v_f_conv_b, v_f_w_down, v_final_norm_g):
    args = locals()
    wts = {n: args[n] for n in _ORDER}
    ms = {n: args["m_" + n] for n in _ORDER}
    vs = {n: args["v_" + n] for n in _ORDER}
    return _step(x[0], loss_target[0], wts, ms, vs)
```

```python
import functools

import jax
import jax.numpy as jnp
from jax import lax
from jax.experimental import pallas as pl
from jax.experimental.pallas import tpu as pltpu

F32 = jnp.float32
BF16 = jnp.bfloat16
MESH = pl.DeviceIdType.MESH

D_MODEL = 1024
HEAD = 64
RW = 512
N_HEADS = RW // HEAD
LRU_W = 512
SHIFT_COLS = 1792
W_LORA, A_LORA, G_LORA = 64, 64, 128
S5_GROUPS, S5_GROUP, S5_STATE = 64, 16, 64
D_FF = 2816
NORM_EPS = 1e-6
GN_EPS = 64e-5
LRU_C = 8.0
ADAM_LR, ADAM_B1, ADAM_B2, ADAM_EPS, ADAM_WD, ADAM_STEP = 0.001, 0.9, 0.999, 1e-08, 0.01, 10

VMEM_BIG = 56 * 1024 * 1024
VMEM_MID = 40 * 1024 * 1024
LANES = 128
PT = 16
WKV_CHUNK = 32
S5_SLAB = 128


def _blocked(*args, **kw):
    call = pl.pallas_call(*args, **kw)

    def run(*ops):
        return call(*[pltpu.with_memory_space_constraint(a, pltpu.HBM) if a.ndim >= 2 else a for a in ops])

    return run


def _cparams(sem=None, vmem=None):
    kw = {}
    if sem is not None:
        kw["dimension_semantics"] = sem
    if vmem is not None:
        kw["vmem_limit_bytes"] = vmem
    return pltpu.CompilerParams(**kw)


def _tile(dim, cands):
    for c in cands:
        if dim % c == 0:
            return c
    return dim


def _full(shape):
    n = len(shape)
    return pl.BlockSpec(shape, lambda *_: (0,) * n)


_TILES = (2816, 2048, 1408, 1024, 512, 256, 128)
MM_BUDGET = 36 * 1024 * 1024
VMEM_SLACK = 12 * 1024 * 1024


MXU_FLOPS = 9.0e14
HBM_BYTES = 3.3e12
STEP_SECONDS = 0.35e-6


def _mm_tiles(m, n, k, size_a, size_b, size_o, has_add, parts=1, tk_only=None, tm_max=None):
    best = None
    for tm in _TILES:
        for tk in _TILES:
            for tn in _TILES:
                if m % tm or n % tn or k % tk or (tk_only and tk != tk_only) or (tm_max and tm_max % tm):
                    continue
                need = (2 * (parts * tm * tk * size_a + tk * tn * size_b + tm * tn * size_o)
                        + tm * tn * 4 * (1 + 2 * has_add))
                if k > tk:
                    need += tm * tn * 4
                if need > MM_BUDGET:
                    continue
                steps = (m // tm) * (n // tn) * (k // tk)
                a_reads = n // tn if k > tk else 1
                moved = (m * k * size_a * a_reads + k * n * size_b * (m // tm) + m * n * (size_o + 4 * has_add))
                cost = max(2.0 * m * n * k / MXU_FLOPS, moved / HBM_BYTES) + steps * STEP_SECONDS
                cand = (-cost, tk, tm, tn)
                if best is None or cand > best[0]:
                    best = (cand, need)
    (_, tk, tm, tn), need = best
    return tm, tn, tk, need


def _matmul(a, b, mode, name, out_dtype=F32, add=None):
    parts = a if isinstance(a, tuple) else (a,)
    na = len(parts)
    wide = parts[0].shape[1]
    if mode == "nn":
        (m, k), (k2, n) = (parts[0].shape[0], na * wide), b.shape
    elif mode == "nt":
        (m, k), (n, k2) = (parts[0].shape[0], na * wide), b.shape
    else:
        (k, m), (k2, n) = (parts[0].shape[0], na * wide), b.shape
    assert k == k2, (parts[0].shape, b.shape, mode)
    split = {} if na == 1 else ({"tm_max": wide} if mode == "tn" else {"tk_only": wide})
    tm, tn, tk, need = _mm_tiles(m, n, k, parts[0].dtype.itemsize, b.dtype.itemsize, jnp.dtype(out_dtype).itemsize,
                                 add is not None, parts=na, **split)
    nk = k // tk
    per_part = wide // (tm if mode == "tn" else tk)
    dims = {"nn": (((1,), (0,)), ((), ())), "nt": (((1,), (1,)), ((), ())), "tn": (((0,), (0,)), ((), ()))}[mode]

    def body(*refs):
        a_refs, b_ref = refs[:na], refs[na]
        add_ref = refs[na + 1] if add is not None else None
        o_ref = refs[na + 2] if add is not None else refs[na + 1]
        kk = pl.program_id(2)

        def finish(r):
            if add_ref is not None:
                r = r + add_ref[...]
            o_ref[...] = r.astype(o_ref.dtype)

        def use(a_ref):
            part = lax.dot_general(a_ref[...].astype(BF16), b_ref[...].astype(BF16), dims, preferred_element_type=F32)
            if nk == 1:
                finish(part)
                return
            acc = refs[-1]

            @pl.when(kk == 0)
            def _():
                acc[...] = part

            @pl.when(kk > 0)
            def _():
                acc[...] += part

            @pl.when(kk == nk - 1)
            def _():
                finish(acc[...])

        if na == 1:
            use(a_refs[0])
        else:
            which = (pl.program_id(0) if mode == "tn" else kk) // per_part
            for p in range(na):
                pl.when(which == p)(functools.partial(use, a_refs[p]))

    def a_spec(p):
        def along(pos):
            return jnp.clip(pos - p * per_part, 0, per_part - 1) if na > 1 else pos
        if mode == "tn":
            return pl.BlockSpec((tk, tm), lambda i, j, kk: (kk, along(i)))
        return pl.BlockSpec((tm, tk), lambda i, j, kk: (i, along(kk)))

    if mode == "nn":
        b_spec = pl.BlockSpec((tk, tn), lambda i, j, kk: (kk, j))
    elif mode == "nt":
        b_spec = pl.BlockSpec((tn, tk), lambda i, j, kk: (j, kk))
    else:
        b_spec = pl.BlockSpec((tk, tn), lambda i, j, kk: (kk, j))
    o_spec = pl.BlockSpec((tm, tn), lambda i, j, kk: (i, j))
    in_specs = [a_spec(p) for p in range(na)] + [b_spec] + ([o_spec] if add is not None else [])
    args = parts + (b,) + ((add,) if add is not None else ())
    return _blocked(
        body, name=name, grid=(m // tm, n // tn, nk),
        in_specs=in_specs, out_specs=o_spec,
        out_shape=jax.ShapeDtypeStruct((m, n), out_dtype),
        scratch_shapes=[pltpu.VMEM((tm, tn), F32)] if nk > 1 else [],
        compiler_params=_cparams(("parallel", "parallel", "arbitrary"), min(VMEM_BIG, need + VMEM_SLACK)),
    )(*args)


TOK = 256
ROWS = 512


def _rms(x, g):
    return x * lax.rsqrt(jnp.mean(x * x, axis=-1, keepdims=True) + NORM_EPS) * g


def _rms_fwd(x, g, name):
    t, d = x.shape

    def body(x_ref, g_ref, o_ref):
        o_ref[...] = _rms(x_ref[...], g_ref[...]).astype(BF16)

    row = pl.BlockSpec((ROWS, d), lambda i: (i, 0))
    return _blocked(body, name=name, grid=(t // ROWS,), in_specs=[row, _full((1, d))], out_specs=row,
                          out_shape=jax.ShapeDtypeStruct((t, d), BF16),
                          compiler_params=_cparams(("parallel",), VMEM_MID))(x, g)


def _rms_bwd(x, g, dxn, res, name):
    t, d = x.shape

    def body(x_ref, g_ref, d_ref, res_ref, dx_ref, dg_ref):
        _, vjp = jax.vjp(_rms, x_ref[...], g_ref[...])
        dx, dg = vjp(d_ref[...].astype(F32))
        dx_ref[...] = dx + res_ref[...]

        @pl.when(pl.program_id(0) == 0)
        def _():
            dg_ref[...] = jnp.zeros_like(dg_ref)

        dg_ref[...] += dg

    row = pl.BlockSpec((ROWS, d), lambda i: (i, 0))
    return _blocked(body, name=name, grid=(t // ROWS,), in_specs=[row, _full((1, d)), row, row],
                          out_specs=[row, _full((1, d))],
                          out_shape=[jax.ShapeDtypeStruct((t, d), F32), jax.ShapeDtypeStruct((1, d), F32)],
                          compiler_params=_cparams(("arbitrary",), VMEM_MID))(x, g, dxn, res)


def _loss_head(x, g, tgt):
    t, d = x.shape

    def body(x_ref, g_ref, t_ref, l_ref, dx_ref, dg_ref):
        tg = t_ref[...]

        def fn(xv, gv):
            err = _rms(xv, gv) - tg
            per_tok = jnp.mean(err * err, axis=-1, keepdims=True)
            return 0.5 * jnp.sum(per_tok, axis=0, keepdims=True)

        l, vjp = jax.vjp(fn, x_ref[...], g_ref[...])
        dx, dg = vjp(jnp.ones((1, 1), F32))
        dx_ref[...] = dx

        @pl.when(pl.program_id(0) == 0)
        def _():
            dg_ref[...] = jnp.zeros_like(dg_ref)
            l_ref[...] = jnp.zeros_like(l_ref)

        dg_ref[...] += dg
        l_ref[...] += jnp.broadcast_to(l, l_ref.shape)

    row = pl.BlockSpec((ROWS, d), lambda i: (i, 0))
    return _blocked(body, name="loss_head", grid=(t // ROWS,), in_specs=[row, _full((1, d)), row],
                          out_specs=[_full((1, LANES)), row, _full((1, d))],
                          out_shape=[jax.ShapeDtypeStruct((1, LANES), F32), jax.ShapeDtypeStruct((t, d), F32),
                                     jax.ShapeDtypeStruct((1, d), F32)],
                          compiler_params=_cparams(("arbitrary",), VMEM_MID))(x, g, tgt)


def _glu_fwd(x, z):
    t, d = x.shape

    def body(x_ref, v_ref, g_ref, o_ref):
        o_ref[...] = x_ref[...] + v_ref[...] * jax.nn.sigmoid(g_ref[...])

    row = pl.BlockSpec((ROWS, d), lambda i: (i, 0))
    gate = pl.BlockSpec((ROWS, d), lambda i: (i, 1))
    return _blocked(body, name="glu_fwd", grid=(t // ROWS,), in_specs=[row, row, gate], out_specs=row,
                          out_shape=jax.ShapeDtypeStruct((t, d), F32),
                          compiler_params=_cparams(("parallel",), VMEM_MID))(x, z, z)


def _glu_bwd(z, g):
    t, d = g.shape

    def body(v_ref, g_ref, d_ref, o_ref):
        s = jax.nn.sigmoid(g_ref[...])
        dy = d_ref[...]
        o_ref[:, :d] = (dy * s).astype(BF16)
        o_ref[:, d:] = (dy * v_ref[...] * s * (1.0 - s)).astype(BF16)

    row = pl.BlockSpec((ROWS, d), lambda i: (i, 0))
    gate = pl.BlockSpec((ROWS, d), lambda i: (i, 1))
    return _blocked(body, name="glu_bwd", grid=(t // ROWS,), in_specs=[row, gate, row],
                          out_specs=pl.BlockSpec((ROWS, 2 * d), lambda i: (i, 0)),
                          out_shape=jax.ShapeDtypeStruct((t, 2 * d), BF16),
                          compiler_params=_cparams(("parallel",), VMEM_MID))(z, z, g)


def _shift_down(x, d):
    row = lax.broadcasted_iota(jnp.int32, x.shape, 0)
    return jnp.where(row < d, 0.0, pltpu.roll(x, d, 0))


def _shift_up(x, d):
    n = x.shape[0]
    row = lax.broadcasted_iota(jnp.int32, x.shape, 0)
    return jnp.where(row >= n - d, 0.0, pltpu.roll(x, n - d, 0))


def _make_sd():
    @functools.partial(jax.custom_vjp, nondiff_argnums=(1,))
    def sd(x, d):
        return _shift_down(x, d)

    def fwd(x, d):
        return _shift_down(x, d), None

    def bwd(d, _, g):
        return (_shift_up(g, d),)

    sd.defvjp(fwd, bwd)
    return sd


def _lin_scan(a, u, reverse=False):
    n = a.shape[0]
    row = lax.broadcasted_iota(jnp.int32, a.shape, 0)
    d = 1
    while d < n:
        if reverse:
            keep = row < n - d
            a_s, u_s = pltpu.roll(a, n - d, 0), pltpu.roll(u, n - d, 0)
        else:
            keep = row >= d
            a_s, u_s = pltpu.roll(a, d, 0), pltpu.roll(u, d, 0)
        u = u + a * jnp.where(keep, u_s, 0.0)
        a = a * jnp.where(keep, a_s, 1.0)
        d *= 2
    return u


def _make_scan():
    @jax.custom_vjp
    def scan(a, u):
        return _lin_scan(a, u)

    def fwd(a, u):
        h = _lin_scan(a, u)
        return h, (a, h)

    def bwd(res, dh):
        a, h = res
        g = _lin_scan(_shift_up(a, 1), dh, reverse=True)
        return g * _shift_down(h, 1), g

    scan.defvjp(fwd, bwd)
    return scan


def _acc_out(ref, val):
    @pl.when(pl.program_id(0) == 0)
    def _():
        ref[...] = jnp.zeros_like(ref)

    ref[...] += val


FFN_CW = 128


def _ffn_fn(hg, hv, wg, wv, bg, bv, sd):
    cg = wg[0:1] * sd(hg, 2) + wg[1:2] * sd(hg, 1) + wg[2:3] * hg + bg
    cv = wv[0:1] * sd(hv, 2) + wv[1:2] * sd(hv, 1) + wv[2:3] * hv + bv
    return jax.nn.silu(cg) * cv


def _ffn_specs(t):
    nb = D_FF // FFN_CW
    col = lambda r, off: pl.BlockSpec((r, FFN_CW), lambda j: (0, j + off))
    return nb, [col(t, 0), col(t, nb), col(3, 0), col(3, nb), col(1, 0), col(1, nb)], col


def _ffn_mid_fwd(h, cw, cb, name):
    t = h.shape[0]
    nb, in_specs, col = _ffn_specs(t)

    def body(hg, hv, wg, wv, bg, bv, o_ref):
        o_ref[...] = _ffn_fn(hg[...], hv[...], wg[...], wv[...], bg[...], bv[...], _shift_down).astype(BF16)

    return _blocked(body, name=name, grid=(nb,), in_specs=in_specs, out_specs=col(t, 0),
                          out_shape=jax.ShapeDtypeStruct((t, D_FF), BF16),
                          compiler_params=_cparams(("parallel",), VMEM_MID))(h, h, cw, cw, cb, cb)


def _ffn_mid_bwd(h, cw, cb, dact, name):
    t = h.shape[0]
    nb, in_specs, col = _ffn_specs(t)

    def body(hg, hv, wg, wv, bg, bv, d_ref, dhg, dhv, dwg, dwv, dbg, dbv):
        fn = functools.partial(_ffn_fn, sd=_make_sd())
        _, vjp = jax.vjp(fn, hg[...], hv[...], wg[...], wv[...], bg[...], bv[...])
        g = vjp(d_ref[...])
        dhg[...] = g[0].astype(BF16)
        dhv[...] = g[1].astype(BF16)
        dwg[...], dwv[...], dbg[...], dbv[...] = g[2], g[3], g[4], g[5]

    big = jax.ShapeDtypeStruct((t, D_FF), BF16)
    w3 = jax.ShapeDtypeStruct((3, D_FF), F32)
    b1 = jax.ShapeDtypeStruct((1, D_FF), F32)
    return _blocked(body, name=name, grid=(nb,), in_specs=in_specs + [col(t, 0)],
                          out_specs=[col(t, 0), col(t, 0), col(3, 0), col(3, 0), col(1, 0), col(1, 0)],
                          out_shape=[big, big, w3, w3, b1, b1],
                          compiler_params=_cparams(("parallel",), VMEM_BIG))(h, h, cw, cw, cb, cb, dact)


TS_CW = 256


def _tshift_fn(p, mu, sd):
    return p + mu * (sd(p, 1) - p)


def _tshift_fwd(p, mu):
    t = p.shape[0]
    col = lambda r: pl.BlockSpec((r, TS_CW), lambda j: (0, j))

    def body(p_ref, mu_ref, o_ref):
        o_ref[...] = _tshift_fn(p_ref[...], mu_ref[...], _shift_down)

    return _blocked(body, name="tshift_fwd", grid=(SHIFT_COLS // TS_CW,), in_specs=[col(t), col(1)],
                          out_specs=col(t), out_shape=jax.ShapeDtypeStruct((t, SHIFT_COLS), F32),
                          compiler_params=_cparams(("parallel",), VMEM_MID))(p, mu)


def _tshift_bwd(p, mu, dpam):
    t = p.shape[0]
    col = lambda r: pl.BlockSpec((r, TS_CW), lambda j: (0, j))

    def body(p_ref, mu_ref, d_ref, dp_ref, dmu_ref):
        _, vjp = jax.vjp(functools.partial(_tshift_fn, sd=_make_sd()), p_ref[...], mu_ref[...])
        dp, dmu = vjp(d_ref[...])
        dp_ref[...] = dp.astype(BF16)
        dmu_ref[...] = dmu

    return _blocked(body, name="tshift_bwd", grid=(SHIFT_COLS // TS_CW,), in_specs=[col(t), col(1), col(t)],
                          out_specs=[col(t), col(1)],
                          out_shape=[jax.ShapeDtypeStruct((t, SHIFT_COLS), BF16),
                                     jax.ShapeDtypeStruct((1, SHIFT_COLS), F32)],
                          compiler_params=_cparams(("parallel",), VMEM_MID))(p, mu, dpam)


_HI = lax.Precision.HIGHEST
_O = (0, RW, 2 * RW, 3 * RW, 3 * RW + W_LORA, 3 * RW + W_LORA + A_LORA, SHIFT_COLS)


def _dot16(a, b, dims=(((1,), (0,)), ((), ()))):
    return lax.dot_general(a.astype(BF16), b.astype(BF16), dims, preferred_element_type=F32)


def _make_dot16():
    @jax.custom_vjp
    def dot(a, b):
        return _dot16(a, b)

    def fwd(a, b):
        return _dot16(a, b), (a, b)

    def bwd(res, g):
        a, b = res
        return _dot16(g, b, (((1,), (1,)), ((), ()))), _dot16(a, g, (((0,), (0,)), ((), ())))

    dot.defvjp(fwd, bwd)
    return dot


def _seg(x):
    first = lax.broadcasted_iota(jnp.int32, (x.shape[0], LANES), 1) < HEAD
    parts = []
    for p in range(x.shape[1] // LANES):
        xp = x[:, p * LANES:(p + 1) * LANES]
        s0 = jnp.sum(jnp.where(first, xp, 0.0), axis=-1, keepdims=True)
        s1 = jnp.sum(jnp.where(first, 0.0, xp), axis=-1, keepdims=True)
        parts.append(jnp.where(first, s0, s1))
    return jnp.concatenate(parts, axis=1)


def _prep_fn(r, k, v, wd, ad, gd, w0, w2, a0, a2, g2, k_k, k_a, dot):
    w_log = -jax.nn.softplus(-(w0 + dot(jnp.tanh(wd), w2))) - 0.5
    decay = jnp.exp(-jnp.exp(w_log))
    a = jax.nn.sigmoid(a0 + dot(ad, a2))
    g = dot(jax.nn.sigmoid(gd), g2)
    kk = k * k_k
    kk = kk / jnp.maximum(jnp.sqrt(_seg(kk * kk)), 1e-12)
    k2 = k * (1.0 + (a - 1.0) * k_a)
    return r, decay, k2, v, -kk, kk * a, g


_PREP_W = ("w0", "w2", "a0", "a2", "g2", "k_k", "k_a")


def _prep_wspecs(w):
    return [_full(w[n].shape) for n in _PREP_W]


def _rwkv_prep_fwd(pam, w):
    t = pam.shape[0]

    def body(p_ref, *refs):
        wr, outs = refs[:7], refs[7:]
        pieces = [p_ref[:, _O[i]:_O[i + 1]] for i in range(6)]
        res = _prep_fn(*pieces, *[x[...] for x in wr], _dot16)
        for o, val in zip(outs, res):
            o[...] = val

    row = lambda c: pl.BlockSpec((TOK, c), lambda i: (i, 0))
    return _blocked(body, name="rwkv_prep_fwd", grid=(t // TOK,),
                          in_specs=[row(SHIFT_COLS)] + _prep_wspecs(w), out_specs=[row(RW)] * 7,
                          out_shape=[jax.ShapeDtypeStruct((t, RW), F32)] * 7,
                          compiler_params=_cparams(("parallel",), VMEM_MID))(pam, *[w[n] for n in _PREP_W])


def _rwkv_prep_bwd(pam, w, cts, more):
    t = pam.shape[0]

    def body(p_ref, *refs):
        wr, ct, ex, dp_ref, dws = refs[:7], refs[7:14], refs[14:17], refs[17], refs[18:]
        pieces = [p_ref[:, _O[i]:_O[i + 1]] for i in range(6)]
        fn = lambda *a: _prep_fn(*a, _make_dot16())
        _, vjp = jax.vjp(fn, *pieces, *[x[...] for x in wr])
        c = [x[...] for x in ct]
        c[0] = c[0] + ex[0][...]
        c[2] = c[2] + ex[1][...]
        c[3] = c[3] + ex[2][...]
        g = vjp(tuple(c))
        for i in range(6):
            dp_ref[:, _O[i]:_O[i + 1]] = g[i]
        for o, val in zip(dws, g[6:]):
            _acc_out(o, val)

    row = lambda c: pl.BlockSpec((TOK, c), lambda i: (i, 0))
    return _blocked(body, name="rwkv_prep_bwd", grid=(t // TOK,),
                          in_specs=[row(SHIFT_COLS)] + _prep_wspecs(w) + [row(RW)] * 10,
                          out_specs=[row(SHIFT_COLS)] + [_full(w[n].shape) for n in _PREP_W],
                          out_shape=[jax.ShapeDtypeStruct((t, SHIFT_COLS), F32)]
                          + [jax.ShapeDtypeStruct(w[n].shape, F32) for n in _PREP_W],
                          compiler_params=_cparams(("arbitrary",), VMEM_MID))(
                              pam, *[w[n] for n in _PREP_W], *cts, *more)


def _post_fn(y, r, k2, v, g, ln_w, ln_b, r_k):
    inv = 1.0 / HEAD
    d = y - _seg(y) * inv
    yn = d * lax.rsqrt(_seg(d * d) * inv + GN_EPS) * ln_w + ln_b
    bonus = _seg(r * k2 * r_k) * v
    return (yn + bonus) * g


def _rwkv_post_fwd(y, r, k2, v, g, ln_w, ln_b, r_k):
    t = y.shape[0]

    def body(*refs):
        o_ref = refs[-1]
        o_ref[...] = _post_fn(*[x[...] for x in refs[:-1]]).astype(BF16)

    row = pl.BlockSpec((TOK, RW), lambda i: (i, 0))
    return _blocked(body, name="rwkv_post_fwd", grid=(t // TOK,),
                          in_specs=[row] * 5 + [_full((1, RW))] * 3, out_specs=row,
                          out_shape=jax.ShapeDtypeStruct((t, RW), BF16),
                          compiler_params=_cparams(("parallel",), VMEM_MID))(y, r, k2, v, g, ln_w, ln_b, r_k)


def _rwkv_post_bwd(y, r, k2, v, g, ln_w, ln_b, r_k, dya):
    t = y.shape[0]

    def body(*refs):
        ins, d_ref, outs = refs[:8], refs[8], refs[9:]
        _, vjp = jax.vjp(_post_fn, *[x[...] for x in ins])
        gr = vjp(d_ref[...])
        for o, val in zip(outs[:5], gr[:5]):
            o[...] = val
        for o, val in zip(outs[5:], gr[5:]):
            _acc_out(o, val)

    row = pl.BlockSpec((TOK, RW), lambda i: (i, 0))
    vec = _full((1, RW))
    return _blocked(body, name="rwkv_post_bwd", grid=(t // TOK,),
                          in_specs=[row] * 5 + [vec] * 3 + [row],
                          out_specs=[row] * 5 + [vec] * 3,
                          out_shape=[jax.ShapeDtypeStruct((t, RW), F32)] * 5 + [jax.ShapeDtypeStruct((1, RW), F32)] * 3,
                          compiler_params=_cparams(("arbitrary",), VMEM_MID))(y, r, k2, v, g, ln_w, ln_b, r_k, dya)


def _from_pt(x):
    n = x.shape[0]
    return x.reshape(n, HEAD, N_HEADS, PT).transpose(0, 3, 2, 1).reshape(n * PT, N_HEADS * HEAD)


def _lane_sum(x):
    return jnp.sum(x, axis=-1, keepdims=True)


def _pair_consts():
    lane = lax.broadcasted_iota(jnp.int32, (HEAD, LANES), 1)
    return lane, lane < HEAD


def _seg_sum_pair(x, first):
    return jnp.where(first, _lane_sum(jnp.where(first, x, 0.0)), _lane_sum(jnp.where(first, 0.0, x)))


def _to_pt(x):
    t = x.shape[0]
    return x.reshape(t // PT, PT, N_HEADS, HEAD).transpose(0, 3, 2, 1).reshape(t // PT, HEAD, N_HEADS * PT)


def _expand_cols(x, name):
    t = x.shape[0]
    tiles = WKV_CHUNK // PT

    def body(x_ref, o_ref):
        _, first = _pair_consts()
        for tl in range(tiles):
            tile = x_ref[tl]
            for j in range(PT):
                for p in range(N_HEADS // 2):
                    src = jnp.where(first, (2 * p) * PT + j, (2 * p + 1) * PT + j)
                    o_ref[tl * PT + j, :, p * LANES:(p + 1) * LANES] = jnp.take_along_axis(tile, src, axis=1)

    return _blocked(
        body, name=name, grid=(t // WKV_CHUNK,),
        in_specs=[pl.BlockSpec((tiles, HEAD, LANES), lambda i: (i, 0, 0))],
        out_specs=pl.BlockSpec((WKV_CHUNK, HEAD, RW), lambda i: (i, 0, 0)),
        out_shape=jax.ShapeDtypeStruct((t, HEAD, RW), F32),
        compiler_params=_cparams(("parallel",), VMEM_MID))(_to_pt(x))


def _wkv_fwd(w, k, z, b, v_exp):
    t = w.shape[0]
    chunk = 2 * WKV_CHUNK
    nc = t // chunk
    pairs = N_HEADS // 2

    def body(w_ref, k_ref, z_ref, b_ref, v_ref, s_all, s_ref):
        @pl.when(pl.program_id(0) == 0)
        def _():
            s_ref[...] = jnp.zeros_like(s_ref)

        _, first = _pair_consts()

        def group(gi, carry):
            base = pl.multiple_of(gi * 8, 8)
            rows = [ref[pl.ds(base, 8), :] for ref in (w_ref, k_ref, z_ref, b_ref)]
            s = [s_ref[:, p * LANES:(p + 1) * LANES] for p in range(pairs)]
            for jj in range(8):
                for p in range(pairs):
                    cs = slice(p * LANES, (p + 1) * LANES)
                    wr, kr, zr, br = [x[jj:jj + 1, cs] for x in rows]
                    s_all[base + jj, :, cs] = s[p]
                    sa = _seg_sum_pair(s[p] * zr, first)
                    s[p] = s[p] * wr + sa * br + v_ref[base + jj, :, cs] * kr
            for p in range(pairs):
                s_ref[:, p * LANES:(p + 1) * LANES] = s[p]
            return carry

        lax.fori_loop(0, chunk // 8, group, 0)

    row = pl.BlockSpec((chunk, RW), lambda i: (i, 0))
    big = pl.BlockSpec((chunk, HEAD, RW), lambda i: (i, 0, 0))
    return _blocked(
        body, name="wkv_fwd", grid=(nc,), in_specs=[row] * 4 + [big], out_specs=[big, _full((HEAD, RW))],
        out_shape=[jax.ShapeDtypeStruct((t, HEAD, RW), F32), jax.ShapeDtypeStruct((HEAD, RW), F32)],
        compiler_params=_cparams(("arbitrary",), VMEM_BIG))(w, k, z, b, v_exp)


def _wkv_out(r, s_all, s_last):
    t = r.shape[0]
    nc = t // WKV_CHUNK
    tiles = WKV_CHUNK // PT
    pairs = N_HEADS // 2

    def body(r_ref, s_ref, nxt_ref, last_ref, y_ref):
        lane, first = _pair_consts()
        after = jnp.where(pl.program_id(0) == nc - 1, last_ref[...], nxt_ref[0])
        for tl in range(tiles):
            ytile = jnp.zeros((HEAD, LANES), F32)
            for g in range(PT // 8):
                rows = r_ref[tl * PT + g * 8:tl * PT + g * 8 + 8, :]
                for jj in range(8):
                    tt = tl * PT + g * 8 + jj
                    j = g * 8 + jj
                    for p in range(pairs):
                        cs = slice(p * LANES, (p + 1) * LANES)
                        s = s_ref[tt + 1, :, cs] if tt + 1 < WKV_CHUNK else after[:, cs]
                        pr = s * rows[jj:jj + 1, cs]
                        y0 = _lane_sum(jnp.where(first, pr, 0.0))
                        y1 = _lane_sum(jnp.where(first, 0.0, pr))
                        ytile = jnp.where(lane == (2 * p) * PT + j, y0, ytile)
                        ytile = jnp.where(lane == (2 * p + 1) * PT + j, y1, ytile)
            y_ref[tl] = ytile

    row = pl.BlockSpec((WKV_CHUNK, RW), lambda i: (i, 0))
    pt = pl.BlockSpec((tiles, HEAD, LANES), lambda i: (i, 0, 0))
    big = pl.BlockSpec((WKV_CHUNK, HEAD, RW), lambda i: (i, 0, 0))
    nxt = pl.BlockSpec((1, HEAD, RW), lambda i: (jnp.minimum((i + 1) * WKV_CHUNK, t - 1), 0, 0))
    return _blocked(
        body, name="wkv_out", grid=(nc,), in_specs=[row, big, nxt, _full((HEAD, RW))], out_specs=pt,
        out_shape=jax.ShapeDtypeStruct((t // PT, HEAD, LANES), F32),
        compiler_params=_cparams(("parallel",), VMEM_MID))(r, s_all, s_all, s_last)


def _wkv_bwd(r, w, k, z, b, v_exp, s_all, dy_exp):
    t = r.shape[0]
    nc = t // WKV_CHUNK
    tiles = WKV_CHUNK // PT
    pairs = N_HEADS // 2

    def body(r_ref, w_ref, k_ref, z_ref, b_ref, v_ref, s_all_ref, dy_ref,
             dr_ref, dw_ref, dk_ref, dz_ref, db_ref, dv_ref, ds_ref):
        @pl.when(pl.program_id(0) == 0)
        def _():
            ds_ref[...] = jnp.zeros_like(ds_ref)

        lane, first = _pair_consts()
        col_sum = lambda x: jnp.sum(x, axis=0, keepdims=True)
        row8 = lax.broadcasted_iota(jnp.int32, (8, LANES), 0)
        for tl in reversed(range(tiles)):
            def group(gg, dvtile):
                gi = PT // 8 - 1 - gg
                base = pl.multiple_of(tl * PT + gi * 8, 8)
                rows = [ref[pl.ds(base, 8), :] for ref in (r_ref, w_ref, k_ref, z_ref, b_ref)]
                outs = (dr_ref, dw_ref, dk_ref, dz_ref, db_ref)
                tiles8 = {(id(o), p): jnp.zeros((8, LANES), F32) for o in outs for p in range(pairs)}
                ds = [ds_ref[:, p * LANES:(p + 1) * LANES] for p in range(pairs)]
                for jj in reversed(range(8)):
                    j = gi * 8 + jj
                    for p in range(pairs):
                        cs = slice(p * LANES, (p + 1) * LANES)

                        def put(ref, val, p=p, jj=jj):
                            tiles8[(id(ref), p)] = jnp.where(row8 == jj, val, tiles8[(id(ref), p)])

                        rr, wr, kr, zr, br = [x[jj:jj + 1, cs] for x in rows]
                        sp = s_all_ref[base + jj, :, cs]
                        vc = v_ref[base + jj, :, cs]
                        dyc = dy_ref[base + jj, :, cs]
                        sa = _seg_sum_pair(sp * zr, first)
                        st = sp * wr + sa * br + vc * kr
                        d = ds[p] + dyc * rr
                        put(dr_ref, col_sum(st * dyc))
                        dvk = d * kr
                        dv0 = _lane_sum(jnp.where(first, dvk, 0.0))
                        dv1 = _lane_sum(jnp.where(first, 0.0, dvk))
                        dvtile = jnp.where(lane == (2 * p) * PT + j, dv0, dvtile)
                        dvtile = jnp.where(lane == (2 * p + 1) * PT + j, dv1, dvtile)
                        put(dk_ref, col_sum(d * vc))
                        put(dw_ref, col_sum(sp * d))
                        u = _seg_sum_pair(d * br, first)
                        put(dz_ref, col_sum(sp * u))
                        put(db_ref, col_sum(d * sa))
                        ds[p] = d * wr + u * zr
                for p in range(pairs):
                    ds_ref[:, p * LANES:(p + 1) * LANES] = ds[p]
                for o in outs:
                    for p in range(pairs):
                        o[pl.ds(base, 8), p * LANES:(p + 1) * LANES] = tiles8[(id(o), p)]
                return dvtile

            dv_ref[tl] = lax.fori_loop(0, PT // 8, group, jnp.zeros((HEAD, LANES), F32))

    rev = lambda i: nc - 1 - i
    row = pl.BlockSpec((WKV_CHUNK, RW), lambda i: (rev(i), 0))
    pt = pl.BlockSpec((tiles, HEAD, LANES), lambda i: (rev(i), 0, 0))
    big = pl.BlockSpec((WKV_CHUNK, HEAD, RW), lambda i: (rev(i), 0, 0))
    return _blocked(
        body, name="wkv_bwd", grid=(nc,), in_specs=[row] * 5 + [big, big, big], out_specs=[row] * 5 + [pt],
        out_shape=[jax.ShapeDtypeStruct((t, RW), F32)] * 5 + [jax.ShapeDtypeStruct((t // PT, HEAD, LANES), F32)],
        scratch_shapes=[pltpu.VMEM((HEAD, RW), F32)],
        compiler_params=_cparams(("arbitrary",), VMEM_BIG))(r, w, k, z, b, v_exp, s_all, dy_exp)


LRU_CW = 128
_BX0 = SHIFT_COLS // LRU_CW
_BG0 = (SHIFT_COLS + LRU_W) // LRU_CW


def _lru_fn(bx, bg, cw, cb, ga, ba, gx, bxb, lam, sd, scan, dot):
    xc = cw[0:1] * sd(bx, 3) + cw[1:2] * sd(bx, 2) + cw[2:3] * sd(bx, 1) + cw[3:4] * bx + cb
    gr = jax.nn.sigmoid(dot(xc, ga) + ba)
    gi = jax.nn.sigmoid(dot(xc, gx) + bxb)
    log_a = -LRU_C * gr * jax.nn.softplus(-lam)
    a = jnp.exp(log_a)
    mult = jnp.sqrt(-jnp.tanh(log_a) * (jnp.exp(2.0 * log_a) + 1.0))
    return scan(a, xc * gi * mult) * jax.nn.gelu(bg)


def _lru_specs(t):
    col = lambda r, off=0: pl.BlockSpec((r, LRU_CW), lambda j: (0, j + off))
    diag = pl.BlockSpec((LRU_CW, LRU_CW), lambda j: (j, j))
    return col, [col(t, _BX0), col(t, _BG0), col(4), col(1), diag, col(1), diag, col(1), col(1)]


def _lru_fwd(p, cw, cb, ga, ba, gx, bxb, lam):
    t = p.shape[0]
    col, in_specs = _lru_specs(t)

    def body(*refs):
        o_ref = refs[-1]
        o_ref[...] = _lru_fn(*[x[...] for x in refs[:-1]], _shift_down, _lin_scan, _dot16).astype(BF16)

    return _blocked(body, name="lru_fwd", grid=(LRU_W // LRU_CW,), in_specs=in_specs, out_specs=col(t),
                          out_shape=jax.ShapeDtypeStruct((t, LRU_W), BF16),
                          compiler_params=_cparams(("parallel",), VMEM_MID))(p, p, cw, cb, ga, ba, gx, bxb, lam)


def _lru_bwd(p, cw, cb, ga, ba, gx, bxb, lam, dyb):
    t = p.shape[0]
    col, in_specs = _lru_specs(t)

    def body(*refs):
        ins, d_ref, outs = refs[:9], refs[9], refs[10:]
        fn = functools.partial(_lru_fn, sd=_make_sd(), scan=_make_scan(), dot=_make_dot16())
        _, vjp = jax.vjp(fn, *[x[...] for x in ins])
        g = vjp(d_ref[...])
        outs[0][...] = g[0].astype(BF16)
        outs[1][...] = g[1].astype(BF16)
        for o, val in zip(outs[2:], g[2:]):
            o[...] = val

    sq = pl.BlockSpec((LRU_CW, LRU_CW), lambda j: (j, 0))
    act = jax.ShapeDtypeStruct((t, LRU_W), BF16)
    vec = jax.ShapeDtypeStruct((1, LRU_W), F32)
    sqs = jax.ShapeDtypeStruct((LRU_W, LRU_CW), F32)
    return _blocked(body, name="lru_bwd", grid=(LRU_W // LRU_CW,), in_specs=in_specs + [col(t, RW // LRU_CW)],
                          out_specs=[col(t), col(t), col(4), col(1), sq, col(1), sq, col(1), col(1)],
                          out_shape=[act, act, jax.ShapeDtypeStruct((4, LRU_W), F32), vec, sqs, vec, sqs, vec, vec],
                          compiler_params=_cparams(("parallel",), VMEM_BIG))(p, p, cw, cb, ga, ba, gx, bxb, lam, dyb)


def _s5_disc_fn(a_re, a_im, log_dt, b_re, b_im, e):
    lam_re = jnp.minimum(a_re, -1e-4)
    lam_im = a_im
    dt = jnp.exp(log_dt)
    mag = jnp.exp(lam_re * dt)
    ab_re = mag * jnp.cos(lam_im * dt)
    ab_im = mag * jnp.sin(lam_im * dt)
    den = lam_re * lam_re + lam_im * lam_im
    zr = ab_re - 1.0
    q_re = jnp.dot((zr * lam_re + ab_im * lam_im) / den, e, precision=_HI)
    q_im = jnp.dot((ab_im * lam_re - zr * lam_im) / den, e, precision=_HI)
    return ab_re, ab_im, q_re * b_re - q_im * b_im, q_re * b_im + q_im * b_re


def _s5_disc_fwd(a_re, a_im, log_dt, b_re, b_im, e):
    def body(*refs):
        res = _s5_disc_fn(*[x[...] for x in refs[:6]])
        for o, val in zip(refs[6:], res):
            o[...] = val

    small = jax.ShapeDtypeStruct(a_re.shape, F32)
    wide = jax.ShapeDtypeStruct(b_re.shape, F32)
    return pl.pallas_call(body, name="s5_disc_fwd", out_shape=[small, small, wide, wide])(
        a_re, a_im, log_dt, b_re, b_im, e)


def _s5_disc_bwd(a_re, a_im, log_dt, b_re, b_im, e, cts):
    def body(*refs):
        ins, e_ref, ct, outs = refs[:5], refs[5], refs[6:10], refs[10:]
        _, vjp = jax.vjp(lambda *a: _s5_disc_fn(*a, e_ref[...]), *[x[...] for x in ins])
        for o, val in zip(outs, vjp(tuple(c[...] for c in ct))):
            o[...] = val

    shapes = [jax.ShapeDtypeStruct(x.shape, F32) for x in (a_re, a_im, log_dt, b_re, b_im)]
    return pl.pallas_call(body, name="s5_disc_bwd", out_shape=shapes)(a_re, a_im, log_dt, b_re, b_im, e, *cts)


def _cmul(a, b):
    return a[0] * b[0] - a[1] * b[1], a[0] * b[1] + a[1] * b[0]


def _s5_scan(sr, si, ab, reverse):
    n_tiles = sr.shape[0] // 8
    width = sr.shape[1]
    row8 = lax.broadcasted_iota(jnp.int32, (8, width), 0)
    p1 = ab
    p2 = _cmul(p1, p1)
    p4 = _cmul(p2, p2)
    pw = [p1]
    for _ in range(7):
        pw.append(_cmul(pw[-1], p1))
    cr = jnp.zeros((8, width), F32)
    ci = jnp.zeros((8, width), F32)
    for j in range(8):
        e = pw[7 - j] if reverse else pw[j]
        cr = jnp.where(row8 == j, e[0], cr)
        ci = jnp.where(row8 == j, e[1], ci)

    levels = []
    for d, q in ((1, p1), (2, p2), (4, p4)):
        keep = row8 < 8 - d if reverse else row8 >= d
        levels.append((d, (jnp.where(keep, q[0], 0.0), jnp.where(keep, q[1], 0.0))))

    def tile(i, carry):
        idx = n_tiles - 1 - i if reverse else i
        base = pl.multiple_of(idx * 8, 8)
        x = (sr[pl.ds(base, 8), :], si[pl.ds(base, 8), :])
        for d, q in levels:
            amt = 8 - d if reverse else d
            m = _cmul(q, (pltpu.roll(x[0], amt, 0), pltpu.roll(x[1], amt, 0)))
            x = (x[0] + m[0], x[1] + m[1])
        m = _cmul((cr, ci), carry)
        x = (x[0] + m[0], x[1] + m[1])
        sr[pl.ds(base, 8), :] = x[0]
        si[pl.ds(base, 8), :] = x[1]
        edge = slice(0, 1) if reverse else slice(7, 8)
        return x[0][edge], x[1][edge]

    zero = jnp.zeros((1, width), F32)
    lax.fori_loop(0, n_tiles, tile, (zero, zero))


_S5_W = S5_SLAB // S5_GROUP * S5_STATE


def _s5_specs(t):
    col = lambda r: pl.BlockSpec((r, S5_SLAB), lambda j: (0, j))
    bb = pl.BlockSpec((None, S5_SLAB, _S5_W), lambda j: (j, 0, 0))
    cd = pl.BlockSpec((None, _S5_W, S5_SLAB), lambda j: (j, 0, 0))
    ab = pl.BlockSpec((None, 1, _S5_W), lambda j: (j, 0, 0))
    return col, bb, cd, ab


def _s5_fwd(u, dvec, bbr, bbi, cdr, cdi, abr, abi):
    t, width = u.shape
    col, bb, cd, ab = _s5_specs(t)

    def body(u_ref, d_ref, bbr_ref, bbi_ref, cdr_ref, cdi_ref, abr_ref, abi_ref, o_ref, sr, si):
        uv = u_ref[...]
        sr[...] = _dot16(uv, bbr_ref[...])
        si[...] = _dot16(uv, bbi_ref[...])
        _s5_scan(sr, si, (abr_ref[...], abi_ref[...]), False)
        y = _dot16(sr[...], cdr_ref[...]) - _dot16(si[...], cdi_ref[...])
        o_ref[...] = jax.nn.gelu(y + d_ref[...] * uv).astype(BF16)

    return _blocked(body, name="s5_fwd", grid=(width // S5_SLAB,),
                          in_specs=[col(t), col(1), bb, bb, cd, cd, ab, ab], out_specs=col(t),
                          out_shape=jax.ShapeDtypeStruct((t, width), BF16),
                          scratch_shapes=[pltpu.VMEM((t, _S5_W), F32)] * 2,
                          compiler_params=_cparams(("parallel",), VMEM_BIG))(u, dvec, bbr, bbi, cdr, cdi, abr, abi)


def _s5_bwd(u, dvec, bbr, bbi, cdr, cdi, abr, abi, dyact):
    t, width = u.shape
    col, bb, cd, ab = _s5_specs(t)
    ns = width // S5_SLAB
    tn = (((0,), (0,)), ((), ()))
    nt = (((1,), (1,)), ((), ()))

    def body(u_ref, d_ref, bbr_ref, bbi_ref, cdr_ref, cdi_ref, abr_ref, abi_ref, dy_ref,
             du_ref, dd_ref, dbbr_ref, dbbi_ref, dcdr_ref, dcdi_ref, dabr_ref, dabi_ref, sr, si, gr, gi):
        uv = u_ref[...]
        dv = d_ref[...]
        abv = (abr_ref[...], abi_ref[...])
        sr[...] = _dot16(uv, bbr_ref[...])
        si[...] = _dot16(uv, bbi_ref[...])
        _s5_scan(sr, si, abv, False)
        y = _dot16(sr[...], cdr_ref[...]) - _dot16(si[...], cdi_ref[...])
        _, vjp = jax.vjp(jax.nn.gelu, y + dv * uv)
        (dpre,) = vjp(dy_ref[...].astype(F32))
        dd_ref[...] = jnp.sum(dpre * uv, axis=0, keepdims=True)
        dcdr_ref[...] = _dot16(sr[...], dpre, tn)
        dcdi_ref[...] = -_dot16(si[...], dpre, tn)
        gr[...] = _dot16(dpre, cdr_ref[...], nt)
        gi[...] = -_dot16(dpre, cdi_ref[...], nt)
        _s5_scan(gr, gi, (abv[0], -abv[1]), True)

        row8 = lax.broadcasted_iota(jnp.int32, (8, _S5_W), 0)

        def tile(i, carry):
            acc_r, acc_i, last_r, last_i = carry
            base = pl.multiple_of(i * 8, 8)
            s_r, s_i = sr[pl.ds(base, 8), :], si[pl.ds(base, 8), :]
            g_r, g_i = gr[pl.ds(base, 8), :], gi[pl.ds(base, 8), :]
            p_r = jnp.where(row8 == 0, last_r, pltpu.roll(s_r, 1, 0))
            p_i = jnp.where(row8 == 0, last_i, pltpu.roll(s_i, 1, 0))
            acc_r = acc_r + jnp.sum(g_r * p_r + g_i * p_i, axis=0, keepdims=True)
            acc_i = acc_i + jnp.sum(g_i * p_r - g_r * p_i, axis=0, keepdims=True)
            return acc_r, acc_i, s_r[7:8], s_i[7:8]

        zero = jnp.zeros((1, _S5_W), F32)
        acc_r, acc_i, _, _ = lax.fori_loop(0, t // 8, tile, (zero, zero, zero, zero))
        dabr_ref[...] = acc_r
        dabi_ref[...] = acc_i
        du_ref[...] = dpre * dv + _dot16(gr[...], bbr_ref[...], nt) + _dot16(gi[...], bbi_ref[...], nt)
        dbbr_ref[...] = _dot16(uv, gr[...], tn)
        dbbi_ref[...] = _dot16(uv, gi[...], tn)

    sds = jax.ShapeDtypeStruct
    return _blocked(
        body, name="s5_bwd", grid=(ns,), in_specs=[col(t), col(1), bb, bb, cd, cd, ab, ab, col(t)],
        out_specs=[col(t), col(1), bb, bb, cd, cd, ab, ab],
        out_shape=[sds((t, width), F32), sds((1, width), F32), sds((ns, S5_SLAB, _S5_W), F32),
                   sds((ns, S5_SLAB, _S5_W), F32), sds((ns, _S5_W, S5_SLAB), F32), sds((ns, _S5_W, S5_SLAB), F32),
                   sds((ns, 1, _S5_W), F32), sds((ns, 1, _S5_W), F32)],
        scratch_shapes=[pltpu.VMEM((t, _S5_W), F32)] * 4,
        compiler_params=_cparams(("parallel",), VMEM_BIG))(u, dvec, bbr, bbi, cdr, cdi, abr, abi, dyact)


def _gate_dense(w):
    h = w.shape[0]
    return jnp.einsum("hij,hg->higj", w, jnp.eye(h, dtype=F32)).reshape(h * HEAD, h * HEAD)


def _gate_blocks(d):
    x = d.reshape(LRU_W // LRU_CW, 2, HEAD, 2, HEAD)
    return jnp.einsum("tgihj,gh->tgij", x, jnp.eye(2, dtype=F32)).reshape(LRU_W // HEAD, HEAD, HEAD)


_GPS = S5_SLAB // S5_GROUP
_NS = S5_GROUPS // _GPS


def _s5_in_dense(bb):
    x = bb.reshape(_NS, _GPS, S5_STATE, S5_GROUP)
    return jnp.einsum("sgnc,gh->sgchn", x, jnp.eye(_GPS, dtype=F32)).reshape(_NS, S5_SLAB, _S5_W)


def _s5_in_blocks(d):
    x = d.reshape(_NS, _GPS, S5_GROUP, _GPS, S5_STATE)
    return jnp.einsum("sgchn,gh->sgnc", x, jnp.eye(_GPS, dtype=F32)).reshape(S5_GROUPS, S5_STATE * S5_GROUP)


def _s5_out_dense(c):
    x = c.reshape(_NS, _GPS, S5_GROUP, S5_STATE)
    return jnp.einsum("sgcn,gh->shngc", x, jnp.eye(_GPS, dtype=F32)).reshape(_NS, _S5_W, S5_SLAB)


def _s5_out_blocks(d):
    x = d.reshape(_NS, _GPS, S5_STATE, _GPS, S5_GROUP)
    return jnp.einsum("shngc,gh->sgcn", x, jnp.eye(_GPS, dtype=F32)).reshape(S5_GROUPS, S5_GROUP, S5_STATE)


def _local_step(x, tgt, w, late_weights, send_grads):
    d_model = x.shape[1]
    gs = {}
    n_layers = w["f_norm_g"].shape[0]

    def ffn_fwd(xin, l):
        xn = _rms_fwd(xin, w["f_norm_g"][l:l + 1], f"rms_f{l}")
        h = _matmul(xn, w["f_w_up_t"][l], "nt", f"mm_f{l}_up")
        act = _ffn_mid_fwd(h, w["f_conv_w"][l], w["f_conv_b"][l:l + 1], f"ffn_mid_fwd{l}")
        return _matmul(act, w["f_w_down"][l], "nn", f"mm_f{l}_down", add=xin), (xin, xn, h, act)

    def ffn_bwd(g, saved, l):
        xin, xn, h, act = saved
        dact = _matmul(g, w["f_w_down"][l], "nt", f"mm_f{l}_dact")
        d_down = _matmul(act, g, "tn", f"mm_f{l}_ddown", out_dtype=BF16)
        dhg, dhv, dwg, dwv, dbg, dbv = _ffn_mid_bwd(h, w["f_conv_w"][l], w["f_conv_b"][l:l + 1], dact,
                                                    f"ffn_mid_bwd{l}")
        dxn = _matmul((dhg, dhv), w["f_w_up_t"][l], "nn", f"mm_f{l}_dxn")
        d_up = _matmul((dhg, dhv), xn, "tn", f"mm_f{l}_dup", out_dtype=BF16)
        dx, dgn = _rms_bwd(xin, w["f_norm_g"][l:l + 1], dxn, g, f"rms_f{l}_bwd")
        return dx, d_up, d_down, jnp.concatenate([dwg, dwv], axis=1), jnp.concatenate([dbg, dbv], axis=1), dgn

    xn0 = _rms_fwd(x, w["e_norm_g"], "rms_e")
    p = _matmul(xn0, w["e_w_in_t"], "nt", "mm_e_in")
    pam = _tshift_fwd(p, w["e_mu"])
    pw = dict(w0=w["e_w0"], w2=w["e_w2"][0], a0=w["e_a0"], a2=w["e_a2"][0], g2=w["e_g2"][0],
              k_k=w["e_k_k"], k_a=w["e_k_a"])
    r, dec, k2, v, z, b, gate = _rwkv_prep_fwd(pam, pw)
    v_exp = _expand_cols(v, "wkv_expand_v")
    s_all, s_last = _wkv_fwd(dec, k2, z, b, v_exp)
    y_pt = _wkv_out(r, s_all, s_last)
    y = _from_pt(y_pt)
    rk = w["e_r_k"].reshape(1, RW)
    ya = _rwkv_post_fwd(y, r, k2, v, gate, w["e_ln_w"], w["e_ln_b"], rk)
    ga, gx = _gate_dense(w["e_gate_a_w"][0]), _gate_dense(w["e_gate_x_w"][0])
    lru_w = (w["e_conv_w"][0], w["e_conv_b"], ga, w["e_gate_a_b"], gx, w["e_gate_x_b"], w["e_lru_lambda"])
    yb = _lru_fwd(p, *lru_w)
    ycat = jnp.concatenate([ya, yb], axis=1)
    w = {**w, **late_weights(ycat)}
    x1 = _matmul(ycat, w["e_w_out"], "nn", "mm_e_out", add=x)
    x2, ffn0 = ffn_fwd(x1, 0)

    xn1 = _rms_fwd(x2, w["o_norm_g"], "rms_o")
    u = _matmul(xn1, w["o_w_in"], "nn", "mm_o_in")
    expand = jnp.kron(jnp.eye(S5_STATE, dtype=F32), jnp.ones((1, S5_GROUP), F32))
    disc_in = (w["o_A_re"][0], w["o_A_im"][0], w["o_log_dt"].reshape(S5_GROUPS, 1),
               w["o_B_re"][0].reshape(S5_GROUPS, -1), w["o_B_im"][0].reshape(S5_GROUPS, -1), expand)
    ab_re, ab_im, bb_re, bb_im = _s5_disc_fwd(*disc_in)
    s5_w = (w["o_D"], _s5_in_dense(bb_re), _s5_in_dense(bb_im), _s5_out_dense(w["o_C_re"][0]),
            _s5_out_dense(w["o_C_im"][0]), ab_re.reshape(_NS, 1, _S5_W), ab_im.reshape(_NS, 1, _S5_W))
    yact = _s5_fwd(u, *s5_w)
    zz = _matmul(yact, w["o_w_glu_t"], "nt", "mm_o_glu")
    x3 = _glu_fwd(x2, zz)
    x4, ffn1 = ffn_fwd(x3, 1)

    loss, g, gs["final_norm_g", 0] = _loss_head(x4, w["final_norm_g"].reshape(1, d_model), tgt)

    g, up1, down1, dcw1, dcb1, dfn1 = ffn_bwd(g, ffn1, 1)
    dz = _glu_bwd(zz, g)
    dyact = _matmul(dz, w["o_w_glu_t"], "nn", "mm_o_dyact")
    d_glu = _matmul(dz, yact, "tn", "mm_o_dglu", out_dtype=BF16)
    du, gs["o_D", 0], dbbr, dbbi, dcdr, dcdi, dabr, dabi = _s5_bwd(u, *s5_w, dyact)
    gs["o_C_re", 0] = _s5_out_blocks(dcdr).reshape(S5_GROUPS * S5_GROUP, S5_STATE)
    gs["o_C_im", 0] = _s5_out_blocks(dcdi).reshape(S5_GROUPS * S5_GROUP, S5_STATE)
    cts = (dabr.reshape(S5_GROUPS, S5_STATE), dabi.reshape(S5_GROUPS, S5_STATE), _s5_in_blocks(dbbr),
           _s5_in_blocks(dbbi))
    gs["o_A_re", 0], gs["o_A_im", 0], dlog_dt, gs["o_B_re", 0], gs["o_B_im", 0] = _s5_disc_bwd(*disc_in, cts)
    gs["o_log_dt", 0] = dlog_dt.reshape(1, S5_GROUPS)
    dxn = _matmul(du, w["o_w_in"], "nt", "mm_o_dxn")
    d_oin = _matmul(xn1, du, "tn", "mm_o_din", out_dtype=BF16)
    g, gs["o_norm_g", 0] = _rms_bwd(x2, w["o_norm_g"], dxn, g, "rms_o_bwd")
    g = send_grads("a", [("f_w_up", 1, up1), ("f_w_down", 1, down1), ("o_w_glu", 0, d_glu), ("o_w_in", 0, d_oin)], g)

    g, up0, down0, dcw0, dcb0, dfn0 = ffn_bwd(g, ffn0, 0)
    gs["f_conv_w", 0], gs["f_conv_w", 3] = dcw0, dcw1
    gs["f_conv_b", 0], gs["f_conv_b", 1] = dcb0, dcb1
    gs["f_norm_g", 0], gs["f_norm_g", 1] = dfn0, dfn1

    dycat = _matmul(g, w["e_w_out"], "nt", "mm_e_dycat")
    d_eout = _matmul(ycat, g, "tn", "mm_e_dout", out_dtype=BF16)
    dycat = send_grads("b", [("f_w_up", 0, up0), ("f_w_down", 0, down0), ("e_w_out", 0, d_eout)], dycat)
    dy, dr1, dk1, dv1, dgate, gs["e_ln_w", 0], gs["e_ln_b", 0], gs["e_r_k", 0] = _rwkv_post_bwd(
        y, r, k2, v, gate, w["e_ln_w"], w["e_ln_b"], rk, dycat)
    dr2, ddec, dk2, dzz, dbb, dv_pt = _wkv_bwd(r, dec, k2, z, b, v_exp, s_all, _expand_cols(dy, "wkv_expand_dy"))
    (dpam, gs["e_w0", 0], gs["e_w2", 0], gs["e_a0", 0], gs["e_a2", 0], gs["e_g2", 0], gs["e_k_k", 0],
     gs["e_k_a", 0]) = _rwkv_prep_bwd(pam, pw, (dr2, ddec, dk2, _from_pt(dv_pt), dzz, dbb, dgate), (dr1, dk1, dv1))
    dpa, gs["e_mu", 0] = _tshift_bwd(p, w["e_mu"], dpam)
    (dbx, dbg, gs["e_conv_w", 0], gs["e_conv_b", 0], dga, gs["e_gate_a_b", 0], dgx, gs["e_gate_x_b", 0],
     gs["e_lru_lambda", 0]) = _lru_bwd(p, *lru_w, dycat)
    gs["e_gate_a_w", 0] = _gate_blocks(dga).reshape(LRU_W, HEAD)
    gs["e_gate_x_w", 0] = _gate_blocks(dgx).reshape(LRU_W, HEAD)
    dp = jnp.concatenate([dpa, dbx, dbg], axis=1)
    d_ein = _matmul(dp, xn0, "tn", "mm_e_din", out_dtype=BF16)
    dp = send_grads("c", [("e_w_in", 0, d_ein)], dp)
    dxn = _matmul(dp, w["e_w_in_t"], "nn", "mm_e_dxn")
    grad_x, gs["e_norm_g", 0] = _rms_bwd(x, w["e_norm_g"], dxn, g, "rms_e_bwd")
    return loss, grad_x, gs


CAST_ROWS = 256


def _cast_shard(w3, layer, transpose, chip, name, after=None):
    _, rows, cols = w3.shape
    tr = _tile(rows, (CAST_ROWS, 176, 128))

    def body(c_ref, w_ref, *rest):
        v = w_ref[...]
        rest[-1][...] = (v.T if transpose else v).astype(BF16)

    in_spec = pl.BlockSpec((None, tr, cols), lambda i, c: (layer, i, 0))
    if transpose:
        out_spec, shape = pl.BlockSpec((None, cols, tr), lambda i, c: (c[0], 0, i)), (cols, rows)
    else:
        out_spec, shape = pl.BlockSpec((None, tr, cols), lambda i, c: (c[0], i, 0)), (rows, cols)
    extra = [] if after is None else [after]
    grid_spec = pltpu.PrefetchScalarGridSpec(num_scalar_prefetch=1, grid=(rows // tr,),
                                             in_specs=[in_spec] + [_ANY] * len(extra), out_specs=out_spec)
    return _blocked(body, name=name, grid_spec=grid_spec,
                          out_shape=jax.ShapeDtypeStruct((N_CHIPS,) + shape, BF16),
                          compiler_params=_cparams(("parallel",), VMEM_MID))(chip, w3, *extra)


_ANY = pl.BlockSpec(memory_space=pl.ANY)


def _coords():
    return lax.axis_index("x"), lax.axis_index("y"), lax.axis_index("c")


def _flip(v, d):
    return 1 - v if d else v


_CHIP_RELS = ((1, 0), (0, 1), (1, 1))
_DEV_RELS = tuple((dx, dy, dc) for dx in (0, 1) for dy in (0, 1) for dc in (0, 1))[1:]


_HBM = pl.BlockSpec(memory_space=pltpu.HBM)
_SEM = pl.BlockSpec(memory_space=pltpu.SEMAPHORE)
_EFFECT = pltpu.SideEffectType.DATAFLOW_SIDE_EFFECTING


def _in_hbm(a):
    return pltpu.with_memory_space_constraint(a, pltpu.HBM)


def _gather_copies(bufs, send, recv, landed):
    x, y, c = _coords()
    me = 2 * x + y
    res = []
    for i, buf in enumerate(bufs):
        for j, (dx, dy) in enumerate(_CHIP_RELS):
            px, py = _flip(x, dx), _flip(y, dy)
            k = i * len(_CHIP_RELS) + j
            res.append(pltpu.make_async_remote_copy(
                src_ref=buf.at[me], dst_ref=buf.at[2 * px + py if landed else me], send_sem=send.at[k],
                recv_sem=recv.at[k], device_id=(px, py, c), device_id_type=MESH))
    return res


def _scatter_copies(srcs, lands, send, recv, landed):
    x, y, c = _coords()
    me = 4 * x + 2 * y + c
    res = []
    for i, (src, land) in enumerate(zip(srcs, lands)):
        for j, (dx, dy, dc) in enumerate(_DEV_RELS):
            peer = (_flip(x, dx), _flip(y, dy), _flip(c, dc))
            pid = 4 * peer[0] + 2 * peer[1] + peer[2]
            k = i * len(_DEV_RELS) + j
            res.append(pltpu.make_async_remote_copy(
                src_ref=src.at[pid], dst_ref=land.at[pid if landed else me], send_sem=send.at[k],
                recv_sem=recv.at[k], device_id=peer, device_id_type=MESH))
    return res


def _split_start(bufs, n_src, copies, n_rel, name, after):
    n = len(bufs)
    nk = n_src * n_rel

    def body(*refs):
        ins, send, recv, token = refs[:n], refs[n + 1 + n], refs[n + 2 + n], refs[-1]
        for cp in copies(ins, send, recv, False):
            cp.start()
        token[...] = jnp.zeros_like(token)

    res = pl.pallas_call(
        body, name=name, in_specs=[_HBM] * n + [_ANY],
        out_specs=[_HBM] * n + [_SEM, _SEM, pl.BlockSpec(memory_space=pltpu.VMEM)],
        out_shape=[pltpu.HBM(b.shape, b.dtype) for b in bufs]
        + [pltpu.SemaphoreType.DMA((nk,)), pltpu.SemaphoreType.DMA((nk,)), jax.ShapeDtypeStruct((8, LANES), F32)],
        input_output_aliases={i: i for i in range(n)},
        compiler_params=pltpu.CompilerParams(has_side_effects=_EFFECT))(*[_in_hbm(b) for b in bufs], after)
    return res[n], res[n + 1], list(res[:n]), res[n + 2]


def _split_wait(bufs, send, recv, copies, name, after):
    n = len(bufs)

    def body(*refs):
        ins, send_ref, recv_ref = refs[:n], refs[n], refs[n + 1]
        for cp in copies(ins, send_ref, recv_ref, True):
            cp.wait_send()
            cp.wait_recv()

    return pl.pallas_call(
        body, name=name, in_specs=[_HBM] * n + [_SEM, _SEM, _ANY], out_specs=[_HBM] * n,
        out_shape=[pltpu.HBM(b.shape, b.dtype) for b in bufs], input_output_aliases={i: i for i in range(n)},
        compiler_params=pltpu.CompilerParams(has_side_effects=_EFFECT))(*bufs, send, recv, after)


def _gather_start(bufs, name, after):
    return _split_start(bufs, len(bufs), _gather_copies, len(_CHIP_RELS), name, after)


def _gather_wait(bufs, send, recv, name, after):
    return _split_wait(bufs, send, recv, _gather_copies, name, after)


def _scatter_start(srcs, name, after):
    n = len(srcs)
    lands = [lax.empty(a.shape, a.dtype) for a in srcs]
    fn = lambda refs, send, recv, landed: _scatter_copies(refs[:n], refs[n:], send, recv, landed)
    send, recv, bufs, token = _split_start(list(srcs) + lands, n, fn, len(_DEV_RELS), name, after)
    return send, recv, bufs, token


def _scatter_wait(bufs, send, recv, name, after):
    n = len(bufs) // 2
    fn = lambda refs, s, r, landed: _scatter_copies(refs[:n], refs[n:], s, r, landed)
    res = _split_wait(bufs, send, recv, fn, name, after)
    return res[:n], res[n:]


def _sum_segments(src, land, me, name):
    nd, seg, cols = src.shape
    ts = _tile(seg, (256, 176, 128))

    def body(m_ref, *refs):
        o_ref = refs[-1]
        acc = refs[0][...].astype(F32)
        for r in refs[1:-1]:
            acc = acc + r[...].astype(F32)
        o_ref[...] = acc

    def peer(rel):
        bits = 4 * rel[0] + 2 * rel[1] + rel[2]
        return pl.BlockSpec((None, ts, cols), lambda i, m: (jnp.bitwise_xor(m[0], bits), i, 0))

    grid_spec = pltpu.PrefetchScalarGridSpec(
        num_scalar_prefetch=1, grid=(seg // ts,),
        in_specs=[pl.BlockSpec((None, ts, cols), lambda i, m: (m[0], i, 0))] + [peer(r) for r in _DEV_RELS],
        out_specs=pl.BlockSpec((None, ts, cols), lambda i, m: (m[1], i, 0)))
    return _blocked(body, name=name, grid_spec=grid_spec,
                          out_shape=jax.ShapeDtypeStruct((2, seg, cols), F32),
                          compiler_params=_cparams(("parallel",), VMEM_MID))(me, src, *[land] * len(_DEV_RELS))


def _exchange_sibling(arrs):
    n = len(arrs)

    def body(*refs):
        outs, (send, recv) = refs[n:2 * n], refs[2 * n:]
        x, y, c = _coords()
        sib = (x, y, 1 - c)
        sends, recvs = [], []
        for i in range(n):
            cp = pltpu.make_async_remote_copy(src_ref=outs[i].at[c], dst_ref=outs[i].at[c], send_sem=send.at[i],
                                              recv_sem=recv.at[i], device_id=sib, device_id_type=MESH)
            cp.start()
            sends.append(cp)
            recvs.append(pltpu.make_async_remote_copy(src_ref=outs[i].at[c], dst_ref=outs[i].at[1 - c],
                                                      send_sem=send.at[i], recv_sem=recv.at[i], device_id=sib,
                                                      device_id_type=MESH))
        for cp in recvs:
            cp.wait_recv()
        for cp in sends:
            cp.wait_send()

    return pl.pallas_call(
        body, name="exchange_sibling", in_specs=[_ANY] * n, out_specs=[_ANY] * n,
        out_shape=[jax.ShapeDtypeStruct(a.shape, a.dtype) for a in arrs],
        input_output_aliases={i: i for i in range(n)},
        scratch_shapes=[pltpu.SemaphoreType.DMA((n,)), pltpu.SemaphoreType.DMA((n,))])(*arrs)


def _allreduce_small(vec):
    _, nchips, seg, lanes = vec.shape
    nr = len(_CHIP_RELS)

    def body(in_ref, out_ref, from_sib, half, stage, red, send, recv):
        x, y, c = _coords()
        me = 2 * x + y
        sib = (x, y, 1 - c)
        chips = [(_flip(x, dx), _flip(y, dy)) for dx, dy in _CHIP_RELS]

        def copy(src, dst, k, peer):
            return pltpu.make_async_remote_copy(src_ref=src, dst_ref=dst, send_sem=send.at[k], recv_sem=recv.at[k],
                                                device_id=peer, device_id_type=MESH)

        to_sib = copy(in_ref.at[1 - c], from_sib, 0, sib)
        to_sib.start()
        to_sib.wait_recv()
        half[...] = in_ref[c] + from_sib[...]

        first = [copy(half.at[2 * px + py], stage.at[me], 1 + j, (px, py, c)) for j, (px, py) in enumerate(chips)]
        for cp in first:
            cp.start()
        stage[me] = half[me]
        for j, (px, py) in enumerate(chips):
            copy(half.at[2 * px + py], stage.at[2 * px + py], 1 + j, (px, py, c)).wait_recv()
        acc = stage[0]
        for k in range(1, nchips):
            acc = acc + stage[k]
        red[...] = acc
        out_ref[c, me] = acc

        second = [copy(red, out_ref.at[c, me], 1 + nr + j, (px, py, c)) for j, (px, py) in enumerate(chips)]
        for cp in second:
            cp.start()
        for j, (px, py) in enumerate(chips):
            copy(red, out_ref.at[c, 2 * px + py], 1 + nr + j, (px, py, c)).wait_recv()

        back = copy(out_ref.at[c], out_ref.at[c], 1 + 2 * nr, sib)
        back.start()
        copy(out_ref.at[c], out_ref.at[1 - c], 1 + 2 * nr, sib).wait_recv()
        for cp in [to_sib] + first + second + [back]:
            cp.wait_send()

    vm = pl.BlockSpec(memory_space=pltpu.VMEM)
    nsem = 2 + 2 * nr
    return pl.pallas_call(
        body, name="allreduce_small", in_specs=[vm], out_specs=vm,
        out_shape=jax.ShapeDtypeStruct(vec.shape, F32),
        scratch_shapes=[pltpu.VMEM((nchips, seg, lanes), F32), pltpu.VMEM((nchips, seg, lanes), F32),
                        pltpu.VMEM((nchips, seg, lanes), F32), pltpu.VMEM((seg, lanes), F32),
                        pltpu.SemaphoreType.DMA((nsem,)), pltpu.SemaphoreType.DMA((nsem,))],
        compiler_params=_cparams(None, VMEM_MID))(vec)


def _adam_math(w, g, m, v):
    m2 = ADAM_B1 * m + (1.0 - ADAM_B1) * g
    v2 = ADAM_B2 * v + (1.0 - ADAM_B2) * (g * g)
    m_hat = m2 / (1.0 - ADAM_B1 ** ADAM_STEP)
    v_hat = v2 / (1.0 - ADAM_B2 ** ADAM_STEP)
    return -ADAM_LR * (m_hat / (jnp.sqrt(v_hat) + ADAM_EPS) + ADAM_WD * w), m2, v2


def _adamw_big(w3, m3, v3, layer, g, transposed, name, prev=None):
    nl, rows, cols = w3.shape
    tr = 128 if transposed else _tile(rows, (256, 176, 128))

    def body(w_ref, m_ref, v_ref, g_ref, *rest):
        go_ref, d_ref, mo_ref, vo_ref = rest[-4:]
        g_val = g_ref[...].T if transposed else g_ref[...]
        go_ref[...] = g_val
        d_ref[...], mo_ref[...], vo_ref[...] = _adam_math(w_ref[...], g_val, m_ref[...], v_ref[...])

    wspec = pl.BlockSpec((None, tr, cols), lambda i: (layer, i, 0))
    gspec = pl.BlockSpec((cols, tr), lambda i: (0, i)) if transposed else pl.BlockSpec((tr, cols), lambda i: (i, 0))
    extra = [] if prev is None else list(prev)
    return _blocked(body, name=name, grid=(rows // tr,),
                          in_specs=[wspec, wspec, wspec, gspec] + [_ANY] * len(extra),
                          out_specs=[wspec] * 4, out_shape=[jax.ShapeDtypeStruct((nl, rows, cols), F32)] * 4,
                          input_output_aliases={4 + i: i for i in range(len(extra))},
                          compiler_params=_cparams(("parallel",), VMEM_MID))(w3, m3, v3, g, *extra)


_SMALL = (
    ("e_norm_g", (1, D_MODEL), None), ("e_mu", (1, SHIFT_COLS), None), ("e_w0", (1, RW), None),
    ("e_w2", (W_LORA, RW), RW // 4), ("e_a0", (1, RW), None), ("e_a2", (A_LORA, RW), RW // 4),
    ("e_g2", (G_LORA, RW), RW // 4), ("e_k_k", (1, RW), None), ("e_k_a", (1, RW), None), ("e_r_k", (1, RW), None),
    ("e_ln_w", (1, RW), None), ("e_ln_b", (1, RW), None), ("e_conv_w", (4, LRU_W), LRU_W // 4),
    ("e_conv_b", (1, LRU_W), None), ("e_gate_a_w", (LRU_W, HEAD), None), ("e_gate_a_b", (1, LRU_W), None),
    ("e_gate_x_w", (LRU_W, HEAD), None), ("e_gate_x_b", (1, LRU_W), None), ("e_lru_lambda", (1, LRU_W), None),
    ("o_norm_g", (1, D_MODEL), D_MODEL // 4), ("o_A_re", (S5_GROUPS, S5_STATE), None),
    ("o_A_im", (S5_GROUPS, S5_STATE), None), ("o_log_dt", (1, S5_GROUPS), None),
    ("o_B_re", (S5_GROUPS, S5_STATE * S5_GROUP), None), ("o_B_im", (S5_GROUPS, S5_STATE * S5_GROUP), None),
    ("o_C_re", (S5_GROUPS * S5_GROUP, S5_STATE), None), ("o_C_im", (S5_GROUPS * S5_GROUP, S5_STATE), None),
    ("o_D", (1, D_MODEL), D_MODEL // 4), ("f_norm_g", (2, D_MODEL), None),
    ("f_conv_w", (6, 2 * D_FF), 2 * D_FF // 4), ("f_conv_b", (2, 2 * D_FF), None),
    ("final_norm_g", (1, D_MODEL), None))
_PIECES = {"f_norm_g": ((0, 1), (1, 1)), "f_conv_b": ((0, 1), (1, 1)), "f_conv_w": ((0, 3), (3, 3))}


def _ceil_to(n, m):
    return -(-n // m) * m


def _small_layout():
    groups = {}
    for name, (rows, cols), _ in _SMALL:
        for first, r in _PIECES.get(name, ((0, rows),)):
            groups.setdefault(cols, []).append((name, first, r))
    layout, off = {}, 0
    for cols, items in groups.items():
        stacks = [0, 0] if 2 * cols <= LANES else [0]
        placed = []
        for name, first, r in sorted(items, key=lambda it: -it[2]):
            half = stacks.index(min(stacks))
            r0 = stacks[half]
            if r >= 8 or r0 % 8 + r > 8:
                r0 = _ceil_to(r0, 8)
            placed.append((name, first, r, r0, half * (LANES // 2)))
            stacks[half] = r0 + r
        rpad = _ceil_to(max(stacks), 8)
        for name, first, r, at, lane in placed:
            layout[name, first] = (off, rpad, at, r, cols, lane)
        off += -(-cols // LANES) * rpad
    return layout, _ceil_to(off, 8 * N_DEV)


def _small_pack(gs):
    layout, total = _small_layout()
    keys = list(layout)

    def body(*refs):
        out = refs[-1]
        out[...] = jnp.zeros_like(out)
        for key, g_ref in zip(keys, refs[:-1]):
            off, rpad, at, r, cols, lane = layout[key]
            for j in range(-(-cols // LANES)):
                cw = min(LANES, cols - j * LANES)
                out[off + j * rpad + at:off + j * rpad + at + r, lane:lane + cw] = g_ref[:, j * LANES:j * LANES + cw]

    return pl.pallas_call(body, name="small_pack", out_shape=jax.ShapeDtypeStruct((total, LANES), F32),
                          compiler_params=_cparams(None, VMEM_MID))(*[gs[k] for k in keys])


def _adamw_small(red, chip, wts, ms, vs):
    layout, _ = _small_layout()
    names = [n for n, _, _ in _SMALL]
    n = len(names)

    def body(chip_ref, red_ref, *refs):
        ins, outs = refs[:3 * n], refs[3 * n:]
        c = chip_ref[0]
        for i, (name, (rows, cols), loc) in enumerate(_SMALL):
            w_ref, m_ref, v_ref = ins[3 * i:3 * i + 3]
            o_refs = outs[4 * i:4 * i + 4]
            width = cols if loc is None else loc
            for first, r in _PIECES.get(name, ((0, rows),)):
                off, rpad, at, _, _, lane = layout[name, first]
                for j in range(-(-width // LANES)):
                    cw = min(LANES, width - j * LANES)
                    ls = slice(lane, lane + cw)
                    if loc is None:
                        start = off + j * rpad + at
                        g = red_ref[start:start + r, ls]
                    else:
                        blk = c * (loc // LANES) + j
                        if r >= 8:
                            g = red_ref[pl.ds(pl.multiple_of(off + at + blk * rpad, 8), r), ls]
                        else:
                            tile = red_ref[pl.ds(pl.multiple_of(off + at // 8 * 8 + blk * rpad, 8), 8), ls]
                            g = tile[at % 8:at % 8 + r]
                    rs, cs = slice(first, first + r), slice(j * LANES, j * LANES + cw)
                    d, m2, v2 = _adam_math(w_ref[rs, cs], g, m_ref[rs, cs], v_ref[rs, cs])
                    for o, val in zip(o_refs, (g, d, m2, v2)):
                        o[rs, cs] = val

    args, shapes = [], []
    for name in names:
        args += [wts[name], ms[name], vs[name]]
        shapes += [jax.ShapeDtypeStruct(wts[name].shape, F32)] * 4
    vm = pl.BlockSpec(memory_space=pltpu.VMEM)
    res = pl.pallas_call(body, name="adamw_small",
                         in_specs=[pl.BlockSpec(memory_space=pltpu.SMEM), vm] + [vm] * (3 * n),
                         out_specs=[vm] * (4 * n), out_shape=shapes,
                         compiler_params=_cparams(None, VMEM_BIG))(chip, red, *args)
    return {name: res[4 * i:4 * i + 4] for i, name in enumerate(names)}


PACK_ROWS = 8


def _packed_rows(shape):
    size = 1
    for d in shape:
        size *= d
    return -(-size // (PACK_ROWS * LANES)) * PACK_ROWS


def _pack(arrs, row_mult):
    parts = []
    for a in arrs:
        flat = a.reshape(-1).astype(F32)
        rows = _packed_rows(a.shape)
        parts.append(jnp.pad(flat, (0, rows * LANES - flat.shape[0])).reshape(rows, LANES))
    total = sum(p.shape[0] for p in parts)
    fill = -(-total // row_mult) * row_mult - total
    if fill:
        parts.append(jnp.zeros((fill, LANES), F32))
    return jnp.concatenate(parts, axis=0)


def _unpack(packed, shapes):
    out, off = [], 0
    for s in shapes:
        rows = _packed_rows(s)
        size = 1
        for d in s:
            size *= d
        out.append(packed[off:off + rows].reshape(-1)[:size].reshape(s))
        off += rows
    return out


_SMALL_SH = ("e_w2", "e_a2", "e_g2", "e_conv_w", "o_norm_g", "o_D", "f_conv_w")
_LARGE = (("e_w_in", True), ("e_w_out", False), ("o_w_in", False), ("o_w_glu", True), ("f_w_up", True),
        ("f_w_down", False))
_ORDER = ("e_norm_g", "e_w_in", "e_mu", "e_w0", "e_w2", "e_a0", "e_a2", "e_g2", "e_k_k", "e_k_a", "e_r_k", "e_ln_w",
          "e_ln_b", "e_conv_w", "e_conv_b", "e_gate_a_w", "e_gate_a_b", "e_gate_x_w", "e_gate_x_b", "e_lru_lambda",
          "e_w_out", "o_norm_g", "o_w_in", "o_A_re", "o_A_im", "o_log_dt", "o_B_re", "o_B_im", "o_C_re", "o_C_im",
          "o_D", "o_w_glu", "f_norm_g", "f_w_up", "f_conv_w", "f_conv_b", "f_w_down", "final_norm_g")
N_CHIPS = 4
N_DEV = 8


def _step(x, tgt, wts, ms, vs):
    xi, yi, ci = _coords()
    chip = 2 * xi + yi
    chip1 = chip.astype(jnp.int32).reshape(1)
    me2 = jnp.stack([4 * xi + 2 * yi + ci, ci]).astype(jnp.int32)
    by_cols = dict(_LARGE)

    cast = lambda name, l, after=None: _cast_shard(wts[name], l, by_cols[name], chip1, f"cast_{name}{l}", after)
    sh_shapes = [wts[n].shape for n in _SMALL_SH]
    packed = _pack([wts[n] for n in _SMALL_SH], 8)
    small_buf = lax.dynamic_update_slice(jnp.zeros((N_CHIPS,) + packed.shape, F32), packed[None], (chip, 0, 0))
    late = [(name, l) for name, _ in _LARGE if name != "e_w_in" for l in range(wts[name].shape[0])]
    send, recv, thru, token = _gather_start([cast("e_w_in", 0), small_buf], "gather_start_a", x)
    bufs = {(name, l): cast(name, l, token) for name, l in late}
    got = _gather_wait(thru, send, recv, "gather_wait_a", bufs[late[-1]])
    send_b, recv_b, thru_b, token = _gather_start([bufs[k] for k in late], "gather_start_b", got[0])
    x, _ = lax.optimization_barrier((x, token))

    def rows(g):
        return g.reshape(N_CHIPS * g.shape[1], g.shape[2])

    full = {n: wts[n] for n, _, loc in _SMALL if loc is None}
    full["e_w_in_t"] = rows(got[0])
    per_chip = [_unpack(got[1][k], sh_shapes) for k in range(N_CHIPS)]
    for i, n in enumerate(_SMALL_SH):
        full[n] = jnp.concatenate([per_chip[k][i] for k in range(N_CHIPS)], axis=-1)

    def late_weights(after):
        res = dict(zip(late, _gather_wait(thru_b, send_b, recv_b, "gather_wait_b", after)))
        return {"e_w_out": rows(res[("e_w_out", 0)]), "o_w_in": rows(res[("o_w_in", 0)]),
                "o_w_glu_t": rows(res[("o_w_glu", 0)]),
                "f_w_up_t": [rows(res[("f_w_up", l)]) for l in range(2)],
                "f_w_down": [rows(res[("f_w_down", l)]) for l in range(2)]}

    pending = []

    def send_grads(tag, items, carry):
        srcs = [g.reshape(N_DEV, g.shape[0] // N_DEV, g.shape[1]) for _, _, g in items]
        s_sem, r_sem, both, tok = _scatter_start(srcs, f"scatter_start_{tag}", carry)
        pending.append((tag, [(name, l) for name, l, _ in items], s_sem, r_sem, both))
        carry, _ = lax.optimization_barrier((carry, tok))
        return carry

    loss, grad_x, gs = _local_step(x, tgt, full, late_weights, send_grads)

    final = {}
    red = _allreduce_small(_small_pack(gs).reshape(2, N_CHIPS, -1, LANES)).reshape(-1, LANES)
    view = {name: (rows, cols if loc is None else loc) for name, (rows, cols), loc in _SMALL}
    as2d = lambda d: {name: d[name].reshape(view[name]) for name in view}
    small = _adamw_small(red, chip1, as2d(wts), as2d(ms), as2d(vs))
    for name, res in small.items():
        final[name] = [r.reshape(wts[name].shape) for r in res]
    new_v = small["final_norm_g"][3]

    halves, keys = [], []
    for tag, names, s_sem, r_sem, both in pending:
        srcs, lands = _scatter_wait(both, s_sem, r_sem, f"scatter_wait_{tag}", new_v)
        for (name, l), src, land in zip(names, srcs, lands):
            halves.append(_sum_segments(src, land, me2, f"sum_{name}{l}"))
            keys.append((name, l))
    shards = _exchange_sibling(halves)
    for s, (name, l) in zip(shards, keys):
        final[name] = _adamw_big(wts[name], ms[name], vs[name], l, s.reshape(2 * s.shape[1], s.shape[2]),
                                 by_cols[name], f"adamw_{name}{l}", prev=final.get(name))

    loss = lax.psum(loss[0, 0], ("x", "y", "c"))
    res = [loss, grad_x[None]]
    for k in range(4):
        res += [final[n][k] for n in _ORDER]
    return tuple(res)


def kernel(x, e_norm_g, e_w_in, e_mu, e_w0, e_w2, e_a0, e_a2, e_g2, e_k_k, e_k_a, e_r_k, e_ln_w, e_ln_b, e_conv_w, e_conv_b, e_gate_a_w, e_gate_a_b, e_gate_x_w, e_gate_x_b, e_lru_lambda, e_w_out, o_norm_g, o_w_in, o_A_re, o_A_im, o_log_dt, o_B_re, o_B_im, o_C_re, o_C_im, o_D, o_w_glu, f_norm_g, f_w_up, f_conv_w, f_conv_b, f_w_down, final_norm_g, loss_target, m_e_norm_g, m_e_w_in, m_e_mu, m_e_w0, m_e_w2, m_e_a0, m_e_a2, m_e_g2, m_e_k_k, m_e_k_a, m_e_r_k, m_e_ln_w, m_e_ln_b, m_e_conv_w, m_e_conv_b, m_e_gate_a_w, m_e_gate_a_b, m_e_gate_x_w, m_e_gate_x_b, m_e_lru_lambda, m_e_w_out, m_o_norm_g, m_o_w_in, m_o_A_re, m_o_A_im, m_o_log_dt, m_o_B_re, m_o_B_im, m_o_C_re, m_o_C_im, m_o_D, m_o_w_glu, m_f_norm_g, m_f_w_up, m_f_conv_w, m_f_conv_b, m_f_w_down, m_final_norm_g, v_e_norm_g, v_e_w_in, v_e_mu, v_e_w0, v_e_w2, v_e_a0, v_e_a2, v_e_g2, v_e_k_k, v_e_k_a, v_e_r_k, v_e_ln_w, v_e_ln_b, v_e_conv_w, v_e_conv_b, v_e_gate_a_w, v_e_gate_a_b, v_e_gate_x_w, v_e_gate_x_b, v_e_lru_lambda, v_e_w_out, v_o_norm_g, v_o_w_in, v_o_A_re, v_o_A_im, v_o_log_dt, v_o_B_re, v_o_B_im, v_o_C_re, v_o_C_im, v_o_D, v_o_w_glu, v_f_norm_g, v_f_w_up, v_f_conv_w, v_f_conv_b, v_f_w_down, v_final_norm_g):
    args = locals()
    wts = {n: args[n] for n in _ORDER}
    ms = {n: args["m_" + n] for n in _ORDER}
    vs = {n: args["v_" + n] for n in _ORDER}
    return _step(x[0], loss_target[0], wts, ms, vs)
```

```python
import functools

import jax
import jax.numpy as jnp
from jax import lax
from jax.experimental import pallas as pl
from jax.experimental.pallas import tpu as pltpu

F32 = jnp.float32
BF16 = jnp.bfloat16
MESH = pl.DeviceIdType.MESH

D_MODEL = 1024
HEAD = 64
RW = 512
N_HEADS = RW // HEAD
LRU_W = 512
SHIFT_COLS = 1792
W_LORA, A_LORA, G_LORA = 64, 64, 128
S5_GROUPS, S5_GROUP, S5_STATE = 64, 16, 64
D_FF = 2816
NORM_EPS = 1e-6
GN_EPS = 64e-5
LRU_C = 8.0
ADAM_LR, ADAM_B1, ADAM_B2, ADAM_EPS, ADAM_WD, ADAM_STEP = 0.001, 0.9, 0.999, 1e-08, 0.01, 10

VMEM_BIG = 56 * 1024 * 1024
VMEM_MID = 40 * 1024 * 1024
LANES = 128
PT = 16
WKV_CHUNK = 32
S5_SLAB = 128


def _blocked(*args, **kw):
    call = pl.pallas_call(*args, **kw)

    def run(*ops):
        return call(*[pltpu.with_memory_space_constraint(a, pltpu.HBM) if a.ndim >= 2 else a for a in ops])

    return run


def _cparams(sem=None, vmem=None):
    kw = {}
    if sem is not None:
        kw["dimension_semantics"] = sem
    if vmem is not None:
        kw["vmem_limit_bytes"] = vmem
    return pltpu.CompilerParams(**kw)


def _tile(dim, cands):
    for c in cands:
        if dim % c == 0:
            return c
    return dim


def _full(shape):
    n = len(shape)
    return pl.BlockSpec(shape, lambda *_: (0,) * n)


_TILES = (2816, 2048, 1408, 1024, 512, 256, 128)
MM_BUDGET = 36 * 1024 * 1024
VMEM_SLACK = 12 * 1024 * 1024


MXU_FLOPS = 9.0e14
HBM_BYTES = 3.3e12
STEP_SECONDS = 0.35e-6


def _mm_tiles(m, n, k, size_a, size_b, size_o, has_add, parts=1, tk_only=None, tm_max=None):
    best = None
    for tm in _TILES:
        for tk in _TILES:
            for tn in _TILES:
                if m % tm or n % tn or k % tk or (tk_only and tk != tk_only) or (tm_max and tm_max % tm):
                    continue
                need = (2 * (parts * tm * tk * size_a + tk * tn * size_b + tm * tn * size_o)
                        + tm * tn * 4 * (1 + 2 * has_add))
                if k > tk:
                    need += tm * tn * 4
                if need > MM_BUDGET:
                    continue
                steps = (m // tm) * (n // tn) * (k // tk)
                a_reads = n // tn if k > tk else 1
                moved = (m * k * size_a * a_reads + k * n * size_b * (m // tm) + m * n * (size_o + 4 * has_add))
                cost = max(2.0 * m * n * k / MXU_FLOPS, moved / HBM_BYTES) + steps * STEP_SECONDS
                cand = (-cost, tk, tm, tn)
                if best is None or cand > best[0]:
                    best = (cand, need)
    (_, tk, tm, tn), need = best
    return tm, tn, tk, need


def _matmul(a, b, mode, name, out_dtype=F32, add=None):
    parts = a if isinstance(a, tuple) else (a,)
    na = len(parts)
    wide = parts[0].shape[1]
    if mode == "nn":
        (m, k), (k2, n) = (parts[0].shape[0], na * wide), b.shape
    elif mode == "nt":
        (m, k), (n, k2) = (parts[0].shape[0], na * wide), b.shape
    else:
        (k, m), (k2, n) = (parts[0].shape[0], na * wide), b.shape
    assert k == k2, (parts[0].shape, b.shape, mode)
    split = {} if na == 1 else ({"tm_max": wide} if mode == "tn" else {"tk_only": wide})
    tm, tn, tk, need = _mm_tiles(m, n, k, parts[0].dtype.itemsize, b.dtype.itemsize, jnp.dtype(out_dtype).itemsize,
                                 add is not None, parts=na, **split)
    nk = k // tk
    per_part = wide // (tm if mode == "tn" else tk)
    dims = {"nn": (((1,), (0,)), ((), ())), "nt": (((1,), (1,)), ((), ())), "tn": (((0,), (0,)), ((), ()))}[mode]

    def body(*refs):
        a_refs, b_ref = refs[:na], refs[na]
        add_ref = refs[na + 1] if add is not None else None
        o_ref = refs[na + 2] if add is not None else refs[na + 1]
        kk = pl.program_id(2)

        def finish(r):
            if add_ref is not None:
                r = r + add_ref[...]
            o_ref[...] = r.astype(o_ref.dtype)

        def use(a_ref):
            part = lax.dot_general(a_ref[...].astype(BF16), b_ref[...].astype(BF16), dims, preferred_element_type=F32)
            if nk == 1:
                finish(part)
                return
            acc = refs[-1]

            @pl.when(kk == 0)
            def _():
                acc[...] = part

            @pl.when(kk > 0)
            def _():
                acc[...] += part

            @pl.when(kk == nk - 1)
            def _():
                finish(acc[...])

        if na == 1:
            use(a_refs[0])
        else:
            which = (pl.program_id(0) if mode == "tn" else kk) // per_part
            for p in range(na):
                pl.when(which == p)(functools.partial(use, a_refs[p]))

    def a_spec(p):
        def along(pos):
            return jnp.clip(pos - p * per_part, 0, per_part - 1) if na > 1 else pos
        if mode == "tn":
            return pl.BlockSpec((tk, tm), lambda i, j, kk: (kk, along(i)))
        return pl.BlockSpec((tm, tk), lambda i, j, kk: (i, along(kk)))

    if mode == "nn":
        b_spec = pl.BlockSpec((tk, tn), lambda i, j, kk: (kk, j))
    elif mode == "nt":
        b_spec = pl.BlockSpec((tn, tk), lambda i, j, kk: (j, kk))
    else:
        b_spec = pl.BlockSpec((tk, tn), lambda i, j, kk: (kk, j))
    o_spec = pl.BlockSpec((tm, tn), lambda i, j, kk: (i, j))
    in_specs = [a_spec(p) for p in range(na)] + [b_spec] + ([o_spec] if add is not None else [])
    args = parts + (b,) + ((add,) if add is not None else ())
    return _blocked(
        body, name=name, grid=(m // tm, n // tn, nk),
        in_specs=in_specs, out_specs=o_spec,
        out_shape=jax.ShapeDtypeStruct((m, n), out_dtype),
        scratch_shapes=[pltpu.VMEM((tm, tn), F32)] if nk > 1 else [],
        compiler_params=_cparams(("parallel", "parallel", "arbitrary"), min(VMEM_BIG, need + VMEM_SLACK)),
    )(*args)


TOK = 256
ROWS = 512


def _rms(x, g):
    return x * lax.rsqrt(jnp.mean(x * x, axis=-1, keepdims=True) + NORM_EPS) * g


def _rms_fwd(x, g, name):
    t, d = x.shape

    def body(x_ref, g_ref, o_ref):
        o_ref[...] = _rms(x_ref[...], g_ref[...]).astype(BF16)

    row = pl.BlockSpec((ROWS, d), lambda i: (i, 0))
    return _blocked(body, name=name, grid=(t // ROWS,), in_specs=[row, _full((1, d))], out_specs=row,
                          out_shape=jax.ShapeDtypeStruct((t, d), BF16),
                          compiler_params=_cparams(("parallel",), VMEM_MID))(x, g)


def _rms_bwd(x, g, dxn, res, name):
    t, d = x.shape

    def body(x_ref, g_ref, d_ref, res_ref, dx_ref, dg_ref):
        _, vjp = jax.vjp(_rms, x_ref[...], g_ref[...])
        dx, dg = vjp(d_ref[...].astype(F32))
        dx_ref[...] = dx + res_ref[...]

        @pl.when(pl.program_id(0) == 0)
        def _():
            dg_ref[...] = jnp.zeros_like(dg_ref)

        dg_ref[...] += dg

    row = pl.BlockSpec((ROWS, d), lambda i: (i, 0))
    return _blocked(body, name=name, grid=(t // ROWS,), in_specs=[row, _full((1, d)), row, row],
                          out_specs=[row, _full((1, d))],
                          out_shape=[jax.ShapeDtypeStruct((t, d), F32), jax.ShapeDtypeStruct((1, d), F32)],
                          compiler_params=_cparams(("arbitrary",), VMEM_MID))(x, g, dxn, res)


def _loss_head(x, g, tgt):
    t, d = x.shape

    def body(x_ref, g_ref, t_ref, l_ref, dx_ref, dg_ref):
        tg = t_ref[...]

        def fn(xv, gv):
            err = _rms(xv, gv) - tg
            per_tok = jnp.mean(err * err, axis=-1, keepdims=True)
            return 0.5 * jnp.sum(per_tok, axis=0, keepdims=True)

        l, vjp = jax.vjp(fn, x_ref[...], g_ref[...])
        dx, dg = vjp(jnp.ones((1, 1), F32))
        dx_ref[...] = dx

        @pl.when(pl.program_id(0) == 0)
        def _():
            dg_ref[...] = jnp.zeros_like(dg_ref)
            l_ref[...] = jnp.zeros_like(l_ref)

        dg_ref[...] += dg
        l_ref[...] += jnp.broadcast_to(l, l_ref.shape)

    row = pl.BlockSpec((ROWS, d), lambda i: (i, 0))
    return _blocked(body, name="loss_head", grid=(t // ROWS,), in_specs=[row, _full((1, d)), row],
                          out_specs=[_full((1, LANES)), row, _full((1, d))],
                          out_shape=[jax.ShapeDtypeStruct((1, LANES), F32), jax.ShapeDtypeStruct((t, d), F32),
                                     jax.ShapeDtypeStruct((1, d), F32)],
                          compiler_params=_cparams(("arbitrary",), VMEM_MID))(x, g, tgt)


def _glu_fwd(x, z):
    t, d = x.shape

    def body(x_ref, v_ref, g_ref, o_ref):
        o_ref[...] = x_ref[...] + v_ref[...] * jax.nn.sigmoid(g_ref[...])

    row = pl.BlockSpec((ROWS, d), lambda i: (i, 0))
    gate = pl.BlockSpec((ROWS, d), lambda i: (i, 1))
    return _blocked(body, name="glu_fwd", grid=(t // ROWS,), in_specs=[row, row, gate], out_specs=row,
                          out_shape=jax.ShapeDtypeStruct((t, d), F32),
                          compiler_params=_cparams(("parallel",), VMEM_MID))(x, z, z)


def _glu_bwd(z, g):
    t, d = g.shape

    def body(v_ref, g_ref, d_ref, o_ref):
        s = jax.nn.sigmoid(g_ref[...])
        dy = d_ref[...]
        o_ref[:, :d] = (dy * s).astype(BF16)
        o_ref[:, d:] = (dy * v_ref[...] * s * (1.0 - s)).astype(BF16)

    row = pl.BlockSpec((ROWS, d), lambda i: (i, 0))
    gate = pl.BlockSpec((ROWS, d), lambda i: (i, 1))
    return _blocked(body, name="glu_bwd", grid=(t // ROWS,), in_specs=[row, gate, row],
                          out_specs=pl.BlockSpec((ROWS, 2 * d), lambda i: (i, 0)),
                          out_shape=jax.ShapeDtypeStruct((t, 2 * d), BF16),
                          compiler_params=_cparams(("parallel",), VMEM_MID))(z, z, g)


def _shift_down(x, d):
    row = lax.broadcasted_iota(jnp.int32, x.shape, 0)
    return jnp.where(row < d, 0.0, pltpu.roll(x, d, 0))


def _shift_up(x, d):
    n = x.shape[0]
    row = lax.broadcasted_iota(jnp.int32, x.shape, 0)
    return jnp.where(row >= n - d, 0.0, pltpu.roll(x, n - d, 0))


def _make_sd():
    @functools.partial(jax.custom_vjp, nondiff_argnums=(1,))
    def sd(x, d):
        return _shift_down(x, d)

    def fwd(x, d):
        return _shift_down(x, d), None

    def bwd(d, _, g):
        return (_shift_up(g, d),)

    sd.defvjp(fwd, bwd)
    return sd


def _lin_scan(a, u, reverse=False):
    n = a.shape[0]
    row = lax.broadcasted_iota(jnp.int32, a.shape, 0)
    d = 1
    while d < n:
        if reverse:
            keep = row < n - d
            a_s, u_s = pltpu.roll(a, n - d, 0), pltpu.roll(u, n - d, 0)
        else:
            keep = row >= d
            a_s, u_s = pltpu.roll(a, d, 0), pltpu.roll(u, d, 0)
        u = u + a * jnp.where(keep, u_s, 0.0)
        a = a * jnp.where(keep, a_s, 1.0)
        d *= 2
    return u


def _make_scan():
    @jax.custom_vjp
    def scan(a, u):
        return _lin_scan(a, u)

    def fwd(a, u):
        h = _lin_scan(a, u)
        return h, (a, h)

    def bwd(res, dh):
        a, h = res
        g = _lin_scan(_shift_up(a, 1), dh, reverse=True)
        return g * _shift_down(h, 1), g

    scan.defvjp(fwd, bwd)
    return scan


def _acc_out(ref, val):
    @pl.when(pl.program_id(0) == 0)
    def _():
        ref[...] = jnp.zeros_like(ref)

    ref[...] += val


FFN_CW = 128


def _ffn_fn(hg, hv, wg, wv, bg, bv, sd):
    cg = wg[0:1] * sd(hg, 2) + wg[1:2] * sd(hg, 1) + wg[2:3] * hg + bg
    cv = wv[0:1] * sd(hv, 2) + wv[1:2] * sd(hv, 1) + wv[2:3] * hv + bv
    return jax.nn.silu(cg) * cv


def _ffn_specs(t):
    nb = D_FF // FFN_CW
    col = lambda r, off: pl.BlockSpec((r, FFN_CW), lambda j: (0, j + off))
    return nb, [col(t, 0), col(t, nb), col(3, 0), col(3, nb), col(1, 0), col(1, nb)], col


def _ffn_mid_fwd(h, cw, cb, name):
    t = h.shape[0]
    nb, in_specs, col = _ffn_specs(t)

    def body(hg, hv, wg, wv, bg, bv, o_ref):
        o_ref[...] = _ffn_fn(hg[...], hv[...], wg[...], wv[...], bg[...], bv[...], _shift_down).astype(BF16)

    return _blocked(body, name=name, grid=(nb,), in_specs=in_specs, out_specs=col(t, 0),
                          out_shape=jax.ShapeDtypeStruct((t, D_FF), BF16),
                          compiler_params=_cparams(("parallel",), VMEM_MID))(h, h, cw, cw, cb, cb)


def _ffn_mid_bwd(h, cw, cb, dact, name):
    t = h.shape[0]
    nb, in_specs, col = _ffn_specs(t)

    def body(hg, hv, wg, wv, bg, bv, d_ref, dhg, dhv, dwg, dwv, dbg, dbv):
        fn = functools.partial(_ffn_fn, sd=_make_sd())
        _, vjp = jax.vjp(fn, hg[...], hv[...], wg[...], wv[...], bg[...], bv[...])
        g = vjp(d_ref[...])
        dhg[...] = g[0].astype(BF16)
        dhv[...] = g[1].astype(BF16)
        dwg[...], dwv[...], dbg[...], dbv[...] = g[2], g[3], g[4], g[5]

    big = jax.ShapeDtypeStruct((t, D_FF), BF16)
    w3 = jax.ShapeDtypeStruct((3, D_FF), F32)
    b1 = jax.ShapeDtypeStruct((1, D_FF), F32)
    return _blocked(body, name=name, grid=(nb,), in_specs=in_specs + [col(t, 0)],
                          out_specs=[col(t, 0), col(t, 0), col(3, 0), col(3, 0), col(1, 0), col(1, 0)],
                          out_shape=[big, big, w3, w3, b1, b1],
                          compiler_params=_cparams(("parallel",), VMEM_BIG))(h, h, cw, cw, cb, cb, dact)


TS_CW = 256


def _tshift_fn(p, mu, sd):
    return p + mu * (sd(p, 1) - p)


def _tshift_fwd(p, mu):
    t = p.shape[0]
    col = lambda r: pl.BlockSpec((r, TS_CW), lambda j: (0, j))

    def body(p_ref, mu_ref, o_ref):
        o_ref[...] = _tshift_fn(p_ref[...], mu_ref[...], _shift_down)

    return _blocked(body, name="tshift_fwd", grid=(SHIFT_COLS // TS_CW,), in_specs=[col(t), col(1)],
                          out_specs=col(t), out_shape=jax.ShapeDtypeStruct((t, SHIFT_COLS), F32),
                          compiler_params=_cparams(("parallel",), VMEM_MID))(p, mu)


def _tshift_bwd(p, mu, dpam):
    t = p.shape[0]
    col = lambda r: pl.BlockSpec((r, TS_CW), lambda j: (0, j))

    def body(p_ref, mu_ref, d_ref, dp_ref, dmu_ref):
        _, vjp = jax.vjp(functools.partial(_tshift_fn, sd=_make_sd()), p_ref[...], mu_ref[...])
        dp, dmu = vjp(d_ref[...])
        dp_ref[...] = dp.astype(BF16)
        dmu_ref[...] = dmu

    return _blocked(body, name="tshift_bwd", grid=(SHIFT_COLS // TS_CW,), in_specs=[col(t), col(1), col(t)],
                          out_specs=[col(t), col(1)],
                          out_shape=[jax.ShapeDtypeStruct((t, SHIFT_COLS), BF16),
                                     jax.ShapeDtypeStruct((1, SHIFT_COLS), F32)],
                          compiler_params=_cparams(("parallel",), VMEM_MID))(p, mu, dpam)


_HI = lax.Precision.HIGHEST
_O = (0, RW, 2 * RW, 3 * RW, 3 * RW + W_LORA, 3 * RW + W_LORA + A_LORA, SHIFT_COLS)


def _dot16(a, b, dims=(((1,), (0,)), ((), ()))):
    return lax.dot_general(a.astype(BF16), b.astype(BF16), dims, preferred_element_type=F32)


def _make_dot16():
    @jax.custom_vjp
    def dot(a, b):
        return _dot16(a, b)

    def fwd(a, b):
        return _dot16(a, b), (a, b)

    def bwd(res, g):
        a, b = res
        return _dot16(g, b, (((1,), (1,)), ((), ()))), _dot16(a, g, (((0,), (0,)), ((), ())))

    dot.defvjp(fwd, bwd)
    return dot


def _seg(x):
    first = lax.broadcasted_iota(jnp.int32, (x.shape[0], LANES), 1) < HEAD
    parts = []
    for p in range(x.shape[1] // LANES):
        xp = x[:, p * LANES:(p + 1) * LANES]
        s0 = jnp.sum(jnp.where(first, xp, 0.0), axis=-1, keepdims=True)
        s1 = jnp.sum(jnp.where(first, 0.0, xp), axis=-1, keepdims=True)
        parts.append(jnp.where(first, s0, s1))
    return jnp.concatenate(parts, axis=1)


def _prep_fn(r, k, v, wd, ad, gd, w0, w2, a0, a2, g2, k_k, k_a, dot):
    w_log = -jax.nn.softplus(-(w0 + dot(jnp.tanh(wd), w2))) - 0.5
    decay = jnp.exp(-jnp.exp(w_log))
    a = jax.nn.sigmoid(a0 + dot(ad, a2))
    g = dot(jax.nn.sigmoid(gd), g2)
    kk = k * k_k
    kk = kk / jnp.maximum(jnp.sqrt(_seg(kk * kk)), 1e-12)
    k2 = k * (1.0 + (a - 1.0) * k_a)
    return r, decay, k2, v, -kk, kk * a, g


_PREP_W = ("w0", "w2", "a0", "a2", "g2", "k_k", "k_a")


def _prep_wspecs(w):
    return [_full(w[n].shape) for n in _PREP_W]


def _rwkv_prep_fwd(pam, w):
    t = pam.shape[0]

    def body(p_ref, *refs):
        wr, outs = refs[:7], refs[7:]
        pieces = [p_ref[:, _O[i]:_O[i + 1]] for i in range(6)]
        res = _prep_fn(*pieces, *[x[...] for x in wr], _dot16)
        for o, val in zip(outs, res):
            o[...] = val

    row = lambda c: pl.BlockSpec((TOK, c), lambda i: (i, 0))
    return _blocked(body, name="rwkv_prep_fwd", grid=(t // TOK,),
                          in_specs=[row(SHIFT_COLS)] + _prep_wspecs(w), out_specs=[row(RW)] * 7,
                          out_shape=[jax.ShapeDtypeStruct((t, RW), F32)] * 7,
                          compiler_params=_cparams(("parallel",), VMEM_MID))(pam, *[w[n] for n in _PREP_W])


def _rwkv_prep_bwd(pam, w, cts, more):
    t = pam.shape[0]

    def body(p_ref, *refs):
        wr, ct, ex, dp_ref, dws = refs[:7], refs[7:14], refs[14:17], refs[17], refs[18:]
        pieces = [p_ref[:, _O[i]:_O[i + 1]] for i in range(6)]
        fn = lambda *a: _prep_fn(*a, _make_dot16())
        _, vjp = jax.vjp(fn, *pieces, *[x[...] for x in wr])
        c = [x[...] for x in ct]
        c[0] = c[0] + ex[0][...]
        c[2] = c[2] + ex[1][...]
        c[3] = c[3] + ex[2][...]
        g = vjp(tuple(c))
        for i in range(6):
            dp_ref[:, _O[i]:_O[i + 1]] = g[i]
        for o, val in zip(dws, g[6:]):
            _acc_out(o, val)

    row = lambda c: pl.BlockSpec((TOK, c), lambda i: (i, 0))
    return _blocked(body, name="rwkv_prep_bwd", grid=(t // TOK,),
                          in_specs=[row(SHIFT_COLS)] + _prep_wspecs(w) + [row(RW)] * 10,
                          out_specs=[row(SHIFT_COLS)] + [_full(w[n].shape) for n in _PREP_W],
                          out_shape=[jax.ShapeDtypeStruct((t, SHIFT_COLS), F32)]
                          + [jax.ShapeDtypeStruct(w[n].shape, F32) for n in _PREP_W],
                          compiler_params=_cparams(("arbitrary",), VMEM_MID))(
                              pam, *[w[n] for n in _PREP_W], *cts, *more)


def _post_fn(y, r, k2, v, g, ln_w, ln_b, r_k):
    inv = 1.0 / HEAD
    d = y - _seg(y) * inv
    yn = d * lax.rsqrt(_seg(d * d) * inv + GN_EPS) * ln_w + ln_b
    bonus = _seg(r * k2 * r_k) * v
    return (yn + bonus) * g


def _rwkv_post_fwd(y, r, k2, v, g, ln_w, ln_b, r_k):
    t = y.shape[0]

    def body(*refs):
        o_ref = refs[-1]
        o_ref[...] = _post_fn(*[x[...] for x in refs[:-1]]).astype(BF16)

    row = pl.BlockSpec((TOK, RW), lambda i: (i, 0))
    return _blocked(body, name="rwkv_post_fwd", grid=(t // TOK,),
                          in_specs=[row] * 5 + [_full((1, RW))] * 3, out_specs=row,
                          out_shape=jax.ShapeDtypeStruct((t, RW), BF16),
                          compiler_params=_cparams(("parallel",), VMEM_MID))(y, r, k2, v, g, ln_w, ln_b, r_k)


def _rwkv_post_bwd(y, r, k2, v, g, ln_w, ln_b, r_k, dya):
    t = y.shape[0]

    def body(*refs):
        ins, d_ref, outs = refs[:8], refs[8], refs[9:]
        _, vjp = jax.vjp(_post_fn, *[x[...] for x in ins])
        gr = vjp(d_ref[...])
        for o, val in zip(outs[:5], gr[:5]):
            o[...] = val
        for o, val in zip(outs[5:], gr[5:]):
            _acc_out(o, val)

    row = pl.BlockSpec((TOK, RW), lambda i: (i, 0))
    vec = _full((1, RW))
    return _blocked(body, name="rwkv_post_bwd", grid=(t // TOK,),
                          in_specs=[row] * 5 + [vec] * 3 + [row],
                          out_specs=[row] * 5 + [vec] * 3,
                          out_shape=[jax.ShapeDtypeStruct((t, RW), F32)] * 5 + [jax.ShapeDtypeStruct((1, RW), F32)] * 3,
                          compiler_params=_cparams(("arbitrary",), VMEM_MID))(y, r, k2, v, g, ln_w, ln_b, r_k, dya)


def _from_pt(x):
    n = x.shape[0]
    return x.reshape(n, HEAD, N_HEADS, PT).transpose(0, 3, 2, 1).reshape(n * PT, N_HEADS * HEAD)


def _lane_sum(x):
    return jnp.sum(x, axis=-1, keepdims=True)


def _pair_consts():
    lane = lax.broadcasted_iota(jnp.int32, (HEAD, LANES), 1)
    return lane, lane < HEAD


def _seg_sum_pair(x, first):
    return jnp.where(first, _lane_sum(jnp.where(first, x, 0.0)), _lane_sum(jnp.where(first, 0.0, x)))


def _to_pt(x):
    t = x.shape[0]
    return x.reshape(t // PT, PT, N_HEADS, HEAD).transpose(0, 3, 2, 1).reshape(t // PT, HEAD, N_HEADS * PT)


def _expand_cols(x, name):
    t = x.shape[0]
    tiles = WKV_CHUNK // PT

    def body(x_ref, o_ref):
        _, first = _pair_consts()
        for tl in range(tiles):
            tile = x_ref[tl]
            for j in range(PT):
                for p in range(N_HEADS // 2):
                    src = jnp.where(first, (2 * p) * PT + j, (2 * p + 1) * PT + j)
                    o_ref[tl * PT + j, :, p * LANES:(p + 1) * LANES] = jnp.take_along_axis(tile, src, axis=1)

    return _blocked(
        body, name=name, grid=(t // WKV_CHUNK,),
        in_specs=[pl.BlockSpec((tiles, HEAD, LANES), lambda i: (i, 0, 0))],
        out_specs=pl.BlockSpec((WKV_CHUNK, HEAD, RW), lambda i: (i, 0, 0)),
        out_shape=jax.ShapeDtypeStruct((t, HEAD, RW), F32),
        compiler_params=_cparams(("parallel",), VMEM_MID))(_to_pt(x))


def _wkv_fwd(w, k, z, b, v_exp):
    t = w.shape[0]
    chunk = 2 * WKV_CHUNK
    nc = t // chunk
    pairs = N_HEADS // 2

    def body(w_ref, k_ref, z_ref, b_ref, v_ref, s_all, s_ref):
        @pl.when(pl.program_id(0) == 0)
        def _():
            s_ref[...] = jnp.zeros_like(s_ref)

        _, first = _pair_consts()

        def group(gi, carry):
            base = pl.multiple_of(gi * 8, 8)
            rows = [ref[pl.ds(base, 8), :] for ref in (w_ref, k_ref, z_ref, b_ref)]
            s = [s_ref[:, p * LANES:(p + 1) * LANES] for p in range(pairs)]
            for jj in range(8):
                for p in range(pairs):
                    cs = slice(p * LANES, (p + 1) * LANES)
                    wr, kr, zr, br = [x[jj:jj + 1, cs] for x in rows]
                    s_all[base + jj, :, cs] = s[p]
                    sa = _seg_sum_pair(s[p] * zr, first)
                    s[p] = s[p] * wr + sa * br + v_ref[base + jj, :, cs] * kr
            for p in range(pairs):
                s_ref[:, p * LANES:(p + 1) * LANES] = s[p]
            return carry

        lax.fori_loop(0, chunk // 8, group, 0)

    row = pl.BlockSpec((chunk, RW), lambda i: (i, 0))
    big = pl.BlockSpec((chunk, HEAD, RW), lambda i: (i, 0, 0))
    return _blocked(
        body, name="wkv_fwd", grid=(nc,), in_specs=[row] * 4 + [big], out_specs=[big, _full((HEAD, RW))],
        out_shape=[jax.ShapeDtypeStruct((t, HEAD, RW), F32), jax.ShapeDtypeStruct((HEAD, RW), F32)],
        compiler_params=_cparams(("arbitrary",), VMEM_BIG))(w, k, z, b, v_exp)


def _wkv_out(r, s_all, s_last):
    t = r.shape[0]
    nc = t // WKV_CHUNK
    tiles = WKV_CHUNK // PT
    pairs = N_HEADS // 2

    def body(r_ref, s_ref, nxt_ref, last_ref, y_ref):
        lane, first = _pair_consts()
        after = jnp.where(pl.program_id(0) == nc - 1, last_ref[...], nxt_ref[0])
        for tl in range(tiles):
            ytile = jnp.zeros((HEAD, LANES), F32)
            for g in range(PT // 8):
                rows = r_ref[tl * PT + g * 8:tl * PT + g * 8 + 8, :]
                for jj in range(8):
                    tt = tl * PT + g * 8 + jj
                    j = g * 8 + jj
                    for p in range(pairs):
                        cs = slice(p * LANES, (p + 1) * LANES)
                        s = s_ref[tt + 1, :, cs] if tt + 1 < WKV_CHUNK else after[:, cs]
                        pr = s * rows[jj:jj + 1, cs]
                        y0 = _lane_sum(jnp.where(first, pr, 0.0))
                        y1 = _lane_sum(jnp.where(first, 0.0, pr))
                        ytile = jnp.where(lane == (2 * p) * PT + j, y0, ytile)
                        ytile = jnp.where(lane == (2 * p + 1) * PT + j, y1, ytile)
            y_ref[tl] = ytile

    row = pl.BlockSpec((WKV_CHUNK, RW), lambda i: (i, 0))
    pt = pl.BlockSpec((tiles, HEAD, LANES), lambda i: (i, 0, 0))
    big = pl.BlockSpec((WKV_CHUNK, HEAD, RW), lambda i: (i, 0, 0))
    nxt = pl.BlockSpec((1, HEAD, RW), lambda i: (jnp.minimum((i + 1) * WKV_CHUNK, t - 1), 0, 0))
    return _blocked(
        body, name="wkv_out", grid=(nc,), in_specs=[row, big, nxt, _full((HEAD, RW))], out_specs=pt,
        out_shape=jax.ShapeDtypeStruct((t // PT, HEAD, LANES), F32),
        compiler_params=_cparams(("parallel",), VMEM_MID))(r, s_all, s_all, s_last)


def _wkv_bwd(r, w, k, z, b, v_exp, s_all, dy_exp):
    t = r.shape[0]
    nc = t // WKV_CHUNK
    tiles = WKV_CHUNK // PT
    pairs = N_HEADS // 2

    def body(r_ref, w_ref, k_ref, z_ref, b_ref, v_ref, s_all_ref, dy_ref,
             dr_ref, dw_ref, dk_ref, dz_ref, db_ref, dv_ref, ds_ref):
        @pl.when(pl.program_id(0) == 0)
        def _():
            ds_ref[...] = jnp.zeros_like(ds_ref)

        lane, first = _pair_consts()
        col_sum = lambda x: jnp.sum(x, axis=0, keepdims=True)
        row8 = lax.broadcasted_iota(jnp.int32, (8, LANES), 0)
        for tl in reversed(range(tiles)):
            def group(gg, dvtile):
                gi = PT // 8 - 1 - gg
                base = pl.multiple_of(tl * PT + gi * 8, 8)
                rows = [ref[pl.ds(base, 8), :] for ref in (r_ref, w_ref, k_ref, z_ref, b_ref)]
                outs = (dr_ref, dw_ref, dk_ref, dz_ref, db_ref)
                tiles8 = {(id(o), p): jnp.zeros((8, LANES), F32) for o in outs for p in range(pairs)}
                ds = [ds_ref[:, p * LANES:(p + 1) * LANES] for p in range(pairs)]
                for jj in reversed(range(8)):
                    j = gi * 8 + jj
                    for p in range(pairs):
                        cs = slice(p * LANES, (p + 1) * LANES)

                        def put(ref, val, p=p, jj=jj):
                            tiles8[(id(ref), p)] = jnp.where(row8 == jj, val, tiles8[(id(ref), p)])

                        rr, wr, kr, zr, br = [x[jj:jj + 1, cs] for x in rows]
                        sp = s_all_ref[base + jj, :, cs]
                        vc = v_ref[base + jj, :, cs]
                        dyc = dy_ref[base + jj, :, cs]
                        sa = _seg_sum_pair(sp * zr, first)
                        st = sp * wr + sa * br + vc * kr
                        d = ds[p] + dyc * rr
                        put(dr_ref, col_sum(st * dyc))
                        dvk = d * kr
                        dv0 = _lane_sum(jnp.where(first, dvk, 0.0))
                        dv1 = _lane_sum(jnp.where(first, 0.0, dvk))
                        dvtile = jnp.where(lane == (2 * p) * PT + j, dv0, dvtile)
                        dvtile = jnp.where(lane == (2 * p + 1) * PT + j, dv1, dvtile)
                        put(dk_ref, col_sum(d * vc))
                        put(dw_ref, col_sum(sp * d))
                        u = _seg_sum_pair(d * br, first)
                        put(dz_ref, col_sum(sp * u))
                        put(db_ref, col_sum(d * sa))
                        ds[p] = d * wr + u * zr
                for p in range(pairs):
                    ds_ref[:, p * LANES:(p + 1) * LANES] = ds[p]
                for o in outs:
                    for p in range(pairs):
                        o[pl.ds(base, 8), p * LANES:(p + 1) * LANES] = tiles8[(id(o), p)]
                return dvtile

            dv_ref[tl] = lax.fori_loop(0, PT // 8, group, jnp.zeros((HEAD, LANES), F32))

    rev = lambda i: nc - 1 - i
    row = pl.BlockSpec((WKV_CHUNK, RW), lambda i: (rev(i), 0))
    pt = pl.BlockSpec((tiles, HEAD, LANES), lambda i: (rev(i), 0, 0))
    big = pl.BlockSpec((WKV_CHUNK, HEAD, RW), lambda i: (rev(i), 0, 0))
    return _blocked(
        body, name="wkv_bwd", grid=(nc,), in_specs=[row] * 5 + [big, big, big], out_specs=[row] * 5 + [pt],
        out_shape=[jax.ShapeDtypeStruct((t, RW), F32)] * 5 + [jax.ShapeDtypeStruct((t // PT, HEAD, LANES), F32)],
        scratch_shapes=[pltpu.VMEM((HEAD, RW), F32)],
        compiler_params=_cparams(("arbitrary",), VMEM_BIG))(r, w, k, z, b, v_exp, s_all, dy_exp)


LRU_CW = 128
_BX0 = SHIFT_COLS // LRU_CW
_BG0 = (SHIFT_COLS + LRU_W) // LRU_CW


def _lru_fn(bx, bg, cw, cb, ga, ba, gx, bxb, lam, sd, scan, dot):
    xc = cw[0:1] * sd(bx, 3) + cw[1:2] * sd(bx, 2) + cw[2:3] * sd(bx, 1) + cw[3:4] * bx + cb
    gr = jax.nn.sigmoid(dot(xc, ga) + ba)
    gi = jax.nn.sigmoid(dot(xc, gx) + bxb)
    log_a = -LRU_C * gr * jax.nn.softplus(-lam)
    a = jnp.exp(log_a)
    mult = jnp.sqrt(-jnp.tanh(log_a) * (jnp.exp(2.0 * log_a) + 1.0))
    return scan(a, xc * gi * mult) * jax.nn.gelu(bg)


def _lru_specs(t):
    col = lambda r, off=0: pl.BlockSpec((r, LRU_CW), lambda j: (0, j + off))
    diag = pl.BlockSpec((LRU_CW, LRU_CW), lambda j: (j, j))
    return col, [col(t, _BX0), col(t, _BG0), col(4), col(1), diag, col(1), diag, col(1), col(1)]


def _lru_fwd(p, cw, cb, ga, ba, gx, bxb, lam):
    t = p.shape[0]
    col, in_specs = _lru_specs(t)

    def body(*refs):
        o_ref = refs[-1]
        o_ref[...] = _lru_fn(*[x[...] for x in refs[:-1]], _shift_down, _lin_scan, _dot16).astype(BF16)

    return _blocked(body, name="lru_fwd", grid=(LRU_W // LRU_CW,), in_specs=in_specs, out_specs=col(t),
                          out_shape=jax.ShapeDtypeStruct((t, LRU_W), BF16),
                          compiler_params=_cparams(("parallel",), VMEM_MID))(p, p, cw, cb, ga, ba, gx, bxb, lam)


def _lru_bwd(p, cw, cb, ga, ba, gx, bxb, lam, dyb):
    t = p.shape[0]
    col, in_specs = _lru_specs(t)

    def body(*refs):
        ins, d_ref, outs = refs[:9], refs[9], refs[10:]
        fn = functools.partial(_lru_fn, sd=_make_sd(), scan=_make_scan(), dot=_make_dot16())
        _, vjp = jax.vjp(fn, *[x[...] for x in ins])
        g = vjp(d_ref[...])
        outs[0][...] = g[0].astype(BF16)
        outs[1][...] = g[1].astype(BF16)
        for o, val in zip(outs[2:], g[2:]):
            o[...] = val

    sq = pl.BlockSpec((LRU_CW, LRU_CW), lambda j: (j, 0))
    act = jax.ShapeDtypeStruct((t, LRU_W), BF16)
    vec = jax.ShapeDtypeStruct((1, LRU_W), F32)
    sqs = jax.ShapeDtypeStruct((LRU_W, LRU_CW), F32)
    return _blocked(body, name="lru_bwd", grid=(LRU_W // LRU_CW,), in_specs=in_specs + [col(t, RW // LRU_CW)],
                          out_specs=[col(t), col(t), col(4), col(1), sq, col(1), sq, col(1), col(1)],
                          out_shape=[act, act, jax.ShapeDtypeStruct((4, LRU_W), F32), vec, sqs, vec, sqs, vec, vec],
                          compiler_params=_cparams(("parallel",), VMEM_BIG))(p, p, cw, cb, ga, ba, gx, bxb, lam, dyb)


def _s5_disc_fn(a_re, a_im, log_dt, b_re, b_im, e):
    lam_re = jnp.minimum(a_re, -1e-4)
    lam_im = a_im
    dt = jnp.exp(log_dt)
    mag = jnp.exp(lam_re * dt)
    ab_re = mag * jnp.cos(lam_im * dt)
    ab_im = mag * jnp.sin(lam_im * dt)
    den = lam_re * lam_re + lam_im * lam_im
    zr = ab_re - 1.0
    q_re = jnp.dot((zr * lam_re + ab_im * lam_im) / den, e, precision=_HI)
    q_im = jnp.dot((ab_im * lam_re - zr * lam_im) / den, e, precision=_HI)
    return ab_re, ab_im, q_re * b_re - q_im * b_im, q_re * b_im + q_im * b_re


def _s5_disc_fwd(a_re, a_im, log_dt, b_re, b_im, e):
    def body(*refs):
        res = _s5_disc_fn(*[x[...] for x in refs[:6]])
        for o, val in zip(refs[6:], res):
            o[...] = val

    small = jax.ShapeDtypeStruct(a_re.shape, F32)
    wide = jax.ShapeDtypeStruct(b_re.shape, F32)
    return pl.pallas_call(body, name="s5_disc_fwd", out_shape=[small, small, wide, wide])(
        a_re, a_im, log_dt, b_re, b_im, e)


def _s5_disc_bwd(a_re, a_im, log_dt, b_re, b_im, e, cts):
    def body(*refs):
        ins, e_ref, ct, outs = refs[:5], refs[5], refs[6:10], refs[10:]
        _, vjp = jax.vjp(lambda *a: _s5_disc_fn(*a, e_ref[...]), *[x[...] for x in ins])
        for o, val in zip(outs, vjp(tuple(c[...] for c in ct))):
            o[...] = val

    shapes = [jax.ShapeDtypeStruct(x.shape, F32) for x in (a_re, a_im, log_dt, b_re, b_im)]
    return pl.pallas_call(body, name="s5_disc_bwd", out_shape=shapes)(a_re, a_im, log_dt, b_re, b_im, e, *cts)


def _cmul(a, b):
    return a[0] * b[0] - a[1] * b[1], a[0] * b[1] + a[1] * b[0]


def _s5_scan(sr, si, ab, reverse):
    n_tiles = sr.shape[0] // 8
    width = sr.shape[1]
    row8 = lax.broadcasted_iota(jnp.int32, (8, width), 0)
    p1 = ab
    p2 = _cmul(p1, p1)
    p4 = _cmul(p2, p2)
    pw = [p1]
    for _ in range(7):
        pw.append(_cmul(pw[-1], p1))
    cr = jnp.zeros((8, width), F32)
    ci = jnp.zeros((8, width), F32)
    for j in range(8):
        e = pw[7 - j] if reverse else pw[j]
        cr = jnp.where(row8 == j, e[0], cr)
        ci = jnp.where(row8 == j, e[1], ci)

    levels = []
    for d, q in ((1, p1), (2, p2), (4, p4)):
        keep = row8 < 8 - d if reverse else row8 >= d
        levels.append((d, (jnp.where(keep, q[0], 0.0), jnp.where(keep, q[1], 0.0))))

    def tile(i, carry):
        idx = n_tiles - 1 - i if reverse else i
        base = pl.multiple_of(idx * 8, 8)
        x = (sr[pl.ds(base, 8), :], si[pl.ds(base, 8), :])
        for d, q in levels:
            amt = 8 - d if reverse else d
            m = _cmul(q, (pltpu.roll(x[0], amt, 0), pltpu.roll(x[1], amt, 0)))
            x = (x[0] + m[0], x[1] + m[1])
        m = _cmul((cr, ci), carry)
        x = (x[0] + m[0], x[1] + m[1])
        sr[pl.ds(base, 8), :] = x[0]
        si[pl.ds(base, 8), :] = x[1]
        edge = slice(0, 1) if reverse else slice(7, 8)
        return x[0][edge], x[1][edge]

    zero = jnp.zeros((1, width), F32)
    lax.fori_loop(0, n_tiles, tile, (zero, zero))


_S5_W = S5_SLAB // S5_GROUP * S5_STATE


def _s5_specs(t):
    col = lambda r: pl.BlockSpec((r, S5_SLAB), lambda j: (0, j))
    bb = pl.BlockSpec((None, S5_SLAB, _S5_W), lambda j: (j, 0, 0))
    cd = pl.BlockSpec((None, _S5_W, S5_SLAB), lambda j: (j, 0, 0))
    ab = pl.BlockSpec((None, 1, _S5_W), lambda j: (j, 0, 0))
    return col, bb, cd, ab


def _s5_fwd(u, dvec, bbr, bbi, cdr, cdi, abr, abi):
    t, width = u.shape
    col, bb, cd, ab = _s5_specs(t)

    def body(u_ref, d_ref, bbr_ref, bbi_ref, cdr_ref, cdi_ref, abr_ref, abi_ref, o_ref, sr, si):
        uv = u_ref[...]
        sr[...] = _dot16(uv, bbr_ref[...])
        si[...] = _dot16(uv, bbi_ref[...])
        _s5_scan(sr, si, (abr_ref[...], abi_ref[...]), False)
        y = _dot16(sr[...], cdr_ref[...]) - _dot16(si[...], cdi_ref[...])
        o_ref[...] = jax.nn.gelu(y + d_ref[...] * uv).astype(BF16)

    return _blocked(body, name="s5_fwd", grid=(width // S5_SLAB,),
                          in_specs=[col(t), col(1), bb, bb, cd, cd, ab, ab], out_specs=col(t),
                          out_shape=jax.ShapeDtypeStruct((t, width), BF16),
                          scratch_shapes=[pltpu.VMEM((t, _S5_W), F32)] * 2,
                          compiler_params=_cparams(("parallel",), VMEM_BIG))(u, dvec, bbr, bbi, cdr, cdi, abr, abi)


def _s5_bwd(u, dvec, bbr, bbi, cdr, cdi, abr, abi, dyact):
    t, width = u.shape
    col, bb, cd, ab = _s5_specs(t)
    ns = width // S5_SLAB
    tn = (((0,), (0,)), ((), ()))
    nt = (((1,), (1,)), ((), ()))

    def body(u_ref, d_ref, bbr_ref, bbi_ref, cdr_ref, cdi_ref, abr_ref, abi_ref, dy_ref,
             du_ref, dd_ref, dbbr_ref, dbbi_ref, dcdr_ref, dcdi_ref, dabr_ref, dabi_ref, sr, si, gr, gi):
        uv = u_ref[...]
        dv = d_ref[...]
        abv = (abr_ref[...], abi_ref[...])
        sr[...] = _dot16(uv, bbr_ref[...])
        si[...] = _dot16(uv, bbi_ref[...])
        _s5_scan(sr, si, abv, False)
        y = _dot16(sr[...], cdr_ref[...]) - _dot16(si[...], cdi_ref[...])
        _, vjp = jax.vjp(jax.nn.gelu, y + dv * uv)
        (dpre,) = vjp(dy_ref[...].astype(F32))
        dd_ref[...] = jnp.sum(dpre * uv, axis=0, keepdims=True)
        dcdr_ref[...] = _dot16(sr[...], dpre, tn)
        dcdi_ref[...] = -_dot16(si[...], dpre, tn)
        gr[...] = _dot16(dpre, cdr_ref[...], nt)
        gi[...] = -_dot16(dpre, cdi_ref[...], nt)
        _s5_scan(gr, gi, (abv[0], -abv[1]), True)

        row8 = lax.broadcasted_iota(jnp.int32, (8, _S5_W), 0)

        def tile(i, carry):
            acc_r, acc_i, last_r, last_i = carry
            base = pl.multiple_of(i * 8, 8)
            s_r, s_i = sr[pl.ds(base, 8), :], si[pl.ds(base, 8), :]
            g_r, g_i = gr[pl.ds(base, 8), :], gi[pl.ds(base, 8), :]
            p_r = jnp.where(row8 == 0, last_r, pltpu.roll(s_r, 1, 0))
            p_i = jnp.where(row8 == 0, last_i, pltpu.roll(s_i, 1, 0))
            acc_r = acc_r + jnp.sum(g_r * p_r + g_i * p_i, axis=0, keepdims=True)
            acc_i = acc_i + jnp.sum(g_i * p_r - g_r * p_i, axis=0, keepdims=True)
            return acc_r, acc_i, s_r[7:8], s_i[7:8]

        zero = jnp.zeros((1, _S5_W), F32)
        acc_r, acc_i, _, _ = lax.fori_loop(0, t // 8, tile, (zero, zero, zero, zero))
        dabr_ref[...] = acc_r
        dabi_ref[...] = acc_i
        du_ref[...] = dpre * dv + _dot16(gr[...], bbr_ref[...], nt) + _dot16(gi[...], bbi_ref[...], nt)
        dbbr_ref[...] = _dot16(uv, gr[...], tn)
        dbbi_ref[...] = _dot16(uv, gi[...], tn)

    sds = jax.ShapeDtypeStruct
    return _blocked(
        body, name="s5_bwd", grid=(ns,), in_specs=[col(t), col(1), bb, bb, cd, cd, ab, ab, col(t)],
        out_specs=[col(t), col(1), bb, bb, cd, cd, ab, ab],
        out_shape=[sds((t, width), F32), sds((1, width), F32), sds((ns, S5_SLAB, _S5_W), F32),
                   sds((ns, S5_SLAB, _S5_W), F32), sds((ns, _S5_W, S5_SLAB), F32), sds((ns, _S5_W, S5_SLAB), F32),
                   sds((ns, 1, _S5_W), F32), sds((ns, 1, _S5_W), F32)],
        scratch_shapes=[pltpu.VMEM((t, _S5_W), F32)] * 4,
        compiler_params=_cparams(("parallel",), VMEM_BIG))(u, dvec, bbr, bbi, cdr, cdi, abr, abi, dyact)


def _gate_dense(w):
    h = w.shape[0]
    return jnp.einsum("hij,hg->higj", w, jnp.eye(h, dtype=F32)).reshape(h * HEAD, h * HEAD)


def _gate_blocks(d):
    x = d.reshape(LRU_W // LRU_CW, 2, HEAD, 2, HEAD)
    return jnp.einsum("tgihj,gh->tgij", x, jnp.eye(2, dtype=F32)).reshape(LRU_W // HEAD, HEAD, HEAD)


_GPS = S5_SLAB // S5_GROUP
_NS = S5_GROUPS // _GPS


def _s5_in_dense(bb):
    x = bb.reshape(_NS, _GPS, S5_STATE, S5_GROUP)
    return jnp.einsum("sgnc,gh->sgchn", x, jnp.eye(_GPS, dtype=F32)).reshape(_NS, S5_SLAB, _S5_W)


def _s5_in_blocks(d):
    x = d.reshape(_NS, _GPS, S5_GROUP, _GPS, S5_STATE)
    return jnp.einsum("sgchn,gh->sgnc", x, jnp.eye(_GPS, dtype=F32)).reshape(S5_GROUPS, S5_STATE * S5_GROUP)


def _s5_out_dense(c):
    x = c.reshape(_NS, _GPS, S5_GROUP, S5_STATE)
    return jnp.einsum("sgcn,gh->shngc", x, jnp.eye(_GPS, dtype=F32)).reshape(_NS, _S5_W, S5_SLAB)


def _s5_out_blocks(d):
    x = d.reshape(_NS, _GPS, S5_STATE, _GPS, S5_GROUP)
    return jnp.einsum("shngc,gh->sgcn", x, jnp.eye(_GPS, dtype=F32)).reshape(S5_GROUPS, S5_GROUP, S5_STATE)


def _local_step(x, tgt, w, late_weights, send_grads):
    d_model = x.shape[1]
    gs = {}
    n_layers = w["f_norm_g"].shape[0]

    def ffn_fwd(xin, l):
        xn = _rms_fwd(xin, w["f_norm_g"][l:l + 1], f"rms_f{l}")
        h = _matmul(xn, w["f_w_up_t"][l], "nt", f"mm_f{l}_up")
        act = _ffn_mid_fwd(h, w["f_conv_w"][l], w["f_conv_b"][l:l + 1], f"ffn_mid_fwd{l}")
        return _matmul(act, w["f_w_down"][l], "nn", f"mm_f{l}_down", add=xin), (xin, xn, h, act)

    def ffn_bwd(g, saved, l):
        xin, xn, h, act = saved
        dact = _matmul(g, w["f_w_down"][l], "nt", f"mm_f{l}_dact")
        d_down = _matmul(act, g, "tn", f"mm_f{l}_ddown", out_dtype=BF16)
        dhg, dhv, dwg, dwv, dbg, dbv = _ffn_mid_bwd(h, w["f_conv_w"][l], w["f_conv_b"][l:l + 1], dact,
                                                    f"ffn_mid_bwd{l}")
        dxn = _matmul((dhg, dhv), w["f_w_up_t"][l], "nn", f"mm_f{l}_dxn")
        d_up = _matmul((dhg, dhv), xn, "tn", f"mm_f{l}_dup", out_dtype=BF16)
        dx, dgn = _rms_bwd(xin, w["f_norm_g"][l:l + 1], dxn, g, f"rms_f{l}_bwd")
        return dx, d_up, d_down, jnp.concatenate([dwg, dwv], axis=1), jnp.concatenate([dbg, dbv], axis=1), dgn

    xn0 = _rms_fwd(x, w["e_norm_g"], "rms_e")
    p = _matmul(xn0, w["e_w_in_t"], "nt", "mm_e_in")
    pam = _tshift_fwd(p, w["e_mu"])
    pw = dict(w0=w["e_w0"], w2=w["e_w2"][0], a0=w["e_a0"], a2=w["e_a2"][0], g2=w["e_g2"][0],
              k_k=w["e_k_k"], k_a=w["e_k_a"])
    r, dec, k2, v, z, b, gate = _rwkv_prep_fwd(pam, pw)
    v_exp = _expand_cols(v, "wkv_expand_v")
    s_all, s_last = _wkv_fwd(dec, k2, z, b, v_exp)
    y_pt = _wkv_out(r, s_all, s_last)
    y = _from_pt(y_pt)
    rk = w["e_r_k"].reshape(1, RW)
    ya = _rwkv_post_fwd(y, r, k2, v, gate, w["e_ln_w"], w["e_ln_b"], rk)
    ga, gx = _gate_dense(w["e_gate_a_w"][0]), _gate_dense(w["e_gate_x_w"][0])
    lru_w = (w["e_conv_w"][0], w["e_conv_b"], ga, w["e_gate_a_b"], gx, w["e_gate_x_b"], w["e_lru_lambda"])
    yb = _lru_fwd(p, *lru_w)
    ycat = jnp.concatenate([ya, yb], axis=1)
    w = {**w, **late_weights(ycat)}
    x1 = _matmul(ycat, w["e_w_out"], "nn", "mm_e_out", add=x)
    x2, ffn0 = ffn_fwd(x1, 0)

    xn1 = _rms_fwd(x2, w["o_norm_g"], "rms_o")
    u = _matmul(xn1, w["o_w_in"], "nn", "mm_o_in")
    expand = jnp.kron(jnp.eye(S5_STATE, dtype=F32), jnp.ones((1, S5_GROUP), F32))
    disc_in = (w["o_A_re"][0], w["o_A_im"][0], w["o_log_dt"].reshape(S5_GROUPS, 1),
               w["o_B_re"][0].reshape(S5_GROUPS, -1), w["o_B_im"][0].reshape(S5_GROUPS, -1), expand)
    ab_re, ab_im, bb_re, bb_im = _s5_disc_fwd(*disc_in)
    s5_w = (w["o_D"], _s5_in_dense(bb_re), _s5_in_dense(bb_im), _s5_out_dense(w["o_C_re"][0]),
            _s5_out_dense(w["o_C_im"][0]), ab_re.reshape(_NS, 1, _S5_W), ab_im.reshape(_NS, 1, _S5_W))
    yact = _s5_fwd(u, *s5_w)
    zz = _matmul(yact, w["o_w_glu_t"], "nt", "mm_o_glu")
    x3 = _glu_fwd(x2, zz)
    x4, ffn1 = ffn_fwd(x3, 1)

    loss, g, gs["final_norm_g", 0] = _loss_head(x4, w["final_norm_g"].reshape(1, d_model), tgt)

    g, up1, down1, dcw1, dcb1, dfn1 = ffn_bwd(g, ffn1, 1)
    dz = _glu_bwd(zz, g)
    dyact = _matmul(dz, w["o_w_glu_t"], "nn", "mm_o_dyact")
    d_glu = _matmul(dz, yact, "tn", "mm_o_dglu", out_dtype=BF16)
    du, gs["o_D", 0], dbbr, dbbi, dcdr, dcdi, dabr, dabi = _s5_bwd(u, *s5_w, dyact)
    gs["o_C_re", 0] = _s5_out_blocks(dcdr).reshape(S5_GROUPS * S5_GROUP, S5_STATE)
    gs["o_C_im", 0] = _s5_out_blocks(dcdi).reshape(S5_GROUPS * S5_GROUP, S5_STATE)
    cts = (dabr.reshape(S5_GROUPS, S5_STATE), dabi.reshape(S5_GROUPS, S5_STATE), _s5_in_blocks(dbbr),
           _s5_in_blocks(dbbi))
    gs["o_A_re", 0], gs["o_A_im", 0], dlog_dt, gs["o_B_re", 0], gs["o_B_im", 0] = _s5_disc_bwd(*disc_in, cts)
    gs["o_log_dt", 0] = dlog_dt.reshape(1, S5_GROUPS)
    dxn = _matmul(du, w["o_w_in"], "nt", "mm_o_dxn")
    d_oin = _matmul(xn1, du, "tn", "mm_o_din", out_dtype=BF16)
    g, gs["o_norm_g", 0] = _rms_bwd(x2, w["o_norm_g"], dxn, g, "rms_o_bwd")
    g = send_grads("a", [("f_w_up", 1, up1), ("f_w_down", 1, down1), ("o_w_glu", 0, d_glu), ("o_w_in", 0, d_oin)], g)

    g, up0, down0, dcw0, dcb0, dfn0 = ffn_bwd(g, ffn0, 0)
    gs["f_conv_w", 0], gs["f_conv_w", 3] = dcw0, dcw1
    gs["f_conv_b", 0], gs["f_conv_b", 1] = dcb0, dcb1
    gs["f_norm_g", 0], gs["f_norm_g", 1] = dfn0, dfn1

    dycat = _matmul(g, w["e_w_out"], "nt", "mm_e_dycat")
    d_eout = _matmul(ycat, g, "tn", "mm_e_dout", out_dtype=BF16)
    dycat = send_grads("b", [("f_w_up", 0, up0), ("f_w_down", 0, down0), ("e_w_out", 0, d_eout)], dycat)
    dy, dr1, dk1, dv1, dgate, gs["e_ln_w", 0], gs["e_ln_b", 0], gs["e_r_k", 0] = _rwkv_post_bwd(
        y, r, k2, v, gate, w["e_ln_w"], w["e_ln_b"], rk, dycat)
    dr2, ddec, dk2, dzz, dbb, dv_pt = _wkv_bwd(r, dec, k2, z, b, v_exp, s_all, _expand_cols(dy, "wkv_expand_dy"))
    (dpam, gs["e_w0", 0], gs["e_w2", 0], gs["e_a0", 0], gs["e_a2", 0], gs["e_g2", 0], gs["e_k_k", 0],
     gs["e_k_a", 0]) = _rwkv_prep_bwd(pam, pw, (dr2, ddec, dk2, _from_pt(dv_pt), dzz, dbb, dgate), (dr1, dk1, dv1))
    dpa, gs["e_mu", 0] = _tshift_bwd(p, w["e_mu"], dpam)
    (dbx, dbg, gs["e_conv_w", 0], gs["e_conv_b", 0], dga, gs["e_gate_a_b", 0], dgx, gs["e_gate_x_b", 0],
     gs["e_lru_lambda", 0]) = _lru_bwd(p, *lru_w, dycat)
    gs["e_gate_a_w", 0] = _gate_blocks(dga).reshape(LRU_W, HEAD)
    gs["e_gate_x_w", 0] = _gate_blocks(dgx).reshape(LRU_W, HEAD)
    dp = jnp.concatenate([dpa, dbx, dbg], axis=1)
    d_ein = _matmul(dp, xn0, "tn", "mm_e_din", out_dtype=BF16)
    dp = send_grads("c", [("e_w_in", 0, d_ein)], dp)
    dxn = _matmul(dp, w["e_w_in_t"], "nn", "mm_e_dxn")
    grad_x, gs["e_norm_g", 0] = _rms_bwd(x, w["e_norm_g"], dxn, g, "rms_e_bwd")
    return loss, grad_x, gs


CAST_ROWS = 256


def _cast_shard(w3, layer, transpose, chip, name, after=None):
    _, rows, cols = w3.shape
    tr = _tile(rows, (CAST_ROWS, 176, 128))

    def body(c_ref, w_ref, *rest):
        v = w_ref[...]
        rest[-1][...] = (v.T if transpose else v).astype(BF16)

    in_spec = pl.BlockSpec((None, tr, cols), lambda i, c: (layer, i, 0))
    if transpose:
        out_spec, shape = pl.BlockSpec((None, cols, tr), lambda i, c: (c[0], 0, i)), (cols, rows)
    else:
        out_spec, shape = pl.BlockSpec((None, tr, cols), lambda i, c: (c[0], i, 0)), (rows, cols)
    extra = [] if after is None else [after]
    grid_spec = pltpu.PrefetchScalarGridSpec(num_scalar_prefetch=1, grid=(rows // tr,),
                                             in_specs=[in_spec] + [_ANY] * len(extra), out_specs=out_spec)
    return _blocked(body, name=name, grid_spec=grid_spec,
                          out_shape=jax.ShapeDtypeStruct((N_CHIPS,) + shape, BF16),
                          compiler_params=_cparams(("parallel",), VMEM_MID))(chip, w3, *extra)


_ANY = pl.BlockSpec(memory_space=pl.ANY)


def _coords():
    return lax.axis_index("x"), lax.axis_index("y"), lax.axis_index("c")


def _flip(v, d):
    return 1 - v if d else v


_CHIP_RELS = ((1, 0), (0, 1), (1, 1))
_DEV_RELS = tuple((dx, dy, dc) for dx in (0, 1) for dy in (0, 1) for dc in (0, 1))[1:]


_HBM = pl.BlockSpec(memory_space=pltpu.HBM)
_SEM = pl.BlockSpec(memory_space=pltpu.SEMAPHORE)
_EFFECT = pltpu.SideEffectType.DATAFLOW_SIDE_EFFECTING


def _in_hbm(a):
    return pltpu.with_memory_space_constraint(a, pltpu.HBM)


def _gather_copies(bufs, send, recv, landed, halved=False):
    x, y, c = _coords()
    me = 2 * x + y
    res = []
    for i, buf in enumerate(bufs):
        half = buf.shape[1] // 2
        part = (lambda slot: buf.at[slot, pl.ds(c * half, half)]) if halved else (lambda slot: buf.at[slot])
        for j, (dx, dy) in enumerate(_CHIP_RELS):
            px, py = _flip(x, dx), _flip(y, dy)
            k = i * len(_CHIP_RELS) + j
            res.append(pltpu.make_async_remote_copy(
                src_ref=part(me), dst_ref=part(2 * px + py if landed else me), send_sem=send.at[k],
                recv_sem=recv.at[k], device_id=(px, py, c), device_id_type=MESH))
    return res


def _swap_fetched(bufs):
    n = len(bufs)
    nr = len(_CHIP_RELS)

    def body(*refs):
        outs, (send, recv) = refs[n:2 * n], refs[2 * n:]
        x, y, c = _coords()
        sib = (x, y, 1 - c)
        sends, recvs = [], []
        for i in range(n):
            half = outs[i].shape[1] // 2
            for j, (dx, dy) in enumerate(_CHIP_RELS):
                slot = 2 * _flip(x, dx) + _flip(y, dy)
                mine = outs[i].at[slot, pl.ds(c * half, half)]
                k = i * nr + j
                cp = pltpu.make_async_remote_copy(src_ref=mine, dst_ref=mine, send_sem=send.at[k], recv_sem=recv.at[k],
                                                  device_id=sib, device_id_type=MESH)
                cp.start()
                sends.append(cp)
                recvs.append(pltpu.make_async_remote_copy(
                    src_ref=mine, dst_ref=outs[i].at[slot, pl.ds((1 - c) * half, half)], send_sem=send.at[k],
                    recv_sem=recv.at[k], device_id=sib, device_id_type=MESH))
        for cp in recvs:
            cp.wait_recv()
        for cp in sends:
            cp.wait_send()

    return pl.pallas_call(
        body, name="swap_fetched", in_specs=[_ANY] * n, out_specs=[_ANY] * n,
        out_shape=[jax.ShapeDtypeStruct(a.shape, a.dtype) for a in bufs],
        input_output_aliases={i: i for i in range(n)},
        scratch_shapes=[pltpu.SemaphoreType.DMA((n * nr,)), pltpu.SemaphoreType.DMA((n * nr,))])(*bufs)


def _scatter_copies(srcs, lands, send, recv, landed):
    x, y, c = _coords()
    me = 4 * x + 2 * y + c
    res = []
    for i, (src, land) in enumerate(zip(srcs, lands)):
        for j, (dx, dy, dc) in enumerate(_DEV_RELS):
            peer = (_flip(x, dx), _flip(y, dy), _flip(c, dc))
            pid = 4 * peer[0] + 2 * peer[1] + peer[2]
            k = i * len(_DEV_RELS) + j
            res.append(pltpu.make_async_remote_copy(
                src_ref=src.at[pid], dst_ref=land.at[pid if landed else me], send_sem=send.at[k],
                recv_sem=recv.at[k], device_id=peer, device_id_type=MESH))
    return res


def _split_start(bufs, n_src, copies, n_rel, name, after):
    n = len(bufs)
    nk = n_src * n_rel

    def body(*refs):
        ins, send, recv, token = refs[:n], refs[n + 1 + n], refs[n + 2 + n], refs[-1]
        for cp in copies(ins, send, recv, False):
            cp.start()
        token[...] = jnp.zeros_like(token)

    res = pl.pallas_call(
        body, name=name, in_specs=[_HBM] * n + [_ANY],
        out_specs=[_HBM] * n + [_SEM, _SEM, pl.BlockSpec(memory_space=pltpu.VMEM)],
        out_shape=[pltpu.HBM(b.shape, b.dtype) for b in bufs]
        + [pltpu.SemaphoreType.DMA((nk,)), pltpu.SemaphoreType.DMA((nk,)), jax.ShapeDtypeStruct((8, LANES), F32)],
        input_output_aliases={i: i for i in range(n)},
        compiler_params=pltpu.CompilerParams(has_side_effects=_EFFECT))(*[_in_hbm(b) for b in bufs], after)
    return res[n], res[n + 1], list(res[:n]), res[n + 2]


def _split_wait(bufs, send, recv, copies, name, after):
    n = len(bufs)

    def body(*refs):
        ins, send_ref, recv_ref = refs[:n], refs[n], refs[n + 1]
        for cp in copies(ins, send_ref, recv_ref, True):
            cp.wait_send()
            cp.wait_recv()

    return pl.pallas_call(
        body, name=name, in_specs=[_HBM] * n + [_SEM, _SEM, _ANY], out_specs=[_HBM] * n,
        out_shape=[pltpu.HBM(b.shape, b.dtype) for b in bufs], input_output_aliases={i: i for i in range(n)},
        compiler_params=pltpu.CompilerParams(has_side_effects=_EFFECT))(*bufs, send, recv, after)


def _gather_start(bufs, name, after, halved=False):
    fn = functools.partial(_gather_copies, halved=halved)
    return _split_start(bufs, len(bufs), fn, len(_CHIP_RELS), name, after)


def _gather_wait(bufs, send, recv, name, after, halved=False):
    return _split_wait(bufs, send, recv, functools.partial(_gather_copies, halved=halved), name, after)


def _scatter_start(srcs, name, after):
    n = len(srcs)
    lands = [lax.empty(a.shape, a.dtype) for a in srcs]
    fn = lambda refs, send, recv, landed: _scatter_copies(refs[:n], refs[n:], send, recv, landed)
    send, recv, bufs, token = _split_start(list(srcs) + lands, n, fn, len(_DEV_RELS), name, after)
    return send, recv, bufs, token


def _scatter_wait(bufs, send, recv, name, after):
    n = len(bufs) // 2
    fn = lambda refs, s, r, landed: _scatter_copies(refs[:n], refs[n:], s, r, landed)
    res = _split_wait(bufs, send, recv, fn, name, after)
    return res[:n], res[n:]


def _sum_segments(src, land, me, name):
    nd, seg, cols = src.shape
    ts = _tile(seg, (256, 176, 128))

    def body(m_ref, *refs):
        o_ref = refs[-1]
        acc = refs[0][...].astype(F32)
        for r in refs[1:-1]:
            acc = acc + r[...].astype(F32)
        o_ref[...] = acc

    def peer(rel):
        bits = 4 * rel[0] + 2 * rel[1] + rel[2]
        return pl.BlockSpec((None, ts, cols), lambda i, m: (jnp.bitwise_xor(m[0], bits), i, 0))

    grid_spec = pltpu.PrefetchScalarGridSpec(
        num_scalar_prefetch=1, grid=(seg // ts,),
        in_specs=[pl.BlockSpec((None, ts, cols), lambda i, m: (m[0], i, 0))] + [peer(r) for r in _DEV_RELS],
        out_specs=pl.BlockSpec((None, ts, cols), lambda i, m: (m[1], i, 0)))
    return _blocked(body, name=name, grid_spec=grid_spec,
                          out_shape=jax.ShapeDtypeStruct((2, seg, cols), F32),
                          compiler_params=_cparams(("parallel",), VMEM_MID))(me, src, *[land] * len(_DEV_RELS))


def _exchange_sibling(arrs):
    n = len(arrs)

    def body(*refs):
        outs, (send, recv) = refs[n:2 * n], refs[2 * n:]
        x, y, c = _coords()
        sib = (x, y, 1 - c)
        sends, recvs = [], []
        for i in range(n):
            cp = pltpu.make_async_remote_copy(src_ref=outs[i].at[c], dst_ref=outs[i].at[c], send_sem=send.at[i],
                                              recv_sem=recv.at[i], device_id=sib, device_id_type=MESH)
            cp.start()
            sends.append(cp)
            recvs.append(pltpu.make_async_remote_copy(src_ref=outs[i].at[c], dst_ref=outs[i].at[1 - c],
                                                      send_sem=send.at[i], recv_sem=recv.at[i], device_id=sib,
                                                      device_id_type=MESH))
        for cp in recvs:
            cp.wait_recv()
        for cp in sends:
            cp.wait_send()

    return pl.pallas_call(
        body, name="exchange_sibling", in_specs=[_ANY] * n, out_specs=[_ANY] * n,
        out_shape=[jax.ShapeDtypeStruct(a.shape, a.dtype) for a in arrs],
        input_output_aliases={i: i for i in range(n)},
        scratch_shapes=[pltpu.SemaphoreType.DMA((n,)), pltpu.SemaphoreType.DMA((n,))])(*arrs)


def _allreduce_small(vec):
    _, nchips, seg, lanes = vec.shape
    nr = len(_CHIP_RELS)

    def body(in_ref, out_ref, from_sib, half, stage, red, send, recv):
        x, y, c = _coords()
        me = 2 * x + y
        sib = (x, y, 1 - c)
        chips = [(_flip(x, dx), _flip(y, dy)) for dx, dy in _CHIP_RELS]

        def copy(src, dst, k, peer):
            return pltpu.make_async_remote_copy(src_ref=src, dst_ref=dst, send_sem=send.at[k], recv_sem=recv.at[k],
                                                device_id=peer, device_id_type=MESH)

        to_sib = copy(in_ref.at[1 - c], from_sib, 0, sib)
        to_sib.start()
        to_sib.wait_recv()
        half[...] = in_ref[c] + from_sib[...]

        first = [copy(half.at[2 * px + py], stage.at[me], 1 + j, (px, py, c)) for j, (px, py) in enumerate(chips)]
        for cp in first:
            cp.start()
        stage[me] = half[me]
        for j, (px, py) in enumerate(chips):
            copy(half.at[2 * px + py], stage.at[2 * px + py], 1 + j, (px, py, c)).wait_recv()
        acc = stage[0]
        for k in range(1, nchips):
            acc = acc + stage[k]
        red[...] = acc
        out_ref[c, me] = acc

        second = [copy(red, out_ref.at[c, me], 1 + nr + j, (px, py, c)) for j, (px, py) in enumerate(chips)]
        for cp in second:
            cp.start()
        for j, (px, py) in enumerate(chips):
            copy(red, out_ref.at[c, 2 * px + py], 1 + nr + j, (px, py, c)).wait_recv()

        back = copy(out_ref.at[c], out_ref.at[c], 1 + 2 * nr, sib)
        back.start()
        copy(out_ref.at[c], out_ref.at[1 - c], 1 + 2 * nr, sib).wait_recv()
        for cp in [to_sib] + first + second + [back]:
            cp.wait_send()

    vm = pl.BlockSpec(memory_space=pltpu.VMEM)
    nsem = 2 + 2 * nr
    return pl.pallas_call(
        body, name="allreduce_small", in_specs=[vm], out_specs=vm,
        out_shape=jax.ShapeDtypeStruct(vec.shape, F32),
        scratch_shapes=[pltpu.VMEM((nchips, seg, lanes), F32), pltpu.VMEM((nchips, seg, lanes), F32),
                        pltpu.VMEM((nchips, seg, lanes), F32), pltpu.VMEM((seg, lanes), F32),
                        pltpu.SemaphoreType.DMA((nsem,)), pltpu.SemaphoreType.DMA((nsem,))],
        compiler_params=_cparams(None, VMEM_MID))(vec)


def _adam_math(w, g, m, v):
    m2 = ADAM_B1 * m + (1.0 - ADAM_B1) * g
    v2 = ADAM_B2 * v + (1.0 - ADAM_B2) * (g * g)
    m_hat = m2 / (1.0 - ADAM_B1 ** ADAM_STEP)
    v_hat = v2 / (1.0 - ADAM_B2 ** ADAM_STEP)
    return -ADAM_LR * (m_hat / (jnp.sqrt(v_hat) + ADAM_EPS) + ADAM_WD * w), m2, v2


def _adamw_big(w3, m3, v3, layer, g, transposed, name, prev=None):
    nl, rows, cols = w3.shape
    tr = 128 if transposed else _tile(rows, (256, 176, 128))

    def body(w_ref, m_ref, v_ref, g_ref, *rest):
        go_ref, d_ref, mo_ref, vo_ref = rest[-4:]
        g_val = g_ref[...].T if transposed else g_ref[...]
        go_ref[...] = g_val
        d_ref[...], mo_ref[...], vo_ref[...] = _adam_math(w_ref[...], g_val, m_ref[...], v_ref[...])

    wspec = pl.BlockSpec((None, tr, cols), lambda i: (layer, i, 0))
    gspec = pl.BlockSpec((cols, tr), lambda i: (0, i)) if transposed else pl.BlockSpec((tr, cols), lambda i: (i, 0))
    extra = [] if prev is None else list(prev)
    return _blocked(body, name=name, grid=(rows // tr,),
                          in_specs=[wspec, wspec, wspec, gspec] + [_ANY] * len(extra),
                          out_specs=[wspec] * 4, out_shape=[jax.ShapeDtypeStruct((nl, rows, cols), F32)] * 4,
                          input_output_aliases={4 + i: i for i in range(len(extra))},
                          compiler_params=_cparams(("parallel",), VMEM_MID))(w3, m3, v3, g, *extra)


_SMALL = (
    ("e_norm_g", (1, D_MODEL), None), ("e_mu", (1, SHIFT_COLS), None), ("e_w0", (1, RW), None),
    ("e_w2", (W_LORA, RW), RW // 4), ("e_a0", (1, RW), None), ("e_a2", (A_LORA, RW), RW // 4),
    ("e_g2", (G_LORA, RW), RW // 4), ("e_k_k", (1, RW), None), ("e_k_a", (1, RW), None), ("e_r_k", (1, RW), None),
    ("e_ln_w", (1, RW), None), ("e_ln_b", (1, RW), None), ("e_conv_w", (4, LRU_W), LRU_W // 4),
    ("e_conv_b", (1, LRU_W), None), ("e_gate_a_w", (LRU_W, HEAD), None), ("e_gate_a_b", (1, LRU_W), None),
    ("e_gate_x_w", (LRU_W, HEAD), None), ("e_gate_x_b", (1, LRU_W), None), ("e_lru_lambda", (1, LRU_W), None),
    ("o_norm_g", (1, D_MODEL), D_MODEL // 4), ("o_A_re", (S5_GROUPS, S5_STATE), None),
    ("o_A_im", (S5_GROUPS, S5_STATE), None), ("o_log_dt", (1, S5_GROUPS), None),
    ("o_B_re", (S5_GROUPS, S5_STATE * S5_GROUP), None), ("o_B_im", (S5_GROUPS, S5_STATE * S5_GROUP), None),
    ("o_C_re", (S5_GROUPS * S5_GROUP, S5_STATE), None), ("o_C_im", (S5_GROUPS * S5_GROUP, S5_STATE), None),
    ("o_D", (1, D_MODEL), D_MODEL // 4), ("f_norm_g", (2, D_MODEL), None),
    ("f_conv_w", (6, 2 * D_FF), 2 * D_FF // 4), ("f_conv_b", (2, 2 * D_FF), None),
    ("final_norm_g", (1, D_MODEL), None))
_PIECES = {"f_norm_g": ((0, 1), (1, 1)), "f_conv_b": ((0, 1), (1, 1)), "f_conv_w": ((0, 3), (3, 3))}


def _ceil_to(n, m):
    return -(-n // m) * m


def _small_layout():
    groups = {}
    for name, (rows, cols), _ in _SMALL:
        for first, r in _PIECES.get(name, ((0, rows),)):
            groups.setdefault(cols, []).append((name, first, r))
    layout, off = {}, 0
    for cols, items in groups.items():
        stacks = [0, 0] if 2 * cols <= LANES else [0]
        placed = []
        for name, first, r in sorted(items, key=lambda it: -it[2]):
            half = stacks.index(min(stacks))
            r0 = stacks[half]
            if r >= 8 or r0 % 8 + r > 8:
                r0 = _ceil_to(r0, 8)
            placed.append((name, first, r, r0, half * (LANES // 2)))
            stacks[half] = r0 + r
        rpad = _ceil_to(max(stacks), 8)
        for name, first, r, at, lane in placed:
            layout[name, first] = (off, rpad, at, r, cols, lane)
        off += -(-cols // LANES) * rpad
    return layout, _ceil_to(off, 8 * N_DEV)


def _small_pack(gs):
    layout, total = _small_layout()
    keys = list(layout)

    def body(*refs):
        out = refs[-1]
        out[...] = jnp.zeros_like(out)
        for key, g_ref in zip(keys, refs[:-1]):
            off, rpad, at, r, cols, lane = layout[key]
            for j in range(-(-cols // LANES)):
                cw = min(LANES, cols - j * LANES)
                out[off + j * rpad + at:off + j * rpad + at + r, lane:lane + cw] = g_ref[:, j * LANES:j * LANES + cw]

    return pl.pallas_call(body, name="small_pack", out_shape=jax.ShapeDtypeStruct((total, LANES), F32),
                          compiler_params=_cparams(None, VMEM_MID))(*[gs[k] for k in keys])


def _adamw_small(red, chip, wts, ms, vs):
    layout, _ = _small_layout()
    names = [n for n, _, _ in _SMALL]
    n = len(names)

    def body(chip_ref, red_ref, *refs):
        ins, outs = refs[:3 * n], refs[3 * n:]
        c = chip_ref[0]
        for i, (name, (rows, cols), loc) in enumerate(_SMALL):
            w_ref, m_ref, v_ref = ins[3 * i:3 * i + 3]
            o_refs = outs[4 * i:4 * i + 4]
            width = cols if loc is None else loc
            for first, r in _PIECES.get(name, ((0, rows),)):
                off, rpad, at, _, _, lane = layout[name, first]
                for j in range(-(-width // LANES)):
                    cw = min(LANES, width - j * LANES)
                    ls = slice(lane, lane + cw)
                    if loc is None:
                        start = off + j * rpad + at
                        g = red_ref[start:start + r, ls]
                    else:
                        blk = c * (loc // LANES) + j
                        if r >= 8:
                            g = red_ref[pl.ds(pl.multiple_of(off + at + blk * rpad, 8), r), ls]
                        else:
                            tile = red_ref[pl.ds(pl.multiple_of(off + at // 8 * 8 + blk * rpad, 8), 8), ls]
                            g = tile[at % 8:at % 8 + r]
                    rs, cs = slice(first, first + r), slice(j * LANES, j * LANES + cw)
                    d, m2, v2 = _adam_math(w_ref[rs, cs], g, m_ref[rs, cs], v_ref[rs, cs])
                    for o, val in zip(o_refs, (g, d, m2, v2)):
                        o[rs, cs] = val

    args, shapes = [], []
    for name in names:
        args += [wts[name], ms[name], vs[name]]
        shapes += [jax.ShapeDtypeStruct(wts[name].shape, F32)] * 4
    vm = pl.BlockSpec(memory_space=pltpu.VMEM)
    res = pl.pallas_call(body, name="adamw_small",
                         in_specs=[pl.BlockSpec(memory_space=pltpu.SMEM), vm] + [vm] * (3 * n),
                         out_specs=[vm] * (4 * n), out_shape=shapes,
                         compiler_params=_cparams(None, VMEM_BIG))(chip, red, *args)
    return {name: res[4 * i:4 * i + 4] for i, name in enumerate(names)}


PACK_ROWS = 8


def _packed_rows(shape):
    size = 1
    for d in shape:
        size *= d
    return -(-size // (PACK_ROWS * LANES)) * PACK_ROWS


def _pack(arrs, row_mult):
    parts = []
    for a in arrs:
        flat = a.reshape(-1).astype(F32)
        rows = _packed_rows(a.shape)
        parts.append(jnp.pad(flat, (0, rows * LANES - flat.shape[0])).reshape(rows, LANES))
    total = sum(p.shape[0] for p in parts)
    fill = -(-total // row_mult) * row_mult - total
    if fill:
        parts.append(jnp.zeros((fill, LANES), F32))
    return jnp.concatenate(parts, axis=0)


def _unpack(packed, shapes):
    out, off = [], 0
    for s in shapes:
        rows = _packed_rows(s)
        size = 1
        for d in s:
            size *= d
        out.append(packed[off:off + rows].reshape(-1)[:size].reshape(s))
        off += rows
    return out


_SMALL_SH = ("e_w2", "e_a2", "e_g2", "e_conv_w", "o_norm_g", "o_D", "f_conv_w")
_LARGE = (("e_w_in", True), ("e_w_out", False), ("o_w_in", False), ("o_w_glu", True), ("f_w_up", True),
        ("f_w_down", False))
_ORDER = ("e_norm_g", "e_w_in", "e_mu", "e_w0", "e_w2", "e_a0", "e_a2", "e_g2", "e_k_k", "e_k_a", "e_r_k", "e_ln_w",
          "e_ln_b", "e_conv_w", "e_conv_b", "e_gate_a_w", "e_gate_a_b", "e_gate_x_w", "e_gate_x_b", "e_lru_lambda",
          "e_w_out", "o_norm_g", "o_w_in", "o_A_re", "o_A_im", "o_log_dt", "o_B_re", "o_B_im", "o_C_re", "o_C_im",
          "o_D", "o_w_glu", "f_norm_g", "f_w_up", "f_conv_w", "f_conv_b", "f_w_down", "final_norm_g")
N_CHIPS = 4
N_DEV = 8


def _step(x, tgt, wts, ms, vs):
    xi, yi, ci = _coords()
    chip = 2 * xi + yi
    chip1 = chip.astype(jnp.int32).reshape(1)
    me2 = jnp.stack([4 * xi + 2 * yi + ci, ci]).astype(jnp.int32)
    by_cols = dict(_LARGE)

    cast = lambda name, l, after=None: _cast_shard(wts[name], l, by_cols[name], chip1, f"cast_{name}{l}", after)
    sh_shapes = [wts[n].shape for n in _SMALL_SH]
    packed = _pack([wts[n] for n in _SMALL_SH], 16)
    small_buf = lax.dynamic_update_slice(jnp.zeros((N_CHIPS,) + packed.shape, F32), packed[None], (chip, 0, 0))
    late = [(name, l) for name, _ in _LARGE if name != "e_w_in" for l in range(wts[name].shape[0])]
    send, recv, thru, token = _gather_start([cast("e_w_in", 0), small_buf], "gather_start_a", x, halved=True)
    bufs = {(name, l): cast(name, l, token) for name, l in late}
    got = _swap_fetched(_gather_wait(thru, send, recv, "gather_wait_a", bufs[late[-1]], halved=True))
    send_b, recv_b, thru_b, token = _gather_start([bufs[k] for k in late], "gather_start_b", got[0])
    x, _ = lax.optimization_barrier((x, token))

    def rows(g):
        return g.reshape(N_CHIPS * g.shape[1], g.shape[2])

    full = {n: wts[n] for n, _, loc in _SMALL if loc is None}
    full["e_w_in_t"] = rows(got[0])
    per_chip = [_unpack(got[1][k], sh_shapes) for k in range(N_CHIPS)]
    for i, n in enumerate(_SMALL_SH):
        full[n] = jnp.concatenate([per_chip[k][i] for k in range(N_CHIPS)], axis=-1)

    def late_weights(after):
        res = dict(zip(late, _gather_wait(thru_b, send_b, recv_b, "gather_wait_b", after)))
        return {"e_w_out": rows(res[("e_w_out", 0)]), "o_w_in": rows(res[("o_w_in", 0)]),
                "o_w_glu_t": rows(res[("o_w_glu", 0)]),
                "f_w_up_t": [rows(res[("f_w_up", l)]) for l in range(2)],
                "f_w_down": [rows(res[("f_w_down", l)]) for l in range(2)]}

    pending = []

    def send_grads(tag, items, carry):
        srcs = [g.reshape(N_DEV, g.shape[0] // N_DEV, g.shape[1]) for _, _, g in items]
        s_sem, r_sem, both, tok = _scatter_start(srcs, f"scatter_start_{tag}", carry)
        pending.append((tag, [(name, l) for name, l, _ in items], s_sem, r_sem, both))
        carry, _ = lax.optimization_barrier((carry, tok))
        return carry

    loss, grad_x, gs = _local_step(x, tgt, full, late_weights, send_grads)

    final = {}
    red = _allreduce_small(_small_pack(gs).reshape(2, N_CHIPS, -1, LANES)).reshape(-1, LANES)
    view = {name: (rows, cols if loc is None else loc) for name, (rows, cols), loc in _SMALL}
    as2d = lambda d: {name: d[name].reshape(view[name]) for name in view}
    small = _adamw_small(red, chip1, as2d(wts), as2d(ms), as2d(vs))
    for name, res in small.items():
        final[name] = [r.reshape(wts[name].shape) for r in res]
    new_v = small["final_norm_g"][3]

    halves, keys = [], []
    for tag, names, s_sem, r_sem, both in pending:
        srcs, lands = _scatter_wait(both, s_sem, r_sem, f"scatter_wait_{tag}", new_v)
        for (name, l), src, land in zip(names, srcs, lands):
            halves.append(_sum_segments(src, land, me2, f"sum_{name}{l}"))
            keys.append((name, l))
    shards = _exchange_sibling(halves)
    for s, (name, l) in zip(shards, keys):
        final[name] = _adamw_big(wts[name], ms[name], vs[name], l, s.reshape(2 * s.shape[1], s.shape[2]),
                                 by_cols[name], f"adamw_{name}{l}", prev=final.get(name))

    loss = lax.psum(loss[0, 0], ("x", "y", "c"))
    res = [loss, grad_x[None]]
    for k in range(4):
        res += [final[n][k] for n in _ORDER]
    return tuple(res)


def kernel(x, e_norm_g, e_w_in, e_mu, e_w0, e_w2, e_a0, e_a2, e_g2, e_k_k, e_k_a, e_r_k, e_ln_w, e_ln_b, e_conv_w, e_conv_b, e_gate_a_w, e_gate_a_b, e_gate_x_w, e_gate_x_b, e_lru_lambda, e_w_out, o_norm_g, o_w_in, o_A_re, o_A_im, o_log_dt, o_B_re, o_B_im, o_C_re, o_C_im, o_D, o_w_glu, f_norm_g, f_w_up, f_conv_w, f_conv_b, f_w_down, final_norm_g, loss_target, m_e_norm_g, m_e_w_in, m_e_mu, m_e_w0, m_e_w2, m_e_a0, m_e_a2, m_e_g2, m_e_k_k, m_e_k_a, m_e_r_k, m_e_ln_w, m_e_ln_b, m_e_conv_w, m_e_conv_b, m_e_gate_a_w, m_e_gate_a_b, m_e_gate_x_w, m_e_gate_x_b, m_e_lru_lambda, m_e_w_out, m_o_norm_g, m_o_w_in, m_o_A_re, m_o_A_im, m_o_log_dt, m_o_B_re, m_o_B_im, m_o_C_re, m_o_C_im, m_o_D, m_o_w_glu, m_f_norm_g, m_f_w_up, m_f_conv_w, m_f_conv_b, m_f_w_down, m_final_norm_g, v_e_norm_g, v_e_w_in, v_e_mu, v_e_w0, v_e_w2, v_e_a0, v_e_a2, v_e_g2, v_e_k_k, v_e_k_a, v_e_r_k, v_e_ln_w, v_e_ln_b, v_e_conv_w, v_e_conv_b, v_e_gate_a_w, v_e_gate_a_b, v_e_gate_x_w, v_e_gate_x_b, v_e_lru_lambda, v_e_w_out, v_o_norm_g, v_o_w_in, v_o_A_re, v_o_A_im, v_o_log_dt, v_o_B_re, v_o_B_im, v_o_C_re, v_o_C_im, v_o_D, v_o_w_glu, v_f_norm_g, v_f_w_up, v_f_conv_w, v_f_conv_b, v_f_w_down, v_final_norm_g):
    args = locals()
    wts = {n: args[n] for n in _ORDER}
    ms = {n: args["m_" + n] for n in _ORDER}
    vs = {n: args["v_" + n] for n in _ORDER}
    return _step(x[0], loss_target[0], wts, ms, vs)
```

```python
import functools

import jax
import jax.numpy as jnp
from jax import lax
from jax.experimental import pallas as pl
from jax.experimental.pallas import tpu as pltpu

F32 = jnp.float32
BF16 = jnp.bfloat16
MESH = pl.DeviceIdType.MESH

D_MODEL = 1024
HEAD = 64
RW = 512
N_HEADS = RW // HEAD
LRU_W = 512
SHIFT_COLS = 1792
W_LORA, A_LORA, G_LORA = 64, 64, 128
S5_GROUPS, S5_GROUP, S5_STATE = 64, 16, 64
D_FF = 2816
NORM_EPS = 1e-6
GN_EPS = 64e-5
LRU_C = 8.0
ADAM_LR, ADAM_B1, ADAM_B2, ADAM_EPS, ADAM_WD, ADAM_STEP = 0.001, 0.9, 0.999, 1e-08, 0.01, 10

VMEM_BIG = 56 * 1024 * 1024
VMEM_MID = 40 * 1024 * 1024
LANES = 128
PT = 16
WKV_CHUNK = 32
S5_SLAB = 128


def _blocked(*args, **kw):
    call = pl.pallas_call(*args, **kw)

    def run(*ops):
        return call(*[pltpu.with_memory_space_constraint(a, pltpu.HBM) if a.ndim >= 2 else a for a in ops])

    return run


def _cparams(sem=None, vmem=None):
    kw = {}
    if sem is not None:
        kw["dimension_semantics"] = sem
    if vmem is not None:
        kw["vmem_limit_bytes"] = vmem
    return pltpu.CompilerParams(**kw)


def _tile(dim, cands):
    for c in cands:
        if dim % c == 0:
            return c
    return dim


def _full(shape):
    n = len(shape)
    return pl.BlockSpec(shape, lambda *_: (0,) * n)


_TILES = (2816, 2048, 1408, 1024, 512, 256, 128)
MM_BUDGET = 36 * 1024 * 1024
VMEM_SLACK = 12 * 1024 * 1024


MXU_FLOPS = 9.0e14
HBM_BYTES = 3.3e12
STEP_SECONDS = 0.35e-6


def _mm_tiles(m, n, k, size_a, size_b, size_o, has_add, parts=1, tk_only=None, tm_max=None):
    best = None
    for tm in _TILES:
        for tk in _TILES:
            for tn in _TILES:
                if m % tm or n % tn or k % tk or (tk_only and tk != tk_only) or (tm_max and tm_max % tm):
                    continue
                need = (2 * (parts * tm * tk * size_a + tk * tn * size_b + tm * tn * size_o)
                        + tm * tn * 4 * (1 + 2 * has_add))
                if k > tk:
                    need += tm * tn * 4
                if need > MM_BUDGET:
                    continue
                steps = (m // tm) * (n // tn) * (k // tk)
                a_reads = n // tn if k > tk else 1
                moved = (m * k * size_a * a_reads + k * n * size_b * (m // tm) + m * n * (size_o + 4 * has_add))
                cost = max(2.0 * m * n * k / MXU_FLOPS, moved / HBM_BYTES) + steps * STEP_SECONDS
                cand = (-cost, tk, tm, tn)
                if best is None or cand > best[0]:
                    best = (cand, need)
    (_, tk, tm, tn), need = best
    return tm, tn, tk, need


def _matmul(a, b, mode, name, out_dtype=F32, add=None):
    parts = a if isinstance(a, tuple) else (a,)
    na = len(parts)
    wide = parts[0].shape[1]
    if mode == "nn":
        (m, k), (k2, n) = (parts[0].shape[0], na * wide), b.shape
    elif mode == "nt":
        (m, k), (n, k2) = (parts[0].shape[0], na * wide), b.shape
    else:
        (k, m), (k2, n) = (parts[0].shape[0], na * wide), b.shape
    assert k == k2, (parts[0].shape, b.shape, mode)
    split = {} if na == 1 else ({"tm_max": wide} if mode == "tn" else {"tk_only": wide})
    tm, tn, tk, need = _mm_tiles(m, n, k, parts[0].dtype.itemsize, b.dtype.itemsize, jnp.dtype(out_dtype).itemsize,
                                 add is not None, parts=na, **split)
    nk = k // tk
    per_part = wide // (tm if mode == "tn" else tk)
    dims = {"nn": (((1,), (0,)), ((), ())), "nt": (((1,), (1,)), ((), ())), "tn": (((0,), (0,)), ((), ()))}[mode]

    def body(*refs):
        a_refs, b_ref = refs[:na], refs[na]
        add_ref = refs[na + 1] if add is not None else None
        o_ref = refs[na + 2] if add is not None else refs[na + 1]
        kk = pl.program_id(2)

        def finish(r):
            if add_ref is not None:
                r = r + add_ref[...]
            o_ref[...] = r.astype(o_ref.dtype)

        def use(a_ref):
            part = lax.dot_general(a_ref[...].astype(BF16), b_ref[...].astype(BF16), dims, preferred_element_type=F32)
            if nk == 1:
                finish(part)
                return
            acc = refs[-1]

            @pl.when(kk == 0)
            def _():
                acc[...] = part

            @pl.when(kk > 0)
            def _():
                acc[...] += part

            @pl.when(kk == nk - 1)
            def _():
                finish(acc[...])

        if na == 1:
            use(a_refs[0])
        else:
            which = (pl.program_id(0) if mode == "tn" else kk) // per_part
            for p in range(na):
                pl.when(which == p)(functools.partial(use, a_refs[p]))

    def a_spec(p):
        def along(pos):
            return jnp.clip(pos - p * per_part, 0, per_part - 1) if na > 1 else pos
        if mode == "tn":
            return pl.BlockSpec((tk, tm), lambda i, j, kk: (kk, along(i)))
        return pl.BlockSpec((tm, tk), lambda i, j, kk: (i, along(kk)))

    if mode == "nn":
        b_spec = pl.BlockSpec((tk, tn), lambda i, j, kk: (kk, j))
    elif mode == "nt":
        b_spec = pl.BlockSpec((tn, tk), lambda i, j, kk: (j, kk))
    else:
        b_spec = pl.BlockSpec((tk, tn), lambda i, j, kk: (kk, j))
    o_spec = pl.BlockSpec((tm, tn), lambda i, j, kk: (i, j))
    in_specs = [a_spec(p) for p in range(na)] + [b_spec] + ([o_spec] if add is not None else [])
    args = parts + (b,) + ((add,) if add is not None else ())
    return _blocked(
        body, name=name, grid=(m // tm, n // tn, nk),
        in_specs=in_specs, out_specs=o_spec,
        out_shape=jax.ShapeDtypeStruct((m, n), out_dtype),
        scratch_shapes=[pltpu.VMEM((tm, tn), F32)] if nk > 1 else [],
        compiler_params=_cparams(("parallel", "parallel", "arbitrary"), min(VMEM_BIG, need + VMEM_SLACK)),
    )(*args)


TOK = 256
ROWS = 512


def _rms(x, g):
    return x * lax.rsqrt(jnp.mean(x * x, axis=-1, keepdims=True) + NORM_EPS) * g


def _rms_fwd(x, g, name):
    t, d = x.shape

    def body(x_ref, g_ref, o_ref):
        o_ref[...] = _rms(x_ref[...], g_ref[...]).astype(BF16)

    row = pl.BlockSpec((ROWS, d), lambda i: (i, 0))
    return _blocked(body, name=name, grid=(t // ROWS,), in_specs=[row, _full((1, d))], out_specs=row,
                          out_shape=jax.ShapeDtypeStruct((t, d), BF16),
                          compiler_params=_cparams(("parallel",), VMEM_MID))(x, g)


def _rms_bwd(x, g, dxn, res, name):
    t, d = x.shape

    def body(x_ref, g_ref, d_ref, res_ref, dx_ref, dg_ref):
        _, vjp = jax.vjp(_rms, x_ref[...], g_ref[...])
        dx, dg = vjp(d_ref[...].astype(F32))
        dx_ref[...] = dx + res_ref[...]

        @pl.when(pl.program_id(0) == 0)
        def _():
            dg_ref[...] = jnp.zeros_like(dg_ref)

        dg_ref[...] += dg

    row = pl.BlockSpec((ROWS, d), lambda i: (i, 0))
    return _blocked(body, name=name, grid=(t // ROWS,), in_specs=[row, _full((1, d)), row, row],
                          out_specs=[row, _full((1, d))],
                          out_shape=[jax.ShapeDtypeStruct((t, d), F32), jax.ShapeDtypeStruct((1, d), F32)],
                          compiler_params=_cparams(("arbitrary",), VMEM_MID))(x, g, dxn, res)


def _loss_head(x, g, tgt):
    t, d = x.shape

    def body(x_ref, g_ref, t_ref, l_ref, dx_ref, dg_ref):
        tg = t_ref[...]

        def fn(xv, gv):
            err = _rms(xv, gv) - tg
            per_tok = jnp.mean(err * err, axis=-1, keepdims=True)
            return 0.5 * jnp.sum(per_tok, axis=0, keepdims=True)

        l, vjp = jax.vjp(fn, x_ref[...], g_ref[...])
        dx, dg = vjp(jnp.ones((1, 1), F32))
        dx_ref[...] = dx

        @pl.when(pl.program_id(0) == 0)
        def _():
            dg_ref[...] = jnp.zeros_like(dg_ref)
            l_ref[...] = jnp.zeros_like(l_ref)

        dg_ref[...] += dg
        l_ref[...] += jnp.broadcast_to(l, l_ref.shape)

    row = pl.BlockSpec((ROWS, d), lambda i: (i, 0))
    return _blocked(body, name="loss_head", grid=(t // ROWS,), in_specs=[row, _full((1, d)), row],
                          out_specs=[_full((1, LANES)), row, _full((1, d))],
                          out_shape=[jax.ShapeDtypeStruct((1, LANES), F32), jax.ShapeDtypeStruct((t, d), F32),
                                     jax.ShapeDtypeStruct((1, d), F32)],
                          compiler_params=_cparams(("arbitrary",), VMEM_MID))(x, g, tgt)


def _glu_fwd(x, z):
    t, d = x.shape

    def body(x_ref, v_ref, g_ref, o_ref):
        o_ref[...] = x_ref[...] + v_ref[...] * jax.nn.sigmoid(g_ref[...])

    row = pl.BlockSpec((ROWS, d), lambda i: (i, 0))
    gate = pl.BlockSpec((ROWS, d), lambda i: (i, 1))
    return _blocked(body, name="glu_fwd", grid=(t // ROWS,), in_specs=[row, row, gate], out_specs=row,
                          out_shape=jax.ShapeDtypeStruct((t, d), F32),
                          compiler_params=_cparams(("parallel",), VMEM_MID))(x, z, z)


def _glu_bwd(z, g):
    t, d = g.shape

    def body(v_ref, g_ref, d_ref, o_ref):
        s = jax.nn.sigmoid(g_ref[...])
        dy = d_ref[...]
        o_ref[:, :d] = (dy * s).astype(BF16)
        o_ref[:, d:] = (dy * v_ref[...] * s * (1.0 - s)).astype(BF16)

    row = pl.BlockSpec((ROWS, d), lambda i: (i, 0))
    gate = pl.BlockSpec((ROWS, d), lambda i: (i, 1))
    return _blocked(body, name="glu_bwd", grid=(t // ROWS,), in_specs=[row, gate, row],
                          out_specs=pl.BlockSpec((ROWS, 2 * d), lambda i: (i, 0)),
                          out_shape=jax.ShapeDtypeStruct((t, 2 * d), BF16),
                          compiler_params=_cparams(("parallel",), VMEM_MID))(z, z, g)


def _shift_down(x, d):
    row = lax.broadcasted_iota(jnp.int32, x.shape, 0)
    return jnp.where(row < d, 0.0, pltpu.roll(x, d, 0))


def _shift_up(x, d):
    n = x.shape[0]
    row = lax.broadcasted_iota(jnp.int32, x.shape, 0)
    return jnp.where(row >= n - d, 0.0, pltpu.roll(x, n - d, 0))


def _make_sd():
    @functools.partial(jax.custom_vjp, nondiff_argnums=(1,))
    def sd(x, d):
        return _shift_down(x, d)

    def fwd(x, d):
        return _shift_down(x, d), None

    def bwd(d, _, g):
        return (_shift_up(g, d),)

    sd.defvjp(fwd, bwd)
    return sd


def _lin_scan(a, u, reverse=False):
    n = a.shape[0]
    row = lax.broadcasted_iota(jnp.int32, a.shape, 0)
    d = 1
    while d < n:
        if reverse:
            keep = row < n - d
            a_s, u_s = pltpu.roll(a, n - d, 0), pltpu.roll(u, n - d, 0)
        else:
            keep = row >= d
            a_s, u_s = pltpu.roll(a, d, 0), pltpu.roll(u, d, 0)
        u = u + a * jnp.where(keep, u_s, 0.0)
        a = a * jnp.where(keep, a_s, 1.0)
        d *= 2
    return u


def _make_scan():
    @jax.custom_vjp
    def scan(a, u):
        return _lin_scan(a, u)

    def fwd(a, u):
        h = _lin_scan(a, u)
        return h, (a, h)

    def bwd(res, dh):
        a, h = res
        g = _lin_scan(_shift_up(a, 1), dh, reverse=True)
        return g * _shift_down(h, 1), g

    scan.defvjp(fwd, bwd)
    return scan


def _acc_out(ref, val):
    @pl.when(pl.program_id(0) == 0)
    def _():
        ref[...] = jnp.zeros_like(ref)

    ref[...] += val


FFN_CW = 128


def _ffn_fn(hg, hv, wg, wv, bg, bv, sd):
    cg = wg[0:1] * sd(hg, 2) + wg[1:2] * sd(hg, 1) + wg[2:3] * hg + bg
    cv = wv[0:1] * sd(hv, 2) + wv[1:2] * sd(hv, 1) + wv[2:3] * hv + bv
    return jax.nn.silu(cg) * cv


def _ffn_specs(t):
    nb = D_FF // FFN_CW
    col = lambda r, off: pl.BlockSpec((r, FFN_CW), lambda j: (0, j + off))
    return nb, [col(t, 0), col(t, nb), col(3, 0), col(3, nb), col(1, 0), col(1, nb)], col


def _ffn_mid_fwd(h, cw, cb, name):
    t = h.shape[0]
    nb, in_specs, col = _ffn_specs(t)

    def body(hg, hv, wg, wv, bg, bv, o_ref):
        o_ref[...] = _ffn_fn(hg[...], hv[...], wg[...], wv[...], bg[...], bv[...], _shift_down).astype(BF16)

    return _blocked(body, name=name, grid=(nb,), in_specs=in_specs, out_specs=col(t, 0),
                          out_shape=jax.ShapeDtypeStruct((t, D_FF), BF16),
                          compiler_params=_cparams(("parallel",), VMEM_MID))(h, h, cw, cw, cb, cb)


def _ffn_mid_bwd(h, cw, cb, dact, name):
    t = h.shape[0]
    nb, in_specs, col = _ffn_specs(t)

    def body(hg, hv, wg, wv, bg, bv, d_ref, dhg, dhv, dwg, dwv, dbg, dbv):
        fn = functools.partial(_ffn_fn, sd=_make_sd())
        _, vjp = jax.vjp(fn, hg[...], hv[...], wg[...], wv[...], bg[...], bv[...])
        g = vjp(d_ref[...])
        dhg[...] = g[0].astype(BF16)
        dhv[...] = g[1].astype(BF16)
        dwg[...], dwv[...], dbg[...], dbv[...] = g[2], g[3], g[4], g[5]

    big = jax.ShapeDtypeStruct((t, D_FF), BF16)
    w3 = jax.ShapeDtypeStruct((3, D_FF), F32)
    b1 = jax.ShapeDtypeStruct((1, D_FF), F32)
    return _blocked(body, name=name, grid=(nb,), in_specs=in_specs + [col(t, 0)],
                          out_specs=[col(t, 0), col(t, 0), col(3, 0), col(3, 0), col(1, 0), col(1, 0)],
                          out_shape=[big, big, w3, w3, b1, b1],
                          compiler_params=_cparams(("parallel",), VMEM_BIG))(h, h, cw, cw, cb, cb, dact)


TS_CW = 256


def _tshift_fn(p, mu, sd):
    return p + mu * (sd(p, 1) - p)


def _tshift_fwd(p, mu):
    t = p.shape[0]
    col = lambda r: pl.BlockSpec((r, TS_CW), lambda j: (0, j))

    def body(p_ref, mu_ref, o_ref):
        o_ref[...] = _tshift_fn(p_ref[...], mu_ref[...], _shift_down)

    return _blocked(body, name="tshift_fwd", grid=(SHIFT_COLS // TS_CW,), in_specs=[col(t), col(1)],
                          out_specs=col(t), out_shape=jax.ShapeDtypeStruct((t, SHIFT_COLS), F32),
                          compiler_params=_cparams(("parallel",), VMEM_MID))(p, mu)


def _tshift_bwd(p, mu, dpam):
    t = p.shape[0]
    col = lambda r: pl.BlockSpec((r, TS_CW), lambda j: (0, j))

    def body(p_ref, mu_ref, d_ref, dp_ref, dmu_ref):
        _, vjp = jax.vjp(functools.partial(_tshift_fn, sd=_make_sd()), p_ref[...], mu_ref[...])
        dp, dmu = vjp(d_ref[...])
        dp_ref[...] = dp.astype(BF16)
        dmu_ref[...] = dmu

    return _blocked(body, name="tshift_bwd", grid=(SHIFT_COLS // TS_CW,), in_specs=[col(t), col(1), col(t)],
                          out_specs=[col(t), col(1)],
                          out_shape=[jax.ShapeDtypeStruct((t, SHIFT_COLS), BF16),
                                     jax.ShapeDtypeStruct((1, SHIFT_COLS), F32)],
                          compiler_params=_cparams(("parallel",), VMEM_MID))(p, mu, dpam)


_HI = lax.Precision.HIGHEST
_O = (0, RW, 2 * RW, 3 * RW, 3 * RW + W_LORA, 3 * RW + W_LORA + A_LORA, SHIFT_COLS)


def _dot16(a, b, dims=(((1,), (0,)), ((), ()))):
    return lax.dot_general(a.astype(BF16), b.astype(BF16), dims, preferred_element_type=F32)


def _make_dot16():
    @jax.custom_vjp
    def dot(a, b):
        return _dot16(a, b)

    def fwd(a, b):
        return _dot16(a, b), (a, b)

    def bwd(res, g):
        a, b = res
        return _dot16(g, b, (((1,), (1,)), ((), ()))), _dot16(a, g, (((0,), (0,)), ((), ())))

    dot.defvjp(fwd, bwd)
    return dot


def _seg(x):
    first = lax.broadcasted_iota(jnp.int32, (x.shape[0], LANES), 1) < HEAD
    parts = []
    for p in range(x.shape[1] // LANES):
        xp = x[:, p * LANES:(p + 1) * LANES]
        s0 = jnp.sum(jnp.where(first, xp, 0.0), axis=-1, keepdims=True)
        s1 = jnp.sum(jnp.where(first, 0.0, xp), axis=-1, keepdims=True)
        parts.append(jnp.where(first, s0, s1))
    return jnp.concatenate(parts, axis=1)


def _prep_fn(r, k, v, wd, ad, gd, w0, w2, a0, a2, g2, k_k, k_a, dot):
    w_log = -jax.nn.softplus(-(w0 + dot(jnp.tanh(wd), w2))) - 0.5
    decay = jnp.exp(-jnp.exp(w_log))
    a = jax.nn.sigmoid(a0 + dot(ad, a2))
    g = dot(jax.nn.sigmoid(gd), g2)
    kk = k * k_k
    kk = kk / jnp.maximum(jnp.sqrt(_seg(kk * kk)), 1e-12)
    k2 = k * (1.0 + (a - 1.0) * k_a)
    return r, decay, k2, v, -kk, kk * a, g


_PREP_W = ("w0", "w2", "a0", "a2", "g2", "k_k", "k_a")


def _prep_wspecs(w):
    return [_full(w[n].shape) for n in _PREP_W]


def _rwkv_prep_fwd(pam, w):
    t = pam.shape[0]

    def body(p_ref, *refs):
        wr, outs = refs[:7], refs[7:]
        pieces = [p_ref[:, _O[i]:_O[i + 1]] for i in range(6)]
        res = _prep_fn(*pieces, *[x[...] for x in wr], _dot16)
        for o, val in zip(outs, res):
            o[...] = val

    row = lambda c: pl.BlockSpec((TOK, c), lambda i: (i, 0))
    return _blocked(body, name="rwkv_prep_fwd", grid=(t // TOK,),
                          in_specs=[row(SHIFT_COLS)] + _prep_wspecs(w), out_specs=[row(RW)] * 7,
                          out_shape=[jax.ShapeDtypeStruct((t, RW), F32)] * 7,
                          compiler_params=_cparams(("parallel",), VMEM_MID))(pam, *[w[n] for n in _PREP_W])


def _rwkv_prep_bwd(pam, w, cts, more):
    t = pam.shape[0]

    def body(p_ref, *refs):
        wr, ct, ex, dp_ref, dws = refs[:7], refs[7:14], refs[14:17], refs[17], refs[18:]
        pieces = [p_ref[:, _O[i]:_O[i + 1]] for i in range(6)]
        fn = lambda *a: _prep_fn(*a, _make_dot16())
        _, vjp = jax.vjp(fn, *pieces, *[x[...] for x in wr])
        c = [x[...] for x in ct]
        c[0] = c[0] + ex[0][...]
        c[2] = c[2] + ex[1][...]
        c[3] = c[3] + ex[2][...]
        g = vjp(tuple(c))
        for i in range(6):
            dp_ref[:, _O[i]:_O[i + 1]] = g[i]
        for o, val in zip(dws, g[6:]):
            _acc_out(o, val)

    row = lambda c: pl.BlockSpec((TOK, c), lambda i: (i, 0))
    return _blocked(body, name="rwkv_prep_bwd", grid=(t // TOK,),
                          in_specs=[row(SHIFT_COLS)] + _prep_wspecs(w) + [row(RW)] * 10,
                          out_specs=[row(SHIFT_COLS)] + [_full(w[n].shape) for n in _PREP_W],
                          out_shape=[jax.ShapeDtypeStruct((t, SHIFT_COLS), F32)]
                          + [jax.ShapeDtypeStruct(w[n].shape, F32) for n in _PREP_W],
                          compiler_params=_cparams(("arbitrary",), VMEM_MID))(
                              pam, *[w[n] for n in _PREP_W], *cts, *more)


def _post_fn(y, r, k2, v, g, ln_w, ln_b, r_k):
    inv = 1.0 / HEAD
    d = y - _seg(y) * inv
    yn = d * lax.rsqrt(_seg(d * d) * inv + GN_EPS) * ln_w + ln_b
    bonus = _seg(r * k2 * r_k) * v
    return (yn + bonus) * g


def _rwkv_post_fwd(y, r, k2, v, g, ln_w, ln_b, r_k):
    t = y.shape[0]

    def body(*refs):
        o_ref = refs[-1]
        o_ref[...] = _post_fn(*[x[...] for x in refs[:-1]]).astype(BF16)

    row = pl.BlockSpec((ROWS, RW), lambda i: (i, 0))
    return _blocked(body, name="rwkv_post_fwd", grid=(t // ROWS,),
                          in_specs=[row] * 5 + [_full((1, RW))] * 3, out_specs=row,
                          out_shape=jax.ShapeDtypeStruct((t, RW), BF16),
                          compiler_params=_cparams(("parallel",), VMEM_MID))(y, r, k2, v, g, ln_w, ln_b, r_k)


def _rwkv_post_bwd(y, r, k2, v, g, ln_w, ln_b, r_k, dya):
    t = y.shape[0]

    def body(*refs):
        ins, d_ref, outs = refs[:8], refs[8], refs[9:]
        _, vjp = jax.vjp(_post_fn, *[x[...] for x in ins])
        gr = vjp(d_ref[...])
        for o, val in zip(outs[:5], gr[:5]):
            o[...] = val
        for o, val in zip(outs[5:], gr[5:]):
            _acc_out(o, val)

    row = pl.BlockSpec((TOK, RW), lambda i: (i, 0))
    vec = _full((1, RW))
    return _blocked(body, name="rwkv_post_bwd", grid=(t // TOK,),
                          in_specs=[row] * 5 + [vec] * 3 + [row],
                          out_specs=[row] * 5 + [vec] * 3,
                          out_shape=[jax.ShapeDtypeStruct((t, RW), F32)] * 5 + [jax.ShapeDtypeStruct((1, RW), F32)] * 3,
                          compiler_params=_cparams(("arbitrary",), VMEM_MID))(y, r, k2, v, g, ln_w, ln_b, r_k, dya)


def _from_pt(x):
    n = x.shape[0]
    return x.reshape(n, HEAD, N_HEADS, PT).transpose(0, 3, 2, 1).reshape(n * PT, N_HEADS * HEAD)


def _lane_sum(x):
    return jnp.sum(x, axis=-1, keepdims=True)


def _pair_consts():
    lane = lax.broadcasted_iota(jnp.int32, (HEAD, LANES), 1)
    return lane, lane < HEAD


def _seg_sum_pair(x, first):
    return jnp.where(first, _lane_sum(jnp.where(first, x, 0.0)), _lane_sum(jnp.where(first, 0.0, x)))


def _to_pt(x):
    t = x.shape[0]
    return x.reshape(t // PT, PT, N_HEADS, HEAD).transpose(0, 3, 2, 1).reshape(t // PT, HEAD, N_HEADS * PT)


def _expand_cols(x, name):
    t = x.shape[0]
    chunk = 2 * WKV_CHUNK
    tiles = chunk // PT

    def body(x_ref, o_ref):
        _, first = _pair_consts()
        for tl in range(tiles):
            tile = x_ref[tl]
            for j in range(PT):
                for p in range(N_HEADS // 2):
                    src = jnp.where(first, (2 * p) * PT + j, (2 * p + 1) * PT + j)
                    o_ref[tl * PT + j, :, p * LANES:(p + 1) * LANES] = jnp.take_along_axis(tile, src, axis=1)

    return _blocked(
        body, name=name, grid=(t // chunk,),
        in_specs=[pl.BlockSpec((tiles, HEAD, LANES), lambda i: (i, 0, 0))],
        out_specs=pl.BlockSpec((chunk, HEAD, RW), lambda i: (i, 0, 0)),
        out_shape=jax.ShapeDtypeStruct((t, HEAD, RW), F32),
        compiler_params=_cparams(("parallel",), VMEM_MID))(_to_pt(x))


def _wkv_fwd(w, k, z, b, v_exp):
    t = w.shape[0]
    chunk = 2 * WKV_CHUNK
    nc = t // chunk
    pairs = N_HEADS // 2

    def body(w_ref, k_ref, z_ref, b_ref, v_ref, s_all, s_ref):
        @pl.when(pl.program_id(0) == 0)
        def _():
            s_ref[...] = jnp.zeros_like(s_ref)

        _, first = _pair_consts()

        def group(gi, carry):
            base = pl.multiple_of(gi * 8, 8)
            rows = [ref[pl.ds(base, 8), :] for ref in (w_ref, k_ref, z_ref, b_ref)]
            s = [s_ref[:, p * LANES:(p + 1) * LANES] for p in range(pairs)]
            for jj in range(8):
                for p in range(pairs):
                    cs = slice(p * LANES, (p + 1) * LANES)
                    wr, kr, zr, br = [x[jj:jj + 1, cs] for x in rows]
                    s_all[base + jj, :, cs] = s[p]
                    sa = _seg_sum_pair(s[p] * zr, first)
                    s[p] = s[p] * wr + sa * br + v_ref[base + jj, :, cs] * kr
            for p in range(pairs):
                s_ref[:, p * LANES:(p + 1) * LANES] = s[p]
            return carry

        lax.fori_loop(0, chunk // 8, group, 0)

    row = pl.BlockSpec((chunk, RW), lambda i: (i, 0))
    big = pl.BlockSpec((chunk, HEAD, RW), lambda i: (i, 0, 0))
    return _blocked(
        body, name="wkv_fwd", grid=(nc,), in_specs=[row] * 4 + [big], out_specs=[big, _full((HEAD, RW))],
        out_shape=[jax.ShapeDtypeStruct((t, HEAD, RW), F32), jax.ShapeDtypeStruct((HEAD, RW), F32)],
        compiler_params=_cparams(("arbitrary",), VMEM_BIG))(w, k, z, b, v_exp)


def _wkv_out(r, s_all, s_last):
    t = r.shape[0]
    nc = t // WKV_CHUNK
    tiles = WKV_CHUNK // PT
    pairs = N_HEADS // 2

    def body(r_ref, s_ref, nxt_ref, last_ref, y_ref):
        lane, first = _pair_consts()
        after = jnp.where(pl.program_id(0) == nc - 1, last_ref[...], nxt_ref[0])
        for tl in range(tiles):
            ytile = jnp.zeros((HEAD, LANES), F32)
            for g in range(PT // 8):
                rows = r_ref[tl * PT + g * 8:tl * PT + g * 8 + 8, :]
                for jj in range(8):
                    tt = tl * PT + g * 8 + jj
                    j = g * 8 + jj
                    for p in range(pairs):
                        cs = slice(p * LANES, (p + 1) * LANES)
                        s = s_ref[tt + 1, :, cs] if tt + 1 < WKV_CHUNK else after[:, cs]
                        pr = s * rows[jj:jj + 1, cs]
                        y0 = _lane_sum(jnp.where(first, pr, 0.0))
                        y1 = _lane_sum(jnp.where(first, 0.0, pr))
                        ytile = jnp.where(lane == (2 * p) * PT + j, y0, ytile)
                        ytile = jnp.where(lane == (2 * p + 1) * PT + j, y1, ytile)
            y_ref[tl] = ytile

    row = pl.BlockSpec((WKV_CHUNK, RW), lambda i: (i, 0))
    pt = pl.BlockSpec((tiles, HEAD, LANES), lambda i: (i, 0, 0))
    big = pl.BlockSpec((WKV_CHUNK, HEAD, RW), lambda i: (i, 0, 0))
    nxt = pl.BlockSpec((1, HEAD, RW), lambda i: (jnp.minimum((i + 1) * WKV_CHUNK, t - 1), 0, 0))
    return _blocked(
        body, name="wkv_out", grid=(nc,), in_specs=[row, big, nxt, _full((HEAD, RW))], out_specs=pt,
        out_shape=jax.ShapeDtypeStruct((t // PT, HEAD, LANES), F32),
        compiler_params=_cparams(("parallel",), VMEM_MID))(r, s_all, s_all, s_last)


def _wkv_bwd(r, w, k, z, b, v_exp, s_all, dy_exp):
    t = r.shape[0]
    nc = t // WKV_CHUNK
    tiles = WKV_CHUNK // PT
    pairs = N_HEADS // 2

    def body(r_ref, w_ref, k_ref, z_ref, b_ref, v_ref, s_all_ref, dy_ref,
             dr_ref, dw_ref, dk_ref, dz_ref, db_ref, dv_ref, ds_ref):
        @pl.when(pl.program_id(0) == 0)
        def _():
            ds_ref[...] = jnp.zeros_like(ds_ref)

        lane, first = _pair_consts()
        col_sum = lambda x: jnp.sum(x, axis=0, keepdims=True)
        row8 = lax.broadcasted_iota(jnp.int32, (8, LANES), 0)
        for tl in reversed(range(tiles)):
            def group(gg, dvtile):
                gi = PT // 8 - 1 - gg
                base = pl.multiple_of(tl * PT + gi * 8, 8)
                rows = [ref[pl.ds(base, 8), :] for ref in (r_ref, w_ref, k_ref, z_ref, b_ref)]
                outs = (dr_ref, dw_ref, dk_ref, dz_ref, db_ref)
                tiles8 = {(id(o), p): jnp.zeros((8, LANES), F32) for o in outs for p in range(pairs)}
                ds = [ds_ref[:, p * LANES:(p + 1) * LANES] for p in range(pairs)]
                for jj in reversed(range(8)):
                    j = gi * 8 + jj
                    for p in range(pairs):
                        cs = slice(p * LANES, (p + 1) * LANES)

                        def put(ref, val, p=p, jj=jj):
                            tiles8[(id(ref), p)] = jnp.where(row8 == jj, val, tiles8[(id(ref), p)])

                        rr, wr, kr, zr, br = [x[jj:jj + 1, cs] for x in rows]
                        sp = s_all_ref[base + jj, :, cs]
                        vc = v_ref[base + jj, :, cs]
                        dyc = dy_ref[base + jj, :, cs]
                        sa = _seg_sum_pair(sp * zr, first)
                        st = sp * wr + sa * br + vc * kr
                        d = ds[p] + dyc * rr
                        put(dr_ref, col_sum(st * dyc))
                        dvk = d * kr
                        dv0 = _lane_sum(jnp.where(first, dvk, 0.0))
                        dv1 = _lane_sum(jnp.where(first, 0.0, dvk))
                        dvtile = jnp.where(lane == (2 * p) * PT + j, dv0, dvtile)
                        dvtile = jnp.where(lane == (2 * p + 1) * PT + j, dv1, dvtile)
                        put(dk_ref, col_sum(d * vc))
                        put(dw_ref, col_sum(sp * d))
                        u = _seg_sum_pair(d * br, first)
                        put(dz_ref, col_sum(sp * u))
                        put(db_ref, col_sum(d * sa))
                        ds[p] = d * wr + u * zr
                for p in range(pairs):
                    ds_ref[:, p * LANES:(p + 1) * LANES] = ds[p]
                for o in outs:
                    for p in range(pairs):
                        o[pl.ds(base, 8), p * LANES:(p + 1) * LANES] = tiles8[(id(o), p)]
                return dvtile

            dv_ref[tl] = lax.fori_loop(0, PT // 8, group, jnp.zeros((HEAD, LANES), F32))

    rev = lambda i: nc - 1 - i
    row = pl.BlockSpec((WKV_CHUNK, RW), lambda i: (rev(i), 0))
    pt = pl.BlockSpec((tiles, HEAD, LANES), lambda i: (rev(i), 0, 0))
    big = pl.BlockSpec((WKV_CHUNK, HEAD, RW), lambda i: (rev(i), 0, 0))
    return _blocked(
        body, name="wkv_bwd", grid=(nc,), in_specs=[row] * 5 + [big, big, big], out_specs=[row] * 5 + [pt],
        out_shape=[jax.ShapeDtypeStruct((t, RW), F32)] * 5 + [jax.ShapeDtypeStruct((t // PT, HEAD, LANES), F32)],
        scratch_shapes=[pltpu.VMEM((HEAD, RW), F32)],
        compiler_params=_cparams(("arbitrary",), VMEM_BIG))(r, w, k, z, b, v_exp, s_all, dy_exp)


LRU_CW = 128
_BX0 = SHIFT_COLS // LRU_CW
_BG0 = (SHIFT_COLS + LRU_W) // LRU_CW


def _lru_fn(bx, bg, cw, cb, ga, ba, gx, bxb, lam, sd, scan, dot):
    xc = cw[0:1] * sd(bx, 3) + cw[1:2] * sd(bx, 2) + cw[2:3] * sd(bx, 1) + cw[3:4] * bx + cb
    gr = jax.nn.sigmoid(dot(xc, ga) + ba)
    gi = jax.nn.sigmoid(dot(xc, gx) + bxb)
    log_a = -LRU_C * gr * jax.nn.softplus(-lam)
    a = jnp.exp(log_a)
    mult = jnp.sqrt(-jnp.tanh(log_a) * (jnp.exp(2.0 * log_a) + 1.0))
    return scan(a, xc * gi * mult) * jax.nn.gelu(bg)


def _lru_specs(t):
    col = lambda r, off=0: pl.BlockSpec((r, LRU_CW), lambda j: (0, j + off))
    diag = pl.BlockSpec((LRU_CW, LRU_CW), lambda j: (j, j))
    return col, [col(t, _BX0), col(t, _BG0), col(4), col(1), diag, col(1), diag, col(1), col(1)]


def _lru_fwd(p, cw, cb, ga, ba, gx, bxb, lam):
    t = p.shape[0]
    col, in_specs = _lru_specs(t)

    def body(*refs):
        o_ref = refs[-1]
        o_ref[...] = _lru_fn(*[x[...] for x in refs[:-1]], _shift_down, _lin_scan, _dot16).astype(BF16)

    return _blocked(body, name="lru_fwd", grid=(LRU_W // LRU_CW,), in_specs=in_specs, out_specs=col(t),
                          out_shape=jax.ShapeDtypeStruct((t, LRU_W), BF16),
                          compiler_params=_cparams(("parallel",), VMEM_MID))(p, p, cw, cb, ga, ba, gx, bxb, lam)


def _lru_bwd(p, cw, cb, ga, ba, gx, bxb, lam, dyb):
    t = p.shape[0]
    col, in_specs = _lru_specs(t)

    def body(*refs):
        ins, d_ref, outs = refs[:9], refs[9], refs[10:]
        fn = functools.partial(_lru_fn, sd=_make_sd(), scan=_make_scan(), dot=_make_dot16())
        _, vjp = jax.vjp(fn, *[x[...] for x in ins])
        g = vjp(d_ref[...])
        outs[0][...] = g[0].astype(BF16)
        outs[1][...] = g[1].astype(BF16)
        for o, val in zip(outs[2:], g[2:]):
            o[...] = val

    sq = pl.BlockSpec((LRU_CW, LRU_CW), lambda j: (j, 0))
    act = jax.ShapeDtypeStruct((t, LRU_W), BF16)
    vec = jax.ShapeDtypeStruct((1, LRU_W), F32)
    sqs = jax.ShapeDtypeStruct((LRU_W, LRU_CW), F32)
    return _blocked(body, name="lru_bwd", grid=(LRU_W // LRU_CW,), in_specs=in_specs + [col(t, RW // LRU_CW)],
                          out_specs=[col(t), col(t), col(4), col(1), sq, col(1), sq, col(1), col(1)],
                          out_shape=[act, act, jax.ShapeDtypeStruct((4, LRU_W), F32), vec, sqs, vec, sqs, vec, vec],
                          compiler_params=_cparams(("parallel",), VMEM_BIG))(p, p, cw, cb, ga, ba, gx, bxb, lam, dyb)


def _s5_disc_fn(a_re, a_im, log_dt, b_re, b_im, e):
    lam_re = jnp.minimum(a_re, -1e-4)
    lam_im = a_im
    dt = jnp.exp(log_dt)
    mag = jnp.exp(lam_re * dt)
    ab_re = mag * jnp.cos(lam_im * dt)
    ab_im = mag * jnp.sin(lam_im * dt)
    den = lam_re * lam_re + lam_im * lam_im
    zr = ab_re - 1.0
    q_re = jnp.dot((zr * lam_re + ab_im * lam_im) / den, e, precision=_HI)
    q_im = jnp.dot((ab_im * lam_re - zr * lam_im) / den, e, precision=_HI)
    return ab_re, ab_im, q_re * b_re - q_im * b_im, q_re * b_im + q_im * b_re


def _s5_disc_fwd(a_re, a_im, log_dt, b_re, b_im, e):
    def body(*refs):
        res = _s5_disc_fn(*[x[...] for x in refs[:6]])
        for o, val in zip(refs[6:], res):
            o[...] = val

    small = jax.ShapeDtypeStruct(a_re.shape, F32)
    wide = jax.ShapeDtypeStruct(b_re.shape, F32)
    return pl.pallas_call(body, name="s5_disc_fwd", out_shape=[small, small, wide, wide])(
        a_re, a_im, log_dt, b_re, b_im, e)


def _s5_disc_bwd(a_re, a_im, log_dt, b_re, b_im, e, cts):
    def body(*refs):
        ins, e_ref, ct, outs = refs[:5], refs[5], refs[6:10], refs[10:]
        _, vjp = jax.vjp(lambda *a: _s5_disc_fn(*a, e_ref[...]), *[x[...] for x in ins])
        for o, val in zip(outs, vjp(tuple(c[...] for c in ct))):
            o[...] = val

    shapes = [jax.ShapeDtypeStruct(x.shape, F32) for x in (a_re, a_im, log_dt, b_re, b_im)]
    return pl.pallas_call(body, name="s5_disc_bwd", out_shape=shapes)(a_re, a_im, log_dt, b_re, b_im, e, *cts)


def _cmul(a, b):
    return a[0] * b[0] - a[1] * b[1], a[0] * b[1] + a[1] * b[0]


def _s5_scan(sr, si, ab, reverse):
    n_tiles = sr.shape[0] // 8
    width = sr.shape[1]
    row8 = lax.broadcasted_iota(jnp.int32, (8, width), 0)
    p1 = ab
    p2 = _cmul(p1, p1)
    p4 = _cmul(p2, p2)
    pw = [p1]
    for _ in range(7):
        pw.append(_cmul(pw[-1], p1))
    cr = jnp.zeros((8, width), F32)
    ci = jnp.zeros((8, width), F32)
    for j in range(8):
        e = pw[7 - j] if reverse else pw[j]
        cr = jnp.where(row8 == j, e[0], cr)
        ci = jnp.where(row8 == j, e[1], ci)

    levels = []
    for d, q in ((1, p1), (2, p2), (4, p4)):
        keep = row8 < 8 - d if reverse else row8 >= d
        levels.append((d, (jnp.where(keep, q[0], 0.0), jnp.where(keep, q[1], 0.0))))

    def tile(i, carry):
        idx = n_tiles - 1 - i if reverse else i
        base = pl.multiple_of(idx * 8, 8)
        x = (sr[pl.ds(base, 8), :], si[pl.ds(base, 8), :])
        for d, q in levels:
            amt = 8 - d if reverse else d
            m = _cmul(q, (pltpu.roll(x[0], amt, 0), pltpu.roll(x[1], amt, 0)))
            x = (x[0] + m[0], x[1] + m[1])
        m = _cmul((cr, ci), carry)
        x = (x[0] + m[0], x[1] + m[1])
        sr[pl.ds(base, 8), :] = x[0]
        si[pl.ds(base, 8), :] = x[1]
        edge = slice(0, 1) if reverse else slice(7, 8)
        return x[0][edge], x[1][edge]

    zero = jnp.zeros((1, width), F32)
    lax.fori_loop(0, n_tiles, tile, (zero, zero))


_S5_W = S5_SLAB // S5_GROUP * S5_STATE


def _s5_specs(t):
    col = lambda r: pl.BlockSpec((r, S5_SLAB), lambda j: (0, j))
    bb = pl.BlockSpec((None, S5_SLAB, _S5_W), lambda j: (j, 0, 0))
    cd = pl.BlockSpec((None, _S5_W, S5_SLAB), lambda j: (j, 0, 0))
    ab = pl.BlockSpec((None, 1, _S5_W), lambda j: (j, 0, 0))
    return col, bb, cd, ab


def _s5_fwd(u, dvec, bbr, bbi, cdr, cdi, abr, abi):
    t, width = u.shape
    col, bb, cd, ab = _s5_specs(t)

    def body(u_ref, d_ref, bbr_ref, bbi_ref, cdr_ref, cdi_ref, abr_ref, abi_ref, o_ref, sr, si):
        uv = u_ref[...]
        sr[...] = _dot16(uv, bbr_ref[...])
        si[...] = _dot16(uv, bbi_ref[...])
        _s5_scan(sr, si, (abr_ref[...], abi_ref[...]), False)
        y = _dot16(sr[...], cdr_ref[...]) - _dot16(si[...], cdi_ref[...])
        o_ref[...] = jax.nn.gelu(y + d_ref[...] * uv).astype(BF16)

    return _blocked(body, name="s5_fwd", grid=(width // S5_SLAB,),
                          in_specs=[col(t), col(1), bb, bb, cd, cd, ab, ab], out_specs=col(t),
                          out_shape=jax.ShapeDtypeStruct((t, width), BF16),
                          scratch_shapes=[pltpu.VMEM((t, _S5_W), F32)] * 2,
                          compiler_params=_cparams(("parallel",), VMEM_BIG))(u, dvec, bbr, bbi, cdr, cdi, abr, abi)


def _s5_bwd(u, dvec, bbr, bbi, cdr, cdi, abr, abi, dyact):
    t, width = u.shape
    col, bb, cd, ab = _s5_specs(t)
    ns = width // S5_SLAB
    tn = (((0,), (0,)), ((), ()))
    nt = (((1,), (1,)), ((), ()))

    def body(u_ref, d_ref, bbr_ref, bbi_ref, cdr_ref, cdi_ref, abr_ref, abi_ref, dy_ref,
             du_ref, dd_ref, dbbr_ref, dbbi_ref, dcdr_ref, dcdi_ref, dabr_ref, dabi_ref, sr, si, gr, gi):
        uv = u_ref[...]
        dv = d_ref[...]
        abv = (abr_ref[...], abi_ref[...])
        sr[...] = _dot16(uv, bbr_ref[...])
        si[...] = _dot16(uv, bbi_ref[...])
        _s5_scan(sr, si, abv, False)
        y = _dot16(sr[...], cdr_ref[...]) - _dot16(si[...], cdi_ref[...])
        _, vjp = jax.vjp(jax.nn.gelu, y + dv * uv)
        (dpre,) = vjp(dy_ref[...].astype(F32))
        dd_ref[...] = jnp.sum(dpre * uv, axis=0, keepdims=True)
        dcdr_ref[...] = _dot16(sr[...], dpre, tn)
        dcdi_ref[...] = -_dot16(si[...], dpre, tn)
        gr[...] = _dot16(dpre, cdr_ref[...], nt)
        gi[...] = -_dot16(dpre, cdi_ref[...], nt)
        _s5_scan(gr, gi, (abv[0], -abv[1]), True)

        row8 = lax.broadcasted_iota(jnp.int32, (8, _S5_W), 0)

        def tile(i, carry):
            acc_r, acc_i, last_r, last_i = carry
            base = pl.multiple_of(i * 8, 8)
            s_r, s_i = sr[pl.ds(base, 8), :], si[pl.ds(base, 8), :]
            g_r, g_i = gr[pl.ds(base, 8), :], gi[pl.ds(base, 8), :]
            p_r = jnp.where(row8 == 0, last_r, pltpu.roll(s_r, 1, 0))
            p_i = jnp.where(row8 == 0, last_i, pltpu.roll(s_i, 1, 0))
            acc_r = acc_r + jnp.sum(g_r * p_r + g_i * p_i, axis=0, keepdims=True)
            acc_i = acc_i + jnp.sum(g_i * p_r - g_r * p_i, axis=0, keepdims=True)
            return acc_r, acc_i, s_r[7:8], s_i[7:8]

        zero = jnp.zeros((1, _S5_W), F32)
        acc_r, acc_i, _, _ = lax.fori_loop(0, t // 8, tile, (zero, zero, zero, zero))
        dabr_ref[...] = acc_r
        dabi_ref[...] = acc_i
        du_ref[...] = dpre * dv + _dot16(gr[...], bbr_ref[...], nt) + _dot16(gi[...], bbi_ref[...], nt)
        dbbr_ref[...] = _dot16(uv, gr[...], tn)
        dbbi_ref[...] = _dot16(uv, gi[...], tn)

    sds = jax.ShapeDtypeStruct
    return _blocked(
        body, name="s5_bwd", grid=(ns,), in_specs=[col(t), col(1), bb, bb, cd, cd, ab, ab, col(t)],
        out_specs=[col(t), col(1), bb, bb, cd, cd, ab, ab],
        out_shape=[sds((t, width), F32), sds((1, width), F32), sds((ns, S5_SLAB, _S5_W), F32),
                   sds((ns, S5_SLAB, _S5_W), F32), sds((ns, _S5_W, S5_SLAB), F32), sds((ns, _S5_W, S5_SLAB), F32),
                   sds((ns, 1, _S5_W), F32), sds((ns, 1, _S5_W), F32)],
        scratch_shapes=[pltpu.VMEM((t, _S5_W), F32)] * 4,
        compiler_params=_cparams(("parallel",), VMEM_BIG))(u, dvec, bbr, bbi, cdr, cdi, abr, abi, dyact)


def _gate_dense(w):
    h = w.shape[0]
    return jnp.einsum("hij,hg->higj", w, jnp.eye(h, dtype=F32)).reshape(h * HEAD, h * HEAD)


def _gate_blocks(d):
    x = d.reshape(LRU_W // LRU_CW, 2, HEAD, 2, HEAD)
    return jnp.einsum("tgihj,gh->tgij", x, jnp.eye(2, dtype=F32)).reshape(LRU_W // HEAD, HEAD, HEAD)


_GPS = S5_SLAB // S5_GROUP
_NS = S5_GROUPS // _GPS


def _s5_in_dense(bb):
    x = bb.reshape(_NS, _GPS, S5_STATE, S5_GROUP)
    return jnp.einsum("sgnc,gh->sgchn", x, jnp.eye(_GPS, dtype=F32)).reshape(_NS, S5_SLAB, _S5_W)


def _s5_in_blocks(d):
    x = d.reshape(_NS, _GPS, S5_GROUP, _GPS, S5_STATE)
    return jnp.einsum("sgchn,gh->sgnc", x, jnp.eye(_GPS, dtype=F32)).reshape(S5_GROUPS, S5_STATE * S5_GROUP)


def _s5_out_dense(c):
    x = c.reshape(_NS, _GPS, S5_GROUP, S5_STATE)
    return jnp.einsum("sgcn,gh->shngc", x, jnp.eye(_GPS, dtype=F32)).reshape(_NS, _S5_W, S5_SLAB)


def _s5_out_blocks(d):
    x = d.reshape(_NS, _GPS, S5_STATE, _GPS, S5_GROUP)
    return jnp.einsum("shngc,gh->sgcn", x, jnp.eye(_GPS, dtype=F32)).reshape(S5_GROUPS, S5_GROUP, S5_STATE)


def _local_step(x, tgt, w, late_weights, send_grads):
    d_model = x.shape[1]
    gs = {}
    n_layers = w["f_norm_g"].shape[0]

    def ffn_fwd(xin, l):
        xn = _rms_fwd(xin, w["f_norm_g"][l:l + 1], f"rms_f{l}")
        h = _matmul(xn, w["f_w_up_t"][l], "nt", f"mm_f{l}_up")
        act = _ffn_mid_fwd(h, w["f_conv_w"][l], w["f_conv_b"][l:l + 1], f"ffn_mid_fwd{l}")
        return _matmul(act, w["f_w_down"][l], "nn", f"mm_f{l}_down", add=xin), (xin, xn, h, act)

    def ffn_bwd(g, saved, l):
        xin, xn, h, act = saved
        dact = _matmul(g, w["f_w_down"][l], "nt", f"mm_f{l}_dact")
        d_down = _matmul(act, g, "tn", f"mm_f{l}_ddown", out_dtype=BF16)
        dhg, dhv, dwg, dwv, dbg, dbv = _ffn_mid_bwd(h, w["f_conv_w"][l], w["f_conv_b"][l:l + 1], dact,
                                                    f"ffn_mid_bwd{l}")
        dxn = _matmul((dhg, dhv), w["f_w_up_t"][l], "nn", f"mm_f{l}_dxn")
        d_up = _matmul((dhg, dhv), xn, "tn", f"mm_f{l}_dup", out_dtype=BF16)
        dx, dgn = _rms_bwd(xin, w["f_norm_g"][l:l + 1], dxn, g, f"rms_f{l}_bwd")
        return dx, d_up, d_down, jnp.concatenate([dwg, dwv], axis=1), jnp.concatenate([dbg, dbv], axis=1), dgn

    xn0 = _rms_fwd(x, w["e_norm_g"], "rms_e")
    p = _matmul(xn0, w["e_w_in_t"], "nt", "mm_e_in")
    pam = _tshift_fwd(p, w["e_mu"])
    pw = dict(w0=w["e_w0"], w2=w["e_w2"][0], a0=w["e_a0"], a2=w["e_a2"][0], g2=w["e_g2"][0],
              k_k=w["e_k_k"], k_a=w["e_k_a"])
    r, dec, k2, v, z, b, gate = _rwkv_prep_fwd(pam, pw)
    v_exp = _expand_cols(v, "wkv_expand_v")
    s_all, s_last = _wkv_fwd(dec, k2, z, b, v_exp)
    y_pt = _wkv_out(r, s_all, s_last)
    y = _from_pt(y_pt)
    rk = w["e_r_k"].reshape(1, RW)
    ya = _rwkv_post_fwd(y, r, k2, v, gate, w["e_ln_w"], w["e_ln_b"], rk)
    ga, gx = _gate_dense(w["e_gate_a_w"][0]), _gate_dense(w["e_gate_x_w"][0])
    lru_w = (w["e_conv_w"][0], w["e_conv_b"], ga, w["e_gate_a_b"], gx, w["e_gate_x_b"], w["e_lru_lambda"])
    yb = _lru_fwd(p, *lru_w)
    ycat = jnp.concatenate([ya, yb], axis=1)
    w = {**w, **late_weights(ycat)}
    x1 = _matmul(ycat, w["e_w_out"], "nn", "mm_e_out", add=x)
    x2, ffn0 = ffn_fwd(x1, 0)

    xn1 = _rms_fwd(x2, w["o_norm_g"], "rms_o")
    u = _matmul(xn1, w["o_w_in"], "nn", "mm_o_in")
    expand = jnp.kron(jnp.eye(S5_STATE, dtype=F32), jnp.ones((1, S5_GROUP), F32))
    disc_in = (w["o_A_re"][0], w["o_A_im"][0], w["o_log_dt"].reshape(S5_GROUPS, 1),
               w["o_B_re"][0].reshape(S5_GROUPS, -1), w["o_B_im"][0].reshape(S5_GROUPS, -1), expand)
    ab_re, ab_im, bb_re, bb_im = _s5_disc_fwd(*disc_in)
    s5_w = (w["o_D"], _s5_in_dense(bb_re), _s5_in_dense(bb_im), _s5_out_dense(w["o_C_re"][0]),
            _s5_out_dense(w["o_C_im"][0]), ab_re.reshape(_NS, 1, _S5_W), ab_im.reshape(_NS, 1, _S5_W))
    yact = _s5_fwd(u, *s5_w)
    zz = _matmul(yact, w["o_w_glu_t"], "nt", "mm_o_glu")
    x3 = _glu_fwd(x2, zz)
    x4, ffn1 = ffn_fwd(x3, 1)

    loss, g, gs["final_norm_g", 0] = _loss_head(x4, w["final_norm_g"].reshape(1, d_model), tgt)

    g, up1, down1, dcw1, dcb1, dfn1 = ffn_bwd(g, ffn1, 1)
    dz = _glu_bwd(zz, g)
    dyact = _matmul(dz, w["o_w_glu_t"], "nn", "mm_o_dyact")
    d_glu = _matmul(dz, yact, "tn", "mm_o_dglu", out_dtype=BF16)
    du, gs["o_D", 0], dbbr, dbbi, dcdr, dcdi, dabr, dabi = _s5_bwd(u, *s5_w, dyact)
    gs["o_C_re", 0] = _s5_out_blocks(dcdr).reshape(S5_GROUPS * S5_GROUP, S5_STATE)
    gs["o_C_im", 0] = _s5_out_blocks(dcdi).reshape(S5_GROUPS * S5_GROUP, S5_STATE)
    cts = (dabr.reshape(S5_GROUPS, S5_STATE), dabi.reshape(S5_GROUPS, S5_STATE), _s5_in_blocks(dbbr),
           _s5_in_blocks(dbbi))
    gs["o_A_re", 0], gs["o_A_im", 0], dlog_dt, gs["o_B_re", 0], gs["o_B_im", 0] = _s5_disc_bwd(*disc_in, cts)
    gs["o_log_dt", 0] = dlog_dt.reshape(1, S5_GROUPS)
    dxn = _matmul(du, w["o_w_in"], "nt", "mm_o_dxn")
    d_oin = _matmul(xn1, du, "tn", "mm_o_din", out_dtype=BF16)
    g, gs["o_norm_g", 0] = _rms_bwd(x2, w["o_norm_g"], dxn, g, "rms_o_bwd")
    g = send_grads("a", [("f_w_up", 1, up1), ("f_w_down", 1, down1), ("o_w_glu", 0, d_glu), ("o_w_in", 0, d_oin)], g)

    g, up0, down0, dcw0, dcb0, dfn0 = ffn_bwd(g, ffn0, 0)
    gs["f_conv_w", 0], gs["f_conv_w", 3] = dcw0, dcw1
    gs["f_conv_b", 0], gs["f_conv_b", 1] = dcb0, dcb1
    gs["f_norm_g", 0], gs["f_norm_g", 1] = dfn0, dfn1

    dycat = _matmul(g, w["e_w_out"], "nt", "mm_e_dycat")
    d_eout = _matmul(ycat, g, "tn", "mm_e_dout", out_dtype=BF16)
    dycat = send_grads("b", [("f_w_up", 0, up0), ("f_w_down", 0, down0), ("e_w_out", 0, d_eout)], dycat)
    dy, dr1, dk1, dv1, dgate, gs["e_ln_w", 0], gs["e_ln_b", 0], gs["e_r_k", 0] = _rwkv_post_bwd(
        y, r, k2, v, gate, w["e_ln_w"], w["e_ln_b"], rk, dycat)
    dr2, ddec, dk2, dzz, dbb, dv_pt = _wkv_bwd(r, dec, k2, z, b, v_exp, s_all, _expand_cols(dy, "wkv_expand_dy"))
    (dpam, gs["e_w0", 0], gs["e_w2", 0], gs["e_a0", 0], gs["e_a2", 0], gs["e_g2", 0], gs["e_k_k", 0],
     gs["e_k_a", 0]) = _rwkv_prep_bwd(pam, pw, (dr2, ddec, dk2, _from_pt(dv_pt), dzz, dbb, dgate), (dr1, dk1, dv1))
    dpa, gs["e_mu", 0] = _tshift_bwd(p, w["e_mu"], dpam)
    (dbx, dbg, gs["e_conv_w", 0], gs["e_conv_b", 0], dga, gs["e_gate_a_b", 0], dgx, gs["e_gate_x_b", 0],
     gs["e_lru_lambda", 0]) = _lru_bwd(p, *lru_w, dycat)
    gs["e_gate_a_w", 0] = _gate_blocks(dga).reshape(LRU_W, HEAD)
    gs["e_gate_x_w", 0] = _gate_blocks(dgx).reshape(LRU_W, HEAD)
    dp = jnp.concatenate([dpa, dbx, dbg], axis=1)
    d_ein = _matmul(dp, xn0, "tn", "mm_e_din", out_dtype=BF16)
    dp = send_grads("c", [("e_w_in", 0, d_ein)], dp)
    dxn = _matmul(dp, w["e_w_in_t"], "nn", "mm_e_dxn")
    grad_x, gs["e_norm_g", 0] = _rms_bwd(x, w["e_norm_g"], dxn, g, "rms_e_bwd")
    return loss, grad_x, gs


CAST_ROWS = 256


def _cast_shard(w3, layer, transpose, chip, name, after=None):
    _, rows, cols = w3.shape
    tr = _tile(rows, (CAST_ROWS, 176, 128))

    def body(c_ref, w_ref, *rest):
        v = w_ref[...]
        rest[-1][...] = (v.T if transpose else v).astype(BF16)

    in_spec = pl.BlockSpec((None, tr, cols), lambda i, c: (layer, i, 0))
    if transpose:
        out_spec, shape = pl.BlockSpec((None, cols, tr), lambda i, c: (c[0], 0, i)), (cols, rows)
    else:
        out_spec, shape = pl.BlockSpec((None, tr, cols), lambda i, c: (c[0], i, 0)), (rows, cols)
    extra = [] if after is None else [after]
    grid_spec = pltpu.PrefetchScalarGridSpec(num_scalar_prefetch=1, grid=(rows // tr,),
                                             in_specs=[in_spec] + [_ANY] * len(extra), out_specs=out_spec)
    return _blocked(body, name=name, grid_spec=grid_spec,
                          out_shape=jax.ShapeDtypeStruct((N_CHIPS,) + shape, BF16),
                          compiler_params=_cparams(("parallel",), VMEM_MID))(chip, w3, *extra)


_ANY = pl.BlockSpec(memory_space=pl.ANY)


def _coords():
    return lax.axis_index("x"), lax.axis_index("y"), lax.axis_index("c")


def _flip(v, d):
    return 1 - v if d else v


_CHIP_RELS = ((1, 0), (0, 1), (1, 1))
_DEV_RELS = tuple((dx, dy, dc) for dx in (0, 1) for dy in (0, 1) for dc in (0, 1))[1:]


_HBM = pl.BlockSpec(memory_space=pltpu.HBM)
_SEM = pl.BlockSpec(memory_space=pltpu.SEMAPHORE)
_EFFECT = pltpu.SideEffectType.DATAFLOW_SIDE_EFFECTING


def _in_hbm(a):
    return pltpu.with_memory_space_constraint(a, pltpu.HBM)


def _gather_copies(bufs, send, recv, landed, halved=False):
    x, y, c = _coords()
    me = 2 * x + y
    res = []
    for i, buf in enumerate(bufs):
        half = buf.shape[1] // 2
        part = (lambda slot: buf.at[slot, pl.ds(c * half, half)]) if halved else (lambda slot: buf.at[slot])
        for j, (dx, dy) in enumerate(_CHIP_RELS):
            px, py = _flip(x, dx), _flip(y, dy)
            k = i * len(_CHIP_RELS) + j
            res.append(pltpu.make_async_remote_copy(
                src_ref=part(me), dst_ref=part(2 * px + py if landed else me), send_sem=send.at[k],
                recv_sem=recv.at[k], device_id=(px, py, c), device_id_type=MESH))
    return res


def _swap_fetched(bufs):
    n = len(bufs)
    nr = len(_CHIP_RELS)

    def body(*refs):
        outs, (send, recv) = refs[n:2 * n], refs[2 * n:]
        x, y, c = _coords()
        sib = (x, y, 1 - c)
        sends, recvs = [], []
        for i in range(n):
            half = outs[i].shape[1] // 2
            for j, (dx, dy) in enumerate(_CHIP_RELS):
                slot = 2 * _flip(x, dx) + _flip(y, dy)
                mine = outs[i].at[slot, pl.ds(c * half, half)]
                k = i * nr + j
                cp = pltpu.make_async_remote_copy(src_ref=mine, dst_ref=mine, send_sem=send.at[k], recv_sem=recv.at[k],
                                                  device_id=sib, device_id_type=MESH)
                cp.start()
                sends.append(cp)
                recvs.append(pltpu.make_async_remote_copy(
                    src_ref=mine, dst_ref=outs[i].at[slot, pl.ds((1 - c) * half, half)], send_sem=send.at[k],
                    recv_sem=recv.at[k], device_id=sib, device_id_type=MESH))
        for cp in recvs:
            cp.wait_recv()
        for cp in sends:
            cp.wait_send()

    return pl.pallas_call(
        body, name="swap_fetched", in_specs=[_ANY] * n, out_specs=[_ANY] * n,
        out_shape=[jax.ShapeDtypeStruct(a.shape, a.dtype) for a in bufs],
        input_output_aliases={i: i for i in range(n)},
        scratch_shapes=[pltpu.SemaphoreType.DMA((n * nr,)), pltpu.SemaphoreType.DMA((n * nr,))])(*bufs)


def _scatter_copies(srcs, lands, send, recv, landed):
    x, y, c = _coords()
    me = 4 * x + 2 * y + c
    res = []
    for i, (src, land) in enumerate(zip(srcs, lands)):
        for j, (dx, dy, dc) in enumerate(_DEV_RELS):
            peer = (_flip(x, dx), _flip(y, dy), _flip(c, dc))
            pid = 4 * peer[0] + 2 * peer[1] + peer[2]
            k = i * len(_DEV_RELS) + j
            res.append(pltpu.make_async_remote_copy(
                src_ref=src.at[pid], dst_ref=land.at[pid if landed else me], send_sem=send.at[k],
                recv_sem=recv.at[k], device_id=peer, device_id_type=MESH))
    return res


def _split_start(bufs, n_src, copies, n_rel, name, after):
    n = len(bufs)
    nk = n_src * n_rel

    def body(*refs):
        ins, send, recv, token = refs[:n], refs[n + 1 + n], refs[n + 2 + n], refs[-1]
        for cp in copies(ins, send, recv, False):
            cp.start()
        token[...] = jnp.zeros_like(token)

    res = pl.pallas_call(
        body, name=name, in_specs=[_HBM] * n + [_ANY],
        out_specs=[_HBM] * n + [_SEM, _SEM, pl.BlockSpec(memory_space=pltpu.VMEM)],
        out_shape=[pltpu.HBM(b.shape, b.dtype) for b in bufs]
        + [pltpu.SemaphoreType.DMA((nk,)), pltpu.SemaphoreType.DMA((nk,)), jax.ShapeDtypeStruct((8, LANES), F32)],
        input_output_aliases={i: i for i in range(n)},
        compiler_params=pltpu.CompilerParams(has_side_effects=_EFFECT))(*[_in_hbm(b) for b in bufs], after)
    return res[n], res[n + 1], list(res[:n]), res[n + 2]


def _split_wait(bufs, send, recv, copies, name, after):
    n = len(bufs)

    def body(*refs):
        ins, send_ref, recv_ref = refs[:n], refs[n], refs[n + 1]
        for cp in copies(ins, send_ref, recv_ref, True):
            cp.wait_send()
            cp.wait_recv()

    return pl.pallas_call(
        body, name=name, in_specs=[_HBM] * n + [_SEM, _SEM, _ANY], out_specs=[_HBM] * n,
        out_shape=[pltpu.HBM(b.shape, b.dtype) for b in bufs], input_output_aliases={i: i for i in range(n)},
        compiler_params=pltpu.CompilerParams(has_side_effects=_EFFECT))(*bufs, send, recv, after)


def _gather_start(bufs, name, after, halved=False):
    fn = functools.partial(_gather_copies, halved=halved)
    return _split_start(bufs, len(bufs), fn, len(_CHIP_RELS), name, after)


def _gather_wait(bufs, send, recv, name, after, halved=False):
    return _split_wait(bufs, send, recv, functools.partial(_gather_copies, halved=halved), name, after)


def _scatter_start(srcs, name, after):
    n = len(srcs)
    lands = [lax.empty(a.shape, a.dtype) for a in srcs]
    fn = lambda refs, send, recv, landed: _scatter_copies(refs[:n], refs[n:], send, recv, landed)
    send, recv, bufs, token = _split_start(list(srcs) + lands, n, fn, len(_DEV_RELS), name, after)
    return send, recv, bufs, token


def _scatter_wait(bufs, send, recv, name, after):
    n = len(bufs) // 2
    fn = lambda refs, s, r, landed: _scatter_copies(refs[:n], refs[n:], s, r, landed)
    res = _split_wait(bufs, send, recv, fn, name, after)
    return res[:n], res[n:]


def _sum_segments(src, land, me, name):
    nd, seg, cols = src.shape
    ts = _tile(seg, (256, 176, 128))

    def body(m_ref, *refs):
        o_ref = refs[-1]
        acc = refs[0][...].astype(F32)
        for r in refs[1:-1]:
            acc = acc + r[...].astype(F32)
        o_ref[...] = acc

    def peer(rel):
        bits = 4 * rel[0] + 2 * rel[1] + rel[2]
        return pl.BlockSpec((None, ts, cols), lambda i, m: (jnp.bitwise_xor(m[0], bits), i, 0))

    grid_spec = pltpu.PrefetchScalarGridSpec(
        num_scalar_prefetch=1, grid=(seg // ts,),
        in_specs=[pl.BlockSpec((None, ts, cols), lambda i, m: (m[0], i, 0))] + [peer(r) for r in _DEV_RELS],
        out_specs=pl.BlockSpec((None, ts, cols), lambda i, m: (m[1], i, 0)))
    return _blocked(body, name=name, grid_spec=grid_spec,
                          out_shape=jax.ShapeDtypeStruct((2, seg, cols), F32),
                          compiler_params=_cparams(("parallel",), VMEM_MID))(me, src, *[land] * len(_DEV_RELS))


def _exchange_sibling(arrs):
    n = len(arrs)

    def body(*refs):
        outs, (send, recv) = refs[n:2 * n], refs[2 * n:]
        x, y, c = _coords()
        sib = (x, y, 1 - c)
        sends, recvs = [], []
        for i in range(n):
            cp = pltpu.make_async_remote_copy(src_ref=outs[i].at[c], dst_ref=outs[i].at[c], send_sem=send.at[i],
                                              recv_sem=recv.at[i], device_id=sib, device_id_type=MESH)
            cp.start()
            sends.append(cp)
            recvs.append(pltpu.make_async_remote_copy(src_ref=outs[i].at[c], dst_ref=outs[i].at[1 - c],
                                                      send_sem=send.at[i], recv_sem=recv.at[i], device_id=sib,
                                                      device_id_type=MESH))
        for cp in recvs:
            cp.wait_recv()
        for cp in sends:
            cp.wait_send()

    return pl.pallas_call(
        body, name="exchange_sibling", in_specs=[_ANY] * n, out_specs=[_ANY] * n,
        out_shape=[jax.ShapeDtypeStruct(a.shape, a.dtype) for a in arrs],
        input_output_aliases={i: i for i in range(n)},
        scratch_shapes=[pltpu.SemaphoreType.DMA((n,)), pltpu.SemaphoreType.DMA((n,))])(*arrs)


def _allreduce_small(vec):
    _, nchips, seg, lanes = vec.shape
    nr = len(_CHIP_RELS)

    def body(in_ref, out_ref, from_sib, half, stage, red, send, recv):
        x, y, c = _coords()
        me = 2 * x + y
        sib = (x, y, 1 - c)
        chips = [(_flip(x, dx), _flip(y, dy)) for dx, dy in _CHIP_RELS]

        def copy(src, dst, k, peer):
            return pltpu.make_async_remote_copy(src_ref=src, dst_ref=dst, send_sem=send.at[k], recv_sem=recv.at[k],
                                                device_id=peer, device_id_type=MESH)

        to_sib = copy(in_ref.at[1 - c], from_sib, 0, sib)
        to_sib.start()
        to_sib.wait_recv()
        half[...] = in_ref[c] + from_sib[...]

        first = [copy(half.at[2 * px + py], stage.at[me], 1 + j, (px, py, c)) for j, (px, py) in enumerate(chips)]
        for cp in first:
            cp.start()
        stage[me] = half[me]
        for j, (px, py) in enumerate(chips):
            copy(half.at[2 * px + py], stage.at[2 * px + py], 1 + j, (px, py, c)).wait_recv()
        acc = stage[0]
        for k in range(1, nchips):
            acc = acc + stage[k]
        red[...] = acc
        out_ref[c, me] = acc

        second = [copy(red, out_ref.at[c, me], 1 + nr + j, (px, py, c)) for j, (px, py) in enumerate(chips)]
        for cp in second:
            cp.start()
        for j, (px, py) in enumerate(chips):
            copy(red, out_ref.at[c, 2 * px + py], 1 + nr + j, (px, py, c)).wait_recv()

        back = copy(out_ref.at[c], out_ref.at[c], 1 + 2 * nr, sib)
        back.start()
        copy(out_ref.at[c], out_ref.at[1 - c], 1 + 2 * nr, sib).wait_recv()
        for cp in [to_sib] + first + second + [back]:
            cp.wait_send()

    vm = pl.BlockSpec(memory_space=pltpu.VMEM)
    nsem = 2 + 2 * nr
    return pl.pallas_call(
        body, name="allreduce_small", in_specs=[vm], out_specs=vm,
        out_shape=jax.ShapeDtypeStruct(vec.shape, F32),
        scratch_shapes=[pltpu.VMEM((nchips, seg, lanes), F32), pltpu.VMEM((nchips, seg, lanes), F32),
                        pltpu.VMEM((nchips, seg, lanes), F32), pltpu.VMEM((seg, lanes), F32),
                        pltpu.SemaphoreType.DMA((nsem,)), pltpu.SemaphoreType.DMA((nsem,))],
        compiler_params=_cparams(None, VMEM_MID))(vec)


def _adam_math(w, g, m, v):
    m2 = ADAM_B1 * m + (1.0 - ADAM_B1) * g
    v2 = ADAM_B2 * v + (1.0 - ADAM_B2) * (g * g)
    m_hat = m2 / (1.0 - ADAM_B1 ** ADAM_STEP)
    v_hat = v2 / (1.0 - ADAM_B2 ** ADAM_STEP)
    return -ADAM_LR * (m_hat / (jnp.sqrt(v_hat) + ADAM_EPS) + ADAM_WD * w), m2, v2


def _adamw_big(w3, m3, v3, layer, g, transposed, name, prev=None):
    nl, rows, cols = w3.shape
    tr = 128 if transposed else _tile(rows, (256, 176, 128))

    def body(w_ref, m_ref, v_ref, g_ref, *rest):
        go_ref, d_ref, mo_ref, vo_ref = rest[-4:]
        g_val = g_ref[...].T if transposed else g_ref[...]
        go_ref[...] = g_val
        d_ref[...], mo_ref[...], vo_ref[...] = _adam_math(w_ref[...], g_val, m_ref[...], v_ref[...])

    wspec = pl.BlockSpec((None, tr, cols), lambda i: (layer, i, 0))
    gspec = pl.BlockSpec((cols, tr), lambda i: (0, i)) if transposed else pl.BlockSpec((tr, cols), lambda i: (i, 0))
    extra = [] if prev is None else list(prev)
    return _blocked(body, name=name, grid=(rows // tr,),
                          in_specs=[wspec, wspec, wspec, gspec] + [_ANY] * len(extra),
                          out_specs=[wspec] * 4, out_shape=[jax.ShapeDtypeStruct((nl, rows, cols), F32)] * 4,
                          input_output_aliases={4 + i: i for i in range(len(extra))},
                          compiler_params=_cparams(("parallel",), VMEM_MID))(w3, m3, v3, g, *extra)


_SMALL = (
    ("e_norm_g", (1, D_MODEL), None), ("e_mu", (1, SHIFT_COLS), None), ("e_w0", (1, RW), None),
    ("e_w2", (W_LORA, RW), RW // 4), ("e_a0", (1, RW), None), ("e_a2", (A_LORA, RW), RW // 4),
    ("e_g2", (G_LORA, RW), RW // 4), ("e_k_k", (1, RW), None), ("e_k_a", (1, RW), None), ("e_r_k", (1, RW), None),
    ("e_ln_w", (1, RW), None), ("e_ln_b", (1, RW), None), ("e_conv_w", (4, LRU_W), LRU_W // 4),
    ("e_conv_b", (1, LRU_W), None), ("e_gate_a_w", (LRU_W, HEAD), None), ("e_gate_a_b", (1, LRU_W), None),
    ("e_gate_x_w", (LRU_W, HEAD), None), ("e_gate_x_b", (1, LRU_W), None), ("e_lru_lambda", (1, LRU_W), None),
    ("o_norm_g", (1, D_MODEL), D_MODEL // 4), ("o_A_re", (S5_GROUPS, S5_STATE), None),
    ("o_A_im", (S5_GROUPS, S5_STATE), None), ("o_log_dt", (1, S5_GROUPS), None),
    ("o_B_re", (S5_GROUPS, S5_STATE * S5_GROUP), None), ("o_B_im", (S5_GROUPS, S5_STATE * S5_GROUP), None),
    ("o_C_re", (S5_GROUPS * S5_GROUP, S5_STATE), None), ("o_C_im", (S5_GROUPS * S5_GROUP, S5_STATE), None),
    ("o_D", (1, D_MODEL), D_MODEL // 4), ("f_norm_g", (2, D_MODEL), None),
    ("f_conv_w", (6, 2 * D_FF), 2 * D_FF // 4), ("f_conv_b", (2, 2 * D_FF), None),
    ("final_norm_g", (1, D_MODEL), None))
_PIECES = {"f_norm_g": ((0, 1), (1, 1)), "f_conv_b": ((0, 1), (1, 1)), "f_conv_w": ((0, 3), (3, 3))}


def _ceil_to(n, m):
    return -(-n // m) * m


def _small_layout():
    groups = {}
    for name, (rows, cols), _ in _SMALL:
        for first, r in _PIECES.get(name, ((0, rows),)):
            groups.setdefault(cols, []).append((name, first, r))
    layout, off = {}, 0
    for cols, items in groups.items():
        stacks = [0, 0] if 2 * cols <= LANES else [0]
        placed = []
        for name, first, r in sorted(items, key=lambda it: -it[2]):
            half = stacks.index(min(stacks))
            r0 = stacks[half]
            if r >= 8 or r0 % 8 + r > 8:
                r0 = _ceil_to(r0, 8)
            placed.append((name, first, r, r0, half * (LANES // 2)))
            stacks[half] = r0 + r
        rpad = _ceil_to(max(stacks), 8)
        for name, first, r, at, lane in placed:
            layout[name, first] = (off, rpad, at, r, cols, lane)
        off += -(-cols // LANES) * rpad
    return layout, _ceil_to(off, 8 * N_DEV)


def _small_pack(gs):
    layout, total = _small_layout()
    keys = list(layout)

    def body(*refs):
        out = refs[-1]
        out[...] = jnp.zeros_like(out)
        for key, g_ref in zip(keys, refs[:-1]):
            off, rpad, at, r, cols, lane = layout[key]
            for j in range(-(-cols // LANES)):
                cw = min(LANES, cols - j * LANES)
                out[off + j * rpad + at:off + j * rpad + at + r, lane:lane + cw] = g_ref[:, j * LANES:j * LANES + cw]

    return pl.pallas_call(body, name="small_pack", out_shape=jax.ShapeDtypeStruct((total, LANES), F32),
                          compiler_params=_cparams(None, VMEM_MID))(*[gs[k] for k in keys])


def _adamw_small(red, chip, wts, ms, vs):
    layout, _ = _small_layout()
    names = [n for n, _, _ in _SMALL]
    n = len(names)

    def body(chip_ref, red_ref, *refs):
        ins, outs = refs[:3 * n], refs[3 * n:]
        c = chip_ref[0]
        for i, (name, (rows, cols), loc) in enumerate(_SMALL):
            w_ref, m_ref, v_ref = ins[3 * i:3 * i + 3]
            o_refs = outs[4 * i:4 * i + 4]
            width = cols if loc is None else loc
            for first, r in _PIECES.get(name, ((0, rows),)):
                off, rpad, at, _, _, lane = layout[name, first]
                for j in range(-(-width // LANES)):
                    cw = min(LANES, width - j * LANES)
                    ls = slice(lane, lane + cw)
                    if loc is None:
                        start = off + j * rpad + at
                        g = red_ref[start:start + r, ls]
                    else:
                        blk = c * (loc // LANES) + j
                        if r >= 8:
                            g = red_ref[pl.ds(pl.multiple_of(off + at + blk * rpad, 8), r), ls]
                        else:
                            tile = red_ref[pl.ds(pl.multiple_of(off + at // 8 * 8 + blk * rpad, 8), 8), ls]
                            g = tile[at % 8:at % 8 + r]
                    rs, cs = slice(first, first + r), slice(j * LANES, j * LANES + cw)
                    d, m2, v2 = _adam_math(w_ref[rs, cs], g, m_ref[rs, cs], v_ref[rs, cs])
                    for o, val in zip(o_refs, (g, d, m2, v2)):
                        o[rs, cs] = val

    args, shapes = [], []
    for name in names:
        args += [wts[name], ms[name], vs[name]]
        shapes += [jax.ShapeDtypeStruct(wts[name].shape, F32)] * 4
    vm = pl.BlockSpec(memory_space=pltpu.VMEM)
    res = pl.pallas_call(body, name="adamw_small",
                         in_specs=[pl.BlockSpec(memory_space=pltpu.SMEM), vm] + [vm] * (3 * n),
                         out_specs=[vm] * (4 * n), out_shape=shapes,
                         compiler_params=_cparams(None, VMEM_BIG))(chip, red, *args)
    return {name: res[4 * i:4 * i + 4] for i, name in enumerate(names)}


PACK_ROWS = 8


def _packed_rows(shape):
    size = 1
    for d in shape:
        size *= d
    return -(-size // (PACK_ROWS * LANES)) * PACK_ROWS


def _pack(arrs, row_mult):
    parts = []
    for a in arrs:
        flat = a.reshape(-1).astype(F32)
        rows = _packed_rows(a.shape)
        parts.append(jnp.pad(flat, (0, rows * LANES - flat.shape[0])).reshape(rows, LANES))
    total = sum(p.shape[0] for p in parts)
    fill = -(-total // row_mult) * row_mult - total
    if fill:
        parts.append(jnp.zeros((fill, LANES), F32))
    return jnp.concatenate(parts, axis=0)


def _unpack(packed, shapes):
    out, off = [], 0
    for s in shapes:
        rows = _packed_rows(s)
        size = 1
        for d in s:
            size *= d
        out.append(packed[off:off + rows].reshape(-1)[:size].reshape(s))
        off += rows
    return out


_SMALL_SH = ("e_w2", "e_a2", "e_g2", "e_conv_w", "o_norm_g", "o_D", "f_conv_w")
_LARGE = (("e_w_in", True), ("e_w_out", False), ("o_w_in", False), ("o_w_glu", True), ("f_w_up", True),
        ("f_w_down", False))
_ORDER = ("e_norm_g", "e_w_in", "e_mu", "e_w0", "e_w2", "e_a0", "e_a2", "e_g2", "e_k_k", "e_k_a", "e_r_k", "e_ln_w",
          "e_ln_b", "e_conv_w", "e_conv_b", "e_gate_a_w", "e_gate_a_b", "e_gate_x_w", "e_gate_x_b", "e_lru_lambda",
          "e_w_out", "o_norm_g", "o_w_in", "o_A_re", "o_A_im", "o_log_dt", "o_B_re", "o_B_im", "o_C_re", "o_C_im",
          "o_D", "o_w_glu", "f_norm_g", "f_w_up", "f_conv_w", "f_conv_b", "f_w_down", "final_norm_g")
N_CHIPS = 4
N_DEV = 8


def _step(x, tgt, wts, ms, vs):
    xi, yi, ci = _coords()
    chip = 2 * xi + yi
    chip1 = chip.astype(jnp.int32).reshape(1)
    me2 = jnp.stack([4 * xi + 2 * yi + ci, ci]).astype(jnp.int32)
    by_cols = dict(_LARGE)

    cast = lambda name, l, after=None: _cast_shard(wts[name], l, by_cols[name], chip1, f"cast_{name}{l}", after)
    sh_shapes = [wts[n].shape for n in _SMALL_SH]
    packed = _pack([wts[n] for n in _SMALL_SH], 16)
    small_buf = lax.dynamic_update_slice(jnp.zeros((N_CHIPS,) + packed.shape, F32), packed[None], (chip, 0, 0))
    late = [(name, l) for name, _ in _LARGE if name != "e_w_in" for l in range(wts[name].shape[0])]
    send, recv, thru, token = _gather_start([cast("e_w_in", 0), small_buf], "gather_start_a", x, halved=True)
    bufs = {(name, l): cast(name, l, token) for name, l in late}
    got = _swap_fetched(_gather_wait(thru, send, recv, "gather_wait_a", bufs[late[-1]], halved=True))
    send_b, recv_b, thru_b, token = _gather_start([bufs[k] for k in late], "gather_start_b", got[0])
    x, _ = lax.optimization_barrier((x, token))

    def rows(g):
        return g.reshape(N_CHIPS * g.shape[1], g.shape[2])

    full = {n: wts[n] for n, _, loc in _SMALL if loc is None}
    full["e_w_in_t"] = rows(got[0])
    per_chip = [_unpack(got[1][k], sh_shapes) for k in range(N_CHIPS)]
    for i, n in enumerate(_SMALL_SH):
        full[n] = jnp.concatenate([per_chip[k][i] for k in range(N_CHIPS)], axis=-1)

    def late_weights(after):
        res = dict(zip(late, _gather_wait(thru_b, send_b, recv_b, "gather_wait_b", after)))
        return {"e_w_out": rows(res[("e_w_out", 0)]), "o_w_in": rows(res[("o_w_in", 0)]),
                "o_w_glu_t": rows(res[("o_w_glu", 0)]),
                "f_w_up_t": [rows(res[("f_w_up", l)]) for l in range(2)],
                "f_w_down": [rows(res[("f_w_down", l)]) for l in range(2)]}

    pending = []

    def send_grads(tag, items, carry):
        srcs = [g.reshape(N_DEV, g.shape[0] // N_DEV, g.shape[1]) for _, _, g in items]
        s_sem, r_sem, both, tok = _scatter_start(srcs, f"scatter_start_{tag}", carry)
        pending.append((tag, [(name, l) for name, l, _ in items], s_sem, r_sem, both))
        carry, _ = lax.optimization_barrier((carry, tok))
        return carry

    loss, grad_x, gs = _local_step(x, tgt, full, late_weights, send_grads)

    final = {}
    red = _allreduce_small(_small_pack(gs).reshape(2, N_CHIPS, -1, LANES)).reshape(-1, LANES)
    view = {name: (rows, cols if loc is None else loc) for name, (rows, cols), loc in _SMALL}
    as2d = lambda d: {name: d[name].reshape(view[name]) for name in view}
    small = _adamw_small(red, chip1, as2d(wts), as2d(ms), as2d(vs))
    for name, res in small.items():
        final[name] = [r.reshape(wts[name].shape) for r in res]
    new_v = small["final_norm_g"][3]

    halves, keys = [], []
    for tag, names, s_sem, r_sem, both in pending:
        srcs, lands = _scatter_wait(both, s_sem, r_sem, f"scatter_wait_{tag}", new_v)
        for (name, l), src, land in zip(names, srcs, lands):
            halves.append(_sum_segments(src, land, me2, f"sum_{name}{l}"))
            keys.append((name, l))
    shards = _exchange_sibling(halves)
    for s, (name, l) in zip(shards, keys):
        final[name] = _adamw_big(wts[name], ms[name], vs[name], l, s.reshape(2 * s.shape[1], s.shape[2]),
                                 by_cols[name], f"adamw_{name}{l}", prev=final.get(name))

    loss = lax.psum(loss[0, 0], ("x", "y", "c"))
    res = [loss, grad_x[None]]
    for k in range(4):
        res += [final[n][k] for n in _ORDER]
    return tuple(res)


def kernel(x, e_norm_g, e_w_in, e_mu, e_w0, e_w2, e_a0, e_a2, e_g2, e_k_k, e_k_a, e_r_k, e_ln_w, e_ln_b, e_conv_w, e_conv_b, e_gate_a_w, e_gate_a_b, e_gate_x_w, e_gate_x_b, e_lru_lambda, e_w_out, o_norm_g, o_w_in, o_A_re, o_A_im, o_log_dt, o_B_re, o_B_im, o_C_re, o_C_im, o_D, o_w_glu, f_norm_g, f_w_up, f_conv_w, f_conv_b, f_w_down, final_norm_g, loss_target, m_e_norm_g, m_e_w_in, m_e_mu, m_e_w0, m_e_w2, m_e_a0, m_e_a2, m_e_g2, m_e_k_k, m_e_k_a, m_e_r_k, m_e_ln_w, m_e_ln_b, m_e_conv_w, m_e_conv_b, m_e_gate_a_w, m_e_gate_a_b, m_e_gate_x_w, m_e_gate_x_b, m_e_lru_lambda, m_e_w_out, m_o_norm_g, m_o_w_in, m_o_A_re, m_o_A_im, m_o_log_dt, m_o_B_re, m_o_B_im, m_o_C_re, m_o_C_im, m_o_D, m_o_w_glu, m_f_norm_g, m_f_w_up, m_f_conv_w, m_f_conv_b, m_f_w_down, m_final_norm_g, v_e_norm_g, v_e_w_in, v_e_mu, v_e_w0, v_e_w2, v_e_a0, v_e_a2, v_e_g2, v_e_k_k, v_e_k_a, v_e_r_k, v_e_ln_w, v_e_ln_b, v_e_conv_w, v_e_conv_b, v_e_gate_a_w, v_e_gate_a_b, v_e_gate_x_w, v_e_gate_x_b, v_e_lru_lambda, v_e_w_out, v_o_norm_g, v_o_w_in, v_o_A_re, v_o_A_im, v_o_log_dt, v_o_B_re, v_o_B_im, v_o_C_re, v_o_C_im, v_o_D, v_o_w_glu, v_f_norm_g, v_f_w_up, v_f_conv_w, v_f_conv_b, v_f_w_down, v_final_norm_g):
    args = locals()
    wts = {n: args[n] for n in _ORDER}
    ms = {n: args["m_" + n] for n in _ORDER}
    vs = {n: args["v_" + n] for n in _ORDER}
    return _step(x[0], loss_target[0], wts, ms, vs)
```

```python
import functools

import jax
import jax.numpy as jnp
from jax import lax
from jax.experimental import pallas as pl
from jax.experimental.pallas import tpu as pltpu

F32 = jnp.float32
BF16 = jnp.bfloat16
MESH = pl.DeviceIdType.MESH

D_MODEL = 1024
HEAD = 64
RW = 512
N_HEADS = RW // HEAD
LRU_W = 512
SHIFT_COLS = 1792
W_LORA, A_LORA, G_LORA = 64, 64, 128
S5_GROUPS, S5_GROUP, S5_STATE = 64, 16, 64
D_FF = 2816
NORM_EPS = 1e-6
GN_EPS = 64e-5
LRU_C = 8.0
ADAM_LR, ADAM_B1, ADAM_B2, ADAM_EPS, ADAM_WD, ADAM_STEP = 0.001, 0.9, 0.999, 1e-08, 0.01, 10

VMEM_BIG = 56 * 1024 * 1024
VMEM_MID = 40 * 1024 * 1024
LANES = 128
PT = 16
WKV_CHUNK = 32
S5_SLAB = 128


def _blocked(*args, **kw):
    call = pl.pallas_call(*args, **kw)

    def run(*ops):
        return call(*[pltpu.with_memory_space_constraint(a, pltpu.HBM) if a.ndim >= 2 else a for a in ops])

    return run


def _cparams(sem=None, vmem=None):
    kw = {}
    if sem is not None:
        kw["dimension_semantics"] = sem
    if vmem is not None:
        kw["vmem_limit_bytes"] = vmem
    return pltpu.CompilerParams(**kw)


def _tile(dim, cands):
    for c in cands:
        if dim % c == 0:
            return c
    return dim


def _full(shape):
    n = len(shape)
    return pl.BlockSpec(shape, lambda *_: (0,) * n)


_TILES = (2816, 2048, 1408, 1024, 512, 256, 128)
MM_BUDGET = 36 * 1024 * 1024
VMEM_SLACK = 12 * 1024 * 1024


MXU_FLOPS = 9.0e14
HBM_BYTES = 3.3e12
STEP_SECONDS = 0.35e-6


def _mm_tiles(m, n, k, size_a, size_b, size_o, has_add, parts=1, tk_only=None, tm_max=None):
    best = None
    for tm in _TILES:
        for tk in _TILES:
            for tn in _TILES:
                if m % tm or n % tn or k % tk or (tk_only and tk != tk_only) or (tm_max and tm_max % tm):
                    continue
                need = (2 * (parts * tm * tk * size_a + tk * tn * size_b + tm * tn * size_o)
                        + tm * tn * 4 * (1 + 2 * has_add))
                if k > tk:
                    need += tm * tn * 4
                if need > MM_BUDGET:
                    continue
                steps = (m // tm) * (n // tn) * (k // tk)
                a_reads = n // tn if k > tk else 1
                moved = (m * k * size_a * a_reads + k * n * size_b * (m // tm) + m * n * (size_o + 4 * has_add))
                cost = max(2.0 * m * n * k / MXU_FLOPS, moved / HBM_BYTES) + steps * STEP_SECONDS
                cand = (-cost, tk, tm, tn)
                if best is None or cand > best[0]:
                    best = (cand, need)
    (_, tk, tm, tn), need = best
    return tm, tn, tk, need


def _matmul(a, b, mode, name, out_dtype=F32, add=None):
    parts = a if isinstance(a, tuple) else (a,)
    na = len(parts)
    wide = parts[0].shape[1]
    if mode == "nn":
        (m, k), (k2, n) = (parts[0].shape[0], na * wide), b.shape
    elif mode == "nt":
        (m, k), (n, k2) = (parts[0].shape[0], na * wide), b.shape
    else:
        (k, m), (k2, n) = (parts[0].shape[0], na * wide), b.shape
    assert k == k2, (parts[0].shape, b.shape, mode)
    split = {} if na == 1 else ({"tm_max": wide} if mode == "tn" else {"tk_only": wide})
    tm, tn, tk, need = _mm_tiles(m, n, k, parts[0].dtype.itemsize, b.dtype.itemsize, jnp.dtype(out_dtype).itemsize,
                                 add is not None, parts=na, **split)
    nk = k // tk
    per_part = wide // (tm if mode == "tn" else tk)
    dims = {"nn": (((1,), (0,)), ((), ())), "nt": (((1,), (1,)), ((), ())), "tn": (((0,), (0,)), ((), ()))}[mode]

    def body(*refs):
        a_refs, b_ref = refs[:na], refs[na]
        add_ref = refs[na + 1] if add is not None else None
        o_ref = refs[na + 2] if add is not None else refs[na + 1]
        kk = pl.program_id(2)

        def finish(r):
            if add_ref is not None:
                r = r + add_ref[...]
            o_ref[...] = r.astype(o_ref.dtype)

        def use(a_ref):
            part = lax.dot_general(a_ref[...].astype(BF16), b_ref[...].astype(BF16), dims, preferred_element_type=F32)
            if nk == 1:
                finish(part)
                return
            acc = refs[-1]

            @pl.when(kk == 0)
            def _():
                acc[...] = part

            @pl.when(kk > 0)
            def _():
                acc[...] += part

            @pl.when(kk == nk - 1)
            def _():
                finish(acc[...])

        if na == 1:
            use(a_refs[0])
        else:
            which = (pl.program_id(0) if mode == "tn" else kk) // per_part
            for p in range(na):
                pl.when(which == p)(functools.partial(use, a_refs[p]))

    def a_spec(p):
        def along(pos):
            return jnp.clip(pos - p * per_part, 0, per_part - 1) if na > 1 else pos
        if mode == "tn":
            return pl.BlockSpec((tk, tm), lambda i, j, kk: (kk, along(i)))
        return pl.BlockSpec((tm, tk), lambda i, j, kk: (i, along(kk)))

    if mode == "nn":
        b_spec = pl.BlockSpec((tk, tn), lambda i, j, kk: (kk, j))
    elif mode == "nt":
        b_spec = pl.BlockSpec((tn, tk), lambda i, j, kk: (j, kk))
    else:
        b_spec = pl.BlockSpec((tk, tn), lambda i, j, kk: (kk, j))
    o_spec = pl.BlockSpec((tm, tn), lambda i, j, kk: (i, j))
    in_specs = [a_spec(p) for p in range(na)] + [b_spec] + ([o_spec] if add is not None else [])
    args = parts + (b,) + ((add,) if add is not None else ())
    return _blocked(
        body, name=name, grid=(m // tm, n // tn, nk),
        in_specs=in_specs, out_specs=o_spec,
        out_shape=jax.ShapeDtypeStruct((m, n), out_dtype),
        scratch_shapes=[pltpu.VMEM((tm, tn), F32)] if nk > 1 else [],
        compiler_params=_cparams(("parallel", "parallel", "arbitrary"), min(VMEM_BIG, need + VMEM_SLACK)),
    )(*args)


TOK = 256
ROWS = 512


def _rms(x, g):
    return x * lax.rsqrt(jnp.mean(x * x, axis=-1, keepdims=True) + NORM_EPS) * g


def _rms_fwd(x, g, name):
    t, d = x.shape

    def body(x_ref, g_ref, o_ref):
        o_ref[...] = _rms(x_ref[...], g_ref[...]).astype(BF16)

    row = pl.BlockSpec((ROWS, d), lambda i: (i, 0))
    return _blocked(body, name=name, grid=(t // ROWS,), in_specs=[row, _full((1, d))], out_specs=row,
                          out_shape=jax.ShapeDtypeStruct((t, d), BF16),
                          compiler_params=_cparams(("parallel",), VMEM_MID))(x, g)


def _rms_bwd(x, g, dxn, res, name):
    t, d = x.shape

    def body(x_ref, g_ref, d_ref, res_ref, dx_ref, dg_ref):
        _, vjp = jax.vjp(_rms, x_ref[...], g_ref[...])
        dx, dg = vjp(d_ref[...].astype(F32))
        dx_ref[...] = dx + res_ref[...]

        @pl.when(pl.program_id(0) == 0)
        def _():
            dg_ref[...] = jnp.zeros_like(dg_ref)

        dg_ref[...] += dg

    row = pl.BlockSpec((ROWS, d), lambda i: (i, 0))
    return _blocked(body, name=name, grid=(t // ROWS,), in_specs=[row, _full((1, d)), row, row],
                          out_specs=[row, _full((1, d))],
                          out_shape=[jax.ShapeDtypeStruct((t, d), F32), jax.ShapeDtypeStruct((1, d), F32)],
                          compiler_params=_cparams(("arbitrary",), VMEM_MID))(x, g, dxn, res)


def _loss_head(x, g, tgt):
    t, d = x.shape

    def body(x_ref, g_ref, t_ref, l_ref, dx_ref, dg_ref):
        tg = t_ref[...]

        def fn(xv, gv):
            err = _rms(xv, gv) - tg
            per_tok = jnp.mean(err * err, axis=-1, keepdims=True)
            return 0.5 * jnp.sum(per_tok, axis=0, keepdims=True)

        l, vjp = jax.vjp(fn, x_ref[...], g_ref[...])
        dx, dg = vjp(jnp.ones((1, 1), F32))
        dx_ref[...] = dx

        @pl.when(pl.program_id(0) == 0)
        def _():
            dg_ref[...] = jnp.zeros_like(dg_ref)
            l_ref[...] = jnp.zeros_like(l_ref)

        dg_ref[...] += dg
        l_ref[...] += jnp.broadcast_to(l, l_ref.shape)

    row = pl.BlockSpec((ROWS, d), lambda i: (i, 0))
    return _blocked(body, name="loss_head", grid=(t // ROWS,), in_specs=[row, _full((1, d)), row],
                          out_specs=[_full((1, LANES)), row, _full((1, d))],
                          out_shape=[jax.ShapeDtypeStruct((1, LANES), F32), jax.ShapeDtypeStruct((t, d), F32),
                                     jax.ShapeDtypeStruct((1, d), F32)],
                          compiler_params=_cparams(("arbitrary",), VMEM_MID))(x, g, tgt)


def _glu_fwd(x, z):
    t, d = x.shape

    def body(x_ref, v_ref, g_ref, o_ref):
        o_ref[...] = x_ref[...] + v_ref[...] * jax.nn.sigmoid(g_ref[...])

    row = pl.BlockSpec((ROWS, d), lambda i: (i, 0))
    gate = pl.BlockSpec((ROWS, d), lambda i: (i, 1))
    return _blocked(body, name="glu_fwd", grid=(t // ROWS,), in_specs=[row, row, gate], out_specs=row,
                          out_shape=jax.ShapeDtypeStruct((t, d), F32),
                          compiler_params=_cparams(("parallel",), VMEM_MID))(x, z, z)


def _glu_bwd(z, g):
    t, d = g.shape

    def body(v_ref, g_ref, d_ref, o_ref):
        s = jax.nn.sigmoid(g_ref[...])
        dy = d_ref[...]
        o_ref[:, :d] = (dy * s).astype(BF16)
        o_ref[:, d:] = (dy * v_ref[...] * s * (1.0 - s)).astype(BF16)

    row = pl.BlockSpec((ROWS, d), lambda i: (i, 0))
    gate = pl.BlockSpec((ROWS, d), lambda i: (i, 1))
    return _blocked(body, name="glu_bwd", grid=(t // ROWS,), in_specs=[row, gate, row],
                          out_specs=pl.BlockSpec((ROWS, 2 * d), lambda i: (i, 0)),
                          out_shape=jax.ShapeDtypeStruct((t, 2 * d), BF16),
                          compiler_params=_cparams(("parallel",), VMEM_MID))(z, z, g)


def _shift_down(x, d):
    row = lax.broadcasted_iota(jnp.int32, x.shape, 0)
    return jnp.where(row < d, 0.0, pltpu.roll(x, d, 0))


def _shift_up(x, d):
    n = x.shape[0]
    row = lax.broadcasted_iota(jnp.int32, x.shape, 0)
    return jnp.where(row >= n - d, 0.0, pltpu.roll(x, n - d, 0))


def _make_sd():
    @functools.partial(jax.custom_vjp, nondiff_argnums=(1,))
    def sd(x, d):
        return _shift_down(x, d)

    def fwd(x, d):
        return _shift_down(x, d), None

    def bwd(d, _, g):
        return (_shift_up(g, d),)

    sd.defvjp(fwd, bwd)
    return sd


def _lin_scan(a, u, reverse=False):
    n = a.shape[0]
    row = lax.broadcasted_iota(jnp.int32, a.shape, 0)
    d = 1
    while d < n:
        if reverse:
            keep = row < n - d
            a_s, u_s = pltpu.roll(a, n - d, 0), pltpu.roll(u, n - d, 0)
        else:
            keep = row >= d
            a_s, u_s = pltpu.roll(a, d, 0), pltpu.roll(u, d, 0)
        u = u + a * jnp.where(keep, u_s, 0.0)
        a = a * jnp.where(keep, a_s, 1.0)
        d *= 2
    return u


def _make_scan():
    @jax.custom_vjp
    def scan(a, u):
        return _lin_scan(a, u)

    def fwd(a, u):
        h = _lin_scan(a, u)
        return h, (a, h)

    def bwd(res, dh):
        a, h = res
        g = _lin_scan(_shift_up(a, 1), dh, reverse=True)
        return g * _shift_down(h, 1), g

    scan.defvjp(fwd, bwd)
    return scan


def _acc_out(ref, val):
    @pl.when(pl.program_id(0) == 0)
    def _():
        ref[...] = jnp.zeros_like(ref)

    ref[...] += val


FFN_CW = 128


def _ffn_fn(hg, hv, wg, wv, bg, bv, sd):
    cg = wg[0:1] * sd(hg, 2) + wg[1:2] * sd(hg, 1) + wg[2:3] * hg + bg
    cv = wv[0:1] * sd(hv, 2) + wv[1:2] * sd(hv, 1) + wv[2:3] * hv + bv
    return jax.nn.silu(cg) * cv


def _ffn_specs(t):
    nb = D_FF // FFN_CW
    col = lambda r, off: pl.BlockSpec((r, FFN_CW), lambda j: (0, j + off))
    return nb, [col(t, 0), col(t, nb), col(3, 0), col(3, nb), col(1, 0), col(1, nb)], col


def _ffn_mid_fwd(h, cw, cb, name):
    t = h.shape[0]
    nb, in_specs, col = _ffn_specs(t)

    def body(hg, hv, wg, wv, bg, bv, o_ref):
        o_ref[...] = _ffn_fn(hg[...], hv[...], wg[...], wv[...], bg[...], bv[...], _shift_down).astype(BF16)

    return _blocked(body, name=name, grid=(nb,), in_specs=in_specs, out_specs=col(t, 0),
                          out_shape=jax.ShapeDtypeStruct((t, D_FF), BF16),
                          compiler_params=_cparams(("parallel",), VMEM_MID))(h, h, cw, cw, cb, cb)


def _ffn_mid_bwd(h, cw, cb, dact, name):
    t = h.shape[0]
    nb, in_specs, col = _ffn_specs(t)

    def body(hg, hv, wg, wv, bg, bv, d_ref, dhg, dhv, dwg, dwv, dbg, dbv):
        fn = functools.partial(_ffn_fn, sd=_make_sd())
        _, vjp = jax.vjp(fn, hg[...], hv[...], wg[...], wv[...], bg[...], bv[...])
        g = vjp(d_ref[...])
        dhg[...] = g[0].astype(BF16)
        dhv[...] = g[1].astype(BF16)
        dwg[...], dwv[...], dbg[...], dbv[...] = g[2], g[3], g[4], g[5]

    big = jax.ShapeDtypeStruct((t, D_FF), BF16)
    w3 = jax.ShapeDtypeStruct((3, D_FF), F32)
    b1 = jax.ShapeDtypeStruct((1, D_FF), F32)
    return _blocked(body, name=name, grid=(nb,), in_specs=in_specs + [col(t, 0)],
                          out_specs=[col(t, 0), col(t, 0), col(3, 0), col(3, 0), col(1, 0), col(1, 0)],
                          out_shape=[big, big, w3, w3, b1, b1],
                          compiler_params=_cparams(("parallel",), VMEM_BIG))(h, h, cw, cw, cb, cb, dact)


TS_CW = 256


def _tshift_fn(p, mu, sd):
    return p + mu * (sd(p, 1) - p)


def _tshift_fwd(p, mu):
    t = p.shape[0]
    col = lambda r: pl.BlockSpec((r, TS_CW), lambda j: (0, j))

    def body(p_ref, mu_ref, o_ref):
        o_ref[...] = _tshift_fn(p_ref[...], mu_ref[...], _shift_down)

    return _blocked(body, name="tshift_fwd", grid=(SHIFT_COLS // TS_CW,), in_specs=[col(t), col(1)],
                          out_specs=col(t), out_shape=jax.ShapeDtypeStruct((t, SHIFT_COLS), F32),
                          compiler_params=_cparams(("parallel",), VMEM_MID))(p, mu)


def _tshift_bwd(p, mu, dpam):
    t = p.shape[0]
    col = lambda r: pl.BlockSpec((r, TS_CW), lambda j: (0, j))

    def body(p_ref, mu_ref, d_ref, dp_ref, dmu_ref):
        _, vjp = jax.vjp(functools.partial(_tshift_fn, sd=_make_sd()), p_ref[...], mu_ref[...])
        dp, dmu = vjp(d_ref[...])
        dp_ref[...] = dp.astype(BF16)
        dmu_ref[...] = dmu

    return _blocked(body, name="tshift_bwd", grid=(SHIFT_COLS // TS_CW,), in_specs=[col(t), col(1), col(t)],
                          out_specs=[col(t), col(1)],
                          out_shape=[jax.ShapeDtypeStruct((t, SHIFT_COLS), BF16),
                                     jax.ShapeDtypeStruct((1, SHIFT_COLS), F32)],
                          compiler_params=_cparams(("parallel",), VMEM_MID))(p, mu, dpam)


_HI = lax.Precision.HIGHEST
_O = (0, RW, 2 * RW, 3 * RW, 3 * RW + W_LORA, 3 * RW + W_LORA + A_LORA, SHIFT_COLS)


def _dot16(a, b, dims=(((1,), (0,)), ((), ()))):
    return lax.dot_general(a.astype(BF16), b.astype(BF16), dims, preferred_element_type=F32)


def _make_dot16():
    @jax.custom_vjp
    def dot(a, b):
        return _dot16(a, b)

    def fwd(a, b):
        return _dot16(a, b), (a, b)

    def bwd(res, g):
        a, b = res
        return _dot16(g, b, (((1,), (1,)), ((), ()))), _dot16(a, g, (((0,), (0,)), ((), ())))

    dot.defvjp(fwd, bwd)
    return dot


def _seg(x):
    first = lax.broadcasted_iota(jnp.int32, (x.shape[0], LANES), 1) < HEAD
    parts = []
    for p in range(x.shape[1] // LANES):
        xp = x[:, p * LANES:(p + 1) * LANES]
        s0 = jnp.sum(jnp.where(first, xp, 0.0), axis=-1, keepdims=True)
        s1 = jnp.sum(jnp.where(first, 0.0, xp), axis=-1, keepdims=True)
        parts.append(jnp.where(first, s0, s1))
    return jnp.concatenate(parts, axis=1)


def _prep_fn(r, k, v, wd, ad, gd, w0, w2, a0, a2, g2, k_k, k_a, dot):
    w_log = -jax.nn.softplus(-(w0 + dot(jnp.tanh(wd), w2))) - 0.5
    decay = jnp.exp(-jnp.exp(w_log))
    a = jax.nn.sigmoid(a0 + dot(ad, a2))
    g = dot(jax.nn.sigmoid(gd), g2)
    kk = k * k_k
    kk = kk / jnp.maximum(jnp.sqrt(_seg(kk * kk)), 1e-12)
    k2 = k * (1.0 + (a - 1.0) * k_a)
    return r, decay, k2, v, -kk, kk * a, g


_PREP_W = ("w0", "w2", "a0", "a2", "g2", "k_k", "k_a")


def _prep_wspecs(w):
    return [_full(w[n].shape) for n in _PREP_W]


def _rwkv_prep_fwd(pam, w):
    t = pam.shape[0]

    def body(p_ref, *refs):
        wr, outs = refs[:7], refs[7:]
        pieces = [p_ref[:, _O[i]:_O[i + 1]] for i in range(6)]
        res = _prep_fn(*pieces, *[x[...] for x in wr], _dot16)
        for o, val in zip(outs, res):
            o[...] = val

    row = lambda c: pl.BlockSpec((TOK, c), lambda i: (i, 0))
    return _blocked(body, name="rwkv_prep_fwd", grid=(t // TOK,),
                          in_specs=[row(SHIFT_COLS)] + _prep_wspecs(w), out_specs=[row(RW)] * 7,
                          out_shape=[jax.ShapeDtypeStruct((t, RW), F32)] * 7,
                          compiler_params=_cparams(("parallel",), VMEM_MID))(pam, *[w[n] for n in _PREP_W])


def _rwkv_prep_bwd(pam, w, cts, more):
    t = pam.shape[0]

    def body(p_ref, *refs):
        wr, ct, ex, dp_ref, dws = refs[:7], refs[7:14], refs[14:17], refs[17], refs[18:]
        pieces = [p_ref[:, _O[i]:_O[i + 1]] for i in range(6)]
        fn = lambda *a: _prep_fn(*a, _make_dot16())
        _, vjp = jax.vjp(fn, *pieces, *[x[...] for x in wr])
        c = [x[...] for x in ct]
        c[0] = c[0] + ex[0][...]
        c[2] = c[2] + ex[1][...]
        c[3] = c[3] + ex[2][...]
        g = vjp(tuple(c))
        for i in range(6):
            dp_ref[:, _O[i]:_O[i + 1]] = g[i]
        for o, val in zip(dws, g[6:]):
            _acc_out(o, val)

    row = lambda c: pl.BlockSpec((TOK, c), lambda i: (i, 0))
    return _blocked(body, name="rwkv_prep_bwd", grid=(t // TOK,),
                          in_specs=[row(SHIFT_COLS)] + _prep_wspecs(w) + [row(RW)] * 10,
                          out_specs=[row(SHIFT_COLS)] + [_full(w[n].shape) for n in _PREP_W],
                          out_shape=[jax.ShapeDtypeStruct((t, SHIFT_COLS), F32)]
                          + [jax.ShapeDtypeStruct(w[n].shape, F32) for n in _PREP_W],
                          compiler_params=_cparams(("arbitrary",), VMEM_MID))(
                              pam, *[w[n] for n in _PREP_W], *cts, *more)


def _post_fn(y, r, k2, v, g, ln_w, ln_b, r_k):
    inv = 1.0 / HEAD
    d = y - _seg(y) * inv
    yn = d * lax.rsqrt(_seg(d * d) * inv + GN_EPS) * ln_w + ln_b
    bonus = _seg(r * k2 * r_k) * v
    return (yn + bonus) * g


def _rwkv_post_fwd(y, r, k2, v, g, ln_w, ln_b, r_k):
    t = y.shape[0]

    def body(*refs):
        o_ref = refs[-1]
        o_ref[...] = _post_fn(*[x[...] for x in refs[:-1]]).astype(BF16)

    row = pl.BlockSpec((ROWS, RW), lambda i: (i, 0))
    return _blocked(body, name="rwkv_post_fwd", grid=(t // ROWS,),
                          in_specs=[row] * 5 + [_full((1, RW))] * 3, out_specs=row,
                          out_shape=jax.ShapeDtypeStruct((t, RW), BF16),
                          compiler_params=_cparams(("parallel",), VMEM_MID))(y, r, k2, v, g, ln_w, ln_b, r_k)


def _rwkv_post_bwd(y, r, k2, v, g, ln_w, ln_b, r_k, dya):
    t = y.shape[0]

    def body(*refs):
        ins, d_ref, outs = refs[:8], refs[8], refs[9:]
        _, vjp = jax.vjp(_post_fn, *[x[...] for x in ins])
        gr = vjp(d_ref[...])
        for o, val in zip(outs[:5], gr[:5]):
            o[...] = val
        for o, val in zip(outs[5:], gr[5:]):
            _acc_out(o, val)

    row = pl.BlockSpec((TOK, RW), lambda i: (i, 0))
    vec = _full((1, RW))
    return _blocked(body, name="rwkv_post_bwd", grid=(t // TOK,),
                          in_specs=[row] * 5 + [vec] * 3 + [row],
                          out_specs=[row] * 5 + [vec] * 3,
                          out_shape=[jax.ShapeDtypeStruct((t, RW), F32)] * 5 + [jax.ShapeDtypeStruct((1, RW), F32)] * 3,
                          compiler_params=_cparams(("arbitrary",), VMEM_MID))(y, r, k2, v, g, ln_w, ln_b, r_k, dya)


def _from_pt(x):
    n = x.shape[0]
    return x.reshape(n, HEAD, N_HEADS, PT).transpose(0, 3, 2, 1).reshape(n * PT, N_HEADS * HEAD)


def _lane_sum(x):
    return jnp.sum(x, axis=-1, keepdims=True)


def _pair_consts():
    lane = lax.broadcasted_iota(jnp.int32, (HEAD, LANES), 1)
    return lane, lane < HEAD


def _seg_sum_pair(x, first):
    return jnp.where(first, _lane_sum(jnp.where(first, x, 0.0)), _lane_sum(jnp.where(first, 0.0, x)))


def _to_pt(x):
    t = x.shape[0]
    return x.reshape(t // PT, PT, N_HEADS, HEAD).transpose(0, 3, 2, 1).reshape(t // PT, HEAD, N_HEADS * PT)


def _expand_cols(x, name):
    t = x.shape[0]
    chunk = 2 * WKV_CHUNK
    tiles = chunk // PT

    def body(x_ref, o_ref):
        _, first = _pair_consts()
        for tl in range(tiles):
            tile = x_ref[tl]
            for j in range(PT):
                for p in range(N_HEADS // 2):
                    src = jnp.where(first, (2 * p) * PT + j, (2 * p + 1) * PT + j)
                    o_ref[tl * PT + j, :, p * LANES:(p + 1) * LANES] = jnp.take_along_axis(tile, src, axis=1)

    return _blocked(
        body, name=name, grid=(t // chunk,),
        in_specs=[pl.BlockSpec((tiles, HEAD, LANES), lambda i: (i, 0, 0))],
        out_specs=pl.BlockSpec((chunk, HEAD, RW), lambda i: (i, 0, 0)),
        out_shape=jax.ShapeDtypeStruct((t, HEAD, RW), F32),
        compiler_params=_cparams(("parallel",), VMEM_MID))(_to_pt(x))


def _wkv_fwd(w, k, z, b, v_exp):
    t = w.shape[0]
    chunk = 2 * WKV_CHUNK
    nc = t // chunk
    pairs = N_HEADS // 2

    def body(w_ref, k_ref, z_ref, b_ref, v_ref, s_all, s_ref):
        @pl.when(pl.program_id(0) == 0)
        def _():
            s_ref[...] = jnp.zeros_like(s_ref)

        _, first = _pair_consts()

        def group(gi, carry):
            base = pl.multiple_of(gi * 8, 8)
            rows = [ref[pl.ds(base, 8), :] for ref in (w_ref, k_ref, z_ref, b_ref)]
            s = [s_ref[:, p * LANES:(p + 1) * LANES] for p in range(pairs)]
            for jj in range(8):
                for p in range(pairs):
                    cs = slice(p * LANES, (p + 1) * LANES)
                    wr, kr, zr, br = [x[jj:jj + 1, cs] for x in rows]
                    s_all[base + jj, :, cs] = s[p]
                    sa = _seg_sum_pair(s[p] * zr, first)
                    s[p] = s[p] * wr + sa * br + v_ref[base + jj, :, cs] * kr
            for p in range(pairs):
                s_ref[:, p * LANES:(p + 1) * LANES] = s[p]
            return carry

        lax.fori_loop(0, chunk // 8, group, 0)

    row = pl.BlockSpec((chunk, RW), lambda i: (i, 0))
    big = pl.BlockSpec((chunk, HEAD, RW), lambda i: (i, 0, 0))
    return _blocked(
        body, name="wkv_fwd", grid=(nc,), in_specs=[row] * 4 + [big], out_specs=[big, _full((HEAD, RW))],
        out_shape=[jax.ShapeDtypeStruct((t, HEAD, RW), F32), jax.ShapeDtypeStruct((HEAD, RW), F32)],
        compiler_params=_cparams(("arbitrary",), VMEM_BIG))(w, k, z, b, v_exp)


def _wkv_out(r, s_all, s_last):
    t = r.shape[0]
    chunk = 2 * WKV_CHUNK
    nc = t // chunk
    tiles = chunk // PT
    pairs = N_HEADS // 2

    def body(r_ref, s_ref, nxt_ref, last_ref, y_ref):
        lane, first = _pair_consts()
        after = jnp.where(pl.program_id(0) == nc - 1, last_ref[...], nxt_ref[0])
        for tl in range(tiles):
            ytile = jnp.zeros((HEAD, LANES), F32)
            for g in range(PT // 8):
                rows = r_ref[tl * PT + g * 8:tl * PT + g * 8 + 8, :]
                for jj in range(8):
                    tt = tl * PT + g * 8 + jj
                    j = g * 8 + jj
                    for p in range(pairs):
                        cs = slice(p * LANES, (p + 1) * LANES)
                        s = s_ref[tt + 1, :, cs] if tt + 1 < chunk else after[:, cs]
                        pr = s * rows[jj:jj + 1, cs]
                        y0 = _lane_sum(jnp.where(first, pr, 0.0))
                        y1 = _lane_sum(jnp.where(first, 0.0, pr))
                        ytile = jnp.where(lane == (2 * p) * PT + j, y0, ytile)
                        ytile = jnp.where(lane == (2 * p + 1) * PT + j, y1, ytile)
            y_ref[tl] = ytile

    row = pl.BlockSpec((chunk, RW), lambda i: (i, 0))
    pt = pl.BlockSpec((tiles, HEAD, LANES), lambda i: (i, 0, 0))
    big = pl.BlockSpec((chunk, HEAD, RW), lambda i: (i, 0, 0))
    nxt = pl.BlockSpec((1, HEAD, RW), lambda i: (jnp.minimum((i + 1) * chunk, t - 1), 0, 0))
    return _blocked(
        body, name="wkv_out", grid=(nc,), in_specs=[row, big, nxt, _full((HEAD, RW))], out_specs=pt,
        out_shape=jax.ShapeDtypeStruct((t // PT, HEAD, LANES), F32),
        compiler_params=_cparams(("parallel",), VMEM_MID))(r, s_all, s_all, s_last)


def _wkv_bwd(r, w, k, z, b, v_exp, s_all, dy_exp):
    t = r.shape[0]
    nc = t // WKV_CHUNK
    tiles = WKV_CHUNK // PT
    pairs = N_HEADS // 2

    def body(r_ref, w_ref, k_ref, z_ref, b_ref, v_ref, s_all_ref, dy_ref,
             dr_ref, dw_ref, dk_ref, dz_ref, db_ref, dv_ref, ds_ref):
        @pl.when(pl.program_id(0) == 0)
        def _():
            ds_ref[...] = jnp.zeros_like(ds_ref)

        lane, first = _pair_consts()
        col_sum = lambda x: jnp.sum(x, axis=0, keepdims=True)
        row8 = lax.broadcasted_iota(jnp.int32, (8, LANES), 0)
        for tl in reversed(range(tiles)):
            def group(gg, dvtile):
                gi = PT // 8 - 1 - gg
                base = pl.multiple_of(tl * PT + gi * 8, 8)
                rows = [ref[pl.ds(base, 8), :] for ref in (r_ref, w_ref, k_ref, z_ref, b_ref)]
                outs = (dr_ref, dw_ref, dk_ref, dz_ref, db_ref)
                tiles8 = {(id(o), p): jnp.zeros((8, LANES), F32) for o in outs for p in range(pairs)}
                ds = [ds_ref[:, p * LANES:(p + 1) * LANES] for p in range(pairs)]
                for jj in reversed(range(8)):
                    j = gi * 8 + jj
                    for p in range(pairs):
                        cs = slice(p * LANES, (p + 1) * LANES)

                        def put(ref, val, p=p, jj=jj):
                            tiles8[(id(ref), p)] = jnp.where(row8 == jj, val, tiles8[(id(ref), p)])

                        rr, wr, kr, zr, br = [x[jj:jj + 1, cs] for x in rows]
                        sp = s_all_ref[base + jj, :, cs]
                        vc = v_ref[base + jj, :, cs]
                        dyc = dy_ref[base + jj, :, cs]
                        sa = _seg_sum_pair(sp * zr, first)
                        st = sp * wr + sa * br + vc * kr
                        d = ds[p] + dyc * rr
                        put(dr_ref, col_sum(st * dyc))
                        dvk = d * kr
                        dv0 = _lane_sum(jnp.where(first, dvk, 0.0))
                        dv1 = _lane_sum(jnp.where(first, 0.0, dvk))
                        dvtile = jnp.where(lane == (2 * p) * PT + j, dv0, dvtile)
                        dvtile = jnp.where(lane == (2 * p + 1) * PT + j, dv1, dvtile)
                        put(dk_ref, col_sum(d * vc))
                        put(dw_ref, col_sum(sp * d))
                        u = _seg_sum_pair(d * br, first)
                        put(dz_ref, col_sum(sp * u))
                        put(db_ref, col_sum(d * sa))
                        ds[p] = d * wr + u * zr
                for p in range(pairs):
                    ds_ref[:, p * LANES:(p + 1) * LANES] = ds[p]
                for o in outs:
                    for p in range(pairs):
                        o[pl.ds(base, 8), p * LANES:(p + 1) * LANES] = tiles8[(id(o), p)]
                return dvtile

            dv_ref[tl] = lax.fori_loop(0, PT // 8, group, jnp.zeros((HEAD, LANES), F32))

    rev = lambda i: nc - 1 - i
    row = pl.BlockSpec((WKV_CHUNK, RW), lambda i: (rev(i), 0))
    pt = pl.BlockSpec((tiles, HEAD, LANES), lambda i: (rev(i), 0, 0))
    big = pl.BlockSpec((WKV_CHUNK, HEAD, RW), lambda i: (rev(i), 0, 0))
    return _blocked(
        body, name="wkv_bwd", grid=(nc,), in_specs=[row] * 5 + [big, big, big], out_specs=[row] * 5 + [pt],
        out_shape=[jax.ShapeDtypeStruct((t, RW), F32)] * 5 + [jax.ShapeDtypeStruct((t // PT, HEAD, LANES), F32)],
        scratch_shapes=[pltpu.VMEM((HEAD, RW), F32)],
        compiler_params=_cparams(("arbitrary",), VMEM_BIG))(r, w, k, z, b, v_exp, s_all, dy_exp)


LRU_CW = 128
_BX0 = SHIFT_COLS // LRU_CW
_BG0 = (SHIFT_COLS + LRU_W) // LRU_CW


def _lru_fn(bx, bg, cw, cb, ga, ba, gx, bxb, lam, sd, scan, dot):
    xc = cw[0:1] * sd(bx, 3) + cw[1:2] * sd(bx, 2) + cw[2:3] * sd(bx, 1) + cw[3:4] * bx + cb
    gr = jax.nn.sigmoid(dot(xc, ga) + ba)
    gi = jax.nn.sigmoid(dot(xc, gx) + bxb)
    log_a = -LRU_C * gr * jax.nn.softplus(-lam)
    a = jnp.exp(log_a)
    mult = jnp.sqrt(-jnp.tanh(log_a) * (jnp.exp(2.0 * log_a) + 1.0))
    return scan(a, xc * gi * mult) * jax.nn.gelu(bg)


def _lru_specs(t):
    col = lambda r, off=0: pl.BlockSpec((r, LRU_CW), lambda j: (0, j + off))
    diag = pl.BlockSpec((LRU_CW, LRU_CW), lambda j: (j, j))
    return col, [col(t, _BX0), col(t, _BG0), col(4), col(1), diag, col(1), diag, col(1), col(1)]


def _lru_fwd(p, cw, cb, ga, ba, gx, bxb, lam):
    t = p.shape[0]
    col, in_specs = _lru_specs(t)

    def body(*refs):
        o_ref = refs[-1]
        o_ref[...] = _lru_fn(*[x[...] for x in refs[:-1]], _shift_down, _lin_scan, _dot16).astype(BF16)

    return _blocked(body, name="lru_fwd", grid=(LRU_W // LRU_CW,), in_specs=in_specs, out_specs=col(t),
                          out_shape=jax.ShapeDtypeStruct((t, LRU_W), BF16),
                          compiler_params=_cparams(("parallel",), VMEM_MID))(p, p, cw, cb, ga, ba, gx, bxb, lam)


def _lru_bwd(p, cw, cb, ga, ba, gx, bxb, lam, dyb):
    t = p.shape[0]
    col, in_specs = _lru_specs(t)

    def body(*refs):
        ins, d_ref, outs = refs[:9], refs[9], refs[10:]
        fn = functools.partial(_lru_fn, sd=_make_sd(), scan=_make_scan(), dot=_make_dot16())
        _, vjp = jax.vjp(fn, *[x[...] for x in ins])
        g = vjp(d_ref[...])
        outs[0][...] = g[0].astype(BF16)
        outs[1][...] = g[1].astype(BF16)
        for o, val in zip(outs[2:], g[2:]):
            o[...] = val

    sq = pl.BlockSpec((LRU_CW, LRU_CW), lambda j: (j, 0))
    act = jax.ShapeDtypeStruct((t, LRU_W), BF16)
    vec = jax.ShapeDtypeStruct((1, LRU_W), F32)
    sqs = jax.ShapeDtypeStruct((LRU_W, LRU_CW), F32)
    return _blocked(body, name="lru_bwd", grid=(LRU_W // LRU_CW,), in_specs=in_specs + [col(t, RW // LRU_CW)],
                          out_specs=[col(t), col(t), col(4), col(1), sq, col(1), sq, col(1), col(1)],
                          out_shape=[act, act, jax.ShapeDtypeStruct((4, LRU_W), F32), vec, sqs, vec, sqs, vec, vec],
                          compiler_params=_cparams(("parallel",), VMEM_BIG))(p, p, cw, cb, ga, ba, gx, bxb, lam, dyb)


def _s5_disc_fn(a_re, a_im, log_dt, b_re, b_im, e):
    lam_re = jnp.minimum(a_re, -1e-4)
    lam_im = a_im
    dt = jnp.exp(log_dt)
    mag = jnp.exp(lam_re * dt)
    ab_re = mag * jnp.cos(lam_im * dt)
    ab_im = mag * jnp.sin(lam_im * dt)
    den = lam_re * lam_re + lam_im * lam_im
    zr = ab_re - 1.0
    q_re = jnp.dot((zr * lam_re + ab_im * lam_im) / den, e, precision=_HI)
    q_im = jnp.dot((ab_im * lam_re - zr * lam_im) / den, e, precision=_HI)
    return ab_re, ab_im, q_re * b_re - q_im * b_im, q_re * b_im + q_im * b_re


def _s5_disc_fwd(a_re, a_im, log_dt, b_re, b_im, e):
    def body(*refs):
        res = _s5_disc_fn(*[x[...] for x in refs[:6]])
        for o, val in zip(refs[6:], res):
            o[...] = val

    small = jax.ShapeDtypeStruct(a_re.shape, F32)
    wide = jax.ShapeDtypeStruct(b_re.shape, F32)
    return pl.pallas_call(body, name="s5_disc_fwd", out_shape=[small, small, wide, wide])(
        a_re, a_im, log_dt, b_re, b_im, e)


def _s5_disc_bwd(a_re, a_im, log_dt, b_re, b_im, e, cts):
    def body(*refs):
        ins, e_ref, ct, outs = refs[:5], refs[5], refs[6:10], refs[10:]
        _, vjp = jax.vjp(lambda *a: _s5_disc_fn(*a, e_ref[...]), *[x[...] for x in ins])
        for o, val in zip(outs, vjp(tuple(c[...] for c in ct))):
            o[...] = val

    shapes = [jax.ShapeDtypeStruct(x.shape, F32) for x in (a_re, a_im, log_dt, b_re, b_im)]
    return pl.pallas_call(body, name="s5_disc_bwd", out_shape=shapes)(a_re, a_im, log_dt, b_re, b_im, e, *cts)


def _cmul(a, b):
    return a[0] * b[0] - a[1] * b[1], a[0] * b[1] + a[1] * b[0]


def _s5_scan(sr, si, ab, reverse):
    n_tiles = sr.shape[0] // 8
    width = sr.shape[1]
    row8 = lax.broadcasted_iota(jnp.int32, (8, width), 0)
    p1 = ab
    p2 = _cmul(p1, p1)
    p4 = _cmul(p2, p2)
    pw = [p1]
    for _ in range(7):
        pw.append(_cmul(pw[-1], p1))
    cr = jnp.zeros((8, width), F32)
    ci = jnp.zeros((8, width), F32)
    for j in range(8):
        e = pw[7 - j] if reverse else pw[j]
        cr = jnp.where(row8 == j, e[0], cr)
        ci = jnp.where(row8 == j, e[1], ci)

    levels = []
    for d, q in ((1, p1), (2, p2), (4, p4)):
        keep = row8 < 8 - d if reverse else row8 >= d
        levels.append((d, (jnp.where(keep, q[0], 0.0), jnp.where(keep, q[1], 0.0))))

    def tile(i, carry):
        idx = n_tiles - 1 - i if reverse else i
        base = pl.multiple_of(idx * 8, 8)
        x = (sr[pl.ds(base, 8), :], si[pl.ds(base, 8), :])
        for d, q in levels:
            amt = 8 - d if reverse else d
            m = _cmul(q, (pltpu.roll(x[0], amt, 0), pltpu.roll(x[1], amt, 0)))
            x = (x[0] + m[0], x[1] + m[1])
        m = _cmul((cr, ci), carry)
        x = (x[0] + m[0], x[1] + m[1])
        sr[pl.ds(base, 8), :] = x[0]
        si[pl.ds(base, 8), :] = x[1]
        edge = slice(0, 1) if reverse else slice(7, 8)
        return x[0][edge], x[1][edge]

    zero = jnp.zeros((1, width), F32)
    lax.fori_loop(0, n_tiles, tile, (zero, zero))


_S5_W = S5_SLAB // S5_GROUP * S5_STATE


def _s5_specs(t):
    col = lambda r: pl.BlockSpec((r, S5_SLAB), lambda j: (0, j))
    bb = pl.BlockSpec((None, S5_SLAB, _S5_W), lambda j: (j, 0, 0))
    cd = pl.BlockSpec((None, _S5_W, S5_SLAB), lambda j: (j, 0, 0))
    ab = pl.BlockSpec((None, 1, _S5_W), lambda j: (j, 0, 0))
    return col, bb, cd, ab


def _s5_fwd(u, dvec, bbr, bbi, cdr, cdi, abr, abi):
    t, width = u.shape
    col, bb, cd, ab = _s5_specs(t)

    def body(u_ref, d_ref, bbr_ref, bbi_ref, cdr_ref, cdi_ref, abr_ref, abi_ref, o_ref, sr, si):
        uv = u_ref[...]
        sr[...] = _dot16(uv, bbr_ref[...])
        si[...] = _dot16(uv, bbi_ref[...])
        _s5_scan(sr, si, (abr_ref[...], abi_ref[...]), False)
        y = _dot16(sr[...], cdr_ref[...]) - _dot16(si[...], cdi_ref[...])
        o_ref[...] = jax.nn.gelu(y + d_ref[...] * uv).astype(BF16)

    return _blocked(body, name="s5_fwd", grid=(width // S5_SLAB,),
                          in_specs=[col(t), col(1), bb, bb, cd, cd, ab, ab], out_specs=col(t),
                          out_shape=jax.ShapeDtypeStruct((t, width), BF16),
                          scratch_shapes=[pltpu.VMEM((t, _S5_W), F32)] * 2,
                          compiler_params=_cparams(("parallel",), VMEM_BIG))(u, dvec, bbr, bbi, cdr, cdi, abr, abi)


def _s5_bwd(u, dvec, bbr, bbi, cdr, cdi, abr, abi, dyact):
    t, width = u.shape
    col, bb, cd, ab = _s5_specs(t)
    ns = width // S5_SLAB
    tn = (((0,), (0,)), ((), ()))
    nt = (((1,), (1,)), ((), ()))

    def body(u_ref, d_ref, bbr_ref, bbi_ref, cdr_ref, cdi_ref, abr_ref, abi_ref, dy_ref,
             du_ref, dd_ref, dbbr_ref, dbbi_ref, dcdr_ref, dcdi_ref, dabr_ref, dabi_ref, sr, si, gr, gi):
        uv = u_ref[...]
        dv = d_ref[...]
        abv = (abr_ref[...], abi_ref[...])
        sr[...] = _dot16(uv, bbr_ref[...])
        si[...] = _dot16(uv, bbi_ref[...])
        _s5_scan(sr, si, abv, False)
        y = _dot16(sr[...], cdr_ref[...]) - _dot16(si[...], cdi_ref[...])
        _, vjp = jax.vjp(jax.nn.gelu, y + dv * uv)
        (dpre,) = vjp(dy_ref[...].astype(F32))
        dd_ref[...] = jnp.sum(dpre * uv, axis=0, keepdims=True)
        dcdr_ref[...] = _dot16(sr[...], dpre, tn)
        dcdi_ref[...] = -_dot16(si[...], dpre, tn)
        gr[...] = _dot16(dpre, cdr_ref[...], nt)
        gi[...] = -_dot16(dpre, cdi_ref[...], nt)
        _s5_scan(gr, gi, (abv[0], -abv[1]), True)

        row8 = lax.broadcasted_iota(jnp.int32, (8, _S5_W), 0)

        def tile(i, carry):
            acc_r, acc_i, last_r, last_i = carry
            base = pl.multiple_of(i * 8, 8)
            s_r, s_i = sr[pl.ds(base, 8), :], si[pl.ds(base, 8), :]
            g_r, g_i = gr[pl.ds(base, 8), :], gi[pl.ds(base, 8), :]
            p_r = jnp.where(row8 == 0, last_r, pltpu.roll(s_r, 1, 0))
            p_i = jnp.where(row8 == 0, last_i, pltpu.roll(s_i, 1, 0))
            acc_r = acc_r + jnp.sum(g_r * p_r + g_i * p_i, axis=0, keepdims=True)
            acc_i = acc_i + jnp.sum(g_i * p_r - g_r * p_i, axis=0, keepdims=True)
            return acc_r, acc_i, s_r[7:8], s_i[7:8]

        zero = jnp.zeros((1, _S5_W), F32)
        acc_r, acc_i, _, _ = lax.fori_loop(0, t // 8, tile, (zero, zero, zero, zero))
        dabr_ref[...] = acc_r
        dabi_ref[...] = acc_i
        du_ref[...] = dpre * dv + _dot16(gr[...], bbr_ref[...], nt) + _dot16(gi[...], bbi_ref[...], nt)
        dbbr_ref[...] = _dot16(uv, gr[...], tn)
        dbbi_ref[...] = _dot16(uv, gi[...], tn)

    sds = jax.ShapeDtypeStruct
    return _blocked(
        body, name="s5_bwd", grid=(ns,), in_specs=[col(t), col(1), bb, bb, cd, cd, ab, ab, col(t)],
        out_specs=[col(t), col(1), bb, bb, cd, cd, ab, ab],
        out_shape=[sds((t, width), F32), sds((1, width), F32), sds((ns, S5_SLAB, _S5_W), F32),
                   sds((ns, S5_SLAB, _S5_W), F32), sds((ns, _S5_W, S5_SLAB), F32), sds((ns, _S5_W, S5_SLAB), F32),
                   sds((ns, 1, _S5_W), F32), sds((ns, 1, _S5_W), F32)],
        scratch_shapes=[pltpu.VMEM((t, _S5_W), F32)] * 4,
        compiler_params=_cparams(("parallel",), VMEM_BIG))(u, dvec, bbr, bbi, cdr, cdi, abr, abi, dyact)


def _gate_dense(w):
    h = w.shape[0]
    return jnp.einsum("hij,hg->higj", w, jnp.eye(h, dtype=F32)).reshape(h * HEAD, h * HEAD)


def _gate_blocks(d):
    x = d.reshape(LRU_W // LRU_CW, 2, HEAD, 2, HEAD)
    return jnp.einsum("tgihj,gh->tgij", x, jnp.eye(2, dtype=F32)).reshape(LRU_W // HEAD, HEAD, HEAD)


_GPS = S5_SLAB // S5_GROUP
_NS = S5_GROUPS // _GPS


def _s5_in_dense(bb):
    x = bb.reshape(_NS, _GPS, S5_STATE, S5_GROUP)
    return jnp.einsum("sgnc,gh->sgchn", x, jnp.eye(_GPS, dtype=F32)).reshape(_NS, S5_SLAB, _S5_W)


def _s5_in_blocks(d):
    x = d.reshape(_NS, _GPS, S5_GROUP, _GPS, S5_STATE)
    return jnp.einsum("sgchn,gh->sgnc", x, jnp.eye(_GPS, dtype=F32)).reshape(S5_GROUPS, S5_STATE * S5_GROUP)


def _s5_out_dense(c):
    x = c.reshape(_NS, _GPS, S5_GROUP, S5_STATE)
    return jnp.einsum("sgcn,gh->shngc", x, jnp.eye(_GPS, dtype=F32)).reshape(_NS, _S5_W, S5_SLAB)


def _s5_out_blocks(d):
    x = d.reshape(_NS, _GPS, S5_STATE, _GPS, S5_GROUP)
    return jnp.einsum("shngc,gh->sgcn", x, jnp.eye(_GPS, dtype=F32)).reshape(S5_GROUPS, S5_GROUP, S5_STATE)


def _local_step(x, tgt, w, late_weights, send_grads):
    d_model = x.shape[1]
    gs = {}
    n_layers = w["f_norm_g"].shape[0]

    def ffn_fwd(xin, l):
        xn = _rms_fwd(xin, w["f_norm_g"][l:l + 1], f"rms_f{l}")
        h = _matmul(xn, w["f_w_up_t"][l], "nt", f"mm_f{l}_up")
        act = _ffn_mid_fwd(h, w["f_conv_w"][l], w["f_conv_b"][l:l + 1], f"ffn_mid_fwd{l}")
        return _matmul(act, w["f_w_down"][l], "nn", f"mm_f{l}_down", add=xin), (xin, xn, h, act)

    def ffn_bwd(g, saved, l):
        xin, xn, h, act = saved
        dact = _matmul(g, w["f_w_down"][l], "nt", f"mm_f{l}_dact")
        d_down = _matmul(act, g, "tn", f"mm_f{l}_ddown", out_dtype=BF16)
        dhg, dhv, dwg, dwv, dbg, dbv = _ffn_mid_bwd(h, w["f_conv_w"][l], w["f_conv_b"][l:l + 1], dact,
                                                    f"ffn_mid_bwd{l}")
        dxn = _matmul((dhg, dhv), w["f_w_up_t"][l], "nn", f"mm_f{l}_dxn")
        d_up = _matmul((dhg, dhv), xn, "tn", f"mm_f{l}_dup", out_dtype=BF16)
        dx, dgn = _rms_bwd(xin, w["f_norm_g"][l:l + 1], dxn, g, f"rms_f{l}_bwd")
        return dx, d_up, d_down, jnp.concatenate([dwg, dwv], axis=1), jnp.concatenate([dbg, dbv], axis=1), dgn

    xn0 = _rms_fwd(x, w["e_norm_g"], "rms_e")
    p = _matmul(xn0, w["e_w_in_t"], "nt", "mm_e_in")
    pam = _tshift_fwd(p, w["e_mu"])
    pw = dict(w0=w["e_w0"], w2=w["e_w2"][0], a0=w["e_a0"], a2=w["e_a2"][0], g2=w["e_g2"][0],
              k_k=w["e_k_k"], k_a=w["e_k_a"])
    r, dec, k2, v, z, b, gate = _rwkv_prep_fwd(pam, pw)
    v_exp = _expand_cols(v, "wkv_expand_v")
    s_all, s_last = _wkv_fwd(dec, k2, z, b, v_exp)
    y_pt = _wkv_out(r, s_all, s_last)
    y = _from_pt(y_pt)
    rk = w["e_r_k"].reshape(1, RW)
    ya = _rwkv_post_fwd(y, r, k2, v, gate, w["e_ln_w"], w["e_ln_b"], rk)
    ga, gx = _gate_dense(w["e_gate_a_w"][0]), _gate_dense(w["e_gate_x_w"][0])
    lru_w = (w["e_conv_w"][0], w["e_conv_b"], ga, w["e_gate_a_b"], gx, w["e_gate_x_b"], w["e_lru_lambda"])
    yb = _lru_fwd(p, *lru_w)
    ycat = jnp.concatenate([ya, yb], axis=1)
    w = {**w, **late_weights(ycat)}
    x1 = _matmul(ycat, w["e_w_out"], "nn", "mm_e_out", add=x)
    x2, ffn0 = ffn_fwd(x1, 0)

    xn1 = _rms_fwd(x2, w["o_norm_g"], "rms_o")
    u = _matmul(xn1, w["o_w_in"], "nn", "mm_o_in")
    expand = jnp.kron(jnp.eye(S5_STATE, dtype=F32), jnp.ones((1, S5_GROUP), F32))
    disc_in = (w["o_A_re"][0], w["o_A_im"][0], w["o_log_dt"].reshape(S5_GROUPS, 1),
               w["o_B_re"][0].reshape(S5_GROUPS, -1), w["o_B_im"][0].reshape(S5_GROUPS, -1), expand)
    ab_re, ab_im, bb_re, bb_im = _s5_disc_fwd(*disc_in)
    s5_w = (w["o_D"], _s5_in_dense(bb_re), _s5_in_dense(bb_im), _s5_out_dense(w["o_C_re"][0]),
            _s5_out_dense(w["o_C_im"][0]), ab_re.reshape(_NS, 1, _S5_W), ab_im.reshape(_NS, 1, _S5_W))
    yact = _s5_fwd(u, *s5_w)
    zz = _matmul(yact, w["o_w_glu_t"], "nt", "mm_o_glu")
    x3 = _glu_fwd(x2, zz)
    x4, ffn1 = ffn_fwd(x3, 1)

    loss, g, gs["final_norm_g", 0] = _loss_head(x4, w["final_norm_g"].reshape(1, d_model), tgt)

    g, up1, down1, dcw1, dcb1, dfn1 = ffn_bwd(g, ffn1, 1)
    dz = _glu_bwd(zz, g)
    dyact = _matmul(dz, w["o_w_glu_t"], "nn", "mm_o_dyact")
    d_glu = _matmul(dz, yact, "tn", "mm_o_dglu", out_dtype=BF16)
    du, gs["o_D", 0], dbbr, dbbi, dcdr, dcdi, dabr, dabi = _s5_bwd(u, *s5_w, dyact)
    gs["o_C_re", 0] = _s5_out_blocks(dcdr).reshape(S5_GROUPS * S5_GROUP, S5_STATE)
    gs["o_C_im", 0] = _s5_out_blocks(dcdi).reshape(S5_GROUPS * S5_GROUP, S5_STATE)
    cts = (dabr.reshape(S5_GROUPS, S5_STATE), dabi.reshape(S5_GROUPS, S5_STATE), _s5_in_blocks(dbbr),
           _s5_in_blocks(dbbi))
    gs["o_A_re", 0], gs["o_A_im", 0], dlog_dt, gs["o_B_re", 0], gs["o_B_im", 0] = _s5_disc_bwd(*disc_in, cts)
    gs["o_log_dt", 0] = dlog_dt.reshape(1, S5_GROUPS)
    dxn = _matmul(du, w["o_w_in"], "nt", "mm_o_dxn")
    d_oin = _matmul(xn1, du, "tn", "mm_o_din", out_dtype=BF16)
    g, gs["o_norm_g", 0] = _rms_bwd(x2, w["o_norm_g"], dxn, g, "rms_o_bwd")
    g = send_grads("a", [("f_w_up", 1, up1), ("f_w_down", 1, down1), ("o_w_glu", 0, d_glu), ("o_w_in", 0, d_oin)], g)

    g, up0, down0, dcw0, dcb0, dfn0 = ffn_bwd(g, ffn0, 0)
    gs["f_conv_w", 0], gs["f_conv_w", 3] = dcw0, dcw1
    gs["f_conv_b", 0], gs["f_conv_b", 1] = dcb0, dcb1
    gs["f_norm_g", 0], gs["f_norm_g", 1] = dfn0, dfn1

    dycat = _matmul(g, w["e_w_out"], "nt", "mm_e_dycat")
    d_eout = _matmul(ycat, g, "tn", "mm_e_dout", out_dtype=BF16)
    dycat = send_grads("b", [("f_w_up", 0, up0), ("f_w_down", 0, down0), ("e_w_out", 0, d_eout)], dycat)
    dy, dr1, dk1, dv1, dgate, gs["e_ln_w", 0], gs["e_ln_b", 0], gs["e_r_k", 0] = _rwkv_post_bwd(
        y, r, k2, v, gate, w["e_ln_w"], w["e_ln_b"], rk, dycat)
    dr2, ddec, dk2, dzz, dbb, dv_pt = _wkv_bwd(r, dec, k2, z, b, v_exp, s_all, _expand_cols(dy, "wkv_expand_dy"))
    (dpam, gs["e_w0", 0], gs["e_w2", 0], gs["e_a0", 0], gs["e_a2", 0], gs["e_g2", 0], gs["e_k_k", 0],
     gs["e_k_a", 0]) = _rwkv_prep_bwd(pam, pw, (dr2, ddec, dk2, _from_pt(dv_pt), dzz, dbb, dgate), (dr1, dk1, dv1))
    dpa, gs["e_mu", 0] = _tshift_bwd(p, w["e_mu"], dpam)
    (dbx, dbg, gs["e_conv_w", 0], gs["e_conv_b", 0], dga, gs["e_gate_a_b", 0], dgx, gs["e_gate_x_b", 0],
     gs["e_lru_lambda", 0]) = _lru_bwd(p, *lru_w, dycat)
    gs["e_gate_a_w", 0] = _gate_blocks(dga).reshape(LRU_W, HEAD)
    gs["e_gate_x_w", 0] = _gate_blocks(dgx).reshape(LRU_W, HEAD)
    dp = jnp.concatenate([dpa, dbx, dbg], axis=1)
    d_ein = _matmul(dp, xn0, "tn", "mm_e_din", out_dtype=BF16)
    dp = send_grads("c", [("e_w_in", 0, d_ein)], dp)
    dxn = _matmul(dp, w["e_w_in_t"], "nn", "mm_e_dxn")
    grad_x, gs["e_norm_g", 0] = _rms_bwd(x, w["e_norm_g"], dxn, g, "rms_e_bwd")
    return loss, grad_x, gs


CAST_ROWS = 256


def _cast_shard(w3, layer, transpose, chip, name, after=None):
    _, rows, cols = w3.shape
    tr = _tile(rows, (CAST_ROWS, 176, 128))

    def body(c_ref, w_ref, *rest):
        v = w_ref[...]
        rest[-1][...] = (v.T if transpose else v).astype(BF16)

    in_spec = pl.BlockSpec((None, tr, cols), lambda i, c: (layer, i, 0))
    if transpose:
        out_spec, shape = pl.BlockSpec((None, cols, tr), lambda i, c: (c[0], 0, i)), (cols, rows)
    else:
        out_spec, shape = pl.BlockSpec((None, tr, cols), lambda i, c: (c[0], i, 0)), (rows, cols)
    extra = [] if after is None else [after]
    grid_spec = pltpu.PrefetchScalarGridSpec(num_scalar_prefetch=1, grid=(rows // tr,),
                                             in_specs=[in_spec] + [_ANY] * len(extra), out_specs=out_spec)
    return _blocked(body, name=name, grid_spec=grid_spec,
                          out_shape=jax.ShapeDtypeStruct((N_CHIPS,) + shape, BF16),
                          compiler_params=_cparams(("parallel",), VMEM_MID))(chip, w3, *extra)


_ANY = pl.BlockSpec(memory_space=pl.ANY)


def _coords():
    return lax.axis_index("x"), lax.axis_index("y"), lax.axis_index("c")


def _flip(v, d):
    return 1 - v if d else v


_CHIP_RELS = ((1, 0), (0, 1), (1, 1))
_DEV_RELS = tuple((dx, dy, dc) for dx in (0, 1) for dy in (0, 1) for dc in (0, 1))[1:]


_HBM = pl.BlockSpec(memory_space=pltpu.HBM)
_SEM = pl.BlockSpec(memory_space=pltpu.SEMAPHORE)
_EFFECT = pltpu.SideEffectType.DATAFLOW_SIDE_EFFECTING


def _in_hbm(a):
    return pltpu.with_memory_space_constraint(a, pltpu.HBM)


def _gather_copies(bufs, send, recv, landed, halved=False):
    x, y, c = _coords()
    me = 2 * x + y
    res = []
    for i, buf in enumerate(bufs):
        half = buf.shape[1] // 2
        part = (lambda slot: buf.at[slot, pl.ds(c * half, half)]) if halved else (lambda slot: buf.at[slot])
        for j, (dx, dy) in enumerate(_CHIP_RELS):
            px, py = _flip(x, dx), _flip(y, dy)
            k = i * len(_CHIP_RELS) + j
            res.append(pltpu.make_async_remote_copy(
                src_ref=part(me), dst_ref=part(2 * px + py if landed else me), send_sem=send.at[k],
                recv_sem=recv.at[k], device_id=(px, py, c), device_id_type=MESH))
    return res


def _swap_fetched(bufs):
    n = len(bufs)
    nr = len(_CHIP_RELS)

    def body(*refs):
        outs, (send, recv) = refs[n:2 * n], refs[2 * n:]
        x, y, c = _coords()
        sib = (x, y, 1 - c)
        sends, recvs = [], []
        for i in range(n):
            half = outs[i].shape[1] // 2
            for j, (dx, dy) in enumerate(_CHIP_RELS):
                slot = 2 * _flip(x, dx) + _flip(y, dy)
                mine = outs[i].at[slot, pl.ds(c * half, half)]
                k = i * nr + j
                cp = pltpu.make_async_remote_copy(src_ref=mine, dst_ref=mine, send_sem=send.at[k], recv_sem=recv.at[k],
                                                  device_id=sib, device_id_type=MESH)
                cp.start()
                sends.append(cp)
                recvs.append(pltpu.make_async_remote_copy(
                    src_ref=mine, dst_ref=outs[i].at[slot, pl.ds((1 - c) * half, half)], send_sem=send.at[k],
                    recv_sem=recv.at[k], device_id=sib, device_id_type=MESH))
        for cp in recvs:
            cp.wait_recv()
        for cp in sends:
            cp.wait_send()

    return pl.pallas_call(
        body, name="swap_fetched", in_specs=[_ANY] * n, out_specs=[_ANY] * n,
        out_shape=[jax.ShapeDtypeStruct(a.shape, a.dtype) for a in bufs],
        input_output_aliases={i: i for i in range(n)},
        scratch_shapes=[pltpu.SemaphoreType.DMA((n * nr,)), pltpu.SemaphoreType.DMA((n * nr,))])(*bufs)


def _scatter_copies(srcs, lands, send, recv, landed):
    x, y, c = _coords()
    me = 4 * x + 2 * y + c
    res = []
    for i, (src, land) in enumerate(zip(srcs, lands)):
        for j, (dx, dy, dc) in enumerate(_DEV_RELS):
            peer = (_flip(x, dx), _flip(y, dy), _flip(c, dc))
            pid = 4 * peer[0] + 2 * peer[1] + peer[2]
            k = i * len(_DEV_RELS) + j
            res.append(pltpu.make_async_remote_copy(
                src_ref=src.at[pid], dst_ref=land.at[pid if landed else me], send_sem=send.at[k],
                recv_sem=recv.at[k], device_id=peer, device_id_type=MESH))
    return res


def _split_start(bufs, n_src, copies, n_rel, name, after):
    n = len(bufs)
    nk = n_src * n_rel

    def body(*refs):
        ins, send, recv, token = refs[:n], refs[n + 1 + n], refs[n + 2 + n], refs[-1]
        for cp in copies(ins, send, recv, False):
            cp.start()
        token[...] = jnp.zeros_like(token)

    res = pl.pallas_call(
        body, name=name, in_specs=[_HBM] * n + [_ANY],
        out_specs=[_HBM] * n + [_SEM, _SEM, pl.BlockSpec(memory_space=pltpu.VMEM)],
        out_shape=[pltpu.HBM(b.shape, b.dtype) for b in bufs]
        + [pltpu.SemaphoreType.DMA((nk,)), pltpu.SemaphoreType.DMA((nk,)), jax.ShapeDtypeStruct((8, LANES), F32)],
        input_output_aliases={i: i for i in range(n)},
        compiler_params=pltpu.CompilerParams(has_side_effects=_EFFECT))(*[_in_hbm(b) for b in bufs], after)
    return res[n], res[n + 1], list(res[:n]), res[n + 2]


def _split_wait(bufs, send, recv, copies, name, after):
    n = len(bufs)

    def body(*refs):
        ins, send_ref, recv_ref = refs[:n], refs[n], refs[n + 1]
        for cp in copies(ins, send_ref, recv_ref, True):
            cp.wait_send()
            cp.wait_recv()

    return pl.pallas_call(
        body, name=name, in_specs=[_HBM] * n + [_SEM, _SEM, _ANY], out_specs=[_HBM] * n,
        out_shape=[pltpu.HBM(b.shape, b.dtype) for b in bufs], input_output_aliases={i: i for i in range(n)},
        compiler_params=pltpu.CompilerParams(has_side_effects=_EFFECT))(*bufs, send, recv, after)


def _gather_start(bufs, name, after, halved=False):
    fn = functools.partial(_gather_copies, halved=halved)
    return _split_start(bufs, len(bufs), fn, len(_CHIP_RELS), name, after)


def _gather_wait(bufs, send, recv, name, after, halved=False):
    return _split_wait(bufs, send, recv, functools.partial(_gather_copies, halved=halved), name, after)


def _scatter_start(srcs, name, after):
    n = len(srcs)
    lands = [lax.empty(a.shape, a.dtype) for a in srcs]
    fn = lambda refs, send, recv, landed: _scatter_copies(refs[:n], refs[n:], send, recv, landed)
    send, recv, bufs, token = _split_start(list(srcs) + lands, n, fn, len(_DEV_RELS), name, after)
    return send, recv, bufs, token


def _scatter_wait(bufs, send, recv, name, after):
    n = len(bufs) // 2
    fn = lambda refs, s, r, landed: _scatter_copies(refs[:n], refs[n:], s, r, landed)
    res = _split_wait(bufs, send, recv, fn, name, after)
    return res[:n], res[n:]


def _sum_segments(src, land, me, name):
    nd, seg, cols = src.shape
    ts = _tile(seg, (256, 176, 128))

    def body(m_ref, *refs):
        o_ref = refs[-1]
        acc = refs[0][...].astype(F32)
        for r in refs[1:-1]:
            acc = acc + r[...].astype(F32)
        o_ref[...] = acc

    def peer(rel):
        bits = 4 * rel[0] + 2 * rel[1] + rel[2]
        return pl.BlockSpec((None, ts, cols), lambda i, m: (jnp.bitwise_xor(m[0], bits), i, 0))

    grid_spec = pltpu.PrefetchScalarGridSpec(
        num_scalar_prefetch=1, grid=(seg // ts,),
        in_specs=[pl.BlockSpec((None, ts, cols), lambda i, m: (m[0], i, 0))] + [peer(r) for r in _DEV_RELS],
        out_specs=pl.BlockSpec((None, ts, cols), lambda i, m: (m[1], i, 0)))
    return _blocked(body, name=name, grid_spec=grid_spec,
                          out_shape=jax.ShapeDtypeStruct((2, seg, cols), F32),
                          compiler_params=_cparams(("parallel",), VMEM_MID))(me, src, *[land] * len(_DEV_RELS))


def _exchange_sibling(arrs):
    n = len(arrs)

    def body(*refs):
        outs, (send, recv) = refs[n:2 * n], refs[2 * n:]
        x, y, c = _coords()
        sib = (x, y, 1 - c)
        sends, recvs = [], []
        for i in range(n):
            cp = pltpu.make_async_remote_copy(src_ref=outs[i].at[c], dst_ref=outs[i].at[c], send_sem=send.at[i],
                                              recv_sem=recv.at[i], device_id=sib, device_id_type=MESH)
            cp.start()
            sends.append(cp)
            recvs.append(pltpu.make_async_remote_copy(src_ref=outs[i].at[c], dst_ref=outs[i].at[1 - c],
                                                      send_sem=send.at[i], recv_sem=recv.at[i], device_id=sib,
                                                      device_id_type=MESH))
        for cp in recvs:
            cp.wait_recv()
        for cp in sends:
            cp.wait_send()

    return pl.pallas_call(
        body, name="exchange_sibling", in_specs=[_ANY] * n, out_specs=[_ANY] * n,
        out_shape=[jax.ShapeDtypeStruct(a.shape, a.dtype) for a in arrs],
        input_output_aliases={i: i for i in range(n)},
        scratch_shapes=[pltpu.SemaphoreType.DMA((n,)), pltpu.SemaphoreType.DMA((n,))])(*arrs)


def _allreduce_small(vec):
    _, nchips, seg, lanes = vec.shape
    nr = len(_CHIP_RELS)

    def body(in_ref, out_ref, from_sib, half, stage, red, send, recv):
        x, y, c = _coords()
        me = 2 * x + y
        sib = (x, y, 1 - c)
        chips = [(_flip(x, dx), _flip(y, dy)) for dx, dy in _CHIP_RELS]

        def copy(src, dst, k, peer):
            return pltpu.make_async_remote_copy(src_ref=src, dst_ref=dst, send_sem=send.at[k], recv_sem=recv.at[k],
                                                device_id=peer, device_id_type=MESH)

        to_sib = copy(in_ref.at[1 - c], from_sib, 0, sib)
        to_sib.start()
        to_sib.wait_recv()
        half[...] = in_ref[c] + from_sib[...]

        first = [copy(half.at[2 * px + py], stage.at[me], 1 + j, (px, py, c)) for j, (px, py) in enumerate(chips)]
        for cp in first:
            cp.start()
        stage[me] = half[me]
        for j, (px, py) in enumerate(chips):
            copy(half.at[2 * px + py], stage.at[2 * px + py], 1 + j, (px, py, c)).wait_recv()
        acc = stage[0]
        for k in range(1, nchips):
            acc = acc + stage[k]
        red[...] = acc
        out_ref[c, me] = acc

        second = [copy(red, out_ref.at[c, me], 1 + nr + j, (px, py, c)) for j, (px, py) in enumerate(chips)]
        for cp in second:
            cp.start()
        for j, (px, py) in enumerate(chips):
            copy(red, out_ref.at[c, 2 * px + py], 1 + nr + j, (px, py, c)).wait_recv()

        back = copy(out_ref.at[c], out_ref.at[c], 1 + 2 * nr, sib)
        back.start()
        copy(out_ref.at[c], out_ref.at[1 - c], 1 + 2 * nr, sib).wait_recv()
        for cp in [to_sib] + first + second + [back]:
            cp.wait_send()

    vm = pl.BlockSpec(memory_space=pltpu.VMEM)
    nsem = 2 + 2 * nr
    return pl.pallas_call(
        body, name="allreduce_small", in_specs=[vm], out_specs=vm,
        out_shape=jax.ShapeDtypeStruct(vec.shape, F32),
        scratch_shapes=[pltpu.VMEM((nchips, seg, lanes), F32), pltpu.VMEM((nchips, seg, lanes), F32),
                        pltpu.VMEM((nchips, seg, lanes), F32), pltpu.VMEM((seg, lanes), F32),
                        pltpu.SemaphoreType.DMA((nsem,)), pltpu.SemaphoreType.DMA((nsem,))],
        compiler_params=_cparams(None, VMEM_MID))(vec)


def _adam_math(w, g, m, v):
    m2 = ADAM_B1 * m + (1.0 - ADAM_B1) * g
    v2 = ADAM_B2 * v + (1.0 - ADAM_B2) * (g * g)
    m_hat = m2 / (1.0 - ADAM_B1 ** ADAM_STEP)
    v_hat = v2 / (1.0 - ADAM_B2 ** ADAM_STEP)
    return -ADAM_LR * (m_hat / (jnp.sqrt(v_hat) + ADAM_EPS) + ADAM_WD * w), m2, v2


def _adamw_big(w3, m3, v3, layer, g, transposed, name, prev=None):
    nl, rows, cols = w3.shape
    tr = 128 if transposed else _tile(rows, (256, 176, 128))

    def body(w_ref, m_ref, v_ref, g_ref, *rest):
        go_ref, d_ref, mo_ref, vo_ref = rest[-4:]
        g_val = g_ref[...].T if transposed else g_ref[...]
        go_ref[...] = g_val
        d_ref[...], mo_ref[...], vo_ref[...] = _adam_math(w_ref[...], g_val, m_ref[...], v_ref[...])

    wspec = pl.BlockSpec((None, tr, cols), lambda i: (layer, i, 0))
    gspec = pl.BlockSpec((cols, tr), lambda i: (0, i)) if transposed else pl.BlockSpec((tr, cols), lambda i: (i, 0))
    extra = [] if prev is None else list(prev)
    return _blocked(body, name=name, grid=(rows // tr,),
                          in_specs=[wspec, wspec, wspec, gspec] + [_ANY] * len(extra),
                          out_specs=[wspec] * 4, out_shape=[jax.ShapeDtypeStruct((nl, rows, cols), F32)] * 4,
                          input_output_aliases={4 + i: i for i in range(len(extra))},
                          compiler_params=_cparams(("parallel",), VMEM_MID))(w3, m3, v3, g, *extra)


_SMALL = (
    ("e_norm_g", (1, D_MODEL), None), ("e_mu", (1, SHIFT_COLS), None), ("e_w0", (1, RW), None),
    ("e_w2", (W_LORA, RW), RW // 4), ("e_a0", (1, RW), None), ("e_a2", (A_LORA, RW), RW // 4),
    ("e_g2", (G_LORA, RW), RW // 4), ("e_k_k", (1, RW), None), ("e_k_a", (1, RW), None), ("e_r_k", (1, RW), None),
    ("e_ln_w", (1, RW), None), ("e_ln_b", (1, RW), None), ("e_conv_w", (4, LRU_W), LRU_W // 4),
    ("e_conv_b", (1, LRU_W), None), ("e_gate_a_w", (LRU_W, HEAD), None), ("e_gate_a_b", (1, LRU_W), None),
    ("e_gate_x_w", (LRU_W, HEAD), None), ("e_gate_x_b", (1, LRU_W), None), ("e_lru_lambda", (1, LRU_W), None),
    ("o_norm_g", (1, D_MODEL), D_MODEL // 4), ("o_A_re", (S5_GROUPS, S5_STATE), None),
    ("o_A_im", (S5_GROUPS, S5_STATE), None), ("o_log_dt", (1, S5_GROUPS), None),
    ("o_B_re", (S5_GROUPS, S5_STATE * S5_GROUP), None), ("o_B_im", (S5_GROUPS, S5_STATE * S5_GROUP), None),
    ("o_C_re", (S5_GROUPS * S5_GROUP, S5_STATE), None), ("o_C_im", (S5_GROUPS * S5_GROUP, S5_STATE), None),
    ("o_D", (1, D_MODEL), D_MODEL // 4), ("f_norm_g", (2, D_MODEL), None),
    ("f_conv_w", (6, 2 * D_FF), 2 * D_FF // 4), ("f_conv_b", (2, 2 * D_FF), None),
    ("final_norm_g", (1, D_MODEL), None))
_PIECES = {"f_norm_g": ((0, 1), (1, 1)), "f_conv_b": ((0, 1), (1, 1)), "f_conv_w": ((0, 3), (3, 3))}


def _ceil_to(n, m):
    return -(-n // m) * m


def _small_layout():
    groups = {}
    for name, (rows, cols), _ in _SMALL:
        for first, r in _PIECES.get(name, ((0, rows),)):
            groups.setdefault(cols, []).append((name, first, r))
    layout, off = {}, 0
    for cols, items in groups.items():
        stacks = [0, 0] if 2 * cols <= LANES else [0]
        placed = []
        for name, first, r in sorted(items, key=lambda it: -it[2]):
            half = stacks.index(min(stacks))
            r0 = stacks[half]
            if r >= 8 or r0 % 8 + r > 8:
                r0 = _ceil_to(r0, 8)
            placed.append((name, first, r, r0, half * (LANES // 2)))
            stacks[half] = r0 + r
        rpad = _ceil_to(max(stacks), 8)
        for name, first, r, at, lane in placed:
            layout[name, first] = (off, rpad, at, r, cols, lane)
        off += -(-cols // LANES) * rpad
    return layout, _ceil_to(off, 8 * N_DEV)


def _small_pack(gs):
    layout, total = _small_layout()
    keys = list(layout)

    def body(*refs):
        out = refs[-1]
        out[...] = jnp.zeros_like(out)
        for key, g_ref in zip(keys, refs[:-1]):
            off, rpad, at, r, cols, lane = layout[key]
            for j in range(-(-cols // LANES)):
                cw = min(LANES, cols - j * LANES)
                out[off + j * rpad + at:off + j * rpad + at + r, lane:lane + cw] = g_ref[:, j * LANES:j * LANES + cw]

    return pl.pallas_call(body, name="small_pack", out_shape=jax.ShapeDtypeStruct((total, LANES), F32),
                          compiler_params=_cparams(None, VMEM_MID))(*[gs[k] for k in keys])


def _adamw_small(red, chip, wts, ms, vs):
    layout, _ = _small_layout()
    names = [n for n, _, _ in _SMALL]
    n = len(names)

    def body(chip_ref, red_ref, *refs):
        ins, outs = refs[:3 * n], refs[3 * n:]
        c = chip_ref[0]
        for i, (name, (rows, cols), loc) in enumerate(_SMALL):
            w_ref, m_ref, v_ref = ins[3 * i:3 * i + 3]
            o_refs = outs[4 * i:4 * i + 4]
            width = cols if loc is None else loc
            for first, r in _PIECES.get(name, ((0, rows),)):
                off, rpad, at, _, _, lane = layout[name, first]
                for j in range(-(-width // LANES)):
                    cw = min(LANES, width - j * LANES)
                    ls = slice(lane, lane + cw)
                    if loc is None:
                        start = off + j * rpad + at
                        g = red_ref[start:start + r, ls]
                    else:
                        blk = c * (loc // LANES) + j
                        if r >= 8:
                            g = red_ref[pl.ds(pl.multiple_of(off + at + blk * rpad, 8), r), ls]
                        else:
                            tile = red_ref[pl.ds(pl.multiple_of(off + at // 8 * 8 + blk * rpad, 8), 8), ls]
                            g = tile[at % 8:at % 8 + r]
                    rs, cs = slice(first, first + r), slice(j * LANES, j * LANES + cw)
                    d, m2, v2 = _adam_math(w_ref[rs, cs], g, m_ref[rs, cs], v_ref[rs, cs])
                    for o, val in zip(o_refs, (g, d, m2, v2)):
                        o[rs, cs] = val

    args, shapes = [], []
    for name in names:
        args += [wts[name], ms[name], vs[name]]
        shapes += [jax.ShapeDtypeStruct(wts[name].shape, F32)] * 4
    vm = pl.BlockSpec(memory_space=pltpu.VMEM)
    res = pl.pallas_call(body, name="adamw_small",
                         in_specs=[pl.BlockSpec(memory_space=pltpu.SMEM), vm] + [vm] * (3 * n),
                         out_specs=[vm] * (4 * n), out_shape=shapes,
                         compiler_params=_cparams(None, VMEM_BIG))(chip, red, *args)
    return {name: res[4 * i:4 * i + 4] for i, name in enumerate(names)}


PACK_ROWS = 8


def _packed_rows(shape):
    size = 1
    for d in shape:
        size *= d
    return -(-size // (PACK_ROWS * LANES)) * PACK_ROWS


def _pack(arrs, row_mult):
    parts = []
    for a in arrs:
        flat = a.reshape(-1).astype(F32)
        rows = _packed_rows(a.shape)
        parts.append(jnp.pad(flat, (0, rows * LANES - flat.shape[0])).reshape(rows, LANES))
    total = sum(p.shape[0] for p in parts)
    fill = -(-total // row_mult) * row_mult - total
    if fill:
        parts.append(jnp.zeros((fill, LANES), F32))
    return jnp.concatenate(parts, axis=0)


def _unpack(packed, shapes):
    out, off = [], 0
    for s in shapes:
        rows = _packed_rows(s)
        size = 1
        for d in s:
            size *= d
        out.append(packed[off:off + rows].reshape(-1)[:size].reshape(s))
        off += rows
    return out


_SMALL_SH = ("e_w2", "e_a2", "e_g2", "e_conv_w", "o_norm_g", "o_D", "f_conv_w")
_LARGE = (("e_w_in", True), ("e_w_out", False), ("o_w_in", False), ("o_w_glu", True), ("f_w_up", True),
        ("f_w_down", False))
_ORDER = ("e_norm_g", "e_w_in", "e_mu", "e_w0", "e_w2", "e_a0", "e_a2", "e_g2", "e_k_k", "e_k_a", "e_r_k", "e_ln_w",
          "e_ln_b", "e_conv_w", "e_conv_b", "e_gate_a_w", "e_gate_a_b", "e_gate_x_w", "e_gate_x_b", "e_lru_lambda",
          "e_w_out", "o_norm_g", "o_w_in", "o_A_re", "o_A_im", "o_log_dt", "o_B_re", "o_B_im", "o_C_re", "o_C_im",
          "o_D", "o_w_glu", "f_norm_g", "f_w_up", "f_conv_w", "f_conv_b", "f_w_down", "final_norm_g")
N_CHIPS = 4
N_DEV = 8


def _step(x, tgt, wts, ms, vs):
    xi, yi, ci = _coords()
    chip = 2 * xi + yi
    chip1 = chip.astype(jnp.int32).reshape(1)
    me2 = jnp.stack([4 * xi + 2 * yi + ci, ci]).astype(jnp.int32)
    by_cols = dict(_LARGE)

    cast = lambda name, l, after=None: _cast_shard(wts[name], l, by_cols[name], chip1, f"cast_{name}{l}", after)
    sh_shapes = [wts[n].shape for n in _SMALL_SH]
    packed = _pack([wts[n] for n in _SMALL_SH], 16)
    small_buf = lax.dynamic_update_slice(jnp.zeros((N_CHIPS,) + packed.shape, F32), packed[None], (chip, 0, 0))
    late = [(name, l) for name, _ in _LARGE if name != "e_w_in" for l in range(wts[name].shape[0])]
    send, recv, thru, token = _gather_start([cast("e_w_in", 0), small_buf], "gather_start_a", x, halved=True)
    bufs = {(name, l): cast(name, l, token) for name, l in late}
    got = _swap_fetched(_gather_wait(thru, send, recv, "gather_wait_a", bufs[late[-1]], halved=True))
    send_b, recv_b, thru_b, token = _gather_start([bufs[k] for k in late], "gather_start_b", got[0])
    x, _ = lax.optimization_barrier((x, token))

    def rows(g):
        return g.reshape(N_CHIPS * g.shape[1], g.shape[2])

    full = {n: wts[n] for n, _, loc in _SMALL if loc is None}
    full["e_w_in_t"] = rows(got[0])
    per_chip = [_unpack(got[1][k], sh_shapes) for k in range(N_CHIPS)]
    for i, n in enumerate(_SMALL_SH):
        full[n] = jnp.concatenate([per_chip[k][i] for k in range(N_CHIPS)], axis=-1)

    def late_weights(after):
        res = dict(zip(late, _gather_wait(thru_b, send_b, recv_b, "gather_wait_b", after)))
        return {"e_w_out": rows(res[("e_w_out", 0)]), "o_w_in": rows(res[("o_w_in", 0)]),
                "o_w_glu_t": rows(res[("o_w_glu", 0)]),
                "f_w_up_t": [rows(res[("f_w_up", l)]) for l in range(2)],
                "f_w_down": [rows(res[("f_w_down", l)]) for l in range(2)]}

    pending = []

    def send_grads(tag, items, carry):
        srcs = [g.reshape(N_DEV, g.shape[0] // N_DEV, g.shape[1]) for _, _, g in items]
        s_sem, r_sem, both, tok = _scatter_start(srcs, f"scatter_start_{tag}", carry)
        pending.append((tag, [(name, l) for name, l, _ in items], s_sem, r_sem, both))
        carry, _ = lax.optimization_barrier((carry, tok))
        return carry

    loss, grad_x, gs = _local_step(x, tgt, full, late_weights, send_grads)

    final = {}
    red = _allreduce_small(_small_pack(gs).reshape(2, N_CHIPS, -1, LANES)).reshape(-1, LANES)
    view = {name: (rows, cols if loc is None else loc) for name, (rows, cols), loc in _SMALL}
    as2d = lambda d: {name: d[name].reshape(view[name]) for name in view}
    small = _adamw_small(red, chip1, as2d(wts), as2d(ms), as2d(vs))
    for name, res in small.items():
        final[name] = [r.reshape(wts[name].shape) for r in res]
    new_v = small["final_norm_g"][3]

    halves, keys = [], []
    for tag, names, s_sem, r_sem, both in pending:
        srcs, lands = _scatter_wait(both, s_sem, r_sem, f"scatter_wait_{tag}", new_v)
        for (name, l), src, land in zip(names, srcs, lands):
            halves.append(_sum_segments(src, land, me2, f"sum_{name}{l}"))
            keys.append((name, l))
    shards = _exchange_sibling(halves)
    for s, (name, l) in zip(shards, keys):
        final[name] = _adamw_big(wts[name], ms[name], vs[name], l, s.reshape(2 * s.shape[1], s.shape[2]),
                                 by_cols[name], f"adamw_{name}{l}", prev=final.get(name))

    loss = lax.psum(loss[0, 0], ("x", "y", "c"))
    res = [loss, grad_x[None]]
    for k in range(4):
        res += [final[n][k] for n in _ORDER]
    return tuple(res)


def kernel(x, e_norm_g, e_w_in, e_mu, e_w0, e_w2, e_a0, e_a2, e_g2, e_k_k, e_k_a, e_r_k, e_ln_w, e_ln_b, e_conv_w, e_conv_b, e_gate_a_w, e_gate_a_b, e_gate_x_w, e_gate_x_b, e_lru_lambda, e_w_out, o_norm_g, o_w_in, o_A_re, o_A_im, o_log_dt, o_B_re, o_B_im, o_C_re, o_C_im, o_D, o_w_glu, f_norm_g, f_w_up, f_conv_w, f_conv_b, f_w_down, final_norm_g, loss_target, m_e_norm_g, m_e_w_in, m_e_mu, m_e_w0, m_e_w2, m_e_a0, m_e_a2, m_e_g2, m_e_k_k, m_e_k_a, m_e_r_k, m_e_ln_w, m_e_ln_b, m_e_conv_w, m_e_conv_b, m_e_gate_a_w, m_e_gate_a_b, m_e_gate_x_w, m_e_gate_x_b, m_e_lru_lambda, m_e_w_out, m_o_norm_g, m_o_w_in, m_o_A_re, m_o_A_im, m_o_log_dt, m_o_B_re, m_o_B_im, m_o_C_re, m_o_C_im, m_o_D, m_o_w_glu, m_f_norm_g, m_f_w_up, m_f_conv_w, m_f_conv_b, m_f_w_down, m_final_norm_g, v_e_norm_g, v_e_w_in, v_e_mu, v_e_w0, v_e_w2, v_e_a0, v_e_a2, v_e_g2, v_e_k_k, v_e_k_a, v_e_r_k, v_e_ln_w, v_e_ln_b, v_e_conv_w, v_e_conv_b, v_e_gate_a_w, v_e_gate_a_b, v_e_gate_x_w, v_e_gate_x_b, v_e_lru_lambda, v_e_w_out, v_o_norm_g, v_o_w_in, v_o_A_re, v_o_A_im, v_o_log_dt, v_o_B_re, v_o_B_im, v_o_C_re, v_o_C_im, v_o_D, v_o_w_glu, v_f_norm_g, v_f_w_up, v_f_conv_w, v_f_conv_b, v_f_w_down, v_final_norm_g):
    args = locals()
    wts = {n: args[n] for n in _ORDER}
    ms = {n: args["m_" + n] for n in _ORDER}
    vs = {n: args["v_" + n] for n in _ORDER}
    return _step(x[0], loss_target[0], wts, ms, vs)
```

```python
import functools

import jax
import jax.numpy as jnp
from jax import lax
from jax.experimental import pallas as pl
from jax.experimental.pallas import tpu as pltpu

F32 = jnp.float32
BF16 = jnp.bfloat16
MESH = pl.DeviceIdType.MESH

D_MODEL = 1024
HEAD = 64
RW = 512
N_HEADS = RW // HEAD
LRU_W = 512
SHIFT_COLS = 1792
W_LORA, A_LORA, G_LORA = 64, 64, 128
S5_GROUPS, S5_GROUP, S5_STATE = 64, 16, 64
D_FF = 2816
NORM_EPS = 1e-6
GN_EPS = 64e-5
LRU_C = 8.0
ADAM_LR, ADAM_B1, ADAM_B2, ADAM_EPS, ADAM_WD, ADAM_STEP = 0.001, 0.9, 0.999, 1e-08, 0.01, 10

VMEM_BIG = 56 * 1024 * 1024
VMEM_MID = 40 * 1024 * 1024
LANES = 128
PT = 16
WKV_CHUNK = 32
S5_SLAB = 128


def _blocked(*args, **kw):
    call = pl.pallas_call(*args, **kw)

    def run(*ops):
        return call(*[pltpu.with_memory_space_constraint(a, pltpu.HBM) if a.ndim >= 2 else a for a in ops])

    return run


def _cparams(sem=None, vmem=None):
    kw = {}
    if sem is not None:
        kw["dimension_semantics"] = sem
    if vmem is not None:
        kw["vmem_limit_bytes"] = vmem
    return pltpu.CompilerParams(**kw)


def _tile(dim, cands):
    for c in cands:
        if dim % c == 0:
            return c
    return dim


def _full(shape):
    n = len(shape)
    return pl.BlockSpec(shape, lambda *_: (0,) * n)


_TILES = (2816, 2048, 1408, 1024, 512, 256, 128)
MM_BUDGET = 36 * 1024 * 1024
VMEM_SLACK = 12 * 1024 * 1024


MXU_FLOPS = 9.0e14
HBM_BYTES = 3.3e12
STEP_SECONDS = 0.35e-6


def _mm_tiles(m, n, k, size_a, size_b, size_o, has_add, parts=1, tk_only=None, tm_max=None):
    best = None
    for tm in _TILES:
        for tk in _TILES:
            for tn in _TILES:
                if m % tm or n % tn or k % tk or (tk_only and tk != tk_only) or (tm_max and tm_max % tm):
                    continue
                need = (2 * (parts * tm * tk * size_a + tk * tn * size_b + tm * tn * size_o)
                        + tm * tn * 4 * (1 + 2 * has_add))
                if k > tk:
                    need += tm * tn * 4
                if need > MM_BUDGET:
                    continue
                steps = (m // tm) * (n // tn) * (k // tk)
                a_reads = n // tn if k > tk else 1
                moved = (m * k * size_a * a_reads + k * n * size_b * (m // tm) + m * n * (size_o + 4 * has_add))
                cost = max(2.0 * m * n * k / MXU_FLOPS, moved / HBM_BYTES) + steps * STEP_SECONDS
                cand = (-cost, tk, tm, tn)
                if best is None or cand > best[0]:
                    best = (cand, need)
    (_, tk, tm, tn), need = best
    return tm, tn, tk, need


def _matmul(a, b, mode, name, out_dtype=F32, add=None):
    parts = a if isinstance(a, tuple) else (a,)
    na = len(parts)
    wide = parts[0].shape[1]
    if mode == "nn":
        (m, k), (k2, n) = (parts[0].shape[0], na * wide), b.shape
    elif mode == "nt":
        (m, k), (n, k2) = (parts[0].shape[0], na * wide), b.shape
    else:
        (k, m), (k2, n) = (parts[0].shape[0], na * wide), b.shape
    assert k == k2, (parts[0].shape, b.shape, mode)
    split = {} if na == 1 else ({"tm_max": wide} if mode == "tn" else {"tk_only": wide})
    tm, tn, tk, need = _mm_tiles(m, n, k, parts[0].dtype.itemsize, b.dtype.itemsize, jnp.dtype(out_dtype).itemsize,
                                 add is not None, parts=na, **split)
    nk = k // tk
    per_part = wide // (tm if mode == "tn" else tk)
    dims = {"nn": (((1,), (0,)), ((), ())), "nt": (((1,), (1,)), ((), ())), "tn": (((0,), (0,)), ((), ()))}[mode]

    def body(*refs):
        a_refs, b_ref = refs[:na], refs[na]
        add_ref = refs[na + 1] if add is not None else None
        o_ref = refs[na + 2] if add is not None else refs[na + 1]
        kk = pl.program_id(2)

        def finish(r):
            if add_ref is not None:
                r = r + add_ref[...]
            o_ref[...] = r.astype(o_ref.dtype)

        def use(a_ref):
            part = lax.dot_general(a_ref[...].astype(BF16), b_ref[...].astype(BF16), dims, preferred_element_type=F32)
            if nk == 1:
                finish(part)
                return
            acc = refs[-1]

            @pl.when(kk == 0)
            def _():
                acc[...] = part

            @pl.when(kk > 0)
            def _():
                acc[...] += part

            @pl.when(kk == nk - 1)
            def _():
                finish(acc[...])

        if na == 1:
            use(a_refs[0])
        else:
            which = (pl.program_id(0) if mode == "tn" else kk) // per_part
            for p in range(na):
                pl.when(which == p)(functools.partial(use, a_refs[p]))

    def a_spec(p):
        def along(pos):
            return jnp.clip(pos - p * per_part, 0, per_part - 1) if na > 1 else pos
        if mode == "tn":
            return pl.BlockSpec((tk, tm), lambda i, j, kk: (kk, along(i)))
        return pl.BlockSpec((tm, tk), lambda i, j, kk: (i, along(kk)))

    if mode == "nn":
        b_spec = pl.BlockSpec((tk, tn), lambda i, j, kk: (kk, j))
    elif mode == "nt":
        b_spec = pl.BlockSpec((tn, tk), lambda i, j, kk: (j, kk))
    else:
        b_spec = pl.BlockSpec((tk, tn), lambda i, j, kk: (kk, j))
    o_spec = pl.BlockSpec((tm, tn), lambda i, j, kk: (i, j))
    in_specs = [a_spec(p) for p in range(na)] + [b_spec] + ([o_spec] if add is not None else [])
    args = parts + (b,) + ((add,) if add is not None else ())
    return _blocked(
        body, name=name, grid=(m // tm, n // tn, nk),
        in_specs=in_specs, out_specs=o_spec,
        out_shape=jax.ShapeDtypeStruct((m, n), out_dtype),
        scratch_shapes=[pltpu.VMEM((tm, tn), F32)] if nk > 1 else [],
        compiler_params=_cparams(("parallel", "parallel", "arbitrary"), min(VMEM_BIG, need + VMEM_SLACK)),
    )(*args)


TOK = 256
ROWS = 512


def _rms(x, g):
    return x * lax.rsqrt(jnp.mean(x * x, axis=-1, keepdims=True) + NORM_EPS) * g


def _rms_fwd(x, g, name, after=None):
    t, d = x.shape

    def body(x_ref, g_ref, *rest):
        rest[-1][...] = _rms(x_ref[...], g_ref[...]).astype(BF16)

    row = pl.BlockSpec((ROWS, d), lambda i: (i, 0))
    extra = [] if after is None else [after]
    return _blocked(body, name=name, grid=(t // ROWS,),
                          in_specs=[row, _full((1, d))] + [pl.BlockSpec(memory_space=pl.ANY)] * len(extra),
                          out_specs=row, out_shape=jax.ShapeDtypeStruct((t, d), BF16),
                          compiler_params=_cparams(("parallel",), VMEM_MID))(x, g, *extra)


def _rms_bwd(x, g, dxn, res, name):
    t, d = x.shape

    def body(x_ref, g_ref, d_ref, res_ref, dx_ref, dg_ref):
        _, vjp = jax.vjp(_rms, x_ref[...], g_ref[...])
        dx, dg = vjp(d_ref[...].astype(F32))
        dx_ref[...] = dx + res_ref[...]

        @pl.when(pl.program_id(0) == 0)
        def _():
            dg_ref[...] = jnp.zeros_like(dg_ref)

        dg_ref[...] += dg

    row = pl.BlockSpec((ROWS, d), lambda i: (i, 0))
    return _blocked(body, name=name, grid=(t // ROWS,), in_specs=[row, _full((1, d)), row, row],
                          out_specs=[row, _full((1, d))],
                          out_shape=[jax.ShapeDtypeStruct((t, d), F32), jax.ShapeDtypeStruct((1, d), F32)],
                          compiler_params=_cparams(("arbitrary",), VMEM_MID))(x, g, dxn, res)


def _loss_head(x, g, tgt):
    t, d = x.shape

    def body(x_ref, g_ref, t_ref, l_ref, dx_ref, dg_ref):
        tg = t_ref[...]

        def fn(xv, gv):
            err = _rms(xv, gv) - tg
            per_tok = jnp.mean(err * err, axis=-1, keepdims=True)
            return 0.5 * jnp.sum(per_tok, axis=0, keepdims=True)

        l, vjp = jax.vjp(fn, x_ref[...], g_ref[...])
        dx, dg = vjp(jnp.ones((1, 1), F32))
        dx_ref[...] = dx

        @pl.when(pl.program_id(0) == 0)
        def _():
            dg_ref[...] = jnp.zeros_like(dg_ref)
            l_ref[...] = jnp.zeros_like(l_ref)

        dg_ref[...] += dg
        l_ref[...] += jnp.broadcast_to(l, l_ref.shape)

    row = pl.BlockSpec((ROWS, d), lambda i: (i, 0))
    return _blocked(body, name="loss_head", grid=(t // ROWS,), in_specs=[row, _full((1, d)), row],
                          out_specs=[_full((1, LANES)), row, _full((1, d))],
                          out_shape=[jax.ShapeDtypeStruct((1, LANES), F32), jax.ShapeDtypeStruct((t, d), F32),
                                     jax.ShapeDtypeStruct((1, d), F32)],
                          compiler_params=_cparams(("arbitrary",), VMEM_MID))(x, g, tgt)


def _glu_fwd(x, z):
    t, d = x.shape

    def body(x_ref, v_ref, g_ref, o_ref):
        o_ref[...] = x_ref[...] + v_ref[...] * jax.nn.sigmoid(g_ref[...])

    row = pl.BlockSpec((ROWS, d), lambda i: (i, 0))
    gate = pl.BlockSpec((ROWS, d), lambda i: (i, 1))
    return _blocked(body, name="glu_fwd", grid=(t // ROWS,), in_specs=[row, row, gate], out_specs=row,
                          out_shape=jax.ShapeDtypeStruct((t, d), F32),
                          compiler_params=_cparams(("parallel",), VMEM_MID))(x, z, z)


def _glu_bwd(z, g):
    t, d = g.shape

    def body(v_ref, g_ref, d_ref, o_ref):
        s = jax.nn.sigmoid(g_ref[...])
        dy = d_ref[...]
        o_ref[:, :d] = (dy * s).astype(BF16)
        o_ref[:, d:] = (dy * v_ref[...] * s * (1.0 - s)).astype(BF16)

    row = pl.BlockSpec((ROWS, d), lambda i: (i, 0))
    gate = pl.BlockSpec((ROWS, d), lambda i: (i, 1))
    return _blocked(body, name="glu_bwd", grid=(t // ROWS,), in_specs=[row, gate, row],
                          out_specs=pl.BlockSpec((ROWS, 2 * d), lambda i: (i, 0)),
                          out_shape=jax.ShapeDtypeStruct((t, 2 * d), BF16),
                          compiler_params=_cparams(("parallel",), VMEM_MID))(z, z, g)


def _shift_down(x, d):
    row = lax.broadcasted_iota(jnp.int32, x.shape, 0)
    return jnp.where(row < d, 0.0, pltpu.roll(x, d, 0))


def _shift_up(x, d):
    n = x.shape[0]
    row = lax.broadcasted_iota(jnp.int32, x.shape, 0)
    return jnp.where(row >= n - d, 0.0, pltpu.roll(x, n - d, 0))


def _make_sd():
    @functools.partial(jax.custom_vjp, nondiff_argnums=(1,))
    def sd(x, d):
        return _shift_down(x, d)

    def fwd(x, d):
        return _shift_down(x, d), None

    def bwd(d, _, g):
        return (_shift_up(g, d),)

    sd.defvjp(fwd, bwd)
    return sd


def _lin_scan(a, u, reverse=False):
    n = a.shape[0]
    row = lax.broadcasted_iota(jnp.int32, a.shape, 0)
    d = 1
    while d < n:
        if reverse:
            keep = row < n - d
            a_s, u_s = pltpu.roll(a, n - d, 0), pltpu.roll(u, n - d, 0)
        else:
            keep = row >= d
            a_s, u_s = pltpu.roll(a, d, 0), pltpu.roll(u, d, 0)
        u = u + a * jnp.where(keep, u_s, 0.0)
        a = a * jnp.where(keep, a_s, 1.0)
        d *= 2
    return u


def _make_scan():
    @jax.custom_vjp
    def scan(a, u):
        return _lin_scan(a, u)

    def fwd(a, u):
        h = _lin_scan(a, u)
        return h, (a, h)

    def bwd(res, dh):
        a, h = res
        g = _lin_scan(_shift_up(a, 1), dh, reverse=True)
        return g * _shift_down(h, 1), g

    scan.defvjp(fwd, bwd)
    return scan


def _acc_out(ref, val):
    @pl.when(pl.program_id(0) == 0)
    def _():
        ref[...] = jnp.zeros_like(ref)

    ref[...] += val


FFN_CW = 128


def _ffn_fn(hg, hv, wg, wv, bg, bv, sd):
    cg = wg[0:1] * sd(hg, 2) + wg[1:2] * sd(hg, 1) + wg[2:3] * hg + bg
    cv = wv[0:1] * sd(hv, 2) + wv[1:2] * sd(hv, 1) + wv[2:3] * hv + bv
    return jax.nn.silu(cg) * cv


def _ffn_specs(t):
    nb = D_FF // FFN_CW
    col = lambda r, off: pl.BlockSpec((r, FFN_CW), lambda j: (0, j + off))
    return nb, [col(t, 0), col(t, nb), col(3, 0), col(3, nb), col(1, 0), col(1, nb)], col


def _ffn_mid_fwd(h, cw, cb, name):
    t = h.shape[0]
    nb, in_specs, col = _ffn_specs(t)

    def body(hg, hv, wg, wv, bg, bv, o_ref):
        o_ref[...] = _ffn_fn(hg[...], hv[...], wg[...], wv[...], bg[...], bv[...], _shift_down).astype(BF16)

    return _blocked(body, name=name, grid=(nb,), in_specs=in_specs, out_specs=col(t, 0),
                          out_shape=jax.ShapeDtypeStruct((t, D_FF), BF16),
                          compiler_params=_cparams(("parallel",), VMEM_MID))(h, h, cw, cw, cb, cb)


def _ffn_mid_bwd(h, cw, cb, dact, name):
    t = h.shape[0]
    nb, in_specs, col = _ffn_specs(t)

    def body(hg, hv, wg, wv, bg, bv, d_ref, dhg, dhv, dwg, dwv, dbg, dbv):
        fn = functools.partial(_ffn_fn, sd=_make_sd())
        _, vjp = jax.vjp(fn, hg[...], hv[...], wg[...], wv[...], bg[...], bv[...])
        g = vjp(d_ref[...])
        dhg[...] = g[0].astype(BF16)
        dhv[...] = g[1].astype(BF16)
        dwg[...], dwv[...], dbg[...], dbv[...] = g[2], g[3], g[4], g[5]

    big = jax.ShapeDtypeStruct((t, D_FF), BF16)
    w3 = jax.ShapeDtypeStruct((3, D_FF), F32)
    b1 = jax.ShapeDtypeStruct((1, D_FF), F32)
    return _blocked(body, name=name, grid=(nb,), in_specs=in_specs + [col(t, 0)],
                          out_specs=[col(t, 0), col(t, 0), col(3, 0), col(3, 0), col(1, 0), col(1, 0)],
                          out_shape=[big, big, w3, w3, b1, b1],
                          compiler_params=_cparams(("parallel",), VMEM_BIG))(h, h, cw, cw, cb, cb, dact)


TS_CW = 256


def _tshift_fn(p, mu, sd):
    return p + mu * (sd(p, 1) - p)


def _tshift_fwd(p, mu):
    t = p.shape[0]
    col = lambda r: pl.BlockSpec((r, TS_CW), lambda j: (0, j))

    def body(p_ref, mu_ref, o_ref):
        o_ref[...] = _tshift_fn(p_ref[...], mu_ref[...], _shift_down)

    return _blocked(body, name="tshift_fwd", grid=(SHIFT_COLS // TS_CW,), in_specs=[col(t), col(1)],
                          out_specs=col(t), out_shape=jax.ShapeDtypeStruct((t, SHIFT_COLS), F32),
                          compiler_params=_cparams(("parallel",), VMEM_MID))(p, mu)


def _tshift_bwd(p, mu, dpam):
    t = p.shape[0]
    col = lambda r: pl.BlockSpec((r, TS_CW), lambda j: (0, j))

    def body(p_ref, mu_ref, d_ref, dp_ref, dmu_ref):
        _, vjp = jax.vjp(functools.partial(_tshift_fn, sd=_make_sd()), p_ref[...], mu_ref[...])
        dp, dmu = vjp(d_ref[...])
        dp_ref[...] = dp.astype(BF16)
        dmu_ref[...] = dmu

    return _blocked(body, name="tshift_bwd", grid=(SHIFT_COLS // TS_CW,), in_specs=[col(t), col(1), col(t)],
                          out_specs=[col(t), col(1)],
                          out_shape=[jax.ShapeDtypeStruct((t, SHIFT_COLS), BF16),
                                     jax.ShapeDtypeStruct((1, SHIFT_COLS), F32)],
                          compiler_params=_cparams(("parallel",), VMEM_MID))(p, mu, dpam)


_HI = lax.Precision.HIGHEST
_O = (0, RW, 2 * RW, 3 * RW, 3 * RW + W_LORA, 3 * RW + W_LORA + A_LORA, SHIFT_COLS)


def _dot16(a, b, dims=(((1,), (0,)), ((), ()))):
    return lax.dot_general(a.astype(BF16), b.astype(BF16), dims, preferred_element_type=F32)


def _make_dot16():
    @jax.custom_vjp
    def dot(a, b):
        return _dot16(a, b)

    def fwd(a, b):
        return _dot16(a, b), (a, b)

    def bwd(res, g):
        a, b = res
        return _dot16(g, b, (((1,), (1,)), ((), ()))), _dot16(a, g, (((0,), (0,)), ((), ())))

    dot.defvjp(fwd, bwd)
    return dot


def _seg(x):
    first = lax.broadcasted_iota(jnp.int32, (x.shape[0], LANES), 1) < HEAD
    parts = []
    for p in range(x.shape[1] // LANES):
        xp = x[:, p * LANES:(p + 1) * LANES]
        s0 = jnp.sum(jnp.where(first, xp, 0.0), axis=-1, keepdims=True)
        s1 = jnp.sum(jnp.where(first, 0.0, xp), axis=-1, keepdims=True)
        parts.append(jnp.where(first, s0, s1))
    return jnp.concatenate(parts, axis=1)


def _prep_fn(r, k, v, wd, ad, gd, w0, w2, a0, a2, g2, k_k, k_a, dot):
    w_log = -jax.nn.softplus(-(w0 + dot(jnp.tanh(wd), w2))) - 0.5
    decay = jnp.exp(-jnp.exp(w_log))
    a = jax.nn.sigmoid(a0 + dot(ad, a2))
    g = dot(jax.nn.sigmoid(gd), g2)
    kk = k * k_k
    kk = kk / jnp.maximum(jnp.sqrt(_seg(kk * kk)), 1e-12)
    k2 = k * (1.0 + (a - 1.0) * k_a)
    return r, decay, k2, v, -kk, kk * a, g


_PREP_W = ("w0", "w2", "a0", "a2", "g2", "k_k", "k_a")


def _prep_wspecs(w):
    return [_full(w[n].shape) for n in _PREP_W]


def _rwkv_prep_fwd(pam, w):
    t = pam.shape[0]

    def body(p_ref, *refs):
        wr, outs = refs[:7], refs[7:]
        pieces = [p_ref[:, _O[i]:_O[i + 1]] for i in range(6)]
        res = _prep_fn(*pieces, *[x[...] for x in wr], _dot16)
        for o, val in zip(outs, res):
            o[...] = val

    row = lambda c: pl.BlockSpec((TOK, c), lambda i: (i, 0))
    return _blocked(body, name="rwkv_prep_fwd", grid=(t // TOK,),
                          in_specs=[row(SHIFT_COLS)] + _prep_wspecs(w), out_specs=[row(RW)] * 7,
                          out_shape=[jax.ShapeDtypeStruct((t, RW), F32)] * 7,
                          compiler_params=_cparams(("parallel",), VMEM_MID))(pam, *[w[n] for n in _PREP_W])


def _rwkv_prep_bwd(pam, w, cts, more):
    t = pam.shape[0]

    def body(p_ref, *refs):
        wr, ct, ex, dp_ref, dws = refs[:7], refs[7:14], refs[14:17], refs[17], refs[18:]
        pieces = [p_ref[:, _O[i]:_O[i + 1]] for i in range(6)]
        fn = lambda *a: _prep_fn(*a, _make_dot16())
        _, vjp = jax.vjp(fn, *pieces, *[x[...] for x in wr])
        c = [x[...] for x in ct]
        c[0] = c[0] + ex[0][...]
        c[2] = c[2] + ex[1][...]
        c[3] = c[3] + ex[2][...]
        g = vjp(tuple(c))
        for i in range(6):
            dp_ref[:, _O[i]:_O[i + 1]] = g[i]
        for o, val in zip(dws, g[6:]):
            _acc_out(o, val)

    row = lambda c: pl.BlockSpec((TOK, c), lambda i: (i, 0))
    return _blocked(body, name="rwkv_prep_bwd", grid=(t // TOK,),
                          in_specs=[row(SHIFT_COLS)] + _prep_wspecs(w) + [row(RW)] * 10,
                          out_specs=[row(SHIFT_COLS)] + [_full(w[n].shape) for n in _PREP_W],
                          out_shape=[jax.ShapeDtypeStruct((t, SHIFT_COLS), F32)]
                          + [jax.ShapeDtypeStruct(w[n].shape, F32) for n in _PREP_W],
                          compiler_params=_cparams(("arbitrary",), VMEM_MID))(
                              pam, *[w[n] for n in _PREP_W], *cts, *more)


def _post_fn(y, r, k2, v, g, ln_w, ln_b, r_k):
    inv = 1.0 / HEAD
    d = y - _seg(y) * inv
    yn = d * lax.rsqrt(_seg(d * d) * inv + GN_EPS) * ln_w + ln_b
    bonus = _seg(r * k2 * r_k) * v
    return (yn + bonus) * g


def _rwkv_post_fwd(y, r, k2, v, g, ln_w, ln_b, r_k):
    t = y.shape[0]

    def body(*refs):
        o_ref = refs[-1]
        o_ref[...] = _post_fn(*[x[...] for x in refs[:-1]]).astype(BF16)

    row = pl.BlockSpec((ROWS, RW), lambda i: (i, 0))
    return _blocked(body, name="rwkv_post_fwd", grid=(t // ROWS,),
                          in_specs=[row] * 5 + [_full((1, RW))] * 3, out_specs=row,
                          out_shape=jax.ShapeDtypeStruct((t, RW), BF16),
                          compiler_params=_cparams(("parallel",), VMEM_MID))(y, r, k2, v, g, ln_w, ln_b, r_k)


def _rwkv_post_bwd(y, r, k2, v, g, ln_w, ln_b, r_k, dya):
    t = y.shape[0]

    def body(*refs):
        ins, d_ref, outs = refs[:8], refs[8], refs[9:]
        _, vjp = jax.vjp(_post_fn, *[x[...] for x in ins])
        gr = vjp(d_ref[...])
        for o, val in zip(outs[:5], gr[:5]):
            o[...] = val
        for o, val in zip(outs[5:], gr[5:]):
            _acc_out(o, val)

    row = pl.BlockSpec((TOK, RW), lambda i: (i, 0))
    vec = _full((1, RW))
    return _blocked(body, name="rwkv_post_bwd", grid=(t // TOK,),
                          in_specs=[row] * 5 + [vec] * 3 + [row],
                          out_specs=[row] * 5 + [vec] * 3,
                          out_shape=[jax.ShapeDtypeStruct((t, RW), F32)] * 5 + [jax.ShapeDtypeStruct((1, RW), F32)] * 3,
                          compiler_params=_cparams(("arbitrary",), VMEM_MID))(y, r, k2, v, g, ln_w, ln_b, r_k, dya)


def _from_pt(x):
    n = x.shape[0]
    return x.reshape(n, HEAD, N_HEADS, PT).transpose(0, 3, 2, 1).reshape(n * PT, N_HEADS * HEAD)


def _lane_sum(x):
    return jnp.sum(x, axis=-1, keepdims=True)


def _pair_consts():
    lane = lax.broadcasted_iota(jnp.int32, (HEAD, LANES), 1)
    return lane, lane < HEAD


def _seg_sum_pair(x, first):
    return jnp.where(first, _lane_sum(jnp.where(first, x, 0.0)), _lane_sum(jnp.where(first, 0.0, x)))


def _to_pt(x):
    t = x.shape[0]
    return x.reshape(t // PT, PT, N_HEADS, HEAD).transpose(0, 3, 2, 1).reshape(t // PT, HEAD, N_HEADS * PT)


def _expand_cols(x, name):
    t = x.shape[0]
    chunk = 2 * WKV_CHUNK
    tiles = chunk // PT

    def body(x_ref, o_ref):
        _, first = _pair_consts()
        for tl in range(tiles):
            tile = x_ref[tl]
            for j in range(PT):
                for p in range(N_HEADS // 2):
                    src = jnp.where(first, (2 * p) * PT + j, (2 * p + 1) * PT + j)
                    o_ref[tl * PT + j, :, p * LANES:(p + 1) * LANES] = jnp.take_along_axis(tile, src, axis=1)

    return _blocked(
        body, name=name, grid=(t // chunk,),
        in_specs=[pl.BlockSpec((tiles, HEAD, LANES), lambda i: (i, 0, 0))],
        out_specs=pl.BlockSpec((chunk, HEAD, RW), lambda i: (i, 0, 0)),
        out_shape=jax.ShapeDtypeStruct((t, HEAD, RW), F32),
        compiler_params=_cparams(("parallel",), VMEM_MID))(_to_pt(x))


def _wkv_fwd(w, k, z, b, v_exp):
    t = w.shape[0]
    chunk = 2 * WKV_CHUNK
    nc = t // chunk
    pairs = N_HEADS // 2

    def body(w_ref, k_ref, z_ref, b_ref, v_ref, s_all, s_ref):
        @pl.when(pl.program_id(0) == 0)
        def _():
            s_ref[...] = jnp.zeros_like(s_ref)

        _, first = _pair_consts()

        def group(gi, carry):
            base = pl.multiple_of(gi * 8, 8)
            rows = [ref[pl.ds(base, 8), :] for ref in (w_ref, k_ref, z_ref, b_ref)]
            s = [s_ref[:, p * LANES:(p + 1) * LANES] for p in range(pairs)]
            for jj in range(8):
                for p in range(pairs):
                    cs = slice(p * LANES, (p + 1) * LANES)
                    wr, kr, zr, br = [x[jj:jj + 1, cs] for x in rows]
                    s_all[base + jj, :, cs] = s[p]
                    sa = _seg_sum_pair(s[p] * zr, first)
                    s[p] = s[p] * wr + sa * br + v_ref[base + jj, :, cs] * kr
            for p in range(pairs):
                s_ref[:, p * LANES:(p + 1) * LANES] = s[p]
            return carry

        lax.fori_loop(0, chunk // 8, group, 0)

    row = pl.BlockSpec((chunk, RW), lambda i: (i, 0))
    big = pl.BlockSpec((chunk, HEAD, RW), lambda i: (i, 0, 0))
    return _blocked(
        body, name="wkv_fwd", grid=(nc,), in_specs=[row] * 4 + [big], out_specs=[big, _full((HEAD, RW))],
        out_shape=[jax.ShapeDtypeStruct((t, HEAD, RW), F32), jax.ShapeDtypeStruct((HEAD, RW), F32)],
        compiler_params=_cparams(("arbitrary",), VMEM_BIG))(w, k, z, b, v_exp)


def _wkv_out(r, s_all, s_last):
    t = r.shape[0]
    chunk = 2 * WKV_CHUNK
    nc = t // chunk
    tiles = chunk // PT
    pairs = N_HEADS // 2

    def body(r_ref, s_ref, nxt_ref, last_ref, y_ref):
        lane, first = _pair_consts()
        after = jnp.where(pl.program_id(0) == nc - 1, last_ref[...], nxt_ref[0])
        for tl in range(tiles):
            ytile = jnp.zeros((HEAD, LANES), F32)
            for g in range(PT // 8):
                rows = r_ref[tl * PT + g * 8:tl * PT + g * 8 + 8, :]
                for jj in range(8):
                    tt = tl * PT + g * 8 + jj
                    j = g * 8 + jj
                    for p in range(pairs):
                        cs = slice(p * LANES, (p + 1) * LANES)
                        s = s_ref[tt + 1, :, cs] if tt + 1 < chunk else after[:, cs]
                        pr = s * rows[jj:jj + 1, cs]
                        y0 = _lane_sum(jnp.where(first, pr, 0.0))
                        y1 = _lane_sum(jnp.where(first, 0.0, pr))
                        ytile = jnp.where(lane == (2 * p) * PT + j, y0, ytile)
                        ytile = jnp.where(lane == (2 * p + 1) * PT + j, y1, ytile)
            y_ref[tl] = ytile

    row = pl.BlockSpec((chunk, RW), lambda i: (i, 0))
    pt = pl.BlockSpec((tiles, HEAD, LANES), lambda i: (i, 0, 0))
    big = pl.BlockSpec((chunk, HEAD, RW), lambda i: (i, 0, 0))
    nxt = pl.BlockSpec((1, HEAD, RW), lambda i: (jnp.minimum((i + 1) * chunk, t - 1), 0, 0))
    return _blocked(
        body, name="wkv_out", grid=(nc,), in_specs=[row, big, nxt, _full((HEAD, RW))], out_specs=pt,
        out_shape=jax.ShapeDtypeStruct((t // PT, HEAD, LANES), F32),
        compiler_params=_cparams(("parallel",), VMEM_MID))(r, s_all, s_all, s_last)


def _wkv_bwd(r, w, k, z, b, v_exp, s_all, dy_exp):
    t = r.shape[0]
    nc = t // WKV_CHUNK
    tiles = WKV_CHUNK // PT
    pairs = N_HEADS // 2

    def body(r_ref, w_ref, k_ref, z_ref, b_ref, v_ref, s_all_ref, dy_ref,
             dr_ref, dw_ref, dk_ref, dz_ref, db_ref, dv_ref, ds_ref):
        @pl.when(pl.program_id(0) == 0)
        def _():
            ds_ref[...] = jnp.zeros_like(ds_ref)

        lane, first = _pair_consts()
        col_sum = lambda x: jnp.sum(x, axis=0, keepdims=True)
        row8 = lax.broadcasted_iota(jnp.int32, (8, LANES), 0)
        for tl in reversed(range(tiles)):
            def group(gg, dvtile):
                gi = PT // 8 - 1 - gg
                base = pl.multiple_of(tl * PT + gi * 8, 8)
                rows = [ref[pl.ds(base, 8), :] for ref in (r_ref, w_ref, k_ref, z_ref, b_ref)]
                outs = (dr_ref, dw_ref, dk_ref, dz_ref, db_ref)
                tiles8 = {(id(o), p): jnp.zeros((8, LANES), F32) for o in outs for p in range(pairs)}
                ds = [ds_ref[:, p * LANES:(p + 1) * LANES] for p in range(pairs)]
                for jj in reversed(range(8)):
                    j = gi * 8 + jj
                    for p in range(pairs):
                        cs = slice(p * LANES, (p + 1) * LANES)

                        def put(ref, val, p=p, jj=jj):
                            tiles8[(id(ref), p)] = jnp.where(row8 == jj, val, tiles8[(id(ref), p)])

                        rr, wr, kr, zr, br = [x[jj:jj + 1, cs] for x in rows]
                        sp = s_all_ref[base + jj, :, cs]
                        vc = v_ref[base + jj, :, cs]
                        dyc = dy_ref[base + jj, :, cs]
                        sa = _seg_sum_pair(sp * zr, first)
                        st = sp * wr + sa * br + vc * kr
                        d = ds[p] + dyc * rr
                        put(dr_ref, col_sum(st * dyc))
                        dvk = d * kr
                        dv0 = _lane_sum(jnp.where(first, dvk, 0.0))
                        dv1 = _lane_sum(jnp.where(first, 0.0, dvk))
                        dvtile = jnp.where(lane == (2 * p) * PT + j, dv0, dvtile)
                        dvtile = jnp.where(lane == (2 * p + 1) * PT + j, dv1, dvtile)
                        put(dk_ref, col_sum(d * vc))
                        put(dw_ref, col_sum(sp * d))
                        u = _seg_sum_pair(d * br, first)
                        put(dz_ref, col_sum(sp * u))
                        put(db_ref, col_sum(d * sa))
                        ds[p] = d * wr + u * zr
                for p in range(pairs):
                    ds_ref[:, p * LANES:(p + 1) * LANES] = ds[p]
                for o in outs:
                    for p in range(pairs):
                        o[pl.ds(base, 8), p * LANES:(p + 1) * LANES] = tiles8[(id(o), p)]
                return dvtile

            dv_ref[tl] = lax.fori_loop(0, PT // 8, group, jnp.zeros((HEAD, LANES), F32))

    rev = lambda i: nc - 1 - i
    row = pl.BlockSpec((WKV_CHUNK, RW), lambda i: (rev(i), 0))
    pt = pl.BlockSpec((tiles, HEAD, LANES), lambda i: (rev(i), 0, 0))
    big = pl.BlockSpec((WKV_CHUNK, HEAD, RW), lambda i: (rev(i), 0, 0))
    return _blocked(
        body, name="wkv_bwd", grid=(nc,), in_specs=[row] * 5 + [big, big, big], out_specs=[row] * 5 + [pt],
        out_shape=[jax.ShapeDtypeStruct((t, RW), F32)] * 5 + [jax.ShapeDtypeStruct((t // PT, HEAD, LANES), F32)],
        scratch_shapes=[pltpu.VMEM((HEAD, RW), F32)],
        compiler_params=_cparams(("arbitrary",), VMEM_BIG))(r, w, k, z, b, v_exp, s_all, dy_exp)


LRU_CW = 128
_BX0 = SHIFT_COLS // LRU_CW
_BG0 = (SHIFT_COLS + LRU_W) // LRU_CW


def _lru_fn(bx, bg, cw, cb, ga, ba, gx, bxb, lam, sd, scan, dot):
    xc = cw[0:1] * sd(bx, 3) + cw[1:2] * sd(bx, 2) + cw[2:3] * sd(bx, 1) + cw[3:4] * bx + cb
    gr = jax.nn.sigmoid(dot(xc, ga) + ba)
    gi = jax.nn.sigmoid(dot(xc, gx) + bxb)
    log_a = -LRU_C * gr * jax.nn.softplus(-lam)
    a = jnp.exp(log_a)
    mult = jnp.sqrt(-jnp.tanh(log_a) * (jnp.exp(2.0 * log_a) + 1.0))
    return scan(a, xc * gi * mult) * jax.nn.gelu(bg)


def _lru_specs(t):
    col = lambda r, off=0: pl.BlockSpec((r, LRU_CW), lambda j: (0, j + off))
    diag = pl.BlockSpec((LRU_CW, LRU_CW), lambda j: (j, j))
    return col, [col(t, _BX0), col(t, _BG0), col(4), col(1), diag, col(1), diag, col(1), col(1)]


def _lru_fwd(p, cw, cb, ga, ba, gx, bxb, lam):
    t = p.shape[0]
    col, in_specs = _lru_specs(t)

    def body(*refs):
        o_ref = refs[-1]
        o_ref[...] = _lru_fn(*[x[...] for x in refs[:-1]], _shift_down, _lin_scan, _dot16).astype(BF16)

    return _blocked(body, name="lru_fwd", grid=(LRU_W // LRU_CW,), in_specs=in_specs, out_specs=col(t),
                          out_shape=jax.ShapeDtypeStruct((t, LRU_W), BF16),
                          compiler_params=_cparams(("parallel",), VMEM_MID))(p, p, cw, cb, ga, ba, gx, bxb, lam)


def _lru_bwd(p, cw, cb, ga, ba, gx, bxb, lam, dyb):
    t = p.shape[0]
    col, in_specs = _lru_specs(t)

    def body(*refs):
        ins, d_ref, outs = refs[:9], refs[9], refs[10:]
        fn = functools.partial(_lru_fn, sd=_make_sd(), scan=_make_scan(), dot=_make_dot16())
        _, vjp = jax.vjp(fn, *[x[...] for x in ins])
        g = vjp(d_ref[...])
        outs[0][...] = g[0].astype(BF16)
        outs[1][...] = g[1].astype(BF16)
        for o, val in zip(outs[2:], g[2:]):
            o[...] = val

    sq = pl.BlockSpec((LRU_CW, LRU_CW), lambda j: (j, 0))
    act = jax.ShapeDtypeStruct((t, LRU_W), BF16)
    vec = jax.ShapeDtypeStruct((1, LRU_W), F32)
    sqs = jax.ShapeDtypeStruct((LRU_W, LRU_CW), F32)
    return _blocked(body, name="lru_bwd", grid=(LRU_W // LRU_CW,), in_specs=in_specs + [col(t, RW // LRU_CW)],
                          out_specs=[col(t), col(t), col(4), col(1), sq, col(1), sq, col(1), col(1)],
                          out_shape=[act, act, jax.ShapeDtypeStruct((4, LRU_W), F32), vec, sqs, vec, sqs, vec, vec],
                          compiler_params=_cparams(("parallel",), VMEM_BIG))(p, p, cw, cb, ga, ba, gx, bxb, lam, dyb)


def _s5_disc_fn(a_re, a_im, log_dt, b_re, b_im, e):
    lam_re = jnp.minimum(a_re, -1e-4)
    lam_im = a_im
    dt = jnp.exp(log_dt)
    mag = jnp.exp(lam_re * dt)
    ab_re = mag * jnp.cos(lam_im * dt)
    ab_im = mag * jnp.sin(lam_im * dt)
    den = lam_re * lam_re + lam_im * lam_im
    zr = ab_re - 1.0
    q_re = jnp.dot((zr * lam_re + ab_im * lam_im) / den, e, precision=_HI)
    q_im = jnp.dot((ab_im * lam_re - zr * lam_im) / den, e, precision=_HI)
    return ab_re, ab_im, q_re * b_re - q_im * b_im, q_re * b_im + q_im * b_re


def _s5_disc_fwd(a_re, a_im, log_dt, b_re, b_im, e):
    def body(*refs):
        res = _s5_disc_fn(*[x[...] for x in refs[:6]])
        for o, val in zip(refs[6:], res):
            o[...] = val

    small = jax.ShapeDtypeStruct(a_re.shape, F32)
    wide = jax.ShapeDtypeStruct(b_re.shape, F32)
    return pl.pallas_call(body, name="s5_disc_fwd", out_shape=[small, small, wide, wide])(
        a_re, a_im, log_dt, b_re, b_im, e)


def _s5_disc_bwd(a_re, a_im, log_dt, b_re, b_im, e, cts):
    def body(*refs):
        ins, e_ref, ct, outs = refs[:5], refs[5], refs[6:10], refs[10:]
        _, vjp = jax.vjp(lambda *a: _s5_disc_fn(*a, e_ref[...]), *[x[...] for x in ins])
        for o, val in zip(outs, vjp(tuple(c[...] for c in ct))):
            o[...] = val

    shapes = [jax.ShapeDtypeStruct(x.shape, F32) for x in (a_re, a_im, log_dt, b_re, b_im)]
    return pl.pallas_call(body, name="s5_disc_bwd", out_shape=shapes)(a_re, a_im, log_dt, b_re, b_im, e, *cts)


def _cmul(a, b):
    return a[0] * b[0] - a[1] * b[1], a[0] * b[1] + a[1] * b[0]


def _s5_scan(sr, si, ab, reverse):
    n_tiles = sr.shape[0] // 8
    width = sr.shape[1]
    row8 = lax.broadcasted_iota(jnp.int32, (8, width), 0)
    p1 = ab
    p2 = _cmul(p1, p1)
    p4 = _cmul(p2, p2)
    pw = [p1]
    for _ in range(7):
        pw.append(_cmul(pw[-1], p1))
    cr = jnp.zeros((8, width), F32)
    ci = jnp.zeros((8, width), F32)
    for j in range(8):
        e = pw[7 - j] if reverse else pw[j]
        cr = jnp.where(row8 == j, e[0], cr)
        ci = jnp.where(row8 == j, e[1], ci)

    levels = []
    for d, q in ((1, p1), (2, p2), (4, p4)):
        keep = row8 < 8 - d if reverse else row8 >= d
        levels.append((d, (jnp.where(keep, q[0], 0.0), jnp.where(keep, q[1], 0.0))))

    def tile(i, carry):
        idx = n_tiles - 1 - i if reverse else i
        base = pl.multiple_of(idx * 8, 8)
        x = (sr[pl.ds(base, 8), :], si[pl.ds(base, 8), :])
        for d, q in levels:
            amt = 8 - d if reverse else d
            m = _cmul(q, (pltpu.roll(x[0], amt, 0), pltpu.roll(x[1], amt, 0)))
            x = (x[0] + m[0], x[1] + m[1])
        m = _cmul((cr, ci), carry)
        x = (x[0] + m[0], x[1] + m[1])
        sr[pl.ds(base, 8), :] = x[0]
        si[pl.ds(base, 8), :] = x[1]
        edge = slice(0, 1) if reverse else slice(7, 8)
        return x[0][edge], x[1][edge]

    zero = jnp.zeros((1, width), F32)
    lax.fori_loop(0, n_tiles, tile, (zero, zero))


_S5_W = S5_SLAB // S5_GROUP * S5_STATE


def _s5_specs(t):
    col = lambda r: pl.BlockSpec((r, S5_SLAB), lambda j: (0, j))
    bb = pl.BlockSpec((None, S5_SLAB, _S5_W), lambda j: (j, 0, 0))
    cd = pl.BlockSpec((None, _S5_W, S5_SLAB), lambda j: (j, 0, 0))
    ab = pl.BlockSpec((None, 1, _S5_W), lambda j: (j, 0, 0))
    return col, bb, cd, ab


def _s5_fwd(u, dvec, bbr, bbi, cdr, cdi, abr, abi):
    t, width = u.shape
    col, bb, cd, ab = _s5_specs(t)

    def body(u_ref, d_ref, bbr_ref, bbi_ref, cdr_ref, cdi_ref, abr_ref, abi_ref, o_ref, sr, si):
        uv = u_ref[...]
        sr[...] = _dot16(uv, bbr_ref[...])
        si[...] = _dot16(uv, bbi_ref[...])
        _s5_scan(sr, si, (abr_ref[...], abi_ref[...]), False)
        y = _dot16(sr[...], cdr_ref[...]) - _dot16(si[...], cdi_ref[...])
        o_ref[...] = jax.nn.gelu(y + d_ref[...] * uv).astype(BF16)

    return _blocked(body, name="s5_fwd", grid=(width // S5_SLAB,),
                          in_specs=[col(t), col(1), bb, bb, cd, cd, ab, ab], out_specs=col(t),
                          out_shape=jax.ShapeDtypeStruct((t, width), BF16),
                          scratch_shapes=[pltpu.VMEM((t, _S5_W), F32)] * 2,
                          compiler_params=_cparams(("parallel",), VMEM_BIG))(u, dvec, bbr, bbi, cdr, cdi, abr, abi)


def _s5_bwd(u, dvec, bbr, bbi, cdr, cdi, abr, abi, dyact):
    t, width = u.shape
    col, bb, cd, ab = _s5_specs(t)
    ns = width // S5_SLAB
    tn = (((0,), (0,)), ((), ()))
    nt = (((1,), (1,)), ((), ()))

    def body(u_ref, d_ref, bbr_ref, bbi_ref, cdr_ref, cdi_ref, abr_ref, abi_ref, dy_ref,
             du_ref, dd_ref, dbbr_ref, dbbi_ref, dcdr_ref, dcdi_ref, dabr_ref, dabi_ref, sr, si, gr, gi):
        uv = u_ref[...]
        dv = d_ref[...]
        abv = (abr_ref[...], abi_ref[...])
        sr[...] = _dot16(uv, bbr_ref[...])
        si[...] = _dot16(uv, bbi_ref[...])
        _s5_scan(sr, si, abv, False)
        y = _dot16(sr[...], cdr_ref[...]) - _dot16(si[...], cdi_ref[...])
        _, vjp = jax.vjp(jax.nn.gelu, y + dv * uv)
        (dpre,) = vjp(dy_ref[...].astype(F32))
        dd_ref[...] = jnp.sum(dpre * uv, axis=0, keepdims=True)
        dcdr_ref[...] = _dot16(sr[...], dpre, tn)
        dcdi_ref[...] = -_dot16(si[...], dpre, tn)
        gr[...] = _dot16(dpre, cdr_ref[...], nt)
        gi[...] = -_dot16(dpre, cdi_ref[...], nt)
        _s5_scan(gr, gi, (abv[0], -abv[1]), True)

        row8 = lax.broadcasted_iota(jnp.int32, (8, _S5_W), 0)

        def tile(i, carry):
            acc_r, acc_i, last_r, last_i = carry
            base = pl.multiple_of(i * 8, 8)
            s_r, s_i = sr[pl.ds(base, 8), :], si[pl.ds(base, 8), :]
            g_r, g_i = gr[pl.ds(base, 8), :], gi[pl.ds(base, 8), :]
            p_r = jnp.where(row8 == 0, last_r, pltpu.roll(s_r, 1, 0))
            p_i = jnp.where(row8 == 0, last_i, pltpu.roll(s_i, 1, 0))
            acc_r = acc_r + jnp.sum(g_r * p_r + g_i * p_i, axis=0, keepdims=True)
            acc_i = acc_i + jnp.sum(g_i * p_r - g_r * p_i, axis=0, keepdims=True)
            return acc_r, acc_i, s_r[7:8], s_i[7:8]

        zero = jnp.zeros((1, _S5_W), F32)
        acc_r, acc_i, _, _ = lax.fori_loop(0, t // 8, tile, (zero, zero, zero, zero))
        dabr_ref[...] = acc_r
        dabi_ref[...] = acc_i
        du_ref[...] = dpre * dv + _dot16(gr[...], bbr_ref[...], nt) + _dot16(gi[...], bbi_ref[...], nt)
        dbbr_ref[...] = _dot16(uv, gr[...], tn)
        dbbi_ref[...] = _dot16(uv, gi[...], tn)

    sds = jax.ShapeDtypeStruct
    return _blocked(
        body, name="s5_bwd", grid=(ns,), in_specs=[col(t), col(1), bb, bb, cd, cd, ab, ab, col(t)],
        out_specs=[col(t), col(1), bb, bb, cd, cd, ab, ab],
        out_shape=[sds((t, width), F32), sds((1, width), F32), sds((ns, S5_SLAB, _S5_W), F32),
                   sds((ns, S5_SLAB, _S5_W), F32), sds((ns, _S5_W, S5_SLAB), F32), sds((ns, _S5_W, S5_SLAB), F32),
                   sds((ns, 1, _S5_W), F32), sds((ns, 1, _S5_W), F32)],
        scratch_shapes=[pltpu.VMEM((t, _S5_W), F32)] * 4,
        compiler_params=_cparams(("parallel",), VMEM_BIG))(u, dvec, bbr, bbi, cdr, cdi, abr, abi, dyact)


def _gate_dense(w):
    h = w.shape[0]
    return jnp.einsum("hij,hg->higj", w, jnp.eye(h, dtype=F32)).reshape(h * HEAD, h * HEAD)


def _gate_blocks(d):
    x = d.reshape(LRU_W // LRU_CW, 2, HEAD, 2, HEAD)
    return jnp.einsum("tgihj,gh->tgij", x, jnp.eye(2, dtype=F32)).reshape(LRU_W // HEAD, HEAD, HEAD)


_GPS = S5_SLAB // S5_GROUP
_NS = S5_GROUPS // _GPS


def _s5_in_dense(bb):
    x = bb.reshape(_NS, _GPS, S5_STATE, S5_GROUP)
    return jnp.einsum("sgnc,gh->sgchn", x, jnp.eye(_GPS, dtype=F32)).reshape(_NS, S5_SLAB, _S5_W)


def _s5_in_blocks(d):
    x = d.reshape(_NS, _GPS, S5_GROUP, _GPS, S5_STATE)
    return jnp.einsum("sgchn,gh->sgnc", x, jnp.eye(_GPS, dtype=F32)).reshape(S5_GROUPS, S5_STATE * S5_GROUP)


def _s5_out_dense(c):
    x = c.reshape(_NS, _GPS, S5_GROUP, S5_STATE)
    return jnp.einsum("sgcn,gh->shngc", x, jnp.eye(_GPS, dtype=F32)).reshape(_NS, _S5_W, S5_SLAB)


def _s5_out_blocks(d):
    x = d.reshape(_NS, _GPS, S5_STATE, _GPS, S5_GROUP)
    return jnp.einsum("shngc,gh->sgcn", x, jnp.eye(_GPS, dtype=F32)).reshape(S5_GROUPS, S5_GROUP, S5_STATE)


def _local_step(x, tgt, w, late_weights, send_grads):
    d_model = x.shape[1]
    gs = {}
    n_layers = w["f_norm_g"].shape[0]

    def ffn_fwd(xin, l):
        xn = _rms_fwd(xin, w["f_norm_g"][l:l + 1], f"rms_f{l}")
        h = _matmul(xn, w["f_w_up_t"][l], "nt", f"mm_f{l}_up")
        act = _ffn_mid_fwd(h, w["f_conv_w"][l], w["f_conv_b"][l:l + 1], f"ffn_mid_fwd{l}")
        return _matmul(act, w["f_w_down"][l], "nn", f"mm_f{l}_down", add=xin), (xin, xn, h, act)

    def ffn_bwd(g, saved, l):
        xin, xn, h, act = saved
        dact = _matmul(g, w["f_w_down"][l], "nt", f"mm_f{l}_dact")
        d_down = _matmul(act, g, "tn", f"mm_f{l}_ddown", out_dtype=BF16)
        dhg, dhv, dwg, dwv, dbg, dbv = _ffn_mid_bwd(h, w["f_conv_w"][l], w["f_conv_b"][l:l + 1], dact,
                                                    f"ffn_mid_bwd{l}")
        dxn = _matmul((dhg, dhv), w["f_w_up_t"][l], "nn", f"mm_f{l}_dxn")
        d_up = _matmul((dhg, dhv), xn, "tn", f"mm_f{l}_dup", out_dtype=BF16)
        dx, dgn = _rms_bwd(xin, w["f_norm_g"][l:l + 1], dxn, g, f"rms_f{l}_bwd")
        return dx, d_up, d_down, jnp.concatenate([dwg, dwv], axis=1), jnp.concatenate([dbg, dbv], axis=1), dgn

    xn0 = _rms_fwd(x, w["e_norm_g"], "rms_e", after=w.get("first_gather_started"))
    p = _matmul(xn0, w["e_w_in_t"], "nt", "mm_e_in")
    pam = _tshift_fwd(p, w["e_mu"])
    pw = dict(w0=w["e_w0"], w2=w["e_w2"][0], a0=w["e_a0"], a2=w["e_a2"][0], g2=w["e_g2"][0],
              k_k=w["e_k_k"], k_a=w["e_k_a"])
    r, dec, k2, v, z, b, gate = _rwkv_prep_fwd(pam, pw)
    v_exp = _expand_cols(v, "wkv_expand_v")
    s_all, s_last = _wkv_fwd(dec, k2, z, b, v_exp)
    y_pt = _wkv_out(r, s_all, s_last)
    y = _from_pt(y_pt)
    rk = w["e_r_k"].reshape(1, RW)
    ya = _rwkv_post_fwd(y, r, k2, v, gate, w["e_ln_w"], w["e_ln_b"], rk)
    ga, gx = _gate_dense(w["e_gate_a_w"][0]), _gate_dense(w["e_gate_x_w"][0])
    lru_w = (w["e_conv_w"][0], w["e_conv_b"], ga, w["e_gate_a_b"], gx, w["e_gate_x_b"], w["e_lru_lambda"])
    yb = _lru_fwd(p, *lru_w)
    ycat = jnp.concatenate([ya, yb], axis=1)
    w = {**w, **late_weights(ycat)}
    x1 = _matmul(ycat, w["e_w_out"], "nn", "mm_e_out", add=x)
    x2, ffn0 = ffn_fwd(x1, 0)

    xn1 = _rms_fwd(x2, w["o_norm_g"], "rms_o")
    u = _matmul(xn1, w["o_w_in"], "nn", "mm_o_in")
    expand = jnp.kron(jnp.eye(S5_STATE, dtype=F32), jnp.ones((1, S5_GROUP), F32))
    disc_in = (w["o_A_re"][0], w["o_A_im"][0], w["o_log_dt"].reshape(S5_GROUPS, 1),
               w["o_B_re"][0].reshape(S5_GROUPS, -1), w["o_B_im"][0].reshape(S5_GROUPS, -1), expand)
    ab_re, ab_im, bb_re, bb_im = _s5_disc_fwd(*disc_in)
    s5_w = (w["o_D"], _s5_in_dense(bb_re), _s5_in_dense(bb_im), _s5_out_dense(w["o_C_re"][0]),
            _s5_out_dense(w["o_C_im"][0]), ab_re.reshape(_NS, 1, _S5_W), ab_im.reshape(_NS, 1, _S5_W))
    yact = _s5_fwd(u, *s5_w)
    zz = _matmul(yact, w["o_w_glu_t"], "nt", "mm_o_glu")
    x3 = _glu_fwd(x2, zz)
    x4, ffn1 = ffn_fwd(x3, 1)

    loss, g, gs["final_norm_g", 0] = _loss_head(x4, w["final_norm_g"].reshape(1, d_model), tgt)

    g, up1, down1, dcw1, dcb1, dfn1 = ffn_bwd(g, ffn1, 1)
    dz = _glu_bwd(zz, g)
    dyact = _matmul(dz, w["o_w_glu_t"], "nn", "mm_o_dyact")
    d_glu = _matmul(dz, yact, "tn", "mm_o_dglu", out_dtype=BF16)
    du, gs["o_D", 0], dbbr, dbbi, dcdr, dcdi, dabr, dabi = _s5_bwd(u, *s5_w, dyact)
    gs["o_C_re", 0] = _s5_out_blocks(dcdr).reshape(S5_GROUPS * S5_GROUP, S5_STATE)
    gs["o_C_im", 0] = _s5_out_blocks(dcdi).reshape(S5_GROUPS * S5_GROUP, S5_STATE)
    cts = (dabr.reshape(S5_GROUPS, S5_STATE), dabi.reshape(S5_GROUPS, S5_STATE), _s5_in_blocks(dbbr),
           _s5_in_blocks(dbbi))
    gs["o_A_re", 0], gs["o_A_im", 0], dlog_dt, gs["o_B_re", 0], gs["o_B_im", 0] = _s5_disc_bwd(*disc_in, cts)
    gs["o_log_dt", 0] = dlog_dt.reshape(1, S5_GROUPS)
    dxn = _matmul(du, w["o_w_in"], "nt", "mm_o_dxn")
    d_oin = _matmul(xn1, du, "tn", "mm_o_din", out_dtype=BF16)
    g, gs["o_norm_g", 0] = _rms_bwd(x2, w["o_norm_g"], dxn, g, "rms_o_bwd")
    g = send_grads("a", [("f_w_up", 1, up1), ("f_w_down", 1, down1), ("o_w_glu", 0, d_glu), ("o_w_in", 0, d_oin)], g)

    g, up0, down0, dcw0, dcb0, dfn0 = ffn_bwd(g, ffn0, 0)
    gs["f_conv_w", 0], gs["f_conv_w", 3] = dcw0, dcw1
    gs["f_conv_b", 0], gs["f_conv_b", 1] = dcb0, dcb1
    gs["f_norm_g", 0], gs["f_norm_g", 1] = dfn0, dfn1

    dycat = _matmul(g, w["e_w_out"], "nt", "mm_e_dycat")
    d_eout = _matmul(ycat, g, "tn", "mm_e_dout", out_dtype=BF16)
    dycat = send_grads("b", [("f_w_up", 0, up0), ("f_w_down", 0, down0), ("e_w_out", 0, d_eout)], dycat)
    dy, dr1, dk1, dv1, dgate, gs["e_ln_w", 0], gs["e_ln_b", 0], gs["e_r_k", 0] = _rwkv_post_bwd(
        y, r, k2, v, gate, w["e_ln_w"], w["e_ln_b"], rk, dycat)
    dr2, ddec, dk2, dzz, dbb, dv_pt = _wkv_bwd(r, dec, k2, z, b, v_exp, s_all, _expand_cols(dy, "wkv_expand_dy"))
    (dpam, gs["e_w0", 0], gs["e_w2", 0], gs["e_a0", 0], gs["e_a2", 0], gs["e_g2", 0], gs["e_k_k", 0],
     gs["e_k_a", 0]) = _rwkv_prep_bwd(pam, pw, (dr2, ddec, dk2, _from_pt(dv_pt), dzz, dbb, dgate), (dr1, dk1, dv1))
    dpa, gs["e_mu", 0] = _tshift_bwd(p, w["e_mu"], dpam)
    (dbx, dbg, gs["e_conv_w", 0], gs["e_conv_b", 0], dga, gs["e_gate_a_b", 0], dgx, gs["e_gate_x_b", 0],
     gs["e_lru_lambda", 0]) = _lru_bwd(p, *lru_w, dycat)
    gs["e_gate_a_w", 0] = _gate_blocks(dga).reshape(LRU_W, HEAD)
    gs["e_gate_x_w", 0] = _gate_blocks(dgx).reshape(LRU_W, HEAD)
    dp = jnp.concatenate([dpa, dbx, dbg], axis=1)
    d_ein = _matmul(dp, xn0, "tn", "mm_e_din", out_dtype=BF16)
    dp = send_grads("c", [("e_w_in", 0, d_ein)], dp)
    dxn = _matmul(dp, w["e_w_in_t"], "nn", "mm_e_dxn")
    grad_x, gs["e_norm_g", 0] = _rms_bwd(x, w["e_norm_g"], dxn, g, "rms_e_bwd")
    return loss, grad_x, gs


CAST_ROWS = 256


def _cast_shard(w3, layer, transpose, chip, name, after=None):
    _, rows, cols = w3.shape
    tr = _tile(rows, (CAST_ROWS, 176, 128))

    def body(c_ref, w_ref, *rest):
        v = w_ref[...]
        rest[-1][...] = (v.T if transpose else v).astype(BF16)

    in_spec = pl.BlockSpec((None, tr, cols), lambda i, c: (layer, i, 0))
    if transpose:
        out_spec, shape = pl.BlockSpec((None, cols, tr), lambda i, c: (c[0], 0, i)), (cols, rows)
    else:
        out_spec, shape = pl.BlockSpec((None, tr, cols), lambda i, c: (c[0], i, 0)), (rows, cols)
    extra = [] if after is None else [after]
    grid_spec = pltpu.PrefetchScalarGridSpec(num_scalar_prefetch=1, grid=(rows // tr,),
                                             in_specs=[in_spec] + [_ANY] * len(extra), out_specs=out_spec)
    return _blocked(body, name=name, grid_spec=grid_spec,
                          out_shape=jax.ShapeDtypeStruct((N_CHIPS,) + shape, BF16),
                          compiler_params=_cparams(("parallel",), VMEM_MID))(chip, w3, *extra)


_ANY = pl.BlockSpec(memory_space=pl.ANY)


def _coords():
    return lax.axis_index("x"), lax.axis_index("y"), lax.axis_index("c")


def _flip(v, d):
    return 1 - v if d else v


_CHIP_RELS = ((1, 0), (0, 1), (1, 1))
_DEV_RELS = tuple((dx, dy, dc) for dx in (0, 1) for dy in (0, 1) for dc in (0, 1))[1:]


_HBM = pl.BlockSpec(memory_space=pltpu.HBM)
_SEM = pl.BlockSpec(memory_space=pltpu.SEMAPHORE)
_EFFECT = pltpu.SideEffectType.DATAFLOW_SIDE_EFFECTING


def _in_hbm(a):
    return pltpu.with_memory_space_constraint(a, pltpu.HBM)


def _gather_copies(bufs, send, recv, landed, halved=False):
    x, y, c = _coords()
    me = 2 * x + y
    res = []
    for i, buf in enumerate(bufs):
        half = buf.shape[1] // 2
        part = (lambda slot: buf.at[slot, pl.ds(c * half, half)]) if halved else (lambda slot: buf.at[slot])
        for j, (dx, dy) in enumerate(_CHIP_RELS):
            px, py = _flip(x, dx), _flip(y, dy)
            k = i * len(_CHIP_RELS) + j
            res.append(pltpu.make_async_remote_copy(
                src_ref=part(me), dst_ref=part(2 * px + py if landed else me), send_sem=send.at[k],
                recv_sem=recv.at[k], device_id=(px, py, c), device_id_type=MESH))
    return res


def _swap_fetched(bufs):
    n = len(bufs)
    nr = len(_CHIP_RELS)

    def body(*refs):
        outs, (send, recv) = refs[n:2 * n], refs[2 * n:]
        x, y, c = _coords()
        sib = (x, y, 1 - c)
        sends, recvs = [], []
        for i in range(n):
            half = outs[i].shape[1] // 2
            for j, (dx, dy) in enumerate(_CHIP_RELS):
                slot = 2 * _flip(x, dx) + _flip(y, dy)
                mine = outs[i].at[slot, pl.ds(c * half, half)]
                k = i * nr + j
                cp = pltpu.make_async_remote_copy(src_ref=mine, dst_ref=mine, send_sem=send.at[k], recv_sem=recv.at[k],
                                                  device_id=sib, device_id_type=MESH)
                cp.start()
                sends.append(cp)
                recvs.append(pltpu.make_async_remote_copy(
                    src_ref=mine, dst_ref=outs[i].at[slot, pl.ds((1 - c) * half, half)], send_sem=send.at[k],
                    recv_sem=recv.at[k], device_id=sib, device_id_type=MESH))
        for cp in recvs:
            cp.wait_recv()
        for cp in sends:
            cp.wait_send()

    return pl.pallas_call(
        body, name="swap_fetched", in_specs=[_ANY] * n, out_specs=[_ANY] * n,
        out_shape=[jax.ShapeDtypeStruct(a.shape, a.dtype) for a in bufs],
        input_output_aliases={i: i for i in range(n)},
        scratch_shapes=[pltpu.SemaphoreType.DMA((n * nr,)), pltpu.SemaphoreType.DMA((n * nr,))])(*bufs)


def _scatter_copies(srcs, lands, send, recv, landed):
    x, y, c = _coords()
    me = 4 * x + 2 * y + c
    res = []
    for i, (src, land) in enumerate(zip(srcs, lands)):
        for j, (dx, dy, dc) in enumerate(_DEV_RELS):
            peer = (_flip(x, dx), _flip(y, dy), _flip(c, dc))
            pid = 4 * peer[0] + 2 * peer[1] + peer[2]
            k = i * len(_DEV_RELS) + j
            res.append(pltpu.make_async_remote_copy(
                src_ref=src.at[pid], dst_ref=land.at[pid if landed else me], send_sem=send.at[k],
                recv_sem=recv.at[k], device_id=peer, device_id_type=MESH))
    return res


def _split_start(bufs, n_src, copies, n_rel, name, after):
    n = len(bufs)
    nk = n_src * n_rel

    def body(*refs):
        ins, send, recv, token = refs[:n], refs[n + 1 + n], refs[n + 2 + n], refs[-1]
        for cp in copies(ins, send, recv, False):
            cp.start()
        token[...] = jnp.zeros_like(token)

    res = pl.pallas_call(
        body, name=name, in_specs=[_HBM] * n + [_ANY],
        out_specs=[_HBM] * n + [_SEM, _SEM, pl.BlockSpec(memory_space=pltpu.VMEM)],
        out_shape=[pltpu.HBM(b.shape, b.dtype) for b in bufs]
        + [pltpu.SemaphoreType.DMA((nk,)), pltpu.SemaphoreType.DMA((nk,)), jax.ShapeDtypeStruct((8, LANES), F32)],
        input_output_aliases={i: i for i in range(n)},
        compiler_params=pltpu.CompilerParams(has_side_effects=_EFFECT))(*[_in_hbm(b) for b in bufs], after)
    return res[n], res[n + 1], list(res[:n]), res[n + 2]


def _split_wait(bufs, send, recv, copies, name, after):
    n = len(bufs)

    def body(*refs):
        ins, send_ref, recv_ref = refs[:n], refs[n], refs[n + 1]
        for cp in copies(ins, send_ref, recv_ref, True):
            cp.wait_send()
            cp.wait_recv()

    return pl.pallas_call(
        body, name=name, in_specs=[_HBM] * n + [_SEM, _SEM, _ANY], out_specs=[_HBM] * n,
        out_shape=[pltpu.HBM(b.shape, b.dtype) for b in bufs], input_output_aliases={i: i for i in range(n)},
        compiler_params=pltpu.CompilerParams(has_side_effects=_EFFECT))(*bufs, send, recv, after)


def _gather_start(bufs, name, after, halved=False):
    fn = functools.partial(_gather_copies, halved=halved)
    return _split_start(bufs, len(bufs), fn, len(_CHIP_RELS), name, after)


def _gather_wait(bufs, send, recv, name, after, halved=False):
    return _split_wait(bufs, send, recv, functools.partial(_gather_copies, halved=halved), name, after)


def _scatter_start(srcs, name, after):
    n = len(srcs)
    lands = [lax.empty(a.shape, a.dtype) for a in srcs]
    fn = lambda refs, send, recv, landed: _scatter_copies(refs[:n], refs[n:], send, recv, landed)
    send, recv, bufs, token = _split_start(list(srcs) + lands, n, fn, len(_DEV_RELS), name, after)
    return send, recv, bufs, token


def _scatter_wait(bufs, send, recv, name, after):
    n = len(bufs) // 2
    fn = lambda refs, s, r, landed: _scatter_copies(refs[:n], refs[n:], s, r, landed)
    res = _split_wait(bufs, send, recv, fn, name, after)
    return res[:n], res[n:]


def _sum_segments(src, land, me, name):
    nd, seg, cols = src.shape
    ts = _tile(seg, (256, 176, 128))

    def body(m_ref, *refs):
        o_ref = refs[-1]
        acc = refs[0][...].astype(F32)
        for r in refs[1:-1]:
            acc = acc + r[...].astype(F32)
        o_ref[...] = acc

    def peer(rel):
        bits = 4 * rel[0] + 2 * rel[1] + rel[2]
        return pl.BlockSpec((None, ts, cols), lambda i, m: (jnp.bitwise_xor(m[0], bits), i, 0))

    grid_spec = pltpu.PrefetchScalarGridSpec(
        num_scalar_prefetch=1, grid=(seg // ts,),
        in_specs=[pl.BlockSpec((None, ts, cols), lambda i, m: (m[0], i, 0))] + [peer(r) for r in _DEV_RELS],
        out_specs=pl.BlockSpec((None, ts, cols), lambda i, m: (m[1], i, 0)))
    return _blocked(body, name=name, grid_spec=grid_spec,
                          out_shape=jax.ShapeDtypeStruct((2, seg, cols), F32),
                          compiler_params=_cparams(("parallel",), VMEM_MID))(me, src, *[land] * len(_DEV_RELS))


def _exchange_sibling(arrs):
    n = len(arrs)

    def body(*refs):
        outs, (send, recv) = refs[n:2 * n], refs[2 * n:]
        x, y, c = _coords()
        sib = (x, y, 1 - c)
        sends, recvs = [], []
        for i in range(n):
            cp = pltpu.make_async_remote_copy(src_ref=outs[i].at[c], dst_ref=outs[i].at[c], send_sem=send.at[i],
                                              recv_sem=recv.at[i], device_id=sib, device_id_type=MESH)
            cp.start()
            sends.append(cp)
            recvs.append(pltpu.make_async_remote_copy(src_ref=outs[i].at[c], dst_ref=outs[i].at[1 - c],
                                                      send_sem=send.at[i], recv_sem=recv.at[i], device_id=sib,
                                                      device_id_type=MESH))
        for cp in recvs:
            cp.wait_recv()
        for cp in sends:
            cp.wait_send()

    return pl.pallas_call(
        body, name="exchange_sibling", in_specs=[_ANY] * n, out_specs=[_ANY] * n,
        out_shape=[jax.ShapeDtypeStruct(a.shape, a.dtype) for a in arrs],
        input_output_aliases={i: i for i in range(n)},
        scratch_shapes=[pltpu.SemaphoreType.DMA((n,)), pltpu.SemaphoreType.DMA((n,))])(*arrs)


def _allreduce_small(vec):
    _, nchips, seg, lanes = vec.shape
    nr = len(_CHIP_RELS)

    def body(in_ref, out_ref, from_sib, half, stage, red, send, recv):
        x, y, c = _coords()
        me = 2 * x + y
        sib = (x, y, 1 - c)
        chips = [(_flip(x, dx), _flip(y, dy)) for dx, dy in _CHIP_RELS]

        def copy(src, dst, k, peer):
            return pltpu.make_async_remote_copy(src_ref=src, dst_ref=dst, send_sem=send.at[k], recv_sem=recv.at[k],
                                                device_id=peer, device_id_type=MESH)

        to_sib = copy(in_ref.at[1 - c], from_sib, 0, sib)
        to_sib.start()
        to_sib.wait_recv()
        half[...] = in_ref[c] + from_sib[...]

        first = [copy(half.at[2 * px + py], stage.at[me], 1 + j, (px, py, c)) for j, (px, py) in enumerate(chips)]
        for cp in first:
            cp.start()
        stage[me] = half[me]
        for j, (px, py) in enumerate(chips):
            copy(half.at[2 * px + py], stage.at[2 * px + py], 1 + j, (px, py, c)).wait_recv()
        acc = stage[0]
        for k in range(1, nchips):
            acc = acc + stage[k]
        red[...] = acc
        out_ref[c, me] = acc

        second = [copy(red, out_ref.at[c, me], 1 + nr + j, (px, py, c)) for j, (px, py) in enumerate(chips)]
        for cp in second:
            cp.start()
        for j, (px, py) in enumerate(chips):
            copy(red, out_ref.at[c, 2 * px + py], 1 + nr + j, (px, py, c)).wait_recv()

        back = copy(out_ref.at[c], out_ref.at[c], 1 + 2 * nr, sib)
        back.start()
        copy(out_ref.at[c], out_ref.at[1 - c], 1 + 2 * nr, sib).wait_recv()
        for cp in [to_sib] + first + second + [back]:
            cp.wait_send()

    vm = pl.BlockSpec(memory_space=pltpu.VMEM)
    nsem = 2 + 2 * nr
    return pl.pallas_call(
        body, name="allreduce_small", in_specs=[vm], out_specs=vm,
        out_shape=jax.ShapeDtypeStruct(vec.shape, F32),
        scratch_shapes=[pltpu.VMEM((nchips, seg, lanes), F32), pltpu.VMEM((nchips, seg, lanes), F32),
                        pltpu.VMEM((nchips, seg, lanes), F32), pltpu.VMEM((seg, lanes), F32),
                        pltpu.SemaphoreType.DMA((nsem,)), pltpu.SemaphoreType.DMA((nsem,))],
        compiler_params=_cparams(None, VMEM_MID))(vec)


def _adam_math(w, g, m, v):
    m2 = ADAM_B1 * m + (1.0 - ADAM_B1) * g
    v2 = ADAM_B2 * v + (1.0 - ADAM_B2) * (g * g)
    m_hat = m2 / (1.0 - ADAM_B1 ** ADAM_STEP)
    v_hat = v2 / (1.0 - ADAM_B2 ** ADAM_STEP)
    return -ADAM_LR * (m_hat / (jnp.sqrt(v_hat) + ADAM_EPS) + ADAM_WD * w), m2, v2


def _adamw_big(w3, m3, v3, layer, g, transposed, name, prev=None):
    nl, rows, cols = w3.shape
    tr = 128 if transposed else _tile(rows, (256, 176, 128))

    def body(w_ref, m_ref, v_ref, g_ref, *rest):
        go_ref, d_ref, mo_ref, vo_ref = rest[-4:]
        g_val = g_ref[...].T if transposed else g_ref[...]
        go_ref[...] = g_val
        d_ref[...], mo_ref[...], vo_ref[...] = _adam_math(w_ref[...], g_val, m_ref[...], v_ref[...])

    wspec = pl.BlockSpec((None, tr, cols), lambda i: (layer, i, 0))
    gspec = pl.BlockSpec((cols, tr), lambda i: (0, i)) if transposed else pl.BlockSpec((tr, cols), lambda i: (i, 0))
    extra = [] if prev is None else list(prev)
    return _blocked(body, name=name, grid=(rows // tr,),
                          in_specs=[wspec, wspec, wspec, gspec] + [_ANY] * len(extra),
                          out_specs=[wspec] * 4, out_shape=[jax.ShapeDtypeStruct((nl, rows, cols), F32)] * 4,
                          input_output_aliases={4 + i: i for i in range(len(extra))},
                          compiler_params=_cparams(("parallel",), VMEM_MID))(w3, m3, v3, g, *extra)


_SMALL = (
    ("e_norm_g", (1, D_MODEL), None), ("e_mu", (1, SHIFT_COLS), None), ("e_w0", (1, RW), None),
    ("e_w2", (W_LORA, RW), RW // 4), ("e_a0", (1, RW), None), ("e_a2", (A_LORA, RW), RW // 4),
    ("e_g2", (G_LORA, RW), RW // 4), ("e_k_k", (1, RW), None), ("e_k_a", (1, RW), None), ("e_r_k", (1, RW), None),
    ("e_ln_w", (1, RW), None), ("e_ln_b", (1, RW), None), ("e_conv_w", (4, LRU_W), LRU_W // 4),
    ("e_conv_b", (1, LRU_W), None), ("e_gate_a_w", (LRU_W, HEAD), None), ("e_gate_a_b", (1, LRU_W), None),
    ("e_gate_x_w", (LRU_W, HEAD), None), ("e_gate_x_b", (1, LRU_W), None), ("e_lru_lambda", (1, LRU_W), None),
    ("o_norm_g", (1, D_MODEL), D_MODEL // 4), ("o_A_re", (S5_GROUPS, S5_STATE), None),
    ("o_A_im", (S5_GROUPS, S5_STATE), None), ("o_log_dt", (1, S5_GROUPS), None),
    ("o_B_re", (S5_GROUPS, S5_STATE * S5_GROUP), None), ("o_B_im", (S5_GROUPS, S5_STATE * S5_GROUP), None),
    ("o_C_re", (S5_GROUPS * S5_GROUP, S5_STATE), None), ("o_C_im", (S5_GROUPS * S5_GROUP, S5_STATE), None),
    ("o_D", (1, D_MODEL), D_MODEL // 4), ("f_norm_g", (2, D_MODEL), None),
    ("f_conv_w", (6, 2 * D_FF), 2 * D_FF // 4), ("f_conv_b", (2, 2 * D_FF), None),
    ("final_norm_g", (1, D_MODEL), None))
_PIECES = {"f_norm_g": ((0, 1), (1, 1)), "f_conv_b": ((0, 1), (1, 1)), "f_conv_w": ((0, 3), (3, 3))}


def _ceil_to(n, m):
    return -(-n // m) * m


def _small_layout():
    groups = {}
    for name, (rows, cols), _ in _SMALL:
        for first, r in _PIECES.get(name, ((0, rows),)):
            groups.setdefault(cols, []).append((name, first, r))
    layout, off = {}, 0
    for cols, items in groups.items():
        stacks = [0, 0] if 2 * cols <= LANES else [0]
        placed = []
        for name, first, r in sorted(items, key=lambda it: -it[2]):
            half = stacks.index(min(stacks))
            r0 = stacks[half]
            if r >= 8 or r0 % 8 + r > 8:
                r0 = _ceil_to(r0, 8)
            placed.append((name, first, r, r0, half * (LANES // 2)))
            stacks[half] = r0 + r
        rpad = _ceil_to(max(stacks), 8)
        for name, first, r, at, lane in placed:
            layout[name, first] = (off, rpad, at, r, cols, lane)
        off += -(-cols // LANES) * rpad
    return layout, _ceil_to(off, 8 * N_DEV)


def _small_pack(gs):
    layout, total = _small_layout()
    keys = list(layout)

    def body(*refs):
        out = refs[-1]
        out[...] = jnp.zeros_like(out)
        for key, g_ref in zip(keys, refs[:-1]):
            off, rpad, at, r, cols, lane = layout[key]
            for j in range(-(-cols // LANES)):
                cw = min(LANES, cols - j * LANES)
                out[off + j * rpad + at:off + j * rpad + at + r, lane:lane + cw] = g_ref[:, j * LANES:j * LANES + cw]

    return pl.pallas_call(body, name="small_pack", out_shape=jax.ShapeDtypeStruct((total, LANES), F32),
                          compiler_params=_cparams(None, VMEM_MID))(*[gs[k] for k in keys])


def _adamw_small(red, chip, wts, ms, vs):
    layout, _ = _small_layout()
    names = [n for n, _, _ in _SMALL]
    n = len(names)

    def body(chip_ref, red_ref, *refs):
        ins, outs = refs[:3 * n], refs[3 * n:]
        c = chip_ref[0]
        for i, (name, (rows, cols), loc) in enumerate(_SMALL):
            w_ref, m_ref, v_ref = ins[3 * i:3 * i + 3]
            o_refs = outs[4 * i:4 * i + 4]
            width = cols if loc is None else loc
            for first, r in _PIECES.get(name, ((0, rows),)):
                off, rpad, at, _, _, lane = layout[name, first]
                for j in range(-(-width // LANES)):
                    cw = min(LANES, width - j * LANES)
                    ls = slice(lane, lane + cw)
                    if loc is None:
                        start = off + j * rpad + at
                        g = red_ref[start:start + r, ls]
                    else:
                        blk = c * (loc // LANES) + j
                        if r >= 8:
                            g = red_ref[pl.ds(pl.multiple_of(off + at + blk * rpad, 8), r), ls]
                        else:
                            tile = red_ref[pl.ds(pl.multiple_of(off + at // 8 * 8 + blk * rpad, 8), 8), ls]
                            g = tile[at % 8:at % 8 + r]
                    rs, cs = slice(first, first + r), slice(j * LANES, j * LANES + cw)
                    d, m2, v2 = _adam_math(w_ref[rs, cs], g, m_ref[rs, cs], v_ref[rs, cs])
                    for o, val in zip(o_refs, (g, d, m2, v2)):
                        o[rs, cs] = val

    args, shapes = [], []
    for name in names:
        args += [wts[name], ms[name], vs[name]]
        shapes += [jax.ShapeDtypeStruct(wts[name].shape, F32)] * 4
    vm = pl.BlockSpec(memory_space=pltpu.VMEM)
    res = pl.pallas_call(body, name="adamw_small",
                         in_specs=[pl.BlockSpec(memory_space=pltpu.SMEM), vm] + [vm] * (3 * n),
                         out_specs=[vm] * (4 * n), out_shape=shapes,
                         compiler_params=_cparams(None, VMEM_BIG))(chip, red, *args)
    return {name: res[4 * i:4 * i + 4] for i, name in enumerate(names)}


PACK_ROWS = 8


def _packed_rows(shape):
    size = 1
    for d in shape:
        size *= d
    return -(-size // (PACK_ROWS * LANES)) * PACK_ROWS


def _pack(arrs, row_mult):
    parts = []
    for a in arrs:
        flat = a.reshape(-1).astype(F32)
        rows = _packed_rows(a.shape)
        parts.append(jnp.pad(flat, (0, rows * LANES - flat.shape[0])).reshape(rows, LANES))
    total = sum(p.shape[0] for p in parts)
    fill = -(-total // row_mult) * row_mult - total
    if fill:
        parts.append(jnp.zeros((fill, LANES), F32))
    return jnp.concatenate(parts, axis=0)


def _unpack(packed, shapes):
    out, off = [], 0
    for s in shapes:
        rows = _packed_rows(s)
        size = 1
        for d in s:
            size *= d
        out.append(packed[off:off + rows].reshape(-1)[:size].reshape(s))
        off += rows
    return out


_SMALL_SH = ("e_w2", "e_a2", "e_g2", "e_conv_w", "o_norm_g", "o_D", "f_conv_w")
_LARGE = (("e_w_in", True), ("e_w_out", False), ("o_w_in", False), ("o_w_glu", True), ("f_w_up", True),
        ("f_w_down", False))
_ORDER = ("e_norm_g", "e_w_in", "e_mu", "e_w0", "e_w2", "e_a0", "e_a2", "e_g2", "e_k_k", "e_k_a", "e_r_k", "e_ln_w",
          "e_ln_b", "e_conv_w", "e_conv_b", "e_gate_a_w", "e_gate_a_b", "e_gate_x_w", "e_gate_x_b", "e_lru_lambda",
          "e_w_out", "o_norm_g", "o_w_in", "o_A_re", "o_A_im", "o_log_dt", "o_B_re", "o_B_im", "o_C_re", "o_C_im",
          "o_D", "o_w_glu", "f_norm_g", "f_w_up", "f_conv_w", "f_conv_b", "f_w_down", "final_norm_g")
N_CHIPS = 4
N_DEV = 8


def _step(x, tgt, wts, ms, vs):
    xi, yi, ci = _coords()
    chip = 2 * xi + yi
    chip1 = chip.astype(jnp.int32).reshape(1)
    me2 = jnp.stack([4 * xi + 2 * yi + ci, ci]).astype(jnp.int32)
    by_cols = dict(_LARGE)

    cast = lambda name, l, after=None: _cast_shard(wts[name], l, by_cols[name], chip1, f"cast_{name}{l}", after)
    sh_shapes = [wts[n].shape for n in _SMALL_SH]
    packed = _pack([wts[n] for n in _SMALL_SH], 16)
    small_buf = lax.dynamic_update_slice(jnp.zeros((N_CHIPS,) + packed.shape, F32), packed[None], (chip, 0, 0))
    late = [(name, l) for name, _ in _LARGE if name != "e_w_in" for l in range(wts[name].shape[0])]
    send, recv, thru, token = _gather_start([cast("e_w_in", 0), small_buf], "gather_start_a", x, halved=True)
    first_token = token
    bufs = {(name, l): cast(name, l, token) for name, l in late}
    got = _swap_fetched(_gather_wait(thru, send, recv, "gather_wait_a", bufs[late[-1]], halved=True))
    send_b, recv_b, thru_b, token = _gather_start([bufs[k] for k in late], "gather_start_b", got[0])
    x, _ = lax.optimization_barrier((x, token))

    def rows(g):
        return g.reshape(N_CHIPS * g.shape[1], g.shape[2])

    full = {n: wts[n] for n, _, loc in _SMALL if loc is None}
    full["first_gather_started"] = first_token
    full["e_w_in_t"] = rows(got[0])
    per_chip = [_unpack(got[1][k], sh_shapes) for k in range(N_CHIPS)]
    for i, n in enumerate(_SMALL_SH):
        full[n] = jnp.concatenate([per_chip[k][i] for k in range(N_CHIPS)], axis=-1)

    def late_weights(after):
        res = dict(zip(late, _gather_wait(thru_b, send_b, recv_b, "gather_wait_b", after)))
        return {"e_w_out": rows(res[("e_w_out", 0)]), "o_w_in": rows(res[("o_w_in", 0)]),
                "o_w_glu_t": rows(res[("o_w_glu", 0)]),
                "f_w_up_t": [rows(res[("f_w_up", l)]) for l in range(2)],
                "f_w_down": [rows(res[("f_w_down", l)]) for l in range(2)]}

    pending = []

    def send_grads(tag, items, carry):
        srcs = [g.reshape(N_DEV, g.shape[0] // N_DEV, g.shape[1]) for _, _, g in items]
        s_sem, r_sem, both, tok = _scatter_start(srcs, f"scatter_start_{tag}", carry)
        pending.append((tag, [(name, l) for name, l, _ in items], s_sem, r_sem, both))
        carry, _ = lax.optimization_barrier((carry, tok))
        return carry

    loss, grad_x, gs = _local_step(x, tgt, full, late_weights, send_grads)

    final = {}
    red = _allreduce_small(_small_pack(gs).reshape(2, N_CHIPS, -1, LANES)).reshape(-1, LANES)
    view = {name: (rows, cols if loc is None else loc) for name, (rows, cols), loc in _SMALL}
    as2d = lambda d: {name: d[name].reshape(view[name]) for name in view}
    small = _adamw_small(red, chip1, as2d(wts), as2d(ms), as2d(vs))
    for name, res in small.items():
        final[name] = [r.reshape(wts[name].shape) for r in res]
    new_v = small["final_norm_g"][3]

    halves, keys = [], []
    for tag, names, s_sem, r_sem, both in pending:
        srcs, lands = _scatter_wait(both, s_sem, r_sem, f"scatter_wait_{tag}", new_v)
        for (name, l), src, land in zip(names, srcs, lands):
            halves.append(_sum_segments(src, land, me2, f"sum_{name}{l}"))
            keys.append((name, l))
    shards = _exchange_sibling(halves)
    for s, (name, l) in zip(shards, keys):
        final[name] = _adamw_big(wts[name], ms[name], vs[name], l, s.reshape(2 * s.shape[1], s.shape[2]),
                                 by_cols[name], f"adamw_{name}{l}", prev=final.get(name))

    loss = lax.psum(loss[0, 0], ("x", "y", "c"))
    res = [loss, grad_x[None]]
    for k in range(4):
        res += [final[n][k] for n in _ORDER]
    return tuple(res)


def kernel(x, e_norm_g, e_w_in, e_mu, e_w0, e_w2, e_a0, e_a2, e_g2, e_k_k, e_k_a, e_r_k, e_ln_w, e_ln_b, e_conv_w, e_conv_b, e_gate_a_w, e_gate_a_b, e_gate_x_w, e_gate_x_b, e_lru_lambda, e_w_out, o_norm_g, o_w_in, o_A_re, o_A_im, o_log_dt, o_B_re, o_B_im, o_C_re, o_C_im, o_D, o_w_glu, f_norm_g, f_w_up, f_conv_w, f_conv_b, f_w_down, final_norm_g, loss_target, m_e_norm_g, m_e_w_in, m_e_mu, m_e_w0, m_e_w2, m_e_a0, m_e_a2, m_e_g2, m_e_k_k, m_e_k_a, m_e_r_k, m_e_ln_w, m_e_ln_b, m_e_conv_w, m_e_conv_b, m_e_gate_a_w, m_e_gate_a_b, m_e_gate_x_w, m_e_gate_x_b, m_e_lru_lambda, m_e_w_out, m_o_norm_g, m_o_w_in, m_o_A_re, m_o_A_im, m_o_log_dt, m_o_B_re, m_o_B_im, m_o_C_re, m_o_C_im, m_o_D, m_o_w_glu, m_f_norm_g, m_f_w_up, m_f_conv_w, m_f_conv_b, m_f_w_down, m_final_norm_g, v_e_norm_g, v_e_w_in, v_e_mu, v_e_w0, v_e_w2, v_e_a0, v_e_a2, v_e_g2, v_e_k_k, v_e_k_a, v_e_r_k, v_e_ln_w, v_e_ln_b, v_e_conv_w, v_e_conv_b, v_e_gate_a_w, v_e_gate_a_b, v_e_gate_x_w, v_e_gate_x_b, v_e_lru_lambda, v_e_w_out, v_o_norm_g, v_o_w_in, v_o_A_re, v_o_A_im, v_o_log_dt, v_o_B_re, v_o_B_im, v_o_C_re, v_o_C_im, v_o_D, v_o_w_glu, v_f_norm_g, v_f_w_up, v_f_conv_w, v_f_conv_b, v_f_w_down, v_final_norm_g):
    args = locals()
    wts = {n: args[n] for n in _ORDER}
    ms = {n: args["m_" + n] for n in _ORDER}
    vs = {n: args["v_" + n] for n in _ORDER}
    return _step(x[0], loss_target[0], wts, ms, vs)
```
